```python
import math
import jax, jax.numpy as jnp
from jax import lax
import numpy as np

D_MODEL = 1024
BATCH = 16
SEQ = 2048
DEPTH = 1

SB_HEADS = 16
SB_HEAD_DIM = 64
SB_WIDTH = SB_HEADS * SB_HEAD_DIM
Q_BLOCK = 128
SSD_EXPAND = 2
SSD_WIDTH = SSD_EXPAND * D_MODEL
SSD_HEAD_DIM = 64
SSD_HEADS = SSD_WIDTH // SSD_HEAD_DIM
SSD_GROUPS = 4
SSD_HEADS_PER_GROUP = SSD_HEADS // SSD_GROUPS
SSD_STATE = 128
SSD_CONV = 4
SSD_CHUNK = 128
SSD_BC_WIDTH = SSD_GROUPS * SSD_STATE
SSD_CONV_DIM = SSD_WIDTH + 2 * SSD_BC_WIDTH
N_BRANCHES = 2
PROJ_SPLITS = (SB_WIDTH, SB_WIDTH, SB_WIDTH, SB_WIDTH, SSD_WIDTH, SSD_CONV_DIM, SSD_HEADS, N_BRANCHES * D_MODEL)
D_PROJ = 4 * SB_WIDTH + SSD_WIDTH + SSD_CONV_DIM + SSD_HEADS + N_BRANCHES * D_MODEL
EPS = 1e-6
DT_MIN = 0.001
DT_MAX = 0.1
A_INIT_MIN = 1.0
A_INIT_MAX = 16.0

kernel_name = "hybrid_stickbreaking_ssd_gated_block"


def rms_norm(x, w):
    xf = x.astype(jnp.float32)
    y = xf * lax.rsqrt(jnp.mean(xf * xf, axis=-1, keepdims=True) + EPS)
    return (y * w.astype(jnp.float32)).astype(x.dtype)


def stick_breaking_attention(q, k, v):
    s_len = q.shape[2]
    scale = q.shape[-1] ** -0.5
    outs = []
    for blk in range(s_len // Q_BLOCK):
        start = blk * Q_BLOCK
        end = start + Q_BLOCK
        qb = q[:, :, start:end]
        kb = k[:, :, :end]
        vb = v[:, :, :end]
        z = jnp.einsum('bhqd,bhkd->bhqk', qb, kb) * scale
        t_idx = start + jnp.arange(Q_BLOCK)[:, None]
        s_idx = jnp.arange(end)[None, :]
        mask = s_idx < t_idx
        log_beta = jax.nn.log_sigmoid(z)
        log_one_minus = jnp.where(mask, jax.nn.log_sigmoid(-z), 0.0)
        later = lax.cumsum(log_one_minus, axis=3, reverse=True) - log_one_minus
        a = jnp.where(mask, jnp.exp(log_beta + later), 0.0)
        outs.append(jnp.einsum('bhqk,bhkd->bhqd', a, vb))
    return jnp.concatenate(outs, axis=2)


def causal_depthwise_conv(x, w, b):
    c = x.shape[-1]
    y = lax.conv_general_dilated(
        x, w[:, None, :].astype(x.dtype), window_strides=(1,),
        padding=[(SSD_CONV - 1, 0)],
        dimension_numbers=('NWC', 'WIO', 'NWC'),
        feature_group_count=c)
    return y + b.astype(x.dtype)


def ssd_chunked(x, dt, a, bm, cm):
    b, s, g, hg, p = x.shape
    n = bm.shape[-1]
    nc = s // SSD_CHUNK
    x = x.reshape(b, nc, SSD_CHUNK, g, hg, p)
    dt = dt.reshape(b, nc, SSD_CHUNK, g, hg)
    bm = bm.reshape(b, nc, SSD_CHUNK, g, n)
    cm = cm.reshape(b, nc, SSD_CHUNK, g, n)
    a_cs = jnp.cumsum(dt * a, axis=2)
    xdt = x * dt[..., None]
    l_idx = jnp.arange(SSD_CHUNK)
    causal = (l_idx[:, None] >= l_idx[None, :])[:, :, None, None]
    seg = a_cs[:, :, :, None] - a_cs[:, :, None, :]
    decay = jnp.exp(jnp.where(causal, seg, -jnp.inf))
    cb = jnp.einsum('bclgn,bcsgn->bclsg', cm, bm)
    y_diag = jnp.einsum('bclsgh,bcsghp->bclghp', cb[..., None] * decay, xdt)
    decay_to_end = jnp.exp(a_cs[:, :, -1:] - a_cs)
    states = jnp.einsum('bclgn,bclghp->bcghpn', bm, xdt * decay_to_end[..., None])
    chunk_decay = jnp.exp(a_cs[:, :, -1])

    def step(h_prev, inp):
        st, dec = inp
        return h_prev * dec[..., None, None] + st, h_prev

    h0 = jnp.zeros((b, g, hg, p, n), jnp.float32)
    _, h_in = lax.scan(step, h0, (jnp.moveaxis(states, 1, 0), jnp.moveaxis(chunk_decay, 1, 0)))
    h_in = jnp.moveaxis(h_in, 0, 1)
    y_off = jnp.einsum('bclgn,bcghpn->bclghp', cm, h_in) * jnp.exp(a_cs)[..., None]
    return (y_diag + y_off).reshape(b, s, g, hg, p)


def hybrid_layer(x, norm_w, w_in, conv_w, conv_b, dt_bias, a_log, d_skip,
                 ssm_norm_w, w_attn_out, w_ssm_out, w_o):
    b, s, _ = x.shape
    f32 = jnp.float32
    h = rms_norm(x, norm_w)
    proj = jnp.einsum('bsd,de->bse', h, w_in)
    split_points = [int(v) for v in np.cumsum(PROJ_SPLITS)[:-1]]
    q, k, v, z_a, z_s, xbc, dt_raw, gate_raw = jnp.split(proj, split_points, axis=-1)

    def to_heads(t):
        return t.reshape(b, s, SB_HEADS, SB_HEAD_DIM).transpose(0, 2, 1, 3).astype(f32)

    o = stick_breaking_attention(to_heads(q), to_heads(k), to_heads(v))
    o = o.transpose(0, 2, 1, 3).reshape(b, s, SB_WIDTH)
    y_a = (o * jax.nn.silu(z_a.astype(f32))).astype(x.dtype)
    y_a = jnp.einsum('bse,ed->bsd', y_a, w_attn_out)

    xbc = jax.nn.silu(causal_depthwise_conv(xbc, conv_w, conv_b))
    xs, bm, cm = jnp.split(xbc, [SSD_WIDTH, SSD_WIDTH + SSD_BC_WIDTH], axis=-1)
    xs = xs.reshape(b, s, SSD_GROUPS, SSD_HEADS_PER_GROUP, SSD_HEAD_DIM).astype(f32)
    bm = bm.reshape(b, s, SSD_GROUPS, SSD_STATE).astype(f32)
    cm = cm.reshape(b, s, SSD_GROUPS, SSD_STATE).astype(f32)
    dt = jax.nn.softplus(dt_raw.astype(f32) + dt_bias.astype(f32))
    dt = dt.reshape(b, s, SSD_GROUPS, SSD_HEADS_PER_GROUP)
    a = -jnp.exp(a_log.astype(f32)).reshape(SSD_GROUPS, SSD_HEADS_PER_GROUP)
    y = ssd_chunked(xs, dt, a, bm, cm)
    y = y + xs * d_skip.astype(f32).reshape(SSD_GROUPS, SSD_HEADS_PER_GROUP)[..., None]
    y = y.reshape(b, s, SSD_WIDTH) * jax.nn.silu(z_s.astype(f32))
    yg = y.reshape(b, s, SSD_GROUPS, SSD_WIDTH // SSD_GROUPS)
    yg = yg * lax.rsqrt(jnp.mean(yg * yg, axis=-1, keepdims=True) + EPS)
    y = yg.reshape(b, s, SSD_WIDTH) * ssm_norm_w.astype(f32)
    y_s = jnp.einsum('bse,ed->bsd', y.astype(x.dtype), w_ssm_out)

    g_a, g_s = jnp.split(jax.nn.sigmoid(gate_raw.astype(f32)), N_BRANCHES, axis=-1)
    merged = (g_a * y_a.astype(f32) + g_s * y_s.astype(f32)).astype(x.dtype)
    return x + jnp.einsum('bsd,de->bse', merged, w_o)


def _fwd_setup_inputs(seed: int = 0) -> dict:
    key = jax.random.key(seed)
    ks = jax.random.split(key, 14)
    f32 = jnp.float32
    x = jax.random.normal(ks[0], (BATCH, SEQ, D_MODEL), f32)
    norm_w = 1.0 + 0.02 * jax.random.normal(ks[1], (DEPTH, D_MODEL), f32)
    w_in = jax.random.normal(ks[2], (DEPTH, D_MODEL, D_PROJ), f32) * D_MODEL ** -0.5
    conv_w = jax.random.normal(ks[3], (DEPTH, SSD_CONV, SSD_CONV_DIM), f32) * SSD_CONV ** -0.5
    conv_b = 0.02 * jax.random.normal(ks[4], (DEPTH, SSD_CONV_DIM), f32)
    u = jax.random.uniform(ks[5], (DEPTH, SSD_HEADS), f32)
    dt0 = jnp.exp(u * (math.log(DT_MAX) - math.log(DT_MIN)) + math.log(DT_MIN))
    dt_bias = dt0 + jnp.log(-jnp.expm1(-dt0))
    a_log = jnp.log(jax.random.uniform(ks[6], (DEPTH, SSD_HEADS), f32, A_INIT_MIN, A_INIT_MAX))
    d_skip = 1.0 + 0.02 * jax.random.normal(ks[7], (DEPTH, SSD_HEADS), f32)
    ssm_norm_w = 1.0 + 0.02 * jax.random.normal(ks[8], (DEPTH, SSD_WIDTH), f32)
    w_attn_out = jax.random.normal(ks[9], (DEPTH, SB_WIDTH, D_MODEL), f32) * SB_WIDTH ** -0.5
    w_ssm_out = jax.random.normal(ks[10], (DEPTH, SSD_WIDTH, D_MODEL), f32) * SSD_WIDTH ** -0.5
    w_o = jax.random.normal(ks[11], (DEPTH, D_MODEL, D_MODEL), f32) * D_MODEL ** -0.5
    final_norm_w = 1.0 + 0.02 * jax.random.normal(ks[12], (D_MODEL,), f32)
    return {"x": x, "norm_w": norm_w, "w_in": w_in, "conv_w": conv_w, "conv_b": conv_b,
            "dt_bias": dt_bias, "a_log": a_log, "d_skip": d_skip, "ssm_norm_w": ssm_norm_w,
            "w_attn_out": w_attn_out, "w_ssm_out": w_ssm_out, "w_o": w_o,
            "final_norm_w": final_norm_w}


def _fwd_reference(x, norm_w, w_in, conv_w, conv_b, dt_bias, a_log, d_skip, ssm_norm_w,
              w_attn_out, w_ssm_out, w_o, final_norm_w):
    h = x
    for layer in range(DEPTH):
        h = hybrid_layer(h, norm_w[layer], w_in[layer], conv_w[layer], conv_b[layer],
                         dt_bias[layer], a_log[layer], d_skip[layer], ssm_norm_w[layer],
                         w_attn_out[layer], w_ssm_out[layer], w_o[layer])
    return rms_norm(h, final_norm_w)


import jax as _jax
import jax.numpy as _jnp

TWIN_FORMAT = 'train_step'
FWD_PARAMS = ['x', 'norm_w', 'w_in', 'conv_w', 'conv_b', 'dt_bias', 'a_log', 'd_skip', 'ssm_norm_w', 'w_attn_out', 'w_ssm_out', 'w_o', 'final_norm_w']
TWIN_WEIGHTS = ['norm_w', 'w_in', 'conv_w', 'conv_b', 'dt_bias', 'a_log', 'd_skip', 'ssm_norm_w', 'w_attn_out', 'w_ssm_out', 'w_o', 'final_norm_w']
TWIN_DIFF_INPUT = 'x'
TWIN_INPUTS = ['x', 'norm_w', 'w_in', 'conv_w', 'conv_b', 'dt_bias', 'a_log', 'd_skip', 'ssm_norm_w', 'w_attn_out', 'w_ssm_out', 'w_o', 'final_norm_w', 'loss_target', 'm_norm_w', 'm_w_in', 'm_conv_w', 'm_conv_b', 'm_dt_bias', 'm_a_log', 'm_d_skip', 'm_ssm_norm_w', 'm_w_attn_out', 'm_w_ssm_out', 'm_w_o', 'm_final_norm_w', 'v_norm_w', 'v_w_in', 'v_conv_w', 'v_conv_b', 'v_dt_bias', 'v_a_log', 'v_d_skip', 'v_ssm_norm_w', 'v_w_attn_out', 'v_w_ssm_out', 'v_w_o', 'v_final_norm_w']
TWIN_OUTPUTS = ['loss', 'grad_x', 'grad_norm_w', 'grad_w_in', 'grad_conv_w', 'grad_conv_b', 'grad_dt_bias', 'grad_a_log', 'grad_d_skip', 'grad_ssm_norm_w', 'grad_w_attn_out', 'grad_w_ssm_out', 'grad_w_o', 'grad_final_norm_w', 'delta_norm_w', 'delta_w_in', 'delta_conv_w', 'delta_conv_b', 'delta_dt_bias', 'delta_a_log', 'delta_d_skip', 'delta_ssm_norm_w', 'delta_w_attn_out', 'delta_w_ssm_out', 'delta_w_o', 'delta_final_norm_w', 'new_m_norm_w', 'new_m_w_in', 'new_m_conv_w', 'new_m_conv_b', 'new_m_dt_bias', 'new_m_a_log', 'new_m_d_skip', 'new_m_ssm_norm_w', 'new_m_w_attn_out', 'new_m_w_ssm_out', 'new_m_w_o', 'new_m_final_norm_w', 'new_v_norm_w', 'new_v_w_in', 'new_v_conv_w', 'new_v_conv_b', 'new_v_dt_bias', 'new_v_a_log', 'new_v_d_skip', 'new_v_ssm_norm_w', 'new_v_w_attn_out', 'new_v_w_ssm_out', 'new_v_w_o', 'new_v_final_norm_w']
TWIN_LEAF_KINDS = {'loss': 'loss', 'grad_x': 'grad_x', 'grad_norm_w': 'grad_w', 'grad_w_in': 'grad_w', 'grad_conv_w': 'grad_w', 'grad_conv_b': 'grad_w', 'grad_dt_bias': 'grad_w', 'grad_a_log': 'grad_w', 'grad_d_skip': 'grad_w', 'grad_ssm_norm_w': 'grad_w', 'grad_w_attn_out': 'grad_w', 'grad_w_ssm_out': 'grad_w', 'grad_w_o': 'grad_w', 'grad_final_norm_w': 'grad_w', 'delta_norm_w': 'delta_w', 'delta_w_in': 'delta_w', 'delta_conv_w': 'delta_w', 'delta_conv_b': 'delta_w', 'delta_dt_bias': 'delta_w', 'delta_a_log': 'delta_w', 'delta_d_skip': 'delta_w', 'delta_ssm_norm_w': 'delta_w', 'delta_w_attn_out': 'delta_w', 'delta_w_ssm_out': 'delta_w', 'delta_w_o': 'delta_w', 'delta_final_norm_w': 'delta_w', 'new_m_norm_w': 'new_m', 'new_m_w_in': 'new_m', 'new_m_conv_w': 'new_m', 'new_m_conv_b': 'new_m', 'new_m_dt_bias': 'new_m', 'new_m_a_log': 'new_m', 'new_m_d_skip': 'new_m', 'new_m_ssm_norm_w': 'new_m', 'new_m_w_attn_out': 'new_m', 'new_m_w_ssm_out': 'new_m', 'new_m_w_o': 'new_m', 'new_m_final_norm_w': 'new_m', 'new_v_norm_w': 'new_v', 'new_v_w_in': 'new_v', 'new_v_conv_w': 'new_v', 'new_v_conv_b': 'new_v', 'new_v_dt_bias': 'new_v', 'new_v_a_log': 'new_v', 'new_v_d_skip': 'new_v', 'new_v_ssm_norm_w': 'new_v', 'new_v_w_attn_out': 'new_v', 'new_v_w_ssm_out': 'new_v', 'new_v_w_o': 'new_v', 'new_v_final_norm_w': 'new_v'}


def _forward(args):
    return _fwd_reference(*[args[k] for k in FWD_PARAMS])


def _output_shape():
    out = _jax.eval_shape(lambda: _forward(_fwd_setup_inputs(0)))
    return out.shape, out.dtype

N_MICROBATCH = 1
ADAM_LR = 0.001
ADAM_B1 = 0.9
ADAM_B2 = 0.999
ADAM_EPS = 1e-08
ADAM_WD = 0.01
ADAM_STEP = 10
PER_EXAMPLE_BATCH_AXIS = {'x': 0, 'loss_target': 0}
SHARED_INPUTS = []
_WEIGHT_DTYPES = {'norm_w': _jnp.float32, 'w_in': _jnp.float32, 'conv_w': _jnp.float32, 'conv_b': _jnp.float32, 'dt_bias': _jnp.float32, 'a_log': _jnp.float32, 'd_skip': _jnp.float32, 'ssm_norm_w': _jnp.float32, 'w_attn_out': _jnp.float32, 'w_ssm_out': _jnp.float32, 'w_o': _jnp.float32, 'final_norm_w': _jnp.float32}
MOMENT_SCALE = {'norm_w': 1.597317e-01, 'w_in': 4.202042e-02, 'conv_w': 5.112296e-02, 'conv_b': 7.873374e-02, 'dt_bias': 2.041793e-01, 'a_log': 1.533078e-01, 'd_skip': 3.450966e-01, 'ssm_norm_w': 6.367041e-02, 'w_attn_out': 3.179888e-02, 'w_ssm_out': 8.326739e-02, 'w_o': 8.904598e-02, 'final_norm_w': 3.202136e+01}


def _to_microbatches(a, axis):
    t = _jnp.moveaxis(a, axis, 0)
    t = t.reshape((N_MICROBATCH, t.shape[0] // N_MICROBATCH) + t.shape[1:])
    return _jnp.moveaxis(t, 1, axis + 1)


def setup_inputs(seed: int = 0) -> dict:
    inp = _fwd_setup_inputs(seed)
    key = _jax.random.fold_in(_jax.random.key(seed), 7919)
    shape, _ = _output_shape()
    out = dict(inp)
    out["loss_target"] = _jax.random.normal(_jax.random.fold_in(key, 0), shape, _jnp.float32)
    for i, name in enumerate(TWIN_WEIGHTS):
        w = inp[name].astype(_jnp.float32)
        if MOMENT_SCALE is None:
            s = _jnp.sqrt(_jnp.mean(_jnp.square(w)) + 1e-30)
        else:
            s = MOMENT_SCALE[name]
        km, kv = _jax.random.split(_jax.random.fold_in(key, i + 1))
        out[name] = w
        out["m_" + name] = s * _jax.random.normal(km, w.shape, _jnp.float32)
        out["v_" + name] = (s * s) * _jax.random.uniform(kv, w.shape, _jnp.float32, 0.5, 1.5)
    if N_MICROBATCH > 1:
        for name, axis in PER_EXAMPLE_BATCH_AXIS.items():
            out[name] = _to_microbatches(out[name], axis)
    return {'x': out['x'], 'norm_w': out['norm_w'], 'w_in': out['w_in'], 'conv_w': out['conv_w'], 'conv_b': out['conv_b'], 'dt_bias': out['dt_bias'], 'a_log': out['a_log'], 'd_skip': out['d_skip'], 'ssm_norm_w': out['ssm_norm_w'], 'w_attn_out': out['w_attn_out'], 'w_ssm_out': out['w_ssm_out'], 'w_o': out['w_o'], 'final_norm_w': out['final_norm_w'], 'loss_target': out['loss_target'], 'm_norm_w': out['m_norm_w'], 'm_w_in': out['m_w_in'], 'm_conv_w': out['m_conv_w'], 'm_conv_b': out['m_conv_b'], 'm_dt_bias': out['m_dt_bias'], 'm_a_log': out['m_a_log'], 'm_d_skip': out['m_d_skip'], 'm_ssm_norm_w': out['m_ssm_norm_w'], 'm_w_attn_out': out['m_w_attn_out'], 'm_w_ssm_out': out['m_w_ssm_out'], 'm_w_o': out['m_w_o'], 'm_final_norm_w': out['m_final_norm_w'], 'v_norm_w': out['v_norm_w'], 'v_w_in': out['v_w_in'], 'v_conv_w': out['v_conv_w'], 'v_conv_b': out['v_conv_b'], 'v_dt_bias': out['v_dt_bias'], 'v_a_log': out['v_a_log'], 'v_d_skip': out['v_d_skip'], 'v_ssm_norm_w': out['v_ssm_norm_w'], 'v_w_attn_out': out['v_w_attn_out'], 'v_w_ssm_out': out['v_w_ssm_out'], 'v_w_o': out['v_w_o'], 'v_final_norm_w': out['v_final_norm_w']}


def _loss(weights, diff, rest, loss_target):
    with _jax.named_scope("forward"):
        args = {**rest, TWIN_DIFF_INPUT: diff, **{k: w.astype(_WEIGHT_DTYPES[k]) for k, w in weights.items()}}
        y = _forward(args)
    with _jax.named_scope("loss_head"):
        err = _jnp.square(y.astype(_jnp.float32) - loss_target)
        return 0.5 * _jnp.sum(_jnp.mean(err, axis=-1)) if err.ndim else 0.5 * err


def _adamw(w, g, m, v):
    m = ADAM_B1 * m + (1.0 - ADAM_B1) * g
    v = ADAM_B2 * v + (1.0 - ADAM_B2) * _jnp.square(g)
    m_hat = m / (1.0 - ADAM_B1 ** ADAM_STEP)
    v_hat = v / (1.0 - ADAM_B2 ** ADAM_STEP)
    delta = -ADAM_LR * (m_hat / (_jnp.sqrt(v_hat) + ADAM_EPS) + ADAM_WD * w)
    return delta, m, v


def reference(x, norm_w, w_in, conv_w, conv_b, dt_bias, a_log, d_skip, ssm_norm_w, w_attn_out, w_ssm_out, w_o, final_norm_w, loss_target, m_norm_w, m_w_in, m_conv_w, m_conv_b, m_dt_bias, m_a_log, m_d_skip, m_ssm_norm_w, m_w_attn_out, m_w_ssm_out, m_w_o, m_final_norm_w, v_norm_w, v_w_in, v_conv_w, v_conv_b, v_dt_bias, v_a_log, v_d_skip, v_ssm_norm_w, v_w_attn_out, v_w_ssm_out, v_w_o, v_final_norm_w):
    given = dict(x=x, norm_w=norm_w, w_in=w_in, conv_w=conv_w, conv_b=conv_b, dt_bias=dt_bias, a_log=a_log, d_skip=d_skip, ssm_norm_w=ssm_norm_w, w_attn_out=w_attn_out, w_ssm_out=w_ssm_out, w_o=w_o, final_norm_w=final_norm_w, loss_target=loss_target, m_norm_w=m_norm_w, m_w_in=m_w_in, m_conv_w=m_conv_w, m_conv_b=m_conv_b, m_dt_bias=m_dt_bias, m_a_log=m_a_log, m_d_skip=m_d_skip, m_ssm_norm_w=m_ssm_norm_w, m_w_attn_out=m_w_attn_out, m_w_ssm_out=m_w_ssm_out, m_w_o=m_w_o, m_final_norm_w=m_final_norm_w, v_norm_w=v_norm_w, v_w_in=v_w_in, v_conv_w=v_conv_w, v_conv_b=v_conv_b, v_dt_bias=v_dt_bias, v_a_log=v_a_log, v_d_skip=v_d_skip, v_ssm_norm_w=v_ssm_norm_w, v_w_attn_out=v_w_attn_out, v_w_ssm_out=v_w_ssm_out, v_w_o=v_w_o, v_final_norm_w=v_final_norm_w)
    weights = {n: given[n] for n in TWIN_WEIGHTS}
    shared = {n: given[n] for n in SHARED_INPUTS}
    per_example = {n: given[n] for n in ['x']}
    grad_fn = _jax.value_and_grad(_loss, argnums=(0, 1))

    def one_microbatch(ex, loss_target):
        ex = dict(ex)
        diff = ex.pop(TWIN_DIFF_INPUT)
        return grad_fn(weights, diff, {**shared, **ex}, loss_target)

    if N_MICROBATCH == 1:
        loss, (grad_w, grad_x) = one_microbatch(per_example, given["loss_target"])
    else:
        def body(carry, xs):
            loss_sum, grad_sum = carry
            l_k, (gw_k, gx_k) = one_microbatch(xs[0], xs[1])
            with _jax.named_scope("update"):
                return (loss_sum + l_k, _jax.tree.map(_jnp.add, grad_sum, gw_k)), gx_k

        init = (_jnp.zeros((), _jnp.float32), _jax.tree.map(_jnp.zeros_like, weights))
        (loss, grad_w), grad_x = _jax.lax.scan(body, init, (per_example, given["loss_target"]))
    with _jax.named_scope("update"):
        delta_w, new_m, new_v = {}, {}, {}
        for n in TWIN_WEIGHTS:
            delta_w[n], new_m[n], new_v[n] = _adamw(weights[n], grad_w[n], given["m_" + n], given["v_" + n])
    return (loss, grad_x, *[grad_w[n] for n in TWIN_WEIGHTS], *[delta_w[n] for n in TWIN_WEIGHTS],
            *[new_m[n] for n in TWIN_WEIGHTS], *[new_v[n] for n in TWIN_WEIGHTS])
```

```python
import functools

import jax
import jax.numpy as jnp
from jax import lax
from jax.experimental import pallas as pl
from jax.experimental.pallas import tpu as pltpu

F32 = jnp.float32
BF16 = jnp.bfloat16

D_MODEL = 1024
SB_WIDTH = 1024
HEAD_DIM = 64
SSD_WIDTH = 2048
SSD_GROUPS = 4
SSD_HPG = 8
SSD_STATE = 128
SSD_CHUNK = 128
CONV_K = 4
CONV_DIM = 3072
N_DT = 32
D_PROJ = 11296
N_MAIN = 11264
DT_COL0 = 9216
EPS = 1e-6
N_DEV = 8

ADAM_LR = 0.001
ADAM_B1 = 0.9
ADAM_B2 = 0.999
ADAM_EPS = 1e-08
ADAM_WD = 0.01
ADAM_STEP = 10

LANES = 128
VMEM_LIMIT = 56 * 1024 * 1024

CB_Q, CB_K, CB_V, CB_ZA, CB_ZS, CB_XBC, CB_GATE = 0, 8, 16, 24, 32, 48, 72

NT_DIMS = (((1,), (1,)), ((), ()))
TN_DIMS = (((0,), (0,)), ((), ()))
HI = lax.Precision.HIGHEST


def _params(sem):
    return pltpu.CompilerParams(dimension_semantics=sem, vmem_limit_bytes=VMEM_LIMIT)


def _dot(a, b):
    return jnp.dot(a, b, preferred_element_type=F32)


def _dot_nt(a, b):
    return lax.dot_general(a, b, NT_DIMS, preferred_element_type=F32)


def _dot_tn(a, b):
    return lax.dot_general(a, b, TN_DIMS, preferred_element_type=F32)


def _dot_hi(a, b):
    return jnp.dot(a, b, preferred_element_type=F32, precision=HI)


def _bf(a):
    return a.astype(BF16)


def _sigmoid(x):
    return 1.0 / (1.0 + jnp.exp(-x))


def _softplus(x):
    return jnp.maximum(x, 0.0) + jnp.log(1.0 + jnp.exp(-jnp.abs(x)))


def _split_dot(a, u):
    hi = _bf(a)
    lo = _bf(a - hi.astype(F32))
    return _dot(hi, u) + _dot(lo, u)


def _matmul(a, b, *, name, nt=False, add=None, out_dtype=F32, tm=512, tn=512, tk=1024):
    m, k = a.shape
    n = b.shape[0] if nt else b.shape[1]
    tm, tn, tk = min(tm, m), min(tn, n), min(tk, k)
    assert m % tm == 0 and n % tn == 0 and k % tk == 0, (name, a.shape, b.shape)
    nk = k // tk
    has_add = add is not None

    def body(*refs):
        if has_add:
            a_ref, b_ref, add_ref, o_ref, acc_ref = refs
        else:
            a_ref, b_ref, o_ref, acc_ref = refs
        kk = pl.program_id(2)

        @pl.when(kk == 0)
        def _():
            acc_ref[...] = jnp.zeros_like(acc_ref)

        av, bv = _bf(a_ref[...]), _bf(b_ref[...])
        acc_ref[...] += _dot_nt(av, bv) if nt else _dot(av, bv)

        @pl.when(kk == nk - 1)
        def _():
            r = acc_ref[...]
            if has_add:
                r = r + add_ref[...]
            o_ref[...] = r.astype(out_dtype)

    in_specs = [pl.BlockSpec((tm, tk), lambda j, i, kk: (i, kk)),
                pl.BlockSpec((tn, tk), lambda j, i, kk: (j, kk)) if nt
                else pl.BlockSpec((tk, tn), lambda j, i, kk: (kk, j))]
    args = [a, b]
    if has_add:
        in_specs.append(pl.BlockSpec((tm, tn), lambda j, i, kk: (i, j)))
        args.append(add)
    return pl.pallas_call(
        body, grid=(n // tn, m // tm, nk), in_specs=in_specs,
        out_specs=pl.BlockSpec((tm, tn), lambda j, i, kk: (i, j)),
        out_shape=jax.ShapeDtypeStruct((m, n), out_dtype),
        scratch_shapes=[pltpu.VMEM((tm, tn), F32)],
        compiler_params=_params(("parallel", "parallel", "arbitrary")), name=name)(*args)


def _rms_fwd(x, w, *, tm=512):
    t = x.shape[0]

    def body(x_ref, w_ref, h_ref, r_ref):
        xv = x_ref[...]
        r = lax.rsqrt(jnp.mean(xv * xv, axis=-1, keepdims=True) + EPS)
        h_ref[...] = _bf(xv * r * w_ref[...])
        r_ref[...] = r

    row = lambda i: (i, 0)
    return pl.pallas_call(
        body, grid=(t // tm,),
        in_specs=[pl.BlockSpec((tm, D_MODEL), row), pl.BlockSpec((1, D_MODEL), lambda i: (0, 0))],
        out_specs=[pl.BlockSpec((tm, D_MODEL), row), pl.BlockSpec((tm, 1), row)],
        out_shape=[jax.ShapeDtypeStruct((t, D_MODEL), BF16), jax.ShapeDtypeStruct((t, 1), F32)],
        compiler_params=_params(("parallel",)), name="rms_fwd")(x, w)


def _rms_bwd(dh, x, r, w, dres, *, tm=512):
    t = x.shape[0]

    def body(dh_ref, x_ref, r_ref, w_ref, dres_ref, dx_ref, dw_ref):
        @pl.when(pl.program_id(0) == 0)
        def _():
            dw_ref[...] = jnp.zeros_like(dw_ref)

        nrm = x_ref[...] * r_ref[...]
        dhv = dh_ref[...]
        dw_ref[...] += jnp.sum(dhv * nrm, axis=0, keepdims=True)
        dn = dhv * w_ref[...]
        dx = r_ref[...] * (dn - nrm * jnp.mean(dn * nrm, axis=-1, keepdims=True))
        dx_ref[...] = dx + dres_ref[...]

    row = lambda i: (i, 0)
    fix = lambda i: (0, 0)
    return pl.pallas_call(
        body, grid=(t // tm,),
        in_specs=[pl.BlockSpec((tm, D_MODEL), row), pl.BlockSpec((tm, D_MODEL), row),
                  pl.BlockSpec((tm, 1), row), pl.BlockSpec((1, D_MODEL), fix),
                  pl.BlockSpec((tm, D_MODEL), row)],
        out_specs=[pl.BlockSpec((tm, D_MODEL), row), pl.BlockSpec((1, D_MODEL), fix)],
        out_shape=[jax.ShapeDtypeStruct((t, D_MODEL), F32), jax.ShapeDtypeStruct((1, D_MODEL), F32)],
        compiler_params=_params(("arbitrary",)), name="rms_bwd")(dh, x, r, w, dres)


ATT_BLK = 256


def _attn_masks(blk):
    r_io = lax.broadcasted_iota(jnp.int32, (blk, blk), 0)
    c_io = lax.broadcasted_iota(jnp.int32, (blk, blk), 1)
    return r_io, c_io


def _attn_fwd(proj, nb, s):
    blk = min(ATT_BLK, s)
    nq = s // blk
    scale = HEAD_DIM ** -0.5

    def body(q_ref, k_ref, v_ref, o_ref, tot_ref):
        lane = lax.broadcasted_iota(jnp.int32, (blk, LANES), 1)
        lane2 = lax.broadcasted_iota(jnp.int32, (blk, 2), 1)
        r_io, c_io = _attn_masks(blk)
        u_gt = _bf(r_io > c_io)

        def qblock(qi, _):
            q0 = pl.multiple_of(qi * blk, blk)
            qf = q_ref[pl.ds(q0, blk), :]
            accs, tots = [], []
            for hh in range(2):
                hm = (lane >= HEAD_DIM) if hh else (lane < HEAD_DIM)
                qh = _bf(jnp.where(hm, qf, 0.0))

                def kblock(j, carry, qh=qh):
                    acc, cr = carry
                    k0 = pl.multiple_of((qi - j) * blk, blk)
                    kb = _bf(k_ref[pl.ds(k0, blk), :])
                    vb = _bf(v_ref[pl.ds(k0, blk), :])
                    z = _dot_nt(qh, kb) * scale
                    mask = (c_io + k0) < (r_io + q0)
                    lneg = -_softplus(z)
                    lom = jnp.where(mask, lneg, 0.0)
                    later = _split_dot(lom, u_gt) + cr
                    a = jnp.where(mask, jnp.exp(lneg + z + later), 0.0)
                    acc = acc + _dot(_bf(a), vb)
                    cr = cr + jnp.sum(lom, axis=1, keepdims=True)
                    return acc, cr

                acc, cr = lax.fori_loop(
                    0, qi + 1, kblock,
                    (jnp.zeros((blk, LANES), F32), jnp.zeros((blk, 1), F32)))
                accs.append(acc)
                tots.append(cr)
            o_ref[pl.ds(q0, blk), :] = jnp.where(lane < HEAD_DIM, accs[0], accs[1])
            tot_ref[pl.ds(q0, blk), :] = jnp.where(lane2 == 0, tots[0], tots[1])
            return 0

        lax.fori_loop(0, nq, qblock, 0)

    t = nb * s
    return pl.pallas_call(
        body, grid=(nb, 8),
        in_specs=[pl.BlockSpec((s, LANES), lambda b, p: (b, CB_Q + p)),
                  pl.BlockSpec((s, LANES), lambda b, p: (b, CB_K + p)),
                  pl.BlockSpec((s, LANES), lambda b, p: (b, CB_V + p))],
        out_specs=[pl.BlockSpec((s, LANES), lambda b, p: (b, p)),
                   pl.BlockSpec((None, None, s, 2), lambda b, p: (b, p, 0, 0))],
        out_shape=[jax.ShapeDtypeStruct((t, SB_WIDTH), F32),
                   jax.ShapeDtypeStruct((nb, 8, s, 2), F32)],
        compiler_params=_params(("parallel", "parallel")), name="attn_fwd")(proj, proj, proj)


def _attn_bwd(proj, do, tot, nb, s):
    blk = min(ATT_BLK, s)
    nq = s // blk
    scale = HEAD_DIM ** -0.5

    def body(q_ref, k_ref, v_ref, do_ref, tot_ref, dq_ref, dk_ref, dv_ref):
        lane = lax.broadcasted_iota(jnp.int32, (blk, LANES), 1)
        r_io, c_io = _attn_masks(blk)
        u_le = _bf(r_io <= c_io)
        u_lt = _bf(r_io < c_io)
        dk_ref[...] = jnp.zeros_like(dk_ref)
        dv_ref[...] = jnp.zeros_like(dv_ref)

        def qblock(qi, _):
            q0 = pl.multiple_of(qi * blk, blk)
            qf = q_ref[pl.ds(q0, blk), :]
            dof = do_ref[pl.ds(q0, blk), :]
            totf = tot_ref[pl.ds(q0, blk), :]
            dqs = []
            for hh in range(2):
                hm = (lane >= HEAD_DIM) if hh else (lane < HEAD_DIM)
                qh = _bf(jnp.where(hm, qf, 0.0))
                doh = _bf(jnp.where(hm, dof, 0.0))
                tot_h = totf[:, hh:hh + 1]

                def kblock(j, carry, hm=hm, qh=qh, doh=doh, tot_h=tot_h):
                    dq, crl, crw = carry
                    k0 = pl.multiple_of(j * blk, blk)
                    kf = k_ref[pl.ds(k0, blk), :]
                    kb = _bf(kf)
                    kh = _bf(jnp.where(hm, kf, 0.0))
                    vb = _bf(v_ref[pl.ds(k0, blk), :])
                    z = _dot_nt(qh, kb) * scale
                    mask = (c_io + k0) < (r_io + q0)
                    lneg = -_softplus(z)
                    lom = jnp.where(mask, lneg, 0.0)
                    later = tot_h - (crl + _split_dot(lom, u_le))
                    lb = lneg + z
                    a = jnp.where(mask, jnp.exp(lb + later), 0.0)
                    da = _dot_nt(doh, vb)
                    w = a * da
                    wex = crw + _split_dot(w, u_lt)
                    sig = jnp.exp(lb)
                    dz = _bf(jnp.where(mask, w * (1.0 - sig) - wex * sig, 0.0) * scale)
                    dq = dq + _dot(dz, kh)
                    dk_ref[pl.ds(k0, blk), :] += _dot_tn(dz, qh)
                    dv_ref[pl.ds(k0, blk), :] += _dot_tn(_bf(a), doh)
                    crl = crl + jnp.sum(lom, axis=1, keepdims=True)
                    crw = crw + jnp.sum(w, axis=1, keepdims=True)
                    return dq, crl, crw

                zc = jnp.zeros((blk, 1), F32)
                dq, _, _ = lax.fori_loop(0, qi + 1, kblock, (jnp.zeros((blk, LANES), F32), zc, zc))
                dqs.append(dq)
            dq_ref[pl.ds(q0, blk), :] = dqs[0] + dqs[1]
            return 0

        lax.fori_loop(0, nq, qblock, 0)

    t = nb * s
    hp = lambda b, p: (b, p)
    out = jax.ShapeDtypeStruct((t, SB_WIDTH), F32)
    return pl.pallas_call(
        body, grid=(nb, 8),
        in_specs=[pl.BlockSpec((s, LANES), lambda b, p: (b, CB_Q + p)),
                  pl.BlockSpec((s, LANES), lambda b, p: (b, CB_K + p)),
                  pl.BlockSpec((s, LANES), lambda b, p: (b, CB_V + p)),
                  pl.BlockSpec((s, LANES), hp),
                  pl.BlockSpec((None, None, s, 2), lambda b, p: (b, p, 0, 0))],
        out_specs=[pl.BlockSpec((s, LANES), hp)] * 3,
        out_shape=[out, out, out],
        compiler_params=_params(("parallel", "parallel")), name="attn_bwd")(proj, proj, proj, do, tot)


CONV_TC = 256


def _shift_down(x, k, row):
    return jnp.where(row >= k, pltpu.roll(x, k, 0), 0.0)


def _shift_up(x, k, row, s):
    return jnp.where(row < s - k, pltpu.roll(x, s - k, 0), 0.0)


def _conv_fwd(proj, conv_w, conv_b, nb, s):
    tc = CONV_TC
    cb0 = CB_XBC * LANES // tc

    def body(x_ref, w_ref, b_ref, o_ref):
        x = x_ref[...]
        row = lax.broadcasted_iota(jnp.int32, x.shape, 0)
        y = b_ref[...] + w_ref[CONV_K - 1:CONV_K, :] * x
        for kk in range(1, CONV_K):
            y = y + w_ref[CONV_K - 1 - kk:CONV_K - kk, :] * _shift_down(x, kk, row)
        o_ref[...] = y * _sigmoid(y)

    t = nb * s
    return pl.pallas_call(
        body, grid=(nb, CONV_DIM // tc),
        in_specs=[pl.BlockSpec((s, tc), lambda b, c: (b, cb0 + c)),
                  pl.BlockSpec((CONV_K, tc), lambda b, c: (0, c)),
                  pl.BlockSpec((1, tc), lambda b, c: (0, c))],
        out_specs=pl.BlockSpec((s, tc), lambda b, c: (b, c)),
        out_shape=jax.ShapeDtypeStruct((t, CONV_DIM), F32),
        compiler_params=_params(("parallel", "parallel")), name="conv_fwd")(proj, conv_w, conv_b)


def _conv_bwd(proj, dact, conv_w, conv_b, nb, s, *, col0, name):
    width = dact.shape[1]
    tc = min(CONV_TC, width)
    cb0 = (CB_XBC * LANES + col0) // tc
    wb0 = col0 // tc

    def body(x_ref, d_ref, w_ref, b_ref, dx_ref, dw_ref, db_ref):
        @pl.when(pl.program_id(1) == 0)
        def _():
            dw_ref[...] = jnp.zeros_like(dw_ref)
            db_ref[...] = jnp.zeros_like(db_ref)

        x = x_ref[...]
        row = lax.broadcasted_iota(jnp.int32, x.shape, 0)
        xs = [x] + [_shift_down(x, kk, row) for kk in range(1, CONV_K)]
        y = b_ref[...] + w_ref[CONV_K - 1:CONV_K, :] * x
        for kk in range(1, CONV_K):
            y = y + w_ref[CONV_K - 1 - kk:CONV_K - kk, :] * xs[kk]
        sg = _sigmoid(y)
        dy = d_ref[...] * (sg * (1.0 + y * (1.0 - sg)))
        dx = w_ref[CONV_K - 1:CONV_K, :] * dy
        for kk in range(1, CONV_K):
            dx = dx + w_ref[CONV_K - 1 - kk:CONV_K - kk, :] * _shift_up(dy, kk, row, s)
        dx_ref[...] = dx
        db_ref[...] += jnp.sum(dy, axis=0, keepdims=True)
        for kk in range(CONV_K):
            dw_ref[CONV_K - 1 - kk:CONV_K - kk, :] += jnp.sum(dy * xs[kk], axis=0, keepdims=True)

    t = nb * s
    return pl.pallas_call(
        body, grid=(width // tc, nb),
        in_specs=[pl.BlockSpec((s, tc), lambda c, b: (b, cb0 + c)),
                  pl.BlockSpec((s, tc), lambda c, b: (b, c)),
                  pl.BlockSpec((CONV_K, tc), lambda c, b: (0, wb0 + c)),
                  pl.BlockSpec((1, tc), lambda c, b: (0, wb0 + c))],
        out_specs=[pl.BlockSpec((s, tc), lambda c, b: (b, c)),
                   pl.BlockSpec((CONV_K, tc), lambda c, b: (0, c)),
                   pl.BlockSpec((1, tc), lambda c, b: (0, c))],
        out_shape=[jax.ShapeDtypeStruct((t, width), F32),
                   jax.ShapeDtypeStruct((CONV_K, width), F32),
                   jax.ShapeDtypeStruct((1, width), F32)],
        compiler_params=_params(("parallel", "arbitrary")), name=name)(proj, dact, conv_w, conv_b)


L = SSD_CHUNK
N_PAIR = SSD_HPG // 2


def _pair_lanes(v0, v1, lane):
    return jnp.where(lane < HEAD_DIM, v0, v1)


def _ssd_chunk_common(dtr_ref, bias_ref, alog_ref, r0):
    rr = lax.broadcasted_iota(jnp.int32, (L, L), 0)
    cc = lax.broadcasted_iota(jnp.int32, (L, L), 1)
    tri = (cc <= rr).astype(F32)
    raw = dtr_ref[pl.ds(r0, L), :] + bias_ref[...]
    dt = _softplus(raw)
    a_neg = -jnp.exp(alog_ref[...])
    acs = _dot_hi(tri, dt * a_neg)
    return raw, dt, a_neg, acs, acs.T, rr, cc


def _ssd_fwd(xbc, dtr, bias, alog, dskip, nb, s):
    nc = s // L
    t = nb * s

    def body(x_ref, b_ref, c_ref, dtr_ref, bias_ref, alog_ref, dsk_ref, y_ref, hs_ref, h_scr):
        lane = lax.broadcasted_iota(jnp.int32, (L, LANES), 1)
        lane1 = lax.broadcasted_iota(jnp.int32, (1, LANES), 1)
        rowc = lax.broadcasted_iota(jnp.int32, (L, 1), 0)
        h_scr[...] = jnp.zeros_like(h_scr)
        dsk = dsk_ref[...]

        def chunk(c, _):
            r0 = pl.multiple_of(c * L, L)
            raw, dt, a_neg, acs, acs_t, rr, cc = _ssd_chunk_common(dtr_ref, bias_ref, alog_ref, r0)
            causal = rr >= cc
            bb = _bf(b_ref[pl.ds(r0, L), :])
            cb = _bf(c_ref[pl.ds(r0, L), :])
            gm = _dot_nt(cb, bb)
            for pr in range(N_PAIR):
                j0, j1 = 2 * pr, 2 * pr + 1
                ms = []
                for j in (j0, j1):
                    seg = acs[:, j:j + 1] - acs_t[j:j + 1, :]
                    ms.append(_bf(gm * jnp.exp(jnp.where(causal, seg, -1e30))))
                x = x_ref[pl.ds(r0, L), pr * LANES:(pr + 1) * LANES]
                dt_p = _pair_lanes(dt[:, j0:j0 + 1], dt[:, j1:j1 + 1], lane)
                acs_p = _pair_lanes(acs[:, j0:j0 + 1], acs[:, j1:j1 + 1], lane)
                last_p = acs_p[L - 1:L, :]
                d_p = _pair_lanes(dsk[:, j0:j0 + 1], dsk[:, j1:j1 + 1], lane1)
                xd = x * dt_p
                xdb = _bf(xd)
                yd = _pair_lanes(_dot(ms[0], xdb), _dot(ms[1], xdb), lane)
                hp = h_scr[pr]
                yo = _dot_nt(cb, _bf(hp)) * jnp.exp(acs_p)
                y_ref[pl.ds(r0, L), pr * LANES:(pr + 1) * LANES] = yd + yo + x * d_p
                hs_ref[c, pr] = hp
                sp = _dot_tn(_bf(xd * jnp.exp(last_p - acs_p)), bb)
                cd = jnp.where(rowc < HEAD_DIM, jnp.exp(acs[L - 1:L, j0:j0 + 1]),
                               jnp.exp(acs[L - 1:L, j1:j1 + 1]))
                h_scr[pr] = hp * cd + sp
            return 0

        lax.fori_loop(0, nc, chunk, 0)

    grp = lambda b, g: (g, 0, 0)
    return pl.pallas_call(
        body, grid=(nb, SSD_GROUPS),
        in_specs=[pl.BlockSpec((s, 4 * LANES), lambda b, g: (b, g)),
                  pl.BlockSpec((s, LANES), lambda b, g: (b, 16 + g)),
                  pl.BlockSpec((s, LANES), lambda b, g: (b, 20 + g)),
                  pl.BlockSpec((None, s, LANES), lambda b, g: (g, b, 0)),
                  pl.BlockSpec((None, 1, LANES), grp), pl.BlockSpec((None, 1, LANES), grp),
                  pl.BlockSpec((None, 1, LANES), grp)],
        out_specs=[pl.BlockSpec((s, 4 * LANES), lambda b, g: (b, g)),
                   pl.BlockSpec((None, None, nc, N_PAIR, L, LANES), lambda b, g: (b, g, 0, 0, 0, 0))],
        out_shape=[jax.ShapeDtypeStruct((t, SSD_WIDTH), F32),
                   jax.ShapeDtypeStruct((nb, SSD_GROUPS, nc, N_PAIR, L, LANES), F32)],
        scratch_shapes=[pltpu.VMEM((N_PAIR, L, LANES), F32)],
        compiler_params=_params(("parallel", "parallel")), name="ssd_fwd")(
            xbc, xbc, xbc, dtr, bias, alog, dskip)


def _ssd_bwd(xbc, dtr, bias, alog, dskip, hs, dy, nb, s):
    nc = s // L
    t = nb * s

    def body(x_ref, b_ref, c_ref, dtr_ref, bias_ref, alog_ref, dsk_ref, hs_ref, dy_ref,
             dx_ref, db_ref, dc_ref, ddtr_ref, dbias_ref, dalog_ref, ddsk_ref, dh_scr):
        lane = lax.broadcasted_iota(jnp.int32, (L, LANES), 1)
        lane1 = lax.broadcasted_iota(jnp.int32, (1, LANES), 1)
        rowc = lax.broadcasted_iota(jnp.int32, (L, 1), 0)
        dh_scr[...] = jnp.zeros_like(dh_scr)
        dsk = dsk_ref[...]

        @pl.when(pl.program_id(1) == 0)
        def _():
            dbias_ref[...] = jnp.zeros_like(dbias_ref)
            dalog_ref[...] = jnp.zeros_like(dalog_ref)
            ddsk_ref[...] = jnp.zeros_like(ddsk_ref)

        def seg_sums(v):
            s0 = jnp.sum(jnp.where(lane < HEAD_DIM, v, 0.0), axis=1, keepdims=True)
            s1 = jnp.sum(jnp.where(lane >= HEAD_DIM, v, 0.0), axis=1, keepdims=True)
            return s0, s1

        def chunk(i, _):
            c = nc - 1 - i
            r0 = pl.multiple_of(c * L, L)
            raw, dt, a_neg, acs, acs_t, rr, cc = _ssd_chunk_common(dtr_ref, bias_ref, alog_ref, r0)
            causal = rr >= cc
            tri_t = (cc >= rr).astype(F32)
            bb = _bf(b_ref[pl.ds(r0, L), :])
            cb = _bf(c_ref[pl.ds(r0, L), :])
            gm = _dot_nt(cb, bb)
            dg = jnp.zeros((L, L), F32)
            dbacc = jnp.zeros((L, LANES), F32)
            dcacc = jnp.zeros((L, LANES), F32)
            dacs = jnp.zeros((L, LANES), F32)
            dacs_t = jnp.zeros((L, L), F32)
            ddt = jnp.zeros((L, LANES), F32)
            last_add = jnp.zeros((1, LANES), F32)
            dd_add = jnp.zeros((1, LANES), F32)
            for pr in range(N_PAIR):
                js = (2 * pr, 2 * pr + 1)
                lms, mfs = [], []
                for j in js:
                    seg = acs[:, j:j + 1] - acs_t[j:j + 1, :]
                    lm = jnp.exp(jnp.where(causal, seg, -1e30))
                    lms.append(lm)
                    mfs.append(gm * lm)
                x = x_ref[pl.ds(r0, L), pr * LANES:(pr + 1) * LANES]
                dyv = dy_ref[pl.ds(r0, L), pr * LANES:(pr + 1) * LANES]
                dt_p = _pair_lanes(dt[:, js[0]:js[0] + 1], dt[:, js[1]:js[1] + 1], lane)
                acs_p = _pair_lanes(acs[:, js[0]:js[0] + 1], acs[:, js[1]:js[1] + 1], lane)
                last_p = acs_p[L - 1:L, :]
                d_p = _pair_lanes(dsk[:, js[0]:js[0] + 1], dsk[:, js[1]:js[1] + 1], lane1)
                e_p = jnp.exp(acs_p)
                dte_p = jnp.exp(last_p - acs_p)
                xd = x * dt_p
                xdb = _bf(xd)
                hp = hs_ref[c, pr]
                hb = _bf(hp)
                dye = _bf(dyv * e_p)
                yo = _dot_nt(cb, hb) * e_p
                dcacc = dcacc + _dot(dye, hb)
                dh_out = _dot_tn(dye, cb)
                dyb = _bf(dyv)
                dxd = jnp.zeros((L, LANES), F32)
                for hh in range(2):
                    hm = (lane >= HEAD_DIM) if hh else (lane < HEAD_DIM)
                    dm = _dot_nt(_bf(jnp.where(hm, dyv, 0.0)), xdb)
                    dxd = dxd + jnp.where(hm, _dot_tn(_bf(mfs[hh]), dyb), 0.0)
                    dg = dg + dm * lms[hh]
                    qm = dm * mfs[hh]
                    j = js[hh]
                    dacs = dacs + jnp.where(lane == j, jnp.sum(qm, axis=1, keepdims=True), 0.0)
                    dacs_t = dacs_t - jnp.where(rr == j, jnp.sum(qm, axis=0, keepdims=True), 0.0)
                ds = dh_scr[pr]
                dsb = _bf(ds)
                xdd = xd * dte_p
                dxdd = _dot_nt(bb, dsb)
                dbacc = dbacc + _dot(_bf(xdd), dsb)
                dxd = dxd + dxdd * dte_p
                cds = [jnp.exp(acs[L - 1:L, j:j + 1]) for j in js]
                cd = jnp.where(rowc < HEAD_DIM, cds[0], cds[1])
                prod = jnp.sum(ds * hp, axis=1, keepdims=True)
                dcds = [jnp.sum(jnp.where(rowc < HEAD_DIM, prod, 0.0), axis=0, keepdims=True),
                        jnp.sum(jnp.where(rowc >= HEAD_DIM, prod, 0.0), axis=0, keepdims=True)]
                dh_scr[pr] = dh_out + ds * cd
                ta = seg_sums(dyv * yo - dxdd * xdd)
                tc_ = seg_sums(dxdd * xdd)
                te = seg_sums(dxd * x)
                td = seg_sums(dyv * x)
                for hh in range(2):
                    j = js[hh]
                    dacs = dacs + jnp.where(lane == j, ta[hh], 0.0)
                    ddt = ddt + jnp.where(lane == j, te[hh], 0.0)
                    last_add = last_add + jnp.where(
                        lane1 == j, jnp.sum(tc_[hh], axis=0, keepdims=True) + dcds[hh] * cds[hh], 0.0)
                    dd_add = dd_add + jnp.where(lane1 == j, jnp.sum(td[hh], axis=0, keepdims=True), 0.0)
                dx_ref[pl.ds(r0, L), pr * LANES:(pr + 1) * LANES] = dxd * dt_p + dyv * d_p
            dgb = _bf(dg)
            dc_ref[pl.ds(r0, L), :] = dcacc + _dot(dgb, bb)
            db_ref[pl.ds(r0, L), :] = dbacc + _dot_tn(dgb, cb)
            dacs = dacs + dacs_t.T + jnp.where(rowc == L - 1, last_add, 0.0)
            ddta = _dot_hi(tri_t, dacs)
            ddt = ddt + ddta * a_neg
            ddraw = ddt * _sigmoid(raw)
            ddtr_ref[pl.ds(r0, L), :] = ddraw
            dbias_ref[...] += jnp.sum(ddraw, axis=0, keepdims=True)
            dalog_ref[...] += jnp.sum(ddta * dt, axis=0, keepdims=True) * a_neg
            ddsk_ref[...] += dd_add
            return 0

        lax.fori_loop(0, nc, chunk, 0)

    grp = lambda g, b: (g, 0, 0)
    small = jax.ShapeDtypeStruct((SSD_GROUPS, 1, LANES), F32)
    return pl.pallas_call(
        body, grid=(SSD_GROUPS, nb),
        in_specs=[pl.BlockSpec((s, 4 * LANES), lambda g, b: (b, g)),
                  pl.BlockSpec((s, LANES), lambda g, b: (b, 16 + g)),
                  pl.BlockSpec((s, LANES), lambda g, b: (b, 20 + g)),
                  pl.BlockSpec((None, s, LANES), lambda g, b: (g, b, 0)),
                  pl.BlockSpec((None, 1, LANES), grp), pl.BlockSpec((None, 1, LANES), grp),
                  pl.BlockSpec((None, 1, LANES), grp),
                  pl.BlockSpec((None, None, nc, N_PAIR, L, LANES), lambda g, b: (b, g, 0, 0, 0, 0)),
                  pl.BlockSpec((s, 4 * LANES), lambda g, b: (b, g))],
        out_specs=[pl.BlockSpec((s, 4 * LANES), lambda g, b: (b, g)),
                   pl.BlockSpec((s, LANES), lambda g, b: (b, g)),
                   pl.BlockSpec((s, LANES), lambda g, b: (b, g)),
                   pl.BlockSpec((None, s, LANES), lambda g, b: (g, b, 0)),
                   pl.BlockSpec((None, 1, LANES), grp), pl.BlockSpec((None, 1, LANES), grp),
                   pl.BlockSpec((None, 1, LANES), grp)],
        out_shape=[jax.ShapeDtypeStruct((t, SSD_WIDTH), F32),
                   jax.ShapeDtypeStruct((t, SSD_GROUPS * SSD_STATE), F32),
                   jax.ShapeDtypeStruct((t, SSD_GROUPS * SSD_STATE), F32),
                   jax.ShapeDtypeStruct((SSD_GROUPS, t, LANES), F32), small, small, small],
        scratch_shapes=[pltpu.VMEM((N_PAIR, L, LANES), F32)],
        compiler_params=_params(("parallel", "arbitrary")), name="ssd_bwd")(
            xbc, xbc, xbc, dtr, bias, alog, dskip, hs, dy)


GN = SSD_WIDTH // SSD_GROUPS


def _post_fwd(o, proj, y, nw, *, tm=256):
    t = o.shape[0]

    def body(o_ref, za_ref, y_ref, zs_ref, nw_ref, ya_ref, ys_ref):
        za = za_ref[...]
        ya_ref[...] = _bf(o_ref[...] * (za * _sigmoid(za)))
        zs = zs_ref[...]
        yz = y_ref[...] * (zs * _sigmoid(zs))
        for g in range(SSD_GROUPS):
            sl = slice(g * GN, (g + 1) * GN)
            v = yz[:, sl]
            r = lax.rsqrt(jnp.mean(v * v, axis=-1, keepdims=True) + EPS)
            ys_ref[:, sl] = _bf(v * r * nw_ref[:, sl])

    return pl.pallas_call(
        body, grid=(t // tm,),
        in_specs=[pl.BlockSpec((tm, SB_WIDTH), lambda i: (i, 0)),
                  pl.BlockSpec((tm, SB_WIDTH), lambda i: (i, CB_ZA * LANES // SB_WIDTH)),
                  pl.BlockSpec((tm, SSD_WIDTH), lambda i: (i, 0)),
                  pl.BlockSpec((tm, SSD_WIDTH), lambda i: (i, CB_ZS * LANES // SSD_WIDTH)),
                  pl.BlockSpec((1, SSD_WIDTH), lambda i: (0, 0))],
        out_specs=[pl.BlockSpec((tm, SB_WIDTH), lambda i: (i, 0)),
                   pl.BlockSpec((tm, SSD_WIDTH), lambda i: (i, 0))],
        out_shape=[jax.ShapeDtypeStruct((t, SB_WIDTH), BF16), jax.ShapeDtypeStruct((t, SSD_WIDTH), BF16)],
        compiler_params=_params(("parallel",)), name="post_fwd")(o, proj, y, proj, nw)


def _post_bwd(dya, o, proj, dys, y, nw, *, tm=256):
    t = o.shape[0]

    def body(dya_ref, o_ref, za_ref, dys_ref, y_ref, zs_ref, nw_ref,
             do_ref, dza_ref, dy_ref, dzs_ref, dnw_ref):
        @pl.when(pl.program_id(0) == 0)
        def _():
            dnw_ref[...] = jnp.zeros_like(dnw_ref)

        za = za_ref[...]
        sa = _sigmoid(za)
        d = dya_ref[...]
        do_ref[...] = d * (za * sa)
        dza_ref[...] = d * o_ref[...] * (sa * (1.0 + za * (1.0 - sa)))
        zs = zs_ref[...]
        ss = _sigmoid(zs)
        silu = zs * ss
        yv = y_ref[...]
        yz = yv * silu
        dv = dys_ref[...]
        for g in range(SSD_GROUPS):
            sl = slice(g * GN, (g + 1) * GN)
            v = yz[:, sl]
            r = lax.rsqrt(jnp.mean(v * v, axis=-1, keepdims=True) + EPS)
            nrm = v * r
            dg = dv[:, sl]
            dnw_ref[:, sl] += jnp.sum(dg * nrm, axis=0, keepdims=True)
            dn = dg * nw_ref[:, sl]
            dyz = r * (dn - nrm * jnp.mean(dn * nrm, axis=-1, keepdims=True))
            dy_ref[:, sl] = dyz * silu[:, sl]
            dzs_ref[:, sl] = dyz * yv[:, sl] * (ss[:, sl] * (1.0 + zs[:, sl] * (1.0 - ss[:, sl])))

    a_spec = pl.BlockSpec((tm, SB_WIDTH), lambda i: (i, 0))
    s_spec = pl.BlockSpec((tm, SSD_WIDTH), lambda i: (i, 0))
    w_spec = pl.BlockSpec((1, SSD_WIDTH), lambda i: (0, 0))
    return pl.pallas_call(
        body, grid=(t // tm,),
        in_specs=[a_spec, a_spec, pl.BlockSpec((tm, SB_WIDTH), lambda i: (i, CB_ZA * LANES // SB_WIDTH)),
                  s_spec, s_spec, pl.BlockSpec((tm, SSD_WIDTH), lambda i: (i, CB_ZS * LANES // SSD_WIDTH)),
                  w_spec],
        out_specs=[a_spec, a_spec, s_spec, s_spec, w_spec],
        out_shape=[jax.ShapeDtypeStruct((t, SB_WIDTH), F32), jax.ShapeDtypeStruct((t, SB_WIDTH), F32),
                   jax.ShapeDtypeStruct((t, SSD_WIDTH), F32), jax.ShapeDtypeStruct((t, SSD_WIDTH), F32),
                   jax.ShapeDtypeStruct((1, SSD_WIDTH), F32)],
        compiler_params=_params(("arbitrary",)), name="post_bwd")(dya, o, proj, dys, y, proj, nw)


def _merge_fwd(ya, ys, proj, *, tm=512):
    t = ya.shape[0]
    gb = CB_GATE * LANES // D_MODEL

    def body(ya_ref, ys_ref, ga_ref, gs_ref, m_ref):
        m_ref[...] = _bf(_sigmoid(ga_ref[...]) * ya_ref[...] + _sigmoid(gs_ref[...]) * ys_ref[...])

    spec = pl.BlockSpec((tm, D_MODEL), lambda i: (i, 0))
    return pl.pallas_call(
        body, grid=(t // tm,),
        in_specs=[spec, spec, pl.BlockSpec((tm, D_MODEL), lambda i: (i, gb)),
                  pl.BlockSpec((tm, D_MODEL), lambda i: (i, gb + 1))],
        out_specs=spec, out_shape=jax.ShapeDtypeStruct((t, D_MODEL), BF16),
        compiler_params=_params(("parallel",)), name="merge_fwd")(ya, ys, proj, proj)


def _merge_bwd(dm, ya, ys, proj, *, tm=512):
    t = ya.shape[0]
    gb = CB_GATE * LANES // D_MODEL

    def body(dm_ref, ya_ref, ys_ref, ga_ref, gs_ref, dya_ref, dys_ref, dg_ref):
        d = dm_ref[...]
        ga = _sigmoid(ga_ref[...])
        gs = _sigmoid(gs_ref[...])
        dya_ref[...] = _bf(d * ga)
        dys_ref[...] = _bf(d * gs)
        dg_ref[:, :D_MODEL] = d * ya_ref[...] * (ga * (1.0 - ga))
        dg_ref[:, D_MODEL:] = d * ys_ref[...] * (gs * (1.0 - gs))

    spec = pl.BlockSpec((tm, D_MODEL), lambda i: (i, 0))
    return pl.pallas_call(
        body, grid=(t // tm,),
        in_specs=[spec, spec, spec, pl.BlockSpec((tm, D_MODEL), lambda i: (i, gb)),
                  pl.BlockSpec((tm, D_MODEL), lambda i: (i, gb + 1))],
        out_specs=[spec, spec, pl.BlockSpec((tm, 2 * D_MODEL), lambda i: (i, 0))],
        out_shape=[jax.ShapeDtypeStruct((t, D_MODEL), BF16), jax.ShapeDtypeStruct((t, D_MODEL), BF16),
                   jax.ShapeDtypeStruct((t, 2 * D_MODEL), F32)],
        compiler_params=_params(("parallel",)), name="merge_bwd")(dm, ya, ys, proj, proj)


def _final(x, out, wf, target, *, tm=512):
    t = x.shape[0]

    def body(x_ref, o_ref, w_ref, t_ref, loss_ref, dx_ref, dw_ref):
        @pl.when(pl.program_id(0) == 0)
        def _():
            loss_ref[...] = jnp.zeros_like(loss_ref)
            dw_ref[...] = jnp.zeros_like(dw_ref)

        x2 = x_ref[...] + o_ref[...]
        r = lax.rsqrt(jnp.mean(x2 * x2, axis=-1, keepdims=True) + EPS)
        nrm = x2 * r
        e = nrm * w_ref[...] - t_ref[...]
        row_loss = jnp.mean(e * e, axis=-1, keepdims=True)
        loss_ref[...] += 0.5 * jnp.sum(row_loss, axis=0, keepdims=True)
        dyv = e * (1.0 / D_MODEL)
        dw_ref[...] += jnp.sum(dyv * nrm, axis=0, keepdims=True)
        dn = dyv * w_ref[...]
        dx_ref[...] = r * (dn - nrm * jnp.mean(dn * nrm, axis=-1, keepdims=True))

    spec = pl.BlockSpec((tm, D_MODEL), lambda i: (i, 0))
    fix = pl.BlockSpec((1, D_MODEL), lambda i: (0, 0))
    return pl.pallas_call(
        body, grid=(t // tm,),
        in_specs=[spec, spec, fix, spec],
        out_specs=[pl.BlockSpec((1, LANES), lambda i: (0, 0)), spec, fix],
        out_shape=[jax.ShapeDtypeStruct((1, LANES), F32), jax.ShapeDtypeStruct((t, D_MODEL), F32),
                   jax.ShapeDtypeStruct((1, D_MODEL), F32)],
        compiler_params=_params(("arbitrary",)), name="final_loss")(x, out, wf, target)


def _group_lanes(v):
    return jnp.pad(v.reshape(SSD_GROUPS, 1, SSD_HPG), ((0, 0), (0, 0), (0, LANES - SSD_HPG)))


def _ungroup_lanes(v):
    return v[:, :, :SSD_HPG].reshape(1, N_DT)


def _local_step(x, target, nb, s, w_main, w_dt, conv_w, conv_b, norm_w, dt_bias, a_log, d_skip,
                ssm_norm_w, w_attn_out, w_ssm_out, w_o, final_norm_w):
    t = nb * s
    h, r1 = _rms_fwd(x, norm_w)
    proj = _matmul(h, w_main, name="proj_main", tm=512, tn=1024)
    dt_raw = _matmul(h, w_dt, name="proj_dt")
    dtr = jnp.pad(dt_raw[:, :N_DT].reshape(t, SSD_GROUPS, SSD_HPG).transpose(1, 0, 2),
                  ((0, 0), (0, 0), (0, LANES - SSD_HPG)))
    bias_g, alog_g, dsk_g = _group_lanes(dt_bias), _group_lanes(a_log), _group_lanes(d_skip)

    o, tot = _attn_fwd(proj, nb, s)
    xbc = _conv_fwd(proj, conv_w, conv_b, nb, s)
    y, hs = _ssd_fwd(xbc, dtr, bias_g, alog_g, dsk_g, nb, s)
    ya_in, ys_in = _post_fwd(o, proj, y, ssm_norm_w)
    ya = _matmul(ya_in, w_attn_out, name="attn_out")
    ys = _matmul(ys_in, w_ssm_out, name="ssm_out", tk=2048)
    merged = _merge_fwd(ya, ys, proj)
    out = _matmul(merged, w_o, name="out_proj")
    loss, dx2, d_final_w = _final(x, out, final_norm_w, target)

    dx2b = _bf(dx2)
    dmerged = _matmul(dx2b, w_o, nt=True, name="d_merged")
    d_w_o = _matmul(merged.T, dx2b, name="d_w_o", tm=1024, tk=1024)
    dya, dys, dgate = _merge_bwd(dmerged, ya, ys, proj)
    d_w_attn_out = _matmul(ya_in.T, dya, name="d_w_attn_out", tm=1024)
    d_w_ssm_out = _matmul(ys_in.T, dys, name="d_w_ssm_out", tm=1024)
    dya_in = _matmul(dya, w_attn_out, nt=True, name="d_ya_in")
    dys_in = _matmul(dys, w_ssm_out, nt=True, name="d_ys_in")
    do, dza, dy, dzs, d_ssm_norm_w = _post_bwd(dya_in, o, proj, dys_in, y, ssm_norm_w)
    dq, dk, dv = _attn_bwd(proj, do, tot, nb, s)
    dxs_act, db_act, dc_act, ddtr, dbias_g, dalog_g, ddsk_g = _ssd_bwd(
        xbc, dtr, bias_g, alog_g, dsk_g, hs, dy, nb, s)
    dxs, dwx, dbx = _conv_bwd(proj, dxs_act, conv_w, conv_b, nb, s, col0=0, name="conv_bwd_x")
    dbm, dwb, dbb = _conv_bwd(proj, db_act, conv_w, conv_b, nb, s, col0=SSD_WIDTH, name="conv_bwd_b")
    dcm, dwc, dbc = _conv_bwd(proj, dc_act, conv_w, conv_b, nb, s, col0=SSD_WIDTH + 512, name="conv_bwd_c")
    dproj = jnp.concatenate([_bf(v) for v in (dq, dk, dv, dza, dzs, dxs, dbm, dcm, dgate)], axis=1)
    ddt = jnp.pad(_bf(ddtr[:, :, :SSD_HPG].transpose(1, 0, 2).reshape(t, N_DT)), ((0, 0), (0, LANES - N_DT)))
    ht = h.T
    d_w_main = _matmul(ht, dproj, name="d_w_main", tm=1024, tn=512, tk=1024)
    d_w_dt = _matmul(ht, ddt, name="d_w_dt", tm=1024)
    dh_dt = _matmul(ddt, w_dt, nt=True, name="d_h_dt")
    dh = _matmul(dproj, w_main, nt=True, add=dh_dt, name="d_h", tm=1024, tn=1024, tk=1024)
    grad_x, d_norm_w = _rms_bwd(dh, x, r1, norm_w, dx2)

    grads = dict(
        norm_w=d_norm_w, w_main=d_w_main, w_dt=d_w_dt,
        conv_w=jnp.concatenate([dwx, dwb, dwc], axis=1), conv_b=jnp.concatenate([dbx, dbb, dbc], axis=1),
        dt_bias=_ungroup_lanes(dbias_g), a_log=_ungroup_lanes(dalog_g), d_skip=_ungroup_lanes(ddsk_g),
        ssm_norm_w=d_ssm_norm_w, w_attn_out=d_w_attn_out, w_ssm_out=d_w_ssm_out, w_o=d_w_o,
        final_norm_w=d_final_w)
    return loss, grad_x, grads


def _exchange(scatter, bcast, *, name):
    arrays = list(scatter) + list(bcast)
    ns, n = len(scatter), len(arrays)
    npeer = N_DEV - 1

    def body(*refs):
        ins, outs = refs[:n], refs[n:2 * n]
        send_sems, recv_sems, local_sems = refs[2 * n:]
        x, y, c = lax.axis_index("x"), lax.axis_index("y"), lax.axis_index("c")
        me = 4 * x + 2 * y + c
        copies = []
        for a in range(n):
            own = ins[a].at[me] if a < ns else ins[a]
            cp = pltpu.make_async_copy(own, outs[a].at[me], local_sems.at[a])
            cp.start()
            copies.append(cp)
            for r in range(1, N_DEV):
                tx = 1 - x if (r >> 2) & 1 else x
                ty = 1 - y if (r >> 1) & 1 else y
                tc = 1 - c if r & 1 else c
                src = ins[a].at[4 * tx + 2 * ty + tc] if a < ns else ins[a]
                cp = pltpu.make_async_remote_copy(
                    src_ref=src, dst_ref=outs[a].at[me],
                    send_sem=send_sems.at[a * npeer + r - 1], recv_sem=recv_sems.at[a * npeer + r - 1],
                    device_id=(tx, ty, tc), device_id_type=pl.DeviceIdType.MESH)
                cp.start()
                copies.append(cp)
        for cp in copies:
            cp.wait()

    out_shape = [jax.ShapeDtypeStruct(v.shape if i < ns else (N_DEV,) + v.shape, v.dtype)
                 for i, v in enumerate(arrays)]
    return pl.pallas_call(
        body, in_specs=[pl.BlockSpec(memory_space=pl.ANY)] * n,
        out_specs=[pl.BlockSpec(memory_space=pl.ANY)] * n, out_shape=out_shape,
        scratch_shapes=[pltpu.SemaphoreType.DMA((n * npeer,)), pltpu.SemaphoreType.DMA((n * npeer,)),
                        pltpu.SemaphoreType.DMA((n,))],
        compiler_params=pltpu.CompilerParams(has_side_effects=True), name=name)(*arrays)


def _adamw(parts, w, m, v, *, name, tr=None):
    r, c = w.shape
    tr = r if tr is None else tr
    assert r % tr == 0

    def body(p_ref, w_ref, m_ref, v_ref, g_ref, d_ref, nm_ref, nv_ref):
        g = p_ref[0].astype(F32)
        for d in range(1, N_DEV):
            g = g + p_ref[d].astype(F32)
        mm = ADAM_B1 * m_ref[...] + (1.0 - ADAM_B1) * g
        vv = ADAM_B2 * v_ref[...] + (1.0 - ADAM_B2) * (g * g)
        m_hat = mm / (1.0 - ADAM_B1 ** ADAM_STEP)
        v_hat = vv / (1.0 - ADAM_B2 ** ADAM_STEP)
        g_ref[...] = g
        d_ref[...] = -ADAM_LR * (m_hat / (jnp.sqrt(v_hat) + ADAM_EPS) + ADAM_WD * w_ref[...])
        nm_ref[...] = mm
        nv_ref[...] = vv

    spec = pl.BlockSpec((tr, c), lambda i: (i, 0))
    out = jax.ShapeDtypeStruct((r, c), F32)
    return pl.pallas_call(
        body, grid=(r // tr,),
        in_specs=[pl.BlockSpec((N_DEV, tr, c), lambda i: (0, i, 0)), spec, spec, spec],
        out_specs=[spec] * 4, out_shape=[out] * 4,
        compiler_params=_params(("parallel",)), name=name)(parts, w, m, v)


SMALL = (("norm_w", D_MODEL), ("conv_b", CONV_DIM), ("dt_bias", N_DT), ("a_log", N_DT),
         ("d_skip", N_DT), ("ssm_norm_w", SSD_WIDTH), ("final_norm_w", D_MODEL))
SMALL_ROWS = 64
SHARD = D_PROJ // N_DEV


def _pack_small(vals):
    flat = jnp.concatenate([vals[k].reshape(-1).astype(F32) for k, _ in SMALL])
    return jnp.pad(flat, (0, SMALL_ROWS * LANES - flat.shape[0])).reshape(SMALL_ROWS, LANES)


def _unpack_small(packed, shapes):
    flat = packed.reshape(-1)
    out, off = {}, 0
    for k, size in SMALL:
        out[k] = flat[off:off + size].reshape(shapes[k])
        off += size
    return out


def kernel(x, norm_w, w_in, conv_w, conv_b, dt_bias, a_log, d_skip, ssm_norm_w, w_attn_out, w_ssm_out, w_o, final_norm_w, loss_target, m_norm_w, m_w_in, m_conv_w, m_conv_b, m_dt_bias, m_a_log, m_d_skip, m_ssm_norm_w, m_w_attn_out, m_w_ssm_out, m_w_o, m_final_norm_w, v_norm_w, v_w_in, v_conv_w, v_conv_b, v_dt_bias, v_a_log, v_d_skip, v_ssm_norm_w, v_w_attn_out, v_w_ssm_out, v_w_o, v_final_norm_w):
    nb, s, _ = x.shape
    t = nb * s
    weights = dict(norm_w=norm_w, w_in=w_in, conv_w=conv_w, conv_b=conv_b, dt_bias=dt_bias, a_log=a_log,
                   d_skip=d_skip, ssm_norm_w=ssm_norm_w, w_attn_out=w_attn_out, w_ssm_out=w_ssm_out,
                   w_o=w_o, final_norm_w=final_norm_w)
    moms = dict(norm_w=m_norm_w, w_in=m_w_in, conv_w=m_conv_w, conv_b=m_conv_b, dt_bias=m_dt_bias,
                a_log=m_a_log, d_skip=m_d_skip, ssm_norm_w=m_ssm_norm_w, w_attn_out=m_w_attn_out,
                w_ssm_out=m_w_ssm_out, w_o=m_w_o, final_norm_w=m_final_norm_w)
    vels = dict(norm_w=v_norm_w, w_in=v_w_in, conv_w=v_conv_w, conv_b=v_conv_b, dt_bias=v_dt_bias,
                a_log=v_a_log, d_skip=v_d_skip, ssm_norm_w=v_ssm_norm_w, w_attn_out=v_w_attn_out,
                w_ssm_out=v_w_ssm_out, w_o=v_w_o, final_norm_w=v_final_norm_w)

    g_in, g_conv, g_wa, g_ws, g_wo = _exchange(
        [], [_bf(w_in[0]), conv_w[0], _bf(w_attn_out[0]), _bf(w_ssm_out[0]), _bf(w_o[0])],
        name="gather_weights")
    w_full = g_in.transpose(1, 0, 2).reshape(D_MODEL, D_PROJ)
    w_main = jnp.concatenate([w_full[:, :DT_COL0], w_full[:, DT_COL0 + N_DT:]], axis=1)
    w_dt = jnp.pad(w_full[:, DT_COL0:DT_COL0 + N_DT], ((0, 0), (0, LANES - N_DT)))
    conv_full = g_conv.transpose(1, 0, 2).reshape(CONV_K, CONV_DIM)

    loss_part, grad_x, grads = _local_step(
        x.reshape(t, D_MODEL), loss_target.reshape(t, D_MODEL), nb, s, w_main, w_dt, conv_full, conv_b,
        norm_w, dt_bias, a_log, d_skip, ssm_norm_w, g_wa.reshape(SB_WIDTH, D_MODEL),
        g_ws.reshape(SSD_WIDTH, D_MODEL), g_wo.reshape(D_MODEL, D_MODEL), final_norm_w.reshape(1, D_MODEL))

    d_main, d_dt = grads["w_main"], grads["w_dt"]
    d_in = jnp.concatenate([d_main[:, :DT_COL0], d_dt[:, :N_DT], d_main[:, DT_COL0:]], axis=1)
    s_in = _bf(d_in.reshape(D_MODEL, N_DEV, SHARD).transpose(1, 0, 2))
    s_conv = grads["conv_w"].reshape(CONV_K, N_DEV, CONV_DIM // N_DEV).transpose(1, 0, 2)
    s_wa = _bf(grads["w_attn_out"].reshape(N_DEV, SB_WIDTH // N_DEV, D_MODEL))
    s_ws = _bf(grads["w_ssm_out"].reshape(N_DEV, SSD_WIDTH // N_DEV, D_MODEL))
    s_wo = _bf(grads["w_o"].reshape(N_DEV, D_MODEL // N_DEV, D_MODEL))
    small = _pack_small(grads)
    p_in, p_conv, p_wa, p_ws, p_wo, p_small = _exchange(
        [s_in, s_conv, s_wa, s_ws, s_wo], [small], name="exchange_grads")

    res = {}
    res["w_in"] = _adamw(p_in, w_in[0], m_w_in[0], v_w_in[0], name="adamw_w_in", tr=128)
    res["conv_w"] = _adamw(p_conv, conv_w[0], m_conv_w[0], v_conv_w[0], name="adamw_conv_w")
    res["w_attn_out"] = _adamw(p_wa, w_attn_out[0], m_w_attn_out[0], v_w_attn_out[0], name="adamw_w_attn_out")
    res["w_ssm_out"] = _adamw(p_ws, w_ssm_out[0], m_w_ssm_out[0], v_w_ssm_out[0], name="adamw_w_ssm_out")
    res["w_o"] = _adamw(p_wo, w_o[0], m_w_o[0], v_w_o[0], name="adamw_w_o")
    sm = _adamw(p_small, _pack_small(weights), _pack_small(moms), _pack_small(vels), name="adamw_small")
    shapes = {k: weights[k].shape for k, _ in SMALL}
    sm = [_unpack_small(v, shapes) for v in sm]
    for k, _ in SMALL:
        res[k] = tuple(part[k] for part in sm)

    order = ("norm_w", "w_in", "conv_w", "conv_b", "dt_bias", "a_log", "d_skip", "ssm_norm_w",
             "w_attn_out", "w_ssm_out", "w_o", "final_norm_w")
    loss = lax.psum(loss_part[0, 0], ("x", "y", "c"))
    outs = [loss, grad_x.reshape(nb, s, D_MODEL)]
    for i in range(4):
        outs += [res[k][i].reshape(weights[k].shape) for k in order]
    return tuple(outs)
```

```python
import functools

import jax
import jax.numpy as jnp
from jax import lax
from jax.experimental import pallas as pl
from jax.experimental.pallas import tpu as pltpu

F32 = jnp.float32
BF16 = jnp.bfloat16

D_MODEL = 1024
SB_WIDTH = 1024
HEAD_DIM = 64
SSD_WIDTH = 2048
SSD_GROUPS = 4
SSD_HPG = 8
SSD_STATE = 128
SSD_CHUNK = 128
CONV_K = 4
CONV_DIM = 3072
N_DT = 32
D_PROJ = 11296
DT_COL0 = 9216
GATE_COL0 = DT_COL0 + N_DT
EPS = 1e-6
N_DEV = 8

ADAM_LR = 0.001
ADAM_B1 = 0.9
ADAM_B2 = 0.999
ADAM_EPS = 1e-08
ADAM_WD = 0.01
ADAM_STEP = 10

LANES = 128
VMEM_LIMIT = 56 * 1024 * 1024

CB_Q, CB_K, CB_V, CB_ZA, CB_ZS, CB_XBC = 0, 8, 16, 24, 32, 48

NT_DIMS = (((1,), (1,)), ((), ()))
TN_DIMS = (((0,), (0,)), ((), ()))
HI = lax.Precision.HIGHEST


def _params(sem):
    return pltpu.CompilerParams(dimension_semantics=sem, vmem_limit_bytes=VMEM_LIMIT)


def _dot(a, b):
    return jnp.dot(a, b, preferred_element_type=F32)


def _dot_nt(a, b):
    return lax.dot_general(a, b, NT_DIMS, preferred_element_type=F32)


def _dot_tn(a, b):
    return lax.dot_general(a, b, TN_DIMS, preferred_element_type=F32)


def _dot_hi(a, b):
    return jnp.dot(a, b, preferred_element_type=F32, precision=HI)


def _bf(a):
    return a.astype(BF16)


def _sigmoid(x):
    return 1.0 / (1.0 + jnp.exp(-x))


def _softplus(x):
    return jnp.maximum(x, 0.0) + jnp.log(1.0 + jnp.exp(-jnp.abs(x)))


def _trunc_split(a):
    bits = lax.bitcast_convert_type(a, jnp.uint32) & jnp.uint32(0xFFFF0000)
    hi = lax.bitcast_convert_type(bits, F32)
    return _bf(hi), _bf(a - hi)


def _matmul(a, b, *, name, nt=False, add=None, out_dtype=F32, tm=512, tn=512, tk=1024, n=None, out_t=False):
    m, k = a.shape
    n = (b.shape[0] if nt else b.shape[1]) if n is None else n
    tm, tn, tk = min(tm, m), min(tn, n), min(tk, k)
    assert m % tm == 0 and n % tn == 0 and k % tk == 0, (name, a.shape, b.shape)
    assert not (out_t and add is not None)
    nk = k // tk
    has_add = add is not None

    def body(*refs):
        if has_add:
            a_ref, b_ref, add_ref, o_ref, acc_ref = refs
        else:
            a_ref, b_ref, o_ref, acc_ref = refs
        kk = pl.program_id(2)

        @pl.when(kk == 0)
        def _():
            acc_ref[...] = jnp.zeros_like(acc_ref)

        av, bv = _bf(a_ref[...]), _bf(b_ref[...])
        acc_ref[...] += _dot_nt(av, bv) if nt else _dot(av, bv)

        @pl.when(kk == nk - 1)
        def _():
            r = acc_ref[...]
            if has_add:
                r = r + add_ref[...]
            o_ref[...] = (r.T if out_t else r).astype(out_dtype)

    in_specs = [pl.BlockSpec((tm, tk), lambda j, i, kk: (i, kk)),
                pl.BlockSpec((tn, tk), lambda j, i, kk: (j, kk)) if nt
                else pl.BlockSpec((tk, tn), lambda j, i, kk: (kk, j))]
    args = [a, b]
    if has_add:
        in_specs.append(pl.BlockSpec((tm, tn), lambda j, i, kk: (i, j)))
        args.append(add)
    if out_t:
        out_spec, out_dims = pl.BlockSpec((tn, tm), lambda j, i, kk: (j, i)), (n, m)
    else:
        out_spec, out_dims = pl.BlockSpec((tm, tn), lambda j, i, kk: (i, j)), (m, n)
    return pl.pallas_call(
        body, grid=(n // tn, m // tm, nk), in_specs=in_specs, out_specs=out_spec,
        out_shape=jax.ShapeDtypeStruct(out_dims, out_dtype),
        scratch_shapes=[pltpu.VMEM((tm, tn), F32)],
        compiler_params=_params(("parallel", "parallel", "arbitrary")), name=name)(*args)


def _rms_fwd(x, w, *, tm=512):
    t = x.shape[0]

    def body(x_ref, w_ref, h_ref, r_ref):
        xv = x_ref[...]
        r = lax.rsqrt(jnp.mean(xv * xv, axis=-1, keepdims=True) + EPS)
        h_ref[...] = _bf(xv * r * w_ref[...])
        r_ref[...] = r

    row = lambda i: (i, 0)
    return pl.pallas_call(
        body, grid=(t // tm,),
        in_specs=[pl.BlockSpec((tm, D_MODEL), row), pl.BlockSpec((1, D_MODEL), lambda i: (0, 0))],
        out_specs=[pl.BlockSpec((tm, D_MODEL), row), pl.BlockSpec((tm, 1), row)],
        out_shape=[jax.ShapeDtypeStruct((t, D_MODEL), BF16), jax.ShapeDtypeStruct((t, 1), F32)],
        compiler_params=_params(("parallel",)), name="rms_fwd")(x, w)


def _rms_bwd(dh, x, r, w, dres, *, tm=512):
    t = x.shape[0]

    def body(dh_ref, x_ref, r_ref, w_ref, dres_ref, dx_ref, dw_ref):
        @pl.when(pl.program_id(0) == 0)
        def _():
            dw_ref[...] = jnp.zeros_like(dw_ref)

        nrm = x_ref[...] * r_ref[...]
        dhv = dh_ref[...]
        dw_ref[...] += jnp.sum(dhv * nrm, axis=0, keepdims=True)
        dn = dhv * w_ref[...]
        dx = r_ref[...] * (dn - nrm * jnp.mean(dn * nrm, axis=-1, keepdims=True))
        dx_ref[...] = dx + dres_ref[...]

    row = lambda i: (i, 0)
    fix = lambda i: (0, 0)
    return pl.pallas_call(
        body, grid=(t // tm,),
        in_specs=[pl.BlockSpec((tm, D_MODEL), row), pl.BlockSpec((tm, D_MODEL), row),
                  pl.BlockSpec((tm, 1), row), pl.BlockSpec((1, D_MODEL), fix),
                  pl.BlockSpec((tm, D_MODEL), row)],
        out_specs=[pl.BlockSpec((tm, D_MODEL), row), pl.BlockSpec((1, D_MODEL), fix)],
        out_shape=[jax.ShapeDtypeStruct((t, D_MODEL), F32), jax.ShapeDtypeStruct((1, D_MODEL), F32)],
        compiler_params=_params(("arbitrary",)), name="rms_bwd")(dh, x, r, w, dres)


ATT_BLK = 256


def _attn_masks(blk):
    r_io = lax.broadcasted_iota(jnp.int32, (blk, blk), 0)
    c_io = lax.broadcasted_iota(jnp.int32, (blk, blk), 1)
    return r_io, c_io


def _attn_fwd(proj, nb, s):
    blk = min(ATT_BLK, s)
    nq = s // blk
    scale = HEAD_DIM ** -0.5

    def body(q_ref, k_ref, v_ref, o_ref, tot_ref, q0_s, q1_s, kb_s, v0_s, v1_s, z_s, a_s):
        head0 = lax.broadcasted_iota(jnp.int32, (s, LANES), 1) < HEAD_DIM
        lane2 = lax.broadcasted_iota(jnp.int32, (blk, 2), 1)
        r_io, c_io = _attn_masks(blk)
        tri = c_io < r_io
        u_gt = _bf(r_io > c_io)
        u2 = jnp.concatenate([u_gt, u_gt], axis=0)
        qv = q_ref[...] * scale
        q0_s[...] = _bf(jnp.where(head0, qv, 0.0))
        q1_s[...] = _bf(jnp.where(head0, 0.0, qv))
        kb_s[...] = _bf(k_ref[...])
        vv = v_ref[...]
        v0_s[...] = _bf(jnp.where(head0, vv, 0.0))
        v1_s[...] = _bf(jnp.where(head0, 0.0, vv))

        def scores(qhs, kblk, slot):
            k0 = pl.multiple_of(jnp.maximum(kblk, 0) * blk, blk)
            kb = kb_s[pl.ds(k0, blk), :]
            for hh in range(2):
                z_s[slot, hh] = _dot_nt(qhs[hh], kb)

        def weights(slot, crs, diag):
            new = []
            for hh in range(2):
                z = z_s[slot, hh]
                sp = _softplus(z)
                spm = jnp.where(tri, sp, 0.0) if diag else sp
                hi, lo = _trunc_split(spm)
                ps = _dot(jnp.concatenate([hi, lo], axis=1), u2)
                a = jnp.exp(z - (sp + ps + crs[hh]))
                if diag:
                    a = jnp.where(tri, a, 0.0)
                a_s[slot, :, hh * blk:(hh + 1) * blk] = _bf(a)
                new.append(crs[hh] + ps[:, 0:1] + spm[:, 0:1])
            return new

        def apply(acc, slot, k0):
            vcat = jnp.concatenate([v0_s[pl.ds(k0, blk), :], v1_s[pl.ds(k0, blk), :]], axis=0)
            return acc + _dot(a_s[slot], vcat)

        def qblock(qi, _):
            q0 = pl.multiple_of(qi * blk, blk)
            qhs = (q0_s[pl.ds(q0, blk), :], q1_s[pl.ds(q0, blk), :])
            zc = jnp.zeros((blk, 1), F32)
            scores(qhs, qi, 0)
            scores(qhs, qi - 1, 1)
            crs = weights(0, (zc, zc), True)

            def step(j, slot, st):
                acc, c0, c1 = st
                acc = apply(acc, 1 - slot, pl.multiple_of((qi - j + 1) * blk, blk))
                scores(qhs, qi - j - 1, 1 - slot)
                crs = weights(slot, (c0, c1), False)
                return acc, crs[0], crs[1]

            st = (jnp.zeros((blk, LANES), F32), crs[0], crs[1])
            st = lax.fori_loop(0, qi // 2, lambda i, st: step(2 * i + 2, 0, step(2 * i + 1, 1, st)), st)
            st = lax.cond(qi % 2 == 1, lambda st: step(qi, 1, st), lambda st: st, st)
            o_ref[pl.ds(q0, blk), :] = apply(st[0], qi & 1, 0)
            tot_ref[pl.ds(q0, blk), :] = jnp.where(lane2 == 0, st[1], st[2])
            return 0

        lax.fori_loop(0, nq, qblock, 0)

    t = nb * s
    return pl.pallas_call(
        body, grid=(nb, 8),
        in_specs=[pl.BlockSpec((s, LANES), lambda b, p: (b, CB_Q + p)),
                  pl.BlockSpec((s, LANES), lambda b, p: (b, CB_K + p)),
                  pl.BlockSpec((s, LANES), lambda b, p: (b, CB_V + p))],
        out_specs=[pl.BlockSpec((s, LANES), lambda b, p: (b, p)),
                   pl.BlockSpec((None, None, s, 2), lambda b, p: (b, p, 0, 0))],
        out_shape=[jax.ShapeDtypeStruct((t, SB_WIDTH), F32),
                   jax.ShapeDtypeStruct((nb, 8, s, 2), F32)],
        scratch_shapes=[pltpu.VMEM((s, LANES), BF16)] * 5 + [
            pltpu.VMEM((2, 2, blk, blk), F32), pltpu.VMEM((2, blk, 2 * blk), BF16)],
        compiler_params=_params(("parallel", "parallel")), name="attn_fwd")(proj, proj, proj)


def _attn_bwd(proj, do, tot, nb, s):
    blk = min(ATT_BLK, s)
    nq = s // blk
    scale = HEAD_DIM ** -0.5

    def body(q_ref, k_ref, v_ref, do_ref, tot_ref, dq_ref, dk_ref, dv_ref,
             q0_s, q1_s, kb_s, k0_s, k1_s, vb_s, d0_s, d1_s, z_s, da_s, dz_s, a_s, dk_acc, dv_acc):
        head0 = lax.broadcasted_iota(jnp.int32, (s, LANES), 1) < HEAD_DIM
        r_io, c_io = _attn_masks(blk)
        tri = c_io < r_io
        u_le = _bf(r_io <= c_io)
        u2 = jnp.concatenate([u_le, u_le], axis=0)
        u_lt = _bf(r_io < c_io)
        qv = q_ref[...] * scale
        q0_s[...] = _bf(jnp.where(head0, qv, 0.0))
        q1_s[...] = _bf(jnp.where(head0, 0.0, qv))
        kv = k_ref[...]
        kb_s[...] = _bf(kv)
        k0_s[...] = _bf(jnp.where(head0, kv, 0.0))
        k1_s[...] = _bf(jnp.where(head0, 0.0, kv))
        vb_s[...] = _bf(v_ref[...])
        dov = do_ref[...]
        d0_s[...] = _bf(jnp.where(head0, dov, 0.0))
        d1_s[...] = _bf(jnp.where(head0, 0.0, dov))
        dk_acc[...] = jnp.zeros_like(dk_acc)
        dv_acc[...] = jnp.zeros_like(dv_acc)

        def qblock(qi, _):
            q0 = pl.multiple_of(qi * blk, blk)
            qs_ = pl.ds(q0, blk)
            qhs = (q0_s[qs_, :], q1_s[qs_, :])
            dhs = (d0_s[qs_, :], d1_s[qs_, :])
            qcat = jnp.concatenate(qhs, axis=0)
            dcat = jnp.concatenate(dhs, axis=0)
            totf = tot_ref[qs_, :]

            def scores(kblk, slot):
                ks_ = pl.ds(pl.multiple_of(kblk * blk, blk), blk)
                kb, vb = kb_s[ks_, :], vb_s[ks_, :]
                for hh in range(2):
                    z_s[slot, hh] = _dot_nt(qhs[hh], kb)
                    da_s[slot, hh] = _dot_nt(dhs[hh], vb)

            def grads(slot, st, diag):
                new = []
                for hh in range(2):
                    cn, cw = st[2 * hh], st[2 * hh + 1]
                    z = z_s[slot, hh]
                    sp = _softplus(z)
                    lb = z - sp
                    spm = jnp.where(tri, sp, 0.0) if diag else sp
                    hi, lo = _trunc_split(spm)
                    pin = _dot(jnp.concatenate([hi, lo], axis=1), u2)
                    a = jnp.exp(lb + (pin + cn))
                    if diag:
                        a = jnp.where(tri, a, 0.0)
                    w = a * da_s[slot, hh]
                    wex = _dot(_bf(w), u_lt)
                    dz = w - jnp.exp(lb) * (w + (wex + cw))
                    if diag:
                        dz = jnp.where(tri, dz, 0.0)
                    dz_s[slot, hh] = _bf(dz)
                    a_s[slot, hh] = _bf(a)
                    new += [cn + pin[:, blk - 1:blk], cw + wex[:, blk - 1:blk] + w[:, blk - 1:blk]]
                return tuple(new)

            def apply(dq, slot, kblk):
                ks_ = pl.ds(pl.multiple_of(kblk * blk, blk), blk)
                dzs = (dz_s[slot, 0], dz_s[slot, 1])
                dq = dq + _dot(jnp.concatenate(dzs, axis=1), jnp.concatenate([k0_s[ks_, :], k1_s[ks_, :]], axis=0))
                dk_acc[ks_, :] += _dot_tn(jnp.concatenate(dzs, axis=0), qcat)
                dv_acc[ks_, :] += _dot_tn(jnp.concatenate([a_s[slot, 0], a_s[slot, 1]], axis=0), dcat)
                return dq

            def step(j, slot, st, diag):
                dq = apply(st[0], 1 - slot, jnp.maximum(j - 1, 0))
                if not diag:
                    scores(j + 1, 1 - slot)
                return (dq, *grads(slot, st[1:], diag))

            dz_s[1] = jnp.zeros_like(dz_s[1])
            a_s[1] = jnp.zeros_like(a_s[1])
            scores(0, 0)
            zc = jnp.zeros((blk, 1), F32)
            st = (jnp.zeros((blk, LANES), F32), -totf[:, 0:1], zc, -totf[:, 1:2], zc)
            st = lax.fori_loop(
                0, qi // 2, lambda i, st: step(2 * i + 1, 1, step(2 * i, 0, st, False), False), st)
            st = lax.cond(qi % 2 == 1,
                          lambda st: step(qi, 1, step(qi - 1, 0, st, False), True),
                          lambda st: step(qi, 0, st, True), st)
            dq_ref[qs_, :] = _bf(apply(st[0], qi & 1, qi) * scale)
            return 0

        lax.fori_loop(0, nq, qblock, 0)
        dk_ref[...] = _bf(dk_acc[...])
        dv_ref[...] = _bf(dv_acc[...])

    t = nb * s
    hp = lambda b, p: (b, p)
    out = jax.ShapeDtypeStruct((t, SB_WIDTH), BF16)
    return pl.pallas_call(
        body, grid=(nb, 8),
        in_specs=[pl.BlockSpec((s, LANES), lambda b, p: (b, CB_Q + p)),
                  pl.BlockSpec((s, LANES), lambda b, p: (b, CB_K + p)),
                  pl.BlockSpec((s, LANES), lambda b, p: (b, CB_V + p)),
                  pl.BlockSpec((s, LANES), hp),
                  pl.BlockSpec((None, None, s, 2), lambda b, p: (b, p, 0, 0))],
        out_specs=[pl.BlockSpec((s, LANES), hp)] * 3,
        out_shape=[out, out, out],
        scratch_shapes=[pltpu.VMEM((s, LANES), BF16)] * 8 + [
            pltpu.VMEM((2, 2, blk, blk), F32), pltpu.VMEM((2, 2, blk, blk), F32),
            pltpu.VMEM((2, 2, blk, blk), BF16), pltpu.VMEM((2, 2, blk, blk), BF16),
            pltpu.VMEM((s, LANES), F32), pltpu.VMEM((s, LANES), F32)],
        compiler_params=_params(("parallel", "parallel")), name="attn_bwd")(proj, proj, proj, do, tot)


CONV_TC = 256


def _shift_down(x, k, row):
    return jnp.where(row >= k, pltpu.roll(x, k, 0), 0.0)


def _shift_up(x, k, row, s):
    return jnp.where(row < s - k, pltpu.roll(x, s - k, 0), 0.0)


def _conv_fwd(proj, conv_w, conv_b, nb, s):
    tc = CONV_TC
    cb0 = CB_XBC * LANES // tc

    def body(x_ref, w_ref, b_ref, o_ref):
        x = x_ref[...]
        row = lax.broadcasted_iota(jnp.int32, x.shape, 0)
        y = b_ref[...] + w_ref[CONV_K - 1:CONV_K, :] * x
        for kk in range(1, CONV_K):
            y = y + w_ref[CONV_K - 1 - kk:CONV_K - kk, :] * _shift_down(x, kk, row)
        o_ref[...] = y * _sigmoid(y)

    t = nb * s
    return pl.pallas_call(
        body, grid=(nb, CONV_DIM // tc),
        in_specs=[pl.BlockSpec((s, tc), lambda b, c: (b, cb0 + c)),
                  pl.BlockSpec((CONV_K, tc), lambda b, c: (0, c)),
                  pl.BlockSpec((1, tc), lambda b, c: (0, c))],
        out_specs=pl.BlockSpec((s, tc), lambda b, c: (b, c)),
        out_shape=jax.ShapeDtypeStruct((t, CONV_DIM), F32),
        compiler_params=_params(("parallel", "parallel")), name="conv_fwd")(proj, conv_w, conv_b)


def _conv_bwd(proj, dact, conv_w, conv_b, nb, s, *, col0, name):
    width = dact.shape[1]
    tc = min(CONV_TC, width)
    cb0 = (CB_XBC * LANES + col0) // tc
    wb0 = col0 // tc

    def body(x_ref, d_ref, w_ref, b_ref, dx_ref, dw_ref, db_ref):
        @pl.when(pl.program_id(1) == 0)
        def _():
            dw_ref[...] = jnp.zeros_like(dw_ref)
            db_ref[...] = jnp.zeros_like(db_ref)

        x = x_ref[...]
        row = lax.broadcasted_iota(jnp.int32, x.shape, 0)
        xs = [x] + [_shift_down(x, kk, row) for kk in range(1, CONV_K)]
        y = b_ref[...] + w_ref[CONV_K - 1:CONV_K, :] * x
        for kk in range(1, CONV_K):
            y = y + w_ref[CONV_K - 1 - kk:CONV_K - kk, :] * xs[kk]
        sg = _sigmoid(y)
        dy = d_ref[...] * (sg * (1.0 + y * (1.0 - sg)))
        dx = w_ref[CONV_K - 1:CONV_K, :] * dy
        for kk in range(1, CONV_K):
            dx = dx + w_ref[CONV_K - 1 - kk:CONV_K - kk, :] * _shift_up(dy, kk, row, s)
        dx_ref[...] = _bf(dx)
        db_ref[...] += jnp.sum(dy, axis=0, keepdims=True)
        for kk in range(CONV_K):
            dw_ref[CONV_K - 1 - kk:CONV_K - kk, :] += jnp.sum(dy * xs[kk], axis=0, keepdims=True)

    t = nb * s
    return pl.pallas_call(
        body, grid=(width // tc, nb),
        in_specs=[pl.BlockSpec((s, tc), lambda c, b: (b, cb0 + c)),
                  pl.BlockSpec((s, tc), lambda c, b: (b, c)),
                  pl.BlockSpec((CONV_K, tc), lambda c, b: (0, wb0 + c)),
                  pl.BlockSpec((1, tc), lambda c, b: (0, wb0 + c))],
        out_specs=[pl.BlockSpec((s, tc), lambda c, b: (b, c)),
                   pl.BlockSpec((CONV_K, tc), lambda c, b: (0, c)),
                   pl.BlockSpec((1, tc), lambda c, b: (0, c))],
        out_shape=[jax.ShapeDtypeStruct((t, width), BF16),
                   jax.ShapeDtypeStruct((CONV_K, width), F32),
                   jax.ShapeDtypeStruct((1, width), F32)],
        compiler_params=_params(("parallel", "arbitrary")), name=name)(proj, dact, conv_w, conv_b)


L = SSD_CHUNK
N_PAIR = SSD_HPG // 2


def _pair_lanes(v0, v1, lane):
    return jnp.where(lane < HEAD_DIM, v0, v1)


def _ssd_chunk_common(dtr_ref, bias_ref, alog_ref, r0):
    rr = lax.broadcasted_iota(jnp.int32, (L, L), 0)
    cc = lax.broadcasted_iota(jnp.int32, (L, L), 1)
    tri = (cc <= rr).astype(F32)
    raw = dtr_ref[pl.ds(r0, L), :] + bias_ref[...]
    dt = _softplus(raw)
    a_neg = -jnp.exp(alog_ref[...])
    acs = _dot_hi(tri, dt * a_neg)
    return raw, dt, a_neg, acs, acs.T, rr, cc


def _ssd_fwd(xbc, dtr, bias, alog, dskip, nb, s):
    nc = s // L
    t = nb * s

    def body(x_ref, b_ref, c_ref, dtr_ref, bias_ref, alog_ref, dsk_ref, y_ref, hs_ref, h_scr):
        lane = lax.broadcasted_iota(jnp.int32, (L, LANES), 1)
        lane1 = lax.broadcasted_iota(jnp.int32, (1, LANES), 1)
        rowc = lax.broadcasted_iota(jnp.int32, (L, 1), 0)
        h_scr[...] = jnp.zeros_like(h_scr)
        dsk = dsk_ref[...]

        def chunk(c, _):
            r0 = pl.multiple_of(c * L, L)
            raw, dt, a_neg, acs, acs_t, rr, cc = _ssd_chunk_common(dtr_ref, bias_ref, alog_ref, r0)
            causal = rr >= cc
            bb = _bf(b_ref[pl.ds(r0, L), :])
            cb = _bf(c_ref[pl.ds(r0, L), :])
            gm = _dot_nt(cb, bb)
            for pr in range(N_PAIR):
                j0, j1 = 2 * pr, 2 * pr + 1
                ms = []
                for j in (j0, j1):
                    seg = acs[:, j:j + 1] - acs_t[j:j + 1, :]
                    ms.append(_bf(gm * jnp.exp(jnp.where(causal, seg, -1e30))))
                x = x_ref[pl.ds(r0, L), pr * LANES:(pr + 1) * LANES]
                dt_p = _pair_lanes(dt[:, j0:j0 + 1], dt[:, j1:j1 + 1], lane)
                acs_p = _pair_lanes(acs[:, j0:j0 + 1], acs[:, j1:j1 + 1], lane)
                last_p = acs_p[L - 1:L, :]
                d_p = _pair_lanes(dsk[:, j0:j0 + 1], dsk[:, j1:j1 + 1], lane1)
                xd = x * dt_p
                xdb = _bf(xd)
                yd = _pair_lanes(_dot(ms[0], xdb), _dot(ms[1], xdb), lane)
                hp = h_scr[pr]
                yo = _dot_nt(cb, _bf(hp)) * jnp.exp(acs_p)
                y_ref[pl.ds(r0, L), pr * LANES:(pr + 1) * LANES] = yd + yo + x * d_p
                hs_ref[c, pr] = hp
                sp = _dot_tn(_bf(xd * jnp.exp(last_p - acs_p)), bb)
                cd = jnp.where(rowc < HEAD_DIM, jnp.exp(acs[L - 1:L, j0:j0 + 1]),
                               jnp.exp(acs[L - 1:L, j1:j1 + 1]))
                h_scr[pr] = hp * cd + sp
            return 0

        lax.fori_loop(0, nc, chunk, 0)

    grp = lambda b, g: (g, 0, 0)
    return pl.pallas_call(
        body, grid=(nb, SSD_GROUPS),
        in_specs=[pl.BlockSpec((s, 4 * LANES), lambda b, g: (b, g)),
                  pl.BlockSpec((s, LANES), lambda b, g: (b, 16 + g)),
                  pl.BlockSpec((s, LANES), lambda b, g: (b, 20 + g)),
                  pl.BlockSpec((None, s, LANES), lambda b, g: (g, b, 0)),
                  pl.BlockSpec((None, 1, LANES), grp), pl.BlockSpec((None, 1, LANES), grp),
                  pl.BlockSpec((None, 1, LANES), grp)],
        out_specs=[pl.BlockSpec((s, 4 * LANES), lambda b, g: (b, g)),
                   pl.BlockSpec((None, None, nc, N_PAIR, L, LANES), lambda b, g: (b, g, 0, 0, 0, 0))],
        out_shape=[jax.ShapeDtypeStruct((t, SSD_WIDTH), F32),
                   jax.ShapeDtypeStruct((nb, SSD_GROUPS, nc, N_PAIR, L, LANES), F32)],
        scratch_shapes=[pltpu.VMEM((N_PAIR, L, LANES), F32)],
        compiler_params=_params(("parallel", "parallel")), name="ssd_fwd")(
            xbc, xbc, xbc, dtr, bias, alog, dskip)


def _ssd_bwd(xbc, dtr, bias, alog, dskip, hs, dy, nb, s):
    nc = s // L
    t = nb * s

    def body(x_ref, b_ref, c_ref, dtr_ref, bias_ref, alog_ref, dsk_ref, hs_ref, dy_ref,
             dx_ref, db_ref, dc_ref, ddtr_ref, dbias_ref, dalog_ref, ddsk_ref, dh_scr):
        lane = lax.broadcasted_iota(jnp.int32, (L, LANES), 1)
        lane1 = lax.broadcasted_iota(jnp.int32, (1, LANES), 1)
        rowc = lax.broadcasted_iota(jnp.int32, (L, 1), 0)
        dh_scr[...] = jnp.zeros_like(dh_scr)
        dsk = dsk_ref[...]

        @pl.when(pl.program_id(1) == 0)
        def _():
            dbias_ref[...] = jnp.zeros_like(dbias_ref)
            dalog_ref[...] = jnp.zeros_like(dalog_ref)
            ddsk_ref[...] = jnp.zeros_like(ddsk_ref)

        def seg_sums(v):
            s0 = jnp.sum(jnp.where(lane < HEAD_DIM, v, 0.0), axis=1, keepdims=True)
            s1 = jnp.sum(jnp.where(lane >= HEAD_DIM, v, 0.0), axis=1, keepdims=True)
            return s0, s1

        def chunk(i, _):
            c = nc - 1 - i
            r0 = pl.multiple_of(c * L, L)
            raw, dt, a_neg, acs, acs_t, rr, cc = _ssd_chunk_common(dtr_ref, bias_ref, alog_ref, r0)
            causal = rr >= cc
            tri_t = (cc >= rr).astype(F32)
            bb = _bf(b_ref[pl.ds(r0, L), :])
            cb = _bf(c_ref[pl.ds(r0, L), :])
            gm = _dot_nt(cb, bb)
            dg = jnp.zeros((L, L), F32)
            dbacc = jnp.zeros((L, LANES), F32)
            dcacc = jnp.zeros((L, LANES), F32)
            dacs = jnp.zeros((L, LANES), F32)
            dacs_t = jnp.zeros((L, L), F32)
            ddt = jnp.zeros((L, LANES), F32)
            last_add = jnp.zeros((1, LANES), F32)
            dd_add = jnp.zeros((1, LANES), F32)
            for pr in range(N_PAIR):
                js = (2 * pr, 2 * pr + 1)
                lms, mfs = [], []
                for j in js:
                    seg = acs[:, j:j + 1] - acs_t[j:j + 1, :]
                    lm = jnp.exp(jnp.where(causal, seg, -1e30))
                    lms.append(lm)
                    mfs.append(gm * lm)
                x = x_ref[pl.ds(r0, L), pr * LANES:(pr + 1) * LANES]
                dyv = dy_ref[pl.ds(r0, L), pr * LANES:(pr + 1) * LANES]
                dt_p = _pair_lanes(dt[:, js[0]:js[0] + 1], dt[:, js[1]:js[1] + 1], lane)
                acs_p = _pair_lanes(acs[:, js[0]:js[0] + 1], acs[:, js[1]:js[1] + 1], lane)
                last_p = acs_p[L - 1:L, :]
                d_p = _pair_lanes(dsk[:, js[0]:js[0] + 1], dsk[:, js[1]:js[1] + 1], lane1)
                e_p = jnp.exp(acs_p)
                dte_p = jnp.exp(last_p - acs_p)
                xd = x * dt_p
                xdb = _bf(xd)
                hp = hs_ref[c, pr]
                hb = _bf(hp)
                dye = _bf(dyv * e_p)
                yo = _dot_nt(cb, hb) * e_p
                dcacc = dcacc + _dot(dye, hb)
                dh_out = _dot_tn(dye, cb)
                dyb = _bf(dyv)
                dxd = jnp.zeros((L, LANES), F32)
                for hh in range(2):
                    hm = (lane >= HEAD_DIM) if hh else (lane < HEAD_DIM)
                    dm = _dot_nt(_bf(jnp.where(hm, dyv, 0.0)), xdb)
                    dxd = dxd + jnp.where(hm, _dot_tn(_bf(mfs[hh]), dyb), 0.0)
                    dg = dg + dm * lms[hh]
                    qm = dm * mfs[hh]
                    j = js[hh]
                    dacs = dacs + jnp.where(lane == j, jnp.sum(qm, axis=1, keepdims=True), 0.0)
                    dacs_t = dacs_t - jnp.where(rr == j, jnp.sum(qm, axis=0, keepdims=True), 0.0)
                ds = dh_scr[pr]
                dsb = _bf(ds)
                xdd = xd * dte_p
                dxdd = _dot_nt(bb, dsb)
                dbacc = dbacc + _dot(_bf(xdd), dsb)
                dxd = dxd + dxdd * dte_p
                cds = [jnp.exp(acs[L - 1:L, j:j + 1]) for j in js]
                cd = jnp.where(rowc < HEAD_DIM, cds[0], cds[1])
                prod = jnp.sum(ds * hp, axis=1, keepdims=True)
                dcds = [jnp.sum(jnp.where(rowc < HEAD_DIM, prod, 0.0), axis=0, keepdims=True),
                        jnp.sum(jnp.where(rowc >= HEAD_DIM, prod, 0.0), axis=0, keepdims=True)]
                dh_scr[pr] = dh_out + ds * cd
                ta = seg_sums(dyv * yo - dxdd * xdd)
                tc_ = seg_sums(dxdd * xdd)
                te = seg_sums(dxd * x)
                td = seg_sums(dyv * x)
                for hh in range(2):
                    j = js[hh]
                    dacs = dacs + jnp.where(lane == j, ta[hh], 0.0)
                    ddt = ddt + jnp.where(lane == j, te[hh], 0.0)
                    last_add = last_add + jnp.where(
                        lane1 == j, jnp.sum(tc_[hh], axis=0, keepdims=True) + dcds[hh] * cds[hh], 0.0)
                    dd_add = dd_add + jnp.where(lane1 == j, jnp.sum(td[hh], axis=0, keepdims=True), 0.0)
                dx_ref[pl.ds(r0, L), pr * LANES:(pr + 1) * LANES] = dxd * dt_p + dyv * d_p
            dgb = _bf(dg)
            dc_ref[pl.ds(r0, L), :] = dcacc + _dot(dgb, bb)
            db_ref[pl.ds(r0, L), :] = dbacc + _dot_tn(dgb, cb)
            dacs = dacs + dacs_t.T + jnp.where(rowc == L - 1, last_add, 0.0)
            ddta = _dot_hi(tri_t, dacs)
            ddt = ddt + ddta * a_neg
            ddraw = ddt * _sigmoid(raw)
            ddtr_ref[pl.ds(r0, L), :] = ddraw
            dbias_ref[...] += jnp.sum(ddraw, axis=0, keepdims=True)
            dalog_ref[...] += jnp.sum(ddta * dt, axis=0, keepdims=True) * a_neg
            ddsk_ref[...] += dd_add
            return 0

        lax.fori_loop(0, nc, chunk, 0)

    grp = lambda g, b: (g, 0, 0)
    small = jax.ShapeDtypeStruct((SSD_GROUPS, 1, LANES), F32)
    return pl.pallas_call(
        body, grid=(SSD_GROUPS, nb),
        in_specs=[pl.BlockSpec((s, 4 * LANES), lambda g, b: (b, g)),
                  pl.BlockSpec((s, LANES), lambda g, b: (b, 16 + g)),
                  pl.BlockSpec((s, LANES), lambda g, b: (b, 20 + g)),
                  pl.BlockSpec((None, s, LANES), lambda g, b: (g, b, 0)),
                  pl.BlockSpec((None, 1, LANES), grp), pl.BlockSpec((None, 1, LANES), grp),
                  pl.BlockSpec((None, 1, LANES), grp),
                  pl.BlockSpec((None, None, nc, N_PAIR, L, LANES), lambda g, b: (b, g, 0, 0, 0, 0)),
                  pl.BlockSpec((s, 4 * LANES), lambda g, b: (b, g))],
        out_specs=[pl.BlockSpec((s, 4 * LANES), lambda g, b: (b, g)),
                   pl.BlockSpec((s, LANES), lambda g, b: (b, g)),
                   pl.BlockSpec((s, LANES), lambda g, b: (b, g)),
                   pl.BlockSpec((None, s, LANES), lambda g, b: (g, b, 0)),
                   pl.BlockSpec((None, 1, LANES), grp), pl.BlockSpec((None, 1, LANES), grp),
                   pl.BlockSpec((None, 1, LANES), grp)],
        out_shape=[jax.ShapeDtypeStruct((t, SSD_WIDTH), F32),
                   jax.ShapeDtypeStruct((t, SSD_GROUPS * SSD_STATE), F32),
                   jax.ShapeDtypeStruct((t, SSD_GROUPS * SSD_STATE), F32),
                   jax.ShapeDtypeStruct((SSD_GROUPS, t, LANES), F32), small, small, small],
        scratch_shapes=[pltpu.VMEM((N_PAIR, L, LANES), F32)],
        compiler_params=_params(("parallel", "arbitrary")), name="ssd_bwd")(
            xbc, xbc, xbc, dtr, bias, alog, dskip, hs, dy)


GN = SSD_WIDTH // SSD_GROUPS


def _post_fwd(o, proj, y, nw, *, tm=256):
    t = o.shape[0]

    def body(o_ref, za_ref, y_ref, zs_ref, nw_ref, ya_ref, ys_ref):
        za = za_ref[...]
        ya_ref[...] = _bf(o_ref[...] * (za * _sigmoid(za)))
        zs = zs_ref[...]
        yz = y_ref[...] * (zs * _sigmoid(zs))
        for g in range(SSD_GROUPS):
            sl = slice(g * GN, (g + 1) * GN)
            v = yz[:, sl]
            r = lax.rsqrt(jnp.mean(v * v, axis=-1, keepdims=True) + EPS)
            ys_ref[:, sl] = _bf(v * r * nw_ref[:, sl])

    return pl.pallas_call(
        body, grid=(t // tm,),
        in_specs=[pl.BlockSpec((tm, SB_WIDTH), lambda i: (i, 0)),
                  pl.BlockSpec((tm, SB_WIDTH), lambda i: (i, CB_ZA * LANES // SB_WIDTH)),
                  pl.BlockSpec((tm, SSD_WIDTH), lambda i: (i, 0)),
                  pl.BlockSpec((tm, SSD_WIDTH), lambda i: (i, CB_ZS * LANES // SSD_WIDTH)),
                  pl.BlockSpec((1, SSD_WIDTH), lambda i: (0, 0))],
        out_specs=[pl.BlockSpec((tm, SB_WIDTH), lambda i: (i, 0)),
                   pl.BlockSpec((tm, SSD_WIDTH), lambda i: (i, 0))],
        out_shape=[jax.ShapeDtypeStruct((t, SB_WIDTH), BF16), jax.ShapeDtypeStruct((t, SSD_WIDTH), BF16)],
        compiler_params=_params(("parallel",)), name="post_fwd")(o, proj, y, proj, nw)


def _post_bwd(dya, o, proj, dys, y, nw, *, tm=256):
    t = o.shape[0]

    def body(dya_ref, o_ref, za_ref, dys_ref, y_ref, zs_ref, nw_ref,
             do_ref, dza_ref, dy_ref, dzs_ref, dnw_ref):
        @pl.when(pl.program_id(0) == 0)
        def _():
            dnw_ref[...] = jnp.zeros_like(dnw_ref)

        za = za_ref[...]
        sa = _sigmoid(za)
        d = dya_ref[...]
        do_ref[...] = d * (za * sa)
        dza_ref[...] = _bf(d * o_ref[...] * (sa * (1.0 + za * (1.0 - sa))))
        zs = zs_ref[...]
        ss = _sigmoid(zs)
        silu = zs * ss
        yv = y_ref[...]
        yz = yv * silu
        dv = dys_ref[...]
        for g in range(SSD_GROUPS):
            sl = slice(g * GN, (g + 1) * GN)
            v = yz[:, sl]
            r = lax.rsqrt(jnp.mean(v * v, axis=-1, keepdims=True) + EPS)
            nrm = v * r
            dg = dv[:, sl]
            dnw_ref[:, sl] += jnp.sum(dg * nrm, axis=0, keepdims=True)
            dn = dg * nw_ref[:, sl]
            dyz = r * (dn - nrm * jnp.mean(dn * nrm, axis=-1, keepdims=True))
            dy_ref[:, sl] = dyz * silu[:, sl]
            dzs_ref[:, sl] = _bf(dyz * yv[:, sl] * (ss[:, sl] * (1.0 + zs[:, sl] * (1.0 - ss[:, sl]))))

    a_spec = pl.BlockSpec((tm, SB_WIDTH), lambda i: (i, 0))
    s_spec = pl.BlockSpec((tm, SSD_WIDTH), lambda i: (i, 0))
    w_spec = pl.BlockSpec((1, SSD_WIDTH), lambda i: (0, 0))
    return pl.pallas_call(
        body, grid=(t // tm,),
        in_specs=[a_spec, a_spec, pl.BlockSpec((tm, SB_WIDTH), lambda i: (i, CB_ZA * LANES // SB_WIDTH)),
                  s_spec, s_spec, pl.BlockSpec((tm, SSD_WIDTH), lambda i: (i, CB_ZS * LANES // SSD_WIDTH)),
                  w_spec],
        out_specs=[a_spec, a_spec, s_spec, s_spec, w_spec],
        out_shape=[jax.ShapeDtypeStruct((t, SB_WIDTH), F32), jax.ShapeDtypeStruct((t, SB_WIDTH), BF16),
                   jax.ShapeDtypeStruct((t, SSD_WIDTH), F32), jax.ShapeDtypeStruct((t, SSD_WIDTH), BF16),
                   jax.ShapeDtypeStruct((1, SSD_WIDTH), F32)],
        compiler_params=_params(("arbitrary",)), name="post_bwd")(dya, o, proj, dys, y, proj, nw)


def _merge_fwd(ya, ys, proj_g, *, tm=512):
    t = ya.shape[0]

    def body(ya_ref, ys_ref, ga_ref, gs_ref, m_ref):
        m_ref[...] = _bf(_sigmoid(ga_ref[...]) * ya_ref[...] + _sigmoid(gs_ref[...]) * ys_ref[...])

    spec = pl.BlockSpec((tm, D_MODEL), lambda i: (i, 0))
    return pl.pallas_call(
        body, grid=(t // tm,),
        in_specs=[spec, spec, pl.BlockSpec((tm, D_MODEL), lambda i: (i, 0)),
                  pl.BlockSpec((tm, D_MODEL), lambda i: (i, 1))],
        out_specs=spec, out_shape=jax.ShapeDtypeStruct((t, D_MODEL), BF16),
        compiler_params=_params(("parallel",)), name="merge_fwd")(ya, ys, proj_g, proj_g)


def _merge_bwd(dm, ya, ys, proj_g, *, tm=512):
    t = ya.shape[0]

    def body(dm_ref, ya_ref, ys_ref, ga_ref, gs_ref, dya_ref, dys_ref, dg_ref):
        d = dm_ref[...]
        ga = _sigmoid(ga_ref[...])
        gs = _sigmoid(gs_ref[...])
        dya_ref[...] = _bf(d * ga)
        dys_ref[...] = _bf(d * gs)
        dg_ref[:, :D_MODEL] = _bf(d * ya_ref[...] * (ga * (1.0 - ga)))
        dg_ref[:, D_MODEL:] = _bf(d * ys_ref[...] * (gs * (1.0 - gs)))

    spec = pl.BlockSpec((tm, D_MODEL), lambda i: (i, 0))
    return pl.pallas_call(
        body, grid=(t // tm,),
        in_specs=[spec, spec, spec, pl.BlockSpec((tm, D_MODEL), lambda i: (i, 0)),
                  pl.BlockSpec((tm, D_MODEL), lambda i: (i, 1))],
        out_specs=[spec, spec, pl.BlockSpec((tm, 2 * D_MODEL), lambda i: (i, 0))],
        out_shape=[jax.ShapeDtypeStruct((t, D_MODEL), BF16), jax.ShapeDtypeStruct((t, D_MODEL), BF16),
                   jax.ShapeDtypeStruct((t, 2 * D_MODEL), BF16)],
        compiler_params=_params(("parallel",)), name="merge_bwd")(dm, ya, ys, proj_g, proj_g)


def _final(x, out, wf, target, *, tm=512):
    t = x.shape[0]

    def body(x_ref, o_ref, w_ref, t_ref, loss_ref, dx_ref, dw_ref):
        @pl.when(pl.program_id(0) == 0)
        def _():
            loss_ref[...] = jnp.zeros_like(loss_ref)
            dw_ref[...] = jnp.zeros_like(dw_ref)

        x2 = x_ref[...] + o_ref[...]
        r = lax.rsqrt(jnp.mean(x2 * x2, axis=-1, keepdims=True) + EPS)
        nrm = x2 * r
        e = nrm * w_ref[...] - t_ref[...]
        row_loss = jnp.mean(e * e, axis=-1, keepdims=True)
        loss_ref[...] += 0.5 * jnp.sum(row_loss, axis=0, keepdims=True)
        dyv = e * (1.0 / D_MODEL)
        dw_ref[...] += jnp.sum(dyv * nrm, axis=0, keepdims=True)
        dn = dyv * w_ref[...]
        dx_ref[...] = r * (dn - nrm * jnp.mean(dn * nrm, axis=-1, keepdims=True))

    spec = pl.BlockSpec((tm, D_MODEL), lambda i: (i, 0))
    fix = pl.BlockSpec((1, D_MODEL), lambda i: (0, 0))
    return pl.pallas_call(
        body, grid=(t // tm,),
        in_specs=[spec, spec, fix, spec],
        out_specs=[pl.BlockSpec((1, LANES), lambda i: (0, 0)), spec, fix],
        out_shape=[jax.ShapeDtypeStruct((1, LANES), F32), jax.ShapeDtypeStruct((t, D_MODEL), F32),
                   jax.ShapeDtypeStruct((1, D_MODEL), F32)],
        compiler_params=_params(("arbitrary",)), name="final_loss")(x, out, wf, target)


def _group_lanes(v):
    return jnp.pad(v.reshape(SSD_GROUPS, 1, SSD_HPG), ((0, 0), (0, 0), (0, LANES - SSD_HPG)))


def _ungroup_lanes(v):
    return v[:, :, :SSD_HPG].reshape(1, N_DT)


def _local_step(x, target, nb, s, w_in_t, conv_w, conv_b, norm_w, dt_bias, a_log, d_skip,
                ssm_norm_w, w_attn_out, w_ssm_out, w_o, final_norm_w):
    t = nb * s
    w_g_t = w_in_t[GATE_COL0:]
    w_dt_t = jnp.pad(w_in_t[DT_COL0:GATE_COL0], ((0, LANES - N_DT), (0, 0)))
    h, r1 = _rms_fwd(x, norm_w)
    proj = _matmul(h, w_in_t, nt=True, n=DT_COL0, name="proj_main", tm=512, tn=1024)
    proj_g = _matmul(h, w_g_t, nt=True, name="proj_gate", tm=512, tn=1024)
    dt_raw = _matmul(h, w_dt_t, nt=True, name="proj_dt")
    dtr = jnp.pad(dt_raw[:, :N_DT].reshape(t, SSD_GROUPS, SSD_HPG).transpose(1, 0, 2),
                  ((0, 0), (0, 0), (0, LANES - SSD_HPG)))
    bias_g, alog_g, dsk_g = _group_lanes(dt_bias), _group_lanes(a_log), _group_lanes(d_skip)

    o, tot = _attn_fwd(proj, nb, s)
    xbc = _conv_fwd(proj, conv_w, conv_b, nb, s)
    y, hs = _ssd_fwd(xbc, dtr, bias_g, alog_g, dsk_g, nb, s)
    ya_in, ys_in = _post_fwd(o, proj, y, ssm_norm_w)
    ya = _matmul(ya_in, w_attn_out, name="attn_out")
    ys = _matmul(ys_in, w_ssm_out, name="ssm_out", tk=2048)
    merged = _merge_fwd(ya, ys, proj_g)
    out = _matmul(merged, w_o, name="out_proj")
    loss, dx2, d_final_w = _final(x, out, final_norm_w, target)

    dx2b = _bf(dx2)
    dmerged = _matmul(dx2b, w_o, nt=True, name="d_merged")
    d_w_o = _matmul(merged.T, dx2b, name="d_w_o", tm=1024, tk=1024)
    dya, dys, dgate = _merge_bwd(dmerged, ya, ys, proj_g)
    d_w_attn_out = _matmul(ya_in.T, dya, name="d_w_attn_out", tm=1024)
    d_w_ssm_out = _matmul(ys_in.T, dys, name="d_w_ssm_out", tm=1024)
    dya_in = _matmul(dya, w_attn_out, nt=True, name="d_ya_in")
    dys_in = _matmul(dys, w_ssm_out, nt=True, name="d_ys_in")
    do, dza, dy, dzs, d_ssm_norm_w = _post_bwd(dya_in, o, proj, dys_in, y, ssm_norm_w)
    dq, dk, dv = _attn_bwd(proj, do, tot, nb, s)
    dxs_act, db_act, dc_act, ddtr, dbias_g, dalog_g, ddsk_g = _ssd_bwd(
        xbc, dtr, bias_g, alog_g, dsk_g, hs, dy, nb, s)
    dxs, dwx, dbx = _conv_bwd(proj, dxs_act, conv_w, conv_b, nb, s, col0=0, name="conv_bwd_x")
    dbm, dwb, dbb = _conv_bwd(proj, db_act, conv_w, conv_b, nb, s, col0=SSD_WIDTH, name="conv_bwd_b")
    dcm, dwc, dbc = _conv_bwd(proj, dc_act, conv_w, conv_b, nb, s, col0=SSD_WIDTH + 512, name="conv_bwd_c")
    dproj = jnp.concatenate([dq, dk, dv, dza, dzs, dxs, dbm, dcm], axis=1)
    ddt = jnp.pad(_bf(ddtr[:, :, :SSD_HPG].transpose(1, 0, 2).reshape(t, N_DT)), ((0, 0), (0, LANES - N_DT)))
    ht = h.T
    wt = dict(out_t=True, out_dtype=BF16, tm=1024, tn=512, tk=1024)
    d_w_in_t = jnp.concatenate(
        [_matmul(ht, dproj, name="d_w_main", **wt), _matmul(ht, ddt, name="d_w_dt", **wt)[:N_DT],
         _matmul(ht, dgate, name="d_w_gate", **wt)], axis=0)
    dh = _matmul(ddt, w_dt_t, name="d_h_dt")
    dh = _matmul(dgate, w_g_t, add=dh, name="d_h_gate", tm=1024, tn=1024, tk=1024)
    dh = _matmul(dproj, w_in_t, add=dh, name="d_h", tm=1024, tn=1024, tk=1024)
    grad_x, d_norm_w = _rms_bwd(dh, x, r1, norm_w, dx2)

    grads = dict(
        norm_w=d_norm_w, w_in_t=d_w_in_t,
        conv_w=jnp.concatenate([dwx, dwb, dwc], axis=1), conv_b=jnp.concatenate([dbx, dbb, dbc], axis=1),
        dt_bias=_ungroup_lanes(dbias_g), a_log=_ungroup_lanes(dalog_g), d_skip=_ungroup_lanes(ddsk_g),
        ssm_norm_w=d_ssm_norm_w, w_attn_out=d_w_attn_out, w_ssm_out=d_w_ssm_out, w_o=d_w_o,
        final_norm_w=d_final_w)
    return loss, grad_x, grads


def _exchange(scatter, bcast, *, name):
    arrays = list(scatter) + list(bcast)
    ns, n = len(scatter), len(arrays)
    npeer = N_DEV - 1

    def body(*refs):
        ins, outs = refs[:n], refs[n:2 * n]
        send_sems, recv_sems, local_sems = refs[2 * n:]
        x, y, c = lax.axis_index("x"), lax.axis_index("y"), lax.axis_index("c")
        me = 4 * x + 2 * y + c
        copies = []
        for a in range(n):
            own = ins[a].at[me] if a < ns else ins[a]
            cp = pltpu.make_async_copy(own, outs[a].at[me], local_sems.at[a])
            cp.start()
            copies.append(cp)
            for r in range(1, N_DEV):
                tx = 1 - x if (r >> 2) & 1 else x
                ty = 1 - y if (r >> 1) & 1 else y
                tc = 1 - c if r & 1 else c
                src = ins[a].at[4 * tx + 2 * ty + tc] if a < ns else ins[a]
                cp = pltpu.make_async_remote_copy(
                    src_ref=src, dst_ref=outs[a].at[me],
                    send_sem=send_sems.at[a * npeer + r - 1], recv_sem=recv_sems.at[a * npeer + r - 1],
                    device_id=(tx, ty, tc), device_id_type=pl.DeviceIdType.MESH)
                cp.start()
                copies.append(cp)
        for cp in copies:
            cp.wait()

    out_shape = [jax.ShapeDtypeStruct(v.shape if i < ns else (N_DEV,) + v.shape, v.dtype)
                 for i, v in enumerate(arrays)]
    return pl.pallas_call(
        body, in_specs=[pl.BlockSpec(memory_space=pl.ANY)] * n,
        out_specs=[pl.BlockSpec(memory_space=pl.ANY)] * n, out_shape=out_shape,
        scratch_shapes=[pltpu.SemaphoreType.DMA((n * npeer,)), pltpu.SemaphoreType.DMA((n * npeer,)),
                        pltpu.SemaphoreType.DMA((n,))],
        compiler_params=pltpu.CompilerParams(has_side_effects=True), name=name)(*arrays)


def _gather(arrays, *, name):
    n = len(arrays)
    per = N_DEV - 1

    def body(*refs):
        ins, outs = refs[:n], refs[n:2 * n]
        send_sems, recv_sems, local_sems = refs[2 * n:]
        x, y, c = lax.axis_index("x"), lax.axis_index("y"), lax.axis_index("c")
        me, sibling = (x, y, c), (x, y, 1 - c)
        chips = [(1 - x, y), (x, 1 - y), (1 - x, 1 - y)]

        def copy(a, k, block, to, src=None):
            rows = outs[a].at[4 * block[0] + 2 * block[1] + block[2]]
            return pltpu.make_async_remote_copy(
                src_ref=rows if src is None else src, dst_ref=rows,
                send_sem=send_sems.at[a * per + k], recv_sem=recv_sems.at[a * per + k],
                device_id=to, device_id_type=pl.DeviceIdType.MESH)

        started = []
        for a in range(n):
            mine = pltpu.make_async_copy(ins[a], outs[a].at[4 * x + 2 * y + c], local_sems.at[a])
            mine.start()
            started.append(mine)
        for a in range(n):
            first = [copy(a, 1 + j, me, (*chip, c), src=ins[a]) for j, chip in enumerate(chips)]
            first.append(copy(a, 0, me, sibling, src=ins[a]))
            for cp in first:
                cp.start()
            started += first
        for j, chip in enumerate(chips):
            for a in range(n):
                copy(a, 1 + j, (*chip, c), me).wait_recv()
                passed = copy(a, 4 + j, (*chip, c), sibling)
                passed.start()
                started.append(passed)
        for a in range(n):
            copy(a, 0, sibling, me).wait_recv()
            for j, chip in enumerate(chips):
                copy(a, 4 + j, (*chip, 1 - c), me).wait_recv()
        for cp in started[:n]:
            cp.wait()
        for cp in started[n:]:
            cp.wait_send()

    return pl.pallas_call(
        body, in_specs=[pl.BlockSpec(memory_space=pl.ANY)] * n,
        out_specs=[pl.BlockSpec(memory_space=pl.ANY)] * n,
        out_shape=[jax.ShapeDtypeStruct((N_DEV,) + v.shape, v.dtype) for v in arrays],
        scratch_shapes=[pltpu.SemaphoreType.DMA((n * per,)), pltpu.SemaphoreType.DMA((n * per,)),
                        pltpu.SemaphoreType.DMA((n,))],
        compiler_params=pltpu.CompilerParams(has_side_effects=True), name=name)(*arrays)


def _adamw(parts, w, m, v, *, name, tc=None):
    r, c = w.shape
    tc = c if tc is None else tc
    assert c % tc == 0

    def body(p_ref, w_ref, m_ref, v_ref, g_ref, d_ref, nm_ref, nv_ref):
        g = p_ref[0].astype(F32)
        for d in range(1, N_DEV):
            g = g + p_ref[d].astype(F32)
        mm = ADAM_B1 * m_ref[...] + (1.0 - ADAM_B1) * g
        vv = ADAM_B2 * v_ref[...] + (1.0 - ADAM_B2) * (g * g)
        m_hat = mm / (1.0 - ADAM_B1 ** ADAM_STEP)
        v_hat = vv / (1.0 - ADAM_B2 ** ADAM_STEP)
        g_ref[...] = g
        d_ref[...] = -ADAM_LR * (m_hat / (jnp.sqrt(v_hat) + ADAM_EPS) + ADAM_WD * w_ref[...])
        nm_ref[...] = mm
        nv_ref[...] = vv

    spec = pl.BlockSpec((r, tc), lambda i: (0, i))
    out = jax.ShapeDtypeStruct((r, c), F32)
    return pl.pallas_call(
        body, grid=(c // tc,),
        in_specs=[pl.BlockSpec((N_DEV, r, tc), lambda i: (0, 0, i)), spec, spec, spec],
        out_specs=[spec] * 4, out_shape=[out] * 4,
        compiler_params=_params(("parallel",)), name=name)(parts, w, m, v)


SMALL = (("norm_w", D_MODEL), ("conv_b", CONV_DIM), ("dt_bias", N_DT), ("a_log", N_DT),
         ("d_skip", N_DT), ("ssm_norm_w", SSD_WIDTH), ("final_norm_w", D_MODEL))
SMALL_ROWS = 64
SHARD = D_PROJ // N_DEV


def _pack_small(vals):
    flat = jnp.concatenate([vals[k].reshape(-1).astype(F32) for k, _ in SMALL])
    return jnp.pad(flat, (0, SMALL_ROWS * LANES - flat.shape[0])).reshape(SMALL_ROWS, LANES)


def _unpack_small(packed, shapes):
    flat = packed.reshape(-1)
    out, off = {}, 0
    for k, size in SMALL:
        out[k] = flat[off:off + size].reshape(shapes[k])
        off += size
    return out


def kernel(x, norm_w, w_in, conv_w, conv_b, dt_bias, a_log, d_skip, ssm_norm_w, w_attn_out, w_ssm_out, w_o, final_norm_w, loss_target, m_norm_w, m_w_in, m_conv_w, m_conv_b, m_dt_bias, m_a_log, m_d_skip, m_ssm_norm_w, m_w_attn_out, m_w_ssm_out, m_w_o, m_final_norm_w, v_norm_w, v_w_in, v_conv_w, v_conv_b, v_dt_bias, v_a_log, v_d_skip, v_ssm_norm_w, v_w_attn_out, v_w_ssm_out, v_w_o, v_final_norm_w):
    nb, s, _ = x.shape
    t = nb * s
    weights = dict(norm_w=norm_w, w_in=w_in, conv_w=conv_w, conv_b=conv_b, dt_bias=dt_bias, a_log=a_log,
                   d_skip=d_skip, ssm_norm_w=ssm_norm_w, w_attn_out=w_attn_out, w_ssm_out=w_ssm_out,
                   w_o=w_o, final_norm_w=final_norm_w)
    moms = dict(norm_w=m_norm_w, w_in=m_w_in, conv_w=m_conv_w, conv_b=m_conv_b, dt_bias=m_dt_bias,
                a_log=m_a_log, d_skip=m_d_skip, ssm_norm_w=m_ssm_norm_w, w_attn_out=m_w_attn_out,
                w_ssm_out=m_w_ssm_out, w_o=m_w_o, final_norm_w=m_final_norm_w)
    vels = dict(norm_w=v_norm_w, w_in=v_w_in, conv_w=v_conv_w, conv_b=v_conv_b, dt_bias=v_dt_bias,
                a_log=v_a_log, d_skip=v_d_skip, ssm_norm_w=v_ssm_norm_w, w_attn_out=v_w_attn_out,
                w_ssm_out=v_w_ssm_out, w_o=v_w_o, final_norm_w=v_final_norm_w)

    tr_ = lambda a: jnp.transpose(a[0])
    g_in, g_conv, g_wa, g_ws, g_wo = _gather(
        [_bf(tr_(w_in)), conv_w[0], _bf(w_attn_out[0]), _bf(w_ssm_out[0]), _bf(w_o[0])], name="gather_weights")
    conv_full = g_conv.transpose(1, 0, 2).reshape(CONV_K, CONV_DIM)

    loss_part, grad_x, grads = _local_step(
        x.reshape(t, D_MODEL), loss_target.reshape(t, D_MODEL), nb, s, g_in.reshape(D_PROJ, D_MODEL),
        conv_full, conv_b, norm_w, dt_bias, a_log, d_skip, ssm_norm_w, g_wa.reshape(SB_WIDTH, D_MODEL),
        g_ws.reshape(SSD_WIDTH, D_MODEL), g_wo.reshape(D_MODEL, D_MODEL), final_norm_w.reshape(1, D_MODEL))

    s_in = grads["w_in_t"].reshape(N_DEV, SHARD, D_MODEL)
    s_conv = grads["conv_w"].reshape(CONV_K, N_DEV, CONV_DIM // N_DEV).transpose(1, 0, 2)
    s_wa = _bf(grads["w_attn_out"].reshape(N_DEV, SB_WIDTH // N_DEV, D_MODEL))
    s_ws = _bf(grads["w_ssm_out"].reshape(N_DEV, SSD_WIDTH // N_DEV, D_MODEL))
    s_wo = _bf(grads["w_o"].reshape(N_DEV, D_MODEL // N_DEV, D_MODEL))
    small = _pack_small(grads)
    p_in, p_conv, p_wa, p_ws, p_wo, p_small = _exchange(
        [s_in, s_conv, s_wa, s_ws, s_wo], [small], name="exchange_grads")

    res = {}
    res["w_in"] = [a.T for a in _adamw(p_in, tr_(w_in), tr_(m_w_in), tr_(v_w_in), name="adamw_w_in", tc=LANES)]
    res["conv_w"] = _adamw(p_conv, conv_w[0], m_conv_w[0], v_conv_w[0], name="adamw_conv_w")
    res["w_attn_out"] = _adamw(p_wa, w_attn_out[0], m_w_attn_out[0], v_w_attn_out[0], name="adamw_w_attn_out")
    res["w_ssm_out"] = _adamw(p_ws, w_ssm_out[0], m_w_ssm_out[0], v_w_ssm_out[0], name="adamw_w_ssm_out")
    res["w_o"] = _adamw(p_wo, w_o[0], m_w_o[0], v_w_o[0], name="adamw_w_o")
    sm = _adamw(p_small, _pack_small(weights), _pack_small(moms), _pack_small(vels), name="adamw_small")
    shapes = {k: weights[k].shape for k, _ in SMALL}
    sm = [_unpack_small(v, shapes) for v in sm]
    for k, _ in SMALL:
        res[k] = tuple(part[k] for part in sm)

    order = ("norm_w", "w_in", "conv_w", "conv_b", "dt_bias", "a_log", "d_skip", "ssm_norm_w",
             "w_attn_out", "w_ssm_out", "w_o", "final_norm_w")
    loss = lax.psum(loss_part[0, 0], ("x", "y", "c"))
    outs = [loss, grad_x.reshape(nb, s, D_MODEL)]
    for i in range(4):
        outs += [res[k][i].reshape(weights[k].shape) for k in order]
    return tuple(outs)
```

```python
import functools

import jax
import jax.numpy as jnp
from jax import lax
from jax.experimental import pallas as pl
from jax.experimental.pallas import tpu as pltpu

F32 = jnp.float32
BF16 = jnp.bfloat16

D_MODEL = 1024
SB_WIDTH = 1024
HEAD_DIM = 64
SSD_WIDTH = 2048
SSD_GROUPS = 4
SSD_HPG = 8
SSD_STATE = 128
SSD_CHUNK = 128
CONV_K = 4
CONV_DIM = 3072
N_DT = 32
D_PROJ = 11296
DT_COL0 = 9216
GATE_COL0 = DT_COL0 + N_DT
EPS = 1e-6
N_DEV = 8

ADAM_LR = 0.001
ADAM_B1 = 0.9
ADAM_B2 = 0.999
ADAM_EPS = 1e-08
ADAM_WD = 0.01
ADAM_STEP = 10

LANES = 128
VMEM_LIMIT = 56 * 1024 * 1024

CB_Q, CB_K, CB_V, CB_ZA, CB_ZS, CB_XBC = 0, 8, 16, 24, 32, 48

NT_DIMS = (((1,), (1,)), ((), ()))
TN_DIMS = (((0,), (0,)), ((), ()))
HI = lax.Precision.HIGHEST


def _params(sem):
    return pltpu.CompilerParams(dimension_semantics=sem, vmem_limit_bytes=VMEM_LIMIT)


def _dot(a, b):
    return jnp.dot(a, b, preferred_element_type=F32)


def _dot_nt(a, b):
    return lax.dot_general(a, b, NT_DIMS, preferred_element_type=F32)


def _dot_tn(a, b):
    return lax.dot_general(a, b, TN_DIMS, preferred_element_type=F32)


def _dot_hi(a, b):
    return jnp.dot(a, b, preferred_element_type=F32, precision=HI)


def _bf(a):
    return a.astype(BF16)


def _sigmoid(x):
    return 1.0 / (1.0 + jnp.exp(-x))


def _softplus(x):
    return jnp.maximum(x, 0.0) + jnp.log(1.0 + jnp.exp(-jnp.abs(x)))


def _trunc_split(a):
    bits = lax.bitcast_convert_type(a, jnp.uint32) & jnp.uint32(0xFFFF0000)
    hi = lax.bitcast_convert_type(bits, F32)
    return _bf(hi), _bf(a - hi)


def _matmul(a, b, *, name, nt=False, add=None, out_dtype=F32, tm=512, tn=512, tk=1024, n=None, out_t=False):
    m, k = a.shape
    n = (b.shape[0] if nt else b.shape[1]) if n is None else n
    tm, tn, tk = min(tm, m), min(tn, n), min(tk, k)
    assert m % tm == 0 and n % tn == 0 and k % tk == 0, (name, a.shape, b.shape)
    assert not (out_t and add is not None)
    nk = k // tk
    has_add = add is not None

    def body(*refs):
        if has_add:
            a_ref, b_ref, add_ref, o_ref, acc_ref = refs
        else:
            a_ref, b_ref, o_ref, acc_ref = refs
        kk = pl.program_id(2)

        @pl.when(kk == 0)
        def _():
            acc_ref[...] = jnp.zeros_like(acc_ref)

        av, bv = _bf(a_ref[...]), _bf(b_ref[...])
        acc_ref[...] += _dot_nt(av, bv) if nt else _dot(av, bv)

        @pl.when(kk == nk - 1)
        def _():
            r = acc_ref[...]
            if has_add:
                r = r + add_ref[...]
            o_ref[...] = (r.T if out_t else r).astype(out_dtype)

    in_specs = [pl.BlockSpec((tm, tk), lambda j, i, kk: (i, kk)),
                pl.BlockSpec((tn, tk), lambda j, i, kk: (j, kk)) if nt
                else pl.BlockSpec((tk, tn), lambda j, i, kk: (kk, j))]
    args = [a, b]
    if has_add:
        in_specs.append(pl.BlockSpec((tm, tn), lambda j, i, kk: (i, j)))
        args.append(add)
    if out_t:
        out_spec, out_dims = pl.BlockSpec((tn, tm), lambda j, i, kk: (j, i)), (n, m)
    else:
        out_spec, out_dims = pl.BlockSpec((tm, tn), lambda j, i, kk: (i, j)), (m, n)
    return pl.pallas_call(
        body, grid=(n // tn, m // tm, nk), in_specs=in_specs, out_specs=out_spec,
        out_shape=jax.ShapeDtypeStruct(out_dims, out_dtype),
        scratch_shapes=[pltpu.VMEM((tm, tn), F32)],
        compiler_params=_params(("parallel", "parallel", "arbitrary")), name=name)(*args)


def _rms_fwd(x, w, *, tm=512):
    t = x.shape[0]

    def body(x_ref, w_ref, h_ref, r_ref):
        xv = x_ref[...]
        r = lax.rsqrt(jnp.mean(xv * xv, axis=-1, keepdims=True) + EPS)
        h_ref[...] = _bf(xv * r * w_ref[...])
        r_ref[...] = r

    row = lambda i: (i, 0)
    return pl.pallas_call(
        body, grid=(t // tm,),
        in_specs=[pl.BlockSpec((tm, D_MODEL), row), pl.BlockSpec((1, D_MODEL), lambda i: (0, 0))],
        out_specs=[pl.BlockSpec((tm, D_MODEL), row), pl.BlockSpec((tm, 1), row)],
        out_shape=[jax.ShapeDtypeStruct((t, D_MODEL), BF16), jax.ShapeDtypeStruct((t, 1), F32)],
        compiler_params=_params(("parallel",)), name="rms_fwd")(x, w)


def _rms_bwd(dh, x, r, w, dres, *, tm=512):
    t = x.shape[0]

    def body(dh_ref, x_ref, r_ref, w_ref, dres_ref, dx_ref, dw_ref):
        @pl.when(pl.program_id(0) == 0)
        def _():
            dw_ref[...] = jnp.zeros_like(dw_ref)

        nrm = x_ref[...] * r_ref[...]
        dhv = dh_ref[...]
        dw_ref[...] += jnp.sum(dhv * nrm, axis=0, keepdims=True)
        dn = dhv * w_ref[...]
        dx = r_ref[...] * (dn - nrm * jnp.mean(dn * nrm, axis=-1, keepdims=True))
        dx_ref[...] = dx + dres_ref[...]

    row = lambda i: (i, 0)
    fix = lambda i: (0, 0)
    return pl.pallas_call(
        body, grid=(t // tm,),
        in_specs=[pl.BlockSpec((tm, D_MODEL), row), pl.BlockSpec((tm, D_MODEL), row),
                  pl.BlockSpec((tm, 1), row), pl.BlockSpec((1, D_MODEL), fix),
                  pl.BlockSpec((tm, D_MODEL), row)],
        out_specs=[pl.BlockSpec((tm, D_MODEL), row), pl.BlockSpec((1, D_MODEL), fix)],
        out_shape=[jax.ShapeDtypeStruct((t, D_MODEL), F32), jax.ShapeDtypeStruct((1, D_MODEL), F32)],
        compiler_params=_params(("arbitrary",)), name="rms_bwd")(dh, x, r, w, dres)


ATT_BLK = 256


def _attn_masks(blk):
    r_io = lax.broadcasted_iota(jnp.int32, (blk, blk), 0)
    c_io = lax.broadcasted_iota(jnp.int32, (blk, blk), 1)
    return r_io, c_io


def _attn_fwd(proj, nb, s):
    blk = min(ATT_BLK, s)
    nq = s // blk
    scale = HEAD_DIM ** -0.5

    def body(q_ref, k_ref, v_ref, o_ref, tot_ref, q0_s, q1_s, kb_s, v0_s, v1_s, z_s, a_s):
        head0 = lax.broadcasted_iota(jnp.int32, (s, LANES), 1) < HEAD_DIM
        lane2 = lax.broadcasted_iota(jnp.int32, (blk, 2), 1)
        r_io, c_io = _attn_masks(blk)
        tri = c_io < r_io
        u_gt = _bf(r_io > c_io)
        u2 = jnp.concatenate([u_gt, u_gt], axis=0)
        qv = q_ref[...] * scale
        q0_s[...] = _bf(jnp.where(head0, qv, 0.0))
        q1_s[...] = _bf(jnp.where(head0, 0.0, qv))
        kb_s[...] = _bf(k_ref[...])
        vv = v_ref[...]
        v0_s[...] = _bf(jnp.where(head0, vv, 0.0))
        v1_s[...] = _bf(jnp.where(head0, 0.0, vv))

        def scores(qhs, kblk, slot):
            k0 = pl.multiple_of(jnp.maximum(kblk, 0) * blk, blk)
            kb = kb_s[pl.ds(k0, blk), :]
            for hh in range(2):
                z_s[slot, hh] = _dot_nt(qhs[hh], kb)

        def weights(slot, crs, diag):
            new = []
            for hh in range(2):
                z = z_s[slot, hh]
                sp = _softplus(z)
                spm = jnp.where(tri, sp, 0.0) if diag else sp
                hi, lo = _trunc_split(spm)
                ps = _dot(jnp.concatenate([hi, lo], axis=1), u2)
                a = jnp.exp(z - (sp + ps + crs[hh]))
                if diag:
                    a = jnp.where(tri, a, 0.0)
                a_s[slot, :, hh * blk:(hh + 1) * blk] = _bf(a)
                new.append(crs[hh] + ps[:, 0:1] + spm[:, 0:1])
            return new

        def apply(acc, slot, k0):
            vcat = jnp.concatenate([v0_s[pl.ds(k0, blk), :], v1_s[pl.ds(k0, blk), :]], axis=0)
            return acc + _dot(a_s[slot], vcat)

        def qblock(qi, _):
            q0 = pl.multiple_of(qi * blk, blk)
            qhs = (q0_s[pl.ds(q0, blk), :], q1_s[pl.ds(q0, blk), :])
            zc = jnp.zeros((blk, 1), F32)
            scores(qhs, qi, 0)
            scores(qhs, qi - 1, 1)
            crs = weights(0, (zc, zc), True)

            def step(j, slot, st):
                acc, c0, c1 = st
                acc = apply(acc, 1 - slot, pl.multiple_of((qi - j + 1) * blk, blk))
                scores(qhs, qi - j - 1, 1 - slot)
                crs = weights(slot, (c0, c1), False)
                return acc, crs[0], crs[1]

            st = (jnp.zeros((blk, LANES), F32), crs[0], crs[1])
            st = lax.fori_loop(0, qi // 2, lambda i, st: step(2 * i + 2, 0, step(2 * i + 1, 1, st)), st)
            st = lax.cond(qi % 2 == 1, lambda st: step(qi, 1, st), lambda st: st, st)
            o_ref[pl.ds(q0, blk), :] = apply(st[0], qi & 1, 0)
            tot_ref[pl.ds(q0, blk), :] = jnp.where(lane2 == 0, st[1], st[2])
            return 0

        lax.fori_loop(0, nq, qblock, 0)

    t = nb * s
    return pl.pallas_call(
        body, grid=(nb, 8),
        in_specs=[pl.BlockSpec((s, LANES), lambda b, p: (b, CB_Q + p)),
                  pl.BlockSpec((s, LANES), lambda b, p: (b, CB_K + p)),
                  pl.BlockSpec((s, LANES), lambda b, p: (b, CB_V + p))],
        out_specs=[pl.BlockSpec((s, LANES), lambda b, p: (b, p)),
                   pl.BlockSpec((None, None, s, 2), lambda b, p: (b, p, 0, 0))],
        out_shape=[jax.ShapeDtypeStruct((t, SB_WIDTH), F32),
                   jax.ShapeDtypeStruct((nb, 8, s, 2), F32)],
        scratch_shapes=[pltpu.VMEM((s, LANES), BF16)] * 5 + [
            pltpu.VMEM((2, 2, blk, blk), F32), pltpu.VMEM((2, blk, 2 * blk), BF16)],
        compiler_params=_params(("parallel", "parallel")), name="attn_fwd")(proj, proj, proj)


def _attn_bwd(proj, do, tot, nb, s):
    blk = min(ATT_BLK, s)
    nq = s // blk
    scale = HEAD_DIM ** -0.5

    def body(q_ref, k_ref, v_ref, do_ref, tot_ref, dq_ref, dk_ref, dv_ref,
             q0_s, q1_s, kb_s, k0_s, k1_s, vb_s, d0_s, d1_s, z_s, da_s, dz_s, a_s, dk_acc, dv_acc):
        head0 = lax.broadcasted_iota(jnp.int32, (s, LANES), 1) < HEAD_DIM
        r_io, c_io = _attn_masks(blk)
        tri = c_io < r_io
        u_le = _bf(r_io <= c_io)
        u2 = jnp.concatenate([u_le, u_le], axis=0)
        u_lt = _bf(r_io < c_io)
        qv = q_ref[...] * scale
        q0_s[...] = _bf(jnp.where(head0, qv, 0.0))
        q1_s[...] = _bf(jnp.where(head0, 0.0, qv))
        kv = k_ref[...]
        kb_s[...] = _bf(kv)
        k0_s[...] = _bf(jnp.where(head0, kv, 0.0))
        k1_s[...] = _bf(jnp.where(head0, 0.0, kv))
        vb_s[...] = _bf(v_ref[...])
        dov = do_ref[...]
        d0_s[...] = _bf(jnp.where(head0, dov, 0.0))
        d1_s[...] = _bf(jnp.where(head0, 0.0, dov))
        dk_acc[...] = jnp.zeros_like(dk_acc)
        dv_acc[...] = jnp.zeros_like(dv_acc)

        def qblock(qi, _):
            q0 = pl.multiple_of(qi * blk, blk)
            qs_ = pl.ds(q0, blk)
            qhs = (q0_s[qs_, :], q1_s[qs_, :])
            dhs = (d0_s[qs_, :], d1_s[qs_, :])
            qcat = jnp.concatenate(qhs, axis=0)
            dcat = jnp.concatenate(dhs, axis=0)
            totf = tot_ref[qs_, :]

            def scores(kblk, slot):
                ks_ = pl.ds(pl.multiple_of(kblk * blk, blk), blk)
                kb, vb = kb_s[ks_, :], vb_s[ks_, :]
                for hh in range(2):
                    z_s[slot, hh] = _dot_nt(qhs[hh], kb)
                    da_s[slot, hh] = _dot_nt(dhs[hh], vb)

            def grads(slot, st, diag):
                new = []
                for hh in range(2):
                    cn, cw = st[2 * hh], st[2 * hh + 1]
                    z = z_s[slot, hh]
                    sp = _softplus(z)
                    lb = z - sp
                    spm = jnp.where(tri, sp, 0.0) if diag else sp
                    hi, lo = _trunc_split(spm)
                    pin = _dot(jnp.concatenate([hi, lo], axis=1), u2)
                    a = jnp.exp(lb + (pin + cn))
                    if diag:
                        a = jnp.where(tri, a, 0.0)
                    w = a * da_s[slot, hh]
                    wex = _dot(_bf(w), u_lt)
                    dz = w - jnp.exp(lb) * (w + (wex + cw))
                    if diag:
                        dz = jnp.where(tri, dz, 0.0)
                    dz_s[slot, hh] = _bf(dz)
                    a_s[slot, hh] = _bf(a)
                    new += [cn + pin[:, blk - 1:blk], cw + wex[:, blk - 1:blk] + w[:, blk - 1:blk]]
                return tuple(new)

            def apply(dq, slot, kblk):
                ks_ = pl.ds(pl.multiple_of(kblk * blk, blk), blk)
                dzs = (dz_s[slot, 0], dz_s[slot, 1])
                dq = dq + _dot(jnp.concatenate(dzs, axis=1), jnp.concatenate([k0_s[ks_, :], k1_s[ks_, :]], axis=0))
                dk_acc[ks_, :] += _dot_tn(jnp.concatenate(dzs, axis=0), qcat)
                dv_acc[ks_, :] += _dot_tn(jnp.concatenate([a_s[slot, 0], a_s[slot, 1]], axis=0), dcat)
                return dq

            def step(j, slot, st, diag):
                dq = apply(st[0], 1 - slot, jnp.maximum(j - 1, 0))
                if not diag:
                    scores(j + 1, 1 - slot)
                return (dq, *grads(slot, st[1:], diag))

            dz_s[1] = jnp.zeros_like(dz_s[1])
            a_s[1] = jnp.zeros_like(a_s[1])
            scores(0, 0)
            zc = jnp.zeros((blk, 1), F32)
            st = (jnp.zeros((blk, LANES), F32), -totf[:, 0:1], zc, -totf[:, 1:2], zc)
            st = lax.fori_loop(
                0, qi // 2, lambda i, st: step(2 * i + 1, 1, step(2 * i, 0, st, False), False), st)
            st = lax.cond(qi % 2 == 1,
                          lambda st: step(qi, 1, step(qi - 1, 0, st, False), True),
                          lambda st: step(qi, 0, st, True), st)
            dq_ref[qs_, :] = _bf(apply(st[0], qi & 1, qi) * scale)
            return 0

        lax.fori_loop(0, nq, qblock, 0)
        dk_ref[...] = _bf(dk_acc[...])
        dv_ref[...] = _bf(dv_acc[...])

    t = nb * s
    hp = lambda b, p: (b, p)
    out = jax.ShapeDtypeStruct((t, SB_WIDTH), BF16)
    return pl.pallas_call(
        body, grid=(nb, 8),
        in_specs=[pl.BlockSpec((s, LANES), lambda b, p: (b, CB_Q + p)),
                  pl.BlockSpec((s, LANES), lambda b, p: (b, CB_K + p)),
                  pl.BlockSpec((s, LANES), lambda b, p: (b, CB_V + p)),
                  pl.BlockSpec((s, LANES), hp),
                  pl.BlockSpec((None, None, s, 2), lambda b, p: (b, p, 0, 0))],
        out_specs=[pl.BlockSpec((s, LANES), hp)] * 3,
        out_shape=[out, out, out],
        scratch_shapes=[pltpu.VMEM((s, LANES), BF16)] * 8 + [
            pltpu.VMEM((2, 2, blk, blk), F32), pltpu.VMEM((2, 2, blk, blk), F32),
            pltpu.VMEM((2, 2, blk, blk), BF16), pltpu.VMEM((2, 2, blk, blk), BF16),
            pltpu.VMEM((s, LANES), F32), pltpu.VMEM((s, LANES), F32)],
        compiler_params=_params(("parallel", "parallel")), name="attn_bwd")(proj, proj, proj, do, tot)


CONV_TC = 256


def _shift_down(x, k, row):
    return jnp.where(row >= k, pltpu.roll(x, k, 0), 0.0)


def _shift_up(x, k, row, s):
    return jnp.where(row < s - k, pltpu.roll(x, s - k, 0), 0.0)


def _conv_fwd(proj, conv_w, conv_b, nb, s):
    tc = CONV_TC
    cb0 = CB_XBC * LANES // tc

    def body(x_ref, w_ref, b_ref, o_ref):
        x = x_ref[...]
        row = lax.broadcasted_iota(jnp.int32, x.shape, 0)
        y = b_ref[...] + w_ref[CONV_K - 1:CONV_K, :] * x
        for kk in range(1, CONV_K):
            y = y + w_ref[CONV_K - 1 - kk:CONV_K - kk, :] * _shift_down(x, kk, row)
        o_ref[...] = y * _sigmoid(y)

    t = nb * s
    return pl.pallas_call(
        body, grid=(nb, CONV_DIM // tc),
        in_specs=[pl.BlockSpec((s, tc), lambda b, c: (b, cb0 + c)),
                  pl.BlockSpec((CONV_K, tc), lambda b, c: (0, c)),
                  pl.BlockSpec((1, tc), lambda b, c: (0, c))],
        out_specs=pl.BlockSpec((s, tc), lambda b, c: (b, c)),
        out_shape=jax.ShapeDtypeStruct((t, CONV_DIM), F32),
        compiler_params=_params(("parallel", "parallel")), name="conv_fwd")(proj, conv_w, conv_b)


def _conv_bwd(proj, dact, conv_w, conv_b, nb, s, *, col0, name):
    width = dact.shape[1]
    tc = min(CONV_TC, width)
    cb0 = (CB_XBC * LANES + col0) // tc
    wb0 = col0 // tc

    def body(x_ref, d_ref, w_ref, b_ref, dx_ref, dw_ref, db_ref):
        @pl.when(pl.program_id(1) == 0)
        def _():
            dw_ref[...] = jnp.zeros_like(dw_ref)
            db_ref[...] = jnp.zeros_like(db_ref)

        x = x_ref[...]
        row = lax.broadcasted_iota(jnp.int32, x.shape, 0)
        xs = [x] + [_shift_down(x, kk, row) for kk in range(1, CONV_K)]
        y = b_ref[...] + w_ref[CONV_K - 1:CONV_K, :] * x
        for kk in range(1, CONV_K):
            y = y + w_ref[CONV_K - 1 - kk:CONV_K - kk, :] * xs[kk]
        sg = _sigmoid(y)
        dy = d_ref[...] * (sg * (1.0 + y * (1.0 - sg)))
        dx = w_ref[CONV_K - 1:CONV_K, :] * dy
        for kk in range(1, CONV_K):
            dx = dx + w_ref[CONV_K - 1 - kk:CONV_K - kk, :] * _shift_up(dy, kk, row, s)
        dx_ref[...] = _bf(dx)
        db_ref[...] += jnp.sum(dy, axis=0, keepdims=True)
        for kk in range(CONV_K):
            dw_ref[CONV_K - 1 - kk:CONV_K - kk, :] += jnp.sum(dy * xs[kk], axis=0, keepdims=True)

    t = nb * s
    return pl.pallas_call(
        body, grid=(width // tc, nb),
        in_specs=[pl.BlockSpec((s, tc), lambda c, b: (b, cb0 + c)),
                  pl.BlockSpec((s, tc), lambda c, b: (b, c)),
                  pl.BlockSpec((CONV_K, tc), lambda c, b: (0, wb0 + c)),
                  pl.BlockSpec((1, tc), lambda c, b: (0, wb0 + c))],
        out_specs=[pl.BlockSpec((s, tc), lambda c, b: (b, c)),
                   pl.BlockSpec((CONV_K, tc), lambda c, b: (0, c)),
                   pl.BlockSpec((1, tc), lambda c, b: (0, c))],
        out_shape=[jax.ShapeDtypeStruct((t, width), BF16),
                   jax.ShapeDtypeStruct((CONV_K, width), F32),
                   jax.ShapeDtypeStruct((1, width), F32)],
        compiler_params=_params(("parallel", "arbitrary")), name=name)(proj, dact, conv_w, conv_b)


L = SSD_CHUNK
N_PAIR = SSD_HPG // 2


def _pair_lanes(v0, v1, lane):
    return jnp.where(lane < HEAD_DIM, v0, v1)


def _onehot(shape, row_shift, col_shift):
    row = lax.broadcasted_iota(jnp.int32, shape, 0)
    col = lax.broadcasted_iota(jnp.int32, shape, 1)
    return _bf(lax.shift_right_logical(row, row_shift) == lax.shift_right_logical(col, col_shift))


def _ssd_tables():
    sp = _onehot((LANES, N_PAIR * LANES), 0, 6)
    sf = _onehot((LANES, SSD_HPG * LANES), 0, 7)
    return (jnp.concatenate([sp] * 3, axis=0), jnp.concatenate([sf] * 3, axis=0),
            _onehot((N_PAIR * LANES, LANES), 6, 0), _onehot((SSD_HPG * LANES, LANES), 7, 0))


def _split3(a):
    a1 = _bf(a)
    r1 = a - a1.astype(F32)
    a2 = _bf(r1)
    return jnp.concatenate([a1, a2, _bf(r1 - a2.astype(F32))], axis=1)


def _split2(a):
    hi, lo = _trunc_split(a)
    return jnp.concatenate([hi, lo], axis=1)


def _ssd_chunk_common(dtr_ref, bias_ref, alog_ref, r0, spread_pair, spread_full):
    rr = lax.broadcasted_iota(jnp.int32, (L, L), 0)
    cc = lax.broadcasted_iota(jnp.int32, (L, L), 1)
    tri = (cc <= rr).astype(F32)
    raw = dtr_ref[pl.ds(r0, L), :] + bias_ref[...]
    dt = _softplus(raw)
    a_neg = -jnp.exp(alog_ref[...])
    acs = _dot_hi(tri, dt * a_neg)
    acs3 = _split3(acs)
    return (raw, dt, a_neg, acs, acs.T, rr, cc,
            _dot(_split3(dt), spread_pair), _dot(acs3, spread_pair), _dot(acs3, spread_full))


def _ssd_fwd(xbc, dtr, bias, alog, dskip, nb, s):
    nc = s // L
    t = nb * s

    def body(x_ref, b_ref, c_ref, dtr_ref, bias_ref, alog_ref, dsk_ref, y_ref, hs_ref, h_scr):
        lane = lax.broadcasted_iota(jnp.int32, (L, LANES), 1)
        lane1 = lax.broadcasted_iota(jnp.int32, (1, LANES), 1)
        rowc = lax.broadcasted_iota(jnp.int32, (L, 1), 0)
        h_scr[...] = jnp.zeros_like(h_scr)
        dsk = dsk_ref[...]
        spread_pair, spread_full, _, _ = _ssd_tables()

        def chunk(c, _):
            r0 = pl.multiple_of(c * L, L)
            raw, dt, a_neg, acs, acs_t, rr, cc, dt_all, acs_all, acs_full = _ssd_chunk_common(
                dtr_ref, bias_ref, alog_ref, r0, spread_pair, spread_full)
            causal = rr >= cc
            bb = _bf(b_ref[pl.ds(r0, L), :])
            cb = _bf(c_ref[pl.ds(r0, L), :])
            gm = _dot_nt(cb, bb)
            for pr in range(N_PAIR):
                j0, j1 = 2 * pr, 2 * pr + 1
                ms = []
                for j in (j0, j1):
                    seg = acs_full[:, j * LANES:(j + 1) * LANES] - acs_t[j:j + 1, :]
                    ms.append(_bf(gm * jnp.exp(jnp.where(causal, seg, -1e30))))
                x = x_ref[pl.ds(r0, L), pr * LANES:(pr + 1) * LANES]
                dt_p = dt_all[:, pr * LANES:(pr + 1) * LANES]
                acs_p = acs_all[:, pr * LANES:(pr + 1) * LANES]
                last_p = acs_p[L - 1:L, :]
                d_p = _pair_lanes(dsk[:, j0:j0 + 1], dsk[:, j1:j1 + 1], lane1)
                xd = x * dt_p
                xdb = _bf(xd)
                yd = _pair_lanes(_dot(ms[0], xdb), _dot(ms[1], xdb), lane)
                hp = h_scr[pr]
                yo = _dot_nt(cb, _bf(hp)) * jnp.exp(acs_p)
                y_ref[pl.ds(r0, L), pr * LANES:(pr + 1) * LANES] = yd + yo + x * d_p
                hs_ref[c, pr] = hp
                sp = _dot_tn(_bf(xd * jnp.exp(last_p - acs_p)), bb)
                cd = jnp.where(rowc < HEAD_DIM, jnp.exp(acs[L - 1:L, j0:j0 + 1]),
                               jnp.exp(acs[L - 1:L, j1:j1 + 1]))
                h_scr[pr] = hp * cd + sp
            return 0

        lax.fori_loop(0, nc, chunk, 0)

    grp = lambda b, g: (g, 0, 0)
    return pl.pallas_call(
        body, grid=(nb, SSD_GROUPS),
        in_specs=[pl.BlockSpec((s, 4 * LANES), lambda b, g: (b, g)),
                  pl.BlockSpec((s, LANES), lambda b, g: (b, 16 + g)),
                  pl.BlockSpec((s, LANES), lambda b, g: (b, 20 + g)),
                  pl.BlockSpec((None, s, LANES), lambda b, g: (g, b, 0)),
                  pl.BlockSpec((None, 1, LANES), grp), pl.BlockSpec((None, 1, LANES), grp),
                  pl.BlockSpec((None, 1, LANES), grp)],
        out_specs=[pl.BlockSpec((s, 4 * LANES), lambda b, g: (b, g)),
                   pl.BlockSpec((None, None, nc, N_PAIR, L, LANES), lambda b, g: (b, g, 0, 0, 0, 0))],
        out_shape=[jax.ShapeDtypeStruct((t, SSD_WIDTH), F32),
                   jax.ShapeDtypeStruct((nb, SSD_GROUPS, nc, N_PAIR, L, LANES), F32)],
        scratch_shapes=[pltpu.VMEM((N_PAIR, L, LANES), F32)],
        compiler_params=_params(("parallel", "parallel")), name="ssd_fwd")(
            xbc, xbc, xbc, dtr, bias, alog, dskip)


def _ssd_bwd(xbc, dtr, bias, alog, dskip, hs, dy, nb, s):
    nc = s // L
    t = nb * s

    def body(x_ref, b_ref, c_ref, dtr_ref, bias_ref, alog_ref, dsk_ref, hs_ref, dy_ref,
             dx_ref, db_ref, dc_ref, ddtr_ref, dbias_ref, dalog_ref, ddsk_ref, dh_scr):
        lane = lax.broadcasted_iota(jnp.int32, (L, LANES), 1)
        lane1 = lax.broadcasted_iota(jnp.int32, (1, LANES), 1)
        rowc = lax.broadcasted_iota(jnp.int32, (L, 1), 0)
        dh_scr[...] = jnp.zeros_like(dh_scr)
        dsk = dsk_ref[...]

        @pl.when(pl.program_id(1) == 0)
        def _():
            dbias_ref[...] = jnp.zeros_like(dbias_ref)
            dalog_ref[...] = jnp.zeros_like(dalog_ref)
            ddsk_ref[...] = jnp.zeros_like(ddsk_ref)

        spread_pair, spread_full, sum_pair, sum_full = _ssd_tables()
        sum_pair2 = jnp.concatenate([sum_pair] * 2, axis=0)
        sum_full2 = jnp.concatenate([sum_full] * 2, axis=0)

        def chunk(i, _):
            c = nc - 1 - i
            r0 = pl.multiple_of(c * L, L)
            raw, dt, a_neg, acs, acs_t, rr, cc, dt_all, acs_all, acs_full = _ssd_chunk_common(
                dtr_ref, bias_ref, alog_ref, r0, spread_pair, spread_full)
            causal = rr >= cc
            tri_t = (cc >= rr).astype(F32)
            bb = _bf(b_ref[pl.ds(r0, L), :])
            cb = _bf(c_ref[pl.ds(r0, L), :])
            gm = _dot_nt(cb, bb)
            dg = jnp.zeros((L, L), F32)
            dbacc = jnp.zeros((L, LANES), F32)
            dcacc = jnp.zeros((L, LANES), F32)
            dacs_t = jnp.zeros((L, L), F32)
            ta_all, te_all, qm_all, tc_all, td_all, dcd_all = [], [], [], [], [], []
            for pr in range(N_PAIR):
                js = (2 * pr, 2 * pr + 1)
                lms, mfs = [], []
                for j in js:
                    seg = acs_full[:, j * LANES:(j + 1) * LANES] - acs_t[j:j + 1, :]
                    lm = jnp.exp(jnp.where(causal, seg, -1e30))
                    lms.append(lm)
                    mfs.append(gm * lm)
                x = x_ref[pl.ds(r0, L), pr * LANES:(pr + 1) * LANES]
                dyv = dy_ref[pl.ds(r0, L), pr * LANES:(pr + 1) * LANES]
                dt_p = dt_all[:, pr * LANES:(pr + 1) * LANES]
                acs_p = acs_all[:, pr * LANES:(pr + 1) * LANES]
                last_p = acs_p[L - 1:L, :]
                d_p = _pair_lanes(dsk[:, js[0]:js[0] + 1], dsk[:, js[1]:js[1] + 1], lane1)
                e_p = jnp.exp(acs_p)
                dte_p = jnp.exp(last_p - acs_p)
                xd = x * dt_p
                xdb = _bf(xd)
                hp = hs_ref[c, pr]
                hb = _bf(hp)
                dye = _bf(dyv * e_p)
                yo = _dot_nt(cb, hb) * e_p
                dcacc = dcacc + _dot(dye, hb)
                dh_out = _dot_tn(dye, cb)
                dyb = _bf(dyv)
                dxd = jnp.zeros((L, LANES), F32)
                for hh in range(2):
                    hm = (lane >= HEAD_DIM) if hh else (lane < HEAD_DIM)
                    dm = _dot_nt(_bf(jnp.where(hm, dyv, 0.0)), xdb)
                    dxd = dxd + jnp.where(hm, _dot_tn(_bf(mfs[hh]), dyb), 0.0)
                    dg = dg + dm * lms[hh]
                    qm = dm * mfs[hh]
                    qm_all.append(qm)
                    dacs_t = dacs_t - jnp.where(rr == js[hh], jnp.sum(qm, axis=0, keepdims=True), 0.0)
                ds = dh_scr[pr]
                dsb = _bf(ds)
                xdd = xd * dte_p
                dxdd = _dot_nt(bb, dsb)
                dbacc = dbacc + _dot(_bf(xdd), dsb)
                dxd = dxd + dxdd * dte_p
                cd = jnp.where(rowc < HEAD_DIM, jnp.exp(acs[L - 1:L, js[0]:js[0] + 1]),
                               jnp.exp(acs[L - 1:L, js[1]:js[1] + 1]))
                prod = ds * hp
                dcd_all += [jnp.sum(prod[:HEAD_DIM], axis=0, keepdims=True),
                            jnp.sum(prod[HEAD_DIM:], axis=0, keepdims=True)]
                dh_scr[pr] = dh_out + ds * cd
                tcv = dxdd * xdd
                ta_all.append(dyv * yo - tcv)
                tc_all.append(jnp.sum(tcv, axis=0, keepdims=True))
                te_all.append(dxd * x)
                td_all.append(jnp.sum(dyv * x, axis=0, keepdims=True))
                dx_ref[pl.ds(r0, L), pr * LANES:(pr + 1) * LANES] = dxd * dt_p + dyv * d_p
            dgb = _bf(dg)
            dc_ref[pl.ds(r0, L), :] = dcacc + _dot(dgb, bb)
            db_ref[pl.ds(r0, L), :] = dbacc + _dot_tn(dgb, cb)
            last_add = jnp.zeros((1, LANES), F32)
            dd_add = jnp.zeros((1, LANES), F32)
            cd_row = jnp.exp(acs[L - 1:L, :])
            for j in range(SSD_HPG):
                hm = (lane1 >= HEAD_DIM) if j % 2 else (lane1 < HEAD_DIM)
                tcj = jnp.sum(jnp.where(hm, tc_all[j // 2], 0.0), axis=1, keepdims=True)
                tdj = jnp.sum(jnp.where(hm, td_all[j // 2], 0.0), axis=1, keepdims=True)
                dcdj = jnp.sum(dcd_all[j], axis=1, keepdims=True)
                last_add = last_add + jnp.where(lane1 == j, tcj + dcdj * cd_row, 0.0)
                dd_add = dd_add + jnp.where(lane1 == j, tdj, 0.0)
            dacs = (_dot(_split2(jnp.concatenate(ta_all, axis=1)), sum_pair2)
                    + _dot(_split2(jnp.concatenate(qm_all, axis=1)), sum_full2)
                    + dacs_t.T + jnp.where(rowc == L - 1, last_add, 0.0))
            ddta = _dot_hi(tri_t, dacs)
            ddt = _dot(_bf(jnp.concatenate(te_all, axis=1)), sum_pair) + ddta * a_neg
            ddraw = ddt * _sigmoid(raw)
            ddtr_ref[pl.ds(r0, L), :] = ddraw
            dbias_ref[...] += jnp.sum(ddraw, axis=0, keepdims=True)
            dalog_ref[...] += jnp.sum(ddta * dt, axis=0, keepdims=True) * a_neg
            ddsk_ref[...] += dd_add
            return 0

        lax.fori_loop(0, nc, chunk, 0)

    grp = lambda g, b: (g, 0, 0)
    small = jax.ShapeDtypeStruct((SSD_GROUPS, 1, LANES), F32)
    return pl.pallas_call(
        body, grid=(SSD_GROUPS, nb),
        in_specs=[pl.BlockSpec((s, 4 * LANES), lambda g, b: (b, g)),
                  pl.BlockSpec((s, LANES), lambda g, b: (b, 16 + g)),
                  pl.BlockSpec((s, LANES), lambda g, b: (b, 20 + g)),
                  pl.BlockSpec((None, s, LANES), lambda g, b: (g, b, 0)),
                  pl.BlockSpec((None, 1, LANES), grp), pl.BlockSpec((None, 1, LANES), grp),
                  pl.BlockSpec((None, 1, LANES), grp),
                  pl.BlockSpec((None, None, nc, N_PAIR, L, LANES), lambda g, b: (b, g, 0, 0, 0, 0)),
                  pl.BlockSpec((s, 4 * LANES), lambda g, b: (b, g))],
        out_specs=[pl.BlockSpec((s, 4 * LANES), lambda g, b: (b, g)),
                   pl.BlockSpec((s, LANES), lambda g, b: (b, g)),
                   pl.BlockSpec((s, LANES), lambda g, b: (b, g)),
                   pl.BlockSpec((None, s, LANES), lambda g, b: (g, b, 0)),
                   pl.BlockSpec((None, 1, LANES), grp), pl.BlockSpec((None, 1, LANES), grp),
                   pl.BlockSpec((None, 1, LANES), grp)],
        out_shape=[jax.ShapeDtypeStruct((t, SSD_WIDTH), F32),
                   jax.ShapeDtypeStruct((t, SSD_GROUPS * SSD_STATE), F32),
                   jax.ShapeDtypeStruct((t, SSD_GROUPS * SSD_STATE), F32),
                   jax.ShapeDtypeStruct((SSD_GROUPS, t, LANES), F32), small, small, small],
        scratch_shapes=[pltpu.VMEM((N_PAIR, L, LANES), F32)],
        compiler_params=_params(("parallel", "arbitrary")), name="ssd_bwd")(
            xbc, xbc, xbc, dtr, bias, alog, dskip, hs, dy)


GN = SSD_WIDTH // SSD_GROUPS


def _post_fwd(o, proj, y, nw, *, tm=256):
    t = o.shape[0]

    def body(o_ref, za_ref, y_ref, zs_ref, nw_ref, ya_ref, ys_ref):
        za = za_ref[...]
        ya_ref[...] = _bf(o_ref[...] * (za * _sigmoid(za)))
        zs = zs_ref[...]
        yz = y_ref[...] * (zs * _sigmoid(zs))
        for g in range(SSD_GROUPS):
            sl = slice(g * GN, (g + 1) * GN)
            v = yz[:, sl]
            r = lax.rsqrt(jnp.mean(v * v, axis=-1, keepdims=True) + EPS)
            ys_ref[:, sl] = _bf(v * r * nw_ref[:, sl])

    return pl.pallas_call(
        body, grid=(t // tm,),
        in_specs=[pl.BlockSpec((tm, SB_WIDTH), lambda i: (i, 0)),
                  pl.BlockSpec((tm, SB_WIDTH), lambda i: (i, CB_ZA * LANES // SB_WIDTH)),
                  pl.BlockSpec((tm, SSD_WIDTH), lambda i: (i, 0)),
                  pl.BlockSpec((tm, SSD_WIDTH), lambda i: (i, CB_ZS * LANES // SSD_WIDTH)),
                  pl.BlockSpec((1, SSD_WIDTH), lambda i: (0, 0))],
        out_specs=[pl.BlockSpec((tm, SB_WIDTH), lambda i: (i, 0)),
                   pl.BlockSpec((tm, SSD_WIDTH), lambda i: (i, 0))],
        out_shape=[jax.ShapeDtypeStruct((t, SB_WIDTH), BF16), jax.ShapeDtypeStruct((t, SSD_WIDTH), BF16)],
        compiler_params=_params(("parallel",)), name="post_fwd")(o, proj, y, proj, nw)


def _post_bwd(dya, o, proj, dys, y, nw, *, tm=256):
    t = o.shape[0]

    def body(dya_ref, o_ref, za_ref, dys_ref, y_ref, zs_ref, nw_ref,
             do_ref, dza_ref, dy_ref, dzs_ref, dnw_ref):
        @pl.when(pl.program_id(0) == 0)
        def _():
            dnw_ref[...] = jnp.zeros_like(dnw_ref)

        za = za_ref[...]
        sa = _sigmoid(za)
        d = dya_ref[...]
        do_ref[...] = d * (za * sa)
        dza_ref[...] = _bf(d * o_ref[...] * (sa * (1.0 + za * (1.0 - sa))))
        zs = zs_ref[...]
        ss = _sigmoid(zs)
        silu = zs * ss
        yv = y_ref[...]
        yz = yv * silu
        dv = dys_ref[...]
        for g in range(SSD_GROUPS):
            sl = slice(g * GN, (g + 1) * GN)
            v = yz[:, sl]
            r = lax.rsqrt(jnp.mean(v * v, axis=-1, keepdims=True) + EPS)
            nrm = v * r
            dg = dv[:, sl]
            dnw_ref[:, sl] += jnp.sum(dg * nrm, axis=0, keepdims=True)
            dn = dg * nw_ref[:, sl]
            dyz = r * (dn - nrm * jnp.mean(dn * nrm, axis=-1, keepdims=True))
            dy_ref[:, sl] = dyz * silu[:, sl]
            dzs_ref[:, sl] = _bf(dyz * yv[:, sl] * (ss[:, sl] * (1.0 + zs[:, sl] * (1.0 - ss[:, sl]))))

    a_spec = pl.BlockSpec((tm, SB_WIDTH), lambda i: (i, 0))
    s_spec = pl.BlockSpec((tm, SSD_WIDTH), lambda i: (i, 0))
    w_spec = pl.BlockSpec((1, SSD_WIDTH), lambda i: (0, 0))
    return pl.pallas_call(
        body, grid=(t // tm,),
        in_specs=[a_spec, a_spec, pl.BlockSpec((tm, SB_WIDTH), lambda i: (i, CB_ZA * LANES // SB_WIDTH)),
                  s_spec, s_spec, pl.BlockSpec((tm, SSD_WIDTH), lambda i: (i, CB_ZS * LANES // SSD_WIDTH)),
                  w_spec],
        out_specs=[a_spec, a_spec, s_spec, s_spec, w_spec],
        out_shape=[jax.ShapeDtypeStruct((t, SB_WIDTH), F32), jax.ShapeDtypeStruct((t, SB_WIDTH), BF16),
                   jax.ShapeDtypeStruct((t, SSD_WIDTH), F32), jax.ShapeDtypeStruct((t, SSD_WIDTH), BF16),
                   jax.ShapeDtypeStruct((1, SSD_WIDTH), F32)],
        compiler_params=_params(("arbitrary",)), name="post_bwd")(dya, o, proj, dys, y, proj, nw)


def _merge_fwd(ya, ys, proj_g, *, tm=512):
    t = ya.shape[0]

    def body(ya_ref, ys_ref, ga_ref, gs_ref, m_ref):
        m_ref[...] = _bf(_sigmoid(ga_ref[...]) * ya_ref[...] + _sigmoid(gs_ref[...]) * ys_ref[...])

    spec = pl.BlockSpec((tm, D_MODEL), lambda i: (i, 0))
    return pl.pallas_call(
        body, grid=(t // tm,),
        in_specs=[spec, spec, pl.BlockSpec((tm, D_MODEL), lambda i: (i, 0)),
                  pl.BlockSpec((tm, D_MODEL), lambda i: (i, 1))],
        out_specs=spec, out_shape=jax.ShapeDtypeStruct((t, D_MODEL), BF16),
        compiler_params=_params(("parallel",)), name="merge_fwd")(ya, ys, proj_g, proj_g)


def _merge_bwd(dm, ya, ys, proj_g, *, tm=512):
    t = ya.shape[0]

    def body(dm_ref, ya_ref, ys_ref, ga_ref, gs_ref, dya_ref, dys_ref, dg_ref):
        d = dm_ref[...]
        ga = _sigmoid(ga_ref[...])
        gs = _sigmoid(gs_ref[...])
        dya_ref[...] = _bf(d * ga)
        dys_ref[...] = _bf(d * gs)
        dg_ref[:, :D_MODEL] = _bf(d * ya_ref[...] * (ga * (1.0 - ga)))
        dg_ref[:, D_MODEL:] = _bf(d * ys_ref[...] * (gs * (1.0 - gs)))

    spec = pl.BlockSpec((tm, D_MODEL), lambda i: (i, 0))
    return pl.pallas_call(
        body, grid=(t // tm,),
        in_specs=[spec, spec, spec, pl.BlockSpec((tm, D_MODEL), lambda i: (i, 0)),
                  pl.BlockSpec((tm, D_MODEL), lambda i: (i, 1))],
        out_specs=[spec, spec, pl.BlockSpec((tm, 2 * D_MODEL), lambda i: (i, 0))],
        out_shape=[jax.ShapeDtypeStruct((t, D_MODEL), BF16), jax.ShapeDtypeStruct((t, D_MODEL), BF16),
                   jax.ShapeDtypeStruct((t, 2 * D_MODEL), BF16)],
        compiler_params=_params(("parallel",)), name="merge_bwd")(dm, ya, ys, proj_g, proj_g)


def _final(x, out, wf, target, *, tm=512):
    t = x.shape[0]

    def body(x_ref, o_ref, w_ref, t_ref, loss_ref, dx_ref, dw_ref):
        @pl.when(pl.program_id(0) == 0)
        def _():
            loss_ref[...] = jnp.zeros_like(loss_ref)
            dw_ref[...] = jnp.zeros_like(dw_ref)

        x2 = x_ref[...] + o_ref[...]
        r = lax.rsqrt(jnp.mean(x2 * x2, axis=-1, keepdims=True) + EPS)
        nrm = x2 * r
        e = nrm * w_ref[...] - t_ref[...]
        row_loss = jnp.mean(e * e, axis=-1, keepdims=True)
        loss_ref[...] += 0.5 * jnp.sum(row_loss, axis=0, keepdims=True)
        dyv = e * (1.0 / D_MODEL)
        dw_ref[...] += jnp.sum(dyv * nrm, axis=0, keepdims=True)
        dn = dyv * w_ref[...]
        dx_ref[...] = r * (dn - nrm * jnp.mean(dn * nrm, axis=-1, keepdims=True))

    spec = pl.BlockSpec((tm, D_MODEL), lambda i: (i, 0))
    fix = pl.BlockSpec((1, D_MODEL), lambda i: (0, 0))
    return pl.pallas_call(
        body, grid=(t // tm,),
        in_specs=[spec, spec, fix, spec],
        out_specs=[pl.BlockSpec((1, LANES), lambda i: (0, 0)), spec, fix],
        out_shape=[jax.ShapeDtypeStruct((1, LANES), F32), jax.ShapeDtypeStruct((t, D_MODEL), F32),
                   jax.ShapeDtypeStruct((1, D_MODEL), F32)],
        compiler_params=_params(("arbitrary",)), name="final_loss")(x, out, wf, target)


def _group_lanes(v):
    return jnp.pad(v.reshape(SSD_GROUPS, 1, SSD_HPG), ((0, 0), (0, 0), (0, LANES - SSD_HPG)))


def _ungroup_lanes(v):
    return v[:, :, :SSD_HPG].reshape(1, N_DT)


def _local_step(x, target, nb, s, w_in_t, conv_w, conv_b, norm_w, dt_bias, a_log, d_skip,
                ssm_norm_w, w_attn_out, w_ssm_out, w_o, final_norm_w):
    t = nb * s
    w_g_t = w_in_t[GATE_COL0:]
    w_dt_t = jnp.pad(w_in_t[DT_COL0:GATE_COL0], ((0, LANES - N_DT), (0, 0)))
    h, r1 = _rms_fwd(x, norm_w)
    proj = _matmul(h, w_in_t, nt=True, n=DT_COL0, name="proj_main", tm=512, tn=1024)
    proj_g = _matmul(h, w_g_t, nt=True, name="proj_gate", tm=512, tn=1024)
    dt_raw = _matmul(h, w_dt_t, nt=True, name="proj_dt")
    dtr = jnp.pad(dt_raw[:, :N_DT].reshape(t, SSD_GROUPS, SSD_HPG).transpose(1, 0, 2),
                  ((0, 0), (0, 0), (0, LANES - SSD_HPG)))
    bias_g, alog_g, dsk_g = _group_lanes(dt_bias), _group_lanes(a_log), _group_lanes(d_skip)

    o, tot = _attn_fwd(proj, nb, s)
    xbc = _conv_fwd(proj, conv_w, conv_b, nb, s)
    y, hs = _ssd_fwd(xbc, dtr, bias_g, alog_g, dsk_g, nb, s)
    ya_in, ys_in = _post_fwd(o, proj, y, ssm_norm_w)
    ya = _matmul(ya_in, w_attn_out, name="attn_out")
    ys = _matmul(ys_in, w_ssm_out, name="ssm_out", tk=2048)
    merged = _merge_fwd(ya, ys, proj_g)
    out = _matmul(merged, w_o, name="out_proj")
    loss, dx2, d_final_w = _final(x, out, final_norm_w, target)

    dx2b = _bf(dx2)
    dmerged = _matmul(dx2b, w_o, nt=True, name="d_merged")
    d_w_o = _matmul(merged.T, dx2b, name="d_w_o", tm=1024, tk=1024)
    dya, dys, dgate = _merge_bwd(dmerged, ya, ys, proj_g)
    d_w_attn_out = _matmul(ya_in.T, dya, name="d_w_attn_out", tm=1024)
    d_w_ssm_out = _matmul(ys_in.T, dys, name="d_w_ssm_out", tm=1024)
    dya_in = _matmul(dya, w_attn_out, nt=True, name="d_ya_in")
    dys_in = _matmul(dys, w_ssm_out, nt=True, name="d_ys_in")
    do, dza, dy, dzs, d_ssm_norm_w = _post_bwd(dya_in, o, proj, dys_in, y, ssm_norm_w)
    dq, dk, dv = _attn_bwd(proj, do, tot, nb, s)
    dxs_act, db_act, dc_act, ddtr, dbias_g, dalog_g, ddsk_g = _ssd_bwd(
        xbc, dtr, bias_g, alog_g, dsk_g, hs, dy, nb, s)
    dxs, dwx, dbx = _conv_bwd(proj, dxs_act, conv_w, conv_b, nb, s, col0=0, name="conv_bwd_x")
    dbm, dwb, dbb = _conv_bwd(proj, db_act, conv_w, conv_b, nb, s, col0=SSD_WIDTH, name="conv_bwd_b")
    dcm, dwc, dbc = _conv_bwd(proj, dc_act, conv_w, conv_b, nb, s, col0=SSD_WIDTH + 512, name="conv_bwd_c")
    dproj = jnp.concatenate([dq, dk, dv, dza, dzs, dxs, dbm, dcm], axis=1)
    ddt = jnp.pad(_bf(ddtr[:, :, :SSD_HPG].transpose(1, 0, 2).reshape(t, N_DT)), ((0, 0), (0, LANES - N_DT)))
    ht = h.T
    wt = dict(out_t=True, out_dtype=BF16, tm=1024, tn=512, tk=1024)
    d_w_in_t = jnp.concatenate(
        [_matmul(ht, dproj, name="d_w_main", **wt), _matmul(ht, ddt, name="d_w_dt", **wt)[:N_DT],
         _matmul(ht, dgate, name="d_w_gate", **wt)], axis=0)
    dh = _matmul(ddt, w_dt_t, name="d_h_dt")
    dh = _matmul(dgate, w_g_t, add=dh, name="d_h_gate", tm=1024, tn=1024, tk=1024)
    dh = _matmul(dproj, w_in_t, add=dh, name="d_h", tm=1024, tn=1024, tk=1024)
    grad_x, d_norm_w = _rms_bwd(dh, x, r1, norm_w, dx2)

    grads = dict(
        norm_w=d_norm_w, w_in_t=d_w_in_t,
        conv_w=jnp.concatenate([dwx, dwb, dwc], axis=1), conv_b=jnp.concatenate([dbx, dbb, dbc], axis=1),
        dt_bias=_ungroup_lanes(dbias_g), a_log=_ungroup_lanes(dalog_g), d_skip=_ungroup_lanes(ddsk_g),
        ssm_norm_w=d_ssm_norm_w, w_attn_out=d_w_attn_out, w_ssm_out=d_w_ssm_out, w_o=d_w_o,
        final_norm_w=d_final_w)
    return loss, grad_x, grads


N_CHIP = N_DEV // 2


def _pair_exchange(scatter, bcast, *, name):
    arrays = list(scatter) + list(bcast)
    ns, n = len(scatter), len(arrays)
    n_sem = ns * N_CHIP + (n - ns) * (N_DEV - 1)

    def body(*refs):
        ins, outs = refs[:n], refs[n:2 * n]
        send_sems, recv_sems, local_sems = refs[2 * n:]
        x, y, c = lax.axis_index("x"), lax.axis_index("y"), lax.axis_index("c")
        me = 4 * x + 2 * y + c
        copies, k = [], 0
        for a in range(ns):
            for q in range(N_CHIP):
                cp = pltpu.make_async_remote_copy(
                    src_ref=ins[a].at[2 * q + 1 - c], dst_ref=outs[a].at[q],
                    send_sem=send_sems.at[k], recv_sem=recv_sems.at[k],
                    device_id=(x, y, 1 - c), device_id_type=pl.DeviceIdType.MESH)
                cp.start()
                copies.append(cp)
                k += 1
        for a in range(ns, n):
            cp = pltpu.make_async_copy(ins[a], outs[a].at[me], local_sems.at[a - ns])
            cp.start()
            copies.append(cp)
            for r in range(1, N_DEV):
                tx = 1 - x if (r >> 2) & 1 else x
                ty = 1 - y if (r >> 1) & 1 else y
                tc = 1 - c if r & 1 else c
                cp = pltpu.make_async_remote_copy(
                    src_ref=ins[a], dst_ref=outs[a].at[me], send_sem=send_sems.at[k], recv_sem=recv_sems.at[k],
                    device_id=(tx, ty, tc), device_id_type=pl.DeviceIdType.MESH)
                cp.start()
                copies.append(cp)
                k += 1
        for cp in copies:
            cp.wait()

    out_shape = [jax.ShapeDtypeStruct((N_CHIP,) + v.shape[1:] if i < ns else (N_DEV,) + v.shape, v.dtype)
                 for i, v in enumerate(arrays)]
    return pl.pallas_call(
        body, in_specs=[pl.BlockSpec(memory_space=pl.ANY)] * n,
        out_specs=[pl.BlockSpec(memory_space=pl.ANY)] * n, out_shape=out_shape,
        scratch_shapes=[pltpu.SemaphoreType.DMA((n_sem,)), pltpu.SemaphoreType.DMA((n_sem,)),
                        pltpu.SemaphoreType.DMA((max(n - ns, 1),))],
        compiler_params=pltpu.CompilerParams(has_side_effects=True), name=name)(*arrays)


def _pair_add(mine, recv, core, *, name, tc=None):
    _, r, c = recv.shape
    tc = c if tc is None else tc
    assert c % tc == 0

    def body(core_ref, m_ref, r_ref, o_ref):
        o_ref[...] = (m_ref[...].astype(F32) + r_ref[...].astype(F32)).astype(o_ref.dtype)

    spec = pl.BlockSpec((None, r, tc), lambda q, i, core_ref: (q, 0, i))
    return pl.pallas_call(
        body, grid_spec=pltpu.PrefetchScalarGridSpec(
            num_scalar_prefetch=1, grid=(N_CHIP, c // tc),
            in_specs=[pl.BlockSpec((None, None, r, tc), lambda q, i, core_ref: (q, core_ref[0], 0, i)), spec],
            out_specs=spec),
        out_shape=jax.ShapeDtypeStruct(recv.shape, recv.dtype),
        compiler_params=_params(("parallel", "parallel")), name=name)(
            core, mine.reshape(N_CHIP, 2, r, c), recv)


def _chip_exchange(arrays, *, name):
    n = len(arrays)
    per = N_CHIP - 1

    def body(*refs):
        ins, outs = refs[:n], refs[n:2 * n]
        send_sems, recv_sems, local_sems = refs[2 * n:]
        x, y, c = lax.axis_index("x"), lax.axis_index("y"), lax.axis_index("c")
        chip = 2 * x + y
        copies = []
        for a in range(n):
            cp = pltpu.make_async_copy(ins[a].at[chip], outs[a].at[chip], local_sems.at[a])
            cp.start()
            copies.append(cp)
            for r in range(1, N_CHIP):
                tx = 1 - x if (r >> 1) & 1 else x
                ty = 1 - y if r & 1 else y
                cp = pltpu.make_async_remote_copy(
                    src_ref=ins[a].at[2 * tx + ty], dst_ref=outs[a].at[chip],
                    send_sem=send_sems.at[a * per + r - 1], recv_sem=recv_sems.at[a * per + r - 1],
                    device_id=(tx, ty, c), device_id_type=pl.DeviceIdType.MESH)
                cp.start()
                copies.append(cp)
        for cp in copies:
            cp.wait()

    return pl.pallas_call(
        body, in_specs=[pl.BlockSpec(memory_space=pl.ANY)] * n,
        out_specs=[pl.BlockSpec(memory_space=pl.ANY)] * n,
        out_shape=[jax.ShapeDtypeStruct(v.shape, v.dtype) for v in arrays],
        scratch_shapes=[pltpu.SemaphoreType.DMA((n * per,)), pltpu.SemaphoreType.DMA((n * per,)),
                        pltpu.SemaphoreType.DMA((n,))],
        compiler_params=pltpu.CompilerParams(has_side_effects=True), name=name)(*arrays)


def _gather(arrays, *, name):
    n = len(arrays)
    per = N_DEV - 1

    def body(*refs):
        ins, outs = refs[:n], refs[n:2 * n]
        send_sems, recv_sems, local_sems = refs[2 * n:]
        x, y, c = lax.axis_index("x"), lax.axis_index("y"), lax.axis_index("c")
        me, sibling = (x, y, c), (x, y, 1 - c)
        chips = [(1 - x, y), (x, 1 - y), (1 - x, 1 - y)]

        def copy(a, k, block, to, src=None):
            rows = outs[a].at[4 * block[0] + 2 * block[1] + block[2]]
            return pltpu.make_async_remote_copy(
                src_ref=rows if src is None else src, dst_ref=rows,
                send_sem=send_sems.at[a * per + k], recv_sem=recv_sems.at[a * per + k],
                device_id=to, device_id_type=pl.DeviceIdType.MESH)

        started = []
        for a in range(n):
            mine = pltpu.make_async_copy(ins[a], outs[a].at[4 * x + 2 * y + c], local_sems.at[a])
            mine.start()
            started.append(mine)
        for a in range(n):
            first = [copy(a, 1 + j, me, (*chip, c), src=ins[a]) for j, chip in enumerate(chips)]
            first.append(copy(a, 0, me, sibling, src=ins[a]))
            for cp in first:
                cp.start()
            started += first
        for j, chip in enumerate(chips):
            for a in range(n):
                copy(a, 1 + j, (*chip, c), me).wait_recv()
                passed = copy(a, 4 + j, (*chip, c), sibling)
                passed.start()
                started.append(passed)
        for a in range(n):
            copy(a, 0, sibling, me).wait_recv()
            for j, chip in enumerate(chips):
                copy(a, 4 + j, (*chip, 1 - c), me).wait_recv()
        for cp in started[:n]:
            cp.wait()
        for cp in started[n:]:
            cp.wait_send()

    return pl.pallas_call(
        body, in_specs=[pl.BlockSpec(memory_space=pl.ANY)] * n,
        out_specs=[pl.BlockSpec(memory_space=pl.ANY)] * n,
        out_shape=[jax.ShapeDtypeStruct((N_DEV,) + v.shape, v.dtype) for v in arrays],
        scratch_shapes=[pltpu.SemaphoreType.DMA((n * per,)), pltpu.SemaphoreType.DMA((n * per,)),
                        pltpu.SemaphoreType.DMA((n,))],
        compiler_params=pltpu.CompilerParams(has_side_effects=True), name=name)(*arrays)


def _adamw(parts, w, m, v, *, name, tc=None):
    r, c = w.shape
    n_parts = parts.shape[0]
    tc = c if tc is None else tc
    assert c % tc == 0

    def body(p_ref, w_ref, m_ref, v_ref, g_ref, d_ref, nm_ref, nv_ref):
        g = p_ref[0].astype(F32)
        for d in range(1, n_parts):
            g = g + p_ref[d].astype(F32)
        mm = ADAM_B1 * m_ref[...] + (1.0 - ADAM_B1) * g
        vv = ADAM_B2 * v_ref[...] + (1.0 - ADAM_B2) * (g * g)
        m_hat = mm / (1.0 - ADAM_B1 ** ADAM_STEP)
        v_hat = vv / (1.0 - ADAM_B2 ** ADAM_STEP)
        g_ref[...] = g
        d_ref[...] = -ADAM_LR * (m_hat / (jnp.sqrt(v_hat) + ADAM_EPS) + ADAM_WD * w_ref[...])
        nm_ref[...] = mm
        nv_ref[...] = vv

    spec = pl.BlockSpec((r, tc), lambda i: (0, i))
    out = jax.ShapeDtypeStruct((r, c), F32)
    return pl.pallas_call(
        body, grid=(c // tc,),
        in_specs=[pl.BlockSpec((n_parts, r, tc), lambda i: (0, 0, i)), spec, spec, spec],
        out_specs=[spec] * 4, out_shape=[out] * 4,
        compiler_params=_params(("parallel",)), name=name)(parts, w, m, v)


SMALL = (("norm_w", D_MODEL), ("conv_b", CONV_DIM), ("dt_bias", N_DT), ("a_log", N_DT),
         ("d_skip", N_DT), ("ssm_norm_w", SSD_WIDTH), ("final_norm_w", D_MODEL))
SMALL_ROWS = 64
SHARD = D_PROJ // N_DEV


def _pack_small(vals):
    flat = jnp.concatenate([vals[k].reshape(-1).astype(F32) for k, _ in SMALL])
    return jnp.pad(flat, (0, SMALL_ROWS * LANES - flat.shape[0])).reshape(SMALL_ROWS, LANES)


def _unpack_small(packed, shapes):
    flat = packed.reshape(-1)
    out, off = {}, 0
    for k, size in SMALL:
        out[k] = flat[off:off + size].reshape(shapes[k])
        off += size
    return out


def kernel(x, norm_w, w_in, conv_w, conv_b, dt_bias, a_log, d_skip, ssm_norm_w, w_attn_out, w_ssm_out, w_o, final_norm_w, loss_target, m_norm_w, m_w_in, m_conv_w, m_conv_b, m_dt_bias, m_a_log, m_d_skip, m_ssm_norm_w, m_w_attn_out, m_w_ssm_out, m_w_o, m_final_norm_w, v_norm_w, v_w_in, v_conv_w, v_conv_b, v_dt_bias, v_a_log, v_d_skip, v_ssm_norm_w, v_w_attn_out, v_w_ssm_out, v_w_o, v_final_norm_w):
    nb, s, _ = x.shape
    t = nb * s
    weights = dict(norm_w=norm_w, w_in=w_in, conv_w=conv_w, conv_b=conv_b, dt_bias=dt_bias, a_log=a_log,
                   d_skip=d_skip, ssm_norm_w=ssm_norm_w, w_attn_out=w_attn_out, w_ssm_out=w_ssm_out,
                   w_o=w_o, final_norm_w=final_norm_w)
    moms = dict(norm_w=m_norm_w, w_in=m_w_in, conv_w=m_conv_w, conv_b=m_conv_b, dt_bias=m_dt_bias,
                a_log=m_a_log, d_skip=m_d_skip, ssm_norm_w=m_ssm_norm_w, w_attn_out=m_w_attn_out,
                w_ssm_out=m_w_ssm_out, w_o=m_w_o, final_norm_w=m_final_norm_w)
    vels = dict(norm_w=v_norm_w, w_in=v_w_in, conv_w=v_conv_w, conv_b=v_conv_b, dt_bias=v_dt_bias,
                a_log=v_a_log, d_skip=v_d_skip, ssm_norm_w=v_ssm_norm_w, w_attn_out=v_w_attn_out,
                w_ssm_out=v_w_ssm_out, w_o=v_w_o, final_norm_w=v_final_norm_w)

    tr_ = lambda a: jnp.transpose(a[0])
    g_in, g_conv, g_wa, g_ws, g_wo = _gather(
        [_bf(tr_(w_in)), conv_w[0], _bf(w_attn_out[0]), _bf(w_ssm_out[0]), _bf(w_o[0])], name="gather_weights")
    conv_full = g_conv.transpose(1, 0, 2).reshape(CONV_K, CONV_DIM)

    loss_part, grad_x, grads = _local_step(
        x.reshape(t, D_MODEL), loss_target.reshape(t, D_MODEL), nb, s, g_in.reshape(D_PROJ, D_MODEL),
        conv_full, conv_b, norm_w, dt_bias, a_log, d_skip, ssm_norm_w, g_wa.reshape(SB_WIDTH, D_MODEL),
        g_ws.reshape(SSD_WIDTH, D_MODEL), g_wo.reshape(D_MODEL, D_MODEL), final_norm_w.reshape(1, D_MODEL))

    s_in = grads["w_in_t"].reshape(N_DEV, SHARD, D_MODEL)
    s_conv = grads["conv_w"].reshape(CONV_K, N_DEV, CONV_DIM // N_DEV).transpose(1, 0, 2)
    s_wa = _bf(grads["w_attn_out"].reshape(N_DEV, SB_WIDTH // N_DEV, D_MODEL))
    s_ws = _bf(grads["w_ssm_out"].reshape(N_DEV, SSD_WIDTH // N_DEV, D_MODEL))
    s_wo = _bf(grads["w_o"].reshape(N_DEV, D_MODEL // N_DEV, D_MODEL))
    small = _pack_small(grads)
    slabs = [s_in, s_conv, s_wa, s_ws, s_wo]
    *recv, p_small = _pair_exchange(slabs, [small], name="exchange_pairs")
    core = lax.axis_index("c").astype(jnp.int32).reshape(1)
    names = ("w_in", "conv_w", "w_attn_out", "w_ssm_out", "w_o")
    sums = [_pair_add(a, b, core, name="pair_add_" + k, tc=LANES if k == "w_in" else None)
            for k, a, b in zip(names, slabs, recv)]
    p_in, p_conv, p_wa, p_ws, p_wo = _chip_exchange(sums, name="exchange_chips")

    res = {}
    res["w_in"] = [a.T for a in _adamw(p_in, tr_(w_in), tr_(m_w_in), tr_(v_w_in), name="adamw_w_in", tc=LANES)]
    res["conv_w"] = _adamw(p_conv, conv_w[0], m_conv_w[0], v_conv_w[0], name="adamw_conv_w")
    res["w_attn_out"] = _adamw(p_wa, w_attn_out[0], m_w_attn_out[0], v_w_attn_out[0], name="adamw_w_attn_out")
    res["w_ssm_out"] = _adamw(p_ws, w_ssm_out[0], m_w_ssm_out[0], v_w_ssm_out[0], name="adamw_w_ssm_out")
    res["w_o"] = _adamw(p_wo, w_o[0], m_w_o[0], v_w_o[0], name="adamw_w_o")
    sm = _adamw(p_small, _pack_small(weights), _pack_small(moms), _pack_small(vels), name="adamw_small")
    shapes = {k: weights[k].shape for k, _ in SMALL}
    sm = [_unpack_small(v, shapes) for v in sm]
    for k, _ in SMALL:
        res[k] = tuple(part[k] for part in sm)

    order = ("norm_w", "w_in", "conv_w", "conv_b", "dt_bias", "a_log", "d_skip", "ssm_norm_w",
             "w_attn_out", "w_ssm_out", "w_o", "final_norm_w")
    loss = lax.psum(loss_part[0, 0], ("x", "y", "c"))
    outs = [loss, grad_x.reshape(nb, s, D_MODEL)]
    for i in range(4):
        outs += [res[k][i].reshape(weights[k].shape) for k in order]
    return tuple(outs)
```

```python
import functools

import jax
import jax.numpy as jnp
from jax import lax
from jax.experimental import pallas as pl
from jax.experimental.pallas import tpu as pltpu

F32 = jnp.float32
BF16 = jnp.bfloat16

D_MODEL = 1024
SB_WIDTH = 1024
HEAD_DIM = 64
SSD_WIDTH = 2048
SSD_GROUPS = 4
SSD_HPG = 8
SSD_STATE = 128
SSD_CHUNK = 128
CONV_K = 4
CONV_DIM = 3072
N_DT = 32
D_PROJ = 11296
DT_COL0 = 9216
GATE_COL0 = DT_COL0 + N_DT
EPS = 1e-6
N_DEV = 8

ADAM_LR = 0.001
ADAM_B1 = 0.9
ADAM_B2 = 0.999
ADAM_EPS = 1e-08
ADAM_WD = 0.01
ADAM_STEP = 10

LANES = 128
VMEM_LIMIT = 56 * 1024 * 1024

CB_Q, CB_K, CB_V, CB_ZA, CB_ZS, CB_XBC = 0, 8, 16, 24, 32, 48

NT_DIMS = (((1,), (1,)), ((), ()))
TN_DIMS = (((0,), (0,)), ((), ()))
HI = lax.Precision.HIGHEST


def _params(sem):
    return pltpu.CompilerParams(dimension_semantics=sem, vmem_limit_bytes=VMEM_LIMIT)


def _dot(a, b):
    return jnp.dot(a, b, preferred_element_type=F32)


def _dot_nt(a, b):
    return lax.dot_general(a, b, NT_DIMS, preferred_element_type=F32)


def _dot_tn(a, b):
    return lax.dot_general(a, b, TN_DIMS, preferred_element_type=F32)


def _dot_hi(a, b):
    return jnp.dot(a, b, preferred_element_type=F32, precision=HI)


def _bf(a):
    return a.astype(BF16)


def _sigmoid(x):
    return 1.0 / (1.0 + jnp.exp(-x))


def _softplus(x):
    return jnp.maximum(x, 0.0) + jnp.log(1.0 + jnp.exp(-jnp.abs(x)))


def _trunc_split(a):
    bits = lax.bitcast_convert_type(a, jnp.uint32) & jnp.uint32(0xFFFF0000)
    hi = lax.bitcast_convert_type(bits, F32)
    return _bf(hi), _bf(a - hi)


def _matmul(a, b, *, name, nt=False, add=None, out_dtype=F32, tm=512, tn=512, tk=1024, n=None, out_t=False):
    m, k = a.shape
    n = (b.shape[0] if nt else b.shape[1]) if n is None else n
    tm, tn, tk = min(tm, m), min(tn, n), min(tk, k)
    assert m % tm == 0 and n % tn == 0 and k % tk == 0, (name, a.shape, b.shape)
    assert not (out_t and add is not None)
    nk = k // tk
    has_add = add is not None

    def body(*refs):
        if has_add:
            a_ref, b_ref, add_ref, o_ref, acc_ref = refs
        else:
            a_ref, b_ref, o_ref, acc_ref = refs
        kk = pl.program_id(2)

        @pl.when(kk == 0)
        def _():
            acc_ref[...] = jnp.zeros_like(acc_ref)

        av, bv = _bf(a_ref[...]), _bf(b_ref[...])
        acc_ref[...] += _dot_nt(av, bv) if nt else _dot(av, bv)

        @pl.when(kk == nk - 1)
        def _():
            r = acc_ref[...]
            if has_add:
                r = r + add_ref[...]
            o_ref[...] = (r.T if out_t else r).astype(out_dtype)

    in_specs = [pl.BlockSpec((tm, tk), lambda j, i, kk: (i, kk)),
                pl.BlockSpec((tn, tk), lambda j, i, kk: (j, kk)) if nt
                else pl.BlockSpec((tk, tn), lambda j, i, kk: (kk, j))]
    args = [a, b]
    if has_add:
        in_specs.append(pl.BlockSpec((tm, tn), lambda j, i, kk: (i, j)))
        args.append(add)
    if out_t:
        out_spec, out_dims = pl.BlockSpec((tn, tm), lambda j, i, kk: (j, i)), (n, m)
    else:
        out_spec, out_dims = pl.BlockSpec((tm, tn), lambda j, i, kk: (i, j)), (m, n)
    return pl.pallas_call(
        body, grid=(n // tn, m // tm, nk), in_specs=in_specs, out_specs=out_spec,
        out_shape=jax.ShapeDtypeStruct(out_dims, out_dtype),
        scratch_shapes=[pltpu.VMEM((tm, tn), F32)],
        compiler_params=_params(("parallel", "parallel", "arbitrary")), name=name)(*args)


def _rms_fwd(x, w, *, tm=512):
    t = x.shape[0]

    def body(x_ref, w_ref, h_ref, r_ref):
        xv = x_ref[...]
        r = lax.rsqrt(jnp.mean(xv * xv, axis=-1, keepdims=True) + EPS)
        h_ref[...] = _bf(xv * r * w_ref[...])
        r_ref[...] = r

    row = lambda i: (i, 0)
    return pl.pallas_call(
        body, grid=(t // tm,),
        in_specs=[pl.BlockSpec((tm, D_MODEL), row), pl.BlockSpec((1, D_MODEL), lambda i: (0, 0))],
        out_specs=[pl.BlockSpec((tm, D_MODEL), row), pl.BlockSpec((tm, 1), row)],
        out_shape=[jax.ShapeDtypeStruct((t, D_MODEL), BF16), jax.ShapeDtypeStruct((t, 1), F32)],
        compiler_params=_params(("parallel",)), name="rms_fwd")(x, w)


def _rms_bwd(dh, x, r, w, dres, *, tm=512):
    t = x.shape[0]

    def body(dh_ref, x_ref, r_ref, w_ref, dres_ref, dx_ref, dw_ref):
        @pl.when(pl.program_id(0) == 0)
        def _():
            dw_ref[...] = jnp.zeros_like(dw_ref)

        nrm = x_ref[...] * r_ref[...]
        dhv = dh_ref[...]
        dw_ref[...] += jnp.sum(dhv * nrm, axis=0, keepdims=True)
        dn = dhv * w_ref[...]
        dx = r_ref[...] * (dn - nrm * jnp.mean(dn * nrm, axis=-1, keepdims=True))
        dx_ref[...] = dx + dres_ref[...]

    row = lambda i: (i, 0)
    fix = lambda i: (0, 0)
    return pl.pallas_call(
        body, grid=(t // tm,),
        in_specs=[pl.BlockSpec((tm, D_MODEL), row), pl.BlockSpec((tm, D_MODEL), row),
                  pl.BlockSpec((tm, 1), row), pl.BlockSpec((1, D_MODEL), fix),
                  pl.BlockSpec((tm, D_MODEL), row)],
        out_specs=[pl.BlockSpec((tm, D_MODEL), row), pl.BlockSpec((1, D_MODEL), fix)],
        out_shape=[jax.ShapeDtypeStruct((t, D_MODEL), F32), jax.ShapeDtypeStruct((1, D_MODEL), F32)],
        compiler_params=_params(("arbitrary",)), name="rms_bwd")(dh, x, r, w, dres)


ATT_BLK = 256


def _attn_masks(blk):
    r_io = lax.broadcasted_iota(jnp.int32, (blk, blk), 0)
    c_io = lax.broadcasted_iota(jnp.int32, (blk, blk), 1)
    return r_io, c_io


def _attn_fwd(proj, nb, s):
    blk = min(ATT_BLK, s)
    nq = s // blk
    scale = HEAD_DIM ** -0.5

    def body(q_ref, k_ref, v_ref, o_ref, tot_ref, q0_s, q1_s, kb_s, v0_s, v1_s, z_s, a_s):
        head0 = lax.broadcasted_iota(jnp.int32, (s, LANES), 1) < HEAD_DIM
        lane2 = lax.broadcasted_iota(jnp.int32, (blk, 2), 1)
        r_io, c_io = _attn_masks(blk)
        tri = c_io < r_io
        u_gt = _bf(r_io > c_io)
        u2 = jnp.concatenate([u_gt, u_gt], axis=0)
        qv = q_ref[...] * scale
        q0_s[...] = _bf(jnp.where(head0, qv, 0.0))
        q1_s[...] = _bf(jnp.where(head0, 0.0, qv))
        kb_s[...] = _bf(k_ref[...])
        vv = v_ref[...]
        v0_s[...] = _bf(jnp.where(head0, vv, 0.0))
        v1_s[...] = _bf(jnp.where(head0, 0.0, vv))

        def scores(qhs, kblk, slot):
            k0 = pl.multiple_of(jnp.maximum(kblk, 0) * blk, blk)
            kb = kb_s[pl.ds(k0, blk), :]
            for hh in range(2):
                z_s[slot, hh] = _dot_nt(qhs[hh], kb)

        def weights(slot, crs, diag):
            new = []
            for hh in range(2):
                z = z_s[slot, hh]
                sp = _softplus(z)
                spm = jnp.where(tri, sp, 0.0) if diag else sp
                hi, lo = _trunc_split(spm)
                ps = _dot(jnp.concatenate([hi, lo], axis=1), u2)
                a = jnp.exp(z - (sp + ps + crs[hh]))
                if diag:
                    a = jnp.where(tri, a, 0.0)
                a_s[slot, :, hh * blk:(hh + 1) * blk] = _bf(a)
                new.append(crs[hh] + ps[:, 0:1] + spm[:, 0:1])
            return new

        def apply(acc, slot, k0):
            vcat = jnp.concatenate([v0_s[pl.ds(k0, blk), :], v1_s[pl.ds(k0, blk), :]], axis=0)
            return acc + _dot(a_s[slot], vcat)

        def qblock(qi, _):
            q0 = pl.multiple_of(qi * blk, blk)
            qhs = (q0_s[pl.ds(q0, blk), :], q1_s[pl.ds(q0, blk), :])
            zc = jnp.zeros((blk, 1), F32)
            scores(qhs, qi, 0)
            scores(qhs, qi - 1, 1)
            crs = weights(0, (zc, zc), True)

            def step(j, slot, st):
                acc, c0, c1 = st
                acc = apply(acc, 1 - slot, pl.multiple_of((qi - j + 1) * blk, blk))
                scores(qhs, qi - j - 1, 1 - slot)
                crs = weights(slot, (c0, c1), False)
                return acc, crs[0], crs[1]

            st = (jnp.zeros((blk, LANES), F32), crs[0], crs[1])
            st = lax.fori_loop(0, qi // 2, lambda i, st: step(2 * i + 2, 0, step(2 * i + 1, 1, st)), st)
            st = lax.cond(qi % 2 == 1, lambda st: step(qi, 1, st), lambda st: st, st)
            o_ref[pl.ds(q0, blk), :] = apply(st[0], qi & 1, 0)
            tot_ref[pl.ds(q0, blk), :] = jnp.where(lane2 == 0, st[1], st[2])
            return 0

        lax.fori_loop(0, nq, qblock, 0)

    t = nb * s
    return pl.pallas_call(
        body, grid=(nb, 8),
        in_specs=[pl.BlockSpec((s, LANES), lambda b, p: (b, CB_Q + p)),
                  pl.BlockSpec((s, LANES), lambda b, p: (b, CB_K + p)),
                  pl.BlockSpec((s, LANES), lambda b, p: (b, CB_V + p))],
        out_specs=[pl.BlockSpec((s, LANES), lambda b, p: (b, p)),
                   pl.BlockSpec((None, None, s, 2), lambda b, p: (b, p, 0, 0))],
        out_shape=[jax.ShapeDtypeStruct((t, SB_WIDTH), F32),
                   jax.ShapeDtypeStruct((nb, 8, s, 2), F32)],
        scratch_shapes=[pltpu.VMEM((s, LANES), BF16)] * 5 + [
            pltpu.VMEM((2, 2, blk, blk), F32), pltpu.VMEM((2, blk, 2 * blk), BF16)],
        compiler_params=_params(("parallel", "parallel")), name="attn_fwd")(proj, proj, proj)


def _attn_bwd(proj, do, tot, nb, s):
    blk = min(ATT_BLK, s)
    nq = s // blk
    scale = HEAD_DIM ** -0.5

    def body(q_ref, k_ref, v_ref, do_ref, tot_ref, dq_ref, dk_ref, dv_ref,
             q0_s, q1_s, kb_s, k0_s, k1_s, vb_s, d0_s, d1_s, z_s, da_s, dz_s, a_s, dk_acc, dv_acc):
        head0 = lax.broadcasted_iota(jnp.int32, (s, LANES), 1) < HEAD_DIM
        r_io, c_io = _attn_masks(blk)
        tri = c_io < r_io
        u_le = _bf(r_io <= c_io)
        u2 = jnp.concatenate([u_le, u_le], axis=0)
        u_lt = _bf(r_io < c_io)
        qv = q_ref[...] * scale
        q0_s[...] = _bf(jnp.where(head0, qv, 0.0))
        q1_s[...] = _bf(jnp.where(head0, 0.0, qv))
        kv = k_ref[...]
        kb_s[...] = _bf(kv)
        k0_s[...] = _bf(jnp.where(head0, kv, 0.0))
        k1_s[...] = _bf(jnp.where(head0, 0.0, kv))
        vb_s[...] = _bf(v_ref[...])
        dov = do_ref[...]
        d0_s[...] = _bf(jnp.where(head0, dov, 0.0))
        d1_s[...] = _bf(jnp.where(head0, 0.0, dov))
        dk_acc[...] = jnp.zeros_like(dk_acc)
        dv_acc[...] = jnp.zeros_like(dv_acc)

        def qblock(qi, _):
            q0 = pl.multiple_of(qi * blk, blk)
            qs_ = pl.ds(q0, blk)
            qhs = (q0_s[qs_, :], q1_s[qs_, :])
            dhs = (d0_s[qs_, :], d1_s[qs_, :])
            qcat = jnp.concatenate(qhs, axis=0)
            dcat = jnp.concatenate(dhs, axis=0)
            totf = tot_ref[qs_, :]

            def scores(kblk, slot):
                ks_ = pl.ds(pl.multiple_of(kblk * blk, blk), blk)
                kb, vb = kb_s[ks_, :], vb_s[ks_, :]
                for hh in range(2):
                    z_s[slot, hh] = _dot_nt(qhs[hh], kb)
                    da_s[slot, hh] = _dot_nt(dhs[hh], vb)

            def grads(slot, st, diag):
                new = []
                for hh in range(2):
                    cn, cw = st[2 * hh], st[2 * hh + 1]
                    z = z_s[slot, hh]
                    sp = _softplus(z)
                    lb = z - sp
                    spm = jnp.where(tri, sp, 0.0) if diag else sp
                    hi, lo = _trunc_split(spm)
                    pin = _dot(jnp.concatenate([hi, lo], axis=1), u2)
                    a = jnp.exp(lb + (pin + cn))
                    if diag:
                        a = jnp.where(tri, a, 0.0)
                    w = a * da_s[slot, hh]
                    wex = _dot(_bf(w), u_lt)
                    dz = w - jnp.exp(lb) * (w + (wex + cw))
                    if diag:
                        dz = jnp.where(tri, dz, 0.0)
                    dz_s[slot, hh] = _bf(dz)
                    a_s[slot, hh] = _bf(a)
                    new += [cn + pin[:, blk - 1:blk], cw + wex[:, blk - 1:blk] + w[:, blk - 1:blk]]
                return tuple(new)

            def apply(dq, slot, kblk):
                ks_ = pl.ds(pl.multiple_of(kblk * blk, blk), blk)
                dzs = (dz_s[slot, 0], dz_s[slot, 1])
                dq = dq + _dot(jnp.concatenate(dzs, axis=1), jnp.concatenate([k0_s[ks_, :], k1_s[ks_, :]], axis=0))
                dk_acc[ks_, :] += _dot_tn(jnp.concatenate(dzs, axis=0), qcat)
                dv_acc[ks_, :] += _dot_tn(jnp.concatenate([a_s[slot, 0], a_s[slot, 1]], axis=0), dcat)
                return dq

            def step(j, slot, st, diag):
                dq = apply(st[0], 1 - slot, jnp.maximum(j - 1, 0))
                if not diag:
                    scores(j + 1, 1 - slot)
                return (dq, *grads(slot, st[1:], diag))

            dz_s[1] = jnp.zeros_like(dz_s[1])
            a_s[1] = jnp.zeros_like(a_s[1])
            scores(0, 0)
            zc = jnp.zeros((blk, 1), F32)
            st = (jnp.zeros((blk, LANES), F32), -totf[:, 0:1], zc, -totf[:, 1:2], zc)
            st = lax.fori_loop(
                0, qi // 2, lambda i, st: step(2 * i + 1, 1, step(2 * i, 0, st, False), False), st)
            st = lax.cond(qi % 2 == 1,
                          lambda st: step(qi, 1, step(qi - 1, 0, st, False), True),
                          lambda st: step(qi, 0, st, True), st)
            dq_ref[qs_, :] = _bf(apply(st[0], qi & 1, qi) * scale)
            return 0

        lax.fori_loop(0, nq, qblock, 0)
        dk_ref[...] = _bf(dk_acc[...])
        dv_ref[...] = _bf(dv_acc[...])

    t = nb * s
    hp = lambda b, p: (b, p)
    out = jax.ShapeDtypeStruct((t, SB_WIDTH), BF16)
    return pl.pallas_call(
        body, grid=(nb, 8),
        in_specs=[pl.BlockSpec((s, LANES), lambda b, p: (b, CB_Q + p)),
                  pl.BlockSpec((s, LANES), lambda b, p: (b, CB_K + p)),
                  pl.BlockSpec((s, LANES), lambda b, p: (b, CB_V + p)),
                  pl.BlockSpec((s, LANES), hp),
                  pl.BlockSpec((None, None, s, 2), lambda b, p: (b, p, 0, 0))],
        out_specs=[pl.BlockSpec((s, LANES), hp)] * 3,
        out_shape=[out, out, out],
        scratch_shapes=[pltpu.VMEM((s, LANES), BF16)] * 8 + [
            pltpu.VMEM((2, 2, blk, blk), F32), pltpu.VMEM((2, 2, blk, blk), F32),
            pltpu.VMEM((2, 2, blk, blk), BF16), pltpu.VMEM((2, 2, blk, blk), BF16),
            pltpu.VMEM((s, LANES), F32), pltpu.VMEM((s, LANES), F32)],
        compiler_params=_params(("parallel", "parallel")), name="attn_bwd")(proj, proj, proj, do, tot)


CONV_TC = 256


def _shift_down(x, k, row):
    return jnp.where(row >= k, pltpu.roll(x, k, 0), 0.0)


def _shift_up(x, k, row, s):
    return jnp.where(row < s - k, pltpu.roll(x, s - k, 0), 0.0)


def _conv_fwd(proj, conv_w, conv_b, nb, s):
    tc = CONV_TC
    cb0 = CB_XBC * LANES // tc

    def body(x_ref, w_ref, b_ref, o_ref):
        x = x_ref[...]
        row = lax.broadcasted_iota(jnp.int32, x.shape, 0)
        y = b_ref[...] + w_ref[CONV_K - 1:CONV_K, :] * x
        for kk in range(1, CONV_K):
            y = y + w_ref[CONV_K - 1 - kk:CONV_K - kk, :] * _shift_down(x, kk, row)
        o_ref[...] = y * _sigmoid(y)

    t = nb * s
    return pl.pallas_call(
        body, grid=(nb, CONV_DIM // tc),
        in_specs=[pl.BlockSpec((s, tc), lambda b, c: (b, cb0 + c)),
                  pl.BlockSpec((CONV_K, tc), lambda b, c: (0, c)),
                  pl.BlockSpec((1, tc), lambda b, c: (0, c))],
        out_specs=pl.BlockSpec((s, tc), lambda b, c: (b, c)),
        out_shape=jax.ShapeDtypeStruct((t, CONV_DIM), F32),
        compiler_params=_params(("parallel", "parallel")), name="conv_fwd")(proj, conv_w, conv_b)


def _conv_bwd(proj, dact, conv_w, conv_b, nb, s, *, col0, name):
    width = dact.shape[1]
    tc = min(CONV_TC, width)
    cb0 = (CB_XBC * LANES + col0) // tc
    wb0 = col0 // tc

    def body(x_ref, d_ref, w_ref, b_ref, dx_ref, dw_ref, db_ref):
        @pl.when(pl.program_id(1) == 0)
        def _():
            dw_ref[...] = jnp.zeros_like(dw_ref)
            db_ref[...] = jnp.zeros_like(db_ref)

        x = x_ref[...]
        row = lax.broadcasted_iota(jnp.int32, x.shape, 0)
        xs = [x] + [_shift_down(x, kk, row) for kk in range(1, CONV_K)]
        y = b_ref[...] + w_ref[CONV_K - 1:CONV_K, :] * x
        for kk in range(1, CONV_K):
            y = y + w_ref[CONV_K - 1 - kk:CONV_K - kk, :] * xs[kk]
        sg = _sigmoid(y)
        dy = d_ref[...] * (sg * (1.0 + y * (1.0 - sg)))
        dx = w_ref[CONV_K - 1:CONV_K, :] * dy
        for kk in range(1, CONV_K):
            dx = dx + w_ref[CONV_K - 1 - kk:CONV_K - kk, :] * _shift_up(dy, kk, row, s)
        dx_ref[...] = _bf(dx)
        db_ref[...] += jnp.sum(dy, axis=0, keepdims=True)
        for kk in range(CONV_K):
            dw_ref[CONV_K - 1 - kk:CONV_K - kk, :] += jnp.sum(dy * xs[kk], axis=0, keepdims=True)

    t = nb * s
    return pl.pallas_call(
        body, grid=(width // tc, nb),
        in_specs=[pl.BlockSpec((s, tc), lambda c, b: (b, cb0 + c)),
                  pl.BlockSpec((s, tc), lambda c, b: (b, c)),
                  pl.BlockSpec((CONV_K, tc), lambda c, b: (0, wb0 + c)),
                  pl.BlockSpec((1, tc), lambda c, b: (0, wb0 + c))],
        out_specs=[pl.BlockSpec((s, tc), lambda c, b: (b, c)),
                   pl.BlockSpec((CONV_K, tc), lambda c, b: (0, c)),
                   pl.BlockSpec((1, tc), lambda c, b: (0, c))],
        out_shape=[jax.ShapeDtypeStruct((t, width), BF16),
                   jax.ShapeDtypeStruct((CONV_K, width), F32),
                   jax.ShapeDtypeStruct((1, width), F32)],
        compiler_params=_params(("parallel", "arbitrary")), name=name)(proj, dact, conv_w, conv_b)


L = SSD_CHUNK
N_PAIR = SSD_HPG // 2


def _pair_lanes(v0, v1, lane):
    return jnp.where(lane < HEAD_DIM, v0, v1)


def _onehot(shape, row_shift, col_shift):
    row = lax.broadcasted_iota(jnp.int32, shape, 0)
    col = lax.broadcasted_iota(jnp.int32, shape, 1)
    return _bf(lax.shift_right_logical(row, row_shift) == lax.shift_right_logical(col, col_shift))


def _ssd_tables():
    sp = _onehot((LANES, N_PAIR * LANES), 0, 6)
    sf = _onehot((LANES, SSD_HPG * LANES), 0, 7)
    return (jnp.concatenate([sp] * 3, axis=0), jnp.concatenate([sf] * 3, axis=0),
            _onehot((N_PAIR * LANES, LANES), 6, 0), _onehot((SSD_HPG * LANES, LANES), 7, 0))


def _split3(a):
    a1 = _bf(a)
    r1 = a - a1.astype(F32)
    a2 = _bf(r1)
    return jnp.concatenate([a1, a2, _bf(r1 - a2.astype(F32))], axis=1)


def _split2(a):
    hi, lo = _trunc_split(a)
    return jnp.concatenate([hi, lo], axis=1)


def _ssd_chunk_common(dtr_ref, bias_ref, alog_ref, r0, spread_pair, spread_full):
    rr = lax.broadcasted_iota(jnp.int32, (L, L), 0)
    cc = lax.broadcasted_iota(jnp.int32, (L, L), 1)
    tri = (cc <= rr).astype(F32)
    raw = dtr_ref[pl.ds(r0, L), :] + bias_ref[...]
    dt = _softplus(raw)
    a_neg = -jnp.exp(alog_ref[...])
    acs = _dot_hi(tri, dt * a_neg)
    acs3 = _split3(acs)
    return (raw, dt, a_neg, acs, acs.T, rr, cc,
            _dot(_split3(dt), spread_pair), _dot(acs3, spread_pair), _dot(acs3, spread_full))


def _ssd_fwd(xbc, dtr, bias, alog, dskip, nb, s):
    nc = s // L
    t = nb * s

    def body(x_ref, b_ref, c_ref, dtr_ref, bias_ref, alog_ref, dsk_ref, y_ref, hs_ref, h_scr):
        lane = lax.broadcasted_iota(jnp.int32, (L, LANES), 1)
        lane1 = lax.broadcasted_iota(jnp.int32, (1, LANES), 1)
        rowc = lax.broadcasted_iota(jnp.int32, (L, 1), 0)
        h_scr[...] = jnp.zeros_like(h_scr)
        dsk = dsk_ref[...]
        spread_pair, spread_full, _, _ = _ssd_tables()

        def chunk(c, _):
            r0 = pl.multiple_of(c * L, L)
            raw, dt, a_neg, acs, acs_t, rr, cc, dt_all, acs_all, acs_full = _ssd_chunk_common(
                dtr_ref, bias_ref, alog_ref, r0, spread_pair, spread_full)
            causal = rr >= cc
            bb = _bf(b_ref[pl.ds(r0, L), :])
            cb = _bf(c_ref[pl.ds(r0, L), :])
            gm = _dot_nt(cb, bb)
            for pr in range(N_PAIR):
                j0, j1 = 2 * pr, 2 * pr + 1
                ms = []
                for j in (j0, j1):
                    seg = acs_full[:, j * LANES:(j + 1) * LANES] - acs_t[j:j + 1, :]
                    ms.append(_bf(gm * jnp.exp(jnp.where(causal, seg, -1e30))))
                x = x_ref[pl.ds(r0, L), pr * LANES:(pr + 1) * LANES]
                dt_p = dt_all[:, pr * LANES:(pr + 1) * LANES]
                acs_p = acs_all[:, pr * LANES:(pr + 1) * LANES]
                last_p = acs_p[L - 1:L, :]
                d_p = _pair_lanes(dsk[:, j0:j0 + 1], dsk[:, j1:j1 + 1], lane1)
                xd = x * dt_p
                xdb = _bf(xd)
                yd = _pair_lanes(_dot(ms[0], xdb), _dot(ms[1], xdb), lane)
                hp = h_scr[pr]
                yo = _dot_nt(cb, _bf(hp)) * jnp.exp(acs_p)
                y_ref[pl.ds(r0, L), pr * LANES:(pr + 1) * LANES] = yd + yo + x * d_p
                hs_ref[c, pr] = hp
                sp = _dot_tn(_bf(xd * jnp.exp(last_p - acs_p)), bb)
                cd = jnp.where(rowc < HEAD_DIM, jnp.exp(acs[L - 1:L, j0:j0 + 1]),
                               jnp.exp(acs[L - 1:L, j1:j1 + 1]))
                h_scr[pr] = hp * cd + sp
            return 0

        lax.fori_loop(0, nc, chunk, 0)

    grp = lambda b, g: (g, 0, 0)
    return pl.pallas_call(
        body, grid=(nb, SSD_GROUPS),
        in_specs=[pl.BlockSpec((s, 4 * LANES), lambda b, g: (b, g)),
                  pl.BlockSpec((s, LANES), lambda b, g: (b, 16 + g)),
                  pl.BlockSpec((s, LANES), lambda b, g: (b, 20 + g)),
                  pl.BlockSpec((None, s, LANES), lambda b, g: (g, b, 0)),
                  pl.BlockSpec((None, 1, LANES), grp), pl.BlockSpec((None, 1, LANES), grp),
                  pl.BlockSpec((None, 1, LANES), grp)],
        out_specs=[pl.BlockSpec((s, 4 * LANES), lambda b, g: (b, g)),
                   pl.BlockSpec((None, None, nc, N_PAIR, L, LANES), lambda b, g: (b, g, 0, 0, 0, 0))],
        out_shape=[jax.ShapeDtypeStruct((t, SSD_WIDTH), F32),
                   jax.ShapeDtypeStruct((nb, SSD_GROUPS, nc, N_PAIR, L, LANES), F32)],
        scratch_shapes=[pltpu.VMEM((N_PAIR, L, LANES), F32)],
        compiler_params=_params(("parallel", "parallel")), name="ssd_fwd")(
            xbc, xbc, xbc, dtr, bias, alog, dskip)


def _ssd_bwd(xbc, dtr, bias, alog, dskip, hs, dy, nb, s):
    nc = s // L
    t = nb * s

    def body(x_ref, b_ref, c_ref, dtr_ref, bias_ref, alog_ref, dsk_ref, hs_ref, dy_ref,
             dx_ref, db_ref, dc_ref, ddtr_ref, dbias_ref, dalog_ref, ddsk_ref, dh_scr):
        lane = lax.broadcasted_iota(jnp.int32, (L, LANES), 1)
        lane1 = lax.broadcasted_iota(jnp.int32, (1, LANES), 1)
        rowc = lax.broadcasted_iota(jnp.int32, (L, 1), 0)
        dh_scr[...] = jnp.zeros_like(dh_scr)
        dsk = dsk_ref[...]

        @pl.when(pl.program_id(1) == 0)
        def _():
            dbias_ref[...] = jnp.zeros_like(dbias_ref)
            dalog_ref[...] = jnp.zeros_like(dalog_ref)
            ddsk_ref[...] = jnp.zeros_like(ddsk_ref)

        spread_pair, spread_full, sum_pair, sum_full = _ssd_tables()
        sum_pair2 = jnp.concatenate([sum_pair] * 2, axis=0)
        sum_full2 = jnp.concatenate([sum_full] * 2, axis=0)

        def chunk(i, _):
            c = nc - 1 - i
            r0 = pl.multiple_of(c * L, L)
            raw, dt, a_neg, acs, acs_t, rr, cc, dt_all, acs_all, acs_full = _ssd_chunk_common(
                dtr_ref, bias_ref, alog_ref, r0, spread_pair, spread_full)
            causal = rr >= cc
            tri_t = (cc >= rr).astype(F32)
            bb = _bf(b_ref[pl.ds(r0, L), :])
            cb = _bf(c_ref[pl.ds(r0, L), :])
            gm = _dot_nt(cb, bb)
            dg = jnp.zeros((L, L), F32)
            dbacc = jnp.zeros((L, LANES), F32)
            dcacc = jnp.zeros((L, LANES), F32)
            dacs_t = jnp.zeros((L, L), F32)
            ta_all, te_all, qm_all, tc_all, td_all, dcd_all = [], [], [], [], [], []
            for pr in range(N_PAIR):
                js = (2 * pr, 2 * pr + 1)
                lms, mfs = [], []
                for j in js:
                    seg = acs_full[:, j * LANES:(j + 1) * LANES] - acs_t[j:j + 1, :]
                    lm = jnp.exp(jnp.where(causal, seg, -1e30))
                    lms.append(lm)
                    mfs.append(gm * lm)
                x = x_ref[pl.ds(r0, L), pr * LANES:(pr + 1) * LANES]
                dyv = dy_ref[pl.ds(r0, L), pr * LANES:(pr + 1) * LANES]
                dt_p = dt_all[:, pr * LANES:(pr + 1) * LANES]
                acs_p = acs_all[:, pr * LANES:(pr + 1) * LANES]
                last_p = acs_p[L - 1:L, :]
                d_p = _pair_lanes(dsk[:, js[0]:js[0] + 1], dsk[:, js[1]:js[1] + 1], lane1)
                e_p = jnp.exp(acs_p)
                dte_p = jnp.exp(last_p - acs_p)
                xd = x * dt_p
                xdb = _bf(xd)
                hp = hs_ref[c, pr]
                hb = _bf(hp)
                dye = _bf(dyv * e_p)
                yo = _dot_nt(cb, hb) * e_p
                dcacc = dcacc + _dot(dye, hb)
                dh_out = _dot_tn(dye, cb)
                dyb = _bf(dyv)
                dxd = jnp.zeros((L, LANES), F32)
                for hh in range(2):
                    hm = (lane >= HEAD_DIM) if hh else (lane < HEAD_DIM)
                    dm = _dot_nt(_bf(jnp.where(hm, dyv, 0.0)), xdb)
                    dxd = dxd + jnp.where(hm, _dot_tn(_bf(mfs[hh]), dyb), 0.0)
                    dg = dg + dm * lms[hh]
                    qm = dm * mfs[hh]
                    qm_all.append(qm)
                    dacs_t = dacs_t - jnp.where(rr == js[hh], jnp.sum(qm, axis=0, keepdims=True), 0.0)
                ds = dh_scr[pr]
                dsb = _bf(ds)
                xdd = xd * dte_p
                dxdd = _dot_nt(bb, dsb)
                dbacc = dbacc + _dot(_bf(xdd), dsb)
                dxd = dxd + dxdd * dte_p
                cd = jnp.where(rowc < HEAD_DIM, jnp.exp(acs[L - 1:L, js[0]:js[0] + 1]),
                               jnp.exp(acs[L - 1:L, js[1]:js[1] + 1]))
                prod = ds * hp
                dcd_all += [jnp.sum(prod[:HEAD_DIM], axis=0, keepdims=True),
                            jnp.sum(prod[HEAD_DIM:], axis=0, keepdims=True)]
                dh_scr[pr] = dh_out + ds * cd
                tcv = dxdd * xdd
                ta_all.append(dyv * yo - tcv)
                tc_all.append(jnp.sum(tcv, axis=0, keepdims=True))
                te_all.append(dxd * x)
                td_all.append(jnp.sum(dyv * x, axis=0, keepdims=True))
                dx_ref[pl.ds(r0, L), pr * LANES:(pr + 1) * LANES] = dxd * dt_p + dyv * d_p
            dgb = _bf(dg)
            dc_ref[pl.ds(r0, L), :] = dcacc + _dot(dgb, bb)
            db_ref[pl.ds(r0, L), :] = dbacc + _dot_tn(dgb, cb)
            last_add = jnp.zeros((1, LANES), F32)
            dd_add = jnp.zeros((1, LANES), F32)
            cd_row = jnp.exp(acs[L - 1:L, :])
            for j in range(SSD_HPG):
                hm = (lane1 >= HEAD_DIM) if j % 2 else (lane1 < HEAD_DIM)
                tcj = jnp.sum(jnp.where(hm, tc_all[j // 2], 0.0), axis=1, keepdims=True)
                tdj = jnp.sum(jnp.where(hm, td_all[j // 2], 0.0), axis=1, keepdims=True)
                dcdj = jnp.sum(dcd_all[j], axis=1, keepdims=True)
                last_add = last_add + jnp.where(lane1 == j, tcj + dcdj * cd_row, 0.0)
                dd_add = dd_add + jnp.where(lane1 == j, tdj, 0.0)
            dacs = (_dot(_split2(jnp.concatenate(ta_all, axis=1)), sum_pair2)
                    + _dot(_split2(jnp.concatenate(qm_all, axis=1)), sum_full2)
                    + dacs_t.T + jnp.where(rowc == L - 1, last_add, 0.0))
            ddta = _dot_hi(tri_t, dacs)
            ddt = _dot(_bf(jnp.concatenate(te_all, axis=1)), sum_pair) + ddta * a_neg
            ddraw = ddt * _sigmoid(raw)
            ddtr_ref[pl.ds(r0, L), :] = ddraw
            dbias_ref[...] += jnp.sum(ddraw, axis=0, keepdims=True)
            dalog_ref[...] += jnp.sum(ddta * dt, axis=0, keepdims=True) * a_neg
            ddsk_ref[...] += dd_add
            return 0

        lax.fori_loop(0, nc, chunk, 0)

    grp = lambda g, b: (g, 0, 0)
    small = jax.ShapeDtypeStruct((SSD_GROUPS, 1, LANES), F32)
    return pl.pallas_call(
        body, grid=(SSD_GROUPS, nb),
        in_specs=[pl.BlockSpec((s, 4 * LANES), lambda g, b: (b, g)),
                  pl.BlockSpec((s, LANES), lambda g, b: (b, 16 + g)),
                  pl.BlockSpec((s, LANES), lambda g, b: (b, 20 + g)),
                  pl.BlockSpec((None, s, LANES), lambda g, b: (g, b, 0)),
                  pl.BlockSpec((None, 1, LANES), grp), pl.BlockSpec((None, 1, LANES), grp),
                  pl.BlockSpec((None, 1, LANES), grp),
                  pl.BlockSpec((None, None, nc, N_PAIR, L, LANES), lambda g, b: (b, g, 0, 0, 0, 0)),
                  pl.BlockSpec((s, 4 * LANES), lambda g, b: (b, g))],
        out_specs=[pl.BlockSpec((s, 4 * LANES), lambda g, b: (b, g)),
                   pl.BlockSpec((s, LANES), lambda g, b: (b, g)),
                   pl.BlockSpec((s, LANES), lambda g, b: (b, g)),
                   pl.BlockSpec((None, s, LANES), lambda g, b: (g, b, 0)),
                   pl.BlockSpec((None, 1, LANES), grp), pl.BlockSpec((None, 1, LANES), grp),
                   pl.BlockSpec((None, 1, LANES), grp)],
        out_shape=[jax.ShapeDtypeStruct((t, SSD_WIDTH), F32),
                   jax.ShapeDtypeStruct((t, SSD_GROUPS * SSD_STATE), F32),
                   jax.ShapeDtypeStruct((t, SSD_GROUPS * SSD_STATE), F32),
                   jax.ShapeDtypeStruct((SSD_GROUPS, t, LANES), F32), small, small, small],
        scratch_shapes=[pltpu.VMEM((N_PAIR, L, LANES), F32)],
        compiler_params=_params(("parallel", "arbitrary")), name="ssd_bwd")(
            xbc, xbc, xbc, dtr, bias, alog, dskip, hs, dy)


GN = SSD_WIDTH // SSD_GROUPS


def _post_fwd(o, proj, y, nw, *, tm=256):
    t = o.shape[0]

    def body(o_ref, za_ref, y_ref, zs_ref, nw_ref, ya_ref, ys_ref):
        za = za_ref[...]
        ya_ref[...] = _bf(o_ref[...] * (za * _sigmoid(za)))
        zs = zs_ref[...]
        yz = y_ref[...] * (zs * _sigmoid(zs))
        for g in range(SSD_GROUPS):
            sl = slice(g * GN, (g + 1) * GN)
            v = yz[:, sl]
            r = lax.rsqrt(jnp.mean(v * v, axis=-1, keepdims=True) + EPS)
            ys_ref[:, sl] = _bf(v * r * nw_ref[:, sl])

    return pl.pallas_call(
        body, grid=(t // tm,),
        in_specs=[pl.BlockSpec((tm, SB_WIDTH), lambda i: (i, 0)),
                  pl.BlockSpec((tm, SB_WIDTH), lambda i: (i, CB_ZA * LANES // SB_WIDTH)),
                  pl.BlockSpec((tm, SSD_WIDTH), lambda i: (i, 0)),
                  pl.BlockSpec((tm, SSD_WIDTH), lambda i: (i, CB_ZS * LANES // SSD_WIDTH)),
                  pl.BlockSpec((1, SSD_WIDTH), lambda i: (0, 0))],
        out_specs=[pl.BlockSpec((tm, SB_WIDTH), lambda i: (i, 0)),
                   pl.BlockSpec((tm, SSD_WIDTH), lambda i: (i, 0))],
        out_shape=[jax.ShapeDtypeStruct((t, SB_WIDTH), BF16), jax.ShapeDtypeStruct((t, SSD_WIDTH), BF16)],
        compiler_params=_params(("parallel",)), name="post_fwd")(o, proj, y, proj, nw)


def _post_bwd(dya, o, proj, dys, y, nw, *, tm=256):
    t = o.shape[0]

    def body(dya_ref, o_ref, za_ref, dys_ref, y_ref, zs_ref, nw_ref,
             do_ref, dza_ref, dy_ref, dzs_ref, dnw_ref):
        @pl.when(pl.program_id(0) == 0)
        def _():
            dnw_ref[...] = jnp.zeros_like(dnw_ref)

        za = za_ref[...]
        sa = _sigmoid(za)
        d = dya_ref[...]
        do_ref[...] = d * (za * sa)
        dza_ref[...] = _bf(d * o_ref[...] * (sa * (1.0 + za * (1.0 - sa))))
        zs = zs_ref[...]
        ss = _sigmoid(zs)
        silu = zs * ss
        yv = y_ref[...]
        yz = yv * silu
        dv = dys_ref[...]
        for g in range(SSD_GROUPS):
            sl = slice(g * GN, (g + 1) * GN)
            v = yz[:, sl]
            r = lax.rsqrt(jnp.mean(v * v, axis=-1, keepdims=True) + EPS)
            nrm = v * r
            dg = dv[:, sl]
            dnw_ref[:, sl] += jnp.sum(dg * nrm, axis=0, keepdims=True)
            dn = dg * nw_ref[:, sl]
            dyz = r * (dn - nrm * jnp.mean(dn * nrm, axis=-1, keepdims=True))
            dy_ref[:, sl] = dyz * silu[:, sl]
            dzs_ref[:, sl] = _bf(dyz * yv[:, sl] * (ss[:, sl] * (1.0 + zs[:, sl] * (1.0 - ss[:, sl]))))

    a_spec = pl.BlockSpec((tm, SB_WIDTH), lambda i: (i, 0))
    s_spec = pl.BlockSpec((tm, SSD_WIDTH), lambda i: (i, 0))
    w_spec = pl.BlockSpec((1, SSD_WIDTH), lambda i: (0, 0))
    return pl.pallas_call(
        body, grid=(t // tm,),
        in_specs=[a_spec, a_spec, pl.BlockSpec((tm, SB_WIDTH), lambda i: (i, CB_ZA * LANES // SB_WIDTH)),
                  s_spec, s_spec, pl.BlockSpec((tm, SSD_WIDTH), lambda i: (i, CB_ZS * LANES // SSD_WIDTH)),
                  w_spec],
        out_specs=[a_spec, a_spec, s_spec, s_spec, w_spec],
        out_shape=[jax.ShapeDtypeStruct((t, SB_WIDTH), F32), jax.ShapeDtypeStruct((t, SB_WIDTH), BF16),
                   jax.ShapeDtypeStruct((t, SSD_WIDTH), F32), jax.ShapeDtypeStruct((t, SSD_WIDTH), BF16),
                   jax.ShapeDtypeStruct((1, SSD_WIDTH), F32)],
        compiler_params=_params(("arbitrary",)), name="post_bwd")(dya, o, proj, dys, y, proj, nw)


def _merge_fwd(ya, ys, proj_g, *, tm=512):
    t = ya.shape[0]

    def body(ya_ref, ys_ref, ga_ref, gs_ref, m_ref):
        m_ref[...] = _bf(_sigmoid(ga_ref[...]) * ya_ref[...] + _sigmoid(gs_ref[...]) * ys_ref[...])

    spec = pl.BlockSpec((tm, D_MODEL), lambda i: (i, 0))
    return pl.pallas_call(
        body, grid=(t // tm,),
        in_specs=[spec, spec, pl.BlockSpec((tm, D_MODEL), lambda i: (i, 0)),
                  pl.BlockSpec((tm, D_MODEL), lambda i: (i, 1))],
        out_specs=spec, out_shape=jax.ShapeDtypeStruct((t, D_MODEL), BF16),
        compiler_params=_params(("parallel",)), name="merge_fwd")(ya, ys, proj_g, proj_g)


def _merge_bwd(dm, ya, ys, proj_g, *, tm=512):
    t = ya.shape[0]

    def body(dm_ref, ya_ref, ys_ref, ga_ref, gs_ref, dya_ref, dys_ref, dg_ref):
        d = dm_ref[...]
        ga = _sigmoid(ga_ref[...])
        gs = _sigmoid(gs_ref[...])
        dya_ref[...] = _bf(d * ga)
        dys_ref[...] = _bf(d * gs)
        dg_ref[:, :D_MODEL] = _bf(d * ya_ref[...] * (ga * (1.0 - ga)))
        dg_ref[:, D_MODEL:] = _bf(d * ys_ref[...] * (gs * (1.0 - gs)))

    spec = pl.BlockSpec((tm, D_MODEL), lambda i: (i, 0))
    return pl.pallas_call(
        body, grid=(t // tm,),
        in_specs=[spec, spec, spec, pl.BlockSpec((tm, D_MODEL), lambda i: (i, 0)),
                  pl.BlockSpec((tm, D_MODEL), lambda i: (i, 1))],
        out_specs=[spec, spec, pl.BlockSpec((tm, 2 * D_MODEL), lambda i: (i, 0))],
        out_shape=[jax.ShapeDtypeStruct((t, D_MODEL), BF16), jax.ShapeDtypeStruct((t, D_MODEL), BF16),
                   jax.ShapeDtypeStruct((t, 2 * D_MODEL), BF16)],
        compiler_params=_params(("parallel",)), name="merge_bwd")(dm, ya, ys, proj_g, proj_g)


def _final(x, out, wf, target, *, tm=512):
    t = x.shape[0]

    def body(x_ref, o_ref, w_ref, t_ref, loss_ref, dx_ref, dw_ref):
        @pl.when(pl.program_id(0) == 0)
        def _():
            loss_ref[...] = jnp.zeros_like(loss_ref)
            dw_ref[...] = jnp.zeros_like(dw_ref)

        x2 = x_ref[...] + o_ref[...]
        r = lax.rsqrt(jnp.mean(x2 * x2, axis=-1, keepdims=True) + EPS)
        nrm = x2 * r
        e = nrm * w_ref[...] - t_ref[...]
        row_loss = jnp.mean(e * e, axis=-1, keepdims=True)
        loss_ref[...] += 0.5 * jnp.sum(row_loss, axis=0, keepdims=True)
        dyv = e * (1.0 / D_MODEL)
        dw_ref[...] += jnp.sum(dyv * nrm, axis=0, keepdims=True)
        dn = dyv * w_ref[...]
        dx_ref[...] = r * (dn - nrm * jnp.mean(dn * nrm, axis=-1, keepdims=True))

    spec = pl.BlockSpec((tm, D_MODEL), lambda i: (i, 0))
    fix = pl.BlockSpec((1, D_MODEL), lambda i: (0, 0))
    return pl.pallas_call(
        body, grid=(t // tm,),
        in_specs=[spec, spec, fix, spec],
        out_specs=[pl.BlockSpec((1, LANES), lambda i: (0, 0)), spec, fix],
        out_shape=[jax.ShapeDtypeStruct((1, LANES), F32), jax.ShapeDtypeStruct((t, D_MODEL), F32),
                   jax.ShapeDtypeStruct((1, D_MODEL), F32)],
        compiler_params=_params(("arbitrary",)), name="final_loss")(x, out, wf, target)


def _group_lanes(v):
    return jnp.pad(v.reshape(SSD_GROUPS, 1, SSD_HPG), ((0, 0), (0, 0), (0, LANES - SSD_HPG)))


def _ungroup_lanes(v):
    return v[:, :, :SSD_HPG].reshape(1, N_DT)


def _local_step(x, target, nb, s, w_in_t, conv_w, conv_b, norm_w, dt_bias, a_log, d_skip,
                ssm_norm_w, w_attn_out, w_ssm_out, w_o, final_norm_w, between=None):
    t = nb * s
    w_g_t = w_in_t[GATE_COL0:]
    w_dt_t = jnp.pad(w_in_t[DT_COL0:GATE_COL0], ((0, LANES - N_DT), (0, 0)))
    h, r1 = _rms_fwd(x, norm_w)
    proj = _matmul(h, w_in_t, nt=True, n=DT_COL0, name="proj_main", tm=512, tn=1024)
    proj_g = _matmul(h, w_g_t, nt=True, name="proj_gate", tm=512, tn=1024)
    dt_raw = _matmul(h, w_dt_t, nt=True, name="proj_dt")
    dtr = jnp.pad(dt_raw[:, :N_DT].reshape(t, SSD_GROUPS, SSD_HPG).transpose(1, 0, 2),
                  ((0, 0), (0, 0), (0, LANES - SSD_HPG)))
    bias_g, alog_g, dsk_g = _group_lanes(dt_bias), _group_lanes(a_log), _group_lanes(d_skip)

    o, tot = _attn_fwd(proj, nb, s)
    xbc = _conv_fwd(proj, conv_w, conv_b, nb, s)
    y, hs = _ssd_fwd(xbc, dtr, bias_g, alog_g, dsk_g, nb, s)
    ya_in, ys_in = _post_fwd(o, proj, y, ssm_norm_w)
    ya = _matmul(ya_in, w_attn_out, name="attn_out")
    ys = _matmul(ys_in, w_ssm_out, name="ssm_out", tk=2048)
    merged = _merge_fwd(ya, ys, proj_g)
    out = _matmul(merged, w_o, name="out_proj")
    loss, dx2, d_final_w = _final(x, out, final_norm_w, target)

    dx2b = _bf(dx2)
    dmerged = _matmul(dx2b, w_o, nt=True, name="d_merged")
    d_w_o = _matmul(merged.T, dx2b, name="d_w_o", tm=1024, tk=1024)
    dya, dys, dgate = _merge_bwd(dmerged, ya, ys, proj_g)
    d_w_attn_out = _matmul(ya_in.T, dya, name="d_w_attn_out", tm=1024)
    d_w_ssm_out = _matmul(ys_in.T, dys, name="d_w_ssm_out", tm=1024)
    dya_in = _matmul(dya, w_attn_out, nt=True, name="d_ya_in")
    dys_in = _matmul(dys, w_ssm_out, nt=True, name="d_ys_in")
    do, dza, dy, dzs, d_ssm_norm_w = _post_bwd(dya_in, o, proj, dys_in, y, ssm_norm_w)
    dq, dk, dv = _attn_bwd(proj, do, tot, nb, s)
    dxs_act, db_act, dc_act, ddtr, dbias_g, dalog_g, ddsk_g = _ssd_bwd(
        xbc, dtr, bias_g, alog_g, dsk_g, hs, dy, nb, s)
    dxs, dwx, dbx = _conv_bwd(proj, dxs_act, conv_w, conv_b, nb, s, col0=0, name="conv_bwd_x")
    dbm, dwb, dbb = _conv_bwd(proj, db_act, conv_w, conv_b, nb, s, col0=SSD_WIDTH, name="conv_bwd_b")
    dcm, dwc, dbc = _conv_bwd(proj, dc_act, conv_w, conv_b, nb, s, col0=SSD_WIDTH + 512, name="conv_bwd_c")
    dproj = jnp.concatenate([dq, dk, dv, dza, dzs, dxs, dbm, dcm], axis=1)
    ddt = jnp.pad(_bf(ddtr[:, :, :SSD_HPG].transpose(1, 0, 2).reshape(t, N_DT)), ((0, 0), (0, LANES - N_DT)))
    ht = h.T
    wt = dict(out_t=True, out_dtype=BF16, tm=1024, tn=512, tk=1024)
    d_w_in_t = jnp.concatenate(
        [_matmul(ht, dproj, name="d_w_main", **wt), _matmul(ht, ddt, name="d_w_dt", **wt)[:N_DT],
         _matmul(ht, dgate, name="d_w_gate", **wt)], axis=0)
    grads = dict(
        w_in_t=d_w_in_t,
        conv_w=jnp.concatenate([dwx, dwb, dwc], axis=1), conv_b=jnp.concatenate([dbx, dbb, dbc], axis=1),
        dt_bias=_ungroup_lanes(dbias_g), a_log=_ungroup_lanes(dalog_g), d_skip=_ungroup_lanes(ddsk_g),
        ssm_norm_w=d_ssm_norm_w, w_attn_out=d_w_attn_out, w_ssm_out=d_w_ssm_out, w_o=d_w_o,
        final_norm_w=d_final_w)
    extra = None
    if between is not None:
        zero, extra = between(grads)
        ddt = ddt + _bf(zero[0:1, 0:1])

    dh = _matmul(ddt, w_dt_t, name="d_h_dt")
    dh = _matmul(dgate, w_g_t, add=dh, name="d_h_gate", tm=1024, tn=1024, tk=1024)
    dh = _matmul(dproj, w_in_t, add=dh, name="d_h", tm=1024, tn=1024, tk=1024)
    grad_x, grads["norm_w"] = _rms_bwd(dh, x, r1, norm_w, dx2)
    return loss, grad_x, grads, extra


N_CHIP = N_DEV // 2


def _pair_exchange(scatter, bcast, *, name):
    arrays = list(scatter) + list(bcast)
    ns, n = len(scatter), len(arrays)
    n_sem = ns * N_CHIP + (n - ns) * (N_DEV - 1)

    def body(*refs):
        ins, outs = refs[:n], refs[n:2 * n]
        send_sems, recv_sems, local_sems = refs[2 * n:]
        x, y, c = lax.axis_index("x"), lax.axis_index("y"), lax.axis_index("c")
        me = 4 * x + 2 * y + c
        copies, k = [], 0
        for a in range(ns):
            for q in range(N_CHIP):
                cp = pltpu.make_async_remote_copy(
                    src_ref=ins[a].at[2 * q + 1 - c], dst_ref=outs[a].at[q],
                    send_sem=send_sems.at[k], recv_sem=recv_sems.at[k],
                    device_id=(x, y, 1 - c), device_id_type=pl.DeviceIdType.MESH)
                cp.start()
                copies.append(cp)
                k += 1
        for a in range(ns, n):
            cp = pltpu.make_async_copy(ins[a], outs[a].at[me], local_sems.at[a - ns])
            cp.start()
            copies.append(cp)
            for r in range(1, N_DEV):
                tx = 1 - x if (r >> 2) & 1 else x
                ty = 1 - y if (r >> 1) & 1 else y
                tc = 1 - c if r & 1 else c
                cp = pltpu.make_async_remote_copy(
                    src_ref=ins[a], dst_ref=outs[a].at[me], send_sem=send_sems.at[k], recv_sem=recv_sems.at[k],
                    device_id=(tx, ty, tc), device_id_type=pl.DeviceIdType.MESH)
                cp.start()
                copies.append(cp)
                k += 1
        for cp in copies:
            cp.wait()

    out_shape = [jax.ShapeDtypeStruct((N_CHIP,) + v.shape[1:] if i < ns else (N_DEV,) + v.shape, v.dtype)
                 for i, v in enumerate(arrays)]
    return pl.pallas_call(
        body, in_specs=[pl.BlockSpec(memory_space=pl.ANY)] * n,
        out_specs=[pl.BlockSpec(memory_space=pl.ANY)] * n, out_shape=out_shape,
        scratch_shapes=[pltpu.SemaphoreType.DMA((n_sem,)), pltpu.SemaphoreType.DMA((n_sem,)),
                        pltpu.SemaphoreType.DMA((max(n - ns, 1),))],
        compiler_params=pltpu.CompilerParams(has_side_effects=True), name=name)(*arrays)


def _pair_add(mine, recv, core, *, name, tc=None):
    _, r, c = recv.shape
    tc = c if tc is None else tc
    assert c % tc == 0

    def body(core_ref, m_ref, r_ref, o_ref):
        o_ref[...] = (m_ref[...].astype(F32) + r_ref[...].astype(F32)).astype(o_ref.dtype)

    spec = pl.BlockSpec((None, r, tc), lambda q, i, core_ref: (q, 0, i))
    return pl.pallas_call(
        body, grid_spec=pltpu.PrefetchScalarGridSpec(
            num_scalar_prefetch=1, grid=(N_CHIP, c // tc),
            in_specs=[pl.BlockSpec((None, r, tc), lambda q, i, core_ref: (2 * q + core_ref[0], 0, i)), spec],
            out_specs=spec),
        out_shape=jax.ShapeDtypeStruct(recv.shape, recv.dtype),
        compiler_params=_params(("parallel", "parallel")), name=name)(core, mine, recv)


_HBM = pl.BlockSpec(memory_space=pltpu.HBM)
_SEM = pl.BlockSpec(memory_space=pltpu.SEMAPHORE)
_DATAFLOW = pltpu.SideEffectType.DATAFLOW_SIDE_EFFECTING


def _chip_copies(srcs, lands, send_sems, recv_sems):
    x, y, c = lax.axis_index("x"), lax.axis_index("y"), lax.axis_index("c")
    per = N_CHIP - 1
    copies = []
    for a in range(len(srcs)):
        for r in range(1, N_CHIP):
            tx = 1 - x if (r >> 1) & 1 else x
            ty = 1 - y if r & 1 else y
            copies.append(pltpu.make_async_remote_copy(
                src_ref=srcs[a].at[2 * tx + ty], dst_ref=lands[a].at[2 * x + y],
                send_sem=send_sems.at[a * per + r - 1], recv_sem=recv_sems.at[a * per + r - 1],
                device_id=(tx, ty, c), device_id_type=pl.DeviceIdType.MESH))
    return copies


def _chip_exchange_start(arrays, *, name):
    n = len(arrays)
    n_sem = n * (N_CHIP - 1)

    def body(*refs):
        srcs, lands = refs[:n], refs[n:2 * n]
        send_sems, recv_sems = refs[2 * n], refs[2 * n + 1]
        zero = refs[-1]
        for cp in _chip_copies(srcs, lands, send_sems, recv_sems):
            cp.start()
        zero[...] = jnp.zeros_like(zero)

    hbm = [pltpu.HBM(v.shape, v.dtype) for v in arrays]
    outs = pl.pallas_call(
        body, name=name,
        out_shape=(pltpu.SemaphoreType.DMA((n_sem,)), pltpu.SemaphoreType.DMA((n_sem,)), *hbm, *hbm,
                   jax.ShapeDtypeStruct((8, LANES), F32)),
        in_specs=[_HBM] * (2 * n),
        out_specs=(_SEM, _SEM, *[_HBM] * (2 * n), pl.BlockSpec(memory_space=pltpu.VMEM)),
        input_output_aliases={i: 2 + i for i in range(2 * n)},
        compiler_params=pltpu.CompilerParams(has_side_effects=_DATAFLOW))(
            *[pltpu.with_memory_space_constraint(v, pltpu.HBM) for v in arrays],
            *[pltpu.with_memory_space_constraint(lax.empty(v.shape, v.dtype), pltpu.HBM) for v in arrays])
    return outs[0], outs[1], outs[2:2 + n], outs[2 + n:2 + 2 * n], outs[-1]


def _chip_exchange_wait(send_sems, recv_sems, srcs, lands, after, *, name):
    n = len(srcs)

    def body(*refs):
        for cp in _chip_copies(refs[:n], refs[n:2 * n], refs[2 * n], refs[2 * n + 1]):
            cp.wait_send()
            cp.wait_recv()

    hbm = [pltpu.HBM(v.shape, v.dtype) for v in srcs]
    outs = pl.pallas_call(
        body, name=name, out_shape=(*hbm, *hbm),
        in_specs=[_HBM] * (2 * n) + [_SEM, _SEM, pl.BlockSpec(memory_space=pl.ANY)],
        out_specs=[_HBM] * (2 * n), input_output_aliases={i: i for i in range(2 * n)},
        compiler_params=pltpu.CompilerParams(has_side_effects=_DATAFLOW))(
            *srcs, *lands, send_sems, recv_sems, after)
    return outs[:n], outs[n:]


def _gather(arrays, *, name):
    n = len(arrays)
    per = N_DEV - 1

    def body(*refs):
        ins, outs = refs[:n], refs[n:2 * n]
        send_sems, recv_sems, local_sems = refs[2 * n:]
        x, y, c = lax.axis_index("x"), lax.axis_index("y"), lax.axis_index("c")
        me, sibling = (x, y, c), (x, y, 1 - c)
        chips = [(1 - x, y), (x, 1 - y), (1 - x, 1 - y)]

        def copy(a, k, block, to, src=None):
            rows = outs[a].at[4 * block[0] + 2 * block[1] + block[2]]
            return pltpu.make_async_remote_copy(
                src_ref=rows if src is None else src, dst_ref=rows,
                send_sem=send_sems.at[a * per + k], recv_sem=recv_sems.at[a * per + k],
                device_id=to, device_id_type=pl.DeviceIdType.MESH)

        started = []
        for a in range(n):
            mine = pltpu.make_async_copy(ins[a], outs[a].at[4 * x + 2 * y + c], local_sems.at[a])
            mine.start()
            started.append(mine)
        for a in range(n):
            first = [copy(a, 1 + j, me, (*chip, c), src=ins[a]) for j, chip in enumerate(chips)]
            first.append(copy(a, 0, me, sibling, src=ins[a]))
            for cp in first:
                cp.start()
            started += first
        for j, chip in enumerate(chips):
            for a in range(n):
                copy(a, 1 + j, (*chip, c), me).wait_recv()
                passed = copy(a, 4 + j, (*chip, c), sibling)
                passed.start()
                started.append(passed)
        for a in range(n):
            copy(a, 0, sibling, me).wait_recv()
            for j, chip in enumerate(chips):
                copy(a, 4 + j, (*chip, 1 - c), me).wait_recv()
        for cp in started[:n]:
            cp.wait()
        for cp in started[n:]:
            cp.wait_send()

    return pl.pallas_call(
        body, in_specs=[pl.BlockSpec(memory_space=pl.ANY)] * n,
        out_specs=[pl.BlockSpec(memory_space=pl.ANY)] * n,
        out_shape=[jax.ShapeDtypeStruct((N_DEV,) + v.shape, v.dtype) for v in arrays],
        scratch_shapes=[pltpu.SemaphoreType.DMA((n * per,)), pltpu.SemaphoreType.DMA((n * per,)),
                        pltpu.SemaphoreType.DMA((n,))],
        compiler_params=pltpu.CompilerParams(has_side_effects=True), name=name)(*arrays)


def _adamw(parts, w, m, v, *, name, tc=None):
    r, c = w.shape
    n_parts = parts.shape[0]
    tc = c if tc is None else tc
    assert c % tc == 0

    def body(p_ref, w_ref, m_ref, v_ref, g_ref, d_ref, nm_ref, nv_ref):
        g = p_ref[0].astype(F32)
        for d in range(1, n_parts):
            g = g + p_ref[d].astype(F32)
        mm = ADAM_B1 * m_ref[...] + (1.0 - ADAM_B1) * g
        vv = ADAM_B2 * v_ref[...] + (1.0 - ADAM_B2) * (g * g)
        m_hat = mm / (1.0 - ADAM_B1 ** ADAM_STEP)
        v_hat = vv / (1.0 - ADAM_B2 ** ADAM_STEP)
        g_ref[...] = g
        d_ref[...] = -ADAM_LR * (m_hat / (jnp.sqrt(v_hat) + ADAM_EPS) + ADAM_WD * w_ref[...])
        nm_ref[...] = mm
        nv_ref[...] = vv

    spec = pl.BlockSpec((r, tc), lambda i: (0, i))
    out = jax.ShapeDtypeStruct((r, c), F32)
    return pl.pallas_call(
        body, grid=(c // tc,),
        in_specs=[pl.BlockSpec((n_parts, r, tc), lambda i: (0, 0, i)), spec, spec, spec],
        out_specs=[spec] * 4, out_shape=[out] * 4,
        compiler_params=_params(("parallel",)), name=name)(parts, w, m, v)


SMALL = (("norm_w", D_MODEL), ("conv_b", CONV_DIM), ("dt_bias", N_DT), ("a_log", N_DT),
         ("d_skip", N_DT), ("ssm_norm_w", SSD_WIDTH), ("final_norm_w", D_MODEL))
SMALL_ROWS = 64
SHARD = D_PROJ // N_DEV


def _pack_small(vals):
    flat = jnp.concatenate([vals[k].reshape(-1).astype(F32) for k, _ in SMALL])
    return jnp.pad(flat, (0, SMALL_ROWS * LANES - flat.shape[0])).reshape(SMALL_ROWS, LANES)


def _unpack_small(packed, shapes):
    flat = packed.reshape(-1)
    out, off = {}, 0
    for k, size in SMALL:
        out[k] = flat[off:off + size].reshape(shapes[k])
        off += size
    return out


def kernel(x, norm_w, w_in, conv_w, conv_b, dt_bias, a_log, d_skip, ssm_norm_w, w_attn_out, w_ssm_out, w_o, final_norm_w, loss_target, m_norm_w, m_w_in, m_conv_w, m_conv_b, m_dt_bias, m_a_log, m_d_skip, m_ssm_norm_w, m_w_attn_out, m_w_ssm_out, m_w_o, m_final_norm_w, v_norm_w, v_w_in, v_conv_w, v_conv_b, v_dt_bias, v_a_log, v_d_skip, v_ssm_norm_w, v_w_attn_out, v_w_ssm_out, v_w_o, v_final_norm_w):
    nb, s, _ = x.shape
    t = nb * s
    weights = dict(norm_w=norm_w, w_in=w_in, conv_w=conv_w, conv_b=conv_b, dt_bias=dt_bias, a_log=a_log,
                   d_skip=d_skip, ssm_norm_w=ssm_norm_w, w_attn_out=w_attn_out, w_ssm_out=w_ssm_out,
                   w_o=w_o, final_norm_w=final_norm_w)
    moms = dict(norm_w=m_norm_w, w_in=m_w_in, conv_w=m_conv_w, conv_b=m_conv_b, dt_bias=m_dt_bias,
                a_log=m_a_log, d_skip=m_d_skip, ssm_norm_w=m_ssm_norm_w, w_attn_out=m_w_attn_out,
                w_ssm_out=m_w_ssm_out, w_o=m_w_o, final_norm_w=m_final_norm_w)
    vels = dict(norm_w=v_norm_w, w_in=v_w_in, conv_w=v_conv_w, conv_b=v_conv_b, dt_bias=v_dt_bias,
                a_log=v_a_log, d_skip=v_d_skip, ssm_norm_w=v_ssm_norm_w, w_attn_out=v_w_attn_out,
                w_ssm_out=v_w_ssm_out, w_o=v_w_o, final_norm_w=v_final_norm_w)

    tr_ = lambda a: jnp.transpose(a[0])
    g_in, g_conv, g_wa, g_ws, g_wo = _gather(
        [_bf(tr_(w_in)), conv_w[0], _bf(w_attn_out[0]), _bf(w_ssm_out[0]), _bf(w_o[0])], name="gather_weights")
    conv_full = g_conv.transpose(1, 0, 2).reshape(CONV_K, CONV_DIM)

    core = lax.axis_index("c").astype(jnp.int32).reshape(1)
    chip = 2 * lax.axis_index("x") + lax.axis_index("y")

    def send_weight_grads(grads):
        slabs = [grads["w_in_t"].reshape(N_DEV, SHARD, D_MODEL),
                 grads["conv_w"].reshape(CONV_K, N_DEV, CONV_DIM // N_DEV).transpose(1, 0, 2),
                 _bf(grads["w_attn_out"].reshape(N_DEV, SB_WIDTH // N_DEV, D_MODEL)),
                 _bf(grads["w_ssm_out"].reshape(N_DEV, SSD_WIDTH // N_DEV, D_MODEL)),
                 _bf(grads["w_o"].reshape(N_DEV, D_MODEL // N_DEV, D_MODEL))]
        recv = _pair_exchange(slabs, [], name="exchange_pairs")
        names = ("w_in", "conv_w", "w_attn_out", "w_ssm_out", "w_o")
        sums = [_pair_add(a, b, core, name="pair_add_" + k, tc=LANES if k == "w_in" else None)
                for k, a, b in zip(names, slabs, recv)]
        send_sems, recv_sems, srcs, lands, zero = _chip_exchange_start(sums, name="exchange_chips_start")
        return zero, (send_sems, recv_sems, srcs, lands)

    loss_part, grad_x, grads, (send_sems, recv_sems, srcs, lands) = _local_step(
        x.reshape(t, D_MODEL), loss_target.reshape(t, D_MODEL), nb, s, g_in.reshape(D_PROJ, D_MODEL),
        conv_full, conv_b, norm_w, dt_bias, a_log, d_skip, ssm_norm_w, g_wa.reshape(SB_WIDTH, D_MODEL),
        g_ws.reshape(SSD_WIDTH, D_MODEL), g_wo.reshape(D_MODEL, D_MODEL), final_norm_w.reshape(1, D_MODEL),
        between=send_weight_grads)

    (p_small,) = _pair_exchange([], [_pack_small(grads)], name="exchange_small")
    srcs, lands = _chip_exchange_wait(send_sems, recv_sems, srcs, lands, grad_x, name="exchange_chips_wait")
    p_in, p_conv, p_wa, p_ws, p_wo = [
        lax.dynamic_update_slice(land, lax.dynamic_slice(own, (chip, 0, 0), (1,) + own.shape[1:]), (chip, 0, 0))
        for land, own in zip(lands, srcs)]

    res = {}
    res["w_in"] = [a.T for a in _adamw(p_in, tr_(w_in), tr_(m_w_in), tr_(v_w_in), name="adamw_w_in", tc=LANES)]
    res["conv_w"] = _adamw(p_conv, conv_w[0], m_conv_w[0], v_conv_w[0], name="adamw_conv_w")
    res["w_attn_out"] = _adamw(p_wa, w_attn_out[0], m_w_attn_out[0], v_w_attn_out[0], name="adamw_w_attn_out")
    res["w_ssm_out"] = _adamw(p_ws, w_ssm_out[0], m_w_ssm_out[0], v_w_ssm_out[0], name="adamw_w_ssm_out")
    res["w_o"] = _adamw(p_wo, w_o[0], m_w_o[0], v_w_o[0], name="adamw_w_o")
    sm = _adamw(p_small, _pack_small(weights), _pack_small(moms), _pack_small(vels), name="adamw_small")
    shapes = {k: weights[k].shape for k, _ in SMALL}
    sm = [_unpack_small(v, shapes) for v in sm]
    for k, _ in SMALL:
        res[k] = tuple(part[k] for part in sm)

    order = ("norm_w", "w_in", "conv_w", "conv_b", "dt_bias", "a_log", "d_skip", "ssm_norm_w",
             "w_attn_out", "w_ssm_out", "w_o", "final_norm_w")
    loss = lax.psum(loss_part[0, 0], ("x", "y", "c"))
    outs = [loss, grad_x.reshape(nb, s, D_MODEL)]
    for i in range(4):
        outs += [res[k][i].reshape(weights[k].shape) for k in order]
    return tuple(outs)
```

```python
import functools

import jax
import jax.numpy as jnp
from jax import lax
from jax.experimental import pallas as pl
from jax.experimental.pallas import tpu as pltpu

F32 = jnp.float32
BF16 = jnp.bfloat16

D_MODEL = 1024
SB_WIDTH = 1024
HEAD_DIM = 64
SSD_WIDTH = 2048
SSD_GROUPS = 4
SSD_HPG = 8
SSD_STATE = 128
SSD_CHUNK = 128
CONV_K = 4
CONV_DIM = 3072
N_DT = 32
D_PROJ = 11296
DT_COL0 = 9216
GATE_COL0 = DT_COL0 + N_DT
EPS = 1e-6
N_DEV = 8

ADAM_LR = 0.001
ADAM_B1 = 0.9
ADAM_B2 = 0.999
ADAM_EPS = 1e-08
ADAM_WD = 0.01
ADAM_STEP = 10

LANES = 128
VMEM_LIMIT = 56 * 1024 * 1024

CB_Q, CB_K, CB_V, CB_ZA, CB_ZS, CB_XBC = 0, 8, 16, 24, 32, 48

NT_DIMS = (((1,), (1,)), ((), ()))
TN_DIMS = (((0,), (0,)), ((), ()))
HI = lax.Precision.HIGHEST


def _params(sem):
    return pltpu.CompilerParams(dimension_semantics=sem, vmem_limit_bytes=VMEM_LIMIT)


def _dot(a, b):
    return jnp.dot(a, b, preferred_element_type=F32)


def _dot_nt(a, b):
    return lax.dot_general(a, b, NT_DIMS, preferred_element_type=F32)


def _dot_tn(a, b):
    return lax.dot_general(a, b, TN_DIMS, preferred_element_type=F32)


def _dot_hi(a, b):
    return jnp.dot(a, b, preferred_element_type=F32, precision=HI)


def _bf(a):
    return a.astype(BF16)


def _sigmoid(x):
    return 1.0 / (1.0 + jnp.exp(-x))


def _softplus(x):
    return jnp.maximum(x, 0.0) + jnp.log(1.0 + jnp.exp(-jnp.abs(x)))


def _trunc_split(a):
    bits = lax.bitcast_convert_type(a, jnp.uint32) & jnp.uint32(0xFFFF0000)
    hi = lax.bitcast_convert_type(bits, F32)
    return _bf(hi), _bf(a - hi)


def _matmul(a, b, *, name, nt=False, ta=False, add=None, out_dtype=F32, tm=512, tn=512, tk=1024, n=None):
    k, m = a.shape if ta else a.shape[::-1]
    n = (b.shape[0] if nt else b.shape[1]) if n is None else n
    tm, tn, tk = min(tm, m), min(tn, n), min(tk, k)
    assert m % tm == 0 and n % tn == 0 and k % tk == 0, (name, a.shape, b.shape)
    assert not (ta and nt)
    nk = k // tk
    has_add = add is not None

    def body(*refs):
        if has_add:
            a_ref, b_ref, add_ref, o_ref, acc_ref = refs
        else:
            a_ref, b_ref, o_ref, acc_ref = refs
        kk = pl.program_id(2)

        @pl.when(kk == 0)
        def _():
            acc_ref[...] = jnp.zeros_like(acc_ref)

        av, bv = _bf(a_ref[...]), _bf(b_ref[...])
        acc_ref[...] += _dot_nt(av, bv) if nt else _dot_tn(av, bv) if ta else _dot(av, bv)

        @pl.when(kk == nk - 1)
        def _():
            r = acc_ref[...]
            if has_add:
                r = r + add_ref[...]
            o_ref[...] = r.astype(out_dtype)

    in_specs = [pl.BlockSpec((tk, tm), lambda j, i, kk: (kk, i)) if ta
                else pl.BlockSpec((tm, tk), lambda j, i, kk: (i, kk)),
                pl.BlockSpec((tn, tk), lambda j, i, kk: (j, kk)) if nt
                else pl.BlockSpec((tk, tn), lambda j, i, kk: (kk, j))]
    args = [a, b]
    if has_add:
        in_specs.append(pl.BlockSpec((tm, tn), lambda j, i, kk: (i, j)))
        args.append(add)
    return pl.pallas_call(
        body, grid=(n // tn, m // tm, nk), in_specs=in_specs,
        out_specs=pl.BlockSpec((tm, tn), lambda j, i, kk: (i, j)),
        out_shape=jax.ShapeDtypeStruct((m, n), out_dtype),
        scratch_shapes=[pltpu.VMEM((tm, tn), F32)],
        compiler_params=_params(("parallel", "parallel", "arbitrary")), name=name)(*args)


def _rms_fwd(x, w, *, tm=512):
    t = x.shape[0]

    def body(x_ref, w_ref, h_ref, r_ref):
        xv = x_ref[...]
        r = lax.rsqrt(jnp.mean(xv * xv, axis=-1, keepdims=True) + EPS)
        h_ref[...] = _bf(xv * r * w_ref[...])
        r_ref[...] = r

    row = lambda i: (i, 0)
    return pl.pallas_call(
        body, grid=(t // tm,),
        in_specs=[pl.BlockSpec((tm, D_MODEL), row), pl.BlockSpec((1, D_MODEL), lambda i: (0, 0))],
        out_specs=[pl.BlockSpec((tm, D_MODEL), row), pl.BlockSpec((tm, 1), row)],
        out_shape=[jax.ShapeDtypeStruct((t, D_MODEL), BF16), jax.ShapeDtypeStruct((t, 1), F32)],
        compiler_params=_params(("parallel",)), name="rms_fwd")(x, w)


def _rms_bwd(dh, x, r, w, dres, *, tm=512):
    t = x.shape[0]

    def body(dh_ref, x_ref, r_ref, w_ref, dres_ref, dx_ref, dw_ref):
        @pl.when(pl.program_id(0) == 0)
        def _():
            dw_ref[...] = jnp.zeros_like(dw_ref)

        nrm = x_ref[...] * r_ref[...]
        dhv = dh_ref[...]
        dw_ref[...] += jnp.sum(dhv * nrm, axis=0, keepdims=True)
        dn = dhv * w_ref[...]
        dx = r_ref[...] * (dn - nrm * jnp.mean(dn * nrm, axis=-1, keepdims=True))
        dx_ref[...] = dx + dres_ref[...]

    row = lambda i: (i, 0)
    fix = lambda i: (0, 0)
    return pl.pallas_call(
        body, grid=(t // tm,),
        in_specs=[pl.BlockSpec((tm, D_MODEL), row), pl.BlockSpec((tm, D_MODEL), row),
                  pl.BlockSpec((tm, 1), row), pl.BlockSpec((1, D_MODEL), fix),
                  pl.BlockSpec((tm, D_MODEL), row)],
        out_specs=[pl.BlockSpec((tm, D_MODEL), row), pl.BlockSpec((1, D_MODEL), fix)],
        out_shape=[jax.ShapeDtypeStruct((t, D_MODEL), F32), jax.ShapeDtypeStruct((1, D_MODEL), F32)],
        compiler_params=_params(("arbitrary",)), name="rms_bwd")(dh, x, r, w, dres)


ATT_BLK = 256


def _attn_masks(blk):
    r_io = lax.broadcasted_iota(jnp.int32, (blk, blk), 0)
    c_io = lax.broadcasted_iota(jnp.int32, (blk, blk), 1)
    return r_io, c_io


def _attn_fwd(proj, nb, s):
    blk = min(ATT_BLK, s)
    nq = s // blk
    scale = HEAD_DIM ** -0.5

    def body(q_ref, k_ref, v_ref, o_ref, tot_ref, q0_s, q1_s, kb_s, v0_s, v1_s, z_s, a_s):
        head0 = lax.broadcasted_iota(jnp.int32, (s, LANES), 1) < HEAD_DIM
        lane2 = lax.broadcasted_iota(jnp.int32, (blk, 2), 1)
        r_io, c_io = _attn_masks(blk)
        tri = c_io < r_io
        u_gt = _bf(r_io > c_io)
        u2 = jnp.concatenate([u_gt, u_gt], axis=0)
        qv = q_ref[...] * scale
        q0_s[...] = _bf(jnp.where(head0, qv, 0.0))
        q1_s[...] = _bf(jnp.where(head0, 0.0, qv))
        kb_s[...] = _bf(k_ref[...])
        vv = v_ref[...]
        v0_s[...] = _bf(jnp.where(head0, vv, 0.0))
        v1_s[...] = _bf(jnp.where(head0, 0.0, vv))

        def scores(qhs, kblk, slot):
            k0 = pl.multiple_of(jnp.maximum(kblk, 0) * blk, blk)
            kb = kb_s[pl.ds(k0, blk), :]
            for hh in range(2):
                z_s[slot, hh] = _dot_nt(qhs[hh], kb)

        def weights(slot, crs, diag):
            new = []
            for hh in range(2):
                z = z_s[slot, hh]
                sp = _softplus(z)
                spm = jnp.where(tri, sp, 0.0) if diag else sp
                hi, lo = _trunc_split(spm)
                ps = _dot(jnp.concatenate([hi, lo], axis=1), u2)
                a = jnp.exp(z - (sp + ps + crs[hh]))
                if diag:
                    a = jnp.where(tri, a, 0.0)
                a_s[slot, :, hh * blk:(hh + 1) * blk] = _bf(a)
                new.append(crs[hh] + ps[:, 0:1] + spm[:, 0:1])
            return new

        def apply(acc, slot, k0):
            vcat = jnp.concatenate([v0_s[pl.ds(k0, blk), :], v1_s[pl.ds(k0, blk), :]], axis=0)
            return acc + _dot(a_s[slot], vcat)

        def qblock(qi, _):
            q0 = pl.multiple_of(qi * blk, blk)
            qhs = (q0_s[pl.ds(q0, blk), :], q1_s[pl.ds(q0, blk), :])
            zc = jnp.zeros((blk, 1), F32)
            scores(qhs, qi, 0)
            scores(qhs, qi - 1, 1)
            crs = weights(0, (zc, zc), True)

            def step(j, slot, st):
                acc, c0, c1 = st
                acc = apply(acc, 1 - slot, pl.multiple_of((qi - j + 1) * blk, blk))
                scores(qhs, qi - j - 1, 1 - slot)
                crs = weights(slot, (c0, c1), False)
                return acc, crs[0], crs[1]

            st = (jnp.zeros((blk, LANES), F32), crs[0], crs[1])
            st = lax.fori_loop(0, qi // 2, lambda i, st: step(2 * i + 2, 0, step(2 * i + 1, 1, st)), st)
            st = lax.cond(qi % 2 == 1, lambda st: step(qi, 1, st), lambda st: st, st)
            o_ref[pl.ds(q0, blk), :] = apply(st[0], qi & 1, 0)
            tot_ref[pl.ds(q0, blk), :] = jnp.where(lane2 == 0, st[1], st[2])
            return 0

        lax.fori_loop(0, nq, qblock, 0)

    t = nb * s
    return pl.pallas_call(
        body, grid=(nb, 8),
        in_specs=[pl.BlockSpec((s, LANES), lambda b, p: (b, CB_Q + p)),
                  pl.BlockSpec((s, LANES), lambda b, p: (b, CB_K + p)),
                  pl.BlockSpec((s, LANES), lambda b, p: (b, CB_V + p))],
        out_specs=[pl.BlockSpec((s, LANES), lambda b, p: (b, p)),
                   pl.BlockSpec((None, None, s, 2), lambda b, p: (b, p, 0, 0))],
        out_shape=[jax.ShapeDtypeStruct((t, SB_WIDTH), F32),
                   jax.ShapeDtypeStruct((nb, 8, s, 2), F32)],
        scratch_shapes=[pltpu.VMEM((s, LANES), BF16)] * 5 + [
            pltpu.VMEM((2, 2, blk, blk), F32), pltpu.VMEM((2, blk, 2 * blk), BF16)],
        compiler_params=_params(("parallel", "parallel")), name="attn_fwd")(proj, proj, proj)


def _attn_bwd(proj, do, tot, nb, s):
    blk = min(ATT_BLK, s)
    nq = s // blk
    scale = HEAD_DIM ** -0.5

    def body(q_ref, k_ref, v_ref, do_ref, tot_ref, dq_ref, dk_ref, dv_ref,
             q0_s, q1_s, kb_s, k0_s, k1_s, vb_s, d0_s, d1_s, z_s, da_s, dz_s, a_s, dk_acc, dv_acc):
        head0 = lax.broadcasted_iota(jnp.int32, (s, LANES), 1) < HEAD_DIM
        r_io, c_io = _attn_masks(blk)
        tri = c_io < r_io
        u_le = _bf(r_io <= c_io)
        u2 = jnp.concatenate([u_le, u_le], axis=0)
        u_lt = _bf(r_io < c_io)
        qv = q_ref[...] * scale
        q0_s[...] = _bf(jnp.where(head0, qv, 0.0))
        q1_s[...] = _bf(jnp.where(head0, 0.0, qv))
        kv = k_ref[...]
        kb_s[...] = _bf(kv)
        k0_s[...] = _bf(jnp.where(head0, kv, 0.0))
        k1_s[...] = _bf(jnp.where(head0, 0.0, kv))
        vb_s[...] = _bf(v_ref[...])
        dov = do_ref[...]
        d0_s[...] = _bf(jnp.where(head0, dov, 0.0))
        d1_s[...] = _bf(jnp.where(head0, 0.0, dov))
        dk_acc[...] = jnp.zeros_like(dk_acc)
        dv_acc[...] = jnp.zeros_like(dv_acc)

        def qblock(qi, _):
            q0 = pl.multiple_of(qi * blk, blk)
            qs_ = pl.ds(q0, blk)
            qhs = (q0_s[qs_, :], q1_s[qs_, :])
            dhs = (d0_s[qs_, :], d1_s[qs_, :])
            qcat = jnp.concatenate(qhs, axis=0)
            dcat = jnp.concatenate(dhs, axis=0)
            totf = tot_ref[qs_, :]

            def scores(kblk, slot):
                ks_ = pl.ds(pl.multiple_of(kblk * blk, blk), blk)
                kb, vb = kb_s[ks_, :], vb_s[ks_, :]
                for hh in range(2):
                    z_s[slot, hh] = _dot_nt(qhs[hh], kb)
                    da_s[slot, hh] = _dot_nt(dhs[hh], vb)

            def grads(slot, st, diag):
                new = []
                for hh in range(2):
                    cn, cw = st[2 * hh], st[2 * hh + 1]
                    z = z_s[slot, hh]
                    sp = _softplus(z)
                    lb = z - sp
                    spm = jnp.where(tri, sp, 0.0) if diag else sp
                    hi, lo = _trunc_split(spm)
                    pin = _dot(jnp.concatenate([hi, lo], axis=1), u2)
                    a = jnp.exp(lb + (pin + cn))
                    if diag:
                        a = jnp.where(tri, a, 0.0)
                    w = a * da_s[slot, hh]
                    wex = _dot(_bf(w), u_lt)
                    dz = w - jnp.exp(lb) * (w + (wex + cw))
                    if diag:
                        dz = jnp.where(tri, dz, 0.0)
                    dz_s[slot, hh] = _bf(dz)
                    a_s[slot, hh] = _bf(a)
                    new += [cn + pin[:, blk - 1:blk], cw + wex[:, blk - 1:blk] + w[:, blk - 1:blk]]
                return tuple(new)

            def apply(dq, slot, kblk):
                ks_ = pl.ds(pl.multiple_of(kblk * blk, blk), blk)
                dzs = (dz_s[slot, 0], dz_s[slot, 1])
                dq = dq + _dot(jnp.concatenate(dzs, axis=1), jnp.concatenate([k0_s[ks_, :], k1_s[ks_, :]], axis=0))
                dk_acc[ks_, :] += _dot_tn(jnp.concatenate(dzs, axis=0), qcat)
                dv_acc[ks_, :] += _dot_tn(jnp.concatenate([a_s[slot, 0], a_s[slot, 1]], axis=0), dcat)
                return dq

            def step(j, slot, st, diag):
                dq = apply(st[0], 1 - slot, jnp.maximum(j - 1, 0))
                if not diag:
                    scores(j + 1, 1 - slot)
                return (dq, *grads(slot, st[1:], diag))

            dz_s[1] = jnp.zeros_like(dz_s[1])
            a_s[1] = jnp.zeros_like(a_s[1])
            scores(0, 0)
            zc = jnp.zeros((blk, 1), F32)
            st = (jnp.zeros((blk, LANES), F32), -totf[:, 0:1], zc, -totf[:, 1:2], zc)
            st = lax.fori_loop(
                0, qi // 2, lambda i, st: step(2 * i + 1, 1, step(2 * i, 0, st, False), False), st)
            st = lax.cond(qi % 2 == 1,
                          lambda st: step(qi, 1, step(qi - 1, 0, st, False), True),
                          lambda st: step(qi, 0, st, True), st)
            dq_ref[qs_, :] = _bf(apply(st[0], qi & 1, qi) * scale)
            return 0

        lax.fori_loop(0, nq, qblock, 0)
        dk_ref[...] = _bf(dk_acc[...])
        dv_ref[...] = _bf(dv_acc[...])

    t = nb * s
    hp = lambda b, p: (b, p)
    out = jax.ShapeDtypeStruct((t, SB_WIDTH), BF16)
    return pl.pallas_call(
        body, grid=(nb, 8),
        in_specs=[pl.BlockSpec((s, LANES), lambda b, p: (b, CB_Q + p)),
                  pl.BlockSpec((s, LANES), lambda b, p: (b, CB_K + p)),
                  pl.BlockSpec((s, LANES), lambda b, p: (b, CB_V + p)),
                  pl.BlockSpec((s, LANES), hp),
                  pl.BlockSpec((None, None, s, 2), lambda b, p: (b, p, 0, 0))],
        out_specs=[pl.BlockSpec((s, LANES), hp)] * 3,
        out_shape=[out, out, out],
        scratch_shapes=[pltpu.VMEM((s, LANES), BF16)] * 8 + [
            pltpu.VMEM((2, 2, blk, blk), F32), pltpu.VMEM((2, 2, blk, blk), F32),
            pltpu.VMEM((2, 2, blk, blk), BF16), pltpu.VMEM((2, 2, blk, blk), BF16),
            pltpu.VMEM((s, LANES), F32), pltpu.VMEM((s, LANES), F32)],
        compiler_params=_params(("parallel", "parallel")), name="attn_bwd")(proj, proj, proj, do, tot)


CONV_TC = 256


def _shift_down(x, k, row):
    return jnp.where(row >= k, pltpu.roll(x, k, 0), 0.0)


def _shift_up(x, k, row, s):
    return jnp.where(row < s - k, pltpu.roll(x, s - k, 0), 0.0)


def _conv_fwd(proj, conv_w, conv_b, nb, s):
    tc = CONV_TC
    cb0 = CB_XBC * LANES // tc

    def body(x_ref, w_ref, b_ref, o_ref):
        x = x_ref[...].astype(F32)
        row = lax.broadcasted_iota(jnp.int32, x.shape, 0)
        y = b_ref[...] + w_ref[CONV_K - 1:CONV_K, :] * x
        for kk in range(1, CONV_K):
            y = y + w_ref[CONV_K - 1 - kk:CONV_K - kk, :] * _shift_down(x, kk, row)
        o_ref[...] = y * _sigmoid(y)

    t = nb * s
    return pl.pallas_call(
        body, grid=(nb, CONV_DIM // tc),
        in_specs=[pl.BlockSpec((s, tc), lambda b, c: (b, cb0 + c)),
                  pl.BlockSpec((CONV_K, tc), lambda b, c: (0, c)),
                  pl.BlockSpec((1, tc), lambda b, c: (0, c))],
        out_specs=pl.BlockSpec((s, tc), lambda b, c: (b, c)),
        out_shape=jax.ShapeDtypeStruct((t, CONV_DIM), F32),
        compiler_params=_params(("parallel", "parallel")), name="conv_fwd")(proj, conv_w, conv_b)


def _conv_bwd(proj, dact, conv_w, conv_b, nb, s, *, col0, name):
    width = dact.shape[1]
    tc = min(CONV_TC, width)
    cb0 = (CB_XBC * LANES + col0) // tc
    wb0 = col0 // tc

    def body(x_ref, d_ref, w_ref, b_ref, dx_ref, dw_ref, db_ref):
        @pl.when(pl.program_id(1) == 0)
        def _():
            dw_ref[...] = jnp.zeros_like(dw_ref)
            db_ref[...] = jnp.zeros_like(db_ref)

        x = x_ref[...].astype(F32)
        row = lax.broadcasted_iota(jnp.int32, x.shape, 0)
        xs = [x] + [_shift_down(x, kk, row) for kk in range(1, CONV_K)]
        y = b_ref[...] + w_ref[CONV_K - 1:CONV_K, :] * x
        for kk in range(1, CONV_K):
            y = y + w_ref[CONV_K - 1 - kk:CONV_K - kk, :] * xs[kk]
        sg = _sigmoid(y)
        dy = d_ref[...] * (sg * (1.0 + y * (1.0 - sg)))
        dx = w_ref[CONV_K - 1:CONV_K, :] * dy
        for kk in range(1, CONV_K):
            dx = dx + w_ref[CONV_K - 1 - kk:CONV_K - kk, :] * _shift_up(dy, kk, row, s)
        dx_ref[...] = _bf(dx)
        db_ref[...] += jnp.sum(dy, axis=0, keepdims=True)
        for kk in range(CONV_K):
            dw_ref[CONV_K - 1 - kk:CONV_K - kk, :] += jnp.sum(dy * xs[kk], axis=0, keepdims=True)

    t = nb * s
    return pl.pallas_call(
        body, grid=(width // tc, nb),
        in_specs=[pl.BlockSpec((s, tc), lambda c, b: (b, cb0 + c)),
                  pl.BlockSpec((s, tc), lambda c, b: (b, c)),
                  pl.BlockSpec((CONV_K, tc), lambda c, b: (0, wb0 + c)),
                  pl.BlockSpec((1, tc), lambda c, b: (0, wb0 + c))],
        out_specs=[pl.BlockSpec((s, tc), lambda c, b: (b, c)),
                   pl.BlockSpec((CONV_K, tc), lambda c, b: (0, c)),
                   pl.BlockSpec((1, tc), lambda c, b: (0, c))],
        out_shape=[jax.ShapeDtypeStruct((t, width), BF16),
                   jax.ShapeDtypeStruct((CONV_K, width), F32),
                   jax.ShapeDtypeStruct((1, width), F32)],
        compiler_params=_params(("parallel", "arbitrary")), name=name)(proj, dact, conv_w, conv_b)


L = SSD_CHUNK
N_PAIR = SSD_HPG // 2


def _pair_lanes(v0, v1, lane):
    return jnp.where(lane < HEAD_DIM, v0, v1)


def _onehot(shape, row_shift, col_shift):
    row = lax.broadcasted_iota(jnp.int32, shape, 0)
    col = lax.broadcasted_iota(jnp.int32, shape, 1)
    return _bf(lax.shift_right_logical(row, row_shift) == lax.shift_right_logical(col, col_shift))


def _ssd_tables():
    sp = _onehot((LANES, N_PAIR * LANES), 0, 6)
    sf = _onehot((LANES, SSD_HPG * LANES), 0, 7)
    return (jnp.concatenate([sp] * 3, axis=0), jnp.concatenate([sf] * 3, axis=0),
            _onehot((N_PAIR * LANES, LANES), 6, 0), _onehot((SSD_HPG * LANES, LANES), 7, 0))


def _split3(a):
    a1 = _bf(a)
    r1 = a - a1.astype(F32)
    a2 = _bf(r1)
    return jnp.concatenate([a1, a2, _bf(r1 - a2.astype(F32))], axis=1)


def _split2(a):
    hi, lo = _trunc_split(a)
    return jnp.concatenate([hi, lo], axis=1)


def _ssd_chunk_common(dtr_ref, bias_ref, alog_ref, r0, spread_pair, spread_full):
    rr = lax.broadcasted_iota(jnp.int32, (L, L), 0)
    cc = lax.broadcasted_iota(jnp.int32, (L, L), 1)
    tri = (cc <= rr).astype(F32)
    raw = dtr_ref[pl.ds(r0, L), :] + bias_ref[...]
    dt = _softplus(raw)
    a_neg = -jnp.exp(alog_ref[...])
    acs = _dot_hi(tri, dt * a_neg)
    acs3 = _split3(acs)
    return (raw, dt, a_neg, acs, acs.T, rr, cc,
            _dot(_split3(dt), spread_pair), _dot(acs3, spread_pair), _dot(acs3, spread_full))


def _ssd_fwd(xbc, dtr, bias, alog, dskip, nb, s):
    nc = s // L
    t = nb * s

    def body(x_ref, b_ref, c_ref, dtr_ref, bias_ref, alog_ref, dsk_ref, y_ref, hs_ref, h_scr):
        lane = lax.broadcasted_iota(jnp.int32, (L, LANES), 1)
        lane1 = lax.broadcasted_iota(jnp.int32, (1, LANES), 1)
        rowc = lax.broadcasted_iota(jnp.int32, (L, 1), 0)
        h_scr[...] = jnp.zeros_like(h_scr)
        dsk = dsk_ref[...]
        spread_pair, spread_full, _, _ = _ssd_tables()

        def chunk(c, _):
            r0 = pl.multiple_of(c * L, L)
            raw, dt, a_neg, acs, acs_t, rr, cc, dt_all, acs_all, acs_full = _ssd_chunk_common(
                dtr_ref, bias_ref, alog_ref, r0, spread_pair, spread_full)
            causal = rr >= cc
            bb = _bf(b_ref[pl.ds(r0, L), :])
            cb = _bf(c_ref[pl.ds(r0, L), :])
            gm = _dot_nt(cb, bb)
            for pr in range(N_PAIR):
                j0, j1 = 2 * pr, 2 * pr + 1
                ms = []
                for j in (j0, j1):
                    seg = acs_full[:, j * LANES:(j + 1) * LANES] - acs_t[j:j + 1, :]
                    ms.append(_bf(gm * jnp.exp(jnp.where(causal, seg, -1e30))))
                x = x_ref[pl.ds(r0, L), pr * LANES:(pr + 1) * LANES]
                dt_p = dt_all[:, pr * LANES:(pr + 1) * LANES]
                acs_p = acs_all[:, pr * LANES:(pr + 1) * LANES]
                last_p = acs_p[L - 1:L, :]
                d_p = _pair_lanes(dsk[:, j0:j0 + 1], dsk[:, j1:j1 + 1], lane1)
                xd = x * dt_p
                xdb = _bf(xd)
                yd = _pair_lanes(_dot(ms[0], xdb), _dot(ms[1], xdb), lane)
                hp = h_scr[pr]
                yo = _dot_nt(cb, _bf(hp)) * jnp.exp(acs_p)
                y_ref[pl.ds(r0, L), pr * LANES:(pr + 1) * LANES] = yd + yo + x * d_p
                hs_ref[c, pr] = hp
                sp = _dot_tn(_bf(xd * jnp.exp(last_p - acs_p)), bb)
                cd = jnp.where(rowc < HEAD_DIM, jnp.exp(acs[L - 1:L, j0:j0 + 1]),
                               jnp.exp(acs[L - 1:L, j1:j1 + 1]))
                h_scr[pr] = hp * cd + sp
            return 0

        lax.fori_loop(0, nc, chunk, 0)

    grp = lambda b, g: (g, 0, 0)
    return pl.pallas_call(
        body, grid=(nb, SSD_GROUPS),
        in_specs=[pl.BlockSpec((s, 4 * LANES), lambda b, g: (b, g)),
                  pl.BlockSpec((s, LANES), lambda b, g: (b, 16 + g)),
                  pl.BlockSpec((s, LANES), lambda b, g: (b, 20 + g)),
                  pl.BlockSpec((None, s, LANES), lambda b, g: (g, b, 0)),
                  pl.BlockSpec((None, 1, LANES), grp), pl.BlockSpec((None, 1, LANES), grp),
                  pl.BlockSpec((None, 1, LANES), grp)],
        out_specs=[pl.BlockSpec((s, 4 * LANES), lambda b, g: (b, g)),
                   pl.BlockSpec((None, None, nc, N_PAIR, L, LANES), lambda b, g: (b, g, 0, 0, 0, 0))],
        out_shape=[jax.ShapeDtypeStruct((t, SSD_WIDTH), F32),
                   jax.ShapeDtypeStruct((nb, SSD_GROUPS, nc, N_PAIR, L, LANES), F32)],
        scratch_shapes=[pltpu.VMEM((N_PAIR, L, LANES), F32)],
        compiler_params=_params(("parallel", "parallel")), name="ssd_fwd")(
            xbc, xbc, xbc, dtr, bias, alog, dskip)


def _ssd_bwd(xbc, dtr, bias, alog, dskip, hs, dy, nb, s):
    nc = s // L
    t = nb * s

    def body(x_ref, b_ref, c_ref, dtr_ref, bias_ref, alog_ref, dsk_ref, hs_ref, dy_ref,
             dx_ref, db_ref, dc_ref, ddtr_ref, dbias_ref, dalog_ref, ddsk_ref, dh_scr):
        lane = lax.broadcasted_iota(jnp.int32, (L, LANES), 1)
        lane1 = lax.broadcasted_iota(jnp.int32, (1, LANES), 1)
        rowc = lax.broadcasted_iota(jnp.int32, (L, 1), 0)
        dh_scr[...] = jnp.zeros_like(dh_scr)
        dsk = dsk_ref[...]

        @pl.when(pl.program_id(1) == 0)
        def _():
            dbias_ref[...] = jnp.zeros_like(dbias_ref)
            dalog_ref[...] = jnp.zeros_like(dalog_ref)
            ddsk_ref[...] = jnp.zeros_like(ddsk_ref)

        spread_pair, spread_full, sum_pair, sum_full = _ssd_tables()
        sum_pair2 = jnp.concatenate([sum_pair] * 2, axis=0)
        sum_full2 = jnp.concatenate([sum_full] * 2, axis=0)

        def chunk(i, _):
            c = nc - 1 - i
            r0 = pl.multiple_of(c * L, L)
            raw, dt, a_neg, acs, acs_t, rr, cc, dt_all, acs_all, acs_full = _ssd_chunk_common(
                dtr_ref, bias_ref, alog_ref, r0, spread_pair, spread_full)
            causal = rr >= cc
            tri_t = (cc >= rr).astype(F32)
            bb = _bf(b_ref[pl.ds(r0, L), :])
            cb = _bf(c_ref[pl.ds(r0, L), :])
            gm = _dot_nt(cb, bb)
            dg = jnp.zeros((L, L), F32)
            dbacc = jnp.zeros((L, LANES), F32)
            dcacc = jnp.zeros((L, LANES), F32)
            dacs_t = jnp.zeros((L, L), F32)
            ta_all, te_all, qm_all, tc_all, td_all, dcd_all = [], [], [], [], [], []
            for pr in range(N_PAIR):
                js = (2 * pr, 2 * pr + 1)
                lms, mfs = [], []
                for j in js:
                    seg = acs_full[:, j * LANES:(j + 1) * LANES] - acs_t[j:j + 1, :]
                    lm = jnp.exp(jnp.where(causal, seg, -1e30))
                    lms.append(lm)
                    mfs.append(gm * lm)
                x = x_ref[pl.ds(r0, L), pr * LANES:(pr + 1) * LANES]
                dyv = dy_ref[pl.ds(r0, L), pr * LANES:(pr + 1) * LANES]
                dt_p = dt_all[:, pr * LANES:(pr + 1) * LANES]
                acs_p = acs_all[:, pr * LANES:(pr + 1) * LANES]
                last_p = acs_p[L - 1:L, :]
                d_p = _pair_lanes(dsk[:, js[0]:js[0] + 1], dsk[:, js[1]:js[1] + 1], lane1)
                e_p = jnp.exp(acs_p)
                dte_p = jnp.exp(last_p - acs_p)
                xd = x * dt_p
                xdb = _bf(xd)
                hp = hs_ref[c, pr]
                hb = _bf(hp)
                dye = _bf(dyv * e_p)
                yo = _dot_nt(cb, hb) * e_p
                dcacc = dcacc + _dot(dye, hb)
                dh_out = _dot_tn(dye, cb)
                dyb = _bf(dyv)
                dxd = jnp.zeros((L, LANES), F32)
                for hh in range(2):
                    hm = (lane >= HEAD_DIM) if hh else (lane < HEAD_DIM)
                    dm = _dot_nt(_bf(jnp.where(hm, dyv, 0.0)), xdb)
                    dxd = dxd + jnp.where(hm, _dot_tn(_bf(mfs[hh]), dyb), 0.0)
                    dg = dg + dm * lms[hh]
                    qm = dm * mfs[hh]
                    qm_all.append(qm)
                    dacs_t = dacs_t - jnp.where(rr == js[hh], jnp.sum(qm, axis=0, keepdims=True), 0.0)
                ds = dh_scr[pr]
                dsb = _bf(ds)
                xdd = xd * dte_p
                dxdd = _dot_nt(bb, dsb)
                dbacc = dbacc + _dot(_bf(xdd), dsb)
                dxd = dxd + dxdd * dte_p
                cd = jnp.where(rowc < HEAD_DIM, jnp.exp(acs[L - 1:L, js[0]:js[0] + 1]),
                               jnp.exp(acs[L - 1:L, js[1]:js[1] + 1]))
                prod = ds * hp
                dcd_all += [jnp.sum(prod[:HEAD_DIM], axis=0, keepdims=True),
                            jnp.sum(prod[HEAD_DIM:], axis=0, keepdims=True)]
                dh_scr[pr] = dh_out + ds * cd
                tcv = dxdd * xdd
                ta_all.append(dyv * yo - tcv)
                tc_all.append(jnp.sum(tcv, axis=0, keepdims=True))
                te_all.append(dxd * x)
                td_all.append(jnp.sum(dyv * x, axis=0, keepdims=True))
                dx_ref[pl.ds(r0, L), pr * LANES:(pr + 1) * LANES] = dxd * dt_p + dyv * d_p
            dgb = _bf(dg)
            dc_ref[pl.ds(r0, L), :] = dcacc + _dot(dgb, bb)
            db_ref[pl.ds(r0, L), :] = dbacc + _dot_tn(dgb, cb)
            last_add = jnp.zeros((1, LANES), F32)
            dd_add = jnp.zeros((1, LANES), F32)
            cd_row = jnp.exp(acs[L - 1:L, :])
            for j in range(SSD_HPG):
                hm = (lane1 >= HEAD_DIM) if j % 2 else (lane1 < HEAD_DIM)
                tcj = jnp.sum(jnp.where(hm, tc_all[j // 2], 0.0), axis=1, keepdims=True)
                tdj = jnp.sum(jnp.where(hm, td_all[j // 2], 0.0), axis=1, keepdims=True)
                dcdj = jnp.sum(dcd_all[j], axis=1, keepdims=True)
                last_add = last_add + jnp.where(lane1 == j, tcj + dcdj * cd_row, 0.0)
                dd_add = dd_add + jnp.where(lane1 == j, tdj, 0.0)
            dacs = (_dot(_split2(jnp.concatenate(ta_all, axis=1)), sum_pair2)
                    + _dot(_split2(jnp.concatenate(qm_all, axis=1)), sum_full2)
                    + dacs_t.T + jnp.where(rowc == L - 1, last_add, 0.0))
            ddta = _dot_hi(tri_t, dacs)
            ddt = _dot(_bf(jnp.concatenate(te_all, axis=1)), sum_pair) + ddta * a_neg
            ddraw = ddt * _sigmoid(raw)
            ddtr_ref[pl.ds(r0, L), :] = ddraw
            dbias_ref[...] += jnp.sum(ddraw, axis=0, keepdims=True)
            dalog_ref[...] += jnp.sum(ddta * dt, axis=0, keepdims=True) * a_neg
            ddsk_ref[...] += dd_add
            return 0

        lax.fori_loop(0, nc, chunk, 0)

    grp = lambda g, b: (g, 0, 0)
    small = jax.ShapeDtypeStruct((SSD_GROUPS, 1, LANES), F32)
    return pl.pallas_call(
        body, grid=(SSD_GROUPS, nb),
        in_specs=[pl.BlockSpec((s, 4 * LANES), lambda g, b: (b, g)),
                  pl.BlockSpec((s, LANES), lambda g, b: (b, 16 + g)),
                  pl.BlockSpec((s, LANES), lambda g, b: (b, 20 + g)),
                  pl.BlockSpec((None, s, LANES), lambda g, b: (g, b, 0)),
                  pl.BlockSpec((None, 1, LANES), grp), pl.BlockSpec((None, 1, LANES), grp),
                  pl.BlockSpec((None, 1, LANES), grp),
                  pl.BlockSpec((None, None, nc, N_PAIR, L, LANES), lambda g, b: (b, g, 0, 0, 0, 0)),
                  pl.BlockSpec((s, 4 * LANES), lambda g, b: (b, g))],
        out_specs=[pl.BlockSpec((s, 4 * LANES), lambda g, b: (b, g)),
                   pl.BlockSpec((s, LANES), lambda g, b: (b, g)),
                   pl.BlockSpec((s, LANES), lambda g, b: (b, g)),
                   pl.BlockSpec((None, s, LANES), lambda g, b: (g, b, 0)),
                   pl.BlockSpec((None, 1, LANES), grp), pl.BlockSpec((None, 1, LANES), grp),
                   pl.BlockSpec((None, 1, LANES), grp)],
        out_shape=[jax.ShapeDtypeStruct((t, SSD_WIDTH), F32),
                   jax.ShapeDtypeStruct((t, SSD_GROUPS * SSD_STATE), F32),
                   jax.ShapeDtypeStruct((t, SSD_GROUPS * SSD_STATE), F32),
                   jax.ShapeDtypeStruct((SSD_GROUPS, t, LANES), F32), small, small, small],
        scratch_shapes=[pltpu.VMEM((N_PAIR, L, LANES), F32)],
        compiler_params=_params(("parallel", "arbitrary")), name="ssd_bwd")(
            xbc, xbc, xbc, dtr, bias, alog, dskip, hs, dy)


GN = SSD_WIDTH // SSD_GROUPS


def _post_fwd(o, proj, y, nw, *, tm=256):
    t = o.shape[0]

    def body(o_ref, za_ref, y_ref, zs_ref, nw_ref, ya_ref, ys_ref):
        za = za_ref[...].astype(F32)
        ya_ref[...] = _bf(o_ref[...] * (za * _sigmoid(za)))
        zs = zs_ref[...].astype(F32)
        yz = y_ref[...] * (zs * _sigmoid(zs))
        for g in range(SSD_GROUPS):
            sl = slice(g * GN, (g + 1) * GN)
            v = yz[:, sl]
            r = lax.rsqrt(jnp.mean(v * v, axis=-1, keepdims=True) + EPS)
            ys_ref[:, sl] = _bf(v * r * nw_ref[:, sl])

    return pl.pallas_call(
        body, grid=(t // tm,),
        in_specs=[pl.BlockSpec((tm, SB_WIDTH), lambda i: (i, 0)),
                  pl.BlockSpec((tm, SB_WIDTH), lambda i: (i, CB_ZA * LANES // SB_WIDTH)),
                  pl.BlockSpec((tm, SSD_WIDTH), lambda i: (i, 0)),
                  pl.BlockSpec((tm, SSD_WIDTH), lambda i: (i, CB_ZS * LANES // SSD_WIDTH)),
                  pl.BlockSpec((1, SSD_WIDTH), lambda i: (0, 0))],
        out_specs=[pl.BlockSpec((tm, SB_WIDTH), lambda i: (i, 0)),
                   pl.BlockSpec((tm, SSD_WIDTH), lambda i: (i, 0))],
        out_shape=[jax.ShapeDtypeStruct((t, SB_WIDTH), BF16), jax.ShapeDtypeStruct((t, SSD_WIDTH), BF16)],
        compiler_params=_params(("parallel",)), name="post_fwd")(o, proj, y, proj, nw)


def _post_bwd(dya, o, proj, dys, y, nw, *, tm=256):
    t = o.shape[0]

    def body(dya_ref, o_ref, za_ref, dys_ref, y_ref, zs_ref, nw_ref,
             do_ref, dza_ref, dy_ref, dzs_ref, dnw_ref):
        @pl.when(pl.program_id(0) == 0)
        def _():
            dnw_ref[...] = jnp.zeros_like(dnw_ref)

        za = za_ref[...].astype(F32)
        sa = _sigmoid(za)
        d = dya_ref[...]
        do_ref[...] = d * (za * sa)
        dza_ref[...] = _bf(d * o_ref[...] * (sa * (1.0 + za * (1.0 - sa))))
        zs = zs_ref[...].astype(F32)
        ss = _sigmoid(zs)
        silu = zs * ss
        yv = y_ref[...]
        yz = yv * silu
        dv = dys_ref[...]
        for g in range(SSD_GROUPS):
            sl = slice(g * GN, (g + 1) * GN)
            v = yz[:, sl]
            r = lax.rsqrt(jnp.mean(v * v, axis=-1, keepdims=True) + EPS)
            nrm = v * r
            dg = dv[:, sl]
            dnw_ref[:, sl] += jnp.sum(dg * nrm, axis=0, keepdims=True)
            dn = dg * nw_ref[:, sl]
            dyz = r * (dn - nrm * jnp.mean(dn * nrm, axis=-1, keepdims=True))
            dy_ref[:, sl] = dyz * silu[:, sl]
            dzs_ref[:, sl] = _bf(dyz * yv[:, sl] * (ss[:, sl] * (1.0 + zs[:, sl] * (1.0 - ss[:, sl]))))

    a_spec = pl.BlockSpec((tm, SB_WIDTH), lambda i: (i, 0))
    s_spec = pl.BlockSpec((tm, SSD_WIDTH), lambda i: (i, 0))
    w_spec = pl.BlockSpec((1, SSD_WIDTH), lambda i: (0, 0))
    return pl.pallas_call(
        body, grid=(t // tm,),
        in_specs=[a_spec, a_spec, pl.BlockSpec((tm, SB_WIDTH), lambda i: (i, CB_ZA * LANES // SB_WIDTH)),
                  s_spec, s_spec, pl.BlockSpec((tm, SSD_WIDTH), lambda i: (i, CB_ZS * LANES // SSD_WIDTH)),
                  w_spec],
        out_specs=[a_spec, a_spec, s_spec, s_spec, w_spec],
        out_shape=[jax.ShapeDtypeStruct((t, SB_WIDTH), F32), jax.ShapeDtypeStruct((t, SB_WIDTH), BF16),
                   jax.ShapeDtypeStruct((t, SSD_WIDTH), F32), jax.ShapeDtypeStruct((t, SSD_WIDTH), BF16),
                   jax.ShapeDtypeStruct((1, SSD_WIDTH), F32)],
        compiler_params=_params(("arbitrary",)), name="post_bwd")(dya, o, proj, dys, y, proj, nw)


def _merge_fwd(ya, ys, proj_g, *, tm=512):
    t = ya.shape[0]

    def body(ya_ref, ys_ref, ga_ref, gs_ref, m_ref):
        m_ref[...] = _bf(_sigmoid(ga_ref[...].astype(F32)) * ya_ref[...]
                         + _sigmoid(gs_ref[...].astype(F32)) * ys_ref[...])

    spec = pl.BlockSpec((tm, D_MODEL), lambda i: (i, 0))
    return pl.pallas_call(
        body, grid=(t // tm,),
        in_specs=[spec, spec, pl.BlockSpec((tm, D_MODEL), lambda i: (i, 0)),
                  pl.BlockSpec((tm, D_MODEL), lambda i: (i, 1))],
        out_specs=spec, out_shape=jax.ShapeDtypeStruct((t, D_MODEL), BF16),
        compiler_params=_params(("parallel",)), name="merge_fwd")(ya, ys, proj_g, proj_g)


def _merge_bwd(dm, ya, ys, proj_g, *, tm=512):
    t = ya.shape[0]

    def body(dm_ref, ya_ref, ys_ref, ga_ref, gs_ref, dya_ref, dys_ref, dg_ref):
        d = dm_ref[...]
        ga = _sigmoid(ga_ref[...].astype(F32))
        gs = _sigmoid(gs_ref[...].astype(F32))
        dya_ref[...] = _bf(d * ga)
        dys_ref[...] = _bf(d * gs)
        dg_ref[:, :D_MODEL] = _bf(d * ya_ref[...] * (ga * (1.0 - ga)))
        dg_ref[:, D_MODEL:] = _bf(d * ys_ref[...] * (gs * (1.0 - gs)))

    spec = pl.BlockSpec((tm, D_MODEL), lambda i: (i, 0))
    return pl.pallas_call(
        body, grid=(t // tm,),
        in_specs=[spec, spec, spec, pl.BlockSpec((tm, D_MODEL), lambda i: (i, 0)),
                  pl.BlockSpec((tm, D_MODEL), lambda i: (i, 1))],
        out_specs=[spec, spec, pl.BlockSpec((tm, 2 * D_MODEL), lambda i: (i, 0))],
        out_shape=[jax.ShapeDtypeStruct((t, D_MODEL), BF16), jax.ShapeDtypeStruct((t, D_MODEL), BF16),
                   jax.ShapeDtypeStruct((t, 2 * D_MODEL), BF16)],
        compiler_params=_params(("parallel",)), name="merge_bwd")(dm, ya, ys, proj_g, proj_g)


def _final(x, out, wf, target, *, tm=512):
    t = x.shape[0]

    def body(x_ref, o_ref, w_ref, t_ref, loss_ref, dx_ref, dw_ref):
        @pl.when(pl.program_id(0) == 0)
        def _():
            loss_ref[...] = jnp.zeros_like(loss_ref)
            dw_ref[...] = jnp.zeros_like(dw_ref)

        x2 = x_ref[...] + o_ref[...]
        r = lax.rsqrt(jnp.mean(x2 * x2, axis=-1, keepdims=True) + EPS)
        nrm = x2 * r
        e = nrm * w_ref[...] - t_ref[...]
        row_loss = jnp.mean(e * e, axis=-1, keepdims=True)
        loss_ref[...] += 0.5 * jnp.sum(row_loss, axis=0, keepdims=True)
        dyv = e * (1.0 / D_MODEL)
        dw_ref[...] += jnp.sum(dyv * nrm, axis=0, keepdims=True)
        dn = dyv * w_ref[...]
        dx_ref[...] = r * (dn - nrm * jnp.mean(dn * nrm, axis=-1, keepdims=True))

    spec = pl.BlockSpec((tm, D_MODEL), lambda i: (i, 0))
    fix = pl.BlockSpec((1, D_MODEL), lambda i: (0, 0))
    return pl.pallas_call(
        body, grid=(t // tm,),
        in_specs=[spec, spec, fix, spec],
        out_specs=[pl.BlockSpec((1, LANES), lambda i: (0, 0)), spec, fix],
        out_shape=[jax.ShapeDtypeStruct((1, LANES), F32), jax.ShapeDtypeStruct((t, D_MODEL), F32),
                   jax.ShapeDtypeStruct((1, D_MODEL), F32)],
        compiler_params=_params(("arbitrary",)), name="final_loss")(x, out, wf, target)


def _group_lanes(v):
    return jnp.pad(v.reshape(SSD_GROUPS, 1, SSD_HPG), ((0, 0), (0, 0), (0, LANES - SSD_HPG)))


def _ungroup_lanes(v):
    return v[:, :, :SSD_HPG].reshape(1, N_DT)


def _local_step(x, target, nb, s, w_in_t, conv_w, conv_b, norm_w, dt_bias, a_log, d_skip,
                ssm_norm_w, out_weights, final_norm_w, between=None):
    t = nb * s
    w_g_t = w_in_t[GATE_COL0:]
    w_dt_t = jnp.pad(w_in_t[DT_COL0:GATE_COL0], ((0, LANES - N_DT), (0, 0)))
    h, r1 = _rms_fwd(x, norm_w)
    proj = _matmul(h, w_in_t, nt=True, n=DT_COL0, out_dtype=BF16, name="proj_main", tm=512, tn=1024)
    proj_g = _matmul(h, w_g_t, nt=True, out_dtype=BF16, name="proj_gate", tm=512, tn=1024)
    dt_raw = _matmul(h, w_dt_t, nt=True, name="proj_dt")
    dtr = jnp.pad(dt_raw[:, :N_DT].reshape(t, SSD_GROUPS, SSD_HPG).transpose(1, 0, 2),
                  ((0, 0), (0, 0), (0, LANES - SSD_HPG)))
    bias_g, alog_g, dsk_g = _group_lanes(dt_bias), _group_lanes(a_log), _group_lanes(d_skip)

    o, tot = _attn_fwd(proj, nb, s)
    xbc = _conv_fwd(proj, conv_w, conv_b, nb, s)
    y, hs = _ssd_fwd(xbc, dtr, bias_g, alog_g, dsk_g, nb, s)
    ya_in, ys_in = _post_fwd(o, proj, y, ssm_norm_w)
    w_attn_out, w_ssm_out, w_o = out_weights(ys_in) if callable(out_weights) else out_weights
    ya = _matmul(ya_in, w_attn_out, name="attn_out")
    ys = _matmul(ys_in, w_ssm_out, name="ssm_out", tk=2048)
    merged = _merge_fwd(ya, ys, proj_g)
    out = _matmul(merged, w_o, name="out_proj")
    loss, dx2, d_final_w = _final(x, out, final_norm_w, target)

    dx2b = _bf(dx2)
    dmerged = _matmul(dx2b, w_o, nt=True, name="d_merged")
    d_w_o = _matmul(merged, dx2b, ta=True, name="d_w_o", tm=1024, tk=1024)
    dya, dys, dgate = _merge_bwd(dmerged, ya, ys, proj_g)
    d_w_attn_out = _matmul(ya_in, dya, ta=True, name="d_w_attn_out", tm=1024)
    d_w_ssm_out = _matmul(ys_in, dys, ta=True, name="d_w_ssm_out", tm=1024)
    dya_in = _matmul(dya, w_attn_out, nt=True, name="d_ya_in")
    dys_in = _matmul(dys, w_ssm_out, nt=True, name="d_ys_in")
    do, dza, dy, dzs, d_ssm_norm_w = _post_bwd(dya_in, o, proj, dys_in, y, ssm_norm_w)
    dq, dk, dv = _attn_bwd(proj, do, tot, nb, s)
    dxs_act, db_act, dc_act, ddtr, dbias_g, dalog_g, ddsk_g = _ssd_bwd(
        xbc, dtr, bias_g, alog_g, dsk_g, hs, dy, nb, s)
    dxs, dwx, dbx = _conv_bwd(proj, dxs_act, conv_w, conv_b, nb, s, col0=0, name="conv_bwd_x")
    dbm, dwb, dbb = _conv_bwd(proj, db_act, conv_w, conv_b, nb, s, col0=SSD_WIDTH, name="conv_bwd_b")
    dcm, dwc, dbc = _conv_bwd(proj, dc_act, conv_w, conv_b, nb, s, col0=SSD_WIDTH + 512, name="conv_bwd_c")
    dproj = jnp.concatenate([dq, dk, dv, dza, dzs, dxs, dbm, dcm], axis=1)
    ddt = jnp.pad(_bf(ddtr[:, :, :SSD_HPG].transpose(1, 0, 2).reshape(t, N_DT)), ((0, 0), (0, LANES - N_DT)))
    wt = dict(ta=True, out_dtype=BF16, tm=1024, tn=1024, tk=1024)
    d_w_in_t = jnp.concatenate(
        [_matmul(dproj, h, name="d_w_main", **wt), _matmul(ddt, h, name="d_w_dt", **wt)[:N_DT],
         _matmul(dgate, h, name="d_w_gate", **wt)], axis=0)
    grads = dict(
        w_in_t=d_w_in_t,
        conv_w=jnp.concatenate([dwx, dwb, dwc], axis=1), conv_b=jnp.concatenate([dbx, dbb, dbc], axis=1),
        dt_bias=_ungroup_lanes(dbias_g), a_log=_ungroup_lanes(dalog_g), d_skip=_ungroup_lanes(ddsk_g),
        ssm_norm_w=d_ssm_norm_w, w_attn_out=d_w_attn_out, w_ssm_out=d_w_ssm_out, w_o=d_w_o,
        final_norm_w=d_final_w)
    extra = None
    if between is not None:
        zero, extra = between(grads)
        ddt = ddt + _bf(zero[0:1, 0:1])

    dh = _matmul(ddt, w_dt_t, name="d_h_dt")
    dh = _matmul(dgate, w_g_t, add=dh, name="d_h_gate", tm=1024, tn=1024, tk=1024)
    dh = _matmul(dproj, w_in_t, add=dh, name="d_h", tm=1024, tn=1024, tk=1024)
    grad_x, grads["norm_w"] = _rms_bwd(dh, x, r1, norm_w, dx2)
    return loss, grad_x, grads, extra


N_CHIP = N_DEV // 2


def _pair_exchange(scatter, bcast, *, name):
    arrays = list(scatter) + list(bcast)
    ns, n = len(scatter), len(arrays)
    n_sem = ns * N_CHIP + (n - ns) * (N_DEV - 1)

    def body(*refs):
        ins, outs = refs[:n], refs[n:2 * n]
        send_sems, recv_sems, local_sems = refs[2 * n:]
        x, y, c = lax.axis_index("x"), lax.axis_index("y"), lax.axis_index("c")
        me = 4 * x + 2 * y + c
        copies, k = [], 0
        for a in range(ns):
            for q in range(N_CHIP):
                cp = pltpu.make_async_remote_copy(
                    src_ref=ins[a].at[2 * q + 1 - c], dst_ref=outs[a].at[q],
                    send_sem=send_sems.at[k], recv_sem=recv_sems.at[k],
                    device_id=(x, y, 1 - c), device_id_type=pl.DeviceIdType.MESH)
                cp.start()
                copies.append(cp)
                k += 1
        for a in range(ns, n):
            cp = pltpu.make_async_copy(ins[a], outs[a].at[me], local_sems.at[a - ns])
            cp.start()
            copies.append(cp)
            for r in range(1, N_DEV):
                tx = 1 - x if (r >> 2) & 1 else x
                ty = 1 - y if (r >> 1) & 1 else y
                tc = 1 - c if r & 1 else c
                cp = pltpu.make_async_remote_copy(
                    src_ref=ins[a], dst_ref=outs[a].at[me], send_sem=send_sems.at[k], recv_sem=recv_sems.at[k],
                    device_id=(tx, ty, tc), device_id_type=pl.DeviceIdType.MESH)
                cp.start()
                copies.append(cp)
                k += 1
        for cp in copies:
            cp.wait()

    out_shape = [jax.ShapeDtypeStruct((N_CHIP,) + v.shape[1:] if i < ns else (N_DEV,) + v.shape, v.dtype)
                 for i, v in enumerate(arrays)]
    return pl.pallas_call(
        body, in_specs=[pl.BlockSpec(memory_space=pl.ANY)] * n,
        out_specs=[pl.BlockSpec(memory_space=pl.ANY)] * n, out_shape=out_shape,
        scratch_shapes=[pltpu.SemaphoreType.DMA((n_sem,)), pltpu.SemaphoreType.DMA((n_sem,)),
                        pltpu.SemaphoreType.DMA((max(n - ns, 1),))],
        compiler_params=pltpu.CompilerParams(has_side_effects=True), name=name)(*arrays)


def _pair_add(mine, recv, core, *, name, tc=None):
    _, r, c = recv.shape
    tc = c if tc is None else tc
    assert c % tc == 0

    def body(core_ref, m_ref, r_ref, o_ref):
        o_ref[...] = (m_ref[...].astype(F32) + r_ref[...].astype(F32)).astype(o_ref.dtype)

    spec = pl.BlockSpec((None, r, tc), lambda q, i, core_ref: (q, 0, i))
    return pl.pallas_call(
        body, grid_spec=pltpu.PrefetchScalarGridSpec(
            num_scalar_prefetch=1, grid=(N_CHIP, c // tc),
            in_specs=[pl.BlockSpec((None, r, tc), lambda q, i, core_ref: (2 * q + core_ref[0], 0, i)), spec],
            out_specs=spec),
        out_shape=jax.ShapeDtypeStruct(recv.shape, recv.dtype),
        compiler_params=_params(("parallel", "parallel")), name=name)(core, mine, recv)


_HBM = pl.BlockSpec(memory_space=pltpu.HBM)
_SEM = pl.BlockSpec(memory_space=pltpu.SEMAPHORE)
_DATAFLOW = pltpu.SideEffectType.DATAFLOW_SIDE_EFFECTING
_PEERS = {"chips": N_CHIP - 1, "all": N_DEV - 1}


def _split_copies(pattern, srcs, lands, send_sems, recv_sems):
    x, y, c = lax.axis_index("x"), lax.axis_index("y"), lax.axis_index("c")
    per = _PEERS[pattern]
    copies = []
    for a in range(len(srcs)):
        for r in range(1, per + 1):
            if pattern == "chips":
                tx = 1 - x if (r >> 1) & 1 else x
                ty = 1 - y if r & 1 else y
                tc = c
                src, dst = srcs[a].at[2 * tx + ty], lands[a].at[2 * x + y]
            else:
                tx = 1 - x if (r >> 2) & 1 else x
                ty = 1 - y if (r >> 1) & 1 else y
                tc = 1 - c if r & 1 else c
                src, dst = srcs[a], lands[a].at[4 * x + 2 * y + c]
            copies.append(pltpu.make_async_remote_copy(
                src_ref=src, dst_ref=dst, send_sem=send_sems.at[a * per + r - 1],
                recv_sem=recv_sems.at[a * per + r - 1], device_id=(tx, ty, tc), device_id_type=pl.DeviceIdType.MESH))
    return copies


def _split_start(pattern, arrays, *, name):
    n = len(arrays)
    n_sem = n * _PEERS[pattern]

    def body(*refs):
        srcs, lands = refs[:n], refs[n:2 * n]
        send_sems, recv_sems = refs[2 * n], refs[2 * n + 1]
        zero = refs[-1]
        for cp in _split_copies(pattern, srcs, lands, send_sems, recv_sems):
            cp.start()
        zero[...] = jnp.zeros_like(zero)

    src_t = [pltpu.HBM(v.shape, v.dtype) for v in arrays]
    land_t = [pltpu.HBM(v.shape if pattern == "chips" else (N_DEV,) + v.shape, v.dtype) for v in arrays]
    outs = pl.pallas_call(
        body, name=name,
        out_shape=(pltpu.SemaphoreType.DMA((n_sem,)), pltpu.SemaphoreType.DMA((n_sem,)), *src_t, *land_t,
                   jax.ShapeDtypeStruct((8, LANES), F32)),
        in_specs=[_HBM] * (2 * n),
        out_specs=(_SEM, _SEM, *[_HBM] * (2 * n), pl.BlockSpec(memory_space=pltpu.VMEM)),
        input_output_aliases={i: 2 + i for i in range(2 * n)},
        compiler_params=pltpu.CompilerParams(has_side_effects=_DATAFLOW))(
            *[pltpu.with_memory_space_constraint(v, pltpu.HBM) for v in arrays],
            *[pltpu.with_memory_space_constraint(lax.empty(t.shape, t.dtype), pltpu.HBM) for t in land_t])
    return outs[0], outs[1], outs[2:2 + n], outs[2 + n:2 + 2 * n], outs[-1]


def _split_wait(pattern, send_sems, recv_sems, srcs, lands, after, *, name):
    n = len(srcs)

    def body(*refs):
        for cp in _split_copies(pattern, refs[:n], refs[n:2 * n], refs[2 * n], refs[2 * n + 1]):
            cp.wait_send()
            cp.wait_recv()

    outs = pl.pallas_call(
        body, name=name, out_shape=[pltpu.HBM(v.shape, v.dtype) for v in (*srcs, *lands)],
        in_specs=[_HBM] * (2 * n) + [_SEM, _SEM, pl.BlockSpec(memory_space=pl.ANY)],
        out_specs=[_HBM] * (2 * n), input_output_aliases={i: i for i in range(2 * n)},
        compiler_params=pltpu.CompilerParams(has_side_effects=_DATAFLOW))(
            *srcs, *lands, send_sems, recv_sems, after)
    return outs[:n], outs[n:]


def _own_slab(lands, srcs, index, whole):
    out = []
    for land, src in zip(lands, srcs):
        own = src[None] if whole else lax.dynamic_slice(src, (index, 0, 0), (1,) + src.shape[1:])
        out.append(lax.dynamic_update_slice(land, own, (index, 0, 0)))
    return out


def _gather(arrays, *, name):
    n = len(arrays)
    per = N_DEV - 1

    def body(*refs):
        ins, outs = refs[:n], refs[n:2 * n]
        send_sems, recv_sems, local_sems = refs[2 * n:]
        x, y, c = lax.axis_index("x"), lax.axis_index("y"), lax.axis_index("c")
        me, sibling = (x, y, c), (x, y, 1 - c)
        chips = [(1 - x, y), (x, 1 - y), (1 - x, 1 - y)]

        def copy(a, k, block, to, src=None):
            rows = outs[a].at[4 * block[0] + 2 * block[1] + block[2]]
            return pltpu.make_async_remote_copy(
                src_ref=rows if src is None else src, dst_ref=rows,
                send_sem=send_sems.at[a * per + k], recv_sem=recv_sems.at[a * per + k],
                device_id=to, device_id_type=pl.DeviceIdType.MESH)

        started = []
        for a in range(n):
            mine = pltpu.make_async_copy(ins[a], outs[a].at[4 * x + 2 * y + c], local_sems.at[a])
            mine.start()
            started.append(mine)
        for a in range(n):
            first = [copy(a, 1 + j, me, (*chip, c), src=ins[a]) for j, chip in enumerate(chips)]
            first.append(copy(a, 0, me, sibling, src=ins[a]))
            for cp in first:
                cp.start()
            started += first
        for j, chip in enumerate(chips):
            for a in range(n):
                copy(a, 1 + j, (*chip, c), me).wait_recv()
                passed = copy(a, 4 + j, (*chip, c), sibling)
                passed.start()
                started.append(passed)
        for a in range(n):
            copy(a, 0, sibling, me).wait_recv()
            for j, chip in enumerate(chips):
                copy(a, 4 + j, (*chip, 1 - c), me).wait_recv()
        for cp in started[:n]:
            cp.wait()
        for cp in started[n:]:
            cp.wait_send()

    return pl.pallas_call(
        body, in_specs=[pl.BlockSpec(memory_space=pl.ANY)] * n,
        out_specs=[pl.BlockSpec(memory_space=pl.ANY)] * n,
        out_shape=[jax.ShapeDtypeStruct((N_DEV,) + v.shape, v.dtype) for v in arrays],
        scratch_shapes=[pltpu.SemaphoreType.DMA((n * per,)), pltpu.SemaphoreType.DMA((n * per,)),
                        pltpu.SemaphoreType.DMA((n,))],
        compiler_params=pltpu.CompilerParams(has_side_effects=True), name=name)(*arrays)


def _adamw(parts, w, m, v, *, name, tc=None):
    r, c = w.shape
    n_parts = parts.shape[0]
    tc = c if tc is None else tc
    assert c % tc == 0

    def body(p_ref, w_ref, m_ref, v_ref, g_ref, d_ref, nm_ref, nv_ref):
        g = p_ref[0].astype(F32)
        for d in range(1, n_parts):
            g = g + p_ref[d].astype(F32)
        mm = ADAM_B1 * m_ref[...] + (1.0 - ADAM_B1) * g
        vv = ADAM_B2 * v_ref[...] + (1.0 - ADAM_B2) * (g * g)
        m_hat = mm / (1.0 - ADAM_B1 ** ADAM_STEP)
        v_hat = vv / (1.0 - ADAM_B2 ** ADAM_STEP)
        g_ref[...] = g
        d_ref[...] = -ADAM_LR * (m_hat / (jnp.sqrt(v_hat) + ADAM_EPS) + ADAM_WD * w_ref[...])
        nm_ref[...] = mm
        nv_ref[...] = vv

    spec = pl.BlockSpec((r, tc), lambda i: (0, i))
    out = jax.ShapeDtypeStruct((r, c), F32)
    return pl.pallas_call(
        body, grid=(c // tc,),
        in_specs=[pl.BlockSpec((n_parts, r, tc), lambda i: (0, 0, i)), spec, spec, spec],
        out_specs=[spec] * 4, out_shape=[out] * 4,
        compiler_params=_params(("parallel",)), name=name)(parts, w, m, v)


SMALL = (("norm_w", D_MODEL), ("conv_b", CONV_DIM), ("dt_bias", N_DT), ("a_log", N_DT),
         ("d_skip", N_DT), ("ssm_norm_w", SSD_WIDTH), ("final_norm_w", D_MODEL))
SMALL_ROWS = 64
SHARD = D_PROJ // N_DEV


def _pack_small(vals):
    flat = jnp.concatenate([vals[k].reshape(-1).astype(F32) for k, _ in SMALL])
    return jnp.pad(flat, (0, SMALL_ROWS * LANES - flat.shape[0])).reshape(SMALL_ROWS, LANES)


def _unpack_small(packed, shapes):
    flat = packed.reshape(-1)
    out, off = {}, 0
    for k, size in SMALL:
        out[k] = flat[off:off + size].reshape(shapes[k])
        off += size
    return out


def kernel(x, norm_w, w_in, conv_w, conv_b, dt_bias, a_log, d_skip, ssm_norm_w, w_attn_out, w_ssm_out, w_o, final_norm_w, loss_target, m_norm_w, m_w_in, m_conv_w, m_conv_b, m_dt_bias, m_a_log, m_d_skip, m_ssm_norm_w, m_w_attn_out, m_w_ssm_out, m_w_o, m_final_norm_w, v_norm_w, v_w_in, v_conv_w, v_conv_b, v_dt_bias, v_a_log, v_d_skip, v_ssm_norm_w, v_w_attn_out, v_w_ssm_out, v_w_o, v_final_norm_w):
    nb, s, _ = x.shape
    t = nb * s
    weights = dict(norm_w=norm_w, w_in=w_in, conv_w=conv_w, conv_b=conv_b, dt_bias=dt_bias, a_log=a_log,
                   d_skip=d_skip, ssm_norm_w=ssm_norm_w, w_attn_out=w_attn_out, w_ssm_out=w_ssm_out,
                   w_o=w_o, final_norm_w=final_norm_w)
    moms = dict(norm_w=m_norm_w, w_in=m_w_in, conv_w=m_conv_w, conv_b=m_conv_b, dt_bias=m_dt_bias,
                a_log=m_a_log, d_skip=m_d_skip, ssm_norm_w=m_ssm_norm_w, w_attn_out=m_w_attn_out,
                w_ssm_out=m_w_ssm_out, w_o=m_w_o, final_norm_w=m_final_norm_w)
    vels = dict(norm_w=v_norm_w, w_in=v_w_in, conv_w=v_conv_w, conv_b=v_conv_b, dt_bias=v_dt_bias,
                a_log=v_a_log, d_skip=v_d_skip, ssm_norm_w=v_ssm_norm_w, w_attn_out=v_w_attn_out,
                w_ssm_out=v_w_ssm_out, w_o=v_w_o, final_norm_w=v_final_norm_w)

    tr_ = lambda a: jnp.transpose(a[0])
    g_in, g_conv = _gather([_bf(tr_(w_in)), conv_w[0]], name="gather_weights")
    conv_full = g_conv.transpose(1, 0, 2).reshape(CONV_K, CONV_DIM)
    o_send, o_recv, o_srcs, o_lands, o_zero = _split_start(
        "all", [_bf(w_attn_out[0]), _bf(w_ssm_out[0]), _bf(w_o[0])], name="gather_out_weights_start")

    core = lax.axis_index("c").astype(jnp.int32).reshape(1)
    chip = 2 * lax.axis_index("x") + lax.axis_index("y")
    me = 2 * chip + lax.axis_index("c")

    def out_weights(after):
        srcs, lands = _split_wait("all", o_send, o_recv, o_srcs, o_lands, after, name="gather_out_weights_wait")
        g_wa, g_ws, g_wo = _own_slab(lands, srcs, me, True)
        return (g_wa.reshape(SB_WIDTH, D_MODEL), g_ws.reshape(SSD_WIDTH, D_MODEL), g_wo.reshape(D_MODEL, D_MODEL))

    def send_weight_grads(grads):
        slabs = [grads["w_in_t"].reshape(N_DEV, SHARD, D_MODEL),
                 grads["conv_w"].reshape(CONV_K, N_DEV, CONV_DIM // N_DEV).transpose(1, 0, 2),
                 _bf(grads["w_attn_out"].reshape(N_DEV, SB_WIDTH // N_DEV, D_MODEL)),
                 _bf(grads["w_ssm_out"].reshape(N_DEV, SSD_WIDTH // N_DEV, D_MODEL)),
                 _bf(grads["w_o"].reshape(N_DEV, D_MODEL // N_DEV, D_MODEL))]
        recv = _pair_exchange(slabs, [], name="exchange_pairs")
        names = ("w_in", "conv_w", "w_attn_out", "w_ssm_out", "w_o")
        sums = [_pair_add(a, b, core, name="pair_add_" + k, tc=LANES if k == "w_in" else None)
                for k, a, b in zip(names, slabs, recv)]
        send_sems, recv_sems, srcs, lands, zero = _split_start("chips", sums, name="exchange_chips_start")
        return zero, (send_sems, recv_sems, srcs, lands)

    loss_part, grad_x, grads, (send_sems, recv_sems, srcs, lands) = _local_step(
        x.reshape(t, D_MODEL), loss_target.reshape(t, D_MODEL), nb, s, g_in.reshape(D_PROJ, D_MODEL),
        conv_full, conv_b, norm_w + o_zero[0:1, 0:1], dt_bias, a_log, d_skip, ssm_norm_w, out_weights,
        final_norm_w.reshape(1, D_MODEL), between=send_weight_grads)

    (p_small,) = _pair_exchange([], [_pack_small(grads)], name="exchange_small")
    srcs, lands = _split_wait("chips", send_sems, recv_sems, srcs, lands, grad_x, name="exchange_chips_wait")
    p_in, p_conv, p_wa, p_ws, p_wo = _own_slab(lands, srcs, chip, False)

    res = {}
    res["w_in"] = [a.T for a in _adamw(p_in, tr_(w_in), tr_(m_w_in), tr_(v_w_in), name="adamw_w_in", tc=LANES)]
    res["conv_w"] = _adamw(p_conv, conv_w[0], m_conv_w[0], v_conv_w[0], name="adamw_conv_w")
    res["w_attn_out"] = _adamw(p_wa, w_attn_out[0], m_w_attn_out[0], v_w_attn_out[0], name="adamw_w_attn_out")
    res["w_ssm_out"] = _adamw(p_ws, w_ssm_out[0], m_w_ssm_out[0], v_w_ssm_out[0], name="adamw_w_ssm_out")
    res["w_o"] = _adamw(p_wo, w_o[0], m_w_o[0], v_w_o[0], name="adamw_w_o")
    sm = _adamw(p_small, _pack_small(weights), _pack_small(moms), _pack_small(vels), name="adamw_small")
    shapes = {k: weights[k].shape for k, _ in SMALL}
    sm = [_unpack_small(v, shapes) for v in sm]
    for k, _ in SMALL:
        res[k] = tuple(part[k] for part in sm)

    order = ("norm_w", "w_in", "conv_w", "conv_b", "dt_bias", "a_log", "d_skip", "ssm_norm_w",
             "w_attn_out", "w_ssm_out", "w_o", "final_norm_w")
    loss = lax.psum(loss_part[0, 0], ("x", "y", "c"))
    outs = [loss, grad_x.reshape(nb, s, D_MODEL)]
    for i in range(4):
        outs += [res[k][i].reshape(weights[k].shape) for k in order]
    return tuple(outs)
```

```python
import functools

import jax
import jax.numpy as jnp
from jax import lax
from jax.experimental import pallas as pl
from jax.experimental.pallas import tpu as pltpu

F32 = jnp.float32
BF16 = jnp.bfloat16

D_MODEL = 1024
SB_WIDTH = 1024
HEAD_DIM = 64
SSD_WIDTH = 2048
SSD_GROUPS = 4
SSD_HPG = 8
SSD_STATE = 128
SSD_CHUNK = 128
CONV_K = 4
CONV_DIM = 3072
N_DT = 32
D_PROJ = 11296
DT_COL0 = 9216
GATE_COL0 = DT_COL0 + N_DT
EPS = 1e-6
N_DEV = 8

ADAM_LR = 0.001
ADAM_B1 = 0.9
ADAM_B2 = 0.999
ADAM_EPS = 1e-08
ADAM_WD = 0.01
ADAM_STEP = 10

LANES = 128
VMEM_LIMIT = 56 * 1024 * 1024

CB_Q, CB_K, CB_V, CB_ZA, CB_ZS, CB_XBC = 0, 8, 16, 24, 32, 48

NT_DIMS = (((1,), (1,)), ((), ()))
TN_DIMS = (((0,), (0,)), ((), ()))
HI = lax.Precision.HIGHEST


def _params(sem):
    return pltpu.CompilerParams(dimension_semantics=sem, vmem_limit_bytes=VMEM_LIMIT)


def _dot(a, b):
    return jnp.dot(a, b, preferred_element_type=F32)


def _dot_nt(a, b):
    return lax.dot_general(a, b, NT_DIMS, preferred_element_type=F32)


def _dot_tn(a, b):
    return lax.dot_general(a, b, TN_DIMS, preferred_element_type=F32)


def _dot_hi(a, b):
    return jnp.dot(a, b, preferred_element_type=F32, precision=HI)


def _bf(a):
    return a.astype(BF16)


def _sigmoid(x):
    return 0.5 + 0.5 * jnp.tanh(0.5 * x)


def _softplus(x):
    return jnp.maximum(x, 0.0) + jnp.log(1.0 + jnp.exp(-jnp.abs(x)))


def _trunc_split(a):
    bits = lax.bitcast_convert_type(a, jnp.uint32) & jnp.uint32(0xFFFF0000)
    hi = lax.bitcast_convert_type(bits, F32)
    return _bf(hi), _bf(a - hi)


MM_TILE = 1024


def _matmul(a, b, *, name, nt=False, ta=False, add=None, out_dtype=F32, tm=MM_TILE, tn=MM_TILE, tk=MM_TILE, n=None):
    k, m = a.shape if ta else a.shape[::-1]
    n = (b.shape[0] if nt else b.shape[1]) if n is None else n
    tm, tn, tk = min(tm, m), min(tn, n), min(tk, k)
    assert m % tm == 0 and n % tn == 0 and k % tk == 0, (name, a.shape, b.shape)
    assert not (ta and nt)
    nk = k // tk
    has_add = add is not None

    def body(*refs):
        if has_add:
            a_ref, b_ref, add_ref, o_ref, acc_ref = refs
        else:
            a_ref, b_ref, o_ref, acc_ref = refs
        kk = pl.program_id(2)

        @pl.when(kk == 0)
        def _():
            acc_ref[...] = jnp.zeros_like(acc_ref)

        av, bv = _bf(a_ref[...]), _bf(b_ref[...])
        acc_ref[...] += _dot_nt(av, bv) if nt else _dot_tn(av, bv) if ta else _dot(av, bv)

        @pl.when(kk == nk - 1)
        def _():
            r = acc_ref[...]
            if has_add:
                r = r + add_ref[...]
            o_ref[...] = r.astype(out_dtype)

    in_specs = [pl.BlockSpec((tk, tm), lambda j, i, kk: (kk, i)) if ta
                else pl.BlockSpec((tm, tk), lambda j, i, kk: (i, kk)),
                pl.BlockSpec((tn, tk), lambda j, i, kk: (j, kk)) if nt
                else pl.BlockSpec((tk, tn), lambda j, i, kk: (kk, j))]
    args = [a, b]
    if has_add:
        in_specs.append(pl.BlockSpec((tm, tn), lambda j, i, kk: (i, j)))
        args.append(add)
    return pl.pallas_call(
        body, grid=(n // tn, m // tm, nk), in_specs=in_specs,
        out_specs=pl.BlockSpec((tm, tn), lambda j, i, kk: (i, j)),
        out_shape=jax.ShapeDtypeStruct((m, n), out_dtype),
        scratch_shapes=[pltpu.VMEM((tm, tn), F32)],
        compiler_params=_params(("parallel", "parallel", "arbitrary")), name=name)(*args)


def _rms_fwd(x, w, *, tm=512):
    t = x.shape[0]

    def body(x_ref, w_ref, h_ref, r_ref):
        xv = x_ref[...]
        r = lax.rsqrt(jnp.mean(xv * xv, axis=-1, keepdims=True) + EPS)
        h_ref[...] = _bf(xv * r * w_ref[...])
        r_ref[...] = r

    row = lambda i: (i, 0)
    return pl.pallas_call(
        body, grid=(t // tm,),
        in_specs=[pl.BlockSpec((tm, D_MODEL), row), pl.BlockSpec((1, D_MODEL), lambda i: (0, 0))],
        out_specs=[pl.BlockSpec((tm, D_MODEL), row), pl.BlockSpec((tm, 1), row)],
        out_shape=[jax.ShapeDtypeStruct((t, D_MODEL), BF16), jax.ShapeDtypeStruct((t, 1), F32)],
        compiler_params=_params(("parallel",)), name="rms_fwd")(x, w)


def _rms_bwd(dh, x, r, w, dres, *, tm=512):
    t = x.shape[0]

    def body(dh_ref, x_ref, r_ref, w_ref, dres_ref, dx_ref, dw_ref):
        @pl.when(pl.program_id(0) == 0)
        def _():
            dw_ref[...] = jnp.zeros_like(dw_ref)

        nrm = x_ref[...] * r_ref[...]
        dhv = dh_ref[...]
        dw_ref[...] += jnp.sum(dhv * nrm, axis=0, keepdims=True)
        dn = dhv * w_ref[...]
        dx = r_ref[...] * (dn - nrm * jnp.mean(dn * nrm, axis=-1, keepdims=True))
        dx_ref[...] = dx + dres_ref[...]

    row = lambda i: (i, 0)
    fix = lambda i: (0, 0)
    return pl.pallas_call(
        body, grid=(t // tm,),
        in_specs=[pl.BlockSpec((tm, D_MODEL), row), pl.BlockSpec((tm, D_MODEL), row),
                  pl.BlockSpec((tm, 1), row), pl.BlockSpec((1, D_MODEL), fix),
                  pl.BlockSpec((tm, D_MODEL), row)],
        out_specs=[pl.BlockSpec((tm, D_MODEL), row), pl.BlockSpec((1, D_MODEL), fix)],
        out_shape=[jax.ShapeDtypeStruct((t, D_MODEL), F32), jax.ShapeDtypeStruct((1, D_MODEL), F32)],
        compiler_params=_params(("arbitrary",)), name="rms_bwd")(dh, x, r, w, dres)


ATT_BLK = 256


def _attn_masks(blk):
    r_io = lax.broadcasted_iota(jnp.int32, (blk, blk), 0)
    c_io = lax.broadcasted_iota(jnp.int32, (blk, blk), 1)
    return r_io, c_io


def _attn_fwd(proj, nb, s):
    blk = min(ATT_BLK, s)
    nq = s // blk
    scale = HEAD_DIM ** -0.5

    def body(q_ref, k_ref, v_ref, o_ref, tot_ref, q0_s, q1_s, kb_s, v0_s, v1_s, z_s, a_s):
        head0 = lax.broadcasted_iota(jnp.int32, (s, LANES), 1) < HEAD_DIM
        lane2 = lax.broadcasted_iota(jnp.int32, (blk, 2), 1)
        r_io, c_io = _attn_masks(blk)
        tri = c_io < r_io
        u_gt = _bf(r_io > c_io)
        u2 = jnp.concatenate([u_gt, u_gt], axis=0)
        qv = q_ref[...] * scale
        q0_s[...] = _bf(jnp.where(head0, qv, 0.0))
        q1_s[...] = _bf(jnp.where(head0, 0.0, qv))
        kb_s[...] = _bf(k_ref[...])
        vv = v_ref[...]
        v0_s[...] = _bf(jnp.where(head0, vv, 0.0))
        v1_s[...] = _bf(jnp.where(head0, 0.0, vv))

        def scores(qhs, kblk, slot):
            k0 = pl.multiple_of(jnp.maximum(kblk, 0) * blk, blk)
            kb = kb_s[pl.ds(k0, blk), :]
            for hh in range(2):
                z_s[slot, hh] = _dot_nt(qhs[hh], kb)

        def weights(slot, crs, diag):
            new = []
            for hh in range(2):
                z = z_s[slot, hh]
                sp = _softplus(z)
                spm = jnp.where(tri, sp, 0.0) if diag else sp
                hi, lo = _trunc_split(spm)
                ps = _dot(jnp.concatenate([hi, lo], axis=1), u2)
                a = jnp.exp(z - (sp + ps + crs[hh]))
                if diag:
                    a = jnp.where(tri, a, 0.0)
                a_s[slot, :, hh * blk:(hh + 1) * blk] = _bf(a)
                new.append(crs[hh] + ps[:, 0:1] + spm[:, 0:1])
            return new

        def apply(acc, slot, k0):
            vcat = jnp.concatenate([v0_s[pl.ds(k0, blk), :], v1_s[pl.ds(k0, blk), :]], axis=0)
            return acc + _dot(a_s[slot], vcat)

        def qblock(qi, _):
            q0 = pl.multiple_of(qi * blk, blk)
            qhs = (q0_s[pl.ds(q0, blk), :], q1_s[pl.ds(q0, blk), :])
            zc = jnp.zeros((blk, 1), F32)
            scores(qhs, qi, 0)
            scores(qhs, qi - 1, 1)
            crs = weights(0, (zc, zc), True)

            def step(j, slot, st):
                acc, c0, c1 = st
                acc = apply(acc, 1 - slot, pl.multiple_of((qi - j + 1) * blk, blk))
                scores(qhs, qi - j - 1, 1 - slot)
                crs = weights(slot, (c0, c1), False)
                return acc, crs[0], crs[1]

            st = (jnp.zeros((blk, LANES), F32), crs[0], crs[1])
            st = lax.fori_loop(0, qi // 2, lambda i, st: step(2 * i + 2, 0, step(2 * i + 1, 1, st)), st)
            st = lax.cond(qi % 2 == 1, lambda st: step(qi, 1, st), lambda st: st, st)
            o_ref[pl.ds(q0, blk), :] = apply(st[0], qi & 1, 0)
            tot_ref[pl.ds(q0, blk), :] = jnp.where(lane2 == 0, st[1], st[2])
            return 0

        lax.fori_loop(0, nq, qblock, 0)

    t = nb * s
    return pl.pallas_call(
        body, grid=(nb, 8),
        in_specs=[pl.BlockSpec((s, LANES), lambda b, p: (b, CB_Q + p)),
                  pl.BlockSpec((s, LANES), lambda b, p: (b, CB_K + p)),
                  pl.BlockSpec((s, LANES), lambda b, p: (b, CB_V + p))],
        out_specs=[pl.BlockSpec((s, LANES), lambda b, p: (b, p)),
                   pl.BlockSpec((None, None, s, 2), lambda b, p: (b, p, 0, 0))],
        out_shape=[jax.ShapeDtypeStruct((t, SB_WIDTH), F32),
                   jax.ShapeDtypeStruct((nb, 8, s, 2), F32)],
        scratch_shapes=[pltpu.VMEM((s, LANES), BF16)] * 5 + [
            pltpu.VMEM((2, 2, blk, blk), F32), pltpu.VMEM((2, blk, 2 * blk), BF16)],
        compiler_params=_params(("parallel", "parallel")), name="attn_fwd")(proj, proj, proj)


def _attn_bwd(proj, do, tot, nb, s):
    blk = min(ATT_BLK, s)
    nq = s // blk
    scale = HEAD_DIM ** -0.5

    def body(q_ref, k_ref, v_ref, do_ref, tot_ref, dq_ref, dk_ref, dv_ref,
             q0_s, q1_s, kb_s, k0_s, k1_s, vb_s, d0_s, d1_s, z_s, da_s, dz_s, a_s, dk_acc, dv_acc):
        head0 = lax.broadcasted_iota(jnp.int32, (s, LANES), 1) < HEAD_DIM
        r_io, c_io = _attn_masks(blk)
        tri = c_io < r_io
        u_le = _bf(r_io <= c_io)
        u2 = jnp.concatenate([u_le, u_le], axis=0)
        u_lt = _bf(r_io < c_io)
        qv = q_ref[...] * scale
        q0_s[...] = _bf(jnp.where(head0, qv, 0.0))
        q1_s[...] = _bf(jnp.where(head0, 0.0, qv))
        kv = k_ref[...]
        kb_s[...] = _bf(kv)
        k0_s[...] = _bf(jnp.where(head0, kv, 0.0))
        k1_s[...] = _bf(jnp.where(head0, 0.0, kv))
        vb_s[...] = _bf(v_ref[...])
        dov = do_ref[...]
        d0_s[...] = _bf(jnp.where(head0, dov, 0.0))
        d1_s[...] = _bf(jnp.where(head0, 0.0, dov))
        dk_acc[...] = jnp.zeros_like(dk_acc)
        dv_acc[...] = jnp.zeros_like(dv_acc)

        def qblock(qi, _):
            q0 = pl.multiple_of(qi * blk, blk)
            qs_ = pl.ds(q0, blk)
            qhs = (q0_s[qs_, :], q1_s[qs_, :])
            dhs = (d0_s[qs_, :], d1_s[qs_, :])
            qcat = jnp.concatenate(qhs, axis=0)
            dcat = jnp.concatenate(dhs, axis=0)
            totf = tot_ref[qs_, :]

            def scores(kblk, slot):
                ks_ = pl.ds(pl.multiple_of(kblk * blk, blk), blk)
                kb, vb = kb_s[ks_, :], vb_s[ks_, :]
                for hh in range(2):
                    z_s[slot, hh] = _dot_nt(qhs[hh], kb)
                    da_s[slot, hh] = _dot_nt(dhs[hh], vb)

            def grads(slot, st, diag):
                new = []
                for hh in range(2):
                    cn, cw = st[2 * hh], st[2 * hh + 1]
                    z = z_s[slot, hh]
                    sp = _softplus(z)
                    lb = z - sp
                    spm = jnp.where(tri, sp, 0.0) if diag else sp
                    hi, lo = _trunc_split(spm)
                    pin = _dot(jnp.concatenate([hi, lo], axis=1), u2)
                    a = jnp.exp(lb + (pin + cn))
                    if diag:
                        a = jnp.where(tri, a, 0.0)
                    w = a * da_s[slot, hh]
                    wex = _dot(_bf(w), u_lt)
                    dz = w - jnp.exp(lb) * (w + (wex + cw))
                    if diag:
                        dz = jnp.where(tri, dz, 0.0)
                    dz_s[slot, hh] = _bf(dz)
                    a_s[slot, hh] = _bf(a)
                    new += [cn + pin[:, blk - 1:blk], cw + wex[:, blk - 1:blk] + w[:, blk - 1:blk]]
                return tuple(new)

            def apply(dq, slot, kblk):
                ks_ = pl.ds(pl.multiple_of(kblk * blk, blk), blk)
                dzs = (dz_s[slot, 0], dz_s[slot, 1])
                dq = dq + _dot(jnp.concatenate(dzs, axis=1), jnp.concatenate([k0_s[ks_, :], k1_s[ks_, :]], axis=0))
                dk_acc[ks_, :] += _dot_tn(jnp.concatenate(dzs, axis=0), qcat)
                dv_acc[ks_, :] += _dot_tn(jnp.concatenate([a_s[slot, 0], a_s[slot, 1]], axis=0), dcat)
                return dq

            def step(j, slot, st, diag):
                dq = apply(st[0], 1 - slot, jnp.maximum(j - 1, 0))
                if not diag:
                    scores(j + 1, 1 - slot)
                return (dq, *grads(slot, st[1:], diag))

            dz_s[1] = jnp.zeros_like(dz_s[1])
            a_s[1] = jnp.zeros_like(a_s[1])
            scores(0, 0)
            zc = jnp.zeros((blk, 1), F32)
            st = (jnp.zeros((blk, LANES), F32), -totf[:, 0:1], zc, -totf[:, 1:2], zc)
            st = lax.fori_loop(
                0, qi // 2, lambda i, st: step(2 * i + 1, 1, step(2 * i, 0, st, False), False), st)
            st = lax.cond(qi % 2 == 1,
                          lambda st: step(qi, 1, step(qi - 1, 0, st, False), True),
                          lambda st: step(qi, 0, st, True), st)
            dq_ref[qs_, :] = _bf(apply(st[0], qi & 1, qi) * scale)
            return 0

        lax.fori_loop(0, nq, qblock, 0)
        dk_ref[...] = _bf(dk_acc[...])
        dv_ref[...] = _bf(dv_acc[...])

    t = nb * s
    hp = lambda b, p: (b, p)
    out = jax.ShapeDtypeStruct((t, SB_WIDTH), BF16)
    return pl.pallas_call(
        body, grid=(nb, 8),
        in_specs=[pl.BlockSpec((s, LANES), lambda b, p: (b, CB_Q + p)),
                  pl.BlockSpec((s, LANES), lambda b, p: (b, CB_K + p)),
                  pl.BlockSpec((s, LANES), lambda b, p: (b, CB_V + p)),
                  pl.BlockSpec((s, LANES), hp),
                  pl.BlockSpec((None, None, s, 2), lambda b, p: (b, p, 0, 0))],
        out_specs=[pl.BlockSpec((s, LANES), hp)] * 3,
        out_shape=[out, out, out],
        scratch_shapes=[pltpu.VMEM((s, LANES), BF16)] * 8 + [
            pltpu.VMEM((2, 2, blk, blk), F32), pltpu.VMEM((2, 2, blk, blk), F32),
            pltpu.VMEM((2, 2, blk, blk), BF16), pltpu.VMEM((2, 2, blk, blk), BF16),
            pltpu.VMEM((s, LANES), F32), pltpu.VMEM((s, LANES), F32)],
        compiler_params=_params(("parallel", "parallel")), name="attn_bwd")(proj, proj, proj, do, tot)


CONV_TC = 256


SUBLANES = 8


def _shift_down(x, k, row):
    r = pltpu.roll(x, k, 0)
    return jnp.concatenate([jnp.where(row[:SUBLANES] >= k, r[:SUBLANES], 0.0), r[SUBLANES:]], axis=0)


def _shift_up(x, k, row, s):
    r = pltpu.roll(x, s - k, 0)
    return jnp.concatenate(
        [r[:s - SUBLANES], jnp.where(row[s - SUBLANES:] < s - k, r[s - SUBLANES:], 0.0)], axis=0)


def _conv_fwd(proj, conv_w, conv_b, nb, s):
    tc = CONV_TC
    cb0 = CB_XBC * LANES // tc

    def body(x_ref, w_ref, b_ref, o_ref):
        x = x_ref[...].astype(F32)
        row = lax.broadcasted_iota(jnp.int32, x.shape, 0)
        y = b_ref[...] + w_ref[CONV_K - 1:CONV_K, :] * x
        for kk in range(1, CONV_K):
            y = y + w_ref[CONV_K - 1 - kk:CONV_K - kk, :] * _shift_down(x, kk, row)
        o_ref[...] = y * _sigmoid(y)

    t = nb * s
    return pl.pallas_call(
        body, grid=(nb, CONV_DIM // tc),
        in_specs=[pl.BlockSpec((s, tc), lambda b, c: (b, cb0 + c)),
                  pl.BlockSpec((CONV_K, tc), lambda b, c: (0, c)),
                  pl.BlockSpec((1, tc), lambda b, c: (0, c))],
        out_specs=pl.BlockSpec((s, tc), lambda b, c: (b, c)),
        out_shape=jax.ShapeDtypeStruct((t, CONV_DIM), F32),
        compiler_params=_params(("parallel", "parallel")), name="conv_fwd")(proj, conv_w, conv_b)


def _conv_bwd(proj, dact, conv_w, conv_b, nb, s, *, col0, name):
    width = dact.shape[1]
    tc = min(CONV_TC, width)
    cb0 = (CB_XBC * LANES + col0) // tc
    wb0 = col0 // tc

    def body(x_ref, d_ref, w_ref, b_ref, dx_ref, dw_ref, db_ref):
        @pl.when(pl.program_id(1) == 0)
        def _():
            dw_ref[...] = jnp.zeros_like(dw_ref)
            db_ref[...] = jnp.zeros_like(db_ref)

        x = x_ref[...].astype(F32)
        row = lax.broadcasted_iota(jnp.int32, x.shape, 0)
        xs = [x] + [_shift_down(x, kk, row) for kk in range(1, CONV_K)]
        y = b_ref[...] + w_ref[CONV_K - 1:CONV_K, :] * x
        for kk in range(1, CONV_K):
            y = y + w_ref[CONV_K - 1 - kk:CONV_K - kk, :] * xs[kk]
        sg = _sigmoid(y)
        dy = d_ref[...] * (sg * (1.0 + y * (1.0 - sg)))
        dx = w_ref[CONV_K - 1:CONV_K, :] * dy
        for kk in range(1, CONV_K):
            dx = dx + w_ref[CONV_K - 1 - kk:CONV_K - kk, :] * _shift_up(dy, kk, row, s)
        dx_ref[...] = _bf(dx)
        db_ref[...] += jnp.sum(dy, axis=0, keepdims=True)
        for kk in range(CONV_K):
            dw_ref[CONV_K - 1 - kk:CONV_K - kk, :] += jnp.sum(dy * xs[kk], axis=0, keepdims=True)

    t = nb * s
    return pl.pallas_call(
        body, grid=(width // tc, nb),
        in_specs=[pl.BlockSpec((s, tc), lambda c, b: (b, cb0 + c)),
                  pl.BlockSpec((s, tc), lambda c, b: (b, c)),
                  pl.BlockSpec((CONV_K, tc), lambda c, b: (0, wb0 + c)),
                  pl.BlockSpec((1, tc), lambda c, b: (0, wb0 + c))],
        out_specs=[pl.BlockSpec((s, tc), lambda c, b: (b, c)),
                   pl.BlockSpec((CONV_K, tc), lambda c, b: (0, c)),
                   pl.BlockSpec((1, tc), lambda c, b: (0, c))],
        out_shape=[jax.ShapeDtypeStruct((t, width), BF16),
                   jax.ShapeDtypeStruct((CONV_K, width), F32),
                   jax.ShapeDtypeStruct((1, width), F32)],
        compiler_params=_params(("parallel", "arbitrary")), name=name)(proj, dact, conv_w, conv_b)


L = SSD_CHUNK
N_PAIR = SSD_HPG // 2


def _pair_lanes(v0, v1, lane):
    return jnp.where(lane < HEAD_DIM, v0, v1)


def _onehot(shape, row_shift, col_shift):
    row = lax.broadcasted_iota(jnp.int32, shape, 0)
    col = lax.broadcasted_iota(jnp.int32, shape, 1)
    return _bf(lax.shift_right_logical(row, row_shift) == lax.shift_right_logical(col, col_shift))


def _ssd_tables():
    sp = _onehot((LANES, N_PAIR * LANES), 0, 6)
    sf = _onehot((LANES, SSD_HPG * LANES), 0, 7)
    return (jnp.concatenate([sp] * 3, axis=0), jnp.concatenate([sf] * 3, axis=0),
            _onehot((N_PAIR * LANES, LANES), 6, 0), _onehot((SSD_HPG * LANES, LANES), 7, 0))


def _split3(a):
    a1 = _bf(a)
    r1 = a - a1.astype(F32)
    a2 = _bf(r1)
    return jnp.concatenate([a1, a2, _bf(r1 - a2.astype(F32))], axis=1)


def _split2(a):
    hi, lo = _trunc_split(a)
    return jnp.concatenate([hi, lo], axis=1)


def _ssd_chunk_common(dtr_ref, bias_ref, alog_ref, r0, spread_pair, spread_full):
    rr = lax.broadcasted_iota(jnp.int32, (L, L), 0)
    cc = lax.broadcasted_iota(jnp.int32, (L, L), 1)
    tri = (cc <= rr).astype(F32)
    raw = dtr_ref[pl.ds(r0, L), :] + bias_ref[...]
    dt = _softplus(raw)
    a_neg = -jnp.exp(alog_ref[...])
    acs = _dot_hi(tri, dt * a_neg)
    acs3 = _split3(acs)
    both = _dot(jnp.concatenate([_split3(dt), acs3], axis=0), spread_pair)
    return raw, dt, a_neg, acs, acs.T, rr, cc, both[:L], both[L:], _dot(acs3, spread_full)


def _ssd_fwd(xbc, dtr, bias, alog, dskip, nb, s):
    nc = s // L
    t = nb * s

    def body(x_ref, b_ref, c_ref, dtr_ref, bias_ref, alog_ref, dsk_ref, y_ref, hs_ref, h_scr):
        lane = lax.broadcasted_iota(jnp.int32, (L, LANES), 1)
        lane1 = lax.broadcasted_iota(jnp.int32, (1, LANES), 1)
        rowc = lax.broadcasted_iota(jnp.int32, (L, 1), 0)
        h_scr[...] = jnp.zeros_like(h_scr)
        dsk = dsk_ref[...]
        spread_pair, spread_full, _, _ = _ssd_tables()

        def chunk(c, _):
            r0 = pl.multiple_of(c * L, L)
            raw, dt, a_neg, acs, acs_t, rr, cc, dt_all, acs_all, acs_full = _ssd_chunk_common(
                dtr_ref, bias_ref, alog_ref, r0, spread_pair, spread_full)
            causal = rr >= cc
            bb = _bf(b_ref[pl.ds(r0, L), :])
            cb = _bf(c_ref[pl.ds(r0, L), :])
            gm = _dot_nt(cb, bb)
            for pr in range(N_PAIR):
                j0, j1 = 2 * pr, 2 * pr + 1
                ms = []
                for j in (j0, j1):
                    seg = acs_full[:, j * LANES:(j + 1) * LANES] - acs_t[j:j + 1, :]
                    ms.append(_bf(gm * jnp.exp(jnp.where(causal, seg, -1e30))))
                x = x_ref[pl.ds(r0, L), pr * LANES:(pr + 1) * LANES]
                dt_p = dt_all[:, pr * LANES:(pr + 1) * LANES]
                acs_p = acs_all[:, pr * LANES:(pr + 1) * LANES]
                last_p = acs_p[L - 1:L, :]
                d_p = _pair_lanes(dsk[:, j0:j0 + 1], dsk[:, j1:j1 + 1], lane1)
                xd = x * dt_p
                xdb = _bf(xd)
                yd = _pair_lanes(_dot(ms[0], xdb), _dot(ms[1], xdb), lane)
                hp = h_scr[pr]
                yo = _dot_nt(cb, _bf(hp)) * jnp.exp(acs_p)
                y_ref[pl.ds(r0, L), pr * LANES:(pr + 1) * LANES] = yd + yo + x * d_p
                hs_ref[c, pr] = hp
                sp = _dot_tn(_bf(xd * jnp.exp(last_p - acs_p)), bb)
                cd = jnp.where(rowc < HEAD_DIM, jnp.exp(acs[L - 1:L, j0:j0 + 1]),
                               jnp.exp(acs[L - 1:L, j1:j1 + 1]))
                h_scr[pr] = hp * cd + sp
            return 0

        lax.fori_loop(0, nc, chunk, 0)

    grp = lambda b, g: (g, 0, 0)
    return pl.pallas_call(
        body, grid=(nb, SSD_GROUPS),
        in_specs=[pl.BlockSpec((s, 4 * LANES), lambda b, g: (b, g)),
                  pl.BlockSpec((s, LANES), lambda b, g: (b, 16 + g)),
                  pl.BlockSpec((s, LANES), lambda b, g: (b, 20 + g)),
                  pl.BlockSpec((None, s, LANES), lambda b, g: (g, b, 0)),
                  pl.BlockSpec((None, 1, LANES), grp), pl.BlockSpec((None, 1, LANES), grp),
                  pl.BlockSpec((None, 1, LANES), grp)],
        out_specs=[pl.BlockSpec((s, 4 * LANES), lambda b, g: (b, g)),
                   pl.BlockSpec((None, None, nc, N_PAIR, L, LANES), lambda b, g: (b, g, 0, 0, 0, 0))],
        out_shape=[jax.ShapeDtypeStruct((t, SSD_WIDTH), F32),
                   jax.ShapeDtypeStruct((nb, SSD_GROUPS, nc, N_PAIR, L, LANES), F32)],
        scratch_shapes=[pltpu.VMEM((N_PAIR, L, LANES), F32)],
        compiler_params=_params(("parallel", "parallel")), name="ssd_fwd")(
            xbc, xbc, xbc, dtr, bias, alog, dskip)


def _ssd_bwd(xbc, dtr, bias, alog, dskip, hs, dy, nb, s):
    nc = s // L
    t = nb * s

    def body(x_ref, b_ref, c_ref, dtr_ref, bias_ref, alog_ref, dsk_ref, hs_ref, dy_ref,
             dx_ref, db_ref, dc_ref, ddtr_ref, dbias_ref, dalog_ref, ddsk_ref, dh_scr):
        lane = lax.broadcasted_iota(jnp.int32, (L, LANES), 1)
        lane1 = lax.broadcasted_iota(jnp.int32, (1, LANES), 1)
        rowc = lax.broadcasted_iota(jnp.int32, (L, 1), 0)
        dh_scr[...] = jnp.zeros_like(dh_scr)
        dsk = dsk_ref[...]

        @pl.when(pl.program_id(1) == 0)
        def _():
            dbias_ref[...] = jnp.zeros_like(dbias_ref)
            dalog_ref[...] = jnp.zeros_like(dalog_ref)
            ddsk_ref[...] = jnp.zeros_like(ddsk_ref)

        spread_pair, spread_full, sum_pair, sum_full = _ssd_tables()
        sum_pair2 = jnp.concatenate([sum_pair] * 2, axis=0)
        sum_full2 = jnp.concatenate([sum_full] * 2, axis=0)

        def chunk(i, _):
            c = nc - 1 - i
            r0 = pl.multiple_of(c * L, L)
            raw, dt, a_neg, acs, acs_t, rr, cc, dt_all, acs_all, acs_full = _ssd_chunk_common(
                dtr_ref, bias_ref, alog_ref, r0, spread_pair, spread_full)
            causal = rr >= cc
            tri_t = (cc >= rr).astype(F32)
            bb = _bf(b_ref[pl.ds(r0, L), :])
            cb = _bf(c_ref[pl.ds(r0, L), :])
            gm = _dot_nt(cb, bb)
            dg = jnp.zeros((L, L), F32)
            dbacc = jnp.zeros((L, LANES), F32)
            dcacc = jnp.zeros((L, LANES), F32)
            dacs_t = jnp.zeros((L, L), F32)
            ta_all, te_all, qm_all, tc_all, td_all, dcd_all = [], [], [], [], [], []
            for pr in range(N_PAIR):
                js = (2 * pr, 2 * pr + 1)
                lms, mfs = [], []
                for j in js:
                    seg = acs_full[:, j * LANES:(j + 1) * LANES] - acs_t[j:j + 1, :]
                    lm = jnp.exp(jnp.where(causal, seg, -1e30))
                    lms.append(lm)
                    mfs.append(gm * lm)
                x = x_ref[pl.ds(r0, L), pr * LANES:(pr + 1) * LANES]
                dyv = dy_ref[pl.ds(r0, L), pr * LANES:(pr + 1) * LANES]
                dt_p = dt_all[:, pr * LANES:(pr + 1) * LANES]
                acs_p = acs_all[:, pr * LANES:(pr + 1) * LANES]
                last_p = acs_p[L - 1:L, :]
                d_p = _pair_lanes(dsk[:, js[0]:js[0] + 1], dsk[:, js[1]:js[1] + 1], lane1)
                e_p = jnp.exp(acs_p)
                dte_p = jnp.exp(last_p - acs_p)
                xd = x * dt_p
                xdb = _bf(xd)
                hp = hs_ref[c, pr]
                hb = _bf(hp)
                dye = _bf(dyv * e_p)
                yo = _dot_nt(cb, hb) * e_p
                dcacc = dcacc + _dot(dye, hb)
                dh_out = _dot_tn(dye, cb)
                dyb = _bf(dyv)
                dxd = jnp.zeros((L, LANES), F32)
                for hh in range(2):
                    hm = (lane >= HEAD_DIM) if hh else (lane < HEAD_DIM)
                    dm = _dot_nt(_bf(jnp.where(hm, dyv, 0.0)), xdb)
                    dxd = dxd + jnp.where(hm, _dot_tn(_bf(mfs[hh]), dyb), 0.0)
                    dg = dg + dm * lms[hh]
                    qm = dm * mfs[hh]
                    qm_all.append(qm)
                    dacs_t = dacs_t - jnp.where(rr == js[hh], jnp.sum(qm, axis=0, keepdims=True), 0.0)
                ds = dh_scr[pr]
                dsb = _bf(ds)
                xdd = xd * dte_p
                dxdd = _dot_nt(bb, dsb)
                dbacc = dbacc + _dot(_bf(xdd), dsb)
                dxd = dxd + dxdd * dte_p
                cd = jnp.where(rowc < HEAD_DIM, jnp.exp(acs[L - 1:L, js[0]:js[0] + 1]),
                               jnp.exp(acs[L - 1:L, js[1]:js[1] + 1]))
                prod = ds * hp
                dcd_all += [jnp.sum(prod[:HEAD_DIM], axis=0, keepdims=True),
                            jnp.sum(prod[HEAD_DIM:], axis=0, keepdims=True)]
                dh_scr[pr] = dh_out + ds * cd
                tcv = dxdd * xdd
                ta_all.append(dyv * yo - tcv)
                tc_all.append(jnp.sum(tcv, axis=0, keepdims=True))
                te_all.append(dxd * x)
                td_all.append(jnp.sum(dyv * x, axis=0, keepdims=True))
                dx_ref[pl.ds(r0, L), pr * LANES:(pr + 1) * LANES] = dxd * dt_p + dyv * d_p
            dgb = _bf(dg)
            dc_ref[pl.ds(r0, L), :] = dcacc + _dot(dgb, bb)
            db_ref[pl.ds(r0, L), :] = dbacc + _dot_tn(dgb, cb)
            last_add = jnp.zeros((1, LANES), F32)
            dd_add = jnp.zeros((1, LANES), F32)
            cd_row = jnp.exp(acs[L - 1:L, :])
            for j in range(SSD_HPG):
                hm = (lane1 >= HEAD_DIM) if j % 2 else (lane1 < HEAD_DIM)
                tcj = jnp.sum(jnp.where(hm, tc_all[j // 2], 0.0), axis=1, keepdims=True)
                tdj = jnp.sum(jnp.where(hm, td_all[j // 2], 0.0), axis=1, keepdims=True)
                dcdj = jnp.sum(dcd_all[j], axis=1, keepdims=True)
                last_add = last_add + jnp.where(lane1 == j, tcj + dcdj * cd_row, 0.0)
                dd_add = dd_add + jnp.where(lane1 == j, tdj, 0.0)
            dacs = (_dot(_split2(jnp.concatenate(ta_all, axis=1)), sum_pair2)
                    + _dot(_split2(jnp.concatenate(qm_all, axis=1)), sum_full2)
                    + dacs_t.T + jnp.where(rowc == L - 1, last_add, 0.0))
            ddta = _dot_hi(tri_t, dacs)
            ddt = _dot(_bf(jnp.concatenate(te_all, axis=1)), sum_pair) + ddta * a_neg
            ddraw = ddt * _sigmoid(raw)
            ddtr_ref[pl.ds(r0, L), :] = ddraw
            dbias_ref[...] += jnp.sum(ddraw, axis=0, keepdims=True)
            dalog_ref[...] += jnp.sum(ddta * dt, axis=0, keepdims=True) * a_neg
            ddsk_ref[...] += dd_add
            return 0

        lax.fori_loop(0, nc, chunk, 0)

    grp = lambda g, b: (g, 0, 0)
    small = jax.ShapeDtypeStruct((SSD_GROUPS, 1, LANES), F32)
    return pl.pallas_call(
        body, grid=(SSD_GROUPS, nb),
        in_specs=[pl.BlockSpec((s, 4 * LANES), lambda g, b: (b, g)),
                  pl.BlockSpec((s, LANES), lambda g, b: (b, 16 + g)),
                  pl.BlockSpec((s, LANES), lambda g, b: (b, 20 + g)),
                  pl.BlockSpec((None, s, LANES), lambda g, b: (g, b, 0)),
                  pl.BlockSpec((None, 1, LANES), grp), pl.BlockSpec((None, 1, LANES), grp),
                  pl.BlockSpec((None, 1, LANES), grp),
                  pl.BlockSpec((None, None, nc, N_PAIR, L, LANES), lambda g, b: (b, g, 0, 0, 0, 0)),
                  pl.BlockSpec((s, 4 * LANES), lambda g, b: (b, g))],
        out_specs=[pl.BlockSpec((s, 4 * LANES), lambda g, b: (b, g)),
                   pl.BlockSpec((s, LANES), lambda g, b: (b, g)),
                   pl.BlockSpec((s, LANES), lambda g, b: (b, g)),
                   pl.BlockSpec((None, s, LANES), lambda g, b: (g, b, 0)),
                   pl.BlockSpec((None, 1, LANES), grp), pl.BlockSpec((None, 1, LANES), grp),
                   pl.BlockSpec((None, 1, LANES), grp)],
        out_shape=[jax.ShapeDtypeStruct((t, SSD_WIDTH), F32),
                   jax.ShapeDtypeStruct((t, SSD_GROUPS * SSD_STATE), F32),
                   jax.ShapeDtypeStruct((t, SSD_GROUPS * SSD_STATE), F32),
                   jax.ShapeDtypeStruct((SSD_GROUPS, t, LANES), F32), small, small, small],
        scratch_shapes=[pltpu.VMEM((N_PAIR, L, LANES), F32)],
        compiler_params=_params(("parallel", "arbitrary")), name="ssd_bwd")(
            xbc, xbc, xbc, dtr, bias, alog, dskip, hs, dy)


GN = SSD_WIDTH // SSD_GROUPS


def _post_fwd(o, proj, y, nw, *, tm=256):
    t = o.shape[0]

    def body(o_ref, za_ref, y_ref, zs_ref, nw_ref, ya_ref, ys_ref):
        za = za_ref[...].astype(F32)
        ya_ref[...] = _bf(o_ref[...] * (za * _sigmoid(za)))
        zs = zs_ref[...].astype(F32)
        yz = y_ref[...] * (zs * _sigmoid(zs))
        for g in range(SSD_GROUPS):
            sl = slice(g * GN, (g + 1) * GN)
            v = yz[:, sl]
            r = lax.rsqrt(jnp.mean(v * v, axis=-1, keepdims=True) + EPS)
            ys_ref[:, sl] = _bf(v * r * nw_ref[:, sl])

    return pl.pallas_call(
        body, grid=(t // tm,),
        in_specs=[pl.BlockSpec((tm, SB_WIDTH), lambda i: (i, 0)),
                  pl.BlockSpec((tm, SB_WIDTH), lambda i: (i, CB_ZA * LANES // SB_WIDTH)),
                  pl.BlockSpec((tm, SSD_WIDTH), lambda i: (i, 0)),
                  pl.BlockSpec((tm, SSD_WIDTH), lambda i: (i, CB_ZS * LANES // SSD_WIDTH)),
                  pl.BlockSpec((1, SSD_WIDTH), lambda i: (0, 0))],
        out_specs=[pl.BlockSpec((tm, SB_WIDTH), lambda i: (i, 0)),
                   pl.BlockSpec((tm, SSD_WIDTH), lambda i: (i, 0))],
        out_shape=[jax.ShapeDtypeStruct((t, SB_WIDTH), BF16), jax.ShapeDtypeStruct((t, SSD_WIDTH), BF16)],
        compiler_params=_params(("parallel",)), name="post_fwd")(o, proj, y, proj, nw)


def _post_bwd(dya, o, proj, dys, y, nw, *, tm=256):
    t = o.shape[0]

    def body(dya_ref, o_ref, za_ref, dys_ref, y_ref, zs_ref, nw_ref,
             do_ref, dza_ref, dy_ref, dzs_ref, dnw_ref):
        @pl.when(pl.program_id(0) == 0)
        def _():
            dnw_ref[...] = jnp.zeros_like(dnw_ref)

        za = za_ref[...].astype(F32)
        sa = _sigmoid(za)
        d = dya_ref[...]
        do_ref[...] = d * (za * sa)
        dza_ref[...] = _bf(d * o_ref[...] * (sa * (1.0 + za * (1.0 - sa))))
        zs = zs_ref[...].astype(F32)
        ss = _sigmoid(zs)
        silu = zs * ss
        yv = y_ref[...]
        yz = yv * silu
        dv = dys_ref[...]
        for g in range(SSD_GROUPS):
            sl = slice(g * GN, (g + 1) * GN)
            v = yz[:, sl]
            r = lax.rsqrt(jnp.mean(v * v, axis=-1, keepdims=True) + EPS)
            nrm = v * r
            dg = dv[:, sl]
            dnw_ref[:, sl] += jnp.sum(dg * nrm, axis=0, keepdims=True)
            dn = dg * nw_ref[:, sl]
            dyz = r * (dn - nrm * jnp.mean(dn * nrm, axis=-1, keepdims=True))
            dy_ref[:, sl] = dyz * silu[:, sl]
            dzs_ref[:, sl] = _bf(dyz * yv[:, sl] * (ss[:, sl] * (1.0 + zs[:, sl] * (1.0 - ss[:, sl]))))

    a_spec = pl.BlockSpec((tm, SB_WIDTH), lambda i: (i, 0))
    s_spec = pl.BlockSpec((tm, SSD_WIDTH), lambda i: (i, 0))
    w_spec = pl.BlockSpec((1, SSD_WIDTH), lambda i: (0, 0))
    return pl.pallas_call(
        body, grid=(t // tm,),
        in_specs=[a_spec, a_spec, pl.BlockSpec((tm, SB_WIDTH), lambda i: (i, CB_ZA * LANES // SB_WIDTH)),
                  s_spec, s_spec, pl.BlockSpec((tm, SSD_WIDTH), lambda i: (i, CB_ZS * LANES // SSD_WIDTH)),
                  w_spec],
        out_specs=[a_spec, a_spec, s_spec, s_spec, w_spec],
        out_shape=[jax.ShapeDtypeStruct((t, SB_WIDTH), F32), jax.ShapeDtypeStruct((t, SB_WIDTH), BF16),
                   jax.ShapeDtypeStruct((t, SSD_WIDTH), F32), jax.ShapeDtypeStruct((t, SSD_WIDTH), BF16),
                   jax.ShapeDtypeStruct((1, SSD_WIDTH), F32)],
        compiler_params=_params(("arbitrary",)), name="post_bwd")(dya, o, proj, dys, y, proj, nw)


def _merge_fwd(ya, ys, proj_g, *, tm=512):
    t = ya.shape[0]

    def body(ya_ref, ys_ref, ga_ref, gs_ref, m_ref):
        m_ref[...] = _bf(_sigmoid(ga_ref[...].astype(F32)) * ya_ref[...]
                         + _sigmoid(gs_ref[...].astype(F32)) * ys_ref[...])

    spec = pl.BlockSpec((tm, D_MODEL), lambda i: (i, 0))
    return pl.pallas_call(
        body, grid=(t // tm,),
        in_specs=[spec, spec, pl.BlockSpec((tm, D_MODEL), lambda i: (i, 0)),
                  pl.BlockSpec((tm, D_MODEL), lambda i: (i, 1))],
        out_specs=spec, out_shape=jax.ShapeDtypeStruct((t, D_MODEL), BF16),
        compiler_params=_params(("parallel",)), name="merge_fwd")(ya, ys, proj_g, proj_g)


def _merge_bwd(dm, ya, ys, proj_g, *, tm=512):
    t = ya.shape[0]

    def body(dm_ref, ya_ref, ys_ref, ga_ref, gs_ref, dya_ref, dys_ref, dg_ref):
        d = dm_ref[...]
        ga = _sigmoid(ga_ref[...].astype(F32))
        gs = _sigmoid(gs_ref[...].astype(F32))
        dya_ref[...] = _bf(d * ga)
        dys_ref[...] = _bf(d * gs)
        dg_ref[:, :D_MODEL] = _bf(d * ya_ref[...] * (ga * (1.0 - ga)))
        dg_ref[:, D_MODEL:] = _bf(d * ys_ref[...] * (gs * (1.0 - gs)))

    spec = pl.BlockSpec((tm, D_MODEL), lambda i: (i, 0))
    return pl.pallas_call(
        body, grid=(t // tm,),
        in_specs=[spec, spec, spec, pl.BlockSpec((tm, D_MODEL), lambda i: (i, 0)),
                  pl.BlockSpec((tm, D_MODEL), lambda i: (i, 1))],
        out_specs=[spec, spec, pl.BlockSpec((tm, 2 * D_MODEL), lambda i: (i, 0))],
        out_shape=[jax.ShapeDtypeStruct((t, D_MODEL), BF16), jax.ShapeDtypeStruct((t, D_MODEL), BF16),
                   jax.ShapeDtypeStruct((t, 2 * D_MODEL), BF16)],
        compiler_params=_params(("parallel",)), name="merge_bwd")(dm, ya, ys, proj_g, proj_g)


def _final(x, out, wf, target, *, tm=512):
    t = x.shape[0]

    def body(x_ref, o_ref, w_ref, t_ref, loss_ref, dx_ref, dw_ref):
        @pl.when(pl.program_id(0) == 0)
        def _():
            loss_ref[...] = jnp.zeros_like(loss_ref)
            dw_ref[...] = jnp.zeros_like(dw_ref)

        x2 = x_ref[...] + o_ref[...]
        r = lax.rsqrt(jnp.mean(x2 * x2, axis=-1, keepdims=True) + EPS)
        nrm = x2 * r
        e = nrm * w_ref[...] - t_ref[...]
        row_loss = jnp.mean(e * e, axis=-1, keepdims=True)
        loss_ref[...] += 0.5 * jnp.sum(row_loss, axis=0, keepdims=True)
        dyv = e * (1.0 / D_MODEL)
        dw_ref[...] += jnp.sum(dyv * nrm, axis=0, keepdims=True)
        dn = dyv * w_ref[...]
        dx_ref[...] = r * (dn - nrm * jnp.mean(dn * nrm, axis=-1, keepdims=True))

    spec = pl.BlockSpec((tm, D_MODEL), lambda i: (i, 0))
    fix = pl.BlockSpec((1, D_MODEL), lambda i: (0, 0))
    return pl.pallas_call(
        body, grid=(t // tm,),
        in_specs=[spec, spec, fix, spec],
        out_specs=[pl.BlockSpec((1, LANES), lambda i: (0, 0)), spec, fix],
        out_shape=[jax.ShapeDtypeStruct((1, LANES), F32), jax.ShapeDtypeStruct((t, D_MODEL), F32),
                   jax.ShapeDtypeStruct((1, D_MODEL), F32)],
        compiler_params=_params(("arbitrary",)), name="final_loss")(x, out, wf, target)


def _group_lanes(v):
    return jnp.pad(v.reshape(SSD_GROUPS, 1, SSD_HPG), ((0, 0), (0, 0), (0, LANES - SSD_HPG)))


def _ungroup_lanes(v):
    return v[:, :, :SSD_HPG].reshape(1, N_DT)


def _row_slabs(pieces, rows):
    total = sum(p.shape[0] for p in pieces)
    assert total % rows == 0
    slabs = []
    for lo in range(0, total, rows):
        parts, off = [], 0
        for p in pieces:
            a, b = max(lo, off), min(lo + rows, off + p.shape[0])
            if a < b:
                parts.append(p[a - off:b - off])
            off += p.shape[0]
        slabs.append(parts[0] if len(parts) == 1 else jnp.concatenate(parts, axis=0))
    return jnp.stack(slabs)


def _local_step(x, target, nb, s, w_in_t, conv_w, conv_b, norm_w, dt_bias, a_log, d_skip,
                ssm_norm_w, out_weights, final_norm_w, between=None):
    t = nb * s
    w_g_t = w_in_t[GATE_COL0:]
    w_dt_t = jnp.pad(w_in_t[DT_COL0:GATE_COL0], ((0, LANES - N_DT), (0, 0)))
    h, r1 = _rms_fwd(x, norm_w)
    proj = _matmul(h, w_in_t, nt=True, n=DT_COL0, out_dtype=BF16, name="proj_main")
    proj_g = _matmul(h, w_g_t, nt=True, out_dtype=BF16, name="proj_gate")
    dt_raw = _matmul(h, w_dt_t, nt=True, name="proj_dt")
    dtr = jnp.pad(dt_raw[:, :N_DT].reshape(t, SSD_GROUPS, SSD_HPG).transpose(1, 0, 2),
                  ((0, 0), (0, 0), (0, LANES - SSD_HPG)))
    bias_g, alog_g, dsk_g = _group_lanes(dt_bias), _group_lanes(a_log), _group_lanes(d_skip)

    o, tot = _attn_fwd(proj, nb, s)
    xbc = _conv_fwd(proj, conv_w, conv_b, nb, s)
    y, hs = _ssd_fwd(xbc, dtr, bias_g, alog_g, dsk_g, nb, s)
    ya_in, ys_in = _post_fwd(o, proj, y, ssm_norm_w)
    w_attn_out, w_ssm_out, w_o = out_weights(ys_in) if callable(out_weights) else out_weights
    ya = _matmul(ya_in, w_attn_out, name="attn_out")
    ys = _matmul(ys_in, w_ssm_out, name="ssm_out")
    merged = _merge_fwd(ya, ys, proj_g)
    out = _matmul(merged, w_o, name="out_proj")
    loss, dx2, d_final_w = _final(x, out, final_norm_w, target)

    dx2b = _bf(dx2)
    dmerged = _matmul(dx2b, w_o, nt=True, name="d_merged")
    d_w_o = _matmul(merged, dx2b, ta=True, name="d_w_o")
    dya, dys, dgate = _merge_bwd(dmerged, ya, ys, proj_g)
    d_w_attn_out = _matmul(ya_in, dya, ta=True, name="d_w_attn_out")
    d_w_ssm_out = _matmul(ys_in, dys, ta=True, name="d_w_ssm_out")
    dya_in = _matmul(dya, w_attn_out, nt=True, name="d_ya_in")
    dys_in = _matmul(dys, w_ssm_out, nt=True, name="d_ys_in")
    do, dza, dy, dzs, d_ssm_norm_w = _post_bwd(dya_in, o, proj, dys_in, y, ssm_norm_w)
    dq, dk, dv = _attn_bwd(proj, do, tot, nb, s)
    dxs_act, db_act, dc_act, ddtr, dbias_g, dalog_g, ddsk_g = _ssd_bwd(
        xbc, dtr, bias_g, alog_g, dsk_g, hs, dy, nb, s)
    dxs, dwx, dbx = _conv_bwd(proj, dxs_act, conv_w, conv_b, nb, s, col0=0, name="conv_bwd_x")
    dbm, dwb, dbb = _conv_bwd(proj, db_act, conv_w, conv_b, nb, s, col0=SSD_WIDTH, name="conv_bwd_b")
    dcm, dwc, dbc = _conv_bwd(proj, dc_act, conv_w, conv_b, nb, s, col0=SSD_WIDTH + 512, name="conv_bwd_c")
    dproj = jnp.concatenate([dq, dk, dv, dza, dzs, dxs, dbm, dcm], axis=1)
    ddt = jnp.pad(_bf(ddtr[:, :, :SSD_HPG].transpose(1, 0, 2).reshape(t, N_DT)), ((0, 0), (0, LANES - N_DT)))
    wt = dict(ta=True, out_dtype=BF16)
    d_w_in_t = _row_slabs(
        [_matmul(dproj, h, name="d_w_main", **wt), _matmul(ddt, h, name="d_w_dt", **wt)[:N_DT],
         _matmul(dgate, h, name="d_w_gate", **wt)], SHARD)
    grads = dict(
        w_in_t=d_w_in_t,
        conv_w=jnp.concatenate([dwx, dwb, dwc], axis=1), conv_b=jnp.concatenate([dbx, dbb, dbc], axis=1),
        dt_bias=_ungroup_lanes(dbias_g), a_log=_ungroup_lanes(dalog_g), d_skip=_ungroup_lanes(ddsk_g),
        ssm_norm_w=d_ssm_norm_w, w_attn_out=d_w_attn_out, w_ssm_out=d_w_ssm_out, w_o=d_w_o,
        final_norm_w=d_final_w)
    extra = None
    if between is not None:
        zero, extra = between(grads)
        ddt = ddt + _bf(zero[0:1, 0:1])

    dh = _matmul(ddt, w_dt_t, name="d_h_dt")
    dh = _matmul(dgate, w_g_t, add=dh, name="d_h_gate")
    dh = _matmul(dproj, w_in_t, add=dh, name="d_h")
    grad_x, grads["norm_w"] = _rms_bwd(dh, x, r1, norm_w, dx2)
    return loss, grad_x, grads, extra


N_CHIP = N_DEV // 2


def _pair_exchange(scatter, bcast, *, name):
    arrays = list(scatter) + list(bcast)
    ns, n = len(scatter), len(arrays)
    n_sem = ns * N_CHIP + (n - ns) * (N_DEV - 1)

    def body(*refs):
        ins, outs = refs[:n], refs[n:2 * n]
        send_sems, recv_sems, local_sems = refs[2 * n:]
        x, y, c = lax.axis_index("x"), lax.axis_index("y"), lax.axis_index("c")
        me = 4 * x + 2 * y + c
        copies, k = [], 0
        for a in range(ns):
            for q in range(N_CHIP):
                cp = pltpu.make_async_remote_copy(
                    src_ref=ins[a].at[2 * q + 1 - c], dst_ref=outs[a].at[q],
                    send_sem=send_sems.at[k], recv_sem=recv_sems.at[k],
                    device_id=(x, y, 1 - c), device_id_type=pl.DeviceIdType.MESH)
                cp.start()
                copies.append(cp)
                k += 1
        for a in range(ns, n):
            cp = pltpu.make_async_copy(ins[a], outs[a].at[me], local_sems.at[a - ns])
            cp.start()
            copies.append(cp)
            for r in range(1, N_DEV):
                tx = 1 - x if (r >> 2) & 1 else x
                ty = 1 - y if (r >> 1) & 1 else y
                tc = 1 - c if r & 1 else c
                cp = pltpu.make_async_remote_copy(
                    src_ref=ins[a], dst_ref=outs[a].at[me], send_sem=send_sems.at[k], recv_sem=recv_sems.at[k],
                    device_id=(tx, ty, tc), device_id_type=pl.DeviceIdType.MESH)
                cp.start()
                copies.append(cp)
                k += 1
        for cp in copies:
            cp.wait()

    out_shape = [jax.ShapeDtypeStruct((N_CHIP,) + v.shape[1:] if i < ns else (N_DEV,) + v.shape, v.dtype)
                 for i, v in enumerate(arrays)]
    return pl.pallas_call(
        body, in_specs=[pl.BlockSpec(memory_space=pl.ANY)] * n,
        out_specs=[pl.BlockSpec(memory_space=pl.ANY)] * n, out_shape=out_shape,
        scratch_shapes=[pltpu.SemaphoreType.DMA((n_sem,)), pltpu.SemaphoreType.DMA((n_sem,)),
                        pltpu.SemaphoreType.DMA((max(n - ns, 1),))],
        compiler_params=pltpu.CompilerParams(has_side_effects=True), name=name)(*arrays)


def _pair_add(mine, recv, core, *, name, tc=None):
    _, r, c = recv.shape
    tc = c if tc is None else tc
    assert c % tc == 0

    def body(core_ref, m_ref, r_ref, o_ref):
        o_ref[...] = (m_ref[...].astype(F32) + r_ref[...].astype(F32)).astype(o_ref.dtype)

    spec = pl.BlockSpec((None, r, tc), lambda q, i, core_ref: (q, 0, i))
    return pl.pallas_call(
        body, grid_spec=pltpu.PrefetchScalarGridSpec(
            num_scalar_prefetch=1, grid=(N_CHIP, c // tc),
            in_specs=[pl.BlockSpec((None, r, tc), lambda q, i, core_ref: (2 * q + core_ref[0], 0, i)), spec],
            out_specs=spec),
        out_shape=jax.ShapeDtypeStruct(recv.shape, recv.dtype),
        compiler_params=_params(("parallel", "parallel")), name=name)(core, mine, recv)


_HBM = pl.BlockSpec(memory_space=pltpu.HBM)
_SEM = pl.BlockSpec(memory_space=pltpu.SEMAPHORE)
_DATAFLOW = pltpu.SideEffectType.DATAFLOW_SIDE_EFFECTING
_PEERS = {"chips": N_CHIP - 1, "all": N_DEV - 1}


def _split_copies(pattern, srcs, lands, send_sems, recv_sems):
    x, y, c = lax.axis_index("x"), lax.axis_index("y"), lax.axis_index("c")
    per = _PEERS[pattern]
    copies = []
    for a in range(len(srcs)):
        for r in range(1, per + 1):
            if pattern == "chips":
                tx = 1 - x if (r >> 1) & 1 else x
                ty = 1 - y if r & 1 else y
                tc = c
                src, dst = srcs[a].at[2 * tx + ty], lands[a].at[2 * x + y]
            else:
                tx = 1 - x if (r >> 2) & 1 else x
                ty = 1 - y if (r >> 1) & 1 else y
                tc = 1 - c if r & 1 else c
                src, dst = srcs[a], lands[a].at[4 * x + 2 * y + c]
            copies.append(pltpu.make_async_remote_copy(
                src_ref=src, dst_ref=dst, send_sem=send_sems.at[a * per + r - 1],
                recv_sem=recv_sems.at[a * per + r - 1], device_id=(tx, ty, tc), device_id_type=pl.DeviceIdType.MESH))
    return copies


def _split_start(pattern, arrays, *, name):
    n = len(arrays)
    n_sem = n * _PEERS[pattern]

    def body(*refs):
        srcs, lands = refs[:n], refs[n:2 * n]
        send_sems, recv_sems = refs[2 * n], refs[2 * n + 1]
        zero = refs[-1]
        for cp in _split_copies(pattern, srcs, lands, send_sems, recv_sems):
            cp.start()
        zero[...] = jnp.zeros_like(zero)

    src_t = [pltpu.HBM(v.shape, v.dtype) for v in arrays]
    land_t = [pltpu.HBM(v.shape if pattern == "chips" else (N_DEV,) + v.shape, v.dtype) for v in arrays]
    outs = pl.pallas_call(
        body, name=name,
        out_shape=(pltpu.SemaphoreType.DMA((n_sem,)), pltpu.SemaphoreType.DMA((n_sem,)), *src_t, *land_t,
                   jax.ShapeDtypeStruct((8, LANES), F32)),
        in_specs=[_HBM] * (2 * n),
        out_specs=(_SEM, _SEM, *[_HBM] * (2 * n), pl.BlockSpec(memory_space=pltpu.VMEM)),
        input_output_aliases={i: 2 + i for i in range(2 * n)},
        compiler_params=pltpu.CompilerParams(has_side_effects=_DATAFLOW))(
            *[pltpu.with_memory_space_constraint(v, pltpu.HBM) for v in arrays],
            *[pltpu.with_memory_space_constraint(lax.empty(t.shape, t.dtype), pltpu.HBM) for t in land_t])
    return outs[0], outs[1], outs[2:2 + n], outs[2 + n:2 + 2 * n], outs[-1]


def _split_wait(pattern, send_sems, recv_sems, srcs, lands, after, *, name):
    n = len(srcs)

    def body(*refs):
        for cp in _split_copies(pattern, refs[:n], refs[n:2 * n], refs[2 * n], refs[2 * n + 1]):
            cp.wait_send()
            cp.wait_recv()

    outs = pl.pallas_call(
        body, name=name, out_shape=[pltpu.HBM(v.shape, v.dtype) for v in (*srcs, *lands)],
        in_specs=[_HBM] * (2 * n) + [_SEM, _SEM, pl.BlockSpec(memory_space=pl.ANY)],
        out_specs=[_HBM] * (2 * n), input_output_aliases={i: i for i in range(2 * n)},
        compiler_params=pltpu.CompilerParams(has_side_effects=_DATAFLOW))(
            *srcs, *lands, send_sems, recv_sems, after)
    return outs[:n], outs[n:]


def _own_slab(lands, srcs, index, whole):
    out = []
    for land, src in zip(lands, srcs):
        own = src[None] if whole else lax.dynamic_slice(src, (index, 0, 0), (1,) + src.shape[1:])
        out.append(lax.dynamic_update_slice(land, own, (index, 0, 0)))
    return out


def _gather(arrays, *, name):
    n = len(arrays)
    per = N_DEV - 1

    def body(*refs):
        ins, outs = refs[:n], refs[n:2 * n]
        send_sems, recv_sems, local_sems = refs[2 * n:]
        x, y, c = lax.axis_index("x"), lax.axis_index("y"), lax.axis_index("c")
        me, sibling = (x, y, c), (x, y, 1 - c)
        chips = [(1 - x, y), (x, 1 - y), (1 - x, 1 - y)]

        def copy(a, k, block, to, src=None):
            rows = outs[a].at[4 * block[0] + 2 * block[1] + block[2]]
            return pltpu.make_async_remote_copy(
                src_ref=rows if src is None else src, dst_ref=rows,
                send_sem=send_sems.at[a * per + k], recv_sem=recv_sems.at[a * per + k],
                device_id=to, device_id_type=pl.DeviceIdType.MESH)

        started = []
        for a in range(n):
            mine = pltpu.make_async_copy(ins[a], outs[a].at[4 * x + 2 * y + c], local_sems.at[a])
            mine.start()
            started.append(mine)
        for a in range(n):
            first = [copy(a, 1 + j, me, (*chip, c), src=ins[a]) for j, chip in enumerate(chips)]
            first.append(copy(a, 0, me, sibling, src=ins[a]))
            for cp in first:
                cp.start()
            started += first
        for j, chip in enumerate(chips):
            for a in range(n):
                copy(a, 1 + j, (*chip, c), me).wait_recv()
                passed = copy(a, 4 + j, (*chip, c), sibling)
                passed.start()
                started.append(passed)
        for a in range(n):
            copy(a, 0, sibling, me).wait_recv()
            for j, chip in enumerate(chips):
                copy(a, 4 + j, (*chip, 1 - c), me).wait_recv()
        for cp in started[:n]:
            cp.wait()
        for cp in started[n:]:
            cp.wait_send()

    return pl.pallas_call(
        body, in_specs=[pl.BlockSpec(memory_space=pl.ANY)] * n,
        out_specs=[pl.BlockSpec(memory_space=pl.ANY)] * n,
        out_shape=[jax.ShapeDtypeStruct((N_DEV,) + v.shape, v.dtype) for v in arrays],
        scratch_shapes=[pltpu.SemaphoreType.DMA((n * per,)), pltpu.SemaphoreType.DMA((n * per,)),
                        pltpu.SemaphoreType.DMA((n,))],
        compiler_params=pltpu.CompilerParams(has_side_effects=True), name=name)(*arrays)


def _adamw(parts, w, m, v, *, name, tc=None):
    r, c = w.shape
    n_parts = parts.shape[0]
    tc = c if tc is None else tc
    assert c % tc == 0

    def body(p_ref, w_ref, m_ref, v_ref, g_ref, d_ref, nm_ref, nv_ref):
        g = p_ref[0].astype(F32)
        for d in range(1, n_parts):
            g = g + p_ref[d].astype(F32)
        mm = ADAM_B1 * m_ref[...] + (1.0 - ADAM_B1) * g
        vv = ADAM_B2 * v_ref[...] + (1.0 - ADAM_B2) * (g * g)
        m_hat = mm / (1.0 - ADAM_B1 ** ADAM_STEP)
        v_hat = vv / (1.0 - ADAM_B2 ** ADAM_STEP)
        g_ref[...] = g
        d_ref[...] = -ADAM_LR * (m_hat / (jnp.sqrt(v_hat) + ADAM_EPS) + ADAM_WD * w_ref[...])
        nm_ref[...] = mm
        nv_ref[...] = vv

    spec = pl.BlockSpec((r, tc), lambda i: (0, i))
    out = jax.ShapeDtypeStruct((r, c), F32)
    return pl.pallas_call(
        body, grid=(c // tc,),
        in_specs=[pl.BlockSpec((n_parts, r, tc), lambda i: (0, 0, i)), spec, spec, spec],
        out_specs=[spec] * 4, out_shape=[out] * 4,
        compiler_params=_params(("parallel",)), name=name)(parts, w, m, v)


SMALL = (("norm_w", D_MODEL), ("conv_b", CONV_DIM), ("dt_bias", N_DT), ("a_log", N_DT),
         ("d_skip", N_DT), ("ssm_norm_w", SSD_WIDTH), ("final_norm_w", D_MODEL))
SMALL_ROWS = 64
SHARD = D_PROJ // N_DEV


def _pack_small(vals):
    flat = jnp.concatenate([vals[k].reshape(-1).astype(F32) for k, _ in SMALL])
    return jnp.pad(flat, (0, SMALL_ROWS * LANES - flat.shape[0])).reshape(SMALL_ROWS, LANES)


def _unpack_small(packed, shapes):
    flat = packed.reshape(-1)
    out, off = {}, 0
    for k, size in SMALL:
        out[k] = flat[off:off + size].reshape(shapes[k])
        off += size
    return out


def kernel(x, norm_w, w_in, conv_w, conv_b, dt_bias, a_log, d_skip, ssm_norm_w, w_attn_out, w_ssm_out, w_o, final_norm_w, loss_target, m_norm_w, m_w_in, m_conv_w, m_conv_b, m_dt_bias, m_a_log, m_d_skip, m_ssm_norm_w, m_w_attn_out, m_w_ssm_out, m_w_o, m_final_norm_w, v_norm_w, v_w_in, v_conv_w, v_conv_b, v_dt_bias, v_a_log, v_d_skip, v_ssm_norm_w, v_w_attn_out, v_w_ssm_out, v_w_o, v_final_norm_w):
    nb, s, _ = x.shape
    t = nb * s
    weights = dict(norm_w=norm_w, w_in=w_in, conv_w=conv_w, conv_b=conv_b, dt_bias=dt_bias, a_log=a_log,
                   d_skip=d_skip, ssm_norm_w=ssm_norm_w, w_attn_out=w_attn_out, w_ssm_out=w_ssm_out,
                   w_o=w_o, final_norm_w=final_norm_w)
    moms = dict(norm_w=m_norm_w, w_in=m_w_in, conv_w=m_conv_w, conv_b=m_conv_b, dt_bias=m_dt_bias,
                a_log=m_a_log, d_skip=m_d_skip, ssm_norm_w=m_ssm_norm_w, w_attn_out=m_w_attn_out,
                w_ssm_out=m_w_ssm_out, w_o=m_w_o, final_norm_w=m_final_norm_w)
    vels = dict(norm_w=v_norm_w, w_in=v_w_in, conv_w=v_conv_w, conv_b=v_conv_b, dt_bias=v_dt_bias,
                a_log=v_a_log, d_skip=v_d_skip, ssm_norm_w=v_ssm_norm_w, w_attn_out=v_w_attn_out,
                w_ssm_out=v_w_ssm_out, w_o=v_w_o, final_norm_w=v_final_norm_w)

    tr_ = lambda a: jnp.transpose(a[0])
    g_in, g_conv = _gather([_bf(tr_(w_in)), conv_w[0]], name="gather_weights")
    conv_full = g_conv.transpose(1, 0, 2).reshape(CONV_K, CONV_DIM)
    o_send, o_recv, o_srcs, o_lands, o_zero = _split_start(
        "all", [_bf(w_attn_out[0]), _bf(w_ssm_out[0]), _bf(w_o[0])], name="gather_out_weights_start")

    core = lax.axis_index("c").astype(jnp.int32).reshape(1)
    chip = 2 * lax.axis_index("x") + lax.axis_index("y")
    me = 2 * chip + lax.axis_index("c")

    def out_weights(after):
        srcs, lands = _split_wait("all", o_send, o_recv, o_srcs, o_lands, after, name="gather_out_weights_wait")
        g_wa, g_ws, g_wo = _own_slab(lands, srcs, me, True)
        return (g_wa.reshape(SB_WIDTH, D_MODEL), g_ws.reshape(SSD_WIDTH, D_MODEL), g_wo.reshape(D_MODEL, D_MODEL))

    def send_weight_grads(grads):
        slabs = [grads["w_in_t"],
                 grads["conv_w"].reshape(CONV_K, N_DEV, CONV_DIM // N_DEV).transpose(1, 0, 2),
                 _bf(grads["w_attn_out"].reshape(N_DEV, SB_WIDTH // N_DEV, D_MODEL)),
                 _bf(grads["w_ssm_out"].reshape(N_DEV, SSD_WIDTH // N_DEV, D_MODEL)),
                 _bf(grads["w_o"].reshape(N_DEV, D_MODEL // N_DEV, D_MODEL))]
        recv = _pair_exchange(slabs, [], name="exchange_pairs")
        names = ("w_in", "conv_w", "w_attn_out", "w_ssm_out", "w_o")
        sums = [_pair_add(a, b, core, name="pair_add_" + k, tc=4 * LANES if k == "w_in" else None)
                for k, a, b in zip(names, slabs, recv)]
        send_sems, recv_sems, srcs, lands, zero = _split_start("chips", sums, name="exchange_chips_start")
        return zero, (send_sems, recv_sems, srcs, lands)

    loss_part, grad_x, grads, (send_sems, recv_sems, srcs, lands) = _local_step(
        x.reshape(t, D_MODEL), loss_target.reshape(t, D_MODEL), nb, s, g_in.reshape(D_PROJ, D_MODEL),
        conv_full, conv_b, norm_w + o_zero[0:1, 0:1], dt_bias, a_log, d_skip, ssm_norm_w, out_weights,
        final_norm_w.reshape(1, D_MODEL), between=send_weight_grads)

    (p_small,) = _pair_exchange([], [_pack_small(grads)], name="exchange_small")
    srcs, lands = _split_wait("chips", send_sems, recv_sems, srcs, lands, grad_x, name="exchange_chips_wait")
    p_in, p_conv, p_wa, p_ws, p_wo = _own_slab(lands, srcs, chip, False)

    res = {}
    res["w_in"] = [a.T for a in _adamw(p_in, tr_(w_in), tr_(m_w_in), tr_(v_w_in), name="adamw_w_in", tc=2 * LANES)]
    res["conv_w"] = _adamw(p_conv, conv_w[0], m_conv_w[0], v_conv_w[0], name="adamw_conv_w")
    res["w_attn_out"] = _adamw(p_wa, w_attn_out[0], m_w_attn_out[0], v_w_attn_out[0], name="adamw_w_attn_out")
    res["w_ssm_out"] = _adamw(p_ws, w_ssm_out[0], m_w_ssm_out[0], v_w_ssm_out[0], name="adamw_w_ssm_out")
    res["w_o"] = _adamw(p_wo, w_o[0], m_w_o[0], v_w_o[0], name="adamw_w_o")
    sm = _adamw(p_small, _pack_small(weights), _pack_small(moms), _pack_small(vels), name="adamw_small")
    shapes = {k: weights[k].shape for k, _ in SMALL}
    sm = [_unpack_small(v, shapes) for v in sm]
    for k, _ in SMALL:
        res[k] = tuple(part[k] for part in sm)

    order = ("norm_w", "w_in", "conv_w", "conv_b", "dt_bias", "a_log", "d_skip", "ssm_norm_w",
             "w_attn_out", "w_ssm_out", "w_o", "final_norm_w")
    loss = lax.psum(loss_part[0, 0], ("x", "y", "c"))
    outs = [loss, grad_x.reshape(nb, s, D_MODEL)]
    for i in range(4):
        outs += [res[k][i].reshape(weights[k].shape) for k in order]
    return tuple(outs)
```

```python
import functools

import jax
import jax.numpy as jnp
from jax import lax
from jax.experimental import pallas as pl
from jax.experimental.pallas import tpu as pltpu

F32 = jnp.float32
BF16 = jnp.bfloat16

D_MODEL = 1024
SB_WIDTH = 1024
HEAD_DIM = 64
SSD_WIDTH = 2048
SSD_GROUPS = 4
SSD_HPG = 8
SSD_STATE = 128
SSD_CHUNK = 128
CONV_K = 4
CONV_DIM = 3072
N_DT = 32
D_PROJ = 11296
DT_COL0 = 9216
GATE_COL0 = DT_COL0 + N_DT
EPS = 1e-6
N_DEV = 8

ADAM_LR = 0.001
ADAM_B1 = 0.9
ADAM_B2 = 0.999
ADAM_EPS = 1e-08
ADAM_WD = 0.01
ADAM_STEP = 10

LANES = 128
VMEM_LIMIT = 56 * 1024 * 1024

CB_Q, CB_K, CB_V, CB_ZA, CB_ZS, CB_XBC = 0, 8, 16, 24, 32, 48

NT_DIMS = (((1,), (1,)), ((), ()))
TN_DIMS = (((0,), (0,)), ((), ()))
HI = lax.Precision.HIGHEST


def _params(sem):
    return pltpu.CompilerParams(dimension_semantics=sem, vmem_limit_bytes=VMEM_LIMIT)


def _dot(a, b):
    return jnp.dot(a, b, preferred_element_type=F32)


def _dot_nt(a, b):
    return lax.dot_general(a, b, NT_DIMS, preferred_element_type=F32)


def _dot_tn(a, b):
    return lax.dot_general(a, b, TN_DIMS, preferred_element_type=F32)


def _dot_hi(a, b):
    return jnp.dot(a, b, preferred_element_type=F32, precision=HI)


def _bf(a):
    return a.astype(BF16)


def _sigmoid(x):
    return 0.5 + 0.5 * jnp.tanh(0.5 * x)


def _softplus(x):
    return jnp.maximum(x, 0.0) + jnp.log(1.0 + jnp.exp(-jnp.abs(x)))


def _trunc_split(a):
    bits = lax.bitcast_convert_type(a, jnp.uint32) & jnp.uint32(0xFFFF0000)
    hi = lax.bitcast_convert_type(bits, F32)
    return _bf(hi), _bf(a - hi)


MM_TILE = 1024


def _matmul(a, b, *, name, nt=False, ta=False, add=None, out_dtype=F32, tm=MM_TILE, tn=MM_TILE, tk=MM_TILE, n=None):
    k, m = a.shape if ta else a.shape[::-1]
    n = (b.shape[0] if nt else b.shape[1]) if n is None else n
    tm, tn, tk = min(tm, m), min(tn, n), min(tk, k)
    assert m % tm == 0 and n % tn == 0 and k % tk == 0, (name, a.shape, b.shape)
    assert not (ta and nt)
    nk = k // tk
    has_add = add is not None

    def body(*refs):
        if has_add:
            a_ref, b_ref, add_ref, o_ref, acc_ref = refs
        else:
            a_ref, b_ref, o_ref, acc_ref = refs
        kk = pl.program_id(2)

        @pl.when(kk == 0)
        def _():
            acc_ref[...] = jnp.zeros_like(acc_ref)

        av, bv = _bf(a_ref[...]), _bf(b_ref[...])
        acc_ref[...] += _dot_nt(av, bv) if nt else _dot_tn(av, bv) if ta else _dot(av, bv)

        @pl.when(kk == nk - 1)
        def _():
            r = acc_ref[...]
            if has_add:
                r = r + add_ref[...]
            o_ref[...] = r.astype(out_dtype)

    in_specs = [pl.BlockSpec((tk, tm), lambda j, i, kk: (kk, i)) if ta
                else pl.BlockSpec((tm, tk), lambda j, i, kk: (i, kk)),
                pl.BlockSpec((tn, tk), lambda j, i, kk: (j, kk)) if nt
                else pl.BlockSpec((tk, tn), lambda j, i, kk: (kk, j))]
    args = [a, b]
    if has_add:
        in_specs.append(pl.BlockSpec((tm, tn), lambda j, i, kk: (i, j)))
        args.append(add)
    return pl.pallas_call(
        body, grid=(n // tn, m // tm, nk), in_specs=in_specs,
        out_specs=pl.BlockSpec((tm, tn), lambda j, i, kk: (i, j)),
        out_shape=jax.ShapeDtypeStruct((m, n), out_dtype),
        scratch_shapes=[pltpu.VMEM((tm, tn), F32)],
        compiler_params=_params(("parallel", "parallel", "arbitrary")), name=name)(*args)


def _rms_fwd(x, w, *, tm=512):
    t = x.shape[0]

    def body(x_ref, w_ref, h_ref, r_ref):
        xv = x_ref[...]
        r = lax.rsqrt(jnp.mean(xv * xv, axis=-1, keepdims=True) + EPS)
        h_ref[...] = _bf(xv * r * w_ref[...])
        r_ref[...] = r

    row = lambda i: (i, 0)
    return pl.pallas_call(
        body, grid=(t // tm,),
        in_specs=[pl.BlockSpec((tm, D_MODEL), row), pl.BlockSpec((1, D_MODEL), lambda i: (0, 0))],
        out_specs=[pl.BlockSpec((tm, D_MODEL), row), pl.BlockSpec((tm, 1), row)],
        out_shape=[jax.ShapeDtypeStruct((t, D_MODEL), BF16), jax.ShapeDtypeStruct((t, 1), F32)],
        compiler_params=_params(("parallel",)), name="rms_fwd")(x, w)


def _rms_bwd(dh, x, r, w, dres, *, tm=512):
    t = x.shape[0]

    def body(dh_ref, x_ref, r_ref, w_ref, dres_ref, dx_ref, dw_ref):
        @pl.when(pl.program_id(0) == 0)
        def _():
            dw_ref[...] = jnp.zeros_like(dw_ref)

        nrm = x_ref[...] * r_ref[...]
        dhv = dh_ref[...]
        dw_ref[...] += jnp.sum(dhv * nrm, axis=0, keepdims=True)
        dn = dhv * w_ref[...]
        dx = r_ref[...] * (dn - nrm * jnp.mean(dn * nrm, axis=-1, keepdims=True))
        dx_ref[...] = dx + dres_ref[...]

    row = lambda i: (i, 0)
    fix = lambda i: (0, 0)
    return pl.pallas_call(
        body, grid=(t // tm,),
        in_specs=[pl.BlockSpec((tm, D_MODEL), row), pl.BlockSpec((tm, D_MODEL), row),
                  pl.BlockSpec((tm, 1), row), pl.BlockSpec((1, D_MODEL), fix),
                  pl.BlockSpec((tm, D_MODEL), row)],
        out_specs=[pl.BlockSpec((tm, D_MODEL), row), pl.BlockSpec((1, D_MODEL), fix)],
        out_shape=[jax.ShapeDtypeStruct((t, D_MODEL), F32), jax.ShapeDtypeStruct((1, D_MODEL), F32)],
        compiler_params=_params(("arbitrary",)), name="rms_bwd")(dh, x, r, w, dres)


ATT_BLK = 256


def _attn_masks(blk):
    r_io = lax.broadcasted_iota(jnp.int32, (blk, blk), 0)
    c_io = lax.broadcasted_iota(jnp.int32, (blk, blk), 1)
    return r_io, c_io


def _attn_fwd(proj, nb, s):
    blk = min(ATT_BLK, s)
    nq = s // blk
    scale = HEAD_DIM ** -0.5

    def body(q_ref, k_ref, v_ref, o_ref, tot_ref, q0_s, q1_s, kb_s, v0_s, v1_s, z_s, a_s):
        head0 = lax.broadcasted_iota(jnp.int32, (s, LANES), 1) < HEAD_DIM
        lane2 = lax.broadcasted_iota(jnp.int32, (blk, 2), 1)
        r_io, c_io = _attn_masks(blk)
        tri = c_io < r_io
        tri2 = jnp.concatenate([tri, tri], axis=0)
        u_gt = _bf(r_io > c_io)
        u2 = jnp.concatenate([u_gt, u_gt], axis=0)
        qv = q_ref[...] * scale
        q0_s[...] = _bf(jnp.where(head0, qv, 0.0))
        q1_s[...] = _bf(jnp.where(head0, 0.0, qv))
        kb_s[...] = _bf(k_ref[...])
        vv = v_ref[...]
        v0_s[...] = _bf(jnp.where(head0, vv, 0.0))
        v1_s[...] = _bf(jnp.where(head0, 0.0, vv))

        def scores(qcat, kblk, slot):
            k0 = pl.multiple_of(jnp.maximum(kblk, 0) * blk, blk)
            z_s[slot] = _dot_nt(qcat, kb_s[pl.ds(k0, blk), :])

        def weights(slot, cr, diag):
            z = z_s[slot]
            sp = _softplus(z)
            spm = jnp.where(tri2, sp, 0.0) if diag else sp
            hi, lo = _trunc_split(spm)
            ps = _dot(jnp.concatenate([hi, lo], axis=1), u2)
            a = jnp.exp(z - (sp + ps + cr))
            if diag:
                a = jnp.where(tri2, a, 0.0)
            a_s[slot] = _bf(a)
            return cr + ps[:, 0:1] + spm[:, 0:1]

        def apply(acc, slot, k0):
            a = a_s[slot]
            vcat = jnp.concatenate([v0_s[pl.ds(k0, blk), :], v1_s[pl.ds(k0, blk), :]], axis=0)
            return acc + _dot(jnp.concatenate([a[:blk], a[blk:]], axis=1), vcat)

        def qblock(qi, _):
            q0 = pl.multiple_of(qi * blk, blk)
            qcat = jnp.concatenate([q0_s[pl.ds(q0, blk), :], q1_s[pl.ds(q0, blk), :]], axis=0)
            scores(qcat, qi, 0)
            scores(qcat, qi - 1, 1)
            cr = weights(0, jnp.zeros((2 * blk, 1), F32), True)

            def step(j, slot, st):
                acc = apply(st[0], 1 - slot, pl.multiple_of((qi - j + 1) * blk, blk))
                scores(qcat, qi - j - 1, 1 - slot)
                return acc, weights(slot, st[1], False)

            st = (jnp.zeros((blk, LANES), F32), cr)
            st = lax.fori_loop(0, qi // 2, lambda i, st: step(2 * i + 2, 0, step(2 * i + 1, 1, st)), st)
            st = lax.cond(qi % 2 == 1, lambda st: step(qi, 1, st), lambda st: st, st)
            o_ref[pl.ds(q0, blk), :] = apply(st[0], qi & 1, 0)
            tot_ref[pl.ds(q0, blk), :] = jnp.where(lane2 == 0, st[1][:blk], st[1][blk:])
            return 0

        lax.fori_loop(0, nq, qblock, 0)

    t = nb * s
    return pl.pallas_call(
        body, grid=(nb, 8),
        in_specs=[pl.BlockSpec((s, LANES), lambda b, p: (b, CB_Q + p)),
                  pl.BlockSpec((s, LANES), lambda b, p: (b, CB_K + p)),
                  pl.BlockSpec((s, LANES), lambda b, p: (b, CB_V + p))],
        out_specs=[pl.BlockSpec((s, LANES), lambda b, p: (b, p)),
                   pl.BlockSpec((None, None, s, 2), lambda b, p: (b, p, 0, 0))],
        out_shape=[jax.ShapeDtypeStruct((t, SB_WIDTH), F32),
                   jax.ShapeDtypeStruct((nb, 8, s, 2), F32)],
        scratch_shapes=[pltpu.VMEM((s, LANES), BF16)] * 5 + [
            pltpu.VMEM((2, 2 * blk, blk), F32), pltpu.VMEM((2, 2 * blk, blk), BF16)],
        compiler_params=_params(("parallel", "parallel")), name="attn_fwd")(proj, proj, proj)


def _attn_bwd(proj, do, tot, nb, s):
    blk = min(ATT_BLK, s)
    nq = s // blk
    scale = HEAD_DIM ** -0.5

    def body(q_ref, k_ref, v_ref, do_ref, tot_ref, dq_ref, dk_ref, dv_ref,
             q0_s, q1_s, kb_s, k0_s, k1_s, vb_s, d0_s, d1_s, z_s, da_s, dz_s, a_s, dk_acc, dv_acc):
        head0 = lax.broadcasted_iota(jnp.int32, (s, LANES), 1) < HEAD_DIM
        r_io, c_io = _attn_masks(blk)
        tri = c_io < r_io
        tri2 = jnp.concatenate([tri, tri], axis=0)
        u_le = _bf(r_io <= c_io)
        u2 = jnp.concatenate([u_le, u_le], axis=0)
        u_lt = _bf(r_io < c_io)
        qv = q_ref[...] * scale
        q0_s[...] = _bf(jnp.where(head0, qv, 0.0))
        q1_s[...] = _bf(jnp.where(head0, 0.0, qv))
        kv = k_ref[...]
        kb_s[...] = _bf(kv)
        k0_s[...] = _bf(jnp.where(head0, kv, 0.0))
        k1_s[...] = _bf(jnp.where(head0, 0.0, kv))
        vb_s[...] = _bf(v_ref[...])
        dov = do_ref[...]
        d0_s[...] = _bf(jnp.where(head0, dov, 0.0))
        d1_s[...] = _bf(jnp.where(head0, 0.0, dov))
        dk_acc[...] = jnp.zeros_like(dk_acc)
        dv_acc[...] = jnp.zeros_like(dv_acc)

        def qblock(qi, _):
            q0 = pl.multiple_of(qi * blk, blk)
            qs_ = pl.ds(q0, blk)
            qhs = (q0_s[qs_, :], q1_s[qs_, :])
            dhs = (d0_s[qs_, :], d1_s[qs_, :])
            qcat = jnp.concatenate(qhs, axis=0)
            dcat = jnp.concatenate(dhs, axis=0)
            totf = tot_ref[qs_, :]

            def scores(kblk, slot):
                ks_ = pl.ds(pl.multiple_of(kblk * blk, blk), blk)
                z_s[slot] = _dot_nt(qcat, kb_s[ks_, :])
                da_s[slot] = _dot_nt(dcat, vb_s[ks_, :])

            def grads(slot, st, diag):
                cn, cw = st
                z = z_s[slot]
                sp = _softplus(z)
                lb = z - sp
                spm = jnp.where(tri2, sp, 0.0) if diag else sp
                hi, lo = _trunc_split(spm)
                pin = _dot(jnp.concatenate([hi, lo], axis=1), u2)
                a = jnp.exp(lb + (pin + cn))
                if diag:
                    a = jnp.where(tri2, a, 0.0)
                w = a * da_s[slot]
                wex = _dot(_bf(w), u_lt)
                dz = w - jnp.exp(lb) * (w + (wex + cw))
                if diag:
                    dz = jnp.where(tri2, dz, 0.0)
                dz_s[slot] = _bf(dz)
                a_s[slot] = _bf(a)
                return cn + pin[:, blk - 1:blk], cw + wex[:, blk - 1:blk] + w[:, blk - 1:blk]

            def apply(dq, slot, kblk):
                ks_ = pl.ds(pl.multiple_of(kblk * blk, blk), blk)
                dz = dz_s[slot]
                dq = dq + _dot(jnp.concatenate([dz[:blk], dz[blk:]], axis=1),
                               jnp.concatenate([k0_s[ks_, :], k1_s[ks_, :]], axis=0))
                dk_acc[ks_, :] += _dot_tn(dz, qcat)
                dv_acc[ks_, :] += _dot_tn(a_s[slot], dcat)
                return dq

            def step(j, slot, st, diag):
                dq = apply(st[0], 1 - slot, jnp.maximum(j - 1, 0))
                if not diag:
                    scores(j + 1, 1 - slot)
                return (dq, *grads(slot, st[1:], diag))

            dz_s[1] = jnp.zeros_like(dz_s[1])
            a_s[1] = jnp.zeros_like(a_s[1])
            scores(0, 0)
            st = (jnp.zeros((blk, LANES), F32), -jnp.concatenate([totf[:, 0:1], totf[:, 1:2]], axis=0),
                  jnp.zeros((2 * blk, 1), F32))
            st = lax.fori_loop(
                0, qi // 2, lambda i, st: step(2 * i + 1, 1, step(2 * i, 0, st, False), False), st)
            st = lax.cond(qi % 2 == 1,
                          lambda st: step(qi, 1, step(qi - 1, 0, st, False), True),
                          lambda st: step(qi, 0, st, True), st)
            dq_ref[qs_, :] = _bf(apply(st[0], qi & 1, qi) * scale)
            return 0

        lax.fori_loop(0, nq, qblock, 0)
        dk_ref[...] = _bf(dk_acc[...])
        dv_ref[...] = _bf(dv_acc[...])

    t = nb * s
    hp = lambda b, p: (b, p)
    out = jax.ShapeDtypeStruct((t, SB_WIDTH), BF16)
    return pl.pallas_call(
        body, grid=(nb, 8),
        in_specs=[pl.BlockSpec((s, LANES), lambda b, p: (b, CB_Q + p)),
                  pl.BlockSpec((s, LANES), lambda b, p: (b, CB_K + p)),
                  pl.BlockSpec((s, LANES), lambda b, p: (b, CB_V + p)),
                  pl.BlockSpec((s, LANES), hp),
                  pl.BlockSpec((None, None, s, 2), lambda b, p: (b, p, 0, 0))],
        out_specs=[pl.BlockSpec((s, LANES), hp)] * 3,
        out_shape=[out, out, out],
        scratch_shapes=[pltpu.VMEM((s, LANES), BF16)] * 8 + [
            pltpu.VMEM((2, 2 * blk, blk), F32), pltpu.VMEM((2, 2 * blk, blk), F32),
            pltpu.VMEM((2, 2 * blk, blk), BF16), pltpu.VMEM((2, 2 * blk, blk), BF16),
            pltpu.VMEM((s, LANES), F32), pltpu.VMEM((s, LANES), F32)],
        compiler_params=_params(("parallel", "parallel")), name="attn_bwd")(proj, proj, proj, do, tot)


CONV_TC = 256
CONV_ROWS = 64
HALO = 8


def _conv_rows(ws, w_ref, b_ref, n):
    taps = [ws[HALO - kk:HALO - kk + n, :] for kk in range(CONV_K)]
    y = b_ref[...] + w_ref[CONV_K - 1:CONV_K, :] * taps[0]
    for kk in range(1, CONV_K):
        y = y + w_ref[CONV_K - 1 - kk:CONV_K - kk, :] * taps[kk]
    return y, taps


def _conv_fwd(proj, conv_w, conv_b, nb, s):
    tc, rows = CONV_TC, min(CONV_ROWS, s)
    cb0 = CB_XBC * LANES // tc

    def body(x_ref, w_ref, b_ref, o_ref, xp, ws):
        xp[0:HALO, :] = jnp.zeros((HALO, tc), F32)
        xp[HALO:HALO + s, :] = x_ref[...].astype(F32)

        def chunk(i, _):
            r0 = pl.multiple_of(i * rows, rows)
            ws[...] = xp[pl.ds(r0, rows + HALO), :]
            y, _ = _conv_rows(ws, w_ref, b_ref, rows)
            o_ref[pl.ds(r0, rows), :] = y * _sigmoid(y)
            return 0

        lax.fori_loop(0, s // rows, chunk, 0)

    t = nb * s
    return pl.pallas_call(
        body, grid=(nb, CONV_DIM // tc),
        in_specs=[pl.BlockSpec((s, tc), lambda b, c: (b, cb0 + c)),
                  pl.BlockSpec((CONV_K, tc), lambda b, c: (0, c)),
                  pl.BlockSpec((1, tc), lambda b, c: (0, c))],
        out_specs=pl.BlockSpec((s, tc), lambda b, c: (b, c)),
        out_shape=jax.ShapeDtypeStruct((t, CONV_DIM), F32),
        scratch_shapes=[pltpu.VMEM((s + HALO, tc), F32), pltpu.VMEM((rows + HALO, tc), F32)],
        compiler_params=_params(("parallel", "parallel")), name="conv_fwd")(proj, conv_w, conv_b)


def _conv_bwd(proj, dact, conv_w, conv_b, nb, s, *, col0, name):
    width = dact.shape[1]
    tc, rows = min(CONV_TC, width), min(CONV_ROWS, s)
    cb0 = (CB_XBC * LANES + col0) // tc
    wb0 = col0 // tc

    def body(x_ref, d_ref, w_ref, b_ref, dx_ref, dw_ref, db_ref, xp, dp, dys, ws):
        @pl.when(pl.program_id(1) == 0)
        def _():
            dw_ref[...] = jnp.zeros_like(dw_ref)
            db_ref[...] = jnp.zeros_like(db_ref)

        n = rows + HALO
        pad = jnp.zeros((HALO, tc), F32)
        xp[0:HALO, :] = pad
        xp[HALO:HALO + s, :] = x_ref[...].astype(F32)
        xp[HALO + s:HALO + s + HALO, :] = pad
        dp[0:s, :] = d_ref[...]
        dp[s:s + HALO, :] = pad

        def chunk(i, acc):
            r0 = pl.multiple_of(i * rows, rows)
            ws[...] = xp[pl.ds(r0, n + HALO), :]
            y, taps = _conv_rows(ws, w_ref, b_ref, n)
            sg = _sigmoid(y)
            dy = dp[pl.ds(r0, n), :] * (sg * (1.0 + y * (1.0 - sg)))
            dys[...] = dy
            dx = w_ref[CONV_K - 1:CONV_K, :] * dy[0:rows]
            for kk in range(1, CONV_K):
                dx = dx + w_ref[CONV_K - 1 - kk:CONV_K - kk, :] * dys[kk:kk + rows, :]
            dx_ref[pl.ds(r0, rows), :] = _bf(dx)
            dyc = dy[0:rows]
            new = [acc[0] + jnp.sum(dyc, axis=0, keepdims=True)]
            for kk in range(CONV_K):
                new.append(acc[1 + kk] + jnp.sum(dyc * taps[kk][0:rows], axis=0, keepdims=True))
            return tuple(new)

        acc = lax.fori_loop(0, s // rows, chunk, (jnp.zeros((1, tc), F32),) * (1 + CONV_K))
        db_ref[...] += acc[0]
        for kk in range(CONV_K):
            dw_ref[CONV_K - 1 - kk:CONV_K - kk, :] += acc[1 + kk]

    t = nb * s
    return pl.pallas_call(
        body, grid=(width // tc, nb),
        in_specs=[pl.BlockSpec((s, tc), lambda c, b: (b, cb0 + c)),
                  pl.BlockSpec((s, tc), lambda c, b: (b, c)),
                  pl.BlockSpec((CONV_K, tc), lambda c, b: (0, wb0 + c)),
                  pl.BlockSpec((1, tc), lambda c, b: (0, wb0 + c))],
        out_specs=[pl.BlockSpec((s, tc), lambda c, b: (b, c)),
                   pl.BlockSpec((CONV_K, tc), lambda c, b: (0, c)),
                   pl.BlockSpec((1, tc), lambda c, b: (0, c))],
        out_shape=[jax.ShapeDtypeStruct((t, width), BF16),
                   jax.ShapeDtypeStruct((CONV_K, width), F32),
                   jax.ShapeDtypeStruct((1, width), F32)],
        scratch_shapes=[pltpu.VMEM((s + 2 * HALO, tc), F32), pltpu.VMEM((s + HALO, tc), F32),
                        pltpu.VMEM((rows + HALO, tc), F32), pltpu.VMEM((rows + 2 * HALO, tc), F32)],
        compiler_params=_params(("parallel", "arbitrary")), name=name)(proj, dact, conv_w, conv_b)


L = SSD_CHUNK
N_PAIR = SSD_HPG // 2


def _pair_lanes(v0, v1, lane):
    return jnp.where(lane < HEAD_DIM, v0, v1)


def _onehot(shape, row_shift, col_shift):
    row = lax.broadcasted_iota(jnp.int32, shape, 0)
    col = lax.broadcasted_iota(jnp.int32, shape, 1)
    return _bf(lax.shift_right_logical(row, row_shift) == lax.shift_right_logical(col, col_shift))


def _ssd_tables():
    sp = _onehot((LANES, N_PAIR * LANES), 0, 6)
    sf = _onehot((LANES, SSD_HPG * LANES), 0, 7)
    return (jnp.concatenate([sp] * 3, axis=0), jnp.concatenate([sf] * 3, axis=0),
            _onehot((N_PAIR * LANES, LANES), 6, 0), _onehot((SSD_HPG * LANES, LANES), 7, 0))


def _split3(a):
    a1 = _bf(a)
    r1 = a - a1.astype(F32)
    a2 = _bf(r1)
    return jnp.concatenate([a1, a2, _bf(r1 - a2.astype(F32))], axis=1)


def _split2(a):
    hi, lo = _trunc_split(a)
    return jnp.concatenate([hi, lo], axis=1)


def _ssd_chunk_common(dtr_ref, bias_ref, alog_ref, r0, spread_pair, spread_full):
    rr = lax.broadcasted_iota(jnp.int32, (L, L), 0)
    cc = lax.broadcasted_iota(jnp.int32, (L, L), 1)
    tri = (cc <= rr).astype(F32)
    raw = dtr_ref[pl.ds(r0, L), :] + bias_ref[...]
    dt = _softplus(raw)
    a_neg = -jnp.exp(alog_ref[...])
    acs = _dot_hi(tri, dt * a_neg)
    acs3 = _split3(acs)
    both = _dot(jnp.concatenate([_split3(dt), acs3], axis=0), spread_pair)
    return raw, dt, a_neg, acs, acs.T, rr, cc, both[:L], both[L:], _dot(acs3, spread_full)


def _ssd_fwd(xbc, dtr, bias, alog, dskip, nb, s):
    nc = s // L
    t = nb * s

    def body(x_ref, b_ref, c_ref, dtr_ref, bias_ref, alog_ref, dsk_ref, y_ref, hs_ref, h_scr):
        lane = lax.broadcasted_iota(jnp.int32, (L, LANES), 1)
        lane1 = lax.broadcasted_iota(jnp.int32, (1, LANES), 1)
        rowc = lax.broadcasted_iota(jnp.int32, (L, 1), 0)
        h_scr[...] = jnp.zeros_like(h_scr)
        dsk = dsk_ref[...]
        spread_pair, spread_full, _, _ = _ssd_tables()

        def chunk(c, _):
            r0 = pl.multiple_of(c * L, L)
            raw, dt, a_neg, acs, acs_t, rr, cc, dt_all, acs_all, acs_full = _ssd_chunk_common(
                dtr_ref, bias_ref, alog_ref, r0, spread_pair, spread_full)
            causal = rr >= cc
            bb = _bf(b_ref[pl.ds(r0, L), :])
            cb = _bf(c_ref[pl.ds(r0, L), :])
            gm = _dot_nt(cb, bb)
            for pr in range(N_PAIR):
                j0, j1 = 2 * pr, 2 * pr + 1
                ms = []
                for j in (j0, j1):
                    seg = acs_full[:, j * LANES:(j + 1) * LANES] - acs_t[j:j + 1, :]
                    ms.append(_bf(gm * jnp.exp(jnp.where(causal, seg, -1e30))))
                x = x_ref[pl.ds(r0, L), pr * LANES:(pr + 1) * LANES]
                dt_p = dt_all[:, pr * LANES:(pr + 1) * LANES]
                acs_p = acs_all[:, pr * LANES:(pr + 1) * LANES]
                last_p = acs_p[L - 1:L, :]
                d_p = _pair_lanes(dsk[:, j0:j0 + 1], dsk[:, j1:j1 + 1], lane1)
                xd = x * dt_p
                xdb = _bf(xd)
                yd = _pair_lanes(_dot(ms[0], xdb), _dot(ms[1], xdb), lane)
                hp = h_scr[pr]
                yo = _dot_nt(cb, _bf(hp)) * jnp.exp(acs_p)
                y_ref[pl.ds(r0, L), pr * LANES:(pr + 1) * LANES] = yd + yo + x * d_p
                hs_ref[c, pr] = hp
                sp = _dot_tn(_bf(xd * jnp.exp(last_p - acs_p)), bb)
                cd = jnp.where(rowc < HEAD_DIM, jnp.exp(acs[L - 1:L, j0:j0 + 1]),
                               jnp.exp(acs[L - 1:L, j1:j1 + 1]))
                h_scr[pr] = hp * cd + sp
            return 0

        lax.fori_loop(0, nc, chunk, 0)

    grp = lambda b, g: (g, 0, 0)
    return pl.pallas_call(
        body, grid=(nb, SSD_GROUPS),
        in_specs=[pl.BlockSpec((s, 4 * LANES), lambda b, g: (b, g)),
                  pl.BlockSpec((s, LANES), lambda b, g: (b, 16 + g)),
                  pl.BlockSpec((s, LANES), lambda b, g: (b, 20 + g)),
                  pl.BlockSpec((None, s, LANES), lambda b, g: (g, b, 0)),
                  pl.BlockSpec((None, 1, LANES), grp), pl.BlockSpec((None, 1, LANES), grp),
                  pl.BlockSpec((None, 1, LANES), grp)],
        out_specs=[pl.BlockSpec((s, 4 * LANES), lambda b, g: (b, g)),
                   pl.BlockSpec((None, None, nc, N_PAIR, L, LANES), lambda b, g: (b, g, 0, 0, 0, 0))],
        out_shape=[jax.ShapeDtypeStruct((t, SSD_WIDTH), F32),
                   jax.ShapeDtypeStruct((nb, SSD_GROUPS, nc, N_PAIR, L, LANES), F32)],
        scratch_shapes=[pltpu.VMEM((N_PAIR, L, LANES), F32)],
        compiler_params=_params(("parallel", "parallel")), name="ssd_fwd")(
            xbc, xbc, xbc, dtr, bias, alog, dskip)


def _ssd_bwd(xbc, dtr, bias, alog, dskip, hs, dy, nb, s):
    nc = s // L
    t = nb * s

    def body(x_ref, b_ref, c_ref, dtr_ref, bias_ref, alog_ref, dsk_ref, hs_ref, dy_ref,
             dx_ref, db_ref, dc_ref, ddtr_ref, dbias_ref, dalog_ref, ddsk_ref, dh_scr):
        lane = lax.broadcasted_iota(jnp.int32, (L, LANES), 1)
        lane1 = lax.broadcasted_iota(jnp.int32, (1, LANES), 1)
        rowc = lax.broadcasted_iota(jnp.int32, (L, 1), 0)
        dh_scr[...] = jnp.zeros_like(dh_scr)
        dsk = dsk_ref[...]

        @pl.when(pl.program_id(1) == 0)
        def _():
            dbias_ref[...] = jnp.zeros_like(dbias_ref)
            dalog_ref[...] = jnp.zeros_like(dalog_ref)
            ddsk_ref[...] = jnp.zeros_like(ddsk_ref)

        spread_pair, spread_full, sum_pair, sum_full = _ssd_tables()
        sum_pair2 = jnp.concatenate([sum_pair] * 2, axis=0)
        sum_full2 = jnp.concatenate([sum_full] * 2, axis=0)

        def chunk(i, _):
            c = nc - 1 - i
            r0 = pl.multiple_of(c * L, L)
            raw, dt, a_neg, acs, acs_t, rr, cc, dt_all, acs_all, acs_full = _ssd_chunk_common(
                dtr_ref, bias_ref, alog_ref, r0, spread_pair, spread_full)
            causal = rr >= cc
            tri_t = (cc >= rr).astype(F32)
            bb = _bf(b_ref[pl.ds(r0, L), :])
            cb = _bf(c_ref[pl.ds(r0, L), :])
            gm = _dot_nt(cb, bb)
            dg = jnp.zeros((L, L), F32)
            dbacc = jnp.zeros((L, LANES), F32)
            dcacc = jnp.zeros((L, LANES), F32)
            dacs_t = jnp.zeros((L, L), F32)
            ta_all, te_all, qm_all, tc_all, td_all, dcd_all = [], [], [], [], [], []
            for pr in range(N_PAIR):
                js = (2 * pr, 2 * pr + 1)
                lms, mfs = [], []
                for j in js:
                    seg = acs_full[:, j * LANES:(j + 1) * LANES] - acs_t[j:j + 1, :]
                    lm = jnp.exp(jnp.where(causal, seg, -1e30))
                    lms.append(lm)
                    mfs.append(gm * lm)
                x = x_ref[pl.ds(r0, L), pr * LANES:(pr + 1) * LANES]
                dyv = dy_ref[pl.ds(r0, L), pr * LANES:(pr + 1) * LANES]
                dt_p = dt_all[:, pr * LANES:(pr + 1) * LANES]
                acs_p = acs_all[:, pr * LANES:(pr + 1) * LANES]
                last_p = acs_p[L - 1:L, :]
                d_p = _pair_lanes(dsk[:, js[0]:js[0] + 1], dsk[:, js[1]:js[1] + 1], lane1)
                e_p = jnp.exp(acs_p)
                dte_p = jnp.exp(last_p - acs_p)
                xd = x * dt_p
                xdb = _bf(xd)
                hp = hs_ref[c, pr]
                hb = _bf(hp)
                dye = _bf(dyv * e_p)
                yo = _dot_nt(cb, hb) * e_p
                dcacc = dcacc + _dot(dye, hb)
                dh_out = _dot_tn(dye, cb)
                dyb = _bf(dyv)
                dxd = jnp.zeros((L, LANES), F32)
                for hh in range(2):
                    hm = (lane >= HEAD_DIM) if hh else (lane < HEAD_DIM)
                    dm = _dot_nt(_bf(jnp.where(hm, dyv, 0.0)), xdb)
                    dxd = dxd + jnp.where(hm, _dot_tn(_bf(mfs[hh]), dyb), 0.0)
                    dg = dg + dm * lms[hh]
                    qm = dm * mfs[hh]
                    qm_all.append(qm)
                    dacs_t = dacs_t - jnp.where(rr == js[hh], jnp.sum(qm, axis=0, keepdims=True), 0.0)
                ds = dh_scr[pr]
                dsb = _bf(ds)
                xdd = xd * dte_p
                dxdd = _dot_nt(bb, dsb)
                dbacc = dbacc + _dot(_bf(xdd), dsb)
                dxd = dxd + dxdd * dte_p
                cd = jnp.where(rowc < HEAD_DIM, jnp.exp(acs[L - 1:L, js[0]:js[0] + 1]),
                               jnp.exp(acs[L - 1:L, js[1]:js[1] + 1]))
                prod = ds * hp
                dcd_all += [jnp.sum(prod[:HEAD_DIM], axis=0, keepdims=True),
                            jnp.sum(prod[HEAD_DIM:], axis=0, keepdims=True)]
                dh_scr[pr] = dh_out + ds * cd
                tcv = dxdd * xdd
                ta_all.append(dyv * yo - tcv)
                tc_all.append(jnp.sum(tcv, axis=0, keepdims=True))
                te_all.append(dxd * x)
                td_all.append(jnp.sum(dyv * x, axis=0, keepdims=True))
                dx_ref[pl.ds(r0, L), pr * LANES:(pr + 1) * LANES] = dxd * dt_p + dyv * d_p
            dgb = _bf(dg)
            dc_ref[pl.ds(r0, L), :] = dcacc + _dot(dgb, bb)
            db_ref[pl.ds(r0, L), :] = dbacc + _dot_tn(dgb, cb)
            last_add = jnp.zeros((1, LANES), F32)
            dd_add = jnp.zeros((1, LANES), F32)
            cd_row = jnp.exp(acs[L - 1:L, :])
            for j in range(SSD_HPG):
                hm = (lane1 >= HEAD_DIM) if j % 2 else (lane1 < HEAD_DIM)
                tcj = jnp.sum(jnp.where(hm, tc_all[j // 2], 0.0), axis=1, keepdims=True)
                tdj = jnp.sum(jnp.where(hm, td_all[j // 2], 0.0), axis=1, keepdims=True)
                dcdj = jnp.sum(dcd_all[j], axis=1, keepdims=True)
                last_add = last_add + jnp.where(lane1 == j, tcj + dcdj * cd_row, 0.0)
                dd_add = dd_add + jnp.where(lane1 == j, tdj, 0.0)
            dacs = (_dot(_split2(jnp.concatenate(ta_all, axis=1)), sum_pair2)
                    + _dot(_split2(jnp.concatenate(qm_all, axis=1)), sum_full2)
                    + dacs_t.T + jnp.where(rowc == L - 1, last_add, 0.0))
            ddta = _dot_hi(tri_t, dacs)
            ddt = _dot(_bf(jnp.concatenate(te_all, axis=1)), sum_pair) + ddta * a_neg
            ddraw = ddt * _sigmoid(raw)
            ddtr_ref[pl.ds(r0, L), :] = ddraw
            dbias_ref[...] += jnp.sum(ddraw, axis=0, keepdims=True)
            dalog_ref[...] += jnp.sum(ddta * dt, axis=0, keepdims=True) * a_neg
            ddsk_ref[...] += dd_add
            return 0

        lax.fori_loop(0, nc, chunk, 0)

    grp = lambda g, b: (g, 0, 0)
    small = jax.ShapeDtypeStruct((SSD_GROUPS, 1, LANES), F32)
    return pl.pallas_call(
        body, grid=(SSD_GROUPS, nb),
        in_specs=[pl.BlockSpec((s, 4 * LANES), lambda g, b: (b, g)),
                  pl.BlockSpec((s, LANES), lambda g, b: (b, 16 + g)),
                  pl.BlockSpec((s, LANES), lambda g, b: (b, 20 + g)),
                  pl.BlockSpec((None, s, LANES), lambda g, b: (g, b, 0)),
                  pl.BlockSpec((None, 1, LANES), grp), pl.BlockSpec((None, 1, LANES), grp),
                  pl.BlockSpec((None, 1, LANES), grp),
                  pl.BlockSpec((None, None, nc, N_PAIR, L, LANES), lambda g, b: (b, g, 0, 0, 0, 0)),
                  pl.BlockSpec((s, 4 * LANES), lambda g, b: (b, g))],
        out_specs=[pl.BlockSpec((s, 4 * LANES), lambda g, b: (b, g)),
                   pl.BlockSpec((s, LANES), lambda g, b: (b, g)),
                   pl.BlockSpec((s, LANES), lambda g, b: (b, g)),
                   pl.BlockSpec((None, s, LANES), lambda g, b: (g, b, 0)),
                   pl.BlockSpec((None, 1, LANES), grp), pl.BlockSpec((None, 1, LANES), grp),
                   pl.BlockSpec((None, 1, LANES), grp)],
        out_shape=[jax.ShapeDtypeStruct((t, SSD_WIDTH), F32),
                   jax.ShapeDtypeStruct((t, SSD_GROUPS * SSD_STATE), F32),
                   jax.ShapeDtypeStruct((t, SSD_GROUPS * SSD_STATE), F32),
                   jax.ShapeDtypeStruct((SSD_GROUPS, t, LANES), F32), small, small, small],
        scratch_shapes=[pltpu.VMEM((N_PAIR, L, LANES), F32)],
        compiler_params=_params(("parallel", "arbitrary")), name="ssd_bwd")(
            xbc, xbc, xbc, dtr, bias, alog, dskip, hs, dy)


GN = SSD_WIDTH // SSD_GROUPS


def _post_fwd(o, proj, y, nw, *, tm=256):
    t = o.shape[0]

    def body(o_ref, za_ref, y_ref, zs_ref, nw_ref, ya_ref, ys_ref):
        za = za_ref[...].astype(F32)
        ya_ref[...] = _bf(o_ref[...] * (za * _sigmoid(za)))
        zs = zs_ref[...].astype(F32)
        yz = y_ref[...] * (zs * _sigmoid(zs))
        for g in range(SSD_GROUPS):
            sl = slice(g * GN, (g + 1) * GN)
            v = yz[:, sl]
            r = lax.rsqrt(jnp.mean(v * v, axis=-1, keepdims=True) + EPS)
            ys_ref[:, sl] = _bf(v * r * nw_ref[:, sl])

    return pl.pallas_call(
        body, grid=(t // tm,),
        in_specs=[pl.BlockSpec((tm, SB_WIDTH), lambda i: (i, 0)),
                  pl.BlockSpec((tm, SB_WIDTH), lambda i: (i, CB_ZA * LANES // SB_WIDTH)),
                  pl.BlockSpec((tm, SSD_WIDTH), lambda i: (i, 0)),
                  pl.BlockSpec((tm, SSD_WIDTH), lambda i: (i, CB_ZS * LANES // SSD_WIDTH)),
                  pl.BlockSpec((1, SSD_WIDTH), lambda i: (0, 0))],
        out_specs=[pl.BlockSpec((tm, SB_WIDTH), lambda i: (i, 0)),
                   pl.BlockSpec((tm, SSD_WIDTH), lambda i: (i, 0))],
        out_shape=[jax.ShapeDtypeStruct((t, SB_WIDTH), BF16), jax.ShapeDtypeStruct((t, SSD_WIDTH), BF16)],
        compiler_params=_params(("parallel",)), name="post_fwd")(o, proj, y, proj, nw)


def _post_bwd(dya, o, proj, dys, y, nw, *, tm=256):
    t = o.shape[0]

    def body(dya_ref, o_ref, za_ref, dys_ref, y_ref, zs_ref, nw_ref,
             do_ref, dza_ref, dy_ref, dzs_ref, dnw_ref):
        @pl.when(pl.program_id(0) == 0)
        def _():
            dnw_ref[...] = jnp.zeros_like(dnw_ref)

        za = za_ref[...].astype(F32)
        sa = _sigmoid(za)
        d = dya_ref[...]
        do_ref[...] = d * (za * sa)
        dza_ref[...] = _bf(d * o_ref[...] * (sa * (1.0 + za * (1.0 - sa))))
        zs = zs_ref[...].astype(F32)
        ss = _sigmoid(zs)
        silu = zs * ss
        yv = y_ref[...]
        yz = yv * silu
        dv = dys_ref[...]
        for g in range(SSD_GROUPS):
            sl = slice(g * GN, (g + 1) * GN)
            v = yz[:, sl]
            r = lax.rsqrt(jnp.mean(v * v, axis=-1, keepdims=True) + EPS)
            nrm = v * r
            dg = dv[:, sl]
            dnw_ref[:, sl] += jnp.sum(dg * nrm, axis=0, keepdims=True)
            dn = dg * nw_ref[:, sl]
            dyz = r * (dn - nrm * jnp.mean(dn * nrm, axis=-1, keepdims=True))
            dy_ref[:, sl] = dyz * silu[:, sl]
            dzs_ref[:, sl] = _bf(dyz * yv[:, sl] * (ss[:, sl] * (1.0 + zs[:, sl] * (1.0 - ss[:, sl]))))

    a_spec = pl.BlockSpec((tm, SB_WIDTH), lambda i: (i, 0))
    s_spec = pl.BlockSpec((tm, SSD_WIDTH), lambda i: (i, 0))
    w_spec = pl.BlockSpec((1, SSD_WIDTH), lambda i: (0, 0))
    return pl.pallas_call(
        body, grid=(t // tm,),
        in_specs=[a_spec, a_spec, pl.BlockSpec((tm, SB_WIDTH), lambda i: (i, CB_ZA * LANES // SB_WIDTH)),
                  s_spec, s_spec, pl.BlockSpec((tm, SSD_WIDTH), lambda i: (i, CB_ZS * LANES // SSD_WIDTH)),
                  w_spec],
        out_specs=[a_spec, a_spec, s_spec, s_spec, w_spec],
        out_shape=[jax.ShapeDtypeStruct((t, SB_WIDTH), F32), jax.ShapeDtypeStruct((t, SB_WIDTH), BF16),
                   jax.ShapeDtypeStruct((t, SSD_WIDTH), F32), jax.ShapeDtypeStruct((t, SSD_WIDTH), BF16),
                   jax.ShapeDtypeStruct((1, SSD_WIDTH), F32)],
        compiler_params=_params(("arbitrary",)), name="post_bwd")(dya, o, proj, dys, y, proj, nw)


def _merge_fwd(ya, ys, proj_g, *, tm=512):
    t = ya.shape[0]

    def body(ya_ref, ys_ref, ga_ref, gs_ref, m_ref):
        m_ref[...] = _bf(_sigmoid(ga_ref[...].astype(F32)) * ya_ref[...]
                         + _sigmoid(gs_ref[...].astype(F32)) * ys_ref[...])

    spec = pl.BlockSpec((tm, D_MODEL), lambda i: (i, 0))
    return pl.pallas_call(
        body, grid=(t // tm,),
        in_specs=[spec, spec, pl.BlockSpec((tm, D_MODEL), lambda i: (i, 0)),
                  pl.BlockSpec((tm, D_MODEL), lambda i: (i, 1))],
        out_specs=spec, out_shape=jax.ShapeDtypeStruct((t, D_MODEL), BF16),
        compiler_params=_params(("parallel",)), name="merge_fwd")(ya, ys, proj_g, proj_g)


def _merge_bwd(dm, ya, ys, proj_g, *, tm=512):
    t = ya.shape[0]

    def body(dm_ref, ya_ref, ys_ref, ga_ref, gs_ref, dya_ref, dys_ref, dg_ref):
        d = dm_ref[...]
        ga = _sigmoid(ga_ref[...].astype(F32))
        gs = _sigmoid(gs_ref[...].astype(F32))
        dya_ref[...] = _bf(d * ga)
        dys_ref[...] = _bf(d * gs)
        dg_ref[:, :D_MODEL] = _bf(d * ya_ref[...] * (ga * (1.0 - ga)))
        dg_ref[:, D_MODEL:] = _bf(d * ys_ref[...] * (gs * (1.0 - gs)))

    spec = pl.BlockSpec((tm, D_MODEL), lambda i: (i, 0))
    return pl.pallas_call(
        body, grid=(t // tm,),
        in_specs=[spec, spec, spec, pl.BlockSpec((tm, D_MODEL), lambda i: (i, 0)),
                  pl.BlockSpec((tm, D_MODEL), lambda i: (i, 1))],
        out_specs=[spec, spec, pl.BlockSpec((tm, 2 * D_MODEL), lambda i: (i, 0))],
        out_shape=[jax.ShapeDtypeStruct((t, D_MODEL), BF16), jax.ShapeDtypeStruct((t, D_MODEL), BF16),
                   jax.ShapeDtypeStruct((t, 2 * D_MODEL), BF16)],
        compiler_params=_params(("parallel",)), name="merge_bwd")(dm, ya, ys, proj_g, proj_g)


def _final(x, out, wf, target, *, tm=512):
    t = x.shape[0]

    def body(x_ref, o_ref, w_ref, t_ref, loss_ref, dx_ref, dw_ref):
        @pl.when(pl.program_id(0) == 0)
        def _():
            loss_ref[...] = jnp.zeros_like(loss_ref)
            dw_ref[...] = jnp.zeros_like(dw_ref)

        x2 = x_ref[...] + o_ref[...]
        r = lax.rsqrt(jnp.mean(x2 * x2, axis=-1, keepdims=True) + EPS)
        nrm = x2 * r
        e = nrm * w_ref[...] - t_ref[...]
        row_loss = jnp.mean(e * e, axis=-1, keepdims=True)
        loss_ref[...] += 0.5 * jnp.sum(row_loss, axis=0, keepdims=True)
        dyv = e * (1.0 / D_MODEL)
        dw_ref[...] += jnp.sum(dyv * nrm, axis=0, keepdims=True)
        dn = dyv * w_ref[...]
        dx_ref[...] = r * (dn - nrm * jnp.mean(dn * nrm, axis=-1, keepdims=True))

    spec = pl.BlockSpec((tm, D_MODEL), lambda i: (i, 0))
    fix = pl.BlockSpec((1, D_MODEL), lambda i: (0, 0))
    return pl.pallas_call(
        body, grid=(t // tm,),
        in_specs=[spec, spec, fix, spec],
        out_specs=[pl.BlockSpec((1, LANES), lambda i: (0, 0)), spec, fix],
        out_shape=[jax.ShapeDtypeStruct((1, LANES), F32), jax.ShapeDtypeStruct((t, D_MODEL), F32),
                   jax.ShapeDtypeStruct((1, D_MODEL), F32)],
        compiler_params=_params(("arbitrary",)), name="final_loss")(x, out, wf, target)


def _group_lanes(v):
    return jnp.pad(v.reshape(SSD_GROUPS, 1, SSD_HPG), ((0, 0), (0, 0), (0, LANES - SSD_HPG)))


def _ungroup_lanes(v):
    return v[:, :, :SSD_HPG].reshape(1, N_DT)


def _row_slabs(pieces, rows):
    total = sum(p.shape[0] for p in pieces)
    assert total % rows == 0
    slabs = []
    for lo in range(0, total, rows):
        parts, off = [], 0
        for p in pieces:
            a, b = max(lo, off), min(lo + rows, off + p.shape[0])
            if a < b:
                parts.append(p[a - off:b - off])
            off += p.shape[0]
        slabs.append(parts[0] if len(parts) == 1 else jnp.concatenate(parts, axis=0))
    return jnp.stack(slabs)


def _local_step(x, target, nb, s, w_in_t, conv_w, conv_b, norm_w, dt_bias, a_log, d_skip,
                ssm_norm_w, out_weights, final_norm_w, between=None):
    t = nb * s
    w_g_t = w_in_t[GATE_COL0:]
    w_dt_t = jnp.pad(w_in_t[DT_COL0:GATE_COL0], ((0, LANES - N_DT), (0, 0)))
    h, r1 = _rms_fwd(x, norm_w)
    proj = _matmul(h, w_in_t, nt=True, n=DT_COL0, out_dtype=BF16, name="proj_main")
    proj_g = _matmul(h, w_g_t, nt=True, out_dtype=BF16, name="proj_gate")
    dt_raw = _matmul(h, w_dt_t, nt=True, name="proj_dt")
    dtr = jnp.pad(dt_raw[:, :N_DT].reshape(t, SSD_GROUPS, SSD_HPG).transpose(1, 0, 2),
                  ((0, 0), (0, 0), (0, LANES - SSD_HPG)))
    bias_g, alog_g, dsk_g = _group_lanes(dt_bias), _group_lanes(a_log), _group_lanes(d_skip)

    o, tot = _attn_fwd(proj, nb, s)
    xbc = _conv_fwd(proj, conv_w, conv_b, nb, s)
    y, hs = _ssd_fwd(xbc, dtr, bias_g, alog_g, dsk_g, nb, s)
    ya_in, ys_in = _post_fwd(o, proj, y, ssm_norm_w)
    w_attn_out, w_ssm_out, w_o = out_weights(ys_in) if callable(out_weights) else out_weights
    ya = _matmul(ya_in, w_attn_out, name="attn_out")
    ys = _matmul(ys_in, w_ssm_out, name="ssm_out")
    merged = _merge_fwd(ya, ys, proj_g)
    out = _matmul(merged, w_o, name="out_proj")
    loss, dx2, d_final_w = _final(x, out, final_norm_w, target)

    dx2b = _bf(dx2)
    dmerged = _matmul(dx2b, w_o, nt=True, name="d_merged")
    d_w_o = _matmul(merged, dx2b, ta=True, name="d_w_o")
    dya, dys, dgate = _merge_bwd(dmerged, ya, ys, proj_g)
    d_w_attn_out = _matmul(ya_in, dya, ta=True, name="d_w_attn_out")
    d_w_ssm_out = _matmul(ys_in, dys, ta=True, name="d_w_ssm_out")
    dya_in = _matmul(dya, w_attn_out, nt=True, name="d_ya_in")
    dys_in = _matmul(dys, w_ssm_out, nt=True, name="d_ys_in")
    do, dza, dy, dzs, d_ssm_norm_w = _post_bwd(dya_in, o, proj, dys_in, y, ssm_norm_w)
    dq, dk, dv = _attn_bwd(proj, do, tot, nb, s)
    dxs_act, db_act, dc_act, ddtr, dbias_g, dalog_g, ddsk_g = _ssd_bwd(
        xbc, dtr, bias_g, alog_g, dsk_g, hs, dy, nb, s)
    dxs, dwx, dbx = _conv_bwd(proj, dxs_act, conv_w, conv_b, nb, s, col0=0, name="conv_bwd_x")
    dbm, dwb, dbb = _conv_bwd(proj, db_act, conv_w, conv_b, nb, s, col0=SSD_WIDTH, name="conv_bwd_b")
    dcm, dwc, dbc = _conv_bwd(proj, dc_act, conv_w, conv_b, nb, s, col0=SSD_WIDTH + 512, name="conv_bwd_c")
    dproj = jnp.concatenate([dq, dk, dv, dza, dzs, dxs, dbm, dcm], axis=1)
    ddt = jnp.pad(_bf(ddtr[:, :, :SSD_HPG].transpose(1, 0, 2).reshape(t, N_DT)), ((0, 0), (0, LANES - N_DT)))
    wt = dict(ta=True, out_dtype=BF16)
    d_w_in_t = _row_slabs(
        [_matmul(dproj, h, name="d_w_main", **wt), _matmul(ddt, h, name="d_w_dt", **wt)[:N_DT],
         _matmul(dgate, h, name="d_w_gate", **wt)], SHARD)
    grads = dict(
        w_in_t=d_w_in_t,
        conv_w=jnp.concatenate([dwx, dwb, dwc], axis=1), conv_b=jnp.concatenate([dbx, dbb, dbc], axis=1),
        dt_bias=_ungroup_lanes(dbias_g), a_log=_ungroup_lanes(dalog_g), d_skip=_ungroup_lanes(ddsk_g),
        ssm_norm_w=d_ssm_norm_w, w_attn_out=d_w_attn_out, w_ssm_out=d_w_ssm_out, w_o=d_w_o,
        final_norm_w=d_final_w)
    extra = None
    if between is not None:
        zero, extra = between(grads)
        ddt = ddt + _bf(zero[0:1, 0:1])

    dh = _matmul(ddt, w_dt_t, name="d_h_dt")
    dh = _matmul(dgate, w_g_t, add=dh, name="d_h_gate")
    dh = _matmul(dproj, w_in_t, add=dh, name="d_h")
    grad_x, grads["norm_w"] = _rms_bwd(dh, x, r1, norm_w, dx2)
    return loss, grad_x, grads, extra


N_CHIP = N_DEV // 2


def _pair_exchange(scatter, bcast, *, name):
    arrays = list(scatter) + list(bcast)
    ns, n = len(scatter), len(arrays)
    n_sem = ns * N_CHIP + (n - ns) * (N_DEV - 1)

    def body(*refs):
        ins, outs = refs[:n], refs[n:2 * n]
        send_sems, recv_sems, local_sems = refs[2 * n:]
        x, y, c = lax.axis_index("x"), lax.axis_index("y"), lax.axis_index("c")
        me = 4 * x + 2 * y + c
        copies, k = [], 0
        for a in range(ns):
            for q in range(N_CHIP):
                cp = pltpu.make_async_remote_copy(
                    src_ref=ins[a].at[2 * q + 1 - c], dst_ref=outs[a].at[q],
                    send_sem=send_sems.at[k], recv_sem=recv_sems.at[k],
                    device_id=(x, y, 1 - c), device_id_type=pl.DeviceIdType.MESH)
                cp.start()
                copies.append(cp)
                k += 1
        for a in range(ns, n):
            cp = pltpu.make_async_copy(ins[a], outs[a].at[me], local_sems.at[a - ns])
            cp.start()
            copies.append(cp)
            for r in range(1, N_DEV):
                tx = 1 - x if (r >> 2) & 1 else x
                ty = 1 - y if (r >> 1) & 1 else y
                tc = 1 - c if r & 1 else c
                cp = pltpu.make_async_remote_copy(
                    src_ref=ins[a], dst_ref=outs[a].at[me], send_sem=send_sems.at[k], recv_sem=recv_sems.at[k],
                    device_id=(tx, ty, tc), device_id_type=pl.DeviceIdType.MESH)
                cp.start()
                copies.append(cp)
                k += 1
        for cp in copies:
            cp.wait()

    out_shape = [jax.ShapeDtypeStruct((N_CHIP,) + v.shape[1:] if i < ns else (N_DEV,) + v.shape, v.dtype)
                 for i, v in enumerate(arrays)]
    return pl.pallas_call(
        body, in_specs=[pl.BlockSpec(memory_space=pl.ANY)] * n,
        out_specs=[pl.BlockSpec(memory_space=pl.ANY)] * n, out_shape=out_shape,
        scratch_shapes=[pltpu.SemaphoreType.DMA((n_sem,)), pltpu.SemaphoreType.DMA((n_sem,)),
                        pltpu.SemaphoreType.DMA((max(n - ns, 1),))],
        compiler_params=pltpu.CompilerParams(has_side_effects=True), name=name)(*arrays)


def _pair_add(mine, recv, core, *, name, tc=None):
    _, r, c = recv.shape
    tc = c if tc is None else tc
    assert c % tc == 0

    def body(core_ref, m_ref, r_ref, o_ref):
        o_ref[...] = (m_ref[...].astype(F32) + r_ref[...].astype(F32)).astype(o_ref.dtype)

    spec = pl.BlockSpec((None, r, tc), lambda q, i, core_ref: (q, 0, i))
    return pl.pallas_call(
        body, grid_spec=pltpu.PrefetchScalarGridSpec(
            num_scalar_prefetch=1, grid=(N_CHIP, c // tc),
            in_specs=[pl.BlockSpec((None, r, tc), lambda q, i, core_ref: (2 * q + core_ref[0], 0, i)), spec],
            out_specs=spec),
        out_shape=jax.ShapeDtypeStruct(recv.shape, recv.dtype),
        compiler_params=_params(("parallel", "parallel")), name=name)(core, mine, recv)


_HBM = pl.BlockSpec(memory_space=pltpu.HBM)
_SEM = pl.BlockSpec(memory_space=pltpu.SEMAPHORE)
_DATAFLOW = pltpu.SideEffectType.DATAFLOW_SIDE_EFFECTING
_PEERS = {"chips": N_CHIP - 1, "all": N_DEV - 1}


def _split_copies(pattern, srcs, lands, send_sems, recv_sems):
    x, y, c = lax.axis_index("x"), lax.axis_index("y"), lax.axis_index("c")
    per = _PEERS[pattern]
    copies = []
    for a in range(len(srcs)):
        for r in range(1, per + 1):
            if pattern == "chips":
                tx = 1 - x if (r >> 1) & 1 else x
                ty = 1 - y if r & 1 else y
                tc = c
                src, dst = srcs[a].at[2 * tx + ty], lands[a].at[2 * x + y]
            else:
                tx = 1 - x if (r >> 2) & 1 else x
                ty = 1 - y if (r >> 1) & 1 else y
                tc = 1 - c if r & 1 else c
                src, dst = srcs[a], lands[a].at[4 * x + 2 * y + c]
            copies.append(pltpu.make_async_remote_copy(
                src_ref=src, dst_ref=dst, send_sem=send_sems.at[a * per + r - 1],
                recv_sem=recv_sems.at[a * per + r - 1], device_id=(tx, ty, tc), device_id_type=pl.DeviceIdType.MESH))
    return copies


def _split_start(pattern, arrays, *, name):
    n = len(arrays)
    n_sem = n * _PEERS[pattern]

    def body(*refs):
        srcs, lands = refs[:n], refs[n:2 * n]
        send_sems, recv_sems = refs[2 * n], refs[2 * n + 1]
        zero = refs[-1]
        for cp in _split_copies(pattern, srcs, lands, send_sems, recv_sems):
            cp.start()
        zero[...] = jnp.zeros_like(zero)

    src_t = [pltpu.HBM(v.shape, v.dtype) for v in arrays]
    land_t = [pltpu.HBM(v.shape if pattern == "chips" else (N_DEV,) + v.shape, v.dtype) for v in arrays]
    outs = pl.pallas_call(
        body, name=name,
        out_shape=(pltpu.SemaphoreType.DMA((n_sem,)), pltpu.SemaphoreType.DMA((n_sem,)), *src_t, *land_t,
                   jax.ShapeDtypeStruct((8, LANES), F32)),
        in_specs=[_HBM] * (2 * n),
        out_specs=(_SEM, _SEM, *[_HBM] * (2 * n), pl.BlockSpec(memory_space=pltpu.VMEM)),
        input_output_aliases={i: 2 + i for i in range(2 * n)},
        compiler_params=pltpu.CompilerParams(has_side_effects=_DATAFLOW))(
            *[pltpu.with_memory_space_constraint(v, pltpu.HBM) for v in arrays],
            *[pltpu.with_memory_space_constraint(lax.empty(t.shape, t.dtype), pltpu.HBM) for t in land_t])
    return outs[0], outs[1], outs[2:2 + n], outs[2 + n:2 + 2 * n], outs[-1]


def _split_wait(pattern, send_sems, recv_sems, srcs, lands, after, *, name):
    n = len(srcs)

    def body(*refs):
        for cp in _split_copies(pattern, refs[:n], refs[n:2 * n], refs[2 * n], refs[2 * n + 1]):
            cp.wait_send()
            cp.wait_recv()

    outs = pl.pallas_call(
        body, name=name, out_shape=[pltpu.HBM(v.shape, v.dtype) for v in (*srcs, *lands)],
        in_specs=[_HBM] * (2 * n) + [_SEM, _SEM, pl.BlockSpec(memory_space=pl.ANY)],
        out_specs=[_HBM] * (2 * n), input_output_aliases={i: i for i in range(2 * n)},
        compiler_params=pltpu.CompilerParams(has_side_effects=_DATAFLOW))(
            *srcs, *lands, send_sems, recv_sems, after)
    return outs[:n], outs[n:]


def _own_slab(lands, srcs, index, whole):
    out = []
    for land, src in zip(lands, srcs):
        own = src[None] if whole else lax.dynamic_slice(src, (index, 0, 0), (1,) + src.shape[1:])
        out.append(lax.dynamic_update_slice(land, own, (index, 0, 0)))
    return out


def _gather(arrays, *, name):
    n = len(arrays)
    per = N_DEV - 1

    def body(*refs):
        ins, outs = refs[:n], refs[n:2 * n]
        send_sems, recv_sems, local_sems = refs[2 * n:]
        x, y, c = lax.axis_index("x"), lax.axis_index("y"), lax.axis_index("c")
        me, sibling = (x, y, c), (x, y, 1 - c)
        chips = [(1 - x, y), (x, 1 - y), (1 - x, 1 - y)]

        def copy(a, k, block, to, src=None):
            rows = outs[a].at[4 * block[0] + 2 * block[1] + block[2]]
            return pltpu.make_async_remote_copy(
                src_ref=rows if src is None else src, dst_ref=rows,
                send_sem=send_sems.at[a * per + k], recv_sem=recv_sems.at[a * per + k],
                device_id=to, device_id_type=pl.DeviceIdType.MESH)

        started = []
        for a in range(n):
            mine = pltpu.make_async_copy(ins[a], outs[a].at[4 * x + 2 * y + c], local_sems.at[a])
            mine.start()
            started.append(mine)
        for a in range(n):
            first = [copy(a, 1 + j, me, (*chip, c), src=ins[a]) for j, chip in enumerate(chips)]
            first.append(copy(a, 0, me, sibling, src=ins[a]))
            for cp in first:
                cp.start()
            started += first
        for j, chip in enumerate(chips):
            for a in range(n):
                copy(a, 1 + j, (*chip, c), me).wait_recv()
                passed = copy(a, 4 + j, (*chip, c), sibling)
                passed.start()
                started.append(passed)
        for a in range(n):
            copy(a, 0, sibling, me).wait_recv()
            for j, chip in enumerate(chips):
                copy(a, 4 + j, (*chip, 1 - c), me).wait_recv()
        for cp in started[:n]:
            cp.wait()
        for cp in started[n:]:
            cp.wait_send()

    return pl.pallas_call(
        body, in_specs=[pl.BlockSpec(memory_space=pl.ANY)] * n,
        out_specs=[pl.BlockSpec(memory_space=pl.ANY)] * n,
        out_shape=[jax.ShapeDtypeStruct((N_DEV,) + v.shape, v.dtype) for v in arrays],
        scratch_shapes=[pltpu.SemaphoreType.DMA((n * per,)), pltpu.SemaphoreType.DMA((n * per,)),
                        pltpu.SemaphoreType.DMA((n,))],
        compiler_params=pltpu.CompilerParams(has_side_effects=True), name=name)(*arrays)


def _adamw(parts, w, m, v, *, name, tc=None):
    r, c = w.shape
    n_parts = parts.shape[0]
    tc = c if tc is None else tc
    assert c % tc == 0

    def body(p_ref, w_ref, m_ref, v_ref, g_ref, d_ref, nm_ref, nv_ref):
        g = p_ref[0].astype(F32)
        for d in range(1, n_parts):
            g = g + p_ref[d].astype(F32)
        mm = ADAM_B1 * m_ref[...] + (1.0 - ADAM_B1) * g
        vv = ADAM_B2 * v_ref[...] + (1.0 - ADAM_B2) * (g * g)
        m_hat = mm / (1.0 - ADAM_B1 ** ADAM_STEP)
        v_hat = vv / (1.0 - ADAM_B2 ** ADAM_STEP)
        g_ref[...] = g
        d_ref[...] = -ADAM_LR * (m_hat / (jnp.sqrt(v_hat) + ADAM_EPS) + ADAM_WD * w_ref[...])
        nm_ref[...] = mm
        nv_ref[...] = vv

    spec = pl.BlockSpec((r, tc), lambda i: (0, i))
    out = jax.ShapeDtypeStruct((r, c), F32)
    return pl.pallas_call(
        body, grid=(c // tc,),
        in_specs=[pl.BlockSpec((n_parts, r, tc), lambda i: (0, 0, i)), spec, spec, spec],
        out_specs=[spec] * 4, out_shape=[out] * 4,
        compiler_params=_params(("parallel",)), name=name)(parts, w, m, v)


SMALL = (("norm_w", D_MODEL), ("conv_b", CONV_DIM), ("dt_bias", N_DT), ("a_log", N_DT),
         ("d_skip", N_DT), ("ssm_norm_w", SSD_WIDTH), ("final_norm_w", D_MODEL))
SMALL_ROWS = 64
SHARD = D_PROJ // N_DEV


def _pack_small(vals):
    flat = jnp.concatenate([vals[k].reshape(-1).astype(F32) for k, _ in SMALL])
    return jnp.pad(flat, (0, SMALL_ROWS * LANES - flat.shape[0])).reshape(SMALL_ROWS, LANES)


def _unpack_small(packed, shapes):
    flat = packed.reshape(-1)
    out, off = {}, 0
    for k, size in SMALL:
        out[k] = flat[off:off + size].reshape(shapes[k])
        off += size
    return out


def kernel(x, norm_w, w_in, conv_w, conv_b, dt_bias, a_log, d_skip, ssm_norm_w, w_attn_out, w_ssm_out, w_o, final_norm_w, loss_target, m_norm_w, m_w_in, m_conv_w, m_conv_b, m_dt_bias, m_a_log, m_d_skip, m_ssm_norm_w, m_w_attn_out, m_w_ssm_out, m_w_o, m_final_norm_w, v_norm_w, v_w_in, v_conv_w, v_conv_b, v_dt_bias, v_a_log, v_d_skip, v_ssm_norm_w, v_w_attn_out, v_w_ssm_out, v_w_o, v_final_norm_w):
    nb, s, _ = x.shape
    t = nb * s
    weights = dict(norm_w=norm_w, w_in=w_in, conv_w=conv_w, conv_b=conv_b, dt_bias=dt_bias, a_log=a_log,
                   d_skip=d_skip, ssm_norm_w=ssm_norm_w, w_attn_out=w_attn_out, w_ssm_out=w_ssm_out,
                   w_o=w_o, final_norm_w=final_norm_w)
    moms = dict(norm_w=m_norm_w, w_in=m_w_in, conv_w=m_conv_w, conv_b=m_conv_b, dt_bias=m_dt_bias,
                a_log=m_a_log, d_skip=m_d_skip, ssm_norm_w=m_ssm_norm_w, w_attn_out=m_w_attn_out,
                w_ssm_out=m_w_ssm_out, w_o=m_w_o, final_norm_w=m_final_norm_w)
    vels = dict(norm_w=v_norm_w, w_in=v_w_in, conv_w=v_conv_w, conv_b=v_conv_b, dt_bias=v_dt_bias,
                a_log=v_a_log, d_skip=v_d_skip, ssm_norm_w=v_ssm_norm_w, w_attn_out=v_w_attn_out,
                w_ssm_out=v_w_ssm_out, w_o=v_w_o, final_norm_w=v_final_norm_w)

    tr_ = lambda a: jnp.transpose(a[0])
    g_in, g_conv = _gather([_bf(tr_(w_in)), conv_w[0]], name="gather_weights")
    conv_full = g_conv.transpose(1, 0, 2).reshape(CONV_K, CONV_DIM)
    o_send, o_recv, o_srcs, o_lands, o_zero = _split_start(
        "all", [_bf(w_attn_out[0]), _bf(w_ssm_out[0]), _bf(w_o[0])], name="gather_out_weights_start")

    core = lax.axis_index("c").astype(jnp.int32).reshape(1)
    chip = 2 * lax.axis_index("x") + lax.axis_index("y")
    me = 2 * chip + lax.axis_index("c")

    def out_weights(after):
        srcs, lands = _split_wait("all", o_send, o_recv, o_srcs, o_lands, after, name="gather_out_weights_wait")
        g_wa, g_ws, g_wo = _own_slab(lands, srcs, me, True)
        return (g_wa.reshape(SB_WIDTH, D_MODEL), g_ws.reshape(SSD_WIDTH, D_MODEL), g_wo.reshape(D_MODEL, D_MODEL))

    def send_weight_grads(grads):
        slabs = [grads["w_in_t"],
                 grads["conv_w"].reshape(CONV_K, N_DEV, CONV_DIM // N_DEV).transpose(1, 0, 2),
                 _bf(grads["w_attn_out"].reshape(N_DEV, SB_WIDTH // N_DEV, D_MODEL)),
                 _bf(grads["w_ssm_out"].reshape(N_DEV, SSD_WIDTH // N_DEV, D_MODEL)),
                 _bf(grads["w_o"].reshape(N_DEV, D_MODEL // N_DEV, D_MODEL))]
        recv = _pair_exchange(slabs, [], name="exchange_pairs")
        names = ("w_in", "conv_w", "w_attn_out", "w_ssm_out", "w_o")
        sums = [_pair_add(a, b, core, name="pair_add_" + k, tc=4 * LANES if k == "w_in" else None)
                for k, a, b in zip(names, slabs, recv)]
        send_sems, recv_sems, srcs, lands, zero = _split_start("chips", sums, name="exchange_chips_start")
        return zero, (send_sems, recv_sems, srcs, lands)

    loss_part, grad_x, grads, (send_sems, recv_sems, srcs, lands) = _local_step(
        x.reshape(t, D_MODEL), loss_target.reshape(t, D_MODEL), nb, s, g_in.reshape(D_PROJ, D_MODEL),
        conv_full, conv_b, norm_w + o_zero[0:1, 0:1], dt_bias, a_log, d_skip, ssm_norm_w, out_weights,
        final_norm_w.reshape(1, D_MODEL), between=send_weight_grads)

    (p_small,) = _pair_exchange([], [_pack_small(grads)], name="exchange_small")
    srcs, lands = _split_wait("chips", send_sems, recv_sems, srcs, lands, grad_x, name="exchange_chips_wait")
    p_in, p_conv, p_wa, p_ws, p_wo = _own_slab(lands, srcs, chip, False)

    res = {}
    res["w_in"] = [a.T for a in _adamw(p_in, tr_(w_in), tr_(m_w_in), tr_(v_w_in), name="adamw_w_in", tc=2 * LANES)]
    res["conv_w"] = _adamw(p_conv, conv_w[0], m_conv_w[0], v_conv_w[0], name="adamw_conv_w")
    res["w_attn_out"] = _adamw(p_wa, w_attn_out[0], m_w_attn_out[0], v_w_attn_out[0], name="adamw_w_attn_out")
    res["w_ssm_out"] = _adamw(p_ws, w_ssm_out[0], m_w_ssm_out[0], v_w_ssm_out[0], name="adamw_w_ssm_out")
    res["w_o"] = _adamw(p_wo, w_o[0], m_w_o[0], v_w_o[0], name="adamw_w_o")
    sm = _adamw(p_small, _pack_small(weights), _pack_small(moms), _pack_small(vels), name="adamw_small")
    shapes = {k: weights[k].shape for k, _ in SMALL}
    sm = [_unpack_small(v, shapes) for v in sm]
    for k, _ in SMALL:
        res[k] = tuple(part[k] for part in sm)

    order = ("norm_w", "w_in", "conv_w", "conv_b", "dt_bias", "a_log", "d_skip", "ssm_norm_w",
             "w_attn_out", "w_ssm_out", "w_o", "final_norm_w")
    loss = lax.psum(loss_part[0, 0], ("x", "y", "c"))
    outs = [loss, grad_x.reshape(nb, s, D_MODEL)]
    for i in range(4):
        outs += [res[k][i].reshape(weights[k].shape) for k in order]
    return tuple(outs)
```

```python
import functools

import jax
import jax.numpy as jnp
from jax import lax
from jax.experimental import pallas as pl
from jax.experimental.pallas import tpu as pltpu

F32 = jnp.float32
BF16 = jnp.bfloat16

D_MODEL = 1024
SB_WIDTH = 1024
HEAD_DIM = 64
SSD_WIDTH = 2048
SSD_GROUPS = 4
SSD_HPG = 8
SSD_STATE = 128
SSD_CHUNK = 128
CONV_K = 4
CONV_DIM = 3072
N_DT = 32
D_PROJ = 11296
DT_COL0 = 9216
GATE_COL0 = DT_COL0 + N_DT
EPS = 1e-6
N_DEV = 8

ADAM_LR = 0.001
ADAM_B1 = 0.9
ADAM_B2 = 0.999
ADAM_EPS = 1e-08
ADAM_WD = 0.01
ADAM_STEP = 10

LANES = 128
VMEM_LIMIT = 56 * 1024 * 1024

CB_Q, CB_K, CB_V, CB_ZA, CB_ZS, CB_XBC = 0, 8, 16, 24, 32, 48

NT_DIMS = (((1,), (1,)), ((), ()))
TN_DIMS = (((0,), (0,)), ((), ()))
HI = lax.Precision.HIGHEST


def _params(sem):
    return pltpu.CompilerParams(dimension_semantics=sem, vmem_limit_bytes=VMEM_LIMIT)


def _dot(a, b):
    return jnp.dot(a, b, preferred_element_type=F32)


def _dot_nt(a, b):
    return lax.dot_general(a, b, NT_DIMS, preferred_element_type=F32)


def _dot_tn(a, b):
    return lax.dot_general(a, b, TN_DIMS, preferred_element_type=F32)


def _dot_hi(a, b):
    return jnp.dot(a, b, preferred_element_type=F32, precision=HI)


def _bf(a):
    return a.astype(BF16)


def _sigmoid(x):
    return 0.5 + 0.5 * jnp.tanh(0.5 * x)


def _softplus(x):
    return jnp.maximum(x, 0.0) + jnp.log(1.0 + jnp.exp(-jnp.abs(x)))


def _trunc_split(a):
    bits = lax.bitcast_convert_type(a, jnp.uint32) & jnp.uint32(0xFFFF0000)
    hi = lax.bitcast_convert_type(bits, F32)
    return _bf(hi), _bf(a - hi)


MM_TILE = 1024
MM_TK = 2048


def _matmul(a, b, *, name, nt=False, ta=False, add=None, out_dtype=F32, tm=MM_TILE, tn=MM_TILE, tk=MM_TK, n=None):
    k, m = a.shape if ta else a.shape[::-1]
    n = (b.shape[0] if nt else b.shape[1]) if n is None else n
    tm, tn, tk = min(tm, m), min(tn, n), min(tk, k)
    assert m % tm == 0 and n % tn == 0 and k % tk == 0, (name, a.shape, b.shape)
    assert not (ta and nt)
    nk = k // tk
    has_add = add is not None

    def body(*refs):
        if has_add:
            a_ref, b_ref, add_ref, o_ref, acc_ref = refs
        else:
            a_ref, b_ref, o_ref, acc_ref = refs
        kk = pl.program_id(2)
        av, bv = _bf(a_ref[...]), _bf(b_ref[...])
        d = _dot_nt(av, bv) if nt else _dot_tn(av, bv) if ta else _dot(av, bv)
        if nk == 1:
            o_ref[...] = ((d + add_ref[...]) if has_add else d).astype(out_dtype)
            return

        @pl.when(kk == 0)
        def _():
            acc_ref[...] = jnp.zeros_like(acc_ref)

        acc_ref[...] += d

        @pl.when(kk == nk - 1)
        def _():
            r = acc_ref[...]
            if has_add:
                r = r + add_ref[...]
            o_ref[...] = r.astype(out_dtype)

    in_specs = [pl.BlockSpec((tk, tm), lambda j, i, kk: (kk, i)) if ta
                else pl.BlockSpec((tm, tk), lambda j, i, kk: (i, kk)),
                pl.BlockSpec((tn, tk), lambda j, i, kk: (j, kk)) if nt
                else pl.BlockSpec((tk, tn), lambda j, i, kk: (kk, j))]
    args = [a, b]
    if has_add:
        in_specs.append(pl.BlockSpec((tm, tn), lambda j, i, kk: (i, j)))
        args.append(add)
    return pl.pallas_call(
        body, grid=(n // tn, m // tm, nk), in_specs=in_specs,
        out_specs=pl.BlockSpec((tm, tn), lambda j, i, kk: (i, j)),
        out_shape=jax.ShapeDtypeStruct((m, n), out_dtype),
        scratch_shapes=[pltpu.VMEM((tm, tn), F32)],
        compiler_params=_params(("parallel", "parallel", "arbitrary")), name=name)(*args)


def _rms_fwd(x, w, *, tm=512):
    t = x.shape[0]

    def body(x_ref, w_ref, h_ref, r_ref):
        xv = x_ref[...]
        r = lax.rsqrt(jnp.mean(xv * xv, axis=-1, keepdims=True) + EPS)
        h_ref[...] = _bf(xv * r * w_ref[...])
        r_ref[...] = r

    row = lambda i: (i, 0)
    return pl.pallas_call(
        body, grid=(t // tm,),
        in_specs=[pl.BlockSpec((tm, D_MODEL), row), pl.BlockSpec((1, D_MODEL), lambda i: (0, 0))],
        out_specs=[pl.BlockSpec((tm, D_MODEL), row), pl.BlockSpec((tm, 1), row)],
        out_shape=[jax.ShapeDtypeStruct((t, D_MODEL), BF16), jax.ShapeDtypeStruct((t, 1), F32)],
        compiler_params=_params(("parallel",)), name="rms_fwd")(x, w)


def _rms_bwd(dh, x, r, w, dres, *, tm=512):
    t = x.shape[0]

    def body(dh_ref, x_ref, r_ref, w_ref, dres_ref, dx_ref, dw_ref):
        @pl.when(pl.program_id(0) == 0)
        def _():
            dw_ref[...] = jnp.zeros_like(dw_ref)

        nrm = x_ref[...] * r_ref[...]
        dhv = dh_ref[...]
        dw_ref[...] += jnp.sum(dhv * nrm, axis=0, keepdims=True)
        dn = dhv * w_ref[...]
        dx = r_ref[...] * (dn - nrm * jnp.mean(dn * nrm, axis=-1, keepdims=True))
        dx_ref[...] = dx + dres_ref[...]

    row = lambda i: (i, 0)
    fix = lambda i: (0, 0)
    return pl.pallas_call(
        body, grid=(t // tm,),
        in_specs=[pl.BlockSpec((tm, D_MODEL), row), pl.BlockSpec((tm, D_MODEL), row),
                  pl.BlockSpec((tm, 1), row), pl.BlockSpec((1, D_MODEL), fix),
                  pl.BlockSpec((tm, D_MODEL), row)],
        out_specs=[pl.BlockSpec((tm, D_MODEL), row), pl.BlockSpec((1, D_MODEL), fix)],
        out_shape=[jax.ShapeDtypeStruct((t, D_MODEL), F32), jax.ShapeDtypeStruct((1, D_MODEL), F32)],
        compiler_params=_params(("arbitrary",)), name="rms_bwd")(dh, x, r, w, dres)


ATT_BLK = 256


def _attn_masks(blk):
    r_io = lax.broadcasted_iota(jnp.int32, (blk, blk), 0)
    c_io = lax.broadcasted_iota(jnp.int32, (blk, blk), 1)
    return r_io, c_io


def _attn_fwd(proj, nb, s):
    blk = min(ATT_BLK, s)
    nq = s // blk
    scale = HEAD_DIM ** -0.5

    def body(q_ref, k_ref, v_ref, o_ref, tot_ref, q0_s, q1_s, kb_s, v0_s, v1_s, z_s, a_s):
        head0 = lax.broadcasted_iota(jnp.int32, (s, LANES), 1) < HEAD_DIM
        lane2 = lax.broadcasted_iota(jnp.int32, (blk, 2), 1)
        r_io, c_io = _attn_masks(blk)
        tri = c_io < r_io
        tri2 = jnp.concatenate([tri, tri], axis=0)
        u_gt = _bf(r_io > c_io)
        u2 = jnp.concatenate([u_gt, u_gt], axis=0)
        qv = q_ref[...] * scale
        q0_s[...] = _bf(jnp.where(head0, qv, 0.0))
        q1_s[...] = _bf(jnp.where(head0, 0.0, qv))
        kb_s[...] = _bf(k_ref[...])
        vv = v_ref[...]
        v0_s[...] = _bf(jnp.where(head0, vv, 0.0))
        v1_s[...] = _bf(jnp.where(head0, 0.0, vv))

        def scores(qcat, kblk, slot):
            k0 = pl.multiple_of(jnp.maximum(kblk, 0) * blk, blk)
            z_s[slot] = _dot_nt(qcat, kb_s[pl.ds(k0, blk), :])

        def weights(slot, cr, diag):
            z = z_s[slot]
            sp = _softplus(z)
            spm = jnp.where(tri2, sp, 0.0) if diag else sp
            hi, lo = _trunc_split(spm)
            ps = _dot(jnp.concatenate([hi, lo], axis=1), u2)
            a = jnp.exp(z - (sp + ps + cr))
            if diag:
                a = jnp.where(tri2, a, 0.0)
            a_s[slot] = _bf(a)
            return cr + ps[:, 0:1] + spm[:, 0:1]

        def apply(acc, slot, k0):
            a = a_s[slot]
            vcat = jnp.concatenate([v0_s[pl.ds(k0, blk), :], v1_s[pl.ds(k0, blk), :]], axis=0)
            return acc + _dot(jnp.concatenate([a[:blk], a[blk:]], axis=1), vcat)

        def qblock(qi, _):
            q0 = pl.multiple_of(qi * blk, blk)
            qcat = jnp.concatenate([q0_s[pl.ds(q0, blk), :], q1_s[pl.ds(q0, blk), :]], axis=0)
            scores(qcat, qi, 0)
            scores(qcat, qi - 1, 1)
            cr = weights(0, jnp.zeros((2 * blk, 1), F32), True)

            def step(j, slot, st):
                acc = apply(st[0], 1 - slot, pl.multiple_of((qi - j + 1) * blk, blk))
                scores(qcat, qi - j - 1, 1 - slot)
                return acc, weights(slot, st[1], False)

            st = (jnp.zeros((blk, LANES), F32), cr)
            st = lax.fori_loop(0, qi // 2, lambda i, st: step(2 * i + 2, 0, step(2 * i + 1, 1, st)), st)
            st = lax.cond(qi % 2 == 1, lambda st: step(qi, 1, st), lambda st: st, st)
            o_ref[pl.ds(q0, blk), :] = apply(st[0], qi & 1, 0)
            tot_ref[pl.ds(q0, blk), :] = jnp.where(lane2 == 0, st[1][:blk], st[1][blk:])
            return 0

        lax.fori_loop(0, nq, qblock, 0)

    t = nb * s
    return pl.pallas_call(
        body, grid=(nb, 8),
        in_specs=[pl.BlockSpec((s, LANES), lambda b, p: (b, CB_Q + p)),
                  pl.BlockSpec((s, LANES), lambda b, p: (b, CB_K + p)),
                  pl.BlockSpec((s, LANES), lambda b, p: (b, CB_V + p))],
        out_specs=[pl.BlockSpec((s, LANES), lambda b, p: (b, p)),
                   pl.BlockSpec((None, None, s, 2), lambda b, p: (b, p, 0, 0))],
        out_shape=[jax.ShapeDtypeStruct((t, SB_WIDTH), F32),
                   jax.ShapeDtypeStruct((nb, 8, s, 2), F32)],
        scratch_shapes=[pltpu.VMEM((s, LANES), BF16)] * 5 + [
            pltpu.VMEM((2, 2 * blk, blk), F32), pltpu.VMEM((2, 2 * blk, blk), BF16)],
        compiler_params=_params(("parallel", "parallel")), name="attn_fwd")(proj, proj, proj)


def _attn_bwd(proj, do, tot, nb, s):
    blk = min(ATT_BLK, s)
    nq = s // blk
    scale = HEAD_DIM ** -0.5

    def body(q_ref, k_ref, v_ref, do_ref, tot_ref, dq_ref, dk_ref, dv_ref,
             q0_s, q1_s, kb_s, k0_s, k1_s, vb_s, d0_s, d1_s, z_s, da_s, dz_s, a_s, dk_acc, dv_acc):
        head0 = lax.broadcasted_iota(jnp.int32, (s, LANES), 1) < HEAD_DIM
        r_io, c_io = _attn_masks(blk)
        tri = c_io < r_io
        tri2 = jnp.concatenate([tri, tri], axis=0)
        u_le = _bf(r_io <= c_io)
        u2 = jnp.concatenate([u_le, u_le], axis=0)
        u_lt = _bf(r_io < c_io)
        qv = q_ref[...] * scale
        q0_s[...] = _bf(jnp.where(head0, qv, 0.0))
        q1_s[...] = _bf(jnp.where(head0, 0.0, qv))
        kv = k_ref[...]
        kb_s[...] = _bf(kv)
        k0_s[...] = _bf(jnp.where(head0, kv, 0.0))
        k1_s[...] = _bf(jnp.where(head0, 0.0, kv))
        vb_s[...] = _bf(v_ref[...])
        dov = do_ref[...]
        d0_s[...] = _bf(jnp.where(head0, dov, 0.0))
        d1_s[...] = _bf(jnp.where(head0, 0.0, dov))
        dk_acc[...] = jnp.zeros_like(dk_acc)
        dv_acc[...] = jnp.zeros_like(dv_acc)

        def qblock(qi, _):
            q0 = pl.multiple_of(qi * blk, blk)
            qs_ = pl.ds(q0, blk)
            qhs = (q0_s[qs_, :], q1_s[qs_, :])
            dhs = (d0_s[qs_, :], d1_s[qs_, :])
            qcat = jnp.concatenate(qhs, axis=0)
            dcat = jnp.concatenate(dhs, axis=0)
            totf = tot_ref[qs_, :]

            def scores(kblk, slot):
                ks_ = pl.ds(pl.multiple_of(kblk * blk, blk), blk)
                z_s[slot] = _dot_nt(qcat, kb_s[ks_, :])
                da_s[slot] = _dot_nt(dcat, vb_s[ks_, :])

            def grads(slot, st, diag):
                cn, cw = st
                z = z_s[slot]
                sp = _softplus(z)
                lb = z - sp
                spm = jnp.where(tri2, sp, 0.0) if diag else sp
                hi, lo = _trunc_split(spm)
                pin = _dot(jnp.concatenate([hi, lo], axis=1), u2)
                a = jnp.exp(lb + (pin + cn))
                if diag:
                    a = jnp.where(tri2, a, 0.0)
                w = a * da_s[slot]
                wex = _dot(_bf(w), u_lt)
                dz = w - jnp.exp(lb) * (w + (wex + cw))
                if diag:
                    dz = jnp.where(tri2, dz, 0.0)
                dz_s[slot] = _bf(dz)
                a_s[slot] = _bf(a)
                return cn + pin[:, blk - 1:blk], cw + wex[:, blk - 1:blk] + w[:, blk - 1:blk]

            def apply(dq, slot, kblk):
                ks_ = pl.ds(pl.multiple_of(kblk * blk, blk), blk)
                dz = dz_s[slot]
                dq = dq + _dot(jnp.concatenate([dz[:blk], dz[blk:]], axis=1),
                               jnp.concatenate([k0_s[ks_, :], k1_s[ks_, :]], axis=0))
                dk_acc[ks_, :] += _dot_tn(dz, qcat)
                dv_acc[ks_, :] += _dot_tn(a_s[slot], dcat)
                return dq

            def step(j, slot, st, diag):
                dq = apply(st[0], 1 - slot, jnp.maximum(j - 1, 0))
                if not diag:
                    scores(j + 1, 1 - slot)
                return (dq, *grads(slot, st[1:], diag))

            dz_s[1] = jnp.zeros_like(dz_s[1])
            a_s[1] = jnp.zeros_like(a_s[1])
            scores(0, 0)
            st = (jnp.zeros((blk, LANES), F32), -jnp.concatenate([totf[:, 0:1], totf[:, 1:2]], axis=0),
                  jnp.zeros((2 * blk, 1), F32))
            st = lax.fori_loop(
                0, qi // 2, lambda i, st: step(2 * i + 1, 1, step(2 * i, 0, st, False), False), st)
            st = lax.cond(qi % 2 == 1,
                          lambda st: step(qi, 1, step(qi - 1, 0, st, False), True),
                          lambda st: step(qi, 0, st, True), st)
            dq_ref[qs_, :] = _bf(apply(st[0], qi & 1, qi) * scale)
            return 0

        lax.fori_loop(0, nq, qblock, 0)
        dk_ref[...] = _bf(dk_acc[...])
        dv_ref[...] = _bf(dv_acc[...])

    t = nb * s
    hp = lambda b, p: (b, p)
    out = jax.ShapeDtypeStruct((t, SB_WIDTH), BF16)
    return pl.pallas_call(
        body, grid=(nb, 8),
        in_specs=[pl.BlockSpec((s, LANES), lambda b, p: (b, CB_Q + p)),
                  pl.BlockSpec((s, LANES), lambda b, p: (b, CB_K + p)),
                  pl.BlockSpec((s, LANES), lambda b, p: (b, CB_V + p)),
                  pl.BlockSpec((s, LANES), hp),
                  pl.BlockSpec((None, None, s, 2), lambda b, p: (b, p, 0, 0))],
        out_specs=[pl.BlockSpec((s, LANES), hp)] * 3,
        out_shape=[out, out, out],
        scratch_shapes=[pltpu.VMEM((s, LANES), BF16)] * 8 + [
            pltpu.VMEM((2, 2 * blk, blk), F32), pltpu.VMEM((2, 2 * blk, blk), F32),
            pltpu.VMEM((2, 2 * blk, blk), BF16), pltpu.VMEM((2, 2 * blk, blk), BF16),
            pltpu.VMEM((s, LANES), F32), pltpu.VMEM((s, LANES), F32)],
        compiler_params=_params(("parallel", "parallel")), name="attn_bwd")(proj, proj, proj, do, tot)


CONV_TC = 256
CONV_ROWS = 64
HALO = 8


def _conv_rows(ws, w_ref, b_ref, n):
    taps = [ws[HALO - kk:HALO - kk + n, :] for kk in range(CONV_K)]
    y = b_ref[...] + w_ref[CONV_K - 1:CONV_K, :] * taps[0]
    for kk in range(1, CONV_K):
        y = y + w_ref[CONV_K - 1 - kk:CONV_K - kk, :] * taps[kk]
    return y, taps


def _conv_fwd(proj, conv_w, conv_b, nb, s):
    tc, rows = CONV_TC, min(CONV_ROWS, s)
    cb0 = CB_XBC * LANES // tc

    def body(x_ref, w_ref, b_ref, o_ref, xp, ws):
        xp[0:HALO, :] = jnp.zeros((HALO, tc), F32)
        xp[HALO:HALO + s, :] = x_ref[...].astype(F32)

        def chunk(i, _):
            r0 = pl.multiple_of(i * rows, rows)
            ws[...] = xp[pl.ds(r0, rows + HALO), :]
            y, _ = _conv_rows(ws, w_ref, b_ref, rows)
            o_ref[pl.ds(r0, rows), :] = y * _sigmoid(y)
            return 0

        lax.fori_loop(0, s // rows, chunk, 0)

    t = nb * s
    return pl.pallas_call(
        body, grid=(nb, CONV_DIM // tc),
        in_specs=[pl.BlockSpec((s, tc), lambda b, c: (b, cb0 + c)),
                  pl.BlockSpec((CONV_K, tc), lambda b, c: (0, c)),
                  pl.BlockSpec((1, tc), lambda b, c: (0, c))],
        out_specs=pl.BlockSpec((s, tc), lambda b, c: (b, c)),
        out_shape=jax.ShapeDtypeStruct((t, CONV_DIM), F32),
        scratch_shapes=[pltpu.VMEM((s + HALO, tc), F32), pltpu.VMEM((rows + HALO, tc), F32)],
        compiler_params=_params(("parallel", "parallel")), name="conv_fwd")(proj, conv_w, conv_b)


def _conv_bwd(proj, dact, conv_w, conv_b, nb, s, *, col0, name):
    width = dact.shape[1]
    tc, rows = min(CONV_TC, width), min(CONV_ROWS, s)
    cb0 = (CB_XBC * LANES + col0) // tc
    wb0 = col0 // tc

    def body(x_ref, d_ref, w_ref, b_ref, dx_ref, dw_ref, db_ref, xp, dp, dys, ws):
        @pl.when(pl.program_id(1) == 0)
        def _():
            dw_ref[...] = jnp.zeros_like(dw_ref)
            db_ref[...] = jnp.zeros_like(db_ref)

        n = rows + HALO
        pad = jnp.zeros((HALO, tc), F32)
        xp[0:HALO, :] = pad
        xp[HALO:HALO + s, :] = x_ref[...].astype(F32)
        xp[HALO + s:HALO + s + HALO, :] = pad
        dp[0:s, :] = d_ref[...]
        dp[s:s + HALO, :] = pad

        def chunk(i, acc):
            r0 = pl.multiple_of(i * rows, rows)
            ws[...] = xp[pl.ds(r0, n + HALO), :]
            y, taps = _conv_rows(ws, w_ref, b_ref, n)
            sg = _sigmoid(y)
            dy = dp[pl.ds(r0, n), :] * (sg * (1.0 + y * (1.0 - sg)))
            dys[...] = dy
            dx = w_ref[CONV_K - 1:CONV_K, :] * dy[0:rows]
            for kk in range(1, CONV_K):
                dx = dx + w_ref[CONV_K - 1 - kk:CONV_K - kk, :] * dys[kk:kk + rows, :]
            dx_ref[pl.ds(r0, rows), :] = _bf(dx)
            dyc = dy[0:rows]
            new = [acc[0] + jnp.sum(dyc, axis=0, keepdims=True)]
            for kk in range(CONV_K):
                new.append(acc[1 + kk] + jnp.sum(dyc * taps[kk][0:rows], axis=0, keepdims=True))
            return tuple(new)

        acc = lax.fori_loop(0, s // rows, chunk, (jnp.zeros((1, tc), F32),) * (1 + CONV_K))
        db_ref[...] += acc[0]
        for kk in range(CONV_K):
            dw_ref[CONV_K - 1 - kk:CONV_K - kk, :] += acc[1 + kk]

    t = nb * s
    return pl.pallas_call(
        body, grid=(width // tc, nb),
        in_specs=[pl.BlockSpec((s, tc), lambda c, b: (b, cb0 + c)),
                  pl.BlockSpec((s, tc), lambda c, b: (b, c)),
                  pl.BlockSpec((CONV_K, tc), lambda c, b: (0, wb0 + c)),
                  pl.BlockSpec((1, tc), lambda c, b: (0, wb0 + c))],
        out_specs=[pl.BlockSpec((s, tc), lambda c, b: (b, c)),
                   pl.BlockSpec((CONV_K, tc), lambda c, b: (0, c)),
                   pl.BlockSpec((1, tc), lambda c, b: (0, c))],
        out_shape=[jax.ShapeDtypeStruct((t, width), BF16),
                   jax.ShapeDtypeStruct((CONV_K, width), F32),
                   jax.ShapeDtypeStruct((1, width), F32)],
        scratch_shapes=[pltpu.VMEM((s + 2 * HALO, tc), F32), pltpu.VMEM((s + HALO, tc), F32),
                        pltpu.VMEM((rows + HALO, tc), F32), pltpu.VMEM((rows + 2 * HALO, tc), F32)],
        compiler_params=_params(("parallel", "arbitrary")), name=name)(proj, dact, conv_w, conv_b)


L = SSD_CHUNK
N_PAIR = SSD_HPG // 2


def _pair_lanes(v0, v1, lane):
    return jnp.where(lane < HEAD_DIM, v0, v1)


def _onehot(shape, row_shift, col_shift):
    row = lax.broadcasted_iota(jnp.int32, shape, 0)
    col = lax.broadcasted_iota(jnp.int32, shape, 1)
    return _bf(lax.shift_right_logical(row, row_shift) == lax.shift_right_logical(col, col_shift))


def _ssd_tables():
    sp = _onehot((LANES, N_PAIR * LANES), 0, 6)
    sf = _onehot((LANES, SSD_HPG * LANES), 0, 7)
    return (jnp.concatenate([sp] * 3, axis=0), jnp.concatenate([sf] * 3, axis=0),
            _onehot((N_PAIR * LANES, LANES), 6, 0), _onehot((SSD_HPG * LANES, LANES), 7, 0))


def _split3(a):
    a1 = _bf(a)
    r1 = a - a1.astype(F32)
    a2 = _bf(r1)
    return jnp.concatenate([a1, a2, _bf(r1 - a2.astype(F32))], axis=1)


def _split2(a):
    hi, lo = _trunc_split(a)
    return jnp.concatenate([hi, lo], axis=1)


def _ssd_chunk_common(dtr_ref, bias_ref, alog_ref, r0, spread_pair, spread_full):
    rr = lax.broadcasted_iota(jnp.int32, (L, L), 0)
    cc = lax.broadcasted_iota(jnp.int32, (L, L), 1)
    tri = (cc <= rr).astype(F32)
    raw = dtr_ref[pl.ds(r0, L), :] + bias_ref[...]
    dt = _softplus(raw)
    a_neg = -jnp.exp(alog_ref[...])
    acs = _dot_hi(tri, dt * a_neg)
    acs3 = _split3(acs)
    both = _dot(jnp.concatenate([_split3(dt), acs3], axis=0), spread_pair)
    return raw, dt, a_neg, acs, acs.T, rr, cc, both[:L], both[L:], _dot(acs3, spread_full)


def _ssd_fwd(xbc, dtr, bias, alog, dskip, nb, s):
    nc = s // L
    t = nb * s

    def body(x_ref, b_ref, c_ref, dtr_ref, bias_ref, alog_ref, dsk_ref, y_ref, hs_ref, h_scr):
        lane = lax.broadcasted_iota(jnp.int32, (L, LANES), 1)
        lane1 = lax.broadcasted_iota(jnp.int32, (1, LANES), 1)
        rowc = lax.broadcasted_iota(jnp.int32, (L, 1), 0)
        h_scr[...] = jnp.zeros_like(h_scr)
        dsk = dsk_ref[...]
        spread_pair, spread_full, _, _ = _ssd_tables()

        def chunk(c, _):
            r0 = pl.multiple_of(c * L, L)
            raw, dt, a_neg, acs, acs_t, rr, cc, dt_all, acs_all, acs_full = _ssd_chunk_common(
                dtr_ref, bias_ref, alog_ref, r0, spread_pair, spread_full)
            causal = rr >= cc
            bb = _bf(b_ref[pl.ds(r0, L), :])
            cb = _bf(c_ref[pl.ds(r0, L), :])
            gm = _dot_nt(cb, bb)
            for pr in range(N_PAIR):
                j0, j1 = 2 * pr, 2 * pr + 1
                ms = []
                for j in (j0, j1):
                    seg = acs_full[:, j * LANES:(j + 1) * LANES] - acs_t[j:j + 1, :]
                    ms.append(_bf(gm * jnp.exp(jnp.where(causal, seg, -1e30))))
                x = x_ref[pl.ds(r0, L), pr * LANES:(pr + 1) * LANES]
                dt_p = dt_all[:, pr * LANES:(pr + 1) * LANES]
                acs_p = acs_all[:, pr * LANES:(pr + 1) * LANES]
                last_p = acs_p[L - 1:L, :]
                d_p = _pair_lanes(dsk[:, j0:j0 + 1], dsk[:, j1:j1 + 1], lane1)
                xd = x * dt_p
                xdb = _bf(xd)
                yd = _pair_lanes(_dot(ms[0], xdb), _dot(ms[1], xdb), lane)
                hp = h_scr[pr]
                yo = _dot_nt(cb, _bf(hp)) * jnp.exp(acs_p)
                y_ref[pl.ds(r0, L), pr * LANES:(pr + 1) * LANES] = yd + yo + x * d_p
                hs_ref[c, pr] = hp
                sp = _dot_tn(_bf(xd * jnp.exp(last_p - acs_p)), bb)
                cd = jnp.where(rowc < HEAD_DIM, jnp.exp(acs[L - 1:L, j0:j0 + 1]),
                               jnp.exp(acs[L - 1:L, j1:j1 + 1]))
                h_scr[pr] = hp * cd + sp
            return 0

        lax.fori_loop(0, nc, chunk, 0)

    grp = lambda b, g: (g, 0, 0)
    return pl.pallas_call(
        body, grid=(nb, SSD_GROUPS),
        in_specs=[pl.BlockSpec((s, 4 * LANES), lambda b, g: (b, g)),
                  pl.BlockSpec((s, LANES), lambda b, g: (b, 16 + g)),
                  pl.BlockSpec((s, LANES), lambda b, g: (b, 20 + g)),
                  pl.BlockSpec((None, s, LANES), lambda b, g: (g, b, 0)),
                  pl.BlockSpec((None, 1, LANES), grp), pl.BlockSpec((None, 1, LANES), grp),
                  pl.BlockSpec((None, 1, LANES), grp)],
        out_specs=[pl.BlockSpec((s, 4 * LANES), lambda b, g: (b, g)),
                   pl.BlockSpec((None, None, nc, N_PAIR, L, LANES), lambda b, g: (b, g, 0, 0, 0, 0))],
        out_shape=[jax.ShapeDtypeStruct((t, SSD_WIDTH), F32),
                   jax.ShapeDtypeStruct((nb, SSD_GROUPS, nc, N_PAIR, L, LANES), F32)],
        scratch_shapes=[pltpu.VMEM((N_PAIR, L, LANES), F32)],
        compiler_params=_params(("parallel", "parallel")), name="ssd_fwd")(
            xbc, xbc, xbc, dtr, bias, alog, dskip)


def _ssd_bwd(xbc, dtr, bias, alog, dskip, hs, dy, nb, s):
    nc = s // L
    t = nb * s

    def body(x_ref, b_ref, c_ref, dtr_ref, bias_ref, alog_ref, dsk_ref, hs_ref, dy_ref,
             dx_ref, db_ref, dc_ref, ddtr_ref, dbias_ref, dalog_ref, ddsk_ref, dh_scr):
        lane = lax.broadcasted_iota(jnp.int32, (L, LANES), 1)
        lane1 = lax.broadcasted_iota(jnp.int32, (1, LANES), 1)
        rowc = lax.broadcasted_iota(jnp.int32, (L, 1), 0)
        dh_scr[...] = jnp.zeros_like(dh_scr)
        dsk = dsk_ref[...]

        @pl.when(pl.program_id(1) == 0)
        def _():
            dbias_ref[...] = jnp.zeros_like(dbias_ref)
            dalog_ref[...] = jnp.zeros_like(dalog_ref)
            ddsk_ref[...] = jnp.zeros_like(ddsk_ref)

        spread_pair, spread_full, sum_pair, sum_full = _ssd_tables()
        sum_pair2 = jnp.concatenate([sum_pair] * 2, axis=0)
        sum_full2 = jnp.concatenate([sum_full] * 2, axis=0)

        def chunk(i, _):
            c = nc - 1 - i
            r0 = pl.multiple_of(c * L, L)
            raw, dt, a_neg, acs, acs_t, rr, cc, dt_all, acs_all, acs_full = _ssd_chunk_common(
                dtr_ref, bias_ref, alog_ref, r0, spread_pair, spread_full)
            causal = rr >= cc
            tri_t = (cc >= rr).astype(F32)
            bb = _bf(b_ref[pl.ds(r0, L), :])
            cb = _bf(c_ref[pl.ds(r0, L), :])
            gm = _dot_nt(cb, bb)
            dg = jnp.zeros((L, L), F32)
            dbacc = jnp.zeros((L, LANES), F32)
            dcacc = jnp.zeros((L, LANES), F32)
            dacs_t = jnp.zeros((L, L), F32)
            ta_all, te_all, qm_all, tc_all, td_all, dcd_all = [], [], [], [], [], []
            for pr in range(N_PAIR):
                js = (2 * pr, 2 * pr + 1)
                lms, mfs = [], []
                for j in js:
                    seg = acs_full[:, j * LANES:(j + 1) * LANES] - acs_t[j:j + 1, :]
                    lm = jnp.exp(jnp.where(causal, seg, -1e30))
                    lms.append(lm)
                    mfs.append(gm * lm)
                x = x_ref[pl.ds(r0, L), pr * LANES:(pr + 1) * LANES]
                dyv = dy_ref[pl.ds(r0, L), pr * LANES:(pr + 1) * LANES]
                dt_p = dt_all[:, pr * LANES:(pr + 1) * LANES]
                acs_p = acs_all[:, pr * LANES:(pr + 1) * LANES]
                last_p = acs_p[L - 1:L, :]
                d_p = _pair_lanes(dsk[:, js[0]:js[0] + 1], dsk[:, js[1]:js[1] + 1], lane1)
                e_p = jnp.exp(acs_p)
                dte_p = jnp.exp(last_p - acs_p)
                xd = x * dt_p
                xdb = _bf(xd)
                hp = hs_ref[c, pr]
                hb = _bf(hp)
                dye = _bf(dyv * e_p)
                yo = _dot_nt(cb, hb) * e_p
                dcacc = dcacc + _dot(dye, hb)
                dh_out = _dot_tn(dye, cb)
                dyb = _bf(dyv)
                dxd = jnp.zeros((L, LANES), F32)
                for hh in range(2):
                    hm = (lane >= HEAD_DIM) if hh else (lane < HEAD_DIM)
                    dm = _dot_nt(_bf(jnp.where(hm, dyv, 0.0)), xdb)
                    dxd = dxd + jnp.where(hm, _dot_tn(_bf(mfs[hh]), dyb), 0.0)
                    dg = dg + dm * lms[hh]
                    qm = dm * mfs[hh]
                    qm_all.append(qm)
                    dacs_t = dacs_t - jnp.where(rr == js[hh], jnp.sum(qm, axis=0, keepdims=True), 0.0)
                ds = dh_scr[pr]
                dsb = _bf(ds)
                xdd = xd * dte_p
                dxdd = _dot_nt(bb, dsb)
                dbacc = dbacc + _dot(_bf(xdd), dsb)
                dxd = dxd + dxdd * dte_p
                cd = jnp.where(rowc < HEAD_DIM, jnp.exp(acs[L - 1:L, js[0]:js[0] + 1]),
                               jnp.exp(acs[L - 1:L, js[1]:js[1] + 1]))
                prod = ds * hp
                dcd_all += [jnp.sum(prod[:HEAD_DIM], axis=0, keepdims=True),
                            jnp.sum(prod[HEAD_DIM:], axis=0, keepdims=True)]
                dh_scr[pr] = dh_out + ds * cd
                tcv = dxdd * xdd
                ta_all.append(dyv * yo - tcv)
                tc_all.append(jnp.sum(tcv, axis=0, keepdims=True))
                te_all.append(dxd * x)
                td_all.append(jnp.sum(dyv * x, axis=0, keepdims=True))
                dx_ref[pl.ds(r0, L), pr * LANES:(pr + 1) * LANES] = dxd * dt_p + dyv * d_p
            dgb = _bf(dg)
            dc_ref[pl.ds(r0, L), :] = dcacc + _dot(dgb, bb)
            db_ref[pl.ds(r0, L), :] = dbacc + _dot_tn(dgb, cb)
            last_add = jnp.zeros((1, LANES), F32)
            dd_add = jnp.zeros((1, LANES), F32)
            cd_row = jnp.exp(acs[L - 1:L, :])
            for j in range(SSD_HPG):
                hm = (lane1 >= HEAD_DIM) if j % 2 else (lane1 < HEAD_DIM)
                tcj = jnp.sum(jnp.where(hm, tc_all[j // 2], 0.0), axis=1, keepdims=True)
                tdj = jnp.sum(jnp.where(hm, td_all[j // 2], 0.0), axis=1, keepdims=True)
                dcdj = jnp.sum(dcd_all[j], axis=1, keepdims=True)
                last_add = last_add + jnp.where(lane1 == j, tcj + dcdj * cd_row, 0.0)
                dd_add = dd_add + jnp.where(lane1 == j, tdj, 0.0)
            dacs = (_dot(_split2(jnp.concatenate(ta_all, axis=1)), sum_pair2)
                    + _dot(_split2(jnp.concatenate(qm_all, axis=1)), sum_full2)
                    + dacs_t.T + jnp.where(rowc == L - 1, last_add, 0.0))
            ddta = _dot_hi(tri_t, dacs)
            ddt = _dot(_bf(jnp.concatenate(te_all, axis=1)), sum_pair) + ddta * a_neg
            ddraw = ddt * _sigmoid(raw)
            ddtr_ref[pl.ds(r0, L), :] = ddraw
            dbias_ref[...] += jnp.sum(ddraw, axis=0, keepdims=True)
            dalog_ref[...] += jnp.sum(ddta * dt, axis=0, keepdims=True) * a_neg
            ddsk_ref[...] += dd_add
            return 0

        lax.fori_loop(0, nc, chunk, 0)

    grp = lambda g, b: (g, 0, 0)
    small = jax.ShapeDtypeStruct((SSD_GROUPS, 1, LANES), F32)
    return pl.pallas_call(
        body, grid=(SSD_GROUPS, nb),
        in_specs=[pl.BlockSpec((s, 4 * LANES), lambda g, b: (b, g)),
                  pl.BlockSpec((s, LANES), lambda g, b: (b, 16 + g)),
                  pl.BlockSpec((s, LANES), lambda g, b: (b, 20 + g)),
                  pl.BlockSpec((None, s, LANES), lambda g, b: (g, b, 0)),
                  pl.BlockSpec((None, 1, LANES), grp), pl.BlockSpec((None, 1, LANES), grp),
                  pl.BlockSpec((None, 1, LANES), grp),
                  pl.BlockSpec((None, None, nc, N_PAIR, L, LANES), lambda g, b: (b, g, 0, 0, 0, 0)),
                  pl.BlockSpec((s, 4 * LANES), lambda g, b: (b, g))],
        out_specs=[pl.BlockSpec((s, 4 * LANES), lambda g, b: (b, g)),
                   pl.BlockSpec((s, LANES), lambda g, b: (b, g)),
                   pl.BlockSpec((s, LANES), lambda g, b: (b, g)),
                   pl.BlockSpec((None, s, LANES), lambda g, b: (g, b, 0)),
                   pl.BlockSpec((None, 1, LANES), grp), pl.BlockSpec((None, 1, LANES), grp),
                   pl.BlockSpec((None, 1, LANES), grp)],
        out_shape=[jax.ShapeDtypeStruct((t, SSD_WIDTH), F32),
                   jax.ShapeDtypeStruct((t, SSD_GROUPS * SSD_STATE), F32),
                   jax.ShapeDtypeStruct((t, SSD_GROUPS * SSD_STATE), F32),
                   jax.ShapeDtypeStruct((SSD_GROUPS, t, LANES), F32), small, small, small],
        scratch_shapes=[pltpu.VMEM((N_PAIR, L, LANES), F32)],
        compiler_params=_params(("parallel", "arbitrary")), name="ssd_bwd")(
            xbc, xbc, xbc, dtr, bias, alog, dskip, hs, dy)


GN = SSD_WIDTH // SSD_GROUPS


def _post_fwd(o, proj, y, nw, *, tm=256):
    t = o.shape[0]

    def body(o_ref, za_ref, y_ref, zs_ref, nw_ref, ya_ref, ys_ref):
        za = za_ref[...].astype(F32)
        ya_ref[...] = _bf(o_ref[...] * (za * _sigmoid(za)))
        zs = zs_ref[...].astype(F32)
        yz = y_ref[...] * (zs * _sigmoid(zs))
        for g in range(SSD_GROUPS):
            sl = slice(g * GN, (g + 1) * GN)
            v = yz[:, sl]
            r = lax.rsqrt(jnp.mean(v * v, axis=-1, keepdims=True) + EPS)
            ys_ref[:, sl] = _bf(v * r * nw_ref[:, sl])

    return pl.pallas_call(
        body, grid=(t // tm,),
        in_specs=[pl.BlockSpec((tm, SB_WIDTH), lambda i: (i, 0)),
                  pl.BlockSpec((tm, SB_WIDTH), lambda i: (i, CB_ZA * LANES // SB_WIDTH)),
                  pl.BlockSpec((tm, SSD_WIDTH), lambda i: (i, 0)),
                  pl.BlockSpec((tm, SSD_WIDTH), lambda i: (i, CB_ZS * LANES // SSD_WIDTH)),
                  pl.BlockSpec((1, SSD_WIDTH), lambda i: (0, 0))],
        out_specs=[pl.BlockSpec((tm, SB_WIDTH), lambda i: (i, 0)),
                   pl.BlockSpec((tm, SSD_WIDTH), lambda i: (i, 0))],
        out_shape=[jax.ShapeDtypeStruct((t, SB_WIDTH), BF16), jax.ShapeDtypeStruct((t, SSD_WIDTH), BF16)],
        compiler_params=_params(("parallel",)), name="post_fwd")(o, proj, y, proj, nw)


def _post_bwd(dya, o, proj, dys, y, nw, *, tm=256):
    t = o.shape[0]

    def body(dya_ref, o_ref, za_ref, dys_ref, y_ref, zs_ref, nw_ref,
             do_ref, dza_ref, dy_ref, dzs_ref, dnw_ref):
        @pl.when(pl.program_id(0) == 0)
        def _():
            dnw_ref[...] = jnp.zeros_like(dnw_ref)

        za = za_ref[...].astype(F32)
        sa = _sigmoid(za)
        d = dya_ref[...]
        do_ref[...] = d * (za * sa)
        dza_ref[...] = _bf(d * o_ref[...] * (sa * (1.0 + za * (1.0 - sa))))
        zs = zs_ref[...].astype(F32)
        ss = _sigmoid(zs)
        silu = zs * ss
        yv = y_ref[...]
        yz = yv * silu
        dv = dys_ref[...]
        for g in range(SSD_GROUPS):
            sl = slice(g * GN, (g + 1) * GN)
            v = yz[:, sl]
            r = lax.rsqrt(jnp.mean(v * v, axis=-1, keepdims=True) + EPS)
            nrm = v * r
            dg = dv[:, sl]
            dnw_ref[:, sl] += jnp.sum(dg * nrm, axis=0, keepdims=True)
            dn = dg * nw_ref[:, sl]
            dyz = r * (dn - nrm * jnp.mean(dn * nrm, axis=-1, keepdims=True))
            dy_ref[:, sl] = dyz * silu[:, sl]
            dzs_ref[:, sl] = _bf(dyz * yv[:, sl] * (ss[:, sl] * (1.0 + zs[:, sl] * (1.0 - ss[:, sl]))))

    a_spec = pl.BlockSpec((tm, SB_WIDTH), lambda i: (i, 0))
    s_spec = pl.BlockSpec((tm, SSD_WIDTH), lambda i: (i, 0))
    w_spec = pl.BlockSpec((1, SSD_WIDTH), lambda i: (0, 0))
    return pl.pallas_call(
        body, grid=(t // tm,),
        in_specs=[a_spec, a_spec, pl.BlockSpec((tm, SB_WIDTH), lambda i: (i, CB_ZA * LANES // SB_WIDTH)),
                  s_spec, s_spec, pl.BlockSpec((tm, SSD_WIDTH), lambda i: (i, CB_ZS * LANES // SSD_WIDTH)),
                  w_spec],
        out_specs=[a_spec, a_spec, s_spec, s_spec, w_spec],
        out_shape=[jax.ShapeDtypeStruct((t, SB_WIDTH), F32), jax.ShapeDtypeStruct((t, SB_WIDTH), BF16),
                   jax.ShapeDtypeStruct((t, SSD_WIDTH), F32), jax.ShapeDtypeStruct((t, SSD_WIDTH), BF16),
                   jax.ShapeDtypeStruct((1, SSD_WIDTH), F32)],
        compiler_params=_params(("arbitrary",)), name="post_bwd")(dya, o, proj, dys, y, proj, nw)


def _merge_fwd(ya, ys, proj_g, *, tm=512):
    t = ya.shape[0]

    def body(ya_ref, ys_ref, ga_ref, gs_ref, m_ref):
        m_ref[...] = _bf(_sigmoid(ga_ref[...].astype(F32)) * ya_ref[...]
                         + _sigmoid(gs_ref[...].astype(F32)) * ys_ref[...])

    spec = pl.BlockSpec((tm, D_MODEL), lambda i: (i, 0))
    return pl.pallas_call(
        body, grid=(t // tm,),
        in_specs=[spec, spec, pl.BlockSpec((tm, D_MODEL), lambda i: (i, 0)),
                  pl.BlockSpec((tm, D_MODEL), lambda i: (i, 1))],
        out_specs=spec, out_shape=jax.ShapeDtypeStruct((t, D_MODEL), BF16),
        compiler_params=_params(("parallel",)), name="merge_fwd")(ya, ys, proj_g, proj_g)


def _merge_bwd(dm, ya, ys, proj_g, *, tm=512):
    t = ya.shape[0]

    def body(dm_ref, ya_ref, ys_ref, ga_ref, gs_ref, dya_ref, dys_ref, dg_ref):
        d = dm_ref[...]
        ga = _sigmoid(ga_ref[...].astype(F32))
        gs = _sigmoid(gs_ref[...].astype(F32))
        dya_ref[...] = _bf(d * ga)
        dys_ref[...] = _bf(d * gs)
        dg_ref[:, :D_MODEL] = _bf(d * ya_ref[...] * (ga * (1.0 - ga)))
        dg_ref[:, D_MODEL:] = _bf(d * ys_ref[...] * (gs * (1.0 - gs)))

    spec = pl.BlockSpec((tm, D_MODEL), lambda i: (i, 0))
    return pl.pallas_call(
        body, grid=(t // tm,),
        in_specs=[spec, spec, spec, pl.BlockSpec((tm, D_MODEL), lambda i: (i, 0)),
                  pl.BlockSpec((tm, D_MODEL), lambda i: (i, 1))],
        out_specs=[spec, spec, pl.BlockSpec((tm, 2 * D_MODEL), lambda i: (i, 0))],
        out_shape=[jax.ShapeDtypeStruct((t, D_MODEL), BF16), jax.ShapeDtypeStruct((t, D_MODEL), BF16),
                   jax.ShapeDtypeStruct((t, 2 * D_MODEL), BF16)],
        compiler_params=_params(("parallel",)), name="merge_bwd")(dm, ya, ys, proj_g, proj_g)


def _final(x, out, wf, target, *, tm=512):
    t = x.shape[0]

    def body(x_ref, o_ref, w_ref, t_ref, loss_ref, dx_ref, dw_ref):
        @pl.when(pl.program_id(0) == 0)
        def _():
            loss_ref[...] = jnp.zeros_like(loss_ref)
            dw_ref[...] = jnp.zeros_like(dw_ref)

        x2 = x_ref[...] + o_ref[...]
        r = lax.rsqrt(jnp.mean(x2 * x2, axis=-1, keepdims=True) + EPS)
        nrm = x2 * r
        e = nrm * w_ref[...] - t_ref[...]
        row_loss = jnp.mean(e * e, axis=-1, keepdims=True)
        loss_ref[...] += 0.5 * jnp.sum(row_loss, axis=0, keepdims=True)
        dyv = e * (1.0 / D_MODEL)
        dw_ref[...] += jnp.sum(dyv * nrm, axis=0, keepdims=True)
        dn = dyv * w_ref[...]
        dx_ref[...] = r * (dn - nrm * jnp.mean(dn * nrm, axis=-1, keepdims=True))

    spec = pl.BlockSpec((tm, D_MODEL), lambda i: (i, 0))
    fix = pl.BlockSpec((1, D_MODEL), lambda i: (0, 0))
    return pl.pallas_call(
        body, grid=(t // tm,),
        in_specs=[spec, spec, fix, spec],
        out_specs=[pl.BlockSpec((1, LANES), lambda i: (0, 0)), spec, fix],
        out_shape=[jax.ShapeDtypeStruct((1, LANES), F32), jax.ShapeDtypeStruct((t, D_MODEL), F32),
                   jax.ShapeDtypeStruct((1, D_MODEL), F32)],
        compiler_params=_params(("arbitrary",)), name="final_loss")(x, out, wf, target)


def _group_lanes(v):
    return jnp.pad(v.reshape(SSD_GROUPS, 1, SSD_HPG), ((0, 0), (0, 0), (0, LANES - SSD_HPG)))


def _ungroup_lanes(v):
    return v[:, :, :SSD_HPG].reshape(1, N_DT)


def _row_slabs(pieces, rows):
    total = sum(p.shape[0] for p in pieces)
    assert total % rows == 0
    slabs = []
    for lo in range(0, total, rows):
        parts, off = [], 0
        for p in pieces:
            a, b = max(lo, off), min(lo + rows, off + p.shape[0])
            if a < b:
                parts.append(p[a - off:b - off])
            off += p.shape[0]
        slabs.append(parts[0] if len(parts) == 1 else jnp.concatenate(parts, axis=0))
    return jnp.stack(slabs)


def _local_step(x, target, nb, s, w_in_t, conv_w, conv_b, norm_w, dt_bias, a_log, d_skip,
                ssm_norm_w, out_weights, final_norm_w, between=None):
    t = nb * s
    w_g_t = w_in_t[GATE_COL0:]
    w_dt_t = jnp.pad(w_in_t[DT_COL0:GATE_COL0], ((0, LANES - N_DT), (0, 0)))
    h, r1 = _rms_fwd(x, norm_w)
    proj = _matmul(h, w_in_t, nt=True, n=DT_COL0, out_dtype=BF16, name="proj_main")
    proj_g = _matmul(h, w_g_t, nt=True, out_dtype=BF16, name="proj_gate")
    dt_raw = _matmul(h, w_dt_t, nt=True, name="proj_dt")
    dtr = jnp.pad(dt_raw[:, :N_DT].reshape(t, SSD_GROUPS, SSD_HPG).transpose(1, 0, 2),
                  ((0, 0), (0, 0), (0, LANES - SSD_HPG)))
    bias_g, alog_g, dsk_g = _group_lanes(dt_bias), _group_lanes(a_log), _group_lanes(d_skip)

    o, tot = _attn_fwd(proj, nb, s)
    xbc = _conv_fwd(proj, conv_w, conv_b, nb, s)
    y, hs = _ssd_fwd(xbc, dtr, bias_g, alog_g, dsk_g, nb, s)
    ya_in, ys_in = _post_fwd(o, proj, y, ssm_norm_w)
    w_attn_out, w_ssm_out, w_o = out_weights(ys_in) if callable(out_weights) else out_weights
    ya = _matmul(ya_in, w_attn_out, name="attn_out")
    ys = _matmul(ys_in, w_ssm_out, name="ssm_out")
    merged = _merge_fwd(ya, ys, proj_g)
    out = _matmul(merged, w_o, name="out_proj")
    loss, dx2, d_final_w = _final(x, out, final_norm_w, target)

    dx2b = _bf(dx2)
    dmerged = _matmul(dx2b, w_o, nt=True, name="d_merged")
    d_w_o = _matmul(merged, dx2b, ta=True, name="d_w_o")
    dya, dys, dgate = _merge_bwd(dmerged, ya, ys, proj_g)
    d_w_attn_out = _matmul(ya_in, dya, ta=True, name="d_w_attn_out")
    d_w_ssm_out = _matmul(ys_in, dys, ta=True, name="d_w_ssm_out")
    dya_in = _matmul(dya, w_attn_out, nt=True, name="d_ya_in")
    dys_in = _matmul(dys, w_ssm_out, nt=True, name="d_ys_in")
    do, dza, dy, dzs, d_ssm_norm_w = _post_bwd(dya_in, o, proj, dys_in, y, ssm_norm_w)
    dq, dk, dv = _attn_bwd(proj, do, tot, nb, s)
    dxs_act, db_act, dc_act, ddtr, dbias_g, dalog_g, ddsk_g = _ssd_bwd(
        xbc, dtr, bias_g, alog_g, dsk_g, hs, dy, nb, s)
    dxs, dwx, dbx = _conv_bwd(proj, dxs_act, conv_w, conv_b, nb, s, col0=0, name="conv_bwd_x")
    dbm, dwb, dbb = _conv_bwd(proj, db_act, conv_w, conv_b, nb, s, col0=SSD_WIDTH, name="conv_bwd_b")
    dcm, dwc, dbc = _conv_bwd(proj, dc_act, conv_w, conv_b, nb, s, col0=SSD_WIDTH + 512, name="conv_bwd_c")
    dproj = jnp.concatenate([dq, dk, dv, dza, dzs, dxs, dbm, dcm], axis=1)
    ddt = jnp.pad(_bf(ddtr[:, :, :SSD_HPG].transpose(1, 0, 2).reshape(t, N_DT)), ((0, 0), (0, LANES - N_DT)))
    wt = dict(ta=True, out_dtype=BF16)
    d_w_in_t = _row_slabs(
        [_matmul(dproj, h, name="d_w_main", **wt), _matmul(ddt, h, name="d_w_dt", **wt)[:N_DT],
         _matmul(dgate, h, name="d_w_gate", **wt)], SHARD)
    grads = dict(
        w_in_t=d_w_in_t,
        conv_w=jnp.concatenate([dwx, dwb, dwc], axis=1), conv_b=jnp.concatenate([dbx, dbb, dbc], axis=1),
        dt_bias=_ungroup_lanes(dbias_g), a_log=_ungroup_lanes(dalog_g), d_skip=_ungroup_lanes(ddsk_g),
        ssm_norm_w=d_ssm_norm_w, w_attn_out=d_w_attn_out, w_ssm_out=d_w_ssm_out, w_o=d_w_o,
        final_norm_w=d_final_w)
    extra = None
    if between is not None:
        zero, extra = between(grads)
        ddt = ddt + _bf(zero[0:1, 0:1])

    dh = _matmul(ddt, w_dt_t, name="d_h_dt")
    dh = _matmul(dgate, w_g_t, add=dh, name="d_h_gate")
    dh = _matmul(dproj, w_in_t, add=dh, name="d_h", tk=DT_COL0 // 4)
    grad_x, grads["norm_w"] = _rms_bwd(dh, x, r1, norm_w, dx2)
    return loss, grad_x, grads, extra


N_CHIP = N_DEV // 2


def _pair_exchange(scatter, bcast, *, name):
    arrays = list(scatter) + list(bcast)
    ns, n = len(scatter), len(arrays)
    n_sem = ns * N_CHIP + (n - ns) * (N_DEV - 1)

    def body(*refs):
        ins, outs = refs[:n], refs[n:2 * n]
        send_sems, recv_sems, local_sems = refs[2 * n:]
        x, y, c = lax.axis_index("x"), lax.axis_index("y"), lax.axis_index("c")
        me = 4 * x + 2 * y + c
        copies, k = [], 0
        for a in range(ns):
            for q in range(N_CHIP):
                cp = pltpu.make_async_remote_copy(
                    src_ref=ins[a].at[2 * q + 1 - c], dst_ref=outs[a].at[q],
                    send_sem=send_sems.at[k], recv_sem=recv_sems.at[k],
                    device_id=(x, y, 1 - c), device_id_type=pl.DeviceIdType.MESH)
                cp.start()
                copies.append(cp)
                k += 1
        for a in range(ns, n):
            cp = pltpu.make_async_copy(ins[a], outs[a].at[me], local_sems.at[a - ns])
            cp.start()
            copies.append(cp)
            for r in range(1, N_DEV):
                tx = 1 - x if (r >> 2) & 1 else x
                ty = 1 - y if (r >> 1) & 1 else y
                tc = 1 - c if r & 1 else c
                cp = pltpu.make_async_remote_copy(
                    src_ref=ins[a], dst_ref=outs[a].at[me], send_sem=send_sems.at[k], recv_sem=recv_sems.at[k],
                    device_id=(tx, ty, tc), device_id_type=pl.DeviceIdType.MESH)
                cp.start()
                copies.append(cp)
                k += 1
        for cp in copies:
            cp.wait()

    out_shape = [jax.ShapeDtypeStruct((N_CHIP,) + v.shape[1:] if i < ns else (N_DEV,) + v.shape, v.dtype)
                 for i, v in enumerate(arrays)]
    return pl.pallas_call(
        body, in_specs=[pl.BlockSpec(memory_space=pl.ANY)] * n,
        out_specs=[pl.BlockSpec(memory_space=pl.ANY)] * n, out_shape=out_shape,
        scratch_shapes=[pltpu.SemaphoreType.DMA((n_sem,)), pltpu.SemaphoreType.DMA((n_sem,)),
                        pltpu.SemaphoreType.DMA((max(n - ns, 1),))],
        compiler_params=pltpu.CompilerParams(has_side_effects=True), name=name)(*arrays)


def _pair_add(mine, recv, core, *, name, tc=None):
    _, r, c = recv.shape
    tc = c if tc is None else tc
    assert c % tc == 0

    def body(core_ref, m_ref, r_ref, o_ref):
        o_ref[...] = (m_ref[...].astype(F32) + r_ref[...].astype(F32)).astype(o_ref.dtype)

    spec = pl.BlockSpec((None, r, tc), lambda q, i, core_ref: (q, 0, i))
    return pl.pallas_call(
        body, grid_spec=pltpu.PrefetchScalarGridSpec(
            num_scalar_prefetch=1, grid=(N_CHIP, c // tc),
            in_specs=[pl.BlockSpec((None, r, tc), lambda q, i, core_ref: (2 * q + core_ref[0], 0, i)), spec],
            out_specs=spec),
        out_shape=jax.ShapeDtypeStruct(recv.shape, recv.dtype),
        compiler_params=_params(("parallel", "parallel")), name=name)(core, mine, recv)


_HBM = pl.BlockSpec(memory_space=pltpu.HBM)
_SEM = pl.BlockSpec(memory_space=pltpu.SEMAPHORE)
_DATAFLOW = pltpu.SideEffectType.DATAFLOW_SIDE_EFFECTING
_PEERS = {"chips": N_CHIP - 1, "all": N_DEV - 1}


def _split_copies(pattern, srcs, lands, send_sems, recv_sems):
    x, y, c = lax.axis_index("x"), lax.axis_index("y"), lax.axis_index("c")
    per = _PEERS[pattern]
    copies = []
    for a in range(len(srcs)):
        for r in range(1, per + 1):
            if pattern == "chips":
                tx = 1 - x if (r >> 1) & 1 else x
                ty = 1 - y if r & 1 else y
                tc = c
                src, dst = srcs[a].at[2 * tx + ty], lands[a].at[2 * x + y]
            else:
                tx = 1 - x if (r >> 2) & 1 else x
                ty = 1 - y if (r >> 1) & 1 else y
                tc = 1 - c if r & 1 else c
                src, dst = srcs[a], lands[a].at[4 * x + 2 * y + c]
            copies.append(pltpu.make_async_remote_copy(
                src_ref=src, dst_ref=dst, send_sem=send_sems.at[a * per + r - 1],
                recv_sem=recv_sems.at[a * per + r - 1], device_id=(tx, ty, tc), device_id_type=pl.DeviceIdType.MESH))
    return copies


def _split_start(pattern, arrays, *, name):
    n = len(arrays)
    n_sem = n * _PEERS[pattern]

    def body(*refs):
        srcs, lands = refs[:n], refs[n:2 * n]
        send_sems, recv_sems = refs[2 * n], refs[2 * n + 1]
        zero = refs[-1]
        for cp in _split_copies(pattern, srcs, lands, send_sems, recv_sems):
            cp.start()
        zero[...] = jnp.zeros_like(zero)

    src_t = [pltpu.HBM(v.shape, v.dtype) for v in arrays]
    land_t = [pltpu.HBM(v.shape if pattern == "chips" else (N_DEV,) + v.shape, v.dtype) for v in arrays]
    outs = pl.pallas_call(
        body, name=name,
        out_shape=(pltpu.SemaphoreType.DMA((n_sem,)), pltpu.SemaphoreType.DMA((n_sem,)), *src_t, *land_t,
                   jax.ShapeDtypeStruct((8, LANES), F32)),
        in_specs=[_HBM] * (2 * n),
        out_specs=(_SEM, _SEM, *[_HBM] * (2 * n), pl.BlockSpec(memory_space=pltpu.VMEM)),
        input_output_aliases={i: 2 + i for i in range(2 * n)},
        compiler_params=pltpu.CompilerParams(has_side_effects=_DATAFLOW))(
            *[pltpu.with_memory_space_constraint(v, pltpu.HBM) for v in arrays],
            *[pltpu.with_memory_space_constraint(lax.empty(t.shape, t.dtype), pltpu.HBM) for t in land_t])
    return outs[0], outs[1], outs[2:2 + n], outs[2 + n:2 + 2 * n], outs[-1]


def _split_wait(pattern, send_sems, recv_sems, srcs, lands, after, *, name):
    n = len(srcs)

    def body(*refs):
        for cp in _split_copies(pattern, refs[:n], refs[n:2 * n], refs[2 * n], refs[2 * n + 1]):
            cp.wait_send()
            cp.wait_recv()

    outs = pl.pallas_call(
        body, name=name, out_shape=[pltpu.HBM(v.shape, v.dtype) for v in (*srcs, *lands)],
        in_specs=[_HBM] * (2 * n) + [_SEM, _SEM, pl.BlockSpec(memory_space=pl.ANY)],
        out_specs=[_HBM] * (2 * n), input_output_aliases={i: i for i in range(2 * n)},
        compiler_params=pltpu.CompilerParams(has_side_effects=_DATAFLOW))(
            *srcs, *lands, send_sems, recv_sems, after)
    return outs[:n], outs[n:]


def _own_slab(lands, srcs, index, whole):
    out = []
    for land, src in zip(lands, srcs):
        own = src[None] if whole else lax.dynamic_slice(src, (index, 0, 0), (1,) + src.shape[1:])
        out.append(lax.dynamic_update_slice(land, own, (index, 0, 0)))
    return out


def _gather(arrays, *, name):
    n = len(arrays)
    per = N_DEV - 1

    def body(*refs):
        ins, outs = refs[:n], refs[n:2 * n]
        send_sems, recv_sems, local_sems = refs[2 * n:]
        x, y, c = lax.axis_index("x"), lax.axis_index("y"), lax.axis_index("c")
        me, sibling = (x, y, c), (x, y, 1 - c)
        chips = [(1 - x, y), (x, 1 - y), (1 - x, 1 - y)]

        def copy(a, k, block, to, src=None):
            rows = outs[a].at[4 * block[0] + 2 * block[1] + block[2]]
            return pltpu.make_async_remote_copy(
                src_ref=rows if src is None else src, dst_ref=rows,
                send_sem=send_sems.at[a * per + k], recv_sem=recv_sems.at[a * per + k],
                device_id=to, device_id_type=pl.DeviceIdType.MESH)

        started = []
        for a in range(n):
            mine = pltpu.make_async_copy(ins[a], outs[a].at[4 * x + 2 * y + c], local_sems.at[a])
            mine.start()
            started.append(mine)
        for a in range(n):
            first = [copy(a, 1 + j, me, (*chip, c), src=ins[a]) for j, chip in enumerate(chips)]
            first.append(copy(a, 0, me, sibling, src=ins[a]))
            for cp in first:
                cp.start()
            started += first
        for j, chip in enumerate(chips):
            for a in range(n):
                copy(a, 1 + j, (*chip, c), me).wait_recv()
                passed = copy(a, 4 + j, (*chip, c), sibling)
                passed.start()
                started.append(passed)
        for a in range(n):
            copy(a, 0, sibling, me).wait_recv()
            for j, chip in enumerate(chips):
                copy(a, 4 + j, (*chip, 1 - c), me).wait_recv()
        for cp in started[:n]:
            cp.wait()
        for cp in started[n:]:
            cp.wait_send()

    return pl.pallas_call(
        body, in_specs=[pl.BlockSpec(memory_space=pl.ANY)] * n,
        out_specs=[pl.BlockSpec(memory_space=pl.ANY)] * n,
        out_shape=[jax.ShapeDtypeStruct((N_DEV,) + v.shape, v.dtype) for v in arrays],
        scratch_shapes=[pltpu.SemaphoreType.DMA((n * per,)), pltpu.SemaphoreType.DMA((n * per,)),
                        pltpu.SemaphoreType.DMA((n,))],
        compiler_params=pltpu.CompilerParams(has_side_effects=True), name=name)(*arrays)


def _adamw(parts, w, m, v, *, name, tc=None):
    r, c = w.shape
    n_parts = parts.shape[0]
    tc = c if tc is None else tc
    assert c % tc == 0

    def body(p_ref, w_ref, m_ref, v_ref, g_ref, d_ref, nm_ref, nv_ref):
        g = p_ref[0].astype(F32)
        for d in range(1, n_parts):
            g = g + p_ref[d].astype(F32)
        mm = ADAM_B1 * m_ref[...] + (1.0 - ADAM_B1) * g
        vv = ADAM_B2 * v_ref[...] + (1.0 - ADAM_B2) * (g * g)
        m_hat = mm / (1.0 - ADAM_B1 ** ADAM_STEP)
        v_hat = vv / (1.0 - ADAM_B2 ** ADAM_STEP)
        g_ref[...] = g
        d_ref[...] = -ADAM_LR * (m_hat / (jnp.sqrt(v_hat) + ADAM_EPS) + ADAM_WD * w_ref[...])
        nm_ref[...] = mm
        nv_ref[...] = vv

    spec = pl.BlockSpec((r, tc), lambda i: (0, i))
    out = jax.ShapeDtypeStruct((r, c), F32)
    return pl.pallas_call(
        body, grid=(c // tc,),
        in_specs=[pl.BlockSpec((n_parts, r, tc), lambda i: (0, 0, i)), spec, spec, spec],
        out_specs=[spec] * 4, out_shape=[out] * 4,
        compiler_params=_params(("parallel",)), name=name)(parts, w, m, v)


SMALL = (("norm_w", D_MODEL), ("conv_b", CONV_DIM), ("dt_bias", N_DT), ("a_log", N_DT),
         ("d_skip", N_DT), ("ssm_norm_w", SSD_WIDTH), ("final_norm_w", D_MODEL))
SMALL_ROWS = 64
SHARD = D_PROJ // N_DEV


def _pack_small(vals):
    flat = jnp.concatenate([vals[k].reshape(-1).astype(F32) for k, _ in SMALL])
    return jnp.pad(flat, (0, SMALL_ROWS * LANES - flat.shape[0])).reshape(SMALL_ROWS, LANES)


def _unpack_small(packed, shapes):
    flat = packed.reshape(-1)
    out, off = {}, 0
    for k, size in SMALL:
        out[k] = flat[off:off + size].reshape(shapes[k])
        off += size
    return out


def kernel(x, norm_w, w_in, conv_w, conv_b, dt_bias, a_log, d_skip, ssm_norm_w, w_attn_out, w_ssm_out, w_o, final_norm_w, loss_target, m_norm_w, m_w_in, m_conv_w, m_conv_b, m_dt_bias, m_a_log, m_d_skip, m_ssm_norm_w, m_w_attn_out, m_w_ssm_out, m_w_o, m_final_norm_w, v_norm_w, v_w_in, v_conv_w, v_conv_b, v_dt_bias, v_a_log, v_d_skip, v_ssm_norm_w, v_w_attn_out, v_w_ssm_out, v_w_o, v_final_norm_w):
    nb, s, _ = x.shape
    t = nb * s
    weights = dict(norm_w=norm_w, w_in=w_in, conv_w=conv_w, conv_b=conv_b, dt_bias=dt_bias, a_log=a_log,
                   d_skip=d_skip, ssm_norm_w=ssm_norm_w, w_attn_out=w_attn_out, w_ssm_out=w_ssm_out,
                   w_o=w_o, final_norm_w=final_norm_w)
    moms = dict(norm_w=m_norm_w, w_in=m_w_in, conv_w=m_conv_w, conv_b=m_conv_b, dt_bias=m_dt_bias,
                a_log=m_a_log, d_skip=m_d_skip, ssm_norm_w=m_ssm_norm_w, w_attn_out=m_w_attn_out,
                w_ssm_out=m_w_ssm_out, w_o=m_w_o, final_norm_w=m_final_norm_w)
    vels = dict(norm_w=v_norm_w, w_in=v_w_in, conv_w=v_conv_w, conv_b=v_conv_b, dt_bias=v_dt_bias,
                a_log=v_a_log, d_skip=v_d_skip, ssm_norm_w=v_ssm_norm_w, w_attn_out=v_w_attn_out,
                w_ssm_out=v_w_ssm_out, w_o=v_w_o, final_norm_w=v_final_norm_w)

    tr_ = lambda a: jnp.transpose(a[0])
    g_in, g_conv = _gather([_bf(tr_(w_in)), conv_w[0]], name="gather_weights")
    conv_full = g_conv.transpose(1, 0, 2).reshape(CONV_K, CONV_DIM)
    o_send, o_recv, o_srcs, o_lands, o_zero = _split_start(
        "all", [_bf(w_attn_out[0]), _bf(w_ssm_out[0]), _bf(w_o[0])], name="gather_out_weights_start")

    core = lax.axis_index("c").astype(jnp.int32).reshape(1)
    chip = 2 * lax.axis_index("x") + lax.axis_index("y")
    me = 2 * chip + lax.axis_index("c")

    def out_weights(after):
        srcs, lands = _split_wait("all", o_send, o_recv, o_srcs, o_lands, after, name="gather_out_weights_wait")
        g_wa, g_ws, g_wo = _own_slab(lands, srcs, me, True)
        return (g_wa.reshape(SB_WIDTH, D_MODEL), g_ws.reshape(SSD_WIDTH, D_MODEL), g_wo.reshape(D_MODEL, D_MODEL))

    def send_weight_grads(grads):
        slabs = [grads["w_in_t"],
                 grads["conv_w"].reshape(CONV_K, N_DEV, CONV_DIM // N_DEV).transpose(1, 0, 2),
                 _bf(grads["w_attn_out"].reshape(N_DEV, SB_WIDTH // N_DEV, D_MODEL)),
                 _bf(grads["w_ssm_out"].reshape(N_DEV, SSD_WIDTH // N_DEV, D_MODEL)),
                 _bf(grads["w_o"].reshape(N_DEV, D_MODEL // N_DEV, D_MODEL))]
        recv = _pair_exchange(slabs, [], name="exchange_pairs")
        names = ("w_in", "conv_w", "w_attn_out", "w_ssm_out", "w_o")
        sums = [_pair_add(a, b, core, name="pair_add_" + k, tc=4 * LANES if k == "w_in" else None)
                for k, a, b in zip(names, slabs, recv)]
        send_sems, recv_sems, srcs, lands, zero = _split_start("chips", sums, name="exchange_chips_start")
        return zero, (send_sems, recv_sems, srcs, lands)

    loss_part, grad_x, grads, (send_sems, recv_sems, srcs, lands) = _local_step(
        x.reshape(t, D_MODEL), loss_target.reshape(t, D_MODEL), nb, s, g_in.reshape(D_PROJ, D_MODEL),
        conv_full, conv_b, norm_w + o_zero[0:1, 0:1], dt_bias, a_log, d_skip, ssm_norm_w, out_weights,
        final_norm_w.reshape(1, D_MODEL), between=send_weight_grads)

    (p_small,) = _pair_exchange([], [_pack_small(grads)], name="exchange_small")
    srcs, lands = _split_wait("chips", send_sems, recv_sems, srcs, lands, grad_x, name="exchange_chips_wait")
    p_in, p_conv, p_wa, p_ws, p_wo = _own_slab(lands, srcs, chip, False)

    res = {}
    res["w_in"] = [a.T for a in _adamw(p_in, tr_(w_in), tr_(m_w_in), tr_(v_w_in), name="adamw_w_in", tc=2 * LANES)]
    res["conv_w"] = _adamw(p_conv, conv_w[0], m_conv_w[0], v_conv_w[0], name="adamw_conv_w")
    res["w_attn_out"] = _adamw(p_wa, w_attn_out[0], m_w_attn_out[0], v_w_attn_out[0], name="adamw_w_attn_out")
    res["w_ssm_out"] = _adamw(p_ws, w_ssm_out[0], m_w_ssm_out[0], v_w_ssm_out[0], name="adamw_w_ssm_out")
    res["w_o"] = _adamw(p_wo, w_o[0], m_w_o[0], v_w_o[0], name="adamw_w_o")
    sm = _adamw(p_small, _pack_small(weights), _pack_small(moms), _pack_small(vels), name="adamw_small")
    shapes = {k: weights[k].shape for k, _ in SMALL}
    sm = [_unpack_small(v, shapes) for v in sm]
    for k, _ in SMALL:
        res[k] = tuple(part[k] for part in sm)

    order = ("norm_w", "w_in", "conv_w", "conv_b", "dt_bias", "a_log", "d_skip", "ssm_norm_w",
             "w_attn_out", "w_ssm_out", "w_o", "final_norm_w")
    loss = lax.psum(loss_part[0, 0], ("x", "y", "c"))
    outs = [loss, grad_x.reshape(nb, s, D_MODEL)]
    for i in range(4):
        outs += [res[k][i].reshape(weights[k].shape) for k in order]
    return tuple(outs)
```

```python
import functools

import jax
import jax.numpy as jnp
from jax import lax
from jax.experimental import pallas as pl
from jax.experimental.pallas import tpu as pltpu

F32 = jnp.float32
BF16 = jnp.bfloat16

D_MODEL = 1024
SB_WIDTH = 1024
HEAD_DIM = 64
SSD_WIDTH = 2048
SSD_GROUPS = 4
SSD_HPG = 8
SSD_STATE = 128
SSD_CHUNK = 128
CONV_K = 4
CONV_DIM = 3072
N_DT = 32
D_PROJ = 11296
DT_COL0 = 9216
GATE_COL0 = DT_COL0 + N_DT
EPS = 1e-6
N_DEV = 8

ADAM_LR = 0.001
ADAM_B1 = 0.9
ADAM_B2 = 0.999
ADAM_EPS = 1e-08
ADAM_WD = 0.01
ADAM_STEP = 10

LANES = 128
VMEM_LIMIT = 56 * 1024 * 1024

CB_Q, CB_K, CB_V, CB_ZA, CB_ZS, CB_XBC = 0, 8, 16, 24, 32, 48

NT_DIMS = (((1,), (1,)), ((), ()))
TN_DIMS = (((0,), (0,)), ((), ()))
HI = lax.Precision.HIGHEST


def _params(sem):
    return pltpu.CompilerParams(dimension_semantics=sem, vmem_limit_bytes=VMEM_LIMIT)


def _dot(a, b):
    return jnp.dot(a, b, preferred_element_type=F32)


def _dot_nt(a, b):
    return lax.dot_general(a, b, NT_DIMS, preferred_element_type=F32)


def _dot_tn(a, b):
    return lax.dot_general(a, b, TN_DIMS, preferred_element_type=F32)


def _dot_hi(a, b):
    return jnp.dot(a, b, preferred_element_type=F32, precision=HI)


def _bf(a):
    return a.astype(BF16)


def _sigmoid(x):
    return 0.5 + 0.5 * jnp.tanh(0.5 * x)


def _softplus(x):
    return jnp.maximum(x, 0.0) + jnp.log(1.0 + jnp.exp(-jnp.abs(x)))


def _trunc_split(a):
    bits = lax.bitcast_convert_type(a, jnp.uint32) & jnp.uint32(0xFFFF0000)
    hi = lax.bitcast_convert_type(bits, F32)
    return _bf(hi), _bf(a - hi)


MM_TILE = 1024
MM_TK = 2048


def _matmul(a, b, *, name, nt=False, ta=False, add=None, out_dtype=F32, tm=MM_TILE, tn=MM_TILE, tk=MM_TK, n=None):
    k, m = a.shape if ta else a.shape[::-1]
    n = (b.shape[0] if nt else b.shape[1]) if n is None else n
    tm, tn, tk = min(tm, m), min(tn, n), min(tk, k)
    assert m % tm == 0 and n % tn == 0 and k % tk == 0, (name, a.shape, b.shape)
    assert not (ta and nt)
    nk = k // tk
    has_add = add is not None

    def body(*refs):
        if has_add:
            a_ref, b_ref, add_ref, o_ref, acc_ref = refs
        else:
            a_ref, b_ref, o_ref, acc_ref = refs
        kk = pl.program_id(2)
        av, bv = _bf(a_ref[...]), _bf(b_ref[...])
        d = _dot_nt(av, bv) if nt else _dot_tn(av, bv) if ta else _dot(av, bv)
        if nk == 1:
            o_ref[...] = ((d + add_ref[...]) if has_add else d).astype(out_dtype)
            return

        @pl.when(kk == 0)
        def _():
            acc_ref[...] = jnp.zeros_like(acc_ref)

        acc_ref[...] += d

        @pl.when(kk == nk - 1)
        def _():
            r = acc_ref[...]
            if has_add:
                r = r + add_ref[...]
            o_ref[...] = r.astype(out_dtype)

    in_specs = [pl.BlockSpec((tk, tm), lambda j, i, kk: (kk, i)) if ta
                else pl.BlockSpec((tm, tk), lambda j, i, kk: (i, kk)),
                pl.BlockSpec((tn, tk), lambda j, i, kk: (j, kk)) if nt
                else pl.BlockSpec((tk, tn), lambda j, i, kk: (kk, j))]
    args = [a, b]
    if has_add:
        in_specs.append(pl.BlockSpec((tm, tn), lambda j, i, kk: (i, j)))
        args.append(add)
    return pl.pallas_call(
        body, grid=(n // tn, m // tm, nk), in_specs=in_specs,
        out_specs=pl.BlockSpec((tm, tn), lambda j, i, kk: (i, j)),
        out_shape=jax.ShapeDtypeStruct((m, n), out_dtype),
        scratch_shapes=[pltpu.VMEM((tm, tn), F32)],
        compiler_params=_params(("parallel", "parallel", "arbitrary")), name=name)(*args)


def _rms_fwd(x, w, *, tm=512):
    t = x.shape[0]

    def body(x_ref, w_ref, h_ref, r_ref):
        xv = x_ref[...]
        r = lax.rsqrt(jnp.mean(xv * xv, axis=-1, keepdims=True) + EPS)
        h_ref[...] = _bf(xv * r * w_ref[...])
        r_ref[...] = r

    row = lambda i: (i, 0)
    return pl.pallas_call(
        body, grid=(t // tm,),
        in_specs=[pl.BlockSpec((tm, D_MODEL), row), pl.BlockSpec((1, D_MODEL), lambda i: (0, 0))],
        out_specs=[pl.BlockSpec((tm, D_MODEL), row), pl.BlockSpec((tm, 1), row)],
        out_shape=[jax.ShapeDtypeStruct((t, D_MODEL), BF16), jax.ShapeDtypeStruct((t, 1), F32)],
        compiler_params=_params(("parallel",)), name="rms_fwd")(x, w)


def _rms_bwd(dh, x, r, w, dres, *, tm=512):
    t = x.shape[0]

    def body(dh_ref, x_ref, r_ref, w_ref, dres_ref, dx_ref, dw_ref):
        @pl.when(pl.program_id(0) == 0)
        def _():
            dw_ref[...] = jnp.zeros_like(dw_ref)

        nrm = x_ref[...] * r_ref[...]
        dhv = dh_ref[...]
        dw_ref[...] += jnp.sum(dhv * nrm, axis=0, keepdims=True)
        dn = dhv * w_ref[...]
        dx = r_ref[...] * (dn - nrm * jnp.mean(dn * nrm, axis=-1, keepdims=True))
        dx_ref[...] = dx + dres_ref[...]

    row = lambda i: (i, 0)
    fix = lambda i: (0, 0)
    return pl.pallas_call(
        body, grid=(t // tm,),
        in_specs=[pl.BlockSpec((tm, D_MODEL), row), pl.BlockSpec((tm, D_MODEL), row),
                  pl.BlockSpec((tm, 1), row), pl.BlockSpec((1, D_MODEL), fix),
                  pl.BlockSpec((tm, D_MODEL), row)],
        out_specs=[pl.BlockSpec((tm, D_MODEL), row), pl.BlockSpec((1, D_MODEL), fix)],
        out_shape=[jax.ShapeDtypeStruct((t, D_MODEL), F32), jax.ShapeDtypeStruct((1, D_MODEL), F32)],
        compiler_params=_params(("arbitrary",)), name="rms_bwd")(dh, x, r, w, dres)


ATT_BLK = 256


def _attn_masks(blk):
    r_io = lax.broadcasted_iota(jnp.int32, (blk, blk), 0)
    c_io = lax.broadcasted_iota(jnp.int32, (blk, blk), 1)
    return r_io, c_io


def _attn_fwd(proj, nb, s):
    blk = min(ATT_BLK, s)
    nq = s // blk
    scale = HEAD_DIM ** -0.5

    def body(q_ref, k_ref, v_ref, o_ref, tot_ref, q0_s, q1_s, kb_s, v0_s, v1_s, z_s, a_s):
        head0 = lax.broadcasted_iota(jnp.int32, (s, LANES), 1) < HEAD_DIM
        lane2 = lax.broadcasted_iota(jnp.int32, (blk, 2), 1)
        r_io, c_io = _attn_masks(blk)
        tri = c_io < r_io
        tri2 = jnp.concatenate([tri, tri], axis=0)
        u_gt = _bf(r_io > c_io)
        u2 = jnp.concatenate([u_gt, u_gt], axis=0)
        qv = q_ref[...] * scale
        q0_s[...] = _bf(jnp.where(head0, qv, 0.0))
        q1_s[...] = _bf(jnp.where(head0, 0.0, qv))
        kb_s[...] = _bf(k_ref[...])
        vv = v_ref[...]
        v0_s[...] = _bf(jnp.where(head0, vv, 0.0))
        v1_s[...] = _bf(jnp.where(head0, 0.0, vv))

        def scores(qcat, kblk, slot):
            k0 = pl.multiple_of(jnp.maximum(kblk, 0) * blk, blk)
            z_s[slot] = _dot_nt(qcat, kb_s[pl.ds(k0, blk), :])

        def weights(slot, cr, diag, zslot=None):
            z = z_s[slot if zslot is None else zslot]
            sp = _softplus(z)
            spm = jnp.where(tri2, sp, 0.0) if diag else sp
            hi, lo = _trunc_split(spm)
            ps = _dot(jnp.concatenate([hi, lo], axis=1), u2)
            a = jnp.exp(z - (sp + ps + cr))
            if diag:
                a = jnp.where(tri2, a, 0.0)
            a_s[slot] = _bf(a)
            return cr + ps[:, 0:1] + spm[:, 0:1]

        def apply(acc, slot, k0):
            a = a_s[slot]
            vcat = jnp.concatenate([v0_s[pl.ds(k0, blk), :], v1_s[pl.ds(k0, blk), :]], axis=0)
            return acc + _dot(jnp.concatenate([a[:blk], a[blk:]], axis=1), vcat)

        def queries(qi):
            q0 = pl.multiple_of(qi * blk, blk)
            return jnp.concatenate([q0_s[pl.ds(q0, blk), :], q1_s[pl.ds(q0, blk), :]], axis=0)

        def first_scores(qi):
            qcat = queries(qi)
            scores(qcat, qi, 2)
            scores(qcat, qi - 1, 3)

        first_scores(jnp.int32(0))

        def qblock(qi, _):
            q0 = pl.multiple_of(qi * blk, blk)
            qcat = queries(qi)
            z_s[1] = z_s[3]
            cr = weights(0, jnp.zeros((2 * blk, 1), F32), True, zslot=2)
            first_scores(jnp.minimum(qi + 1, nq - 1))

            def step(j, slot, st):
                acc = apply(st[0], 1 - slot, pl.multiple_of((qi - j + 1) * blk, blk))
                scores(qcat, qi - j - 1, 1 - slot)
                return acc, weights(slot, st[1], False)

            st = (jnp.zeros((blk, LANES), F32), cr)
            st = lax.fori_loop(0, qi // 2, lambda i, st: step(2 * i + 2, 0, step(2 * i + 1, 1, st)), st)
            st = lax.cond(qi % 2 == 1, lambda st: step(qi, 1, st), lambda st: st, st)
            o_ref[pl.ds(q0, blk), :] = apply(st[0], qi & 1, 0)
            tot_ref[pl.ds(q0, blk), :] = jnp.where(lane2 == 0, st[1][:blk], st[1][blk:])
            return 0

        lax.fori_loop(0, nq, qblock, 0)

    t = nb * s
    return pl.pallas_call(
        body, grid=(nb, 8),
        in_specs=[pl.BlockSpec((s, LANES), lambda b, p: (b, CB_Q + p)),
                  pl.BlockSpec((s, LANES), lambda b, p: (b, CB_K + p)),
                  pl.BlockSpec((s, LANES), lambda b, p: (b, CB_V + p))],
        out_specs=[pl.BlockSpec((s, LANES), lambda b, p: (b, p)),
                   pl.BlockSpec((None, None, s, 2), lambda b, p: (b, p, 0, 0))],
        out_shape=[jax.ShapeDtypeStruct((t, SB_WIDTH), F32),
                   jax.ShapeDtypeStruct((nb, 8, s, 2), F32)],
        scratch_shapes=[pltpu.VMEM((s, LANES), BF16)] * 5 + [
            pltpu.VMEM((4, 2 * blk, blk), F32), pltpu.VMEM((2, 2 * blk, blk), BF16)],
        compiler_params=_params(("parallel", "parallel")), name="attn_fwd")(proj, proj, proj)


def _attn_bwd(proj, do, tot, nb, s):
    blk = min(ATT_BLK, s)
    nq = s // blk
    scale = HEAD_DIM ** -0.5

    def body(q_ref, k_ref, v_ref, do_ref, tot_ref, dq_ref, dk_ref, dv_ref,
             q0_s, q1_s, kb_s, k0_s, k1_s, vb_s, d0_s, d1_s, z_s, da_s, dz_s, a_s, dk_acc, dv_acc):
        head0 = lax.broadcasted_iota(jnp.int32, (s, LANES), 1) < HEAD_DIM
        r_io, c_io = _attn_masks(blk)
        tri = c_io < r_io
        tri2 = jnp.concatenate([tri, tri], axis=0)
        u_le = _bf(r_io <= c_io)
        u2 = jnp.concatenate([u_le, u_le], axis=0)
        u_lt = _bf(r_io < c_io)
        qv = q_ref[...] * scale
        q0_s[...] = _bf(jnp.where(head0, qv, 0.0))
        q1_s[...] = _bf(jnp.where(head0, 0.0, qv))
        kv = k_ref[...]
        kb_s[...] = _bf(kv)
        k0_s[...] = _bf(jnp.where(head0, kv, 0.0))
        k1_s[...] = _bf(jnp.where(head0, 0.0, kv))
        vb_s[...] = _bf(v_ref[...])
        dov = do_ref[...]
        d0_s[...] = _bf(jnp.where(head0, dov, 0.0))
        d1_s[...] = _bf(jnp.where(head0, 0.0, dov))
        dk_acc[...] = jnp.zeros_like(dk_acc)
        dv_acc[...] = jnp.zeros_like(dv_acc)

        def qblock(qi, _):
            q0 = pl.multiple_of(qi * blk, blk)
            qs_ = pl.ds(q0, blk)
            qhs = (q0_s[qs_, :], q1_s[qs_, :])
            dhs = (d0_s[qs_, :], d1_s[qs_, :])
            qcat = jnp.concatenate(qhs, axis=0)
            dcat = jnp.concatenate(dhs, axis=0)
            totf = tot_ref[qs_, :]

            def scores(kblk, slot):
                ks_ = pl.ds(pl.multiple_of(kblk * blk, blk), blk)
                z_s[slot] = _dot_nt(qcat, kb_s[ks_, :])
                da_s[slot] = _dot_nt(dcat, vb_s[ks_, :])

            def grads(slot, st, diag):
                cn, cw = st
                z = z_s[slot]
                sp = _softplus(z)
                lb = z - sp
                spm = jnp.where(tri2, sp, 0.0) if diag else sp
                hi, lo = _trunc_split(spm)
                pin = _dot(jnp.concatenate([hi, lo], axis=1), u2)
                a = jnp.exp(lb + (pin + cn))
                if diag:
                    a = jnp.where(tri2, a, 0.0)
                w = a * da_s[slot]
                wex = _dot(_bf(w), u_lt)
                dz = w - jnp.exp(lb) * (w + (wex + cw))
                if diag:
                    dz = jnp.where(tri2, dz, 0.0)
                dz_s[slot] = _bf(dz)
                a_s[slot] = _bf(a)
                return cn + pin[:, blk - 1:blk], cw + wex[:, blk - 1:blk] + w[:, blk - 1:blk]

            def apply(dq, slot, kblk):
                ks_ = pl.ds(pl.multiple_of(kblk * blk, blk), blk)
                dz = dz_s[slot]
                dq = dq + _dot(jnp.concatenate([dz[:blk], dz[blk:]], axis=1),
                               jnp.concatenate([k0_s[ks_, :], k1_s[ks_, :]], axis=0))
                dk_acc[ks_, :] += _dot_tn(dz, qcat)
                dv_acc[ks_, :] += _dot_tn(a_s[slot], dcat)
                return dq

            def step(j, slot, st, diag):
                dq = apply(st[0], 1 - slot, jnp.maximum(j - 1, 0))
                if not diag:
                    scores(j + 1, 1 - slot)
                return (dq, *grads(slot, st[1:], diag))

            dz_s[1] = jnp.zeros_like(dz_s[1])
            a_s[1] = jnp.zeros_like(a_s[1])
            scores(0, 0)
            st = (jnp.zeros((blk, LANES), F32), -jnp.concatenate([totf[:, 0:1], totf[:, 1:2]], axis=0),
                  jnp.zeros((2 * blk, 1), F32))
            st = lax.fori_loop(
                0, qi // 2, lambda i, st: step(2 * i + 1, 1, step(2 * i, 0, st, False), False), st)
            st = lax.cond(qi % 2 == 1,
                          lambda st: step(qi, 1, step(qi - 1, 0, st, False), True),
                          lambda st: step(qi, 0, st, True), st)
            dq_ref[qs_, :] = _bf(apply(st[0], qi & 1, qi) * scale)
            return 0

        lax.fori_loop(0, nq, qblock, 0)
        dk_ref[...] = _bf(dk_acc[...])
        dv_ref[...] = _bf(dv_acc[...])

    t = nb * s
    hp = lambda b, p: (b, p)
    out = jax.ShapeDtypeStruct((t, SB_WIDTH), BF16)
    return pl.pallas_call(
        body, grid=(nb, 8),
        in_specs=[pl.BlockSpec((s, LANES), lambda b, p: (b, CB_Q + p)),
                  pl.BlockSpec((s, LANES), lambda b, p: (b, CB_K + p)),
                  pl.BlockSpec((s, LANES), lambda b, p: (b, CB_V + p)),
                  pl.BlockSpec((s, LANES), hp),
                  pl.BlockSpec((None, None, s, 2), lambda b, p: (b, p, 0, 0))],
        out_specs=[pl.BlockSpec((s, LANES), hp)] * 3,
        out_shape=[out, out, out],
        scratch_shapes=[pltpu.VMEM((s, LANES), BF16)] * 8 + [
            pltpu.VMEM((2, 2 * blk, blk), F32), pltpu.VMEM((2, 2 * blk, blk), F32),
            pltpu.VMEM((2, 2 * blk, blk), BF16), pltpu.VMEM((2, 2 * blk, blk), BF16),
            pltpu.VMEM((s, LANES), F32), pltpu.VMEM((s, LANES), F32)],
        compiler_params=_params(("parallel", "parallel")), name="attn_bwd")(proj, proj, proj, do, tot)


CONV_TC = 256
CONV_ROWS = 64
HALO = 8


def _conv_rows(ws, w_ref, b_ref, n):
    taps = [ws[HALO - kk:HALO - kk + n, :] for kk in range(CONV_K)]
    y = b_ref[...] + w_ref[CONV_K - 1:CONV_K, :] * taps[0]
    for kk in range(1, CONV_K):
        y = y + w_ref[CONV_K - 1 - kk:CONV_K - kk, :] * taps[kk]
    return y, taps


def _conv_fwd(proj, conv_w, conv_b, nb, s):
    tc, rows = CONV_TC, min(CONV_ROWS, s)
    cb0 = CB_XBC * LANES // tc

    def body(x_ref, w_ref, b_ref, o_ref, xp, ws):
        xp[0:HALO, :] = jnp.zeros((HALO, tc), F32)
        xp[HALO:HALO + s, :] = x_ref[...].astype(F32)

        def chunk(i, _):
            r0 = pl.multiple_of(i * rows, rows)
            ws[...] = xp[pl.ds(r0, rows + HALO), :]
            y, _ = _conv_rows(ws, w_ref, b_ref, rows)
            o_ref[pl.ds(r0, rows), :] = y * _sigmoid(y)
            return 0

        lax.fori_loop(0, s // rows, chunk, 0)

    t = nb * s
    return pl.pallas_call(
        body, grid=(nb, CONV_DIM // tc),
        in_specs=[pl.BlockSpec((s, tc), lambda b, c: (b, cb0 + c)),
                  pl.BlockSpec((CONV_K, tc), lambda b, c: (0, c)),
                  pl.BlockSpec((1, tc), lambda b, c: (0, c))],
        out_specs=pl.BlockSpec((s, tc), lambda b, c: (b, c)),
        out_shape=jax.ShapeDtypeStruct((t, CONV_DIM), F32),
        scratch_shapes=[pltpu.VMEM((s + HALO, tc), F32), pltpu.VMEM((rows + HALO, tc), F32)],
        compiler_params=_params(("parallel", "parallel")), name="conv_fwd")(proj, conv_w, conv_b)


def _conv_bwd(proj, dact, conv_w, conv_b, nb, s, *, col0, name):
    width = dact.shape[1]
    tc, rows = min(CONV_TC, width), min(CONV_ROWS, s)
    cb0 = (CB_XBC * LANES + col0) // tc
    wb0 = col0 // tc

    def body(x_ref, d_ref, w_ref, b_ref, dx_ref, dw_ref, db_ref, xp, dp, dys, ws):
        @pl.when(pl.program_id(1) == 0)
        def _():
            dw_ref[...] = jnp.zeros_like(dw_ref)
            db_ref[...] = jnp.zeros_like(db_ref)

        n = rows + HALO
        pad = jnp.zeros((HALO, tc), F32)
        xp[0:HALO, :] = pad
        xp[HALO:HALO + s, :] = x_ref[...].astype(F32)
        xp[HALO + s:HALO + s + HALO, :] = pad
        dp[0:s, :] = d_ref[...]
        dp[s:s + HALO, :] = pad

        def chunk(i, acc):
            r0 = pl.multiple_of(i * rows, rows)
            ws[...] = xp[pl.ds(r0, n + HALO), :]
            y, taps = _conv_rows(ws, w_ref, b_ref, n)
            sg = _sigmoid(y)
            dy = dp[pl.ds(r0, n), :] * (sg * (1.0 + y * (1.0 - sg)))
            dys[...] = dy
            dx = w_ref[CONV_K - 1:CONV_K, :] * dy[0:rows]
            for kk in range(1, CONV_K):
                dx = dx + w_ref[CONV_K - 1 - kk:CONV_K - kk, :] * dys[kk:kk + rows, :]
            dx_ref[pl.ds(r0, rows), :] = _bf(dx)
            dyc = dy[0:rows]
            new = [acc[0] + jnp.sum(dyc, axis=0, keepdims=True)]
            for kk in range(CONV_K):
                new.append(acc[1 + kk] + jnp.sum(dyc * taps[kk][0:rows], axis=0, keepdims=True))
            return tuple(new)

        acc = lax.fori_loop(0, s // rows, chunk, (jnp.zeros((1, tc), F32),) * (1 + CONV_K))
        db_ref[...] += acc[0]
        for kk in range(CONV_K):
            dw_ref[CONV_K - 1 - kk:CONV_K - kk, :] += acc[1 + kk]

    t = nb * s
    return pl.pallas_call(
        body, grid=(width // tc, nb),
        in_specs=[pl.BlockSpec((s, tc), lambda c, b: (b, cb0 + c)),
                  pl.BlockSpec((s, tc), lambda c, b: (b, c)),
                  pl.BlockSpec((CONV_K, tc), lambda c, b: (0, wb0 + c)),
                  pl.BlockSpec((1, tc), lambda c, b: (0, wb0 + c))],
        out_specs=[pl.BlockSpec((s, tc), lambda c, b: (b, c)),
                   pl.BlockSpec((CONV_K, tc), lambda c, b: (0, c)),
                   pl.BlockSpec((1, tc), lambda c, b: (0, c))],
        out_shape=[jax.ShapeDtypeStruct((t, width), BF16),
                   jax.ShapeDtypeStruct((CONV_K, width), F32),
                   jax.ShapeDtypeStruct((1, width), F32)],
        scratch_shapes=[pltpu.VMEM((s + 2 * HALO, tc), F32), pltpu.VMEM((s + HALO, tc), F32),
                        pltpu.VMEM((rows + HALO, tc), F32), pltpu.VMEM((rows + 2 * HALO, tc), F32)],
        compiler_params=_params(("parallel", "arbitrary")), name=name)(proj, dact, conv_w, conv_b)


L = SSD_CHUNK
N_PAIR = SSD_HPG // 2


def _pair_lanes(v0, v1, lane):
    return jnp.where(lane < HEAD_DIM, v0, v1)


def _onehot(shape, row_shift, col_shift):
    row = lax.broadcasted_iota(jnp.int32, shape, 0)
    col = lax.broadcasted_iota(jnp.int32, shape, 1)
    return _bf(lax.shift_right_logical(row, row_shift) == lax.shift_right_logical(col, col_shift))


def _ssd_tables():
    sp = _onehot((LANES, N_PAIR * LANES), 0, 6)
    sf = _onehot((LANES, SSD_HPG * LANES), 0, 7)
    return (jnp.concatenate([sp] * 3, axis=0), jnp.concatenate([sf] * 3, axis=0),
            _onehot((N_PAIR * LANES, LANES), 6, 0), _onehot((SSD_HPG * LANES, LANES), 7, 0))


def _split3(a):
    a1 = _bf(a)
    r1 = a - a1.astype(F32)
    a2 = _bf(r1)
    return jnp.concatenate([a1, a2, _bf(r1 - a2.astype(F32))], axis=1)


def _split2(a):
    hi, lo = _trunc_split(a)
    return jnp.concatenate([hi, lo], axis=1)


def _ssd_chunk_common(dtr_ref, bias_ref, alog_ref, r0, spread_pair, spread_full):
    rr = lax.broadcasted_iota(jnp.int32, (L, L), 0)
    cc = lax.broadcasted_iota(jnp.int32, (L, L), 1)
    tri = (cc <= rr).astype(F32)
    raw = dtr_ref[pl.ds(r0, L), :] + bias_ref[...]
    dt = _softplus(raw)
    a_neg = -jnp.exp(alog_ref[...])
    acs = _dot_hi(tri, dt * a_neg)
    acs3 = _split3(acs)
    both = _dot(jnp.concatenate([_split3(dt), acs3], axis=0), spread_pair)
    return raw, dt, a_neg, acs, acs.T, rr, cc, both[:L], both[L:], _dot(acs3, spread_full)


def _ssd_fwd(xbc, dtr, bias, alog, dskip, nb, s):
    nc = s // L
    t = nb * s

    def body(x_ref, b_ref, c_ref, dtr_ref, bias_ref, alog_ref, dsk_ref, y_ref, hs_ref, h_scr):
        lane = lax.broadcasted_iota(jnp.int32, (L, LANES), 1)
        lane1 = lax.broadcasted_iota(jnp.int32, (1, LANES), 1)
        rowc = lax.broadcasted_iota(jnp.int32, (L, 1), 0)
        h_scr[...] = jnp.zeros_like(h_scr)
        dsk = dsk_ref[...]
        spread_pair, spread_full, _, _ = _ssd_tables()

        def chunk(c, _):
            r0 = pl.multiple_of(c * L, L)
            raw, dt, a_neg, acs, acs_t, rr, cc, dt_all, acs_all, acs_full = _ssd_chunk_common(
                dtr_ref, bias_ref, alog_ref, r0, spread_pair, spread_full)
            causal = rr >= cc
            bb = _bf(b_ref[pl.ds(r0, L), :])
            cb = _bf(c_ref[pl.ds(r0, L), :])
            gm = _dot_nt(cb, bb)
            for pr in range(N_PAIR):
                j0, j1 = 2 * pr, 2 * pr + 1
                ms = []
                for j in (j0, j1):
                    seg = acs_full[:, j * LANES:(j + 1) * LANES] - acs_t[j:j + 1, :]
                    ms.append(_bf(gm * jnp.exp(jnp.where(causal, seg, -1e30))))
                x = x_ref[pl.ds(r0, L), pr * LANES:(pr + 1) * LANES]
                dt_p = dt_all[:, pr * LANES:(pr + 1) * LANES]
                acs_p = acs_all[:, pr * LANES:(pr + 1) * LANES]
                last_p = acs_p[L - 1:L, :]
                d_p = _pair_lanes(dsk[:, j0:j0 + 1], dsk[:, j1:j1 + 1], lane1)
                xd = x * dt_p
                xdb = _bf(xd)
                yds = _dot(jnp.concatenate(ms, axis=0), xdb)
                yd = _pair_lanes(yds[:L], yds[L:], lane)
                hp = h_scr[pr]
                yo = _dot_nt(cb, _bf(hp)) * jnp.exp(acs_p)
                y_ref[pl.ds(r0, L), pr * LANES:(pr + 1) * LANES] = yd + yo + x * d_p
                hs_ref[c, pr] = hp
                sp = _dot_tn(_bf(xd * jnp.exp(last_p - acs_p)), bb)
                cd = jnp.where(rowc < HEAD_DIM, jnp.exp(acs[L - 1:L, j0:j0 + 1]),
                               jnp.exp(acs[L - 1:L, j1:j1 + 1]))
                h_scr[pr] = hp * cd + sp
            return 0

        lax.fori_loop(0, nc, chunk, 0)

    grp = lambda b, g: (g, 0, 0)
    return pl.pallas_call(
        body, grid=(nb, SSD_GROUPS),
        in_specs=[pl.BlockSpec((s, 4 * LANES), lambda b, g: (b, g)),
                  pl.BlockSpec((s, LANES), lambda b, g: (b, 16 + g)),
                  pl.BlockSpec((s, LANES), lambda b, g: (b, 20 + g)),
                  pl.BlockSpec((None, s, LANES), lambda b, g: (g, b, 0)),
                  pl.BlockSpec((None, 1, LANES), grp), pl.BlockSpec((None, 1, LANES), grp),
                  pl.BlockSpec((None, 1, LANES), grp)],
        out_specs=[pl.BlockSpec((s, 4 * LANES), lambda b, g: (b, g)),
                   pl.BlockSpec((None, None, nc, N_PAIR, L, LANES), lambda b, g: (b, g, 0, 0, 0, 0))],
        out_shape=[jax.ShapeDtypeStruct((t, SSD_WIDTH), F32),
                   jax.ShapeDtypeStruct((nb, SSD_GROUPS, nc, N_PAIR, L, LANES), F32)],
        scratch_shapes=[pltpu.VMEM((N_PAIR, L, LANES), F32)],
        compiler_params=_params(("parallel", "parallel")), name="ssd_fwd")(
            xbc, xbc, xbc, dtr, bias, alog, dskip)


def _ssd_bwd(xbc, dtr, bias, alog, dskip, hs, dy, nb, s):
    nc = s // L
    t = nb * s

    def body(x_ref, b_ref, c_ref, dtr_ref, bias_ref, alog_ref, dsk_ref, hs_ref, dy_ref,
             dx_ref, db_ref, dc_ref, ddtr_ref, dbias_ref, dalog_ref, ddsk_ref, dh_scr):
        lane = lax.broadcasted_iota(jnp.int32, (L, LANES), 1)
        lane1 = lax.broadcasted_iota(jnp.int32, (1, LANES), 1)
        rowc = lax.broadcasted_iota(jnp.int32, (L, 1), 0)
        dh_scr[...] = jnp.zeros_like(dh_scr)
        dsk = dsk_ref[...]

        @pl.when(pl.program_id(1) == 0)
        def _():
            dbias_ref[...] = jnp.zeros_like(dbias_ref)
            dalog_ref[...] = jnp.zeros_like(dalog_ref)
            ddsk_ref[...] = jnp.zeros_like(ddsk_ref)

        spread_pair, spread_full, sum_pair, sum_full = _ssd_tables()
        sum_pair2 = jnp.concatenate([sum_pair] * 2, axis=0)
        sum_full2 = jnp.concatenate([sum_full] * 2, axis=0)

        def chunk(i, _):
            c = nc - 1 - i
            r0 = pl.multiple_of(c * L, L)
            raw, dt, a_neg, acs, acs_t, rr, cc, dt_all, acs_all, acs_full = _ssd_chunk_common(
                dtr_ref, bias_ref, alog_ref, r0, spread_pair, spread_full)
            causal = rr >= cc
            tri_t = (cc >= rr).astype(F32)
            bb = _bf(b_ref[pl.ds(r0, L), :])
            cb = _bf(c_ref[pl.ds(r0, L), :])
            gm = _dot_nt(cb, bb)
            dg = jnp.zeros((L, L), F32)
            dbacc = jnp.zeros((L, LANES), F32)
            dcacc = jnp.zeros((L, LANES), F32)
            dacs_t = jnp.zeros((L, L), F32)
            ta_all, te_all, qm_all, tc_all, td_all, dcd_all = [], [], [], [], [], []
            for pr in range(N_PAIR):
                js = (2 * pr, 2 * pr + 1)
                lms, mfs = [], []
                for j in js:
                    seg = acs_full[:, j * LANES:(j + 1) * LANES] - acs_t[j:j + 1, :]
                    lm = jnp.exp(jnp.where(causal, seg, -1e30))
                    lms.append(lm)
                    mfs.append(gm * lm)
                x = x_ref[pl.ds(r0, L), pr * LANES:(pr + 1) * LANES]
                dyv = dy_ref[pl.ds(r0, L), pr * LANES:(pr + 1) * LANES]
                dt_p = dt_all[:, pr * LANES:(pr + 1) * LANES]
                acs_p = acs_all[:, pr * LANES:(pr + 1) * LANES]
                last_p = acs_p[L - 1:L, :]
                d_p = _pair_lanes(dsk[:, js[0]:js[0] + 1], dsk[:, js[1]:js[1] + 1], lane1)
                e_p = jnp.exp(acs_p)
                dte_p = jnp.exp(last_p - acs_p)
                xd = x * dt_p
                xdb = _bf(xd)
                hp = hs_ref[c, pr]
                hb = _bf(hp)
                dye = _bf(dyv * e_p)
                yo = _dot_nt(cb, hb) * e_p
                dcacc = dcacc + _dot(dye, hb)
                dh_out = _dot_tn(dye, cb)
                dyb = _bf(dyv)
                dms = _dot_nt(_bf(jnp.concatenate(
                    [jnp.where(lane < HEAD_DIM, dyv, 0.0), jnp.where(lane < HEAD_DIM, 0.0, dyv)], axis=0)), xdb)
                mty = _dot_tn(_bf(jnp.concatenate(mfs, axis=1)), dyb)
                dxd = jnp.where(lane < HEAD_DIM, mty[:L], mty[L:])
                for hh in range(2):
                    dm = dms[hh * L:(hh + 1) * L]
                    dg = dg + dm * lms[hh]
                    qm = dm * mfs[hh]
                    qm_all.append(qm)
                    dacs_t = dacs_t - jnp.where(rr == js[hh], jnp.sum(qm, axis=0, keepdims=True), 0.0)
                ds = dh_scr[pr]
                dsb = _bf(ds)
                xdd = xd * dte_p
                dxdd = _dot_nt(bb, dsb)
                dbacc = dbacc + _dot(_bf(xdd), dsb)
                dxd = dxd + dxdd * dte_p
                cd = jnp.where(rowc < HEAD_DIM, jnp.exp(acs[L - 1:L, js[0]:js[0] + 1]),
                               jnp.exp(acs[L - 1:L, js[1]:js[1] + 1]))
                prod = ds * hp
                dcd_all += [jnp.sum(prod[:HEAD_DIM], axis=0, keepdims=True),
                            jnp.sum(prod[HEAD_DIM:], axis=0, keepdims=True)]
                dh_scr[pr] = dh_out + ds * cd
                tcv = dxdd * xdd
                ta_all.append(dyv * yo - tcv)
                tc_all.append(jnp.sum(tcv, axis=0, keepdims=True))
                te_all.append(dxd * x)
                td_all.append(jnp.sum(dyv * x, axis=0, keepdims=True))
                dx_ref[pl.ds(r0, L), pr * LANES:(pr + 1) * LANES] = dxd * dt_p + dyv * d_p
            dgb = _bf(dg)
            dc_ref[pl.ds(r0, L), :] = dcacc + _dot(dgb, bb)
            db_ref[pl.ds(r0, L), :] = dbacc + _dot_tn(dgb, cb)
            last_add = jnp.zeros((1, LANES), F32)
            dd_add = jnp.zeros((1, LANES), F32)
            cd_row = jnp.exp(acs[L - 1:L, :])
            for j in range(SSD_HPG):
                hm = (lane1 >= HEAD_DIM) if j % 2 else (lane1 < HEAD_DIM)
                tcj = jnp.sum(jnp.where(hm, tc_all[j // 2], 0.0), axis=1, keepdims=True)
                tdj = jnp.sum(jnp.where(hm, td_all[j // 2], 0.0), axis=1, keepdims=True)
                dcdj = jnp.sum(dcd_all[j], axis=1, keepdims=True)
                last_add = last_add + jnp.where(lane1 == j, tcj + dcdj * cd_row, 0.0)
                dd_add = dd_add + jnp.where(lane1 == j, tdj, 0.0)
            dacs = (_dot(_split2(jnp.concatenate(ta_all, axis=1)), sum_pair2)
                    + _dot(_split2(jnp.concatenate(qm_all, axis=1)), sum_full2)
                    + dacs_t.T + jnp.where(rowc == L - 1, last_add, 0.0))
            ddta = _dot_hi(tri_t, dacs)
            ddt = _dot(_bf(jnp.concatenate(te_all, axis=1)), sum_pair) + ddta * a_neg
            ddraw = ddt * _sigmoid(raw)
            ddtr_ref[pl.ds(r0, L), :] = ddraw
            dbias_ref[...] += jnp.sum(ddraw, axis=0, keepdims=True)
            dalog_ref[...] += jnp.sum(ddta * dt, axis=0, keepdims=True) * a_neg
            ddsk_ref[...] += dd_add
            return 0

        lax.fori_loop(0, nc, chunk, 0)

    grp = lambda g, b: (g, 0, 0)
    small = jax.ShapeDtypeStruct((SSD_GROUPS, 1, LANES), F32)
    return pl.pallas_call(
        body, grid=(SSD_GROUPS, nb),
        in_specs=[pl.BlockSpec((s, 4 * LANES), lambda g, b: (b, g)),
                  pl.BlockSpec((s, LANES), lambda g, b: (b, 16 + g)),
                  pl.BlockSpec((s, LANES), lambda g, b: (b, 20 + g)),
                  pl.BlockSpec((None, s, LANES), lambda g, b: (g, b, 0)),
                  pl.BlockSpec((None, 1, LANES), grp), pl.BlockSpec((None, 1, LANES), grp),
                  pl.BlockSpec((None, 1, LANES), grp),
                  pl.BlockSpec((None, None, nc, N_PAIR, L, LANES), lambda g, b: (b, g, 0, 0, 0, 0)),
                  pl.BlockSpec((s, 4 * LANES), lambda g, b: (b, g))],
        out_specs=[pl.BlockSpec((s, 4 * LANES), lambda g, b: (b, g)),
                   pl.BlockSpec((s, LANES), lambda g, b: (b, g)),
                   pl.BlockSpec((s, LANES), lambda g, b: (b, g)),
                   pl.BlockSpec((None, s, LANES), lambda g, b: (g, b, 0)),
                   pl.BlockSpec((None, 1, LANES), grp), pl.BlockSpec((None, 1, LANES), grp),
                   pl.BlockSpec((None, 1, LANES), grp)],
        out_shape=[jax.ShapeDtypeStruct((t, SSD_WIDTH), F32),
                   jax.ShapeDtypeStruct((t, SSD_GROUPS * SSD_STATE), F32),
                   jax.ShapeDtypeStruct((t, SSD_GROUPS * SSD_STATE), F32),
                   jax.ShapeDtypeStruct((SSD_GROUPS, t, LANES), F32), small, small, small],
        scratch_shapes=[pltpu.VMEM((N_PAIR, L, LANES), F32)],
        compiler_params=_params(("parallel", "arbitrary")), name="ssd_bwd")(
            xbc, xbc, xbc, dtr, bias, alog, dskip, hs, dy)


GN = SSD_WIDTH // SSD_GROUPS


def _post_fwd(o, proj, y, nw, *, tm=256):
    t = o.shape[0]

    def body(o_ref, za_ref, y_ref, zs_ref, nw_ref, ya_ref, ys_ref):
        za = za_ref[...].astype(F32)
        ya_ref[...] = _bf(o_ref[...] * (za * _sigmoid(za)))
        zs = zs_ref[...].astype(F32)
        yz = y_ref[...] * (zs * _sigmoid(zs))
        for g in range(SSD_GROUPS):
            sl = slice(g * GN, (g + 1) * GN)
            v = yz[:, sl]
            r = lax.rsqrt(jnp.mean(v * v, axis=-1, keepdims=True) + EPS)
            ys_ref[:, sl] = _bf(v * r * nw_ref[:, sl])

    return pl.pallas_call(
        body, grid=(t // tm,),
        in_specs=[pl.BlockSpec((tm, SB_WIDTH), lambda i: (i, 0)),
                  pl.BlockSpec((tm, SB_WIDTH), lambda i: (i, CB_ZA * LANES // SB_WIDTH)),
                  pl.BlockSpec((tm, SSD_WIDTH), lambda i: (i, 0)),
                  pl.BlockSpec((tm, SSD_WIDTH), lambda i: (i, CB_ZS * LANES // SSD_WIDTH)),
                  pl.BlockSpec((1, SSD_WIDTH), lambda i: (0, 0))],
        out_specs=[pl.BlockSpec((tm, SB_WIDTH), lambda i: (i, 0)),
                   pl.BlockSpec((tm, SSD_WIDTH), lambda i: (i, 0))],
        out_shape=[jax.ShapeDtypeStruct((t, SB_WIDTH), BF16), jax.ShapeDtypeStruct((t, SSD_WIDTH), BF16)],
        compiler_params=_params(("parallel",)), name="post_fwd")(o, proj, y, proj, nw)


def _post_bwd(dya, o, proj, dys, y, nw, *, tm=256):
    t = o.shape[0]

    def body(dya_ref, o_ref, za_ref, dys_ref, y_ref, zs_ref, nw_ref,
             do_ref, dza_ref, dy_ref, dzs_ref, dnw_ref):
        @pl.when(pl.program_id(0) == 0)
        def _():
            dnw_ref[...] = jnp.zeros_like(dnw_ref)

        za = za_ref[...].astype(F32)
        sa = _sigmoid(za)
        d = dya_ref[...]
        do_ref[...] = d * (za * sa)
        dza_ref[...] = _bf(d * o_ref[...] * (sa * (1.0 + za * (1.0 - sa))))
        zs = zs_ref[...].astype(F32)
        ss = _sigmoid(zs)
        silu = zs * ss
        yv = y_ref[...]
        yz = yv * silu
        dv = dys_ref[...]
        for g in range(SSD_GROUPS):
            sl = slice(g * GN, (g + 1) * GN)
            v = yz[:, sl]
            r = lax.rsqrt(jnp.mean(v * v, axis=-1, keepdims=True) + EPS)
            nrm = v * r
            dg = dv[:, sl]
            dnw_ref[:, sl] += jnp.sum(dg * nrm, axis=0, keepdims=True)
            dn = dg * nw_ref[:, sl]
            dyz = r * (dn - nrm * jnp.mean(dn * nrm, axis=-1, keepdims=True))
            dy_ref[:, sl] = dyz * silu[:, sl]
            dzs_ref[:, sl] = _bf(dyz * yv[:, sl] * (ss[:, sl] * (1.0 + zs[:, sl] * (1.0 - ss[:, sl]))))

    a_spec = pl.BlockSpec((tm, SB_WIDTH), lambda i: (i, 0))
    s_spec = pl.BlockSpec((tm, SSD_WIDTH), lambda i: (i, 0))
    w_spec = pl.BlockSpec((1, SSD_WIDTH), lambda i: (0, 0))
    return pl.pallas_call(
        body, grid=(t // tm,),
        in_specs=[a_spec, a_spec, pl.BlockSpec((tm, SB_WIDTH), lambda i: (i, CB_ZA * LANES // SB_WIDTH)),
                  s_spec, s_spec, pl.BlockSpec((tm, SSD_WIDTH), lambda i: (i, CB_ZS * LANES // SSD_WIDTH)),
                  w_spec],
        out_specs=[a_spec, a_spec, s_spec, s_spec, w_spec],
        out_shape=[jax.ShapeDtypeStruct((t, SB_WIDTH), F32), jax.ShapeDtypeStruct((t, SB_WIDTH), BF16),
                   jax.ShapeDtypeStruct((t, SSD_WIDTH), F32), jax.ShapeDtypeStruct((t, SSD_WIDTH), BF16),
                   jax.ShapeDtypeStruct((1, SSD_WIDTH), F32)],
        compiler_params=_params(("arbitrary",)), name="post_bwd")(dya, o, proj, dys, y, proj, nw)


def _merge_fwd(ya, ys, proj_g, *, tm=512):
    t = ya.shape[0]

    def body(ya_ref, ys_ref, ga_ref, gs_ref, m_ref):
        m_ref[...] = _bf(_sigmoid(ga_ref[...].astype(F32)) * ya_ref[...]
                         + _sigmoid(gs_ref[...].astype(F32)) * ys_ref[...])

    spec = pl.BlockSpec((tm, D_MODEL), lambda i: (i, 0))
    return pl.pallas_call(
        body, grid=(t // tm,),
        in_specs=[spec, spec, pl.BlockSpec((tm, D_MODEL), lambda i: (i, 0)),
                  pl.BlockSpec((tm, D_MODEL), lambda i: (i, 1))],
        out_specs=spec, out_shape=jax.ShapeDtypeStruct((t, D_MODEL), BF16),
        compiler_params=_params(("parallel",)), name="merge_fwd")(ya, ys, proj_g, proj_g)


def _merge_bwd(dm, ya, ys, proj_g, *, tm=512):
    t = ya.shape[0]

    def body(dm_ref, ya_ref, ys_ref, ga_ref, gs_ref, dya_ref, dys_ref, dg_ref):
        d = dm_ref[...]
        ga = _sigmoid(ga_ref[...].astype(F32))
        gs = _sigmoid(gs_ref[...].astype(F32))
        dya_ref[...] = _bf(d * ga)
        dys_ref[...] = _bf(d * gs)
        dg_ref[:, :D_MODEL] = _bf(d * ya_ref[...] * (ga * (1.0 - ga)))
        dg_ref[:, D_MODEL:] = _bf(d * ys_ref[...] * (gs * (1.0 - gs)))

    spec = pl.BlockSpec((tm, D_MODEL), lambda i: (i, 0))
    return pl.pallas_call(
        body, grid=(t // tm,),
        in_specs=[spec, spec, spec, pl.BlockSpec((tm, D_MODEL), lambda i: (i, 0)),
                  pl.BlockSpec((tm, D_MODEL), lambda i: (i, 1))],
        out_specs=[spec, spec, pl.BlockSpec((tm, 2 * D_MODEL), lambda i: (i, 0))],
        out_shape=[jax.ShapeDtypeStruct((t, D_MODEL), BF16), jax.ShapeDtypeStruct((t, D_MODEL), BF16),
                   jax.ShapeDtypeStruct((t, 2 * D_MODEL), BF16)],
        compiler_params=_params(("parallel",)), name="merge_bwd")(dm, ya, ys, proj_g, proj_g)


def _final(x, out, wf, target, *, tm=512):
    t = x.shape[0]

    def body(x_ref, o_ref, w_ref, t_ref, loss_ref, dx_ref, dw_ref):
        @pl.when(pl.program_id(0) == 0)
        def _():
            loss_ref[...] = jnp.zeros_like(loss_ref)
            dw_ref[...] = jnp.zeros_like(dw_ref)

        x2 = x_ref[...] + o_ref[...]
        r = lax.rsqrt(jnp.mean(x2 * x2, axis=-1, keepdims=True) + EPS)
        nrm = x2 * r
        e = nrm * w_ref[...] - t_ref[...]
        row_loss = jnp.mean(e * e, axis=-1, keepdims=True)
        loss_ref[...] += 0.5 * jnp.sum(row_loss, axis=0, keepdims=True)
        dyv = e * (1.0 / D_MODEL)
        dw_ref[...] += jnp.sum(dyv * nrm, axis=0, keepdims=True)
        dn = dyv * w_ref[...]
        dx_ref[...] = r * (dn - nrm * jnp.mean(dn * nrm, axis=-1, keepdims=True))

    spec = pl.BlockSpec((tm, D_MODEL), lambda i: (i, 0))
    fix = pl.BlockSpec((1, D_MODEL), lambda i: (0, 0))
    return pl.pallas_call(
        body, grid=(t // tm,),
        in_specs=[spec, spec, fix, spec],
        out_specs=[pl.BlockSpec((1, LANES), lambda i: (0, 0)), spec, fix],
        out_shape=[jax.ShapeDtypeStruct((1, LANES), F32), jax.ShapeDtypeStruct((t, D_MODEL), F32),
                   jax.ShapeDtypeStruct((1, D_MODEL), F32)],
        compiler_params=_params(("arbitrary",)), name="final_loss")(x, out, wf, target)


def _group_lanes(v):
    return jnp.pad(v.reshape(SSD_GROUPS, 1, SSD_HPG), ((0, 0), (0, 0), (0, LANES - SSD_HPG)))


def _ungroup_lanes(v):
    return v[:, :, :SSD_HPG].reshape(1, N_DT)


def _row_slabs(pieces, rows):
    total = sum(p.shape[0] for p in pieces)
    assert total % rows == 0
    slabs = []
    for lo in range(0, total, rows):
        parts, off = [], 0
        for p in pieces:
            a, b = max(lo, off), min(lo + rows, off + p.shape[0])
            if a < b:
                parts.append(p[a - off:b - off])
            off += p.shape[0]
        slabs.append(parts[0] if len(parts) == 1 else jnp.concatenate(parts, axis=0))
    return jnp.stack(slabs)


def _local_step(x, target, nb, s, w_in_t, conv_w, conv_b, norm_w, dt_bias, a_log, d_skip,
                ssm_norm_w, out_weights, final_norm_w, between=None):
    t = nb * s
    w_g_t = w_in_t[GATE_COL0:]
    w_dt_t = jnp.pad(w_in_t[DT_COL0:GATE_COL0], ((0, LANES - N_DT), (0, 0)))
    h, r1 = _rms_fwd(x, norm_w)
    proj = _matmul(h, w_in_t, nt=True, n=DT_COL0, out_dtype=BF16, name="proj_main")
    proj_g = _matmul(h, w_g_t, nt=True, out_dtype=BF16, name="proj_gate")
    dt_raw = _matmul(h, w_dt_t, nt=True, name="proj_dt")
    dtr = jnp.pad(dt_raw[:, :N_DT].reshape(t, SSD_GROUPS, SSD_HPG).transpose(1, 0, 2),
                  ((0, 0), (0, 0), (0, LANES - SSD_HPG)))
    bias_g, alog_g, dsk_g = _group_lanes(dt_bias), _group_lanes(a_log), _group_lanes(d_skip)

    o, tot = _attn_fwd(proj, nb, s)
    xbc = _conv_fwd(proj, conv_w, conv_b, nb, s)
    y, hs = _ssd_fwd(xbc, dtr, bias_g, alog_g, dsk_g, nb, s)
    ya_in, ys_in = _post_fwd(o, proj, y, ssm_norm_w)
    w_attn_out, w_ssm_out, w_o = out_weights(ys_in) if callable(out_weights) else out_weights
    ya = _matmul(ya_in, w_attn_out, name="attn_out")
    ys = _matmul(ys_in, w_ssm_out, name="ssm_out")
    merged = _merge_fwd(ya, ys, proj_g)
    out = _matmul(merged, w_o, name="out_proj")
    loss, dx2, d_final_w = _final(x, out, final_norm_w, target)

    dx2b = _bf(dx2)
    dmerged = _matmul(dx2b, w_o, nt=True, name="d_merged")
    d_w_o = _matmul(merged, dx2b, ta=True, name="d_w_o")
    dya, dys, dgate = _merge_bwd(dmerged, ya, ys, proj_g)
    d_w_attn_out = _matmul(ya_in, dya, ta=True, name="d_w_attn_out")
    d_w_ssm_out = _matmul(ys_in, dys, ta=True, name="d_w_ssm_out")
    dya_in = _matmul(dya, w_attn_out, nt=True, name="d_ya_in")
    dys_in = _matmul(dys, w_ssm_out, nt=True, name="d_ys_in")
    do, dza, dy, dzs, d_ssm_norm_w = _post_bwd(dya_in, o, proj, dys_in, y, ssm_norm_w)
    dq, dk, dv = _attn_bwd(proj, do, tot, nb, s)
    dxs_act, db_act, dc_act, ddtr, dbias_g, dalog_g, ddsk_g = _ssd_bwd(
        xbc, dtr, bias_g, alog_g, dsk_g, hs, dy, nb, s)
    dxs, dwx, dbx = _conv_bwd(proj, dxs_act, conv_w, conv_b, nb, s, col0=0, name="conv_bwd_x")
    dbm, dwb, dbb = _conv_bwd(proj, db_act, conv_w, conv_b, nb, s, col0=SSD_WIDTH, name="conv_bwd_b")
    dcm, dwc, dbc = _conv_bwd(proj, dc_act, conv_w, conv_b, nb, s, col0=SSD_WIDTH + 512, name="conv_bwd_c")
    dproj = jnp.concatenate([dq, dk, dv, dza, dzs, dxs, dbm, dcm], axis=1)
    ddt = jnp.pad(_bf(ddtr[:, :, :SSD_HPG].transpose(1, 0, 2).reshape(t, N_DT)), ((0, 0), (0, LANES - N_DT)))
    wt = dict(ta=True, out_dtype=BF16)
    d_w_in_t = _row_slabs(
        [_matmul(dproj, h, name="d_w_main", **wt), _matmul(ddt, h, name="d_w_dt", **wt)[:N_DT],
         _matmul(dgate, h, name="d_w_gate", **wt)], SHARD)
    grads = dict(
        w_in_t=d_w_in_t,
        conv_w=jnp.concatenate([dwx, dwb, dwc], axis=1), conv_b=jnp.concatenate([dbx, dbb, dbc], axis=1),
        dt_bias=_ungroup_lanes(dbias_g), a_log=_ungroup_lanes(dalog_g), d_skip=_ungroup_lanes(ddsk_g),
        ssm_norm_w=d_ssm_norm_w, w_attn_out=d_w_attn_out, w_ssm_out=d_w_ssm_out, w_o=d_w_o,
        final_norm_w=d_final_w)
    extra = None
    if between is not None:
        zero, extra = between(grads)
        ddt = ddt + _bf(zero[0:1, 0:1])

    dh = _matmul(ddt, w_dt_t, name="d_h_dt")
    dh = _matmul(dgate, w_g_t, add=dh, name="d_h_gate")
    dh = _matmul(dproj, w_in_t, add=dh, name="d_h", tk=DT_COL0 // 4)
    grad_x, grads["norm_w"] = _rms_bwd(dh, x, r1, norm_w, dx2)
    return loss, grad_x, grads, extra


N_CHIP = N_DEV // 2


def _pair_exchange(scatter, bcast, *, name):
    arrays = list(scatter) + list(bcast)
    ns, n = len(scatter), len(arrays)
    n_sem = ns * N_CHIP + (n - ns) * (N_DEV - 1)

    def body(*refs):
        ins, outs = refs[:n], refs[n:2 * n]
        send_sems, recv_sems, local_sems = refs[2 * n:]
        x, y, c = lax.axis_index("x"), lax.axis_index("y"), lax.axis_index("c")
        me = 4 * x + 2 * y + c
        copies, k = [], 0
        for a in range(ns):
            for q in range(N_CHIP):
                cp = pltpu.make_async_remote_copy(
                    src_ref=ins[a].at[2 * q + 1 - c], dst_ref=outs[a].at[q],
                    send_sem=send_sems.at[k], recv_sem=recv_sems.at[k],
                    device_id=(x, y, 1 - c), device_id_type=pl.DeviceIdType.MESH)
                cp.start()
                copies.append(cp)
                k += 1
        for a in range(ns, n):
            cp = pltpu.make_async_copy(ins[a], outs[a].at[me], local_sems.at[a - ns])
            cp.start()
            copies.append(cp)
            for r in range(1, N_DEV):
                tx = 1 - x if (r >> 2) & 1 else x
                ty = 1 - y if (r >> 1) & 1 else y
                tc = 1 - c if r & 1 else c
                cp = pltpu.make_async_remote_copy(
                    src_ref=ins[a], dst_ref=outs[a].at[me], send_sem=send_sems.at[k], recv_sem=recv_sems.at[k],
                    device_id=(tx, ty, tc), device_id_type=pl.DeviceIdType.MESH)
                cp.start()
                copies.append(cp)
                k += 1
        for cp in copies:
            cp.wait()

    out_shape = [jax.ShapeDtypeStruct((N_CHIP,) + v.shape[1:] if i < ns else (N_DEV,) + v.shape, v.dtype)
                 for i, v in enumerate(arrays)]
    return pl.pallas_call(
        body, in_specs=[pl.BlockSpec(memory_space=pl.ANY)] * n,
        out_specs=[pl.BlockSpec(memory_space=pl.ANY)] * n, out_shape=out_shape,
        scratch_shapes=[pltpu.SemaphoreType.DMA((n_sem,)), pltpu.SemaphoreType.DMA((n_sem,)),
                        pltpu.SemaphoreType.DMA((max(n - ns, 1),))],
        compiler_params=pltpu.CompilerParams(has_side_effects=True), name=name)(*arrays)


def _pair_add(mine, recv, core, *, name, tc=None):
    _, r, c = recv.shape
    tc = c if tc is None else tc
    assert c % tc == 0

    def body(core_ref, m_ref, r_ref, o_ref):
        o_ref[...] = (m_ref[...].astype(F32) + r_ref[...].astype(F32)).astype(o_ref.dtype)

    spec = pl.BlockSpec((None, r, tc), lambda q, i, core_ref: (q, 0, i))
    return pl.pallas_call(
        body, grid_spec=pltpu.PrefetchScalarGridSpec(
            num_scalar_prefetch=1, grid=(N_CHIP, c // tc),
            in_specs=[pl.BlockSpec((None, r, tc), lambda q, i, core_ref: (2 * q + core_ref[0], 0, i)), spec],
            out_specs=spec),
        out_shape=jax.ShapeDtypeStruct(recv.shape, recv.dtype),
        compiler_params=_params(("parallel", "parallel")), name=name)(core, mine, recv)


_HBM = pl.BlockSpec(memory_space=pltpu.HBM)
_SEM = pl.BlockSpec(memory_space=pltpu.SEMAPHORE)
_DATAFLOW = pltpu.SideEffectType.DATAFLOW_SIDE_EFFECTING
_PEERS = {"chips": N_CHIP - 1, "all": N_DEV - 1}


def _split_copies(pattern, srcs, lands, send_sems, recv_sems):
    x, y, c = lax.axis_index("x"), lax.axis_index("y"), lax.axis_index("c")
    per = _PEERS[pattern]
    copies = []
    for a in range(len(srcs)):
        for r in range(1, per + 1):
            if pattern == "chips":
                tx = 1 - x if (r >> 1) & 1 else x
                ty = 1 - y if r & 1 else y
                tc = c
                src, dst = srcs[a].at[2 * tx + ty], lands[a].at[2 * x + y]
            else:
                tx = 1 - x if (r >> 2) & 1 else x
                ty = 1 - y if (r >> 1) & 1 else y
                tc = 1 - c if r & 1 else c
                src, dst = srcs[a], lands[a].at[4 * x + 2 * y + c]
            copies.append(pltpu.make_async_remote_copy(
                src_ref=src, dst_ref=dst, send_sem=send_sems.at[a * per + r - 1],
                recv_sem=recv_sems.at[a * per + r - 1], device_id=(tx, ty, tc), device_id_type=pl.DeviceIdType.MESH))
    return copies


def _split_start(pattern, arrays, *, name):
    n = len(arrays)
    n_sem = n * _PEERS[pattern]

    def body(*refs):
        srcs, lands = refs[:n], refs[n:2 * n]
        send_sems, recv_sems = refs[2 * n], refs[2 * n + 1]
        zero = refs[-1]
        for cp in _split_copies(pattern, srcs, lands, send_sems, recv_sems):
            cp.start()
        zero[...] = jnp.zeros_like(zero)

    src_t = [pltpu.HBM(v.shape, v.dtype) for v in arrays]
    land_t = [pltpu.HBM(v.shape if pattern == "chips" else (N_DEV,) + v.shape, v.dtype) for v in arrays]
    outs = pl.pallas_call(
        body, name=name,
        out_shape=(pltpu.SemaphoreType.DMA((n_sem,)), pltpu.SemaphoreType.DMA((n_sem,)), *src_t, *land_t,
                   jax.ShapeDtypeStruct((8, LANES), F32)),
        in_specs=[_HBM] * (2 * n),
        out_specs=(_SEM, _SEM, *[_HBM] * (2 * n), pl.BlockSpec(memory_space=pltpu.VMEM)),
        input_output_aliases={i: 2 + i for i in range(2 * n)},
        compiler_params=pltpu.CompilerParams(has_side_effects=_DATAFLOW))(
            *[pltpu.with_memory_space_constraint(v, pltpu.HBM) for v in arrays],
            *[pltpu.with_memory_space_constraint(lax.empty(t.shape, t.dtype), pltpu.HBM) for t in land_t])
    return outs[0], outs[1], outs[2:2 + n], outs[2 + n:2 + 2 * n], outs[-1]


def _split_wait(pattern, send_sems, recv_sems, srcs, lands, after, *, name):
    n = len(srcs)

    def body(*refs):
        for cp in _split_copies(pattern, refs[:n], refs[n:2 * n], refs[2 * n], refs[2 * n + 1]):
            cp.wait_send()
            cp.wait_recv()

    outs = pl.pallas_call(
        body, name=name, out_shape=[pltpu.HBM(v.shape, v.dtype) for v in (*srcs, *lands)],
        in_specs=[_HBM] * (2 * n) + [_SEM, _SEM, pl.BlockSpec(memory_space=pl.ANY)],
        out_specs=[_HBM] * (2 * n), input_output_aliases={i: i for i in range(2 * n)},
        compiler_params=pltpu.CompilerParams(has_side_effects=_DATAFLOW))(
            *srcs, *lands, send_sems, recv_sems, after)
    return outs[:n], outs[n:]


def _own_slab(lands, srcs, index, whole):
    out = []
    for land, src in zip(lands, srcs):
        own = src[None] if whole else lax.dynamic_slice(src, (index, 0, 0), (1,) + src.shape[1:])
        out.append(lax.dynamic_update_slice(land, own, (index, 0, 0)))
    return out


def _gather(arrays, *, name):
    n = len(arrays)
    per = N_DEV - 1

    def body(*refs):
        ins, outs = refs[:n], refs[n:2 * n]
        send_sems, recv_sems, local_sems = refs[2 * n:]
        x, y, c = lax.axis_index("x"), lax.axis_index("y"), lax.axis_index("c")
        me, sibling = (x, y, c), (x, y, 1 - c)
        chips = [(1 - x, y), (x, 1 - y), (1 - x, 1 - y)]

        def slab(a, block):
            return outs[a].at[4 * block[0] + 2 * block[1] + block[2]]

        def copy(a, k, block, to, src=None):
            rows = slab(a, block)
            return pltpu.make_async_remote_copy(
                src_ref=rows if src is None else src, dst_ref=rows,
                send_sem=send_sems.at[a * per + k], recv_sem=recv_sems.at[a * per + k],
                device_id=to, device_id_type=pl.DeviceIdType.MESH)

        started = []
        for a in range(n):
            mine = pltpu.make_async_copy(ins[a], slab(a, me), local_sems.at[a])
            mine.start()
            started.append(mine)
        for a in range(n):
            first = [copy(a, 1 + j, me, (*chip, c), src=ins[a]) for j, chip in enumerate(chips)]
            first.append(copy(a, 0, me, sibling, src=ins[a]))
            for cp in first:
                cp.start()
            started += first
        for j, chip in enumerate(chips):
            for a in range(n):
                copy(a, 1 + j, (*chip, c), me).wait_recv()
                passed = copy(a, 4 + j, (*chip, c), sibling)
                passed.start()
                started.append(passed)
        for a in range(n):
            copy(a, 0, sibling, me).wait_recv()
            for j, chip in enumerate(chips):
                copy(a, 4 + j, (*chip, 1 - c), me).wait_recv()
        for cp in started[:n]:
            cp.wait()
        for cp in started[n:]:
            cp.wait_send()

    return pl.pallas_call(
        body, in_specs=[pl.BlockSpec(memory_space=pl.ANY)] * n,
        out_specs=[pl.BlockSpec(memory_space=pl.ANY)] * n,
        out_shape=[jax.ShapeDtypeStruct((N_DEV,) + v.shape, v.dtype) for v in arrays],
        scratch_shapes=[pltpu.SemaphoreType.DMA((n * per,)), pltpu.SemaphoreType.DMA((n * per,)),
                        pltpu.SemaphoreType.DMA((n,))],
        compiler_params=pltpu.CompilerParams(has_side_effects=True), name=name)(*arrays)


def _adamw(parts, w, m, v, *, name, tc=None):
    r, c = w.shape
    n_parts = parts.shape[0]
    tc = c if tc is None else tc
    assert c % tc == 0

    def body(p_ref, w_ref, m_ref, v_ref, g_ref, d_ref, nm_ref, nv_ref):
        g = p_ref[0].astype(F32)
        for d in range(1, n_parts):
            g = g + p_ref[d].astype(F32)
        mm = ADAM_B1 * m_ref[...] + (1.0 - ADAM_B1) * g
        vv = ADAM_B2 * v_ref[...] + (1.0 - ADAM_B2) * (g * g)
        m_hat = mm / (1.0 - ADAM_B1 ** ADAM_STEP)
        v_hat = vv / (1.0 - ADAM_B2 ** ADAM_STEP)
        g_ref[...] = g
        d_ref[...] = -ADAM_LR * (m_hat / (jnp.sqrt(v_hat) + ADAM_EPS) + ADAM_WD * w_ref[...])
        nm_ref[...] = mm
        nv_ref[...] = vv

    spec = pl.BlockSpec((r, tc), lambda i: (0, i))
    out = jax.ShapeDtypeStruct((r, c), F32)
    return pl.pallas_call(
        body, grid=(c // tc,),
        in_specs=[pl.BlockSpec((n_parts, r, tc), lambda i: (0, 0, i)), spec, spec, spec],
        out_specs=[spec] * 4, out_shape=[out] * 4,
        compiler_params=_params(("parallel",)), name=name)(parts, w, m, v)


SMALL = (("norm_w", D_MODEL), ("conv_b", CONV_DIM), ("dt_bias", N_DT), ("a_log", N_DT),
         ("d_skip", N_DT), ("ssm_norm_w", SSD_WIDTH), ("final_norm_w", D_MODEL))
SMALL_ROWS = 64
SHARD = D_PROJ // N_DEV


def _pack_small(vals):
    flat = jnp.concatenate([vals[k].reshape(-1).astype(F32) for k, _ in SMALL])
    return jnp.pad(flat, (0, SMALL_ROWS * LANES - flat.shape[0])).reshape(SMALL_ROWS, LANES)


def _unpack_small(packed, shapes):
    flat = packed.reshape(-1)
    out, off = {}, 0
    for k, size in SMALL:
        out[k] = flat[off:off + size].reshape(shapes[k])
        off += size
    return out


def kernel(x, norm_w, w_in, conv_w, conv_b, dt_bias, a_log, d_skip, ssm_norm_w, w_attn_out, w_ssm_out, w_o, final_norm_w, loss_target, m_norm_w, m_w_in, m_conv_w, m_conv_b, m_dt_bias, m_a_log, m_d_skip, m_ssm_norm_w, m_w_attn_out, m_w_ssm_out, m_w_o, m_final_norm_w, v_norm_w, v_w_in, v_conv_w, v_conv_b, v_dt_bias, v_a_log, v_d_skip, v_ssm_norm_w, v_w_attn_out, v_w_ssm_out, v_w_o, v_final_norm_w):
    nb, s, _ = x.shape
    t = nb * s
    weights = dict(norm_w=norm_w, w_in=w_in, conv_w=conv_w, conv_b=conv_b, dt_bias=dt_bias, a_log=a_log,
                   d_skip=d_skip, ssm_norm_w=ssm_norm_w, w_attn_out=w_attn_out, w_ssm_out=w_ssm_out,
                   w_o=w_o, final_norm_w=final_norm_w)
    moms = dict(norm_w=m_norm_w, w_in=m_w_in, conv_w=m_conv_w, conv_b=m_conv_b, dt_bias=m_dt_bias,
                a_log=m_a_log, d_skip=m_d_skip, ssm_norm_w=m_ssm_norm_w, w_attn_out=m_w_attn_out,
                w_ssm_out=m_w_ssm_out, w_o=m_w_o, final_norm_w=m_final_norm_w)
    vels = dict(norm_w=v_norm_w, w_in=v_w_in, conv_w=v_conv_w, conv_b=v_conv_b, dt_bias=v_dt_bias,
                a_log=v_a_log, d_skip=v_d_skip, ssm_norm_w=v_ssm_norm_w, w_attn_out=v_w_attn_out,
                w_ssm_out=v_w_ssm_out, w_o=v_w_o, final_norm_w=v_final_norm_w)

    tr_ = lambda a: jnp.transpose(a[0])
    g_in, g_conv = _gather([_bf(tr_(w_in)), conv_w[0]], name="gather_weights")
    conv_full = g_conv.transpose(1, 0, 2).reshape(CONV_K, CONV_DIM)
    o_send, o_recv, o_srcs, o_lands, o_zero = _split_start(
        "all", [_bf(w_attn_out[0]), _bf(w_ssm_out[0]), _bf(w_o[0])], name="gather_out_weights_start")

    core = lax.axis_index("c").astype(jnp.int32).reshape(1)
    chip = 2 * lax.axis_index("x") + lax.axis_index("y")
    me = 2 * chip + lax.axis_index("c")

    def out_weights(after):
        srcs, lands = _split_wait("all", o_send, o_recv, o_srcs, o_lands, after, name="gather_out_weights_wait")
        g_wa, g_ws, g_wo = _own_slab(lands, srcs, me, True)
        return (g_wa.reshape(SB_WIDTH, D_MODEL), g_ws.reshape(SSD_WIDTH, D_MODEL), g_wo.reshape(D_MODEL, D_MODEL))

    def send_weight_grads(grads):
        slabs = [grads["w_in_t"],
                 grads["conv_w"].reshape(CONV_K, N_DEV, CONV_DIM // N_DEV).transpose(1, 0, 2),
                 _bf(grads["w_attn_out"].reshape(N_DEV, SB_WIDTH // N_DEV, D_MODEL)),
                 _bf(grads["w_ssm_out"].reshape(N_DEV, SSD_WIDTH // N_DEV, D_MODEL)),
                 _bf(grads["w_o"].reshape(N_DEV, D_MODEL // N_DEV, D_MODEL))]
        recv = _pair_exchange(slabs, [], name="exchange_pairs")
        names = ("w_in", "conv_w", "w_attn_out", "w_ssm_out", "w_o")
        sums = [_pair_add(a, b, core, name="pair_add_" + k, tc=4 * LANES if k == "w_in" else None)
                for k, a, b in zip(names, slabs, recv)]
        send_sems, recv_sems, srcs, lands, zero = _split_start("chips", sums, name="exchange_chips_start")
        return zero, (send_sems, recv_sems, srcs, lands)

    loss_part, grad_x, grads, (send_sems, recv_sems, srcs, lands) = _local_step(
        x.reshape(t, D_MODEL), loss_target.reshape(t, D_MODEL), nb, s, g_in.reshape(D_PROJ, D_MODEL),
        conv_full, conv_b, norm_w + o_zero[0:1, 0:1], dt_bias, a_log, d_skip, ssm_norm_w, out_weights,
        final_norm_w.reshape(1, D_MODEL), between=send_weight_grads)

    (p_small,) = _pair_exchange([], [_pack_small(grads)], name="exchange_small")
    srcs, lands = _split_wait("chips", send_sems, recv_sems, srcs, lands, grad_x, name="exchange_chips_wait")
    p_in, p_conv, p_wa, p_ws, p_wo = _own_slab(lands, srcs, chip, False)

    res = {}
    res["w_in"] = [a.T for a in _adamw(p_in, tr_(w_in), tr_(m_w_in), tr_(v_w_in), name="adamw_w_in", tc=2 * LANES)]
    res["conv_w"] = _adamw(p_conv, conv_w[0], m_conv_w[0], v_conv_w[0], name="adamw_conv_w")
    res["w_attn_out"] = _adamw(p_wa, w_attn_out[0], m_w_attn_out[0], v_w_attn_out[0], name="adamw_w_attn_out")
    res["w_ssm_out"] = _adamw(p_ws, w_ssm_out[0], m_w_ssm_out[0], v_w_ssm_out[0], name="adamw_w_ssm_out")
    res["w_o"] = _adamw(p_wo, w_o[0], m_w_o[0], v_w_o[0], name="adamw_w_o")
    sm = _adamw(p_small, _pack_small(weights), _pack_small(moms), _pack_small(vels), name="adamw_small")
    shapes = {k: weights[k].shape for k, _ in SMALL}
    sm = [_unpack_small(v, shapes) for v in sm]
    for k, _ in SMALL:
        res[k] = tuple(part[k] for part in sm)

    order = ("norm_w", "w_in", "conv_w", "conv_b", "dt_bias", "a_log", "d_skip", "ssm_norm_w",
             "w_attn_out", "w_ssm_out", "w_o", "final_norm_w")
    loss = lax.psum(loss_part[0, 0], ("x", "y", "c"))
    outs = [loss, grad_x.reshape(nb, s, D_MODEL)]
    for i in range(4):
        outs += [res[k][i].reshape(weights[k].shape) for k in order]
    return tuple(outs)
```

```python
import functools

import jax
import jax.numpy as jnp
from jax import lax
from jax.experimental import pallas as pl
from jax.experimental.pallas import tpu as pltpu

F32 = jnp.float32
BF16 = jnp.bfloat16

D_MODEL = 1024
SB_WIDTH = 1024
HEAD_DIM = 64
SSD_WIDTH = 2048
SSD_GROUPS = 4
SSD_HPG = 8
SSD_STATE = 128
SSD_CHUNK = 128
CONV_K = 4
CONV_DIM = 3072
N_DT = 32
D_PROJ = 11296
DT_COL0 = 9216
GATE_COL0 = DT_COL0 + N_DT
EPS = 1e-6
N_DEV = 8

ADAM_LR = 0.001
ADAM_B1 = 0.9
ADAM_B2 = 0.999
ADAM_EPS = 1e-08
ADAM_WD = 0.01
ADAM_STEP = 10

LANES = 128
VMEM_LIMIT = 56 * 1024 * 1024

CB_Q, CB_K, CB_V, CB_ZA, CB_ZS, CB_XBC = 0, 8, 16, 24, 32, 48

NT_DIMS = (((1,), (1,)), ((), ()))
TN_DIMS = (((0,), (0,)), ((), ()))
HI = lax.Precision.HIGHEST


def _params(sem):
    return pltpu.CompilerParams(dimension_semantics=sem, vmem_limit_bytes=VMEM_LIMIT)


def _dot(a, b):
    return jnp.dot(a, b, preferred_element_type=F32)


def _dot_nt(a, b):
    return lax.dot_general(a, b, NT_DIMS, preferred_element_type=F32)


def _dot_tn(a, b):
    return lax.dot_general(a, b, TN_DIMS, preferred_element_type=F32)


def _dot_hi(a, b):
    return jnp.dot(a, b, preferred_element_type=F32, precision=HI)


def _bf(a):
    return a.astype(BF16)


def _sigmoid(x):
    return 0.5 + 0.5 * jnp.tanh(0.5 * x)


def _softplus(x):
    return jnp.maximum(x, 0.0) + jnp.log(1.0 + jnp.exp(-jnp.abs(x)))


def _trunc_split(a):
    bits = lax.bitcast_convert_type(a, jnp.uint32) & jnp.uint32(0xFFFF0000)
    hi = lax.bitcast_convert_type(bits, F32)
    return _bf(hi), _bf(a - hi)


MM_TILE = 1024
MM_TK = 2048


def _matmul(a, b, *, name, nt=False, ta=False, add=None, out_dtype=F32, tm=MM_TILE, tn=MM_TILE, tk=MM_TK, n=None):
    k, m = a.shape if ta else a.shape[::-1]
    n = (b.shape[0] if nt else b.shape[1]) if n is None else n
    tm, tn, tk = min(tm, m), min(tn, n), min(tk, k)
    assert m % tm == 0 and n % tn == 0 and k % tk == 0, (name, a.shape, b.shape)
    assert not (ta and nt)
    nk = k // tk
    has_add = add is not None

    def body(*refs):
        if has_add:
            a_ref, b_ref, add_ref, o_ref, acc_ref = refs
        else:
            a_ref, b_ref, o_ref, acc_ref = refs
        kk = pl.program_id(2)
        av, bv = _bf(a_ref[...]), _bf(b_ref[...])
        d = _dot_nt(av, bv) if nt else _dot_tn(av, bv) if ta else _dot(av, bv)
        if nk == 1:
            o_ref[...] = ((d + add_ref[...]) if has_add else d).astype(out_dtype)
            return

        @pl.when(kk == 0)
        def _():
            acc_ref[...] = jnp.zeros_like(acc_ref)

        acc_ref[...] += d

        @pl.when(kk == nk - 1)
        def _():
            r = acc_ref[...]
            if has_add:
                r = r + add_ref[...]
            o_ref[...] = r.astype(out_dtype)

    in_specs = [pl.BlockSpec((tk, tm), lambda j, i, kk: (kk, i)) if ta
                else pl.BlockSpec((tm, tk), lambda j, i, kk: (i, kk)),
                pl.BlockSpec((tn, tk), lambda j, i, kk: (j, kk)) if nt
                else pl.BlockSpec((tk, tn), lambda j, i, kk: (kk, j))]
    args = [a, b]
    if has_add:
        in_specs.append(pl.BlockSpec((tm, tn), lambda j, i, kk: (i, j)))
        args.append(add)
    return pl.pallas_call(
        body, grid=(n // tn, m // tm, nk), in_specs=in_specs,
        out_specs=pl.BlockSpec((tm, tn), lambda j, i, kk: (i, j)),
        out_shape=jax.ShapeDtypeStruct((m, n), out_dtype),
        scratch_shapes=[pltpu.VMEM((tm, tn), F32)],
        compiler_params=_params(("parallel", "parallel", "arbitrary")), name=name)(*args)


def _rms_fwd(x, w, *, tm=512):
    t = x.shape[0]

    def body(x_ref, w_ref, h_ref, r_ref):
        xv = x_ref[...]
        r = lax.rsqrt(jnp.mean(xv * xv, axis=-1, keepdims=True) + EPS)
        h_ref[...] = _bf(xv * r * w_ref[...])
        r_ref[...] = r

    row = lambda i: (i, 0)
    return pl.pallas_call(
        body, grid=(t // tm,),
        in_specs=[pl.BlockSpec((tm, D_MODEL), row), pl.BlockSpec((1, D_MODEL), lambda i: (0, 0))],
        out_specs=[pl.BlockSpec((tm, D_MODEL), row), pl.BlockSpec((tm, 1), row)],
        out_shape=[jax.ShapeDtypeStruct((t, D_MODEL), BF16), jax.ShapeDtypeStruct((t, 1), F32)],
        compiler_params=_params(("parallel",)), name="rms_fwd")(x, w)


def _rms_bwd(dh, x, r, w, dres, *, tm=512):
    t = x.shape[0]

    def body(dh_ref, x_ref, r_ref, w_ref, dres_ref, dx_ref, dw_ref):
        @pl.when(pl.program_id(0) == 0)
        def _():
            dw_ref[...] = jnp.zeros_like(dw_ref)

        nrm = x_ref[...] * r_ref[...]
        dhv = dh_ref[...]
        dw_ref[...] += jnp.sum(dhv * nrm, axis=0, keepdims=True)
        dn = dhv * w_ref[...]
        dx = r_ref[...] * (dn - nrm * jnp.mean(dn * nrm, axis=-1, keepdims=True))
        dx_ref[...] = dx + dres_ref[...]

    row = lambda i: (i, 0)
    fix = lambda i: (0, 0)
    return pl.pallas_call(
        body, grid=(t // tm,),
        in_specs=[pl.BlockSpec((tm, D_MODEL), row), pl.BlockSpec((tm, D_MODEL), row),
                  pl.BlockSpec((tm, 1), row), pl.BlockSpec((1, D_MODEL), fix),
                  pl.BlockSpec((tm, D_MODEL), row)],
        out_specs=[pl.BlockSpec((tm, D_MODEL), row), pl.BlockSpec((1, D_MODEL), fix)],
        out_shape=[jax.ShapeDtypeStruct((t, D_MODEL), F32), jax.ShapeDtypeStruct((1, D_MODEL), F32)],
        compiler_params=_params(("arbitrary",)), name="rms_bwd")(dh, x, r, w, dres)


ATT_BLK = 256


def _attn_masks(blk):
    r_io = lax.broadcasted_iota(jnp.int32, (blk, blk), 0)
    c_io = lax.broadcasted_iota(jnp.int32, (blk, blk), 1)
    return r_io, c_io


def _attn_fwd(proj, nb, s):
    blk = min(ATT_BLK, s)
    nq = s // blk
    scale = HEAD_DIM ** -0.5

    def body(q_ref, k_ref, v_ref, o_ref, tot_ref, q0_s, q1_s, kb_s, v0_s, v1_s, z_s, a_s, acc_s, cr_s):
        head0 = lax.broadcasted_iota(jnp.int32, (s, LANES), 1) < HEAD_DIM
        lane2 = lax.broadcasted_iota(jnp.int32, (blk, 2), 1)
        r_io, c_io = _attn_masks(blk)
        tri = c_io < r_io
        tri2 = jnp.concatenate([tri, tri], axis=0)
        u_gt = _bf(r_io > c_io)
        u2 = jnp.concatenate([u_gt, u_gt], axis=0)
        qv = q_ref[...] * scale
        q0_s[...] = _bf(jnp.where(head0, qv, 0.0))
        q1_s[...] = _bf(jnp.where(head0, 0.0, qv))
        kb_s[...] = _bf(k_ref[...])
        vv = v_ref[...]
        v0_s[...] = _bf(jnp.where(head0, vv, 0.0))
        v1_s[...] = _bf(jnp.where(head0, 0.0, vv))

        def scores(qcat, kblk, slot):
            k0 = pl.multiple_of(jnp.maximum(kblk, 0) * blk, blk)
            z_s[slot] = _dot_nt(qcat, kb_s[pl.ds(k0, blk), :])

        def weights(slot, diag, zslot=None):
            z = z_s[slot if zslot is None else zslot]
            cr = cr_s[...]
            sp = _softplus(z)
            spm = jnp.where(tri2, sp, 0.0) if diag else sp
            hi, lo = _trunc_split(spm)
            ps = _dot(jnp.concatenate([hi, lo], axis=1), u2)
            a = jnp.exp(z - (sp + ps + cr))
            if diag:
                a = jnp.where(tri2, a, 0.0)
            a_s[slot] = _bf(a)
            cr_s[...] = cr + ps[:, 0:1] + spm[:, 0:1]

        def apply(slot, k0):
            a = a_s[slot]
            vcat = jnp.concatenate([v0_s[pl.ds(k0, blk), :], v1_s[pl.ds(k0, blk), :]], axis=0)
            acc_s[...] += _dot(jnp.concatenate([a[:blk], a[blk:]], axis=1), vcat)

        def queries(qi):
            q0 = pl.multiple_of(qi * blk, blk)
            return jnp.concatenate([q0_s[pl.ds(q0, blk), :], q1_s[pl.ds(q0, blk), :]], axis=0)

        def first_scores(qi):
            qcat = queries(qi)
            scores(qcat, qi, 2)
            scores(qcat, qi - 1, 3)

        first_scores(jnp.int32(0))

        def qblock(qi, _):
            q0 = pl.multiple_of(qi * blk, blk)
            qcat = queries(qi)
            z_s[1] = z_s[3]
            cr_s[...] = jnp.zeros_like(cr_s)
            acc_s[...] = jnp.zeros_like(acc_s)
            weights(0, True, zslot=2)
            first_scores(jnp.minimum(qi + 1, nq - 1))

            def step(j, slot):
                scores(qcat, qi - j - 1, 1 - slot)
                weights(slot, False)
                apply(1 - slot, pl.multiple_of((qi - j + 1) * blk, blk))

            def pair(i, _):
                step(2 * i + 1, 1)
                step(2 * i + 2, 0)
                return 0

            lax.fori_loop(0, qi // 2, pair, 0)

            @pl.when(qi % 2 == 1)
            def _():
                step(qi, 1)

            apply(qi & 1, 0)
            o_ref[pl.ds(q0, blk), :] = acc_s[...]
            cr = cr_s[...]
            tot_ref[pl.ds(q0, blk), :] = jnp.where(lane2 == 0, cr[:blk], cr[blk:])
            return 0

        lax.fori_loop(0, nq, qblock, 0)

    t = nb * s
    return pl.pallas_call(
        body, grid=(nb, 8),
        in_specs=[pl.BlockSpec((s, LANES), lambda b, p: (b, CB_Q + p)),
                  pl.BlockSpec((s, LANES), lambda b, p: (b, CB_K + p)),
                  pl.BlockSpec((s, LANES), lambda b, p: (b, CB_V + p))],
        out_specs=[pl.BlockSpec((s, LANES), lambda b, p: (b, p)),
                   pl.BlockSpec((None, None, s, 2), lambda b, p: (b, p, 0, 0))],
        out_shape=[jax.ShapeDtypeStruct((t, SB_WIDTH), F32),
                   jax.ShapeDtypeStruct((nb, 8, s, 2), F32)],
        scratch_shapes=[pltpu.VMEM((s, LANES), BF16)] * 5 + [
            pltpu.VMEM((4, 2 * blk, blk), F32), pltpu.VMEM((2, 2 * blk, blk), BF16),
            pltpu.VMEM((blk, LANES), F32), pltpu.VMEM((2 * blk, 1), F32)],
        compiler_params=_params(("parallel", "parallel")), name="attn_fwd")(proj, proj, proj)


def _attn_bwd(proj, do, tot, nb, s):
    blk = min(ATT_BLK, s)
    nq = s // blk
    scale = HEAD_DIM ** -0.5

    def body(q_ref, k_ref, v_ref, do_ref, tot_ref, dq_ref, dk_ref, dv_ref,
             q0_s, q1_s, kb_s, k0_s, k1_s, vb_s, d0_s, d1_s, z_s, da_s, dz_s, a_s, dk_acc, dv_acc,
             dq_acc, cn_s, cw_s):
        head0 = lax.broadcasted_iota(jnp.int32, (s, LANES), 1) < HEAD_DIM
        r_io, c_io = _attn_masks(blk)
        tri = c_io < r_io
        tri2 = jnp.concatenate([tri, tri], axis=0)
        u_le = _bf(r_io <= c_io)
        u2 = jnp.concatenate([u_le, u_le], axis=0)
        u_lt = _bf(r_io < c_io)
        qv = q_ref[...] * scale
        q0_s[...] = _bf(jnp.where(head0, qv, 0.0))
        q1_s[...] = _bf(jnp.where(head0, 0.0, qv))
        kv = k_ref[...]
        kb_s[...] = _bf(kv)
        k0_s[...] = _bf(jnp.where(head0, kv, 0.0))
        k1_s[...] = _bf(jnp.where(head0, 0.0, kv))
        vb_s[...] = _bf(v_ref[...])
        dov = do_ref[...]
        d0_s[...] = _bf(jnp.where(head0, dov, 0.0))
        d1_s[...] = _bf(jnp.where(head0, 0.0, dov))
        dk_acc[...] = jnp.zeros_like(dk_acc)
        dv_acc[...] = jnp.zeros_like(dv_acc)

        def qblock(qi, _):
            q0 = pl.multiple_of(qi * blk, blk)
            qs_ = pl.ds(q0, blk)
            qhs = (q0_s[qs_, :], q1_s[qs_, :])
            dhs = (d0_s[qs_, :], d1_s[qs_, :])
            qcat = jnp.concatenate(qhs, axis=0)
            dcat = jnp.concatenate(dhs, axis=0)
            totf = tot_ref[qs_, :]

            def scores(kblk, slot):
                ks_ = pl.ds(pl.multiple_of(kblk * blk, blk), blk)
                z_s[slot] = _dot_nt(qcat, kb_s[ks_, :])
                da_s[slot] = _dot_nt(dcat, vb_s[ks_, :])

            def grads(slot, diag):
                cn, cw = cn_s[...], cw_s[...]
                z = z_s[slot]
                sp = _softplus(z)
                lb = z - sp
                spm = jnp.where(tri2, sp, 0.0) if diag else sp
                hi, lo = _trunc_split(spm)
                pin = _dot(jnp.concatenate([hi, lo], axis=1), u2)
                a = jnp.exp(lb + (pin + cn))
                if diag:
                    a = jnp.where(tri2, a, 0.0)
                w = a * da_s[slot]
                wex = _dot(_bf(w), u_lt)
                dz = w - jnp.exp(lb) * (w + (wex + cw))
                if diag:
                    dz = jnp.where(tri2, dz, 0.0)
                dz_s[slot] = _bf(dz)
                a_s[slot] = _bf(a)
                cn_s[...] = cn + pin[:, blk - 1:blk]
                cw_s[...] = cw + wex[:, blk - 1:blk] + w[:, blk - 1:blk]

            def apply(slot, kblk):
                ks_ = pl.ds(pl.multiple_of(kblk * blk, blk), blk)
                dz = dz_s[slot]
                dq_acc[...] += _dot(jnp.concatenate([dz[:blk], dz[blk:]], axis=1),
                                    jnp.concatenate([k0_s[ks_, :], k1_s[ks_, :]], axis=0))
                dk_acc[ks_, :] += _dot_tn(dz, qcat)
                dv_acc[ks_, :] += _dot_tn(a_s[slot], dcat)

            def step(j, slot, diag):
                if not diag:
                    scores(j + 1, 1 - slot)
                grads(slot, diag)
                apply(1 - slot, jnp.maximum(j - 1, 0))

            dz_s[1] = jnp.zeros_like(dz_s[1])
            a_s[1] = jnp.zeros_like(a_s[1])
            scores(0, 0)
            dq_acc[...] = jnp.zeros_like(dq_acc)
            cn_s[...] = -jnp.concatenate([totf[:, 0:1], totf[:, 1:2]], axis=0)
            cw_s[...] = jnp.zeros_like(cw_s)

            def pair(i, _):
                step(2 * i, 0, False)
                step(2 * i + 1, 1, False)
                return 0

            lax.fori_loop(0, qi // 2, pair, 0)

            @pl.when(qi % 2 == 1)
            def _():
                step(qi - 1, 0, False)
                step(qi, 1, True)

            @pl.when(qi % 2 == 0)
            def _():
                step(qi, 0, True)

            apply(qi & 1, qi)
            dq_ref[qs_, :] = _bf(dq_acc[...] * scale)
            return 0

        lax.fori_loop(0, nq, qblock, 0)
        dk_ref[...] = _bf(dk_acc[...])
        dv_ref[...] = _bf(dv_acc[...])

    t = nb * s
    hp = lambda b, p: (b, p)
    out = jax.ShapeDtypeStruct((t, SB_WIDTH), BF16)
    return pl.pallas_call(
        body, grid=(nb, 8),
        in_specs=[pl.BlockSpec((s, LANES), lambda b, p: (b, CB_Q + p)),
                  pl.BlockSpec((s, LANES), lambda b, p: (b, CB_K + p)),
                  pl.BlockSpec((s, LANES), lambda b, p: (b, CB_V + p)),
                  pl.BlockSpec((s, LANES), hp),
                  pl.BlockSpec((None, None, s, 2), lambda b, p: (b, p, 0, 0))],
        out_specs=[pl.BlockSpec((s, LANES), hp)] * 3,
        out_shape=[out, out, out],
        scratch_shapes=[pltpu.VMEM((s, LANES), BF16)] * 8 + [
            pltpu.VMEM((2, 2 * blk, blk), F32), pltpu.VMEM((2, 2 * blk, blk), F32),
            pltpu.VMEM((2, 2 * blk, blk), BF16), pltpu.VMEM((2, 2 * blk, blk), BF16),
            pltpu.VMEM((s, LANES), F32), pltpu.VMEM((s, LANES), F32),
            pltpu.VMEM((blk, LANES), F32), pltpu.VMEM((2 * blk, 1), F32), pltpu.VMEM((2 * blk, 1), F32)],
        compiler_params=_params(("parallel", "parallel")), name="attn_bwd")(proj, proj, proj, do, tot)


CONV_TC = 256
CONV_ROWS = 64
HALO = 8


def _conv_rows(ws, w_ref, b_ref, n):
    taps = [ws[HALO - kk:HALO - kk + n, :] for kk in range(CONV_K)]
    y = b_ref[...] + w_ref[CONV_K - 1:CONV_K, :] * taps[0]
    for kk in range(1, CONV_K):
        y = y + w_ref[CONV_K - 1 - kk:CONV_K - kk, :] * taps[kk]
    return y, taps


def _conv_fwd(proj, conv_w, conv_b, nb, s):
    tc, rows = CONV_TC, min(CONV_ROWS, s)
    cb0 = CB_XBC * LANES // tc

    def body(x_ref, w_ref, b_ref, o_ref, xp, ws):
        xp[0:HALO, :] = jnp.zeros((HALO, tc), F32)
        xp[HALO:HALO + s, :] = x_ref[...].astype(F32)

        def chunk(i, _):
            r0 = pl.multiple_of(i * rows, rows)
            ws[...] = xp[pl.ds(r0, rows + HALO), :]
            y, _ = _conv_rows(ws, w_ref, b_ref, rows)
            o_ref[pl.ds(r0, rows), :] = y * _sigmoid(y)
            return 0

        lax.fori_loop(0, s // rows, chunk, 0)

    t = nb * s
    return pl.pallas_call(
        body, grid=(nb, CONV_DIM // tc),
        in_specs=[pl.BlockSpec((s, tc), lambda b, c: (b, cb0 + c)),
                  pl.BlockSpec((CONV_K, tc), lambda b, c: (0, c)),
                  pl.BlockSpec((1, tc), lambda b, c: (0, c))],
        out_specs=pl.BlockSpec((s, tc), lambda b, c: (b, c)),
        out_shape=jax.ShapeDtypeStruct((t, CONV_DIM), F32),
        scratch_shapes=[pltpu.VMEM((s + HALO, tc), F32), pltpu.VMEM((rows + HALO, tc), F32)],
        compiler_params=_params(("parallel", "parallel")), name="conv_fwd")(proj, conv_w, conv_b)


def _conv_bwd(proj, dact, conv_w, conv_b, nb, s, *, col0, name):
    width = dact.shape[1]
    tc, rows = min(CONV_TC, width), min(CONV_ROWS, s)
    cb0 = (CB_XBC * LANES + col0) // tc
    wb0 = col0 // tc

    def body(x_ref, d_ref, w_ref, b_ref, dx_ref, dw_ref, db_ref, xp, dp, dys, ws):
        @pl.when(pl.program_id(1) == 0)
        def _():
            dw_ref[...] = jnp.zeros_like(dw_ref)
            db_ref[...] = jnp.zeros_like(db_ref)

        n = rows + HALO
        pad = jnp.zeros((HALO, tc), F32)
        xp[0:HALO, :] = pad
        xp[HALO:HALO + s, :] = x_ref[...].astype(F32)
        xp[HALO + s:HALO + s + HALO, :] = pad
        dp[0:s, :] = d_ref[...]
        dp[s:s + HALO, :] = pad

        def chunk(i, acc):
            r0 = pl.multiple_of(i * rows, rows)
            ws[...] = xp[pl.ds(r0, n + HALO), :]
            y, taps = _conv_rows(ws, w_ref, b_ref, n)
            sg = _sigmoid(y)
            dy = dp[pl.ds(r0, n), :] * (sg * (1.0 + y * (1.0 - sg)))
            dys[...] = dy
            dx = w_ref[CONV_K - 1:CONV_K, :] * dy[0:rows]
            for kk in range(1, CONV_K):
                dx = dx + w_ref[CONV_K - 1 - kk:CONV_K - kk, :] * dys[kk:kk + rows, :]
            dx_ref[pl.ds(r0, rows), :] = _bf(dx)
            dyc = dy[0:rows]
            new = [acc[0] + jnp.sum(dyc, axis=0, keepdims=True)]
            for kk in range(CONV_K):
                new.append(acc[1 + kk] + jnp.sum(dyc * taps[kk][0:rows], axis=0, keepdims=True))
            return tuple(new)

        acc = lax.fori_loop(0, s // rows, chunk, (jnp.zeros((1, tc), F32),) * (1 + CONV_K))
        db_ref[...] += acc[0]
        for kk in range(CONV_K):
            dw_ref[CONV_K - 1 - kk:CONV_K - kk, :] += acc[1 + kk]

    t = nb * s
    return pl.pallas_call(
        body, grid=(width // tc, nb),
        in_specs=[pl.BlockSpec((s, tc), lambda c, b: (b, cb0 + c)),
                  pl.BlockSpec((s, tc), lambda c, b: (b, c)),
                  pl.BlockSpec((CONV_K, tc), lambda c, b: (0, wb0 + c)),
                  pl.BlockSpec((1, tc), lambda c, b: (0, wb0 + c))],
        out_specs=[pl.BlockSpec((s, tc), lambda c, b: (b, c)),
                   pl.BlockSpec((CONV_K, tc), lambda c, b: (0, c)),
                   pl.BlockSpec((1, tc), lambda c, b: (0, c))],
        out_shape=[jax.ShapeDtypeStruct((t, width), BF16),
                   jax.ShapeDtypeStruct((CONV_K, width), F32),
                   jax.ShapeDtypeStruct((1, width), F32)],
        scratch_shapes=[pltpu.VMEM((s + 2 * HALO, tc), F32), pltpu.VMEM((s + HALO, tc), F32),
                        pltpu.VMEM((rows + HALO, tc), F32), pltpu.VMEM((rows + 2 * HALO, tc), F32)],
        compiler_params=_params(("parallel", "arbitrary")), name=name)(proj, dact, conv_w, conv_b)


L = SSD_CHUNK
N_PAIR = SSD_HPG // 2


def _pair_lanes(v0, v1, lane):
    return jnp.where(lane < HEAD_DIM, v0, v1)


def _onehot(shape, row_shift, col_shift):
    row = lax.broadcasted_iota(jnp.int32, shape, 0)
    col = lax.broadcasted_iota(jnp.int32, shape, 1)
    return _bf(lax.shift_right_logical(row, row_shift) == lax.shift_right_logical(col, col_shift))


def _ssd_tables():
    sp = _onehot((LANES, N_PAIR * LANES), 0, 6)
    sf = _onehot((LANES, SSD_HPG * LANES), 0, 7)
    return (jnp.concatenate([sp] * 3, axis=0), jnp.concatenate([sf] * 3, axis=0),
            _onehot((N_PAIR * LANES, LANES), 6, 0), _onehot((SSD_HPG * LANES, LANES), 7, 0))


def _split3(a):
    a1 = _bf(a)
    r1 = a - a1.astype(F32)
    a2 = _bf(r1)
    return jnp.concatenate([a1, a2, _bf(r1 - a2.astype(F32))], axis=1)


def _split2(a):
    hi, lo = _trunc_split(a)
    return jnp.concatenate([hi, lo], axis=1)


def _ssd_chunk_common(dtr_ref, bias_ref, alog_ref, r0, spread_pair, spread_full):
    rr = lax.broadcasted_iota(jnp.int32, (L, L), 0)
    cc = lax.broadcasted_iota(jnp.int32, (L, L), 1)
    tri = (cc <= rr).astype(F32)
    raw = dtr_ref[pl.ds(r0, L), :] + bias_ref[...]
    dt = _softplus(raw)
    a_neg = -jnp.exp(alog_ref[...])
    acs = _dot_hi(tri, dt * a_neg)
    acs3 = _split3(acs)
    both = _dot(jnp.concatenate([_split3(dt), acs3], axis=0), spread_pair)
    return raw, dt, a_neg, acs, acs.T, rr, cc, both[:L], both[L:], _dot(acs3, spread_full)


def _ssd_fwd(xbc, dtr, bias, alog, dskip, nb, s):
    nc = s // L
    t = nb * s

    def body(x_ref, b_ref, c_ref, dtr_ref, bias_ref, alog_ref, dsk_ref, y_ref, hs_ref, h_scr):
        lane = lax.broadcasted_iota(jnp.int32, (L, LANES), 1)
        lane1 = lax.broadcasted_iota(jnp.int32, (1, LANES), 1)
        rowc = lax.broadcasted_iota(jnp.int32, (L, 1), 0)
        h_scr[...] = jnp.zeros_like(h_scr)
        dsk = dsk_ref[...]
        spread_pair, spread_full, _, _ = _ssd_tables()

        def chunk(c, _):
            r0 = pl.multiple_of(c * L, L)
            raw, dt, a_neg, acs, acs_t, rr, cc, dt_all, acs_all, acs_full = _ssd_chunk_common(
                dtr_ref, bias_ref, alog_ref, r0, spread_pair, spread_full)
            causal = rr >= cc
            bb = _bf(b_ref[pl.ds(r0, L), :])
            cb = _bf(c_ref[pl.ds(r0, L), :])
            gm = _dot_nt(cb, bb)
            for pr in range(N_PAIR):
                j0, j1 = 2 * pr, 2 * pr + 1
                ms = []
                for j in (j0, j1):
                    seg = acs_full[:, j * LANES:(j + 1) * LANES] - acs_t[j:j + 1, :]
                    ms.append(_bf(gm * jnp.exp(jnp.where(causal, seg, -1e30))))
                x = x_ref[pl.ds(r0, L), pr * LANES:(pr + 1) * LANES]
                dt_p = dt_all[:, pr * LANES:(pr + 1) * LANES]
                acs_p = acs_all[:, pr * LANES:(pr + 1) * LANES]
                last_p = acs_p[L - 1:L, :]
                d_p = _pair_lanes(dsk[:, j0:j0 + 1], dsk[:, j1:j1 + 1], lane1)
                xd = x * dt_p
                xdb = _bf(xd)
                yds = _dot(jnp.concatenate(ms, axis=0), xdb)
                yd = _pair_lanes(yds[:L], yds[L:], lane)
                hp = h_scr[pr]
                yo = _dot_nt(cb, _bf(hp)) * jnp.exp(acs_p)
                y_ref[pl.ds(r0, L), pr * LANES:(pr + 1) * LANES] = yd + yo + x * d_p
                hs_ref[c, pr] = hp
                sp = _dot_tn(_bf(xd * jnp.exp(last_p - acs_p)), bb)
                cd = jnp.where(rowc < HEAD_DIM, jnp.exp(acs[L - 1:L, j0:j0 + 1]),
                               jnp.exp(acs[L - 1:L, j1:j1 + 1]))
                h_scr[pr] = hp * cd + sp
            return 0

        lax.fori_loop(0, nc, chunk, 0)

    grp = lambda b, g: (g, 0, 0)
    return pl.pallas_call(
        body, grid=(nb, SSD_GROUPS),
        in_specs=[pl.BlockSpec((s, 4 * LANES), lambda b, g: (b, g)),
                  pl.BlockSpec((s, LANES), lambda b, g: (b, 16 + g)),
                  pl.BlockSpec((s, LANES), lambda b, g: (b, 20 + g)),
                  pl.BlockSpec((None, s, LANES), lambda b, g: (g, b, 0)),
                  pl.BlockSpec((None, 1, LANES), grp), pl.BlockSpec((None, 1, LANES), grp),
                  pl.BlockSpec((None, 1, LANES), grp)],
        out_specs=[pl.BlockSpec((s, 4 * LANES), lambda b, g: (b, g)),
                   pl.BlockSpec((None, None, nc, N_PAIR, L, LANES), lambda b, g: (b, g, 0, 0, 0, 0))],
        out_shape=[jax.ShapeDtypeStruct((t, SSD_WIDTH), F32),
                   jax.ShapeDtypeStruct((nb, SSD_GROUPS, nc, N_PAIR, L, LANES), F32)],
        scratch_shapes=[pltpu.VMEM((N_PAIR, L, LANES), F32)],
        compiler_params=_params(("parallel", "parallel")), name="ssd_fwd")(
            xbc, xbc, xbc, dtr, bias, alog, dskip)


def _ssd_bwd(xbc, dtr, bias, alog, dskip, hs, dy, nb, s):
    nc = s // L
    t = nb * s

    def body(x_ref, b_ref, c_ref, dtr_ref, bias_ref, alog_ref, dsk_ref, hs_ref, dy_ref,
             dx_ref, db_ref, dc_ref, ddtr_ref, dbias_ref, dalog_ref, ddsk_ref, dh_scr):
        lane = lax.broadcasted_iota(jnp.int32, (L, LANES), 1)
        lane1 = lax.broadcasted_iota(jnp.int32, (1, LANES), 1)
        rowc = lax.broadcasted_iota(jnp.int32, (L, 1), 0)
        dh_scr[...] = jnp.zeros_like(dh_scr)
        dsk = dsk_ref[...]

        @pl.when(pl.program_id(1) == 0)
        def _():
            dbias_ref[...] = jnp.zeros_like(dbias_ref)
            dalog_ref[...] = jnp.zeros_like(dalog_ref)
            ddsk_ref[...] = jnp.zeros_like(ddsk_ref)

        spread_pair, spread_full, sum_pair, sum_full = _ssd_tables()
        sum_pair2 = jnp.concatenate([sum_pair] * 2, axis=0)
        sum_full2 = jnp.concatenate([sum_full] * 2, axis=0)

        def chunk(i, _):
            c = nc - 1 - i
            r0 = pl.multiple_of(c * L, L)
            raw, dt, a_neg, acs, acs_t, rr, cc, dt_all, acs_all, acs_full = _ssd_chunk_common(
                dtr_ref, bias_ref, alog_ref, r0, spread_pair, spread_full)
            causal = rr >= cc
            tri_t = (cc >= rr).astype(F32)
            bb = _bf(b_ref[pl.ds(r0, L), :])
            cb = _bf(c_ref[pl.ds(r0, L), :])
            gm = _dot_nt(cb, bb)
            dg = jnp.zeros((L, L), F32)
            dbacc = jnp.zeros((L, LANES), F32)
            dcacc = jnp.zeros((L, LANES), F32)
            dacs_t = jnp.zeros((L, L), F32)
            ta_all, te_all, qm_all, tc_all, td_all, dcd_all = [], [], [], [], [], []
            for pr in range(N_PAIR):
                js = (2 * pr, 2 * pr + 1)
                lms, mfs = [], []
                for j in js:
                    seg = acs_full[:, j * LANES:(j + 1) * LANES] - acs_t[j:j + 1, :]
                    lm = jnp.exp(jnp.where(causal, seg, -1e30))
                    lms.append(lm)
                    mfs.append(gm * lm)
                x = x_ref[pl.ds(r0, L), pr * LANES:(pr + 1) * LANES]
                dyv = dy_ref[pl.ds(r0, L), pr * LANES:(pr + 1) * LANES]
                dt_p = dt_all[:, pr * LANES:(pr + 1) * LANES]
                acs_p = acs_all[:, pr * LANES:(pr + 1) * LANES]
                last_p = acs_p[L - 1:L, :]
                d_p = _pair_lanes(dsk[:, js[0]:js[0] + 1], dsk[:, js[1]:js[1] + 1], lane1)
                e_p = jnp.exp(acs_p)
                dte_p = jnp.exp(last_p - acs_p)
                xd = x * dt_p
                xdb = _bf(xd)
                hp = hs_ref[c, pr]
                hb = _bf(hp)
                dye = _bf(dyv * e_p)
                yo = _dot_nt(cb, hb) * e_p
                dcacc = dcacc + _dot(dye, hb)
                dh_out = _dot_tn(dye, cb)
                dyb = _bf(dyv)
                dms = _dot_nt(_bf(jnp.concatenate(
                    [jnp.where(lane < HEAD_DIM, dyv, 0.0), jnp.where(lane < HEAD_DIM, 0.0, dyv)], axis=0)), xdb)
                mty = _dot_tn(_bf(jnp.concatenate(mfs, axis=1)), dyb)
                dxd = jnp.where(lane < HEAD_DIM, mty[:L], mty[L:])
                for hh in range(2):
                    dm = dms[hh * L:(hh + 1) * L]
                    dg = dg + dm * lms[hh]
                    qm = dm * mfs[hh]
                    qm_all.append(qm)
                    dacs_t = dacs_t - jnp.where(rr == js[hh], jnp.sum(qm, axis=0, keepdims=True), 0.0)
                ds = dh_scr[pr]
                dsb = _bf(ds)
                xdd = xd * dte_p
                dxdd = _dot_nt(bb, dsb)
                dbacc = dbacc + _dot(_bf(xdd), dsb)
                dxd = dxd + dxdd * dte_p
                cd = jnp.where(rowc < HEAD_DIM, jnp.exp(acs[L - 1:L, js[0]:js[0] + 1]),
                               jnp.exp(acs[L - 1:L, js[1]:js[1] + 1]))
                prod = ds * hp
                dcd_all += [jnp.sum(prod[:HEAD_DIM], axis=0, keepdims=True),
                            jnp.sum(prod[HEAD_DIM:], axis=0, keepdims=True)]
                dh_scr[pr] = dh_out + ds * cd
                tcv = dxdd * xdd
                ta_all.append(dyv * yo - tcv)
                tc_all.append(jnp.sum(tcv, axis=0, keepdims=True))
                te_all.append(dxd * x)
                td_all.append(jnp.sum(dyv * x, axis=0, keepdims=True))
                dx_ref[pl.ds(r0, L), pr * LANES:(pr + 1) * LANES] = dxd * dt_p + dyv * d_p
            dgb = _bf(dg)
            dc_ref[pl.ds(r0, L), :] = dcacc + _dot(dgb, bb)
            db_ref[pl.ds(r0, L), :] = dbacc + _dot_tn(dgb, cb)
            last_add = jnp.zeros((1, LANES), F32)
            dd_add = jnp.zeros((1, LANES), F32)
            cd_row = jnp.exp(acs[L - 1:L, :])
            for j in range(SSD_HPG):
                hm = (lane1 >= HEAD_DIM) if j % 2 else (lane1 < HEAD_DIM)
                tcj = jnp.sum(jnp.where(hm, tc_all[j // 2], 0.0), axis=1, keepdims=True)
                tdj = jnp.sum(jnp.where(hm, td_all[j // 2], 0.0), axis=1, keepdims=True)
                dcdj = jnp.sum(dcd_all[j], axis=1, keepdims=True)
                last_add = last_add + jnp.where(lane1 == j, tcj + dcdj * cd_row, 0.0)
                dd_add = dd_add + jnp.where(lane1 == j, tdj, 0.0)
            dacs = (_dot(_split2(jnp.concatenate(ta_all, axis=1)), sum_pair2)
                    + _dot(_split2(jnp.concatenate(qm_all, axis=1)), sum_full2)
                    + dacs_t.T + jnp.where(rowc == L - 1, last_add, 0.0))
            ddta = _dot_hi(tri_t, dacs)
            ddt = _dot(_bf(jnp.concatenate(te_all, axis=1)), sum_pair) + ddta * a_neg
            ddraw = ddt * _sigmoid(raw)
            ddtr_ref[pl.ds(r0, L), :] = ddraw
            dbias_ref[...] += jnp.sum(ddraw, axis=0, keepdims=True)
            dalog_ref[...] += jnp.sum(ddta * dt, axis=0, keepdims=True) * a_neg
            ddsk_ref[...] += dd_add
            return 0

        lax.fori_loop(0, nc, chunk, 0)

    grp = lambda g, b: (g, 0, 0)
    small = jax.ShapeDtypeStruct((SSD_GROUPS, 1, LANES), F32)
    return pl.pallas_call(
        body, grid=(SSD_GROUPS, nb),
        in_specs=[pl.BlockSpec((s, 4 * LANES), lambda g, b: (b, g)),
                  pl.BlockSpec((s, LANES), lambda g, b: (b, 16 + g)),
                  pl.BlockSpec((s, LANES), lambda g, b: (b, 20 + g)),
                  pl.BlockSpec((None, s, LANES), lambda g, b: (g, b, 0)),
                  pl.BlockSpec((None, 1, LANES), grp), pl.BlockSpec((None, 1, LANES), grp),
                  pl.BlockSpec((None, 1, LANES), grp),
                  pl.BlockSpec((None, None, nc, N_PAIR, L, LANES), lambda g, b: (b, g, 0, 0, 0, 0)),
                  pl.BlockSpec((s, 4 * LANES), lambda g, b: (b, g))],
        out_specs=[pl.BlockSpec((s, 4 * LANES), lambda g, b: (b, g)),
                   pl.BlockSpec((s, LANES), lambda g, b: (b, g)),
                   pl.BlockSpec((s, LANES), lambda g, b: (b, g)),
                   pl.BlockSpec((None, s, LANES), lambda g, b: (g, b, 0)),
                   pl.BlockSpec((None, 1, LANES), grp), pl.BlockSpec((None, 1, LANES), grp),
                   pl.BlockSpec((None, 1, LANES), grp)],
        out_shape=[jax.ShapeDtypeStruct((t, SSD_WIDTH), F32),
                   jax.ShapeDtypeStruct((t, SSD_GROUPS * SSD_STATE), F32),
                   jax.ShapeDtypeStruct((t, SSD_GROUPS * SSD_STATE), F32),
                   jax.ShapeDtypeStruct((SSD_GROUPS, t, LANES), F32), small, small, small],
        scratch_shapes=[pltpu.VMEM((N_PAIR, L, LANES), F32)],
        compiler_params=_params(("parallel", "arbitrary")), name="ssd_bwd")(
            xbc, xbc, xbc, dtr, bias, alog, dskip, hs, dy)


GN = SSD_WIDTH // SSD_GROUPS


def _post_fwd(o, proj, y, nw, *, tm=256):
    t = o.shape[0]

    def body(o_ref, za_ref, y_ref, zs_ref, nw_ref, ya_ref, ys_ref):
        za = za_ref[...].astype(F32)
        ya_ref[...] = _bf(o_ref[...] * (za * _sigmoid(za)))
        zs = zs_ref[...].astype(F32)
        yz = y_ref[...] * (zs * _sigmoid(zs))
        for g in range(SSD_GROUPS):
            sl = slice(g * GN, (g + 1) * GN)
            v = yz[:, sl]
            r = lax.rsqrt(jnp.mean(v * v, axis=-1, keepdims=True) + EPS)
            ys_ref[:, sl] = _bf(v * r * nw_ref[:, sl])

    return pl.pallas_call(
        body, grid=(t // tm,),
        in_specs=[pl.BlockSpec((tm, SB_WIDTH), lambda i: (i, 0)),
                  pl.BlockSpec((tm, SB_WIDTH), lambda i: (i, CB_ZA * LANES // SB_WIDTH)),
                  pl.BlockSpec((tm, SSD_WIDTH), lambda i: (i, 0)),
                  pl.BlockSpec((tm, SSD_WIDTH), lambda i: (i, CB_ZS * LANES // SSD_WIDTH)),
                  pl.BlockSpec((1, SSD_WIDTH), lambda i: (0, 0))],
        out_specs=[pl.BlockSpec((tm, SB_WIDTH), lambda i: (i, 0)),
                   pl.BlockSpec((tm, SSD_WIDTH), lambda i: (i, 0))],
        out_shape=[jax.ShapeDtypeStruct((t, SB_WIDTH), BF16), jax.ShapeDtypeStruct((t, SSD_WIDTH), BF16)],
        compiler_params=_params(("parallel",)), name="post_fwd")(o, proj, y, proj, nw)


def _post_bwd(dya, o, proj, dys, y, nw, *, tm=256):
    t = o.shape[0]

    def body(dya_ref, o_ref, za_ref, dys_ref, y_ref, zs_ref, nw_ref,
             do_ref, dza_ref, dy_ref, dzs_ref, dnw_ref):
        @pl.when(pl.program_id(0) == 0)
        def _():
            dnw_ref[...] = jnp.zeros_like(dnw_ref)

        za = za_ref[...].astype(F32)
        sa = _sigmoid(za)
        d = dya_ref[...]
        do_ref[...] = d * (za * sa)
        dza_ref[...] = _bf(d * o_ref[...] * (sa * (1.0 + za * (1.0 - sa))))
        zs = zs_ref[...].astype(F32)
        ss = _sigmoid(zs)
        silu = zs * ss
        yv = y_ref[...]
        yz = yv * silu
        dv = dys_ref[...]
        for g in range(SSD_GROUPS):
            sl = slice(g * GN, (g + 1) * GN)
            v = yz[:, sl]
            r = lax.rsqrt(jnp.mean(v * v, axis=-1, keepdims=True) + EPS)
            nrm = v * r
            dg = dv[:, sl]
            dnw_ref[:, sl] += jnp.sum(dg * nrm, axis=0, keepdims=True)
            dn = dg * nw_ref[:, sl]
            dyz = r * (dn - nrm * jnp.mean(dn * nrm, axis=-1, keepdims=True))
            dy_ref[:, sl] = dyz * silu[:, sl]
            dzs_ref[:, sl] = _bf(dyz * yv[:, sl] * (ss[:, sl] * (1.0 + zs[:, sl] * (1.0 - ss[:, sl]))))

    a_spec = pl.BlockSpec((tm, SB_WIDTH), lambda i: (i, 0))
    s_spec = pl.BlockSpec((tm, SSD_WIDTH), lambda i: (i, 0))
    w_spec = pl.BlockSpec((1, SSD_WIDTH), lambda i: (0, 0))
    return pl.pallas_call(
        body, grid=(t // tm,),
        in_specs=[a_spec, a_spec, pl.BlockSpec((tm, SB_WIDTH), lambda i: (i, CB_ZA * LANES // SB_WIDTH)),
                  s_spec, s_spec, pl.BlockSpec((tm, SSD_WIDTH), lambda i: (i, CB_ZS * LANES // SSD_WIDTH)),
                  w_spec],
        out_specs=[a_spec, a_spec, s_spec, s_spec, w_spec],
        out_shape=[jax.ShapeDtypeStruct((t, SB_WIDTH), F32), jax.ShapeDtypeStruct((t, SB_WIDTH), BF16),
                   jax.ShapeDtypeStruct((t, SSD_WIDTH), F32), jax.ShapeDtypeStruct((t, SSD_WIDTH), BF16),
                   jax.ShapeDtypeStruct((1, SSD_WIDTH), F32)],
        compiler_params=_params(("arbitrary",)), name="post_bwd")(dya, o, proj, dys, y, proj, nw)


def _merge_fwd(ya, ys, proj_g, *, tm=512):
    t = ya.shape[0]

    def body(ya_ref, ys_ref, ga_ref, gs_ref, m_ref):
        m_ref[...] = _bf(_sigmoid(ga_ref[...].astype(F32)) * ya_ref[...]
                         + _sigmoid(gs_ref[...].astype(F32)) * ys_ref[...])

    spec = pl.BlockSpec((tm, D_MODEL), lambda i: (i, 0))
    return pl.pallas_call(
        body, grid=(t // tm,),
        in_specs=[spec, spec, pl.BlockSpec((tm, D_MODEL), lambda i: (i, 0)),
                  pl.BlockSpec((tm, D_MODEL), lambda i: (i, 1))],
        out_specs=spec, out_shape=jax.ShapeDtypeStruct((t, D_MODEL), BF16),
        compiler_params=_params(("parallel",)), name="merge_fwd")(ya, ys, proj_g, proj_g)


def _merge_bwd(dm, ya, ys, proj_g, *, tm=512):
    t = ya.shape[0]

    def body(dm_ref, ya_ref, ys_ref, ga_ref, gs_ref, dya_ref, dys_ref, dg_ref):
        d = dm_ref[...]
        ga = _sigmoid(ga_ref[...].astype(F32))
        gs = _sigmoid(gs_ref[...].astype(F32))
        dya_ref[...] = _bf(d * ga)
        dys_ref[...] = _bf(d * gs)
        dg_ref[:, :D_MODEL] = _bf(d * ya_ref[...] * (ga * (1.0 - ga)))
        dg_ref[:, D_MODEL:] = _bf(d * ys_ref[...] * (gs * (1.0 - gs)))

    spec = pl.BlockSpec((tm, D_MODEL), lambda i: (i, 0))
    return pl.pallas_call(
        body, grid=(t // tm,),
        in_specs=[spec, spec, spec, pl.BlockSpec((tm, D_MODEL), lambda i: (i, 0)),
                  pl.BlockSpec((tm, D_MODEL), lambda i: (i, 1))],
        out_specs=[spec, spec, pl.BlockSpec((tm, 2 * D_MODEL), lambda i: (i, 0))],
        out_shape=[jax.ShapeDtypeStruct((t, D_MODEL), BF16), jax.ShapeDtypeStruct((t, D_MODEL), BF16),
                   jax.ShapeDtypeStruct((t, 2 * D_MODEL), BF16)],
        compiler_params=_params(("parallel",)), name="merge_bwd")(dm, ya, ys, proj_g, proj_g)


def _final(x, out, wf, target, *, tm=512):
    t = x.shape[0]

    def body(x_ref, o_ref, w_ref, t_ref, loss_ref, dx_ref, dw_ref):
        @pl.when(pl.program_id(0) == 0)
        def _():
            loss_ref[...] = jnp.zeros_like(loss_ref)
            dw_ref[...] = jnp.zeros_like(dw_ref)

        x2 = x_ref[...] + o_ref[...]
        r = lax.rsqrt(jnp.mean(x2 * x2, axis=-1, keepdims=True) + EPS)
        nrm = x2 * r
        e = nrm * w_ref[...] - t_ref[...]
        row_loss = jnp.mean(e * e, axis=-1, keepdims=True)
        loss_ref[...] += 0.5 * jnp.sum(row_loss, axis=0, keepdims=True)
        dyv = e * (1.0 / D_MODEL)
        dw_ref[...] += jnp.sum(dyv * nrm, axis=0, keepdims=True)
        dn = dyv * w_ref[...]
        dx_ref[...] = r * (dn - nrm * jnp.mean(dn * nrm, axis=-1, keepdims=True))

    spec = pl.BlockSpec((tm, D_MODEL), lambda i: (i, 0))
    fix = pl.BlockSpec((1, D_MODEL), lambda i: (0, 0))
    return pl.pallas_call(
        body, grid=(t // tm,),
        in_specs=[spec, spec, fix, spec],
        out_specs=[pl.BlockSpec((1, LANES), lambda i: (0, 0)), spec, fix],
        out_shape=[jax.ShapeDtypeStruct((1, LANES), F32), jax.ShapeDtypeStruct((t, D_MODEL), F32),
                   jax.ShapeDtypeStruct((1, D_MODEL), F32)],
        compiler_params=_params(("arbitrary",)), name="final_loss")(x, out, wf, target)


def _group_lanes(v):
    return jnp.pad(v.reshape(SSD_GROUPS, 1, SSD_HPG), ((0, 0), (0, 0), (0, LANES - SSD_HPG)))


def _ungroup_lanes(v):
    return v[:, :, :SSD_HPG].reshape(1, N_DT)


def _row_slabs(pieces, rows):
    total = sum(p.shape[0] for p in pieces)
    assert total % rows == 0
    slabs = []
    for lo in range(0, total, rows):
        parts, off = [], 0
        for p in pieces:
            a, b = max(lo, off), min(lo + rows, off + p.shape[0])
            if a < b:
                parts.append(p[a - off:b - off])
            off += p.shape[0]
        slabs.append(parts[0] if len(parts) == 1 else jnp.concatenate(parts, axis=0))
    return jnp.stack(slabs)


def _local_step(x, target, nb, s, w_in_t, conv_w, conv_b, norm_w, dt_bias, a_log, d_skip,
                ssm_norm_w, out_weights, final_norm_w, between=None):
    t = nb * s
    w_g_t = w_in_t[GATE_COL0:]
    w_dt_t = jnp.pad(w_in_t[DT_COL0:GATE_COL0], ((0, LANES - N_DT), (0, 0)))
    h, r1 = _rms_fwd(x, norm_w)
    proj = _matmul(h, w_in_t, nt=True, n=DT_COL0, out_dtype=BF16, name="proj_main")
    proj_g = _matmul(h, w_g_t, nt=True, out_dtype=BF16, name="proj_gate")
    dt_raw = _matmul(h, w_dt_t, nt=True, name="proj_dt")
    dtr = jnp.pad(dt_raw[:, :N_DT].reshape(t, SSD_GROUPS, SSD_HPG).transpose(1, 0, 2),
                  ((0, 0), (0, 0), (0, LANES - SSD_HPG)))
    bias_g, alog_g, dsk_g = _group_lanes(dt_bias), _group_lanes(a_log), _group_lanes(d_skip)

    o, tot = _attn_fwd(proj, nb, s)
    xbc = _conv_fwd(proj, conv_w, conv_b, nb, s)
    y, hs = _ssd_fwd(xbc, dtr, bias_g, alog_g, dsk_g, nb, s)
    ya_in, ys_in = _post_fwd(o, proj, y, ssm_norm_w)
    w_attn_out, w_ssm_out, w_o = out_weights(ys_in) if callable(out_weights) else out_weights
    ya = _matmul(ya_in, w_attn_out, name="attn_out")
    ys = _matmul(ys_in, w_ssm_out, name="ssm_out")
    merged = _merge_fwd(ya, ys, proj_g)
    out = _matmul(merged, w_o, name="out_proj")
    loss, dx2, d_final_w = _final(x, out, final_norm_w, target)

    dx2b = _bf(dx2)
    dmerged = _matmul(dx2b, w_o, nt=True, name="d_merged")
    d_w_o = _matmul(merged, dx2b, ta=True, name="d_w_o")
    dya, dys, dgate = _merge_bwd(dmerged, ya, ys, proj_g)
    d_w_attn_out = _matmul(ya_in, dya, ta=True, name="d_w_attn_out")
    d_w_ssm_out = _matmul(ys_in, dys, ta=True, name="d_w_ssm_out")
    dya_in = _matmul(dya, w_attn_out, nt=True, name="d_ya_in")
    dys_in = _matmul(dys, w_ssm_out, nt=True, name="d_ys_in")
    do, dza, dy, dzs, d_ssm_norm_w = _post_bwd(dya_in, o, proj, dys_in, y, ssm_norm_w)
    dq, dk, dv = _attn_bwd(proj, do, tot, nb, s)
    dxs_act, db_act, dc_act, ddtr, dbias_g, dalog_g, ddsk_g = _ssd_bwd(
        xbc, dtr, bias_g, alog_g, dsk_g, hs, dy, nb, s)
    dxs, dwx, dbx = _conv_bwd(proj, dxs_act, conv_w, conv_b, nb, s, col0=0, name="conv_bwd_x")
    dbm, dwb, dbb = _conv_bwd(proj, db_act, conv_w, conv_b, nb, s, col0=SSD_WIDTH, name="conv_bwd_b")
    dcm, dwc, dbc = _conv_bwd(proj, dc_act, conv_w, conv_b, nb, s, col0=SSD_WIDTH + 512, name="conv_bwd_c")
    dproj = jnp.concatenate([dq, dk, dv, dza, dzs, dxs, dbm, dcm], axis=1)
    ddt = jnp.pad(_bf(ddtr[:, :, :SSD_HPG].transpose(1, 0, 2).reshape(t, N_DT)), ((0, 0), (0, LANES - N_DT)))
    wt = dict(ta=True, out_dtype=BF16)
    d_w_in_t = _row_slabs(
        [_matmul(dproj, h, name="d_w_main", **wt), _matmul(ddt, h, name="d_w_dt", **wt)[:N_DT],
         _matmul(dgate, h, name="d_w_gate", **wt)], SHARD)
    grads = dict(
        w_in_t=d_w_in_t,
        conv_w=jnp.concatenate([dwx, dwb, dwc], axis=1), conv_b=jnp.concatenate([dbx, dbb, dbc], axis=1),
        dt_bias=_ungroup_lanes(dbias_g), a_log=_ungroup_lanes(dalog_g), d_skip=_ungroup_lanes(ddsk_g),
        ssm_norm_w=d_ssm_norm_w, w_attn_out=d_w_attn_out, w_ssm_out=d_w_ssm_out, w_o=d_w_o,
        final_norm_w=d_final_w)
    extra = None
    if between is not None:
        zero, extra = between(grads)
        ddt = ddt + _bf(zero[0:1, 0:1])

    dh = _matmul(ddt, w_dt_t, name="d_h_dt")
    dh = _matmul(dgate, w_g_t, add=dh, name="d_h_gate")
    dh = _matmul(dproj, w_in_t, add=dh, name="d_h", tk=DT_COL0 // 4)
    grad_x, grads["norm_w"] = _rms_bwd(dh, x, r1, norm_w, dx2)
    return loss, grad_x, grads, extra


N_CHIP = N_DEV // 2


def _pair_exchange(scatter, bcast, *, name):
    arrays = list(scatter) + list(bcast)
    ns, n = len(scatter), len(arrays)
    n_sem = ns * N_CHIP + (n - ns) * (N_DEV - 1)

    def body(*refs):
        ins, outs = refs[:n], refs[n:2 * n]
        send_sems, recv_sems, local_sems = refs[2 * n:]
        x, y, c = lax.axis_index("x"), lax.axis_index("y"), lax.axis_index("c")
        me = 4 * x + 2 * y + c
        copies, k = [], 0
        for a in range(ns):
            for q in range(N_CHIP):
                cp = pltpu.make_async_remote_copy(
                    src_ref=ins[a].at[2 * q + 1 - c], dst_ref=outs[a].at[q],
                    send_sem=send_sems.at[k], recv_sem=recv_sems.at[k],
                    device_id=(x, y, 1 - c), device_id_type=pl.DeviceIdType.MESH)
                cp.start()
                copies.append(cp)
                k += 1
        for a in range(ns, n):
            cp = pltpu.make_async_copy(ins[a], outs[a].at[me], local_sems.at[a - ns])
            cp.start()
            copies.append(cp)
            for r in range(1, N_DEV):
                tx = 1 - x if (r >> 2) & 1 else x
                ty = 1 - y if (r >> 1) & 1 else y
                tc = 1 - c if r & 1 else c
                cp = pltpu.make_async_remote_copy(
                    src_ref=ins[a], dst_ref=outs[a].at[me], send_sem=send_sems.at[k], recv_sem=recv_sems.at[k],
                    device_id=(tx, ty, tc), device_id_type=pl.DeviceIdType.MESH)
                cp.start()
                copies.append(cp)
                k += 1
        for cp in copies:
            cp.wait()

    out_shape = [jax.ShapeDtypeStruct((N_CHIP,) + v.shape[1:] if i < ns else (N_DEV,) + v.shape, v.dtype)
                 for i, v in enumerate(arrays)]
    return pl.pallas_call(
        body, in_specs=[pl.BlockSpec(memory_space=pl.ANY)] * n,
        out_specs=[pl.BlockSpec(memory_space=pl.ANY)] * n, out_shape=out_shape,
        scratch_shapes=[pltpu.SemaphoreType.DMA((n_sem,)), pltpu.SemaphoreType.DMA((n_sem,)),
                        pltpu.SemaphoreType.DMA((max(n - ns, 1),))],
        compiler_params=pltpu.CompilerParams(has_side_effects=True), name=name)(*arrays)


def _pair_add(mine, recv, core, *, name, tc=None):
    _, r, c = recv.shape
    tc = c if tc is None else tc
    assert c % tc == 0

    def body(core_ref, m_ref, r_ref, o_ref):
        o_ref[...] = (m_ref[...].astype(F32) + r_ref[...].astype(F32)).astype(o_ref.dtype)

    spec = pl.BlockSpec((None, r, tc), lambda q, i, core_ref: (q, 0, i))
    return pl.pallas_call(
        body, grid_spec=pltpu.PrefetchScalarGridSpec(
            num_scalar_prefetch=1, grid=(N_CHIP, c // tc),
            in_specs=[pl.BlockSpec((None, r, tc), lambda q, i, core_ref: (2 * q + core_ref[0], 0, i)), spec],
            out_specs=spec),
        out_shape=jax.ShapeDtypeStruct(recv.shape, recv.dtype),
        compiler_params=_params(("parallel", "parallel")), name=name)(core, mine, recv)


_HBM = pl.BlockSpec(memory_space=pltpu.HBM)
_SEM = pl.BlockSpec(memory_space=pltpu.SEMAPHORE)
_DATAFLOW = pltpu.SideEffectType.DATAFLOW_SIDE_EFFECTING
_PEERS = {"chips": N_CHIP - 1, "all": N_DEV - 1}


def _split_copies(pattern, srcs, lands, send_sems, recv_sems):
    x, y, c = lax.axis_index("x"), lax.axis_index("y"), lax.axis_index("c")
    per = _PEERS[pattern]
    copies = []
    for a in range(len(srcs)):
        for r in range(1, per + 1):
            if pattern == "chips":
                tx = 1 - x if (r >> 1) & 1 else x
                ty = 1 - y if r & 1 else y
                tc = c
                src, dst = srcs[a].at[2 * tx + ty], lands[a].at[2 * x + y]
            else:
                tx = 1 - x if (r >> 2) & 1 else x
                ty = 1 - y if (r >> 1) & 1 else y
                tc = 1 - c if r & 1 else c
                src, dst = srcs[a], lands[a].at[4 * x + 2 * y + c]
            copies.append(pltpu.make_async_remote_copy(
                src_ref=src, dst_ref=dst, send_sem=send_sems.at[a * per + r - 1],
                recv_sem=recv_sems.at[a * per + r - 1], device_id=(tx, ty, tc), device_id_type=pl.DeviceIdType.MESH))
    return copies


def _split_start(pattern, arrays, *, name):
    n = len(arrays)
    n_sem = n * _PEERS[pattern]

    def body(*refs):
        srcs, lands = refs[:n], refs[n:2 * n]
        send_sems, recv_sems = refs[2 * n], refs[2 * n + 1]
        zero = refs[-1]
        for cp in _split_copies(pattern, srcs, lands, send_sems, recv_sems):
            cp.start()
        zero[...] = jnp.zeros_like(zero)

    src_t = [pltpu.HBM(v.shape, v.dtype) for v in arrays]
    land_t = [pltpu.HBM(v.shape if pattern == "chips" else (N_DEV,) + v.shape, v.dtype) for v in arrays]
    outs = pl.pallas_call(
        body, name=name,
        out_shape=(pltpu.SemaphoreType.DMA((n_sem,)), pltpu.SemaphoreType.DMA((n_sem,)), *src_t, *land_t,
                   jax.ShapeDtypeStruct((8, LANES), F32)),
        in_specs=[_HBM] * (2 * n),
        out_specs=(_SEM, _SEM, *[_HBM] * (2 * n), pl.BlockSpec(memory_space=pltpu.VMEM)),
        input_output_aliases={i: 2 + i for i in range(2 * n)},
        compiler_params=pltpu.CompilerParams(has_side_effects=_DATAFLOW))(
            *[pltpu.with_memory_space_constraint(v, pltpu.HBM) for v in arrays],
            *[pltpu.with_memory_space_constraint(lax.empty(t.shape, t.dtype), pltpu.HBM) for t in land_t])
    return outs[0], outs[1], outs[2:2 + n], outs[2 + n:2 + 2 * n], outs[-1]


def _split_wait(pattern, send_sems, recv_sems, srcs, lands, after, *, name):
    n = len(srcs)

    def body(*refs):
        for cp in _split_copies(pattern, refs[:n], refs[n:2 * n], refs[2 * n], refs[2 * n + 1]):
            cp.wait_send()
            cp.wait_recv()

    outs = pl.pallas_call(
        body, name=name, out_shape=[pltpu.HBM(v.shape, v.dtype) for v in (*srcs, *lands)],
        in_specs=[_HBM] * (2 * n) + [_SEM, _SEM, pl.BlockSpec(memory_space=pl.ANY)],
        out_specs=[_HBM] * (2 * n), input_output_aliases={i: i for i in range(2 * n)},
        compiler_params=pltpu.CompilerParams(has_side_effects=_DATAFLOW))(
            *srcs, *lands, send_sems, recv_sems, after)
    return outs[:n], outs[n:]


def _own_slab(lands, srcs, index, whole):
    out = []
    for land, src in zip(lands, srcs):
        own = src[None] if whole else lax.dynamic_slice(src, (index, 0, 0), (1,) + src.shape[1:])
        out.append(lax.dynamic_update_slice(land, own, (index, 0, 0)))
    return out


def _gather(arrays, *, name):
    n = len(arrays)
    per = N_DEV - 1

    def body(*refs):
        ins, outs = refs[:n], refs[n:2 * n]
        send_sems, recv_sems, local_sems = refs[2 * n:]
        x, y, c = lax.axis_index("x"), lax.axis_index("y"), lax.axis_index("c")
        me, sibling = (x, y, c), (x, y, 1 - c)
        chips = [(1 - x, y), (x, 1 - y), (1 - x, 1 - y)]

        def slab(a, block):
            return outs[a].at[4 * block[0] + 2 * block[1] + block[2]]

        def copy(a, k, block, to, src=None):
            rows = slab(a, block)
            return pltpu.make_async_remote_copy(
                src_ref=rows if src is None else src, dst_ref=rows,
                send_sem=send_sems.at[a * per + k], recv_sem=recv_sems.at[a * per + k],
                device_id=to, device_id_type=pl.DeviceIdType.MESH)

        started = []
        for a in range(n):
            mine = pltpu.make_async_copy(ins[a], slab(a, me), local_sems.at[a])
            mine.start()
            started.append(mine)
        for a in range(n):
            first = [copy(a, 1 + j, me, (*chip, c), src=ins[a]) for j, chip in enumerate(chips)]
            first.append(copy(a, 0, me, sibling, src=ins[a]))
            for cp in first:
                cp.start()
            started += first
        for j, chip in enumerate(chips):
            for a in range(n):
                copy(a, 1 + j, (*chip, c), me).wait_recv()
                passed = copy(a, 4 + j, (*chip, c), sibling)
                passed.start()
                started.append(passed)
        for a in range(n):
            copy(a, 0, sibling, me).wait_recv()
            for j, chip in enumerate(chips):
                copy(a, 4 + j, (*chip, 1 - c), me).wait_recv()
        for cp in started[:n]:
            cp.wait()
        for cp in started[n:]:
            cp.wait_send()

    return pl.pallas_call(
        body, in_specs=[pl.BlockSpec(memory_space=pl.ANY)] * n,
        out_specs=[pl.BlockSpec(memory_space=pl.ANY)] * n,
        out_shape=[jax.ShapeDtypeStruct((N_DEV,) + v.shape, v.dtype) for v in arrays],
        scratch_shapes=[pltpu.SemaphoreType.DMA((n * per,)), pltpu.SemaphoreType.DMA((n * per,)),
                        pltpu.SemaphoreType.DMA((n,))],
        compiler_params=pltpu.CompilerParams(has_side_effects=True), name=name)(*arrays)


def _adamw(parts, w, m, v, *, name, tc=None):
    r, c = w.shape
    n_parts = parts.shape[0]
    tc = c if tc is None else tc
    assert c % tc == 0

    def body(p_ref, w_ref, m_ref, v_ref, g_ref, d_ref, nm_ref, nv_ref):
        g = p_ref[0].astype(F32)
        for d in range(1, n_parts):
            g = g + p_ref[d].astype(F32)
        mm = ADAM_B1 * m_ref[...] + (1.0 - ADAM_B1) * g
        vv = ADAM_B2 * v_ref[...] + (1.0 - ADAM_B2) * (g * g)
        m_hat = mm / (1.0 - ADAM_B1 ** ADAM_STEP)
        v_hat = vv / (1.0 - ADAM_B2 ** ADAM_STEP)
        g_ref[...] = g
        d_ref[...] = -ADAM_LR * (m_hat / (jnp.sqrt(v_hat) + ADAM_EPS) + ADAM_WD * w_ref[...])
        nm_ref[...] = mm
        nv_ref[...] = vv

    spec = pl.BlockSpec((r, tc), lambda i: (0, i))
    out = jax.ShapeDtypeStruct((r, c), F32)
    return pl.pallas_call(
        body, grid=(c // tc,),
        in_specs=[pl.BlockSpec((n_parts, r, tc), lambda i: (0, 0, i)), spec, spec, spec],
        out_specs=[spec] * 4, out_shape=[out] * 4,
        compiler_params=_params(("parallel",)), name=name)(parts, w, m, v)


SMALL = (("norm_w", D_MODEL), ("conv_b", CONV_DIM), ("dt_bias", N_DT), ("a_log", N_DT),
         ("d_skip", N_DT), ("ssm_norm_w", SSD_WIDTH), ("final_norm_w", D_MODEL))
SMALL_ROWS = 64
SHARD = D_PROJ // N_DEV


def _pack_small(vals):
    flat = jnp.concatenate([vals[k].reshape(-1).astype(F32) for k, _ in SMALL])
    return jnp.pad(flat, (0, SMALL_ROWS * LANES - flat.shape[0])).reshape(SMALL_ROWS, LANES)


def _unpack_small(packed, shapes):
    flat = packed.reshape(-1)
    out, off = {}, 0
    for k, size in SMALL:
        out[k] = flat[off:off + size].reshape(shapes[k])
        off += size
    return out


def kernel(x, norm_w, w_in, conv_w, conv_b, dt_bias, a_log, d_skip, ssm_norm_w, w_attn_out, w_ssm_out, w_o, final_norm_w, loss_target, m_norm_w, m_w_in, m_conv_w, m_conv_b, m_dt_bias, m_a_log, m_d_skip, m_ssm_norm_w, m_w_attn_out, m_w_ssm_out, m_w_o, m_final_norm_w, v_norm_w, v_w_in, v_conv_w, v_conv_b, v_dt_bias, v_a_log, v_d_skip, v_ssm_norm_w, v_w_attn_out, v_w_ssm_out, v_w_o, v_final_norm_w):
    nb, s, _ = x.shape
    t = nb * s
    weights = dict(norm_w=norm_w, w_in=w_in, conv_w=conv_w, conv_b=conv_b, dt_bias=dt_bias, a_log=a_log,
                   d_skip=d_skip, ssm_norm_w=ssm_norm_w, w_attn_out=w_attn_out, w_ssm_out=w_ssm_out,
                   w_o=w_o, final_norm_w=final_norm_w)
    moms = dict(norm_w=m_norm_w, w_in=m_w_in, conv_w=m_conv_w, conv_b=m_conv_b, dt_bias=m_dt_bias,
                a_log=m_a_log, d_skip=m_d_skip, ssm_norm_w=m_ssm_norm_w, w_attn_out=m_w_attn_out,
                w_ssm_out=m_w_ssm_out, w_o=m_w_o, final_norm_w=m_final_norm_w)
    vels = dict(norm_w=v_norm_w, w_in=v_w_in, conv_w=v_conv_w, conv_b=v_conv_b, dt_bias=v_dt_bias,
                a_log=v_a_log, d_skip=v_d_skip, ssm_norm_w=v_ssm_norm_w, w_attn_out=v_w_attn_out,
                w_ssm_out=v_w_ssm_out, w_o=v_w_o, final_norm_w=v_final_norm_w)

    tr_ = lambda a: jnp.transpose(a[0])
    g_in, g_conv = _gather([_bf(tr_(w_in)), conv_w[0]], name="gather_weights")
    conv_full = g_conv.transpose(1, 0, 2).reshape(CONV_K, CONV_DIM)
    o_send, o_recv, o_srcs, o_lands, o_zero = _split_start(
        "all", [_bf(w_attn_out[0]), _bf(w_ssm_out[0]), _bf(w_o[0])], name="gather_out_weights_start")

    core = lax.axis_index("c").astype(jnp.int32).reshape(1)
    chip = 2 * lax.axis_index("x") + lax.axis_index("y")
    me = 2 * chip + lax.axis_index("c")

    def out_weights(after):
        srcs, lands = _split_wait("all", o_send, o_recv, o_srcs, o_lands, after, name="gather_out_weights_wait")
        g_wa, g_ws, g_wo = _own_slab(lands, srcs, me, True)
        return (g_wa.reshape(SB_WIDTH, D_MODEL), g_ws.reshape(SSD_WIDTH, D_MODEL), g_wo.reshape(D_MODEL, D_MODEL))

    def send_weight_grads(grads):
        slabs = [grads["w_in_t"],
                 grads["conv_w"].reshape(CONV_K, N_DEV, CONV_DIM // N_DEV).transpose(1, 0, 2),
                 _bf(grads["w_attn_out"].reshape(N_DEV, SB_WIDTH // N_DEV, D_MODEL)),
                 _bf(grads["w_ssm_out"].reshape(N_DEV, SSD_WIDTH // N_DEV, D_MODEL)),
                 _bf(grads["w_o"].reshape(N_DEV, D_MODEL // N_DEV, D_MODEL))]
        recv = _pair_exchange(slabs, [], name="exchange_pairs")
        names = ("w_in", "conv_w", "w_attn_out", "w_ssm_out", "w_o")
        sums = [_pair_add(a, b, core, name="pair_add_" + k, tc=4 * LANES if k == "w_in" else None)
                for k, a, b in zip(names, slabs, recv)]
        send_sems, recv_sems, srcs, lands, zero = _split_start("chips", sums, name="exchange_chips_start")
        return zero, (send_sems, recv_sems, srcs, lands)

    loss_part, grad_x, grads, (send_sems, recv_sems, srcs, lands) = _local_step(
        x.reshape(t, D_MODEL), loss_target.reshape(t, D_MODEL), nb, s, g_in.reshape(D_PROJ, D_MODEL),
        conv_full, conv_b, norm_w + o_zero[0:1, 0:1], dt_bias, a_log, d_skip, ssm_norm_w, out_weights,
        final_norm_w.reshape(1, D_MODEL), between=send_weight_grads)

    (p_small,) = _pair_exchange([], [_pack_small(grads)], name="exchange_small")
    srcs, lands = _split_wait("chips", send_sems, recv_sems, srcs, lands, grad_x, name="exchange_chips_wait")
    p_in, p_conv, p_wa, p_ws, p_wo = _own_slab(lands, srcs, chip, False)

    res = {}
    res["w_in"] = [a.T for a in _adamw(p_in, tr_(w_in), tr_(m_w_in), tr_(v_w_in), name="adamw_w_in", tc=2 * LANES)]
    res["conv_w"] = _adamw(p_conv, conv_w[0], m_conv_w[0], v_conv_w[0], name="adamw_conv_w")
    res["w_attn_out"] = _adamw(p_wa, w_attn_out[0], m_w_attn_out[0], v_w_attn_out[0], name="adamw_w_attn_out")
    res["w_ssm_out"] = _adamw(p_ws, w_ssm_out[0], m_w_ssm_out[0], v_w_ssm_out[0], name="adamw_w_ssm_out")
    res["w_o"] = _adamw(p_wo, w_o[0], m_w_o[0], v_w_o[0], name="adamw_w_o")
    sm = _adamw(p_small, _pack_small(weights), _pack_small(moms), _pack_small(vels), name="adamw_small")
    shapes = {k: weights[k].shape for k, _ in SMALL}
    sm = [_unpack_small(v, shapes) for v in sm]
    for k, _ in SMALL:
        res[k] = tuple(part[k] for part in sm)

    order = ("norm_w", "w_in", "conv_w", "conv_b", "dt_bias", "a_log", "d_skip", "ssm_norm_w",
             "w_attn_out", "w_ssm_out", "w_o", "final_norm_w")
    loss = lax.psum(loss_part[0, 0], ("x", "y", "c"))
    outs = [loss, grad_x.reshape(nb, s, D_MODEL)]
    for i in range(4):
        outs += [res[k][i].reshape(weights[k].shape) for k in order]
    return tuple(outs)
```

```python
import functools

import jax
import jax.numpy as jnp
from jax import lax
from jax.experimental import pallas as pl
from jax.experimental.pallas import tpu as pltpu

F32 = jnp.float32
BF16 = jnp.bfloat16

D_MODEL = 1024
SB_WIDTH = 1024
HEAD_DIM = 64
SSD_WIDTH = 2048
SSD_GROUPS = 4
SSD_HPG = 8
SSD_STATE = 128
SSD_CHUNK = 128
CONV_K = 4
CONV_DIM = 3072
N_DT = 32
D_PROJ = 11296
DT_COL0 = 9216
GATE_COL0 = DT_COL0 + N_DT
EPS = 1e-6
N_DEV = 8

ADAM_LR = 0.001
ADAM_B1 = 0.9
ADAM_B2 = 0.999
ADAM_EPS = 1e-08
ADAM_WD = 0.01
ADAM_STEP = 10

LANES = 128
VMEM_LIMIT = 56 * 1024 * 1024

CB_Q, CB_K, CB_V, CB_ZA, CB_ZS, CB_XBC = 0, 8, 16, 24, 32, 48

NT_DIMS = (((1,), (1,)), ((), ()))
TN_DIMS = (((0,), (0,)), ((), ()))


def _params(sem):
    return pltpu.CompilerParams(dimension_semantics=sem, vmem_limit_bytes=VMEM_LIMIT)


def _dot(a, b):
    return jnp.dot(a, b, preferred_element_type=F32)


def _dot_nt(a, b):
    return lax.dot_general(a, b, NT_DIMS, preferred_element_type=F32)


def _dot_tn(a, b):
    return lax.dot_general(a, b, TN_DIMS, preferred_element_type=F32)


def _bf(a):
    return a.astype(BF16)


def _sigmoid(x):
    return 0.5 + 0.5 * jnp.tanh(0.5 * x)


def _softplus(x):
    return jnp.maximum(x, 0.0) + jnp.log(1.0 + jnp.exp(-jnp.abs(x)))


def _trunc_split(a):
    bits = lax.bitcast_convert_type(a, jnp.uint32) & jnp.uint32(0xFFFF0000)
    hi = lax.bitcast_convert_type(bits, F32)
    return _bf(hi), _bf(a - hi)


MM_TILE = 1024
MM_TK = 2048


def _matmul(a, b, *, name, nt=False, ta=False, add=None, out_dtype=F32, tm=MM_TILE, tn=MM_TILE, tk=MM_TK, n=None):
    k, m = a.shape if ta else a.shape[::-1]
    n = (b.shape[0] if nt else b.shape[1]) if n is None else n
    tm, tn, tk = min(tm, m), min(tn, n), min(tk, k)
    assert m % tm == 0 and n % tn == 0 and k % tk == 0, (name, a.shape, b.shape)
    assert not (ta and nt)
    nk = k // tk
    has_add = add is not None

    def body(*refs):
        if has_add:
            a_ref, b_ref, add_ref, o_ref, acc_ref = refs
        else:
            a_ref, b_ref, o_ref, acc_ref = refs
        kk = pl.program_id(2)
        av, bv = _bf(a_ref[...]), _bf(b_ref[...])
        d = _dot_nt(av, bv) if nt else _dot_tn(av, bv) if ta else _dot(av, bv)
        if nk == 1:
            o_ref[...] = ((d + add_ref[...]) if has_add else d).astype(out_dtype)
            return

        @pl.when(kk == 0)
        def _():
            acc_ref[...] = jnp.zeros_like(acc_ref)

        acc_ref[...] += d

        @pl.when(kk == nk - 1)
        def _():
            r = acc_ref[...]
            if has_add:
                r = r + add_ref[...]
            o_ref[...] = r.astype(out_dtype)

    in_specs = [pl.BlockSpec((tk, tm), lambda j, i, kk: (kk, i)) if ta
                else pl.BlockSpec((tm, tk), lambda j, i, kk: (i, kk)),
                pl.BlockSpec((tn, tk), lambda j, i, kk: (j, kk)) if nt
                else pl.BlockSpec((tk, tn), lambda j, i, kk: (kk, j))]
    args = [a, b]
    if has_add:
        in_specs.append(pl.BlockSpec((tm, tn), lambda j, i, kk: (i, j)))
        args.append(add)
    return pl.pallas_call(
        body, grid=(n // tn, m // tm, nk), in_specs=in_specs,
        out_specs=pl.BlockSpec((tm, tn), lambda j, i, kk: (i, j)),
        out_shape=jax.ShapeDtypeStruct((m, n), out_dtype),
        scratch_shapes=[pltpu.VMEM((tm, tn), F32)],
        compiler_params=_params(("parallel", "parallel", "arbitrary")), name=name)(*args)


def _rms_fwd(x, w, *, tm=512):
    t = x.shape[0]

    def body(x_ref, w_ref, h_ref, r_ref):
        xv = x_ref[...]
        r = lax.rsqrt(jnp.mean(xv * xv, axis=-1, keepdims=True) + EPS)
        h_ref[...] = _bf(xv * r * w_ref[...])
        r_ref[...] = r

    row = lambda i: (i, 0)
    return pl.pallas_call(
        body, grid=(t // tm,),
        in_specs=[pl.BlockSpec((tm, D_MODEL), row), pl.BlockSpec((1, D_MODEL), lambda i: (0, 0))],
        out_specs=[pl.BlockSpec((tm, D_MODEL), row), pl.BlockSpec((tm, 1), row)],
        out_shape=[jax.ShapeDtypeStruct((t, D_MODEL), BF16), jax.ShapeDtypeStruct((t, 1), F32)],
        compiler_params=_params(("parallel",)), name="rms_fwd")(x, w)


def _rms_bwd(dh, x, r, w, dres, *, tm=512):
    t = x.shape[0]

    def body(dh_ref, x_ref, r_ref, w_ref, dres_ref, dx_ref, dw_ref):
        @pl.when(pl.program_id(0) == 0)
        def _():
            dw_ref[...] = jnp.zeros_like(dw_ref)

        nrm = x_ref[...] * r_ref[...]
        dhv = dh_ref[...]
        dw_ref[...] += jnp.sum(dhv * nrm, axis=0, keepdims=True)
        dn = dhv * w_ref[...]
        dx = r_ref[...] * (dn - nrm * jnp.mean(dn * nrm, axis=-1, keepdims=True))
        dx_ref[...] = dx + dres_ref[...]

    row = lambda i: (i, 0)
    fix = lambda i: (0, 0)
    return pl.pallas_call(
        body, grid=(t // tm,),
        in_specs=[pl.BlockSpec((tm, D_MODEL), row), pl.BlockSpec((tm, D_MODEL), row),
                  pl.BlockSpec((tm, 1), row), pl.BlockSpec((1, D_MODEL), fix),
                  pl.BlockSpec((tm, D_MODEL), row)],
        out_specs=[pl.BlockSpec((tm, D_MODEL), row), pl.BlockSpec((1, D_MODEL), fix)],
        out_shape=[jax.ShapeDtypeStruct((t, D_MODEL), F32), jax.ShapeDtypeStruct((1, D_MODEL), F32)],
        compiler_params=_params(("arbitrary",)), name="rms_bwd")(dh, x, r, w, dres)


ATT_BLK = 256


def _attn_masks(blk):
    r_io = lax.broadcasted_iota(jnp.int32, (blk, blk), 0)
    c_io = lax.broadcasted_iota(jnp.int32, (blk, blk), 1)
    return r_io, c_io


def _attn_fwd(proj, nb, s):
    blk = min(ATT_BLK, s)
    nq = s // blk
    scale = HEAD_DIM ** -0.5

    def body(q_ref, k_ref, v_ref, o_ref, tot_ref, q0_s, q1_s, kb_s, v0_s, v1_s, z_s, a_s, acc_s, cr_s):
        head0 = lax.broadcasted_iota(jnp.int32, (s, LANES), 1) < HEAD_DIM
        lane2 = lax.broadcasted_iota(jnp.int32, (blk, 2), 1)
        r_io, c_io = _attn_masks(blk)
        tri = c_io < r_io
        tri2 = jnp.concatenate([tri, tri], axis=0)
        u_gt = _bf(r_io > c_io)
        u2 = jnp.concatenate([u_gt, u_gt], axis=0)
        qv = q_ref[...] * scale
        q0_s[...] = _bf(jnp.where(head0, qv, 0.0))
        q1_s[...] = _bf(jnp.where(head0, 0.0, qv))
        kb_s[...] = _bf(k_ref[...])
        vv = v_ref[...]
        v0_s[...] = _bf(jnp.where(head0, vv, 0.0))
        v1_s[...] = _bf(jnp.where(head0, 0.0, vv))

        def scores(qcat, kblk, slot):
            k0 = pl.multiple_of(jnp.maximum(kblk, 0) * blk, blk)
            z_s[slot] = _dot_nt(qcat, kb_s[pl.ds(k0, blk), :])

        def weights(slot, diag, zslot=None):
            z = z_s[slot if zslot is None else zslot]
            cr = cr_s[...]
            sp = _softplus(z)
            spm = jnp.where(tri2, sp, 0.0) if diag else sp
            hi, lo = _trunc_split(spm)
            ps = _dot(jnp.concatenate([hi, lo], axis=1), u2)
            a = jnp.exp(z - (sp + ps + cr))
            if diag:
                a = jnp.where(tri2, a, 0.0)
            a_s[slot] = _bf(a)
            cr_s[...] = cr + ps[:, 0:1] + spm[:, 0:1]

        def apply(slot, k0):
            a = a_s[slot]
            vcat = jnp.concatenate([v0_s[pl.ds(k0, blk), :], v1_s[pl.ds(k0, blk), :]], axis=0)
            acc_s[...] += _dot(jnp.concatenate([a[:blk], a[blk:]], axis=1), vcat)

        def queries(qi):
            q0 = pl.multiple_of(qi * blk, blk)
            return jnp.concatenate([q0_s[pl.ds(q0, blk), :], q1_s[pl.ds(q0, blk), :]], axis=0)

        def first_scores(qi):
            qcat = queries(qi)
            scores(qcat, qi, 2)
            scores(qcat, qi - 1, 3)

        first_scores(jnp.int32(0))

        def qblock(qi, _):
            q0 = pl.multiple_of(qi * blk, blk)
            qcat = queries(qi)
            z_s[1] = z_s[3]
            cr_s[...] = jnp.zeros_like(cr_s)
            acc_s[...] = jnp.zeros_like(acc_s)
            weights(0, True, zslot=2)
            first_scores(jnp.minimum(qi + 1, nq - 1))

            def step(j, slot):
                scores(qcat, qi - j - 1, 1 - slot)
                weights(slot, False)
                apply(1 - slot, pl.multiple_of((qi - j + 1) * blk, blk))

            def pair(i, _):
                step(2 * i + 1, 1)
                step(2 * i + 2, 0)
                return 0

            lax.fori_loop(0, qi // 2, pair, 0)

            @pl.when(qi % 2 == 1)
            def _():
                step(qi, 1)

            apply(qi & 1, 0)
            o_ref[pl.ds(q0, blk), :] = acc_s[...]
            cr = cr_s[...]
            tot_ref[pl.ds(q0, blk), :] = jnp.where(lane2 == 0, cr[:blk], cr[blk:])
            return 0

        lax.fori_loop(0, nq, qblock, 0)

    t = nb * s
    return pl.pallas_call(
        body, grid=(nb, 8),
        in_specs=[pl.BlockSpec((s, LANES), lambda b, p: (b, CB_Q + p)),
                  pl.BlockSpec((s, LANES), lambda b, p: (b, CB_K + p)),
                  pl.BlockSpec((s, LANES), lambda b, p: (b, CB_V + p))],
        out_specs=[pl.BlockSpec((s, LANES), lambda b, p: (b, p)),
                   pl.BlockSpec((None, None, s, 2), lambda b, p: (b, p, 0, 0))],
        out_shape=[jax.ShapeDtypeStruct((t, SB_WIDTH), F32),
                   jax.ShapeDtypeStruct((nb, 8, s, 2), F32)],
        scratch_shapes=[pltpu.VMEM((s, LANES), BF16)] * 5 + [
            pltpu.VMEM((4, 2 * blk, blk), F32), pltpu.VMEM((2, 2 * blk, blk), BF16),
            pltpu.VMEM((blk, LANES), F32), pltpu.VMEM((2 * blk, 1), F32)],
        compiler_params=_params(("parallel", "parallel")), name="attn_fwd")(proj, proj, proj)


def _attn_bwd(proj, do, tot, nb, s):
    blk = min(ATT_BLK, s)
    nq = s // blk
    scale = HEAD_DIM ** -0.5

    def body(q_ref, k_ref, v_ref, do_ref, tot_ref, dq_ref, dk_ref, dv_ref,
             q0_s, q1_s, kb_s, k0_s, k1_s, vb_s, d0_s, d1_s, z_s, da_s, dz_s, a_s, dk_acc, dv_acc,
             dq_acc, cn_s, cw_s):
        head0 = lax.broadcasted_iota(jnp.int32, (s, LANES), 1) < HEAD_DIM
        r_io, c_io = _attn_masks(blk)
        tri = c_io < r_io
        tri2 = jnp.concatenate([tri, tri], axis=0)
        u_le = _bf(r_io <= c_io)
        u2 = jnp.concatenate([u_le, u_le], axis=0)
        u_lt = _bf(r_io < c_io)
        qv = q_ref[...] * scale
        q0_s[...] = _bf(jnp.where(head0, qv, 0.0))
        q1_s[...] = _bf(jnp.where(head0, 0.0, qv))
        kv = k_ref[...]
        kb_s[...] = _bf(kv)
        k0_s[...] = _bf(jnp.where(head0, kv, 0.0))
        k1_s[...] = _bf(jnp.where(head0, 0.0, kv))
        vb_s[...] = _bf(v_ref[...])
        dov = do_ref[...]
        d0_s[...] = _bf(jnp.where(head0, dov, 0.0))
        d1_s[...] = _bf(jnp.where(head0, 0.0, dov))
        dk_acc[...] = jnp.zeros_like(dk_acc)
        dv_acc[...] = jnp.zeros_like(dv_acc)

        def qblock(qi, _):
            q0 = pl.multiple_of(qi * blk, blk)
            qs_ = pl.ds(q0, blk)
            qhs = (q0_s[qs_, :], q1_s[qs_, :])
            dhs = (d0_s[qs_, :], d1_s[qs_, :])
            qcat = jnp.concatenate(qhs, axis=0)
            dcat = jnp.concatenate(dhs, axis=0)
            totf = tot_ref[qs_, :]

            def scores(kblk, slot):
                ks_ = pl.ds(pl.multiple_of(kblk * blk, blk), blk)
                z_s[slot] = _dot_nt(qcat, kb_s[ks_, :])
                da_s[slot] = _dot_nt(dcat, vb_s[ks_, :])

            def grads(slot, diag):
                cn, cw = cn_s[...], cw_s[...]
                z = z_s[slot]
                sp = _softplus(z)
                lb = z - sp
                spm = jnp.where(tri2, sp, 0.0) if diag else sp
                hi, lo = _trunc_split(spm)
                pin = _dot(jnp.concatenate([hi, lo], axis=1), u2)
                a = jnp.exp(lb + (pin + cn))
                if diag:
                    a = jnp.where(tri2, a, 0.0)
                w = a * da_s[slot]
                wex = _dot(_bf(w), u_lt)
                dz = w - jnp.exp(lb) * (w + (wex + cw))
                if diag:
                    dz = jnp.where(tri2, dz, 0.0)
                dz_s[slot] = _bf(dz)
                a_s[slot] = _bf(a)
                cn_s[...] = cn + pin[:, blk - 1:blk]
                cw_s[...] = cw + wex[:, blk - 1:blk] + w[:, blk - 1:blk]

            def apply(slot, kblk):
                ks_ = pl.ds(pl.multiple_of(kblk * blk, blk), blk)
                dz = dz_s[slot]
                dq_acc[...] += _dot(jnp.concatenate([dz[:blk], dz[blk:]], axis=1),
                                    jnp.concatenate([k0_s[ks_, :], k1_s[ks_, :]], axis=0))
                dk_acc[ks_, :] += _dot_tn(dz, qcat)
                dv_acc[ks_, :] += _dot_tn(a_s[slot], dcat)

            def step(j, slot, diag):
                if not diag:
                    scores(j + 1, 1 - slot)
                grads(slot, diag)
                apply(1 - slot, jnp.maximum(j - 1, 0))

            dz_s[1] = jnp.zeros_like(dz_s[1])
            a_s[1] = jnp.zeros_like(a_s[1])
            scores(0, 0)
            dq_acc[...] = jnp.zeros_like(dq_acc)
            cn_s[...] = -jnp.concatenate([totf[:, 0:1], totf[:, 1:2]], axis=0)
            cw_s[...] = jnp.zeros_like(cw_s)

            def pair(i, _):
                step(2 * i, 0, False)
                step(2 * i + 1, 1, False)
                return 0

            lax.fori_loop(0, qi // 2, pair, 0)

            @pl.when(qi % 2 == 1)
            def _():
                step(qi - 1, 0, False)
                step(qi, 1, True)

            @pl.when(qi % 2 == 0)
            def _():
                step(qi, 0, True)

            apply(qi & 1, qi)
            dq_ref[qs_, :] = _bf(dq_acc[...] * scale)
            return 0

        lax.fori_loop(0, nq, qblock, 0)
        dk_ref[...] = _bf(dk_acc[...])
        dv_ref[...] = _bf(dv_acc[...])

    t = nb * s
    hp = lambda b, p: (b, p)
    out = jax.ShapeDtypeStruct((t, SB_WIDTH), BF16)
    return pl.pallas_call(
        body, grid=(nb, 8),
        in_specs=[pl.BlockSpec((s, LANES), lambda b, p: (b, CB_Q + p)),
                  pl.BlockSpec((s, LANES), lambda b, p: (b, CB_K + p)),
                  pl.BlockSpec((s, LANES), lambda b, p: (b, CB_V + p)),
                  pl.BlockSpec((s, LANES), hp),
                  pl.BlockSpec((None, None, s, 2), lambda b, p: (b, p, 0, 0))],
        out_specs=[pl.BlockSpec((s, LANES), hp)] * 3,
        out_shape=[out, out, out],
        scratch_shapes=[pltpu.VMEM((s, LANES), BF16)] * 8 + [
            pltpu.VMEM((2, 2 * blk, blk), F32), pltpu.VMEM((2, 2 * blk, blk), F32),
            pltpu.VMEM((2, 2 * blk, blk), BF16), pltpu.VMEM((2, 2 * blk, blk), BF16),
            pltpu.VMEM((s, LANES), F32), pltpu.VMEM((s, LANES), F32),
            pltpu.VMEM((blk, LANES), F32), pltpu.VMEM((2 * blk, 1), F32), pltpu.VMEM((2 * blk, 1), F32)],
        compiler_params=_params(("parallel", "parallel")), name="attn_bwd")(proj, proj, proj, do, tot)


CONV_TC = 256
CONV_ROWS = 64
HALO = 8


def _conv_rows(ws, w_ref, b_ref, n):
    taps = [ws[HALO - kk:HALO - kk + n, :] for kk in range(CONV_K)]
    y = b_ref[...] + w_ref[CONV_K - 1:CONV_K, :] * taps[0]
    for kk in range(1, CONV_K):
        y = y + w_ref[CONV_K - 1 - kk:CONV_K - kk, :] * taps[kk]
    return y, taps


def _conv_fwd(proj, conv_w, conv_b, nb, s):
    tc, rows = CONV_TC, min(CONV_ROWS, s)
    cb0 = CB_XBC * LANES // tc

    def body(x_ref, w_ref, b_ref, o_ref, xp, ws):
        xp[0:HALO, :] = jnp.zeros((HALO, tc), F32)
        xp[HALO:HALO + s, :] = x_ref[...].astype(F32)

        def chunk(i, _):
            r0 = pl.multiple_of(i * rows, rows)
            ws[...] = xp[pl.ds(r0, rows + HALO), :]
            y, _ = _conv_rows(ws, w_ref, b_ref, rows)
            o_ref[pl.ds(r0, rows), :] = y * _sigmoid(y)
            return 0

        lax.fori_loop(0, s // rows, chunk, 0)

    t = nb * s
    return pl.pallas_call(
        body, grid=(nb, CONV_DIM // tc),
        in_specs=[pl.BlockSpec((s, tc), lambda b, c: (b, cb0 + c)),
                  pl.BlockSpec((CONV_K, tc), lambda b, c: (0, c)),
                  pl.BlockSpec((1, tc), lambda b, c: (0, c))],
        out_specs=pl.BlockSpec((s, tc), lambda b, c: (b, c)),
        out_shape=jax.ShapeDtypeStruct((t, CONV_DIM), F32),
        scratch_shapes=[pltpu.VMEM((s + HALO, tc), F32), pltpu.VMEM((rows + HALO, tc), F32)],
        compiler_params=_params(("parallel", "parallel")), name="conv_fwd")(proj, conv_w, conv_b)


def _conv_bwd(proj, dact, conv_w, conv_b, nb, s, *, col0, name):
    width = dact.shape[1]
    tc, rows = min(CONV_TC, width), min(CONV_ROWS, s)
    cb0 = (CB_XBC * LANES + col0) // tc
    wb0 = col0 // tc

    def body(x_ref, d_ref, w_ref, b_ref, dx_ref, dw_ref, db_ref, xp, dp, dys, ws):
        @pl.when(pl.program_id(1) == 0)
        def _():
            dw_ref[...] = jnp.zeros_like(dw_ref)
            db_ref[...] = jnp.zeros_like(db_ref)

        n = rows + HALO
        pad = jnp.zeros((HALO, tc), F32)
        xp[0:HALO, :] = pad
        xp[HALO:HALO + s, :] = x_ref[...].astype(F32)
        xp[HALO + s:HALO + s + HALO, :] = pad
        dp[0:s, :] = d_ref[...]
        dp[s:s + HALO, :] = pad

        def chunk(i, acc):
            r0 = pl.multiple_of(i * rows, rows)
            ws[...] = xp[pl.ds(r0, n + HALO), :]
            y, taps = _conv_rows(ws, w_ref, b_ref, n)
            sg = _sigmoid(y)
            dy = dp[pl.ds(r0, n), :] * (sg * (1.0 + y * (1.0 - sg)))
            dys[...] = dy
            dx = w_ref[CONV_K - 1:CONV_K, :] * dy[0:rows]
            for kk in range(1, CONV_K):
                dx = dx + w_ref[CONV_K - 1 - kk:CONV_K - kk, :] * dys[kk:kk + rows, :]
            dx_ref[pl.ds(r0, rows), :] = _bf(dx)
            dyc = dy[0:rows]
            new = [acc[0] + jnp.sum(dyc, axis=0, keepdims=True)]
            for kk in range(CONV_K):
                new.append(acc[1 + kk] + jnp.sum(dyc * taps[kk][0:rows], axis=0, keepdims=True))
            return tuple(new)

        acc = lax.fori_loop(0, s // rows, chunk, (jnp.zeros((1, tc), F32),) * (1 + CONV_K))
        db_ref[...] += acc[0]
        for kk in range(CONV_K):
            dw_ref[CONV_K - 1 - kk:CONV_K - kk, :] += acc[1 + kk]

    t = nb * s
    return pl.pallas_call(
        body, grid=(width // tc, nb),
        in_specs=[pl.BlockSpec((s, tc), lambda c, b: (b, cb0 + c)),
                  pl.BlockSpec((s, tc), lambda c, b: (b, c)),
                  pl.BlockSpec((CONV_K, tc), lambda c, b: (0, wb0 + c)),
                  pl.BlockSpec((1, tc), lambda c, b: (0, wb0 + c))],
        out_specs=[pl.BlockSpec((s, tc), lambda c, b: (b, c)),
                   pl.BlockSpec((CONV_K, tc), lambda c, b: (0, c)),
                   pl.BlockSpec((1, tc), lambda c, b: (0, c))],
        out_shape=[jax.ShapeDtypeStruct((t, width), BF16),
                   jax.ShapeDtypeStruct((CONV_K, width), F32),
                   jax.ShapeDtypeStruct((1, width), F32)],
        scratch_shapes=[pltpu.VMEM((s + 2 * HALO, tc), F32), pltpu.VMEM((s + HALO, tc), F32),
                        pltpu.VMEM((rows + HALO, tc), F32), pltpu.VMEM((rows + 2 * HALO, tc), F32)],
        compiler_params=_params(("parallel", "arbitrary")), name=name)(proj, dact, conv_w, conv_b)


L = SSD_CHUNK
N_PAIR = SSD_HPG // 2


def _pair_lanes(v0, v1, lane):
    return jnp.where(lane < HEAD_DIM, v0, v1)


def _onehot(shape, row_shift, col_shift):
    row = lax.broadcasted_iota(jnp.int32, shape, 0)
    col = lax.broadcasted_iota(jnp.int32, shape, 1)
    return _bf(lax.shift_right_logical(row, row_shift) == lax.shift_right_logical(col, col_shift))


def _ssd_tables():
    sp = _onehot((LANES, N_PAIR * LANES), 0, 6)
    sf = _onehot((LANES, SSD_HPG * LANES), 0, 7)
    return (jnp.concatenate([sp] * 3, axis=0), jnp.concatenate([sf] * 3, axis=0),
            _onehot((N_PAIR * LANES, LANES), 6, 0), _onehot((SSD_HPG * LANES, LANES), 7, 0))


def _split3(a):
    a1 = _bf(a)
    r1 = a - a1.astype(F32)
    a2 = _bf(r1)
    return jnp.concatenate([a1, a2, _bf(r1 - a2.astype(F32))], axis=1)


def _split2(a):
    hi, lo = _trunc_split(a)
    return jnp.concatenate([hi, lo], axis=1)


def _tri_dot(tri_bf, a):
    a1 = _bf(a)
    r1 = a - a1.astype(F32)
    a2 = _bf(r1)
    parts = jnp.concatenate([a1, a2, _bf(r1 - a2.astype(F32))], axis=0)
    return _dot(jnp.concatenate([tri_bf] * 3, axis=1), parts)


HEAD_W = 4 * LANES + 2 * N_PAIR * LANES + SSD_HPG * LANES


def _ssd_head(dtr_ref, bias_ref, a_neg, c, tri_bf, spread_pair, spread_full, head_s, slot):
    raw = dtr_ref[pl.ds(pl.multiple_of(c * L, L), L), :] + bias_ref[...]
    dt = _softplus(raw)
    acs = _tri_dot(tri_bf, dt * a_neg)
    acs3 = _split3(acs)
    both = _dot(jnp.concatenate([_split3(dt), acs3], axis=0), spread_pair)
    parts = (raw, dt, acs, acs.T, both[:L], both[L:], _dot(acs3, spread_full))
    off = 0
    for p in parts:
        head_s[slot, :, off:off + p.shape[1]] = p
        off += p.shape[1]


def _ssd_head_load(head_s, slot):
    out, off = [], 0
    for w in (LANES, LANES, LANES, L, N_PAIR * LANES, N_PAIR * LANES, SSD_HPG * LANES):
        out.append(head_s[slot, :, off:off + w])
        off += w
    return out


def _ssd_fwd(xbc, dtr, bias, alog, dskip, nb, s):
    nc = s // L
    t = nb * s

    def body(x_ref, b_ref, c_ref, dtr_ref, bias_ref, alog_ref, dsk_ref, y_ref, hs_ref, h_scr, head_s):
        lane = lax.broadcasted_iota(jnp.int32, (L, LANES), 1)
        lane1 = lax.broadcasted_iota(jnp.int32, (1, LANES), 1)
        rowc = lax.broadcasted_iota(jnp.int32, (L, 1), 0)
        rr = lax.broadcasted_iota(jnp.int32, (L, L), 0)
        cc = lax.broadcasted_iota(jnp.int32, (L, L), 1)
        causal = rr >= cc
        h_scr[...] = jnp.zeros_like(h_scr)
        dsk = dsk_ref[...]
        spread_pair, spread_full, _, _ = _ssd_tables()
        a_neg = -jnp.exp(alog_ref[...])
        head = functools.partial(_ssd_head, dtr_ref, bias_ref, a_neg, tri_bf=_bf(cc <= rr),
                                 spread_pair=spread_pair, spread_full=spread_full, head_s=head_s)
        head(jnp.int32(0), slot=0)

        def chunk(c, slot):
            r0 = pl.multiple_of(c * L, L)
            head(jnp.minimum(c + 1, nc - 1), slot=1 - slot)
            _, _, acs, acs_t, dt_all, acs_all, acs_full = _ssd_head_load(head_s, slot)
            bb = _bf(b_ref[pl.ds(r0, L), :])
            cb = _bf(c_ref[pl.ds(r0, L), :])
            gm = _dot_nt(cb, bb)
            for pr in range(N_PAIR):
                j0, j1 = 2 * pr, 2 * pr + 1
                ms = []
                for j in (j0, j1):
                    seg = acs_full[:, j * LANES:(j + 1) * LANES] - acs_t[j:j + 1, :]
                    ms.append(_bf(gm * jnp.exp(jnp.where(causal, seg, -1e30))))
                x = x_ref[pl.ds(r0, L), pr * LANES:(pr + 1) * LANES]
                dt_p = dt_all[:, pr * LANES:(pr + 1) * LANES]
                acs_p = acs_all[:, pr * LANES:(pr + 1) * LANES]
                last_p = acs_p[L - 1:L, :]
                d_p = _pair_lanes(dsk[:, j0:j0 + 1], dsk[:, j1:j1 + 1], lane1)
                xd = x * dt_p
                xdb = _bf(xd)
                yds = _dot(jnp.concatenate(ms, axis=0), xdb)
                yd = _pair_lanes(yds[:L], yds[L:], lane)
                hp = h_scr[pr]
                yo = _dot_nt(cb, _bf(hp)) * jnp.exp(acs_p)
                y_ref[pl.ds(r0, L), pr * LANES:(pr + 1) * LANES] = yd + yo + x * d_p
                hs_ref[c, pr] = hp
                sp = _dot_tn(_bf(xd * jnp.exp(last_p - acs_p)), bb)
                cd = jnp.where(rowc < HEAD_DIM, jnp.exp(acs[L - 1:L, j0:j0 + 1]),
                               jnp.exp(acs[L - 1:L, j1:j1 + 1]))
                h_scr[pr] = hp * cd + sp

        def two_chunks(i, _):
            chunk(2 * i, 0)
            chunk(2 * i + 1, 1)
            return 0

        lax.fori_loop(0, nc // 2, two_chunks, 0)

    assert nc % 2 == 0
    grp = lambda b, g: (g, 0, 0)
    return pl.pallas_call(
        body, grid=(nb, SSD_GROUPS),
        in_specs=[pl.BlockSpec((s, 4 * LANES), lambda b, g: (b, g)),
                  pl.BlockSpec((s, LANES), lambda b, g: (b, 16 + g)),
                  pl.BlockSpec((s, LANES), lambda b, g: (b, 20 + g)),
                  pl.BlockSpec((None, s, LANES), lambda b, g: (g, b, 0)),
                  pl.BlockSpec((None, 1, LANES), grp), pl.BlockSpec((None, 1, LANES), grp),
                  pl.BlockSpec((None, 1, LANES), grp)],
        out_specs=[pl.BlockSpec((s, 4 * LANES), lambda b, g: (b, g)),
                   pl.BlockSpec((None, None, nc, N_PAIR, L, LANES), lambda b, g: (b, g, 0, 0, 0, 0))],
        out_shape=[jax.ShapeDtypeStruct((t, SSD_WIDTH), F32),
                   jax.ShapeDtypeStruct((nb, SSD_GROUPS, nc, N_PAIR, L, LANES), F32)],
        scratch_shapes=[pltpu.VMEM((N_PAIR, L, LANES), F32), pltpu.VMEM((2, L, HEAD_W), F32)],
        compiler_params=_params(("parallel", "parallel")), name="ssd_fwd")(
            xbc, xbc, xbc, dtr, bias, alog, dskip)


def _ssd_bwd(xbc, dtr, bias, alog, dskip, hs, dy, nb, s):
    nc = s // L
    t = nb * s

    def body(x_ref, b_ref, c_ref, dtr_ref, bias_ref, alog_ref, dsk_ref, hs_ref, dy_ref,
             dx_ref, db_ref, dc_ref, ddtr_ref, dbias_ref, dalog_ref, ddsk_ref, dh_scr, head_s):
        lane = lax.broadcasted_iota(jnp.int32, (L, LANES), 1)
        lane1 = lax.broadcasted_iota(jnp.int32, (1, LANES), 1)
        rowc = lax.broadcasted_iota(jnp.int32, (L, 1), 0)
        rr = lax.broadcasted_iota(jnp.int32, (L, L), 0)
        cc = lax.broadcasted_iota(jnp.int32, (L, L), 1)
        causal = rr >= cc
        tri_t = _bf(cc >= rr)
        dh_scr[...] = jnp.zeros_like(dh_scr)
        dsk = dsk_ref[...]
        a_neg = -jnp.exp(alog_ref[...])

        @pl.when(pl.program_id(1) == 0)
        def _():
            dbias_ref[...] = jnp.zeros_like(dbias_ref)
            dalog_ref[...] = jnp.zeros_like(dalog_ref)
            ddsk_ref[...] = jnp.zeros_like(ddsk_ref)

        spread_pair, spread_full, sum_pair, sum_full = _ssd_tables()
        sum_pair2 = jnp.concatenate([sum_pair] * 2, axis=0)
        sum_full2 = jnp.concatenate([sum_full] * 2, axis=0)
        head = functools.partial(_ssd_head, dtr_ref, bias_ref, a_neg, tri_bf=_bf(cc <= rr),
                                 spread_pair=spread_pair, spread_full=spread_full, head_s=head_s)
        head(jnp.int32(nc - 1), slot=0)

        def chunk(c, slot):
            r0 = pl.multiple_of(c * L, L)
            head(jnp.maximum(c - 1, 0), slot=1 - slot)
            raw, dt, acs, acs_t, dt_all, acs_all, acs_full = _ssd_head_load(head_s, slot)
            bb = _bf(b_ref[pl.ds(r0, L), :])
            cb = _bf(c_ref[pl.ds(r0, L), :])
            gm = _dot_nt(cb, bb)
            dg = jnp.zeros((L, L), F32)
            dbacc = jnp.zeros((L, LANES), F32)
            dcacc = jnp.zeros((L, LANES), F32)
            dacs_t = jnp.zeros((L, L), F32)
            ta_all, te_all, qm_all, tc_all, td_all, dcd_all = [], [], [], [], [], []
            for pr in range(N_PAIR):
                js = (2 * pr, 2 * pr + 1)
                lms, mfs = [], []
                for j in js:
                    seg = acs_full[:, j * LANES:(j + 1) * LANES] - acs_t[j:j + 1, :]
                    lm = jnp.exp(jnp.where(causal, seg, -1e30))
                    lms.append(lm)
                    mfs.append(gm * lm)
                x = x_ref[pl.ds(r0, L), pr * LANES:(pr + 1) * LANES]
                dyv = dy_ref[pl.ds(r0, L), pr * LANES:(pr + 1) * LANES]
                dt_p = dt_all[:, pr * LANES:(pr + 1) * LANES]
                acs_p = acs_all[:, pr * LANES:(pr + 1) * LANES]
                last_p = acs_p[L - 1:L, :]
                d_p = _pair_lanes(dsk[:, js[0]:js[0] + 1], dsk[:, js[1]:js[1] + 1], lane1)
                e_p = jnp.exp(acs_p)
                dte_p = jnp.exp(last_p - acs_p)
                xd = x * dt_p
                xdb = _bf(xd)
                hp = hs_ref[c, pr]
                hb = _bf(hp)
                dye = _bf(dyv * e_p)
                yo = _dot_nt(cb, hb) * e_p
                dcacc = dcacc + _dot(dye, hb)
                dh_out = _dot_tn(dye, cb)
                dyb = _bf(dyv)
                dms = _dot_nt(_bf(jnp.concatenate(
                    [jnp.where(lane < HEAD_DIM, dyv, 0.0), jnp.where(lane < HEAD_DIM, 0.0, dyv)], axis=0)), xdb)
                mty = _dot_tn(_bf(jnp.concatenate(mfs, axis=1)), dyb)
                dxd = jnp.where(lane < HEAD_DIM, mty[:L], mty[L:])
                for hh in range(2):
                    dm = dms[hh * L:(hh + 1) * L]
                    dg = dg + dm * lms[hh]
                    qm = dm * mfs[hh]
                    qm_all.append(qm)
                    dacs_t = dacs_t - jnp.where(rr == js[hh], jnp.sum(qm, axis=0, keepdims=True), 0.0)
                ds = dh_scr[pr]
                dsb = _bf(ds)
                xdd = xd * dte_p
                dxdd = _dot_nt(bb, dsb)
                dbacc = dbacc + _dot(_bf(xdd), dsb)
                dxd = dxd + dxdd * dte_p
                cd = jnp.where(rowc < HEAD_DIM, jnp.exp(acs[L - 1:L, js[0]:js[0] + 1]),
                               jnp.exp(acs[L - 1:L, js[1]:js[1] + 1]))
                prod = ds * hp
                dcd_all += [jnp.sum(prod[:HEAD_DIM], axis=0, keepdims=True),
                            jnp.sum(prod[HEAD_DIM:], axis=0, keepdims=True)]
                dh_scr[pr] = dh_out + ds * cd
                tcv = dxdd * xdd
                ta_all.append(dyv * yo - tcv)
                tc_all.append(jnp.sum(tcv, axis=0, keepdims=True))
                te_all.append(dxd * x)
                td_all.append(jnp.sum(dyv * x, axis=0, keepdims=True))
                dx_ref[pl.ds(r0, L), pr * LANES:(pr + 1) * LANES] = dxd * dt_p + dyv * d_p
            dgb = _bf(dg)
            dc_ref[pl.ds(r0, L), :] = dcacc + _dot(dgb, bb)
            db_ref[pl.ds(r0, L), :] = dbacc + _dot_tn(dgb, cb)
            last_add = jnp.zeros((1, LANES), F32)
            dd_add = jnp.zeros((1, LANES), F32)
            cd_row = jnp.exp(acs[L - 1:L, :])
            for j in range(SSD_HPG):
                hm = (lane1 >= HEAD_DIM) if j % 2 else (lane1 < HEAD_DIM)
                tcj = jnp.sum(jnp.where(hm, tc_all[j // 2], 0.0), axis=1, keepdims=True)
                tdj = jnp.sum(jnp.where(hm, td_all[j // 2], 0.0), axis=1, keepdims=True)
                dcdj = jnp.sum(dcd_all[j], axis=1, keepdims=True)
                last_add = last_add + jnp.where(lane1 == j, tcj + dcdj * cd_row, 0.0)
                dd_add = dd_add + jnp.where(lane1 == j, tdj, 0.0)
            dacs = (_dot(_split2(jnp.concatenate(ta_all, axis=1)), sum_pair2)
                    + _dot(_split2(jnp.concatenate(qm_all, axis=1)), sum_full2)
                    + dacs_t.T + jnp.where(rowc == L - 1, last_add, 0.0))
            ddta = _tri_dot(tri_t, dacs)
            ddt = _dot(_bf(jnp.concatenate(te_all, axis=1)), sum_pair) + ddta * a_neg
            ddraw = ddt * _sigmoid(raw)
            ddtr_ref[pl.ds(r0, L), :] = ddraw
            dbias_ref[...] += jnp.sum(ddraw, axis=0, keepdims=True)
            dalog_ref[...] += jnp.sum(ddta * dt, axis=0, keepdims=True) * a_neg
            ddsk_ref[...] += dd_add

        def two_chunks(i, _):
            chunk(nc - 1 - 2 * i, 0)
            chunk(nc - 2 - 2 * i, 1)
            return 0

        lax.fori_loop(0, nc // 2, two_chunks, 0)

    assert nc % 2 == 0
    grp = lambda g, b: (g, 0, 0)
    small = jax.ShapeDtypeStruct((SSD_GROUPS, 1, LANES), F32)
    return pl.pallas_call(
        body, grid=(SSD_GROUPS, nb),
        in_specs=[pl.BlockSpec((s, 4 * LANES), lambda g, b: (b, g)),
                  pl.BlockSpec((s, LANES), lambda g, b: (b, 16 + g)),
                  pl.BlockSpec((s, LANES), lambda g, b: (b, 20 + g)),
                  pl.BlockSpec((None, s, LANES), lambda g, b: (g, b, 0)),
                  pl.BlockSpec((None, 1, LANES), grp), pl.BlockSpec((None, 1, LANES), grp),
                  pl.BlockSpec((None, 1, LANES), grp),
                  pl.BlockSpec((None, None, nc, N_PAIR, L, LANES), lambda g, b: (b, g, 0, 0, 0, 0)),
                  pl.BlockSpec((s, 4 * LANES), lambda g, b: (b, g))],
        out_specs=[pl.BlockSpec((s, 4 * LANES), lambda g, b: (b, g)),
                   pl.BlockSpec((s, LANES), lambda g, b: (b, g)),
                   pl.BlockSpec((s, LANES), lambda g, b: (b, g)),
                   pl.BlockSpec((None, s, LANES), lambda g, b: (g, b, 0)),
                   pl.BlockSpec((None, 1, LANES), grp), pl.BlockSpec((None, 1, LANES), grp),
                   pl.BlockSpec((None, 1, LANES), grp)],
        out_shape=[jax.ShapeDtypeStruct((t, SSD_WIDTH), F32),
                   jax.ShapeDtypeStruct((t, SSD_GROUPS * SSD_STATE), F32),
                   jax.ShapeDtypeStruct((t, SSD_GROUPS * SSD_STATE), F32),
                   jax.ShapeDtypeStruct((SSD_GROUPS, t, LANES), F32), small, small, small],
        scratch_shapes=[pltpu.VMEM((N_PAIR, L, LANES), F32), pltpu.VMEM((2, L, HEAD_W), F32)],
        compiler_params=_params(("parallel", "arbitrary")), name="ssd_bwd")(
            xbc, xbc, xbc, dtr, bias, alog, dskip, hs, dy)


GN = SSD_WIDTH // SSD_GROUPS


def _post_fwd(o, proj, y, nw, *, tm=256):
    t = o.shape[0]

    def body(o_ref, za_ref, y_ref, zs_ref, nw_ref, ya_ref, ys_ref):
        za = za_ref[...].astype(F32)
        ya_ref[...] = _bf(o_ref[...] * (za * _sigmoid(za)))
        zs = zs_ref[...].astype(F32)
        yz = y_ref[...] * (zs * _sigmoid(zs))
        for g in range(SSD_GROUPS):
            sl = slice(g * GN, (g + 1) * GN)
            v = yz[:, sl]
            r = lax.rsqrt(jnp.mean(v * v, axis=-1, keepdims=True) + EPS)
            ys_ref[:, sl] = _bf(v * r * nw_ref[:, sl])

    return pl.pallas_call(
        body, grid=(t // tm,),
        in_specs=[pl.BlockSpec((tm, SB_WIDTH), lambda i: (i, 0)),
                  pl.BlockSpec((tm, SB_WIDTH), lambda i: (i, CB_ZA * LANES // SB_WIDTH)),
                  pl.BlockSpec((tm, SSD_WIDTH), lambda i: (i, 0)),
                  pl.BlockSpec((tm, SSD_WIDTH), lambda i: (i, CB_ZS * LANES // SSD_WIDTH)),
                  pl.BlockSpec((1, SSD_WIDTH), lambda i: (0, 0))],
        out_specs=[pl.BlockSpec((tm, SB_WIDTH), lambda i: (i, 0)),
                   pl.BlockSpec((tm, SSD_WIDTH), lambda i: (i, 0))],
        out_shape=[jax.ShapeDtypeStruct((t, SB_WIDTH), BF16), jax.ShapeDtypeStruct((t, SSD_WIDTH), BF16)],
        compiler_params=_params(("parallel",)), name="post_fwd")(o, proj, y, proj, nw)


def _post_bwd(dya, o, proj, dys, y, nw, *, tm=256):
    t = o.shape[0]

    def body(dya_ref, o_ref, za_ref, dys_ref, y_ref, zs_ref, nw_ref,
             do_ref, dza_ref, dy_ref, dzs_ref, dnw_ref):
        @pl.when(pl.program_id(0) == 0)
        def _():
            dnw_ref[...] = jnp.zeros_like(dnw_ref)

        za = za_ref[...].astype(F32)
        sa = _sigmoid(za)
        d = dya_ref[...]
        do_ref[...] = d * (za * sa)
        dza_ref[...] = _bf(d * o_ref[...] * (sa * (1.0 + za * (1.0 - sa))))
        zs = zs_ref[...].astype(F32)
        ss = _sigmoid(zs)
        silu = zs * ss
        yv = y_ref[...]
        yz = yv * silu
        dv = dys_ref[...]
        for g in range(SSD_GROUPS):
            sl = slice(g * GN, (g + 1) * GN)
            v = yz[:, sl]
            r = lax.rsqrt(jnp.mean(v * v, axis=-1, keepdims=True) + EPS)
            nrm = v * r
            dg = dv[:, sl]
            dnw_ref[:, sl] += jnp.sum(dg * nrm, axis=0, keepdims=True)
            dn = dg * nw_ref[:, sl]
            dyz = r * (dn - nrm * jnp.mean(dn * nrm, axis=-1, keepdims=True))
            dy_ref[:, sl] = dyz * silu[:, sl]
            dzs_ref[:, sl] = _bf(dyz * yv[:, sl] * (ss[:, sl] * (1.0 + zs[:, sl] * (1.0 - ss[:, sl]))))

    a_spec = pl.BlockSpec((tm, SB_WIDTH), lambda i: (i, 0))
    s_spec = pl.BlockSpec((tm, SSD_WIDTH), lambda i: (i, 0))
    w_spec = pl.BlockSpec((1, SSD_WIDTH), lambda i: (0, 0))
    return pl.pallas_call(
        body, grid=(t // tm,),
        in_specs=[a_spec, a_spec, pl.BlockSpec((tm, SB_WIDTH), lambda i: (i, CB_ZA * LANES // SB_WIDTH)),
                  s_spec, s_spec, pl.BlockSpec((tm, SSD_WIDTH), lambda i: (i, CB_ZS * LANES // SSD_WIDTH)),
                  w_spec],
        out_specs=[a_spec, a_spec, s_spec, s_spec, w_spec],
        out_shape=[jax.ShapeDtypeStruct((t, SB_WIDTH), F32), jax.ShapeDtypeStruct((t, SB_WIDTH), BF16),
                   jax.ShapeDtypeStruct((t, SSD_WIDTH), F32), jax.ShapeDtypeStruct((t, SSD_WIDTH), BF16),
                   jax.ShapeDtypeStruct((1, SSD_WIDTH), F32)],
        compiler_params=_params(("arbitrary",)), name="post_bwd")(dya, o, proj, dys, y, proj, nw)


def _merge_fwd(ya, ys, proj_g, *, tm=512):
    t = ya.shape[0]

    def body(ya_ref, ys_ref, ga_ref, gs_ref, m_ref):
        m_ref[...] = _bf(_sigmoid(ga_ref[...].astype(F32)) * ya_ref[...]
                         + _sigmoid(gs_ref[...].astype(F32)) * ys_ref[...])

    spec = pl.BlockSpec((tm, D_MODEL), lambda i: (i, 0))
    return pl.pallas_call(
        body, grid=(t // tm,),
        in_specs=[spec, spec, pl.BlockSpec((tm, D_MODEL), lambda i: (i, 0)),
                  pl.BlockSpec((tm, D_MODEL), lambda i: (i, 1))],
        out_specs=spec, out_shape=jax.ShapeDtypeStruct((t, D_MODEL), BF16),
        compiler_params=_params(("parallel",)), name="merge_fwd")(ya, ys, proj_g, proj_g)


def _merge_bwd(dm, ya, ys, proj_g, *, tm=512):
    t = ya.shape[0]

    def body(dm_ref, ya_ref, ys_ref, ga_ref, gs_ref, dya_ref, dys_ref, dg_ref):
        d = dm_ref[...]
        ga = _sigmoid(ga_ref[...].astype(F32))
        gs = _sigmoid(gs_ref[...].astype(F32))
        dya_ref[...] = _bf(d * ga)
        dys_ref[...] = _bf(d * gs)
        dg_ref[:, :D_MODEL] = _bf(d * ya_ref[...] * (ga * (1.0 - ga)))
        dg_ref[:, D_MODEL:] = _bf(d * ys_ref[...] * (gs * (1.0 - gs)))

    spec = pl.BlockSpec((tm, D_MODEL), lambda i: (i, 0))
    return pl.pallas_call(
        body, grid=(t // tm,),
        in_specs=[spec, spec, spec, pl.BlockSpec((tm, D_MODEL), lambda i: (i, 0)),
                  pl.BlockSpec((tm, D_MODEL), lambda i: (i, 1))],
        out_specs=[spec, spec, pl.BlockSpec((tm, 2 * D_MODEL), lambda i: (i, 0))],
        out_shape=[jax.ShapeDtypeStruct((t, D_MODEL), BF16), jax.ShapeDtypeStruct((t, D_MODEL), BF16),
                   jax.ShapeDtypeStruct((t, 2 * D_MODEL), BF16)],
        compiler_params=_params(("parallel",)), name="merge_bwd")(dm, ya, ys, proj_g, proj_g)


def _final(x, out, wf, target, *, tm=512):
    t = x.shape[0]

    def body(x_ref, o_ref, w_ref, t_ref, loss_ref, dx_ref, dw_ref):
        @pl.when(pl.program_id(0) == 0)
        def _():
            loss_ref[...] = jnp.zeros_like(loss_ref)
            dw_ref[...] = jnp.zeros_like(dw_ref)

        x2 = x_ref[...] + o_ref[...]
        r = lax.rsqrt(jnp.mean(x2 * x2, axis=-1, keepdims=True) + EPS)
        nrm = x2 * r
        e = nrm * w_ref[...] - t_ref[...]
        row_loss = jnp.mean(e * e, axis=-1, keepdims=True)
        loss_ref[...] += 0.5 * jnp.sum(row_loss, axis=0, keepdims=True)
        dyv = e * (1.0 / D_MODEL)
        dw_ref[...] += jnp.sum(dyv * nrm, axis=0, keepdims=True)
        dn = dyv * w_ref[...]
        dx_ref[...] = r * (dn - nrm * jnp.mean(dn * nrm, axis=-1, keepdims=True))

    spec = pl.BlockSpec((tm, D_MODEL), lambda i: (i, 0))
    fix = pl.BlockSpec((1, D_MODEL), lambda i: (0, 0))
    return pl.pallas_call(
        body, grid=(t // tm,),
        in_specs=[spec, spec, fix, spec],
        out_specs=[pl.BlockSpec((1, LANES), lambda i: (0, 0)), spec, fix],
        out_shape=[jax.ShapeDtypeStruct((1, LANES), F32), jax.ShapeDtypeStruct((t, D_MODEL), F32),
                   jax.ShapeDtypeStruct((1, D_MODEL), F32)],
        compiler_params=_params(("arbitrary",)), name="final_loss")(x, out, wf, target)


def _group_lanes(v):
    return jnp.pad(v.reshape(SSD_GROUPS, 1, SSD_HPG), ((0, 0), (0, 0), (0, LANES - SSD_HPG)))


def _ungroup_lanes(v):
    return v[:, :, :SSD_HPG].reshape(1, N_DT)


def _row_slabs(pieces, rows):
    total = sum(p.shape[0] for p in pieces)
    assert total % rows == 0
    slabs = []
    for lo in range(0, total, rows):
        parts, off = [], 0
        for p in pieces:
            a, b = max(lo, off), min(lo + rows, off + p.shape[0])
            if a < b:
                parts.append(p[a - off:b - off])
            off += p.shape[0]
        slabs.append(parts[0] if len(parts) == 1 else jnp.concatenate(parts, axis=0))
    return jnp.stack(slabs)


def _local_step(x, target, nb, s, w_in_t, conv_w, conv_b, norm_w, dt_bias, a_log, d_skip,
                ssm_norm_w, out_weights, final_norm_w, between=None):
    t = nb * s
    w_g_t = w_in_t[GATE_COL0:]
    w_dt_t = jnp.pad(w_in_t[DT_COL0:GATE_COL0], ((0, LANES - N_DT), (0, 0)))
    h, r1 = _rms_fwd(x, norm_w)
    proj = _matmul(h, w_in_t, nt=True, n=DT_COL0, out_dtype=BF16, name="proj_main")
    proj_g = _matmul(h, w_g_t, nt=True, out_dtype=BF16, name="proj_gate")
    dt_raw = _matmul(h, w_dt_t, nt=True, name="proj_dt")
    dtr = jnp.pad(dt_raw[:, :N_DT].reshape(t, SSD_GROUPS, SSD_HPG).transpose(1, 0, 2),
                  ((0, 0), (0, 0), (0, LANES - SSD_HPG)))
    bias_g, alog_g, dsk_g = _group_lanes(dt_bias), _group_lanes(a_log), _group_lanes(d_skip)

    o, tot = _attn_fwd(proj, nb, s)
    xbc = _conv_fwd(proj, conv_w, conv_b, nb, s)
    y, hs = _ssd_fwd(xbc, dtr, bias_g, alog_g, dsk_g, nb, s)
    ya_in, ys_in = _post_fwd(o, proj, y, ssm_norm_w)
    w_attn_out, w_ssm_out, w_o = out_weights(ys_in) if callable(out_weights) else out_weights
    ya = _matmul(ya_in, w_attn_out, name="attn_out")
    ys = _matmul(ys_in, w_ssm_out, name="ssm_out")
    merged = _merge_fwd(ya, ys, proj_g)
    out = _matmul(merged, w_o, name="out_proj")
    loss, dx2, d_final_w = _final(x, out, final_norm_w, target)

    dx2b = _bf(dx2)
    dmerged = _matmul(dx2b, w_o, nt=True, name="d_merged")
    d_w_o = _matmul(merged, dx2b, ta=True, name="d_w_o")
    dya, dys, dgate = _merge_bwd(dmerged, ya, ys, proj_g)
    d_w_attn_out = _matmul(ya_in, dya, ta=True, name="d_w_attn_out")
    d_w_ssm_out = _matmul(ys_in, dys, ta=True, name="d_w_ssm_out")
    dya_in = _matmul(dya, w_attn_out, nt=True, name="d_ya_in")
    dys_in = _matmul(dys, w_ssm_out, nt=True, name="d_ys_in")
    do, dza, dy, dzs, d_ssm_norm_w = _post_bwd(dya_in, o, proj, dys_in, y, ssm_norm_w)
    dq, dk, dv = _attn_bwd(proj, do, tot, nb, s)
    dxs_act, db_act, dc_act, ddtr, dbias_g, dalog_g, ddsk_g = _ssd_bwd(
        xbc, dtr, bias_g, alog_g, dsk_g, hs, dy, nb, s)
    dxs, dwx, dbx = _conv_bwd(proj, dxs_act, conv_w, conv_b, nb, s, col0=0, name="conv_bwd_x")
    dbm, dwb, dbb = _conv_bwd(proj, db_act, conv_w, conv_b, nb, s, col0=SSD_WIDTH, name="conv_bwd_b")
    dcm, dwc, dbc = _conv_bwd(proj, dc_act, conv_w, conv_b, nb, s, col0=SSD_WIDTH + 512, name="conv_bwd_c")
    dproj = jnp.concatenate([dq, dk, dv, dza, dzs, dxs, dbm, dcm], axis=1)
    ddt = jnp.pad(_bf(ddtr[:, :, :SSD_HPG].transpose(1, 0, 2).reshape(t, N_DT)), ((0, 0), (0, LANES - N_DT)))
    wt = dict(ta=True, out_dtype=BF16)
    d_w_in_t = _row_slabs(
        [_matmul(dproj, h, name="d_w_main", **wt), _matmul(ddt, h, name="d_w_dt", **wt)[:N_DT],
         _matmul(dgate, h, name="d_w_gate", **wt)], SHARD)
    grads = dict(
        w_in_t=d_w_in_t,
        conv_w=jnp.concatenate([dwx, dwb, dwc], axis=1), conv_b=jnp.concatenate([dbx, dbb, dbc], axis=1),
        dt_bias=_ungroup_lanes(dbias_g), a_log=_ungroup_lanes(dalog_g), d_skip=_ungroup_lanes(ddsk_g),
        ssm_norm_w=d_ssm_norm_w, w_attn_out=d_w_attn_out, w_ssm_out=d_w_ssm_out, w_o=d_w_o,
        final_norm_w=d_final_w)
    extra = None
    if between is not None:
        zero, extra = between(grads)
        ddt = ddt + _bf(zero[0:1, 0:1])

    dh = _matmul(ddt, w_dt_t, name="d_h_dt")
    dh = _matmul(dgate, w_g_t, add=dh, name="d_h_gate")
    dh = _matmul(dproj, w_in_t, add=dh, name="d_h", tk=DT_COL0 // 4)
    grad_x, grads["norm_w"] = _rms_bwd(dh, x, r1, norm_w, dx2)
    return loss, grad_x, grads, extra


N_CHIP = N_DEV // 2


def _pair_exchange(scatter, bcast, *, name):
    arrays = list(scatter) + list(bcast)
    ns, n = len(scatter), len(arrays)
    n_sem = ns * N_CHIP + (n - ns) * (N_DEV - 1)

    def body(*refs):
        ins, outs = refs[:n], refs[n:2 * n]
        send_sems, recv_sems, local_sems = refs[2 * n:]
        x, y, c = lax.axis_index("x"), lax.axis_index("y"), lax.axis_index("c")
        me = 4 * x + 2 * y + c
        copies, k = [], 0
        for a in range(ns):
            for q in range(N_CHIP):
                cp = pltpu.make_async_remote_copy(
                    src_ref=ins[a].at[2 * q + 1 - c], dst_ref=outs[a].at[q],
                    send_sem=send_sems.at[k], recv_sem=recv_sems.at[k],
                    device_id=(x, y, 1 - c), device_id_type=pl.DeviceIdType.MESH)
                cp.start()
                copies.append(cp)
                k += 1
        for a in range(ns, n):
            cp = pltpu.make_async_copy(ins[a], outs[a].at[me], local_sems.at[a - ns])
            cp.start()
            copies.append(cp)
            for r in range(1, N_DEV):
                tx = 1 - x if (r >> 2) & 1 else x
                ty = 1 - y if (r >> 1) & 1 else y
                tc = 1 - c if r & 1 else c
                cp = pltpu.make_async_remote_copy(
                    src_ref=ins[a], dst_ref=outs[a].at[me], send_sem=send_sems.at[k], recv_sem=recv_sems.at[k],
                    device_id=(tx, ty, tc), device_id_type=pl.DeviceIdType.MESH)
                cp.start()
                copies.append(cp)
                k += 1
        for cp in copies:
            cp.wait()

    out_shape = [jax.ShapeDtypeStruct((N_CHIP,) + v.shape[1:] if i < ns else (N_DEV,) + v.shape, v.dtype)
                 for i, v in enumerate(arrays)]
    return pl.pallas_call(
        body, in_specs=[pl.BlockSpec(memory_space=pl.ANY)] * n,
        out_specs=[pl.BlockSpec(memory_space=pl.ANY)] * n, out_shape=out_shape,
        scratch_shapes=[pltpu.SemaphoreType.DMA((n_sem,)), pltpu.SemaphoreType.DMA((n_sem,)),
                        pltpu.SemaphoreType.DMA((max(n - ns, 1),))],
        compiler_params=pltpu.CompilerParams(has_side_effects=True), name=name)(*arrays)


def _pair_add(mine, recv, core, *, name, tc=None):
    _, r, c = recv.shape
    tc = c if tc is None else tc
    assert c % tc == 0

    def body(core_ref, m_ref, r_ref, o_ref):
        o_ref[...] = (m_ref[...].astype(F32) + r_ref[...].astype(F32)).astype(o_ref.dtype)

    spec = pl.BlockSpec((None, r, tc), lambda q, i, core_ref: (q, 0, i))
    return pl.pallas_call(
        body, grid_spec=pltpu.PrefetchScalarGridSpec(
            num_scalar_prefetch=1, grid=(N_CHIP, c // tc),
            in_specs=[pl.BlockSpec((None, r, tc), lambda q, i, core_ref: (2 * q + core_ref[0], 0, i)), spec],
            out_specs=spec),
        out_shape=jax.ShapeDtypeStruct(recv.shape, recv.dtype),
        compiler_params=_params(("parallel", "parallel")), name=name)(core, mine, recv)


_HBM = pl.BlockSpec(memory_space=pltpu.HBM)
_SEM = pl.BlockSpec(memory_space=pltpu.SEMAPHORE)
_DATAFLOW = pltpu.SideEffectType.DATAFLOW_SIDE_EFFECTING
_PEERS = {"chips": N_CHIP - 1, "all": N_DEV - 1}


def _split_copies(pattern, srcs, lands, send_sems, recv_sems):
    x, y, c = lax.axis_index("x"), lax.axis_index("y"), lax.axis_index("c")
    per = _PEERS[pattern]
    copies = []
    for a in range(len(srcs)):
        for r in range(1, per + 1):
            if pattern == "chips":
                tx = 1 - x if (r >> 1) & 1 else x
                ty = 1 - y if r & 1 else y
                tc = c
                src, dst = srcs[a].at[2 * tx + ty], lands[a].at[2 * x + y]
            else:
                tx = 1 - x if (r >> 2) & 1 else x
                ty = 1 - y if (r >> 1) & 1 else y
                tc = 1 - c if r & 1 else c
                src, dst = srcs[a], lands[a].at[4 * x + 2 * y + c]
            copies.append(pltpu.make_async_remote_copy(
                src_ref=src, dst_ref=dst, send_sem=send_sems.at[a * per + r - 1],
                recv_sem=recv_sems.at[a * per + r - 1], device_id=(tx, ty, tc), device_id_type=pl.DeviceIdType.MESH))
    return copies


def _split_start(pattern, arrays, *, name):
    n = len(arrays)
    n_sem = n * _PEERS[pattern]

    def body(*refs):
        srcs, lands = refs[:n], refs[n:2 * n]
        send_sems, recv_sems = refs[2 * n], refs[2 * n + 1]
        zero = refs[-1]
        for cp in _split_copies(pattern, srcs, lands, send_sems, recv_sems):
            cp.start()
        zero[...] = jnp.zeros_like(zero)

    src_t = [pltpu.HBM(v.shape, v.dtype) for v in arrays]
    land_t = [pltpu.HBM(v.shape if pattern == "chips" else (N_DEV,) + v.shape, v.dtype) for v in arrays]
    outs = pl.pallas_call(
        body, name=name,
        out_shape=(pltpu.SemaphoreType.DMA((n_sem,)), pltpu.SemaphoreType.DMA((n_sem,)), *src_t, *land_t,
                   jax.ShapeDtypeStruct((8, LANES), F32)),
        in_specs=[_HBM] * (2 * n),
        out_specs=(_SEM, _SEM, *[_HBM] * (2 * n), pl.BlockSpec(memory_space=pltpu.VMEM)),
        input_output_aliases={i: 2 + i for i in range(2 * n)},
        compiler_params=pltpu.CompilerParams(has_side_effects=_DATAFLOW))(
            *[pltpu.with_memory_space_constraint(v, pltpu.HBM) for v in arrays],
            *[pltpu.with_memory_space_constraint(lax.empty(t.shape, t.dtype), pltpu.HBM) for t in land_t])
    return outs[0], outs[1], outs[2:2 + n], outs[2 + n:2 + 2 * n], outs[-1]


def _split_wait(pattern, send_sems, recv_sems, srcs, lands, after, *, name):
    n = len(srcs)

    def body(*refs):
        for cp in _split_copies(pattern, refs[:n], refs[n:2 * n], refs[2 * n], refs[2 * n + 1]):
            cp.wait_send()
            cp.wait_recv()

    outs = pl.pallas_call(
        body, name=name, out_shape=[pltpu.HBM(v.shape, v.dtype) for v in (*srcs, *lands)],
        in_specs=[_HBM] * (2 * n) + [_SEM, _SEM, pl.BlockSpec(memory_space=pl.ANY)],
        out_specs=[_HBM] * (2 * n), input_output_aliases={i: i for i in range(2 * n)},
        compiler_params=pltpu.CompilerParams(has_side_effects=_DATAFLOW))(
            *srcs, *lands, send_sems, recv_sems, after)
    return outs[:n], outs[n:]


def _own_slab(lands, srcs, index, whole):
    out = []
    for land, src in zip(lands, srcs):
        own = src[None] if whole else lax.dynamic_slice(src, (index, 0, 0), (1,) + src.shape[1:])
        out.append(lax.dynamic_update_slice(land, own, (index, 0, 0)))
    return out


def _gather(arrays, *, name):
    n = len(arrays)
    per = N_DEV - 1

    def body(*refs):
        ins, outs = refs[:n], refs[n:2 * n]
        send_sems, recv_sems, local_sems = refs[2 * n:]
        x, y, c = lax.axis_index("x"), lax.axis_index("y"), lax.axis_index("c")
        me, sibling = (x, y, c), (x, y, 1 - c)
        chips = [(1 - x, y), (x, 1 - y), (1 - x, 1 - y)]

        def slab(a, block):
            return outs[a].at[4 * block[0] + 2 * block[1] + block[2]]

        def copy(a, k, block, to, src=None):
            rows = slab(a, block)
            return pltpu.make_async_remote_copy(
                src_ref=rows if src is None else src, dst_ref=rows,
                send_sem=send_sems.at[a * per + k], recv_sem=recv_sems.at[a * per + k],
                device_id=to, device_id_type=pl.DeviceIdType.MESH)

        started = []
        for a in range(n):
            mine = pltpu.make_async_copy(ins[a], slab(a, me), local_sems.at[a])
            mine.start()
            started.append(mine)
        for a in range(n):
            first = [copy(a, 1 + j, me, (*chip, c), src=ins[a]) for j, chip in enumerate(chips)]
            first.append(copy(a, 0, me, sibling, src=ins[a]))
            for cp in first:
                cp.start()
            started += first
        for j, chip in enumerate(chips):
            for a in range(n):
                copy(a, 1 + j, (*chip, c), me).wait_recv()
                passed = copy(a, 4 + j, (*chip, c), sibling)
                passed.start()
                started.append(passed)
        for a in range(n):
            copy(a, 0, sibling, me).wait_recv()
            for j, chip in enumerate(chips):
                copy(a, 4 + j, (*chip, 1 - c), me).wait_recv()
        for cp in started[:n]:
            cp.wait()
        for cp in started[n:]:
            cp.wait_send()

    return pl.pallas_call(
        body, in_specs=[pl.BlockSpec(memory_space=pl.ANY)] * n,
        out_specs=[pl.BlockSpec(memory_space=pl.ANY)] * n,
        out_shape=[jax.ShapeDtypeStruct((N_DEV,) + v.shape, v.dtype) for v in arrays],
        scratch_shapes=[pltpu.SemaphoreType.DMA((n * per,)), pltpu.SemaphoreType.DMA((n * per,)),
                        pltpu.SemaphoreType.DMA((n,))],
        compiler_params=pltpu.CompilerParams(has_side_effects=True), name=name)(*arrays)


def _adamw(parts, w, m, v, *, name, tc=None):
    r, c = w.shape
    n_parts = parts.shape[0]
    tc = c if tc is None else tc
    assert c % tc == 0

    def body(p_ref, w_ref, m_ref, v_ref, g_ref, d_ref, nm_ref, nv_ref):
        g = p_ref[0].astype(F32)
        for d in range(1, n_parts):
            g = g + p_ref[d].astype(F32)
        mm = ADAM_B1 * m_ref[...] + (1.0 - ADAM_B1) * g
        vv = ADAM_B2 * v_ref[...] + (1.0 - ADAM_B2) * (g * g)
        m_hat = mm / (1.0 - ADAM_B1 ** ADAM_STEP)
        v_hat = vv / (1.0 - ADAM_B2 ** ADAM_STEP)
        g_ref[...] = g
        d_ref[...] = -ADAM_LR * (m_hat / (jnp.sqrt(v_hat) + ADAM_EPS) + ADAM_WD * w_ref[...])
        nm_ref[...] = mm
        nv_ref[...] = vv

    spec = pl.BlockSpec((r, tc), lambda i: (0, i))
    out = jax.ShapeDtypeStruct((r, c), F32)
    return pl.pallas_call(
        body, grid=(c // tc,),
        in_specs=[pl.BlockSpec((n_parts, r, tc), lambda i: (0, 0, i)), spec, spec, spec],
        out_specs=[spec] * 4, out_shape=[out] * 4,
        compiler_params=_params(("parallel",)), name=name)(parts, w, m, v)


SMALL = (("norm_w", D_MODEL), ("conv_b", CONV_DIM), ("dt_bias", N_DT), ("a_log", N_DT),
         ("d_skip", N_DT), ("ssm_norm_w", SSD_WIDTH), ("final_norm_w", D_MODEL))
SMALL_ROWS = 64
SHARD = D_PROJ // N_DEV


def _pack_small(vals):
    flat = jnp.concatenate([vals[k].reshape(-1).astype(F32) for k, _ in SMALL])
    return jnp.pad(flat, (0, SMALL_ROWS * LANES - flat.shape[0])).reshape(SMALL_ROWS, LANES)


def _unpack_small(packed, shapes):
    flat = packed.reshape(-1)
    out, off = {}, 0
    for k, size in SMALL:
        out[k] = flat[off:off + size].reshape(shapes[k])
        off += size
    return out


def kernel(x, norm_w, w_in, conv_w, conv_b, dt_bias, a_log, d_skip, ssm_norm_w, w_attn_out, w_ssm_out, w_o, final_norm_w, loss_target, m_norm_w, m_w_in, m_conv_w, m_conv_b, m_dt_bias, m_a_log, m_d_skip, m_ssm_norm_w, m_w_attn_out, m_w_ssm_out, m_w_o, m_final_norm_w, v_norm_w, v_w_in, v_conv_w, v_conv_b, v_dt_bias, v_a_log, v_d_skip, v_ssm_norm_w, v_w_attn_out, v_w_ssm_out, v_w_o, v_final_norm_w):
    nb, s, _ = x.shape
    t = nb * s
    weights = dict(norm_w=norm_w, w_in=w_in, conv_w=conv_w, conv_b=conv_b, dt_bias=dt_bias, a_log=a_log,
                   d_skip=d_skip, ssm_norm_w=ssm_norm_w, w_attn_out=w_attn_out, w_ssm_out=w_ssm_out,
                   w_o=w_o, final_norm_w=final_norm_w)
    moms = dict(norm_w=m_norm_w, w_in=m_w_in, conv_w=m_conv_w, conv_b=m_conv_b, dt_bias=m_dt_bias,
                a_log=m_a_log, d_skip=m_d_skip, ssm_norm_w=m_ssm_norm_w, w_attn_out=m_w_attn_out,
                w_ssm_out=m_w_ssm_out, w_o=m_w_o, final_norm_w=m_final_norm_w)
    vels = dict(norm_w=v_norm_w, w_in=v_w_in, conv_w=v_conv_w, conv_b=v_conv_b, dt_bias=v_dt_bias,
                a_log=v_a_log, d_skip=v_d_skip, ssm_norm_w=v_ssm_norm_w, w_attn_out=v_w_attn_out,
                w_ssm_out=v_w_ssm_out, w_o=v_w_o, final_norm_w=v_final_norm_w)

    tr_ = lambda a: jnp.transpose(a[0])
    g_in, g_conv = _gather([_bf(tr_(w_in)), conv_w[0]], name="gather_weights")
    conv_full = g_conv.transpose(1, 0, 2).reshape(CONV_K, CONV_DIM)
    o_send, o_recv, o_srcs, o_lands, o_zero = _split_start(
        "all", [_bf(w_attn_out[0]), _bf(w_ssm_out[0]), _bf(w_o[0])], name="gather_out_weights_start")

    core = lax.axis_index("c").astype(jnp.int32).reshape(1)
    chip = 2 * lax.axis_index("x") + lax.axis_index("y")
    me = 2 * chip + lax.axis_index("c")

    def out_weights(after):
        srcs, lands = _split_wait("all", o_send, o_recv, o_srcs, o_lands, after, name="gather_out_weights_wait")
        g_wa, g_ws, g_wo = _own_slab(lands, srcs, me, True)
        return (g_wa.reshape(SB_WIDTH, D_MODEL), g_ws.reshape(SSD_WIDTH, D_MODEL), g_wo.reshape(D_MODEL, D_MODEL))

    def send_weight_grads(grads):
        slabs = [grads["w_in_t"],
                 grads["conv_w"].reshape(CONV_K, N_DEV, CONV_DIM // N_DEV).transpose(1, 0, 2),
                 _bf(grads["w_attn_out"].reshape(N_DEV, SB_WIDTH // N_DEV, D_MODEL)),
                 _bf(grads["w_ssm_out"].reshape(N_DEV, SSD_WIDTH // N_DEV, D_MODEL)),
                 _bf(grads["w_o"].reshape(N_DEV, D_MODEL // N_DEV, D_MODEL))]
        recv = _pair_exchange(slabs, [], name="exchange_pairs")
        names = ("w_in", "conv_w", "w_attn_out", "w_ssm_out", "w_o")
        sums = [_pair_add(a, b, core, name="pair_add_" + k, tc=4 * LANES if k == "w_in" else None)
                for k, a, b in zip(names, slabs, recv)]
        send_sems, recv_sems, srcs, lands, zero = _split_start("chips", sums, name="exchange_chips_start")
        return zero, (send_sems, recv_sems, srcs, lands)

    loss_part, grad_x, grads, (send_sems, recv_sems, srcs, lands) = _local_step(
        x.reshape(t, D_MODEL), loss_target.reshape(t, D_MODEL), nb, s, g_in.reshape(D_PROJ, D_MODEL),
        conv_full, conv_b, norm_w + o_zero[0:1, 0:1], dt_bias, a_log, d_skip, ssm_norm_w, out_weights,
        final_norm_w.reshape(1, D_MODEL), between=send_weight_grads)

    (p_small,) = _pair_exchange([], [_pack_small(grads)], name="exchange_small")
    srcs, lands = _split_wait("chips", send_sems, recv_sems, srcs, lands, grad_x, name="exchange_chips_wait")
    p_in, p_conv, p_wa, p_ws, p_wo = _own_slab(lands, srcs, chip, False)

    res = {}
    res["w_in"] = [a.T for a in _adamw(p_in, tr_(w_in), tr_(m_w_in), tr_(v_w_in), name="adamw_w_in", tc=2 * LANES)]
    res["conv_w"] = _adamw(p_conv, conv_w[0], m_conv_w[0], v_conv_w[0], name="adamw_conv_w")
    res["w_attn_out"] = _adamw(p_wa, w_attn_out[0], m_w_attn_out[0], v_w_attn_out[0], name="adamw_w_attn_out")
    res["w_ssm_out"] = _adamw(p_ws, w_ssm_out[0], m_w_ssm_out[0], v_w_ssm_out[0], name="adamw_w_ssm_out")
    res["w_o"] = _adamw(p_wo, w_o[0], m_w_o[0], v_w_o[0], name="adamw_w_o")
    sm = _adamw(p_small, _pack_small(weights), _pack_small(moms), _pack_small(vels), name="adamw_small")
    shapes = {k: weights[k].shape for k, _ in SMALL}
    sm = [_unpack_small(v, shapes) for v in sm]
    for k, _ in SMALL:
        res[k] = tuple(part[k] for part in sm)

    order = ("norm_w", "w_in", "conv_w", "conv_b", "dt_bias", "a_log", "d_skip", "ssm_norm_w",
             "w_attn_out", "w_ssm_out", "w_o", "final_norm_w")
    loss = lax.psum(loss_part[0, 0], ("x", "y", "c"))
    outs = [loss, grad_x.reshape(nb, s, D_MODEL)]
    for i in range(4):
        outs += [res[k][i].reshape(weights[k].shape) for k in order]
    return tuple(outs)
```

```python
import functools

import jax
import jax.numpy as jnp
from jax import lax
from jax.experimental import pallas as pl
from jax.experimental.pallas import tpu as pltpu

F32 = jnp.float32
BF16 = jnp.bfloat16

D_MODEL = 1024
SB_WIDTH = 1024
HEAD_DIM = 64
SSD_WIDTH = 2048
SSD_GROUPS = 4
SSD_HPG = 8
SSD_STATE = 128
SSD_CHUNK = 128
CONV_K = 4
CONV_DIM = 3072
N_DT = 32
D_PROJ = 11296
DT_COL0 = 9216
GATE_COL0 = DT_COL0 + N_DT
EPS = 1e-6
N_DEV = 8

ADAM_LR = 0.001
ADAM_B1 = 0.9
ADAM_B2 = 0.999
ADAM_EPS = 1e-08
ADAM_WD = 0.01
ADAM_STEP = 10

LANES = 128
VMEM_LIMIT = 56 * 1024 * 1024

CB_Q, CB_K, CB_V, CB_ZA, CB_ZS, CB_XBC = 0, 8, 16, 24, 32, 48

NT_DIMS = (((1,), (1,)), ((), ()))
TN_DIMS = (((0,), (0,)), ((), ()))


def _params(sem):
    return pltpu.CompilerParams(dimension_semantics=sem, vmem_limit_bytes=VMEM_LIMIT)


def _dot(a, b):
    return jnp.dot(a, b, preferred_element_type=F32)


def _dot_nt(a, b):
    return lax.dot_general(a, b, NT_DIMS, preferred_element_type=F32)


def _dot_tn(a, b):
    return lax.dot_general(a, b, TN_DIMS, preferred_element_type=F32)


def _bf(a):
    return a.astype(BF16)


def _sigmoid(x):
    return 0.5 + 0.5 * jnp.tanh(0.5 * x)


def _softplus(x):
    return jnp.maximum(x, 0.0) + jnp.log(1.0 + jnp.exp(-jnp.abs(x)))


def _trunc_split(a):
    bits = lax.bitcast_convert_type(a, jnp.uint32) & jnp.uint32(0xFFFF0000)
    hi = lax.bitcast_convert_type(bits, F32)
    return _bf(hi), _bf(a - hi)


MM_TILE = 1024
MM_TK = 2048


def _matmul(a, b, *, name, nt=False, ta=False, add=None, out_dtype=F32, tm=MM_TILE, tn=MM_TILE, tk=MM_TK, n=None):
    k, m = a.shape if ta else a.shape[::-1]
    n = (b.shape[0] if nt else b.shape[1]) if n is None else n
    tm, tn, tk = min(tm, m), min(tn, n), min(tk, k)
    assert m % tm == 0 and n % tn == 0 and k % tk == 0, (name, a.shape, b.shape)
    assert not (ta and nt)
    nk = k // tk
    has_add = add is not None

    def body(*refs):
        if has_add:
            a_ref, b_ref, add_ref, o_ref, acc_ref = refs
        else:
            a_ref, b_ref, o_ref, acc_ref = refs
        kk = pl.program_id(2)
        av, bv = _bf(a_ref[...]), _bf(b_ref[...])
        d = _dot_nt(av, bv) if nt else _dot_tn(av, bv) if ta else _dot(av, bv)
        if nk == 1:
            o_ref[...] = ((d + add_ref[...]) if has_add else d).astype(out_dtype)
            return

        @pl.when(kk == 0)
        def _():
            acc_ref[...] = jnp.zeros_like(acc_ref)

        acc_ref[...] += d

        @pl.when(kk == nk - 1)
        def _():
            r = acc_ref[...]
            if has_add:
                r = r + add_ref[...]
            o_ref[...] = r.astype(out_dtype)

    in_specs = [pl.BlockSpec((tk, tm), lambda j, i, kk: (kk, i)) if ta
                else pl.BlockSpec((tm, tk), lambda j, i, kk: (i, kk)),
                pl.BlockSpec((tn, tk), lambda j, i, kk: (j, kk)) if nt
                else pl.BlockSpec((tk, tn), lambda j, i, kk: (kk, j))]
    args = [a, b]
    if has_add:
        in_specs.append(pl.BlockSpec((tm, tn), lambda j, i, kk: (i, j)))
        args.append(add)
    return pl.pallas_call(
        body, grid=(n // tn, m // tm, nk), in_specs=in_specs,
        out_specs=pl.BlockSpec((tm, tn), lambda j, i, kk: (i, j)),
        out_shape=jax.ShapeDtypeStruct((m, n), out_dtype),
        scratch_shapes=[pltpu.VMEM((tm, tn), F32)],
        compiler_params=_params(("parallel", "parallel", "arbitrary")), name=name)(*args)


def _rms_fwd(x, w, *, tm=512):
    t = x.shape[0]

    def body(x_ref, w_ref, h_ref, r_ref):
        xv = x_ref[...]
        r = lax.rsqrt(jnp.mean(xv * xv, axis=-1, keepdims=True) + EPS)
        h_ref[...] = _bf(xv * r * w_ref[...])
        r_ref[...] = r

    row = lambda i: (i, 0)
    return pl.pallas_call(
        body, grid=(t // tm,),
        in_specs=[pl.BlockSpec((tm, D_MODEL), row), pl.BlockSpec((1, D_MODEL), lambda i: (0, 0))],
        out_specs=[pl.BlockSpec((tm, D_MODEL), row), pl.BlockSpec((tm, 1), row)],
        out_shape=[jax.ShapeDtypeStruct((t, D_MODEL), BF16), jax.ShapeDtypeStruct((t, 1), F32)],
        compiler_params=_params(("parallel",)), name="rms_fwd")(x, w)


def _rms_bwd(dh, x, r, w, dres, *, tm=512):
    t = x.shape[0]

    def body(dh_ref, x_ref, r_ref, w_ref, dres_ref, dx_ref, dw_ref):
        @pl.when(pl.program_id(0) == 0)
        def _():
            dw_ref[...] = jnp.zeros_like(dw_ref)

        nrm = x_ref[...] * r_ref[...]
        dhv = dh_ref[...]
        dw_ref[...] += jnp.sum(dhv * nrm, axis=0, keepdims=True)
        dn = dhv * w_ref[...]
        dx = r_ref[...] * (dn - nrm * jnp.mean(dn * nrm, axis=-1, keepdims=True))
        dx_ref[...] = dx + dres_ref[...]

    row = lambda i: (i, 0)
    fix = lambda i: (0, 0)
    return pl.pallas_call(
        body, grid=(t // tm,),
        in_specs=[pl.BlockSpec((tm, D_MODEL), row), pl.BlockSpec((tm, D_MODEL), row),
                  pl.BlockSpec((tm, 1), row), pl.BlockSpec((1, D_MODEL), fix),
                  pl.BlockSpec((tm, D_MODEL), row)],
        out_specs=[pl.BlockSpec((tm, D_MODEL), row), pl.BlockSpec((1, D_MODEL), fix)],
        out_shape=[jax.ShapeDtypeStruct((t, D_MODEL), F32), jax.ShapeDtypeStruct((1, D_MODEL), F32)],
        compiler_params=_params(("arbitrary",)), name="rms_bwd")(dh, x, r, w, dres)


ATT_BLK = 256


def _attn_masks(blk):
    r_io = lax.broadcasted_iota(jnp.int32, (blk, blk), 0)
    c_io = lax.broadcasted_iota(jnp.int32, (blk, blk), 1)
    return r_io, c_io


def _attn_fwd(proj, nb, s):
    blk = min(ATT_BLK, s)
    nq = s // blk
    scale = HEAD_DIM ** -0.5

    def body(q_ref, k_ref, v_ref, o_ref, tot_ref, q0_s, q1_s, kb_s, v0_s, v1_s, z_s, a_s, acc_s, cr_s):
        head0 = lax.broadcasted_iota(jnp.int32, (s, LANES), 1) < HEAD_DIM
        lane2 = lax.broadcasted_iota(jnp.int32, (blk, 2), 1)
        r_io, c_io = _attn_masks(blk)
        tri = c_io < r_io
        tri2 = jnp.concatenate([tri, tri], axis=0)
        u_gt = _bf(r_io > c_io)
        u2 = jnp.concatenate([u_gt, u_gt], axis=0)
        qv = q_ref[...] * scale
        q0_s[...] = _bf(jnp.where(head0, qv, 0.0))
        q1_s[...] = _bf(jnp.where(head0, 0.0, qv))
        kb_s[...] = _bf(k_ref[...])
        vv = v_ref[...]
        v0_s[...] = _bf(jnp.where(head0, vv, 0.0))
        v1_s[...] = _bf(jnp.where(head0, 0.0, vv))

        def scores(qcat, kblk, slot):
            k0 = pl.multiple_of(jnp.maximum(kblk, 0) * blk, blk)
            z_s[slot] = _dot_nt(qcat, kb_s[pl.ds(k0, blk), :])

        def weights(slot, diag, zslot=None):
            z = z_s[slot if zslot is None else zslot]
            cr = cr_s[...]
            sp = _softplus(z)
            spm = jnp.where(tri2, sp, 0.0) if diag else sp
            hi, lo = _trunc_split(spm)
            ps = _dot(jnp.concatenate([hi, lo], axis=1), u2)
            a = jnp.exp(z - (sp + ps + cr))
            if diag:
                a = jnp.where(tri2, a, 0.0)
            a_s[slot] = _bf(a)
            cr_s[...] = cr + ps[:, 0:1] + spm[:, 0:1]

        def apply(slot, k0):
            a = a_s[slot]
            vcat = jnp.concatenate([v0_s[pl.ds(k0, blk), :], v1_s[pl.ds(k0, blk), :]], axis=0)
            acc_s[...] += _dot(jnp.concatenate([a[:blk], a[blk:]], axis=1), vcat)

        def queries(qi):
            q0 = pl.multiple_of(qi * blk, blk)
            return jnp.concatenate([q0_s[pl.ds(q0, blk), :], q1_s[pl.ds(q0, blk), :]], axis=0)

        def first_scores(qi):
            qcat = queries(qi)
            scores(qcat, qi, 2)
            scores(qcat, qi - 1, 3)

        first_scores(jnp.int32(0))

        def qblock(qi, _):
            q0 = pl.multiple_of(qi * blk, blk)
            qcat = queries(qi)
            z_s[1] = z_s[3]
            cr_s[...] = jnp.zeros_like(cr_s)
            acc_s[...] = jnp.zeros_like(acc_s)
            weights(0, True, zslot=2)
            first_scores(jnp.minimum(qi + 1, nq - 1))

            def step(j, slot):
                scores(qcat, qi - j - 1, 1 - slot)
                weights(slot, False)
                apply(1 - slot, pl.multiple_of((qi - j + 1) * blk, blk))

            def pair(i, _):
                step(2 * i + 1, 1)
                step(2 * i + 2, 0)
                return 0

            lax.fori_loop(0, qi // 2, pair, 0)

            @pl.when(qi % 2 == 1)
            def _():
                step(qi, 1)

            apply(qi & 1, 0)
            o_ref[pl.ds(q0, blk), :] = acc_s[...]
            cr = cr_s[...]
            tot_ref[pl.ds(q0, blk), :] = jnp.where(lane2 == 0, cr[:blk], cr[blk:])
            return 0

        lax.fori_loop(0, nq, qblock, 0)

    t = nb * s
    return pl.pallas_call(
        body, grid=(nb, 8),
        in_specs=[pl.BlockSpec((s, LANES), lambda b, p: (b, CB_Q + p)),
                  pl.BlockSpec((s, LANES), lambda b, p: (b, CB_K + p)),
                  pl.BlockSpec((s, LANES), lambda b, p: (b, CB_V + p))],
        out_specs=[pl.BlockSpec((s, LANES), lambda b, p: (b, p)),
                   pl.BlockSpec((None, None, s, 2), lambda b, p: (b, p, 0, 0))],
        out_shape=[jax.ShapeDtypeStruct((t, SB_WIDTH), F32),
                   jax.ShapeDtypeStruct((nb, 8, s, 2), F32)],
        scratch_shapes=[pltpu.VMEM((s, LANES), BF16)] * 5 + [
            pltpu.VMEM((4, 2 * blk, blk), F32), pltpu.VMEM((2, 2 * blk, blk), BF16),
            pltpu.VMEM((blk, LANES), F32), pltpu.VMEM((2 * blk, 1), F32)],
        compiler_params=_params(("parallel", "parallel")), name="attn_fwd")(proj, proj, proj)


def _attn_bwd(proj, do, tot, nb, s):
    blk = min(ATT_BLK, s)
    nq = s // blk
    scale = HEAD_DIM ** -0.5

    def body(q_ref, k_ref, v_ref, do_ref, tot_ref, dq_ref, dk_ref, dv_ref,
             q0_s, q1_s, kb_s, k0_s, k1_s, vb_s, d0_s, d1_s, z_s, da_s, dz_s, a_s, dk_acc, dv_acc,
             dq_acc, cn_s, cw_s):
        head0 = lax.broadcasted_iota(jnp.int32, (s, LANES), 1) < HEAD_DIM
        r_io, c_io = _attn_masks(blk)
        tri = c_io < r_io
        tri2 = jnp.concatenate([tri, tri], axis=0)
        u_le = _bf(r_io <= c_io)
        u2 = jnp.concatenate([u_le, u_le], axis=0)
        u_lt = _bf(r_io < c_io)
        qv = q_ref[...] * scale
        q0_s[...] = _bf(jnp.where(head0, qv, 0.0))
        q1_s[...] = _bf(jnp.where(head0, 0.0, qv))
        kv = k_ref[...]
        kb_s[...] = _bf(kv)
        k0_s[...] = _bf(jnp.where(head0, kv, 0.0))
        k1_s[...] = _bf(jnp.where(head0, 0.0, kv))
        vb_s[...] = _bf(v_ref[...])
        dov = do_ref[...]
        d0_s[...] = _bf(jnp.where(head0, dov, 0.0))
        d1_s[...] = _bf(jnp.where(head0, 0.0, dov))
        dk_acc[...] = jnp.zeros_like(dk_acc)
        dv_acc[...] = jnp.zeros_like(dv_acc)

        def qblock(qi, _):
            q0 = pl.multiple_of(qi * blk, blk)
            qs_ = pl.ds(q0, blk)
            qhs = (q0_s[qs_, :], q1_s[qs_, :])
            dhs = (d0_s[qs_, :], d1_s[qs_, :])
            qcat = jnp.concatenate(qhs, axis=0)
            dcat = jnp.concatenate(dhs, axis=0)
            totf = tot_ref[qs_, :]

            def scores(kblk, slot):
                ks_ = pl.ds(pl.multiple_of(kblk * blk, blk), blk)
                z_s[slot] = _dot_nt(qcat, kb_s[ks_, :])
                da_s[slot] = _dot_nt(dcat, vb_s[ks_, :])

            def grads(slot, diag):
                cn, cw = cn_s[...], cw_s[...]
                z = z_s[slot]
                sp = _softplus(z)
                lb = z - sp
                spm = jnp.where(tri2, sp, 0.0) if diag else sp
                hi, lo = _trunc_split(spm)
                pin = _dot(jnp.concatenate([hi, lo], axis=1), u2)
                a = jnp.exp(lb + (pin + cn))
                if diag:
                    a = jnp.where(tri2, a, 0.0)
                w = a * da_s[slot]
                wex = _dot(_bf(w), u_lt)
                dz = w - jnp.exp(lb) * (w + (wex + cw))
                if diag:
                    dz = jnp.where(tri2, dz, 0.0)
                dz_s[slot] = _bf(dz)
                a_s[slot] = _bf(a)
                cn_s[...] = cn + pin[:, blk - 1:blk]
                cw_s[...] = cw + wex[:, blk - 1:blk] + w[:, blk - 1:blk]

            def apply(slot, kblk):
                ks_ = pl.ds(pl.multiple_of(kblk * blk, blk), blk)
                dz = dz_s[slot]
                dq_acc[...] += _dot(jnp.concatenate([dz[:blk], dz[blk:]], axis=1),
                                    jnp.concatenate([k0_s[ks_, :], k1_s[ks_, :]], axis=0))
                dk_acc[ks_, :] += _dot_tn(dz, qcat)
                dv_acc[ks_, :] += _dot_tn(a_s[slot], dcat)

            def step(j, slot, diag):
                if not diag:
                    scores(j + 1, 1 - slot)
                grads(slot, diag)
                apply(1 - slot, jnp.maximum(j - 1, 0))

            dz_s[1] = jnp.zeros_like(dz_s[1])
            a_s[1] = jnp.zeros_like(a_s[1])
            scores(0, 0)
            dq_acc[...] = jnp.zeros_like(dq_acc)
            cn_s[...] = -jnp.concatenate([totf[:, 0:1], totf[:, 1:2]], axis=0)
            cw_s[...] = jnp.zeros_like(cw_s)

            def pair(i, _):
                step(2 * i, 0, False)
                step(2 * i + 1, 1, False)
                return 0

            lax.fori_loop(0, qi // 2, pair, 0)

            @pl.when(qi % 2 == 1)
            def _():
                step(qi - 1, 0, False)
                step(qi, 1, True)

            @pl.when(qi % 2 == 0)
            def _():
                step(qi, 0, True)

            apply(qi & 1, qi)
            dq_ref[qs_, :] = _bf(dq_acc[...] * scale)
            return 0

        lax.fori_loop(0, nq, qblock, 0)
        dk_ref[...] = _bf(dk_acc[...])
        dv_ref[...] = _bf(dv_acc[...])

    t = nb * s
    hp = lambda b, p: (b, p)
    out = jax.ShapeDtypeStruct((t, SB_WIDTH), BF16)
    return pl.pallas_call(
        body, grid=(nb, 8),
        in_specs=[pl.BlockSpec((s, LANES), lambda b, p: (b, CB_Q + p)),
                  pl.BlockSpec((s, LANES), lambda b, p: (b, CB_K + p)),
                  pl.BlockSpec((s, LANES), lambda b, p: (b, CB_V + p)),
                  pl.BlockSpec((s, LANES), hp),
                  pl.BlockSpec((None, None, s, 2), lambda b, p: (b, p, 0, 0))],
        out_specs=[pl.BlockSpec((s, LANES), hp)] * 3,
        out_shape=[out, out, out],
        scratch_shapes=[pltpu.VMEM((s, LANES), BF16)] * 8 + [
            pltpu.VMEM((2, 2 * blk, blk), F32), pltpu.VMEM((2, 2 * blk, blk), F32),
            pltpu.VMEM((2, 2 * blk, blk), BF16), pltpu.VMEM((2, 2 * blk, blk), BF16),
            pltpu.VMEM((s, LANES), F32), pltpu.VMEM((s, LANES), F32),
            pltpu.VMEM((blk, LANES), F32), pltpu.VMEM((2 * blk, 1), F32), pltpu.VMEM((2 * blk, 1), F32)],
        compiler_params=_params(("parallel", "parallel")), name="attn_bwd")(proj, proj, proj, do, tot)


CONV_TC = 256
CONV_ROWS = 64
HALO = 8


def _conv_rows(ws, w_ref, b_ref, n):
    taps = [ws[HALO - kk:HALO - kk + n, :] for kk in range(CONV_K)]
    y = b_ref[...] + w_ref[CONV_K - 1:CONV_K, :] * taps[0]
    for kk in range(1, CONV_K):
        y = y + w_ref[CONV_K - 1 - kk:CONV_K - kk, :] * taps[kk]
    return y, taps


def _conv_fwd(proj, conv_w, conv_b, nb, s):
    tc, rows = CONV_TC, min(CONV_ROWS, s)
    cb0 = CB_XBC * LANES // tc

    def body(x_ref, w_ref, b_ref, o_ref, xp, ws):
        xp[0:HALO, :] = jnp.zeros((HALO, tc), F32)
        xp[HALO:HALO + s, :] = x_ref[...].astype(F32)

        def chunk(i, _):
            r0 = pl.multiple_of(i * rows, rows)
            ws[...] = xp[pl.ds(r0, rows + HALO), :]
            y, _ = _conv_rows(ws, w_ref, b_ref, rows)
            o_ref[pl.ds(r0, rows), :] = y * _sigmoid(y)
            return 0

        lax.fori_loop(0, s // rows, chunk, 0)

    t = nb * s
    return pl.pallas_call(
        body, grid=(nb, CONV_DIM // tc),
        in_specs=[pl.BlockSpec((s, tc), lambda b, c: (b, cb0 + c)),
                  pl.BlockSpec((CONV_K, tc), lambda b, c: (0, c)),
                  pl.BlockSpec((1, tc), lambda b, c: (0, c))],
        out_specs=pl.BlockSpec((s, tc), lambda b, c: (b, c)),
        out_shape=jax.ShapeDtypeStruct((t, CONV_DIM), F32),
        scratch_shapes=[pltpu.VMEM((s + HALO, tc), F32), pltpu.VMEM((rows + HALO, tc), F32)],
        compiler_params=_params(("parallel", "parallel")), name="conv_fwd")(proj, conv_w, conv_b)


def _conv_bwd(proj, dact, conv_w, conv_b, nb, s, *, col0, name):
    width = dact.shape[1]
    tc, rows = min(CONV_TC, width), min(CONV_ROWS, s)
    cb0 = (CB_XBC * LANES + col0) // tc
    wb0 = col0 // tc

    def body(x_ref, d_ref, w_ref, b_ref, dx_ref, dw_ref, db_ref, xp, dp, dys, ws):
        @pl.when(pl.program_id(1) == 0)
        def _():
            dw_ref[...] = jnp.zeros_like(dw_ref)
            db_ref[...] = jnp.zeros_like(db_ref)

        n = rows + HALO
        pad = jnp.zeros((HALO, tc), F32)
        xp[0:HALO, :] = pad
        xp[HALO:HALO + s, :] = x_ref[...].astype(F32)
        xp[HALO + s:HALO + s + HALO, :] = pad
        dp[0:s, :] = d_ref[...]
        dp[s:s + HALO, :] = pad

        def chunk(i, acc):
            r0 = pl.multiple_of(i * rows, rows)
            ws[...] = xp[pl.ds(r0, n + HALO), :]
            y, taps = _conv_rows(ws, w_ref, b_ref, n)
            sg = _sigmoid(y)
            dy = dp[pl.ds(r0, n), :] * (sg * (1.0 + y * (1.0 - sg)))
            dys[...] = dy
            dx = w_ref[CONV_K - 1:CONV_K, :] * dy[0:rows]
            for kk in range(1, CONV_K):
                dx = dx + w_ref[CONV_K - 1 - kk:CONV_K - kk, :] * dys[kk:kk + rows, :]
            dx_ref[pl.ds(r0, rows), :] = _bf(dx)
            dyc = dy[0:rows]
            new = [acc[0] + jnp.sum(dyc, axis=0, keepdims=True)]
            for kk in range(CONV_K):
                new.append(acc[1 + kk] + jnp.sum(dyc * taps[kk][0:rows], axis=0, keepdims=True))
            return tuple(new)

        acc = lax.fori_loop(0, s // rows, chunk, (jnp.zeros((1, tc), F32),) * (1 + CONV_K))
        db_ref[...] += acc[0]
        for kk in range(CONV_K):
            dw_ref[CONV_K - 1 - kk:CONV_K - kk, :] += acc[1 + kk]

    t = nb * s
    return pl.pallas_call(
        body, grid=(width // tc, nb),
        in_specs=[pl.BlockSpec((s, tc), lambda c, b: (b, cb0 + c)),
                  pl.BlockSpec((s, tc), lambda c, b: (b, c)),
                  pl.BlockSpec((CONV_K, tc), lambda c, b: (0, wb0 + c)),
                  pl.BlockSpec((1, tc), lambda c, b: (0, wb0 + c))],
        out_specs=[pl.BlockSpec((s, tc), lambda c, b: (b, c)),
                   pl.BlockSpec((CONV_K, tc), lambda c, b: (0, c)),
                   pl.BlockSpec((1, tc), lambda c, b: (0, c))],
        out_shape=[jax.ShapeDtypeStruct((t, width), BF16),
                   jax.ShapeDtypeStruct((CONV_K, width), F32),
                   jax.ShapeDtypeStruct((1, width), F32)],
        scratch_shapes=[pltpu.VMEM((s + 2 * HALO, tc), F32), pltpu.VMEM((s + HALO, tc), F32),
                        pltpu.VMEM((rows + HALO, tc), F32), pltpu.VMEM((rows + 2 * HALO, tc), F32)],
        compiler_params=_params(("parallel", "arbitrary")), name=name)(proj, dact, conv_w, conv_b)


L = SSD_CHUNK
N_PAIR = SSD_HPG // 2


def _pair_lanes(v0, v1, lane):
    return jnp.where(lane < HEAD_DIM, v0, v1)


def _onehot(shape, row_shift, col_shift):
    row = lax.broadcasted_iota(jnp.int32, shape, 0)
    col = lax.broadcasted_iota(jnp.int32, shape, 1)
    return _bf(lax.shift_right_logical(row, row_shift) == lax.shift_right_logical(col, col_shift))


def _ssd_tables():
    sp = _onehot((LANES, N_PAIR * LANES), 0, 6)
    sf = _onehot((LANES, SSD_HPG * LANES), 0, 7)
    return (jnp.concatenate([sp] * 3, axis=0), jnp.concatenate([sf] * 3, axis=0),
            _onehot((N_PAIR * LANES, LANES), 6, 0), _onehot((SSD_HPG * LANES, LANES), 7, 0))


def _split3(a):
    a1 = _bf(a)
    r1 = a - a1.astype(F32)
    a2 = _bf(r1)
    return jnp.concatenate([a1, a2, _bf(r1 - a2.astype(F32))], axis=1)


def _split2(a):
    hi, lo = _trunc_split(a)
    return jnp.concatenate([hi, lo], axis=1)


def _tri_dot(tri_bf, a):
    a1 = _bf(a)
    r1 = a - a1.astype(F32)
    a2 = _bf(r1)
    parts = jnp.concatenate([a1, a2, _bf(r1 - a2.astype(F32))], axis=0)
    return _dot(jnp.concatenate([tri_bf] * 3, axis=1), parts)


HEAD_W = 4 * LANES + 2 * N_PAIR * LANES + SSD_HPG * LANES


def _ssd_head(dtr_ref, bias_ref, a_neg, c, tri_bf, spread_pair, spread_full, head_s, slot):
    raw = dtr_ref[pl.ds(pl.multiple_of(c * L, L), L), :] + bias_ref[...]
    dt = _softplus(raw)
    acs = _tri_dot(tri_bf, dt * a_neg)
    acs3 = _split3(acs)
    both = _dot(jnp.concatenate([_split3(dt), acs3], axis=0), spread_pair)
    parts = (raw, dt, acs, acs.T, both[:L], both[L:], _dot(acs3, spread_full))
    off = 0
    for p in parts:
        head_s[slot, :, off:off + p.shape[1]] = p
        off += p.shape[1]


def _ssd_head_load(head_s, slot):
    out, off = [], 0
    for w in (LANES, LANES, LANES, L, N_PAIR * LANES, N_PAIR * LANES, SSD_HPG * LANES):
        out.append(head_s[slot, :, off:off + w])
        off += w
    return out


def _ssd_fwd(xbc, dtr, bias, alog, dskip, nb, s):
    nc = s // L
    t = nb * s

    def body(x_ref, b_ref, c_ref, dtr_ref, bias_ref, alog_ref, dsk_ref, y_ref, hs_ref, h_scr, head_s):
        lane = lax.broadcasted_iota(jnp.int32, (L, LANES), 1)
        lane1 = lax.broadcasted_iota(jnp.int32, (1, LANES), 1)
        rowc = lax.broadcasted_iota(jnp.int32, (L, 1), 0)
        rr = lax.broadcasted_iota(jnp.int32, (L, L), 0)
        cc = lax.broadcasted_iota(jnp.int32, (L, L), 1)
        causal = rr >= cc
        h_scr[...] = jnp.zeros_like(h_scr)
        dsk = dsk_ref[...]
        spread_pair, spread_full, _, _ = _ssd_tables()
        a_neg = -jnp.exp(alog_ref[...])
        head = functools.partial(_ssd_head, dtr_ref, bias_ref, a_neg, tri_bf=_bf(cc <= rr),
                                 spread_pair=spread_pair, spread_full=spread_full, head_s=head_s)
        head(jnp.int32(0), slot=0)

        def chunk(c, slot):
            r0 = pl.multiple_of(c * L, L)
            head(jnp.minimum(c + 1, nc - 1), slot=1 - slot)
            _, _, acs, acs_t, dt_all, acs_all, acs_full = _ssd_head_load(head_s, slot)
            bb = _bf(b_ref[pl.ds(r0, L), :])
            cb = _bf(c_ref[pl.ds(r0, L), :])
            gm = _dot_nt(cb, bb)
            for pr in range(N_PAIR):
                j0, j1 = 2 * pr, 2 * pr + 1
                ms = []
                for j in (j0, j1):
                    seg = acs_full[:, j * LANES:(j + 1) * LANES] - acs_t[j:j + 1, :]
                    ms.append(_bf(gm * jnp.exp(jnp.where(causal, seg, -1e30))))
                x = x_ref[pl.ds(r0, L), pr * LANES:(pr + 1) * LANES]
                dt_p = dt_all[:, pr * LANES:(pr + 1) * LANES]
                acs_p = acs_all[:, pr * LANES:(pr + 1) * LANES]
                last_p = acs_p[L - 1:L, :]
                d_p = _pair_lanes(dsk[:, j0:j0 + 1], dsk[:, j1:j1 + 1], lane1)
                xd = x * dt_p
                xdb = _bf(xd)
                yds = _dot(jnp.concatenate(ms, axis=0), xdb)
                yd = _pair_lanes(yds[:L], yds[L:], lane)
                hp = h_scr[pr]
                yo = _dot_nt(cb, _bf(hp)) * jnp.exp(acs_p)
                y_ref[pl.ds(r0, L), pr * LANES:(pr + 1) * LANES] = yd + yo + x * d_p
                hs_ref[c, pr] = hp
                sp = _dot_tn(_bf(xd * jnp.exp(last_p - acs_p)), bb)
                cd = jnp.where(rowc < HEAD_DIM, jnp.exp(acs[L - 1:L, j0:j0 + 1]),
                               jnp.exp(acs[L - 1:L, j1:j1 + 1]))
                h_scr[pr] = hp * cd + sp

        def two_chunks(i, _):
            chunk(2 * i, 0)
            chunk(2 * i + 1, 1)
            return 0

        lax.fori_loop(0, nc // 2, two_chunks, 0)

    assert nc % 2 == 0
    grp = lambda b, g: (g, 0, 0)
    return pl.pallas_call(
        body, grid=(nb, SSD_GROUPS),
        in_specs=[pl.BlockSpec((s, 4 * LANES), lambda b, g: (b, g)),
                  pl.BlockSpec((s, LANES), lambda b, g: (b, 16 + g)),
                  pl.BlockSpec((s, LANES), lambda b, g: (b, 20 + g)),
                  pl.BlockSpec((None, s, LANES), lambda b, g: (g, b, 0)),
                  pl.BlockSpec((None, 1, LANES), grp), pl.BlockSpec((None, 1, LANES), grp),
                  pl.BlockSpec((None, 1, LANES), grp)],
        out_specs=[pl.BlockSpec((s, 4 * LANES), lambda b, g: (b, g)),
                   pl.BlockSpec((None, None, nc, N_PAIR, L, LANES), lambda b, g: (b, g, 0, 0, 0, 0))],
        out_shape=[jax.ShapeDtypeStruct((t, SSD_WIDTH), F32),
                   jax.ShapeDtypeStruct((nb, SSD_GROUPS, nc, N_PAIR, L, LANES), F32)],
        scratch_shapes=[pltpu.VMEM((N_PAIR, L, LANES), F32), pltpu.VMEM((2, L, HEAD_W), F32)],
        compiler_params=_params(("parallel", "parallel")), name="ssd_fwd")(
            xbc, xbc, xbc, dtr, bias, alog, dskip)


def _ssd_bwd(xbc, dtr, bias, alog, dskip, hs, dy, nb, s):
    nc = s // L
    t = nb * s

    def body(x_ref, b_ref, c_ref, dtr_ref, bias_ref, alog_ref, dsk_ref, hs_ref, dy_ref,
             dx_ref, db_ref, dc_ref, ddtr_ref, dbias_ref, dalog_ref, ddsk_ref, dh_scr, head_s):
        lane = lax.broadcasted_iota(jnp.int32, (L, LANES), 1)
        lane1 = lax.broadcasted_iota(jnp.int32, (1, LANES), 1)
        rowc = lax.broadcasted_iota(jnp.int32, (L, 1), 0)
        rr = lax.broadcasted_iota(jnp.int32, (L, L), 0)
        cc = lax.broadcasted_iota(jnp.int32, (L, L), 1)
        causal = rr >= cc
        tri_t = _bf(cc >= rr)
        dh_scr[...] = jnp.zeros_like(dh_scr)
        dsk = dsk_ref[...]
        a_neg = -jnp.exp(alog_ref[...])

        @pl.when(pl.program_id(1) == 0)
        def _():
            dbias_ref[...] = jnp.zeros_like(dbias_ref)
            dalog_ref[...] = jnp.zeros_like(dalog_ref)
            ddsk_ref[...] = jnp.zeros_like(ddsk_ref)

        spread_pair, spread_full, sum_pair, sum_full = _ssd_tables()
        sum_pair2 = jnp.concatenate([sum_pair] * 2, axis=0)
        sum_full2 = jnp.concatenate([sum_full] * 2, axis=0)
        head = functools.partial(_ssd_head, dtr_ref, bias_ref, a_neg, tri_bf=_bf(cc <= rr),
                                 spread_pair=spread_pair, spread_full=spread_full, head_s=head_s)
        head(jnp.int32(nc - 1), slot=0)

        def chunk(c, slot):
            r0 = pl.multiple_of(c * L, L)
            head(jnp.maximum(c - 1, 0), slot=1 - slot)
            raw, dt, acs, acs_t, dt_all, acs_all, acs_full = _ssd_head_load(head_s, slot)
            bb = _bf(b_ref[pl.ds(r0, L), :])
            cb = _bf(c_ref[pl.ds(r0, L), :])
            gm = _dot_nt(cb, bb)
            dg = jnp.zeros((L, L), F32)
            dbacc = jnp.zeros((L, LANES), F32)
            dcacc = jnp.zeros((L, LANES), F32)
            dacs_t = jnp.zeros((L, L), F32)
            ta_all, te_all, qm_all, tc_all, td_all, dcd_all = [], [], [], [], [], []
            for pr in range(N_PAIR):
                js = (2 * pr, 2 * pr + 1)
                lms, mfs = [], []
                for j in js:
                    seg = acs_full[:, j * LANES:(j + 1) * LANES] - acs_t[j:j + 1, :]
                    lm = jnp.exp(jnp.where(causal, seg, -1e30))
                    lms.append(lm)
                    mfs.append(gm * lm)
                x = x_ref[pl.ds(r0, L), pr * LANES:(pr + 1) * LANES]
                dyv = dy_ref[pl.ds(r0, L), pr * LANES:(pr + 1) * LANES]
                dt_p = dt_all[:, pr * LANES:(pr + 1) * LANES]
                acs_p = acs_all[:, pr * LANES:(pr + 1) * LANES]
                last_p = acs_p[L - 1:L, :]
                d_p = _pair_lanes(dsk[:, js[0]:js[0] + 1], dsk[:, js[1]:js[1] + 1], lane1)
                e_p = jnp.exp(acs_p)
                dte_p = jnp.exp(last_p - acs_p)
                xd = x * dt_p
                xdb = _bf(xd)
                hp = hs_ref[c, pr]
                hb = _bf(hp)
                dye = _bf(dyv * e_p)
                yo = _dot_nt(cb, hb) * e_p
                dcacc = dcacc + _dot(dye, hb)
                dh_out = _dot_tn(dye, cb)
                dyb = _bf(dyv)
                dms = _dot_nt(_bf(jnp.concatenate(
                    [jnp.where(lane < HEAD_DIM, dyv, 0.0), jnp.where(lane < HEAD_DIM, 0.0, dyv)], axis=0)), xdb)
                mty = _dot_tn(_bf(jnp.concatenate(mfs, axis=1)), dyb)
                dxd = jnp.where(lane < HEAD_DIM, mty[:L], mty[L:])
                for hh in range(2):
                    dm = dms[hh * L:(hh + 1) * L]
                    dg = dg + dm * lms[hh]
                    qm = dm * mfs[hh]
                    qm_all.append(qm)
                    dacs_t = dacs_t - jnp.where(rr == js[hh], jnp.sum(qm, axis=0, keepdims=True), 0.0)
                ds = dh_scr[pr]
                dsb = _bf(ds)
                xdd = xd * dte_p
                dxdd = _dot_nt(bb, dsb)
                dbacc = dbacc + _dot(_bf(xdd), dsb)
                dxd = dxd + dxdd * dte_p
                cd = jnp.where(rowc < HEAD_DIM, jnp.exp(acs[L - 1:L, js[0]:js[0] + 1]),
                               jnp.exp(acs[L - 1:L, js[1]:js[1] + 1]))
                prod = ds * hp
                dcd_all += [jnp.sum(prod[:HEAD_DIM], axis=0, keepdims=True),
                            jnp.sum(prod[HEAD_DIM:], axis=0, keepdims=True)]
                dh_scr[pr] = dh_out + ds * cd
                tcv = dxdd * xdd
                ta_all.append(dyv * yo - tcv)
                tc_all.append(jnp.sum(tcv, axis=0, keepdims=True))
                te_all.append(dxd * x)
                td_all.append(jnp.sum(dyv * x, axis=0, keepdims=True))
                dx_ref[pl.ds(r0, L), pr * LANES:(pr + 1) * LANES] = dxd * dt_p + dyv * d_p
            dgb = _bf(dg)
            dc_ref[pl.ds(r0, L), :] = dcacc + _dot(dgb, bb)
            db_ref[pl.ds(r0, L), :] = dbacc + _dot_tn(dgb, cb)
            last_add = jnp.zeros((1, LANES), F32)
            dd_add = jnp.zeros((1, LANES), F32)
            cd_row = jnp.exp(acs[L - 1:L, :])
            for j in range(SSD_HPG):
                hm = (lane1 >= HEAD_DIM) if j % 2 else (lane1 < HEAD_DIM)
                tcj = jnp.sum(jnp.where(hm, tc_all[j // 2], 0.0), axis=1, keepdims=True)
                tdj = jnp.sum(jnp.where(hm, td_all[j // 2], 0.0), axis=1, keepdims=True)
                dcdj = jnp.sum(dcd_all[j], axis=1, keepdims=True)
                last_add = last_add + jnp.where(lane1 == j, tcj + dcdj * cd_row, 0.0)
                dd_add = dd_add + jnp.where(lane1 == j, tdj, 0.0)
            dacs = (_dot(_split2(jnp.concatenate(ta_all, axis=1)), sum_pair2)
                    + _dot(_split2(jnp.concatenate(qm_all, axis=1)), sum_full2)
                    + dacs_t.T + jnp.where(rowc == L - 1, last_add, 0.0))
            ddta = _tri_dot(tri_t, dacs)
            ddt = _dot(_bf(jnp.concatenate(te_all, axis=1)), sum_pair) + ddta * a_neg
            ddraw = ddt * _sigmoid(raw)
            ddtr_ref[pl.ds(r0, L), :] = ddraw
            dbias_ref[...] += jnp.sum(ddraw, axis=0, keepdims=True)
            dalog_ref[...] += jnp.sum(ddta * dt, axis=0, keepdims=True) * a_neg
            ddsk_ref[...] += dd_add

        def two_chunks(i, _):
            chunk(nc - 1 - 2 * i, 0)
            chunk(nc - 2 - 2 * i, 1)
            return 0

        lax.fori_loop(0, nc // 2, two_chunks, 0)

    assert nc % 2 == 0
    grp = lambda g, b: (g, 0, 0)
    small = jax.ShapeDtypeStruct((SSD_GROUPS, 1, LANES), F32)
    return pl.pallas_call(
        body, grid=(SSD_GROUPS, nb),
        in_specs=[pl.BlockSpec((s, 4 * LANES), lambda g, b: (b, g)),
                  pl.BlockSpec((s, LANES), lambda g, b: (b, 16 + g)),
                  pl.BlockSpec((s, LANES), lambda g, b: (b, 20 + g)),
                  pl.BlockSpec((None, s, LANES), lambda g, b: (g, b, 0)),
                  pl.BlockSpec((None, 1, LANES), grp), pl.BlockSpec((None, 1, LANES), grp),
                  pl.BlockSpec((None, 1, LANES), grp),
                  pl.BlockSpec((None, None, nc, N_PAIR, L, LANES), lambda g, b: (b, g, 0, 0, 0, 0)),
                  pl.BlockSpec((s, 4 * LANES), lambda g, b: (b, g))],
        out_specs=[pl.BlockSpec((s, 4 * LANES), lambda g, b: (b, g)),
                   pl.BlockSpec((s, LANES), lambda g, b: (b, g)),
                   pl.BlockSpec((s, LANES), lambda g, b: (b, g)),
                   pl.BlockSpec((None, s, LANES), lambda g, b: (g, b, 0)),
                   pl.BlockSpec((None, 1, LANES), grp), pl.BlockSpec((None, 1, LANES), grp),
                   pl.BlockSpec((None, 1, LANES), grp)],
        out_shape=[jax.ShapeDtypeStruct((t, SSD_WIDTH), F32),
                   jax.ShapeDtypeStruct((t, SSD_GROUPS * SSD_STATE), F32),
                   jax.ShapeDtypeStruct((t, SSD_GROUPS * SSD_STATE), F32),
                   jax.ShapeDtypeStruct((SSD_GROUPS, t, LANES), F32), small, small, small],
        scratch_shapes=[pltpu.VMEM((N_PAIR, L, LANES), F32), pltpu.VMEM((2, L, HEAD_W), F32)],
        compiler_params=_params(("parallel", "arbitrary")), name="ssd_bwd")(
            xbc, xbc, xbc, dtr, bias, alog, dskip, hs, dy)


GN = SSD_WIDTH // SSD_GROUPS


def _post_fwd(o, proj, y, nw, *, tm=256):
    t = o.shape[0]

    def body(o_ref, za_ref, y_ref, zs_ref, nw_ref, ya_ref, ys_ref):
        za = za_ref[...].astype(F32)
        ya_ref[...] = _bf(o_ref[...] * (za * _sigmoid(za)))
        zs = zs_ref[...].astype(F32)
        yz = y_ref[...] * (zs * _sigmoid(zs))
        for g in range(SSD_GROUPS):
            sl = slice(g * GN, (g + 1) * GN)
            v = yz[:, sl]
            r = lax.rsqrt(jnp.mean(v * v, axis=-1, keepdims=True) + EPS)
            ys_ref[:, sl] = _bf(v * r * nw_ref[:, sl])

    return pl.pallas_call(
        body, grid=(t // tm,),
        in_specs=[pl.BlockSpec((tm, SB_WIDTH), lambda i: (i, 0)),
                  pl.BlockSpec((tm, SB_WIDTH), lambda i: (i, CB_ZA * LANES // SB_WIDTH)),
                  pl.BlockSpec((tm, SSD_WIDTH), lambda i: (i, 0)),
                  pl.BlockSpec((tm, SSD_WIDTH), lambda i: (i, CB_ZS * LANES // SSD_WIDTH)),
                  pl.BlockSpec((1, SSD_WIDTH), lambda i: (0, 0))],
        out_specs=[pl.BlockSpec((tm, SB_WIDTH), lambda i: (i, 0)),
                   pl.BlockSpec((tm, SSD_WIDTH), lambda i: (i, 0))],
        out_shape=[jax.ShapeDtypeStruct((t, SB_WIDTH), BF16), jax.ShapeDtypeStruct((t, SSD_WIDTH), BF16)],
        compiler_params=_params(("parallel",)), name="post_fwd")(o, proj, y, proj, nw)


def _post_bwd(dya, o, proj, dys, y, nw, *, tm=256):
    t = o.shape[0]

    def body(dya_ref, o_ref, za_ref, dys_ref, y_ref, zs_ref, nw_ref,
             do_ref, dza_ref, dy_ref, dzs_ref, dnw_ref):
        @pl.when(pl.program_id(0) == 0)
        def _():
            dnw_ref[...] = jnp.zeros_like(dnw_ref)

        za = za_ref[...].astype(F32)
        sa = _sigmoid(za)
        d = dya_ref[...].astype(F32)
        do_ref[...] = _bf(d * (za * sa))
        dza_ref[...] = _bf(d * o_ref[...] * (sa * (1.0 + za * (1.0 - sa))))
        zs = zs_ref[...].astype(F32)
        ss = _sigmoid(zs)
        silu = zs * ss
        yv = y_ref[...]
        yz = yv * silu
        dv = dys_ref[...].astype(F32)
        for g in range(SSD_GROUPS):
            sl = slice(g * GN, (g + 1) * GN)
            v = yz[:, sl]
            r = lax.rsqrt(jnp.mean(v * v, axis=-1, keepdims=True) + EPS)
            nrm = v * r
            dg = dv[:, sl]
            dnw_ref[:, sl] += jnp.sum(dg * nrm, axis=0, keepdims=True)
            dn = dg * nw_ref[:, sl]
            dyz = r * (dn - nrm * jnp.mean(dn * nrm, axis=-1, keepdims=True))
            dy_ref[:, sl] = dyz * silu[:, sl]
            dzs_ref[:, sl] = _bf(dyz * yv[:, sl] * (ss[:, sl] * (1.0 + zs[:, sl] * (1.0 - ss[:, sl]))))

    a_spec = pl.BlockSpec((tm, SB_WIDTH), lambda i: (i, 0))
    s_spec = pl.BlockSpec((tm, SSD_WIDTH), lambda i: (i, 0))
    w_spec = pl.BlockSpec((1, SSD_WIDTH), lambda i: (0, 0))
    return pl.pallas_call(
        body, grid=(t // tm,),
        in_specs=[a_spec, a_spec, pl.BlockSpec((tm, SB_WIDTH), lambda i: (i, CB_ZA * LANES // SB_WIDTH)),
                  s_spec, s_spec, pl.BlockSpec((tm, SSD_WIDTH), lambda i: (i, CB_ZS * LANES // SSD_WIDTH)),
                  w_spec],
        out_specs=[a_spec, a_spec, s_spec, s_spec, w_spec],
        out_shape=[jax.ShapeDtypeStruct((t, SB_WIDTH), BF16), jax.ShapeDtypeStruct((t, SB_WIDTH), BF16),
                   jax.ShapeDtypeStruct((t, SSD_WIDTH), F32), jax.ShapeDtypeStruct((t, SSD_WIDTH), BF16),
                   jax.ShapeDtypeStruct((1, SSD_WIDTH), F32)],
        compiler_params=_params(("arbitrary",)), name="post_bwd")(dya, o, proj, dys, y, proj, nw)


def _merge_fwd(ya, ys, proj_g, *, tm=512):
    t = ya.shape[0]

    def body(ya_ref, ys_ref, ga_ref, gs_ref, m_ref):
        m_ref[...] = _bf(_sigmoid(ga_ref[...].astype(F32)) * ya_ref[...].astype(F32)
                         + _sigmoid(gs_ref[...].astype(F32)) * ys_ref[...].astype(F32))

    spec = pl.BlockSpec((tm, D_MODEL), lambda i: (i, 0))
    return pl.pallas_call(
        body, grid=(t // tm,),
        in_specs=[spec, spec, pl.BlockSpec((tm, D_MODEL), lambda i: (i, 0)),
                  pl.BlockSpec((tm, D_MODEL), lambda i: (i, 1))],
        out_specs=spec, out_shape=jax.ShapeDtypeStruct((t, D_MODEL), BF16),
        compiler_params=_params(("parallel",)), name="merge_fwd")(ya, ys, proj_g, proj_g)


def _merge_bwd(dm, ya, ys, proj_g, *, tm=512):
    t = ya.shape[0]

    def body(dm_ref, ya_ref, ys_ref, ga_ref, gs_ref, dya_ref, dys_ref, dg_ref):
        d = dm_ref[...].astype(F32)
        ga = _sigmoid(ga_ref[...].astype(F32))
        gs = _sigmoid(gs_ref[...].astype(F32))
        dya_ref[...] = _bf(d * ga)
        dys_ref[...] = _bf(d * gs)
        dg_ref[:, :D_MODEL] = _bf(d * ya_ref[...].astype(F32) * (ga * (1.0 - ga)))
        dg_ref[:, D_MODEL:] = _bf(d * ys_ref[...].astype(F32) * (gs * (1.0 - gs)))

    spec = pl.BlockSpec((tm, D_MODEL), lambda i: (i, 0))
    return pl.pallas_call(
        body, grid=(t // tm,),
        in_specs=[spec, spec, spec, pl.BlockSpec((tm, D_MODEL), lambda i: (i, 0)),
                  pl.BlockSpec((tm, D_MODEL), lambda i: (i, 1))],
        out_specs=[spec, spec, pl.BlockSpec((tm, 2 * D_MODEL), lambda i: (i, 0))],
        out_shape=[jax.ShapeDtypeStruct((t, D_MODEL), BF16), jax.ShapeDtypeStruct((t, D_MODEL), BF16),
                   jax.ShapeDtypeStruct((t, 2 * D_MODEL), BF16)],
        compiler_params=_params(("parallel",)), name="merge_bwd")(dm, ya, ys, proj_g, proj_g)


def _final(x, out, wf, target, *, tm=512):
    t = x.shape[0]

    def body(x_ref, o_ref, w_ref, t_ref, loss_ref, dx_ref, dw_ref):
        @pl.when(pl.program_id(0) == 0)
        def _():
            loss_ref[...] = jnp.zeros_like(loss_ref)
            dw_ref[...] = jnp.zeros_like(dw_ref)

        x2 = x_ref[...] + o_ref[...]
        r = lax.rsqrt(jnp.mean(x2 * x2, axis=-1, keepdims=True) + EPS)
        nrm = x2 * r
        e = nrm * w_ref[...] - t_ref[...]
        row_loss = jnp.mean(e * e, axis=-1, keepdims=True)
        loss_ref[...] += 0.5 * jnp.sum(row_loss, axis=0, keepdims=True)
        dyv = e * (1.0 / D_MODEL)
        dw_ref[...] += jnp.sum(dyv * nrm, axis=0, keepdims=True)
        dn = dyv * w_ref[...]
        dx_ref[...] = r * (dn - nrm * jnp.mean(dn * nrm, axis=-1, keepdims=True))

    spec = pl.BlockSpec((tm, D_MODEL), lambda i: (i, 0))
    fix = pl.BlockSpec((1, D_MODEL), lambda i: (0, 0))
    return pl.pallas_call(
        body, grid=(t // tm,),
        in_specs=[spec, spec, fix, spec],
        out_specs=[pl.BlockSpec((1, LANES), lambda i: (0, 0)), spec, fix],
        out_shape=[jax.ShapeDtypeStruct((1, LANES), F32), jax.ShapeDtypeStruct((t, D_MODEL), F32),
                   jax.ShapeDtypeStruct((1, D_MODEL), F32)],
        compiler_params=_params(("arbitrary",)), name="final_loss")(x, out, wf, target)


def _group_lanes(v):
    return jnp.pad(v.reshape(SSD_GROUPS, 1, SSD_HPG), ((0, 0), (0, 0), (0, LANES - SSD_HPG)))


def _ungroup_lanes(v):
    return v[:, :, :SSD_HPG].reshape(1, N_DT)


def _row_slabs(pieces, rows):
    total = sum(p.shape[0] for p in pieces)
    assert total % rows == 0
    slabs = []
    for lo in range(0, total, rows):
        parts, off = [], 0
        for p in pieces:
            a, b = max(lo, off), min(lo + rows, off + p.shape[0])
            if a < b:
                parts.append(p[a - off:b - off])
            off += p.shape[0]
        slabs.append(parts[0] if len(parts) == 1 else jnp.concatenate(parts, axis=0))
    return jnp.stack(slabs)


def _local_step(x, target, nb, s, w_in_t, conv_w, conv_b, norm_w, dt_bias, a_log, d_skip,
                ssm_norm_w, out_weights, final_norm_w, between=None):
    t = nb * s
    w_g_t = w_in_t[GATE_COL0:]
    w_dt_t = jnp.pad(w_in_t[DT_COL0:GATE_COL0], ((0, LANES - N_DT), (0, 0)))
    h, r1 = _rms_fwd(x, norm_w)
    proj = _matmul(h, w_in_t, nt=True, n=DT_COL0, out_dtype=BF16, name="proj_main")
    proj_g = _matmul(h, w_g_t, nt=True, out_dtype=BF16, name="proj_gate")
    dt_raw = _matmul(h, w_dt_t, nt=True, name="proj_dt")
    dtr = jnp.pad(dt_raw[:, :N_DT].reshape(t, SSD_GROUPS, SSD_HPG).transpose(1, 0, 2),
                  ((0, 0), (0, 0), (0, LANES - SSD_HPG)))
    bias_g, alog_g, dsk_g = _group_lanes(dt_bias), _group_lanes(a_log), _group_lanes(d_skip)

    o, tot = _attn_fwd(proj, nb, s)
    xbc = _conv_fwd(proj, conv_w, conv_b, nb, s)
    y, hs = _ssd_fwd(xbc, dtr, bias_g, alog_g, dsk_g, nb, s)
    ya_in, ys_in = _post_fwd(o, proj, y, ssm_norm_w)
    w_attn_out, w_ssm_out, w_o = out_weights(ys_in) if callable(out_weights) else out_weights
    ya = _matmul(ya_in, w_attn_out, out_dtype=BF16, name="attn_out")
    ys = _matmul(ys_in, w_ssm_out, out_dtype=BF16, name="ssm_out")
    merged = _merge_fwd(ya, ys, proj_g)
    out = _matmul(merged, w_o, name="out_proj")
    loss, dx2, d_final_w = _final(x, out, final_norm_w, target)

    dx2b = _bf(dx2)
    dmerged = _matmul(dx2b, w_o, nt=True, out_dtype=BF16, name="d_merged")
    d_w_o = _matmul(merged, dx2b, ta=True, name="d_w_o")
    dya, dys, dgate = _merge_bwd(dmerged, ya, ys, proj_g)
    d_w_attn_out = _matmul(ya_in, dya, ta=True, name="d_w_attn_out")
    d_w_ssm_out = _matmul(ys_in, dys, ta=True, name="d_w_ssm_out")
    dya_in = _matmul(dya, w_attn_out, nt=True, out_dtype=BF16, name="d_ya_in")
    dys_in = _matmul(dys, w_ssm_out, nt=True, name="d_ys_in")
    do, dza, dy, dzs, d_ssm_norm_w = _post_bwd(dya_in, o, proj, dys_in, y, ssm_norm_w)
    dq, dk, dv = _attn_bwd(proj, do, tot, nb, s)
    dxs_act, db_act, dc_act, ddtr, dbias_g, dalog_g, ddsk_g = _ssd_bwd(
        xbc, dtr, bias_g, alog_g, dsk_g, hs, dy, nb, s)
    dxs, dwx, dbx = _conv_bwd(proj, dxs_act, conv_w, conv_b, nb, s, col0=0, name="conv_bwd_x")
    dbm, dwb, dbb = _conv_bwd(proj, db_act, conv_w, conv_b, nb, s, col0=SSD_WIDTH, name="conv_bwd_b")
    dcm, dwc, dbc = _conv_bwd(proj, dc_act, conv_w, conv_b, nb, s, col0=SSD_WIDTH + 512, name="conv_bwd_c")
    dproj = jnp.concatenate([dq, dk, dv, dza, dzs, dxs, dbm, dcm], axis=1)
    ddt = jnp.pad(_bf(ddtr[:, :, :SSD_HPG].transpose(1, 0, 2).reshape(t, N_DT)), ((0, 0), (0, LANES - N_DT)))
    wt = dict(ta=True, out_dtype=BF16)
    d_w_in_t = _row_slabs(
        [_matmul(dproj, h, name="d_w_main", **wt), _matmul(ddt, h, name="d_w_dt", **wt)[:N_DT],
         _matmul(dgate, h, name="d_w_gate", **wt)], SHARD)
    grads = dict(
        w_in_t=d_w_in_t,
        conv_w=jnp.concatenate([dwx, dwb, dwc], axis=1), conv_b=jnp.concatenate([dbx, dbb, dbc], axis=1),
        dt_bias=_ungroup_lanes(dbias_g), a_log=_ungroup_lanes(dalog_g), d_skip=_ungroup_lanes(ddsk_g),
        ssm_norm_w=d_ssm_norm_w, w_attn_out=d_w_attn_out, w_ssm_out=d_w_ssm_out, w_o=d_w_o,
        final_norm_w=d_final_w)
    extra = None
    if between is not None:
        zero, extra = between(grads)
        ddt = ddt + _bf(zero[0:1, 0:1])

    dh = _matmul(ddt, w_dt_t, name="d_h_dt")
    dh = _matmul(dgate, w_g_t, add=dh, name="d_h_gate")
    dh = _matmul(dproj, w_in_t, add=dh, name="d_h", tk=DT_COL0 // 4)
    grad_x, grads["norm_w"] = _rms_bwd(dh, x, r1, norm_w, dx2)
    return loss, grad_x, grads, extra


N_CHIP = N_DEV // 2


def _pair_exchange(scatter, bcast, *, name):
    arrays = list(scatter) + list(bcast)
    ns, n = len(scatter), len(arrays)
    n_sem = ns * N_CHIP + (n - ns) * (N_DEV - 1)

    def body(*refs):
        ins, outs = refs[:n], refs[n:2 * n]
        send_sems, recv_sems, local_sems = refs[2 * n:]
        x, y, c = lax.axis_index("x"), lax.axis_index("y"), lax.axis_index("c")
        me = 4 * x + 2 * y + c
        copies, k = [], 0
        for a in range(ns):
            for q in range(N_CHIP):
                cp = pltpu.make_async_remote_copy(
                    src_ref=ins[a].at[2 * q + 1 - c], dst_ref=outs[a].at[q],
                    send_sem=send_sems.at[k], recv_sem=recv_sems.at[k],
                    device_id=(x, y, 1 - c), device_id_type=pl.DeviceIdType.MESH)
                cp.start()
                copies.append(cp)
                k += 1
        for a in range(ns, n):
            cp = pltpu.make_async_copy(ins[a], outs[a].at[me], local_sems.at[a - ns])
            cp.start()
            copies.append(cp)
            for r in range(1, N_DEV):
                tx = 1 - x if (r >> 2) & 1 else x
                ty = 1 - y if (r >> 1) & 1 else y
                tc = 1 - c if r & 1 else c
                cp = pltpu.make_async_remote_copy(
                    src_ref=ins[a], dst_ref=outs[a].at[me], send_sem=send_sems.at[k], recv_sem=recv_sems.at[k],
                    device_id=(tx, ty, tc), device_id_type=pl.DeviceIdType.MESH)
                cp.start()
                copies.append(cp)
                k += 1
        for cp in copies:
            cp.wait()

    out_shape = [jax.ShapeDtypeStruct((N_CHIP,) + v.shape[1:] if i < ns else (N_DEV,) + v.shape, v.dtype)
                 for i, v in enumerate(arrays)]
    return pl.pallas_call(
        body, in_specs=[pl.BlockSpec(memory_space=pl.ANY)] * n,
        out_specs=[pl.BlockSpec(memory_space=pl.ANY)] * n, out_shape=out_shape,
        scratch_shapes=[pltpu.SemaphoreType.DMA((n_sem,)), pltpu.SemaphoreType.DMA((n_sem,)),
                        pltpu.SemaphoreType.DMA((max(n - ns, 1),))],
        compiler_params=pltpu.CompilerParams(has_side_effects=True), name=name)(*arrays)


def _pair_add(mine, recv, core, *, name, tc=None):
    _, r, c = recv.shape
    tc = c if tc is None else tc
    assert c % tc == 0

    def body(core_ref, m_ref, r_ref, o_ref):
        o_ref[...] = (m_ref[...].astype(F32) + r_ref[...].astype(F32)).astype(o_ref.dtype)

    spec = pl.BlockSpec((None, r, tc), lambda q, i, core_ref: (q, 0, i))
    return pl.pallas_call(
        body, grid_spec=pltpu.PrefetchScalarGridSpec(
            num_scalar_prefetch=1, grid=(N_CHIP, c // tc),
            in_specs=[pl.BlockSpec((None, r, tc), lambda q, i, core_ref: (2 * q + core_ref[0], 0, i)), spec],
            out_specs=spec),
        out_shape=jax.ShapeDtypeStruct(recv.shape, recv.dtype),
        compiler_params=_params(("parallel", "parallel")), name=name)(core, mine, recv)


_HBM = pl.BlockSpec(memory_space=pltpu.HBM)
_SEM = pl.BlockSpec(memory_space=pltpu.SEMAPHORE)
_DATAFLOW = pltpu.SideEffectType.DATAFLOW_SIDE_EFFECTING
_PEERS = {"chips": N_CHIP - 1, "all": N_DEV - 1}


def _split_copies(pattern, srcs, lands, send_sems, recv_sems):
    x, y, c = lax.axis_index("x"), lax.axis_index("y"), lax.axis_index("c")
    per = _PEERS[pattern]
    copies = []
    for a in range(len(srcs)):
        for r in range(1, per + 1):
            if pattern == "chips":
                tx = 1 - x if (r >> 1) & 1 else x
                ty = 1 - y if r & 1 else y
                tc = c
                src, dst = srcs[a].at[2 * tx + ty], lands[a].at[2 * x + y]
            else:
                tx = 1 - x if (r >> 2) & 1 else x
                ty = 1 - y if (r >> 1) & 1 else y
                tc = 1 - c if r & 1 else c
                src, dst = srcs[a], lands[a].at[4 * x + 2 * y + c]
            copies.append(pltpu.make_async_remote_copy(
                src_ref=src, dst_ref=dst, send_sem=send_sems.at[a * per + r - 1],
                recv_sem=recv_sems.at[a * per + r - 1], device_id=(tx, ty, tc), device_id_type=pl.DeviceIdType.MESH))
    return copies


def _split_start(pattern, arrays, *, name):
    n = len(arrays)
    n_sem = n * _PEERS[pattern]

    def body(*refs):
        srcs, lands = refs[:n], refs[n:2 * n]
        send_sems, recv_sems = refs[2 * n], refs[2 * n + 1]
        zero = refs[-1]
        for cp in _split_copies(pattern, srcs, lands, send_sems, recv_sems):
            cp.start()
        zero[...] = jnp.zeros_like(zero)

    src_t = [pltpu.HBM(v.shape, v.dtype) for v in arrays]
    land_t = [pltpu.HBM(v.shape if pattern == "chips" else (N_DEV,) + v.shape, v.dtype) for v in arrays]
    outs = pl.pallas_call(
        body, name=name,
        out_shape=(pltpu.SemaphoreType.DMA((n_sem,)), pltpu.SemaphoreType.DMA((n_sem,)), *src_t, *land_t,
                   jax.ShapeDtypeStruct((8, LANES), F32)),
        in_specs=[_HBM] * (2 * n),
        out_specs=(_SEM, _SEM, *[_HBM] * (2 * n), pl.BlockSpec(memory_space=pltpu.VMEM)),
        input_output_aliases={i: 2 + i for i in range(2 * n)},
        compiler_params=pltpu.CompilerParams(has_side_effects=_DATAFLOW))(
            *[pltpu.with_memory_space_constraint(v, pltpu.HBM) for v in arrays],
            *[pltpu.with_memory_space_constraint(lax.empty(t.shape, t.dtype), pltpu.HBM) for t in land_t])
    return outs[0], outs[1], outs[2:2 + n], outs[2 + n:2 + 2 * n], outs[-1]


def _split_wait(pattern, send_sems, recv_sems, srcs, lands, after, *, name):
    n = len(srcs)

    def body(*refs):
        for cp in _split_copies(pattern, refs[:n], refs[n:2 * n], refs[2 * n], refs[2 * n + 1]):
            cp.wait_send()
            cp.wait_recv()

    outs = pl.pallas_call(
        body, name=name, out_shape=[pltpu.HBM(v.shape, v.dtype) for v in (*srcs, *lands)],
        in_specs=[_HBM] * (2 * n) + [_SEM, _SEM, pl.BlockSpec(memory_space=pl.ANY)],
        out_specs=[_HBM] * (2 * n), input_output_aliases={i: i for i in range(2 * n)},
        compiler_params=pltpu.CompilerParams(has_side_effects=_DATAFLOW))(
            *srcs, *lands, send_sems, recv_sems, after)
    return outs[:n], outs[n:]


def _own_slab(lands, srcs, index, whole):
    out = []
    for land, src in zip(lands, srcs):
        own = src[None] if whole else lax.dynamic_slice(src, (index, 0, 0), (1,) + src.shape[1:])
        out.append(lax.dynamic_update_slice(land, own, (index, 0, 0)))
    return out


def _gather(arrays, *, name):
    n = len(arrays)
    per = N_DEV - 1

    def body(*refs):
        ins, outs = refs[:n], refs[n:2 * n]
        send_sems, recv_sems, local_sems = refs[2 * n:]
        x, y, c = lax.axis_index("x"), lax.axis_index("y"), lax.axis_index("c")
        me, sibling = (x, y, c), (x, y, 1 - c)
        chips = [(1 - x, y), (x, 1 - y), (1 - x, 1 - y)]

        def slab(a, block):
            return outs[a].at[4 * block[0] + 2 * block[1] + block[2]]

        def copy(a, k, block, to, src=None):
            rows = slab(a, block)
            return pltpu.make_async_remote_copy(
                src_ref=rows if src is None else src, dst_ref=rows,
                send_sem=send_sems.at[a * per + k], recv_sem=recv_sems.at[a * per + k],
                device_id=to, device_id_type=pl.DeviceIdType.MESH)

        started = []
        for a in range(n):
            mine = pltpu.make_async_copy(ins[a], slab(a, me), local_sems.at[a])
            mine.start()
            started.append(mine)
        for a in range(n):
            first = [copy(a, 1 + j, me, (*chip, c), src=ins[a]) for j, chip in enumerate(chips)]
            first.append(copy(a, 0, me, sibling, src=ins[a]))
            for cp in first:
                cp.start()
            started += first
        for j, chip in enumerate(chips):
            for a in range(n):
                copy(a, 1 + j, (*chip, c), me).wait_recv()
                passed = copy(a, 4 + j, (*chip, c), sibling)
                passed.start()
                started.append(passed)
        for a in range(n):
            copy(a, 0, sibling, me).wait_recv()
            for j, chip in enumerate(chips):
                copy(a, 4 + j, (*chip, 1 - c), me).wait_recv()
        for cp in started[:n]:
            cp.wait()
        for cp in started[n:]:
            cp.wait_send()

    return pl.pallas_call(
        body, in_specs=[pl.BlockSpec(memory_space=pl.ANY)] * n,
        out_specs=[pl.BlockSpec(memory_space=pl.ANY)] * n,
        out_shape=[jax.ShapeDtypeStruct((N_DEV,) + v.shape, v.dtype) for v in arrays],
        scratch_shapes=[pltpu.SemaphoreType.DMA((n * per,)), pltpu.SemaphoreType.DMA((n * per,)),
                        pltpu.SemaphoreType.DMA((n,))],
        compiler_params=pltpu.CompilerParams(has_side_effects=True), name=name)(*arrays)


def _adamw(parts, w, m, v, *, name, tc=None):
    r, c = w.shape
    n_parts = parts.shape[0]
    tc = c if tc is None else tc
    assert c % tc == 0

    def body(p_ref, w_ref, m_ref, v_ref, g_ref, d_ref, nm_ref, nv_ref):
        g = p_ref[0].astype(F32)
        for d in range(1, n_parts):
            g = g + p_ref[d].astype(F32)
        mm = ADAM_B1 * m_ref[...] + (1.0 - ADAM_B1) * g
        vv = ADAM_B2 * v_ref[...] + (1.0 - ADAM_B2) * (g * g)
        m_hat = mm / (1.0 - ADAM_B1 ** ADAM_STEP)
        v_hat = vv / (1.0 - ADAM_B2 ** ADAM_STEP)
        g_ref[...] = g
        d_ref[...] = -ADAM_LR * (m_hat / (jnp.sqrt(v_hat) + ADAM_EPS) + ADAM_WD * w_ref[...])
        nm_ref[...] = mm
        nv_ref[...] = vv

    spec = pl.BlockSpec((r, tc), lambda i: (0, i))
    out = jax.ShapeDtypeStruct((r, c), F32)
    return pl.pallas_call(
        body, grid=(c // tc,),
        in_specs=[pl.BlockSpec((n_parts, r, tc), lambda i: (0, 0, i)), spec, spec, spec],
        out_specs=[spec] * 4, out_shape=[out] * 4,
        compiler_params=_params(("parallel",)), name=name)(parts, w, m, v)


SMALL = (("norm_w", D_MODEL), ("conv_b", CONV_DIM), ("dt_bias", N_DT), ("a_log", N_DT),
         ("d_skip", N_DT), ("ssm_norm_w", SSD_WIDTH), ("final_norm_w", D_MODEL))
SMALL_ROWS = 64
SHARD = D_PROJ // N_DEV


def _pack_small(vals):
    flat = jnp.concatenate([vals[k].reshape(-1).astype(F32) for k, _ in SMALL])
    return jnp.pad(flat, (0, SMALL_ROWS * LANES - flat.shape[0])).reshape(SMALL_ROWS, LANES)


def _unpack_small(packed, shapes):
    flat = packed.reshape(-1)
    out, off = {}, 0
    for k, size in SMALL:
        out[k] = flat[off:off + size].reshape(shapes[k])
        off += size
    return out


def kernel(x, norm_w, w_in, conv_w, conv_b, dt_bias, a_log, d_skip, ssm_norm_w, w_attn_out, w_ssm_out, w_o, final_norm_w, loss_target, m_norm_w, m_w_in, m_conv_w, m_conv_b, m_dt_bias, m_a_log, m_d_skip, m_ssm_norm_w, m_w_attn_out, m_w_ssm_out, m_w_o, m_final_norm_w, v_norm_w, v_w_in, v_conv_w, v_conv_b, v_dt_bias, v_a_log, v_d_skip, v_ssm_norm_w, v_w_attn_out, v_w_ssm_out, v_w_o, v_final_norm_w):
    nb, s, _ = x.shape
    t = nb * s
    weights = dict(norm_w=norm_w, w_in=w_in, conv_w=conv_w, conv_b=conv_b, dt_bias=dt_bias, a_log=a_log,
                   d_skip=d_skip, ssm_norm_w=ssm_norm_w, w_attn_out=w_attn_out, w_ssm_out=w_ssm_out,
                   w_o=w_o, final_norm_w=final_norm_w)
    moms = dict(norm_w=m_norm_w, w_in=m_w_in, conv_w=m_conv_w, conv_b=m_conv_b, dt_bias=m_dt_bias,
                a_log=m_a_log, d_skip=m_d_skip, ssm_norm_w=m_ssm_norm_w, w_attn_out=m_w_attn_out,
                w_ssm_out=m_w_ssm_out, w_o=m_w_o, final_norm_w=m_final_norm_w)
    vels = dict(norm_w=v_norm_w, w_in=v_w_in, conv_w=v_conv_w, conv_b=v_conv_b, dt_bias=v_dt_bias,
                a_log=v_a_log, d_skip=v_d_skip, ssm_norm_w=v_ssm_norm_w, w_attn_out=v_w_attn_out,
                w_ssm_out=v_w_ssm_out, w_o=v_w_o, final_norm_w=v_final_norm_w)

    tr_ = lambda a: jnp.transpose(a[0])
    g_in, g_conv = _gather([_bf(tr_(w_in)), conv_w[0]], name="gather_weights")
    conv_full = g_conv.transpose(1, 0, 2).reshape(CONV_K, CONV_DIM)
    o_send, o_recv, o_srcs, o_lands, o_zero = _split_start(
        "all", [_bf(w_attn_out[0]), _bf(w_ssm_out[0]), _bf(w_o[0])], name="gather_out_weights_start")

    core = lax.axis_index("c").astype(jnp.int32).reshape(1)
    chip = 2 * lax.axis_index("x") + lax.axis_index("y")
    me = 2 * chip + lax.axis_index("c")

    def out_weights(after):
        srcs, lands = _split_wait("all", o_send, o_recv, o_srcs, o_lands, after, name="gather_out_weights_wait")
        g_wa, g_ws, g_wo = _own_slab(lands, srcs, me, True)
        return (g_wa.reshape(SB_WIDTH, D_MODEL), g_ws.reshape(SSD_WIDTH, D_MODEL), g_wo.reshape(D_MODEL, D_MODEL))

    def send_weight_grads(grads):
        slabs = [grads["w_in_t"],
                 grads["conv_w"].reshape(CONV_K, N_DEV, CONV_DIM // N_DEV).transpose(1, 0, 2),
                 _bf(grads["w_attn_out"].reshape(N_DEV, SB_WIDTH // N_DEV, D_MODEL)),
                 _bf(grads["w_ssm_out"].reshape(N_DEV, SSD_WIDTH // N_DEV, D_MODEL)),
                 _bf(grads["w_o"].reshape(N_DEV, D_MODEL // N_DEV, D_MODEL))]
        recv = _pair_exchange(slabs, [], name="exchange_pairs")
        names = ("w_in", "conv_w", "w_attn_out", "w_ssm_out", "w_o")
        sums = [_pair_add(a, b, core, name="pair_add_" + k, tc=4 * LANES if k == "w_in" else None)
                for k, a, b in zip(names, slabs, recv)]
        send_sems, recv_sems, srcs, lands, zero = _split_start("chips", sums, name="exchange_chips_start")
        return zero, (send_sems, recv_sems, srcs, lands)

    loss_part, grad_x, grads, (send_sems, recv_sems, srcs, lands) = _local_step(
        x.reshape(t, D_MODEL), loss_target.reshape(t, D_MODEL), nb, s, g_in.reshape(D_PROJ, D_MODEL),
        conv_full, conv_b, norm_w + o_zero[0:1, 0:1], dt_bias, a_log, d_skip, ssm_norm_w, out_weights,
        final_norm_w.reshape(1, D_MODEL), between=send_weight_grads)

    (p_small,) = _pair_exchange([], [_pack_small(grads)], name="exchange_small")
    srcs, lands = _split_wait("chips", send_sems, recv_sems, srcs, lands, grad_x, name="exchange_chips_wait")
    p_in, p_conv, p_wa, p_ws, p_wo = _own_slab(lands, srcs, chip, False)

    res = {}
    res["w_in"] = [a.T for a in _adamw(p_in, tr_(w_in), tr_(m_w_in), tr_(v_w_in), name="adamw_w_in", tc=2 * LANES)]
    res["conv_w"] = _adamw(p_conv, conv_w[0], m_conv_w[0], v_conv_w[0], name="adamw_conv_w")
    res["w_attn_out"] = _adamw(p_wa, w_attn_out[0], m_w_attn_out[0], v_w_attn_out[0], name="adamw_w_attn_out")
    res["w_ssm_out"] = _adamw(p_ws, w_ssm_out[0], m_w_ssm_out[0], v_w_ssm_out[0], name="adamw_w_ssm_out")
    res["w_o"] = _adamw(p_wo, w_o[0], m_w_o[0], v_w_o[0], name="adamw_w_o")
    sm = _adamw(p_small, _pack_small(weights), _pack_small(moms), _pack_small(vels), name="adamw_small")
    shapes = {k: weights[k].shape for k, _ in SMALL}
    sm = [_unpack_small(v, shapes) for v in sm]
    for k, _ in SMALL:
        res[k] = tuple(part[k] for part in sm)

    order = ("norm_w", "w_in", "conv_w", "conv_b", "dt_bias", "a_log", "d_skip", "ssm_norm_w",
             "w_attn_out", "w_ssm_out", "w_o", "final_norm_w")
    loss = lax.psum(loss_part[0, 0], ("x", "y", "c"))
    outs = [loss, grad_x.reshape(nb, s, D_MODEL)]
    for i in range(4):
        outs += [res[k][i].reshape(weights[k].shape) for k in order]
    return tuple(outs)
```

```python
import functools

import jax
import jax.numpy as jnp
from jax import lax
from jax.experimental import pallas as pl
from jax.experimental.pallas import tpu as pltpu

F32 = jnp.float32
BF16 = jnp.bfloat16

D_MODEL = 1024
SB_WIDTH = 1024
HEAD_DIM = 64
SSD_WIDTH = 2048
SSD_GROUPS = 4
SSD_HPG = 8
SSD_STATE = 128
SSD_CHUNK = 128
CONV_K = 4
CONV_DIM = 3072
N_DT = 32
D_PROJ = 11296
DT_COL0 = 9216
GATE_COL0 = DT_COL0 + N_DT
EPS = 1e-6
N_DEV = 8

ADAM_LR = 0.001
ADAM_B1 = 0.9
ADAM_B2 = 0.999
ADAM_EPS = 1e-08
ADAM_WD = 0.01
ADAM_STEP = 10

LANES = 128
VMEM_LIMIT = 56 * 1024 * 1024

CB_Q, CB_K, CB_V, CB_ZA, CB_ZS, CB_XBC = 0, 8, 16, 24, 32, 48

NT_DIMS = (((1,), (1,)), ((), ()))
TN_DIMS = (((0,), (0,)), ((), ()))


def _params(sem):
    return pltpu.CompilerParams(dimension_semantics=sem, vmem_limit_bytes=VMEM_LIMIT)


def _dot(a, b):
    return jnp.dot(a, b, preferred_element_type=F32)


def _dot_nt(a, b):
    return lax.dot_general(a, b, NT_DIMS, preferred_element_type=F32)


def _dot_tn(a, b):
    return lax.dot_general(a, b, TN_DIMS, preferred_element_type=F32)


def _bf(a):
    return a.astype(BF16)


def _sigmoid(x):
    return 0.5 + 0.5 * jnp.tanh(0.5 * x)


def _softplus(x):
    return jnp.maximum(x, 0.0) + jnp.log(1.0 + jnp.exp(-jnp.abs(x)))


def _trunc_split(a):
    bits = lax.bitcast_convert_type(a, jnp.uint32) & jnp.uint32(0xFFFF0000)
    hi = lax.bitcast_convert_type(bits, F32)
    return _bf(hi), _bf(a - hi)


MM_TILE = 1024
MM_TK = 2048


def _matmul(a, b, *, name, nt=False, ta=False, add=None, out_dtype=F32, tm=MM_TILE, tn=MM_TILE, tk=MM_TK, n=None):
    k, m = a.shape if ta else a.shape[::-1]
    n = (b.shape[0] if nt else b.shape[1]) if n is None else n
    tm, tn, tk = min(tm, m), min(tn, n), min(tk, k)
    assert m % tm == 0 and n % tn == 0 and k % tk == 0, (name, a.shape, b.shape)
    assert not (ta and nt)
    nk = k // tk
    has_add = add is not None

    def body(*refs):
        if has_add:
            a_ref, b_ref, add_ref, o_ref, acc_ref = refs
        else:
            a_ref, b_ref, o_ref, acc_ref = refs
        kk = pl.program_id(2)
        av, bv = _bf(a_ref[...]), _bf(b_ref[...])
        d = _dot_nt(av, bv) if nt else _dot_tn(av, bv) if ta else _dot(av, bv)
        if nk == 1:
            o_ref[...] = ((d + add_ref[...]) if has_add else d).astype(out_dtype)
            return

        @pl.when(kk == 0)
        def _():
            acc_ref[...] = jnp.zeros_like(acc_ref)

        acc_ref[...] += d

        @pl.when(kk == nk - 1)
        def _():
            r = acc_ref[...]
            if has_add:
                r = r + add_ref[...]
            o_ref[...] = r.astype(out_dtype)

    in_specs = [pl.BlockSpec((tk, tm), lambda j, i, kk: (kk, i)) if ta
                else pl.BlockSpec((tm, tk), lambda j, i, kk: (i, kk)),
                pl.BlockSpec((tn, tk), lambda j, i, kk: (j, kk)) if nt
                else pl.BlockSpec((tk, tn), lambda j, i, kk: (kk, j))]
    args = [a, b]
    if has_add:
        in_specs.append(pl.BlockSpec((tm, tn), lambda j, i, kk: (i, j)))
        args.append(add)
    return pl.pallas_call(
        body, grid=(n // tn, m // tm, nk), in_specs=in_specs,
        out_specs=pl.BlockSpec((tm, tn), lambda j, i, kk: (i, j)),
        out_shape=jax.ShapeDtypeStruct((m, n), out_dtype),
        scratch_shapes=[pltpu.VMEM((tm, tn), F32)],
        compiler_params=_params(("parallel", "parallel", "arbitrary")), name=name)(*args)


def _rms_fwd(x, w, *, tm=512):
    t = x.shape[0]

    def body(x_ref, w_ref, h_ref, r_ref):
        xv = x_ref[...]
        r = lax.rsqrt(jnp.mean(xv * xv, axis=-1, keepdims=True) + EPS)
        h_ref[...] = _bf(xv * r * w_ref[...])
        r_ref[...] = r

    row = lambda i: (i, 0)
    return pl.pallas_call(
        body, grid=(t // tm,),
        in_specs=[pl.BlockSpec((tm, D_MODEL), row), pl.BlockSpec((1, D_MODEL), lambda i: (0, 0))],
        out_specs=[pl.BlockSpec((tm, D_MODEL), row), pl.BlockSpec((tm, 1), row)],
        out_shape=[jax.ShapeDtypeStruct((t, D_MODEL), BF16), jax.ShapeDtypeStruct((t, 1), F32)],
        compiler_params=_params(("parallel",)), name="rms_fwd")(x, w)


def _rms_bwd(dh, x, r, w, dres, *, tm=512):
    t = x.shape[0]

    def body(dh_ref, x_ref, r_ref, w_ref, dres_ref, dx_ref, dw_ref):
        @pl.when(pl.program_id(0) == 0)
        def _():
            dw_ref[...] = jnp.zeros_like(dw_ref)

        nrm = x_ref[...] * r_ref[...]
        dhv = dh_ref[...]
        dw_ref[...] += jnp.sum(dhv * nrm, axis=0, keepdims=True)
        dn = dhv * w_ref[...]
        dx = r_ref[...] * (dn - nrm * jnp.mean(dn * nrm, axis=-1, keepdims=True))
        dx_ref[...] = dx + dres_ref[...]

    row = lambda i: (i, 0)
    fix = lambda i: (0, 0)
    return pl.pallas_call(
        body, grid=(t // tm,),
        in_specs=[pl.BlockSpec((tm, D_MODEL), row), pl.BlockSpec((tm, D_MODEL), row),
                  pl.BlockSpec((tm, 1), row), pl.BlockSpec((1, D_MODEL), fix),
                  pl.BlockSpec((tm, D_MODEL), row)],
        out_specs=[pl.BlockSpec((tm, D_MODEL), row), pl.BlockSpec((1, D_MODEL), fix)],
        out_shape=[jax.ShapeDtypeStruct((t, D_MODEL), F32), jax.ShapeDtypeStruct((1, D_MODEL), F32)],
        compiler_params=_params(("arbitrary",)), name="rms_bwd")(dh, x, r, w, dres)


ATT_BLK = 256


def _attn_masks(blk):
    r_io = lax.broadcasted_iota(jnp.int32, (blk, blk), 0)
    c_io = lax.broadcasted_iota(jnp.int32, (blk, blk), 1)
    return r_io, c_io


def _attn_tiles(nq):
    return nq * (nq + 1) // 2


def _attn_fwd(proj, nb, s):
    blk = min(ATT_BLK, s)
    nq = s // blk
    nt = _attn_tiles(nq)
    scale = HEAD_DIM ** -0.5

    def body(q_ref, k_ref, v_ref, o_ref, a_out, sg_out, q0_s, q1_s, kb_s, v0_s, v1_s, z_s, a_s, acc_s, cr_s):
        head0 = lax.broadcasted_iota(jnp.int32, (s, LANES), 1) < HEAD_DIM
        r_io, c_io = _attn_masks(blk)
        tri = c_io < r_io
        tri2 = jnp.concatenate([tri, tri], axis=0)
        u_gt = _bf(r_io > c_io)
        u2 = jnp.concatenate([u_gt, u_gt], axis=0)
        qv = q_ref[...] * scale
        q0_s[...] = _bf(jnp.where(head0, qv, 0.0))
        q1_s[...] = _bf(jnp.where(head0, 0.0, qv))
        kb_s[...] = _bf(k_ref[...])
        vv = v_ref[...]
        v0_s[...] = _bf(jnp.where(head0, vv, 0.0))
        v1_s[...] = _bf(jnp.where(head0, 0.0, vv))

        def scores(qcat, kblk, slot):
            k0 = pl.multiple_of(jnp.maximum(kblk, 0) * blk, blk)
            z_s[slot] = _dot_nt(qcat, kb_s[pl.ds(k0, blk), :])

        def weights(slot, tile, diag, zslot=None):
            z = z_s[slot if zslot is None else zslot]
            cr = cr_s[...]
            sp = _softplus(z)
            spm = jnp.where(tri2, sp, 0.0) if diag else sp
            hi, lo = _trunc_split(spm)
            ps = _dot(jnp.concatenate([hi, lo], axis=1), u2)
            lb = z - sp
            a = jnp.exp(lb - (ps + cr))
            if diag:
                a = jnp.where(tri2, a, 0.0)
            ab = _bf(a)
            a_s[slot] = ab
            a_out[tile] = ab
            sg_out[tile] = _bf(jnp.exp(lb))
            cr_s[...] = cr + ps[:, 0:1] + spm[:, 0:1]

        def apply(slot, k0):
            a = a_s[slot]
            vcat = jnp.concatenate([v0_s[pl.ds(k0, blk), :], v1_s[pl.ds(k0, blk), :]], axis=0)
            acc_s[...] += _dot(jnp.concatenate([a[:blk], a[blk:]], axis=1), vcat)

        def queries(qi):
            q0 = pl.multiple_of(qi * blk, blk)
            return jnp.concatenate([q0_s[pl.ds(q0, blk), :], q1_s[pl.ds(q0, blk), :]], axis=0)

        def first_scores(qi):
            qcat = queries(qi)
            scores(qcat, qi, 2)
            scores(qcat, qi - 1, 3)

        first_scores(jnp.int32(0))

        def qblock(qi, _):
            q0 = pl.multiple_of(qi * blk, blk)
            tile0 = qi * (qi + 1) // 2
            qcat = queries(qi)
            z_s[1] = z_s[3]
            cr_s[...] = jnp.zeros_like(cr_s)
            acc_s[...] = jnp.zeros_like(acc_s)
            weights(0, tile0 + qi, True, zslot=2)
            first_scores(jnp.minimum(qi + 1, nq - 1))

            def step(j, slot):
                scores(qcat, qi - j - 1, 1 - slot)
                weights(slot, tile0 + qi - j, False)
                apply(1 - slot, pl.multiple_of((qi - j + 1) * blk, blk))

            def pair(i, _):
                step(2 * i + 1, 1)
                step(2 * i + 2, 0)
                return 0

            lax.fori_loop(0, qi // 2, pair, 0)

            @pl.when(qi % 2 == 1)
            def _():
                step(qi, 1)

            apply(qi & 1, 0)
            o_ref[pl.ds(q0, blk), :] = acc_s[...]
            return 0

        lax.fori_loop(0, nq, qblock, 0)

    t = nb * s
    tiles = pl.BlockSpec((None, None, nt, 2 * blk, blk), lambda b, p: (b, p, 0, 0, 0))
    tiles_shape = jax.ShapeDtypeStruct((nb, 8, nt, 2 * blk, blk), BF16)
    return pl.pallas_call(
        body, grid=(nb, 8),
        in_specs=[pl.BlockSpec((s, LANES), lambda b, p: (b, CB_Q + p)),
                  pl.BlockSpec((s, LANES), lambda b, p: (b, CB_K + p)),
                  pl.BlockSpec((s, LANES), lambda b, p: (b, CB_V + p))],
        out_specs=[pl.BlockSpec((s, LANES), lambda b, p: (b, p)), tiles, tiles],
        out_shape=[jax.ShapeDtypeStruct((t, SB_WIDTH), F32), tiles_shape, tiles_shape],
        scratch_shapes=[pltpu.VMEM((s, LANES), BF16)] * 5 + [
            pltpu.VMEM((4, 2 * blk, blk), F32), pltpu.VMEM((2, 2 * blk, blk), BF16),
            pltpu.VMEM((blk, LANES), F32), pltpu.VMEM((2 * blk, 1), F32)],
        compiler_params=_params(("parallel", "parallel")), name="attn_fwd")(proj, proj, proj)


def _attn_bwd(proj, do, a_all, sg_all, nb, s):
    blk = min(ATT_BLK, s)
    nq = s // blk
    nt = _attn_tiles(nq)
    scale = HEAD_DIM ** -0.5

    def body(q_ref, k_ref, v_ref, do_ref, a_in, sg_in, dq_ref, dk_ref, dv_ref,
             q0_s, q1_s, k0_s, k1_s, vb_s, d0_s, d1_s, da_s, dz_s, dk_acc, dv_acc, dq_acc, cw_s):
        head0 = lax.broadcasted_iota(jnp.int32, (s, LANES), 1) < HEAD_DIM
        r_io, c_io = _attn_masks(blk)
        tri = c_io < r_io
        tri2 = jnp.concatenate([tri, tri], axis=0)
        u_lt = _bf(r_io < c_io)
        qv = q_ref[...] * scale
        q0_s[...] = _bf(jnp.where(head0, qv, 0.0))
        q1_s[...] = _bf(jnp.where(head0, 0.0, qv))
        kv = k_ref[...]
        k0_s[...] = _bf(jnp.where(head0, kv, 0.0))
        k1_s[...] = _bf(jnp.where(head0, 0.0, kv))
        vb_s[...] = _bf(v_ref[...])
        dov = do_ref[...]
        d0_s[...] = _bf(jnp.where(head0, dov, 0.0))
        d1_s[...] = _bf(jnp.where(head0, 0.0, dov))
        dk_acc[...] = jnp.zeros_like(dk_acc)
        dv_acc[...] = jnp.zeros_like(dv_acc)

        def qblock(qi, _):
            q0 = pl.multiple_of(qi * blk, blk)
            tile0 = qi * (qi + 1) // 2
            qs_ = pl.ds(q0, blk)
            qcat = jnp.concatenate([q0_s[qs_, :], q1_s[qs_, :]], axis=0)
            dcat = jnp.concatenate([d0_s[qs_, :], d1_s[qs_, :]], axis=0)

            def scores(kblk, slot):
                da_s[slot] = _dot_nt(dcat, vb_s[pl.ds(pl.multiple_of(kblk * blk, blk), blk), :])

            def grads(slot, kblk, diag):
                cw = cw_s[...]
                w = a_in[tile0 + kblk].astype(F32) * da_s[slot]
                wex = _dot(_bf(w), u_lt)
                dz = w - sg_in[tile0 + kblk].astype(F32) * (w + (wex + cw))
                if diag:
                    dz = jnp.where(tri2, dz, 0.0)
                dz_s[slot] = _bf(dz)
                cw_s[...] = cw + wex[:, blk - 1:blk] + w[:, blk - 1:blk]

            def apply(slot, kblk):
                ks_ = pl.ds(pl.multiple_of(kblk * blk, blk), blk)
                dz = dz_s[slot]
                dq_acc[...] += _dot(jnp.concatenate([dz[:blk], dz[blk:]], axis=1),
                                    jnp.concatenate([k0_s[ks_, :], k1_s[ks_, :]], axis=0))
                dk_acc[ks_, :] += _dot_tn(dz, qcat)
                dv_acc[ks_, :] += _dot_tn(a_in[tile0 + kblk], dcat)

            def step(j, slot, diag):
                if not diag:
                    scores(j + 1, 1 - slot)
                grads(slot, j, diag)

                @pl.when(j > 0)
                def _():
                    apply(1 - slot, j - 1)

            scores(0, 0)
            dq_acc[...] = jnp.zeros_like(dq_acc)
            cw_s[...] = jnp.zeros_like(cw_s)

            def pair(i, _):
                step(2 * i, 0, False)
                step(2 * i + 1, 1, False)
                return 0

            lax.fori_loop(0, qi // 2, pair, 0)

            @pl.when(qi % 2 == 1)
            def _():
                step(qi - 1, 0, False)
                step(qi, 1, True)

            @pl.when(qi % 2 == 0)
            def _():
                step(qi, 0, True)

            apply(qi & 1, qi)
            dq_ref[qs_, :] = _bf(dq_acc[...] * scale)
            return 0

        lax.fori_loop(0, nq, qblock, 0)
        dk_ref[...] = _bf(dk_acc[...])
        dv_ref[...] = _bf(dv_acc[...])

    t = nb * s
    hp = lambda b, p: (b, p)
    out = jax.ShapeDtypeStruct((t, SB_WIDTH), BF16)
    tiles = pl.BlockSpec((None, None, nt, 2 * blk, blk), lambda b, p: (b, p, 0, 0, 0))
    return pl.pallas_call(
        body, grid=(nb, 8),
        in_specs=[pl.BlockSpec((s, LANES), lambda b, p: (b, CB_Q + p)),
                  pl.BlockSpec((s, LANES), lambda b, p: (b, CB_K + p)),
                  pl.BlockSpec((s, LANES), lambda b, p: (b, CB_V + p)),
                  pl.BlockSpec((s, LANES), hp), tiles, tiles],
        out_specs=[pl.BlockSpec((s, LANES), hp)] * 3,
        out_shape=[out, out, out],
        scratch_shapes=[pltpu.VMEM((s, LANES), BF16)] * 7 + [
            pltpu.VMEM((2, 2 * blk, blk), F32), pltpu.VMEM((2, 2 * blk, blk), BF16),
            pltpu.VMEM((s, LANES), F32), pltpu.VMEM((s, LANES), F32),
            pltpu.VMEM((blk, LANES), F32), pltpu.VMEM((2 * blk, 1), F32)],
        compiler_params=_params(("parallel", "parallel")), name="attn_bwd")(proj, proj, proj, do, a_all, sg_all)


CONV_TC = 256
CONV_ROWS = 64
HALO = 8


def _conv_rows(ws, w_ref, b_ref, n):
    taps = [ws[HALO - kk:HALO - kk + n, :] for kk in range(CONV_K)]
    y = b_ref[...] + w_ref[CONV_K - 1:CONV_K, :] * taps[0]
    for kk in range(1, CONV_K):
        y = y + w_ref[CONV_K - 1 - kk:CONV_K - kk, :] * taps[kk]
    return y, taps


def _conv_fwd(proj, conv_w, conv_b, nb, s):
    tc, rows = CONV_TC, min(CONV_ROWS, s)
    cb0 = CB_XBC * LANES // tc

    def body(x_ref, w_ref, b_ref, o_ref, xp, ws):
        xp[0:HALO, :] = jnp.zeros((HALO, tc), F32)
        xp[HALO:HALO + s, :] = x_ref[...].astype(F32)

        def chunk(i, _):
            r0 = pl.multiple_of(i * rows, rows)
            ws[...] = xp[pl.ds(r0, rows + HALO), :]
            y, _ = _conv_rows(ws, w_ref, b_ref, rows)
            o_ref[pl.ds(r0, rows), :] = y * _sigmoid(y)
            return 0

        lax.fori_loop(0, s // rows, chunk, 0)

    t = nb * s
    return pl.pallas_call(
        body, grid=(nb, CONV_DIM // tc),
        in_specs=[pl.BlockSpec((s, tc), lambda b, c: (b, cb0 + c)),
                  pl.BlockSpec((CONV_K, tc), lambda b, c: (0, c)),
                  pl.BlockSpec((1, tc), lambda b, c: (0, c))],
        out_specs=pl.BlockSpec((s, tc), lambda b, c: (b, c)),
        out_shape=jax.ShapeDtypeStruct((t, CONV_DIM), F32),
        scratch_shapes=[pltpu.VMEM((s + HALO, tc), F32), pltpu.VMEM((rows + HALO, tc), F32)],
        compiler_params=_params(("parallel", "parallel")), name="conv_fwd")(proj, conv_w, conv_b)


def _conv_bwd(proj, dact, conv_w, conv_b, nb, s, *, col0, name):
    width = dact.shape[1]
    tc, rows = min(CONV_TC, width), min(CONV_ROWS, s)
    cb0 = (CB_XBC * LANES + col0) // tc
    wb0 = col0 // tc

    def body(x_ref, d_ref, w_ref, b_ref, dx_ref, dw_ref, db_ref, xp, dp, dys, ws):
        @pl.when(pl.program_id(1) == 0)
        def _():
            dw_ref[...] = jnp.zeros_like(dw_ref)
            db_ref[...] = jnp.zeros_like(db_ref)

        n = rows + HALO
        pad = jnp.zeros((HALO, tc), F32)
        xp[0:HALO, :] = pad
        xp[HALO:HALO + s, :] = x_ref[...].astype(F32)
        xp[HALO + s:HALO + s + HALO, :] = pad
        dp[0:s, :] = d_ref[...]
        dp[s:s + HALO, :] = pad

        def chunk(i, acc):
            r0 = pl.multiple_of(i * rows, rows)
            ws[...] = xp[pl.ds(r0, n + HALO), :]
            y, taps = _conv_rows(ws, w_ref, b_ref, n)
            sg = _sigmoid(y)
            dy = dp[pl.ds(r0, n), :] * (sg * (1.0 + y * (1.0 - sg)))
            dys[...] = dy
            dx = w_ref[CONV_K - 1:CONV_K, :] * dy[0:rows]
            for kk in range(1, CONV_K):
                dx = dx + w_ref[CONV_K - 1 - kk:CONV_K - kk, :] * dys[kk:kk + rows, :]
            dx_ref[pl.ds(r0, rows), :] = _bf(dx)
            dyc = dy[0:rows]
            new = [acc[0] + jnp.sum(dyc, axis=0, keepdims=True)]
            for kk in range(CONV_K):
                new.append(acc[1 + kk] + jnp.sum(dyc * taps[kk][0:rows], axis=0, keepdims=True))
            return tuple(new)

        acc = lax.fori_loop(0, s // rows, chunk, (jnp.zeros((1, tc), F32),) * (1 + CONV_K))
        db_ref[...] += acc[0]
        for kk in range(CONV_K):
            dw_ref[CONV_K - 1 - kk:CONV_K - kk, :] += acc[1 + kk]

    t = nb * s
    return pl.pallas_call(
        body, grid=(width // tc, nb),
        in_specs=[pl.BlockSpec((s, tc), lambda c, b: (b, cb0 + c)),
                  pl.BlockSpec((s, tc), lambda c, b: (b, c)),
                  pl.BlockSpec((CONV_K, tc), lambda c, b: (0, wb0 + c)),
                  pl.BlockSpec((1, tc), lambda c, b: (0, wb0 + c))],
        out_specs=[pl.BlockSpec((s, tc), lambda c, b: (b, c)),
                   pl.BlockSpec((CONV_K, tc), lambda c, b: (0, c)),
                   pl.BlockSpec((1, tc), lambda c, b: (0, c))],
        out_shape=[jax.ShapeDtypeStruct((t, width), BF16),
                   jax.ShapeDtypeStruct((CONV_K, width), F32),
                   jax.ShapeDtypeStruct((1, width), F32)],
        scratch_shapes=[pltpu.VMEM((s + 2 * HALO, tc), F32), pltpu.VMEM((s + HALO, tc), F32),
                        pltpu.VMEM((rows + HALO, tc), F32), pltpu.VMEM((rows + 2 * HALO, tc), F32)],
        compiler_params=_params(("parallel", "arbitrary")), name=name)(proj, dact, conv_w, conv_b)


L = SSD_CHUNK
N_PAIR = SSD_HPG // 2


def _pair_lanes(v0, v1, lane):
    return jnp.where(lane < HEAD_DIM, v0, v1)


def _onehot(shape, row_shift, col_shift):
    row = lax.broadcasted_iota(jnp.int32, shape, 0)
    col = lax.broadcasted_iota(jnp.int32, shape, 1)
    return _bf(lax.shift_right_logical(row, row_shift) == lax.shift_right_logical(col, col_shift))


def _ssd_tables():
    sp = _onehot((LANES, N_PAIR * LANES), 0, 6)
    sf = _onehot((LANES, SSD_HPG * LANES), 0, 7)
    return (jnp.concatenate([sp] * 3, axis=0), jnp.concatenate([sf] * 3, axis=0),
            _onehot((N_PAIR * LANES, LANES), 6, 0), _onehot((SSD_HPG * LANES, LANES), 7, 0))


def _split3(a):
    a1 = _bf(a)
    r1 = a - a1.astype(F32)
    a2 = _bf(r1)
    return jnp.concatenate([a1, a2, _bf(r1 - a2.astype(F32))], axis=1)


def _split2(a):
    hi, lo = _trunc_split(a)
    return jnp.concatenate([hi, lo], axis=1)


def _tri_dot(tri_bf, a):
    a1 = _bf(a)
    r1 = a - a1.astype(F32)
    a2 = _bf(r1)
    parts = jnp.concatenate([a1, a2, _bf(r1 - a2.astype(F32))], axis=0)
    return _dot(jnp.concatenate([tri_bf] * 3, axis=1), parts)


HEAD_W = 4 * LANES + 2 * N_PAIR * LANES + SSD_HPG * LANES


def _ssd_head(dtr_ref, bias_ref, a_neg, c, tri_bf, spread_pair, spread_full, head_s, slot):
    raw = dtr_ref[pl.ds(pl.multiple_of(c * L, L), L), :] + bias_ref[...]
    dt = _softplus(raw)
    acs = _tri_dot(tri_bf, dt * a_neg)
    acs3 = _split3(acs)
    both = _dot(jnp.concatenate([_split3(dt), acs3], axis=0), spread_pair)
    parts = (raw, dt, acs, acs.T, both[:L], both[L:], _dot(acs3, spread_full))
    off = 0
    for p in parts:
        head_s[slot, :, off:off + p.shape[1]] = p
        off += p.shape[1]


def _ssd_head_load(head_s, slot):
    out, off = [], 0
    for w in (LANES, LANES, LANES, L, N_PAIR * LANES, N_PAIR * LANES, SSD_HPG * LANES):
        out.append(head_s[slot, :, off:off + w])
        off += w
    return out


def _ssd_fwd(xbc, dtr, bias, alog, dskip, nb, s):
    nc = s // L
    t = nb * s

    def body(x_ref, b_ref, c_ref, dtr_ref, bias_ref, alog_ref, dsk_ref, y_ref, hs_ref, h_scr, head_s):
        lane = lax.broadcasted_iota(jnp.int32, (L, LANES), 1)
        lane1 = lax.broadcasted_iota(jnp.int32, (1, LANES), 1)
        rowc = lax.broadcasted_iota(jnp.int32, (L, 1), 0)
        rr = lax.broadcasted_iota(jnp.int32, (L, L), 0)
        cc = lax.broadcasted_iota(jnp.int32, (L, L), 1)
        causal = rr >= cc
        h_scr[...] = jnp.zeros_like(h_scr)
        dsk = dsk_ref[...]
        spread_pair, spread_full, _, _ = _ssd_tables()
        a_neg = -jnp.exp(alog_ref[...])
        head = functools.partial(_ssd_head, dtr_ref, bias_ref, a_neg, tri_bf=_bf(cc <= rr),
                                 spread_pair=spread_pair, spread_full=spread_full, head_s=head_s)
        head(jnp.int32(0), slot=0)

        def chunk(c, slot):
            r0 = pl.multiple_of(c * L, L)
            head(jnp.minimum(c + 1, nc - 1), slot=1 - slot)
            _, _, acs, acs_t, dt_all, acs_all, acs_full = _ssd_head_load(head_s, slot)
            bb = _bf(b_ref[pl.ds(r0, L), :])
            cb = _bf(c_ref[pl.ds(r0, L), :])
            gm = _dot_nt(cb, bb)
            for pr in range(N_PAIR):
                j0, j1 = 2 * pr, 2 * pr + 1
                ms = []
                for j in (j0, j1):
                    seg = acs_full[:, j * LANES:(j + 1) * LANES] - acs_t[j:j + 1, :]
                    ms.append(_bf(gm * jnp.exp(jnp.where(causal, seg, -1e30))))
                x = x_ref[pl.ds(r0, L), pr * LANES:(pr + 1) * LANES]
                dt_p = dt_all[:, pr * LANES:(pr + 1) * LANES]
                acs_p = acs_all[:, pr * LANES:(pr + 1) * LANES]
                last_p = acs_p[L - 1:L, :]
                d_p = _pair_lanes(dsk[:, j0:j0 + 1], dsk[:, j1:j1 + 1], lane1)
                xd = x * dt_p
                xdb = _bf(xd)
                yds = _dot(jnp.concatenate(ms, axis=0), xdb)
                yd = _pair_lanes(yds[:L], yds[L:], lane)
                hp = h_scr[pr]
                yo = _dot_nt(cb, _bf(hp)) * jnp.exp(acs_p)
                y_ref[pl.ds(r0, L), pr * LANES:(pr + 1) * LANES] = yd + yo + x * d_p
                hs_ref[c, pr] = hp
                sp = _dot_tn(_bf(xd * jnp.exp(last_p - acs_p)), bb)
                cd = jnp.where(rowc < HEAD_DIM, jnp.exp(acs[L - 1:L, j0:j0 + 1]),
                               jnp.exp(acs[L - 1:L, j1:j1 + 1]))
                h_scr[pr] = hp * cd + sp

        def two_chunks(i, _):
            chunk(2 * i, 0)
            chunk(2 * i + 1, 1)
            return 0

        lax.fori_loop(0, nc // 2, two_chunks, 0)

    assert nc % 2 == 0
    grp = lambda b, g: (g, 0, 0)
    return pl.pallas_call(
        body, grid=(nb, SSD_GROUPS),
        in_specs=[pl.BlockSpec((s, 4 * LANES), lambda b, g: (b, g)),
                  pl.BlockSpec((s, LANES), lambda b, g: (b, 16 + g)),
                  pl.BlockSpec((s, LANES), lambda b, g: (b, 20 + g)),
                  pl.BlockSpec((None, s, LANES), lambda b, g: (g, b, 0)),
                  pl.BlockSpec((None, 1, LANES), grp), pl.BlockSpec((None, 1, LANES), grp),
                  pl.BlockSpec((None, 1, LANES), grp)],
        out_specs=[pl.BlockSpec((s, 4 * LANES), lambda b, g: (b, g)),
                   pl.BlockSpec((None, None, nc, N_PAIR, L, LANES), lambda b, g: (b, g, 0, 0, 0, 0))],
        out_shape=[jax.ShapeDtypeStruct((t, SSD_WIDTH), F32),
                   jax.ShapeDtypeStruct((nb, SSD_GROUPS, nc, N_PAIR, L, LANES), F32)],
        scratch_shapes=[pltpu.VMEM((N_PAIR, L, LANES), F32), pltpu.VMEM((2, L, HEAD_W), F32)],
        compiler_params=_params(("parallel", "parallel")), name="ssd_fwd")(
            xbc, xbc, xbc, dtr, bias, alog, dskip)


def _ssd_bwd(xbc, dtr, bias, alog, dskip, hs, dy, nb, s):
    nc = s // L
    t = nb * s

    def body(x_ref, b_ref, c_ref, dtr_ref, bias_ref, alog_ref, dsk_ref, hs_ref, dy_ref,
             dx_ref, db_ref, dc_ref, ddtr_ref, dbias_ref, dalog_ref, ddsk_ref, dh_scr, head_s):
        lane = lax.broadcasted_iota(jnp.int32, (L, LANES), 1)
        lane1 = lax.broadcasted_iota(jnp.int32, (1, LANES), 1)
        rowc = lax.broadcasted_iota(jnp.int32, (L, 1), 0)
        rr = lax.broadcasted_iota(jnp.int32, (L, L), 0)
        cc = lax.broadcasted_iota(jnp.int32, (L, L), 1)
        causal = rr >= cc
        tri_t = _bf(cc >= rr)
        dh_scr[...] = jnp.zeros_like(dh_scr)
        dsk = dsk_ref[...]
        a_neg = -jnp.exp(alog_ref[...])

        @pl.when(pl.program_id(1) == 0)
        def _():
            dbias_ref[...] = jnp.zeros_like(dbias_ref)
            dalog_ref[...] = jnp.zeros_like(dalog_ref)
            ddsk_ref[...] = jnp.zeros_like(ddsk_ref)

        spread_pair, spread_full, sum_pair, sum_full = _ssd_tables()
        sum_pair2 = jnp.concatenate([sum_pair] * 2, axis=0)
        sum_full2 = jnp.concatenate([sum_full] * 2, axis=0)
        head = functools.partial(_ssd_head, dtr_ref, bias_ref, a_neg, tri_bf=_bf(cc <= rr),
                                 spread_pair=spread_pair, spread_full=spread_full, head_s=head_s)
        head(jnp.int32(nc - 1), slot=0)

        def chunk(c, slot):
            r0 = pl.multiple_of(c * L, L)
            head(jnp.maximum(c - 1, 0), slot=1 - slot)
            raw, dt, acs, acs_t, dt_all, acs_all, acs_full = _ssd_head_load(head_s, slot)
            bb = _bf(b_ref[pl.ds(r0, L), :])
            cb = _bf(c_ref[pl.ds(r0, L), :])
            gm = _dot_nt(cb, bb)
            dg = jnp.zeros((L, L), F32)
            dbacc = jnp.zeros((L, LANES), F32)
            dcacc = jnp.zeros((L, LANES), F32)
            dacs_t = jnp.zeros((L, L), F32)
            ta_all, te_all, qm_all, tc_all, td_all, dcd_all = [], [], [], [], [], []
            for pr in range(N_PAIR):
                js = (2 * pr, 2 * pr + 1)
                lms, mfs = [], []
                for j in js:
                    seg = acs_full[:, j * LANES:(j + 1) * LANES] - acs_t[j:j + 1, :]
                    lm = jnp.exp(jnp.where(causal, seg, -1e30))
                    lms.append(lm)
                    mfs.append(gm * lm)
                x = x_ref[pl.ds(r0, L), pr * LANES:(pr + 1) * LANES]
                dyv = dy_ref[pl.ds(r0, L), pr * LANES:(pr + 1) * LANES]
                dt_p = dt_all[:, pr * LANES:(pr + 1) * LANES]
                acs_p = acs_all[:, pr * LANES:(pr + 1) * LANES]
                last_p = acs_p[L - 1:L, :]
                d_p = _pair_lanes(dsk[:, js[0]:js[0] + 1], dsk[:, js[1]:js[1] + 1], lane1)
                e_p = jnp.exp(acs_p)
                dte_p = jnp.exp(last_p - acs_p)
                xd = x * dt_p
                xdb = _bf(xd)
                hp = hs_ref[c, pr]
                hb = _bf(hp)
                dye = _bf(dyv * e_p)
                yo = _dot_nt(cb, hb) * e_p
                dcacc = dcacc + _dot(dye, hb)
                dh_out = _dot_tn(dye, cb)
                dyb = _bf(dyv)
                dms = _dot_nt(_bf(jnp.concatenate(
                    [jnp.where(lane < HEAD_DIM, dyv, 0.0), jnp.where(lane < HEAD_DIM, 0.0, dyv)], axis=0)), xdb)
                mty = _dot_tn(_bf(jnp.concatenate(mfs, axis=1)), dyb)
                dxd = jnp.where(lane < HEAD_DIM, mty[:L], mty[L:])
                for hh in range(2):
                    dm = dms[hh * L:(hh + 1) * L]
                    dg = dg + dm * lms[hh]
                    qm = dm * mfs[hh]
                    qm_all.append(qm)
                    dacs_t = dacs_t - jnp.where(rr == js[hh], jnp.sum(qm, axis=0, keepdims=True), 0.0)
                ds = dh_scr[pr]
                dsb = _bf(ds)
                xdd = xd * dte_p
                dxdd = _dot_nt(bb, dsb)
                dbacc = dbacc + _dot(_bf(xdd), dsb)
                dxd = dxd + dxdd * dte_p
                cd = jnp.where(rowc < HEAD_DIM, jnp.exp(acs[L - 1:L, js[0]:js[0] + 1]),
                               jnp.exp(acs[L - 1:L, js[1]:js[1] + 1]))
                prod = ds * hp
                dcd_all += [jnp.sum(prod[:HEAD_DIM], axis=0, keepdims=True),
                            jnp.sum(prod[HEAD_DIM:], axis=0, keepdims=True)]
                dh_scr[pr] = dh_out + ds * cd
                tcv = dxdd * xdd
                ta_all.append(dyv * yo - tcv)
                tc_all.append(jnp.sum(tcv, axis=0, keepdims=True))
                te_all.append(dxd * x)
                td_all.append(jnp.sum(dyv * x, axis=0, keepdims=True))
                dx_ref[pl.ds(r0, L), pr * LANES:(pr + 1) * LANES] = dxd * dt_p + dyv * d_p
            dgb = _bf(dg)
            dc_ref[pl.ds(r0, L), :] = dcacc + _dot(dgb, bb)
            db_ref[pl.ds(r0, L), :] = dbacc + _dot_tn(dgb, cb)
            last_add = jnp.zeros((1, LANES), F32)
            dd_add = jnp.zeros((1, LANES), F32)
            cd_row = jnp.exp(acs[L - 1:L, :])
            for j in range(SSD_HPG):
                hm = (lane1 >= HEAD_DIM) if j % 2 else (lane1 < HEAD_DIM)
                tcj = jnp.sum(jnp.where(hm, tc_all[j // 2], 0.0), axis=1, keepdims=True)
                tdj = jnp.sum(jnp.where(hm, td_all[j // 2], 0.0), axis=1, keepdims=True)
                dcdj = jnp.sum(dcd_all[j], axis=1, keepdims=True)
                last_add = last_add + jnp.where(lane1 == j, tcj + dcdj * cd_row, 0.0)
                dd_add = dd_add + jnp.where(lane1 == j, tdj, 0.0)
            dacs = (_dot(_split2(jnp.concatenate(ta_all, axis=1)), sum_pair2)
                    + _dot(_split2(jnp.concatenate(qm_all, axis=1)), sum_full2)
                    + dacs_t.T + jnp.where(rowc == L - 1, last_add, 0.0))
            ddta = _tri_dot(tri_t, dacs)
            ddt = _dot(_bf(jnp.concatenate(te_all, axis=1)), sum_pair) + ddta * a_neg
            ddraw = ddt * _sigmoid(raw)
            ddtr_ref[pl.ds(r0, L), :] = ddraw
            dbias_ref[...] += jnp.sum(ddraw, axis=0, keepdims=True)
            dalog_ref[...] += jnp.sum(ddta * dt, axis=0, keepdims=True) * a_neg
            ddsk_ref[...] += dd_add

        def two_chunks(i, _):
            chunk(nc - 1 - 2 * i, 0)
            chunk(nc - 2 - 2 * i, 1)
            return 0

        lax.fori_loop(0, nc // 2, two_chunks, 0)

    assert nc % 2 == 0
    grp = lambda g, b: (g, 0, 0)
    small = jax.ShapeDtypeStruct((SSD_GROUPS, 1, LANES), F32)
    return pl.pallas_call(
        body, grid=(SSD_GROUPS, nb),
        in_specs=[pl.BlockSpec((s, 4 * LANES), lambda g, b: (b, g)),
                  pl.BlockSpec((s, LANES), lambda g, b: (b, 16 + g)),
                  pl.BlockSpec((s, LANES), lambda g, b: (b, 20 + g)),
                  pl.BlockSpec((None, s, LANES), lambda g, b: (g, b, 0)),
                  pl.BlockSpec((None, 1, LANES), grp), pl.BlockSpec((None, 1, LANES), grp),
                  pl.BlockSpec((None, 1, LANES), grp),
                  pl.BlockSpec((None, None, nc, N_PAIR, L, LANES), lambda g, b: (b, g, 0, 0, 0, 0)),
                  pl.BlockSpec((s, 4 * LANES), lambda g, b: (b, g))],
        out_specs=[pl.BlockSpec((s, 4 * LANES), lambda g, b: (b, g)),
                   pl.BlockSpec((s, LANES), lambda g, b: (b, g)),
                   pl.BlockSpec((s, LANES), lambda g, b: (b, g)),
                   pl.BlockSpec((None, s, LANES), lambda g, b: (g, b, 0)),
                   pl.BlockSpec((None, 1, LANES), grp), pl.BlockSpec((None, 1, LANES), grp),
                   pl.BlockSpec((None, 1, LANES), grp)],
        out_shape=[jax.ShapeDtypeStruct((t, SSD_WIDTH), F32),
                   jax.ShapeDtypeStruct((t, SSD_GROUPS * SSD_STATE), F32),
                   jax.ShapeDtypeStruct((t, SSD_GROUPS * SSD_STATE), F32),
                   jax.ShapeDtypeStruct((SSD_GROUPS, t, LANES), F32), small, small, small],
        scratch_shapes=[pltpu.VMEM((N_PAIR, L, LANES), F32), pltpu.VMEM((2, L, HEAD_W), F32)],
        compiler_params=_params(("parallel", "arbitrary")), name="ssd_bwd")(
            xbc, xbc, xbc, dtr, bias, alog, dskip, hs, dy)


GN = SSD_WIDTH // SSD_GROUPS


def _post_fwd(o, proj, y, nw, *, tm=256):
    t = o.shape[0]

    def body(o_ref, za_ref, y_ref, zs_ref, nw_ref, ya_ref, ys_ref):
        za = za_ref[...].astype(F32)
        ya_ref[...] = _bf(o_ref[...] * (za * _sigmoid(za)))
        zs = zs_ref[...].astype(F32)
        yz = y_ref[...] * (zs * _sigmoid(zs))
        for g in range(SSD_GROUPS):
            sl = slice(g * GN, (g + 1) * GN)
            v = yz[:, sl]
            r = lax.rsqrt(jnp.mean(v * v, axis=-1, keepdims=True) + EPS)
            ys_ref[:, sl] = _bf(v * r * nw_ref[:, sl])

    return pl.pallas_call(
        body, grid=(t // tm,),
        in_specs=[pl.BlockSpec((tm, SB_WIDTH), lambda i: (i, 0)),
                  pl.BlockSpec((tm, SB_WIDTH), lambda i: (i, CB_ZA * LANES // SB_WIDTH)),
                  pl.BlockSpec((tm, SSD_WIDTH), lambda i: (i, 0)),
                  pl.BlockSpec((tm, SSD_WIDTH), lambda i: (i, CB_ZS * LANES // SSD_WIDTH)),
                  pl.BlockSpec((1, SSD_WIDTH), lambda i: (0, 0))],
        out_specs=[pl.BlockSpec((tm, SB_WIDTH), lambda i: (i, 0)),
                   pl.BlockSpec((tm, SSD_WIDTH), lambda i: (i, 0))],
        out_shape=[jax.ShapeDtypeStruct((t, SB_WIDTH), BF16), jax.ShapeDtypeStruct((t, SSD_WIDTH), BF16)],
        compiler_params=_params(("parallel",)), name="post_fwd")(o, proj, y, proj, nw)


def _post_bwd(dya, o, proj, dys, y, nw, *, tm=256):
    t = o.shape[0]

    def body(dya_ref, o_ref, za_ref, dys_ref, y_ref, zs_ref, nw_ref,
             do_ref, dza_ref, dy_ref, dzs_ref, dnw_ref):
        @pl.when(pl.program_id(0) == 0)
        def _():
            dnw_ref[...] = jnp.zeros_like(dnw_ref)

        za = za_ref[...].astype(F32)
        sa = _sigmoid(za)
        d = dya_ref[...].astype(F32)
        do_ref[...] = _bf(d * (za * sa))
        dza_ref[...] = _bf(d * o_ref[...] * (sa * (1.0 + za * (1.0 - sa))))
        zs = zs_ref[...].astype(F32)
        ss = _sigmoid(zs)
        silu = zs * ss
        yv = y_ref[...]
        yz = yv * silu
        dv = dys_ref[...].astype(F32)
        for g in range(SSD_GROUPS):
            sl = slice(g * GN, (g + 1) * GN)
            v = yz[:, sl]
            r = lax.rsqrt(jnp.mean(v * v, axis=-1, keepdims=True) + EPS)
            nrm = v * r
            dg = dv[:, sl]
            dnw_ref[:, sl] += jnp.sum(dg * nrm, axis=0, keepdims=True)
            dn = dg * nw_ref[:, sl]
            dyz = r * (dn - nrm * jnp.mean(dn * nrm, axis=-1, keepdims=True))
            dy_ref[:, sl] = dyz * silu[:, sl]
            dzs_ref[:, sl] = _bf(dyz * yv[:, sl] * (ss[:, sl] * (1.0 + zs[:, sl] * (1.0 - ss[:, sl]))))

    a_spec = pl.BlockSpec((tm, SB_WIDTH), lambda i: (i, 0))
    s_spec = pl.BlockSpec((tm, SSD_WIDTH), lambda i: (i, 0))
    w_spec = pl.BlockSpec((1, SSD_WIDTH), lambda i: (0, 0))
    return pl.pallas_call(
        body, grid=(t // tm,),
        in_specs=[a_spec, a_spec, pl.BlockSpec((tm, SB_WIDTH), lambda i: (i, CB_ZA * LANES // SB_WIDTH)),
                  s_spec, s_spec, pl.BlockSpec((tm, SSD_WIDTH), lambda i: (i, CB_ZS * LANES // SSD_WIDTH)),
                  w_spec],
        out_specs=[a_spec, a_spec, s_spec, s_spec, w_spec],
        out_shape=[jax.ShapeDtypeStruct((t, SB_WIDTH), BF16), jax.ShapeDtypeStruct((t, SB_WIDTH), BF16),
                   jax.ShapeDtypeStruct((t, SSD_WIDTH), F32), jax.ShapeDtypeStruct((t, SSD_WIDTH), BF16),
                   jax.ShapeDtypeStruct((1, SSD_WIDTH), F32)],
        compiler_params=_params(("arbitrary",)), name="post_bwd")(dya, o, proj, dys, y, proj, nw)


def _merge_fwd(ya, ys, proj_g, *, tm=512):
    t = ya.shape[0]

    def body(ya_ref, ys_ref, ga_ref, gs_ref, m_ref):
        m_ref[...] = _bf(_sigmoid(ga_ref[...].astype(F32)) * ya_ref[...].astype(F32)
                         + _sigmoid(gs_ref[...].astype(F32)) * ys_ref[...].astype(F32))

    spec = pl.BlockSpec((tm, D_MODEL), lambda i: (i, 0))
    return pl.pallas_call(
        body, grid=(t // tm,),
        in_specs=[spec, spec, pl.BlockSpec((tm, D_MODEL), lambda i: (i, 0)),
                  pl.BlockSpec((tm, D_MODEL), lambda i: (i, 1))],
        out_specs=spec, out_shape=jax.ShapeDtypeStruct((t, D_MODEL), BF16),
        compiler_params=_params(("parallel",)), name="merge_fwd")(ya, ys, proj_g, proj_g)


def _merge_bwd(dm, ya, ys, proj_g, *, tm=512):
    t = ya.shape[0]

    def body(dm_ref, ya_ref, ys_ref, ga_ref, gs_ref, dya_ref, dys_ref, dg_ref):
        d = dm_ref[...].astype(F32)
        ga = _sigmoid(ga_ref[...].astype(F32))
        gs = _sigmoid(gs_ref[...].astype(F32))
        dya_ref[...] = _bf(d * ga)
        dys_ref[...] = _bf(d * gs)
        dg_ref[:, :D_MODEL] = _bf(d * ya_ref[...].astype(F32) * (ga * (1.0 - ga)))
        dg_ref[:, D_MODEL:] = _bf(d * ys_ref[...].astype(F32) * (gs * (1.0 - gs)))

    spec = pl.BlockSpec((tm, D_MODEL), lambda i: (i, 0))
    return pl.pallas_call(
        body, grid=(t // tm,),
        in_specs=[spec, spec, spec, pl.BlockSpec((tm, D_MODEL), lambda i: (i, 0)),
                  pl.BlockSpec((tm, D_MODEL), lambda i: (i, 1))],
        out_specs=[spec, spec, pl.BlockSpec((tm, 2 * D_MODEL), lambda i: (i, 0))],
        out_shape=[jax.ShapeDtypeStruct((t, D_MODEL), BF16), jax.ShapeDtypeStruct((t, D_MODEL), BF16),
                   jax.ShapeDtypeStruct((t, 2 * D_MODEL), BF16)],
        compiler_params=_params(("parallel",)), name="merge_bwd")(dm, ya, ys, proj_g, proj_g)


def _final(x, out, wf, target, *, tm=512):
    t = x.shape[0]

    def body(x_ref, o_ref, w_ref, t_ref, loss_ref, dx_ref, dw_ref):
        @pl.when(pl.program_id(0) == 0)
        def _():
            loss_ref[...] = jnp.zeros_like(loss_ref)
            dw_ref[...] = jnp.zeros_like(dw_ref)

        x2 = x_ref[...] + o_ref[...]
        r = lax.rsqrt(jnp.mean(x2 * x2, axis=-1, keepdims=True) + EPS)
        nrm = x2 * r
        e = nrm * w_ref[...] - t_ref[...]
        row_loss = jnp.mean(e * e, axis=-1, keepdims=True)
        loss_ref[...] += 0.5 * jnp.sum(row_loss, axis=0, keepdims=True)
        dyv = e * (1.0 / D_MODEL)
        dw_ref[...] += jnp.sum(dyv * nrm, axis=0, keepdims=True)
        dn = dyv * w_ref[...]
        dx_ref[...] = r * (dn - nrm * jnp.mean(dn * nrm, axis=-1, keepdims=True))

    spec = pl.BlockSpec((tm, D_MODEL), lambda i: (i, 0))
    fix = pl.BlockSpec((1, D_MODEL), lambda i: (0, 0))
    return pl.pallas_call(
        body, grid=(t // tm,),
        in_specs=[spec, spec, fix, spec],
        out_specs=[pl.BlockSpec((1, LANES), lambda i: (0, 0)), spec, fix],
        out_shape=[jax.ShapeDtypeStruct((1, LANES), F32), jax.ShapeDtypeStruct((t, D_MODEL), F32),
                   jax.ShapeDtypeStruct((1, D_MODEL), F32)],
        compiler_params=_params(("arbitrary",)), name="final_loss")(x, out, wf, target)


def _group_lanes(v):
    return jnp.pad(v.reshape(SSD_GROUPS, 1, SSD_HPG), ((0, 0), (0, 0), (0, LANES - SSD_HPG)))


def _ungroup_lanes(v):
    return v[:, :, :SSD_HPG].reshape(1, N_DT)


def _row_slabs(pieces, rows):
    total = sum(p.shape[0] for p in pieces)
    assert total % rows == 0
    slabs = []
    for lo in range(0, total, rows):
        parts, off = [], 0
        for p in pieces:
            a, b = max(lo, off), min(lo + rows, off + p.shape[0])
            if a < b:
                parts.append(p[a - off:b - off])
            off += p.shape[0]
        slabs.append(parts[0] if len(parts) == 1 else jnp.concatenate(parts, axis=0))
    return jnp.stack(slabs)


def _local_step(x, target, nb, s, w_in_t, conv_w, conv_b, norm_w, dt_bias, a_log, d_skip,
                ssm_norm_w, out_weights, final_norm_w, between=None):
    t = nb * s
    w_g_t = w_in_t[GATE_COL0:]
    w_dt_t = jnp.pad(w_in_t[DT_COL0:GATE_COL0], ((0, LANES - N_DT), (0, 0)))
    h, r1 = _rms_fwd(x, norm_w)
    proj = _matmul(h, w_in_t, nt=True, n=DT_COL0, out_dtype=BF16, name="proj_main")
    proj_g = _matmul(h, w_g_t, nt=True, out_dtype=BF16, name="proj_gate")
    dt_raw = _matmul(h, w_dt_t, nt=True, name="proj_dt")
    dtr = jnp.pad(dt_raw[:, :N_DT].reshape(t, SSD_GROUPS, SSD_HPG).transpose(1, 0, 2),
                  ((0, 0), (0, 0), (0, LANES - SSD_HPG)))
    bias_g, alog_g, dsk_g = _group_lanes(dt_bias), _group_lanes(a_log), _group_lanes(d_skip)

    o, att_a, att_sg = _attn_fwd(proj, nb, s)
    xbc = _conv_fwd(proj, conv_w, conv_b, nb, s)
    y, hs = _ssd_fwd(xbc, dtr, bias_g, alog_g, dsk_g, nb, s)
    ya_in, ys_in = _post_fwd(o, proj, y, ssm_norm_w)
    w_attn_out, w_ssm_out, w_o = out_weights(ys_in) if callable(out_weights) else out_weights
    ya = _matmul(ya_in, w_attn_out, out_dtype=BF16, name="attn_out")
    ys = _matmul(ys_in, w_ssm_out, out_dtype=BF16, name="ssm_out")
    merged = _merge_fwd(ya, ys, proj_g)
    out = _matmul(merged, w_o, name="out_proj")
    loss, dx2, d_final_w = _final(x, out, final_norm_w, target)

    dx2b = _bf(dx2)
    dmerged = _matmul(dx2b, w_o, nt=True, out_dtype=BF16, name="d_merged")
    d_w_o = _matmul(merged, dx2b, ta=True, name="d_w_o")
    dya, dys, dgate = _merge_bwd(dmerged, ya, ys, proj_g)
    d_w_attn_out = _matmul(ya_in, dya, ta=True, name="d_w_attn_out")
    d_w_ssm_out = _matmul(ys_in, dys, ta=True, name="d_w_ssm_out")
    dya_in = _matmul(dya, w_attn_out, nt=True, out_dtype=BF16, name="d_ya_in")
    dys_in = _matmul(dys, w_ssm_out, nt=True, name="d_ys_in")
    do, dza, dy, dzs, d_ssm_norm_w = _post_bwd(dya_in, o, proj, dys_in, y, ssm_norm_w)
    dq, dk, dv = _attn_bwd(proj, do, att_a, att_sg, nb, s)
    dxs_act, db_act, dc_act, ddtr, dbias_g, dalog_g, ddsk_g = _ssd_bwd(
        xbc, dtr, bias_g, alog_g, dsk_g, hs, dy, nb, s)
    dxs, dwx, dbx = _conv_bwd(proj, dxs_act, conv_w, conv_b, nb, s, col0=0, name="conv_bwd_x")
    dbm, dwb, dbb = _conv_bwd(proj, db_act, conv_w, conv_b, nb, s, col0=SSD_WIDTH, name="conv_bwd_b")
    dcm, dwc, dbc = _conv_bwd(proj, dc_act, conv_w, conv_b, nb, s, col0=SSD_WIDTH + 512, name="conv_bwd_c")
    dproj = jnp.concatenate([dq, dk, dv, dza, dzs, dxs, dbm, dcm], axis=1)
    ddt = jnp.pad(_bf(ddtr[:, :, :SSD_HPG].transpose(1, 0, 2).reshape(t, N_DT)), ((0, 0), (0, LANES - N_DT)))
    wt = dict(ta=True, out_dtype=BF16)
    d_w_in_t = _row_slabs(
        [_matmul(dproj, h, name="d_w_main", **wt), _matmul(ddt, h, name="d_w_dt", **wt)[:N_DT],
         _matmul(dgate, h, name="d_w_gate", **wt)], SHARD)
    grads = dict(
        w_in_t=d_w_in_t,
        conv_w=jnp.concatenate([dwx, dwb, dwc], axis=1), conv_b=jnp.concatenate([dbx, dbb, dbc], axis=1),
        dt_bias=_ungroup_lanes(dbias_g), a_log=_ungroup_lanes(dalog_g), d_skip=_ungroup_lanes(ddsk_g),
        ssm_norm_w=d_ssm_norm_w, w_attn_out=d_w_attn_out, w_ssm_out=d_w_ssm_out, w_o=d_w_o,
        final_norm_w=d_final_w)
    extra = None
    if between is not None:
        zero, extra = between(grads)
        ddt = ddt + _bf(zero[0:1, 0:1])

    dh = _matmul(ddt, w_dt_t, name="d_h_dt")
    dh = _matmul(dgate, w_g_t, add=dh, name="d_h_gate")
    dh = _matmul(dproj, w_in_t, add=dh, name="d_h", tk=DT_COL0 // 4)
    grad_x, grads["norm_w"] = _rms_bwd(dh, x, r1, norm_w, dx2)
    return loss, grad_x, grads, extra


N_CHIP = N_DEV // 2


def _pair_exchange(scatter, bcast, *, name):
    arrays = list(scatter) + list(bcast)
    ns, n = len(scatter), len(arrays)
    n_sem = ns * N_CHIP + (n - ns) * (N_DEV - 1)

    def body(*refs):
        ins, outs = refs[:n], refs[n:2 * n]
        send_sems, recv_sems, local_sems = refs[2 * n:]
        x, y, c = lax.axis_index("x"), lax.axis_index("y"), lax.axis_index("c")
        me = 4 * x + 2 * y + c
        copies, k = [], 0
        for a in range(ns):
            for q in range(N_CHIP):
                cp = pltpu.make_async_remote_copy(
                    src_ref=ins[a].at[2 * q + 1 - c], dst_ref=outs[a].at[q],
                    send_sem=send_sems.at[k], recv_sem=recv_sems.at[k],
                    device_id=(x, y, 1 - c), device_id_type=pl.DeviceIdType.MESH)
                cp.start()
                copies.append(cp)
                k += 1
        for a in range(ns, n):
            cp = pltpu.make_async_copy(ins[a], outs[a].at[me], local_sems.at[a - ns])
            cp.start()
            copies.append(cp)
            for r in range(1, N_DEV):
                tx = 1 - x if (r >> 2) & 1 else x
                ty = 1 - y if (r >> 1) & 1 else y
                tc = 1 - c if r & 1 else c
                cp = pltpu.make_async_remote_copy(
                    src_ref=ins[a], dst_ref=outs[a].at[me], send_sem=send_sems.at[k], recv_sem=recv_sems.at[k],
                    device_id=(tx, ty, tc), device_id_type=pl.DeviceIdType.MESH)
                cp.start()
                copies.append(cp)
                k += 1
        for cp in copies:
            cp.wait()

    out_shape = [jax.ShapeDtypeStruct((N_CHIP,) + v.shape[1:] if i < ns else (N_DEV,) + v.shape, v.dtype)
                 for i, v in enumerate(arrays)]
    return pl.pallas_call(
        body, in_specs=[pl.BlockSpec(memory_space=pl.ANY)] * n,
        out_specs=[pl.BlockSpec(memory_space=pl.ANY)] * n, out_shape=out_shape,
        scratch_shapes=[pltpu.SemaphoreType.DMA((n_sem,)), pltpu.SemaphoreType.DMA((n_sem,)),
                        pltpu.SemaphoreType.DMA((max(n - ns, 1),))],
        compiler_params=pltpu.CompilerParams(has_side_effects=True), name=name)(*arrays)


def _pair_add(mine, recv, core, *, name, tc=None):
    _, r, c = recv.shape
    tc = c if tc is None else tc
    assert c % tc == 0

    def body(core_ref, m_ref, r_ref, o_ref):
        o_ref[...] = (m_ref[...].astype(F32) + r_ref[...].astype(F32)).astype(o_ref.dtype)

    spec = pl.BlockSpec((None, r, tc), lambda q, i, core_ref: (q, 0, i))
    return pl.pallas_call(
        body, grid_spec=pltpu.PrefetchScalarGridSpec(
            num_scalar_prefetch=1, grid=(N_CHIP, c // tc),
            in_specs=[pl.BlockSpec((None, r, tc), lambda q, i, core_ref: (2 * q + core_ref[0], 0, i)), spec],
            out_specs=spec),
        out_shape=jax.ShapeDtypeStruct(recv.shape, recv.dtype),
        compiler_params=_params(("parallel", "parallel")), name=name)(core, mine, recv)


_HBM = pl.BlockSpec(memory_space=pltpu.HBM)
_SEM = pl.BlockSpec(memory_space=pltpu.SEMAPHORE)
_DATAFLOW = pltpu.SideEffectType.DATAFLOW_SIDE_EFFECTING
_PEERS = {"chips": N_CHIP - 1, "all": N_DEV - 1}


def _split_copies(pattern, srcs, lands, send_sems, recv_sems):
    x, y, c = lax.axis_index("x"), lax.axis_index("y"), lax.axis_index("c")
    per = _PEERS[pattern]
    copies = []
    for a in range(len(srcs)):
        for r in range(1, per + 1):
            if pattern == "chips":
                tx = 1 - x if (r >> 1) & 1 else x
                ty = 1 - y if r & 1 else y
                tc = c
                src, dst = srcs[a].at[2 * tx + ty], lands[a].at[2 * x + y]
            else:
                tx = 1 - x if (r >> 2) & 1 else x
                ty = 1 - y if (r >> 1) & 1 else y
                tc = 1 - c if r & 1 else c
                src, dst = srcs[a], lands[a].at[4 * x + 2 * y + c]
            copies.append(pltpu.make_async_remote_copy(
                src_ref=src, dst_ref=dst, send_sem=send_sems.at[a * per + r - 1],
                recv_sem=recv_sems.at[a * per + r - 1], device_id=(tx, ty, tc), device_id_type=pl.DeviceIdType.MESH))
    return copies


def _split_start(pattern, arrays, *, name):
    n = len(arrays)
    n_sem = n * _PEERS[pattern]

    def body(*refs):
        srcs, lands = refs[:n], refs[n:2 * n]
        send_sems, recv_sems = refs[2 * n], refs[2 * n + 1]
        zero = refs[-1]
        for cp in _split_copies(pattern, srcs, lands, send_sems, recv_sems):
            cp.start()
        zero[...] = jnp.zeros_like(zero)

    src_t = [pltpu.HBM(v.shape, v.dtype) for v in arrays]
    land_t = [pltpu.HBM(v.shape if pattern == "chips" else (N_DEV,) + v.shape, v.dtype) for v in arrays]
    outs = pl.pallas_call(
        body, name=name,
        out_shape=(pltpu.SemaphoreType.DMA((n_sem,)), pltpu.SemaphoreType.DMA((n_sem,)), *src_t, *land_t,
                   jax.ShapeDtypeStruct((8, LANES), F32)),
        in_specs=[_HBM] * (2 * n),
        out_specs=(_SEM, _SEM, *[_HBM] * (2 * n), pl.BlockSpec(memory_space=pltpu.VMEM)),
        input_output_aliases={i: 2 + i for i in range(2 * n)},
        compiler_params=pltpu.CompilerParams(has_side_effects=_DATAFLOW))(
            *[pltpu.with_memory_space_constraint(v, pltpu.HBM) for v in arrays],
            *[pltpu.with_memory_space_constraint(lax.empty(t.shape, t.dtype), pltpu.HBM) for t in land_t])
    return outs[0], outs[1], outs[2:2 + n], outs[2 + n:2 + 2 * n], outs[-1]


def _split_wait(pattern, send_sems, recv_sems, srcs, lands, after, *, name):
    n = len(srcs)

    def body(*refs):
        for cp in _split_copies(pattern, refs[:n], refs[n:2 * n], refs[2 * n], refs[2 * n + 1]):
            cp.wait_send()
            cp.wait_recv()

    outs = pl.pallas_call(
        body, name=name, out_shape=[pltpu.HBM(v.shape, v.dtype) for v in (*srcs, *lands)],
        in_specs=[_HBM] * (2 * n) + [_SEM, _SEM, pl.BlockSpec(memory_space=pl.ANY)],
        out_specs=[_HBM] * (2 * n), input_output_aliases={i: i for i in range(2 * n)},
        compiler_params=pltpu.CompilerParams(has_side_effects=_DATAFLOW))(
            *srcs, *lands, send_sems, recv_sems, after)
    return outs[:n], outs[n:]


def _own_slab(lands, srcs, index, whole):
    out = []
    for land, src in zip(lands, srcs):
        own = src[None] if whole else lax.dynamic_slice(src, (index, 0, 0), (1,) + src.shape[1:])
        out.append(lax.dynamic_update_slice(land, own, (index, 0, 0)))
    return out


def _gather(arrays, *, name):
    n = len(arrays)
    per = N_DEV - 1

    def body(*refs):
        ins, outs = refs[:n], refs[n:2 * n]
        send_sems, recv_sems, local_sems = refs[2 * n:]
        x, y, c = lax.axis_index("x"), lax.axis_index("y"), lax.axis_index("c")
        me, sibling = (x, y, c), (x, y, 1 - c)
        chips = [(1 - x, y), (x, 1 - y), (1 - x, 1 - y)]

        def slab(a, block):
            return outs[a].at[4 * block[0] + 2 * block[1] + block[2]]

        def copy(a, k, block, to, src=None):
            rows = slab(a, block)
            return pltpu.make_async_remote_copy(
                src_ref=rows if src is None else src, dst_ref=rows,
                send_sem=send_sems.at[a * per + k], recv_sem=recv_sems.at[a * per + k],
                device_id=to, device_id_type=pl.DeviceIdType.MESH)

        started = []
        for a in range(n):
            mine = pltpu.make_async_copy(ins[a], slab(a, me), local_sems.at[a])
            mine.start()
            started.append(mine)
        for a in range(n):
            first = [copy(a, 1 + j, me, (*chip, c), src=ins[a]) for j, chip in enumerate(chips)]
            first.append(copy(a, 0, me, sibling, src=ins[a]))
            for cp in first:
                cp.start()
            started += first
        for j, chip in enumerate(chips):
            for a in range(n):
                copy(a, 1 + j, (*chip, c), me).wait_recv()
                passed = copy(a, 4 + j, (*chip, c), sibling)
                passed.start()
                started.append(passed)
        for a in range(n):
            copy(a, 0, sibling, me).wait_recv()
            for j, chip in enumerate(chips):
                copy(a, 4 + j, (*chip, 1 - c), me).wait_recv()
        for cp in started[:n]:
            cp.wait()
        for cp in started[n:]:
            cp.wait_send()

    return pl.pallas_call(
        body, in_specs=[pl.BlockSpec(memory_space=pl.ANY)] * n,
        out_specs=[pl.BlockSpec(memory_space=pl.ANY)] * n,
        out_shape=[jax.ShapeDtypeStruct((N_DEV,) + v.shape, v.dtype) for v in arrays],
        scratch_shapes=[pltpu.SemaphoreType.DMA((n * per,)), pltpu.SemaphoreType.DMA((n * per,)),
                        pltpu.SemaphoreType.DMA((n,))],
        compiler_params=pltpu.CompilerParams(has_side_effects=True), name=name)(*arrays)


def _adamw(parts, w, m, v, *, name, tc=None):
    r, c = w.shape
    n_parts = parts.shape[0]
    tc = c if tc is None else tc
    assert c % tc == 0

    def body(p_ref, w_ref, m_ref, v_ref, g_ref, d_ref, nm_ref, nv_ref):
        g = p_ref[0].astype(F32)
        for d in range(1, n_parts):
            g = g + p_ref[d].astype(F32)
        mm = ADAM_B1 * m_ref[...] + (1.0 - ADAM_B1) * g
        vv = ADAM_B2 * v_ref[...] + (1.0 - ADAM_B2) * (g * g)
        m_hat = mm / (1.0 - ADAM_B1 ** ADAM_STEP)
        v_hat = vv / (1.0 - ADAM_B2 ** ADAM_STEP)
        g_ref[...] = g
        d_ref[...] = -ADAM_LR * (m_hat / (jnp.sqrt(v_hat) + ADAM_EPS) + ADAM_WD * w_ref[...])
        nm_ref[...] = mm
        nv_ref[...] = vv

    spec = pl.BlockSpec((r, tc), lambda i: (0, i))
    out = jax.ShapeDtypeStruct((r, c), F32)
    return pl.pallas_call(
        body, grid=(c // tc,),
        in_specs=[pl.BlockSpec((n_parts, r, tc), lambda i: (0, 0, i)), spec, spec, spec],
        out_specs=[spec] * 4, out_shape=[out] * 4,
        compiler_params=_params(("parallel",)), name=name)(parts, w, m, v)


SMALL = (("norm_w", D_MODEL), ("conv_b", CONV_DIM), ("dt_bias", N_DT), ("a_log", N_DT),
         ("d_skip", N_DT), ("ssm_norm_w", SSD_WIDTH), ("final_norm_w", D_MODEL))
SMALL_ROWS = 64
SHARD = D_PROJ // N_DEV


def _pack_small(vals):
    flat = jnp.concatenate([vals[k].reshape(-1).astype(F32) for k, _ in SMALL])
    return jnp.pad(flat, (0, SMALL_ROWS * LANES - flat.shape[0])).reshape(SMALL_ROWS, LANES)


def _unpack_small(packed, shapes):
    flat = packed.reshape(-1)
    out, off = {}, 0
    for k, size in SMALL:
        out[k] = flat[off:off + size].reshape(shapes[k])
        off += size
    return out


def kernel(x, norm_w, w_in, conv_w, conv_b, dt_bias, a_log, d_skip, ssm_norm_w, w_attn_out, w_ssm_out, w_o, final_norm_w, loss_target, m_norm_w, m_w_in, m_conv_w, m_conv_b, m_dt_bias, m_a_log, m_d_skip, m_ssm_norm_w, m_w_attn_out, m_w_ssm_out, m_w_o, m_final_norm_w, v_norm_w, v_w_in, v_conv_w, v_conv_b, v_dt_bias, v_a_log, v_d_skip, v_ssm_norm_w, v_w_attn_out, v_w_ssm_out, v_w_o, v_final_norm_w):
    nb, s, _ = x.shape
    t = nb * s
    weights = dict(norm_w=norm_w, w_in=w_in, conv_w=conv_w, conv_b=conv_b, dt_bias=dt_bias, a_log=a_log,
                   d_skip=d_skip, ssm_norm_w=ssm_norm_w, w_attn_out=w_attn_out, w_ssm_out=w_ssm_out,
                   w_o=w_o, final_norm_w=final_norm_w)
    moms = dict(norm_w=m_norm_w, w_in=m_w_in, conv_w=m_conv_w, conv_b=m_conv_b, dt_bias=m_dt_bias,
                a_log=m_a_log, d_skip=m_d_skip, ssm_norm_w=m_ssm_norm_w, w_attn_out=m_w_attn_out,
                w_ssm_out=m_w_ssm_out, w_o=m_w_o, final_norm_w=m_final_norm_w)
    vels = dict(norm_w=v_norm_w, w_in=v_w_in, conv_w=v_conv_w, conv_b=v_conv_b, dt_bias=v_dt_bias,
                a_log=v_a_log, d_skip=v_d_skip, ssm_norm_w=v_ssm_norm_w, w_attn_out=v_w_attn_out,
                w_ssm_out=v_w_ssm_out, w_o=v_w_o, final_norm_w=v_final_norm_w)

    tr_ = lambda a: jnp.transpose(a[0])
    g_in, g_conv = _gather([_bf(tr_(w_in)), conv_w[0]], name="gather_weights")
    conv_full = g_conv.transpose(1, 0, 2).reshape(CONV_K, CONV_DIM)
    o_send, o_recv, o_srcs, o_lands, o_zero = _split_start(
        "all", [_bf(w_attn_out[0]), _bf(w_ssm_out[0]), _bf(w_o[0])], name="gather_out_weights_start")

    core = lax.axis_index("c").astype(jnp.int32).reshape(1)
    chip = 2 * lax.axis_index("x") + lax.axis_index("y")
    me = 2 * chip + lax.axis_index("c")

    def out_weights(after):
        srcs, lands = _split_wait("all", o_send, o_recv, o_srcs, o_lands, after, name="gather_out_weights_wait")
        g_wa, g_ws, g_wo = _own_slab(lands, srcs, me, True)
        return (g_wa.reshape(SB_WIDTH, D_MODEL), g_ws.reshape(SSD_WIDTH, D_MODEL), g_wo.reshape(D_MODEL, D_MODEL))

    def send_weight_grads(grads):
        slabs = [grads["w_in_t"],
                 grads["conv_w"].reshape(CONV_K, N_DEV, CONV_DIM // N_DEV).transpose(1, 0, 2),
                 _bf(grads["w_attn_out"].reshape(N_DEV, SB_WIDTH // N_DEV, D_MODEL)),
                 _bf(grads["w_ssm_out"].reshape(N_DEV, SSD_WIDTH // N_DEV, D_MODEL)),
                 _bf(grads["w_o"].reshape(N_DEV, D_MODEL // N_DEV, D_MODEL))]
        recv = _pair_exchange(slabs, [], name="exchange_pairs")
        names = ("w_in", "conv_w", "w_attn_out", "w_ssm_out", "w_o")
        sums = [_pair_add(a, b, core, name="pair_add_" + k, tc=4 * LANES if k == "w_in" else None)
                for k, a, b in zip(names, slabs, recv)]
        send_sems, recv_sems, srcs, lands, zero = _split_start("chips", sums, name="exchange_chips_start")
        return zero, (send_sems, recv_sems, srcs, lands)

    loss_part, grad_x, grads, (send_sems, recv_sems, srcs, lands) = _local_step(
        x.reshape(t, D_MODEL), loss_target.reshape(t, D_MODEL), nb, s, g_in.reshape(D_PROJ, D_MODEL),
        conv_full, conv_b, norm_w + o_zero[0:1, 0:1], dt_bias, a_log, d_skip, ssm_norm_w, out_weights,
        final_norm_w.reshape(1, D_MODEL), between=send_weight_grads)

    (p_small,) = _pair_exchange([], [_pack_small(grads)], name="exchange_small")
    srcs, lands = _split_wait("chips", send_sems, recv_sems, srcs, lands, grad_x, name="exchange_chips_wait")
    p_in, p_conv, p_wa, p_ws, p_wo = _own_slab(lands, srcs, chip, False)

    res = {}
    res["w_in"] = [a.T for a in _adamw(p_in, tr_(w_in), tr_(m_w_in), tr_(v_w_in), name="adamw_w_in", tc=2 * LANES)]
    res["conv_w"] = _adamw(p_conv, conv_w[0], m_conv_w[0], v_conv_w[0], name="adamw_conv_w")
    res["w_attn_out"] = _adamw(p_wa, w_attn_out[0], m_w_attn_out[0], v_w_attn_out[0], name="adamw_w_attn_out")
    res["w_ssm_out"] = _adamw(p_ws, w_ssm_out[0], m_w_ssm_out[0], v_w_ssm_out[0], name="adamw_w_ssm_out")
    res["w_o"] = _adamw(p_wo, w_o[0], m_w_o[0], v_w_o[0], name="adamw_w_o")
    sm = _adamw(p_small, _pack_small(weights), _pack_small(moms), _pack_small(vels), name="adamw_small")
    shapes = {k: weights[k].shape for k, _ in SMALL}
    sm = [_unpack_small(v, shapes) for v in sm]
    for k, _ in SMALL:
        res[k] = tuple(part[k] for part in sm)

    order = ("norm_w", "w_in", "conv_w", "conv_b", "dt_bias", "a_log", "d_skip", "ssm_norm_w",
             "w_attn_out", "w_ssm_out", "w_o", "final_norm_w")
    loss = lax.psum(loss_part[0, 0], ("x", "y", "c"))
    outs = [loss, grad_x.reshape(nb, s, D_MODEL)]
    for i in range(4):
        outs += [res[k][i].reshape(weights[k].shape) for k in order]
    return tuple(outs)
```

```python
import functools

import jax
import jax.numpy as jnp
from jax import lax
from jax.experimental import pallas as pl
from jax.experimental.pallas import tpu as pltpu

F32 = jnp.float32
BF16 = jnp.bfloat16

D_MODEL = 1024
SB_WIDTH = 1024
HEAD_DIM = 64
SSD_WIDTH = 2048
SSD_GROUPS = 4
SSD_HPG = 8
SSD_STATE = 128
SSD_CHUNK = 128
CONV_K = 4
CONV_DIM = 3072
N_DT = 32
D_PROJ = 11296
DT_COL0 = 9216
GATE_COL0 = DT_COL0 + N_DT
EPS = 1e-6
N_DEV = 8

ADAM_LR = 0.001
ADAM_B1 = 0.9
ADAM_B2 = 0.999
ADAM_EPS = 1e-08
ADAM_WD = 0.01
ADAM_STEP = 10

LANES = 128
VMEM_LIMIT = 56 * 1024 * 1024

CB_Q, CB_K, CB_V, CB_ZA, CB_ZS, CB_XBC = 0, 8, 16, 24, 32, 48

NT_DIMS = (((1,), (1,)), ((), ()))
TN_DIMS = (((0,), (0,)), ((), ()))


def _params(sem):
    return pltpu.CompilerParams(dimension_semantics=sem, vmem_limit_bytes=VMEM_LIMIT)


def _dot(a, b):
    return jnp.dot(a, b, preferred_element_type=F32)


def _dot_nt(a, b):
    return lax.dot_general(a, b, NT_DIMS, preferred_element_type=F32)


def _dot_tn(a, b):
    return lax.dot_general(a, b, TN_DIMS, preferred_element_type=F32)


def _bf(a):
    return a.astype(BF16)


def _sigmoid(x):
    return 0.5 + 0.5 * jnp.tanh(0.5 * x)


def _softplus(x):
    return jnp.maximum(x, 0.0) + jnp.log(1.0 + jnp.exp(-jnp.abs(x)))


def _trunc_split(a):
    bits = lax.bitcast_convert_type(a, jnp.uint32) & jnp.uint32(0xFFFF0000)
    hi = lax.bitcast_convert_type(bits, F32)
    return _bf(hi), _bf(a - hi)


MM_TILE = 1024
MM_TK = 2048


def _matmul(a, b, *, name, nt=False, ta=False, add=None, out_dtype=F32, tm=MM_TILE, tn=MM_TILE, tk=MM_TK, n=None):
    k, m = a.shape if ta else a.shape[::-1]
    n = (b.shape[0] if nt else b.shape[1]) if n is None else n
    tm, tn, tk = min(tm, m), min(tn, n), min(tk, k)
    assert m % tm == 0 and n % tn == 0 and k % tk == 0, (name, a.shape, b.shape)
    assert not (ta and nt)
    nk = k // tk
    has_add = add is not None

    def body(*refs):
        if has_add:
            a_ref, b_ref, add_ref, o_ref, acc_ref = refs
        else:
            a_ref, b_ref, o_ref, acc_ref = refs
        kk = pl.program_id(2)
        av, bv = _bf(a_ref[...]), _bf(b_ref[...])
        d = _dot_nt(av, bv) if nt else _dot_tn(av, bv) if ta else _dot(av, bv)
        if nk == 1:
            o_ref[...] = ((d + add_ref[...]) if has_add else d).astype(out_dtype)
            return

        @pl.when(kk == 0)
        def _():
            acc_ref[...] = jnp.zeros_like(acc_ref)

        acc_ref[...] += d

        @pl.when(kk == nk - 1)
        def _():
            r = acc_ref[...]
            if has_add:
                r = r + add_ref[...]
            o_ref[...] = r.astype(out_dtype)

    in_specs = [pl.BlockSpec((tk, tm), lambda j, i, kk: (kk, i)) if ta
                else pl.BlockSpec((tm, tk), lambda j, i, kk: (i, kk)),
                pl.BlockSpec((tn, tk), lambda j, i, kk: (j, kk)) if nt
                else pl.BlockSpec((tk, tn), lambda j, i, kk: (kk, j))]
    args = [a, b]
    if has_add:
        in_specs.append(pl.BlockSpec((tm, tn), lambda j, i, kk: (i, j)))
        args.append(add)
    return pl.pallas_call(
        body, grid=(n // tn, m // tm, nk), in_specs=in_specs,
        out_specs=pl.BlockSpec((tm, tn), lambda j, i, kk: (i, j)),
        out_shape=jax.ShapeDtypeStruct((m, n), out_dtype),
        scratch_shapes=[pltpu.VMEM((tm, tn), F32)],
        compiler_params=_params(("parallel", "parallel", "arbitrary")), name=name)(*args)


def _rms_fwd(x, w, *, tm=512):
    t = x.shape[0]

    def body(x_ref, w_ref, h_ref, r_ref):
        xv = x_ref[...]
        r = lax.rsqrt(jnp.mean(xv * xv, axis=-1, keepdims=True) + EPS)
        h_ref[...] = _bf(xv * r * w_ref[...])
        r_ref[...] = r

    row = lambda i: (i, 0)
    return pl.pallas_call(
        body, grid=(t // tm,),
        in_specs=[pl.BlockSpec((tm, D_MODEL), row), pl.BlockSpec((1, D_MODEL), lambda i: (0, 0))],
        out_specs=[pl.BlockSpec((tm, D_MODEL), row), pl.BlockSpec((tm, 1), row)],
        out_shape=[jax.ShapeDtypeStruct((t, D_MODEL), BF16), jax.ShapeDtypeStruct((t, 1), F32)],
        compiler_params=_params(("parallel",)), name="rms_fwd")(x, w)


def _rms_bwd(dh, x, r, w, dres, *, tm=512):
    t = x.shape[0]

    def body(dh_ref, x_ref, r_ref, w_ref, dres_ref, dx_ref, dw_ref):
        @pl.when(pl.program_id(0) == 0)
        def _():
            dw_ref[...] = jnp.zeros_like(dw_ref)

        nrm = x_ref[...] * r_ref[...]
        dhv = dh_ref[...]
        dw_ref[...] += jnp.sum(dhv * nrm, axis=0, keepdims=True)
        dn = dhv * w_ref[...]
        dx = r_ref[...] * (dn - nrm * jnp.mean(dn * nrm, axis=-1, keepdims=True))
        dx_ref[...] = dx + dres_ref[...]

    row = lambda i: (i, 0)
    fix = lambda i: (0, 0)
    return pl.pallas_call(
        body, grid=(t // tm,),
        in_specs=[pl.BlockSpec((tm, D_MODEL), row), pl.BlockSpec((tm, D_MODEL), row),
                  pl.BlockSpec((tm, 1), row), pl.BlockSpec((1, D_MODEL), fix),
                  pl.BlockSpec((tm, D_MODEL), row)],
        out_specs=[pl.BlockSpec((tm, D_MODEL), row), pl.BlockSpec((1, D_MODEL), fix)],
        out_shape=[jax.ShapeDtypeStruct((t, D_MODEL), F32), jax.ShapeDtypeStruct((1, D_MODEL), F32)],
        compiler_params=_params(("arbitrary",)), name="rms_bwd")(dh, x, r, w, dres)


ATT_BLK = 256


def _attn_masks(blk):
    r_io = lax.broadcasted_iota(jnp.int32, (blk, blk), 0)
    c_io = lax.broadcasted_iota(jnp.int32, (blk, blk), 1)
    return r_io, c_io


def _attn_tiles(nq):
    return nq * (nq + 1) // 2


def _attn_fwd(proj, nb, s):
    blk = min(ATT_BLK, s)
    nq = s // blk
    nt = _attn_tiles(nq)
    scale = HEAD_DIM ** -0.5

    def body(q_ref, k_ref, v_ref, o_ref, a_out, sg_out, q0_s, q1_s, kb_s, v0_s, v1_s, z_s, a_s, acc_s, cr_s):
        head0 = lax.broadcasted_iota(jnp.int32, (s, LANES), 1) < HEAD_DIM
        r_io, c_io = _attn_masks(blk)
        tri = c_io < r_io
        tri2 = jnp.concatenate([tri, tri], axis=0)
        u_gt = _bf(r_io > c_io)
        u2 = jnp.concatenate([u_gt, u_gt], axis=0)
        qv = q_ref[...] * scale
        q0_s[...] = _bf(jnp.where(head0, qv, 0.0))
        q1_s[...] = _bf(jnp.where(head0, 0.0, qv))
        kb_s[...] = _bf(k_ref[...])
        vv = v_ref[...]
        v0_s[...] = _bf(jnp.where(head0, vv, 0.0))
        v1_s[...] = _bf(jnp.where(head0, 0.0, vv))

        def scores(qcat, kblk, slot):
            k0 = pl.multiple_of(jnp.maximum(kblk, 0) * blk, blk)
            z_s[slot] = _dot_nt(qcat, kb_s[pl.ds(k0, blk), :])

        def weights(slot, tile, diag, zslot=None):
            z = z_s[slot if zslot is None else zslot]
            cr = cr_s[...]
            sp = _softplus(z)
            spm = jnp.where(tri2, sp, 0.0) if diag else sp
            hi, lo = _trunc_split(spm)
            ps = _dot(jnp.concatenate([hi, lo], axis=1), u2)
            lb = z - sp
            a = jnp.exp(lb - (ps + cr))
            if diag:
                a = jnp.where(tri2, a, 0.0)
            ab = _bf(a)
            a_s[slot] = ab
            a_out[tile] = ab
            sg_out[tile] = _bf(jnp.exp(lb))
            cr_s[...] = cr + ps[:, 0:1] + spm[:, 0:1]

        def apply(slot, k0):
            a = a_s[slot]
            vcat = jnp.concatenate([v0_s[pl.ds(k0, blk), :], v1_s[pl.ds(k0, blk), :]], axis=0)
            acc_s[...] += _dot(jnp.concatenate([a[:blk], a[blk:]], axis=1), vcat)

        def queries(qi):
            q0 = pl.multiple_of(qi * blk, blk)
            return jnp.concatenate([q0_s[pl.ds(q0, blk), :], q1_s[pl.ds(q0, blk), :]], axis=0)

        def first_scores(qi):
            qcat = queries(qi)
            scores(qcat, qi, 2)
            scores(qcat, qi - 1, 3)

        first_scores(jnp.int32(0))

        def qblock(qi, _):
            q0 = pl.multiple_of(qi * blk, blk)
            tile0 = qi * (qi + 1) // 2
            qcat = queries(qi)
            z_s[1] = z_s[3]
            cr_s[...] = jnp.zeros_like(cr_s)
            acc_s[...] = jnp.zeros_like(acc_s)
            weights(0, tile0 + qi, True, zslot=2)
            first_scores(jnp.minimum(qi + 1, nq - 1))

            def step(j, slot):
                scores(qcat, qi - j - 1, 1 - slot)
                weights(slot, tile0 + qi - j, False)
                apply(1 - slot, pl.multiple_of((qi - j + 1) * blk, blk))

            def pair(i, _):
                step(2 * i + 1, 1)
                step(2 * i + 2, 0)
                return 0

            lax.fori_loop(0, qi // 2, pair, 0)

            @pl.when(qi % 2 == 1)
            def _():
                step(qi, 1)

            apply(qi & 1, 0)
            o_ref[pl.ds(q0, blk), :] = acc_s[...]
            return 0

        lax.fori_loop(0, nq, qblock, 0)

    t = nb * s
    tiles = pl.BlockSpec((None, None, nt, 2 * blk, blk), lambda b, p: (b, p, 0, 0, 0))
    tiles_shape = jax.ShapeDtypeStruct((nb, 8, nt, 2 * blk, blk), BF16)
    return pl.pallas_call(
        body, grid=(nb, 8),
        in_specs=[pl.BlockSpec((s, LANES), lambda b, p: (b, CB_Q + p)),
                  pl.BlockSpec((s, LANES), lambda b, p: (b, CB_K + p)),
                  pl.BlockSpec((s, LANES), lambda b, p: (b, CB_V + p))],
        out_specs=[pl.BlockSpec((s, LANES), lambda b, p: (b, p)), tiles, tiles],
        out_shape=[jax.ShapeDtypeStruct((t, SB_WIDTH), F32), tiles_shape, tiles_shape],
        scratch_shapes=[pltpu.VMEM((s, LANES), BF16)] * 5 + [
            pltpu.VMEM((4, 2 * blk, blk), F32), pltpu.VMEM((2, 2 * blk, blk), BF16),
            pltpu.VMEM((blk, LANES), F32), pltpu.VMEM((2 * blk, 1), F32)],
        compiler_params=_params(("parallel", "parallel")), name="attn_fwd")(proj, proj, proj)


def _attn_bwd(proj, do, a_all, sg_all, nb, s):
    blk = min(ATT_BLK, s)
    nq = s // blk
    nt = _attn_tiles(nq)
    scale = HEAD_DIM ** -0.5

    def body(q_ref, k_ref, v_ref, do_ref, a_in, sg_in, dq_ref, dk_ref, dv_ref,
             qt0_s, qt1_s, k0_s, k1_s, vb_s, d0_s, d1_s, dt0_s, dt1_s, da_s, dz_s, dkt_acc, dvt_acc, dq_acc, cw_s):
        head0 = lax.broadcasted_iota(jnp.int32, (s, LANES), 1) < HEAD_DIM
        r_io, c_io = _attn_masks(blk)
        tri = c_io < r_io
        tri2 = jnp.concatenate([tri, tri], axis=0)
        u_lt = _bf(r_io < c_io)
        qv = q_ref[...].astype(F32) * scale
        qt0_s[...] = _bf(jnp.where(head0, qv, 0.0).T)
        qt1_s[...] = _bf(jnp.where(head0, 0.0, qv).T)
        kv = k_ref[...]
        k0_s[...] = _bf(jnp.where(head0, kv, 0.0))
        k1_s[...] = _bf(jnp.where(head0, 0.0, kv))
        vb_s[...] = _bf(v_ref[...])
        dov = do_ref[...].astype(F32)
        d0 = jnp.where(head0, dov, 0.0)
        d1 = jnp.where(head0, 0.0, dov)
        d0_s[...] = _bf(d0)
        d1_s[...] = _bf(d1)
        dt0_s[...] = _bf(d0.T)
        dt1_s[...] = _bf(d1.T)
        dkt_acc[...] = jnp.zeros_like(dkt_acc)
        dvt_acc[...] = jnp.zeros_like(dvt_acc)

        def qblock(qi, _):
            q0 = pl.multiple_of(qi * blk, blk)
            tile0 = qi * (qi + 1) // 2
            qs_ = pl.ds(q0, blk)
            dcat = jnp.concatenate([d0_s[qs_, :], d1_s[qs_, :]], axis=0)
            qcat_t = jnp.concatenate([qt0_s[:, qs_], qt1_s[:, qs_]], axis=1)
            dcat_t = jnp.concatenate([dt0_s[:, qs_], dt1_s[:, qs_]], axis=1)

            def scores(kblk, slot):
                da_s[slot] = _dot_nt(dcat, vb_s[pl.ds(pl.multiple_of(kblk * blk, blk), blk), :])

            def grads(slot, kblk, diag):
                cw = cw_s[...]
                w = a_in[tile0 + kblk].astype(F32) * da_s[slot]
                wex = _dot(_bf(w), u_lt)
                dz = w - sg_in[tile0 + kblk].astype(F32) * (w + (wex + cw))
                if diag:
                    dz = jnp.where(tri2, dz, 0.0)
                dz_s[slot] = _bf(dz)
                cw_s[...] = cw + wex[:, blk - 1:blk] + w[:, blk - 1:blk]

            def apply(slot, kblk):
                ks_ = pl.ds(pl.multiple_of(kblk * blk, blk), blk)
                dz = dz_s[slot]
                dq_acc[...] += _dot(jnp.concatenate([dz[:blk], dz[blk:]], axis=1),
                                    jnp.concatenate([k0_s[ks_, :], k1_s[ks_, :]], axis=0))
                dkt_acc[:, ks_] += _dot(qcat_t, dz)
                dvt_acc[:, ks_] += _dot(dcat_t, a_in[tile0 + kblk])

            def step(j, slot, diag):
                if not diag:
                    scores(j + 1, 1 - slot)
                grads(slot, j, diag)

                @pl.when(j > 0)
                def _():
                    apply(1 - slot, j - 1)

            scores(0, 0)
            dq_acc[...] = jnp.zeros_like(dq_acc)
            cw_s[...] = jnp.zeros_like(cw_s)

            def pair(i, _):
                step(2 * i, 0, False)
                step(2 * i + 1, 1, False)
                return 0

            lax.fori_loop(0, qi // 2, pair, 0)

            @pl.when(qi % 2 == 1)
            def _():
                step(qi - 1, 0, False)
                step(qi, 1, True)

            @pl.when(qi % 2 == 0)
            def _():
                step(qi, 0, True)

            apply(qi & 1, qi)
            dq_ref[qs_, :] = _bf(dq_acc[...] * scale)
            return 0

        lax.fori_loop(0, nq, qblock, 0)
        dk_ref[...] = _bf(dkt_acc[...].T)
        dv_ref[...] = _bf(dvt_acc[...].T)

    t = nb * s
    hp = lambda b, p: (b, p)
    out = jax.ShapeDtypeStruct((t, SB_WIDTH), BF16)
    tiles = pl.BlockSpec((None, None, nt, 2 * blk, blk), lambda b, p: (b, p, 0, 0, 0))
    return pl.pallas_call(
        body, grid=(nb, 8),
        in_specs=[pl.BlockSpec((s, LANES), lambda b, p: (b, CB_Q + p)),
                  pl.BlockSpec((s, LANES), lambda b, p: (b, CB_K + p)),
                  pl.BlockSpec((s, LANES), lambda b, p: (b, CB_V + p)),
                  pl.BlockSpec((s, LANES), hp), tiles, tiles],
        out_specs=[pl.BlockSpec((s, LANES), hp)] * 3,
        out_shape=[out, out, out],
        scratch_shapes=[pltpu.VMEM((LANES, s), BF16)] * 2 + [pltpu.VMEM((s, LANES), BF16)] * 5 + [
            pltpu.VMEM((LANES, s), BF16)] * 2 + [
            pltpu.VMEM((2, 2 * blk, blk), F32), pltpu.VMEM((2, 2 * blk, blk), BF16),
            pltpu.VMEM((LANES, s), F32), pltpu.VMEM((LANES, s), F32),
            pltpu.VMEM((blk, LANES), F32), pltpu.VMEM((2 * blk, 1), F32)],
        compiler_params=_params(("parallel", "parallel")), name="attn_bwd")(proj, proj, proj, do, a_all, sg_all)


CONV_TC = 256
CONV_ROWS = 64
HALO = 8


def _conv_rows(ws, w_ref, b_ref, n):
    taps = [ws[HALO - kk:HALO - kk + n, :] for kk in range(CONV_K)]
    y = b_ref[...] + w_ref[CONV_K - 1:CONV_K, :] * taps[0]
    for kk in range(1, CONV_K):
        y = y + w_ref[CONV_K - 1 - kk:CONV_K - kk, :] * taps[kk]
    return y, taps


def _conv_fwd(proj, conv_w, conv_b, nb, s):
    tc, rows = CONV_TC, min(CONV_ROWS, s)
    cb0 = CB_XBC * LANES // tc

    def body(x_ref, w_ref, b_ref, o_ref, xp, ws):
        xp[0:HALO, :] = jnp.zeros((HALO, tc), F32)
        xp[HALO:HALO + s, :] = x_ref[...].astype(F32)

        def chunk(i, _):
            r0 = pl.multiple_of(i * rows, rows)
            ws[...] = xp[pl.ds(r0, rows + HALO), :]
            y, _ = _conv_rows(ws, w_ref, b_ref, rows)
            o_ref[pl.ds(r0, rows), :] = y * _sigmoid(y)
            return 0

        lax.fori_loop(0, s // rows, chunk, 0)

    t = nb * s
    return pl.pallas_call(
        body, grid=(nb, CONV_DIM // tc),
        in_specs=[pl.BlockSpec((s, tc), lambda b, c: (b, cb0 + c)),
                  pl.BlockSpec((CONV_K, tc), lambda b, c: (0, c)),
                  pl.BlockSpec((1, tc), lambda b, c: (0, c))],
        out_specs=pl.BlockSpec((s, tc), lambda b, c: (b, c)),
        out_shape=jax.ShapeDtypeStruct((t, CONV_DIM), F32),
        scratch_shapes=[pltpu.VMEM((s + HALO, tc), F32), pltpu.VMEM((rows + HALO, tc), F32)],
        compiler_params=_params(("parallel", "parallel")), name="conv_fwd")(proj, conv_w, conv_b)


def _conv_bwd(proj, dact, conv_w, conv_b, nb, s, *, col0, name):
    width = dact.shape[1]
    tc, rows = min(CONV_TC, width), min(CONV_ROWS, s)
    cb0 = (CB_XBC * LANES + col0) // tc
    wb0 = col0 // tc

    def body(x_ref, d_ref, w_ref, b_ref, dx_ref, dw_ref, db_ref, xp, dp, dys, ws):
        @pl.when(pl.program_id(1) == 0)
        def _():
            dw_ref[...] = jnp.zeros_like(dw_ref)
            db_ref[...] = jnp.zeros_like(db_ref)

        n = rows + HALO
        pad = jnp.zeros((HALO, tc), F32)
        xp[0:HALO, :] = pad
        xp[HALO:HALO + s, :] = x_ref[...].astype(F32)
        xp[HALO + s:HALO + s + HALO, :] = pad
        dp[0:s, :] = d_ref[...]
        dp[s:s + HALO, :] = pad

        def chunk(i, acc):
            r0 = pl.multiple_of(i * rows, rows)
            ws[...] = xp[pl.ds(r0, n + HALO), :]
            y, taps = _conv_rows(ws, w_ref, b_ref, n)
            sg = _sigmoid(y)
            dy = dp[pl.ds(r0, n), :] * (sg * (1.0 + y * (1.0 - sg)))
            dys[...] = dy
            dx = w_ref[CONV_K - 1:CONV_K, :] * dy[0:rows]
            for kk in range(1, CONV_K):
                dx = dx + w_ref[CONV_K - 1 - kk:CONV_K - kk, :] * dys[kk:kk + rows, :]
            dx_ref[pl.ds(r0, rows), :] = _bf(dx)
            dyc = dy[0:rows]
            new = [acc[0] + jnp.sum(dyc, axis=0, keepdims=True)]
            for kk in range(CONV_K):
                new.append(acc[1 + kk] + jnp.sum(dyc * taps[kk][0:rows], axis=0, keepdims=True))
            return tuple(new)

        acc = lax.fori_loop(0, s // rows, chunk, (jnp.zeros((1, tc), F32),) * (1 + CONV_K))
        db_ref[...] += acc[0]
        for kk in range(CONV_K):
            dw_ref[CONV_K - 1 - kk:CONV_K - kk, :] += acc[1 + kk]

    t = nb * s
    return pl.pallas_call(
        body, grid=(width // tc, nb),
        in_specs=[pl.BlockSpec((s, tc), lambda c, b: (b, cb0 + c)),
                  pl.BlockSpec((s, tc), lambda c, b: (b, c)),
                  pl.BlockSpec((CONV_K, tc), lambda c, b: (0, wb0 + c)),
                  pl.BlockSpec((1, tc), lambda c, b: (0, wb0 + c))],
        out_specs=[pl.BlockSpec((s, tc), lambda c, b: (b, c)),
                   pl.BlockSpec((CONV_K, tc), lambda c, b: (0, c)),
                   pl.BlockSpec((1, tc), lambda c, b: (0, c))],
        out_shape=[jax.ShapeDtypeStruct((t, width), BF16),
                   jax.ShapeDtypeStruct((CONV_K, width), F32),
                   jax.ShapeDtypeStruct((1, width), F32)],
        scratch_shapes=[pltpu.VMEM((s + 2 * HALO, tc), F32), pltpu.VMEM((s + HALO, tc), F32),
                        pltpu.VMEM((rows + HALO, tc), F32), pltpu.VMEM((rows + 2 * HALO, tc), F32)],
        compiler_params=_params(("parallel", "arbitrary")), name=name)(proj, dact, conv_w, conv_b)


L = SSD_CHUNK
N_PAIR = SSD_HPG // 2


def _pair_lanes(v0, v1, lane):
    return jnp.where(lane < HEAD_DIM, v0, v1)


def _onehot(shape, row_shift, col_shift):
    row = lax.broadcasted_iota(jnp.int32, shape, 0)
    col = lax.broadcasted_iota(jnp.int32, shape, 1)
    return _bf(lax.shift_right_logical(row, row_shift) == lax.shift_right_logical(col, col_shift))


def _ssd_tables():
    sp = _onehot((LANES, N_PAIR * LANES), 0, 6)
    sf = _onehot((LANES, SSD_HPG * LANES), 0, 7)
    return (jnp.concatenate([sp] * 3, axis=0), jnp.concatenate([sf] * 3, axis=0),
            _onehot((N_PAIR * LANES, LANES), 6, 0), _onehot((SSD_HPG * LANES, LANES), 7, 0))


def _split3(a):
    a1 = _bf(a)
    r1 = a - a1.astype(F32)
    a2 = _bf(r1)
    return jnp.concatenate([a1, a2, _bf(r1 - a2.astype(F32))], axis=1)


def _split2(a):
    hi, lo = _trunc_split(a)
    return jnp.concatenate([hi, lo], axis=1)


def _tri_dot(tri_bf, a):
    a1 = _bf(a)
    r1 = a - a1.astype(F32)
    a2 = _bf(r1)
    parts = jnp.concatenate([a1, a2, _bf(r1 - a2.astype(F32))], axis=0)
    return _dot(jnp.concatenate([tri_bf] * 3, axis=1), parts)


HEAD_W = 4 * LANES + 2 * N_PAIR * LANES + SSD_HPG * LANES


def _ssd_head(dtr_ref, bias_ref, a_neg, c, tri_bf, spread_pair, spread_full, head_s, slot):
    raw = dtr_ref[pl.ds(pl.multiple_of(c * L, L), L), :] + bias_ref[...]
    dt = _softplus(raw)
    acs = _tri_dot(tri_bf, dt * a_neg)
    acs3 = _split3(acs)
    both = _dot(jnp.concatenate([_split3(dt), acs3], axis=0), spread_pair)
    parts = (raw, dt, acs, acs.T, both[:L], both[L:], _dot(acs3, spread_full))
    off = 0
    for p in parts:
        head_s[slot, :, off:off + p.shape[1]] = p
        off += p.shape[1]


def _ssd_head_load(head_s, slot):
    out, off = [], 0
    for w in (LANES, LANES, LANES, L, N_PAIR * LANES, N_PAIR * LANES, SSD_HPG * LANES):
        out.append(head_s[slot, :, off:off + w])
        off += w
    return out


def _ssd_fwd(xbc, dtr, bias, alog, dskip, nb, s):
    nc = s // L
    t = nb * s

    def body(x_ref, b_ref, c_ref, dtr_ref, bias_ref, alog_ref, dsk_ref, y_ref, hs_ref, h_scr, head_s):
        lane = lax.broadcasted_iota(jnp.int32, (L, LANES), 1)
        lane1 = lax.broadcasted_iota(jnp.int32, (1, LANES), 1)
        rowc = lax.broadcasted_iota(jnp.int32, (L, 1), 0)
        rr = lax.broadcasted_iota(jnp.int32, (L, L), 0)
        cc = lax.broadcasted_iota(jnp.int32, (L, L), 1)
        causal = rr >= cc
        h_scr[...] = jnp.zeros_like(h_scr)
        dsk = dsk_ref[...]
        spread_pair, spread_full, _, _ = _ssd_tables()
        a_neg = -jnp.exp(alog_ref[...])
        head = functools.partial(_ssd_head, dtr_ref, bias_ref, a_neg, tri_bf=_bf(cc <= rr),
                                 spread_pair=spread_pair, spread_full=spread_full, head_s=head_s)
        head(jnp.int32(0), slot=0)

        def chunk(c, slot):
            r0 = pl.multiple_of(c * L, L)
            head(jnp.minimum(c + 1, nc - 1), slot=1 - slot)
            _, _, acs, acs_t, dt_all, acs_all, acs_full = _ssd_head_load(head_s, slot)
            bb = _bf(b_ref[pl.ds(r0, L), :])
            cb = _bf(c_ref[pl.ds(r0, L), :])
            gm = _dot_nt(cb, bb)
            for pr in range(N_PAIR):
                j0, j1 = 2 * pr, 2 * pr + 1
                ms = []
                for j in (j0, j1):
                    seg = acs_full[:, j * LANES:(j + 1) * LANES] - acs_t[j:j + 1, :]
                    ms.append(_bf(gm * jnp.exp(jnp.where(causal, seg, -1e30))))
                x = x_ref[pl.ds(r0, L), pr * LANES:(pr + 1) * LANES]
                dt_p = dt_all[:, pr * LANES:(pr + 1) * LANES]
                acs_p = acs_all[:, pr * LANES:(pr + 1) * LANES]
                last_p = acs_p[L - 1:L, :]
                d_p = _pair_lanes(dsk[:, j0:j0 + 1], dsk[:, j1:j1 + 1], lane1)
                xd = x * dt_p
                xdb = _bf(xd)
                yds = _dot(jnp.concatenate(ms, axis=0), xdb)
                yd = _pair_lanes(yds[:L], yds[L:], lane)
                hp = h_scr[pr]
                yo = _dot_nt(cb, _bf(hp)) * jnp.exp(acs_p)
                y_ref[pl.ds(r0, L), pr * LANES:(pr + 1) * LANES] = yd + yo + x * d_p
                hs_ref[c, pr] = hp
                sp = _dot_tn(_bf(xd * jnp.exp(last_p - acs_p)), bb)
                cd = jnp.where(rowc < HEAD_DIM, jnp.exp(acs[L - 1:L, j0:j0 + 1]),
                               jnp.exp(acs[L - 1:L, j1:j1 + 1]))
                h_scr[pr] = hp * cd + sp

        def two_chunks(i, _):
            chunk(2 * i, 0)
            chunk(2 * i + 1, 1)
            return 0

        lax.fori_loop(0, nc // 2, two_chunks, 0)

    assert nc % 2 == 0
    grp = lambda b, g: (g, 0, 0)
    return pl.pallas_call(
        body, grid=(nb, SSD_GROUPS),
        in_specs=[pl.BlockSpec((s, 4 * LANES), lambda b, g: (b, g)),
                  pl.BlockSpec((s, LANES), lambda b, g: (b, 16 + g)),
                  pl.BlockSpec((s, LANES), lambda b, g: (b, 20 + g)),
                  pl.BlockSpec((None, s, LANES), lambda b, g: (g, b, 0)),
                  pl.BlockSpec((None, 1, LANES), grp), pl.BlockSpec((None, 1, LANES), grp),
                  pl.BlockSpec((None, 1, LANES), grp)],
        out_specs=[pl.BlockSpec((s, 4 * LANES), lambda b, g: (b, g)),
                   pl.BlockSpec((None, None, nc, N_PAIR, L, LANES), lambda b, g: (b, g, 0, 0, 0, 0))],
        out_shape=[jax.ShapeDtypeStruct((t, SSD_WIDTH), F32),
                   jax.ShapeDtypeStruct((nb, SSD_GROUPS, nc, N_PAIR, L, LANES), F32)],
        scratch_shapes=[pltpu.VMEM((N_PAIR, L, LANES), F32), pltpu.VMEM((2, L, HEAD_W), F32)],
        compiler_params=_params(("parallel", "parallel")), name="ssd_fwd")(
            xbc, xbc, xbc, dtr, bias, alog, dskip)


def _ssd_bwd(xbc, dtr, bias, alog, dskip, hs, dy, nb, s):
    nc = s // L
    t = nb * s

    def body(x_ref, b_ref, c_ref, dtr_ref, bias_ref, alog_ref, dsk_ref, hs_ref, dy_ref,
             dx_ref, db_ref, dc_ref, ddtr_ref, dbias_ref, dalog_ref, ddsk_ref, dh_scr, head_s):
        lane = lax.broadcasted_iota(jnp.int32, (L, LANES), 1)
        lane1 = lax.broadcasted_iota(jnp.int32, (1, LANES), 1)
        rowc = lax.broadcasted_iota(jnp.int32, (L, 1), 0)
        rr = lax.broadcasted_iota(jnp.int32, (L, L), 0)
        cc = lax.broadcasted_iota(jnp.int32, (L, L), 1)
        causal = rr >= cc
        tri_t = _bf(cc >= rr)
        dh_scr[...] = jnp.zeros_like(dh_scr)
        dsk = dsk_ref[...]
        a_neg = -jnp.exp(alog_ref[...])

        @pl.when(pl.program_id(1) == 0)
        def _():
            dbias_ref[...] = jnp.zeros_like(dbias_ref)
            dalog_ref[...] = jnp.zeros_like(dalog_ref)
            ddsk_ref[...] = jnp.zeros_like(ddsk_ref)

        spread_pair, spread_full, sum_pair, sum_full = _ssd_tables()
        sum_pair2 = jnp.concatenate([sum_pair] * 2, axis=0)
        sum_full2 = jnp.concatenate([sum_full] * 2, axis=0)
        head = functools.partial(_ssd_head, dtr_ref, bias_ref, a_neg, tri_bf=_bf(cc <= rr),
                                 spread_pair=spread_pair, spread_full=spread_full, head_s=head_s)
        head(jnp.int32(nc - 1), slot=0)

        def chunk(c, slot):
            r0 = pl.multiple_of(c * L, L)
            head(jnp.maximum(c - 1, 0), slot=1 - slot)
            raw, dt, acs, acs_t, dt_all, acs_all, acs_full = _ssd_head_load(head_s, slot)
            bb = _bf(b_ref[pl.ds(r0, L), :])
            cb = _bf(c_ref[pl.ds(r0, L), :])
            gm = _dot_nt(cb, bb)
            dg = jnp.zeros((L, L), F32)
            dbacc = jnp.zeros((L, LANES), F32)
            dcacc = jnp.zeros((L, LANES), F32)
            dacs_t = jnp.zeros((L, L), F32)
            ta_all, te_all, qm_all, tc_all, td_all, dcd_all = [], [], [], [], [], []
            for pr in range(N_PAIR):
                js = (2 * pr, 2 * pr + 1)
                lms, mfs = [], []
                for j in js:
                    seg = acs_full[:, j * LANES:(j + 1) * LANES] - acs_t[j:j + 1, :]
                    lm = jnp.exp(jnp.where(causal, seg, -1e30))
                    lms.append(lm)
                    mfs.append(gm * lm)
                x = x_ref[pl.ds(r0, L), pr * LANES:(pr + 1) * LANES]
                dyv = dy_ref[pl.ds(r0, L), pr * LANES:(pr + 1) * LANES]
                dt_p = dt_all[:, pr * LANES:(pr + 1) * LANES]
                acs_p = acs_all[:, pr * LANES:(pr + 1) * LANES]
                last_p = acs_p[L - 1:L, :]
                d_p = _pair_lanes(dsk[:, js[0]:js[0] + 1], dsk[:, js[1]:js[1] + 1], lane1)
                e_p = jnp.exp(acs_p)
                dte_p = jnp.exp(last_p - acs_p)
                xd = x * dt_p
                xdb = _bf(xd)
                hp = hs_ref[c, pr]
                hb = _bf(hp)
                dye = _bf(dyv * e_p)
                yo = _dot_nt(cb, hb) * e_p
                dcacc = dcacc + _dot(dye, hb)
                dh_out = _dot_tn(dye, cb)
                dyb = _bf(dyv)
                dms = _dot_nt(_bf(jnp.concatenate(
                    [jnp.where(lane < HEAD_DIM, dyv, 0.0), jnp.where(lane < HEAD_DIM, 0.0, dyv)], axis=0)), xdb)
                mty = _dot_tn(_bf(jnp.concatenate(mfs, axis=1)), dyb)
                dxd = jnp.where(lane < HEAD_DIM, mty[:L], mty[L:])
                for hh in range(2):
                    dm = dms[hh * L:(hh + 1) * L]
                    dg = dg + dm * lms[hh]
                    qm = dm * mfs[hh]
                    qm_all.append(qm)
                    dacs_t = dacs_t - jnp.where(rr == js[hh], jnp.sum(qm, axis=0, keepdims=True), 0.0)
                ds = dh_scr[pr]
                dsb = _bf(ds)
                xdd = xd * dte_p
                dxdd = _dot_nt(bb, dsb)
                dbacc = dbacc + _dot(_bf(xdd), dsb)
                dxd = dxd + dxdd * dte_p
                cd = jnp.where(rowc < HEAD_DIM, jnp.exp(acs[L - 1:L, js[0]:js[0] + 1]),
                               jnp.exp(acs[L - 1:L, js[1]:js[1] + 1]))
                prod = ds * hp
                dcd_all += [jnp.sum(prod[:HEAD_DIM], axis=0, keepdims=True),
                            jnp.sum(prod[HEAD_DIM:], axis=0, keepdims=True)]
                dh_scr[pr] = dh_out + ds * cd
                tcv = dxdd * xdd
                ta_all.append(dyv * yo - tcv)
                tc_all.append(jnp.sum(tcv, axis=0, keepdims=True))
                te_all.append(dxd * x)
                td_all.append(jnp.sum(dyv * x, axis=0, keepdims=True))
                dx_ref[pl.ds(r0, L), pr * LANES:(pr + 1) * LANES] = dxd * dt_p + dyv * d_p
            dgb = _bf(dg)
            dc_ref[pl.ds(r0, L), :] = dcacc + _dot(dgb, bb)
            db_ref[pl.ds(r0, L), :] = dbacc + _dot_tn(dgb, cb)
            last_add = jnp.zeros((1, LANES), F32)
            dd_add = jnp.zeros((1, LANES), F32)
            cd_row = jnp.exp(acs[L - 1:L, :])
            for j in range(SSD_HPG):
                hm = (lane1 >= HEAD_DIM) if j % 2 else (lane1 < HEAD_DIM)
                tcj = jnp.sum(jnp.where(hm, tc_all[j // 2], 0.0), axis=1, keepdims=True)
                tdj = jnp.sum(jnp.where(hm, td_all[j // 2], 0.0), axis=1, keepdims=True)
                dcdj = jnp.sum(dcd_all[j], axis=1, keepdims=True)
                last_add = last_add + jnp.where(lane1 == j, tcj + dcdj * cd_row, 0.0)
                dd_add = dd_add + jnp.where(lane1 == j, tdj, 0.0)
            dacs = (_dot(_split2(jnp.concatenate(ta_all, axis=1)), sum_pair2)
                    + _dot(_split2(jnp.concatenate(qm_all, axis=1)), sum_full2)
                    + dacs_t.T + jnp.where(rowc == L - 1, last_add, 0.0))
            ddta = _tri_dot(tri_t, dacs)
            ddt = _dot(_bf(jnp.concatenate(te_all, axis=1)), sum_pair) + ddta * a_neg
            ddraw = ddt * _sigmoid(raw)
            ddtr_ref[pl.ds(r0, L), :] = ddraw
            dbias_ref[...] += jnp.sum(ddraw, axis=0, keepdims=True)
            dalog_ref[...] += jnp.sum(ddta * dt, axis=0, keepdims=True) * a_neg
            ddsk_ref[...] += dd_add

        def two_chunks(i, _):
            chunk(nc - 1 - 2 * i, 0)
            chunk(nc - 2 - 2 * i, 1)
            return 0

        lax.fori_loop(0, nc // 2, two_chunks, 0)

    assert nc % 2 == 0
    grp = lambda g, b: (g, 0, 0)
    small = jax.ShapeDtypeStruct((SSD_GROUPS, 1, LANES), F32)
    return pl.pallas_call(
        body, grid=(SSD_GROUPS, nb),
        in_specs=[pl.BlockSpec((s, 4 * LANES), lambda g, b: (b, g)),
                  pl.BlockSpec((s, LANES), lambda g, b: (b, 16 + g)),
                  pl.BlockSpec((s, LANES), lambda g, b: (b, 20 + g)),
                  pl.BlockSpec((None, s, LANES), lambda g, b: (g, b, 0)),
                  pl.BlockSpec((None, 1, LANES), grp), pl.BlockSpec((None, 1, LANES), grp),
                  pl.BlockSpec((None, 1, LANES), grp),
                  pl.BlockSpec((None, None, nc, N_PAIR, L, LANES), lambda g, b: (b, g, 0, 0, 0, 0)),
                  pl.BlockSpec((s, 4 * LANES), lambda g, b: (b, g))],
        out_specs=[pl.BlockSpec((s, 4 * LANES), lambda g, b: (b, g)),
                   pl.BlockSpec((s, LANES), lambda g, b: (b, g)),
                   pl.BlockSpec((s, LANES), lambda g, b: (b, g)),
                   pl.BlockSpec((None, s, LANES), lambda g, b: (g, b, 0)),
                   pl.BlockSpec((None, 1, LANES), grp), pl.BlockSpec((None, 1, LANES), grp),
                   pl.BlockSpec((None, 1, LANES), grp)],
        out_shape=[jax.ShapeDtypeStruct((t, SSD_WIDTH), F32),
                   jax.ShapeDtypeStruct((t, SSD_GROUPS * SSD_STATE), F32),
                   jax.ShapeDtypeStruct((t, SSD_GROUPS * SSD_STATE), F32),
                   jax.ShapeDtypeStruct((SSD_GROUPS, t, LANES), F32), small, small, small],
        scratch_shapes=[pltpu.VMEM((N_PAIR, L, LANES), F32), pltpu.VMEM((2, L, HEAD_W), F32)],
        compiler_params=_params(("parallel", "arbitrary")), name="ssd_bwd")(
            xbc, xbc, xbc, dtr, bias, alog, dskip, hs, dy)


GN = SSD_WIDTH // SSD_GROUPS


def _post_fwd(o, proj, y, nw, *, tm=256):
    t = o.shape[0]

    def body(o_ref, za_ref, y_ref, zs_ref, nw_ref, ya_ref, ys_ref):
        za = za_ref[...].astype(F32)
        ya_ref[...] = _bf(o_ref[...] * (za * _sigmoid(za)))
        zs = zs_ref[...].astype(F32)
        yz = y_ref[...] * (zs * _sigmoid(zs))
        for g in range(SSD_GROUPS):
            sl = slice(g * GN, (g + 1) * GN)
            v = yz[:, sl]
            r = lax.rsqrt(jnp.mean(v * v, axis=-1, keepdims=True) + EPS)
            ys_ref[:, sl] = _bf(v * r * nw_ref[:, sl])

    return pl.pallas_call(
        body, grid=(t // tm,),
        in_specs=[pl.BlockSpec((tm, SB_WIDTH), lambda i: (i, 0)),
                  pl.BlockSpec((tm, SB_WIDTH), lambda i: (i, CB_ZA * LANES // SB_WIDTH)),
                  pl.BlockSpec((tm, SSD_WIDTH), lambda i: (i, 0)),
                  pl.BlockSpec((tm, SSD_WIDTH), lambda i: (i, CB_ZS * LANES // SSD_WIDTH)),
                  pl.BlockSpec((1, SSD_WIDTH), lambda i: (0, 0))],
        out_specs=[pl.BlockSpec((tm, SB_WIDTH), lambda i: (i, 0)),
                   pl.BlockSpec((tm, SSD_WIDTH), lambda i: (i, 0))],
        out_shape=[jax.ShapeDtypeStruct((t, SB_WIDTH), BF16), jax.ShapeDtypeStruct((t, SSD_WIDTH), BF16)],
        compiler_params=_params(("parallel",)), name="post_fwd")(o, proj, y, proj, nw)


def _post_bwd(dya, o, proj, dys, y, nw, *, tm=256):
    t = o.shape[0]

    def body(dya_ref, o_ref, za_ref, dys_ref, y_ref, zs_ref, nw_ref,
             do_ref, dza_ref, dy_ref, dzs_ref, dnw_ref):
        @pl.when(pl.program_id(0) == 0)
        def _():
            dnw_ref[...] = jnp.zeros_like(dnw_ref)

        za = za_ref[...].astype(F32)
        sa = _sigmoid(za)
        d = dya_ref[...].astype(F32)
        do_ref[...] = _bf(d * (za * sa))
        dza_ref[...] = _bf(d * o_ref[...] * (sa * (1.0 + za * (1.0 - sa))))
        zs = zs_ref[...].astype(F32)
        ss = _sigmoid(zs)
        silu = zs * ss
        yv = y_ref[...]
        yz = yv * silu
        dv = dys_ref[...].astype(F32)
        for g in range(SSD_GROUPS):
            sl = slice(g * GN, (g + 1) * GN)
            v = yz[:, sl]
            r = lax.rsqrt(jnp.mean(v * v, axis=-1, keepdims=True) + EPS)
            nrm = v * r
            dg = dv[:, sl]
            dnw_ref[:, sl] += jnp.sum(dg * nrm, axis=0, keepdims=True)
            dn = dg * nw_ref[:, sl]
            dyz = r * (dn - nrm * jnp.mean(dn * nrm, axis=-1, keepdims=True))
            dy_ref[:, sl] = dyz * silu[:, sl]
            dzs_ref[:, sl] = _bf(dyz * yv[:, sl] * (ss[:, sl] * (1.0 + zs[:, sl] * (1.0 - ss[:, sl]))))

    a_spec = pl.BlockSpec((tm, SB_WIDTH), lambda i: (i, 0))
    s_spec = pl.BlockSpec((tm, SSD_WIDTH), lambda i: (i, 0))
    w_spec = pl.BlockSpec((1, SSD_WIDTH), lambda i: (0, 0))
    return pl.pallas_call(
        body, grid=(t // tm,),
        in_specs=[a_spec, a_spec, pl.BlockSpec((tm, SB_WIDTH), lambda i: (i, CB_ZA * LANES // SB_WIDTH)),
                  s_spec, s_spec, pl.BlockSpec((tm, SSD_WIDTH), lambda i: (i, CB_ZS * LANES // SSD_WIDTH)),
                  w_spec],
        out_specs=[a_spec, a_spec, s_spec, s_spec, w_spec],
        out_shape=[jax.ShapeDtypeStruct((t, SB_WIDTH), BF16), jax.ShapeDtypeStruct((t, SB_WIDTH), BF16),
                   jax.ShapeDtypeStruct((t, SSD_WIDTH), F32), jax.ShapeDtypeStruct((t, SSD_WIDTH), BF16),
                   jax.ShapeDtypeStruct((1, SSD_WIDTH), F32)],
        compiler_params=_params(("arbitrary",)), name="post_bwd")(dya, o, proj, dys, y, proj, nw)


def _merge_fwd(ya, ys, proj_g, *, tm=512):
    t = ya.shape[0]

    def body(ya_ref, ys_ref, ga_ref, gs_ref, m_ref):
        m_ref[...] = _bf(_sigmoid(ga_ref[...].astype(F32)) * ya_ref[...].astype(F32)
                         + _sigmoid(gs_ref[...].astype(F32)) * ys_ref[...].astype(F32))

    spec = pl.BlockSpec((tm, D_MODEL), lambda i: (i, 0))
    return pl.pallas_call(
        body, grid=(t // tm,),
        in_specs=[spec, spec, pl.BlockSpec((tm, D_MODEL), lambda i: (i, 0)),
                  pl.BlockSpec((tm, D_MODEL), lambda i: (i, 1))],
        out_specs=spec, out_shape=jax.ShapeDtypeStruct((t, D_MODEL), BF16),
        compiler_params=_params(("parallel",)), name="merge_fwd")(ya, ys, proj_g, proj_g)


def _merge_bwd(dm, ya, ys, proj_g, *, tm=512):
    t = ya.shape[0]

    def body(dm_ref, ya_ref, ys_ref, ga_ref, gs_ref, dya_ref, dys_ref, dg_ref):
        d = dm_ref[...].astype(F32)
        ga = _sigmoid(ga_ref[...].astype(F32))
        gs = _sigmoid(gs_ref[...].astype(F32))
        dya_ref[...] = _bf(d * ga)
        dys_ref[...] = _bf(d * gs)
        dg_ref[:, :D_MODEL] = _bf(d * ya_ref[...].astype(F32) * (ga * (1.0 - ga)))
        dg_ref[:, D_MODEL:] = _bf(d * ys_ref[...].astype(F32) * (gs * (1.0 - gs)))

    spec = pl.BlockSpec((tm, D_MODEL), lambda i: (i, 0))
    return pl.pallas_call(
        body, grid=(t // tm,),
        in_specs=[spec, spec, spec, pl.BlockSpec((tm, D_MODEL), lambda i: (i, 0)),
                  pl.BlockSpec((tm, D_MODEL), lambda i: (i, 1))],
        out_specs=[spec, spec, pl.BlockSpec((tm, 2 * D_MODEL), lambda i: (i, 0))],
        out_shape=[jax.ShapeDtypeStruct((t, D_MODEL), BF16), jax.ShapeDtypeStruct((t, D_MODEL), BF16),
                   jax.ShapeDtypeStruct((t, 2 * D_MODEL), BF16)],
        compiler_params=_params(("parallel",)), name="merge_bwd")(dm, ya, ys, proj_g, proj_g)


def _final(x, out, wf, target, *, tm=512):
    t = x.shape[0]

    def body(x_ref, o_ref, w_ref, t_ref, loss_ref, dx_ref, dw_ref):
        @pl.when(pl.program_id(0) == 0)
        def _():
            loss_ref[...] = jnp.zeros_like(loss_ref)
            dw_ref[...] = jnp.zeros_like(dw_ref)

        x2 = x_ref[...] + o_ref[...]
        r = lax.rsqrt(jnp.mean(x2 * x2, axis=-1, keepdims=True) + EPS)
        nrm = x2 * r
        e = nrm * w_ref[...] - t_ref[...]
        row_loss = jnp.mean(e * e, axis=-1, keepdims=True)
        loss_ref[...] += 0.5 * jnp.sum(row_loss, axis=0, keepdims=True)
        dyv = e * (1.0 / D_MODEL)
        dw_ref[...] += jnp.sum(dyv * nrm, axis=0, keepdims=True)
        dn = dyv * w_ref[...]
        dx_ref[...] = r * (dn - nrm * jnp.mean(dn * nrm, axis=-1, keepdims=True))

    spec = pl.BlockSpec((tm, D_MODEL), lambda i: (i, 0))
    fix = pl.BlockSpec((1, D_MODEL), lambda i: (0, 0))
    return pl.pallas_call(
        body, grid=(t // tm,),
        in_specs=[spec, spec, fix, spec],
        out_specs=[pl.BlockSpec((1, LANES), lambda i: (0, 0)), spec, fix],
        out_shape=[jax.ShapeDtypeStruct((1, LANES), F32), jax.ShapeDtypeStruct((t, D_MODEL), F32),
                   jax.ShapeDtypeStruct((1, D_MODEL), F32)],
        compiler_params=_params(("arbitrary",)), name="final_loss")(x, out, wf, target)


def _group_lanes(v):
    return jnp.pad(v.reshape(SSD_GROUPS, 1, SSD_HPG), ((0, 0), (0, 0), (0, LANES - SSD_HPG)))


def _ungroup_lanes(v):
    return v[:, :, :SSD_HPG].reshape(1, N_DT)


def _row_slabs(pieces, rows):
    total = sum(p.shape[0] for p in pieces)
    assert total % rows == 0
    slabs = []
    for lo in range(0, total, rows):
        parts, off = [], 0
        for p in pieces:
            a, b = max(lo, off), min(lo + rows, off + p.shape[0])
            if a < b:
                parts.append(p[a - off:b - off])
            off += p.shape[0]
        slabs.append(parts[0] if len(parts) == 1 else jnp.concatenate(parts, axis=0))
    return jnp.stack(slabs)


def _local_step(x, target, nb, s, w_in_t, conv_w, conv_b, norm_w, dt_bias, a_log, d_skip,
                ssm_norm_w, out_weights, final_norm_w, between=None):
    t = nb * s
    w_g_t = w_in_t[GATE_COL0:]
    w_dt_t = jnp.pad(w_in_t[DT_COL0:GATE_COL0], ((0, LANES - N_DT), (0, 0)))
    h, r1 = _rms_fwd(x, norm_w)
    proj = _matmul(h, w_in_t, nt=True, n=DT_COL0, out_dtype=BF16, name="proj_main")
    proj_g = _matmul(h, w_g_t, nt=True, out_dtype=BF16, name="proj_gate")
    dt_raw = _matmul(h, w_dt_t, nt=True, name="proj_dt")
    dtr = jnp.pad(dt_raw[:, :N_DT].reshape(t, SSD_GROUPS, SSD_HPG).transpose(1, 0, 2),
                  ((0, 0), (0, 0), (0, LANES - SSD_HPG)))
    bias_g, alog_g, dsk_g = _group_lanes(dt_bias), _group_lanes(a_log), _group_lanes(d_skip)

    o, att_a, att_sg = _attn_fwd(proj, nb, s)
    xbc = _conv_fwd(proj, conv_w, conv_b, nb, s)
    y, hs = _ssd_fwd(xbc, dtr, bias_g, alog_g, dsk_g, nb, s)
    ya_in, ys_in = _post_fwd(o, proj, y, ssm_norm_w)
    w_attn_out, w_ssm_out, w_o = out_weights(ys_in) if callable(out_weights) else out_weights
    ya = _matmul(ya_in, w_attn_out, out_dtype=BF16, name="attn_out")
    ys = _matmul(ys_in, w_ssm_out, out_dtype=BF16, name="ssm_out")
    merged = _merge_fwd(ya, ys, proj_g)
    out = _matmul(merged, w_o, name="out_proj")
    loss, dx2, d_final_w = _final(x, out, final_norm_w, target)

    dx2b = _bf(dx2)
    dmerged = _matmul(dx2b, w_o, nt=True, out_dtype=BF16, name="d_merged")
    d_w_o = _matmul(merged, dx2b, ta=True, name="d_w_o")
    dya, dys, dgate = _merge_bwd(dmerged, ya, ys, proj_g)
    d_w_attn_out = _matmul(ya_in, dya, ta=True, name="d_w_attn_out")
    d_w_ssm_out = _matmul(ys_in, dys, ta=True, name="d_w_ssm_out")
    dya_in = _matmul(dya, w_attn_out, nt=True, out_dtype=BF16, name="d_ya_in")
    dys_in = _matmul(dys, w_ssm_out, nt=True, name="d_ys_in")
    do, dza, dy, dzs, d_ssm_norm_w = _post_bwd(dya_in, o, proj, dys_in, y, ssm_norm_w)
    dq, dk, dv = _attn_bwd(proj, do, att_a, att_sg, nb, s)
    dxs_act, db_act, dc_act, ddtr, dbias_g, dalog_g, ddsk_g = _ssd_bwd(
        xbc, dtr, bias_g, alog_g, dsk_g, hs, dy, nb, s)
    dxs, dwx, dbx = _conv_bwd(proj, dxs_act, conv_w, conv_b, nb, s, col0=0, name="conv_bwd_x")
    dbm, dwb, dbb = _conv_bwd(proj, db_act, conv_w, conv_b, nb, s, col0=SSD_WIDTH, name="conv_bwd_b")
    dcm, dwc, dbc = _conv_bwd(proj, dc_act, conv_w, conv_b, nb, s, col0=SSD_WIDTH + 512, name="conv_bwd_c")
    dproj = jnp.concatenate([dq, dk, dv, dza, dzs, dxs, dbm, dcm], axis=1)
    ddt = jnp.pad(_bf(ddtr[:, :, :SSD_HPG].transpose(1, 0, 2).reshape(t, N_DT)), ((0, 0), (0, LANES - N_DT)))
    wt = dict(ta=True, out_dtype=BF16)
    d_w_in_t = _row_slabs(
        [_matmul(dproj, h, name="d_w_main", **wt), _matmul(ddt, h, name="d_w_dt", **wt)[:N_DT],
         _matmul(dgate, h, name="d_w_gate", **wt)], SHARD)
    grads = dict(
        w_in_t=d_w_in_t,
        conv_w=jnp.concatenate([dwx, dwb, dwc], axis=1), conv_b=jnp.concatenate([dbx, dbb, dbc], axis=1),
        dt_bias=_ungroup_lanes(dbias_g), a_log=_ungroup_lanes(dalog_g), d_skip=_ungroup_lanes(ddsk_g),
        ssm_norm_w=d_ssm_norm_w, w_attn_out=d_w_attn_out, w_ssm_out=d_w_ssm_out, w_o=d_w_o,
        final_norm_w=d_final_w)
    extra = None
    if between is not None:
        zero, extra = between(grads)
        ddt = ddt + _bf(zero[0:1, 0:1])

    dh = _matmul(ddt, w_dt_t, name="d_h_dt")
    dh = _matmul(dgate, w_g_t, add=dh, name="d_h_gate")
    dh = _matmul(dproj, w_in_t, add=dh, name="d_h", tk=DT_COL0 // 4)
    grad_x, grads["norm_w"] = _rms_bwd(dh, x, r1, norm_w, dx2)
    return loss, grad_x, grads, extra


N_CHIP = N_DEV // 2


def _pair_exchange(scatter, bcast, *, name):
    arrays = list(scatter) + list(bcast)
    ns, n = len(scatter), len(arrays)
    n_sem = ns * N_CHIP + (n - ns) * (N_DEV - 1)

    def body(*refs):
        ins, outs = refs[:n], refs[n:2 * n]
        send_sems, recv_sems, local_sems = refs[2 * n:]
        x, y, c = lax.axis_index("x"), lax.axis_index("y"), lax.axis_index("c")
        me = 4 * x + 2 * y + c
        copies, k = [], 0
        for a in range(ns):
            for q in range(N_CHIP):
                cp = pltpu.make_async_remote_copy(
                    src_ref=ins[a].at[2 * q + 1 - c], dst_ref=outs[a].at[q],
                    send_sem=send_sems.at[k], recv_sem=recv_sems.at[k],
                    device_id=(x, y, 1 - c), device_id_type=pl.DeviceIdType.MESH)
                cp.start()
                copies.append(cp)
                k += 1
        for a in range(ns, n):
            cp = pltpu.make_async_copy(ins[a], outs[a].at[me], local_sems.at[a - ns])
            cp.start()
            copies.append(cp)
            for r in range(1, N_DEV):
                tx = 1 - x if (r >> 2) & 1 else x
                ty = 1 - y if (r >> 1) & 1 else y
                tc = 1 - c if r & 1 else c
                cp = pltpu.make_async_remote_copy(
                    src_ref=ins[a], dst_ref=outs[a].at[me], send_sem=send_sems.at[k], recv_sem=recv_sems.at[k],
                    device_id=(tx, ty, tc), device_id_type=pl.DeviceIdType.MESH)
                cp.start()
                copies.append(cp)
                k += 1
        for cp in copies:
            cp.wait()

    out_shape = [jax.ShapeDtypeStruct((N_CHIP,) + v.shape[1:] if i < ns else (N_DEV,) + v.shape, v.dtype)
                 for i, v in enumerate(arrays)]
    return pl.pallas_call(
        body, in_specs=[pl.BlockSpec(memory_space=pl.ANY)] * n,
        out_specs=[pl.BlockSpec(memory_space=pl.ANY)] * n, out_shape=out_shape,
        scratch_shapes=[pltpu.SemaphoreType.DMA((n_sem,)), pltpu.SemaphoreType.DMA((n_sem,)),
                        pltpu.SemaphoreType.DMA((max(n - ns, 1),))],
        compiler_params=pltpu.CompilerParams(has_side_effects=True), name=name)(*arrays)


def _pair_add(mine, recv, core, *, name, tc=None):
    _, r, c = recv.shape
    tc = c if tc is None else tc
    assert c % tc == 0

    def body(core_ref, m_ref, r_ref, o_ref):
        o_ref[...] = (m_ref[...].astype(F32) + r_ref[...].astype(F32)).astype(o_ref.dtype)

    spec = pl.BlockSpec((None, r, tc), lambda q, i, core_ref: (q, 0, i))
    return pl.pallas_call(
        body, grid_spec=pltpu.PrefetchScalarGridSpec(
            num_scalar_prefetch=1, grid=(N_CHIP, c // tc),
            in_specs=[pl.BlockSpec((None, r, tc), lambda q, i, core_ref: (2 * q + core_ref[0], 0, i)), spec],
            out_specs=spec),
        out_shape=jax.ShapeDtypeStruct(recv.shape, recv.dtype),
        compiler_params=_params(("parallel", "parallel")), name=name)(core, mine, recv)


_HBM = pl.BlockSpec(memory_space=pltpu.HBM)
_SEM = pl.BlockSpec(memory_space=pltpu.SEMAPHORE)
_DATAFLOW = pltpu.SideEffectType.DATAFLOW_SIDE_EFFECTING
_PEERS = {"chips": N_CHIP - 1, "all": N_DEV - 1}


def _split_copies(pattern, srcs, lands, send_sems, recv_sems):
    x, y, c = lax.axis_index("x"), lax.axis_index("y"), lax.axis_index("c")
    per = _PEERS[pattern]
    copies = []
    for a in range(len(srcs)):
        for r in range(1, per + 1):
            if pattern == "chips":
                tx = 1 - x if (r >> 1) & 1 else x
                ty = 1 - y if r & 1 else y
                tc = c
                src, dst = srcs[a].at[2 * tx + ty], lands[a].at[2 * x + y]
            else:
                tx = 1 - x if (r >> 2) & 1 else x
                ty = 1 - y if (r >> 1) & 1 else y
                tc = 1 - c if r & 1 else c
                src, dst = srcs[a], lands[a].at[4 * x + 2 * y + c]
            copies.append(pltpu.make_async_remote_copy(
                src_ref=src, dst_ref=dst, send_sem=send_sems.at[a * per + r - 1],
                recv_sem=recv_sems.at[a * per + r - 1], device_id=(tx, ty, tc), device_id_type=pl.DeviceIdType.MESH))
    return copies


def _split_start(pattern, arrays, *, name):
    n = len(arrays)
    n_sem = n * _PEERS[pattern]

    def body(*refs):
        srcs, lands = refs[:n], refs[n:2 * n]
        send_sems, recv_sems = refs[2 * n], refs[2 * n + 1]
        zero = refs[-1]
        for cp in _split_copies(pattern, srcs, lands, send_sems, recv_sems):
            cp.start()
        zero[...] = jnp.zeros_like(zero)

    src_t = [pltpu.HBM(v.shape, v.dtype) for v in arrays]
    land_t = [pltpu.HBM(v.shape if pattern == "chips" else (N_DEV,) + v.shape, v.dtype) for v in arrays]
    outs = pl.pallas_call(
        body, name=name,
        out_shape=(pltpu.SemaphoreType.DMA((n_sem,)), pltpu.SemaphoreType.DMA((n_sem,)), *src_t, *land_t,
                   jax.ShapeDtypeStruct((8, LANES), F32)),
        in_specs=[_HBM] * (2 * n),
        out_specs=(_SEM, _SEM, *[_HBM] * (2 * n), pl.BlockSpec(memory_space=pltpu.VMEM)),
        input_output_aliases={i: 2 + i for i in range(2 * n)},
        compiler_params=pltpu.CompilerParams(has_side_effects=_DATAFLOW))(
            *[pltpu.with_memory_space_constraint(v, pltpu.HBM) for v in arrays],
            *[pltpu.with_memory_space_constraint(lax.empty(t.shape, t.dtype), pltpu.HBM) for t in land_t])
    return outs[0], outs[1], outs[2:2 + n], outs[2 + n:2 + 2 * n], outs[-1]


def _split_wait(pattern, send_sems, recv_sems, srcs, lands, after, *, name):
    n = len(srcs)

    def body(*refs):
        for cp in _split_copies(pattern, refs[:n], refs[n:2 * n], refs[2 * n], refs[2 * n + 1]):
            cp.wait_send()
            cp.wait_recv()

    outs = pl.pallas_call(
        body, name=name, out_shape=[pltpu.HBM(v.shape, v.dtype) for v in (*srcs, *lands)],
        in_specs=[_HBM] * (2 * n) + [_SEM, _SEM, pl.BlockSpec(memory_space=pl.ANY)],
        out_specs=[_HBM] * (2 * n), input_output_aliases={i: i for i in range(2 * n)},
        compiler_params=pltpu.CompilerParams(has_side_effects=_DATAFLOW))(
            *srcs, *lands, send_sems, recv_sems, after)
    return outs[:n], outs[n:]


def _own_slab(lands, srcs, index, whole):
    out = []
    for land, src in zip(lands, srcs):
        own = src[None] if whole else lax.dynamic_slice(src, (index, 0, 0), (1,) + src.shape[1:])
        out.append(lax.dynamic_update_slice(land, own, (index, 0, 0)))
    return out


def _gather(arrays, *, name):
    n = len(arrays)
    per = N_DEV - 1

    def body(*refs):
        ins, outs = refs[:n], refs[n:2 * n]
        send_sems, recv_sems, local_sems = refs[2 * n:]
        x, y, c = lax.axis_index("x"), lax.axis_index("y"), lax.axis_index("c")
        me, sibling = (x, y, c), (x, y, 1 - c)
        chips = [(1 - x, y), (x, 1 - y), (1 - x, 1 - y)]

        def slab(a, block):
            return outs[a].at[4 * block[0] + 2 * block[1] + block[2]]

        def copy(a, k, block, to, src=None):
            rows = slab(a, block)
            return pltpu.make_async_remote_copy(
                src_ref=rows if src is None else src, dst_ref=rows,
                send_sem=send_sems.at[a * per + k], recv_sem=recv_sems.at[a * per + k],
                device_id=to, device_id_type=pl.DeviceIdType.MESH)

        started = []
        for a in range(n):
            mine = pltpu.make_async_copy(ins[a], slab(a, me), local_sems.at[a])
            mine.start()
            started.append(mine)
        for a in range(n):
            first = [copy(a, 1 + j, me, (*chip, c), src=ins[a]) for j, chip in enumerate(chips)]
            first.append(copy(a, 0, me, sibling, src=ins[a]))
            for cp in first:
                cp.start()
            started += first
        for j, chip in enumerate(chips):
            for a in range(n):
                copy(a, 1 + j, (*chip, c), me).wait_recv()
                passed = copy(a, 4 + j, (*chip, c), sibling)
                passed.start()
                started.append(passed)
        for a in range(n):
            copy(a, 0, sibling, me).wait_recv()
            for j, chip in enumerate(chips):
                copy(a, 4 + j, (*chip, 1 - c), me).wait_recv()
        for cp in started[:n]:
            cp.wait()
        for cp in started[n:]:
            cp.wait_send()

    return pl.pallas_call(
        body, in_specs=[pl.BlockSpec(memory_space=pl.ANY)] * n,
        out_specs=[pl.BlockSpec(memory_space=pl.ANY)] * n,
        out_shape=[jax.ShapeDtypeStruct((N_DEV,) + v.shape, v.dtype) for v in arrays],
        scratch_shapes=[pltpu.SemaphoreType.DMA((n * per,)), pltpu.SemaphoreType.DMA((n * per,)),
                        pltpu.SemaphoreType.DMA((n,))],
        compiler_params=pltpu.CompilerParams(has_side_effects=True), name=name)(*arrays)


def _adamw(parts, w, m, v, *, name, tc=None):
    r, c = w.shape
    n_parts = parts.shape[0]
    tc = c if tc is None else tc
    assert c % tc == 0

    def body(p_ref, w_ref, m_ref, v_ref, g_ref, d_ref, nm_ref, nv_ref):
        g = p_ref[0].astype(F32)
        for d in range(1, n_parts):
            g = g + p_ref[d].astype(F32)
        mm = ADAM_B1 * m_ref[...] + (1.0 - ADAM_B1) * g
        vv = ADAM_B2 * v_ref[...] + (1.0 - ADAM_B2) * (g * g)
        m_hat = mm / (1.0 - ADAM_B1 ** ADAM_STEP)
        v_hat = vv / (1.0 - ADAM_B2 ** ADAM_STEP)
        g_ref[...] = g
        d_ref[...] = -ADAM_LR * (m_hat / (jnp.sqrt(v_hat) + ADAM_EPS) + ADAM_WD * w_ref[...])
        nm_ref[...] = mm
        nv_ref[...] = vv

    spec = pl.BlockSpec((r, tc), lambda i: (0, i))
    out = jax.ShapeDtypeStruct((r, c), F32)
    return pl.pallas_call(
        body, grid=(c // tc,),
        in_specs=[pl.BlockSpec((n_parts, r, tc), lambda i: (0, 0, i)), spec, spec, spec],
        out_specs=[spec] * 4, out_shape=[out] * 4,
        compiler_params=_params(("parallel",)), name=name)(parts, w, m, v)


SMALL = (("norm_w", D_MODEL), ("conv_b", CONV_DIM), ("dt_bias", N_DT), ("a_log", N_DT),
         ("d_skip", N_DT), ("ssm_norm_w", SSD_WIDTH), ("final_norm_w", D_MODEL))
SMALL_ROWS = 64
SHARD = D_PROJ // N_DEV


def _pack_small(vals):
    flat = jnp.concatenate([vals[k].reshape(-1).astype(F32) for k, _ in SMALL])
    return jnp.pad(flat, (0, SMALL_ROWS * LANES - flat.shape[0])).reshape(SMALL_ROWS, LANES)


def _unpack_small(packed, shapes):
    flat = packed.reshape(-1)
    out, off = {}, 0
    for k, size in SMALL:
        out[k] = flat[off:off + size].reshape(shapes[k])
        off += size
    return out


def kernel(x, norm_w, w_in, conv_w, conv_b, dt_bias, a_log, d_skip, ssm_norm_w, w_attn_out, w_ssm_out, w_o, final_norm_w, loss_target, m_norm_w, m_w_in, m_conv_w, m_conv_b, m_dt_bias, m_a_log, m_d_skip, m_ssm_norm_w, m_w_attn_out, m_w_ssm_out, m_w_o, m_final_norm_w, v_norm_w, v_w_in, v_conv_w, v_conv_b, v_dt_bias, v_a_log, v_d_skip, v_ssm_norm_w, v_w_attn_out, v_w_ssm_out, v_w_o, v_final_norm_w):
    nb, s, _ = x.shape
    t = nb * s
    weights = dict(norm_w=norm_w, w_in=w_in, conv_w=conv_w, conv_b=conv_b, dt_bias=dt_bias, a_log=a_log,
                   d_skip=d_skip, ssm_norm_w=ssm_norm_w, w_attn_out=w_attn_out, w_ssm_out=w_ssm_out,
                   w_o=w_o, final_norm_w=final_norm_w)
    moms = dict(norm_w=m_norm_w, w_in=m_w_in, conv_w=m_conv_w, conv_b=m_conv_b, dt_bias=m_dt_bias,
                a_log=m_a_log, d_skip=m_d_skip, ssm_norm_w=m_ssm_norm_w, w_attn_out=m_w_attn_out,
                w_ssm_out=m_w_ssm_out, w_o=m_w_o, final_norm_w=m_final_norm_w)
    vels = dict(norm_w=v_norm_w, w_in=v_w_in, conv_w=v_conv_w, conv_b=v_conv_b, dt_bias=v_dt_bias,
                a_log=v_a_log, d_skip=v_d_skip, ssm_norm_w=v_ssm_norm_w, w_attn_out=v_w_attn_out,
                w_ssm_out=v_w_ssm_out, w_o=v_w_o, final_norm_w=v_final_norm_w)

    tr_ = lambda a: jnp.transpose(a[0])
    g_in, g_conv = _gather([_bf(tr_(w_in)), conv_w[0]], name="gather_weights")
    conv_full = g_conv.transpose(1, 0, 2).reshape(CONV_K, CONV_DIM)
    o_send, o_recv, o_srcs, o_lands, o_zero = _split_start(
        "all", [_bf(w_attn_out[0]), _bf(w_ssm_out[0]), _bf(w_o[0])], name="gather_out_weights_start")

    core = lax.axis_index("c").astype(jnp.int32).reshape(1)
    chip = 2 * lax.axis_index("x") + lax.axis_index("y")
    me = 2 * chip + lax.axis_index("c")

    def out_weights(after):
        srcs, lands = _split_wait("all", o_send, o_recv, o_srcs, o_lands, after, name="gather_out_weights_wait")
        g_wa, g_ws, g_wo = _own_slab(lands, srcs, me, True)
        return (g_wa.reshape(SB_WIDTH, D_MODEL), g_ws.reshape(SSD_WIDTH, D_MODEL), g_wo.reshape(D_MODEL, D_MODEL))

    def send_weight_grads(grads):
        slabs = [grads["w_in_t"],
                 grads["conv_w"].reshape(CONV_K, N_DEV, CONV_DIM // N_DEV).transpose(1, 0, 2),
                 _bf(grads["w_attn_out"].reshape(N_DEV, SB_WIDTH // N_DEV, D_MODEL)),
                 _bf(grads["w_ssm_out"].reshape(N_DEV, SSD_WIDTH // N_DEV, D_MODEL)),
                 _bf(grads["w_o"].reshape(N_DEV, D_MODEL // N_DEV, D_MODEL))]
        recv = _pair_exchange(slabs, [], name="exchange_pairs")
        names = ("w_in", "conv_w", "w_attn_out", "w_ssm_out", "w_o")
        sums = [_pair_add(a, b, core, name="pair_add_" + k, tc=4 * LANES if k == "w_in" else None)
                for k, a, b in zip(names, slabs, recv)]
        send_sems, recv_sems, srcs, lands, zero = _split_start("chips", sums, name="exchange_chips_start")
        return zero, (send_sems, recv_sems, srcs, lands)

    loss_part, grad_x, grads, (send_sems, recv_sems, srcs, lands) = _local_step(
        x.reshape(t, D_MODEL), loss_target.reshape(t, D_MODEL), nb, s, g_in.reshape(D_PROJ, D_MODEL),
        conv_full, conv_b, norm_w + o_zero[0:1, 0:1], dt_bias, a_log, d_skip, ssm_norm_w, out_weights,
        final_norm_w.reshape(1, D_MODEL), between=send_weight_grads)

    (p_small,) = _pair_exchange([], [_pack_small(grads)], name="exchange_small")
    srcs, lands = _split_wait("chips", send_sems, recv_sems, srcs, lands, grad_x, name="exchange_chips_wait")
    p_in, p_conv, p_wa, p_ws, p_wo = _own_slab(lands, srcs, chip, False)

    res = {}
    res["w_in"] = [a.T for a in _adamw(p_in, tr_(w_in), tr_(m_w_in), tr_(v_w_in), name="adamw_w_in", tc=2 * LANES)]
    res["conv_w"] = _adamw(p_conv, conv_w[0], m_conv_w[0], v_conv_w[0], name="adamw_conv_w")
    res["w_attn_out"] = _adamw(p_wa, w_attn_out[0], m_w_attn_out[0], v_w_attn_out[0], name="adamw_w_attn_out")
    res["w_ssm_out"] = _adamw(p_ws, w_ssm_out[0], m_w_ssm_out[0], v_w_ssm_out[0], name="adamw_w_ssm_out")
    res["w_o"] = _adamw(p_wo, w_o[0], m_w_o[0], v_w_o[0], name="adamw_w_o")
    sm = _adamw(p_small, _pack_small(weights), _pack_small(moms), _pack_small(vels), name="adamw_small")
    shapes = {k: weights[k].shape for k, _ in SMALL}
    sm = [_unpack_small(v, shapes) for v in sm]
    for k, _ in SMALL:
        res[k] = tuple(part[k] for part in sm)

    order = ("norm_w", "w_in", "conv_w", "conv_b", "dt_bias", "a_log", "d_skip", "ssm_norm_w",
             "w_attn_out", "w_ssm_out", "w_o", "final_norm_w")
    loss = lax.psum(loss_part[0, 0], ("x", "y", "c"))
    outs = [loss, grad_x.reshape(nb, s, D_MODEL)]
    for i in range(4):
        outs += [res[k][i].reshape(weights[k].shape) for k in order]
    return tuple(outs)
```

```python
import functools

import jax
import jax.numpy as jnp
from jax import lax
from jax.experimental import pallas as pl
from jax.experimental.pallas import tpu as pltpu

F32 = jnp.float32
BF16 = jnp.bfloat16

D_MODEL = 1024
SB_WIDTH = 1024
HEAD_DIM = 64
SSD_WIDTH = 2048
SSD_GROUPS = 4
SSD_HPG = 8
SSD_STATE = 128
SSD_CHUNK = 128
CONV_K = 4
CONV_DIM = 3072
N_DT = 32
D_PROJ = 11296
DT_COL0 = 9216
GATE_COL0 = DT_COL0 + N_DT
EPS = 1e-6
N_DEV = 8

ADAM_LR = 0.001
ADAM_B1 = 0.9
ADAM_B2 = 0.999
ADAM_EPS = 1e-08
ADAM_WD = 0.01
ADAM_STEP = 10

LANES = 128
VMEM_LIMIT = 56 * 1024 * 1024

CB_Q, CB_K, CB_V, CB_ZA, CB_ZS, CB_XBC = 0, 8, 16, 24, 32, 48

NT_DIMS = (((1,), (1,)), ((), ()))
TN_DIMS = (((0,), (0,)), ((), ()))


def _params(sem):
    return pltpu.CompilerParams(dimension_semantics=sem, vmem_limit_bytes=VMEM_LIMIT)


def _dot(a, b):
    return jnp.dot(a, b, preferred_element_type=F32)


def _dot_nt(a, b):
    return lax.dot_general(a, b, NT_DIMS, preferred_element_type=F32)


def _dot_tn(a, b):
    return lax.dot_general(a, b, TN_DIMS, preferred_element_type=F32)


def _bf(a):
    return a.astype(BF16)


def _sigmoid(x):
    return 0.5 + 0.5 * jnp.tanh(0.5 * x)


def _softplus(x):
    return jnp.maximum(x, 0.0) + jnp.log(1.0 + jnp.exp(-jnp.abs(x)))


def _trunc_split(a):
    bits = lax.bitcast_convert_type(a, jnp.uint32) & jnp.uint32(0xFFFF0000)
    hi = lax.bitcast_convert_type(bits, F32)
    return _bf(hi), _bf(a - hi)


MM_TILE = 1024
MM_TK = 2048


def _matmul(a, b, *, name, nt=False, ta=False, add=None, out_dtype=F32, tm=MM_TILE, tn=MM_TILE, tk=MM_TK, n=None):
    k, m = a.shape if ta else a.shape[::-1]
    n = (b.shape[0] if nt else b.shape[1]) if n is None else n
    tm, tn, tk = min(tm, m), min(tn, n), min(tk, k)
    assert m % tm == 0 and n % tn == 0 and k % tk == 0, (name, a.shape, b.shape)
    assert not (ta and nt)
    nk = k // tk
    has_add = add is not None

    def body(*refs):
        if has_add:
            a_ref, b_ref, add_ref, o_ref, acc_ref = refs
        else:
            a_ref, b_ref, o_ref, acc_ref = refs
        kk = pl.program_id(2)
        av, bv = _bf(a_ref[...]), _bf(b_ref[...])
        d = _dot_nt(av, bv) if nt else _dot_tn(av, bv) if ta else _dot(av, bv)
        if nk == 1:
            o_ref[...] = ((d + add_ref[...]) if has_add else d).astype(out_dtype)
            return

        @pl.when(kk == 0)
        def _():
            acc_ref[...] = jnp.zeros_like(acc_ref)

        acc_ref[...] += d

        @pl.when(kk == nk - 1)
        def _():
            r = acc_ref[...]
            if has_add:
                r = r + add_ref[...]
            o_ref[...] = r.astype(out_dtype)

    in_specs = [pl.BlockSpec((tk, tm), lambda j, i, kk: (kk, i)) if ta
                else pl.BlockSpec((tm, tk), lambda j, i, kk: (i, kk)),
                pl.BlockSpec((tn, tk), lambda j, i, kk: (j, kk)) if nt
                else pl.BlockSpec((tk, tn), lambda j, i, kk: (kk, j))]
    args = [a, b]
    if has_add:
        in_specs.append(pl.BlockSpec((tm, tn), lambda j, i, kk: (i, j)))
        args.append(add)
    return pl.pallas_call(
        body, grid=(n // tn, m // tm, nk), in_specs=in_specs,
        out_specs=pl.BlockSpec((tm, tn), lambda j, i, kk: (i, j)),
        out_shape=jax.ShapeDtypeStruct((m, n), out_dtype),
        scratch_shapes=[pltpu.VMEM((tm, tn), F32)],
        compiler_params=_params(("parallel", "parallel", "arbitrary")), name=name)(*args)


def _rms_fwd(x, w, *, tm=512):
    t = x.shape[0]

    def body(x_ref, w_ref, h_ref, r_ref):
        xv = x_ref[...]
        r = lax.rsqrt(jnp.mean(xv * xv, axis=-1, keepdims=True) + EPS)
        h_ref[...] = _bf(xv * r * w_ref[...])
        r_ref[...] = r

    row = lambda i: (i, 0)
    return pl.pallas_call(
        body, grid=(t // tm,),
        in_specs=[pl.BlockSpec((tm, D_MODEL), row), pl.BlockSpec((1, D_MODEL), lambda i: (0, 0))],
        out_specs=[pl.BlockSpec((tm, D_MODEL), row), pl.BlockSpec((tm, 1), row)],
        out_shape=[jax.ShapeDtypeStruct((t, D_MODEL), BF16), jax.ShapeDtypeStruct((t, 1), F32)],
        compiler_params=_params(("parallel",)), name="rms_fwd")(x, w)


def _rms_bwd(dh, x, r, w, dres, *, tm=512):
    t = x.shape[0]

    def body(dh_ref, x_ref, r_ref, w_ref, dres_ref, dx_ref, dw_ref):
        @pl.when(pl.program_id(0) == 0)
        def _():
            dw_ref[...] = jnp.zeros_like(dw_ref)

        nrm = x_ref[...] * r_ref[...]
        dhv = dh_ref[...]
        dw_ref[...] += jnp.sum(dhv * nrm, axis=0, keepdims=True)
        dn = dhv * w_ref[...]
        dx = r_ref[...] * (dn - nrm * jnp.mean(dn * nrm, axis=-1, keepdims=True))
        dx_ref[...] = dx + dres_ref[...]

    row = lambda i: (i, 0)
    fix = lambda i: (0, 0)
    return pl.pallas_call(
        body, grid=(t // tm,),
        in_specs=[pl.BlockSpec((tm, D_MODEL), row), pl.BlockSpec((tm, D_MODEL), row),
                  pl.BlockSpec((tm, 1), row), pl.BlockSpec((1, D_MODEL), fix),
                  pl.BlockSpec((tm, D_MODEL), row)],
        out_specs=[pl.BlockSpec((tm, D_MODEL), row), pl.BlockSpec((1, D_MODEL), fix)],
        out_shape=[jax.ShapeDtypeStruct((t, D_MODEL), F32), jax.ShapeDtypeStruct((1, D_MODEL), F32)],
        compiler_params=_params(("arbitrary",)), name="rms_bwd")(dh, x, r, w, dres)


ATT_BLK = 256


def _attn_masks(blk):
    r_io = lax.broadcasted_iota(jnp.int32, (blk, blk), 0)
    c_io = lax.broadcasted_iota(jnp.int32, (blk, blk), 1)
    return r_io, c_io


def _attn_tiles(nq):
    return nq * (nq + 1) // 2


def _attn_fwd(proj, nb, s):
    blk = min(ATT_BLK, s)
    nq = s // blk
    nt = _attn_tiles(nq)
    scale = HEAD_DIM ** -0.5

    def body(q_ref, k_ref, v_ref, o_ref, a_out, sg_out, q0_s, q1_s, kb_s, v0_s, v1_s, z_s, a_s, acc_s, cr_s):
        head0 = lax.broadcasted_iota(jnp.int32, (s, LANES), 1) < HEAD_DIM
        r_io, c_io = _attn_masks(blk)
        tri = c_io < r_io
        tri2 = jnp.concatenate([tri, tri], axis=0)
        u_gt = _bf(r_io > c_io)
        u2 = jnp.concatenate([u_gt, u_gt], axis=0)
        qv = q_ref[...] * scale
        q0_s[...] = _bf(jnp.where(head0, qv, 0.0))
        q1_s[...] = _bf(jnp.where(head0, 0.0, qv))
        kb_s[...] = _bf(k_ref[...])
        vv = v_ref[...]
        v0_s[...] = _bf(jnp.where(head0, vv, 0.0))
        v1_s[...] = _bf(jnp.where(head0, 0.0, vv))

        def scores(qcat, kblk, slot):
            k0 = pl.multiple_of(jnp.maximum(kblk, 0) * blk, blk)
            z_s[slot] = _dot_nt(qcat, kb_s[pl.ds(k0, blk), :])

        def weights(slot, tile, diag, zslot=None):
            z = z_s[slot if zslot is None else zslot]
            cr = cr_s[...]
            sp = _softplus(z)
            spm = jnp.where(tri2, sp, 0.0) if diag else sp
            hi, lo = _trunc_split(spm)
            ps = _dot(jnp.concatenate([hi, lo], axis=1), u2)
            lb = z - sp
            a = jnp.exp(lb - (ps + cr))
            if diag:
                a = jnp.where(tri2, a, 0.0)
            ab = _bf(a)
            a_s[slot] = ab
            a_out[tile] = ab
            sg_out[tile] = _bf(jnp.exp(lb))
            cr_s[...] = cr + ps[:, 0:1] + spm[:, 0:1]

        def apply(slot, k0):
            a = a_s[slot]
            vcat = jnp.concatenate([v0_s[pl.ds(k0, blk), :], v1_s[pl.ds(k0, blk), :]], axis=0)
            acc_s[...] += _dot(jnp.concatenate([a[:blk], a[blk:]], axis=1), vcat)

        def queries(qi):
            q0 = pl.multiple_of(qi * blk, blk)
            return jnp.concatenate([q0_s[pl.ds(q0, blk), :], q1_s[pl.ds(q0, blk), :]], axis=0)

        def first_scores(qi):
            qcat = queries(qi)
            scores(qcat, qi, 2)
            scores(qcat, qi - 1, 3)

        first_scores(jnp.int32(0))

        def qblock(qi, _):
            q0 = pl.multiple_of(qi * blk, blk)
            tile0 = qi * (qi + 1) // 2
            qcat = queries(qi)
            z_s[1] = z_s[3]
            cr_s[...] = jnp.zeros_like(cr_s)
            acc_s[...] = jnp.zeros_like(acc_s)
            weights(0, tile0 + qi, True, zslot=2)
            first_scores(jnp.minimum(qi + 1, nq - 1))

            def step(j, slot):
                scores(qcat, qi - j - 1, 1 - slot)
                weights(slot, tile0 + qi - j, False)
                apply(1 - slot, pl.multiple_of((qi - j + 1) * blk, blk))

            def pair(i, _):
                step(2 * i + 1, 1)
                step(2 * i + 2, 0)
                return 0

            lax.fori_loop(0, qi // 2, pair, 0)

            @pl.when(qi % 2 == 1)
            def _():
                step(qi, 1)

            apply(qi & 1, 0)
            o_ref[pl.ds(q0, blk), :] = acc_s[...]
            return 0

        lax.fori_loop(0, nq, qblock, 0)

    t = nb * s
    tiles = pl.BlockSpec((None, None, nt, 2 * blk, blk), lambda b, p: (b, p, 0, 0, 0))
    tiles_shape = jax.ShapeDtypeStruct((nb, 8, nt, 2 * blk, blk), BF16)
    return pl.pallas_call(
        body, grid=(nb, 8),
        in_specs=[pl.BlockSpec((s, LANES), lambda b, p: (b, CB_Q + p)),
                  pl.BlockSpec((s, LANES), lambda b, p: (b, CB_K + p)),
                  pl.BlockSpec((s, LANES), lambda b, p: (b, CB_V + p))],
        out_specs=[pl.BlockSpec((s, LANES), lambda b, p: (b, p)), tiles, tiles],
        out_shape=[jax.ShapeDtypeStruct((t, SB_WIDTH), F32), tiles_shape, tiles_shape],
        scratch_shapes=[pltpu.VMEM((s, LANES), BF16)] * 5 + [
            pltpu.VMEM((4, 2 * blk, blk), F32), pltpu.VMEM((2, 2 * blk, blk), BF16),
            pltpu.VMEM((blk, LANES), F32), pltpu.VMEM((2 * blk, 1), F32)],
        compiler_params=_params(("parallel", "parallel")), name="attn_fwd")(proj, proj, proj)


def _attn_bwd(proj, do, a_all, sg_all, nb, s):
    blk = min(ATT_BLK, s)
    nq = s // blk
    nt = _attn_tiles(nq)
    scale = HEAD_DIM ** -0.5

    def body(q_ref, k_ref, v_ref, do_ref, a_in, sg_in, dq_ref, dk_ref, dv_ref,
             qt0_s, qt1_s, k0_s, k1_s, vb_s, d0_s, d1_s, dt0_s, dt1_s, da_s, dz_s, dkt_acc, dvt_acc, dq_acc, cw_s):
        head0 = lax.broadcasted_iota(jnp.int32, (s, LANES), 1) < HEAD_DIM
        r_io, c_io = _attn_masks(blk)
        tri = c_io < r_io
        tri2 = jnp.concatenate([tri, tri], axis=0)
        u_lt = _bf(r_io < c_io)
        qv = q_ref[...].astype(F32) * scale
        qt0_s[...] = _bf(jnp.where(head0, qv, 0.0).T)
        qt1_s[...] = _bf(jnp.where(head0, 0.0, qv).T)
        kv = k_ref[...]
        k0_s[...] = _bf(jnp.where(head0, kv, 0.0))
        k1_s[...] = _bf(jnp.where(head0, 0.0, kv))
        vb_s[...] = _bf(v_ref[...])
        dov = do_ref[...].astype(F32)
        d0 = jnp.where(head0, dov, 0.0)
        d1 = jnp.where(head0, 0.0, dov)
        d0_s[...] = _bf(d0)
        d1_s[...] = _bf(d1)
        dt0_s[...] = _bf(d0.T)
        dt1_s[...] = _bf(d1.T)
        dkt_acc[...] = jnp.zeros_like(dkt_acc)
        dvt_acc[...] = jnp.zeros_like(dvt_acc)

        def qblock(qi, _):
            q0 = pl.multiple_of(qi * blk, blk)
            tile0 = qi * (qi + 1) // 2
            qs_ = pl.ds(q0, blk)
            dcat = jnp.concatenate([d0_s[qs_, :], d1_s[qs_, :]], axis=0)
            qcat_t = jnp.concatenate([qt0_s[:, qs_], qt1_s[:, qs_]], axis=1)
            dcat_t = jnp.concatenate([dt0_s[:, qs_], dt1_s[:, qs_]], axis=1)

            def scores(kblk, slot):
                da_s[slot] = _dot_nt(dcat, vb_s[pl.ds(pl.multiple_of(kblk * blk, blk), blk), :])

            def grads(slot, kblk, diag):
                cw = cw_s[...]
                w = a_in[tile0 + kblk].astype(F32) * da_s[slot]
                wex = _dot(_bf(w), u_lt)
                dz = w - sg_in[tile0 + kblk].astype(F32) * (w + (wex + cw))
                if diag:
                    dz = jnp.where(tri2, dz, 0.0)
                dz_s[slot] = _bf(dz)
                cw_s[...] = cw + wex[:, blk - 1:blk] + w[:, blk - 1:blk]

            def apply(slot, kblk):
                ks_ = pl.ds(pl.multiple_of(kblk * blk, blk), blk)
                dz = dz_s[slot]
                dq_acc[...] += _dot(jnp.concatenate([dz[:blk], dz[blk:]], axis=1),
                                    jnp.concatenate([k0_s[ks_, :], k1_s[ks_, :]], axis=0))
                dkt_acc[:, ks_] += _dot(qcat_t, dz)
                dvt_acc[:, ks_] += _dot(dcat_t, a_in[tile0 + kblk])

            def step(j, slot, diag):
                if not diag:
                    scores(j + 1, 1 - slot)
                grads(slot, j, diag)

                @pl.when(j > 0)
                def _():
                    apply(1 - slot, j - 1)

            scores(0, 0)
            dq_acc[...] = jnp.zeros_like(dq_acc)
            cw_s[...] = jnp.zeros_like(cw_s)

            def pair(i, _):
                step(2 * i, 0, False)
                step(2 * i + 1, 1, False)
                return 0

            lax.fori_loop(0, qi // 2, pair, 0)

            @pl.when(qi % 2 == 1)
            def _():
                step(qi - 1, 0, False)
                step(qi, 1, True)

            @pl.when(qi % 2 == 0)
            def _():
                step(qi, 0, True)

            apply(qi & 1, qi)
            dq_ref[qs_, :] = _bf(dq_acc[...] * scale)
            return 0

        lax.fori_loop(0, nq, qblock, 0)
        dk_ref[...] = _bf(dkt_acc[...].T)
        dv_ref[...] = _bf(dvt_acc[...].T)

    t = nb * s
    hp = lambda b, p: (b, p)
    out = jax.ShapeDtypeStruct((t, SB_WIDTH), BF16)
    tiles = pl.BlockSpec((None, None, nt, 2 * blk, blk), lambda b, p: (b, p, 0, 0, 0))
    return pl.pallas_call(
        body, grid=(nb, 8),
        in_specs=[pl.BlockSpec((s, LANES), lambda b, p: (b, CB_Q + p)),
                  pl.BlockSpec((s, LANES), lambda b, p: (b, CB_K + p)),
                  pl.BlockSpec((s, LANES), lambda b, p: (b, CB_V + p)),
                  pl.BlockSpec((s, LANES), hp), tiles, tiles],
        out_specs=[pl.BlockSpec((s, LANES), hp)] * 3,
        out_shape=[out, out, out],
        scratch_shapes=[pltpu.VMEM((LANES, s), BF16)] * 2 + [pltpu.VMEM((s, LANES), BF16)] * 5 + [
            pltpu.VMEM((LANES, s), BF16)] * 2 + [
            pltpu.VMEM((2, 2 * blk, blk), F32), pltpu.VMEM((2, 2 * blk, blk), BF16),
            pltpu.VMEM((LANES, s), F32), pltpu.VMEM((LANES, s), F32),
            pltpu.VMEM((blk, LANES), F32), pltpu.VMEM((2 * blk, 1), F32)],
        compiler_params=_params(("parallel", "parallel")), name="attn_bwd")(proj, proj, proj, do, a_all, sg_all)


CONV_TC = 256
CONV_ROWS = 64
HALO = 8


def _conv_rows(ws, w_ref, b_ref, n):
    taps = [ws[HALO - kk:HALO - kk + n, :] for kk in range(CONV_K)]
    y = b_ref[...] + w_ref[CONV_K - 1:CONV_K, :] * taps[0]
    for kk in range(1, CONV_K):
        y = y + w_ref[CONV_K - 1 - kk:CONV_K - kk, :] * taps[kk]
    return y, taps


def _conv_fwd(proj, conv_w, conv_b, nb, s):
    tc, rows = CONV_TC, min(CONV_ROWS, s)
    cb0 = CB_XBC * LANES // tc

    def body(x_ref, w_ref, b_ref, o_ref, xp, ws):
        xp[0:HALO, :] = jnp.zeros((HALO, tc), F32)
        xp[HALO:HALO + s, :] = x_ref[...].astype(F32)

        def chunk(i, _):
            r0 = pl.multiple_of(i * rows, rows)
            ws[...] = xp[pl.ds(r0, rows + HALO), :]
            y, _ = _conv_rows(ws, w_ref, b_ref, rows)
            o_ref[pl.ds(r0, rows), :] = y * _sigmoid(y)
            return 0

        lax.fori_loop(0, s // rows, chunk, 0)

    t = nb * s
    return pl.pallas_call(
        body, grid=(nb, CONV_DIM // tc),
        in_specs=[pl.BlockSpec((s, tc), lambda b, c: (b, cb0 + c)),
                  pl.BlockSpec((CONV_K, tc), lambda b, c: (0, c)),
                  pl.BlockSpec((1, tc), lambda b, c: (0, c))],
        out_specs=pl.BlockSpec((s, tc), lambda b, c: (b, c)),
        out_shape=jax.ShapeDtypeStruct((t, CONV_DIM), F32),
        scratch_shapes=[pltpu.VMEM((s + HALO, tc), F32), pltpu.VMEM((rows + HALO, tc), F32)],
        compiler_params=_params(("parallel", "parallel")), name="conv_fwd")(proj, conv_w, conv_b)


def _conv_bwd(proj, dact, conv_w, conv_b, dproj, nb, s, *, col0, name):
    width = dact.shape[1]
    tc, rows = min(CONV_TC, width), min(CONV_ROWS, s)
    cb0 = (CB_XBC * LANES + col0) // tc
    wb0 = col0 // tc

    def body(x_ref, d_ref, w_ref, b_ref, _, dx_ref, dw_ref, db_ref, xp, dp, dys, ws):
        @pl.when(pl.program_id(1) == 0)
        def _():
            dw_ref[...] = jnp.zeros_like(dw_ref)
            db_ref[...] = jnp.zeros_like(db_ref)

        n = rows + HALO
        pad = jnp.zeros((HALO, tc), F32)
        xp[0:HALO, :] = pad
        xp[HALO:HALO + s, :] = x_ref[...].astype(F32)
        xp[HALO + s:HALO + s + HALO, :] = pad
        dp[0:s, :] = d_ref[...]
        dp[s:s + HALO, :] = pad

        def chunk(i, acc):
            r0 = pl.multiple_of(i * rows, rows)
            ws[...] = xp[pl.ds(r0, n + HALO), :]
            y, taps = _conv_rows(ws, w_ref, b_ref, n)
            sg = _sigmoid(y)
            dy = dp[pl.ds(r0, n), :] * (sg * (1.0 + y * (1.0 - sg)))
            dys[...] = dy
            dx = w_ref[CONV_K - 1:CONV_K, :] * dy[0:rows]
            for kk in range(1, CONV_K):
                dx = dx + w_ref[CONV_K - 1 - kk:CONV_K - kk, :] * dys[kk:kk + rows, :]
            dx_ref[pl.ds(r0, rows), :] = _bf(dx)
            dyc = dy[0:rows]
            new = [acc[0] + jnp.sum(dyc, axis=0, keepdims=True)]
            for kk in range(CONV_K):
                new.append(acc[1 + kk] + jnp.sum(dyc * taps[kk][0:rows], axis=0, keepdims=True))
            return tuple(new)

        acc = lax.fori_loop(0, s // rows, chunk, (jnp.zeros((1, tc), F32),) * (1 + CONV_K))
        db_ref[...] += acc[0]
        for kk in range(CONV_K):
            dw_ref[CONV_K - 1 - kk:CONV_K - kk, :] += acc[1 + kk]

    t = nb * s
    return pl.pallas_call(
        body, grid=(width // tc, nb),
        in_specs=[pl.BlockSpec((s, tc), lambda c, b: (b, cb0 + c)),
                  pl.BlockSpec((s, tc), lambda c, b: (b, c)),
                  pl.BlockSpec((CONV_K, tc), lambda c, b: (0, wb0 + c)),
                  pl.BlockSpec((1, tc), lambda c, b: (0, wb0 + c)), pl.BlockSpec(memory_space=pl.ANY)],
        out_specs=[pl.BlockSpec((s, tc), lambda c, b: (b, cb0 + c)),
                   pl.BlockSpec((CONV_K, tc), lambda c, b: (0, c)),
                   pl.BlockSpec((1, tc), lambda c, b: (0, c))],
        out_shape=[jax.ShapeDtypeStruct(dproj.shape, dproj.dtype),
                   jax.ShapeDtypeStruct((CONV_K, width), F32),
                   jax.ShapeDtypeStruct((1, width), F32)],
        input_output_aliases={4: 0},
        scratch_shapes=[pltpu.VMEM((s + 2 * HALO, tc), F32), pltpu.VMEM((s + HALO, tc), F32),
                        pltpu.VMEM((rows + HALO, tc), F32), pltpu.VMEM((rows + 2 * HALO, tc), F32)],
        compiler_params=_params(("parallel", "arbitrary")), name=name)(proj, dact, conv_w, conv_b, dproj)


L = SSD_CHUNK
N_PAIR = SSD_HPG // 2


def _pair_lanes(v0, v1, lane):
    return jnp.where(lane < HEAD_DIM, v0, v1)


def _onehot(shape, row_shift, col_shift):
    row = lax.broadcasted_iota(jnp.int32, shape, 0)
    col = lax.broadcasted_iota(jnp.int32, shape, 1)
    return _bf(lax.shift_right_logical(row, row_shift) == lax.shift_right_logical(col, col_shift))


def _ssd_tables():
    sp = _onehot((LANES, N_PAIR * LANES), 0, 6)
    sf = _onehot((LANES, SSD_HPG * LANES), 0, 7)
    return (jnp.concatenate([sp] * 3, axis=0), jnp.concatenate([sf] * 3, axis=0),
            _onehot((N_PAIR * LANES, LANES), 6, 0), _onehot((SSD_HPG * LANES, LANES), 7, 0))


def _split3(a):
    a1 = _bf(a)
    r1 = a - a1.astype(F32)
    a2 = _bf(r1)
    return jnp.concatenate([a1, a2, _bf(r1 - a2.astype(F32))], axis=1)


def _split2(a):
    hi, lo = _trunc_split(a)
    return jnp.concatenate([hi, lo], axis=1)


def _tri_dot(tri_bf, a):
    a1 = _bf(a)
    r1 = a - a1.astype(F32)
    a2 = _bf(r1)
    parts = jnp.concatenate([a1, a2, _bf(r1 - a2.astype(F32))], axis=0)
    return _dot(jnp.concatenate([tri_bf] * 3, axis=1), parts)


HEAD_W = 4 * LANES + 2 * N_PAIR * LANES + SSD_HPG * LANES


def _ssd_head(dtr_ref, bias_ref, a_neg, c, tri_bf, spread_pair, spread_full, head_s, slot):
    raw = dtr_ref[pl.ds(pl.multiple_of(c * L, L), L), :] + bias_ref[...]
    dt = _softplus(raw)
    acs = _tri_dot(tri_bf, dt * a_neg)
    acs3 = _split3(acs)
    both = _dot(jnp.concatenate([_split3(dt), acs3], axis=0), spread_pair)
    parts = (raw, dt, acs, acs.T, both[:L], both[L:], _dot(acs3, spread_full))
    off = 0
    for p in parts:
        head_s[slot, :, off:off + p.shape[1]] = p
        off += p.shape[1]


def _ssd_head_load(head_s, slot):
    out, off = [], 0
    for w in (LANES, LANES, LANES, L, N_PAIR * LANES, N_PAIR * LANES, SSD_HPG * LANES):
        out.append(head_s[slot, :, off:off + w])
        off += w
    return out


def _ssd_fwd(xbc, dtr, bias, alog, dskip, nb, s):
    nc = s // L
    t = nb * s

    def body(x_ref, b_ref, c_ref, dtr_ref, bias_ref, alog_ref, dsk_ref, y_ref, hs_ref, h_scr, head_s):
        lane = lax.broadcasted_iota(jnp.int32, (L, LANES), 1)
        lane1 = lax.broadcasted_iota(jnp.int32, (1, LANES), 1)
        rowc = lax.broadcasted_iota(jnp.int32, (L, 1), 0)
        rr = lax.broadcasted_iota(jnp.int32, (L, L), 0)
        cc = lax.broadcasted_iota(jnp.int32, (L, L), 1)
        causal = rr >= cc
        h_scr[...] = jnp.zeros_like(h_scr)
        dsk = dsk_ref[...]
        spread_pair, spread_full, _, _ = _ssd_tables()
        a_neg = -jnp.exp(alog_ref[...])
        head = functools.partial(_ssd_head, dtr_ref, bias_ref, a_neg, tri_bf=_bf(cc <= rr),
                                 spread_pair=spread_pair, spread_full=spread_full, head_s=head_s)
        head(jnp.int32(0), slot=0)

        def chunk(c, slot):
            r0 = pl.multiple_of(c * L, L)
            head(jnp.minimum(c + 1, nc - 1), slot=1 - slot)
            _, _, acs, acs_t, dt_all, acs_all, acs_full = _ssd_head_load(head_s, slot)
            bb = _bf(b_ref[pl.ds(r0, L), :])
            cb = _bf(c_ref[pl.ds(r0, L), :])
            gm = _dot_nt(cb, bb)
            for pr in range(N_PAIR):
                j0, j1 = 2 * pr, 2 * pr + 1
                ms = []
                for j in (j0, j1):
                    seg = acs_full[:, j * LANES:(j + 1) * LANES] - acs_t[j:j + 1, :]
                    ms.append(_bf(gm * jnp.exp(jnp.where(causal, seg, -1e30))))
                x = x_ref[pl.ds(r0, L), pr * LANES:(pr + 1) * LANES]
                dt_p = dt_all[:, pr * LANES:(pr + 1) * LANES]
                acs_p = acs_all[:, pr * LANES:(pr + 1) * LANES]
                last_p = acs_p[L - 1:L, :]
                d_p = _pair_lanes(dsk[:, j0:j0 + 1], dsk[:, j1:j1 + 1], lane1)
                xd = x * dt_p
                xdb = _bf(xd)
                yds = _dot(jnp.concatenate(ms, axis=0), xdb)
                yd = _pair_lanes(yds[:L], yds[L:], lane)
                hp = h_scr[pr]
                yo = _dot_nt(cb, _bf(hp)) * jnp.exp(acs_p)
                y_ref[pl.ds(r0, L), pr * LANES:(pr + 1) * LANES] = yd + yo + x * d_p
                hs_ref[c, pr] = hp
                sp = _dot_tn(_bf(xd * jnp.exp(last_p - acs_p)), bb)
                cd = jnp.where(rowc < HEAD_DIM, jnp.exp(acs[L - 1:L, j0:j0 + 1]),
                               jnp.exp(acs[L - 1:L, j1:j1 + 1]))
                h_scr[pr] = hp * cd + sp

        def two_chunks(i, _):
            chunk(2 * i, 0)
            chunk(2 * i + 1, 1)
            return 0

        lax.fori_loop(0, nc // 2, two_chunks, 0)

    assert nc % 2 == 0
    grp = lambda b, g: (g, 0, 0)
    return pl.pallas_call(
        body, grid=(nb, SSD_GROUPS),
        in_specs=[pl.BlockSpec((s, 4 * LANES), lambda b, g: (b, g)),
                  pl.BlockSpec((s, LANES), lambda b, g: (b, 16 + g)),
                  pl.BlockSpec((s, LANES), lambda b, g: (b, 20 + g)),
                  pl.BlockSpec((None, s, LANES), lambda b, g: (g, b, 0)),
                  pl.BlockSpec((None, 1, LANES), grp), pl.BlockSpec((None, 1, LANES), grp),
                  pl.BlockSpec((None, 1, LANES), grp)],
        out_specs=[pl.BlockSpec((s, 4 * LANES), lambda b, g: (b, g)),
                   pl.BlockSpec((None, None, nc, N_PAIR, L, LANES), lambda b, g: (b, g, 0, 0, 0, 0))],
        out_shape=[jax.ShapeDtypeStruct((t, SSD_WIDTH), F32),
                   jax.ShapeDtypeStruct((nb, SSD_GROUPS, nc, N_PAIR, L, LANES), F32)],
        scratch_shapes=[pltpu.VMEM((N_PAIR, L, LANES), F32), pltpu.VMEM((2, L, HEAD_W), F32)],
        compiler_params=_params(("parallel", "parallel")), name="ssd_fwd")(
            xbc, xbc, xbc, dtr, bias, alog, dskip)


def _ssd_bwd(xbc, dtr, bias, alog, dskip, hs, dy, nb, s):
    nc = s // L
    t = nb * s

    def body(x_ref, b_ref, c_ref, dtr_ref, bias_ref, alog_ref, dsk_ref, hs_ref, dy_ref,
             dx_ref, db_ref, dc_ref, ddtr_ref, dbias_ref, dalog_ref, ddsk_ref, dh_scr, head_s):
        lane = lax.broadcasted_iota(jnp.int32, (L, LANES), 1)
        lane1 = lax.broadcasted_iota(jnp.int32, (1, LANES), 1)
        rowc = lax.broadcasted_iota(jnp.int32, (L, 1), 0)
        rr = lax.broadcasted_iota(jnp.int32, (L, L), 0)
        cc = lax.broadcasted_iota(jnp.int32, (L, L), 1)
        causal = rr >= cc
        tri_t = _bf(cc >= rr)
        dh_scr[...] = jnp.zeros_like(dh_scr)
        dsk = dsk_ref[...]
        a_neg = -jnp.exp(alog_ref[...])

        @pl.when(pl.program_id(1) == 0)
        def _():
            dbias_ref[...] = jnp.zeros_like(dbias_ref)
            dalog_ref[...] = jnp.zeros_like(dalog_ref)
            ddsk_ref[...] = jnp.zeros_like(ddsk_ref)

        spread_pair, spread_full, sum_pair, sum_full = _ssd_tables()
        sum_pair2 = jnp.concatenate([sum_pair] * 2, axis=0)
        sum_full2 = jnp.concatenate([sum_full] * 2, axis=0)
        head = functools.partial(_ssd_head, dtr_ref, bias_ref, a_neg, tri_bf=_bf(cc <= rr),
                                 spread_pair=spread_pair, spread_full=spread_full, head_s=head_s)
        head(jnp.int32(nc - 1), slot=0)

        def chunk(c, slot):
            r0 = pl.multiple_of(c * L, L)
            head(jnp.maximum(c - 1, 0), slot=1 - slot)
            raw, dt, acs, acs_t, dt_all, acs_all, acs_full = _ssd_head_load(head_s, slot)
            bb = _bf(b_ref[pl.ds(r0, L), :])
            cb = _bf(c_ref[pl.ds(r0, L), :])
            gm = _dot_nt(cb, bb)
            dg = jnp.zeros((L, L), F32)
            dbacc = jnp.zeros((L, LANES), F32)
            dcacc = jnp.zeros((L, LANES), F32)
            dacs_t = jnp.zeros((L, L), F32)
            ta_all, te_all, qm_all, tc_all, td_all, dcd_all = [], [], [], [], [], []
            for pr in range(N_PAIR):
                js = (2 * pr, 2 * pr + 1)
                lms, mfs = [], []
                for j in js:
                    seg = acs_full[:, j * LANES:(j + 1) * LANES] - acs_t[j:j + 1, :]
                    lm = jnp.exp(jnp.where(causal, seg, -1e30))
                    lms.append(lm)
                    mfs.append(gm * lm)
                x = x_ref[pl.ds(r0, L), pr * LANES:(pr + 1) * LANES]
                dyv = dy_ref[pl.ds(r0, L), pr * LANES:(pr + 1) * LANES]
                dt_p = dt_all[:, pr * LANES:(pr + 1) * LANES]
                acs_p = acs_all[:, pr * LANES:(pr + 1) * LANES]
                last_p = acs_p[L - 1:L, :]
                d_p = _pair_lanes(dsk[:, js[0]:js[0] + 1], dsk[:, js[1]:js[1] + 1], lane1)
                e_p = jnp.exp(acs_p)
                dte_p = jnp.exp(last_p - acs_p)
                xd = x * dt_p
                xdb = _bf(xd)
                hp = hs_ref[c, pr]
                hb = _bf(hp)
                dye = _bf(dyv * e_p)
                yo = _dot_nt(cb, hb) * e_p
                dcacc = dcacc + _dot(dye, hb)
                dh_out = _dot_tn(dye, cb)
                dyb = _bf(dyv)
                dms = _dot_nt(_bf(jnp.concatenate(
                    [jnp.where(lane < HEAD_DIM, dyv, 0.0), jnp.where(lane < HEAD_DIM, 0.0, dyv)], axis=0)), xdb)
                mty = _dot_tn(_bf(jnp.concatenate(mfs, axis=1)), dyb)
                dxd = jnp.where(lane < HEAD_DIM, mty[:L], mty[L:])
                for hh in range(2):
                    dm = dms[hh * L:(hh + 1) * L]
                    dg = dg + dm * lms[hh]
                    qm = dm * mfs[hh]
                    qm_all.append(qm)
                    dacs_t = dacs_t - jnp.where(rr == js[hh], jnp.sum(qm, axis=0, keepdims=True), 0.0)
                ds = dh_scr[pr]
                dsb = _bf(ds)
                xdd = xd * dte_p
                dxdd = _dot_nt(bb, dsb)
                dbacc = dbacc + _dot(_bf(xdd), dsb)
                dxd = dxd + dxdd * dte_p
                cd = jnp.where(rowc < HEAD_DIM, jnp.exp(acs[L - 1:L, js[0]:js[0] + 1]),
                               jnp.exp(acs[L - 1:L, js[1]:js[1] + 1]))
                prod = ds * hp
                dcd_all += [jnp.sum(prod[:HEAD_DIM], axis=0, keepdims=True),
                            jnp.sum(prod[HEAD_DIM:], axis=0, keepdims=True)]
                dh_scr[pr] = dh_out + ds * cd
                tcv = dxdd * xdd
                ta_all.append(dyv * yo - tcv)
                tc_all.append(jnp.sum(tcv, axis=0, keepdims=True))
                te_all.append(dxd * x)
                td_all.append(jnp.sum(dyv * x, axis=0, keepdims=True))
                dx_ref[pl.ds(r0, L), pr * LANES:(pr + 1) * LANES] = dxd * dt_p + dyv * d_p
            dgb = _bf(dg)
            dc_ref[pl.ds(r0, L), :] = dcacc + _dot(dgb, bb)
            db_ref[pl.ds(r0, L), :] = dbacc + _dot_tn(dgb, cb)
            last_add = jnp.zeros((1, LANES), F32)
            dd_add = jnp.zeros((1, LANES), F32)
            cd_row = jnp.exp(acs[L - 1:L, :])
            for j in range(SSD_HPG):
                hm = (lane1 >= HEAD_DIM) if j % 2 else (lane1 < HEAD_DIM)
                tcj = jnp.sum(jnp.where(hm, tc_all[j // 2], 0.0), axis=1, keepdims=True)
                tdj = jnp.sum(jnp.where(hm, td_all[j // 2], 0.0), axis=1, keepdims=True)
                dcdj = jnp.sum(dcd_all[j], axis=1, keepdims=True)
                last_add = last_add + jnp.where(lane1 == j, tcj + dcdj * cd_row, 0.0)
                dd_add = dd_add + jnp.where(lane1 == j, tdj, 0.0)
            dacs = (_dot(_split2(jnp.concatenate(ta_all, axis=1)), sum_pair2)
                    + _dot(_split2(jnp.concatenate(qm_all, axis=1)), sum_full2)
                    + dacs_t.T + jnp.where(rowc == L - 1, last_add, 0.0))
            ddta = _tri_dot(tri_t, dacs)
            ddt = _dot(_bf(jnp.concatenate(te_all, axis=1)), sum_pair) + ddta * a_neg
            ddraw = ddt * _sigmoid(raw)
            ddtr_ref[pl.ds(r0, L), :] = ddraw
            dbias_ref[...] += jnp.sum(ddraw, axis=0, keepdims=True)
            dalog_ref[...] += jnp.sum(ddta * dt, axis=0, keepdims=True) * a_neg
            ddsk_ref[...] += dd_add

        def two_chunks(i, _):
            chunk(nc - 1 - 2 * i, 0)
            chunk(nc - 2 - 2 * i, 1)
            return 0

        lax.fori_loop(0, nc // 2, two_chunks, 0)

    assert nc % 2 == 0
    grp = lambda g, b: (g, 0, 0)
    small = jax.ShapeDtypeStruct((SSD_GROUPS, 1, LANES), F32)
    return pl.pallas_call(
        body, grid=(SSD_GROUPS, nb),
        in_specs=[pl.BlockSpec((s, 4 * LANES), lambda g, b: (b, g)),
                  pl.BlockSpec((s, LANES), lambda g, b: (b, 16 + g)),
                  pl.BlockSpec((s, LANES), lambda g, b: (b, 20 + g)),
                  pl.BlockSpec((None, s, LANES), lambda g, b: (g, b, 0)),
                  pl.BlockSpec((None, 1, LANES), grp), pl.BlockSpec((None, 1, LANES), grp),
                  pl.BlockSpec((None, 1, LANES), grp),
                  pl.BlockSpec((None, None, nc, N_PAIR, L, LANES), lambda g, b: (b, g, 0, 0, 0, 0)),
                  pl.BlockSpec((s, 4 * LANES), lambda g, b: (b, g))],
        out_specs=[pl.BlockSpec((s, 4 * LANES), lambda g, b: (b, g)),
                   pl.BlockSpec((s, LANES), lambda g, b: (b, g)),
                   pl.BlockSpec((s, LANES), lambda g, b: (b, g)),
                   pl.BlockSpec((None, s, LANES), lambda g, b: (g, b, 0)),
                   pl.BlockSpec((None, 1, LANES), grp), pl.BlockSpec((None, 1, LANES), grp),
                   pl.BlockSpec((None, 1, LANES), grp)],
        out_shape=[jax.ShapeDtypeStruct((t, SSD_WIDTH), F32),
                   jax.ShapeDtypeStruct((t, SSD_GROUPS * SSD_STATE), F32),
                   jax.ShapeDtypeStruct((t, SSD_GROUPS * SSD_STATE), F32),
                   jax.ShapeDtypeStruct((SSD_GROUPS, t, LANES), F32), small, small, small],
        scratch_shapes=[pltpu.VMEM((N_PAIR, L, LANES), F32), pltpu.VMEM((2, L, HEAD_W), F32)],
        compiler_params=_params(("parallel", "arbitrary")), name="ssd_bwd")(
            xbc, xbc, xbc, dtr, bias, alog, dskip, hs, dy)


GN = SSD_WIDTH // SSD_GROUPS


def _post_fwd(o, proj, y, nw, *, tm=256):
    t = o.shape[0]

    def body(o_ref, za_ref, y_ref, zs_ref, nw_ref, ya_ref, ys_ref):
        za = za_ref[...].astype(F32)
        ya_ref[...] = _bf(o_ref[...] * (za * _sigmoid(za)))
        zs = zs_ref[...].astype(F32)
        yz = y_ref[...] * (zs * _sigmoid(zs))
        for g in range(SSD_GROUPS):
            sl = slice(g * GN, (g + 1) * GN)
            v = yz[:, sl]
            r = lax.rsqrt(jnp.mean(v * v, axis=-1, keepdims=True) + EPS)
            ys_ref[:, sl] = _bf(v * r * nw_ref[:, sl])

    return pl.pallas_call(
        body, grid=(t // tm,),
        in_specs=[pl.BlockSpec((tm, SB_WIDTH), lambda i: (i, 0)),
                  pl.BlockSpec((tm, SB_WIDTH), lambda i: (i, CB_ZA * LANES // SB_WIDTH)),
                  pl.BlockSpec((tm, SSD_WIDTH), lambda i: (i, 0)),
                  pl.BlockSpec((tm, SSD_WIDTH), lambda i: (i, CB_ZS * LANES // SSD_WIDTH)),
                  pl.BlockSpec((1, SSD_WIDTH), lambda i: (0, 0))],
        out_specs=[pl.BlockSpec((tm, SB_WIDTH), lambda i: (i, 0)),
                   pl.BlockSpec((tm, SSD_WIDTH), lambda i: (i, 0))],
        out_shape=[jax.ShapeDtypeStruct((t, SB_WIDTH), BF16), jax.ShapeDtypeStruct((t, SSD_WIDTH), BF16)],
        compiler_params=_params(("parallel",)), name="post_fwd")(o, proj, y, proj, nw)


def _post_bwd(dya, o, proj, dys, y, nw, dproj, *, tm=256):
    t = o.shape[0]
    zw = SB_WIDTH + SSD_WIDTH

    def body(dya_ref, o_ref, za_ref, dys_ref, y_ref, zs_ref, nw_ref, _, do_ref, dy_ref, dnw_ref, dz_ref):
        @pl.when(pl.program_id(0) == 0)
        def _():
            dnw_ref[...] = jnp.zeros_like(dnw_ref)

        za = za_ref[...].astype(F32)
        sa = _sigmoid(za)
        d = dya_ref[...].astype(F32)
        do_ref[...] = _bf(d * (za * sa))
        dz_ref[:, :SB_WIDTH] = _bf(d * o_ref[...] * (sa * (1.0 + za * (1.0 - sa))))
        zs = zs_ref[...].astype(F32)
        ss = _sigmoid(zs)
        silu = zs * ss
        yv = y_ref[...]
        yz = yv * silu
        dv = dys_ref[...].astype(F32)
        for g in range(SSD_GROUPS):
            sl = slice(g * GN, (g + 1) * GN)
            v = yz[:, sl]
            r = lax.rsqrt(jnp.mean(v * v, axis=-1, keepdims=True) + EPS)
            nrm = v * r
            dg = dv[:, sl]
            dnw_ref[:, sl] += jnp.sum(dg * nrm, axis=0, keepdims=True)
            dn = dg * nw_ref[:, sl]
            dyz = r * (dn - nrm * jnp.mean(dn * nrm, axis=-1, keepdims=True))
            dy_ref[:, sl] = dyz * silu[:, sl]
            dz_ref[:, SB_WIDTH + g * GN:SB_WIDTH + (g + 1) * GN] = _bf(
                dyz * yv[:, sl] * (ss[:, sl] * (1.0 + zs[:, sl] * (1.0 - ss[:, sl]))))

    a_spec = pl.BlockSpec((tm, SB_WIDTH), lambda i: (i, 0))
    s_spec = pl.BlockSpec((tm, SSD_WIDTH), lambda i: (i, 0))
    w_spec = pl.BlockSpec((1, SSD_WIDTH), lambda i: (0, 0))
    return pl.pallas_call(
        body, grid=(t // tm,),
        in_specs=[a_spec, a_spec, pl.BlockSpec((tm, SB_WIDTH), lambda i: (i, CB_ZA * LANES // SB_WIDTH)),
                  s_spec, s_spec, pl.BlockSpec((tm, SSD_WIDTH), lambda i: (i, CB_ZS * LANES // SSD_WIDTH)),
                  w_spec, pl.BlockSpec(memory_space=pl.ANY)],
        out_specs=[a_spec, s_spec, w_spec, pl.BlockSpec((tm, zw), lambda i: (i, 1))],
        out_shape=[jax.ShapeDtypeStruct((t, SB_WIDTH), BF16), jax.ShapeDtypeStruct((t, SSD_WIDTH), F32),
                   jax.ShapeDtypeStruct((1, SSD_WIDTH), F32), jax.ShapeDtypeStruct(dproj.shape, dproj.dtype)],
        input_output_aliases={7: 3},
        compiler_params=_params(("arbitrary",)), name="post_bwd")(dya, o, proj, dys, y, proj, nw, dproj)


def _merge_fwd(ya, ys, proj_g, *, tm=512):
    t = ya.shape[0]

    def body(ya_ref, ys_ref, ga_ref, gs_ref, m_ref):
        m_ref[...] = _bf(_sigmoid(ga_ref[...].astype(F32)) * ya_ref[...].astype(F32)
                         + _sigmoid(gs_ref[...].astype(F32)) * ys_ref[...].astype(F32))

    spec = pl.BlockSpec((tm, D_MODEL), lambda i: (i, 0))
    return pl.pallas_call(
        body, grid=(t // tm,),
        in_specs=[spec, spec, pl.BlockSpec((tm, D_MODEL), lambda i: (i, 0)),
                  pl.BlockSpec((tm, D_MODEL), lambda i: (i, 1))],
        out_specs=spec, out_shape=jax.ShapeDtypeStruct((t, D_MODEL), BF16),
        compiler_params=_params(("parallel",)), name="merge_fwd")(ya, ys, proj_g, proj_g)


def _merge_bwd(dm, ya, ys, proj_g, *, tm=512):
    t = ya.shape[0]

    def body(dm_ref, ya_ref, ys_ref, ga_ref, gs_ref, dya_ref, dys_ref, dg_ref):
        d = dm_ref[...].astype(F32)
        ga = _sigmoid(ga_ref[...].astype(F32))
        gs = _sigmoid(gs_ref[...].astype(F32))
        dya_ref[...] = _bf(d * ga)
        dys_ref[...] = _bf(d * gs)
        dg_ref[:, :D_MODEL] = _bf(d * ya_ref[...].astype(F32) * (ga * (1.0 - ga)))
        dg_ref[:, D_MODEL:] = _bf(d * ys_ref[...].astype(F32) * (gs * (1.0 - gs)))

    spec = pl.BlockSpec((tm, D_MODEL), lambda i: (i, 0))
    return pl.pallas_call(
        body, grid=(t // tm,),
        in_specs=[spec, spec, spec, pl.BlockSpec((tm, D_MODEL), lambda i: (i, 0)),
                  pl.BlockSpec((tm, D_MODEL), lambda i: (i, 1))],
        out_specs=[spec, spec, pl.BlockSpec((tm, 2 * D_MODEL), lambda i: (i, 0))],
        out_shape=[jax.ShapeDtypeStruct((t, D_MODEL), BF16), jax.ShapeDtypeStruct((t, D_MODEL), BF16),
                   jax.ShapeDtypeStruct((t, 2 * D_MODEL), BF16)],
        compiler_params=_params(("parallel",)), name="merge_bwd")(dm, ya, ys, proj_g, proj_g)


def _final(x, out, wf, target, *, tm=512):
    t = x.shape[0]

    def body(x_ref, o_ref, w_ref, t_ref, loss_ref, dx_ref, dw_ref):
        @pl.when(pl.program_id(0) == 0)
        def _():
            loss_ref[...] = jnp.zeros_like(loss_ref)
            dw_ref[...] = jnp.zeros_like(dw_ref)

        x2 = x_ref[...] + o_ref[...]
        r = lax.rsqrt(jnp.mean(x2 * x2, axis=-1, keepdims=True) + EPS)
        nrm = x2 * r
        e = nrm * w_ref[...] - t_ref[...]
        row_loss = jnp.mean(e * e, axis=-1, keepdims=True)
        loss_ref[...] += 0.5 * jnp.sum(row_loss, axis=0, keepdims=True)
        dyv = e * (1.0 / D_MODEL)
        dw_ref[...] += jnp.sum(dyv * nrm, axis=0, keepdims=True)
        dn = dyv * w_ref[...]
        dx_ref[...] = r * (dn - nrm * jnp.mean(dn * nrm, axis=-1, keepdims=True))

    spec = pl.BlockSpec((tm, D_MODEL), lambda i: (i, 0))
    fix = pl.BlockSpec((1, D_MODEL), lambda i: (0, 0))
    return pl.pallas_call(
        body, grid=(t // tm,),
        in_specs=[spec, spec, fix, spec],
        out_specs=[pl.BlockSpec((1, LANES), lambda i: (0, 0)), spec, fix],
        out_shape=[jax.ShapeDtypeStruct((1, LANES), F32), jax.ShapeDtypeStruct((t, D_MODEL), F32),
                   jax.ShapeDtypeStruct((1, D_MODEL), F32)],
        compiler_params=_params(("arbitrary",)), name="final_loss")(x, out, wf, target)


def _group_lanes(v):
    return jnp.pad(v.reshape(SSD_GROUPS, 1, SSD_HPG), ((0, 0), (0, 0), (0, LANES - SSD_HPG)))


def _ungroup_lanes(v):
    return v[:, :, :SSD_HPG].reshape(1, N_DT)


def _row_slabs(pieces, rows):
    total = sum(p.shape[0] for p in pieces)
    assert total % rows == 0
    slabs = []
    for lo in range(0, total, rows):
        parts, off = [], 0
        for p in pieces:
            a, b = max(lo, off), min(lo + rows, off + p.shape[0])
            if a < b:
                parts.append(p[a - off:b - off])
            off += p.shape[0]
        slabs.append(parts[0] if len(parts) == 1 else jnp.concatenate(parts, axis=0))
    return jnp.stack(slabs)


def _local_step(x, target, nb, s, w_in_t, conv_w, conv_b, norm_w, dt_bias, a_log, d_skip,
                ssm_norm_w, out_weights, final_norm_w, between=None):
    t = nb * s
    w_g_t = w_in_t[GATE_COL0:]
    w_dt_t = jnp.pad(w_in_t[DT_COL0:GATE_COL0], ((0, LANES - N_DT), (0, 0)))
    h, r1 = _rms_fwd(x, norm_w)
    proj = _matmul(h, w_in_t, nt=True, n=DT_COL0, out_dtype=BF16, name="proj_main")
    proj_g = _matmul(h, w_g_t, nt=True, out_dtype=BF16, name="proj_gate")
    dt_raw = _matmul(h, w_dt_t, nt=True, name="proj_dt")
    dtr = jnp.pad(dt_raw[:, :N_DT].reshape(t, SSD_GROUPS, SSD_HPG).transpose(1, 0, 2),
                  ((0, 0), (0, 0), (0, LANES - SSD_HPG)))
    bias_g, alog_g, dsk_g = _group_lanes(dt_bias), _group_lanes(a_log), _group_lanes(d_skip)

    o, att_a, att_sg = _attn_fwd(proj, nb, s)
    xbc = _conv_fwd(proj, conv_w, conv_b, nb, s)
    y, hs = _ssd_fwd(xbc, dtr, bias_g, alog_g, dsk_g, nb, s)
    ya_in, ys_in = _post_fwd(o, proj, y, ssm_norm_w)
    w_attn_out, w_ssm_out, w_o = out_weights(ys_in) if callable(out_weights) else out_weights
    ya = _matmul(ya_in, w_attn_out, out_dtype=BF16, name="attn_out")
    ys = _matmul(ys_in, w_ssm_out, out_dtype=BF16, name="ssm_out")
    merged = _merge_fwd(ya, ys, proj_g)
    out = _matmul(merged, w_o, name="out_proj")
    loss, dx2, d_final_w = _final(x, out, final_norm_w, target)

    dx2b = _bf(dx2)
    dmerged = _matmul(dx2b, w_o, nt=True, out_dtype=BF16, name="d_merged")
    d_w_o = _matmul(merged, dx2b, ta=True, name="d_w_o")
    dya, dys, dgate = _merge_bwd(dmerged, ya, ys, proj_g)
    d_w_attn_out = _matmul(ya_in, dya, ta=True, name="d_w_attn_out")
    d_w_ssm_out = _matmul(ys_in, dys, ta=True, name="d_w_ssm_out")
    dya_in = _matmul(dya, w_attn_out, nt=True, out_dtype=BF16, name="d_ya_in")
    dys_in = _matmul(dys, w_ssm_out, nt=True, name="d_ys_in")
    do, dy, d_ssm_norm_w, dproj = _post_bwd(dya_in, o, proj, dys_in, y, ssm_norm_w, lax.empty(proj.shape, BF16))
    dq, dk, dv = _attn_bwd(proj, do, att_a, att_sg, nb, s)
    dxs_act, db_act, dc_act, ddtr, dbias_g, dalog_g, ddsk_g = _ssd_bwd(
        xbc, dtr, bias_g, alog_g, dsk_g, hs, dy, nb, s)
    dproj, dwx, dbx = _conv_bwd(proj, dxs_act, conv_w, conv_b, dproj, nb, s, col0=0, name="conv_bwd_x")
    dproj, dwb, dbb = _conv_bwd(proj, db_act, conv_w, conv_b, dproj, nb, s, col0=SSD_WIDTH, name="conv_bwd_b")
    dproj, dwc, dbc = _conv_bwd(proj, dc_act, conv_w, conv_b, dproj, nb, s, col0=SSD_WIDTH + 512,
                                name="conv_bwd_c")
    for i, piece in enumerate((dq, dk, dv)):
        dproj = lax.dynamic_update_slice(dproj, piece, (0, i * SB_WIDTH))
    ddt = jnp.pad(_bf(ddtr[:, :, :SSD_HPG].transpose(1, 0, 2).reshape(t, N_DT)), ((0, 0), (0, LANES - N_DT)))
    wt = dict(ta=True, out_dtype=BF16)
    d_w_in_t = _row_slabs(
        [_matmul(dproj, h, name="d_w_main", **wt), _matmul(ddt, h, name="d_w_dt", **wt)[:N_DT],
         _matmul(dgate, h, name="d_w_gate", **wt)], SHARD)
    grads = dict(
        w_in_t=d_w_in_t,
        conv_w=jnp.concatenate([dwx, dwb, dwc], axis=1), conv_b=jnp.concatenate([dbx, dbb, dbc], axis=1),
        dt_bias=_ungroup_lanes(dbias_g), a_log=_ungroup_lanes(dalog_g), d_skip=_ungroup_lanes(ddsk_g),
        ssm_norm_w=d_ssm_norm_w, w_attn_out=d_w_attn_out, w_ssm_out=d_w_ssm_out, w_o=d_w_o,
        final_norm_w=d_final_w)
    extra = None
    if between is not None:
        zero, extra = between(grads)
        ddt = ddt + _bf(zero[0:1, 0:1])

    dh = _matmul(ddt, w_dt_t, name="d_h_dt")
    dh = _matmul(dgate, w_g_t, add=dh, name="d_h_gate")
    dh = _matmul(dproj, w_in_t, add=dh, name="d_h", tk=DT_COL0 // 4)
    grad_x, grads["norm_w"] = _rms_bwd(dh, x, r1, norm_w, dx2)
    return loss, grad_x, grads, extra


N_CHIP = N_DEV // 2


def _pair_exchange(scatter, bcast, *, name):
    arrays = list(scatter) + list(bcast)
    ns, n = len(scatter), len(arrays)
    n_sem = ns * N_CHIP + (n - ns) * (N_DEV - 1)

    def body(*refs):
        ins, outs = refs[:n], refs[n:2 * n]
        send_sems, recv_sems, local_sems = refs[2 * n:]
        x, y, c = lax.axis_index("x"), lax.axis_index("y"), lax.axis_index("c")
        me = 4 * x + 2 * y + c
        copies, k = [], 0
        for a in range(ns):
            for q in range(N_CHIP):
                cp = pltpu.make_async_remote_copy(
                    src_ref=ins[a].at[2 * q + 1 - c], dst_ref=outs[a].at[q],
                    send_sem=send_sems.at[k], recv_sem=recv_sems.at[k],
                    device_id=(x, y, 1 - c), device_id_type=pl.DeviceIdType.MESH)
                cp.start()
                copies.append(cp)
                k += 1
        for a in range(ns, n):
            cp = pltpu.make_async_copy(ins[a], outs[a].at[me], local_sems.at[a - ns])
            cp.start()
            copies.append(cp)
            for r in range(1, N_DEV):
                tx = 1 - x if (r >> 2) & 1 else x
                ty = 1 - y if (r >> 1) & 1 else y
                tc = 1 - c if r & 1 else c
                cp = pltpu.make_async_remote_copy(
                    src_ref=ins[a], dst_ref=outs[a].at[me], send_sem=send_sems.at[k], recv_sem=recv_sems.at[k],
                    device_id=(tx, ty, tc), device_id_type=pl.DeviceIdType.MESH)
                cp.start()
                copies.append(cp)
                k += 1
        for cp in copies:
            cp.wait()

    out_shape = [jax.ShapeDtypeStruct((N_CHIP,) + v.shape[1:] if i < ns else (N_DEV,) + v.shape, v.dtype)
                 for i, v in enumerate(arrays)]
    return pl.pallas_call(
        body, in_specs=[pl.BlockSpec(memory_space=pl.ANY)] * n,
        out_specs=[pl.BlockSpec(memory_space=pl.ANY)] * n, out_shape=out_shape,
        scratch_shapes=[pltpu.SemaphoreType.DMA((n_sem,)), pltpu.SemaphoreType.DMA((n_sem,)),
                        pltpu.SemaphoreType.DMA((max(n - ns, 1),))],
        compiler_params=pltpu.CompilerParams(has_side_effects=True), name=name)(*arrays)


def _pair_add(mine, recv, core, *, name, tc=None):
    _, r, c = recv.shape
    tc = c if tc is None else tc
    assert c % tc == 0

    def body(core_ref, m_ref, r_ref, o_ref):
        o_ref[...] = (m_ref[...].astype(F32) + r_ref[...].astype(F32)).astype(o_ref.dtype)

    spec = pl.BlockSpec((None, r, tc), lambda q, i, core_ref: (q, 0, i))
    return pl.pallas_call(
        body, grid_spec=pltpu.PrefetchScalarGridSpec(
            num_scalar_prefetch=1, grid=(N_CHIP, c // tc),
            in_specs=[pl.BlockSpec((None, r, tc), lambda q, i, core_ref: (2 * q + core_ref[0], 0, i)), spec],
            out_specs=spec),
        out_shape=jax.ShapeDtypeStruct(recv.shape, recv.dtype),
        compiler_params=_params(("parallel", "parallel")), name=name)(core, mine, recv)


_HBM = pl.BlockSpec(memory_space=pltpu.HBM)
_SEM = pl.BlockSpec(memory_space=pltpu.SEMAPHORE)
_DATAFLOW = pltpu.SideEffectType.DATAFLOW_SIDE_EFFECTING
_PEERS = {"chips": N_CHIP - 1, "all": N_DEV - 1}


def _split_copies(pattern, srcs, lands, send_sems, recv_sems):
    x, y, c = lax.axis_index("x"), lax.axis_index("y"), lax.axis_index("c")
    per = _PEERS[pattern]
    copies = []
    for a in range(len(srcs)):
        for r in range(1, per + 1):
            if pattern == "chips":
                tx = 1 - x if (r >> 1) & 1 else x
                ty = 1 - y if r & 1 else y
                tc = c
                src, dst = srcs[a].at[2 * tx + ty], lands[a].at[2 * x + y]
            else:
                tx = 1 - x if (r >> 2) & 1 else x
                ty = 1 - y if (r >> 1) & 1 else y
                tc = 1 - c if r & 1 else c
                src, dst = srcs[a], lands[a].at[4 * x + 2 * y + c]
            copies.append(pltpu.make_async_remote_copy(
                src_ref=src, dst_ref=dst, send_sem=send_sems.at[a * per + r - 1],
                recv_sem=recv_sems.at[a * per + r - 1], device_id=(tx, ty, tc), device_id_type=pl.DeviceIdType.MESH))
    return copies


def _split_start(pattern, arrays, *, name):
    n = len(arrays)
    n_sem = n * _PEERS[pattern]

    def body(*refs):
        srcs, lands = refs[:n], refs[n:2 * n]
        send_sems, recv_sems = refs[2 * n], refs[2 * n + 1]
        zero = refs[-1]
        for cp in _split_copies(pattern, srcs, lands, send_sems, recv_sems):
            cp.start()
        zero[...] = jnp.zeros_like(zero)

    src_t = [pltpu.HBM(v.shape, v.dtype) for v in arrays]
    land_t = [pltpu.HBM(v.shape if pattern == "chips" else (N_DEV,) + v.shape, v.dtype) for v in arrays]
    outs = pl.pallas_call(
        body, name=name,
        out_shape=(pltpu.SemaphoreType.DMA((n_sem,)), pltpu.SemaphoreType.DMA((n_sem,)), *src_t, *land_t,
                   jax.ShapeDtypeStruct((8, LANES), F32)),
        in_specs=[_HBM] * (2 * n),
        out_specs=(_SEM, _SEM, *[_HBM] * (2 * n), pl.BlockSpec(memory_space=pltpu.VMEM)),
        input_output_aliases={i: 2 + i for i in range(2 * n)},
        compiler_params=pltpu.CompilerParams(has_side_effects=_DATAFLOW))(
            *[pltpu.with_memory_space_constraint(v, pltpu.HBM) for v in arrays],
            *[pltpu.with_memory_space_constraint(lax.empty(t.shape, t.dtype), pltpu.HBM) for t in land_t])
    return outs[0], outs[1], outs[2:2 + n], outs[2 + n:2 + 2 * n], outs[-1]


def _split_wait(pattern, send_sems, recv_sems, srcs, lands, after, *, name):
    n = len(srcs)

    def body(*refs):
        for cp in _split_copies(pattern, refs[:n], refs[n:2 * n], refs[2 * n], refs[2 * n + 1]):
            cp.wait_send()
            cp.wait_recv()

    outs = pl.pallas_call(
        body, name=name, out_shape=[pltpu.HBM(v.shape, v.dtype) for v in (*srcs, *lands)],
        in_specs=[_HBM] * (2 * n) + [_SEM, _SEM, pl.BlockSpec(memory_space=pl.ANY)],
        out_specs=[_HBM] * (2 * n), input_output_aliases={i: i for i in range(2 * n)},
        compiler_params=pltpu.CompilerParams(has_side_effects=_DATAFLOW))(
            *srcs, *lands, send_sems, recv_sems, after)
    return outs[:n], outs[n:]


def _own_slab(lands, srcs, index, whole):
    out = []
    for land, src in zip(lands, srcs):
        own = src[None] if whole else lax.dynamic_slice(src, (index, 0, 0), (1,) + src.shape[1:])
        out.append(lax.dynamic_update_slice(land, own, (index, 0, 0)))
    return out


def _gather(arrays, *, name):
    n = len(arrays)
    per = N_DEV - 1

    def body(*refs):
        ins, outs = refs[:n], refs[n:2 * n]
        send_sems, recv_sems, local_sems = refs[2 * n:]
        x, y, c = lax.axis_index("x"), lax.axis_index("y"), lax.axis_index("c")
        me, sibling = (x, y, c), (x, y, 1 - c)
        chips = [(1 - x, y), (x, 1 - y), (1 - x, 1 - y)]

        def slab(a, block):
            return outs[a].at[4 * block[0] + 2 * block[1] + block[2]]

        def copy(a, k, block, to, src=None):
            rows = slab(a, block)
            return pltpu.make_async_remote_copy(
                src_ref=rows if src is None else src, dst_ref=rows,
                send_sem=send_sems.at[a * per + k], recv_sem=recv_sems.at[a * per + k],
                device_id=to, device_id_type=pl.DeviceIdType.MESH)

        started = []
        for a in range(n):
            mine = pltpu.make_async_copy(ins[a], slab(a, me), local_sems.at[a])
            mine.start()
            started.append(mine)
        for a in range(n):
            first = [copy(a, 1 + j, me, (*chip, c), src=ins[a]) for j, chip in enumerate(chips)]
            first.append(copy(a, 0, me, sibling, src=ins[a]))
            for cp in first:
                cp.start()
            started += first
        for j, chip in enumerate(chips):
            for a in range(n):
                copy(a, 1 + j, (*chip, c), me).wait_recv()
                passed = copy(a, 4 + j, (*chip, c), sibling)
                passed.start()
                started.append(passed)
        for a in range(n):
            copy(a, 0, sibling, me).wait_recv()
            for j, chip in enumerate(chips):
                copy(a, 4 + j, (*chip, 1 - c), me).wait_recv()
        for cp in started[:n]:
            cp.wait()
        for cp in started[n:]:
            cp.wait_send()

    return pl.pallas_call(
        body, in_specs=[pl.BlockSpec(memory_space=pl.ANY)] * n,
        out_specs=[pl.BlockSpec(memory_space=pl.ANY)] * n,
        out_shape=[jax.ShapeDtypeStruct((N_DEV,) + v.shape, v.dtype) for v in arrays],
        scratch_shapes=[pltpu.SemaphoreType.DMA((n * per,)), pltpu.SemaphoreType.DMA((n * per,)),
                        pltpu.SemaphoreType.DMA((n,))],
        compiler_params=pltpu.CompilerParams(has_side_effects=True), name=name)(*arrays)


def _adamw(parts, w, m, v, *, name, tc=None):
    r, c = w.shape
    n_parts = parts.shape[0]
    tc = c if tc is None else tc
    assert c % tc == 0

    def body(p_ref, w_ref, m_ref, v_ref, g_ref, d_ref, nm_ref, nv_ref):
        g = p_ref[0].astype(F32)
        for d in range(1, n_parts):
            g = g + p_ref[d].astype(F32)
        mm = ADAM_B1 * m_ref[...] + (1.0 - ADAM_B1) * g
        vv = ADAM_B2 * v_ref[...] + (1.0 - ADAM_B2) * (g * g)
        m_hat = mm / (1.0 - ADAM_B1 ** ADAM_STEP)
        v_hat = vv / (1.0 - ADAM_B2 ** ADAM_STEP)
        g_ref[...] = g
        d_ref[...] = -ADAM_LR * (m_hat / (jnp.sqrt(v_hat) + ADAM_EPS) + ADAM_WD * w_ref[...])
        nm_ref[...] = mm
        nv_ref[...] = vv

    spec = pl.BlockSpec((r, tc), lambda i: (0, i))
    out = jax.ShapeDtypeStruct((r, c), F32)
    return pl.pallas_call(
        body, grid=(c // tc,),
        in_specs=[pl.BlockSpec((n_parts, r, tc), lambda i: (0, 0, i)), spec, spec, spec],
        out_specs=[spec] * 4, out_shape=[out] * 4,
        compiler_params=_params(("parallel",)), name=name)(parts, w, m, v)


SMALL = (("norm_w", D_MODEL), ("conv_b", CONV_DIM), ("dt_bias", N_DT), ("a_log", N_DT),
         ("d_skip", N_DT), ("ssm_norm_w", SSD_WIDTH), ("final_norm_w", D_MODEL))
SMALL_ROWS = 64
SHARD = D_PROJ // N_DEV


def _pack_small(vals):
    flat = jnp.concatenate([vals[k].reshape(-1).astype(F32) for k, _ in SMALL])
    return jnp.pad(flat, (0, SMALL_ROWS * LANES - flat.shape[0])).reshape(SMALL_ROWS, LANES)


def _unpack_small(packed, shapes):
    flat = packed.reshape(-1)
    out, off = {}, 0
    for k, size in SMALL:
        out[k] = flat[off:off + size].reshape(shapes[k])
        off += size
    return out


def kernel(x, norm_w, w_in, conv_w, conv_b, dt_bias, a_log, d_skip, ssm_norm_w, w_attn_out, w_ssm_out, w_o, final_norm_w, loss_target, m_norm_w, m_w_in, m_conv_w, m_conv_b, m_dt_bias, m_a_log, m_d_skip, m_ssm_norm_w, m_w_attn_out, m_w_ssm_out, m_w_o, m_final_norm_w, v_norm_w, v_w_in, v_conv_w, v_conv_b, v_dt_bias, v_a_log, v_d_skip, v_ssm_norm_w, v_w_attn_out, v_w_ssm_out, v_w_o, v_final_norm_w):
    nb, s, _ = x.shape
    t = nb * s
    weights = dict(norm_w=norm_w, w_in=w_in, conv_w=conv_w, conv_b=conv_b, dt_bias=dt_bias, a_log=a_log,
                   d_skip=d_skip, ssm_norm_w=ssm_norm_w, w_attn_out=w_attn_out, w_ssm_out=w_ssm_out,
                   w_o=w_o, final_norm_w=final_norm_w)
    moms = dict(norm_w=m_norm_w, w_in=m_w_in, conv_w=m_conv_w, conv_b=m_conv_b, dt_bias=m_dt_bias,
                a_log=m_a_log, d_skip=m_d_skip, ssm_norm_w=m_ssm_norm_w, w_attn_out=m_w_attn_out,
                w_ssm_out=m_w_ssm_out, w_o=m_w_o, final_norm_w=m_final_norm_w)
    vels = dict(norm_w=v_norm_w, w_in=v_w_in, conv_w=v_conv_w, conv_b=v_conv_b, dt_bias=v_dt_bias,
                a_log=v_a_log, d_skip=v_d_skip, ssm_norm_w=v_ssm_norm_w, w_attn_out=v_w_attn_out,
                w_ssm_out=v_w_ssm_out, w_o=v_w_o, final_norm_w=v_final_norm_w)

    tr_ = lambda a: jnp.transpose(a[0])
    g_in, g_conv = _gather([_bf(tr_(w_in)), conv_w[0]], name="gather_weights")
    conv_full = g_conv.transpose(1, 0, 2).reshape(CONV_K, CONV_DIM)
    o_send, o_recv, o_srcs, o_lands, o_zero = _split_start(
        "all", [_bf(w_attn_out[0]), _bf(w_ssm_out[0]), _bf(w_o[0])], name="gather_out_weights_start")

    core = lax.axis_index("c").astype(jnp.int32).reshape(1)
    chip = 2 * lax.axis_index("x") + lax.axis_index("y")
    me = 2 * chip + lax.axis_index("c")

    def out_weights(after):
        srcs, lands = _split_wait("all", o_send, o_recv, o_srcs, o_lands, after, name="gather_out_weights_wait")
        g_wa, g_ws, g_wo = _own_slab(lands, srcs, me, True)
        return (g_wa.reshape(SB_WIDTH, D_MODEL), g_ws.reshape(SSD_WIDTH, D_MODEL), g_wo.reshape(D_MODEL, D_MODEL))

    def send_weight_grads(grads):
        slabs = [grads["w_in_t"],
                 grads["conv_w"].reshape(CONV_K, N_DEV, CONV_DIM // N_DEV).transpose(1, 0, 2),
                 _bf(grads["w_attn_out"].reshape(N_DEV, SB_WIDTH // N_DEV, D_MODEL)),
                 _bf(grads["w_ssm_out"].reshape(N_DEV, SSD_WIDTH // N_DEV, D_MODEL)),
                 _bf(grads["w_o"].reshape(N_DEV, D_MODEL // N_DEV, D_MODEL))]
        recv = _pair_exchange(slabs, [], name="exchange_pairs")
        names = ("w_in", "conv_w", "w_attn_out", "w_ssm_out", "w_o")
        sums = [_pair_add(a, b, core, name="pair_add_" + k, tc=4 * LANES if k == "w_in" else None)
                for k, a, b in zip(names, slabs, recv)]
        send_sems, recv_sems, srcs, lands, zero = _split_start("chips", sums, name="exchange_chips_start")
        return zero, (send_sems, recv_sems, srcs, lands)

    loss_part, grad_x, grads, (send_sems, recv_sems, srcs, lands) = _local_step(
        x.reshape(t, D_MODEL), loss_target.reshape(t, D_MODEL), nb, s, g_in.reshape(D_PROJ, D_MODEL),
        conv_full, conv_b, norm_w + o_zero[0:1, 0:1], dt_bias, a_log, d_skip, ssm_norm_w, out_weights,
        final_norm_w.reshape(1, D_MODEL), between=send_weight_grads)

    (p_small,) = _pair_exchange([], [_pack_small(grads)], name="exchange_small")
    srcs, lands = _split_wait("chips", send_sems, recv_sems, srcs, lands, grad_x, name="exchange_chips_wait")
    p_in, p_conv, p_wa, p_ws, p_wo = _own_slab(lands, srcs, chip, False)

    res = {}
    res["w_in"] = [a.T for a in _adamw(p_in, tr_(w_in), tr_(m_w_in), tr_(v_w_in), name="adamw_w_in", tc=2 * LANES)]
    res["conv_w"] = _adamw(p_conv, conv_w[0], m_conv_w[0], v_conv_w[0], name="adamw_conv_w")
    res["w_attn_out"] = _adamw(p_wa, w_attn_out[0], m_w_attn_out[0], v_w_attn_out[0], name="adamw_w_attn_out")
    res["w_ssm_out"] = _adamw(p_ws, w_ssm_out[0], m_w_ssm_out[0], v_w_ssm_out[0], name="adamw_w_ssm_out")
    res["w_o"] = _adamw(p_wo, w_o[0], m_w_o[0], v_w_o[0], name="adamw_w_o")
    sm = _adamw(p_small, _pack_small(weights), _pack_small(moms), _pack_small(vels), name="adamw_small")
    shapes = {k: weights[k].shape for k, _ in SMALL}
    sm = [_unpack_small(v, shapes) for v in sm]
    for k, _ in SMALL:
        res[k] = tuple(part[k] for part in sm)

    order = ("norm_w", "w_in", "conv_w", "conv_b", "dt_bias", "a_log", "d_skip", "ssm_norm_w",
             "w_attn_out", "w_ssm_out", "w_o", "final_norm_w")
    loss = lax.psum(loss_part[0, 0], ("x", "y", "c"))
    outs = [loss, grad_x.reshape(nb, s, D_MODEL)]
    for i in range(4):
        outs += [res[k][i].reshape(weights[k].shape) for k in order]
    return tuple(outs)
```

```python
import functools

import jax
import jax.numpy as jnp
from jax import lax
from jax.experimental import pallas as pl
from jax.experimental.pallas import tpu as pltpu

F32 = jnp.float32
BF16 = jnp.bfloat16

D_MODEL = 1024
SB_WIDTH = 1024
HEAD_DIM = 64
SSD_WIDTH = 2048
SSD_GROUPS = 4
SSD_HPG = 8
SSD_STATE = 128
SSD_CHUNK = 128
CONV_K = 4
CONV_DIM = 3072
N_DT = 32
D_PROJ = 11296
DT_COL0 = 9216
GATE_COL0 = DT_COL0 + N_DT
EPS = 1e-6
N_DEV = 8

ADAM_LR = 0.001
ADAM_B1 = 0.9
ADAM_B2 = 0.999
ADAM_EPS = 1e-08
ADAM_WD = 0.01
ADAM_STEP = 10

LANES = 128
VMEM_LIMIT = 56 * 1024 * 1024

CB_Q, CB_K, CB_V, CB_ZA, CB_ZS, CB_XBC = 0, 8, 16, 24, 32, 48

NT_DIMS = (((1,), (1,)), ((), ()))
TN_DIMS = (((0,), (0,)), ((), ()))


def _params(sem):
    return pltpu.CompilerParams(dimension_semantics=sem, vmem_limit_bytes=VMEM_LIMIT)


def _dot(a, b):
    return jnp.dot(a, b, preferred_element_type=F32)


def _dot_nt(a, b):
    return lax.dot_general(a, b, NT_DIMS, preferred_element_type=F32)


def _dot_tn(a, b):
    return lax.dot_general(a, b, TN_DIMS, preferred_element_type=F32)


def _bf(a):
    return a.astype(BF16)


def _sigmoid(x):
    return 0.5 + 0.5 * jnp.tanh(0.5 * x)


def _softplus(x):
    return jnp.maximum(x, 0.0) + jnp.log(1.0 + jnp.exp(-jnp.abs(x)))


def _trunc_split(a):
    bits = lax.bitcast_convert_type(a, jnp.uint32) & jnp.uint32(0xFFFF0000)
    hi = lax.bitcast_convert_type(bits, F32)
    return _bf(hi), _bf(a - hi)


MM_TILE = 1024
MM_TK = 2048


def _matmul(a, b, *, name, nt=False, ta=False, add=None, out_dtype=F32, tm=MM_TILE, tn=MM_TILE, tk=MM_TK, n=None):
    k, m = a.shape if ta else a.shape[::-1]
    n = (b.shape[0] if nt else b.shape[1]) if n is None else n
    tm, tn, tk = min(tm, m), min(tn, n), min(tk, k)
    assert m % tm == 0 and n % tn == 0 and k % tk == 0, (name, a.shape, b.shape)
    assert not (ta and nt)
    nk = k // tk
    has_add = add is not None

    def body(*refs):
        if has_add:
            a_ref, b_ref, add_ref, o_ref, acc_ref = refs
        else:
            a_ref, b_ref, o_ref, acc_ref = refs
        kk = pl.program_id(2)
        av, bv = _bf(a_ref[...]), _bf(b_ref[...])
        d = _dot_nt(av, bv) if nt else _dot_tn(av, bv) if ta else _dot(av, bv)
        if nk == 1:
            o_ref[...] = ((d + add_ref[...]) if has_add else d).astype(out_dtype)
            return

        @pl.when(kk == 0)
        def _():
            acc_ref[...] = jnp.zeros_like(acc_ref)

        acc_ref[...] += d

        @pl.when(kk == nk - 1)
        def _():
            r = acc_ref[...]
            if has_add:
                r = r + add_ref[...]
            o_ref[...] = r.astype(out_dtype)

    in_specs = [pl.BlockSpec((tk, tm), lambda j, i, kk: (kk, i)) if ta
                else pl.BlockSpec((tm, tk), lambda j, i, kk: (i, kk)),
                pl.BlockSpec((tn, tk), lambda j, i, kk: (j, kk)) if nt
                else pl.BlockSpec((tk, tn), lambda j, i, kk: (kk, j))]
    args = [a, b]
    if has_add:
        in_specs.append(pl.BlockSpec((tm, tn), lambda j, i, kk: (i, j)))
        args.append(add)
    return pl.pallas_call(
        body, grid=(n // tn, m // tm, nk), in_specs=in_specs,
        out_specs=pl.BlockSpec((tm, tn), lambda j, i, kk: (i, j)),
        out_shape=jax.ShapeDtypeStruct((m, n), out_dtype),
        scratch_shapes=[pltpu.VMEM((tm, tn), F32)],
        compiler_params=_params(("parallel", "parallel", "arbitrary")), name=name)(*args)


def _rms_fwd(x, w, *, tm=512):
    t = x.shape[0]

    def body(x_ref, w_ref, h_ref, r_ref):
        xv = x_ref[...]
        r = lax.rsqrt(jnp.mean(xv * xv, axis=-1, keepdims=True) + EPS)
        h_ref[...] = _bf(xv * r * w_ref[...])
        r_ref[...] = r

    row = lambda i: (i, 0)
    return pl.pallas_call(
        body, grid=(t // tm,),
        in_specs=[pl.BlockSpec((tm, D_MODEL), row), pl.BlockSpec((1, D_MODEL), lambda i: (0, 0))],
        out_specs=[pl.BlockSpec((tm, D_MODEL), row), pl.BlockSpec((tm, 1), row)],
        out_shape=[jax.ShapeDtypeStruct((t, D_MODEL), BF16), jax.ShapeDtypeStruct((t, 1), F32)],
        compiler_params=_params(("parallel",)), name="rms_fwd")(x, w)


def _rms_bwd(dh, x, r, w, dres, *, tm=512):
    t = x.shape[0]

    def body(dh_ref, x_ref, r_ref, w_ref, dres_ref, dx_ref, dw_ref):
        @pl.when(pl.program_id(0) == 0)
        def _():
            dw_ref[...] = jnp.zeros_like(dw_ref)

        nrm = x_ref[...] * r_ref[...]
        dhv = dh_ref[...]
        dw_ref[...] += jnp.sum(dhv * nrm, axis=0, keepdims=True)
        dn = dhv * w_ref[...]
        dx = r_ref[...] * (dn - nrm * jnp.mean(dn * nrm, axis=-1, keepdims=True))
        dx_ref[...] = dx + dres_ref[...]

    row = lambda i: (i, 0)
    fix = lambda i: (0, 0)
    return pl.pallas_call(
        body, grid=(t // tm,),
        in_specs=[pl.BlockSpec((tm, D_MODEL), row), pl.BlockSpec((tm, D_MODEL), row),
                  pl.BlockSpec((tm, 1), row), pl.BlockSpec((1, D_MODEL), fix),
                  pl.BlockSpec((tm, D_MODEL), row)],
        out_specs=[pl.BlockSpec((tm, D_MODEL), row), pl.BlockSpec((1, D_MODEL), fix)],
        out_shape=[jax.ShapeDtypeStruct((t, D_MODEL), F32), jax.ShapeDtypeStruct((1, D_MODEL), F32)],
        compiler_params=_params(("arbitrary",)), name="rms_bwd")(dh, x, r, w, dres)


ATT_BLK = 256


def _attn_masks(blk):
    r_io = lax.broadcasted_iota(jnp.int32, (blk, blk), 0)
    c_io = lax.broadcasted_iota(jnp.int32, (blk, blk), 1)
    return r_io, c_io


def _attn_tiles(nq):
    return nq * (nq + 1) // 2


def _attn_fwd(proj, nb, s):
    blk = min(ATT_BLK, s)
    nq = s // blk
    nt = _attn_tiles(nq)
    scale = HEAD_DIM ** -0.5

    def body(q_ref, k_ref, v_ref, o_ref, a_out, sg_out, q0_s, q1_s, kb_s, v0_s, v1_s, z_s, a_s, acc_s, cr_s):
        head0 = lax.broadcasted_iota(jnp.int32, (s, LANES), 1) < HEAD_DIM
        r_io, c_io = _attn_masks(blk)
        tri = c_io < r_io
        tri2 = jnp.concatenate([tri, tri], axis=0)
        u_gt = _bf(r_io > c_io)
        u2 = jnp.concatenate([u_gt, u_gt], axis=0)
        qv = q_ref[...] * scale
        q0_s[...] = _bf(jnp.where(head0, qv, 0.0))
        q1_s[...] = _bf(jnp.where(head0, 0.0, qv))
        kb_s[...] = _bf(k_ref[...])
        vv = v_ref[...]
        v0_s[...] = _bf(jnp.where(head0, vv, 0.0))
        v1_s[...] = _bf(jnp.where(head0, 0.0, vv))

        def scores(qcat, kblk, slot):
            k0 = pl.multiple_of(jnp.maximum(kblk, 0) * blk, blk)
            z_s[slot] = _dot_nt(qcat, kb_s[pl.ds(k0, blk), :])

        def weights(slot, tile, diag, zslot=None):
            z = z_s[slot if zslot is None else zslot]
            cr = cr_s[...]
            sp = _softplus(z)
            spm = jnp.where(tri2, sp, 0.0) if diag else sp
            hi, lo = _trunc_split(spm)
            ps = _dot(jnp.concatenate([hi, lo], axis=1), u2)
            lb = z - sp
            a = jnp.exp(lb - (ps + cr))
            if diag:
                a = jnp.where(tri2, a, 0.0)
            ab = _bf(a)
            a_s[slot] = ab
            a_out[tile] = ab
            sg_out[tile] = _bf(jnp.exp(lb))
            cr_s[...] = cr + ps[:, 0:1] + spm[:, 0:1]

        def apply(slot, k0):
            a = a_s[slot]
            vcat = jnp.concatenate([v0_s[pl.ds(k0, blk), :], v1_s[pl.ds(k0, blk), :]], axis=0)
            acc_s[...] += _dot(jnp.concatenate([a[:blk], a[blk:]], axis=1), vcat)

        def queries(qi):
            q0 = pl.multiple_of(qi * blk, blk)
            return jnp.concatenate([q0_s[pl.ds(q0, blk), :], q1_s[pl.ds(q0, blk), :]], axis=0)

        def first_scores(qi):
            qcat = queries(qi)
            scores(qcat, qi, 2)
            scores(qcat, qi - 1, 3)

        first_scores(jnp.int32(0))

        def qblock(qi, _):
            q0 = pl.multiple_of(qi * blk, blk)
            tile0 = qi * (qi + 1) // 2
            qcat = queries(qi)
            z_s[1] = z_s[3]
            cr_s[...] = jnp.zeros_like(cr_s)
            acc_s[...] = jnp.zeros_like(acc_s)
            weights(0, tile0 + qi, True, zslot=2)
            first_scores(jnp.minimum(qi + 1, nq - 1))

            def step(j, slot):
                scores(qcat, qi - j - 1, 1 - slot)
                weights(slot, tile0 + qi - j, False)
                apply(1 - slot, pl.multiple_of((qi - j + 1) * blk, blk))

            def pair(i, _):
                step(2 * i + 1, 1)
                step(2 * i + 2, 0)
                return 0

            lax.fori_loop(0, qi // 2, pair, 0)

            @pl.when(qi % 2 == 1)
            def _():
                step(qi, 1)

            apply(qi & 1, 0)
            o_ref[pl.ds(q0, blk), :] = acc_s[...]
            return 0

        lax.fori_loop(0, nq, qblock, 0)

    t = nb * s
    tiles = pl.BlockSpec((None, None, nt, 2 * blk, blk), lambda b, p: (b, p, 0, 0, 0))
    tiles_shape = jax.ShapeDtypeStruct((nb, 8, nt, 2 * blk, blk), BF16)
    return pl.pallas_call(
        body, grid=(nb, 8),
        in_specs=[pl.BlockSpec((s, LANES), lambda b, p: (b, CB_Q + p)),
                  pl.BlockSpec((s, LANES), lambda b, p: (b, CB_K + p)),
                  pl.BlockSpec((s, LANES), lambda b, p: (b, CB_V + p))],
        out_specs=[pl.BlockSpec((s, LANES), lambda b, p: (b, p)), tiles, tiles],
        out_shape=[jax.ShapeDtypeStruct((t, SB_WIDTH), F32), tiles_shape, tiles_shape],
        scratch_shapes=[pltpu.VMEM((s, LANES), BF16)] * 5 + [
            pltpu.VMEM((4, 2 * blk, blk), F32), pltpu.VMEM((2, 2 * blk, blk), BF16),
            pltpu.VMEM((blk, LANES), F32), pltpu.VMEM((2 * blk, 1), F32)],
        compiler_params=_params(("parallel", "parallel")), name="attn_fwd")(proj, proj, proj)


def _attn_bwd(proj, do, a_all, sg_all, nb, s):
    blk = min(ATT_BLK, s)
    nq = s // blk
    nt = _attn_tiles(nq)
    scale = HEAD_DIM ** -0.5

    def body(q_ref, k_ref, v_ref, do_ref, a_in, sg_in, dq_ref, dk_ref, dv_ref,
             qt0_s, qt1_s, k0_s, k1_s, vb_s, d0_s, d1_s, dt0_s, dt1_s, da_s, dz_s, dkt_acc, dvt_acc, dq_acc, cw_s):
        head0 = lax.broadcasted_iota(jnp.int32, (s, LANES), 1) < HEAD_DIM
        r_io, c_io = _attn_masks(blk)
        tri = c_io < r_io
        tri2 = jnp.concatenate([tri, tri], axis=0)
        u_lt = _bf(r_io < c_io)
        qv = q_ref[...].astype(F32) * scale
        qt0_s[...] = _bf(jnp.where(head0, qv, 0.0).T)
        qt1_s[...] = _bf(jnp.where(head0, 0.0, qv).T)
        kv = k_ref[...]
        k0_s[...] = _bf(jnp.where(head0, kv, 0.0))
        k1_s[...] = _bf(jnp.where(head0, 0.0, kv))
        vb_s[...] = _bf(v_ref[...])
        dov = do_ref[...].astype(F32)
        d0 = jnp.where(head0, dov, 0.0)
        d1 = jnp.where(head0, 0.0, dov)
        d0_s[...] = _bf(d0)
        d1_s[...] = _bf(d1)
        dt0_s[...] = _bf(d0.T)
        dt1_s[...] = _bf(d1.T)
        dkt_acc[...] = jnp.zeros_like(dkt_acc)
        dvt_acc[...] = jnp.zeros_like(dvt_acc)

        def qblock(qi, _):
            q0 = pl.multiple_of(qi * blk, blk)
            tile0 = qi * (qi + 1) // 2
            qs_ = pl.ds(q0, blk)
            dcat = jnp.concatenate([d0_s[qs_, :], d1_s[qs_, :]], axis=0)
            qcat_t = jnp.concatenate([qt0_s[:, qs_], qt1_s[:, qs_]], axis=1)
            dcat_t = jnp.concatenate([dt0_s[:, qs_], dt1_s[:, qs_]], axis=1)

            def scores(kblk, slot):
                da_s[slot] = _dot_nt(dcat, vb_s[pl.ds(pl.multiple_of(kblk * blk, blk), blk), :])

            def grads(slot, kblk, diag):
                cw = cw_s[...]
                w = a_in[tile0 + kblk].astype(F32) * da_s[slot]
                wex = _dot(_bf(w), u_lt)
                dz = w - sg_in[tile0 + kblk].astype(F32) * (w + (wex + cw))
                if diag:
                    dz = jnp.where(tri2, dz, 0.0)
                dz_s[slot] = _bf(dz)
                cw_s[...] = cw + wex[:, blk - 1:blk] + w[:, blk - 1:blk]

            def apply(slot, kblk):
                ks_ = pl.ds(pl.multiple_of(kblk * blk, blk), blk)
                dz = dz_s[slot]
                dq_acc[...] += _dot(jnp.concatenate([dz[:blk], dz[blk:]], axis=1),
                                    jnp.concatenate([k0_s[ks_, :], k1_s[ks_, :]], axis=0))
                dkt_acc[:, ks_] += _dot(qcat_t, dz)
                dvt_acc[:, ks_] += _dot(dcat_t, a_in[tile0 + kblk])

            def step(j, slot, diag):
                if not diag:
                    scores(j + 1, 1 - slot)
                grads(slot, j, diag)

                @pl.when(j > 0)
                def _():
                    apply(1 - slot, j - 1)

            scores(0, 0)
            dq_acc[...] = jnp.zeros_like(dq_acc)
            cw_s[...] = jnp.zeros_like(cw_s)

            def pair(i, _):
                step(2 * i, 0, False)
                step(2 * i + 1, 1, False)
                return 0

            lax.fori_loop(0, qi // 2, pair, 0)

            @pl.when(qi % 2 == 1)
            def _():
                step(qi - 1, 0, False)
                step(qi, 1, True)

            @pl.when(qi % 2 == 0)
            def _():
                step(qi, 0, True)

            apply(qi & 1, qi)
            dq_ref[qs_, :] = _bf(dq_acc[...] * scale)
            return 0

        lax.fori_loop(0, nq, qblock, 0)
        dk_ref[...] = _bf(dkt_acc[...].T)
        dv_ref[...] = _bf(dvt_acc[...].T)

    t = nb * s
    hp = lambda b, p: (b, p)
    out = jax.ShapeDtypeStruct((t, SB_WIDTH), BF16)
    tiles = pl.BlockSpec((None, None, nt, 2 * blk, blk), lambda b, p: (b, p, 0, 0, 0))
    return pl.pallas_call(
        body, grid=(nb, 8),
        in_specs=[pl.BlockSpec((s, LANES), lambda b, p: (b, CB_Q + p)),
                  pl.BlockSpec((s, LANES), lambda b, p: (b, CB_K + p)),
                  pl.BlockSpec((s, LANES), lambda b, p: (b, CB_V + p)),
                  pl.BlockSpec((s, LANES), hp), tiles, tiles],
        out_specs=[pl.BlockSpec((s, LANES), hp)] * 3,
        out_shape=[out, out, out],
        scratch_shapes=[pltpu.VMEM((LANES, s), BF16)] * 2 + [pltpu.VMEM((s, LANES), BF16)] * 5 + [
            pltpu.VMEM((LANES, s), BF16)] * 2 + [
            pltpu.VMEM((2, 2 * blk, blk), F32), pltpu.VMEM((2, 2 * blk, blk), BF16),
            pltpu.VMEM((LANES, s), F32), pltpu.VMEM((LANES, s), F32),
            pltpu.VMEM((blk, LANES), F32), pltpu.VMEM((2 * blk, 1), F32)],
        compiler_params=_params(("parallel", "parallel")), name="attn_bwd")(proj, proj, proj, do, a_all, sg_all)


CONV_TC = 256
CONV_ROWS = 64
HALO = 8


def _conv_rows(ws, w_ref, b_ref, n):
    taps = [ws[HALO - kk:HALO - kk + n, :] for kk in range(CONV_K)]
    y = b_ref[...] + w_ref[CONV_K - 1:CONV_K, :] * taps[0]
    for kk in range(1, CONV_K):
        y = y + w_ref[CONV_K - 1 - kk:CONV_K - kk, :] * taps[kk]
    return y, taps


def _conv_fwd(proj, conv_w, conv_b, nb, s):
    tc, rows = CONV_TC, min(CONV_ROWS, s)
    cb0 = CB_XBC * LANES // tc

    def body(x_ref, w_ref, b_ref, o_ref, xp, ws):
        xp[0:HALO, :] = jnp.zeros((HALO, tc), F32)
        xp[HALO:HALO + s, :] = x_ref[...].astype(F32)

        def chunk(i, _):
            r0 = pl.multiple_of(i * rows, rows)
            ws[...] = xp[pl.ds(r0, rows + HALO), :]
            y, _ = _conv_rows(ws, w_ref, b_ref, rows)
            o_ref[pl.ds(r0, rows), :] = y * _sigmoid(y)
            return 0

        lax.fori_loop(0, s // rows, chunk, 0)

    t = nb * s
    return pl.pallas_call(
        body, grid=(nb, CONV_DIM // tc),
        in_specs=[pl.BlockSpec((s, tc), lambda b, c: (b, cb0 + c)),
                  pl.BlockSpec((CONV_K, tc), lambda b, c: (0, c)),
                  pl.BlockSpec((1, tc), lambda b, c: (0, c))],
        out_specs=pl.BlockSpec((s, tc), lambda b, c: (b, c)),
        out_shape=jax.ShapeDtypeStruct((t, CONV_DIM), F32),
        scratch_shapes=[pltpu.VMEM((s + HALO, tc), F32), pltpu.VMEM((rows + HALO, tc), F32)],
        compiler_params=_params(("parallel", "parallel")), name="conv_fwd")(proj, conv_w, conv_b)


def _conv_bwd(proj, dact, conv_w, conv_b, dproj, nb, s, *, col0, name):
    width = dact.shape[1]
    tc, rows = min(CONV_TC, width), min(CONV_ROWS, s)
    cb0 = (CB_XBC * LANES + col0) // tc
    wb0 = col0 // tc

    def body(x_ref, d_ref, w_ref, b_ref, _, dx_ref, dw_ref, db_ref, xp, dp, dys, ws):
        @pl.when(pl.program_id(1) == 0)
        def _():
            dw_ref[...] = jnp.zeros_like(dw_ref)
            db_ref[...] = jnp.zeros_like(db_ref)

        n = rows + HALO
        pad = jnp.zeros((HALO, tc), F32)
        xp[0:HALO, :] = pad
        xp[HALO:HALO + s, :] = x_ref[...].astype(F32)
        xp[HALO + s:HALO + s + HALO, :] = pad
        dp[0:s, :] = d_ref[...]
        dp[s:s + HALO, :] = pad

        def chunk(i, acc):
            r0 = pl.multiple_of(i * rows, rows)
            ws[...] = xp[pl.ds(r0, n + HALO), :]
            y, taps = _conv_rows(ws, w_ref, b_ref, n)
            sg = _sigmoid(y)
            dy = dp[pl.ds(r0, n), :] * (sg * (1.0 + y * (1.0 - sg)))
            dys[...] = dy
            dx = w_ref[CONV_K - 1:CONV_K, :] * dy[0:rows]
            for kk in range(1, CONV_K):
                dx = dx + w_ref[CONV_K - 1 - kk:CONV_K - kk, :] * dys[kk:kk + rows, :]
            dx_ref[pl.ds(r0, rows), :] = _bf(dx)
            dyc = dy[0:rows]
            new = [acc[0] + jnp.sum(dyc, axis=0, keepdims=True)]
            for kk in range(CONV_K):
                new.append(acc[1 + kk] + jnp.sum(dyc * taps[kk][0:rows], axis=0, keepdims=True))
            return tuple(new)

        acc = lax.fori_loop(0, s // rows, chunk, (jnp.zeros((1, tc), F32),) * (1 + CONV_K))
        db_ref[...] += acc[0]
        for kk in range(CONV_K):
            dw_ref[CONV_K - 1 - kk:CONV_K - kk, :] += acc[1 + kk]

    t = nb * s
    return pl.pallas_call(
        body, grid=(width // tc, nb),
        in_specs=[pl.BlockSpec((s, tc), lambda c, b: (b, cb0 + c)),
                  pl.BlockSpec((s, tc), lambda c, b: (b, c)),
                  pl.BlockSpec((CONV_K, tc), lambda c, b: (0, wb0 + c)),
                  pl.BlockSpec((1, tc), lambda c, b: (0, wb0 + c)), pl.BlockSpec(memory_space=pl.ANY)],
        out_specs=[pl.BlockSpec((s, tc), lambda c, b: (b, cb0 + c)),
                   pl.BlockSpec((CONV_K, tc), lambda c, b: (0, c)),
                   pl.BlockSpec((1, tc), lambda c, b: (0, c))],
        out_shape=[jax.ShapeDtypeStruct(dproj.shape, dproj.dtype),
                   jax.ShapeDtypeStruct((CONV_K, width), F32),
                   jax.ShapeDtypeStruct((1, width), F32)],
        input_output_aliases={4: 0},
        scratch_shapes=[pltpu.VMEM((s + 2 * HALO, tc), F32), pltpu.VMEM((s + HALO, tc), F32),
                        pltpu.VMEM((rows + HALO, tc), F32), pltpu.VMEM((rows + 2 * HALO, tc), F32)],
        compiler_params=_params(("parallel", "arbitrary")), name=name)(proj, dact, conv_w, conv_b, dproj)


L = SSD_CHUNK
N_PAIR = SSD_HPG // 2


def _pair_lanes(v0, v1, lane):
    return jnp.where(lane < HEAD_DIM, v0, v1)


def _onehot(shape, row_shift, col_shift):
    row = lax.broadcasted_iota(jnp.int32, shape, 0)
    col = lax.broadcasted_iota(jnp.int32, shape, 1)
    return _bf(lax.shift_right_logical(row, row_shift) == lax.shift_right_logical(col, col_shift))


def _ssd_tables():
    sp = _onehot((LANES, N_PAIR * LANES), 0, 6)
    sf = _onehot((LANES, SSD_HPG * LANES), 0, 7)
    return (jnp.concatenate([sp] * 3, axis=0), jnp.concatenate([sf] * 3, axis=0),
            _onehot((N_PAIR * LANES, LANES), 6, 0), _onehot((SSD_HPG * LANES, LANES), 7, 0))


def _split3(a):
    a1 = _bf(a)
    r1 = a - a1.astype(F32)
    a2 = _bf(r1)
    return jnp.concatenate([a1, a2, _bf(r1 - a2.astype(F32))], axis=1)


def _split2(a):
    hi, lo = _trunc_split(a)
    return jnp.concatenate([hi, lo], axis=1)


def _tri_dot(tri_bf, a):
    a1 = _bf(a)
    r1 = a - a1.astype(F32)
    a2 = _bf(r1)
    parts = jnp.concatenate([a1, a2, _bf(r1 - a2.astype(F32))], axis=0)
    return _dot(jnp.concatenate([tri_bf] * 3, axis=1), parts)


HEAD_W = 4 * LANES + 2 * N_PAIR * LANES + SSD_HPG * LANES


def _ssd_head(dtr_ref, bias_ref, a_neg, c, tri_bf, spread_pair, spread_full, head_s, slot):
    raw = dtr_ref[pl.ds(pl.multiple_of(c * L, L), L), :] + bias_ref[...]
    dt = _softplus(raw)
    acs = _tri_dot(tri_bf, dt * a_neg)
    acs3 = _split3(acs)
    both = _dot(jnp.concatenate([_split3(dt), acs3], axis=0), spread_pair)
    parts = (raw, dt, acs, acs.T, both[:L], both[L:], _dot(acs3, spread_full))
    off = 0
    for p in parts:
        head_s[slot, :, off:off + p.shape[1]] = p
        off += p.shape[1]


def _ssd_head_load(head_s, slot):
    out, off = [], 0
    for w in (LANES, LANES, LANES, L, N_PAIR * LANES, N_PAIR * LANES, SSD_HPG * LANES):
        out.append(head_s[slot, :, off:off + w])
        off += w
    return out


def _ssd_fwd(xbc, dtr, bias, alog, dskip, nb, s):
    nc = s // L
    t = nb * s

    def body(x_ref, b_ref, c_ref, dtr_ref, bias_ref, alog_ref, dsk_ref, y_ref, hs_ref, h_scr, head_s):
        lane = lax.broadcasted_iota(jnp.int32, (L, LANES), 1)
        lane1 = lax.broadcasted_iota(jnp.int32, (1, LANES), 1)
        rowc = lax.broadcasted_iota(jnp.int32, (L, 1), 0)
        rr = lax.broadcasted_iota(jnp.int32, (L, L), 0)
        cc = lax.broadcasted_iota(jnp.int32, (L, L), 1)
        causal = rr >= cc
        h_scr[...] = jnp.zeros_like(h_scr)
        dsk = dsk_ref[...]
        spread_pair, spread_full, _, _ = _ssd_tables()
        a_neg = -jnp.exp(alog_ref[...])
        head = functools.partial(_ssd_head, dtr_ref, bias_ref, a_neg, tri_bf=_bf(cc <= rr),
                                 spread_pair=spread_pair, spread_full=spread_full, head_s=head_s)
        head(jnp.int32(0), slot=0)

        def chunk(c, slot):
            r0 = pl.multiple_of(c * L, L)
            head(jnp.minimum(c + 1, nc - 1), slot=1 - slot)
            _, _, acs, acs_t, dt_all, acs_all, acs_full = _ssd_head_load(head_s, slot)
            bb = _bf(b_ref[pl.ds(r0, L), :])
            cb = _bf(c_ref[pl.ds(r0, L), :])
            gm = _dot_nt(cb, bb)
            for pr in range(N_PAIR):
                j0, j1 = 2 * pr, 2 * pr + 1
                ms = []
                for j in (j0, j1):
                    seg = acs_full[:, j * LANES:(j + 1) * LANES] - acs_t[j:j + 1, :]
                    ms.append(_bf(gm * jnp.exp(jnp.where(causal, seg, -1e30))))
                x = x_ref[pl.ds(r0, L), pr * LANES:(pr + 1) * LANES]
                dt_p = dt_all[:, pr * LANES:(pr + 1) * LANES]
                acs_p = acs_all[:, pr * LANES:(pr + 1) * LANES]
                last_p = acs_p[L - 1:L, :]
                d_p = _pair_lanes(dsk[:, j0:j0 + 1], dsk[:, j1:j1 + 1], lane1)
                xd = x * dt_p
                xdb = _bf(xd)
                yds = _dot(jnp.concatenate(ms, axis=0), xdb)
                yd = _pair_lanes(yds[:L], yds[L:], lane)
                hp = h_scr[pr]
                yo = _dot_nt(cb, _bf(hp)) * jnp.exp(acs_p)
                y_ref[pl.ds(r0, L), pr * LANES:(pr + 1) * LANES] = yd + yo + x * d_p
                hs_ref[c, pr] = hp
                sp = _dot_tn(_bf(xd * jnp.exp(last_p - acs_p)), bb)
                cd = jnp.where(rowc < HEAD_DIM, jnp.exp(acs[L - 1:L, j0:j0 + 1]),
                               jnp.exp(acs[L - 1:L, j1:j1 + 1]))
                h_scr[pr] = hp * cd + sp

        def two_chunks(i, _):
            chunk(2 * i, 0)
            chunk(2 * i + 1, 1)
            return 0

        lax.fori_loop(0, nc // 2, two_chunks, 0)

    assert nc % 2 == 0
    grp = lambda b, g: (g, 0, 0)
    return pl.pallas_call(
        body, grid=(nb, SSD_GROUPS),
        in_specs=[pl.BlockSpec((s, 4 * LANES), lambda b, g: (b, g)),
                  pl.BlockSpec((s, LANES), lambda b, g: (b, 16 + g)),
                  pl.BlockSpec((s, LANES), lambda b, g: (b, 20 + g)),
                  pl.BlockSpec((None, s, LANES), lambda b, g: (g, b, 0)),
                  pl.BlockSpec((None, 1, LANES), grp), pl.BlockSpec((None, 1, LANES), grp),
                  pl.BlockSpec((None, 1, LANES), grp)],
        out_specs=[pl.BlockSpec((s, 4 * LANES), lambda b, g: (b, g)),
                   pl.BlockSpec((None, None, nc, N_PAIR, L, LANES), lambda b, g: (b, g, 0, 0, 0, 0))],
        out_shape=[jax.ShapeDtypeStruct((t, SSD_WIDTH), F32),
                   jax.ShapeDtypeStruct((nb, SSD_GROUPS, nc, N_PAIR, L, LANES), F32)],
        scratch_shapes=[pltpu.VMEM((N_PAIR, L, LANES), F32), pltpu.VMEM((2, L, HEAD_W), F32)],
        compiler_params=_params(("parallel", "parallel")), name="ssd_fwd")(
            xbc, xbc, xbc, dtr, bias, alog, dskip)


def _ssd_bwd(xbc, dtr, bias, alog, dskip, hs, dy, nb, s):
    nc = s // L
    t = nb * s

    def body(x_ref, b_ref, c_ref, dtr_ref, bias_ref, alog_ref, dsk_ref, hs_ref, dy_ref,
             dx_ref, db_ref, dc_ref, ddtr_ref, dbias_ref, dalog_ref, ddsk_ref, dh_scr, head_s):
        lane = lax.broadcasted_iota(jnp.int32, (L, LANES), 1)
        lane1 = lax.broadcasted_iota(jnp.int32, (1, LANES), 1)
        rowc = lax.broadcasted_iota(jnp.int32, (L, 1), 0)
        rr = lax.broadcasted_iota(jnp.int32, (L, L), 0)
        cc = lax.broadcasted_iota(jnp.int32, (L, L), 1)
        causal = rr >= cc
        tri_t = _bf(cc >= rr)
        dh_scr[...] = jnp.zeros_like(dh_scr)
        dsk = dsk_ref[...]
        a_neg = -jnp.exp(alog_ref[...])

        @pl.when(pl.program_id(1) == 0)
        def _():
            dbias_ref[...] = jnp.zeros_like(dbias_ref)
            dalog_ref[...] = jnp.zeros_like(dalog_ref)
            ddsk_ref[...] = jnp.zeros_like(ddsk_ref)

        spread_pair, spread_full, sum_pair, sum_full = _ssd_tables()
        sum_pair2 = jnp.concatenate([sum_pair] * 2, axis=0)
        sum_full2 = jnp.concatenate([sum_full] * 2, axis=0)
        head = functools.partial(_ssd_head, dtr_ref, bias_ref, a_neg, tri_bf=_bf(cc <= rr),
                                 spread_pair=spread_pair, spread_full=spread_full, head_s=head_s)
        head(jnp.int32(nc - 1), slot=0)

        def chunk(c, slot):
            r0 = pl.multiple_of(c * L, L)
            head(jnp.maximum(c - 1, 0), slot=1 - slot)
            raw, dt, acs, acs_t, dt_all, acs_all, acs_full = _ssd_head_load(head_s, slot)
            bb = _bf(b_ref[pl.ds(r0, L), :])
            cb = _bf(c_ref[pl.ds(r0, L), :])
            gm = _dot_nt(cb, bb)
            dg = jnp.zeros((L, L), F32)
            dbacc = jnp.zeros((L, LANES), F32)
            dcacc = jnp.zeros((L, LANES), F32)
            dacs_t = jnp.zeros((L, L), F32)
            ta_all, te_all, qm_all, tc_all, td_all, dcd_all = [], [], [], [], [], []
            for pr in range(N_PAIR):
                js = (2 * pr, 2 * pr + 1)
                lms, mfs = [], []
                for j in js:
                    seg = acs_full[:, j * LANES:(j + 1) * LANES] - acs_t[j:j + 1, :]
                    lm = jnp.exp(jnp.where(causal, seg, -1e30))
                    lms.append(lm)
                    mfs.append(gm * lm)
                x = x_ref[pl.ds(r0, L), pr * LANES:(pr + 1) * LANES]
                dyv = dy_ref[pl.ds(r0, L), pr * LANES:(pr + 1) * LANES]
                dt_p = dt_all[:, pr * LANES:(pr + 1) * LANES]
                acs_p = acs_all[:, pr * LANES:(pr + 1) * LANES]
                last_p = acs_p[L - 1:L, :]
                d_p = _pair_lanes(dsk[:, js[0]:js[0] + 1], dsk[:, js[1]:js[1] + 1], lane1)
                e_p = jnp.exp(acs_p)
                dte_p = jnp.exp(last_p - acs_p)
                xd = x * dt_p
                xdb = _bf(xd)
                hp = hs_ref[c, pr]
                hb = _bf(hp)
                dye = _bf(dyv * e_p)
                yo = _dot_nt(cb, hb) * e_p
                dcacc = dcacc + _dot(dye, hb)
                dh_out = _dot_tn(dye, cb)
                dyb = _bf(dyv)
                dms = _dot_nt(_bf(jnp.concatenate(
                    [jnp.where(lane < HEAD_DIM, dyv, 0.0), jnp.where(lane < HEAD_DIM, 0.0, dyv)], axis=0)), xdb)
                mty = _dot_tn(_bf(jnp.concatenate(mfs, axis=1)), dyb)
                dxd = jnp.where(lane < HEAD_DIM, mty[:L], mty[L:])
                for hh in range(2):
                    dm = dms[hh * L:(hh + 1) * L]
                    dg = dg + dm * lms[hh]
                    qm = dm * mfs[hh]
                    qm_all.append(qm)
                    dacs_t = dacs_t - jnp.where(rr == js[hh], jnp.sum(qm, axis=0, keepdims=True), 0.0)
                ds = dh_scr[pr]
                dsb = _bf(ds)
                xdd = xd * dte_p
                dxdd = _dot_nt(bb, dsb)
                dbacc = dbacc + _dot(_bf(xdd), dsb)
                dxd = dxd + dxdd * dte_p
                cd = jnp.where(rowc < HEAD_DIM, jnp.exp(acs[L - 1:L, js[0]:js[0] + 1]),
                               jnp.exp(acs[L - 1:L, js[1]:js[1] + 1]))
                prod = ds * hp
                dcd_all += [jnp.sum(prod[:HEAD_DIM], axis=0, keepdims=True),
                            jnp.sum(prod[HEAD_DIM:], axis=0, keepdims=True)]
                dh_scr[pr] = dh_out + ds * cd
                tcv = dxdd * xdd
                ta_all.append(dyv * yo - tcv)
                tc_all.append(jnp.sum(tcv, axis=0, keepdims=True))
                te_all.append(dxd * x)
                td_all.append(jnp.sum(dyv * x, axis=0, keepdims=True))
                dx_ref[pl.ds(r0, L), pr * LANES:(pr + 1) * LANES] = dxd * dt_p + dyv * d_p
            dgb = _bf(dg)
            dc_ref[pl.ds(r0, L), :] = dcacc + _dot(dgb, bb)
            db_ref[pl.ds(r0, L), :] = dbacc + _dot_tn(dgb, cb)
            last_add = jnp.zeros((1, LANES), F32)
            dd_add = jnp.zeros((1, LANES), F32)
            cd_row = jnp.exp(acs[L - 1:L, :])
            for j in range(SSD_HPG):
                hm = (lane1 >= HEAD_DIM) if j % 2 else (lane1 < HEAD_DIM)
                tcj = jnp.sum(jnp.where(hm, tc_all[j // 2], 0.0), axis=1, keepdims=True)
                tdj = jnp.sum(jnp.where(hm, td_all[j // 2], 0.0), axis=1, keepdims=True)
                dcdj = jnp.sum(dcd_all[j], axis=1, keepdims=True)
                last_add = last_add + jnp.where(lane1 == j, tcj + dcdj * cd_row, 0.0)
                dd_add = dd_add + jnp.where(lane1 == j, tdj, 0.0)
            dacs = (_dot(_split2(jnp.concatenate(ta_all, axis=1)), sum_pair2)
                    + _dot(_split2(jnp.concatenate(qm_all, axis=1)), sum_full2)
                    + dacs_t.T + jnp.where(rowc == L - 1, last_add, 0.0))
            ddta = _tri_dot(tri_t, dacs)
            ddt = _dot(_bf(jnp.concatenate(te_all, axis=1)), sum_pair) + ddta * a_neg
            ddraw = ddt * _sigmoid(raw)
            ddtr_ref[pl.ds(r0, L), :] = ddraw
            dbias_ref[...] += jnp.sum(ddraw, axis=0, keepdims=True)
            dalog_ref[...] += jnp.sum(ddta * dt, axis=0, keepdims=True) * a_neg
            ddsk_ref[...] += dd_add

        def two_chunks(i, _):
            chunk(nc - 1 - 2 * i, 0)
            chunk(nc - 2 - 2 * i, 1)
            return 0

        lax.fori_loop(0, nc // 2, two_chunks, 0)

    assert nc % 2 == 0
    grp = lambda g, b: (g, 0, 0)
    small = jax.ShapeDtypeStruct((SSD_GROUPS, 1, LANES), F32)
    return pl.pallas_call(
        body, grid=(SSD_GROUPS, nb),
        in_specs=[pl.BlockSpec((s, 4 * LANES), lambda g, b: (b, g)),
                  pl.BlockSpec((s, LANES), lambda g, b: (b, 16 + g)),
                  pl.BlockSpec((s, LANES), lambda g, b: (b, 20 + g)),
                  pl.BlockSpec((None, s, LANES), lambda g, b: (g, b, 0)),
                  pl.BlockSpec((None, 1, LANES), grp), pl.BlockSpec((None, 1, LANES), grp),
                  pl.BlockSpec((None, 1, LANES), grp),
                  pl.BlockSpec((None, None, nc, N_PAIR, L, LANES), lambda g, b: (b, g, 0, 0, 0, 0)),
                  pl.BlockSpec((s, 4 * LANES), lambda g, b: (b, g))],
        out_specs=[pl.BlockSpec((s, 4 * LANES), lambda g, b: (b, g)),
                   pl.BlockSpec((s, LANES), lambda g, b: (b, g)),
                   pl.BlockSpec((s, LANES), lambda g, b: (b, g)),
                   pl.BlockSpec((None, s, LANES), lambda g, b: (g, b, 0)),
                   pl.BlockSpec((None, 1, LANES), grp), pl.BlockSpec((None, 1, LANES), grp),
                   pl.BlockSpec((None, 1, LANES), grp)],
        out_shape=[jax.ShapeDtypeStruct((t, SSD_WIDTH), F32),
                   jax.ShapeDtypeStruct((t, SSD_GROUPS * SSD_STATE), F32),
                   jax.ShapeDtypeStruct((t, SSD_GROUPS * SSD_STATE), F32),
                   jax.ShapeDtypeStruct((SSD_GROUPS, t, LANES), F32), small, small, small],
        scratch_shapes=[pltpu.VMEM((N_PAIR, L, LANES), F32), pltpu.VMEM((2, L, HEAD_W), F32)],
        compiler_params=_params(("parallel", "arbitrary")), name="ssd_bwd")(
            xbc, xbc, xbc, dtr, bias, alog, dskip, hs, dy)


GN = SSD_WIDTH // SSD_GROUPS


def _post_fwd(o, proj, y, nw, *, tm=256):
    t = o.shape[0]

    def body(o_ref, za_ref, y_ref, zs_ref, nw_ref, ya_ref, ys_ref):
        za = za_ref[...].astype(F32)
        ya_ref[...] = _bf(o_ref[...] * (za * _sigmoid(za)))
        zs = zs_ref[...].astype(F32)
        yz = y_ref[...] * (zs * _sigmoid(zs))
        for g in range(SSD_GROUPS):
            sl = slice(g * GN, (g + 1) * GN)
            v = yz[:, sl]
            r = lax.rsqrt(jnp.mean(v * v, axis=-1, keepdims=True) + EPS)
            ys_ref[:, sl] = _bf(v * r * nw_ref[:, sl])

    return pl.pallas_call(
        body, grid=(t // tm,),
        in_specs=[pl.BlockSpec((tm, SB_WIDTH), lambda i: (i, 0)),
                  pl.BlockSpec((tm, SB_WIDTH), lambda i: (i, CB_ZA * LANES // SB_WIDTH)),
                  pl.BlockSpec((tm, SSD_WIDTH), lambda i: (i, 0)),
                  pl.BlockSpec((tm, SSD_WIDTH), lambda i: (i, CB_ZS * LANES // SSD_WIDTH)),
                  pl.BlockSpec((1, SSD_WIDTH), lambda i: (0, 0))],
        out_specs=[pl.BlockSpec((tm, SB_WIDTH), lambda i: (i, 0)),
                   pl.BlockSpec((tm, SSD_WIDTH), lambda i: (i, 0))],
        out_shape=[jax.ShapeDtypeStruct((t, SB_WIDTH), BF16), jax.ShapeDtypeStruct((t, SSD_WIDTH), BF16)],
        compiler_params=_params(("parallel",)), name="post_fwd")(o, proj, y, proj, nw)


def _post_bwd(dya, o, proj, dys, y, nw, dproj, *, tm=256):
    t = o.shape[0]
    zw = SB_WIDTH + SSD_WIDTH

    def body(dya_ref, o_ref, za_ref, dys_ref, y_ref, zs_ref, nw_ref, _, do_ref, dy_ref, dnw_ref, dz_ref):
        @pl.when(pl.program_id(0) == 0)
        def _():
            dnw_ref[...] = jnp.zeros_like(dnw_ref)

        za = za_ref[...].astype(F32)
        sa = _sigmoid(za)
        d = dya_ref[...].astype(F32)
        do_ref[...] = _bf(d * (za * sa))
        dz_ref[:, :SB_WIDTH] = _bf(d * o_ref[...] * (sa * (1.0 + za * (1.0 - sa))))
        zs = zs_ref[...].astype(F32)
        ss = _sigmoid(zs)
        silu = zs * ss
        yv = y_ref[...]
        yz = yv * silu
        dv = dys_ref[...].astype(F32)
        for g in range(SSD_GROUPS):
            sl = slice(g * GN, (g + 1) * GN)
            v = yz[:, sl]
            r = lax.rsqrt(jnp.mean(v * v, axis=-1, keepdims=True) + EPS)
            nrm = v * r
            dg = dv[:, sl]
            dnw_ref[:, sl] += jnp.sum(dg * nrm, axis=0, keepdims=True)
            dn = dg * nw_ref[:, sl]
            dyz = r * (dn - nrm * jnp.mean(dn * nrm, axis=-1, keepdims=True))
            dy_ref[:, sl] = dyz * silu[:, sl]
            dz_ref[:, SB_WIDTH + g * GN:SB_WIDTH + (g + 1) * GN] = _bf(
                dyz * yv[:, sl] * (ss[:, sl] * (1.0 + zs[:, sl] * (1.0 - ss[:, sl]))))

    a_spec = pl.BlockSpec((tm, SB_WIDTH), lambda i: (i, 0))
    s_spec = pl.BlockSpec((tm, SSD_WIDTH), lambda i: (i, 0))
    w_spec = pl.BlockSpec((1, SSD_WIDTH), lambda i: (0, 0))
    return pl.pallas_call(
        body, grid=(t // tm,),
        in_specs=[a_spec, a_spec, pl.BlockSpec((tm, SB_WIDTH), lambda i: (i, CB_ZA * LANES // SB_WIDTH)),
                  s_spec, s_spec, pl.BlockSpec((tm, SSD_WIDTH), lambda i: (i, CB_ZS * LANES // SSD_WIDTH)),
                  w_spec, pl.BlockSpec(memory_space=pl.ANY)],
        out_specs=[a_spec, s_spec, w_spec, pl.BlockSpec((tm, zw), lambda i: (i, 1))],
        out_shape=[jax.ShapeDtypeStruct((t, SB_WIDTH), BF16), jax.ShapeDtypeStruct((t, SSD_WIDTH), F32),
                   jax.ShapeDtypeStruct((1, SSD_WIDTH), F32), jax.ShapeDtypeStruct(dproj.shape, dproj.dtype)],
        input_output_aliases={7: 3},
        compiler_params=_params(("arbitrary",)), name="post_bwd")(dya, o, proj, dys, y, proj, nw, dproj)


def _merge_fwd(ya, ys, proj_g, *, tm=512):
    t = ya.shape[0]

    def body(ya_ref, ys_ref, ga_ref, gs_ref, m_ref):
        m_ref[...] = _bf(_sigmoid(ga_ref[...].astype(F32)) * ya_ref[...].astype(F32)
                         + _sigmoid(gs_ref[...].astype(F32)) * ys_ref[...].astype(F32))

    spec = pl.BlockSpec((tm, D_MODEL), lambda i: (i, 0))
    return pl.pallas_call(
        body, grid=(t // tm,),
        in_specs=[spec, spec, pl.BlockSpec((tm, D_MODEL), lambda i: (i, 0)),
                  pl.BlockSpec((tm, D_MODEL), lambda i: (i, 1))],
        out_specs=spec, out_shape=jax.ShapeDtypeStruct((t, D_MODEL), BF16),
        compiler_params=_params(("parallel",)), name="merge_fwd")(ya, ys, proj_g, proj_g)


def _merge_bwd(dm, ya, ys, proj_g, *, tm=512):
    t = ya.shape[0]

    def body(dm_ref, ya_ref, ys_ref, ga_ref, gs_ref, dya_ref, dys_ref, dg_ref):
        d = dm_ref[...].astype(F32)
        ga = _sigmoid(ga_ref[...].astype(F32))
        gs = _sigmoid(gs_ref[...].astype(F32))
        dya_ref[...] = _bf(d * ga)
        dys_ref[...] = _bf(d * gs)
        dg_ref[:, :D_MODEL] = _bf(d * ya_ref[...].astype(F32) * (ga * (1.0 - ga)))
        dg_ref[:, D_MODEL:] = _bf(d * ys_ref[...].astype(F32) * (gs * (1.0 - gs)))

    spec = pl.BlockSpec((tm, D_MODEL), lambda i: (i, 0))
    return pl.pallas_call(
        body, grid=(t // tm,),
        in_specs=[spec, spec, spec, pl.BlockSpec((tm, D_MODEL), lambda i: (i, 0)),
                  pl.BlockSpec((tm, D_MODEL), lambda i: (i, 1))],
        out_specs=[spec, spec, pl.BlockSpec((tm, 2 * D_MODEL), lambda i: (i, 0))],
        out_shape=[jax.ShapeDtypeStruct((t, D_MODEL), BF16), jax.ShapeDtypeStruct((t, D_MODEL), BF16),
                   jax.ShapeDtypeStruct((t, 2 * D_MODEL), BF16)],
        compiler_params=_params(("parallel",)), name="merge_bwd")(dm, ya, ys, proj_g, proj_g)


def _final(x, out, wf, target, *, tm=512):
    t = x.shape[0]

    def body(x_ref, o_ref, w_ref, t_ref, loss_ref, dx_ref, dw_ref):
        @pl.when(pl.program_id(0) == 0)
        def _():
            loss_ref[...] = jnp.zeros_like(loss_ref)
            dw_ref[...] = jnp.zeros_like(dw_ref)

        x2 = x_ref[...] + o_ref[...]
        r = lax.rsqrt(jnp.mean(x2 * x2, axis=-1, keepdims=True) + EPS)
        nrm = x2 * r
        e = nrm * w_ref[...] - t_ref[...]
        row_loss = jnp.mean(e * e, axis=-1, keepdims=True)
        loss_ref[...] += 0.5 * jnp.sum(row_loss, axis=0, keepdims=True)
        dyv = e * (1.0 / D_MODEL)
        dw_ref[...] += jnp.sum(dyv * nrm, axis=0, keepdims=True)
        dn = dyv * w_ref[...]
        dx_ref[...] = r * (dn - nrm * jnp.mean(dn * nrm, axis=-1, keepdims=True))

    spec = pl.BlockSpec((tm, D_MODEL), lambda i: (i, 0))
    fix = pl.BlockSpec((1, D_MODEL), lambda i: (0, 0))
    return pl.pallas_call(
        body, grid=(t // tm,),
        in_specs=[spec, spec, fix, spec],
        out_specs=[pl.BlockSpec((1, LANES), lambda i: (0, 0)), spec, fix],
        out_shape=[jax.ShapeDtypeStruct((1, LANES), F32), jax.ShapeDtypeStruct((t, D_MODEL), F32),
                   jax.ShapeDtypeStruct((1, D_MODEL), F32)],
        compiler_params=_params(("arbitrary",)), name="final_loss")(x, out, wf, target)


def _group_lanes(v):
    return jnp.pad(v.reshape(SSD_GROUPS, 1, SSD_HPG), ((0, 0), (0, 0), (0, LANES - SSD_HPG)))


def _ungroup_lanes(v):
    return v[:, :, :SSD_HPG].reshape(1, N_DT)


def _row_slabs(pieces, rows):
    total = sum(p.shape[0] for p in pieces)
    assert total % rows == 0
    slabs = []
    for lo in range(0, total, rows):
        parts, off = [], 0
        for p in pieces:
            a, b = max(lo, off), min(lo + rows, off + p.shape[0])
            if a < b:
                parts.append(p[a - off:b - off])
            off += p.shape[0]
        slabs.append(parts[0] if len(parts) == 1 else jnp.concatenate(parts, axis=0))
    return jnp.stack(slabs)


def _local_step(x, target, nb, s, w_in_t, conv_w, conv_b, norm_w, dt_bias, a_log, d_skip,
                ssm_norm_w, out_weights, final_norm_w, between=None):
    t = nb * s
    w_g_t = w_in_t[GATE_COL0:]
    w_dt_t = jnp.pad(w_in_t[DT_COL0:GATE_COL0], ((0, LANES - N_DT), (0, 0)))
    h, r1 = _rms_fwd(x, norm_w)
    proj = _matmul(h, w_in_t, nt=True, n=DT_COL0, out_dtype=BF16, name="proj_main")
    proj_g = _matmul(h, w_g_t, nt=True, out_dtype=BF16, name="proj_gate")
    dt_raw = _matmul(h, w_dt_t, nt=True, name="proj_dt")
    dtr = jnp.pad(dt_raw[:, :N_DT].reshape(t, SSD_GROUPS, SSD_HPG).transpose(1, 0, 2),
                  ((0, 0), (0, 0), (0, LANES - SSD_HPG)))
    bias_g, alog_g, dsk_g = _group_lanes(dt_bias), _group_lanes(a_log), _group_lanes(d_skip)

    o, att_a, att_sg = _attn_fwd(proj, nb, s)
    xbc = _conv_fwd(proj, conv_w, conv_b, nb, s)
    y, hs = _ssd_fwd(xbc, dtr, bias_g, alog_g, dsk_g, nb, s)
    ya_in, ys_in = _post_fwd(o, proj, y, ssm_norm_w)
    w_attn_out, w_ssm_out, w_o = out_weights(ys_in) if callable(out_weights) else out_weights
    ya = _matmul(ya_in, w_attn_out, out_dtype=BF16, name="attn_out")
    ys = _matmul(ys_in, w_ssm_out, out_dtype=BF16, name="ssm_out")
    merged = _merge_fwd(ya, ys, proj_g)
    out = _matmul(merged, w_o, name="out_proj")
    loss, dx2, d_final_w = _final(x, out, final_norm_w, target)

    dx2b = _bf(dx2)
    dmerged = _matmul(dx2b, w_o, nt=True, out_dtype=BF16, name="d_merged")
    d_w_o = _matmul(merged, dx2b, ta=True, name="d_w_o")
    dya, dys, dgate = _merge_bwd(dmerged, ya, ys, proj_g)
    d_w_attn_out = _matmul(ya_in, dya, ta=True, name="d_w_attn_out")
    d_w_ssm_out = _matmul(ys_in, dys, ta=True, name="d_w_ssm_out")
    dya_in = _matmul(dya, w_attn_out, nt=True, out_dtype=BF16, name="d_ya_in")
    dys_in = _matmul(dys, w_ssm_out, nt=True, name="d_ys_in")
    do, dy, d_ssm_norm_w, dproj = _post_bwd(dya_in, o, proj, dys_in, y, ssm_norm_w, lax.empty(proj.shape, BF16))
    dq, dk, dv = _attn_bwd(proj, do, att_a, att_sg, nb, s)
    dxs_act, db_act, dc_act, ddtr, dbias_g, dalog_g, ddsk_g = _ssd_bwd(
        xbc, dtr, bias_g, alog_g, dsk_g, hs, dy, nb, s)
    dproj, dwx, dbx = _conv_bwd(proj, dxs_act, conv_w, conv_b, dproj, nb, s, col0=0, name="conv_bwd_x")
    dproj, dwb, dbb = _conv_bwd(proj, db_act, conv_w, conv_b, dproj, nb, s, col0=SSD_WIDTH, name="conv_bwd_b")
    dproj, dwc, dbc = _conv_bwd(proj, dc_act, conv_w, conv_b, dproj, nb, s, col0=SSD_WIDTH + 512,
                                name="conv_bwd_c")
    for i, piece in enumerate((dq, dk, dv)):
        dproj = lax.dynamic_update_slice(dproj, piece, (0, i * SB_WIDTH))
    ddt = jnp.pad(_bf(ddtr[:, :, :SSD_HPG].transpose(1, 0, 2).reshape(t, N_DT)), ((0, 0), (0, LANES - N_DT)))
    wt = dict(ta=True, out_dtype=BF16)
    d_w_in_t = _row_slabs(
        [_matmul(dproj, h, name="d_w_main", **wt), _matmul(ddt, h, name="d_w_dt", **wt)[:N_DT],
         _matmul(dgate, h, name="d_w_gate", **wt)], SHARD)
    grads = dict(
        w_in_t=d_w_in_t,
        conv_w=jnp.concatenate([dwx, dwb, dwc], axis=1), conv_b=jnp.concatenate([dbx, dbb, dbc], axis=1),
        dt_bias=_ungroup_lanes(dbias_g), a_log=_ungroup_lanes(dalog_g), d_skip=_ungroup_lanes(ddsk_g),
        ssm_norm_w=d_ssm_norm_w, w_attn_out=d_w_attn_out, w_ssm_out=d_w_ssm_out, w_o=d_w_o,
        final_norm_w=d_final_w)
    extra = None
    if between is not None:
        zero, extra = between(grads)
        ddt = ddt + _bf(zero[0:1, 0:1])

    dh = _matmul(ddt, w_dt_t, name="d_h_dt")
    dh = _matmul(dgate, w_g_t, add=dh, name="d_h_gate")
    dh = _matmul(dproj, w_in_t, add=dh, name="d_h", tk=DT_COL0 // 4)
    grad_x, grads["norm_w"] = _rms_bwd(dh, x, r1, norm_w, dx2)
    return loss, grad_x, grads, extra


N_CHIP = N_DEV // 2


def _pair_exchange(scatter, bcast, *, name):
    arrays = list(scatter) + list(bcast)
    ns, n = len(scatter), len(arrays)
    n_sem = ns * N_CHIP + (n - ns) * (N_DEV - 1)

    def body(*refs):
        ins, outs = refs[:n], refs[n:2 * n]
        send_sems, recv_sems, local_sems = refs[2 * n:]
        x, y, c = lax.axis_index("x"), lax.axis_index("y"), lax.axis_index("c")
        me = 4 * x + 2 * y + c
        copies, k = [], 0
        for a in range(ns):
            for q in range(N_CHIP):
                cp = pltpu.make_async_remote_copy(
                    src_ref=ins[a].at[2 * q + 1 - c], dst_ref=outs[a].at[q],
                    send_sem=send_sems.at[k], recv_sem=recv_sems.at[k],
                    device_id=(x, y, 1 - c), device_id_type=pl.DeviceIdType.MESH)
                cp.start()
                copies.append(cp)
                k += 1
        for a in range(ns, n):
            cp = pltpu.make_async_copy(ins[a], outs[a].at[me], local_sems.at[a - ns])
            cp.start()
            copies.append(cp)
            for r in range(1, N_DEV):
                tx = 1 - x if (r >> 2) & 1 else x
                ty = 1 - y if (r >> 1) & 1 else y
                tc = 1 - c if r & 1 else c
                cp = pltpu.make_async_remote_copy(
                    src_ref=ins[a], dst_ref=outs[a].at[me], send_sem=send_sems.at[k], recv_sem=recv_sems.at[k],
                    device_id=(tx, ty, tc), device_id_type=pl.DeviceIdType.MESH)
                cp.start()
                copies.append(cp)
                k += 1
        for cp in copies:
            cp.wait()

    out_shape = [jax.ShapeDtypeStruct((N_CHIP,) + v.shape[1:] if i < ns else (N_DEV,) + v.shape, v.dtype)
                 for i, v in enumerate(arrays)]
    return pl.pallas_call(
        body, in_specs=[pl.BlockSpec(memory_space=pl.ANY)] * n,
        out_specs=[pl.BlockSpec(memory_space=pl.ANY)] * n, out_shape=out_shape,
        scratch_shapes=[pltpu.SemaphoreType.DMA((n_sem,)), pltpu.SemaphoreType.DMA((n_sem,)),
                        pltpu.SemaphoreType.DMA((max(n - ns, 1),))],
        compiler_params=pltpu.CompilerParams(has_side_effects=True), name=name)(*arrays)


def _pair_add(mine, recv, core, *, name, tc=None):
    _, r, c = recv.shape
    tc = c if tc is None else tc
    assert c % tc == 0

    def body(core_ref, m_ref, r_ref, o_ref):
        o_ref[...] = (m_ref[...].astype(F32) + r_ref[...].astype(F32)).astype(o_ref.dtype)

    spec = pl.BlockSpec((None, r, tc), lambda q, i, core_ref: (q, 0, i))
    return pl.pallas_call(
        body, grid_spec=pltpu.PrefetchScalarGridSpec(
            num_scalar_prefetch=1, grid=(N_CHIP, c // tc),
            in_specs=[pl.BlockSpec((None, r, tc), lambda q, i, core_ref: (2 * q + core_ref[0], 0, i)), spec],
            out_specs=spec),
        out_shape=jax.ShapeDtypeStruct(recv.shape, recv.dtype),
        compiler_params=_params(("parallel", "parallel")), name=name)(core, mine, recv)


_HBM = pl.BlockSpec(memory_space=pltpu.HBM)
_SEM = pl.BlockSpec(memory_space=pltpu.SEMAPHORE)
_DATAFLOW = pltpu.SideEffectType.DATAFLOW_SIDE_EFFECTING
_PEERS = {"chips": N_CHIP - 1, "all": N_DEV - 1}


def _split_copies(pattern, srcs, lands, send_sems, recv_sems):
    x, y, c = lax.axis_index("x"), lax.axis_index("y"), lax.axis_index("c")
    per = _PEERS[pattern]
    copies = []
    for a in range(len(srcs)):
        for r in range(1, per + 1):
            if pattern == "chips":
                tx = 1 - x if (r >> 1) & 1 else x
                ty = 1 - y if r & 1 else y
                tc = c
                src, dst = srcs[a].at[2 * tx + ty], lands[a].at[2 * x + y]
            else:
                tx = 1 - x if (r >> 2) & 1 else x
                ty = 1 - y if (r >> 1) & 1 else y
                tc = 1 - c if r & 1 else c
                src, dst = srcs[a], lands[a].at[4 * x + 2 * y + c]
            copies.append(pltpu.make_async_remote_copy(
                src_ref=src, dst_ref=dst, send_sem=send_sems.at[a * per + r - 1],
                recv_sem=recv_sems.at[a * per + r - 1], device_id=(tx, ty, tc), device_id_type=pl.DeviceIdType.MESH))
    return copies


def _split_start(pattern, arrays, *, name):
    n = len(arrays)
    n_sem = n * _PEERS[pattern]

    def body(*refs):
        srcs, lands = refs[:n], refs[n:2 * n]
        send_sems, recv_sems = refs[2 * n], refs[2 * n + 1]
        zero = refs[-1]
        for cp in _split_copies(pattern, srcs, lands, send_sems, recv_sems):
            cp.start()
        zero[...] = jnp.zeros_like(zero)

    src_t = [pltpu.HBM(v.shape, v.dtype) for v in arrays]
    land_t = [pltpu.HBM(v.shape if pattern == "chips" else (N_DEV,) + v.shape, v.dtype) for v in arrays]
    outs = pl.pallas_call(
        body, name=name,
        out_shape=(pltpu.SemaphoreType.DMA((n_sem,)), pltpu.SemaphoreType.DMA((n_sem,)), *src_t, *land_t,
                   jax.ShapeDtypeStruct((8, LANES), F32)),
        in_specs=[_HBM] * (2 * n),
        out_specs=(_SEM, _SEM, *[_HBM] * (2 * n), pl.BlockSpec(memory_space=pltpu.VMEM)),
        input_output_aliases={i: 2 + i for i in range(2 * n)},
        compiler_params=pltpu.CompilerParams(has_side_effects=_DATAFLOW))(
            *[pltpu.with_memory_space_constraint(v, pltpu.HBM) for v in arrays],
            *[pltpu.with_memory_space_constraint(lax.empty(t.shape, t.dtype), pltpu.HBM) for t in land_t])
    return outs[0], outs[1], outs[2:2 + n], outs[2 + n:2 + 2 * n], outs[-1]


def _split_wait(pattern, send_sems, recv_sems, srcs, lands, after, *, name):
    n = len(srcs)

    def body(*refs):
        for cp in _split_copies(pattern, refs[:n], refs[n:2 * n], refs[2 * n], refs[2 * n + 1]):
            cp.wait_send()
            cp.wait_recv()

    outs = pl.pallas_call(
        body, name=name, out_shape=[pltpu.HBM(v.shape, v.dtype) for v in (*srcs, *lands)],
        in_specs=[_HBM] * (2 * n) + [_SEM, _SEM, pl.BlockSpec(memory_space=pl.ANY)],
        out_specs=[_HBM] * (2 * n), input_output_aliases={i: i for i in range(2 * n)},
        compiler_params=pltpu.CompilerParams(has_side_effects=_DATAFLOW))(
            *srcs, *lands, send_sems, recv_sems, after)
    return outs[:n], outs[n:]


def _own_slab(lands, srcs, index, whole):
    out = []
    for land, src in zip(lands, srcs):
        own = src[None] if whole else lax.dynamic_slice(src, (index, 0, 0), (1,) + src.shape[1:])
        out.append(lax.dynamic_update_slice(land, own, (index, 0, 0)))
    return out


def _gather(arrays, *, name):
    n = len(arrays)
    per = N_DEV - 1

    def body(*refs):
        ins, outs = refs[:n], refs[n:2 * n]
        send_sems, recv_sems, local_sems = refs[2 * n:]
        x, y, c = lax.axis_index("x"), lax.axis_index("y"), lax.axis_index("c")
        me, sibling = (x, y, c), (x, y, 1 - c)
        chips = [(1 - x, y), (x, 1 - y), (1 - x, 1 - y)]

        def slab(a, block):
            return outs[a].at[4 * block[0] + 2 * block[1] + block[2]]

        def copy(a, k, block, to, src=None):
            rows = slab(a, block)
            return pltpu.make_async_remote_copy(
                src_ref=rows if src is None else src, dst_ref=rows,
                send_sem=send_sems.at[a * per + k], recv_sem=recv_sems.at[a * per + k],
                device_id=to, device_id_type=pl.DeviceIdType.MESH)

        started = []
        for a in range(n):
            mine = pltpu.make_async_copy(ins[a], slab(a, me), local_sems.at[a])
            mine.start()
            started.append(mine)
        for a in range(n):
            first = [copy(a, 1 + j, me, (*chip, c), src=ins[a]) for j, chip in enumerate(chips)]
            first.append(copy(a, 0, me, sibling, src=ins[a]))
            for cp in first:
                cp.start()
            started += first
        for j, chip in enumerate(chips):
            for a in range(n):
                copy(a, 1 + j, (*chip, c), me).wait_recv()
                passed = copy(a, 4 + j, (*chip, c), sibling)
                passed.start()
                started.append(passed)
        for a in range(n):
            copy(a, 0, sibling, me).wait_recv()
            for j, chip in enumerate(chips):
                copy(a, 4 + j, (*chip, 1 - c), me).wait_recv()
        for cp in started[:n]:
            cp.wait()
        for cp in started[n:]:
            cp.wait_send()

    return pl.pallas_call(
        body, in_specs=[pl.BlockSpec(memory_space=pl.ANY)] * n,
        out_specs=[pl.BlockSpec(memory_space=pl.ANY)] * n,
        out_shape=[jax.ShapeDtypeStruct((N_DEV,) + v.shape, v.dtype) for v in arrays],
        scratch_shapes=[pltpu.SemaphoreType.DMA((n * per,)), pltpu.SemaphoreType.DMA((n * per,)),
                        pltpu.SemaphoreType.DMA((n,))],
        compiler_params=pltpu.CompilerParams(has_side_effects=True), name=name)(*arrays)


def _adamw(parts, w, m, v, *, name, tc=None):
    r, c = w.shape
    n_parts = parts.shape[0]
    tc = c if tc is None else tc
    assert c % tc == 0

    def body(p_ref, w_ref, m_ref, v_ref, g_ref, d_ref, nm_ref, nv_ref):
        g = p_ref[0].astype(F32)
        for d in range(1, n_parts):
            g = g + p_ref[d].astype(F32)
        mm = ADAM_B1 * m_ref[...] + (1.0 - ADAM_B1) * g
        vv = ADAM_B2 * v_ref[...] + (1.0 - ADAM_B2) * (g * g)
        m_hat = mm / (1.0 - ADAM_B1 ** ADAM_STEP)
        v_hat = vv / (1.0 - ADAM_B2 ** ADAM_STEP)
        g_ref[...] = g
        d_ref[...] = -ADAM_LR * (m_hat / (jnp.sqrt(v_hat) + ADAM_EPS) + ADAM_WD * w_ref[...])
        nm_ref[...] = mm
        nv_ref[...] = vv

    spec = pl.BlockSpec((r, tc), lambda i: (0, i))
    out = jax.ShapeDtypeStruct((r, c), F32)
    return pl.pallas_call(
        body, grid=(c // tc,),
        in_specs=[pl.BlockSpec((n_parts, r, tc), lambda i: (0, 0, i)), spec, spec, spec],
        out_specs=[spec] * 4, out_shape=[out] * 4,
        compiler_params=_params(("parallel",)), name=name)(parts, w, m, v)


SMALL = (("norm_w", D_MODEL), ("conv_b", CONV_DIM), ("dt_bias", N_DT), ("a_log", N_DT),
         ("d_skip", N_DT), ("ssm_norm_w", SSD_WIDTH), ("final_norm_w", D_MODEL))
SMALL_USED = sum(size for _, size in SMALL)
SMALL_ROWS = 64
SHARD = D_PROJ // N_DEV


def _pack_small(vals, scalar=None):
    parts = [vals[k].reshape(-1).astype(F32) for k, _ in SMALL]
    if scalar is not None:
        parts.append(scalar.reshape(1))
    flat = jnp.concatenate(parts)
    return jnp.pad(flat, (0, SMALL_ROWS * LANES - flat.shape[0])).reshape(SMALL_ROWS, LANES)


def _unpack_small(packed, shapes):
    flat = packed.reshape(-1)
    out, off = {}, 0
    for k, size in SMALL:
        out[k] = flat[off:off + size].reshape(shapes[k])
        off += size
    return out


def kernel(x, norm_w, w_in, conv_w, conv_b, dt_bias, a_log, d_skip, ssm_norm_w, w_attn_out, w_ssm_out, w_o, final_norm_w, loss_target, m_norm_w, m_w_in, m_conv_w, m_conv_b, m_dt_bias, m_a_log, m_d_skip, m_ssm_norm_w, m_w_attn_out, m_w_ssm_out, m_w_o, m_final_norm_w, v_norm_w, v_w_in, v_conv_w, v_conv_b, v_dt_bias, v_a_log, v_d_skip, v_ssm_norm_w, v_w_attn_out, v_w_ssm_out, v_w_o, v_final_norm_w):
    nb, s, _ = x.shape
    t = nb * s
    weights = dict(norm_w=norm_w, w_in=w_in, conv_w=conv_w, conv_b=conv_b, dt_bias=dt_bias, a_log=a_log,
                   d_skip=d_skip, ssm_norm_w=ssm_norm_w, w_attn_out=w_attn_out, w_ssm_out=w_ssm_out,
                   w_o=w_o, final_norm_w=final_norm_w)
    moms = dict(norm_w=m_norm_w, w_in=m_w_in, conv_w=m_conv_w, conv_b=m_conv_b, dt_bias=m_dt_bias,
                a_log=m_a_log, d_skip=m_d_skip, ssm_norm_w=m_ssm_norm_w, w_attn_out=m_w_attn_out,
                w_ssm_out=m_w_ssm_out, w_o=m_w_o, final_norm_w=m_final_norm_w)
    vels = dict(norm_w=v_norm_w, w_in=v_w_in, conv_w=v_conv_w, conv_b=v_conv_b, dt_bias=v_dt_bias,
                a_log=v_a_log, d_skip=v_d_skip, ssm_norm_w=v_ssm_norm_w, w_attn_out=v_w_attn_out,
                w_ssm_out=v_w_ssm_out, w_o=v_w_o, final_norm_w=v_final_norm_w)

    tr_ = lambda a: jnp.transpose(a[0])
    g_in, g_conv = _gather([_bf(tr_(w_in)), conv_w[0]], name="gather_weights")
    conv_full = g_conv.transpose(1, 0, 2).reshape(CONV_K, CONV_DIM)
    o_send, o_recv, o_srcs, o_lands, o_zero = _split_start(
        "all", [_bf(w_attn_out[0]), _bf(w_ssm_out[0]), _bf(w_o[0])], name="gather_out_weights_start")

    core = lax.axis_index("c").astype(jnp.int32).reshape(1)
    chip = 2 * lax.axis_index("x") + lax.axis_index("y")
    me = 2 * chip + lax.axis_index("c")

    def out_weights(after):
        srcs, lands = _split_wait("all", o_send, o_recv, o_srcs, o_lands, after, name="gather_out_weights_wait")
        g_wa, g_ws, g_wo = _own_slab(lands, srcs, me, True)
        return (g_wa.reshape(SB_WIDTH, D_MODEL), g_ws.reshape(SSD_WIDTH, D_MODEL), g_wo.reshape(D_MODEL, D_MODEL))

    def send_weight_grads(grads):
        slabs = [grads["w_in_t"],
                 grads["conv_w"].reshape(CONV_K, N_DEV, CONV_DIM // N_DEV).transpose(1, 0, 2),
                 _bf(grads["w_attn_out"].reshape(N_DEV, SB_WIDTH // N_DEV, D_MODEL)),
                 _bf(grads["w_ssm_out"].reshape(N_DEV, SSD_WIDTH // N_DEV, D_MODEL)),
                 _bf(grads["w_o"].reshape(N_DEV, D_MODEL // N_DEV, D_MODEL))]
        recv = _pair_exchange(slabs, [], name="exchange_pairs")
        names = ("w_in", "conv_w", "w_attn_out", "w_ssm_out", "w_o")
        sums = [_pair_add(a, b, core, name="pair_add_" + k, tc=4 * LANES if k == "w_in" else None)
                for k, a, b in zip(names, slabs, recv)]
        send_sems, recv_sems, srcs, lands, zero = _split_start("chips", sums, name="exchange_chips_start")
        return zero, (send_sems, recv_sems, srcs, lands)

    loss_part, grad_x, grads, (send_sems, recv_sems, srcs, lands) = _local_step(
        x.reshape(t, D_MODEL), loss_target.reshape(t, D_MODEL), nb, s, g_in.reshape(D_PROJ, D_MODEL),
        conv_full, conv_b, norm_w + o_zero[0:1, 0:1], dt_bias, a_log, d_skip, ssm_norm_w, out_weights,
        final_norm_w.reshape(1, D_MODEL), between=send_weight_grads)

    (p_small,) = _pair_exchange([], [_pack_small(grads, loss_part[0, 0])], name="exchange_small")
    srcs, lands = _split_wait("chips", send_sems, recv_sems, srcs, lands, grad_x, name="exchange_chips_wait")
    p_in, p_conv, p_wa, p_ws, p_wo = _own_slab(lands, srcs, chip, False)

    res = {}
    res["w_in"] = [a.T for a in _adamw(p_in, tr_(w_in), tr_(m_w_in), tr_(v_w_in), name="adamw_w_in", tc=2 * LANES)]
    res["conv_w"] = _adamw(p_conv, conv_w[0], m_conv_w[0], v_conv_w[0], name="adamw_conv_w")
    res["w_attn_out"] = _adamw(p_wa, w_attn_out[0], m_w_attn_out[0], v_w_attn_out[0], name="adamw_w_attn_out")
    res["w_ssm_out"] = _adamw(p_ws, w_ssm_out[0], m_w_ssm_out[0], v_w_ssm_out[0], name="adamw_w_ssm_out")
    res["w_o"] = _adamw(p_wo, w_o[0], m_w_o[0], v_w_o[0], name="adamw_w_o")
    sm = _adamw(p_small, _pack_small(weights), _pack_small(moms), _pack_small(vels), name="adamw_small")
    loss = sm[0].reshape(-1)[SMALL_USED]
    shapes = {k: weights[k].shape for k, _ in SMALL}
    sm = [_unpack_small(v, shapes) for v in sm]
    for k, _ in SMALL:
        res[k] = tuple(part[k] for part in sm)

    order = ("norm_w", "w_in", "conv_w", "conv_b", "dt_bias", "a_log", "d_skip", "ssm_norm_w",
             "w_attn_out", "w_ssm_out", "w_o", "final_norm_w")
    outs = [loss, grad_x.reshape(nb, s, D_MODEL)]
    for i in range(4):
        outs += [res[k][i].reshape(weights[k].shape) for k in order]
    return tuple(outs)
```

```python
import functools

import jax
import jax.numpy as jnp
from jax import lax
from jax.experimental import pallas as pl
from jax.experimental.pallas import tpu as pltpu

F32 = jnp.float32
BF16 = jnp.bfloat16

D_MODEL = 1024
SB_WIDTH = 1024
HEAD_DIM = 64
SSD_WIDTH = 2048
SSD_GROUPS = 4
SSD_HPG = 8
SSD_STATE = 128
SSD_CHUNK = 128
CONV_K = 4
CONV_DIM = 3072
N_DT = 32
D_PROJ = 11296
DT_COL0 = 9216
GATE_COL0 = DT_COL0 + N_DT
EPS = 1e-6
N_DEV = 8

ADAM_LR = 0.001
ADAM_B1 = 0.9
ADAM_B2 = 0.999
ADAM_EPS = 1e-08
ADAM_WD = 0.01
ADAM_STEP = 10

LANES = 128
VMEM_LIMIT = 56 * 1024 * 1024

CB_Q, CB_K, CB_V, CB_ZA, CB_ZS, CB_XBC = 0, 8, 16, 24, 32, 48

NT_DIMS = (((1,), (1,)), ((), ()))
TN_DIMS = (((0,), (0,)), ((), ()))


def _params(sem):
    return pltpu.CompilerParams(dimension_semantics=sem, vmem_limit_bytes=VMEM_LIMIT)


def _dot(a, b):
    return jnp.dot(a, b, preferred_element_type=F32)


def _dot_nt(a, b):
    return lax.dot_general(a, b, NT_DIMS, preferred_element_type=F32)


def _dot_tn(a, b):
    return lax.dot_general(a, b, TN_DIMS, preferred_element_type=F32)


def _bf(a):
    return a.astype(BF16)


def _sigmoid(x):
    return 0.5 + 0.5 * jnp.tanh(0.5 * x)


def _softplus(x):
    return jnp.maximum(x, 0.0) + jnp.log(1.0 + jnp.exp(-jnp.abs(x)))


def _trunc_split(a):
    bits = lax.bitcast_convert_type(a, jnp.uint32) & jnp.uint32(0xFFFF0000)
    hi = lax.bitcast_convert_type(bits, F32)
    return _bf(hi), _bf(a - hi)


MM_TILE = 1024
MM_TK = 2048
MM_TALL = 2048


def _matmul(a, b, *, name, nt=False, ta=False, add=None, out_dtype=F32, tm=MM_TILE, tn=MM_TILE, tk=MM_TK, n=None):
    k, m = a.shape if ta else a.shape[::-1]
    n = (b.shape[0] if nt else b.shape[1]) if n is None else n
    tm, tn, tk = min(tm, m), min(tn, n), min(tk, k)
    assert m % tm == 0 and n % tn == 0 and k % tk == 0, (name, a.shape, b.shape)
    assert not (ta and nt)
    nk = k // tk
    has_add = add is not None

    def body(*refs):
        a_ref, b_ref = refs[:2]
        add_ref = refs[2] if has_add else None
        o_ref = refs[3 if has_add else 2]
        acc_ref = refs[-1] if nk > 1 else None
        kk = pl.program_id(2)
        av, bv = _bf(a_ref[...]), _bf(b_ref[...])
        d = _dot_nt(av, bv) if nt else _dot_tn(av, bv) if ta else _dot(av, bv)
        if nk == 1:
            o_ref[...] = ((d + add_ref[...]) if has_add else d).astype(out_dtype)
            return

        @pl.when(kk == 0)
        def _():
            acc_ref[...] = jnp.zeros_like(acc_ref)

        acc_ref[...] += d

        @pl.when(kk == nk - 1)
        def _():
            r = acc_ref[...]
            if has_add:
                r = r + add_ref[...]
            o_ref[...] = r.astype(out_dtype)

    in_specs = [pl.BlockSpec((tk, tm), lambda j, i, kk: (kk, i)) if ta
                else pl.BlockSpec((tm, tk), lambda j, i, kk: (i, kk)),
                pl.BlockSpec((tn, tk), lambda j, i, kk: (j, kk)) if nt
                else pl.BlockSpec((tk, tn), lambda j, i, kk: (kk, j))]
    args = [a, b]
    if has_add:
        in_specs.append(pl.BlockSpec((tm, tn), lambda j, i, kk: (i, j)))
        args.append(add)
    return pl.pallas_call(
        body, grid=(n // tn, m // tm, nk), in_specs=in_specs,
        out_specs=pl.BlockSpec((tm, tn), lambda j, i, kk: (i, j)),
        out_shape=jax.ShapeDtypeStruct((m, n), out_dtype),
        scratch_shapes=[pltpu.VMEM((tm, tn), F32)] if nk > 1 else [],
        compiler_params=_params(("parallel", "parallel", "arbitrary")), name=name)(*args)


def _rms_fwd(x, w, *, tm=512):
    t = x.shape[0]

    def body(x_ref, w_ref, h_ref, r_ref):
        xv = x_ref[...]
        r = lax.rsqrt(jnp.mean(xv * xv, axis=-1, keepdims=True) + EPS)
        h_ref[...] = _bf(xv * r * w_ref[...])
        r_ref[...] = r

    row = lambda i: (i, 0)
    return pl.pallas_call(
        body, grid=(t // tm,),
        in_specs=[pl.BlockSpec((tm, D_MODEL), row), pl.BlockSpec((1, D_MODEL), lambda i: (0, 0))],
        out_specs=[pl.BlockSpec((tm, D_MODEL), row), pl.BlockSpec((tm, 1), row)],
        out_shape=[jax.ShapeDtypeStruct((t, D_MODEL), BF16), jax.ShapeDtypeStruct((t, 1), F32)],
        compiler_params=_params(("parallel",)), name="rms_fwd")(x, w)


def _rms_bwd(dh, x, r, w, dres, *, tm=512):
    t = x.shape[0]

    def body(dh_ref, x_ref, r_ref, w_ref, dres_ref, dx_ref, dw_ref):
        @pl.when(pl.program_id(0) == 0)
        def _():
            dw_ref[...] = jnp.zeros_like(dw_ref)

        nrm = x_ref[...] * r_ref[...]
        dhv = dh_ref[...]
        dw_ref[...] += jnp.sum(dhv * nrm, axis=0, keepdims=True)
        dn = dhv * w_ref[...]
        dx = r_ref[...] * (dn - nrm * jnp.mean(dn * nrm, axis=-1, keepdims=True))
        dx_ref[...] = dx + dres_ref[...]

    row = lambda i: (i, 0)
    fix = lambda i: (0, 0)
    return pl.pallas_call(
        body, grid=(t // tm,),
        in_specs=[pl.BlockSpec((tm, D_MODEL), row), pl.BlockSpec((tm, D_MODEL), row),
                  pl.BlockSpec((tm, 1), row), pl.BlockSpec((1, D_MODEL), fix),
                  pl.BlockSpec((tm, D_MODEL), row)],
        out_specs=[pl.BlockSpec((tm, D_MODEL), row), pl.BlockSpec((1, D_MODEL), fix)],
        out_shape=[jax.ShapeDtypeStruct((t, D_MODEL), F32), jax.ShapeDtypeStruct((1, D_MODEL), F32)],
        compiler_params=_params(("arbitrary",)), name="rms_bwd")(dh, x, r, w, dres)


ATT_BLK = 256


def _attn_masks(blk):
    r_io = lax.broadcasted_iota(jnp.int32, (blk, blk), 0)
    c_io = lax.broadcasted_iota(jnp.int32, (blk, blk), 1)
    return r_io, c_io


def _attn_tiles(nq):
    return nq * (nq + 1) // 2


def _attn_fwd(proj, nb, s):
    blk = min(ATT_BLK, s)
    nq = s // blk
    nt = _attn_tiles(nq)
    scale = HEAD_DIM ** -0.5

    def body(q_ref, k_ref, v_ref, o_ref, a_out, sg_out, q0_s, q1_s, kb_s, v0_s, v1_s, z_s, a_s, acc_s, cr_s):
        head0 = lax.broadcasted_iota(jnp.int32, (s, LANES), 1) < HEAD_DIM
        r_io, c_io = _attn_masks(blk)
        tri = c_io < r_io
        tri2 = jnp.concatenate([tri, tri], axis=0)
        u_gt = _bf(r_io > c_io)
        u2 = jnp.concatenate([u_gt, u_gt], axis=0)
        qv = q_ref[...] * scale
        q0_s[...] = _bf(jnp.where(head0, qv, 0.0))
        q1_s[...] = _bf(jnp.where(head0, 0.0, qv))
        kb_s[...] = _bf(k_ref[...])
        vv = v_ref[...]
        v0_s[...] = _bf(jnp.where(head0, vv, 0.0))
        v1_s[...] = _bf(jnp.where(head0, 0.0, vv))

        def scores(qcat, kblk, slot):
            k0 = pl.multiple_of(jnp.maximum(kblk, 0) * blk, blk)
            z_s[slot] = _dot_nt(qcat, kb_s[pl.ds(k0, blk), :])

        def weights(slot, tile, diag, zslot=None):
            z = z_s[slot if zslot is None else zslot]
            cr = cr_s[...]
            sp = _softplus(z)
            spm = jnp.where(tri2, sp, 0.0) if diag else sp
            hi, lo = _trunc_split(spm)
            ps = _dot(jnp.concatenate([hi, lo], axis=1), u2)
            lb = z - sp
            a = jnp.exp(lb - (ps + cr))
            if diag:
                a = jnp.where(tri2, a, 0.0)
            ab = _bf(a)
            a_s[slot] = ab
            a_out[tile] = ab
            sg_out[tile] = _bf(jnp.exp(lb))
            cr_s[...] = cr + ps[:, 0:1] + spm[:, 0:1]

        def apply(slot, k0):
            a = a_s[slot]
            vcat = jnp.concatenate([v0_s[pl.ds(k0, blk), :], v1_s[pl.ds(k0, blk), :]], axis=0)
            acc_s[...] += _dot(jnp.concatenate([a[:blk], a[blk:]], axis=1), vcat)

        def queries(qi):
            q0 = pl.multiple_of(qi * blk, blk)
            return jnp.concatenate([q0_s[pl.ds(q0, blk), :], q1_s[pl.ds(q0, blk), :]], axis=0)

        def first_scores(qi):
            qcat = queries(qi)
            scores(qcat, qi, 2)
            scores(qcat, qi - 1, 3)

        first_scores(jnp.int32(0))

        def qblock(qi, _):
            q0 = pl.multiple_of(qi * blk, blk)
            tile0 = qi * (qi + 1) // 2
            qcat = queries(qi)
            z_s[1] = z_s[3]
            cr_s[...] = jnp.zeros_like(cr_s)
            acc_s[...] = jnp.zeros_like(acc_s)
            weights(0, tile0 + qi, True, zslot=2)
            first_scores(jnp.minimum(qi + 1, nq - 1))

            def step(j, slot):
                scores(qcat, qi - j - 1, 1 - slot)
                weights(slot, tile0 + qi - j, False)
                apply(1 - slot, pl.multiple_of((qi - j + 1) * blk, blk))

            def pair(i, _):
                step(2 * i + 1, 1)
                step(2 * i + 2, 0)
                return 0

            lax.fori_loop(0, qi // 2, pair, 0)

            @pl.when(qi % 2 == 1)
            def _():
                step(qi, 1)

            apply(qi & 1, 0)
            o_ref[pl.ds(q0, blk), :] = acc_s[...]
            return 0

        lax.fori_loop(0, nq, qblock, 0)

    t = nb * s
    tiles = pl.BlockSpec((None, None, nt, 2 * blk, blk), lambda b, p: (b, p, 0, 0, 0))
    tiles_shape = jax.ShapeDtypeStruct((nb, 8, nt, 2 * blk, blk), BF16)
    return pl.pallas_call(
        body, grid=(nb, 8),
        in_specs=[pl.BlockSpec((s, LANES), lambda b, p: (b, CB_Q + p)),
                  pl.BlockSpec((s, LANES), lambda b, p: (b, CB_K + p)),
                  pl.BlockSpec((s, LANES), lambda b, p: (b, CB_V + p))],
        out_specs=[pl.BlockSpec((s, LANES), lambda b, p: (b, p)), tiles, tiles],
        out_shape=[jax.ShapeDtypeStruct((t, SB_WIDTH), F32), tiles_shape, tiles_shape],
        scratch_shapes=[pltpu.VMEM((s, LANES), BF16)] * 5 + [
            pltpu.VMEM((4, 2 * blk, blk), F32), pltpu.VMEM((2, 2 * blk, blk), BF16),
            pltpu.VMEM((blk, LANES), F32), pltpu.VMEM((2 * blk, 1), F32)],
        compiler_params=_params(("parallel", "parallel")), name="attn_fwd")(proj, proj, proj)


def _attn_bwd(proj, do, a_all, sg_all, nb, s):
    blk = min(ATT_BLK, s)
    nq = s // blk
    nt = _attn_tiles(nq)
    scale = HEAD_DIM ** -0.5

    def body(q_ref, k_ref, v_ref, do_ref, a_in, sg_in, dq_ref, dk_ref, dv_ref,
             qt0_s, qt1_s, k0_s, k1_s, vb_s, d0_s, d1_s, dt0_s, dt1_s, da_s, dz_s, dkt_acc, dvt_acc, dq_acc, cw_s):
        head0 = lax.broadcasted_iota(jnp.int32, (s, LANES), 1) < HEAD_DIM
        r_io, c_io = _attn_masks(blk)
        tri = c_io < r_io
        tri2 = jnp.concatenate([tri, tri], axis=0)
        u_lt = _bf(r_io < c_io)
        qv = q_ref[...].astype(F32) * scale
        qt0_s[...] = _bf(jnp.where(head0, qv, 0.0).T)
        qt1_s[...] = _bf(jnp.where(head0, 0.0, qv).T)
        kv = k_ref[...]
        k0_s[...] = _bf(jnp.where(head0, kv, 0.0))
        k1_s[...] = _bf(jnp.where(head0, 0.0, kv))
        vb_s[...] = _bf(v_ref[...])
        dov = do_ref[...].astype(F32)
        d0 = jnp.where(head0, dov, 0.0)
        d1 = jnp.where(head0, 0.0, dov)
        d0_s[...] = _bf(d0)
        d1_s[...] = _bf(d1)
        dt0_s[...] = _bf(d0.T)
        dt1_s[...] = _bf(d1.T)
        dkt_acc[...] = jnp.zeros_like(dkt_acc)
        dvt_acc[...] = jnp.zeros_like(dvt_acc)

        def qblock(qi, _):
            q0 = pl.multiple_of(qi * blk, blk)
            tile0 = qi * (qi + 1) // 2
            qs_ = pl.ds(q0, blk)
            dcat = jnp.concatenate([d0_s[qs_, :], d1_s[qs_, :]], axis=0)
            qcat_t = jnp.concatenate([qt0_s[:, qs_], qt1_s[:, qs_]], axis=1)
            dcat_t = jnp.concatenate([dt0_s[:, qs_], dt1_s[:, qs_]], axis=1)

            def scores(kblk, slot):
                da_s[slot] = _dot_nt(dcat, vb_s[pl.ds(pl.multiple_of(kblk * blk, blk), blk), :])

            def grads(slot, kblk, diag):
                cw = cw_s[...]
                w = a_in[tile0 + kblk].astype(F32) * da_s[slot]
                wex = _dot(_bf(w), u_lt)
                dz = w - sg_in[tile0 + kblk].astype(F32) * (w + (wex + cw))
                if diag:
                    dz = jnp.where(tri2, dz, 0.0)
                dz_s[slot] = _bf(dz)
                cw_s[...] = cw + wex[:, blk - 1:blk] + w[:, blk - 1:blk]

            def apply(slot, kblk):
                ks_ = pl.ds(pl.multiple_of(kblk * blk, blk), blk)
                dz = dz_s[slot]
                dq_acc[...] += _dot(jnp.concatenate([dz[:blk], dz[blk:]], axis=1),
                                    jnp.concatenate([k0_s[ks_, :], k1_s[ks_, :]], axis=0))
                dkt_acc[:, ks_] += _dot(qcat_t, dz)
                dvt_acc[:, ks_] += _dot(dcat_t, a_in[tile0 + kblk])

            def step(j, slot, diag):
                if not diag:
                    scores(j + 1, 1 - slot)
                grads(slot, j, diag)

                @pl.when(j > 0)
                def _():
                    apply(1 - slot, j - 1)

            scores(0, 0)
            dq_acc[...] = jnp.zeros_like(dq_acc)
            cw_s[...] = jnp.zeros_like(cw_s)

            def pair(i, _):
                step(2 * i, 0, False)
                step(2 * i + 1, 1, False)
                return 0

            lax.fori_loop(0, qi // 2, pair, 0)

            @pl.when(qi % 2 == 1)
            def _():
                step(qi - 1, 0, False)
                step(qi, 1, True)

            @pl.when(qi % 2 == 0)
            def _():
                step(qi, 0, True)

            apply(qi & 1, qi)
            dq_ref[qs_, :] = _bf(dq_acc[...] * scale)
            return 0

        lax.fori_loop(0, nq, qblock, 0)
        dk_ref[...] = _bf(dkt_acc[...].T)
        dv_ref[...] = _bf(dvt_acc[...].T)

    t = nb * s
    hp = lambda b, p: (b, p)
    out = jax.ShapeDtypeStruct((t, SB_WIDTH), BF16)
    tiles = pl.BlockSpec((None, None, nt, 2 * blk, blk), lambda b, p: (b, p, 0, 0, 0))
    return pl.pallas_call(
        body, grid=(nb, 8),
        in_specs=[pl.BlockSpec((s, LANES), lambda b, p: (b, CB_Q + p)),
                  pl.BlockSpec((s, LANES), lambda b, p: (b, CB_K + p)),
                  pl.BlockSpec((s, LANES), lambda b, p: (b, CB_V + p)),
                  pl.BlockSpec((s, LANES), hp), tiles, tiles],
        out_specs=[pl.BlockSpec((s, LANES), hp)] * 3,
        out_shape=[out, out, out],
        scratch_shapes=[pltpu.VMEM((LANES, s), BF16)] * 2 + [pltpu.VMEM((s, LANES), BF16)] * 5 + [
            pltpu.VMEM((LANES, s), BF16)] * 2 + [
            pltpu.VMEM((2, 2 * blk, blk), F32), pltpu.VMEM((2, 2 * blk, blk), BF16),
            pltpu.VMEM((LANES, s), F32), pltpu.VMEM((LANES, s), F32),
            pltpu.VMEM((blk, LANES), F32), pltpu.VMEM((2 * blk, 1), F32)],
        compiler_params=_params(("parallel", "parallel")), name="attn_bwd")(proj, proj, proj, do, a_all, sg_all)


CONV_TC = 256
CONV_ROWS = 64
HALO = 8


def _conv_rows(ws, w_ref, b_ref, n):
    taps = [ws[HALO - kk:HALO - kk + n, :] for kk in range(CONV_K)]
    y = b_ref[...] + w_ref[CONV_K - 1:CONV_K, :] * taps[0]
    for kk in range(1, CONV_K):
        y = y + w_ref[CONV_K - 1 - kk:CONV_K - kk, :] * taps[kk]
    return y, taps


def _conv_fwd(proj, conv_w, conv_b, nb, s):
    tc, rows = CONV_TC, min(CONV_ROWS, s)
    cb0 = CB_XBC * LANES // tc

    def body(x_ref, w_ref, b_ref, o_ref, xp, ws):
        xp[0:HALO, :] = jnp.zeros((HALO, tc), F32)
        xp[HALO:HALO + s, :] = x_ref[...].astype(F32)

        def chunk(i, _):
            r0 = pl.multiple_of(i * rows, rows)
            ws[...] = xp[pl.ds(r0, rows + HALO), :]
            y, _ = _conv_rows(ws, w_ref, b_ref, rows)
            o_ref[pl.ds(r0, rows), :] = y * _sigmoid(y)
            return 0

        lax.fori_loop(0, s // rows, chunk, 0)

    t = nb * s
    return pl.pallas_call(
        body, grid=(nb, CONV_DIM // tc),
        in_specs=[pl.BlockSpec((s, tc), lambda b, c: (b, cb0 + c)),
                  pl.BlockSpec((CONV_K, tc), lambda b, c: (0, c)),
                  pl.BlockSpec((1, tc), lambda b, c: (0, c))],
        out_specs=pl.BlockSpec((s, tc), lambda b, c: (b, c)),
        out_shape=jax.ShapeDtypeStruct((t, CONV_DIM), F32),
        scratch_shapes=[pltpu.VMEM((s + HALO, tc), F32), pltpu.VMEM((rows + HALO, tc), F32)],
        compiler_params=_params(("parallel", "parallel")), name="conv_fwd")(proj, conv_w, conv_b)


def _conv_bwd(proj, dact, conv_w, conv_b, dproj, nb, s, *, col0, name):
    width = dact.shape[1]
    tc, rows = min(CONV_TC, width), min(CONV_ROWS, s)
    cb0 = (CB_XBC * LANES + col0) // tc
    wb0 = col0 // tc

    def body(x_ref, d_ref, w_ref, b_ref, _, dx_ref, dw_ref, db_ref, xp, dp, dys, ws):
        @pl.when(pl.program_id(1) == 0)
        def _():
            dw_ref[...] = jnp.zeros_like(dw_ref)
            db_ref[...] = jnp.zeros_like(db_ref)

        n = rows + HALO
        pad = jnp.zeros((HALO, tc), F32)
        xp[0:HALO, :] = pad
        xp[HALO:HALO + s, :] = x_ref[...].astype(F32)
        xp[HALO + s:HALO + s + HALO, :] = pad
        dp[0:s, :] = d_ref[...]
        dp[s:s + HALO, :] = pad

        def chunk(i, acc):
            r0 = pl.multiple_of(i * rows, rows)
            ws[...] = xp[pl.ds(r0, n + HALO), :]
            y, taps = _conv_rows(ws, w_ref, b_ref, n)
            sg = _sigmoid(y)
            dy = dp[pl.ds(r0, n), :] * (sg * (1.0 + y * (1.0 - sg)))
            dys[...] = dy
            dx = w_ref[CONV_K - 1:CONV_K, :] * dy[0:rows]
            for kk in range(1, CONV_K):
                dx = dx + w_ref[CONV_K - 1 - kk:CONV_K - kk, :] * dys[kk:kk + rows, :]
            dx_ref[pl.ds(r0, rows), :] = _bf(dx)
            dyc = dy[0:rows]
            new = [acc[0] + jnp.sum(dyc, axis=0, keepdims=True)]
            for kk in range(CONV_K):
                new.append(acc[1 + kk] + jnp.sum(dyc * taps[kk][0:rows], axis=0, keepdims=True))
            return tuple(new)

        acc = lax.fori_loop(0, s // rows, chunk, (jnp.zeros((1, tc), F32),) * (1 + CONV_K))
        db_ref[...] += acc[0]
        for kk in range(CONV_K):
            dw_ref[CONV_K - 1 - kk:CONV_K - kk, :] += acc[1 + kk]

    t = nb * s
    return pl.pallas_call(
        body, grid=(width // tc, nb),
        in_specs=[pl.BlockSpec((s, tc), lambda c, b: (b, cb0 + c)),
                  pl.BlockSpec((s, tc), lambda c, b: (b, c)),
                  pl.BlockSpec((CONV_K, tc), lambda c, b: (0, wb0 + c)),
                  pl.BlockSpec((1, tc), lambda c, b: (0, wb0 + c)), pl.BlockSpec(memory_space=pl.ANY)],
        out_specs=[pl.BlockSpec((s, tc), lambda c, b: (b, cb0 + c)),
                   pl.BlockSpec((CONV_K, tc), lambda c, b: (0, c)),
                   pl.BlockSpec((1, tc), lambda c, b: (0, c))],
        out_shape=[jax.ShapeDtypeStruct(dproj.shape, dproj.dtype),
                   jax.ShapeDtypeStruct((CONV_K, width), F32),
                   jax.ShapeDtypeStruct((1, width), F32)],
        input_output_aliases={4: 0},
        scratch_shapes=[pltpu.VMEM((s + 2 * HALO, tc), F32), pltpu.VMEM((s + HALO, tc), F32),
                        pltpu.VMEM((rows + HALO, tc), F32), pltpu.VMEM((rows + 2 * HALO, tc), F32)],
        compiler_params=_params(("parallel", "arbitrary")), name=name)(proj, dact, conv_w, conv_b, dproj)


L = SSD_CHUNK
N_PAIR = SSD_HPG // 2


def _pair_lanes(v0, v1, lane):
    return jnp.where(lane < HEAD_DIM, v0, v1)


def _onehot(shape, row_shift, col_shift):
    row = lax.broadcasted_iota(jnp.int32, shape, 0)
    col = lax.broadcasted_iota(jnp.int32, shape, 1)
    return _bf(lax.shift_right_logical(row, row_shift) == lax.shift_right_logical(col, col_shift))


def _ssd_tables():
    sp = _onehot((LANES, N_PAIR * LANES), 0, 6)
    sf = _onehot((LANES, SSD_HPG * LANES), 0, 7)
    return (jnp.concatenate([sp] * 3, axis=0), jnp.concatenate([sf] * 3, axis=0),
            _onehot((N_PAIR * LANES, LANES), 6, 0), _onehot((SSD_HPG * LANES, LANES), 7, 0))


def _split3(a):
    a1 = _bf(a)
    r1 = a - a1.astype(F32)
    a2 = _bf(r1)
    return jnp.concatenate([a1, a2, _bf(r1 - a2.astype(F32))], axis=1)


def _split2(a):
    hi, lo = _trunc_split(a)
    return jnp.concatenate([hi, lo], axis=1)


def _tri_dot(tri_bf, a):
    a1 = _bf(a)
    r1 = a - a1.astype(F32)
    a2 = _bf(r1)
    parts = jnp.concatenate([a1, a2, _bf(r1 - a2.astype(F32))], axis=0)
    return _dot(jnp.concatenate([tri_bf] * 3, axis=1), parts)


HEAD_W = 4 * LANES + 2 * N_PAIR * LANES + SSD_HPG * LANES


def _ssd_head(dtr_ref, bias_ref, a_neg, c, tri_bf, spread_pair, spread_full, head_s, slot):
    raw = dtr_ref[pl.ds(pl.multiple_of(c * L, L), L), :] + bias_ref[...]
    dt = _softplus(raw)
    acs = _tri_dot(tri_bf, dt * a_neg)
    acs3 = _split3(acs)
    both = _dot(jnp.concatenate([_split3(dt), acs3], axis=0), spread_pair)
    parts = (raw, dt, acs, acs.T, both[:L], both[L:], _dot(acs3, spread_full))
    off = 0
    for p in parts:
        head_s[slot, :, off:off + p.shape[1]] = p
        off += p.shape[1]


def _ssd_head_load(head_s, slot):
    out, off = [], 0
    for w in (LANES, LANES, LANES, L, N_PAIR * LANES, N_PAIR * LANES, SSD_HPG * LANES):
        out.append(head_s[slot, :, off:off + w])
        off += w
    return out


def _ssd_fwd(xbc, dtr, bias, alog, dskip, nb, s):
    nc = s // L
    t = nb * s

    def body(x_ref, b_ref, c_ref, dtr_ref, bias_ref, alog_ref, dsk_ref, y_ref, hs_ref, h_scr, head_s):
        lane = lax.broadcasted_iota(jnp.int32, (L, LANES), 1)
        lane1 = lax.broadcasted_iota(jnp.int32, (1, LANES), 1)
        rowc = lax.broadcasted_iota(jnp.int32, (L, 1), 0)
        rr = lax.broadcasted_iota(jnp.int32, (L, L), 0)
        cc = lax.broadcasted_iota(jnp.int32, (L, L), 1)
        causal = rr >= cc
        h_scr[...] = jnp.zeros_like(h_scr)
        dsk = dsk_ref[...]
        spread_pair, spread_full, _, _ = _ssd_tables()
        a_neg = -jnp.exp(alog_ref[...])
        head = functools.partial(_ssd_head, dtr_ref, bias_ref, a_neg, tri_bf=_bf(cc <= rr),
                                 spread_pair=spread_pair, spread_full=spread_full, head_s=head_s)
        head(jnp.int32(0), slot=0)

        def chunk(c, slot):
            r0 = pl.multiple_of(c * L, L)
            head(jnp.minimum(c + 1, nc - 1), slot=1 - slot)
            _, _, acs, acs_t, dt_all, acs_all, acs_full = _ssd_head_load(head_s, slot)
            bb = _bf(b_ref[pl.ds(r0, L), :])
            cb = _bf(c_ref[pl.ds(r0, L), :])
            gm = _dot_nt(cb, bb)
            for pr in range(N_PAIR):
                j0, j1 = 2 * pr, 2 * pr + 1
                ms = []
                for j in (j0, j1):
                    seg = acs_full[:, j * LANES:(j + 1) * LANES] - acs_t[j:j + 1, :]
                    ms.append(_bf(gm * jnp.exp(jnp.where(causal, seg, -1e30))))
                x = x_ref[pl.ds(r0, L), pr * LANES:(pr + 1) * LANES]
                dt_p = dt_all[:, pr * LANES:(pr + 1) * LANES]
                acs_p = acs_all[:, pr * LANES:(pr + 1) * LANES]
                last_p = acs_p[L - 1:L, :]
                d_p = _pair_lanes(dsk[:, j0:j0 + 1], dsk[:, j1:j1 + 1], lane1)
                xd = x * dt_p
                xdb = _bf(xd)
                yds = _dot(jnp.concatenate(ms, axis=0), xdb)
                yd = _pair_lanes(yds[:L], yds[L:], lane)
                hp = h_scr[pr]
                yo = _dot_nt(cb, _bf(hp)) * jnp.exp(acs_p)
                y_ref[pl.ds(r0, L), pr * LANES:(pr + 1) * LANES] = yd + yo + x * d_p
                hs_ref[c, pr] = hp
                sp = _dot_tn(_bf(xd * jnp.exp(last_p - acs_p)), bb)
                cd = jnp.where(rowc < HEAD_DIM, jnp.exp(acs[L - 1:L, j0:j0 + 1]),
                               jnp.exp(acs[L - 1:L, j1:j1 + 1]))
                h_scr[pr] = hp * cd + sp

        def two_chunks(i, _):
            chunk(2 * i, 0)
            chunk(2 * i + 1, 1)
            return 0

        lax.fori_loop(0, nc // 2, two_chunks, 0)

    assert nc % 2 == 0
    grp = lambda b, g: (g, 0, 0)
    return pl.pallas_call(
        body, grid=(nb, SSD_GROUPS),
        in_specs=[pl.BlockSpec((s, 4 * LANES), lambda b, g: (b, g)),
                  pl.BlockSpec((s, LANES), lambda b, g: (b, 16 + g)),
                  pl.BlockSpec((s, LANES), lambda b, g: (b, 20 + g)),
                  pl.BlockSpec((None, s, LANES), lambda b, g: (g, b, 0)),
                  pl.BlockSpec((None, 1, LANES), grp), pl.BlockSpec((None, 1, LANES), grp),
                  pl.BlockSpec((None, 1, LANES), grp)],
        out_specs=[pl.BlockSpec((s, 4 * LANES), lambda b, g: (b, g)),
                   pl.BlockSpec((None, None, nc, N_PAIR, L, LANES), lambda b, g: (b, g, 0, 0, 0, 0))],
        out_shape=[jax.ShapeDtypeStruct((t, SSD_WIDTH), F32),
                   jax.ShapeDtypeStruct((nb, SSD_GROUPS, nc, N_PAIR, L, LANES), F32)],
        scratch_shapes=[pltpu.VMEM((N_PAIR, L, LANES), F32), pltpu.VMEM((2, L, HEAD_W), F32)],
        compiler_params=_params(("parallel", "parallel")), name="ssd_fwd")(
            xbc, xbc, xbc, dtr, bias, alog, dskip)


def _ssd_bwd(xbc, dtr, bias, alog, dskip, hs, dy, nb, s):
    nc = s // L
    t = nb * s

    def body(x_ref, b_ref, c_ref, dtr_ref, bias_ref, alog_ref, dsk_ref, hs_ref, dy_ref,
             dx_ref, db_ref, dc_ref, ddtr_ref, dbias_ref, dalog_ref, ddsk_ref, dh_scr, head_s):
        lane = lax.broadcasted_iota(jnp.int32, (L, LANES), 1)
        lane1 = lax.broadcasted_iota(jnp.int32, (1, LANES), 1)
        rowc = lax.broadcasted_iota(jnp.int32, (L, 1), 0)
        rr = lax.broadcasted_iota(jnp.int32, (L, L), 0)
        cc = lax.broadcasted_iota(jnp.int32, (L, L), 1)
        causal = rr >= cc
        tri_t = _bf(cc >= rr)
        dh_scr[...] = jnp.zeros_like(dh_scr)
        dsk = dsk_ref[...]
        a_neg = -jnp.exp(alog_ref[...])

        @pl.when(pl.program_id(1) == 0)
        def _():
            dbias_ref[...] = jnp.zeros_like(dbias_ref)
            dalog_ref[...] = jnp.zeros_like(dalog_ref)
            ddsk_ref[...] = jnp.zeros_like(ddsk_ref)

        spread_pair, spread_full, sum_pair, sum_full = _ssd_tables()
        sum_pair2 = jnp.concatenate([sum_pair] * 2, axis=0)
        sum_full2 = jnp.concatenate([sum_full] * 2, axis=0)
        head = functools.partial(_ssd_head, dtr_ref, bias_ref, a_neg, tri_bf=_bf(cc <= rr),
                                 spread_pair=spread_pair, spread_full=spread_full, head_s=head_s)
        head(jnp.int32(nc - 1), slot=0)

        def chunk(c, slot):
            r0 = pl.multiple_of(c * L, L)
            head(jnp.maximum(c - 1, 0), slot=1 - slot)
            raw, dt, acs, acs_t, dt_all, acs_all, acs_full = _ssd_head_load(head_s, slot)
            bb = _bf(b_ref[pl.ds(r0, L), :])
            cb = _bf(c_ref[pl.ds(r0, L), :])
            gm = _dot_nt(cb, bb)
            dg = jnp.zeros((L, L), F32)
            dbacc = jnp.zeros((L, LANES), F32)
            dcacc = jnp.zeros((L, LANES), F32)
            dacs_t = jnp.zeros((L, L), F32)
            ta_all, te_all, qm_all, tc_all, td_all, dcd_all = [], [], [], [], [], []
            for pr in range(N_PAIR):
                js = (2 * pr, 2 * pr + 1)
                lms, mfs = [], []
                for j in js:
                    seg = acs_full[:, j * LANES:(j + 1) * LANES] - acs_t[j:j + 1, :]
                    lm = jnp.exp(jnp.where(causal, seg, -1e30))
                    lms.append(lm)
                    mfs.append(gm * lm)
                x = x_ref[pl.ds(r0, L), pr * LANES:(pr + 1) * LANES]
                dyv = dy_ref[pl.ds(r0, L), pr * LANES:(pr + 1) * LANES]
                dt_p = dt_all[:, pr * LANES:(pr + 1) * LANES]
                acs_p = acs_all[:, pr * LANES:(pr + 1) * LANES]
                last_p = acs_p[L - 1:L, :]
                d_p = _pair_lanes(dsk[:, js[0]:js[0] + 1], dsk[:, js[1]:js[1] + 1], lane1)
                e_p = jnp.exp(acs_p)
                dte_p = jnp.exp(last_p - acs_p)
                xd = x * dt_p
                xdb = _bf(xd)
                hp = hs_ref[c, pr]
                hb = _bf(hp)
                dye = _bf(dyv * e_p)
                yo = _dot_nt(cb, hb) * e_p
                dcacc = dcacc + _dot(dye, hb)
                dh_out = _dot_tn(dye, cb)
                dyb = _bf(dyv)
                dms = _dot_nt(_bf(jnp.concatenate(
                    [jnp.where(lane < HEAD_DIM, dyv, 0.0), jnp.where(lane < HEAD_DIM, 0.0, dyv)], axis=0)), xdb)
                mty = _dot_tn(_bf(jnp.concatenate(mfs, axis=1)), dyb)
                dxd = jnp.where(lane < HEAD_DIM, mty[:L], mty[L:])
                for hh in range(2):
                    dm = dms[hh * L:(hh + 1) * L]
                    dg = dg + dm * lms[hh]
                    qm = dm * mfs[hh]
                    qm_all.append(qm)
                    dacs_t = dacs_t - jnp.where(rr == js[hh], jnp.sum(qm, axis=0, keepdims=True), 0.0)
                ds = dh_scr[pr]
                dsb = _bf(ds)
                xdd = xd * dte_p
                dxdd = _dot_nt(bb, dsb)
                dbacc = dbacc + _dot(_bf(xdd), dsb)
                dxd = dxd + dxdd * dte_p
                cd = jnp.where(rowc < HEAD_DIM, jnp.exp(acs[L - 1:L, js[0]:js[0] + 1]),
                               jnp.exp(acs[L - 1:L, js[1]:js[1] + 1]))
                prod = ds * hp
                dcd_all += [jnp.sum(prod[:HEAD_DIM], axis=0, keepdims=True),
                            jnp.sum(prod[HEAD_DIM:], axis=0, keepdims=True)]
                dh_scr[pr] = dh_out + ds * cd
                tcv = dxdd * xdd
                ta_all.append(dyv * yo - tcv)
                tc_all.append(jnp.sum(tcv, axis=0, keepdims=True))
                te_all.append(dxd * x)
                td_all.append(jnp.sum(dyv * x, axis=0, keepdims=True))
                dx_ref[pl.ds(r0, L), pr * LANES:(pr + 1) * LANES] = dxd * dt_p + dyv * d_p
            dgb = _bf(dg)
            dc_ref[pl.ds(r0, L), :] = dcacc + _dot(dgb, bb)
            db_ref[pl.ds(r0, L), :] = dbacc + _dot_tn(dgb, cb)
            last_add = jnp.zeros((1, LANES), F32)
            dd_add = jnp.zeros((1, LANES), F32)
            cd_row = jnp.exp(acs[L - 1:L, :])
            for j in range(SSD_HPG):
                hm = (lane1 >= HEAD_DIM) if j % 2 else (lane1 < HEAD_DIM)
                tcj = jnp.sum(jnp.where(hm, tc_all[j // 2], 0.0), axis=1, keepdims=True)
                tdj = jnp.sum(jnp.where(hm, td_all[j // 2], 0.0), axis=1, keepdims=True)
                dcdj = jnp.sum(dcd_all[j], axis=1, keepdims=True)
                last_add = last_add + jnp.where(lane1 == j, tcj + dcdj * cd_row, 0.0)
                dd_add = dd_add + jnp.where(lane1 == j, tdj, 0.0)
            dacs = (_dot(_split2(jnp.concatenate(ta_all, axis=1)), sum_pair2)
                    + _dot(_split2(jnp.concatenate(qm_all, axis=1)), sum_full2)
                    + dacs_t.T + jnp.where(rowc == L - 1, last_add, 0.0))
            ddta = _tri_dot(tri_t, dacs)
            ddt = _dot(_bf(jnp.concatenate(te_all, axis=1)), sum_pair) + ddta * a_neg
            ddraw = ddt * _sigmoid(raw)
            ddtr_ref[pl.ds(r0, L), :] = ddraw
            dbias_ref[...] += jnp.sum(ddraw, axis=0, keepdims=True)
            dalog_ref[...] += jnp.sum(ddta * dt, axis=0, keepdims=True) * a_neg
            ddsk_ref[...] += dd_add

        def two_chunks(i, _):
            chunk(nc - 1 - 2 * i, 0)
            chunk(nc - 2 - 2 * i, 1)
            return 0

        lax.fori_loop(0, nc // 2, two_chunks, 0)

    assert nc % 2 == 0
    grp = lambda g, b: (g, 0, 0)
    small = jax.ShapeDtypeStruct((SSD_GROUPS, 1, LANES), F32)
    return pl.pallas_call(
        body, grid=(SSD_GROUPS, nb),
        in_specs=[pl.BlockSpec((s, 4 * LANES), lambda g, b: (b, g)),
                  pl.BlockSpec((s, LANES), lambda g, b: (b, 16 + g)),
                  pl.BlockSpec((s, LANES), lambda g, b: (b, 20 + g)),
                  pl.BlockSpec((None, s, LANES), lambda g, b: (g, b, 0)),
                  pl.BlockSpec((None, 1, LANES), grp), pl.BlockSpec((None, 1, LANES), grp),
                  pl.BlockSpec((None, 1, LANES), grp),
                  pl.BlockSpec((None, None, nc, N_PAIR, L, LANES), lambda g, b: (b, g, 0, 0, 0, 0)),
                  pl.BlockSpec((s, 4 * LANES), lambda g, b: (b, g))],
        out_specs=[pl.BlockSpec((s, 4 * LANES), lambda g, b: (b, g)),
                   pl.BlockSpec((s, LANES), lambda g, b: (b, g)),
                   pl.BlockSpec((s, LANES), lambda g, b: (b, g)),
                   pl.BlockSpec((None, s, LANES), lambda g, b: (g, b, 0)),
                   pl.BlockSpec((None, 1, LANES), grp), pl.BlockSpec((None, 1, LANES), grp),
                   pl.BlockSpec((None, 1, LANES), grp)],
        out_shape=[jax.ShapeDtypeStruct((t, SSD_WIDTH), F32),
                   jax.ShapeDtypeStruct((t, SSD_GROUPS * SSD_STATE), F32),
                   jax.ShapeDtypeStruct((t, SSD_GROUPS * SSD_STATE), F32),
                   jax.ShapeDtypeStruct((SSD_GROUPS, t, LANES), F32), small, small, small],
        scratch_shapes=[pltpu.VMEM((N_PAIR, L, LANES), F32), pltpu.VMEM((2, L, HEAD_W), F32)],
        compiler_params=_params(("parallel", "arbitrary")), name="ssd_bwd")(
            xbc, xbc, xbc, dtr, bias, alog, dskip, hs, dy)


GN = SSD_WIDTH // SSD_GROUPS


def _post_fwd(o, proj, y, nw, *, tm=256):
    t = o.shape[0]

    def body(o_ref, za_ref, y_ref, zs_ref, nw_ref, ya_ref, ys_ref):
        za = za_ref[...].astype(F32)
        ya_ref[...] = _bf(o_ref[...] * (za * _sigmoid(za)))
        zs = zs_ref[...].astype(F32)
        yz = y_ref[...] * (zs * _sigmoid(zs))
        for g in range(SSD_GROUPS):
            sl = slice(g * GN, (g + 1) * GN)
            v = yz[:, sl]
            r = lax.rsqrt(jnp.mean(v * v, axis=-1, keepdims=True) + EPS)
            ys_ref[:, sl] = _bf(v * r * nw_ref[:, sl])

    return pl.pallas_call(
        body, grid=(t // tm,),
        in_specs=[pl.BlockSpec((tm, SB_WIDTH), lambda i: (i, 0)),
                  pl.BlockSpec((tm, SB_WIDTH), lambda i: (i, CB_ZA * LANES // SB_WIDTH)),
                  pl.BlockSpec((tm, SSD_WIDTH), lambda i: (i, 0)),
                  pl.BlockSpec((tm, SSD_WIDTH), lambda i: (i, CB_ZS * LANES // SSD_WIDTH)),
                  pl.BlockSpec((1, SSD_WIDTH), lambda i: (0, 0))],
        out_specs=[pl.BlockSpec((tm, SB_WIDTH), lambda i: (i, 0)),
                   pl.BlockSpec((tm, SSD_WIDTH), lambda i: (i, 0))],
        out_shape=[jax.ShapeDtypeStruct((t, SB_WIDTH), BF16), jax.ShapeDtypeStruct((t, SSD_WIDTH), BF16)],
        compiler_params=_params(("parallel",)), name="post_fwd")(o, proj, y, proj, nw)


def _post_bwd(dya, o, proj, dys, y, nw, dproj, *, tm=256):
    t = o.shape[0]
    zw = SB_WIDTH + SSD_WIDTH

    def body(dya_ref, o_ref, za_ref, dys_ref, y_ref, zs_ref, nw_ref, _, do_ref, dy_ref, dnw_ref, dz_ref):
        @pl.when(pl.program_id(0) == 0)
        def _():
            dnw_ref[...] = jnp.zeros_like(dnw_ref)

        za = za_ref[...].astype(F32)
        sa = _sigmoid(za)
        d = dya_ref[...].astype(F32)
        do_ref[...] = _bf(d * (za * sa))
        dz_ref[:, :SB_WIDTH] = _bf(d * o_ref[...] * (sa * (1.0 + za * (1.0 - sa))))
        zs = zs_ref[...].astype(F32)
        ss = _sigmoid(zs)
        silu = zs * ss
        yv = y_ref[...]
        yz = yv * silu
        dv = dys_ref[...].astype(F32)
        for g in range(SSD_GROUPS):
            sl = slice(g * GN, (g + 1) * GN)
            v = yz[:, sl]
            r = lax.rsqrt(jnp.mean(v * v, axis=-1, keepdims=True) + EPS)
            nrm = v * r
            dg = dv[:, sl]
            dnw_ref[:, sl] += jnp.sum(dg * nrm, axis=0, keepdims=True)
            dn = dg * nw_ref[:, sl]
            dyz = r * (dn - nrm * jnp.mean(dn * nrm, axis=-1, keepdims=True))
            dy_ref[:, sl] = dyz * silu[:, sl]
            dz_ref[:, SB_WIDTH + g * GN:SB_WIDTH + (g + 1) * GN] = _bf(
                dyz * yv[:, sl] * (ss[:, sl] * (1.0 + zs[:, sl] * (1.0 - ss[:, sl]))))

    a_spec = pl.BlockSpec((tm, SB_WIDTH), lambda i: (i, 0))
    s_spec = pl.BlockSpec((tm, SSD_WIDTH), lambda i: (i, 0))
    w_spec = pl.BlockSpec((1, SSD_WIDTH), lambda i: (0, 0))
    return pl.pallas_call(
        body, grid=(t // tm,),
        in_specs=[a_spec, a_spec, pl.BlockSpec((tm, SB_WIDTH), lambda i: (i, CB_ZA * LANES // SB_WIDTH)),
                  s_spec, s_spec, pl.BlockSpec((tm, SSD_WIDTH), lambda i: (i, CB_ZS * LANES // SSD_WIDTH)),
                  w_spec, pl.BlockSpec(memory_space=pl.ANY)],
        out_specs=[a_spec, s_spec, w_spec, pl.BlockSpec((tm, zw), lambda i: (i, 1))],
        out_shape=[jax.ShapeDtypeStruct((t, SB_WIDTH), BF16), jax.ShapeDtypeStruct((t, SSD_WIDTH), F32),
                   jax.ShapeDtypeStruct((1, SSD_WIDTH), F32), jax.ShapeDtypeStruct(dproj.shape, dproj.dtype)],
        input_output_aliases={7: 3},
        compiler_params=_params(("arbitrary",)), name="post_bwd")(dya, o, proj, dys, y, proj, nw, dproj)


def _merge_fwd(ya, ys, proj_g, *, tm=512):
    t = ya.shape[0]

    def body(ya_ref, ys_ref, ga_ref, gs_ref, m_ref):
        m_ref[...] = _bf(_sigmoid(ga_ref[...].astype(F32)) * ya_ref[...].astype(F32)
                         + _sigmoid(gs_ref[...].astype(F32)) * ys_ref[...].astype(F32))

    spec = pl.BlockSpec((tm, D_MODEL), lambda i: (i, 0))
    return pl.pallas_call(
        body, grid=(t // tm,),
        in_specs=[spec, spec, pl.BlockSpec((tm, D_MODEL), lambda i: (i, 0)),
                  pl.BlockSpec((tm, D_MODEL), lambda i: (i, 1))],
        out_specs=spec, out_shape=jax.ShapeDtypeStruct((t, D_MODEL), BF16),
        compiler_params=_params(("parallel",)), name="merge_fwd")(ya, ys, proj_g, proj_g)


def _merge_bwd(dm, ya, ys, proj_g, *, tm=512):
    t = ya.shape[0]

    def body(dm_ref, ya_ref, ys_ref, ga_ref, gs_ref, dya_ref, dys_ref, dg_ref):
        d = dm_ref[...].astype(F32)
        ga = _sigmoid(ga_ref[...].astype(F32))
        gs = _sigmoid(gs_ref[...].astype(F32))
        dya_ref[...] = _bf(d * ga)
        dys_ref[...] = _bf(d * gs)
        dg_ref[:, :D_MODEL] = _bf(d * ya_ref[...].astype(F32) * (ga * (1.0 - ga)))
        dg_ref[:, D_MODEL:] = _bf(d * ys_ref[...].astype(F32) * (gs * (1.0 - gs)))

    spec = pl.BlockSpec((tm, D_MODEL), lambda i: (i, 0))
    return pl.pallas_call(
        body, grid=(t // tm,),
        in_specs=[spec, spec, spec, pl.BlockSpec((tm, D_MODEL), lambda i: (i, 0)),
                  pl.BlockSpec((tm, D_MODEL), lambda i: (i, 1))],
        out_specs=[spec, spec, pl.BlockSpec((tm, 2 * D_MODEL), lambda i: (i, 0))],
        out_shape=[jax.ShapeDtypeStruct((t, D_MODEL), BF16), jax.ShapeDtypeStruct((t, D_MODEL), BF16),
                   jax.ShapeDtypeStruct((t, 2 * D_MODEL), BF16)],
        compiler_params=_params(("parallel",)), name="merge_bwd")(dm, ya, ys, proj_g, proj_g)


def _final(x, out, wf, target, *, tm=512):
    t = x.shape[0]

    def body(x_ref, o_ref, w_ref, t_ref, loss_ref, dx_ref, dw_ref):
        @pl.when(pl.program_id(0) == 0)
        def _():
            loss_ref[...] = jnp.zeros_like(loss_ref)
            dw_ref[...] = jnp.zeros_like(dw_ref)

        x2 = x_ref[...] + o_ref[...]
        r = lax.rsqrt(jnp.mean(x2 * x2, axis=-1, keepdims=True) + EPS)
        nrm = x2 * r
        e = nrm * w_ref[...] - t_ref[...]
        row_loss = jnp.mean(e * e, axis=-1, keepdims=True)
        loss_ref[...] += 0.5 * jnp.sum(row_loss, axis=0, keepdims=True)
        dyv = e * (1.0 / D_MODEL)
        dw_ref[...] += jnp.sum(dyv * nrm, axis=0, keepdims=True)
        dn = dyv * w_ref[...]
        dx_ref[...] = r * (dn - nrm * jnp.mean(dn * nrm, axis=-1, keepdims=True))

    spec = pl.BlockSpec((tm, D_MODEL), lambda i: (i, 0))
    fix = pl.BlockSpec((1, D_MODEL), lambda i: (0, 0))
    return pl.pallas_call(
        body, grid=(t // tm,),
        in_specs=[spec, spec, fix, spec],
        out_specs=[pl.BlockSpec((1, LANES), lambda i: (0, 0)), spec, fix],
        out_shape=[jax.ShapeDtypeStruct((1, LANES), F32), jax.ShapeDtypeStruct((t, D_MODEL), F32),
                   jax.ShapeDtypeStruct((1, D_MODEL), F32)],
        compiler_params=_params(("arbitrary",)), name="final_loss")(x, out, wf, target)


def _group_lanes(v):
    return jnp.pad(v.reshape(SSD_GROUPS, 1, SSD_HPG), ((0, 0), (0, 0), (0, LANES - SSD_HPG)))


def _ungroup_lanes(v):
    return v[:, :, :SSD_HPG].reshape(1, N_DT)


def _row_slabs(pieces, rows):
    total = sum(p.shape[0] for p in pieces)
    assert total % rows == 0
    slabs = []
    for lo in range(0, total, rows):
        parts, off = [], 0
        for p in pieces:
            a, b = max(lo, off), min(lo + rows, off + p.shape[0])
            if a < b:
                parts.append(p[a - off:b - off])
            off += p.shape[0]
        slabs.append(parts[0] if len(parts) == 1 else jnp.concatenate(parts, axis=0))
    return jnp.stack(slabs)


def _local_step(x, target, nb, s, w_in_t, conv_w, conv_b, norm_w, dt_bias, a_log, d_skip,
                ssm_norm_w, out_weights, final_norm_w, between=None):
    t = nb * s
    w_g_t = w_in_t[GATE_COL0:]
    w_dt_t = jnp.pad(w_in_t[DT_COL0:GATE_COL0], ((0, LANES - N_DT), (0, 0)))
    h, r1 = _rms_fwd(x, norm_w)
    proj = _matmul(h, w_in_t, nt=True, n=DT_COL0, out_dtype=BF16, name="proj_main", tm=MM_TALL)
    proj_g = _matmul(h, w_g_t, nt=True, out_dtype=BF16, name="proj_gate", tm=MM_TALL)
    dt_raw = _matmul(h, w_dt_t, nt=True, name="proj_dt")
    dtr = jnp.pad(dt_raw[:, :N_DT].reshape(t, SSD_GROUPS, SSD_HPG).transpose(1, 0, 2),
                  ((0, 0), (0, 0), (0, LANES - SSD_HPG)))
    bias_g, alog_g, dsk_g = _group_lanes(dt_bias), _group_lanes(a_log), _group_lanes(d_skip)

    o, att_a, att_sg = _attn_fwd(proj, nb, s)
    xbc = _conv_fwd(proj, conv_w, conv_b, nb, s)
    y, hs = _ssd_fwd(xbc, dtr, bias_g, alog_g, dsk_g, nb, s)
    ya_in, ys_in = _post_fwd(o, proj, y, ssm_norm_w)
    w_attn_out, w_ssm_out, w_o = out_weights(ys_in) if callable(out_weights) else out_weights
    ya = _matmul(ya_in, w_attn_out, out_dtype=BF16, name="attn_out", tm=MM_TALL)
    ys = _matmul(ys_in, w_ssm_out, out_dtype=BF16, name="ssm_out", tm=MM_TALL)
    merged = _merge_fwd(ya, ys, proj_g)
    out = _matmul(merged, w_o, name="out_proj", tm=MM_TALL)
    loss, dx2, d_final_w = _final(x, out, final_norm_w, target)

    dx2b = _bf(dx2)
    dmerged = _matmul(dx2b, w_o, nt=True, out_dtype=BF16, name="d_merged", tm=MM_TALL)
    d_w_o = _matmul(merged, dx2b, ta=True, name="d_w_o")
    dya, dys, dgate = _merge_bwd(dmerged, ya, ys, proj_g)
    d_w_attn_out = _matmul(ya_in, dya, ta=True, name="d_w_attn_out")
    d_w_ssm_out = _matmul(ys_in, dys, ta=True, name="d_w_ssm_out")
    dya_in = _matmul(dya, w_attn_out, nt=True, out_dtype=BF16, name="d_ya_in", tm=MM_TALL)
    dys_in = _matmul(dys, w_ssm_out, nt=True, name="d_ys_in", tm=MM_TALL)
    do, dy, d_ssm_norm_w, dproj = _post_bwd(dya_in, o, proj, dys_in, y, ssm_norm_w, lax.empty(proj.shape, BF16))
    dq, dk, dv = _attn_bwd(proj, do, att_a, att_sg, nb, s)
    dxs_act, db_act, dc_act, ddtr, dbias_g, dalog_g, ddsk_g = _ssd_bwd(
        xbc, dtr, bias_g, alog_g, dsk_g, hs, dy, nb, s)
    dproj, dwx, dbx = _conv_bwd(proj, dxs_act, conv_w, conv_b, dproj, nb, s, col0=0, name="conv_bwd_x")
    dproj, dwb, dbb = _conv_bwd(proj, db_act, conv_w, conv_b, dproj, nb, s, col0=SSD_WIDTH, name="conv_bwd_b")
    dproj, dwc, dbc = _conv_bwd(proj, dc_act, conv_w, conv_b, dproj, nb, s, col0=SSD_WIDTH + 512,
                                name="conv_bwd_c")
    for i, piece in enumerate((dq, dk, dv)):
        dproj = lax.dynamic_update_slice(dproj, piece, (0, i * SB_WIDTH))
    ddt = jnp.pad(_bf(ddtr[:, :, :SSD_HPG].transpose(1, 0, 2).reshape(t, N_DT)), ((0, 0), (0, LANES - N_DT)))
    wt = dict(ta=True, out_dtype=BF16)
    d_w_in_t = _row_slabs(
        [_matmul(dproj, h, name="d_w_main", **wt), _matmul(ddt, h, name="d_w_dt", **wt)[:N_DT],
         _matmul(dgate, h, name="d_w_gate", **wt)], SHARD)
    grads = dict(
        w_in_t=d_w_in_t,
        conv_w=jnp.concatenate([dwx, dwb, dwc], axis=1), conv_b=jnp.concatenate([dbx, dbb, dbc], axis=1),
        dt_bias=_ungroup_lanes(dbias_g), a_log=_ungroup_lanes(dalog_g), d_skip=_ungroup_lanes(ddsk_g),
        ssm_norm_w=d_ssm_norm_w, w_attn_out=d_w_attn_out, w_ssm_out=d_w_ssm_out, w_o=d_w_o,
        final_norm_w=d_final_w)
    extra = None
    if between is not None:
        zero, extra = between(grads)
        ddt = ddt + _bf(zero[0:1, 0:1])

    dh = _matmul(ddt, w_dt_t, name="d_h_dt")
    dh = _matmul(dgate, w_g_t, add=dh, name="d_h_gate")
    dh = _matmul(dproj, w_in_t, add=dh, name="d_h", tk=DT_COL0 // 4)
    grad_x, grads["norm_w"] = _rms_bwd(dh, x, r1, norm_w, dx2)
    return loss, grad_x, grads, extra


N_CHIP = N_DEV // 2


def _pair_exchange(scatter, bcast, *, name):
    arrays = list(scatter) + list(bcast)
    ns, n = len(scatter), len(arrays)
    n_sem = ns * N_CHIP + (n - ns) * (N_DEV - 1)

    def body(*refs):
        ins, outs = refs[:n], refs[n:2 * n]
        send_sems, recv_sems, local_sems = refs[2 * n:]
        x, y, c = lax.axis_index("x"), lax.axis_index("y"), lax.axis_index("c")
        me = 4 * x + 2 * y + c
        copies, k = [], 0
        for a in range(ns):
            for q in range(N_CHIP):
                cp = pltpu.make_async_remote_copy(
                    src_ref=ins[a].at[2 * q + 1 - c], dst_ref=outs[a].at[q],
                    send_sem=send_sems.at[k], recv_sem=recv_sems.at[k],
                    device_id=(x, y, 1 - c), device_id_type=pl.DeviceIdType.MESH)
                cp.start()
                copies.append(cp)
                k += 1
        for a in range(ns, n):
            cp = pltpu.make_async_copy(ins[a], outs[a].at[me], local_sems.at[a - ns])
            cp.start()
            copies.append(cp)
            for r in range(1, N_DEV):
                tx = 1 - x if (r >> 2) & 1 else x
                ty = 1 - y if (r >> 1) & 1 else y
                tc = 1 - c if r & 1 else c
                cp = pltpu.make_async_remote_copy(
                    src_ref=ins[a], dst_ref=outs[a].at[me], send_sem=send_sems.at[k], recv_sem=recv_sems.at[k],
                    device_id=(tx, ty, tc), device_id_type=pl.DeviceIdType.MESH)
                cp.start()
                copies.append(cp)
                k += 1
        for cp in copies:
            cp.wait()

    out_shape = [jax.ShapeDtypeStruct((N_CHIP,) + v.shape[1:] if i < ns else (N_DEV,) + v.shape, v.dtype)
                 for i, v in enumerate(arrays)]
    return pl.pallas_call(
        body, in_specs=[pl.BlockSpec(memory_space=pl.ANY)] * n,
        out_specs=[pl.BlockSpec(memory_space=pl.ANY)] * n, out_shape=out_shape,
        scratch_shapes=[pltpu.SemaphoreType.DMA((n_sem,)), pltpu.SemaphoreType.DMA((n_sem,)),
                        pltpu.SemaphoreType.DMA((max(n - ns, 1),))],
        compiler_params=pltpu.CompilerParams(has_side_effects=True), name=name)(*arrays)


def _pair_add(mine, recv, core, *, name, tc=None):
    _, r, c = recv.shape
    tc = c if tc is None else tc
    assert c % tc == 0

    def body(core_ref, m_ref, r_ref, o_ref):
        o_ref[...] = (m_ref[...].astype(F32) + r_ref[...].astype(F32)).astype(o_ref.dtype)

    spec = pl.BlockSpec((None, r, tc), lambda q, i, core_ref: (q, 0, i))
    return pl.pallas_call(
        body, grid_spec=pltpu.PrefetchScalarGridSpec(
            num_scalar_prefetch=1, grid=(N_CHIP, c // tc),
            in_specs=[pl.BlockSpec((None, r, tc), lambda q, i, core_ref: (2 * q + core_ref[0], 0, i)), spec],
            out_specs=spec),
        out_shape=jax.ShapeDtypeStruct(recv.shape, recv.dtype),
        compiler_params=_params(("parallel", "parallel")), name=name)(core, mine, recv)


_HBM = pl.BlockSpec(memory_space=pltpu.HBM)
_SEM = pl.BlockSpec(memory_space=pltpu.SEMAPHORE)
_DATAFLOW = pltpu.SideEffectType.DATAFLOW_SIDE_EFFECTING
_PEERS = {"chips": N_CHIP - 1, "all": N_DEV - 1}


def _split_copies(pattern, srcs, lands, send_sems, recv_sems):
    x, y, c = lax.axis_index("x"), lax.axis_index("y"), lax.axis_index("c")
    per = _PEERS[pattern]
    copies = []
    for a in range(len(srcs)):
        for r in range(1, per + 1):
            if pattern == "chips":
                tx = 1 - x if (r >> 1) & 1 else x
                ty = 1 - y if r & 1 else y
                tc = c
                src, dst = srcs[a].at[2 * tx + ty], lands[a].at[2 * x + y]
            else:
                tx = 1 - x if (r >> 2) & 1 else x
                ty = 1 - y if (r >> 1) & 1 else y
                tc = 1 - c if r & 1 else c
                src, dst = srcs[a], lands[a].at[4 * x + 2 * y + c]
            copies.append(pltpu.make_async_remote_copy(
                src_ref=src, dst_ref=dst, send_sem=send_sems.at[a * per + r - 1],
                recv_sem=recv_sems.at[a * per + r - 1], device_id=(tx, ty, tc), device_id_type=pl.DeviceIdType.MESH))
    return copies


def _split_start(pattern, arrays, *, name):
    n = len(arrays)
    n_sem = n * _PEERS[pattern]

    def body(*refs):
        srcs, lands = refs[:n], refs[n:2 * n]
        send_sems, recv_sems = refs[2 * n], refs[2 * n + 1]
        zero = refs[-1]
        for cp in _split_copies(pattern, srcs, lands, send_sems, recv_sems):
            cp.start()
        zero[...] = jnp.zeros_like(zero)

    src_t = [pltpu.HBM(v.shape, v.dtype) for v in arrays]
    land_t = [pltpu.HBM(v.shape if pattern == "chips" else (N_DEV,) + v.shape, v.dtype) for v in arrays]
    outs = pl.pallas_call(
        body, name=name,
        out_shape=(pltpu.SemaphoreType.DMA((n_sem,)), pltpu.SemaphoreType.DMA((n_sem,)), *src_t, *land_t,
                   jax.ShapeDtypeStruct((8, LANES), F32)),
        in_specs=[_HBM] * (2 * n),
        out_specs=(_SEM, _SEM, *[_HBM] * (2 * n), pl.BlockSpec(memory_space=pltpu.VMEM)),
        input_output_aliases={i: 2 + i for i in range(2 * n)},
        compiler_params=pltpu.CompilerParams(has_side_effects=_DATAFLOW))(
            *[pltpu.with_memory_space_constraint(v, pltpu.HBM) for v in arrays],
            *[pltpu.with_memory_space_constraint(lax.empty(t.shape, t.dtype), pltpu.HBM) for t in land_t])
    return outs[0], outs[1], outs[2:2 + n], outs[2 + n:2 + 2 * n], outs[-1]


def _split_wait(pattern, send_sems, recv_sems, srcs, lands, after, *, name):
    n = len(srcs)

    def body(*refs):
        for cp in _split_copies(pattern, refs[:n], refs[n:2 * n], refs[2 * n], refs[2 * n + 1]):
            cp.wait_send()
            cp.wait_recv()

    outs = pl.pallas_call(
        body, name=name, out_shape=[pltpu.HBM(v.shape, v.dtype) for v in (*srcs, *lands)],
        in_specs=[_HBM] * (2 * n) + [_SEM, _SEM, pl.BlockSpec(memory_space=pl.ANY)],
        out_specs=[_HBM] * (2 * n), input_output_aliases={i: i for i in range(2 * n)},
        compiler_params=pltpu.CompilerParams(has_side_effects=_DATAFLOW))(
            *srcs, *lands, send_sems, recv_sems, after)
    return outs[:n], outs[n:]


def _own_slab(lands, srcs, index, whole):
    out = []
    for land, src in zip(lands, srcs):
        own = src[None] if whole else lax.dynamic_slice(src, (index, 0, 0), (1,) + src.shape[1:])
        out.append(lax.dynamic_update_slice(land, own, (index, 0, 0)))
    return out


def _gather(arrays, *, name):
    n = len(arrays)
    per = N_DEV - 1

    def body(*refs):
        ins, outs = refs[:n], refs[n:2 * n]
        send_sems, recv_sems, local_sems = refs[2 * n:]
        x, y, c = lax.axis_index("x"), lax.axis_index("y"), lax.axis_index("c")
        me, sibling = (x, y, c), (x, y, 1 - c)
        chips = [(1 - x, y), (x, 1 - y), (1 - x, 1 - y)]

        def slab(a, block):
            return outs[a].at[4 * block[0] + 2 * block[1] + block[2]]

        def copy(a, k, block, to, src=None):
            rows = slab(a, block)
            return pltpu.make_async_remote_copy(
                src_ref=rows if src is None else src, dst_ref=rows,
                send_sem=send_sems.at[a * per + k], recv_sem=recv_sems.at[a * per + k],
                device_id=to, device_id_type=pl.DeviceIdType.MESH)

        started = []
        for a in range(n):
            mine = pltpu.make_async_copy(ins[a], slab(a, me), local_sems.at[a])
            mine.start()
            started.append(mine)
        for a in range(n):
            first = [copy(a, 1 + j, me, (*chip, c), src=ins[a]) for j, chip in enumerate(chips)]
            first.append(copy(a, 0, me, sibling, src=ins[a]))
            for cp in first:
                cp.start()
            started += first
        for j, chip in enumerate(chips):
            for a in range(n):
                copy(a, 1 + j, (*chip, c), me).wait_recv()
                passed = copy(a, 4 + j, (*chip, c), sibling)
                passed.start()
                started.append(passed)
        for a in range(n):
            copy(a, 0, sibling, me).wait_recv()
            for j, chip in enumerate(chips):
                copy(a, 4 + j, (*chip, 1 - c), me).wait_recv()
        for cp in started[:n]:
            cp.wait()
        for cp in started[n:]:
            cp.wait_send()

    return pl.pallas_call(
        body, in_specs=[pl.BlockSpec(memory_space=pl.ANY)] * n,
        out_specs=[pl.BlockSpec(memory_space=pl.ANY)] * n,
        out_shape=[jax.ShapeDtypeStruct((N_DEV,) + v.shape, v.dtype) for v in arrays],
        scratch_shapes=[pltpu.SemaphoreType.DMA((n * per,)), pltpu.SemaphoreType.DMA((n * per,)),
                        pltpu.SemaphoreType.DMA((n,))],
        compiler_params=pltpu.CompilerParams(has_side_effects=True), name=name)(*arrays)


def _adamw(parts, w, m, v, *, name, tc=None):
    r, c = w.shape
    n_parts = parts.shape[0]
    tc = c if tc is None else tc
    assert c % tc == 0

    def body(p_ref, w_ref, m_ref, v_ref, g_ref, d_ref, nm_ref, nv_ref):
        g = p_ref[0].astype(F32)
        for d in range(1, n_parts):
            g = g + p_ref[d].astype(F32)
        mm = ADAM_B1 * m_ref[...] + (1.0 - ADAM_B1) * g
        vv = ADAM_B2 * v_ref[...] + (1.0 - ADAM_B2) * (g * g)
        m_hat = mm / (1.0 - ADAM_B1 ** ADAM_STEP)
        v_hat = vv / (1.0 - ADAM_B2 ** ADAM_STEP)
        g_ref[...] = g
        d_ref[...] = -ADAM_LR * (m_hat / (jnp.sqrt(v_hat) + ADAM_EPS) + ADAM_WD * w_ref[...])
        nm_ref[...] = mm
        nv_ref[...] = vv

    spec = pl.BlockSpec((r, tc), lambda i: (0, i))
    out = jax.ShapeDtypeStruct((r, c), F32)
    return pl.pallas_call(
        body, grid=(c // tc,),
        in_specs=[pl.BlockSpec((n_parts, r, tc), lambda i: (0, 0, i)), spec, spec, spec],
        out_specs=[spec] * 4, out_shape=[out] * 4,
        compiler_params=_params(("parallel",)), name=name)(parts, w, m, v)


SMALL = (("norm_w", D_MODEL), ("conv_b", CONV_DIM), ("dt_bias", N_DT), ("a_log", N_DT),
         ("d_skip", N_DT), ("ssm_norm_w", SSD_WIDTH), ("final_norm_w", D_MODEL))
SMALL_USED = sum(size for _, size in SMALL)
SMALL_ROWS = 64
SHARD = D_PROJ // N_DEV


def _pack_small(vals, scalar=None):
    parts = [vals[k].reshape(-1).astype(F32) for k, _ in SMALL]
    if scalar is not None:
        parts.append(scalar.reshape(1))
    flat = jnp.concatenate(parts)
    return jnp.pad(flat, (0, SMALL_ROWS * LANES - flat.shape[0])).reshape(SMALL_ROWS, LANES)


def _unpack_small(packed, shapes):
    flat = packed.reshape(-1)
    out, off = {}, 0
    for k, size in SMALL:
        out[k] = flat[off:off + size].reshape(shapes[k])
        off += size
    return out


def kernel(x, norm_w, w_in, conv_w, conv_b, dt_bias, a_log, d_skip, ssm_norm_w, w_attn_out, w_ssm_out, w_o, final_norm_w, loss_target, m_norm_w, m_w_in, m_conv_w, m_conv_b, m_dt_bias, m_a_log, m_d_skip, m_ssm_norm_w, m_w_attn_out, m_w_ssm_out, m_w_o, m_final_norm_w, v_norm_w, v_w_in, v_conv_w, v_conv_b, v_dt_bias, v_a_log, v_d_skip, v_ssm_norm_w, v_w_attn_out, v_w_ssm_out, v_w_o, v_final_norm_w):
    nb, s, _ = x.shape
    t = nb * s
    weights = dict(norm_w=norm_w, w_in=w_in, conv_w=conv_w, conv_b=conv_b, dt_bias=dt_bias, a_log=a_log,
                   d_skip=d_skip, ssm_norm_w=ssm_norm_w, w_attn_out=w_attn_out, w_ssm_out=w_ssm_out,
                   w_o=w_o, final_norm_w=final_norm_w)
    moms = dict(norm_w=m_norm_w, w_in=m_w_in, conv_w=m_conv_w, conv_b=m_conv_b, dt_bias=m_dt_bias,
                a_log=m_a_log, d_skip=m_d_skip, ssm_norm_w=m_ssm_norm_w, w_attn_out=m_w_attn_out,
                w_ssm_out=m_w_ssm_out, w_o=m_w_o, final_norm_w=m_final_norm_w)
    vels = dict(norm_w=v_norm_w, w_in=v_w_in, conv_w=v_conv_w, conv_b=v_conv_b, dt_bias=v_dt_bias,
                a_log=v_a_log, d_skip=v_d_skip, ssm_norm_w=v_ssm_norm_w, w_attn_out=v_w_attn_out,
                w_ssm_out=v_w_ssm_out, w_o=v_w_o, final_norm_w=v_final_norm_w)

    tr_ = lambda a: jnp.transpose(a[0])
    g_in, g_conv = _gather([_bf(tr_(w_in)), conv_w[0]], name="gather_weights")
    conv_full = g_conv.transpose(1, 0, 2).reshape(CONV_K, CONV_DIM)
    o_send, o_recv, o_srcs, o_lands, o_zero = _split_start(
        "all", [_bf(w_attn_out[0]), _bf(w_ssm_out[0]), _bf(w_o[0])], name="gather_out_weights_start")

    core = lax.axis_index("c").astype(jnp.int32).reshape(1)
    chip = 2 * lax.axis_index("x") + lax.axis_index("y")
    me = 2 * chip + lax.axis_index("c")

    def out_weights(after):
        srcs, lands = _split_wait("all", o_send, o_recv, o_srcs, o_lands, after, name="gather_out_weights_wait")
        g_wa, g_ws, g_wo = _own_slab(lands, srcs, me, True)
        return (g_wa.reshape(SB_WIDTH, D_MODEL), g_ws.reshape(SSD_WIDTH, D_MODEL), g_wo.reshape(D_MODEL, D_MODEL))

    def send_weight_grads(grads):
        slabs = [grads["w_in_t"],
                 grads["conv_w"].reshape(CONV_K, N_DEV, CONV_DIM // N_DEV).transpose(1, 0, 2),
                 _bf(grads["w_attn_out"].reshape(N_DEV, SB_WIDTH // N_DEV, D_MODEL)),
                 _bf(grads["w_ssm_out"].reshape(N_DEV, SSD_WIDTH // N_DEV, D_MODEL)),
                 _bf(grads["w_o"].reshape(N_DEV, D_MODEL // N_DEV, D_MODEL))]
        recv = _pair_exchange(slabs, [], name="exchange_pairs")
        names = ("w_in", "conv_w", "w_attn_out", "w_ssm_out", "w_o")
        sums = [_pair_add(a, b, core, name="pair_add_" + k, tc=4 * LANES if k == "w_in" else None)
                for k, a, b in zip(names, slabs, recv)]
        send_sems, recv_sems, srcs, lands, zero = _split_start("chips", sums, name="exchange_chips_start")
        return zero, (send_sems, recv_sems, srcs, lands)

    loss_part, grad_x, grads, (send_sems, recv_sems, srcs, lands) = _local_step(
        x.reshape(t, D_MODEL), loss_target.reshape(t, D_MODEL), nb, s, g_in.reshape(D_PROJ, D_MODEL),
        conv_full, conv_b, norm_w + o_zero[0:1, 0:1], dt_bias, a_log, d_skip, ssm_norm_w, out_weights,
        final_norm_w.reshape(1, D_MODEL), between=send_weight_grads)

    (p_small,) = _pair_exchange([], [_pack_small(grads, loss_part[0, 0])], name="exchange_small")
    srcs, lands = _split_wait("chips", send_sems, recv_sems, srcs, lands, grad_x, name="exchange_chips_wait")
    p_in, p_conv, p_wa, p_ws, p_wo = _own_slab(lands, srcs, chip, False)

    res = {}
    res["w_in"] = [a.T for a in _adamw(p_in, tr_(w_in), tr_(m_w_in), tr_(v_w_in), name="adamw_w_in", tc=2 * LANES)]
    res["conv_w"] = _adamw(p_conv, conv_w[0], m_conv_w[0], v_conv_w[0], name="adamw_conv_w")
    res["w_attn_out"] = _adamw(p_wa, w_attn_out[0], m_w_attn_out[0], v_w_attn_out[0], name="adamw_w_attn_out")
    res["w_ssm_out"] = _adamw(p_ws, w_ssm_out[0], m_w_ssm_out[0], v_w_ssm_out[0], name="adamw_w_ssm_out")
    res["w_o"] = _adamw(p_wo, w_o[0], m_w_o[0], v_w_o[0], name="adamw_w_o")
    sm = _adamw(p_small, _pack_small(weights), _pack_small(moms), _pack_small(vels), name="adamw_small")
    loss = sm[0].reshape(-1)[SMALL_USED]
    shapes = {k: weights[k].shape for k, _ in SMALL}
    sm = [_unpack_small(v, shapes) for v in sm]
    for k, _ in SMALL:
        res[k] = tuple(part[k] for part in sm)

    order = ("norm_w", "w_in", "conv_w", "conv_b", "dt_bias", "a_log", "d_skip", "ssm_norm_w",
             "w_attn_out", "w_ssm_out", "w_o", "final_norm_w")
    outs = [loss, grad_x.reshape(nb, s, D_MODEL)]
    for i in range(4):
        outs += [res[k][i].reshape(weights[k].shape) for k in order]
    return tuple(outs)
```

```python
import functools

import jax
import jax.numpy as jnp
from jax import lax
from jax.experimental import pallas as pl
from jax.experimental.pallas import tpu as pltpu

F32 = jnp.float32
BF16 = jnp.bfloat16

D_MODEL = 1024
SB_WIDTH = 1024
HEAD_DIM = 64
SSD_WIDTH = 2048
SSD_GROUPS = 4
SSD_HPG = 8
SSD_STATE = 128
SSD_CHUNK = 128
CONV_K = 4
CONV_DIM = 3072
N_DT = 32
D_PROJ = 11296
DT_COL0 = 9216
GATE_COL0 = DT_COL0 + N_DT
EPS = 1e-6
N_DEV = 8

ADAM_LR = 0.001
ADAM_B1 = 0.9
ADAM_B2 = 0.999
ADAM_EPS = 1e-08
ADAM_WD = 0.01
ADAM_STEP = 10

LANES = 128
VMEM_LIMIT = 56 * 1024 * 1024

CB_Q, CB_K, CB_V, CB_ZA, CB_ZS, CB_XBC = 0, 8, 16, 24, 32, 48

NT_DIMS = (((1,), (1,)), ((), ()))
TN_DIMS = (((0,), (0,)), ((), ()))


def _params(sem):
    return pltpu.CompilerParams(dimension_semantics=sem, vmem_limit_bytes=VMEM_LIMIT)


def _dot(a, b):
    return jnp.dot(a, b, preferred_element_type=F32)


def _dot_nt(a, b):
    return lax.dot_general(a, b, NT_DIMS, preferred_element_type=F32)


def _dot_tn(a, b):
    return lax.dot_general(a, b, TN_DIMS, preferred_element_type=F32)


def _bf(a):
    return a.astype(BF16)


def _sigmoid(x):
    return 0.5 + 0.5 * jnp.tanh(0.5 * x)


def _softplus(x):
    return jnp.maximum(x, 0.0) + jnp.log(1.0 + jnp.exp(-jnp.abs(x)))


def _trunc_split(a):
    bits = lax.bitcast_convert_type(a, jnp.uint32) & jnp.uint32(0xFFFF0000)
    hi = lax.bitcast_convert_type(bits, F32)
    return _bf(hi), _bf(a - hi)


MM_TILE = 1024
MM_TK = 2048
MM_TALL = 2048


def _matmul(a, b, *, name, nt=False, ta=False, add=None, out_dtype=F32, tm=MM_TILE, tn=MM_TILE, tk=MM_TK, n=None):
    k, m = a.shape if ta else a.shape[::-1]
    n = (b.shape[0] if nt else b.shape[1]) if n is None else n
    tm, tn, tk = min(tm, m), min(tn, n), min(tk, k)
    assert m % tm == 0 and n % tn == 0 and k % tk == 0, (name, a.shape, b.shape)
    assert not (ta and nt)
    nk = k // tk
    has_add = add is not None

    def body(*refs):
        a_ref, b_ref = refs[:2]
        add_ref = refs[2] if has_add else None
        o_ref = refs[3 if has_add else 2]
        acc_ref = refs[-1] if nk > 1 else None
        kk = pl.program_id(2)
        av, bv = _bf(a_ref[...]), _bf(b_ref[...])
        d = _dot_nt(av, bv) if nt else _dot_tn(av, bv) if ta else _dot(av, bv)
        if nk == 1:
            o_ref[...] = ((d + add_ref[...]) if has_add else d).astype(out_dtype)
            return

        @pl.when(kk == 0)
        def _():
            acc_ref[...] = jnp.zeros_like(acc_ref)

        acc_ref[...] += d

        @pl.when(kk == nk - 1)
        def _():
            r = acc_ref[...]
            if has_add:
                r = r + add_ref[...]
            o_ref[...] = r.astype(out_dtype)

    in_specs = [pl.BlockSpec((tk, tm), lambda j, i, kk: (kk, i)) if ta
                else pl.BlockSpec((tm, tk), lambda j, i, kk: (i, kk)),
                pl.BlockSpec((tn, tk), lambda j, i, kk: (j, kk)) if nt
                else pl.BlockSpec((tk, tn), lambda j, i, kk: (kk, j))]
    args = [a, b]
    if has_add:
        in_specs.append(pl.BlockSpec((tm, tn), lambda j, i, kk: (i, j)))
        args.append(add)
    return pl.pallas_call(
        body, grid=(n // tn, m // tm, nk), in_specs=in_specs,
        out_specs=pl.BlockSpec((tm, tn), lambda j, i, kk: (i, j)),
        out_shape=jax.ShapeDtypeStruct((m, n), out_dtype),
        scratch_shapes=[pltpu.VMEM((tm, tn), F32)] if nk > 1 else [],
        compiler_params=_params(("parallel", "parallel", "arbitrary")), name=name)(*args)


def _rms_fwd(x, w, *, tm=512):
    t = x.shape[0]

    def body(x_ref, w_ref, h_ref, r_ref):
        xv = x_ref[...]
        r = lax.rsqrt(jnp.mean(xv * xv, axis=-1, keepdims=True) + EPS)
        h_ref[...] = _bf(xv * r * w_ref[...])
        r_ref[...] = r

    row = lambda i: (i, 0)
    return pl.pallas_call(
        body, grid=(t // tm,),
        in_specs=[pl.BlockSpec((tm, D_MODEL), row), pl.BlockSpec((1, D_MODEL), lambda i: (0, 0))],
        out_specs=[pl.BlockSpec((tm, D_MODEL), row), pl.BlockSpec((tm, 1), row)],
        out_shape=[jax.ShapeDtypeStruct((t, D_MODEL), BF16), jax.ShapeDtypeStruct((t, 1), F32)],
        compiler_params=_params(("parallel",)), name="rms_fwd")(x, w)


def _rms_bwd(dh, x, r, w, dres, *, tm=512):
    t = x.shape[0]

    def body(dh_ref, x_ref, r_ref, w_ref, dres_ref, dx_ref, dw_ref):
        @pl.when(pl.program_id(0) == 0)
        def _():
            dw_ref[...] = jnp.zeros_like(dw_ref)

        nrm = x_ref[...] * r_ref[...]
        dhv = dh_ref[...]
        dw_ref[...] += jnp.sum(dhv * nrm, axis=0, keepdims=True)
        dn = dhv * w_ref[...]
        dx = r_ref[...] * (dn - nrm * jnp.mean(dn * nrm, axis=-1, keepdims=True))
        dx_ref[...] = dx + dres_ref[...]

    row = lambda i: (i, 0)
    fix = lambda i: (0, 0)
    return pl.pallas_call(
        body, grid=(t // tm,),
        in_specs=[pl.BlockSpec((tm, D_MODEL), row), pl.BlockSpec((tm, D_MODEL), row),
                  pl.BlockSpec((tm, 1), row), pl.BlockSpec((1, D_MODEL), fix),
                  pl.BlockSpec((tm, D_MODEL), row)],
        out_specs=[pl.BlockSpec((tm, D_MODEL), row), pl.BlockSpec((1, D_MODEL), fix)],
        out_shape=[jax.ShapeDtypeStruct((t, D_MODEL), F32), jax.ShapeDtypeStruct((1, D_MODEL), F32)],
        compiler_params=_params(("arbitrary",)), name="rms_bwd")(dh, x, r, w, dres)


ATT_BLK = 256


def _attn_masks(blk):
    r_io = lax.broadcasted_iota(jnp.int32, (blk, blk), 0)
    c_io = lax.broadcasted_iota(jnp.int32, (blk, blk), 1)
    return r_io, c_io


def _attn_tiles(nq):
    return nq * (nq + 1) // 2


def _attn_fwd(proj, nb, s):
    blk = min(ATT_BLK, s)
    nq = s // blk
    nt = _attn_tiles(nq)
    scale = HEAD_DIM ** -0.5

    def body(q_ref, k_ref, v_ref, o_ref, a_out, sg_out, q0_s, q1_s, kb_s, v0_s, v1_s, z_s, a_s, acc_s, cr_s):
        head0 = lax.broadcasted_iota(jnp.int32, (s, LANES), 1) < HEAD_DIM
        r_io, c_io = _attn_masks(blk)
        tri = c_io < r_io
        tri2 = jnp.concatenate([tri, tri], axis=0)
        u_gt = _bf(r_io > c_io)
        u2 = jnp.concatenate([u_gt, u_gt], axis=0)
        qv = q_ref[...] * scale
        q0_s[...] = _bf(jnp.where(head0, qv, 0.0))
        q1_s[...] = _bf(jnp.where(head0, 0.0, qv))
        kb_s[...] = _bf(k_ref[...])
        vv = v_ref[...]
        v0_s[...] = _bf(jnp.where(head0, vv, 0.0))
        v1_s[...] = _bf(jnp.where(head0, 0.0, vv))

        def scores(qcat, kblk, slot):
            k0 = pl.multiple_of(jnp.maximum(kblk, 0) * blk, blk)
            z_s[slot] = _dot_nt(qcat, kb_s[pl.ds(k0, blk), :])

        def weights(slot, tile, diag, zslot=None):
            z = z_s[slot if zslot is None else zslot]
            cr = cr_s[...]
            sp = _softplus(z)
            spm = jnp.where(tri2, sp, 0.0) if diag else sp
            hi, lo = _trunc_split(spm)
            ps = _dot(jnp.concatenate([hi, lo], axis=1), u2)
            lb = z - sp
            a = jnp.exp(lb - (ps + cr))
            if diag:
                a = jnp.where(tri2, a, 0.0)
            ab = _bf(a)
            a_s[slot] = ab
            a_out[tile] = ab
            sg_out[tile] = _bf(jnp.exp(lb))
            cr_s[...] = cr + ps[:, 0:1] + spm[:, 0:1]

        def apply(slot, k0):
            a = a_s[slot]
            vcat = jnp.concatenate([v0_s[pl.ds(k0, blk), :], v1_s[pl.ds(k0, blk), :]], axis=0)
            acc_s[...] += _dot(jnp.concatenate([a[:blk], a[blk:]], axis=1), vcat)

        def queries(qi):
            q0 = pl.multiple_of(qi * blk, blk)
            return jnp.concatenate([q0_s[pl.ds(q0, blk), :], q1_s[pl.ds(q0, blk), :]], axis=0)

        def first_scores(qi):
            qcat = queries(qi)
            scores(qcat, qi, 2)
            scores(qcat, qi - 1, 3)

        first_scores(jnp.int32(0))

        def qblock(qi, _):
            q0 = pl.multiple_of(qi * blk, blk)
            tile0 = qi * (qi + 1) // 2
            qcat = queries(qi)
            z_s[1] = z_s[3]
            cr_s[...] = jnp.zeros_like(cr_s)
            acc_s[...] = jnp.zeros_like(acc_s)
            weights(0, tile0 + qi, True, zslot=2)
            first_scores(jnp.minimum(qi + 1, nq - 1))

            def step(j, slot):
                scores(qcat, qi - j - 1, 1 - slot)
                weights(slot, tile0 + qi - j, False)
                apply(1 - slot, pl.multiple_of((qi - j + 1) * blk, blk))

            def pair(i, _):
                step(2 * i + 1, 1)
                step(2 * i + 2, 0)
                return 0

            lax.fori_loop(0, qi // 2, pair, 0)

            @pl.when(qi % 2 == 1)
            def _():
                step(qi, 1)

            apply(qi & 1, 0)
            o_ref[pl.ds(q0, blk), :] = acc_s[...]
            return 0

        lax.fori_loop(0, nq, qblock, 0)

    t = nb * s
    tiles = pl.BlockSpec((None, None, nt, 2 * blk, blk), lambda b, p: (b, p, 0, 0, 0))
    tiles_shape = jax.ShapeDtypeStruct((nb, 8, nt, 2 * blk, blk), BF16)
    return pl.pallas_call(
        body, grid=(nb, 8),
        in_specs=[pl.BlockSpec((s, LANES), lambda b, p: (b, CB_Q + p)),
                  pl.BlockSpec((s, LANES), lambda b, p: (b, CB_K + p)),
                  pl.BlockSpec((s, LANES), lambda b, p: (b, CB_V + p))],
        out_specs=[pl.BlockSpec((s, LANES), lambda b, p: (b, p)), tiles, tiles],
        out_shape=[jax.ShapeDtypeStruct((t, SB_WIDTH), F32), tiles_shape, tiles_shape],
        scratch_shapes=[pltpu.VMEM((s, LANES), BF16)] * 5 + [
            pltpu.VMEM((4, 2 * blk, blk), F32), pltpu.VMEM((2, 2 * blk, blk), BF16),
            pltpu.VMEM((blk, LANES), F32), pltpu.VMEM((2 * blk, 1), F32)],
        compiler_params=_params(("parallel", "parallel")), name="attn_fwd")(proj, proj, proj)


def _attn_bwd(proj, do, a_all, sg_all, nb, s):
    blk = min(ATT_BLK, s)
    nq = s // blk
    nt = _attn_tiles(nq)
    scale = HEAD_DIM ** -0.5

    def body(q_ref, k_ref, v_ref, do_ref, a_in, sg_in, dq_ref, dk_ref, dv_ref,
             qt0_s, qt1_s, k0_s, k1_s, vb_s, d0_s, d1_s, dt0_s, dt1_s, da_s, dz_s, dkt_acc, dvt_acc, dq_acc, cw_s):
        head0 = lax.broadcasted_iota(jnp.int32, (s, LANES), 1) < HEAD_DIM
        r_io, c_io = _attn_masks(blk)
        tri = c_io < r_io
        tri2 = jnp.concatenate([tri, tri], axis=0)
        u_lt = _bf(r_io < c_io)
        head0_t = lax.broadcasted_iota(jnp.int32, (LANES, s), 0) < HEAD_DIM
        q_t = (q_ref[...].astype(F32) * scale).T
        qt0_s[...] = _bf(jnp.where(head0_t, q_t, 0.0))
        qt1_s[...] = _bf(jnp.where(head0_t, 0.0, q_t))
        kv = k_ref[...]
        k0_s[...] = _bf(jnp.where(head0, kv, 0.0))
        k1_s[...] = _bf(jnp.where(head0, 0.0, kv))
        vb_s[...] = _bf(v_ref[...])
        dov = do_ref[...]
        d0_s[...] = _bf(jnp.where(head0, dov, 0.0))
        d1_s[...] = _bf(jnp.where(head0, 0.0, dov))
        do_t = dov.astype(F32).T
        dt0_s[...] = _bf(jnp.where(head0_t, do_t, 0.0))
        dt1_s[...] = _bf(jnp.where(head0_t, 0.0, do_t))
        dkt_acc[...] = jnp.zeros_like(dkt_acc)
        dvt_acc[...] = jnp.zeros_like(dvt_acc)

        def qblock(qi, _):
            q0 = pl.multiple_of(qi * blk, blk)
            tile0 = qi * (qi + 1) // 2
            qs_ = pl.ds(q0, blk)
            dcat = jnp.concatenate([d0_s[qs_, :], d1_s[qs_, :]], axis=0)
            qcat_t = jnp.concatenate([qt0_s[:, qs_], qt1_s[:, qs_]], axis=1)
            dcat_t = jnp.concatenate([dt0_s[:, qs_], dt1_s[:, qs_]], axis=1)

            def scores(kblk, slot):
                da_s[slot] = _dot_nt(dcat, vb_s[pl.ds(pl.multiple_of(kblk * blk, blk), blk), :])

            def grads(slot, kblk, diag):
                cw = cw_s[...]
                w = a_in[tile0 + kblk].astype(F32) * da_s[slot]
                wex = _dot(_bf(w), u_lt)
                dz = w - sg_in[tile0 + kblk].astype(F32) * (w + (wex + cw))
                if diag:
                    dz = jnp.where(tri2, dz, 0.0)
                dz_s[slot] = _bf(dz)
                cw_s[...] = cw + wex[:, blk - 1:blk] + w[:, blk - 1:blk]

            def apply(slot, kblk):
                ks_ = pl.ds(pl.multiple_of(kblk * blk, blk), blk)
                dz = dz_s[slot]
                dq_acc[...] += _dot(jnp.concatenate([dz[:blk], dz[blk:]], axis=1),
                                    jnp.concatenate([k0_s[ks_, :], k1_s[ks_, :]], axis=0))
                dkt_acc[:, ks_] += _dot(qcat_t, dz)
                dvt_acc[:, ks_] += _dot(dcat_t, a_in[tile0 + kblk])

            def step(j, slot, diag):
                if not diag:
                    scores(j + 1, 1 - slot)
                grads(slot, j, diag)

                @pl.when(j > 0)
                def _():
                    apply(1 - slot, j - 1)

            scores(0, 0)
            dq_acc[...] = jnp.zeros_like(dq_acc)
            cw_s[...] = jnp.zeros_like(cw_s)

            def pair(i, _):
                step(2 * i, 0, False)
                step(2 * i + 1, 1, False)
                return 0

            lax.fori_loop(0, qi // 2, pair, 0)

            @pl.when(qi % 2 == 1)
            def _():
                step(qi - 1, 0, False)
                step(qi, 1, True)

            @pl.when(qi % 2 == 0)
            def _():
                step(qi, 0, True)

            apply(qi & 1, qi)
            dq_ref[qs_, :] = _bf(dq_acc[...] * scale)
            return 0

        lax.fori_loop(0, nq, qblock, 0)
        dk_ref[...] = _bf(dkt_acc[...].T)
        dv_ref[...] = _bf(dvt_acc[...].T)

    t = nb * s
    hp = lambda b, p: (b, p)
    out = jax.ShapeDtypeStruct((t, SB_WIDTH), BF16)
    tiles = pl.BlockSpec((None, None, nt, 2 * blk, blk), lambda b, p: (b, p, 0, 0, 0))
    return pl.pallas_call(
        body, grid=(nb, 8),
        in_specs=[pl.BlockSpec((s, LANES), lambda b, p: (b, CB_Q + p)),
                  pl.BlockSpec((s, LANES), lambda b, p: (b, CB_K + p)),
                  pl.BlockSpec((s, LANES), lambda b, p: (b, CB_V + p)),
                  pl.BlockSpec((s, LANES), hp), tiles, tiles],
        out_specs=[pl.BlockSpec((s, LANES), hp)] * 3,
        out_shape=[out, out, out],
        scratch_shapes=[pltpu.VMEM((LANES, s), BF16)] * 2 + [pltpu.VMEM((s, LANES), BF16)] * 5 + [
            pltpu.VMEM((LANES, s), BF16)] * 2 + [
            pltpu.VMEM((2, 2 * blk, blk), F32), pltpu.VMEM((2, 2 * blk, blk), BF16),
            pltpu.VMEM((LANES, s), F32), pltpu.VMEM((LANES, s), F32),
            pltpu.VMEM((blk, LANES), F32), pltpu.VMEM((2 * blk, 1), F32)],
        compiler_params=_params(("parallel", "parallel")), name="attn_bwd")(proj, proj, proj, do, a_all, sg_all)


CONV_TC = 256
CONV_ROWS = 64
HALO = 8


def _conv_rows(ws, w_ref, b_ref, n):
    taps = [ws[HALO - kk:HALO - kk + n, :] for kk in range(CONV_K)]
    y = b_ref[...] + w_ref[CONV_K - 1:CONV_K, :] * taps[0]
    for kk in range(1, CONV_K):
        y = y + w_ref[CONV_K - 1 - kk:CONV_K - kk, :] * taps[kk]
    return y, taps


def _conv_fwd(proj, conv_w, conv_b, nb, s):
    tc, rows = CONV_TC, min(CONV_ROWS, s)
    cb0 = CB_XBC * LANES // tc

    def body(x_ref, w_ref, b_ref, o_ref, xp, ws):
        xp[0:HALO, :] = jnp.zeros((HALO, tc), F32)
        xp[HALO:HALO + s, :] = x_ref[...].astype(F32)

        def chunk(i, _):
            r0 = pl.multiple_of(i * rows, rows)
            ws[...] = xp[pl.ds(r0, rows + HALO), :]
            y, _ = _conv_rows(ws, w_ref, b_ref, rows)
            o_ref[pl.ds(r0, rows), :] = y * _sigmoid(y)
            return 0

        lax.fori_loop(0, s // rows, chunk, 0)

    t = nb * s
    return pl.pallas_call(
        body, grid=(nb, CONV_DIM // tc),
        in_specs=[pl.BlockSpec((s, tc), lambda b, c: (b, cb0 + c)),
                  pl.BlockSpec((CONV_K, tc), lambda b, c: (0, c)),
                  pl.BlockSpec((1, tc), lambda b, c: (0, c))],
        out_specs=pl.BlockSpec((s, tc), lambda b, c: (b, c)),
        out_shape=jax.ShapeDtypeStruct((t, CONV_DIM), F32),
        scratch_shapes=[pltpu.VMEM((s + HALO, tc), F32), pltpu.VMEM((rows + HALO, tc), F32)],
        compiler_params=_params(("parallel", "parallel")), name="conv_fwd")(proj, conv_w, conv_b)


def _conv_bwd(proj, dact, conv_w, conv_b, dproj, nb, s, *, col0, name):
    width = dact.shape[1]
    tc, rows = min(CONV_TC, width), min(CONV_ROWS, s)
    cb0 = (CB_XBC * LANES + col0) // tc
    wb0 = col0 // tc

    def body(x_ref, d_ref, w_ref, b_ref, _, dx_ref, dw_ref, db_ref, xp, dp, dys, ws):
        @pl.when(pl.program_id(1) == 0)
        def _():
            dw_ref[...] = jnp.zeros_like(dw_ref)
            db_ref[...] = jnp.zeros_like(db_ref)

        n = rows + HALO
        pad = jnp.zeros((HALO, tc), F32)
        xp[0:HALO, :] = pad
        xp[HALO:HALO + s, :] = x_ref[...].astype(F32)
        xp[HALO + s:HALO + s + HALO, :] = pad
        dp[0:s, :] = d_ref[...]
        dp[s:s + HALO, :] = pad

        def chunk(i, acc):
            r0 = pl.multiple_of(i * rows, rows)
            ws[...] = xp[pl.ds(r0, n + HALO), :]
            y, taps = _conv_rows(ws, w_ref, b_ref, n)
            sg = _sigmoid(y)
            dy = dp[pl.ds(r0, n), :] * (sg * (1.0 + y * (1.0 - sg)))
            dys[...] = dy
            dx = w_ref[CONV_K - 1:CONV_K, :] * dy[0:rows]
            for kk in range(1, CONV_K):
                dx = dx + w_ref[CONV_K - 1 - kk:CONV_K - kk, :] * dys[kk:kk + rows, :]
            dx_ref[pl.ds(r0, rows), :] = _bf(dx)
            dyc = dy[0:rows]
            new = [acc[0] + jnp.sum(dyc, axis=0, keepdims=True)]
            for kk in range(CONV_K):
                new.append(acc[1 + kk] + jnp.sum(dyc * taps[kk][0:rows], axis=0, keepdims=True))
            return tuple(new)

        acc = lax.fori_loop(0, s // rows, chunk, (jnp.zeros((1, tc), F32),) * (1 + CONV_K))
        db_ref[...] += acc[0]
        for kk in range(CONV_K):
            dw_ref[CONV_K - 1 - kk:CONV_K - kk, :] += acc[1 + kk]

    t = nb * s
    return pl.pallas_call(
        body, grid=(width // tc, nb),
        in_specs=[pl.BlockSpec((s, tc), lambda c, b: (b, cb0 + c)),
                  pl.BlockSpec((s, tc), lambda c, b: (b, c)),
                  pl.BlockSpec((CONV_K, tc), lambda c, b: (0, wb0 + c)),
                  pl.BlockSpec((1, tc), lambda c, b: (0, wb0 + c)), pl.BlockSpec(memory_space=pl.ANY)],
        out_specs=[pl.BlockSpec((s, tc), lambda c, b: (b, cb0 + c)),
                   pl.BlockSpec((CONV_K, tc), lambda c, b: (0, c)),
                   pl.BlockSpec((1, tc), lambda c, b: (0, c))],
        out_shape=[jax.ShapeDtypeStruct(dproj.shape, dproj.dtype),
                   jax.ShapeDtypeStruct((CONV_K, width), F32),
                   jax.ShapeDtypeStruct((1, width), F32)],
        input_output_aliases={4: 0},
        scratch_shapes=[pltpu.VMEM((s + 2 * HALO, tc), F32), pltpu.VMEM((s + HALO, tc), F32),
                        pltpu.VMEM((rows + HALO, tc), F32), pltpu.VMEM((rows + 2 * HALO, tc), F32)],
        compiler_params=_params(("parallel", "arbitrary")), name=name)(proj, dact, conv_w, conv_b, dproj)


L = SSD_CHUNK
N_PAIR = SSD_HPG // 2


def _pair_lanes(v0, v1, lane):
    return jnp.where(lane < HEAD_DIM, v0, v1)


def _onehot(shape, row_shift, col_shift):
    row = lax.broadcasted_iota(jnp.int32, shape, 0)
    col = lax.broadcasted_iota(jnp.int32, shape, 1)
    return _bf(lax.shift_right_logical(row, row_shift) == lax.shift_right_logical(col, col_shift))


def _ssd_tables():
    sp = _onehot((LANES, N_PAIR * LANES), 0, 6)
    sf = _onehot((LANES, SSD_HPG * LANES), 0, 7)
    return (jnp.concatenate([sp] * 3, axis=0), jnp.concatenate([sf] * 3, axis=0),
            _onehot((N_PAIR * LANES, LANES), 6, 0), _onehot((SSD_HPG * LANES, LANES), 7, 0))


def _split3(a):
    a1 = _bf(a)
    r1 = a - a1.astype(F32)
    a2 = _bf(r1)
    return jnp.concatenate([a1, a2, _bf(r1 - a2.astype(F32))], axis=1)


def _split2(a):
    hi, lo = _trunc_split(a)
    return jnp.concatenate([hi, lo], axis=1)


def _tri_dot(tri_bf, a):
    a1 = _bf(a)
    r1 = a - a1.astype(F32)
    a2 = _bf(r1)
    parts = jnp.concatenate([a1, a2, _bf(r1 - a2.astype(F32))], axis=0)
    return _dot(jnp.concatenate([tri_bf] * 3, axis=1), parts)


HEAD_W = 4 * LANES + 2 * N_PAIR * LANES + SSD_HPG * LANES


def _ssd_head(dtr_ref, bias_ref, a_neg, c, tri_bf, spread_pair, spread_full, head_s, slot):
    raw = dtr_ref[pl.ds(pl.multiple_of(c * L, L), L), :] + bias_ref[...]
    dt = _softplus(raw)
    acs = _tri_dot(tri_bf, dt * a_neg)
    acs3 = _split3(acs)
    both = _dot(jnp.concatenate([_split3(dt), acs3], axis=0), spread_pair)
    parts = (raw, dt, acs, acs.T, both[:L], both[L:], _dot(acs3, spread_full))
    off = 0
    for p in parts:
        head_s[slot, :, off:off + p.shape[1]] = p
        off += p.shape[1]


def _ssd_head_load(head_s, slot):
    out, off = [], 0
    for w in (LANES, LANES, LANES, L, N_PAIR * LANES, N_PAIR * LANES, SSD_HPG * LANES):
        out.append(head_s[slot, :, off:off + w])
        off += w
    return out


def _ssd_fwd(xbc, dtr, bias, alog, dskip, nb, s):
    nc = s // L
    t = nb * s

    def body(x_ref, b_ref, c_ref, dtr_ref, bias_ref, alog_ref, dsk_ref, y_ref, hs_ref, h_scr, head_s):
        lane = lax.broadcasted_iota(jnp.int32, (L, LANES), 1)
        lane1 = lax.broadcasted_iota(jnp.int32, (1, LANES), 1)
        rowc = lax.broadcasted_iota(jnp.int32, (L, 1), 0)
        rr = lax.broadcasted_iota(jnp.int32, (L, L), 0)
        cc = lax.broadcasted_iota(jnp.int32, (L, L), 1)
        causal = rr >= cc
        h_scr[...] = jnp.zeros_like(h_scr)
        dsk = dsk_ref[...]
        spread_pair, spread_full, _, _ = _ssd_tables()
        a_neg = -jnp.exp(alog_ref[...])
        head = functools.partial(_ssd_head, dtr_ref, bias_ref, a_neg, tri_bf=_bf(cc <= rr),
                                 spread_pair=spread_pair, spread_full=spread_full, head_s=head_s)
        head(jnp.int32(0), slot=0)

        def chunk(c, slot):
            r0 = pl.multiple_of(c * L, L)
            head(jnp.minimum(c + 1, nc - 1), slot=1 - slot)
            _, _, acs, acs_t, dt_all, acs_all, acs_full = _ssd_head_load(head_s, slot)
            bb = _bf(b_ref[pl.ds(r0, L), :])
            cb = _bf(c_ref[pl.ds(r0, L), :])
            gm = _dot_nt(cb, bb)
            for pr in range(N_PAIR):
                j0, j1 = 2 * pr, 2 * pr + 1
                ms = []
                for j in (j0, j1):
                    seg = acs_full[:, j * LANES:(j + 1) * LANES] - acs_t[j:j + 1, :]
                    ms.append(_bf(gm * jnp.exp(jnp.where(causal, seg, -1e30))))
                x = x_ref[pl.ds(r0, L), pr * LANES:(pr + 1) * LANES]
                dt_p = dt_all[:, pr * LANES:(pr + 1) * LANES]
                acs_p = acs_all[:, pr * LANES:(pr + 1) * LANES]
                last_p = acs_p[L - 1:L, :]
                d_p = _pair_lanes(dsk[:, j0:j0 + 1], dsk[:, j1:j1 + 1], lane1)
                xd = x * dt_p
                xdb = _bf(xd)
                yds = _dot(jnp.concatenate(ms, axis=0), xdb)
                yd = _pair_lanes(yds[:L], yds[L:], lane)
                hp = h_scr[pr]
                yo = _dot_nt(cb, _bf(hp)) * jnp.exp(acs_p)
                y_ref[pl.ds(r0, L), pr * LANES:(pr + 1) * LANES] = yd + yo + x * d_p
                hs_ref[c, pr] = hp
                sp = _dot_tn(_bf(xd * jnp.exp(last_p - acs_p)), bb)
                cd = jnp.where(rowc < HEAD_DIM, jnp.exp(acs[L - 1:L, j0:j0 + 1]),
                               jnp.exp(acs[L - 1:L, j1:j1 + 1]))
                h_scr[pr] = hp * cd + sp

        def two_chunks(i, _):
            chunk(2 * i, 0)
            chunk(2 * i + 1, 1)
            return 0

        lax.fori_loop(0, nc // 2, two_chunks, 0)

    assert nc % 2 == 0
    grp = lambda b, g: (g, 0, 0)
    return pl.pallas_call(
        body, grid=(nb, SSD_GROUPS),
        in_specs=[pl.BlockSpec((s, 4 * LANES), lambda b, g: (b, g)),
                  pl.BlockSpec((s, LANES), lambda b, g: (b, 16 + g)),
                  pl.BlockSpec((s, LANES), lambda b, g: (b, 20 + g)),
                  pl.BlockSpec((None, s, LANES), lambda b, g: (g, b, 0)),
                  pl.BlockSpec((None, 1, LANES), grp), pl.BlockSpec((None, 1, LANES), grp),
                  pl.BlockSpec((None, 1, LANES), grp)],
        out_specs=[pl.BlockSpec((s, 4 * LANES), lambda b, g: (b, g)),
                   pl.BlockSpec((None, None, nc, N_PAIR, L, LANES), lambda b, g: (b, g, 0, 0, 0, 0))],
        out_shape=[jax.ShapeDtypeStruct((t, SSD_WIDTH), F32),
                   jax.ShapeDtypeStruct((nb, SSD_GROUPS, nc, N_PAIR, L, LANES), F32)],
        scratch_shapes=[pltpu.VMEM((N_PAIR, L, LANES), F32), pltpu.VMEM((2, L, HEAD_W), F32)],
        compiler_params=_params(("parallel", "parallel")), name="ssd_fwd")(
            xbc, xbc, xbc, dtr, bias, alog, dskip)


def _ssd_bwd(xbc, dtr, bias, alog, dskip, hs, dy, nb, s):
    nc = s // L
    t = nb * s

    def body(x_ref, b_ref, c_ref, dtr_ref, bias_ref, alog_ref, dsk_ref, hs_ref, dy_ref,
             dx_ref, db_ref, dc_ref, ddtr_ref, dbias_ref, dalog_ref, ddsk_ref, dh_scr, head_s):
        lane = lax.broadcasted_iota(jnp.int32, (L, LANES), 1)
        lane1 = lax.broadcasted_iota(jnp.int32, (1, LANES), 1)
        rowc = lax.broadcasted_iota(jnp.int32, (L, 1), 0)
        rr = lax.broadcasted_iota(jnp.int32, (L, L), 0)
        cc = lax.broadcasted_iota(jnp.int32, (L, L), 1)
        causal = rr >= cc
        tri_t = _bf(cc >= rr)
        dh_scr[...] = jnp.zeros_like(dh_scr)
        dsk = dsk_ref[...]
        a_neg = -jnp.exp(alog_ref[...])

        @pl.when(pl.program_id(1) == 0)
        def _():
            dbias_ref[...] = jnp.zeros_like(dbias_ref)
            dalog_ref[...] = jnp.zeros_like(dalog_ref)
            ddsk_ref[...] = jnp.zeros_like(ddsk_ref)

        spread_pair, spread_full, sum_pair, sum_full = _ssd_tables()
        sum_pair2 = jnp.concatenate([sum_pair] * 2, axis=0)
        sum_full2 = jnp.concatenate([sum_full] * 2, axis=0)
        head = functools.partial(_ssd_head, dtr_ref, bias_ref, a_neg, tri_bf=_bf(cc <= rr),
                                 spread_pair=spread_pair, spread_full=spread_full, head_s=head_s)
        head(jnp.int32(nc - 1), slot=0)

        def chunk(c, slot):
            r0 = pl.multiple_of(c * L, L)
            head(jnp.maximum(c - 1, 0), slot=1 - slot)
            raw, dt, acs, acs_t, dt_all, acs_all, acs_full = _ssd_head_load(head_s, slot)
            bb = _bf(b_ref[pl.ds(r0, L), :])
            cb = _bf(c_ref[pl.ds(r0, L), :])
            gm = _dot_nt(cb, bb)
            dg = jnp.zeros((L, L), F32)
            dbacc = jnp.zeros((L, LANES), F32)
            dcacc = jnp.zeros((L, LANES), F32)
            dacs_t = jnp.zeros((L, L), F32)
            ta_all, te_all, qm_all, tc_all, td_all, dcd_all = [], [], [], [], [], []
            for pr in range(N_PAIR):
                js = (2 * pr, 2 * pr + 1)
                lms, mfs = [], []
                for j in js:
                    seg = acs_full[:, j * LANES:(j + 1) * LANES] - acs_t[j:j + 1, :]
                    lm = jnp.exp(jnp.where(causal, seg, -1e30))
                    lms.append(lm)
                    mfs.append(gm * lm)
                x = x_ref[pl.ds(r0, L), pr * LANES:(pr + 1) * LANES]
                dyv = dy_ref[pl.ds(r0, L), pr * LANES:(pr + 1) * LANES]
                dt_p = dt_all[:, pr * LANES:(pr + 1) * LANES]
                acs_p = acs_all[:, pr * LANES:(pr + 1) * LANES]
                last_p = acs_p[L - 1:L, :]
                d_p = _pair_lanes(dsk[:, js[0]:js[0] + 1], dsk[:, js[1]:js[1] + 1], lane1)
                e_p = jnp.exp(acs_p)
                dte_p = jnp.exp(last_p - acs_p)
                xd = x * dt_p
                xdb = _bf(xd)
                hp = hs_ref[c, pr]
                hb = _bf(hp)
                dye = _bf(dyv * e_p)
                yo = _dot_nt(cb, hb) * e_p
                dcacc = dcacc + _dot(dye, hb)
                dh_out = _dot_tn(dye, cb)
                dyb = _bf(dyv)
                dms = _dot_nt(_bf(jnp.concatenate(
                    [jnp.where(lane < HEAD_DIM, dyv, 0.0), jnp.where(lane < HEAD_DIM, 0.0, dyv)], axis=0)), xdb)
                mty = _dot_tn(_bf(jnp.concatenate(mfs, axis=1)), dyb)
                dxd = jnp.where(lane < HEAD_DIM, mty[:L], mty[L:])
                for hh in range(2):
                    dm = dms[hh * L:(hh + 1) * L]
                    dg = dg + dm * lms[hh]
                    qm = dm * mfs[hh]
                    qm_all.append(qm)
                    dacs_t = dacs_t - jnp.where(rr == js[hh], jnp.sum(qm, axis=0, keepdims=True), 0.0)
                ds = dh_scr[pr]
                dsb = _bf(ds)
                xdd = xd * dte_p
                dxdd = _dot_nt(bb, dsb)
                dbacc = dbacc + _dot(_bf(xdd), dsb)
                dxd = dxd + dxdd * dte_p
                cd = jnp.where(rowc < HEAD_DIM, jnp.exp(acs[L - 1:L, js[0]:js[0] + 1]),
                               jnp.exp(acs[L - 1:L, js[1]:js[1] + 1]))
                prod = ds * hp
                dcd_all += [jnp.sum(prod[:HEAD_DIM], axis=0, keepdims=True),
                            jnp.sum(prod[HEAD_DIM:], axis=0, keepdims=True)]
                dh_scr[pr] = dh_out + ds * cd
                tcv = dxdd * xdd
                ta_all.append(dyv * yo - tcv)
                tc_all.append(jnp.sum(tcv, axis=0, keepdims=True))
                te_all.append(dxd * x)
                td_all.append(jnp.sum(dyv * x, axis=0, keepdims=True))
                dx_ref[pl.ds(r0, L), pr * LANES:(pr + 1) * LANES] = dxd * dt_p + dyv * d_p
            dgb = _bf(dg)
            dc_ref[pl.ds(r0, L), :] = dcacc + _dot(dgb, bb)
            db_ref[pl.ds(r0, L), :] = dbacc + _dot_tn(dgb, cb)
            last_add = jnp.zeros((1, LANES), F32)
            dd_add = jnp.zeros((1, LANES), F32)
            cd_row = jnp.exp(acs[L - 1:L, :])
            for j in range(SSD_HPG):
                hm = (lane1 >= HEAD_DIM) if j % 2 else (lane1 < HEAD_DIM)
                tcj = jnp.sum(jnp.where(hm, tc_all[j // 2], 0.0), axis=1, keepdims=True)
                tdj = jnp.sum(jnp.where(hm, td_all[j // 2], 0.0), axis=1, keepdims=True)
                dcdj = jnp.sum(dcd_all[j], axis=1, keepdims=True)
                last_add = last_add + jnp.where(lane1 == j, tcj + dcdj * cd_row, 0.0)
                dd_add = dd_add + jnp.where(lane1 == j, tdj, 0.0)
            dacs = (_dot(_split2(jnp.concatenate(ta_all, axis=1)), sum_pair2)
                    + _dot(_split2(jnp.concatenate(qm_all, axis=1)), sum_full2)
                    + dacs_t.T + jnp.where(rowc == L - 1, last_add, 0.0))
            ddta = _tri_dot(tri_t, dacs)
            ddt = _dot(_bf(jnp.concatenate(te_all, axis=1)), sum_pair) + ddta * a_neg
            ddraw = ddt * _sigmoid(raw)
            ddtr_ref[pl.ds(r0, L), :] = ddraw
            dbias_ref[...] += jnp.sum(ddraw, axis=0, keepdims=True)
            dalog_ref[...] += jnp.sum(ddta * dt, axis=0, keepdims=True) * a_neg
            ddsk_ref[...] += dd_add

        def two_chunks(i, _):
            chunk(nc - 1 - 2 * i, 0)
            chunk(nc - 2 - 2 * i, 1)
            return 0

        lax.fori_loop(0, nc // 2, two_chunks, 0)

    assert nc % 2 == 0
    grp = lambda g, b: (g, 0, 0)
    small = jax.ShapeDtypeStruct((SSD_GROUPS, 1, LANES), F32)
    return pl.pallas_call(
        body, grid=(SSD_GROUPS, nb),
        in_specs=[pl.BlockSpec((s, 4 * LANES), lambda g, b: (b, g)),
                  pl.BlockSpec((s, LANES), lambda g, b: (b, 16 + g)),
                  pl.BlockSpec((s, LANES), lambda g, b: (b, 20 + g)),
                  pl.BlockSpec((None, s, LANES), lambda g, b: (g, b, 0)),
                  pl.BlockSpec((None, 1, LANES), grp), pl.BlockSpec((None, 1, LANES), grp),
                  pl.BlockSpec((None, 1, LANES), grp),
                  pl.BlockSpec((None, None, nc, N_PAIR, L, LANES), lambda g, b: (b, g, 0, 0, 0, 0)),
                  pl.BlockSpec((s, 4 * LANES), lambda g, b: (b, g))],
        out_specs=[pl.BlockSpec((s, 4 * LANES), lambda g, b: (b, g)),
                   pl.BlockSpec((s, LANES), lambda g, b: (b, g)),
                   pl.BlockSpec((s, LANES), lambda g, b: (b, g)),
                   pl.BlockSpec((None, s, LANES), lambda g, b: (g, b, 0)),
                   pl.BlockSpec((None, 1, LANES), grp), pl.BlockSpec((None, 1, LANES), grp),
                   pl.BlockSpec((None, 1, LANES), grp)],
        out_shape=[jax.ShapeDtypeStruct((t, SSD_WIDTH), F32),
                   jax.ShapeDtypeStruct((t, SSD_GROUPS * SSD_STATE), F32),
                   jax.ShapeDtypeStruct((t, SSD_GROUPS * SSD_STATE), F32),
                   jax.ShapeDtypeStruct((SSD_GROUPS, t, LANES), F32), small, small, small],
        scratch_shapes=[pltpu.VMEM((N_PAIR, L, LANES), F32), pltpu.VMEM((2, L, HEAD_W), F32)],
        compiler_params=_params(("parallel", "arbitrary")), name="ssd_bwd")(
            xbc, xbc, xbc, dtr, bias, alog, dskip, hs, dy)


GN = SSD_WIDTH // SSD_GROUPS


def _post_fwd(o, proj, y, nw, *, tm=256):
    t = o.shape[0]

    def body(o_ref, za_ref, y_ref, zs_ref, nw_ref, ya_ref, ys_ref):
        za = za_ref[...].astype(F32)
        ya_ref[...] = _bf(o_ref[...] * (za * _sigmoid(za)))
        zs = zs_ref[...].astype(F32)
        yz = y_ref[...] * (zs * _sigmoid(zs))
        for g in range(SSD_GROUPS):
            sl = slice(g * GN, (g + 1) * GN)
            v = yz[:, sl]
            r = lax.rsqrt(jnp.mean(v * v, axis=-1, keepdims=True) + EPS)
            ys_ref[:, sl] = _bf(v * r * nw_ref[:, sl])

    return pl.pallas_call(
        body, grid=(t // tm,),
        in_specs=[pl.BlockSpec((tm, SB_WIDTH), lambda i: (i, 0)),
                  pl.BlockSpec((tm, SB_WIDTH), lambda i: (i, CB_ZA * LANES // SB_WIDTH)),
                  pl.BlockSpec((tm, SSD_WIDTH), lambda i: (i, 0)),
                  pl.BlockSpec((tm, SSD_WIDTH), lambda i: (i, CB_ZS * LANES // SSD_WIDTH)),
                  pl.BlockSpec((1, SSD_WIDTH), lambda i: (0, 0))],
        out_specs=[pl.BlockSpec((tm, SB_WIDTH), lambda i: (i, 0)),
                   pl.BlockSpec((tm, SSD_WIDTH), lambda i: (i, 0))],
        out_shape=[jax.ShapeDtypeStruct((t, SB_WIDTH), BF16), jax.ShapeDtypeStruct((t, SSD_WIDTH), BF16)],
        compiler_params=_params(("parallel",)), name="post_fwd")(o, proj, y, proj, nw)


def _post_bwd(dya, o, proj, dys, y, nw, dproj, *, tm=256):
    t = o.shape[0]
    zw = SB_WIDTH + SSD_WIDTH

    def body(dya_ref, o_ref, za_ref, dys_ref, y_ref, zs_ref, nw_ref, _, do_ref, dy_ref, dnw_ref, dz_ref):
        @pl.when(pl.program_id(0) == 0)
        def _():
            dnw_ref[...] = jnp.zeros_like(dnw_ref)

        za = za_ref[...].astype(F32)
        sa = _sigmoid(za)
        d = dya_ref[...].astype(F32)
        do_ref[...] = _bf(d * (za * sa))
        dz_ref[:, :SB_WIDTH] = _bf(d * o_ref[...] * (sa * (1.0 + za * (1.0 - sa))))
        zs = zs_ref[...].astype(F32)
        ss = _sigmoid(zs)
        silu = zs * ss
        yv = y_ref[...]
        yz = yv * silu
        dv = dys_ref[...].astype(F32)
        for g in range(SSD_GROUPS):
            sl = slice(g * GN, (g + 1) * GN)
            v = yz[:, sl]
            r = lax.rsqrt(jnp.mean(v * v, axis=-1, keepdims=True) + EPS)
            nrm = v * r
            dg = dv[:, sl]
            dnw_ref[:, sl] += jnp.sum(dg * nrm, axis=0, keepdims=True)
            dn = dg * nw_ref[:, sl]
            dyz = r * (dn - nrm * jnp.mean(dn * nrm, axis=-1, keepdims=True))
            dy_ref[:, sl] = dyz * silu[:, sl]
            dz_ref[:, SB_WIDTH + g * GN:SB_WIDTH + (g + 1) * GN] = _bf(
                dyz * yv[:, sl] * (ss[:, sl] * (1.0 + zs[:, sl] * (1.0 - ss[:, sl]))))

    a_spec = pl.BlockSpec((tm, SB_WIDTH), lambda i: (i, 0))
    s_spec = pl.BlockSpec((tm, SSD_WIDTH), lambda i: (i, 0))
    w_spec = pl.BlockSpec((1, SSD_WIDTH), lambda i: (0, 0))
    return pl.pallas_call(
        body, grid=(t // tm,),
        in_specs=[a_spec, a_spec, pl.BlockSpec((tm, SB_WIDTH), lambda i: (i, CB_ZA * LANES // SB_WIDTH)),
                  s_spec, s_spec, pl.BlockSpec((tm, SSD_WIDTH), lambda i: (i, CB_ZS * LANES // SSD_WIDTH)),
                  w_spec, pl.BlockSpec(memory_space=pl.ANY)],
        out_specs=[a_spec, s_spec, w_spec, pl.BlockSpec((tm, zw), lambda i: (i, 1))],
        out_shape=[jax.ShapeDtypeStruct((t, SB_WIDTH), BF16), jax.ShapeDtypeStruct((t, SSD_WIDTH), F32),
                   jax.ShapeDtypeStruct((1, SSD_WIDTH), F32), jax.ShapeDtypeStruct(dproj.shape, dproj.dtype)],
        input_output_aliases={7: 3},
        compiler_params=_params(("arbitrary",)), name="post_bwd")(dya, o, proj, dys, y, proj, nw, dproj)


def _merge_fwd(ya, ys, proj_g, *, tm=512):
    t = ya.shape[0]

    def body(ya_ref, ys_ref, ga_ref, gs_ref, m_ref):
        m_ref[...] = _bf(_sigmoid(ga_ref[...].astype(F32)) * ya_ref[...].astype(F32)
                         + _sigmoid(gs_ref[...].astype(F32)) * ys_ref[...].astype(F32))

    spec = pl.BlockSpec((tm, D_MODEL), lambda i: (i, 0))
    return pl.pallas_call(
        body, grid=(t // tm,),
        in_specs=[spec, spec, pl.BlockSpec((tm, D_MODEL), lambda i: (i, 0)),
                  pl.BlockSpec((tm, D_MODEL), lambda i: (i, 1))],
        out_specs=spec, out_shape=jax.ShapeDtypeStruct((t, D_MODEL), BF16),
        compiler_params=_params(("parallel",)), name="merge_fwd")(ya, ys, proj_g, proj_g)


def _merge_bwd(dm, ya, ys, proj_g, *, tm=512):
    t = ya.shape[0]

    def body(dm_ref, ya_ref, ys_ref, ga_ref, gs_ref, dya_ref, dys_ref, dg_ref):
        d = dm_ref[...].astype(F32)
        ga = _sigmoid(ga_ref[...].astype(F32))
        gs = _sigmoid(gs_ref[...].astype(F32))
        dya_ref[...] = _bf(d * ga)
        dys_ref[...] = _bf(d * gs)
        dg_ref[:, :D_MODEL] = _bf(d * ya_ref[...].astype(F32) * (ga * (1.0 - ga)))
        dg_ref[:, D_MODEL:] = _bf(d * ys_ref[...].astype(F32) * (gs * (1.0 - gs)))

    spec = pl.BlockSpec((tm, D_MODEL), lambda i: (i, 0))
    return pl.pallas_call(
        body, grid=(t // tm,),
        in_specs=[spec, spec, spec, pl.BlockSpec((tm, D_MODEL), lambda i: (i, 0)),
                  pl.BlockSpec((tm, D_MODEL), lambda i: (i, 1))],
        out_specs=[spec, spec, pl.BlockSpec((tm, 2 * D_MODEL), lambda i: (i, 0))],
        out_shape=[jax.ShapeDtypeStruct((t, D_MODEL), BF16), jax.ShapeDtypeStruct((t, D_MODEL), BF16),
                   jax.ShapeDtypeStruct((t, 2 * D_MODEL), BF16)],
        compiler_params=_params(("parallel",)), name="merge_bwd")(dm, ya, ys, proj_g, proj_g)


def _final(x, out, wf, target, *, tm=512):
    t = x.shape[0]

    def body(x_ref, o_ref, w_ref, t_ref, loss_ref, dx_ref, dw_ref):
        @pl.when(pl.program_id(0) == 0)
        def _():
            loss_ref[...] = jnp.zeros_like(loss_ref)
            dw_ref[...] = jnp.zeros_like(dw_ref)

        x2 = x_ref[...] + o_ref[...]
        r = lax.rsqrt(jnp.mean(x2 * x2, axis=-1, keepdims=True) + EPS)
        nrm = x2 * r
        e = nrm * w_ref[...] - t_ref[...]
        row_loss = jnp.mean(e * e, axis=-1, keepdims=True)
        loss_ref[...] += 0.5 * jnp.sum(row_loss, axis=0, keepdims=True)
        dyv = e * (1.0 / D_MODEL)
        dw_ref[...] += jnp.sum(dyv * nrm, axis=0, keepdims=True)
        dn = dyv * w_ref[...]
        dx_ref[...] = r * (dn - nrm * jnp.mean(dn * nrm, axis=-1, keepdims=True))

    spec = pl.BlockSpec((tm, D_MODEL), lambda i: (i, 0))
    fix = pl.BlockSpec((1, D_MODEL), lambda i: (0, 0))
    return pl.pallas_call(
        body, grid=(t // tm,),
        in_specs=[spec, spec, fix, spec],
        out_specs=[pl.BlockSpec((1, LANES), lambda i: (0, 0)), spec, fix],
        out_shape=[jax.ShapeDtypeStruct((1, LANES), F32), jax.ShapeDtypeStruct((t, D_MODEL), F32),
                   jax.ShapeDtypeStruct((1, D_MODEL), F32)],
        compiler_params=_params(("arbitrary",)), name="final_loss")(x, out, wf, target)


def _group_lanes(v):
    return jnp.pad(v.reshape(SSD_GROUPS, 1, SSD_HPG), ((0, 0), (0, 0), (0, LANES - SSD_HPG)))


def _ungroup_lanes(v):
    return v[:, :, :SSD_HPG].reshape(1, N_DT)


def _row_slabs(pieces, rows):
    total = sum(p.shape[0] for p in pieces)
    assert total % rows == 0
    slabs = []
    for lo in range(0, total, rows):
        parts, off = [], 0
        for p in pieces:
            a, b = max(lo, off), min(lo + rows, off + p.shape[0])
            if a < b:
                parts.append(p[a - off:b - off])
            off += p.shape[0]
        slabs.append(parts[0] if len(parts) == 1 else jnp.concatenate(parts, axis=0))
    return jnp.stack(slabs)


def _local_step(x, target, nb, s, w_in_t, conv_w, conv_b, norm_w, dt_bias, a_log, d_skip,
                ssm_norm_w, out_weights, final_norm_w, between=None):
    t = nb * s
    w_g_t = w_in_t[GATE_COL0:]
    w_dt_t = jnp.pad(w_in_t[DT_COL0:GATE_COL0], ((0, LANES - N_DT), (0, 0)))
    h, r1 = _rms_fwd(x, norm_w)
    proj = _matmul(h, w_in_t, nt=True, n=DT_COL0, out_dtype=BF16, name="proj_main", tm=MM_TALL)
    proj_g = _matmul(h, w_g_t, nt=True, out_dtype=BF16, name="proj_gate", tm=MM_TALL)
    dt_raw = _matmul(h, w_dt_t, nt=True, name="proj_dt")
    dtr = jnp.pad(dt_raw[:, :N_DT].reshape(t, SSD_GROUPS, SSD_HPG).transpose(1, 0, 2),
                  ((0, 0), (0, 0), (0, LANES - SSD_HPG)))
    bias_g, alog_g, dsk_g = _group_lanes(dt_bias), _group_lanes(a_log), _group_lanes(d_skip)

    o, att_a, att_sg = _attn_fwd(proj, nb, s)
    xbc = _conv_fwd(proj, conv_w, conv_b, nb, s)
    y, hs = _ssd_fwd(xbc, dtr, bias_g, alog_g, dsk_g, nb, s)
    ya_in, ys_in = _post_fwd(o, proj, y, ssm_norm_w)
    w_attn_out, w_ssm_out, w_o = out_weights(ys_in) if callable(out_weights) else out_weights
    ya = _matmul(ya_in, w_attn_out, out_dtype=BF16, name="attn_out")
    ys = _matmul(ys_in, w_ssm_out, out_dtype=BF16, name="ssm_out")
    merged = _merge_fwd(ya, ys, proj_g)
    out = _matmul(merged, w_o, name="out_proj")
    loss, dx2, d_final_w = _final(x, out, final_norm_w, target)

    dx2b = _bf(dx2)
    dmerged = _matmul(dx2b, w_o, nt=True, out_dtype=BF16, name="d_merged")
    d_w_o = _matmul(merged, dx2b, ta=True, name="d_w_o")
    dya, dys, dgate = _merge_bwd(dmerged, ya, ys, proj_g)
    d_w_attn_out = _matmul(ya_in, dya, ta=True, name="d_w_attn_out")
    d_w_ssm_out = _matmul(ys_in, dys, ta=True, name="d_w_ssm_out")
    dya_in = _matmul(dya, w_attn_out, nt=True, out_dtype=BF16, name="d_ya_in")
    dys_in = _matmul(dys, w_ssm_out, nt=True, name="d_ys_in")
    do, dy, d_ssm_norm_w, dproj = _post_bwd(dya_in, o, proj, dys_in, y, ssm_norm_w, lax.empty(proj.shape, BF16))
    dq, dk, dv = _attn_bwd(proj, do, att_a, att_sg, nb, s)
    dxs_act, db_act, dc_act, ddtr, dbias_g, dalog_g, ddsk_g = _ssd_bwd(
        xbc, dtr, bias_g, alog_g, dsk_g, hs, dy, nb, s)
    dproj, dwx, dbx = _conv_bwd(proj, dxs_act, conv_w, conv_b, dproj, nb, s, col0=0, name="conv_bwd_x")
    dproj, dwb, dbb = _conv_bwd(proj, db_act, conv_w, conv_b, dproj, nb, s, col0=SSD_WIDTH, name="conv_bwd_b")
    dproj, dwc, dbc = _conv_bwd(proj, dc_act, conv_w, conv_b, dproj, nb, s, col0=SSD_WIDTH + 512,
                                name="conv_bwd_c")
    for i, piece in enumerate((dq, dk, dv)):
        dproj = lax.dynamic_update_slice(dproj, piece, (0, i * SB_WIDTH))
    ddt = jnp.pad(_bf(ddtr[:, :, :SSD_HPG].transpose(1, 0, 2).reshape(t, N_DT)), ((0, 0), (0, LANES - N_DT)))
    wt = dict(ta=True, out_dtype=BF16)
    d_w_in_t = _row_slabs(
        [_matmul(dproj, h, name="d_w_main", **wt), _matmul(ddt, h, name="d_w_dt", **wt)[:N_DT],
         _matmul(dgate, h, name="d_w_gate", **wt)], SHARD)
    grads = dict(
        w_in_t=d_w_in_t,
        conv_w=jnp.concatenate([dwx, dwb, dwc], axis=1), conv_b=jnp.concatenate([dbx, dbb, dbc], axis=1),
        dt_bias=_ungroup_lanes(dbias_g), a_log=_ungroup_lanes(dalog_g), d_skip=_ungroup_lanes(ddsk_g),
        ssm_norm_w=d_ssm_norm_w, w_attn_out=d_w_attn_out, w_ssm_out=d_w_ssm_out, w_o=d_w_o,
        final_norm_w=d_final_w)
    extra = None
    if between is not None:
        zero, extra = between(grads)
        ddt = ddt + _bf(zero[0:1, 0:1])

    dh = _matmul(ddt, w_dt_t, name="d_h_dt")
    dh = _matmul(dgate, w_g_t, add=dh, name="d_h_gate")
    dh = _matmul(dproj, w_in_t, add=dh, name="d_h", tk=DT_COL0 // 4)
    grad_x, grads["norm_w"] = _rms_bwd(dh, x, r1, norm_w, dx2)
    return loss, grad_x, grads, extra


N_CHIP = N_DEV // 2


def _pair_exchange(scatter, bcast, *, name):
    arrays = list(scatter) + list(bcast)
    ns, n = len(scatter), len(arrays)
    n_sem = ns * N_CHIP + (n - ns) * (N_DEV - 1)

    def body(*refs):
        ins, outs = refs[:n], refs[n:2 * n]
        send_sems, recv_sems, local_sems = refs[2 * n:]
        x, y, c = lax.axis_index("x"), lax.axis_index("y"), lax.axis_index("c")
        me = 4 * x + 2 * y + c
        copies, k = [], 0
        for a in range(ns):
            for q in range(N_CHIP):
                cp = pltpu.make_async_remote_copy(
                    src_ref=ins[a].at[2 * q + 1 - c], dst_ref=outs[a].at[q],
                    send_sem=send_sems.at[k], recv_sem=recv_sems.at[k],
                    device_id=(x, y, 1 - c), device_id_type=pl.DeviceIdType.MESH)
                cp.start()
                copies.append(cp)
                k += 1
        for a in range(ns, n):
            cp = pltpu.make_async_copy(ins[a], outs[a].at[me], local_sems.at[a - ns])
            cp.start()
            copies.append(cp)
            for r in range(1, N_DEV):
                tx = 1 - x if (r >> 2) & 1 else x
                ty = 1 - y if (r >> 1) & 1 else y
                tc = 1 - c if r & 1 else c
                cp = pltpu.make_async_remote_copy(
                    src_ref=ins[a], dst_ref=outs[a].at[me], send_sem=send_sems.at[k], recv_sem=recv_sems.at[k],
                    device_id=(tx, ty, tc), device_id_type=pl.DeviceIdType.MESH)
                cp.start()
                copies.append(cp)
                k += 1
        for cp in copies:
            cp.wait()

    out_shape = [jax.ShapeDtypeStruct((N_CHIP,) + v.shape[1:] if i < ns else (N_DEV,) + v.shape, v.dtype)
                 for i, v in enumerate(arrays)]
    return pl.pallas_call(
        body, in_specs=[pl.BlockSpec(memory_space=pl.ANY)] * n,
        out_specs=[pl.BlockSpec(memory_space=pl.ANY)] * n, out_shape=out_shape,
        scratch_shapes=[pltpu.SemaphoreType.DMA((n_sem,)), pltpu.SemaphoreType.DMA((n_sem,)),
                        pltpu.SemaphoreType.DMA((max(n - ns, 1),))],
        compiler_params=pltpu.CompilerParams(has_side_effects=True), name=name)(*arrays)


def _pair_add(mine, recv, core, *, name, tc=None):
    _, r, c = recv.shape
    tc = c if tc is None else tc
    assert c % tc == 0

    def body(core_ref, m_ref, r_ref, o_ref):
        o_ref[...] = (m_ref[...].astype(F32) + r_ref[...].astype(F32)).astype(o_ref.dtype)

    spec = pl.BlockSpec((None, r, tc), lambda q, i, core_ref: (q, 0, i))
    return pl.pallas_call(
        body, grid_spec=pltpu.PrefetchScalarGridSpec(
            num_scalar_prefetch=1, grid=(N_CHIP, c // tc),
            in_specs=[pl.BlockSpec((None, r, tc), lambda q, i, core_ref: (2 * q + core_ref[0], 0, i)), spec],
            out_specs=spec),
        out_shape=jax.ShapeDtypeStruct(recv.shape, recv.dtype),
        compiler_params=_params(("parallel", "parallel")), name=name)(core, mine, recv)


_HBM = pl.BlockSpec(memory_space=pltpu.HBM)
_SEM = pl.BlockSpec(memory_space=pltpu.SEMAPHORE)
_DATAFLOW = pltpu.SideEffectType.DATAFLOW_SIDE_EFFECTING
_PEERS = {"chips": N_CHIP - 1, "all": N_DEV - 1}


def _split_copies(pattern, srcs, lands, send_sems, recv_sems):
    x, y, c = lax.axis_index("x"), lax.axis_index("y"), lax.axis_index("c")
    per = _PEERS[pattern]
    copies = []
    for a in range(len(srcs)):
        for r in range(1, per + 1):
            if pattern == "chips":
                tx = 1 - x if (r >> 1) & 1 else x
                ty = 1 - y if r & 1 else y
                tc = c
                src, dst = srcs[a].at[2 * tx + ty], lands[a].at[2 * x + y]
            else:
                tx = 1 - x if (r >> 2) & 1 else x
                ty = 1 - y if (r >> 1) & 1 else y
                tc = 1 - c if r & 1 else c
                src, dst = srcs[a], lands[a].at[4 * x + 2 * y + c]
            copies.append(pltpu.make_async_remote_copy(
                src_ref=src, dst_ref=dst, send_sem=send_sems.at[a * per + r - 1],
                recv_sem=recv_sems.at[a * per + r - 1], device_id=(tx, ty, tc), device_id_type=pl.DeviceIdType.MESH))
    return copies


def _split_start(pattern, arrays, *, name):
    n = len(arrays)
    n_sem = n * _PEERS[pattern]

    def body(*refs):
        srcs, lands = refs[:n], refs[n:2 * n]
        send_sems, recv_sems = refs[2 * n], refs[2 * n + 1]
        zero = refs[-1]
        for cp in _split_copies(pattern, srcs, lands, send_sems, recv_sems):
            cp.start()
        zero[...] = jnp.zeros_like(zero)

    src_t = [pltpu.HBM(v.shape, v.dtype) for v in arrays]
    land_t = [pltpu.HBM(v.shape if pattern == "chips" else (N_DEV,) + v.shape, v.dtype) for v in arrays]
    outs = pl.pallas_call(
        body, name=name,
        out_shape=(pltpu.SemaphoreType.DMA((n_sem,)), pltpu.SemaphoreType.DMA((n_sem,)), *src_t, *land_t,
                   jax.ShapeDtypeStruct((8, LANES), F32)),
        in_specs=[_HBM] * (2 * n),
        out_specs=(_SEM, _SEM, *[_HBM] * (2 * n), pl.BlockSpec(memory_space=pltpu.VMEM)),
        input_output_aliases={i: 2 + i for i in range(2 * n)},
        compiler_params=pltpu.CompilerParams(has_side_effects=_DATAFLOW))(
            *[pltpu.with_memory_space_constraint(v, pltpu.HBM) for v in arrays],
            *[pltpu.with_memory_space_constraint(lax.empty(t.shape, t.dtype), pltpu.HBM) for t in land_t])
    return outs[0], outs[1], outs[2:2 + n], outs[2 + n:2 + 2 * n], outs[-1]


def _split_wait(pattern, send_sems, recv_sems, srcs, lands, after, *, name):
    n = len(srcs)

    def body(*refs):
        for cp in _split_copies(pattern, refs[:n], refs[n:2 * n], refs[2 * n], refs[2 * n + 1]):
            cp.wait_send()
            cp.wait_recv()

    outs = pl.pallas_call(
        body, name=name, out_shape=[pltpu.HBM(v.shape, v.dtype) for v in (*srcs, *lands)],
        in_specs=[_HBM] * (2 * n) + [_SEM, _SEM, pl.BlockSpec(memory_space=pl.ANY)],
        out_specs=[_HBM] * (2 * n), input_output_aliases={i: i for i in range(2 * n)},
        compiler_params=pltpu.CompilerParams(has_side_effects=_DATAFLOW))(
            *srcs, *lands, send_sems, recv_sems, after)
    return outs[:n], outs[n:]


def _own_slab(lands, srcs, index, whole):
    out = []
    for land, src in zip(lands, srcs):
        own = src[None] if whole else lax.dynamic_slice(src, (index, 0, 0), (1,) + src.shape[1:])
        out.append(lax.dynamic_update_slice(land, own, (index, 0, 0)))
    return out


def _gather(arrays, *, name):
    n = len(arrays)
    per = N_DEV - 1

    def body(*refs):
        ins, outs = refs[:n], refs[n:2 * n]
        send_sems, recv_sems, local_sems = refs[2 * n:]
        x, y, c = lax.axis_index("x"), lax.axis_index("y"), lax.axis_index("c")
        me, sibling = (x, y, c), (x, y, 1 - c)
        chips = [(1 - x, y), (x, 1 - y), (1 - x, 1 - y)]

        def slab(a, block):
            return outs[a].at[4 * block[0] + 2 * block[1] + block[2]]

        def copy(a, k, block, to, src=None):
            rows = slab(a, block)
            return pltpu.make_async_remote_copy(
                src_ref=rows if src is None else src, dst_ref=rows,
                send_sem=send_sems.at[a * per + k], recv_sem=recv_sems.at[a * per + k],
                device_id=to, device_id_type=pl.DeviceIdType.MESH)

        started = []
        for a in range(n):
            mine = pltpu.make_async_copy(ins[a], slab(a, me), local_sems.at[a])
            mine.start()
            started.append(mine)
        for a in range(n):
            first = [copy(a, 1 + j, me, (*chip, c), src=ins[a]) for j, chip in enumerate(chips)]
            first.append(copy(a, 0, me, sibling, src=ins[a]))
            for cp in first:
                cp.start()
            started += first
        for j, chip in enumerate(chips):
            for a in range(n):
                copy(a, 1 + j, (*chip, c), me).wait_recv()
                passed = copy(a, 4 + j, (*chip, c), sibling)
                passed.start()
                started.append(passed)
        for a in range(n):
            copy(a, 0, sibling, me).wait_recv()
            for j, chip in enumerate(chips):
                copy(a, 4 + j, (*chip, 1 - c), me).wait_recv()
        for cp in started[:n]:
            cp.wait()
        for cp in started[n:]:
            cp.wait_send()

    return pl.pallas_call(
        body, in_specs=[pl.BlockSpec(memory_space=pl.ANY)] * n,
        out_specs=[pl.BlockSpec(memory_space=pl.ANY)] * n,
        out_shape=[jax.ShapeDtypeStruct((N_DEV,) + v.shape, v.dtype) for v in arrays],
        scratch_shapes=[pltpu.SemaphoreType.DMA((n * per,)), pltpu.SemaphoreType.DMA((n * per,)),
                        pltpu.SemaphoreType.DMA((n,))],
        compiler_params=pltpu.CompilerParams(has_side_effects=True), name=name)(*arrays)


def _adamw(parts, w, m, v, *, name, tc=None):
    r, c = w.shape
    n_parts = parts.shape[0]
    tc = c if tc is None else tc
    assert c % tc == 0

    def body(p_ref, w_ref, m_ref, v_ref, g_ref, d_ref, nm_ref, nv_ref):
        g = p_ref[0].astype(F32)
        for d in range(1, n_parts):
            g = g + p_ref[d].astype(F32)
        mm = ADAM_B1 * m_ref[...] + (1.0 - ADAM_B1) * g
        vv = ADAM_B2 * v_ref[...] + (1.0 - ADAM_B2) * (g * g)
        m_hat = mm / (1.0 - ADAM_B1 ** ADAM_STEP)
        v_hat = vv / (1.0 - ADAM_B2 ** ADAM_STEP)
        g_ref[...] = g
        d_ref[...] = -ADAM_LR * (m_hat / (jnp.sqrt(v_hat) + ADAM_EPS) + ADAM_WD * w_ref[...])
        nm_ref[...] = mm
        nv_ref[...] = vv

    spec = pl.BlockSpec((r, tc), lambda i: (0, i))
    out = jax.ShapeDtypeStruct((r, c), F32)
    return pl.pallas_call(
        body, grid=(c // tc,),
        in_specs=[pl.BlockSpec((n_parts, r, tc), lambda i: (0, 0, i)), spec, spec, spec],
        out_specs=[spec] * 4, out_shape=[out] * 4,
        compiler_params=_params(("parallel",)), name=name)(parts, w, m, v)


SMALL = (("norm_w", D_MODEL), ("conv_b", CONV_DIM), ("dt_bias", N_DT), ("a_log", N_DT),
         ("d_skip", N_DT), ("ssm_norm_w", SSD_WIDTH), ("final_norm_w", D_MODEL))
SMALL_USED = sum(size for _, size in SMALL)
SMALL_ROWS = 64
SHARD = D_PROJ // N_DEV


def _pack_small(vals, scalar=None):
    parts = [vals[k].reshape(-1).astype(F32) for k, _ in SMALL]
    if scalar is not None:
        parts.append(scalar.reshape(1))
    flat = jnp.concatenate(parts)
    return jnp.pad(flat, (0, SMALL_ROWS * LANES - flat.shape[0])).reshape(SMALL_ROWS, LANES)


def _unpack_small(packed, shapes):
    flat = packed.reshape(-1)
    out, off = {}, 0
    for k, size in SMALL:
        out[k] = flat[off:off + size].reshape(shapes[k])
        off += size
    return out


def kernel(x, norm_w, w_in, conv_w, conv_b, dt_bias, a_log, d_skip, ssm_norm_w, w_attn_out, w_ssm_out, w_o, final_norm_w, loss_target, m_norm_w, m_w_in, m_conv_w, m_conv_b, m_dt_bias, m_a_log, m_d_skip, m_ssm_norm_w, m_w_attn_out, m_w_ssm_out, m_w_o, m_final_norm_w, v_norm_w, v_w_in, v_conv_w, v_conv_b, v_dt_bias, v_a_log, v_d_skip, v_ssm_norm_w, v_w_attn_out, v_w_ssm_out, v_w_o, v_final_norm_w):
    nb, s, _ = x.shape
    t = nb * s
    weights = dict(norm_w=norm_w, w_in=w_in, conv_w=conv_w, conv_b=conv_b, dt_bias=dt_bias, a_log=a_log,
                   d_skip=d_skip, ssm_norm_w=ssm_norm_w, w_attn_out=w_attn_out, w_ssm_out=w_ssm_out,
                   w_o=w_o, final_norm_w=final_norm_w)
    moms = dict(norm_w=m_norm_w, w_in=m_w_in, conv_w=m_conv_w, conv_b=m_conv_b, dt_bias=m_dt_bias,
                a_log=m_a_log, d_skip=m_d_skip, ssm_norm_w=m_ssm_norm_w, w_attn_out=m_w_attn_out,
                w_ssm_out=m_w_ssm_out, w_o=m_w_o, final_norm_w=m_final_norm_w)
    vels = dict(norm_w=v_norm_w, w_in=v_w_in, conv_w=v_conv_w, conv_b=v_conv_b, dt_bias=v_dt_bias,
                a_log=v_a_log, d_skip=v_d_skip, ssm_norm_w=v_ssm_norm_w, w_attn_out=v_w_attn_out,
                w_ssm_out=v_w_ssm_out, w_o=v_w_o, final_norm_w=v_final_norm_w)

    tr_ = lambda a: jnp.transpose(a[0])
    g_in, g_conv = _gather([_bf(tr_(w_in)), conv_w[0]], name="gather_weights")
    conv_full = g_conv.transpose(1, 0, 2).reshape(CONV_K, CONV_DIM)
    o_send, o_recv, o_srcs, o_lands, o_zero = _split_start(
        "all", [_bf(w_attn_out[0]), _bf(w_ssm_out[0]), _bf(w_o[0])], name="gather_out_weights_start")

    core = lax.axis_index("c").astype(jnp.int32).reshape(1)
    chip = 2 * lax.axis_index("x") + lax.axis_index("y")
    me = 2 * chip + lax.axis_index("c")

    def out_weights(after):
        srcs, lands = _split_wait("all", o_send, o_recv, o_srcs, o_lands, after, name="gather_out_weights_wait")
        g_wa, g_ws, g_wo = _own_slab(lands, srcs, me, True)
        return (g_wa.reshape(SB_WIDTH, D_MODEL), g_ws.reshape(SSD_WIDTH, D_MODEL), g_wo.reshape(D_MODEL, D_MODEL))

    def send_weight_grads(grads):
        slabs = [grads["w_in_t"],
                 grads["conv_w"].reshape(CONV_K, N_DEV, CONV_DIM // N_DEV).transpose(1, 0, 2),
                 _bf(grads["w_attn_out"].reshape(N_DEV, SB_WIDTH // N_DEV, D_MODEL)),
                 _bf(grads["w_ssm_out"].reshape(N_DEV, SSD_WIDTH // N_DEV, D_MODEL)),
                 _bf(grads["w_o"].reshape(N_DEV, D_MODEL // N_DEV, D_MODEL))]
        recv = _pair_exchange(slabs, [], name="exchange_pairs")
        names = ("w_in", "conv_w", "w_attn_out", "w_ssm_out", "w_o")
        sums = [_pair_add(a, b, core, name="pair_add_" + k, tc=4 * LANES if k == "w_in" else None)
                for k, a, b in zip(names, slabs, recv)]
        send_sems, recv_sems, srcs, lands, zero = _split_start("chips", sums, name="exchange_chips_start")
        return zero, (send_sems, recv_sems, srcs, lands)

    loss_part, grad_x, grads, (send_sems, recv_sems, srcs, lands) = _local_step(
        x.reshape(t, D_MODEL), loss_target.reshape(t, D_MODEL), nb, s, g_in.reshape(D_PROJ, D_MODEL),
        conv_full, conv_b, norm_w + o_zero[0:1, 0:1], dt_bias, a_log, d_skip, ssm_norm_w, out_weights,
        final_norm_w.reshape(1, D_MODEL), between=send_weight_grads)

    (p_small,) = _pair_exchange([], [_pack_small(grads, loss_part[0, 0])], name="exchange_small")
    srcs, lands = _split_wait("chips", send_sems, recv_sems, srcs, lands, grad_x, name="exchange_chips_wait")
    p_in, p_conv, p_wa, p_ws, p_wo = _own_slab(lands, srcs, chip, False)

    res = {}
    res["w_in"] = [a.T for a in _adamw(p_in, tr_(w_in), tr_(m_w_in), tr_(v_w_in), name="adamw_w_in", tc=2 * LANES)]
    res["conv_w"] = _adamw(p_conv, conv_w[0], m_conv_w[0], v_conv_w[0], name="adamw_conv_w")
    res["w_attn_out"] = _adamw(p_wa, w_attn_out[0], m_w_attn_out[0], v_w_attn_out[0], name="adamw_w_attn_out")
    res["w_ssm_out"] = _adamw(p_ws, w_ssm_out[0], m_w_ssm_out[0], v_w_ssm_out[0], name="adamw_w_ssm_out")
    res["w_o"] = _adamw(p_wo, w_o[0], m_w_o[0], v_w_o[0], name="adamw_w_o")
    sm = _adamw(p_small, _pack_small(weights), _pack_small(moms), _pack_small(vels), name="adamw_small")
    loss = sm[0].reshape(-1)[SMALL_USED]
    shapes = {k: weights[k].shape for k, _ in SMALL}
    sm = [_unpack_small(v, shapes) for v in sm]
    for k, _ in SMALL:
        res[k] = tuple(part[k] for part in sm)

    order = ("norm_w", "w_in", "conv_w", "conv_b", "dt_bias", "a_log", "d_skip", "ssm_norm_w",
             "w_attn_out", "w_ssm_out", "w_o", "final_norm_w")
    outs = [loss, grad_x.reshape(nb, s, D_MODEL)]
    for i in range(4):
        outs += [res[k][i].reshape(weights[k].shape) for k in order]
    return tuple(outs)
```

```python
import functools

import jax
import jax.numpy as jnp
from jax import lax
from jax.experimental import pallas as pl
from jax.experimental.pallas import tpu as pltpu

F32 = jnp.float32
BF16 = jnp.bfloat16

D_MODEL = 1024
SB_WIDTH = 1024
HEAD_DIM = 64
SSD_WIDTH = 2048
SSD_GROUPS = 4
SSD_HPG = 8
SSD_STATE = 128
SSD_CHUNK = 128
CONV_K = 4
CONV_DIM = 3072
N_DT = 32
D_PROJ = 11296
DT_COL0 = 9216
GATE_COL0 = DT_COL0 + N_DT
EPS = 1e-6
N_DEV = 8

ADAM_LR = 0.001
ADAM_B1 = 0.9
ADAM_B2 = 0.999
ADAM_EPS = 1e-08
ADAM_WD = 0.01
ADAM_STEP = 10

LANES = 128
VMEM_LIMIT = 56 * 1024 * 1024

CB_Q, CB_K, CB_V, CB_ZA, CB_ZS, CB_XBC = 0, 8, 16, 24, 32, 48

NT_DIMS = (((1,), (1,)), ((), ()))
TN_DIMS = (((0,), (0,)), ((), ()))


def _params(sem):
    return pltpu.CompilerParams(dimension_semantics=sem, vmem_limit_bytes=VMEM_LIMIT)


def _dot(a, b):
    return jnp.dot(a, b, preferred_element_type=F32)


def _dot_nt(a, b):
    return lax.dot_general(a, b, NT_DIMS, preferred_element_type=F32)


def _dot_tn(a, b):
    return lax.dot_general(a, b, TN_DIMS, preferred_element_type=F32)


def _bf(a):
    return a.astype(BF16)


def _sigmoid(x):
    return 0.5 + 0.5 * jnp.tanh(0.5 * x)


def _softplus(x):
    return jnp.maximum(x, 0.0) + jnp.log(1.0 + jnp.exp(-jnp.abs(x)))


def _trunc_split(a):
    bits = lax.bitcast_convert_type(a, jnp.uint32) & jnp.uint32(0xFFFF0000)
    hi = lax.bitcast_convert_type(bits, F32)
    return _bf(hi), _bf(a - hi)


MM_TILE = 1024
MM_TK = 2048
MM_TALL = 2048


def _matmul(a, b, *, name, nt=False, ta=False, add=None, out_dtype=F32, tm=MM_TILE, tn=MM_TILE, tk=MM_TK, n=None):
    k, m = a.shape if ta else a.shape[::-1]
    n = (b.shape[0] if nt else b.shape[1]) if n is None else n
    tm, tn, tk = min(tm, m), min(tn, n), min(tk, k)
    assert m % tm == 0 and n % tn == 0 and k % tk == 0, (name, a.shape, b.shape)
    assert not (ta and nt)
    nk = k // tk
    has_add = add is not None

    def body(*refs):
        a_ref, b_ref = refs[:2]
        add_ref = refs[2] if has_add else None
        o_ref = refs[3 if has_add else 2]
        acc_ref = refs[-1] if nk > 1 else None
        kk = pl.program_id(2)
        av, bv = _bf(a_ref[...]), _bf(b_ref[...])
        d = _dot_nt(av, bv) if nt else _dot_tn(av, bv) if ta else _dot(av, bv)
        if nk == 1:
            o_ref[...] = ((d + add_ref[...]) if has_add else d).astype(out_dtype)
            return

        @pl.when(kk == 0)
        def _():
            acc_ref[...] = jnp.zeros_like(acc_ref)

        acc_ref[...] += d

        @pl.when(kk == nk - 1)
        def _():
            r = acc_ref[...]
            if has_add:
                r = r + add_ref[...]
            o_ref[...] = r.astype(out_dtype)

    in_specs = [pl.BlockSpec((tk, tm), lambda j, i, kk: (kk, i)) if ta
                else pl.BlockSpec((tm, tk), lambda j, i, kk: (i, kk)),
                pl.BlockSpec((tn, tk), lambda j, i, kk: (j, kk)) if nt
                else pl.BlockSpec((tk, tn), lambda j, i, kk: (kk, j))]
    args = [a, b]
    if has_add:
        in_specs.append(pl.BlockSpec((tm, tn), lambda j, i, kk: (i, j)))
        args.append(add)
    return pl.pallas_call(
        body, grid=(n // tn, m // tm, nk), in_specs=in_specs,
        out_specs=pl.BlockSpec((tm, tn), lambda j, i, kk: (i, j)),
        out_shape=jax.ShapeDtypeStruct((m, n), out_dtype),
        scratch_shapes=[pltpu.VMEM((tm, tn), F32)] if nk > 1 else [],
        compiler_params=_params(("parallel", "parallel", "arbitrary")), name=name)(*args)


def _rms_fwd(x, w, *, tm=512):
    t = x.shape[0]

    def body(x_ref, w_ref, h_ref, r_ref):
        xv = x_ref[...]
        r = lax.rsqrt(jnp.mean(xv * xv, axis=-1, keepdims=True) + EPS)
        h_ref[...] = _bf(xv * r * w_ref[...])
        r_ref[...] = r

    row = lambda i: (i, 0)
    return pl.pallas_call(
        body, grid=(t // tm,),
        in_specs=[pl.BlockSpec((tm, D_MODEL), row), pl.BlockSpec((1, D_MODEL), lambda i: (0, 0))],
        out_specs=[pl.BlockSpec((tm, D_MODEL), row), pl.BlockSpec((tm, 1), row)],
        out_shape=[jax.ShapeDtypeStruct((t, D_MODEL), BF16), jax.ShapeDtypeStruct((t, 1), F32)],
        compiler_params=_params(("parallel",)), name="rms_fwd")(x, w)


def _rms_bwd(dh, x, r, w, dres, *, tm=512):
    t = x.shape[0]

    def body(dh_ref, x_ref, r_ref, w_ref, dres_ref, dx_ref, dw_ref):
        @pl.when(pl.program_id(0) == 0)
        def _():
            dw_ref[...] = jnp.zeros_like(dw_ref)

        nrm = x_ref[...] * r_ref[...]
        dhv = dh_ref[...]
        dw_ref[...] += jnp.sum(dhv * nrm, axis=0, keepdims=True)
        dn = dhv * w_ref[...]
        dx = r_ref[...] * (dn - nrm * jnp.mean(dn * nrm, axis=-1, keepdims=True))
        dx_ref[...] = dx + dres_ref[...]

    row = lambda i: (i, 0)
    fix = lambda i: (0, 0)
    return pl.pallas_call(
        body, grid=(t // tm,),
        in_specs=[pl.BlockSpec((tm, D_MODEL), row), pl.BlockSpec((tm, D_MODEL), row),
                  pl.BlockSpec((tm, 1), row), pl.BlockSpec((1, D_MODEL), fix),
                  pl.BlockSpec((tm, D_MODEL), row)],
        out_specs=[pl.BlockSpec((tm, D_MODEL), row), pl.BlockSpec((1, D_MODEL), fix)],
        out_shape=[jax.ShapeDtypeStruct((t, D_MODEL), F32), jax.ShapeDtypeStruct((1, D_MODEL), F32)],
        compiler_params=_params(("arbitrary",)), name="rms_bwd")(dh, x, r, w, dres)


ATT_BLK = 256


def _attn_masks(blk):
    r_io = lax.broadcasted_iota(jnp.int32, (blk, blk), 0)
    c_io = lax.broadcasted_iota(jnp.int32, (blk, blk), 1)
    return r_io, c_io


def _attn_tiles(nq):
    return nq * (nq + 1) // 2


def _attn_fwd(proj, nb, s):
    blk = min(ATT_BLK, s)
    nq = s // blk
    nt = _attn_tiles(nq)
    scale = HEAD_DIM ** -0.5

    def body(q_ref, k_ref, v_ref, o_ref, a_out, sg_out, q0_s, q1_s, kb_s, v0_s, v1_s, z_s, a_s, acc_s, cr_s):
        head0 = lax.broadcasted_iota(jnp.int32, (s, LANES), 1) < HEAD_DIM
        r_io, c_io = _attn_masks(blk)
        tri = c_io < r_io
        tri2 = jnp.concatenate([tri, tri], axis=0)
        u_gt = _bf(r_io > c_io)
        u2 = jnp.concatenate([u_gt, u_gt], axis=0)
        qv = q_ref[...] * scale
        q0_s[...] = _bf(jnp.where(head0, qv, 0.0))
        q1_s[...] = _bf(jnp.where(head0, 0.0, qv))
        kb_s[...] = _bf(k_ref[...])
        vv = v_ref[...]
        v0_s[...] = _bf(jnp.where(head0, vv, 0.0))
        v1_s[...] = _bf(jnp.where(head0, 0.0, vv))

        def scores(qcat, kblk, slot):
            k0 = pl.multiple_of(jnp.maximum(kblk, 0) * blk, blk)
            z_s[slot] = _dot_nt(qcat, kb_s[pl.ds(k0, blk), :])

        def weights(slot, tile, diag, zslot=None):
            z = z_s[slot if zslot is None else zslot]
            cr = cr_s[...]
            sp = _softplus(z)
            spm = jnp.where(tri2, sp, 0.0) if diag else sp
            hi, lo = _trunc_split(spm)
            ps = _dot(jnp.concatenate([hi, lo], axis=1), u2)
            lb = z - sp
            a = jnp.exp(lb - (ps + cr))
            if diag:
                a = jnp.where(tri2, a, 0.0)
            ab = _bf(a)
            a_s[slot] = ab
            a_out[tile] = ab
            sg_out[tile] = _bf(jnp.exp(lb))
            cr_s[...] = cr + ps[:, 0:1] + spm[:, 0:1]

        def apply(slot, k0):
            a = a_s[slot]
            vcat = jnp.concatenate([v0_s[pl.ds(k0, blk), :], v1_s[pl.ds(k0, blk), :]], axis=0)
            acc_s[...] += _dot(jnp.concatenate([a[:blk], a[blk:]], axis=1), vcat)

        def queries(qi):
            q0 = pl.multiple_of(qi * blk, blk)
            return jnp.concatenate([q0_s[pl.ds(q0, blk), :], q1_s[pl.ds(q0, blk), :]], axis=0)

        def first_scores(qi):
            qcat = queries(qi)
            scores(qcat, qi, 2)
            scores(qcat, qi - 1, 3)

        first_scores(jnp.int32(0))

        def qblock(qi, _):
            q0 = pl.multiple_of(qi * blk, blk)
            tile0 = qi * (qi + 1) // 2
            qcat = queries(qi)
            z_s[1] = z_s[3]
            cr_s[...] = jnp.zeros_like(cr_s)
            acc_s[...] = jnp.zeros_like(acc_s)
            weights(0, tile0 + qi, True, zslot=2)
            first_scores(jnp.minimum(qi + 1, nq - 1))

            def step(j, slot):
                scores(qcat, qi - j - 1, 1 - slot)
                weights(slot, tile0 + qi - j, False)
                apply(1 - slot, pl.multiple_of((qi - j + 1) * blk, blk))

            def pair(i, _):
                step(2 * i + 1, 1)
                step(2 * i + 2, 0)
                return 0

            lax.fori_loop(0, qi // 2, pair, 0)

            @pl.when(qi % 2 == 1)
            def _():
                step(qi, 1)

            apply(qi & 1, 0)
            o_ref[pl.ds(q0, blk), :] = acc_s[...]
            return 0

        lax.fori_loop(0, nq, qblock, 0)

    t = nb * s
    tiles = pl.BlockSpec((None, None, nt, 2 * blk, blk), lambda b, p: (b, p, 0, 0, 0))
    tiles_shape = jax.ShapeDtypeStruct((nb, 8, nt, 2 * blk, blk), BF16)
    return pl.pallas_call(
        body, grid=(nb, 8),
        in_specs=[pl.BlockSpec((s, LANES), lambda b, p: (b, CB_Q + p)),
                  pl.BlockSpec((s, LANES), lambda b, p: (b, CB_K + p)),
                  pl.BlockSpec((s, LANES), lambda b, p: (b, CB_V + p))],
        out_specs=[pl.BlockSpec((s, LANES), lambda b, p: (b, p)), tiles, tiles],
        out_shape=[jax.ShapeDtypeStruct((t, SB_WIDTH), F32), tiles_shape, tiles_shape],
        scratch_shapes=[pltpu.VMEM((s, LANES), BF16)] * 5 + [
            pltpu.VMEM((4, 2 * blk, blk), F32), pltpu.VMEM((2, 2 * blk, blk), BF16),
            pltpu.VMEM((blk, LANES), F32), pltpu.VMEM((2 * blk, 1), F32)],
        compiler_params=_params(("parallel", "parallel")), name="attn_fwd")(proj, proj, proj)


def _attn_bwd(proj, do, a_all, sg_all, nb, s):
    blk = min(ATT_BLK, s)
    nq = s // blk
    nt = _attn_tiles(nq)
    scale = HEAD_DIM ** -0.5

    def body(q_ref, k_ref, v_ref, do_ref, a_in, sg_in, dq_ref, dk_ref, dv_ref,
             qt0_s, qt1_s, k0_s, k1_s, vb_s, d0_s, d1_s, dt0_s, dt1_s, da_s, dz_s, dkt_acc, dvt_acc, dq_acc, cw_s):
        head0 = lax.broadcasted_iota(jnp.int32, (s, LANES), 1) < HEAD_DIM
        r_io, c_io = _attn_masks(blk)
        tri = c_io < r_io
        tri2 = jnp.concatenate([tri, tri], axis=0)
        u_lt = _bf(r_io < c_io)
        head0_t = lax.broadcasted_iota(jnp.int32, (LANES, s), 0) < HEAD_DIM
        q_t = (q_ref[...].astype(F32) * scale).T
        qt0_s[...] = _bf(jnp.where(head0_t, q_t, 0.0))
        qt1_s[...] = _bf(jnp.where(head0_t, 0.0, q_t))
        kv = k_ref[...]
        k0_s[...] = _bf(jnp.where(head0, kv, 0.0))
        k1_s[...] = _bf(jnp.where(head0, 0.0, kv))
        vb_s[...] = _bf(v_ref[...])
        dov = do_ref[...]
        d0_s[...] = _bf(jnp.where(head0, dov, 0.0))
        d1_s[...] = _bf(jnp.where(head0, 0.0, dov))
        do_t = dov.astype(F32).T
        dt0_s[...] = _bf(jnp.where(head0_t, do_t, 0.0))
        dt1_s[...] = _bf(jnp.where(head0_t, 0.0, do_t))
        dkt_acc[...] = jnp.zeros_like(dkt_acc)
        dvt_acc[...] = jnp.zeros_like(dvt_acc)

        def qblock(qi, _):
            q0 = pl.multiple_of(qi * blk, blk)
            tile0 = qi * (qi + 1) // 2
            qs_ = pl.ds(q0, blk)
            dcat = jnp.concatenate([d0_s[qs_, :], d1_s[qs_, :]], axis=0)
            qcat_t = jnp.concatenate([qt0_s[:, qs_], qt1_s[:, qs_]], axis=1)
            dcat_t = jnp.concatenate([dt0_s[:, qs_], dt1_s[:, qs_]], axis=1)

            def scores(kblk, slot):
                da_s[slot] = _dot_nt(dcat, vb_s[pl.ds(pl.multiple_of(kblk * blk, blk), blk), :])

            def grads(slot, kblk, diag):
                cw = cw_s[...]
                w = a_in[tile0 + kblk].astype(F32) * da_s[slot]
                wex = _dot(_bf(w), u_lt)
                dz = w - sg_in[tile0 + kblk].astype(F32) * (w + (wex + cw))
                if diag:
                    dz = jnp.where(tri2, dz, 0.0)
                dz_s[slot] = _bf(dz)
                cw_s[...] = cw + wex[:, blk - 1:blk] + w[:, blk - 1:blk]

            def apply(slot, kblk):
                ks_ = pl.ds(pl.multiple_of(kblk * blk, blk), blk)
                dz = dz_s[slot]
                dq_acc[...] += _dot(jnp.concatenate([dz[:blk], dz[blk:]], axis=1),
                                    jnp.concatenate([k0_s[ks_, :], k1_s[ks_, :]], axis=0))
                dkt_acc[:, ks_] += _dot(qcat_t, dz)
                dvt_acc[:, ks_] += _dot(dcat_t, a_in[tile0 + kblk])

            def step(j, slot, diag, first=False):
                if not diag:
                    scores(j + 1, 1 - slot)
                grads(slot, j, diag)
                if not first:
                    apply(1 - slot, j - 1)

            scores(0, 0)
            dq_acc[...] = jnp.zeros_like(dq_acc)
            cw_s[...] = jnp.zeros_like(cw_s)

            @pl.when(qi == 0)
            def _():
                step(0, 0, True, first=True)

            @pl.when(qi > 0)
            def _():
                step(0, 0, False, first=True)

            def pair(i, _):
                step(2 * i + 1, 1, False)
                step(2 * i + 2, 0, False)
                return 0

            lax.fori_loop(0, (qi - 1) // 2, pair, 0)

            @pl.when((qi > 0) & (qi % 2 == 0))
            def _():
                step(qi - 1, 1, False)
                step(qi, 0, True)

            @pl.when(qi % 2 == 1)
            def _():
                step(qi, 1, True)

            apply(qi & 1, qi)
            dq_ref[qs_, :] = _bf(dq_acc[...] * scale)
            return 0

        lax.fori_loop(0, nq, qblock, 0)
        dk_ref[...] = _bf(dkt_acc[...].T)
        dv_ref[...] = _bf(dvt_acc[...].T)

    t = nb * s
    hp = lambda b, p: (b, p)
    out = jax.ShapeDtypeStruct((t, SB_WIDTH), BF16)
    tiles = pl.BlockSpec((None, None, nt, 2 * blk, blk), lambda b, p: (b, p, 0, 0, 0))
    return pl.pallas_call(
        body, grid=(nb, 8),
        in_specs=[pl.BlockSpec((s, LANES), lambda b, p: (b, CB_Q + p)),
                  pl.BlockSpec((s, LANES), lambda b, p: (b, CB_K + p)),
                  pl.BlockSpec((s, LANES), lambda b, p: (b, CB_V + p)),
                  pl.BlockSpec((s, LANES), hp), tiles, tiles],
        out_specs=[pl.BlockSpec((s, LANES), hp)] * 3,
        out_shape=[out, out, out],
        scratch_shapes=[pltpu.VMEM((LANES, s), BF16)] * 2 + [pltpu.VMEM((s, LANES), BF16)] * 5 + [
            pltpu.VMEM((LANES, s), BF16)] * 2 + [
            pltpu.VMEM((2, 2 * blk, blk), F32), pltpu.VMEM((2, 2 * blk, blk), BF16),
            pltpu.VMEM((LANES, s), F32), pltpu.VMEM((LANES, s), F32),
            pltpu.VMEM((blk, LANES), F32), pltpu.VMEM((2 * blk, 1), F32)],
        compiler_params=_params(("parallel", "parallel")), name="attn_bwd")(proj, proj, proj, do, a_all, sg_all)


CONV_TC = 256
CONV_ROWS = 64
HALO = 8


def _conv_rows(ws, w_ref, b_ref, n):
    taps = [ws[HALO - kk:HALO - kk + n, :] for kk in range(CONV_K)]
    y = b_ref[...] + w_ref[CONV_K - 1:CONV_K, :] * taps[0]
    for kk in range(1, CONV_K):
        y = y + w_ref[CONV_K - 1 - kk:CONV_K - kk, :] * taps[kk]
    return y, taps


def _conv_fwd(proj, conv_w, conv_b, nb, s):
    tc, rows = CONV_TC, min(CONV_ROWS, s)
    cb0 = CB_XBC * LANES // tc

    def body(x_ref, w_ref, b_ref, o_ref, xp, ws):
        xp[0:HALO, :] = jnp.zeros((HALO, tc), F32)
        xp[HALO:HALO + s, :] = x_ref[...].astype(F32)

        def chunk(i, _):
            r0 = pl.multiple_of(i * rows, rows)
            ws[...] = xp[pl.ds(r0, rows + HALO), :]
            y, _ = _conv_rows(ws, w_ref, b_ref, rows)
            o_ref[pl.ds(r0, rows), :] = y * _sigmoid(y)
            return 0

        lax.fori_loop(0, s // rows, chunk, 0)

    t = nb * s
    return pl.pallas_call(
        body, grid=(nb, CONV_DIM // tc),
        in_specs=[pl.BlockSpec((s, tc), lambda b, c: (b, cb0 + c)),
                  pl.BlockSpec((CONV_K, tc), lambda b, c: (0, c)),
                  pl.BlockSpec((1, tc), lambda b, c: (0, c))],
        out_specs=pl.BlockSpec((s, tc), lambda b, c: (b, c)),
        out_shape=jax.ShapeDtypeStruct((t, CONV_DIM), F32),
        scratch_shapes=[pltpu.VMEM((s + HALO, tc), F32), pltpu.VMEM((rows + HALO, tc), F32)],
        compiler_params=_params(("parallel", "parallel")), name="conv_fwd")(proj, conv_w, conv_b)


def _conv_bwd(proj, dact, conv_w, conv_b, dproj, nb, s, *, col0, name):
    width = dact.shape[1]
    tc, rows = min(CONV_TC, width), min(CONV_ROWS, s)
    cb0 = (CB_XBC * LANES + col0) // tc
    wb0 = col0 // tc

    def body(x_ref, d_ref, w_ref, b_ref, _, dx_ref, dw_ref, db_ref, xp, dp, dys, ws):
        @pl.when(pl.program_id(1) == 0)
        def _():
            dw_ref[...] = jnp.zeros_like(dw_ref)
            db_ref[...] = jnp.zeros_like(db_ref)

        n = rows + HALO
        pad = jnp.zeros((HALO, tc), F32)
        xp[0:HALO, :] = pad
        xp[HALO:HALO + s, :] = x_ref[...].astype(F32)
        xp[HALO + s:HALO + s + HALO, :] = pad
        dp[0:s, :] = d_ref[...]
        dp[s:s + HALO, :] = pad

        def chunk(i, acc):
            r0 = pl.multiple_of(i * rows, rows)
            ws[...] = xp[pl.ds(r0, n + HALO), :]
            y, taps = _conv_rows(ws, w_ref, b_ref, n)
            sg = _sigmoid(y)
            dy = dp[pl.ds(r0, n), :] * (sg * (1.0 + y * (1.0 - sg)))
            dys[...] = dy
            dx = w_ref[CONV_K - 1:CONV_K, :] * dy[0:rows]
            for kk in range(1, CONV_K):
                dx = dx + w_ref[CONV_K - 1 - kk:CONV_K - kk, :] * dys[kk:kk + rows, :]
            dx_ref[pl.ds(r0, rows), :] = _bf(dx)
            dyc = dy[0:rows]
            new = [acc[0] + jnp.sum(dyc, axis=0, keepdims=True)]
            for kk in range(CONV_K):
                new.append(acc[1 + kk] + jnp.sum(dyc * taps[kk][0:rows], axis=0, keepdims=True))
            return tuple(new)

        acc = lax.fori_loop(0, s // rows, chunk, (jnp.zeros((1, tc), F32),) * (1 + CONV_K))
        db_ref[...] += acc[0]
        for kk in range(CONV_K):
            dw_ref[CONV_K - 1 - kk:CONV_K - kk, :] += acc[1 + kk]

    t = nb * s
    return pl.pallas_call(
        body, grid=(width // tc, nb),
        in_specs=[pl.BlockSpec((s, tc), lambda c, b: (b, cb0 + c)),
                  pl.BlockSpec((s, tc), lambda c, b: (b, c)),
                  pl.BlockSpec((CONV_K, tc), lambda c, b: (0, wb0 + c)),
                  pl.BlockSpec((1, tc), lambda c, b: (0, wb0 + c)), pl.BlockSpec(memory_space=pl.ANY)],
        out_specs=[pl.BlockSpec((s, tc), lambda c, b: (b, cb0 + c)),
                   pl.BlockSpec((CONV_K, tc), lambda c, b: (0, c)),
                   pl.BlockSpec((1, tc), lambda c, b: (0, c))],
        out_shape=[jax.ShapeDtypeStruct(dproj.shape, dproj.dtype),
                   jax.ShapeDtypeStruct((CONV_K, width), F32),
                   jax.ShapeDtypeStruct((1, width), F32)],
        input_output_aliases={4: 0},
        scratch_shapes=[pltpu.VMEM((s + 2 * HALO, tc), F32), pltpu.VMEM((s + HALO, tc), F32),
                        pltpu.VMEM((rows + HALO, tc), F32), pltpu.VMEM((rows + 2 * HALO, tc), F32)],
        compiler_params=_params(("parallel", "arbitrary")), name=name)(proj, dact, conv_w, conv_b, dproj)


L = SSD_CHUNK
N_PAIR = SSD_HPG // 2


def _pair_lanes(v0, v1, lane):
    return jnp.where(lane < HEAD_DIM, v0, v1)


def _onehot(shape, row_shift, col_shift):
    row = lax.broadcasted_iota(jnp.int32, shape, 0)
    col = lax.broadcasted_iota(jnp.int32, shape, 1)
    return _bf(lax.shift_right_logical(row, row_shift) == lax.shift_right_logical(col, col_shift))


def _ssd_tables():
    sp = _onehot((LANES, N_PAIR * LANES), 0, 6)
    sf = _onehot((LANES, SSD_HPG * LANES), 0, 7)
    return (jnp.concatenate([sp] * 3, axis=0), jnp.concatenate([sf] * 3, axis=0),
            _onehot((N_PAIR * LANES, LANES), 6, 0), _onehot((SSD_HPG * LANES, LANES), 7, 0))


def _split3(a):
    a1 = _bf(a)
    r1 = a - a1.astype(F32)
    a2 = _bf(r1)
    return jnp.concatenate([a1, a2, _bf(r1 - a2.astype(F32))], axis=1)


def _split2(a):
    hi, lo = _trunc_split(a)
    return jnp.concatenate([hi, lo], axis=1)


def _tri_dot(tri_bf, a):
    a1 = _bf(a)
    r1 = a - a1.astype(F32)
    a2 = _bf(r1)
    parts = jnp.concatenate([a1, a2, _bf(r1 - a2.astype(F32))], axis=0)
    return _dot(jnp.concatenate([tri_bf] * 3, axis=1), parts)


HEAD_W = 4 * LANES + 2 * N_PAIR * LANES + SSD_HPG * LANES


def _ssd_head(dtr_ref, bias_ref, a_neg, c, tri_bf, spread_pair, spread_full, head_s, slot):
    raw = dtr_ref[pl.ds(pl.multiple_of(c * L, L), L), :] + bias_ref[...]
    dt = _softplus(raw)
    acs = _tri_dot(tri_bf, dt * a_neg)
    acs3 = _split3(acs)
    both = _dot(jnp.concatenate([_split3(dt), acs3], axis=0), spread_pair)
    parts = (raw, dt, acs, acs.T, both[:L], both[L:], _dot(acs3, spread_full))
    off = 0
    for p in parts:
        head_s[slot, :, off:off + p.shape[1]] = p
        off += p.shape[1]


def _ssd_head_load(head_s, slot):
    out, off = [], 0
    for w in (LANES, LANES, LANES, L, N_PAIR * LANES, N_PAIR * LANES, SSD_HPG * LANES):
        out.append(head_s[slot, :, off:off + w])
        off += w
    return out


def _ssd_fwd(xbc, dtr, bias, alog, dskip, nb, s):
    nc = s // L
    t = nb * s

    def body(x_ref, b_ref, c_ref, dtr_ref, bias_ref, alog_ref, dsk_ref, y_ref, hs_ref, h_scr, head_s):
        lane = lax.broadcasted_iota(jnp.int32, (L, LANES), 1)
        lane1 = lax.broadcasted_iota(jnp.int32, (1, LANES), 1)
        rowc = lax.broadcasted_iota(jnp.int32, (L, 1), 0)
        rr = lax.broadcasted_iota(jnp.int32, (L, L), 0)
        cc = lax.broadcasted_iota(jnp.int32, (L, L), 1)
        causal = rr >= cc
        h_scr[...] = jnp.zeros_like(h_scr)
        dsk = dsk_ref[...]
        spread_pair, spread_full, _, _ = _ssd_tables()
        a_neg = -jnp.exp(alog_ref[...])
        head = functools.partial(_ssd_head, dtr_ref, bias_ref, a_neg, tri_bf=_bf(cc <= rr),
                                 spread_pair=spread_pair, spread_full=spread_full, head_s=head_s)
        head(jnp.int32(0), slot=0)

        def chunk(c, slot):
            r0 = pl.multiple_of(c * L, L)
            head(jnp.minimum(c + 1, nc - 1), slot=1 - slot)
            _, _, acs, acs_t, dt_all, acs_all, acs_full = _ssd_head_load(head_s, slot)
            bb = _bf(b_ref[pl.ds(r0, L), :])
            cb = _bf(c_ref[pl.ds(r0, L), :])
            gm = _dot_nt(cb, bb)
            for pr in range(N_PAIR):
                j0, j1 = 2 * pr, 2 * pr + 1
                ms = []
                for j in (j0, j1):
                    seg = acs_full[:, j * LANES:(j + 1) * LANES] - acs_t[j:j + 1, :]
                    ms.append(_bf(gm * jnp.exp(jnp.where(causal, seg, -1e30))))
                x = x_ref[pl.ds(r0, L), pr * LANES:(pr + 1) * LANES]
                dt_p = dt_all[:, pr * LANES:(pr + 1) * LANES]
                acs_p = acs_all[:, pr * LANES:(pr + 1) * LANES]
                last_p = acs_p[L - 1:L, :]
                d_p = _pair_lanes(dsk[:, j0:j0 + 1], dsk[:, j1:j1 + 1], lane1)
                xd = x * dt_p
                xdb = _bf(xd)
                yds = _dot(jnp.concatenate(ms, axis=0), xdb)
                yd = _pair_lanes(yds[:L], yds[L:], lane)
                hp = h_scr[pr]
                yo = _dot_nt(cb, _bf(hp)) * jnp.exp(acs_p)
                y_ref[pl.ds(r0, L), pr * LANES:(pr + 1) * LANES] = yd + yo + x * d_p
                hs_ref[c, pr] = hp
                sp = _dot_tn(_bf(xd * jnp.exp(last_p - acs_p)), bb)
                cd = jnp.where(rowc < HEAD_DIM, jnp.exp(acs[L - 1:L, j0:j0 + 1]),
                               jnp.exp(acs[L - 1:L, j1:j1 + 1]))
                h_scr[pr] = hp * cd + sp

        def two_chunks(i, _):
            chunk(2 * i, 0)
            chunk(2 * i + 1, 1)
            return 0

        lax.fori_loop(0, nc // 2, two_chunks, 0)

    assert nc % 2 == 0
    grp = lambda b, g: (g, 0, 0)
    return pl.pallas_call(
        body, grid=(nb, SSD_GROUPS),
        in_specs=[pl.BlockSpec((s, 4 * LANES), lambda b, g: (b, g)),
                  pl.BlockSpec((s, LANES), lambda b, g: (b, 16 + g)),
                  pl.BlockSpec((s, LANES), lambda b, g: (b, 20 + g)),
                  pl.BlockSpec((None, s, LANES), lambda b, g: (g, b, 0)),
                  pl.BlockSpec((None, 1, LANES), grp), pl.BlockSpec((None, 1, LANES), grp),
                  pl.BlockSpec((None, 1, LANES), grp)],
        out_specs=[pl.BlockSpec((s, 4 * LANES), lambda b, g: (b, g)),
                   pl.BlockSpec((None, None, nc, N_PAIR, L, LANES), lambda b, g: (b, g, 0, 0, 0, 0))],
        out_shape=[jax.ShapeDtypeStruct((t, SSD_WIDTH), F32),
                   jax.ShapeDtypeStruct((nb, SSD_GROUPS, nc, N_PAIR, L, LANES), F32)],
        scratch_shapes=[pltpu.VMEM((N_PAIR, L, LANES), F32), pltpu.VMEM((2, L, HEAD_W), F32)],
        compiler_params=_params(("parallel", "parallel")), name="ssd_fwd")(
            xbc, xbc, xbc, dtr, bias, alog, dskip)


def _ssd_bwd(xbc, dtr, bias, alog, dskip, hs, dy, nb, s):
    nc = s // L
    t = nb * s

    def body(x_ref, b_ref, c_ref, dtr_ref, bias_ref, alog_ref, dsk_ref, hs_ref, dy_ref,
             dx_ref, db_ref, dc_ref, ddtr_ref, dbias_ref, dalog_ref, ddsk_ref, dh_scr, head_s):
        lane = lax.broadcasted_iota(jnp.int32, (L, LANES), 1)
        lane1 = lax.broadcasted_iota(jnp.int32, (1, LANES), 1)
        rowc = lax.broadcasted_iota(jnp.int32, (L, 1), 0)
        rr = lax.broadcasted_iota(jnp.int32, (L, L), 0)
        cc = lax.broadcasted_iota(jnp.int32, (L, L), 1)
        causal = rr >= cc
        tri_t = _bf(cc >= rr)
        dh_scr[...] = jnp.zeros_like(dh_scr)
        dsk = dsk_ref[...]
        a_neg = -jnp.exp(alog_ref[...])

        @pl.when(pl.program_id(1) == 0)
        def _():
            dbias_ref[...] = jnp.zeros_like(dbias_ref)
            dalog_ref[...] = jnp.zeros_like(dalog_ref)
            ddsk_ref[...] = jnp.zeros_like(ddsk_ref)

        spread_pair, spread_full, sum_pair, sum_full = _ssd_tables()
        sum_pair2 = jnp.concatenate([sum_pair] * 2, axis=0)
        sum_full2 = jnp.concatenate([sum_full] * 2, axis=0)
        head = functools.partial(_ssd_head, dtr_ref, bias_ref, a_neg, tri_bf=_bf(cc <= rr),
                                 spread_pair=spread_pair, spread_full=spread_full, head_s=head_s)
        head(jnp.int32(nc - 1), slot=0)

        def chunk(c, slot):
            r0 = pl.multiple_of(c * L, L)
            head(jnp.maximum(c - 1, 0), slot=1 - slot)
            raw, dt, acs, acs_t, dt_all, acs_all, acs_full = _ssd_head_load(head_s, slot)
            bb = _bf(b_ref[pl.ds(r0, L), :])
            cb = _bf(c_ref[pl.ds(r0, L), :])
            gm = _dot_nt(cb, bb)
            dg = jnp.zeros((L, L), F32)
            dbacc = jnp.zeros((L, LANES), F32)
            dcacc = jnp.zeros((L, LANES), F32)
            dacs_t = jnp.zeros((L, L), F32)
            ta_all, te_all, qm_all, tc_all, td_all, dcd_all = [], [], [], [], [], []
            for pr in range(N_PAIR):
                js = (2 * pr, 2 * pr + 1)
                lms, mfs = [], []
                for j in js:
                    seg = acs_full[:, j * LANES:(j + 1) * LANES] - acs_t[j:j + 1, :]
                    lm = jnp.exp(jnp.where(causal, seg, -1e30))
                    lms.append(lm)
                    mfs.append(gm * lm)
                x = x_ref[pl.ds(r0, L), pr * LANES:(pr + 1) * LANES]
                dyv = dy_ref[pl.ds(r0, L), pr * LANES:(pr + 1) * LANES]
                dt_p = dt_all[:, pr * LANES:(pr + 1) * LANES]
                acs_p = acs_all[:, pr * LANES:(pr + 1) * LANES]
                last_p = acs_p[L - 1:L, :]
                d_p = _pair_lanes(dsk[:, js[0]:js[0] + 1], dsk[:, js[1]:js[1] + 1], lane1)
                e_p = jnp.exp(acs_p)
                dte_p = jnp.exp(last_p - acs_p)
                xd = x * dt_p
                xdb = _bf(xd)
                hp = hs_ref[c, pr]
                hb = _bf(hp)
                dye = _bf(dyv * e_p)
                yo = _dot_nt(cb, hb) * e_p
                dcacc = dcacc + _dot(dye, hb)
                dh_out = _dot_tn(dye, cb)
                dyb = _bf(dyv)
                dms = _dot_nt(_bf(jnp.concatenate(
                    [jnp.where(lane < HEAD_DIM, dyv, 0.0), jnp.where(lane < HEAD_DIM, 0.0, dyv)], axis=0)), xdb)
                mty = _dot_tn(_bf(jnp.concatenate(mfs, axis=1)), dyb)
                dxd = jnp.where(lane < HEAD_DIM, mty[:L], mty[L:])
                for hh in range(2):
                    dm = dms[hh * L:(hh + 1) * L]
                    dg = dg + dm * lms[hh]
                    qm = dm * mfs[hh]
                    qm_all.append(qm)
                    dacs_t = dacs_t - jnp.where(rr == js[hh], jnp.sum(qm, axis=0, keepdims=True), 0.0)
                ds = dh_scr[pr]
                dsb = _bf(ds)
                xdd = xd * dte_p
                dxdd = _dot_nt(bb, dsb)
                dbacc = dbacc + _dot(_bf(xdd), dsb)
                dxd = dxd + dxdd * dte_p
                cd = jnp.where(rowc < HEAD_DIM, jnp.exp(acs[L - 1:L, js[0]:js[0] + 1]),
                               jnp.exp(acs[L - 1:L, js[1]:js[1] + 1]))
                prod = ds * hp
                dcd_all += [jnp.sum(prod[:HEAD_DIM], axis=0, keepdims=True),
                            jnp.sum(prod[HEAD_DIM:], axis=0, keepdims=True)]
                dh_scr[pr] = dh_out + ds * cd
                tcv = dxdd * xdd
                ta_all.append(dyv * yo - tcv)
                tc_all.append(jnp.sum(tcv, axis=0, keepdims=True))
                te_all.append(dxd * x)
                td_all.append(jnp.sum(dyv * x, axis=0, keepdims=True))
                dx_ref[pl.ds(r0, L), pr * LANES:(pr + 1) * LANES] = dxd * dt_p + dyv * d_p
            dgb = _bf(dg)
            dc_ref[pl.ds(r0, L), :] = dcacc + _dot(dgb, bb)
            db_ref[pl.ds(r0, L), :] = dbacc + _dot_tn(dgb, cb)
            last_add = jnp.zeros((1, LANES), F32)
            dd_add = jnp.zeros((1, LANES), F32)
            cd_row = jnp.exp(acs[L - 1:L, :])
            for j in range(SSD_HPG):
                hm = (lane1 >= HEAD_DIM) if j % 2 else (lane1 < HEAD_DIM)
                tcj = jnp.sum(jnp.where(hm, tc_all[j // 2], 0.0), axis=1, keepdims=True)
                tdj = jnp.sum(jnp.where(hm, td_all[j // 2], 0.0), axis=1, keepdims=True)
                dcdj = jnp.sum(dcd_all[j], axis=1, keepdims=True)
                last_add = last_add + jnp.where(lane1 == j, tcj + dcdj * cd_row, 0.0)
                dd_add = dd_add + jnp.where(lane1 == j, tdj, 0.0)
            dacs = (_dot(_split2(jnp.concatenate(ta_all, axis=1)), sum_pair2)
                    + _dot(_split2(jnp.concatenate(qm_all, axis=1)), sum_full2)
                    + dacs_t.T + jnp.where(rowc == L - 1, last_add, 0.0))
            ddta = _tri_dot(tri_t, dacs)
            ddt = _dot(_bf(jnp.concatenate(te_all, axis=1)), sum_pair) + ddta * a_neg
            ddraw = ddt * _sigmoid(raw)
            ddtr_ref[pl.ds(r0, L), :] = ddraw
            dbias_ref[...] += jnp.sum(ddraw, axis=0, keepdims=True)
            dalog_ref[...] += jnp.sum(ddta * dt, axis=0, keepdims=True) * a_neg
            ddsk_ref[...] += dd_add

        def two_chunks(i, _):
            chunk(nc - 1 - 2 * i, 0)
            chunk(nc - 2 - 2 * i, 1)
            return 0

        lax.fori_loop(0, nc // 2, two_chunks, 0)

    assert nc % 2 == 0
    grp = lambda g, b: (g, 0, 0)
    small = jax.ShapeDtypeStruct((SSD_GROUPS, 1, LANES), F32)
    return pl.pallas_call(
        body, grid=(SSD_GROUPS, nb),
        in_specs=[pl.BlockSpec((s, 4 * LANES), lambda g, b: (b, g)),
                  pl.BlockSpec((s, LANES), lambda g, b: (b, 16 + g)),
                  pl.BlockSpec((s, LANES), lambda g, b: (b, 20 + g)),
                  pl.BlockSpec((None, s, LANES), lambda g, b: (g, b, 0)),
                  pl.BlockSpec((None, 1, LANES), grp), pl.BlockSpec((None, 1, LANES), grp),
                  pl.BlockSpec((None, 1, LANES), grp),
                  pl.BlockSpec((None, None, nc, N_PAIR, L, LANES), lambda g, b: (b, g, 0, 0, 0, 0)),
                  pl.BlockSpec((s, 4 * LANES), lambda g, b: (b, g))],
        out_specs=[pl.BlockSpec((s, 4 * LANES), lambda g, b: (b, g)),
                   pl.BlockSpec((s, LANES), lambda g, b: (b, g)),
                   pl.BlockSpec((s, LANES), lambda g, b: (b, g)),
                   pl.BlockSpec((None, s, LANES), lambda g, b: (g, b, 0)),
                   pl.BlockSpec((None, 1, LANES), grp), pl.BlockSpec((None, 1, LANES), grp),
                   pl.BlockSpec((None, 1, LANES), grp)],
        out_shape=[jax.ShapeDtypeStruct((t, SSD_WIDTH), F32),
                   jax.ShapeDtypeStruct((t, SSD_GROUPS * SSD_STATE), F32),
                   jax.ShapeDtypeStruct((t, SSD_GROUPS * SSD_STATE), F32),
                   jax.ShapeDtypeStruct((SSD_GROUPS, t, LANES), F32), small, small, small],
        scratch_shapes=[pltpu.VMEM((N_PAIR, L, LANES), F32), pltpu.VMEM((2, L, HEAD_W), F32)],
        compiler_params=_params(("parallel", "arbitrary")), name="ssd_bwd")(
            xbc, xbc, xbc, dtr, bias, alog, dskip, hs, dy)


GN = SSD_WIDTH // SSD_GROUPS


def _post_fwd(o, proj, y, nw, *, tm=256):
    t = o.shape[0]

    def body(o_ref, za_ref, y_ref, zs_ref, nw_ref, ya_ref, ys_ref):
        za = za_ref[...].astype(F32)
        ya_ref[...] = _bf(o_ref[...] * (za * _sigmoid(za)))
        zs = zs_ref[...].astype(F32)
        yz = y_ref[...] * (zs * _sigmoid(zs))
        for g in range(SSD_GROUPS):
            sl = slice(g * GN, (g + 1) * GN)
            v = yz[:, sl]
            r = lax.rsqrt(jnp.mean(v * v, axis=-1, keepdims=True) + EPS)
            ys_ref[:, sl] = _bf(v * r * nw_ref[:, sl])

    return pl.pallas_call(
        body, grid=(t // tm,),
        in_specs=[pl.BlockSpec((tm, SB_WIDTH), lambda i: (i, 0)),
                  pl.BlockSpec((tm, SB_WIDTH), lambda i: (i, CB_ZA * LANES // SB_WIDTH)),
                  pl.BlockSpec((tm, SSD_WIDTH), lambda i: (i, 0)),
                  pl.BlockSpec((tm, SSD_WIDTH), lambda i: (i, CB_ZS * LANES // SSD_WIDTH)),
                  pl.BlockSpec((1, SSD_WIDTH), lambda i: (0, 0))],
        out_specs=[pl.BlockSpec((tm, SB_WIDTH), lambda i: (i, 0)),
                   pl.BlockSpec((tm, SSD_WIDTH), lambda i: (i, 0))],
        out_shape=[jax.ShapeDtypeStruct((t, SB_WIDTH), BF16), jax.ShapeDtypeStruct((t, SSD_WIDTH), BF16)],
        compiler_params=_params(("parallel",)), name="post_fwd")(o, proj, y, proj, nw)


def _post_bwd(dya, o, proj, dys, y, nw, dproj, *, tm=256):
    t = o.shape[0]
    zw = SB_WIDTH + SSD_WIDTH

    def body(dya_ref, o_ref, za_ref, dys_ref, y_ref, zs_ref, nw_ref, _, do_ref, dy_ref, dnw_ref, dz_ref):
        @pl.when(pl.program_id(0) == 0)
        def _():
            dnw_ref[...] = jnp.zeros_like(dnw_ref)

        za = za_ref[...].astype(F32)
        sa = _sigmoid(za)
        d = dya_ref[...].astype(F32)
        do_ref[...] = _bf(d * (za * sa))
        dz_ref[:, :SB_WIDTH] = _bf(d * o_ref[...] * (sa * (1.0 + za * (1.0 - sa))))
        zs = zs_ref[...].astype(F32)
        ss = _sigmoid(zs)
        silu = zs * ss
        yv = y_ref[...]
        yz = yv * silu
        dv = dys_ref[...].astype(F32)
        for g in range(SSD_GROUPS):
            sl = slice(g * GN, (g + 1) * GN)
            v = yz[:, sl]
            r = lax.rsqrt(jnp.mean(v * v, axis=-1, keepdims=True) + EPS)
            nrm = v * r
            dg = dv[:, sl]
            dnw_ref[:, sl] += jnp.sum(dg * nrm, axis=0, keepdims=True)
            dn = dg * nw_ref[:, sl]
            dyz = r * (dn - nrm * jnp.mean(dn * nrm, axis=-1, keepdims=True))
            dy_ref[:, sl] = dyz * silu[:, sl]
            dz_ref[:, SB_WIDTH + g * GN:SB_WIDTH + (g + 1) * GN] = _bf(
                dyz * yv[:, sl] * (ss[:, sl] * (1.0 + zs[:, sl] * (1.0 - ss[:, sl]))))

    a_spec = pl.BlockSpec((tm, SB_WIDTH), lambda i: (i, 0))
    s_spec = pl.BlockSpec((tm, SSD_WIDTH), lambda i: (i, 0))
    w_spec = pl.BlockSpec((1, SSD_WIDTH), lambda i: (0, 0))
    return pl.pallas_call(
        body, grid=(t // tm,),
        in_specs=[a_spec, a_spec, pl.BlockSpec((tm, SB_WIDTH), lambda i: (i, CB_ZA * LANES // SB_WIDTH)),
                  s_spec, s_spec, pl.BlockSpec((tm, SSD_WIDTH), lambda i: (i, CB_ZS * LANES // SSD_WIDTH)),
                  w_spec, pl.BlockSpec(memory_space=pl.ANY)],
        out_specs=[a_spec, s_spec, w_spec, pl.BlockSpec((tm, zw), lambda i: (i, 1))],
        out_shape=[jax.ShapeDtypeStruct((t, SB_WIDTH), BF16), jax.ShapeDtypeStruct((t, SSD_WIDTH), F32),
                   jax.ShapeDtypeStruct((1, SSD_WIDTH), F32), jax.ShapeDtypeStruct(dproj.shape, dproj.dtype)],
        input_output_aliases={7: 3},
        compiler_params=_params(("arbitrary",)), name="post_bwd")(dya, o, proj, dys, y, proj, nw, dproj)


def _merge_fwd(ya, ys, proj_g, *, tm=512):
    t = ya.shape[0]

    def body(ya_ref, ys_ref, ga_ref, gs_ref, m_ref):
        m_ref[...] = _bf(_sigmoid(ga_ref[...].astype(F32)) * ya_ref[...].astype(F32)
                         + _sigmoid(gs_ref[...].astype(F32)) * ys_ref[...].astype(F32))

    spec = pl.BlockSpec((tm, D_MODEL), lambda i: (i, 0))
    return pl.pallas_call(
        body, grid=(t // tm,),
        in_specs=[spec, spec, pl.BlockSpec((tm, D_MODEL), lambda i: (i, 0)),
                  pl.BlockSpec((tm, D_MODEL), lambda i: (i, 1))],
        out_specs=spec, out_shape=jax.ShapeDtypeStruct((t, D_MODEL), BF16),
        compiler_params=_params(("parallel",)), name="merge_fwd")(ya, ys, proj_g, proj_g)


def _merge_bwd(dm, ya, ys, proj_g, *, tm=512):
    t = ya.shape[0]

    def body(dm_ref, ya_ref, ys_ref, ga_ref, gs_ref, dya_ref, dys_ref, dg_ref):
        d = dm_ref[...].astype(F32)
        ga = _sigmoid(ga_ref[...].astype(F32))
        gs = _sigmoid(gs_ref[...].astype(F32))
        dya_ref[...] = _bf(d * ga)
        dys_ref[...] = _bf(d * gs)
        dg_ref[:, :D_MODEL] = _bf(d * ya_ref[...].astype(F32) * (ga * (1.0 - ga)))
        dg_ref[:, D_MODEL:] = _bf(d * ys_ref[...].astype(F32) * (gs * (1.0 - gs)))

    spec = pl.BlockSpec((tm, D_MODEL), lambda i: (i, 0))
    return pl.pallas_call(
        body, grid=(t // tm,),
        in_specs=[spec, spec, spec, pl.BlockSpec((tm, D_MODEL), lambda i: (i, 0)),
                  pl.BlockSpec((tm, D_MODEL), lambda i: (i, 1))],
        out_specs=[spec, spec, pl.BlockSpec((tm, 2 * D_MODEL), lambda i: (i, 0))],
        out_shape=[jax.ShapeDtypeStruct((t, D_MODEL), BF16), jax.ShapeDtypeStruct((t, D_MODEL), BF16),
                   jax.ShapeDtypeStruct((t, 2 * D_MODEL), BF16)],
        compiler_params=_params(("parallel",)), name="merge_bwd")(dm, ya, ys, proj_g, proj_g)


def _final(x, out, wf, target, *, tm=512):
    t = x.shape[0]

    def body(x_ref, o_ref, w_ref, t_ref, loss_ref, dx_ref, dw_ref):
        @pl.when(pl.program_id(0) == 0)
        def _():
            loss_ref[...] = jnp.zeros_like(loss_ref)
            dw_ref[...] = jnp.zeros_like(dw_ref)

        x2 = x_ref[...] + o_ref[...]
        r = lax.rsqrt(jnp.mean(x2 * x2, axis=-1, keepdims=True) + EPS)
        nrm = x2 * r
        e = nrm * w_ref[...] - t_ref[...]
        row_loss = jnp.mean(e * e, axis=-1, keepdims=True)
        loss_ref[...] += 0.5 * jnp.sum(row_loss, axis=0, keepdims=True)
        dyv = e * (1.0 / D_MODEL)
        dw_ref[...] += jnp.sum(dyv * nrm, axis=0, keepdims=True)
        dn = dyv * w_ref[...]
        dx_ref[...] = r * (dn - nrm * jnp.mean(dn * nrm, axis=-1, keepdims=True))

    spec = pl.BlockSpec((tm, D_MODEL), lambda i: (i, 0))
    fix = pl.BlockSpec((1, D_MODEL), lambda i: (0, 0))
    return pl.pallas_call(
        body, grid=(t // tm,),
        in_specs=[spec, spec, fix, spec],
        out_specs=[pl.BlockSpec((1, LANES), lambda i: (0, 0)), spec, fix],
        out_shape=[jax.ShapeDtypeStruct((1, LANES), F32), jax.ShapeDtypeStruct((t, D_MODEL), F32),
                   jax.ShapeDtypeStruct((1, D_MODEL), F32)],
        compiler_params=_params(("arbitrary",)), name="final_loss")(x, out, wf, target)


def _group_lanes(v):
    return jnp.pad(v.reshape(SSD_GROUPS, 1, SSD_HPG), ((0, 0), (0, 0), (0, LANES - SSD_HPG)))


def _ungroup_lanes(v):
    return v[:, :, :SSD_HPG].reshape(1, N_DT)


def _row_slabs(pieces, rows):
    total = sum(p.shape[0] for p in pieces)
    assert total % rows == 0
    slabs = []
    for lo in range(0, total, rows):
        parts, off = [], 0
        for p in pieces:
            a, b = max(lo, off), min(lo + rows, off + p.shape[0])
            if a < b:
                parts.append(p[a - off:b - off])
            off += p.shape[0]
        slabs.append(parts[0] if len(parts) == 1 else jnp.concatenate(parts, axis=0))
    return jnp.stack(slabs)


def _local_step(x, target, nb, s, w_in_t, conv_w, conv_b, norm_w, dt_bias, a_log, d_skip,
                ssm_norm_w, out_weights, final_norm_w, between=None):
    t = nb * s
    w_g_t = w_in_t[GATE_COL0:]
    w_dt_t = jnp.pad(w_in_t[DT_COL0:GATE_COL0], ((0, LANES - N_DT), (0, 0)))
    h, r1 = _rms_fwd(x, norm_w)
    proj = _matmul(h, w_in_t, nt=True, n=DT_COL0, out_dtype=BF16, name="proj_main", tm=MM_TALL)
    proj_g = _matmul(h, w_g_t, nt=True, out_dtype=BF16, name="proj_gate", tm=MM_TALL)
    dt_raw = _matmul(h, w_dt_t, nt=True, name="proj_dt")
    dtr = jnp.pad(dt_raw[:, :N_DT].reshape(t, SSD_GROUPS, SSD_HPG).transpose(1, 0, 2),
                  ((0, 0), (0, 0), (0, LANES - SSD_HPG)))
    bias_g, alog_g, dsk_g = _group_lanes(dt_bias), _group_lanes(a_log), _group_lanes(d_skip)

    o, att_a, att_sg = _attn_fwd(proj, nb, s)
    xbc = _conv_fwd(proj, conv_w, conv_b, nb, s)
    y, hs = _ssd_fwd(xbc, dtr, bias_g, alog_g, dsk_g, nb, s)
    ya_in, ys_in = _post_fwd(o, proj, y, ssm_norm_w)
    w_attn_out, w_ssm_out, w_o = out_weights(ys_in) if callable(out_weights) else out_weights
    ya = _matmul(ya_in, w_attn_out, out_dtype=BF16, name="attn_out")
    ys = _matmul(ys_in, w_ssm_out, out_dtype=BF16, name="ssm_out")
    merged = _merge_fwd(ya, ys, proj_g)
    out = _matmul(merged, w_o, name="out_proj")
    loss, dx2, d_final_w = _final(x, out, final_norm_w, target)

    dx2b = _bf(dx2)
    dmerged = _matmul(dx2b, w_o, nt=True, out_dtype=BF16, name="d_merged")
    d_w_o = _matmul(merged, dx2b, ta=True, name="d_w_o")
    dya, dys, dgate = _merge_bwd(dmerged, ya, ys, proj_g)
    d_w_attn_out = _matmul(ya_in, dya, ta=True, name="d_w_attn_out")
    d_w_ssm_out = _matmul(ys_in, dys, ta=True, name="d_w_ssm_out")
    dya_in = _matmul(dya, w_attn_out, nt=True, out_dtype=BF16, name="d_ya_in")
    dys_in = _matmul(dys, w_ssm_out, nt=True, name="d_ys_in")
    do, dy, d_ssm_norm_w, dproj = _post_bwd(dya_in, o, proj, dys_in, y, ssm_norm_w, lax.empty(proj.shape, BF16))
    dq, dk, dv = _attn_bwd(proj, do, att_a, att_sg, nb, s)
    dxs_act, db_act, dc_act, ddtr, dbias_g, dalog_g, ddsk_g = _ssd_bwd(
        xbc, dtr, bias_g, alog_g, dsk_g, hs, dy, nb, s)
    dproj, dwx, dbx = _conv_bwd(proj, dxs_act, conv_w, conv_b, dproj, nb, s, col0=0, name="conv_bwd_x")
    dproj, dwb, dbb = _conv_bwd(proj, db_act, conv_w, conv_b, dproj, nb, s, col0=SSD_WIDTH, name="conv_bwd_b")
    dproj, dwc, dbc = _conv_bwd(proj, dc_act, conv_w, conv_b, dproj, nb, s, col0=SSD_WIDTH + 512,
                                name="conv_bwd_c")
    for i, piece in enumerate((dq, dk, dv)):
        dproj = lax.dynamic_update_slice(dproj, piece, (0, i * SB_WIDTH))
    ddt = jnp.pad(_bf(ddtr[:, :, :SSD_HPG].transpose(1, 0, 2).reshape(t, N_DT)), ((0, 0), (0, LANES - N_DT)))
    wt = dict(ta=True, out_dtype=BF16)
    d_w_in_t = _row_slabs(
        [_matmul(dproj, h, name="d_w_main", **wt), _matmul(ddt, h, name="d_w_dt", **wt)[:N_DT],
         _matmul(dgate, h, name="d_w_gate", **wt)], SHARD)
    grads = dict(
        w_in_t=d_w_in_t,
        conv_w=jnp.concatenate([dwx, dwb, dwc], axis=1), conv_b=jnp.concatenate([dbx, dbb, dbc], axis=1),
        dt_bias=_ungroup_lanes(dbias_g), a_log=_ungroup_lanes(dalog_g), d_skip=_ungroup_lanes(ddsk_g),
        ssm_norm_w=d_ssm_norm_w, w_attn_out=d_w_attn_out, w_ssm_out=d_w_ssm_out, w_o=d_w_o,
        final_norm_w=d_final_w)
    extra = None
    if between is not None:
        zero, extra = between(grads)
        ddt = ddt + _bf(zero[0:1, 0:1])

    dh = _matmul(ddt, w_dt_t, name="d_h_dt")
    dh = _matmul(dgate, w_g_t, add=dh, name="d_h_gate")
    dh = _matmul(dproj, w_in_t, add=dh, name="d_h", tk=DT_COL0 // 4)
    grad_x, grads["norm_w"] = _rms_bwd(dh, x, r1, norm_w, dx2)
    return loss, grad_x, grads, extra


N_CHIP = N_DEV // 2


def _pair_exchange(scatter, bcast, *, name):
    arrays = list(scatter) + list(bcast)
    ns, n = len(scatter), len(arrays)
    n_sem = ns * N_CHIP + (n - ns) * (N_DEV - 1)

    def body(*refs):
        ins, outs = refs[:n], refs[n:2 * n]
        send_sems, recv_sems, local_sems = refs[2 * n:]
        x, y, c = lax.axis_index("x"), lax.axis_index("y"), lax.axis_index("c")
        me = 4 * x + 2 * y + c
        copies, k = [], 0
        for a in range(ns):
            for q in range(N_CHIP):
                cp = pltpu.make_async_remote_copy(
                    src_ref=ins[a].at[2 * q + 1 - c], dst_ref=outs[a].at[q],
                    send_sem=send_sems.at[k], recv_sem=recv_sems.at[k],
                    device_id=(x, y, 1 - c), device_id_type=pl.DeviceIdType.MESH)
                cp.start()
                copies.append(cp)
                k += 1
        for a in range(ns, n):
            cp = pltpu.make_async_copy(ins[a], outs[a].at[me], local_sems.at[a - ns])
            cp.start()
            copies.append(cp)
            for r in range(1, N_DEV):
                tx = 1 - x if (r >> 2) & 1 else x
                ty = 1 - y if (r >> 1) & 1 else y
                tc = 1 - c if r & 1 else c
                cp = pltpu.make_async_remote_copy(
                    src_ref=ins[a], dst_ref=outs[a].at[me], send_sem=send_sems.at[k], recv_sem=recv_sems.at[k],
                    device_id=(tx, ty, tc), device_id_type=pl.DeviceIdType.MESH)
                cp.start()
                copies.append(cp)
                k += 1
        for cp in copies:
            cp.wait()

    out_shape = [jax.ShapeDtypeStruct((N_CHIP,) + v.shape[1:] if i < ns else (N_DEV,) + v.shape, v.dtype)
                 for i, v in enumerate(arrays)]
    return pl.pallas_call(
        body, in_specs=[pl.BlockSpec(memory_space=pl.ANY)] * n,
        out_specs=[pl.BlockSpec(memory_space=pl.ANY)] * n, out_shape=out_shape,
        scratch_shapes=[pltpu.SemaphoreType.DMA((n_sem,)), pltpu.SemaphoreType.DMA((n_sem,)),
                        pltpu.SemaphoreType.DMA((max(n - ns, 1),))],
        compiler_params=pltpu.CompilerParams(has_side_effects=True), name=name)(*arrays)


def _pair_add(mine, recv, core, *, name, tc=None):
    _, r, c = recv.shape
    tc = c if tc is None else tc
    assert c % tc == 0

    def body(core_ref, m_ref, r_ref, o_ref):
        o_ref[...] = (m_ref[...].astype(F32) + r_ref[...].astype(F32)).astype(o_ref.dtype)

    spec = pl.BlockSpec((None, r, tc), lambda q, i, core_ref: (q, 0, i))
    return pl.pallas_call(
        body, grid_spec=pltpu.PrefetchScalarGridSpec(
            num_scalar_prefetch=1, grid=(N_CHIP, c // tc),
            in_specs=[pl.BlockSpec((None, r, tc), lambda q, i, core_ref: (2 * q + core_ref[0], 0, i)), spec],
            out_specs=spec),
        out_shape=jax.ShapeDtypeStruct(recv.shape, recv.dtype),
        compiler_params=_params(("parallel", "parallel")), name=name)(core, mine, recv)


_HBM = pl.BlockSpec(memory_space=pltpu.HBM)
_SEM = pl.BlockSpec(memory_space=pltpu.SEMAPHORE)
_DATAFLOW = pltpu.SideEffectType.DATAFLOW_SIDE_EFFECTING
_PEERS = {"chips": N_CHIP - 1, "all": N_DEV - 1}


def _split_copies(pattern, srcs, lands, send_sems, recv_sems):
    x, y, c = lax.axis_index("x"), lax.axis_index("y"), lax.axis_index("c")
    per = _PEERS[pattern]
    copies = []
    for a in range(len(srcs)):
        for r in range(1, per + 1):
            if pattern == "chips":
                tx = 1 - x if (r >> 1) & 1 else x
                ty = 1 - y if r & 1 else y
                tc = c
                src, dst = srcs[a].at[2 * tx + ty], lands[a].at[2 * x + y]
            else:
                tx = 1 - x if (r >> 2) & 1 else x
                ty = 1 - y if (r >> 1) & 1 else y
                tc = 1 - c if r & 1 else c
                src, dst = srcs[a], lands[a].at[4 * x + 2 * y + c]
            copies.append(pltpu.make_async_remote_copy(
                src_ref=src, dst_ref=dst, send_sem=send_sems.at[a * per + r - 1],
                recv_sem=recv_sems.at[a * per + r - 1], device_id=(tx, ty, tc), device_id_type=pl.DeviceIdType.MESH))
    return copies


def _split_start(pattern, arrays, *, name):
    n = len(arrays)
    n_sem = n * _PEERS[pattern]

    def body(*refs):
        srcs, lands = refs[:n], refs[n:2 * n]
        send_sems, recv_sems = refs[2 * n], refs[2 * n + 1]
        zero = refs[-1]
        for cp in _split_copies(pattern, srcs, lands, send_sems, recv_sems):
            cp.start()
        zero[...] = jnp.zeros_like(zero)

    src_t = [pltpu.HBM(v.shape, v.dtype) for v in arrays]
    land_t = [pltpu.HBM(v.shape if pattern == "chips" else (N_DEV,) + v.shape, v.dtype) for v in arrays]
    outs = pl.pallas_call(
        body, name=name,
        out_shape=(pltpu.SemaphoreType.DMA((n_sem,)), pltpu.SemaphoreType.DMA((n_sem,)), *src_t, *land_t,
                   jax.ShapeDtypeStruct((8, LANES), F32)),
        in_specs=[_HBM] * (2 * n),
        out_specs=(_SEM, _SEM, *[_HBM] * (2 * n), pl.BlockSpec(memory_space=pltpu.VMEM)),
        input_output_aliases={i: 2 + i for i in range(2 * n)},
        compiler_params=pltpu.CompilerParams(has_side_effects=_DATAFLOW))(
            *[pltpu.with_memory_space_constraint(v, pltpu.HBM) for v in arrays],
            *[pltpu.with_memory_space_constraint(lax.empty(t.shape, t.dtype), pltpu.HBM) for t in land_t])
    return outs[0], outs[1], outs[2:2 + n], outs[2 + n:2 + 2 * n], outs[-1]


def _split_wait(pattern, send_sems, recv_sems, srcs, lands, after, *, name):
    n = len(srcs)

    def body(*refs):
        for cp in _split_copies(pattern, refs[:n], refs[n:2 * n], refs[2 * n], refs[2 * n + 1]):
            cp.wait_send()
            cp.wait_recv()

    outs = pl.pallas_call(
        body, name=name, out_shape=[pltpu.HBM(v.shape, v.dtype) for v in (*srcs, *lands)],
        in_specs=[_HBM] * (2 * n) + [_SEM, _SEM, pl.BlockSpec(memory_space=pl.ANY)],
        out_specs=[_HBM] * (2 * n), input_output_aliases={i: i for i in range(2 * n)},
        compiler_params=pltpu.CompilerParams(has_side_effects=_DATAFLOW))(
            *srcs, *lands, send_sems, recv_sems, after)
    return outs[:n], outs[n:]


def _own_slab(lands, srcs, index, whole):
    out = []
    for land, src in zip(lands, srcs):
        own = src[None] if whole else lax.dynamic_slice(src, (index, 0, 0), (1,) + src.shape[1:])
        out.append(lax.dynamic_update_slice(land, own, (index, 0, 0)))
    return out


def _gather(arrays, *, name):
    n = len(arrays)
    per = N_DEV - 1

    def body(*refs):
        ins, outs = refs[:n], refs[n:2 * n]
        send_sems, recv_sems, local_sems = refs[2 * n:]
        x, y, c = lax.axis_index("x"), lax.axis_index("y"), lax.axis_index("c")
        me, sibling = (x, y, c), (x, y, 1 - c)
        chips = [(1 - x, y), (x, 1 - y), (1 - x, 1 - y)]

        def slab(a, block):
            return outs[a].at[4 * block[0] + 2 * block[1] + block[2]]

        def copy(a, k, block, to, src=None):
            rows = slab(a, block)
            return pltpu.make_async_remote_copy(
                src_ref=rows if src is None else src, dst_ref=rows,
                send_sem=send_sems.at[a * per + k], recv_sem=recv_sems.at[a * per + k],
                device_id=to, device_id_type=pl.DeviceIdType.MESH)

        started = []
        for a in range(n):
            mine = pltpu.make_async_copy(ins[a], slab(a, me), local_sems.at[a])
            mine.start()
            started.append(mine)
        for a in range(n):
            first = [copy(a, 1 + j, me, (*chip, c), src=ins[a]) for j, chip in enumerate(chips)]
            first.append(copy(a, 0, me, sibling, src=ins[a]))
            for cp in first:
                cp.start()
            started += first
        for j, chip in enumerate(chips):
            for a in range(n):
                copy(a, 1 + j, (*chip, c), me).wait_recv()
                passed = copy(a, 4 + j, (*chip, c), sibling)
                passed.start()
                started.append(passed)
        for a in range(n):
            copy(a, 0, sibling, me).wait_recv()
            for j, chip in enumerate(chips):
                copy(a, 4 + j, (*chip, 1 - c), me).wait_recv()
        for cp in started[:n]:
            cp.wait()
        for cp in started[n:]:
            cp.wait_send()

    return pl.pallas_call(
        body, in_specs=[pl.BlockSpec(memory_space=pl.ANY)] * n,
        out_specs=[pl.BlockSpec(memory_space=pl.ANY)] * n,
        out_shape=[jax.ShapeDtypeStruct((N_DEV,) + v.shape, v.dtype) for v in arrays],
        scratch_shapes=[pltpu.SemaphoreType.DMA((n * per,)), pltpu.SemaphoreType.DMA((n * per,)),
                        pltpu.SemaphoreType.DMA((n,))],
        compiler_params=pltpu.CompilerParams(has_side_effects=True), name=name)(*arrays)


def _adamw(parts, w, m, v, *, name, tc=None):
    r, c = w.shape
    n_parts = parts.shape[0]
    tc = c if tc is None else tc
    assert c % tc == 0

    def body(p_ref, w_ref, m_ref, v_ref, g_ref, d_ref, nm_ref, nv_ref):
        g = p_ref[0].astype(F32)
        for d in range(1, n_parts):
            g = g + p_ref[d].astype(F32)
        mm = ADAM_B1 * m_ref[...] + (1.0 - ADAM_B1) * g
        vv = ADAM_B2 * v_ref[...] + (1.0 - ADAM_B2) * (g * g)
        m_hat = mm / (1.0 - ADAM_B1 ** ADAM_STEP)
        v_hat = vv / (1.0 - ADAM_B2 ** ADAM_STEP)
        g_ref[...] = g
        d_ref[...] = -ADAM_LR * (m_hat / (jnp.sqrt(v_hat) + ADAM_EPS) + ADAM_WD * w_ref[...])
        nm_ref[...] = mm
        nv_ref[...] = vv

    spec = pl.BlockSpec((r, tc), lambda i: (0, i))
    out = jax.ShapeDtypeStruct((r, c), F32)
    return pl.pallas_call(
        body, grid=(c // tc,),
        in_specs=[pl.BlockSpec((n_parts, r, tc), lambda i: (0, 0, i)), spec, spec, spec],
        out_specs=[spec] * 4, out_shape=[out] * 4,
        compiler_params=_params(("parallel",)), name=name)(parts, w, m, v)


SMALL = (("norm_w", D_MODEL), ("conv_b", CONV_DIM), ("dt_bias", N_DT), ("a_log", N_DT),
         ("d_skip", N_DT), ("ssm_norm_w", SSD_WIDTH), ("final_norm_w", D_MODEL))
SMALL_USED = sum(size for _, size in SMALL)
SMALL_ROWS = 64
SHARD = D_PROJ // N_DEV


def _pack_small(vals, scalar=None):
    parts = [vals[k].reshape(-1).astype(F32) for k, _ in SMALL]
    if scalar is not None:
        parts.append(scalar.reshape(1))
    flat = jnp.concatenate(parts)
    return jnp.pad(flat, (0, SMALL_ROWS * LANES - flat.shape[0])).reshape(SMALL_ROWS, LANES)


def _unpack_small(packed, shapes):
    flat = packed.reshape(-1)
    out, off = {}, 0
    for k, size in SMALL:
        out[k] = flat[off:off + size].reshape(shapes[k])
        off += size
    return out


def kernel(x, norm_w, w_in, conv_w, conv_b, dt_bias, a_log, d_skip, ssm_norm_w, w_attn_out, w_ssm_out, w_o, final_norm_w, loss_target, m_norm_w, m_w_in, m_conv_w, m_conv_b, m_dt_bias, m_a_log, m_d_skip, m_ssm_norm_w, m_w_attn_out, m_w_ssm_out, m_w_o, m_final_norm_w, v_norm_w, v_w_in, v_conv_w, v_conv_b, v_dt_bias, v_a_log, v_d_skip, v_ssm_norm_w, v_w_attn_out, v_w_ssm_out, v_w_o, v_final_norm_w):
    nb, s, _ = x.shape
    t = nb * s
    weights = dict(norm_w=norm_w, w_in=w_in, conv_w=conv_w, conv_b=conv_b, dt_bias=dt_bias, a_log=a_log,
                   d_skip=d_skip, ssm_norm_w=ssm_norm_w, w_attn_out=w_attn_out, w_ssm_out=w_ssm_out,
                   w_o=w_o, final_norm_w=final_norm_w)
    moms = dict(norm_w=m_norm_w, w_in=m_w_in, conv_w=m_conv_w, conv_b=m_conv_b, dt_bias=m_dt_bias,
                a_log=m_a_log, d_skip=m_d_skip, ssm_norm_w=m_ssm_norm_w, w_attn_out=m_w_attn_out,
                w_ssm_out=m_w_ssm_out, w_o=m_w_o, final_norm_w=m_final_norm_w)
    vels = dict(norm_w=v_norm_w, w_in=v_w_in, conv_w=v_conv_w, conv_b=v_conv_b, dt_bias=v_dt_bias,
                a_log=v_a_log, d_skip=v_d_skip, ssm_norm_w=v_ssm_norm_w, w_attn_out=v_w_attn_out,
                w_ssm_out=v_w_ssm_out, w_o=v_w_o, final_norm_w=v_final_norm_w)

    tr_ = lambda a: jnp.transpose(a[0])
    g_in, g_conv = _gather([_bf(tr_(w_in)), conv_w[0]], name="gather_weights")
    conv_full = g_conv.transpose(1, 0, 2).reshape(CONV_K, CONV_DIM)
    o_send, o_recv, o_srcs, o_lands, o_zero = _split_start(
        "all", [_bf(w_attn_out[0]), _bf(w_ssm_out[0]), _bf(w_o[0])], name="gather_out_weights_start")

    core = lax.axis_index("c").astype(jnp.int32).reshape(1)
    chip = 2 * lax.axis_index("x") + lax.axis_index("y")
    me = 2 * chip + lax.axis_index("c")

    def out_weights(after):
        srcs, lands = _split_wait("all", o_send, o_recv, o_srcs, o_lands, after, name="gather_out_weights_wait")
        g_wa, g_ws, g_wo = _own_slab(lands, srcs, me, True)
        return (g_wa.reshape(SB_WIDTH, D_MODEL), g_ws.reshape(SSD_WIDTH, D_MODEL), g_wo.reshape(D_MODEL, D_MODEL))

    def send_weight_grads(grads):
        slabs = [grads["w_in_t"],
                 grads["conv_w"].reshape(CONV_K, N_DEV, CONV_DIM // N_DEV).transpose(1, 0, 2),
                 _bf(grads["w_attn_out"].reshape(N_DEV, SB_WIDTH // N_DEV, D_MODEL)),
                 _bf(grads["w_ssm_out"].reshape(N_DEV, SSD_WIDTH // N_DEV, D_MODEL)),
                 _bf(grads["w_o"].reshape(N_DEV, D_MODEL // N_DEV, D_MODEL))]
        recv = _pair_exchange(slabs, [], name="exchange_pairs")
        names = ("w_in", "conv_w", "w_attn_out", "w_ssm_out", "w_o")
        sums = [_pair_add(a, b, core, name="pair_add_" + k, tc=4 * LANES if k == "w_in" else None)
                for k, a, b in zip(names, slabs, recv)]
        send_sems, recv_sems, srcs, lands, zero = _split_start("chips", sums, name="exchange_chips_start")
        return zero, (send_sems, recv_sems, srcs, lands)

    loss_part, grad_x, grads, (send_sems, recv_sems, srcs, lands) = _local_step(
        x.reshape(t, D_MODEL), loss_target.reshape(t, D_MODEL), nb, s, g_in.reshape(D_PROJ, D_MODEL),
        conv_full, conv_b, norm_w + o_zero[0:1, 0:1], dt_bias, a_log, d_skip, ssm_norm_w, out_weights,
        final_norm_w.reshape(1, D_MODEL), between=send_weight_grads)

    (p_small,) = _pair_exchange([], [_pack_small(grads, loss_part[0, 0])], name="exchange_small")
    srcs, lands = _split_wait("chips", send_sems, recv_sems, srcs, lands, grad_x, name="exchange_chips_wait")
    p_in, p_conv, p_wa, p_ws, p_wo = _own_slab(lands, srcs, chip, False)

    res = {}
    res["w_in"] = [a.T for a in _adamw(p_in, tr_(w_in), tr_(m_w_in), tr_(v_w_in), name="adamw_w_in", tc=2 * LANES)]
    res["conv_w"] = _adamw(p_conv, conv_w[0], m_conv_w[0], v_conv_w[0], name="adamw_conv_w")
    res["w_attn_out"] = _adamw(p_wa, w_attn_out[0], m_w_attn_out[0], v_w_attn_out[0], name="adamw_w_attn_out")
    res["w_ssm_out"] = _adamw(p_ws, w_ssm_out[0], m_w_ssm_out[0], v_w_ssm_out[0], name="adamw_w_ssm_out")
    res["w_o"] = _adamw(p_wo, w_o[0], m_w_o[0], v_w_o[0], name="adamw_w_o")
    sm = _adamw(p_small, _pack_small(weights), _pack_small(moms), _pack_small(vels), name="adamw_small")
    loss = sm[0].reshape(-1)[SMALL_USED]
    shapes = {k: weights[k].shape for k, _ in SMALL}
    sm = [_unpack_small(v, shapes) for v in sm]
    for k, _ in SMALL:
        res[k] = tuple(part[k] for part in sm)

    order = ("norm_w", "w_in", "conv_w", "conv_b", "dt_bias", "a_log", "d_skip", "ssm_norm_w",
             "w_attn_out", "w_ssm_out", "w_o", "final_norm_w")
    outs = [loss, grad_x.reshape(nb, s, D_MODEL)]
    for i in range(4):
        outs += [res[k][i].reshape(weights[k].shape) for k in order]
    return tuple(outs)
```

```python
import functools

import jax
import jax.numpy as jnp
from jax import lax
from jax.experimental import pallas as pl
from jax.experimental.pallas import tpu as pltpu

F32 = jnp.float32
BF16 = jnp.bfloat16

D_MODEL = 1024
SB_WIDTH = 1024
HEAD_DIM = 64
SSD_WIDTH = 2048
SSD_GROUPS = 4
SSD_HPG = 8
SSD_STATE = 128
SSD_CHUNK = 128
CONV_K = 4
CONV_DIM = 3072
N_DT = 32
D_PROJ = 11296
DT_COL0 = 9216
GATE_COL0 = DT_COL0 + N_DT
EPS = 1e-6
N_DEV = 8

ADAM_LR = 0.001
ADAM_B1 = 0.9
ADAM_B2 = 0.999
ADAM_EPS = 1e-08
ADAM_WD = 0.01
ADAM_STEP = 10

LANES = 128
VMEM_LIMIT = 56 * 1024 * 1024

CB_Q, CB_K, CB_V, CB_ZA, CB_ZS, CB_XBC = 0, 8, 16, 24, 32, 48

NT_DIMS = (((1,), (1,)), ((), ()))
TN_DIMS = (((0,), (0,)), ((), ()))


def _params(sem):
    return pltpu.CompilerParams(dimension_semantics=sem, vmem_limit_bytes=VMEM_LIMIT)


def _dot(a, b):
    return jnp.dot(a, b, preferred_element_type=F32)


def _dot_nt(a, b):
    return lax.dot_general(a, b, NT_DIMS, preferred_element_type=F32)


def _dot_tn(a, b):
    return lax.dot_general(a, b, TN_DIMS, preferred_element_type=F32)


def _bf(a):
    return a.astype(BF16)


def _sigmoid(x):
    return 0.5 + 0.5 * jnp.tanh(0.5 * x)


def _softplus(x):
    return jnp.maximum(x, 0.0) + jnp.log(1.0 + jnp.exp(-jnp.abs(x)))


def _trunc_split(a):
    bits = lax.bitcast_convert_type(a, jnp.uint32) & jnp.uint32(0xFFFF0000)
    hi = lax.bitcast_convert_type(bits, F32)
    return _bf(hi), _bf(a - hi)


MM_TILE = 1024
MM_TK = 2048
MM_TALL = 2048


def _matmul(a, b, *, name, nt=False, ta=False, add=None, out_dtype=F32, tm=MM_TILE, tn=MM_TILE, tk=MM_TK, n=None):
    k, m = a.shape if ta else a.shape[::-1]
    n = (b.shape[0] if nt else b.shape[1]) if n is None else n
    tm, tn, tk = min(tm, m), min(tn, n), min(tk, k)
    assert m % tm == 0 and n % tn == 0 and k % tk == 0, (name, a.shape, b.shape)
    assert not (ta and nt)
    nk = k // tk
    has_add = add is not None

    def body(*refs):
        a_ref, b_ref = refs[:2]
        add_ref = refs[2] if has_add else None
        o_ref = refs[3 if has_add else 2]
        acc_ref = refs[-1] if nk > 1 else None
        kk = pl.program_id(2)

        def product():
            av, bv = _bf(a_ref[...]), _bf(b_ref[...])
            return _dot_nt(av, bv) if nt else _dot_tn(av, bv) if ta else _dot(av, bv)

        if nk == 1:
            d = product()
            o_ref[...] = ((d + add_ref[...]) if has_add else d).astype(out_dtype)
            return

        @pl.when(kk == 0)
        def _():
            acc_ref[...] = jnp.zeros_like(acc_ref)

        acc_ref[...] += product()

        @pl.when(kk == nk - 1)
        def _():
            r = acc_ref[...]
            if has_add:
                r = r + add_ref[...]
            o_ref[...] = r.astype(out_dtype)

    in_specs = [pl.BlockSpec((tk, tm), lambda j, i, kk: (kk, i)) if ta
                else pl.BlockSpec((tm, tk), lambda j, i, kk: (i, kk)),
                pl.BlockSpec((tn, tk), lambda j, i, kk: (j, kk)) if nt
                else pl.BlockSpec((tk, tn), lambda j, i, kk: (kk, j))]
    args = [a, b]
    if has_add:
        in_specs.append(pl.BlockSpec((tm, tn), lambda j, i, kk: (i, j)))
        args.append(add)
    return pl.pallas_call(
        body, grid=(n // tn, m // tm, nk), in_specs=in_specs,
        out_specs=pl.BlockSpec((tm, tn), lambda j, i, kk: (i, j)),
        out_shape=jax.ShapeDtypeStruct((m, n), out_dtype),
        scratch_shapes=[pltpu.VMEM((tm, tn), F32)] if nk > 1 else [],
        compiler_params=_params(("parallel", "parallel", "arbitrary")), name=name)(*args)


def _rms_fwd(x, w, *, tm=512):
    t = x.shape[0]

    def body(x_ref, w_ref, h_ref, r_ref):
        xv = x_ref[...]
        r = lax.rsqrt(jnp.mean(xv * xv, axis=-1, keepdims=True) + EPS)
        h_ref[...] = _bf(xv * r * w_ref[...])
        r_ref[...] = r

    row = lambda i: (i, 0)
    return pl.pallas_call(
        body, grid=(t // tm,),
        in_specs=[pl.BlockSpec((tm, D_MODEL), row), pl.BlockSpec((1, D_MODEL), lambda i: (0, 0))],
        out_specs=[pl.BlockSpec((tm, D_MODEL), row), pl.BlockSpec((tm, 1), row)],
        out_shape=[jax.ShapeDtypeStruct((t, D_MODEL), BF16), jax.ShapeDtypeStruct((t, 1), F32)],
        compiler_params=_params(("parallel",)), name="rms_fwd")(x, w)


def _rms_bwd(dh, x, r, w, dres, *, tm=512):
    t = x.shape[0]

    def body(dh_ref, x_ref, r_ref, w_ref, dres_ref, dx_ref, dw_ref):
        @pl.when(pl.program_id(0) == 0)
        def _():
            dw_ref[...] = jnp.zeros_like(dw_ref)

        nrm = x_ref[...] * r_ref[...]
        dhv = dh_ref[...]
        dw_ref[...] += jnp.sum(dhv * nrm, axis=0, keepdims=True)
        dn = dhv * w_ref[...]
        dx = r_ref[...] * (dn - nrm * jnp.mean(dn * nrm, axis=-1, keepdims=True))
        dx_ref[...] = dx + dres_ref[...]

    row = lambda i: (i, 0)
    fix = lambda i: (0, 0)
    return pl.pallas_call(
        body, grid=(t // tm,),
        in_specs=[pl.BlockSpec((tm, D_MODEL), row), pl.BlockSpec((tm, D_MODEL), row),
                  pl.BlockSpec((tm, 1), row), pl.BlockSpec((1, D_MODEL), fix),
                  pl.BlockSpec((tm, D_MODEL), row)],
        out_specs=[pl.BlockSpec((tm, D_MODEL), row), pl.BlockSpec((1, D_MODEL), fix)],
        out_shape=[jax.ShapeDtypeStruct((t, D_MODEL), F32), jax.ShapeDtypeStruct((1, D_MODEL), F32)],
        compiler_params=_params(("arbitrary",)), name="rms_bwd")(dh, x, r, w, dres)


ATT_BLK = 256


def _attn_masks(blk):
    r_io = lax.broadcasted_iota(jnp.int32, (blk, blk), 0)
    c_io = lax.broadcasted_iota(jnp.int32, (blk, blk), 1)
    return r_io, c_io


def _attn_tiles(nq):
    return nq * (nq + 1) // 2


def _attn_fwd(proj, nb, s):
    blk = min(ATT_BLK, s)
    nq = s // blk
    nt = _attn_tiles(nq)
    scale = HEAD_DIM ** -0.5

    def body(q_ref, k_ref, v_ref, o_ref, a_out, sg_out, q0_s, q1_s, kb_s, v0_s, v1_s, z_s, a_s, acc_s, cr_s):
        head0 = lax.broadcasted_iota(jnp.int32, (s, LANES), 1) < HEAD_DIM
        r_io, c_io = _attn_masks(blk)
        tri = c_io < r_io
        tri2 = jnp.concatenate([tri, tri], axis=0)
        u_gt = _bf(r_io > c_io)
        u2 = jnp.concatenate([u_gt, u_gt], axis=0)
        qv = q_ref[...] * scale
        q0_s[...] = _bf(jnp.where(head0, qv, 0.0))
        q1_s[...] = _bf(jnp.where(head0, 0.0, qv))
        kb_s[...] = _bf(k_ref[...])
        vv = v_ref[...]
        v0_s[...] = _bf(jnp.where(head0, vv, 0.0))
        v1_s[...] = _bf(jnp.where(head0, 0.0, vv))

        def scores(qcat, kblk, slot):
            k0 = pl.multiple_of(jnp.maximum(kblk, 0) * blk, blk)
            z_s[slot] = _dot_nt(qcat, kb_s[pl.ds(k0, blk), :])

        def weights(slot, tile, diag, zslot=None):
            z = z_s[slot if zslot is None else zslot]
            cr = cr_s[...]
            sp = _softplus(z)
            spm = jnp.where(tri2, sp, 0.0) if diag else sp
            hi, lo = _trunc_split(spm)
            ps = _dot(jnp.concatenate([hi, lo], axis=1), u2)
            lb = z - sp
            a = jnp.exp(lb - (ps + cr))
            if diag:
                a = jnp.where(tri2, a, 0.0)
            ab = _bf(a)
            a_s[slot] = ab
            a_out[tile] = ab
            sg_out[tile] = _bf(jnp.exp(lb))
            cr_s[...] = cr + ps[:, 0:1] + spm[:, 0:1]

        def apply(slot, k0):
            a = a_s[slot]
            vcat = jnp.concatenate([v0_s[pl.ds(k0, blk), :], v1_s[pl.ds(k0, blk), :]], axis=0)
            acc_s[...] += _dot(jnp.concatenate([a[:blk], a[blk:]], axis=1), vcat)

        def queries(qi):
            q0 = pl.multiple_of(qi * blk, blk)
            return jnp.concatenate([q0_s[pl.ds(q0, blk), :], q1_s[pl.ds(q0, blk), :]], axis=0)

        def first_scores(qi):
            qcat = queries(qi)
            scores(qcat, qi, 2)
            scores(qcat, qi - 1, 3)

        first_scores(jnp.int32(0))

        def qblock(qi, _):
            q0 = pl.multiple_of(qi * blk, blk)
            tile0 = qi * (qi + 1) // 2
            qcat = queries(qi)
            z_s[1] = z_s[3]
            cr_s[...] = jnp.zeros_like(cr_s)
            acc_s[...] = jnp.zeros_like(acc_s)
            weights(0, tile0 + qi, True, zslot=2)
            first_scores(jnp.minimum(qi + 1, nq - 1))

            def step(j, slot):
                scores(qcat, qi - j - 1, 1 - slot)
                weights(slot, tile0 + qi - j, False)
                apply(1 - slot, pl.multiple_of((qi - j + 1) * blk, blk))

            def pair(i, _):
                step(2 * i + 1, 1)
                step(2 * i + 2, 0)
                return 0

            lax.fori_loop(0, qi // 2, pair, 0)

            @pl.when(qi % 2 == 1)
            def _():
                step(qi, 1)

            apply(qi & 1, 0)
            o_ref[pl.ds(q0, blk), :] = acc_s[...]
            return 0

        lax.fori_loop(0, nq, qblock, 0)

    t = nb * s
    tiles = pl.BlockSpec((None, None, nt, 2 * blk, blk), lambda b, p: (b, p, 0, 0, 0))
    tiles_shape = jax.ShapeDtypeStruct((nb, 8, nt, 2 * blk, blk), BF16)
    return pl.pallas_call(
        body, grid=(nb, 8),
        in_specs=[pl.BlockSpec((s, LANES), lambda b, p: (b, CB_Q + p)),
                  pl.BlockSpec((s, LANES), lambda b, p: (b, CB_K + p)),
                  pl.BlockSpec((s, LANES), lambda b, p: (b, CB_V + p))],
        out_specs=[pl.BlockSpec((s, LANES), lambda b, p: (b, p)), tiles, tiles],
        out_shape=[jax.ShapeDtypeStruct((t, SB_WIDTH), F32), tiles_shape, tiles_shape],
        scratch_shapes=[pltpu.VMEM((s, LANES), BF16)] * 5 + [
            pltpu.VMEM((4, 2 * blk, blk), F32), pltpu.VMEM((2, 2 * blk, blk), BF16),
            pltpu.VMEM((blk, LANES), F32), pltpu.VMEM((2 * blk, 1), F32)],
        compiler_params=_params(("parallel", "parallel")), name="attn_fwd")(proj, proj, proj)


def _attn_bwd(proj, do, a_all, sg_all, nb, s):
    blk = min(ATT_BLK, s)
    nq = s // blk
    nt = _attn_tiles(nq)
    scale = HEAD_DIM ** -0.5

    def body(q_ref, k_ref, v_ref, do_ref, a_in, sg_in, dq_ref, dk_ref, dv_ref,
             qt0_s, qt1_s, k0_s, k1_s, vb_s, d0_s, d1_s, dt0_s, dt1_s, da_s, dz_s, dkt_acc, dvt_acc, dq_acc, cw_s):
        head0 = lax.broadcasted_iota(jnp.int32, (s, LANES), 1) < HEAD_DIM
        r_io, c_io = _attn_masks(blk)
        tri = c_io < r_io
        tri2 = jnp.concatenate([tri, tri], axis=0)
        u_lt = _bf(r_io < c_io)
        head0_t = lax.broadcasted_iota(jnp.int32, (LANES, s), 0) < HEAD_DIM
        q_t = (q_ref[...].astype(F32) * scale).T
        qt0_s[...] = _bf(jnp.where(head0_t, q_t, 0.0))
        qt1_s[...] = _bf(jnp.where(head0_t, 0.0, q_t))
        kv = k_ref[...]
        k0_s[...] = _bf(jnp.where(head0, kv, 0.0))
        k1_s[...] = _bf(jnp.where(head0, 0.0, kv))
        vb_s[...] = _bf(v_ref[...])
        dov = do_ref[...]
        d0_s[...] = _bf(jnp.where(head0, dov, 0.0))
        d1_s[...] = _bf(jnp.where(head0, 0.0, dov))
        do_t = dov.astype(F32).T
        dt0_s[...] = _bf(jnp.where(head0_t, do_t, 0.0))
        dt1_s[...] = _bf(jnp.where(head0_t, 0.0, do_t))
        dkt_acc[...] = jnp.zeros_like(dkt_acc)
        dvt_acc[...] = jnp.zeros_like(dvt_acc)

        def qblock(qi, _):
            q0 = pl.multiple_of(qi * blk, blk)
            tile0 = qi * (qi + 1) // 2
            qs_ = pl.ds(q0, blk)
            dcat = jnp.concatenate([d0_s[qs_, :], d1_s[qs_, :]], axis=0)
            qcat_t = jnp.concatenate([qt0_s[:, qs_], qt1_s[:, qs_]], axis=1)
            dcat_t = jnp.concatenate([dt0_s[:, qs_], dt1_s[:, qs_]], axis=1)

            def scores(kblk, slot):
                da_s[slot] = _dot_nt(dcat, vb_s[pl.ds(pl.multiple_of(kblk * blk, blk), blk), :])

            def grads(slot, kblk, diag):
                cw = cw_s[...]
                w = a_in[tile0 + kblk].astype(F32) * da_s[slot]
                wex = _dot(_bf(w), u_lt)
                dz = w - sg_in[tile0 + kblk].astype(F32) * (w + (wex + cw))
                if diag:
                    dz = jnp.where(tri2, dz, 0.0)
                dz_s[slot] = _bf(dz)
                cw_s[...] = cw + wex[:, blk - 1:blk] + w[:, blk - 1:blk]

            def apply(slot, kblk):
                ks_ = pl.ds(pl.multiple_of(kblk * blk, blk), blk)
                dz = dz_s[slot]
                dq_acc[...] += _dot(jnp.concatenate([dz[:blk], dz[blk:]], axis=1),
                                    jnp.concatenate([k0_s[ks_, :], k1_s[ks_, :]], axis=0))
                dkt_acc[:, ks_] += _dot(qcat_t, dz)
                dvt_acc[:, ks_] += _dot(dcat_t, a_in[tile0 + kblk])

            def step(j, slot, diag, first=False):
                if not diag:
                    scores(j + 1, 1 - slot)
                grads(slot, j, diag)
                if not first:
                    apply(1 - slot, j - 1)

            scores(0, 0)
            dq_acc[...] = jnp.zeros_like(dq_acc)
            cw_s[...] = jnp.zeros_like(cw_s)

            @pl.when(qi == 0)
            def _():
                step(0, 0, True, first=True)

            @pl.when(qi > 0)
            def _():
                step(0, 0, False, first=True)

            def pair(i, _):
                step(2 * i + 1, 1, False)
                step(2 * i + 2, 0, False)
                return 0

            lax.fori_loop(0, (qi - 1) // 2, pair, 0)

            @pl.when((qi > 0) & (qi % 2 == 0))
            def _():
                step(qi - 1, 1, False)
                step(qi, 0, True)

            @pl.when(qi % 2 == 1)
            def _():
                step(qi, 1, True)

            apply(qi & 1, qi)
            dq_ref[qs_, :] = _bf(dq_acc[...] * scale)
            return 0

        lax.fori_loop(0, nq, qblock, 0)
        dk_ref[...] = _bf(dkt_acc[...].T)
        dv_ref[...] = _bf(dvt_acc[...].T)

    t = nb * s
    hp = lambda b, p: (b, p)
    out = jax.ShapeDtypeStruct((t, SB_WIDTH), BF16)
    tiles = pl.BlockSpec((None, None, nt, 2 * blk, blk), lambda b, p: (b, p, 0, 0, 0))
    return pl.pallas_call(
        body, grid=(nb, 8),
        in_specs=[pl.BlockSpec((s, LANES), lambda b, p: (b, CB_Q + p)),
                  pl.BlockSpec((s, LANES), lambda b, p: (b, CB_K + p)),
                  pl.BlockSpec((s, LANES), lambda b, p: (b, CB_V + p)),
                  pl.BlockSpec((s, LANES), hp), tiles, tiles],
        out_specs=[pl.BlockSpec((s, LANES), hp)] * 3,
        out_shape=[out, out, out],
        scratch_shapes=[pltpu.VMEM((LANES, s), BF16)] * 2 + [pltpu.VMEM((s, LANES), BF16)] * 5 + [
            pltpu.VMEM((LANES, s), BF16)] * 2 + [
            pltpu.VMEM((2, 2 * blk, blk), F32), pltpu.VMEM((2, 2 * blk, blk), BF16),
            pltpu.VMEM((LANES, s), F32), pltpu.VMEM((LANES, s), F32),
            pltpu.VMEM((blk, LANES), F32), pltpu.VMEM((2 * blk, 1), F32)],
        compiler_params=_params(("parallel", "parallel")), name="attn_bwd")(proj, proj, proj, do, a_all, sg_all)


CONV_TC = 256
CONV_ROWS = 64
HALO = 8


def _conv_rows(ws, w_ref, b_ref, n):
    taps = [ws[HALO - kk:HALO - kk + n, :] for kk in range(CONV_K)]
    y = b_ref[...] + w_ref[CONV_K - 1:CONV_K, :] * taps[0]
    for kk in range(1, CONV_K):
        y = y + w_ref[CONV_K - 1 - kk:CONV_K - kk, :] * taps[kk]
    return y, taps


def _conv_fwd(proj, conv_w, conv_b, nb, s):
    tc, rows = CONV_TC, min(CONV_ROWS, s)
    cb0 = CB_XBC * LANES // tc

    def body(x_ref, w_ref, b_ref, o_ref, xp, ws):
        xp[0:HALO, :] = jnp.zeros((HALO, tc), F32)
        xp[HALO:HALO + s, :] = x_ref[...].astype(F32)

        def chunk(i, _):
            r0 = pl.multiple_of(i * rows, rows)
            ws[...] = xp[pl.ds(r0, rows + HALO), :]
            y, _ = _conv_rows(ws, w_ref, b_ref, rows)
            o_ref[pl.ds(r0, rows), :] = y * _sigmoid(y)
            return 0

        lax.fori_loop(0, s // rows, chunk, 0)

    t = nb * s
    return pl.pallas_call(
        body, grid=(nb, CONV_DIM // tc),
        in_specs=[pl.BlockSpec((s, tc), lambda b, c: (b, cb0 + c)),
                  pl.BlockSpec((CONV_K, tc), lambda b, c: (0, c)),
                  pl.BlockSpec((1, tc), lambda b, c: (0, c))],
        out_specs=pl.BlockSpec((s, tc), lambda b, c: (b, c)),
        out_shape=jax.ShapeDtypeStruct((t, CONV_DIM), F32),
        scratch_shapes=[pltpu.VMEM((s + HALO, tc), F32), pltpu.VMEM((rows + HALO, tc), F32)],
        compiler_params=_params(("parallel", "parallel")), name="conv_fwd")(proj, conv_w, conv_b)


def _conv_bwd(proj, dact, conv_w, conv_b, dproj, nb, s, *, col0, name):
    width = dact.shape[1]
    tc, rows = min(CONV_TC, width), min(CONV_ROWS, s)
    cb0 = (CB_XBC * LANES + col0) // tc
    wb0 = col0 // tc

    def body(x_ref, d_ref, w_ref, b_ref, _, dx_ref, dw_ref, db_ref, xp, dp, dys, ws):
        @pl.when(pl.program_id(1) == 0)
        def _():
            dw_ref[...] = jnp.zeros_like(dw_ref)
            db_ref[...] = jnp.zeros_like(db_ref)

        n = rows + HALO
        pad = jnp.zeros((HALO, tc), F32)
        xp[0:HALO, :] = pad
        xp[HALO:HALO + s, :] = x_ref[...].astype(F32)
        xp[HALO + s:HALO + s + HALO, :] = pad
        dp[0:s, :] = d_ref[...]
        dp[s:s + HALO, :] = pad

        def chunk(i, acc):
            r0 = pl.multiple_of(i * rows, rows)
            ws[...] = xp[pl.ds(r0, n + HALO), :]
            y, taps = _conv_rows(ws, w_ref, b_ref, n)
            sg = _sigmoid(y)
            dy = dp[pl.ds(r0, n), :] * (sg * (1.0 + y * (1.0 - sg)))
            dys[...] = dy
            dx = w_ref[CONV_K - 1:CONV_K, :] * dy[0:rows]
            for kk in range(1, CONV_K):
                dx = dx + w_ref[CONV_K - 1 - kk:CONV_K - kk, :] * dys[kk:kk + rows, :]
            dx_ref[pl.ds(r0, rows), :] = _bf(dx)
            dyc = dy[0:rows]
            new = [acc[0] + jnp.sum(dyc, axis=0, keepdims=True)]
            for kk in range(CONV_K):
                new.append(acc[1 + kk] + jnp.sum(dyc * taps[kk][0:rows], axis=0, keepdims=True))
            return tuple(new)

        acc = lax.fori_loop(0, s // rows, chunk, (jnp.zeros((1, tc), F32),) * (1 + CONV_K))
        db_ref[...] += acc[0]
        for kk in range(CONV_K):
            dw_ref[CONV_K - 1 - kk:CONV_K - kk, :] += acc[1 + kk]

    t = nb * s
    return pl.pallas_call(
        body, grid=(width // tc, nb),
        in_specs=[pl.BlockSpec((s, tc), lambda c, b: (b, cb0 + c)),
                  pl.BlockSpec((s, tc), lambda c, b: (b, c)),
                  pl.BlockSpec((CONV_K, tc), lambda c, b: (0, wb0 + c)),
                  pl.BlockSpec((1, tc), lambda c, b: (0, wb0 + c)), pl.BlockSpec(memory_space=pl.ANY)],
        out_specs=[pl.BlockSpec((s, tc), lambda c, b: (b, cb0 + c)),
                   pl.BlockSpec((CONV_K, tc), lambda c, b: (0, c)),
                   pl.BlockSpec((1, tc), lambda c, b: (0, c))],
        out_shape=[jax.ShapeDtypeStruct(dproj.shape, dproj.dtype),
                   jax.ShapeDtypeStruct((CONV_K, width), F32),
                   jax.ShapeDtypeStruct((1, width), F32)],
        input_output_aliases={4: 0},
        scratch_shapes=[pltpu.VMEM((s + 2 * HALO, tc), F32), pltpu.VMEM((s + HALO, tc), F32),
                        pltpu.VMEM((rows + HALO, tc), F32), pltpu.VMEM((rows + 2 * HALO, tc), F32)],
        compiler_params=_params(("parallel", "arbitrary")), name=name)(proj, dact, conv_w, conv_b, dproj)


L = SSD_CHUNK
N_PAIR = SSD_HPG // 2


def _pair_lanes(v0, v1, lane):
    return jnp.where(lane < HEAD_DIM, v0, v1)


def _onehot(shape, row_shift, col_shift):
    row = lax.broadcasted_iota(jnp.int32, shape, 0)
    col = lax.broadcasted_iota(jnp.int32, shape, 1)
    return _bf(lax.shift_right_logical(row, row_shift) == lax.shift_right_logical(col, col_shift))


def _ssd_tables():
    sp = _onehot((LANES, N_PAIR * LANES), 0, 6)
    sf = _onehot((LANES, SSD_HPG * LANES), 0, 7)
    return (jnp.concatenate([sp] * 3, axis=0), jnp.concatenate([sf] * 3, axis=0),
            _onehot((N_PAIR * LANES, LANES), 6, 0), _onehot((SSD_HPG * LANES, LANES), 7, 0))


def _split3(a):
    a1 = _bf(a)
    r1 = a - a1.astype(F32)
    a2 = _bf(r1)
    return jnp.concatenate([a1, a2, _bf(r1 - a2.astype(F32))], axis=1)


def _split2(a):
    hi, lo = _trunc_split(a)
    return jnp.concatenate([hi, lo], axis=1)


def _tri_dot(tri_bf, a):
    a1 = _bf(a)
    r1 = a - a1.astype(F32)
    a2 = _bf(r1)
    parts = jnp.concatenate([a1, a2, _bf(r1 - a2.astype(F32))], axis=0)
    return _dot(jnp.concatenate([tri_bf] * 3, axis=1), parts)


HEAD_W = 4 * LANES + 2 * N_PAIR * LANES + SSD_HPG * LANES


def _ssd_head(dtr_ref, bias_ref, a_neg, c, tri_bf, spread_pair, spread_full, head_s, slot):
    raw = dtr_ref[pl.ds(pl.multiple_of(c * L, L), L), :] + bias_ref[...]
    dt = _softplus(raw)
    acs = _tri_dot(tri_bf, dt * a_neg)
    acs3 = _split3(acs)
    both = _dot(jnp.concatenate([_split3(dt), acs3], axis=0), spread_pair)
    parts = (raw, dt, acs, acs.T, both[:L], both[L:], _dot(acs3, spread_full))
    off = 0
    for p in parts:
        head_s[slot, :, off:off + p.shape[1]] = p
        off += p.shape[1]


def _ssd_head_load(head_s, slot):
    out, off = [], 0
    for w in (LANES, LANES, LANES, L, N_PAIR * LANES, N_PAIR * LANES, SSD_HPG * LANES):
        out.append(head_s[slot, :, off:off + w])
        off += w
    return out


def _ssd_fwd(xbc, dtr, bias, alog, dskip, nb, s):
    nc = s // L
    t = nb * s

    def body(x_ref, b_ref, c_ref, dtr_ref, bias_ref, alog_ref, dsk_ref, y_ref, hs_ref, h_scr, head_s):
        lane = lax.broadcasted_iota(jnp.int32, (L, LANES), 1)
        lane1 = lax.broadcasted_iota(jnp.int32, (1, LANES), 1)
        rowc = lax.broadcasted_iota(jnp.int32, (L, 1), 0)
        rr = lax.broadcasted_iota(jnp.int32, (L, L), 0)
        cc = lax.broadcasted_iota(jnp.int32, (L, L), 1)
        causal = rr >= cc
        h_scr[...] = jnp.zeros_like(h_scr)
        dsk = dsk_ref[...]
        spread_pair, spread_full, _, _ = _ssd_tables()
        a_neg = -jnp.exp(alog_ref[...])
        head = functools.partial(_ssd_head, dtr_ref, bias_ref, a_neg, tri_bf=_bf(cc <= rr),
                                 spread_pair=spread_pair, spread_full=spread_full, head_s=head_s)
        head(jnp.int32(0), slot=0)

        def chunk(c, slot):
            r0 = pl.multiple_of(c * L, L)
            head(jnp.minimum(c + 1, nc - 1), slot=1 - slot)
            _, _, acs, acs_t, dt_all, acs_all, acs_full = _ssd_head_load(head_s, slot)
            bb = _bf(b_ref[pl.ds(r0, L), :])
            cb = _bf(c_ref[pl.ds(r0, L), :])
            gm = _dot_nt(cb, bb)
            for pr in range(N_PAIR):
                j0, j1 = 2 * pr, 2 * pr + 1
                ms = []
                for j in (j0, j1):
                    seg = acs_full[:, j * LANES:(j + 1) * LANES] - acs_t[j:j + 1, :]
                    ms.append(_bf(gm * jnp.exp(jnp.where(causal, seg, -1e30))))
                x = x_ref[pl.ds(r0, L), pr * LANES:(pr + 1) * LANES]
                dt_p = dt_all[:, pr * LANES:(pr + 1) * LANES]
                acs_p = acs_all[:, pr * LANES:(pr + 1) * LANES]
                last_p = acs_p[L - 1:L, :]
                d_p = _pair_lanes(dsk[:, j0:j0 + 1], dsk[:, j1:j1 + 1], lane1)
                xd = x * dt_p
                xdb = _bf(xd)
                yds = _dot(jnp.concatenate(ms, axis=0), xdb)
                yd = _pair_lanes(yds[:L], yds[L:], lane)
                hp = h_scr[pr]
                yo = _dot_nt(cb, _bf(hp)) * jnp.exp(acs_p)
                y_ref[pl.ds(r0, L), pr * LANES:(pr + 1) * LANES] = yd + yo + x * d_p
                hs_ref[c, pr] = hp
                sp = _dot_tn(_bf(xd * jnp.exp(last_p - acs_p)), bb)
                cd = jnp.where(rowc < HEAD_DIM, jnp.exp(acs[L - 1:L, j0:j0 + 1]),
                               jnp.exp(acs[L - 1:L, j1:j1 + 1]))
                h_scr[pr] = hp * cd + sp

        def two_chunks(i, _):
            chunk(2 * i, 0)
            chunk(2 * i + 1, 1)
            return 0

        lax.fori_loop(0, nc // 2, two_chunks, 0)

    assert nc % 2 == 0
    grp = lambda b, g: (g, 0, 0)
    return pl.pallas_call(
        body, grid=(nb, SSD_GROUPS),
        in_specs=[pl.BlockSpec((s, 4 * LANES), lambda b, g: (b, g)),
                  pl.BlockSpec((s, LANES), lambda b, g: (b, 16 + g)),
                  pl.BlockSpec((s, LANES), lambda b, g: (b, 20 + g)),
                  pl.BlockSpec((None, s, LANES), lambda b, g: (g, b, 0)),
                  pl.BlockSpec((None, 1, LANES), grp), pl.BlockSpec((None, 1, LANES), grp),
                  pl.BlockSpec((None, 1, LANES), grp)],
        out_specs=[pl.BlockSpec((s, 4 * LANES), lambda b, g: (b, g)),
                   pl.BlockSpec((None, None, nc, N_PAIR, L, LANES), lambda b, g: (b, g, 0, 0, 0, 0))],
        out_shape=[jax.ShapeDtypeStruct((t, SSD_WIDTH), F32),
                   jax.ShapeDtypeStruct((nb, SSD_GROUPS, nc, N_PAIR, L, LANES), F32)],
        scratch_shapes=[pltpu.VMEM((N_PAIR, L, LANES), F32), pltpu.VMEM((2, L, HEAD_W), F32)],
        compiler_params=_params(("parallel", "parallel")), name="ssd_fwd")(
            xbc, xbc, xbc, dtr, bias, alog, dskip)


def _ssd_bwd(xbc, dtr, bias, alog, dskip, hs, dy, nb, s):
    nc = s // L
    t = nb * s

    def body(x_ref, b_ref, c_ref, dtr_ref, bias_ref, alog_ref, dsk_ref, hs_ref, dy_ref,
             dx_ref, db_ref, dc_ref, ddtr_ref, dbias_ref, dalog_ref, ddsk_ref, dh_scr, head_s):
        lane = lax.broadcasted_iota(jnp.int32, (L, LANES), 1)
        lane1 = lax.broadcasted_iota(jnp.int32, (1, LANES), 1)
        rowc = lax.broadcasted_iota(jnp.int32, (L, 1), 0)
        rr = lax.broadcasted_iota(jnp.int32, (L, L), 0)
        cc = lax.broadcasted_iota(jnp.int32, (L, L), 1)
        causal = rr >= cc
        tri_t = _bf(cc >= rr)
        dh_scr[...] = jnp.zeros_like(dh_scr)
        dsk = dsk_ref[...]
        a_neg = -jnp.exp(alog_ref[...])

        @pl.when(pl.program_id(1) == 0)
        def _():
            dbias_ref[...] = jnp.zeros_like(dbias_ref)
            dalog_ref[...] = jnp.zeros_like(dalog_ref)
            ddsk_ref[...] = jnp.zeros_like(ddsk_ref)

        spread_pair, spread_full, sum_pair, sum_full = _ssd_tables()
        sum_pair2 = jnp.concatenate([sum_pair] * 2, axis=0)
        sum_full2 = jnp.concatenate([sum_full] * 2, axis=0)
        head = functools.partial(_ssd_head, dtr_ref, bias_ref, a_neg, tri_bf=_bf(cc <= rr),
                                 spread_pair=spread_pair, spread_full=spread_full, head_s=head_s)
        head(jnp.int32(nc - 1), slot=0)

        def chunk(c, slot):
            r0 = pl.multiple_of(c * L, L)
            head(jnp.maximum(c - 1, 0), slot=1 - slot)
            raw, dt, acs, acs_t, dt_all, acs_all, acs_full = _ssd_head_load(head_s, slot)
            bb = _bf(b_ref[pl.ds(r0, L), :])
            cb = _bf(c_ref[pl.ds(r0, L), :])
            gm = _dot_nt(cb, bb)
            dg = jnp.zeros((L, L), F32)
            dbacc = jnp.zeros((L, LANES), F32)
            dcacc = jnp.zeros((L, LANES), F32)
            dacs_t = jnp.zeros((L, L), F32)
            ta_all, te_all, qm_all, tc_all, td_all, dcd_all = [], [], [], [], [], []
            for pr in range(N_PAIR):
                js = (2 * pr, 2 * pr + 1)
                lms, mfs = [], []
                for j in js:
                    seg = acs_full[:, j * LANES:(j + 1) * LANES] - acs_t[j:j + 1, :]
                    lm = jnp.exp(jnp.where(causal, seg, -1e30))
                    lms.append(lm)
                    mfs.append(gm * lm)
                x = x_ref[pl.ds(r0, L), pr * LANES:(pr + 1) * LANES]
                dyv = dy_ref[pl.ds(r0, L), pr * LANES:(pr + 1) * LANES]
                dt_p = dt_all[:, pr * LANES:(pr + 1) * LANES]
                acs_p = acs_all[:, pr * LANES:(pr + 1) * LANES]
                last_p = acs_p[L - 1:L, :]
                d_p = _pair_lanes(dsk[:, js[0]:js[0] + 1], dsk[:, js[1]:js[1] + 1], lane1)
                e_p = jnp.exp(acs_p)
                dte_p = jnp.exp(last_p - acs_p)
                xd = x * dt_p
                xdb = _bf(xd)
                hp = hs_ref[c, pr]
                hb = _bf(hp)
                dye = _bf(dyv * e_p)
                yo = _dot_nt(cb, hb) * e_p
                dcacc = dcacc + _dot(dye, hb)
                dh_out = _dot_tn(dye, cb)
                dyb = _bf(dyv)
                dms = _dot_nt(_bf(jnp.concatenate(
                    [jnp.where(lane < HEAD_DIM, dyv, 0.0), jnp.where(lane < HEAD_DIM, 0.0, dyv)], axis=0)), xdb)
                mty = _dot_tn(_bf(jnp.concatenate(mfs, axis=1)), dyb)
                dxd = jnp.where(lane < HEAD_DIM, mty[:L], mty[L:])
                for hh in range(2):
                    dm = dms[hh * L:(hh + 1) * L]
                    dg = dg + dm * lms[hh]
                    qm = dm * mfs[hh]
                    qm_all.append(qm)
                    dacs_t = dacs_t - jnp.where(rr == js[hh], jnp.sum(qm, axis=0, keepdims=True), 0.0)
                ds = dh_scr[pr]
                dsb = _bf(ds)
                xdd = xd * dte_p
                dxdd = _dot_nt(bb, dsb)
                dbacc = dbacc + _dot(_bf(xdd), dsb)
                dxd = dxd + dxdd * dte_p
                cd = jnp.where(rowc < HEAD_DIM, jnp.exp(acs[L - 1:L, js[0]:js[0] + 1]),
                               jnp.exp(acs[L - 1:L, js[1]:js[1] + 1]))
                prod = ds * hp
                dcd_all += [jnp.sum(prod[:HEAD_DIM], axis=0, keepdims=True),
                            jnp.sum(prod[HEAD_DIM:], axis=0, keepdims=True)]
                dh_scr[pr] = dh_out + ds * cd
                tcv = dxdd * xdd
                ta_all.append(dyv * yo - tcv)
                tc_all.append(jnp.sum(tcv, axis=0, keepdims=True))
                te_all.append(dxd * x)
                td_all.append(jnp.sum(dyv * x, axis=0, keepdims=True))
                dx_ref[pl.ds(r0, L), pr * LANES:(pr + 1) * LANES] = dxd * dt_p + dyv * d_p
            dgb = _bf(dg)
            dc_ref[pl.ds(r0, L), :] = dcacc + _dot(dgb, bb)
            db_ref[pl.ds(r0, L), :] = dbacc + _dot_tn(dgb, cb)
            last_add = jnp.zeros((1, LANES), F32)
            dd_add = jnp.zeros((1, LANES), F32)
            cd_row = jnp.exp(acs[L - 1:L, :])
            for j in range(SSD_HPG):
                hm = (lane1 >= HEAD_DIM) if j % 2 else (lane1 < HEAD_DIM)
                tcj = jnp.sum(jnp.where(hm, tc_all[j // 2], 0.0), axis=1, keepdims=True)
                tdj = jnp.sum(jnp.where(hm, td_all[j // 2], 0.0), axis=1, keepdims=True)
                dcdj = jnp.sum(dcd_all[j], axis=1, keepdims=True)
                last_add = last_add + jnp.where(lane1 == j, tcj + dcdj * cd_row, 0.0)
                dd_add = dd_add + jnp.where(lane1 == j, tdj, 0.0)
            dacs = (_dot(_split2(jnp.concatenate(ta_all, axis=1)), sum_pair2)
                    + _dot(_split2(jnp.concatenate(qm_all, axis=1)), sum_full2)
                    + dacs_t.T + jnp.where(rowc == L - 1, last_add, 0.0))
            ddta = _tri_dot(tri_t, dacs)
            ddt = _dot(_bf(jnp.concatenate(te_all, axis=1)), sum_pair) + ddta * a_neg
            ddraw = ddt * _sigmoid(raw)
            ddtr_ref[pl.ds(r0, L), :] = ddraw
            dbias_ref[...] += jnp.sum(ddraw, axis=0, keepdims=True)
            dalog_ref[...] += jnp.sum(ddta * dt, axis=0, keepdims=True) * a_neg
            ddsk_ref[...] += dd_add

        def two_chunks(i, _):
            chunk(nc - 1 - 2 * i, 0)
            chunk(nc - 2 - 2 * i, 1)
            return 0

        lax.fori_loop(0, nc // 2, two_chunks, 0)

    assert nc % 2 == 0
    grp = lambda g, b: (g, 0, 0)
    small = jax.ShapeDtypeStruct((SSD_GROUPS, 1, LANES), F32)
    return pl.pallas_call(
        body, grid=(SSD_GROUPS, nb),
        in_specs=[pl.BlockSpec((s, 4 * LANES), lambda g, b: (b, g)),
                  pl.BlockSpec((s, LANES), lambda g, b: (b, 16 + g)),
                  pl.BlockSpec((s, LANES), lambda g, b: (b, 20 + g)),
                  pl.BlockSpec((None, s, LANES), lambda g, b: (g, b, 0)),
                  pl.BlockSpec((None, 1, LANES), grp), pl.BlockSpec((None, 1, LANES), grp),
                  pl.BlockSpec((None, 1, LANES), grp),
                  pl.BlockSpec((None, None, nc, N_PAIR, L, LANES), lambda g, b: (b, g, 0, 0, 0, 0)),
                  pl.BlockSpec((s, 4 * LANES), lambda g, b: (b, g))],
        out_specs=[pl.BlockSpec((s, 4 * LANES), lambda g, b: (b, g)),
                   pl.BlockSpec((s, LANES), lambda g, b: (b, g)),
                   pl.BlockSpec((s, LANES), lambda g, b: (b, g)),
                   pl.BlockSpec((None, s, LANES), lambda g, b: (g, b, 0)),
                   pl.BlockSpec((None, 1, LANES), grp), pl.BlockSpec((None, 1, LANES), grp),
                   pl.BlockSpec((None, 1, LANES), grp)],
        out_shape=[jax.ShapeDtypeStruct((t, SSD_WIDTH), F32),
                   jax.ShapeDtypeStruct((t, SSD_GROUPS * SSD_STATE), F32),
                   jax.ShapeDtypeStruct((t, SSD_GROUPS * SSD_STATE), F32),
                   jax.ShapeDtypeStruct((SSD_GROUPS, t, LANES), F32), small, small, small],
        scratch_shapes=[pltpu.VMEM((N_PAIR, L, LANES), F32), pltpu.VMEM((2, L, HEAD_W), F32)],
        compiler_params=_params(("parallel", "arbitrary")), name="ssd_bwd")(
            xbc, xbc, xbc, dtr, bias, alog, dskip, hs, dy)


GN = SSD_WIDTH // SSD_GROUPS


def _post_fwd(o, proj, y, nw, *, tm=256):
    t = o.shape[0]

    def body(o_ref, za_ref, y_ref, zs_ref, nw_ref, ya_ref, ys_ref):
        za = za_ref[...].astype(F32)
        ya_ref[...] = _bf(o_ref[...] * (za * _sigmoid(za)))
        zs = zs_ref[...].astype(F32)
        yz = y_ref[...] * (zs * _sigmoid(zs))
        for g in range(SSD_GROUPS):
            sl = slice(g * GN, (g + 1) * GN)
            v = yz[:, sl]
            r = lax.rsqrt(jnp.mean(v * v, axis=-1, keepdims=True) + EPS)
            ys_ref[:, sl] = _bf(v * r * nw_ref[:, sl])

    return pl.pallas_call(
        body, grid=(t // tm,),
        in_specs=[pl.BlockSpec((tm, SB_WIDTH), lambda i: (i, 0)),
                  pl.BlockSpec((tm, SB_WIDTH), lambda i: (i, CB_ZA * LANES // SB_WIDTH)),
                  pl.BlockSpec((tm, SSD_WIDTH), lambda i: (i, 0)),
                  pl.BlockSpec((tm, SSD_WIDTH), lambda i: (i, CB_ZS * LANES // SSD_WIDTH)),
                  pl.BlockSpec((1, SSD_WIDTH), lambda i: (0, 0))],
        out_specs=[pl.BlockSpec((tm, SB_WIDTH), lambda i: (i, 0)),
                   pl.BlockSpec((tm, SSD_WIDTH), lambda i: (i, 0))],
        out_shape=[jax.ShapeDtypeStruct((t, SB_WIDTH), BF16), jax.ShapeDtypeStruct((t, SSD_WIDTH), BF16)],
        compiler_params=_params(("parallel",)), name="post_fwd")(o, proj, y, proj, nw)


def _post_bwd(dya, o, proj, dys, y, nw, dproj, *, tm=256):
    t = o.shape[0]
    zw = SB_WIDTH + SSD_WIDTH

    def body(dya_ref, o_ref, za_ref, dys_ref, y_ref, zs_ref, nw_ref, _, do_ref, dy_ref, dnw_ref, dz_ref):
        @pl.when(pl.program_id(0) == 0)
        def _():
            dnw_ref[...] = jnp.zeros_like(dnw_ref)

        za = za_ref[...].astype(F32)
        sa = _sigmoid(za)
        d = dya_ref[...].astype(F32)
        do_ref[...] = _bf(d * (za * sa))
        dz_ref[:, :SB_WIDTH] = _bf(d * o_ref[...] * (sa * (1.0 + za * (1.0 - sa))))
        zs = zs_ref[...].astype(F32)
        ss = _sigmoid(zs)
        silu = zs * ss
        yv = y_ref[...]
        yz = yv * silu
        dv = dys_ref[...].astype(F32)
        for g in range(SSD_GROUPS):
            sl = slice(g * GN, (g + 1) * GN)
            v = yz[:, sl]
            r = lax.rsqrt(jnp.mean(v * v, axis=-1, keepdims=True) + EPS)
            nrm = v * r
            dg = dv[:, sl]
            dnw_ref[:, sl] += jnp.sum(dg * nrm, axis=0, keepdims=True)
            dn = dg * nw_ref[:, sl]
            dyz = r * (dn - nrm * jnp.mean(dn * nrm, axis=-1, keepdims=True))
            dy_ref[:, sl] = dyz * silu[:, sl]
            dz_ref[:, SB_WIDTH + g * GN:SB_WIDTH + (g + 1) * GN] = _bf(
                dyz * yv[:, sl] * (ss[:, sl] * (1.0 + zs[:, sl] * (1.0 - ss[:, sl]))))

    a_spec = pl.BlockSpec((tm, SB_WIDTH), lambda i: (i, 0))
    s_spec = pl.BlockSpec((tm, SSD_WIDTH), lambda i: (i, 0))
    w_spec = pl.BlockSpec((1, SSD_WIDTH), lambda i: (0, 0))
    return pl.pallas_call(
        body, grid=(t // tm,),
        in_specs=[a_spec, a_spec, pl.BlockSpec((tm, SB_WIDTH), lambda i: (i, CB_ZA * LANES // SB_WIDTH)),
                  s_spec, s_spec, pl.BlockSpec((tm, SSD_WIDTH), lambda i: (i, CB_ZS * LANES // SSD_WIDTH)),
                  w_spec, pl.BlockSpec(memory_space=pl.ANY)],
        out_specs=[a_spec, s_spec, w_spec, pl.BlockSpec((tm, zw), lambda i: (i, 1))],
        out_shape=[jax.ShapeDtypeStruct((t, SB_WIDTH), BF16), jax.ShapeDtypeStruct((t, SSD_WIDTH), F32),
                   jax.ShapeDtypeStruct((1, SSD_WIDTH), F32), jax.ShapeDtypeStruct(dproj.shape, dproj.dtype)],
        input_output_aliases={7: 3},
        compiler_params=_params(("arbitrary",)), name="post_bwd")(dya, o, proj, dys, y, proj, nw, dproj)


def _merge_fwd(ya, ys, proj_g, *, tm=512):
    t = ya.shape[0]

    def body(ya_ref, ys_ref, ga_ref, gs_ref, m_ref):
        m_ref[...] = _bf(_sigmoid(ga_ref[...].astype(F32)) * ya_ref[...].astype(F32)
                         + _sigmoid(gs_ref[...].astype(F32)) * ys_ref[...].astype(F32))

    spec = pl.BlockSpec((tm, D_MODEL), lambda i: (i, 0))
    return pl.pallas_call(
        body, grid=(t // tm,),
        in_specs=[spec, spec, pl.BlockSpec((tm, D_MODEL), lambda i: (i, 0)),
                  pl.BlockSpec((tm, D_MODEL), lambda i: (i, 1))],
        out_specs=spec, out_shape=jax.ShapeDtypeStruct((t, D_MODEL), BF16),
        compiler_params=_params(("parallel",)), name="merge_fwd")(ya, ys, proj_g, proj_g)


def _merge_bwd(dm, ya, ys, proj_g, *, tm=512):
    t = ya.shape[0]

    def body(dm_ref, ya_ref, ys_ref, ga_ref, gs_ref, dya_ref, dys_ref, dg_ref):
        d = dm_ref[...].astype(F32)
        ga = _sigmoid(ga_ref[...].astype(F32))
        gs = _sigmoid(gs_ref[...].astype(F32))
        dya_ref[...] = _bf(d * ga)
        dys_ref[...] = _bf(d * gs)
        dg_ref[:, :D_MODEL] = _bf(d * ya_ref[...].astype(F32) * (ga * (1.0 - ga)))
        dg_ref[:, D_MODEL:] = _bf(d * ys_ref[...].astype(F32) * (gs * (1.0 - gs)))

    spec = pl.BlockSpec((tm, D_MODEL), lambda i: (i, 0))
    return pl.pallas_call(
        body, grid=(t // tm,),
        in_specs=[spec, spec, spec, pl.BlockSpec((tm, D_MODEL), lambda i: (i, 0)),
                  pl.BlockSpec((tm, D_MODEL), lambda i: (i, 1))],
        out_specs=[spec, spec, pl.BlockSpec((tm, 2 * D_MODEL), lambda i: (i, 0))],
        out_shape=[jax.ShapeDtypeStruct((t, D_MODEL), BF16), jax.ShapeDtypeStruct((t, D_MODEL), BF16),
                   jax.ShapeDtypeStruct((t, 2 * D_MODEL), BF16)],
        compiler_params=_params(("parallel",)), name="merge_bwd")(dm, ya, ys, proj_g, proj_g)


def _final(x, out, wf, target, *, tm=512):
    t = x.shape[0]

    def body(x_ref, o_ref, w_ref, t_ref, loss_ref, dx_ref, dw_ref):
        @pl.when(pl.program_id(0) == 0)
        def _():
            loss_ref[...] = jnp.zeros_like(loss_ref)
            dw_ref[...] = jnp.zeros_like(dw_ref)

        x2 = x_ref[...] + o_ref[...]
        r = lax.rsqrt(jnp.mean(x2 * x2, axis=-1, keepdims=True) + EPS)
        nrm = x2 * r
        e = nrm * w_ref[...] - t_ref[...]
        row_loss = jnp.mean(e * e, axis=-1, keepdims=True)
        loss_ref[...] += 0.5 * jnp.sum(row_loss, axis=0, keepdims=True)
        dyv = e * (1.0 / D_MODEL)
        dw_ref[...] += jnp.sum(dyv * nrm, axis=0, keepdims=True)
        dn = dyv * w_ref[...]
        dx_ref[...] = r * (dn - nrm * jnp.mean(dn * nrm, axis=-1, keepdims=True))

    spec = pl.BlockSpec((tm, D_MODEL), lambda i: (i, 0))
    fix = pl.BlockSpec((1, D_MODEL), lambda i: (0, 0))
    return pl.pallas_call(
        body, grid=(t // tm,),
        in_specs=[spec, spec, fix, spec],
        out_specs=[pl.BlockSpec((1, LANES), lambda i: (0, 0)), spec, fix],
        out_shape=[jax.ShapeDtypeStruct((1, LANES), F32), jax.ShapeDtypeStruct((t, D_MODEL), F32),
                   jax.ShapeDtypeStruct((1, D_MODEL), F32)],
        compiler_params=_params(("arbitrary",)), name="final_loss")(x, out, wf, target)


def _group_lanes(v):
    return jnp.pad(v.reshape(SSD_GROUPS, 1, SSD_HPG), ((0, 0), (0, 0), (0, LANES - SSD_HPG)))


def _ungroup_lanes(v):
    return v[:, :, :SSD_HPG].reshape(1, N_DT)


def _row_slabs(pieces, rows):
    total = sum(p.shape[0] for p in pieces)
    assert total % rows == 0
    slabs = []
    for lo in range(0, total, rows):
        parts, off = [], 0
        for p in pieces:
            a, b = max(lo, off), min(lo + rows, off + p.shape[0])
            if a < b:
                parts.append(p[a - off:b - off])
            off += p.shape[0]
        slabs.append(parts[0] if len(parts) == 1 else jnp.concatenate(parts, axis=0))
    return jnp.stack(slabs)


def _local_step(x, target, nb, s, w_in_t, conv_w, conv_b, norm_w, dt_bias, a_log, d_skip,
                ssm_norm_w, out_weights, final_norm_w, between=None):
    t = nb * s
    w_g_t = w_in_t[GATE_COL0:]
    w_dt_t = jnp.pad(w_in_t[DT_COL0:GATE_COL0], ((0, LANES - N_DT), (0, 0)))
    h, r1 = _rms_fwd(x, norm_w)
    proj = _matmul(h, w_in_t, nt=True, n=DT_COL0, out_dtype=BF16, name="proj_main", tm=MM_TALL)
    proj_g = _matmul(h, w_g_t, nt=True, out_dtype=BF16, name="proj_gate", tm=MM_TALL)
    dt_raw = _matmul(h, w_dt_t, nt=True, name="proj_dt")
    dtr = jnp.pad(dt_raw[:, :N_DT].reshape(t, SSD_GROUPS, SSD_HPG).transpose(1, 0, 2),
                  ((0, 0), (0, 0), (0, LANES - SSD_HPG)))
    bias_g, alog_g, dsk_g = _group_lanes(dt_bias), _group_lanes(a_log), _group_lanes(d_skip)

    o, att_a, att_sg = _attn_fwd(proj, nb, s)
    xbc = _conv_fwd(proj, conv_w, conv_b, nb, s)
    y, hs = _ssd_fwd(xbc, dtr, bias_g, alog_g, dsk_g, nb, s)
    ya_in, ys_in = _post_fwd(o, proj, y, ssm_norm_w)
    w_attn_out, w_ssm_out, w_o = out_weights(ys_in) if callable(out_weights) else out_weights
    ya = _matmul(ya_in, w_attn_out, out_dtype=BF16, name="attn_out")
    ys = _matmul(ys_in, w_ssm_out, out_dtype=BF16, name="ssm_out")
    merged = _merge_fwd(ya, ys, proj_g)
    out = _matmul(merged, w_o, name="out_proj")
    loss, dx2, d_final_w = _final(x, out, final_norm_w, target)

    dx2b = _bf(dx2)
    dmerged = _matmul(dx2b, w_o, nt=True, out_dtype=BF16, name="d_merged")
    d_w_o = _matmul(merged, dx2b, ta=True, name="d_w_o")
    dya, dys, dgate = _merge_bwd(dmerged, ya, ys, proj_g)
    d_w_attn_out = _matmul(ya_in, dya, ta=True, name="d_w_attn_out")
    d_w_ssm_out = _matmul(ys_in, dys, ta=True, name="d_w_ssm_out")
    dya_in = _matmul(dya, w_attn_out, nt=True, out_dtype=BF16, name="d_ya_in")
    dys_in = _matmul(dys, w_ssm_out, nt=True, name="d_ys_in")
    do, dy, d_ssm_norm_w, dproj = _post_bwd(dya_in, o, proj, dys_in, y, ssm_norm_w, lax.empty(proj.shape, BF16))
    dq, dk, dv = _attn_bwd(proj, do, att_a, att_sg, nb, s)
    dxs_act, db_act, dc_act, ddtr, dbias_g, dalog_g, ddsk_g = _ssd_bwd(
        xbc, dtr, bias_g, alog_g, dsk_g, hs, dy, nb, s)
    dproj, dwx, dbx = _conv_bwd(proj, dxs_act, conv_w, conv_b, dproj, nb, s, col0=0, name="conv_bwd_x")
    dproj, dwb, dbb = _conv_bwd(proj, db_act, conv_w, conv_b, dproj, nb, s, col0=SSD_WIDTH, name="conv_bwd_b")
    dproj, dwc, dbc = _conv_bwd(proj, dc_act, conv_w, conv_b, dproj, nb, s, col0=SSD_WIDTH + 512,
                                name="conv_bwd_c")
    for i, piece in enumerate((dq, dk, dv)):
        dproj = lax.dynamic_update_slice(dproj, piece, (0, i * SB_WIDTH))
    ddt = jnp.pad(_bf(ddtr[:, :, :SSD_HPG].transpose(1, 0, 2).reshape(t, N_DT)), ((0, 0), (0, LANES - N_DT)))
    wt = dict(ta=True, out_dtype=BF16)
    d_w_in_t = _row_slabs(
        [_matmul(dproj, h, name="d_w_main", **wt), _matmul(ddt, h, name="d_w_dt", **wt)[:N_DT],
         _matmul(dgate, h, name="d_w_gate", **wt)], SHARD)
    grads = dict(
        w_in_t=d_w_in_t,
        conv_w=jnp.concatenate([dwx, dwb, dwc], axis=1), conv_b=jnp.concatenate([dbx, dbb, dbc], axis=1),
        dt_bias=_ungroup_lanes(dbias_g), a_log=_ungroup_lanes(dalog_g), d_skip=_ungroup_lanes(ddsk_g),
        ssm_norm_w=d_ssm_norm_w, w_attn_out=d_w_attn_out, w_ssm_out=d_w_ssm_out, w_o=d_w_o,
        final_norm_w=d_final_w)
    extra = None
    if between is not None:
        zero, extra = between(grads)
        ddt = ddt + _bf(zero[0:1, 0:1])

    dh = _matmul(ddt, w_dt_t, name="d_h_dt")
    dh = _matmul(dgate, w_g_t, add=dh, name="d_h_gate")
    dh = _matmul(dproj, w_in_t, add=dh, name="d_h", tk=DT_COL0 // 4)
    grad_x, grads["norm_w"] = _rms_bwd(dh, x, r1, norm_w, dx2)
    return loss, grad_x, grads, extra


N_CHIP = N_DEV // 2


def _pair_exchange(scatter, bcast, *, name):
    arrays = list(scatter) + list(bcast)
    ns, n = len(scatter), len(arrays)
    n_sem = ns * N_CHIP + (n - ns) * (N_DEV - 1)

    def body(*refs):
        ins, outs = refs[:n], refs[n:2 * n]
        send_sems, recv_sems, local_sems = refs[2 * n:]
        x, y, c = lax.axis_index("x"), lax.axis_index("y"), lax.axis_index("c")
        me = 4 * x + 2 * y + c
        copies, k = [], 0
        for a in range(ns):
            for q in range(N_CHIP):
                cp = pltpu.make_async_remote_copy(
                    src_ref=ins[a].at[2 * q + 1 - c], dst_ref=outs[a].at[q],
                    send_sem=send_sems.at[k], recv_sem=recv_sems.at[k],
                    device_id=(x, y, 1 - c), device_id_type=pl.DeviceIdType.MESH)
                cp.start()
                copies.append(cp)
                k += 1
        for a in range(ns, n):
            cp = pltpu.make_async_copy(ins[a], outs[a].at[me], local_sems.at[a - ns])
            cp.start()
            copies.append(cp)
            for r in range(1, N_DEV):
                tx = 1 - x if (r >> 2) & 1 else x
                ty = 1 - y if (r >> 1) & 1 else y
                tc = 1 - c if r & 1 else c
                cp = pltpu.make_async_remote_copy(
                    src_ref=ins[a], dst_ref=outs[a].at[me], send_sem=send_sems.at[k], recv_sem=recv_sems.at[k],
                    device_id=(tx, ty, tc), device_id_type=pl.DeviceIdType.MESH)
                cp.start()
                copies.append(cp)
                k += 1
        for cp in copies:
            cp.wait()

    out_shape = [jax.ShapeDtypeStruct((N_CHIP,) + v.shape[1:] if i < ns else (N_DEV,) + v.shape, v.dtype)
                 for i, v in enumerate(arrays)]
    return pl.pallas_call(
        body, in_specs=[pl.BlockSpec(memory_space=pl.ANY)] * n,
        out_specs=[pl.BlockSpec(memory_space=pl.ANY)] * n, out_shape=out_shape,
        scratch_shapes=[pltpu.SemaphoreType.DMA((n_sem,)), pltpu.SemaphoreType.DMA((n_sem,)),
                        pltpu.SemaphoreType.DMA((max(n - ns, 1),))],
        compiler_params=pltpu.CompilerParams(has_side_effects=True), name=name)(*arrays)


def _pair_add(mine, recv, core, *, name, tc=None):
    _, r, c = recv.shape
    tc = c if tc is None else tc
    assert c % tc == 0

    def body(core_ref, m_ref, r_ref, o_ref):
        o_ref[...] = (m_ref[...].astype(F32) + r_ref[...].astype(F32)).astype(o_ref.dtype)

    spec = pl.BlockSpec((None, r, tc), lambda q, i, core_ref: (q, 0, i))
    return pl.pallas_call(
        body, grid_spec=pltpu.PrefetchScalarGridSpec(
            num_scalar_prefetch=1, grid=(N_CHIP, c // tc),
            in_specs=[pl.BlockSpec((None, r, tc), lambda q, i, core_ref: (2 * q + core_ref[0], 0, i)), spec],
            out_specs=spec),
        out_shape=jax.ShapeDtypeStruct(recv.shape, recv.dtype),
        compiler_params=_params(("parallel", "parallel")), name=name)(core, mine, recv)


_HBM = pl.BlockSpec(memory_space=pltpu.HBM)
_SEM = pl.BlockSpec(memory_space=pltpu.SEMAPHORE)
_DATAFLOW = pltpu.SideEffectType.DATAFLOW_SIDE_EFFECTING
_PEERS = {"chips": N_CHIP - 1, "all": N_DEV - 1}


def _split_copies(pattern, srcs, lands, send_sems, recv_sems):
    x, y, c = lax.axis_index("x"), lax.axis_index("y"), lax.axis_index("c")
    per = _PEERS[pattern]
    copies = []
    for a in range(len(srcs)):
        for r in range(1, per + 1):
            if pattern == "chips":
                tx = 1 - x if (r >> 1) & 1 else x
                ty = 1 - y if r & 1 else y
                tc = c
                src, dst = srcs[a].at[2 * tx + ty], lands[a].at[2 * x + y]
            else:
                tx = 1 - x if (r >> 2) & 1 else x
                ty = 1 - y if (r >> 1) & 1 else y
                tc = 1 - c if r & 1 else c
                src, dst = srcs[a], lands[a].at[4 * x + 2 * y + c]
            copies.append(pltpu.make_async_remote_copy(
                src_ref=src, dst_ref=dst, send_sem=send_sems.at[a * per + r - 1],
                recv_sem=recv_sems.at[a * per + r - 1], device_id=(tx, ty, tc), device_id_type=pl.DeviceIdType.MESH))
    return copies


def _split_start(pattern, arrays, *, name):
    n = len(arrays)
    n_sem = n * _PEERS[pattern]

    def body(*refs):
        srcs, lands = refs[:n], refs[n:2 * n]
        send_sems, recv_sems = refs[2 * n], refs[2 * n + 1]
        zero = refs[-1]
        for cp in _split_copies(pattern, srcs, lands, send_sems, recv_sems):
            cp.start()
        zero[...] = jnp.zeros_like(zero)

    src_t = [pltpu.HBM(v.shape, v.dtype) for v in arrays]
    land_t = [pltpu.HBM(v.shape if pattern == "chips" else (N_DEV,) + v.shape, v.dtype) for v in arrays]
    outs = pl.pallas_call(
        body, name=name,
        out_shape=(pltpu.SemaphoreType.DMA((n_sem,)), pltpu.SemaphoreType.DMA((n_sem,)), *src_t, *land_t,
                   jax.ShapeDtypeStruct((8, LANES), F32)),
        in_specs=[_HBM] * (2 * n),
        out_specs=(_SEM, _SEM, *[_HBM] * (2 * n), pl.BlockSpec(memory_space=pltpu.VMEM)),
        input_output_aliases={i: 2 + i for i in range(2 * n)},
        compiler_params=pltpu.CompilerParams(has_side_effects=_DATAFLOW))(
            *[pltpu.with_memory_space_constraint(v, pltpu.HBM) for v in arrays],
            *[pltpu.with_memory_space_constraint(lax.empty(t.shape, t.dtype), pltpu.HBM) for t in land_t])
    return outs[0], outs[1], outs[2:2 + n], outs[2 + n:2 + 2 * n], outs[-1]


def _split_wait(pattern, send_sems, recv_sems, srcs, lands, after, *, name):
    n = len(srcs)

    def body(*refs):
        for cp in _split_copies(pattern, refs[:n], refs[n:2 * n], refs[2 * n], refs[2 * n + 1]):
            cp.wait_send()
            cp.wait_recv()

    outs = pl.pallas_call(
        body, name=name, out_shape=[pltpu.HBM(v.shape, v.dtype) for v in (*srcs, *lands)],
        in_specs=[_HBM] * (2 * n) + [_SEM, _SEM, pl.BlockSpec(memory_space=pl.ANY)],
        out_specs=[_HBM] * (2 * n), input_output_aliases={i: i for i in range(2 * n)},
        compiler_params=pltpu.CompilerParams(has_side_effects=_DATAFLOW))(
            *srcs, *lands, send_sems, recv_sems, after)
    return outs[:n], outs[n:]


def _own_slab(lands, srcs, index, whole):
    out = []
    for land, src in zip(lands, srcs):
        own = src[None] if whole else lax.dynamic_slice(src, (index, 0, 0), (1,) + src.shape[1:])
        out.append(lax.dynamic_update_slice(land, own, (index, 0, 0)))
    return out


def _gather(arrays, *, name):
    n = len(arrays)
    per = N_DEV - 1

    def body(*refs):
        ins, outs = refs[:n], refs[n:2 * n]
        send_sems, recv_sems, local_sems = refs[2 * n:]
        x, y, c = lax.axis_index("x"), lax.axis_index("y"), lax.axis_index("c")
        me, sibling = (x, y, c), (x, y, 1 - c)
        chips = [(1 - x, y), (x, 1 - y), (1 - x, 1 - y)]

        def slab(a, block):
            return outs[a].at[4 * block[0] + 2 * block[1] + block[2]]

        def copy(a, k, block, to, src=None):
            rows = slab(a, block)
            return pltpu.make_async_remote_copy(
                src_ref=rows if src is None else src, dst_ref=rows,
                send_sem=send_sems.at[a * per + k], recv_sem=recv_sems.at[a * per + k],
                device_id=to, device_id_type=pl.DeviceIdType.MESH)

        started = []
        for a in range(n):
            mine = pltpu.make_async_copy(ins[a], slab(a, me), local_sems.at[a])
            mine.start()
            started.append(mine)
        for a in range(n):
            first = [copy(a, 1 + j, me, (*chip, c), src=ins[a]) for j, chip in enumerate(chips)]
            first.append(copy(a, 0, me, sibling, src=ins[a]))
            for cp in first:
                cp.start()
            started += first
        for j, chip in enumerate(chips):
            for a in range(n):
                copy(a, 1 + j, (*chip, c), me).wait_recv()
                passed = copy(a, 4 + j, (*chip, c), sibling)
                passed.start()
                started.append(passed)
        for a in range(n):
            copy(a, 0, sibling, me).wait_recv()
            for j, chip in enumerate(chips):
                copy(a, 4 + j, (*chip, 1 - c), me).wait_recv()
        for cp in started[:n]:
            cp.wait()
        for cp in started[n:]:
            cp.wait_send()

    return pl.pallas_call(
        body, in_specs=[pl.BlockSpec(memory_space=pl.ANY)] * n,
        out_specs=[pl.BlockSpec(memory_space=pl.ANY)] * n,
        out_shape=[jax.ShapeDtypeStruct((N_DEV,) + v.shape, v.dtype) for v in arrays],
        scratch_shapes=[pltpu.SemaphoreType.DMA((n * per,)), pltpu.SemaphoreType.DMA((n * per,)),
                        pltpu.SemaphoreType.DMA((n,))],
        compiler_params=pltpu.CompilerParams(has_side_effects=True), name=name)(*arrays)


def _adamw(parts, w, m, v, *, name, tc=None):
    r, c = w.shape
    n_parts = parts.shape[0]
    tc = c if tc is None else tc
    assert c % tc == 0

    def body(p_ref, w_ref, m_ref, v_ref, g_ref, d_ref, nm_ref, nv_ref):
        g = p_ref[0].astype(F32)
        for d in range(1, n_parts):
            g = g + p_ref[d].astype(F32)
        mm = ADAM_B1 * m_ref[...] + (1.0 - ADAM_B1) * g
        vv = ADAM_B2 * v_ref[...] + (1.0 - ADAM_B2) * (g * g)
        m_hat = mm / (1.0 - ADAM_B1 ** ADAM_STEP)
        v_hat = vv / (1.0 - ADAM_B2 ** ADAM_STEP)
        g_ref[...] = g
        d_ref[...] = -ADAM_LR * (m_hat / (jnp.sqrt(v_hat) + ADAM_EPS) + ADAM_WD * w_ref[...])
        nm_ref[...] = mm
        nv_ref[...] = vv

    spec = pl.BlockSpec((r, tc), lambda i: (0, i))
    out = jax.ShapeDtypeStruct((r, c), F32)
    return pl.pallas_call(
        body, grid=(c // tc,),
        in_specs=[pl.BlockSpec((n_parts, r, tc), lambda i: (0, 0, i)), spec, spec, spec],
        out_specs=[spec] * 4, out_shape=[out] * 4,
        compiler_params=_params(("parallel",)), name=name)(parts, w, m, v)


SMALL = (("norm_w", D_MODEL), ("conv_b", CONV_DIM), ("dt_bias", N_DT), ("a_log", N_DT),
         ("d_skip", N_DT), ("ssm_norm_w", SSD_WIDTH), ("final_norm_w", D_MODEL))
SMALL_USED = sum(size for _, size in SMALL)
SMALL_ROWS = 64
SHARD = D_PROJ // N_DEV


def _pack_small(vals, scalar=None):
    parts = [vals[k].reshape(-1).astype(F32) for k, _ in SMALL]
    if scalar is not None:
        parts.append(scalar.reshape(1))
    flat = jnp.concatenate(parts)
    return jnp.pad(flat, (0, SMALL_ROWS * LANES - flat.shape[0])).reshape(SMALL_ROWS, LANES)


def _unpack_small(packed, shapes):
    flat = packed.reshape(-1)
    out, off = {}, 0
    for k, size in SMALL:
        out[k] = flat[off:off + size].reshape(shapes[k])
        off += size
    return out


def kernel(x, norm_w, w_in, conv_w, conv_b, dt_bias, a_log, d_skip, ssm_norm_w, w_attn_out, w_ssm_out, w_o, final_norm_w, loss_target, m_norm_w, m_w_in, m_conv_w, m_conv_b, m_dt_bias, m_a_log, m_d_skip, m_ssm_norm_w, m_w_attn_out, m_w_ssm_out, m_w_o, m_final_norm_w, v_norm_w, v_w_in, v_conv_w, v_conv_b, v_dt_bias, v_a_log, v_d_skip, v_ssm_norm_w, v_w_attn_out, v_w_ssm_out, v_w_o, v_final_norm_w):
    nb, s, _ = x.shape
    t = nb * s
    weights = dict(norm_w=norm_w, w_in=w_in, conv_w=conv_w, conv_b=conv_b, dt_bias=dt_bias, a_log=a_log,
                   d_skip=d_skip, ssm_norm_w=ssm_norm_w, w_attn_out=w_attn_out, w_ssm_out=w_ssm_out,
                   w_o=w_o, final_norm_w=final_norm_w)
    moms = dict(norm_w=m_norm_w, w_in=m_w_in, conv_w=m_conv_w, conv_b=m_conv_b, dt_bias=m_dt_bias,
                a_log=m_a_log, d_skip=m_d_skip, ssm_norm_w=m_ssm_norm_w, w_attn_out=m_w_attn_out,
                w_ssm_out=m_w_ssm_out, w_o=m_w_o, final_norm_w=m_final_norm_w)
    vels = dict(norm_w=v_norm_w, w_in=v_w_in, conv_w=v_conv_w, conv_b=v_conv_b, dt_bias=v_dt_bias,
                a_log=v_a_log, d_skip=v_d_skip, ssm_norm_w=v_ssm_norm_w, w_attn_out=v_w_attn_out,
                w_ssm_out=v_w_ssm_out, w_o=v_w_o, final_norm_w=v_final_norm_w)

    tr_ = lambda a: jnp.transpose(a[0])
    g_in, g_conv = _gather([_bf(tr_(w_in)), conv_w[0]], name="gather_weights")
    conv_full = g_conv.transpose(1, 0, 2).reshape(CONV_K, CONV_DIM)
    o_send, o_recv, o_srcs, o_lands, o_zero = _split_start(
        "all", [_bf(w_attn_out[0]), _bf(w_ssm_out[0]), _bf(w_o[0])], name="gather_out_weights_start")

    core = lax.axis_index("c").astype(jnp.int32).reshape(1)
    chip = 2 * lax.axis_index("x") + lax.axis_index("y")
    me = 2 * chip + lax.axis_index("c")

    def out_weights(after):
        srcs, lands = _split_wait("all", o_send, o_recv, o_srcs, o_lands, after, name="gather_out_weights_wait")
        g_wa, g_ws, g_wo = _own_slab(lands, srcs, me, True)
        return (g_wa.reshape(SB_WIDTH, D_MODEL), g_ws.reshape(SSD_WIDTH, D_MODEL), g_wo.reshape(D_MODEL, D_MODEL))

    def send_weight_grads(grads):
        slabs = [grads["w_in_t"],
                 grads["conv_w"].reshape(CONV_K, N_DEV, CONV_DIM // N_DEV).transpose(1, 0, 2),
                 _bf(grads["w_attn_out"].reshape(N_DEV, SB_WIDTH // N_DEV, D_MODEL)),
                 _bf(grads["w_ssm_out"].reshape(N_DEV, SSD_WIDTH // N_DEV, D_MODEL)),
                 _bf(grads["w_o"].reshape(N_DEV, D_MODEL // N_DEV, D_MODEL))]
        recv = _pair_exchange(slabs, [], name="exchange_pairs")
        names = ("w_in", "conv_w", "w_attn_out", "w_ssm_out", "w_o")
        sums = [_pair_add(a, b, core, name="pair_add_" + k, tc=4 * LANES if k == "w_in" else None)
                for k, a, b in zip(names, slabs, recv)]
        send_sems, recv_sems, srcs, lands, zero = _split_start("chips", sums, name="exchange_chips_start")
        return zero, (send_sems, recv_sems, srcs, lands)

    loss_part, grad_x, grads, (send_sems, recv_sems, srcs, lands) = _local_step(
        x.reshape(t, D_MODEL), loss_target.reshape(t, D_MODEL), nb, s, g_in.reshape(D_PROJ, D_MODEL),
        conv_full, conv_b, norm_w + o_zero[0:1, 0:1], dt_bias, a_log, d_skip, ssm_norm_w, out_weights,
        final_norm_w.reshape(1, D_MODEL), between=send_weight_grads)

    (p_small,) = _pair_exchange([], [_pack_small(grads, loss_part[0, 0])], name="exchange_small")
    srcs, lands = _split_wait("chips", send_sems, recv_sems, srcs, lands, grad_x, name="exchange_chips_wait")
    p_in, p_conv, p_wa, p_ws, p_wo = _own_slab(lands, srcs, chip, False)

    res = {}
    res["w_in"] = [a.T for a in _adamw(p_in, tr_(w_in), tr_(m_w_in), tr_(v_w_in), name="adamw_w_in", tc=2 * LANES)]
    res["conv_w"] = _adamw(p_conv, conv_w[0], m_conv_w[0], v_conv_w[0], name="adamw_conv_w")
    res["w_attn_out"] = _adamw(p_wa, w_attn_out[0], m_w_attn_out[0], v_w_attn_out[0], name="adamw_w_attn_out")
    res["w_ssm_out"] = _adamw(p_ws, w_ssm_out[0], m_w_ssm_out[0], v_w_ssm_out[0], name="adamw_w_ssm_out")
    res["w_o"] = _adamw(p_wo, w_o[0], m_w_o[0], v_w_o[0], name="adamw_w_o")
    sm = _adamw(p_small, _pack_small(weights), _pack_small(moms), _pack_small(vels), name="adamw_small")
    loss = sm[0].reshape(-1)[SMALL_USED]
    shapes = {k: weights[k].shape for k, _ in SMALL}
    sm = [_unpack_small(v, shapes) for v in sm]
    for k, _ in SMALL:
        res[k] = tuple(part[k] for part in sm)

    order = ("norm_w", "w_in", "conv_w", "conv_b", "dt_bias", "a_log", "d_skip", "ssm_norm_w",
             "w_attn_out", "w_ssm_out", "w_o", "final_norm_w")
    outs = [loss, grad_x.reshape(nb, s, D_MODEL)]
    for i in range(4):
        outs += [res[k][i].reshape(weights[k].shape) for k in order]
    return tuple(outs)
```

```python
import functools

import jax
import jax.numpy as jnp
from jax import lax
from jax.experimental import pallas as pl
from jax.experimental.pallas import tpu as pltpu

F32 = jnp.float32
BF16 = jnp.bfloat16

D_MODEL = 1024
SB_WIDTH = 1024
HEAD_DIM = 64
SSD_WIDTH = 2048
SSD_GROUPS = 4
SSD_HPG = 8
SSD_STATE = 128
SSD_CHUNK = 128
CONV_K = 4
CONV_DIM = 3072
N_DT = 32
D_PROJ = 11296
DT_COL0 = 9216
GATE_COL0 = DT_COL0 + N_DT
EPS = 1e-6
N_DEV = 8

ADAM_LR = 0.001
ADAM_B1 = 0.9
ADAM_B2 = 0.999
ADAM_EPS = 1e-08
ADAM_WD = 0.01
ADAM_STEP = 10

LANES = 128
VMEM_LIMIT = 56 * 1024 * 1024

CB_Q, CB_K, CB_V, CB_ZA, CB_ZS, CB_XBC = 0, 8, 16, 24, 32, 48

NT_DIMS = (((1,), (1,)), ((), ()))
TN_DIMS = (((0,), (0,)), ((), ()))


def _params(sem):
    return pltpu.CompilerParams(dimension_semantics=sem, vmem_limit_bytes=VMEM_LIMIT)


def _dot(a, b):
    return jnp.dot(a, b, preferred_element_type=F32)


def _dot_nt(a, b):
    return lax.dot_general(a, b, NT_DIMS, preferred_element_type=F32)


def _dot_tn(a, b):
    return lax.dot_general(a, b, TN_DIMS, preferred_element_type=F32)


def _bf(a):
    return a.astype(BF16)


def _sigmoid(x):
    return 0.5 + 0.5 * jnp.tanh(0.5 * x)


def _softplus(x):
    return jnp.maximum(x, 0.0) + jnp.log(1.0 + jnp.exp(-jnp.abs(x)))


def _trunc_split(a):
    bits = lax.bitcast_convert_type(a, jnp.uint32) & jnp.uint32(0xFFFF0000)
    hi = lax.bitcast_convert_type(bits, F32)
    return _bf(hi), _bf(a - hi)


MM_TILE = 1024
MM_TK = 2048
MM_TALL = 2048


def _matmul(a, b, *, name, nt=False, ta=False, add=None, out_dtype=F32, tm=MM_TILE, tn=MM_TILE, tk=MM_TK, n=None):
    k, m = a.shape if ta else a.shape[::-1]
    n = (b.shape[0] if nt else b.shape[1]) if n is None else n
    tm, tn, tk = min(tm, m), min(tn, n), min(tk, k)
    assert m % tm == 0 and n % tn == 0 and k % tk == 0, (name, a.shape, b.shape)
    assert not (ta and nt)
    nk = k // tk
    has_add = add is not None

    def body(*refs):
        a_ref, b_ref = refs[:2]
        add_ref = refs[2] if has_add else None
        o_ref = refs[3 if has_add else 2]
        acc_ref = refs[-1] if nk > 1 else None
        kk = pl.program_id(2)

        def product():
            av, bv = _bf(a_ref[...]), _bf(b_ref[...])
            return _dot_nt(av, bv) if nt else _dot_tn(av, bv) if ta else _dot(av, bv)

        if nk == 1:
            d = product()
            o_ref[...] = ((d + add_ref[...]) if has_add else d).astype(out_dtype)
            return

        @pl.when(kk == 0)
        def _():
            acc_ref[...] = product()

        @pl.when((kk > 0) & (kk < nk - 1))
        def _():
            acc_ref[...] += product()

        @pl.when(kk == nk - 1)
        def _():
            r = acc_ref[...] + product()
            if has_add:
                r = r + add_ref[...]
            o_ref[...] = r.astype(out_dtype)

    in_specs = [pl.BlockSpec((tk, tm), lambda j, i, kk: (kk, i)) if ta
                else pl.BlockSpec((tm, tk), lambda j, i, kk: (i, kk)),
                pl.BlockSpec((tn, tk), lambda j, i, kk: (j, kk)) if nt
                else pl.BlockSpec((tk, tn), lambda j, i, kk: (kk, j))]
    args = [a, b]
    if has_add:
        in_specs.append(pl.BlockSpec((tm, tn), lambda j, i, kk: (i, j)))
        args.append(add)
    return pl.pallas_call(
        body, grid=(n // tn, m // tm, nk), in_specs=in_specs,
        out_specs=pl.BlockSpec((tm, tn), lambda j, i, kk: (i, j)),
        out_shape=jax.ShapeDtypeStruct((m, n), out_dtype),
        scratch_shapes=[pltpu.VMEM((tm, tn), F32)] if nk > 1 else [],
        compiler_params=_params(("parallel", "parallel", "arbitrary")), name=name)(*args)


def _rms_fwd(x, w, *, tm=512):
    t = x.shape[0]

    def body(x_ref, w_ref, h_ref, r_ref):
        xv = x_ref[...]
        r = lax.rsqrt(jnp.mean(xv * xv, axis=-1, keepdims=True) + EPS)
        h_ref[...] = _bf(xv * r * w_ref[...])
        r_ref[...] = r

    row = lambda i: (i, 0)
    return pl.pallas_call(
        body, grid=(t // tm,),
        in_specs=[pl.BlockSpec((tm, D_MODEL), row), pl.BlockSpec((1, D_MODEL), lambda i: (0, 0))],
        out_specs=[pl.BlockSpec((tm, D_MODEL), row), pl.BlockSpec((tm, 1), row)],
        out_shape=[jax.ShapeDtypeStruct((t, D_MODEL), BF16), jax.ShapeDtypeStruct((t, 1), F32)],
        compiler_params=_params(("parallel",)), name="rms_fwd")(x, w)


def _rms_bwd(dh, x, r, w, dres, *, tm=512):
    t = x.shape[0]

    def body(dh_ref, x_ref, r_ref, w_ref, dres_ref, dx_ref, dw_ref):
        @pl.when(pl.program_id(0) == 0)
        def _():
            dw_ref[...] = jnp.zeros_like(dw_ref)

        nrm = x_ref[...] * r_ref[...]
        dhv = dh_ref[...]
        dw_ref[...] += jnp.sum(dhv * nrm, axis=0, keepdims=True)
        dn = dhv * w_ref[...]
        dx = r_ref[...] * (dn - nrm * jnp.mean(dn * nrm, axis=-1, keepdims=True))
        dx_ref[...] = dx + dres_ref[...]

    row = lambda i: (i, 0)
    fix = lambda i: (0, 0)
    return pl.pallas_call(
        body, grid=(t // tm,),
        in_specs=[pl.BlockSpec((tm, D_MODEL), row), pl.BlockSpec((tm, D_MODEL), row),
                  pl.BlockSpec((tm, 1), row), pl.BlockSpec((1, D_MODEL), fix),
                  pl.BlockSpec((tm, D_MODEL), row)],
        out_specs=[pl.BlockSpec((tm, D_MODEL), row), pl.BlockSpec((1, D_MODEL), fix)],
        out_shape=[jax.ShapeDtypeStruct((t, D_MODEL), F32), jax.ShapeDtypeStruct((1, D_MODEL), F32)],
        compiler_params=_params(("arbitrary",)), name="rms_bwd")(dh, x, r, w, dres)


ATT_BLK = 256


def _attn_masks(blk):
    r_io = lax.broadcasted_iota(jnp.int32, (blk, blk), 0)
    c_io = lax.broadcasted_iota(jnp.int32, (blk, blk), 1)
    return r_io, c_io


def _attn_tiles(nq):
    return nq * (nq + 1) // 2


def _attn_fwd(proj, nb, s):
    blk = min(ATT_BLK, s)
    nq = s // blk
    nt = _attn_tiles(nq)
    scale = HEAD_DIM ** -0.5

    def body(q_ref, k_ref, v_ref, o_ref, a_out, sg_out, q0_s, q1_s, kb_s, v0_s, v1_s, z_s, a_s, acc_s, cr_s):
        head0 = lax.broadcasted_iota(jnp.int32, (s, LANES), 1) < HEAD_DIM
        r_io, c_io = _attn_masks(blk)
        tri = c_io < r_io
        tri2 = jnp.concatenate([tri, tri], axis=0)
        u_gt = _bf(r_io > c_io)
        u2 = jnp.concatenate([u_gt, u_gt], axis=0)
        qv = q_ref[...] * scale
        q0_s[...] = _bf(jnp.where(head0, qv, 0.0))
        q1_s[...] = _bf(jnp.where(head0, 0.0, qv))
        kb_s[...] = _bf(k_ref[...])
        vv = v_ref[...]
        v0_s[...] = _bf(jnp.where(head0, vv, 0.0))
        v1_s[...] = _bf(jnp.where(head0, 0.0, vv))

        def scores(qcat, kblk, slot):
            k0 = pl.multiple_of(jnp.maximum(kblk, 0) * blk, blk)
            z_s[slot] = _dot_nt(qcat, kb_s[pl.ds(k0, blk), :])

        def weights(slot, tile, diag, zslot=None):
            z = z_s[slot if zslot is None else zslot]
            cr = cr_s[...]
            sp = _softplus(z)
            spm = jnp.where(tri2, sp, 0.0) if diag else sp
            hi, lo = _trunc_split(spm)
            ps = _dot(jnp.concatenate([hi, lo], axis=1), u2)
            lb = z - sp
            a = jnp.exp(lb - (ps + cr))
            if diag:
                a = jnp.where(tri2, a, 0.0)
            ab = _bf(a)
            a_s[slot] = ab
            a_out[tile] = ab
            sg_out[tile] = _bf(jnp.exp(lb))
            cr_s[...] = cr + ps[:, 0:1] + spm[:, 0:1]

        def apply(slot, k0):
            a = a_s[slot]
            vcat = jnp.concatenate([v0_s[pl.ds(k0, blk), :], v1_s[pl.ds(k0, blk), :]], axis=0)
            acc_s[...] += _dot(jnp.concatenate([a[:blk], a[blk:]], axis=1), vcat)

        def queries(qi):
            q0 = pl.multiple_of(qi * blk, blk)
            return jnp.concatenate([q0_s[pl.ds(q0, blk), :], q1_s[pl.ds(q0, blk), :]], axis=0)

        def first_scores(qi):
            qcat = queries(qi)
            scores(qcat, qi, 2)
            scores(qcat, qi - 1, 3)

        first_scores(jnp.int32(0))

        def qblock(qi, _):
            q0 = pl.multiple_of(qi * blk, blk)
            tile0 = qi * (qi + 1) // 2
            qcat = queries(qi)
            z_s[1] = z_s[3]
            cr_s[...] = jnp.zeros_like(cr_s)
            acc_s[...] = jnp.zeros_like(acc_s)
            weights(0, tile0 + qi, True, zslot=2)
            first_scores(jnp.minimum(qi + 1, nq - 1))

            def step(j, slot):
                scores(qcat, qi - j - 1, 1 - slot)
                weights(slot, tile0 + qi - j, False)
                apply(1 - slot, pl.multiple_of((qi - j + 1) * blk, blk))

            def pair(i, _):
                step(2 * i + 1, 1)
                step(2 * i + 2, 0)
                return 0

            lax.fori_loop(0, qi // 2, pair, 0)

            @pl.when(qi % 2 == 1)
            def _():
                step(qi, 1)

            apply(qi & 1, 0)
            o_ref[pl.ds(q0, blk), :] = acc_s[...]
            return 0

        lax.fori_loop(0, nq, qblock, 0)

    t = nb * s
    tiles = pl.BlockSpec((None, None, nt, 2 * blk, blk), lambda b, p: (b, p, 0, 0, 0))
    tiles_shape = jax.ShapeDtypeStruct((nb, 8, nt, 2 * blk, blk), BF16)
    return pl.pallas_call(
        body, grid=(nb, 8),
        in_specs=[pl.BlockSpec((s, LANES), lambda b, p: (b, CB_Q + p)),
                  pl.BlockSpec((s, LANES), lambda b, p: (b, CB_K + p)),
                  pl.BlockSpec((s, LANES), lambda b, p: (b, CB_V + p))],
        out_specs=[pl.BlockSpec((s, LANES), lambda b, p: (b, p)), tiles, tiles],
        out_shape=[jax.ShapeDtypeStruct((t, SB_WIDTH), F32), tiles_shape, tiles_shape],
        scratch_shapes=[pltpu.VMEM((s, LANES), BF16)] * 5 + [
            pltpu.VMEM((4, 2 * blk, blk), F32), pltpu.VMEM((2, 2 * blk, blk), BF16),
            pltpu.VMEM((blk, LANES), F32), pltpu.VMEM((2 * blk, 1), F32)],
        compiler_params=_params(("parallel", "parallel")), name="attn_fwd")(proj, proj, proj)


def _attn_bwd(proj, do, a_all, sg_all, nb, s):
    blk = min(ATT_BLK, s)
    nq = s // blk
    nt = _attn_tiles(nq)
    scale = HEAD_DIM ** -0.5

    def body(q_ref, k_ref, v_ref, do_ref, a_in, sg_in, dq_ref, dk_ref, dv_ref,
             qt0_s, qt1_s, k0_s, k1_s, vb_s, d0_s, d1_s, dt0_s, dt1_s, da_s, dz_s, dkt_acc, dvt_acc, dq_acc, cw_s):
        head0 = lax.broadcasted_iota(jnp.int32, (s, LANES), 1) < HEAD_DIM
        r_io, c_io = _attn_masks(blk)
        tri = c_io < r_io
        tri2 = jnp.concatenate([tri, tri], axis=0)
        u_lt = _bf(r_io < c_io)
        head0_t = lax.broadcasted_iota(jnp.int32, (LANES, s), 0) < HEAD_DIM
        q_t = (q_ref[...].astype(F32) * scale).T
        qt0_s[...] = _bf(jnp.where(head0_t, q_t, 0.0))
        qt1_s[...] = _bf(jnp.where(head0_t, 0.0, q_t))
        kv = k_ref[...]
        k0_s[...] = _bf(jnp.where(head0, kv, 0.0))
        k1_s[...] = _bf(jnp.where(head0, 0.0, kv))
        vb_s[...] = _bf(v_ref[...])
        dov = do_ref[...]
        d0_s[...] = _bf(jnp.where(head0, dov, 0.0))
        d1_s[...] = _bf(jnp.where(head0, 0.0, dov))
        do_t = dov.astype(F32).T
        dt0_s[...] = _bf(jnp.where(head0_t, do_t, 0.0))
        dt1_s[...] = _bf(jnp.where(head0_t, 0.0, do_t))
        dkt_acc[...] = jnp.zeros_like(dkt_acc)
        dvt_acc[...] = jnp.zeros_like(dvt_acc)

        def qblock(qi, _):
            q0 = pl.multiple_of(qi * blk, blk)
            tile0 = qi * (qi + 1) // 2
            qs_ = pl.ds(q0, blk)
            dcat = jnp.concatenate([d0_s[qs_, :], d1_s[qs_, :]], axis=0)
            qcat_t = jnp.concatenate([qt0_s[:, qs_], qt1_s[:, qs_]], axis=1)
            dcat_t = jnp.concatenate([dt0_s[:, qs_], dt1_s[:, qs_]], axis=1)

            def scores(kblk, slot):
                da_s[slot] = _dot_nt(dcat, vb_s[pl.ds(pl.multiple_of(kblk * blk, blk), blk), :])

            def grads(slot, kblk, diag):
                cw = cw_s[...]
                w = a_in[tile0 + kblk].astype(F32) * da_s[slot]
                wex = _dot(_bf(w), u_lt)
                dz = w - sg_in[tile0 + kblk].astype(F32) * (w + (wex + cw))
                if diag:
                    dz = jnp.where(tri2, dz, 0.0)
                dz_s[slot] = _bf(dz)
                cw_s[...] = cw + wex[:, blk - 1:blk] + w[:, blk - 1:blk]

            def apply(slot, kblk):
                ks_ = pl.ds(pl.multiple_of(kblk * blk, blk), blk)
                dz = dz_s[slot]
                dq_acc[...] += _dot(jnp.concatenate([dz[:blk], dz[blk:]], axis=1),
                                    jnp.concatenate([k0_s[ks_, :], k1_s[ks_, :]], axis=0))
                dkt_acc[:, ks_] += _dot(qcat_t, dz)
                dvt_acc[:, ks_] += _dot(dcat_t, a_in[tile0 + kblk])

            def step(j, slot, diag, first=False):
                if not diag:
                    scores(j + 1, 1 - slot)
                grads(slot, j, diag)
                if not first:
                    apply(1 - slot, j - 1)

            scores(0, 0)
            dq_acc[...] = jnp.zeros_like(dq_acc)
            cw_s[...] = jnp.zeros_like(cw_s)

            @pl.when(qi == 0)
            def _():
                step(0, 0, True, first=True)

            @pl.when(qi > 0)
            def _():
                step(0, 0, False, first=True)

            def pair(i, _):
                step(2 * i + 1, 1, False)
                step(2 * i + 2, 0, False)
                return 0

            lax.fori_loop(0, (qi - 1) // 2, pair, 0)

            @pl.when((qi > 0) & (qi % 2 == 0))
            def _():
                step(qi - 1, 1, False)
                step(qi, 0, True)

            @pl.when(qi % 2 == 1)
            def _():
                step(qi, 1, True)

            apply(qi & 1, qi)
            dq_ref[qs_, :] = _bf(dq_acc[...] * scale)
            return 0

        lax.fori_loop(0, nq, qblock, 0)
        dk_ref[...] = _bf(dkt_acc[...].T)
        dv_ref[...] = _bf(dvt_acc[...].T)

    t = nb * s
    hp = lambda b, p: (b, p)
    out = jax.ShapeDtypeStruct((t, SB_WIDTH), BF16)
    tiles = pl.BlockSpec((None, None, nt, 2 * blk, blk), lambda b, p: (b, p, 0, 0, 0))
    return pl.pallas_call(
        body, grid=(nb, 8),
        in_specs=[pl.BlockSpec((s, LANES), lambda b, p: (b, CB_Q + p)),
                  pl.BlockSpec((s, LANES), lambda b, p: (b, CB_K + p)),
                  pl.BlockSpec((s, LANES), lambda b, p: (b, CB_V + p)),
                  pl.BlockSpec((s, LANES), hp), tiles, tiles],
        out_specs=[pl.BlockSpec((s, LANES), hp)] * 3,
        out_shape=[out, out, out],
        scratch_shapes=[pltpu.VMEM((LANES, s), BF16)] * 2 + [pltpu.VMEM((s, LANES), BF16)] * 5 + [
            pltpu.VMEM((LANES, s), BF16)] * 2 + [
            pltpu.VMEM((2, 2 * blk, blk), F32), pltpu.VMEM((2, 2 * blk, blk), BF16),
            pltpu.VMEM((LANES, s), F32), pltpu.VMEM((LANES, s), F32),
            pltpu.VMEM((blk, LANES), F32), pltpu.VMEM((2 * blk, 1), F32)],
        compiler_params=_params(("parallel", "parallel")), name="attn_bwd")(proj, proj, proj, do, a_all, sg_all)


CONV_TC = 256
CONV_ROWS = 64
HALO = 8


def _conv_rows(ws, w_ref, b_ref, n):
    taps = [ws[HALO - kk:HALO - kk + n, :] for kk in range(CONV_K)]
    y = b_ref[...] + w_ref[CONV_K - 1:CONV_K, :] * taps[0]
    for kk in range(1, CONV_K):
        y = y + w_ref[CONV_K - 1 - kk:CONV_K - kk, :] * taps[kk]
    return y, taps


def _conv_fwd(proj, conv_w, conv_b, nb, s):
    tc, rows = CONV_TC, min(CONV_ROWS, s)
    cb0 = CB_XBC * LANES // tc

    def body(x_ref, w_ref, b_ref, o_ref, xp, ws):
        xp[0:HALO, :] = jnp.zeros((HALO, tc), F32)
        xp[HALO:HALO + s, :] = x_ref[...].astype(F32)

        def chunk(i, _):
            r0 = pl.multiple_of(i * rows, rows)
            ws[...] = xp[pl.ds(r0, rows + HALO), :]
            y, _ = _conv_rows(ws, w_ref, b_ref, rows)
            o_ref[pl.ds(r0, rows), :] = y * _sigmoid(y)
            return 0

        lax.fori_loop(0, s // rows, chunk, 0)

    t = nb * s
    return pl.pallas_call(
        body, grid=(nb, CONV_DIM // tc),
        in_specs=[pl.BlockSpec((s, tc), lambda b, c: (b, cb0 + c)),
                  pl.BlockSpec((CONV_K, tc), lambda b, c: (0, c)),
                  pl.BlockSpec((1, tc), lambda b, c: (0, c))],
        out_specs=pl.BlockSpec((s, tc), lambda b, c: (b, c)),
        out_shape=jax.ShapeDtypeStruct((t, CONV_DIM), F32),
        scratch_shapes=[pltpu.VMEM((s + HALO, tc), F32), pltpu.VMEM((rows + HALO, tc), F32)],
        compiler_params=_params(("parallel", "parallel")), name="conv_fwd")(proj, conv_w, conv_b)


def _conv_bwd(proj, dact, conv_w, conv_b, dproj, nb, s, *, col0, name):
    width = dact.shape[1]
    tc, rows = min(CONV_TC, width), min(CONV_ROWS, s)
    cb0 = (CB_XBC * LANES + col0) // tc
    wb0 = col0 // tc

    def body(x_ref, d_ref, w_ref, b_ref, _, dx_ref, dw_ref, db_ref, xp, dp, dys, ws):
        @pl.when(pl.program_id(1) == 0)
        def _():
            dw_ref[...] = jnp.zeros_like(dw_ref)
            db_ref[...] = jnp.zeros_like(db_ref)

        n = rows + HALO
        pad = jnp.zeros((HALO, tc), F32)
        xp[0:HALO, :] = pad
        xp[HALO:HALO + s, :] = x_ref[...].astype(F32)
        xp[HALO + s:HALO + s + HALO, :] = pad
        dp[0:s, :] = d_ref[...]
        dp[s:s + HALO, :] = pad

        def chunk(i, acc):
            r0 = pl.multiple_of(i * rows, rows)
            ws[...] = xp[pl.ds(r0, n + HALO), :]
            y, taps = _conv_rows(ws, w_ref, b_ref, n)
            sg = _sigmoid(y)
            dy = dp[pl.ds(r0, n), :] * (sg * (1.0 + y * (1.0 - sg)))
            dys[...] = dy
            dx = w_ref[CONV_K - 1:CONV_K, :] * dy[0:rows]
            for kk in range(1, CONV_K):
                dx = dx + w_ref[CONV_K - 1 - kk:CONV_K - kk, :] * dys[kk:kk + rows, :]
            dx_ref[pl.ds(r0, rows), :] = _bf(dx)
            dyc = dy[0:rows]
            new = [acc[0] + jnp.sum(dyc, axis=0, keepdims=True)]
            for kk in range(CONV_K):
                new.append(acc[1 + kk] + jnp.sum(dyc * taps[kk][0:rows], axis=0, keepdims=True))
            return tuple(new)

        acc = lax.fori_loop(0, s // rows, chunk, (jnp.zeros((1, tc), F32),) * (1 + CONV_K))
        db_ref[...] += acc[0]
        for kk in range(CONV_K):
            dw_ref[CONV_K - 1 - kk:CONV_K - kk, :] += acc[1 + kk]

    t = nb * s
    return pl.pallas_call(
        body, grid=(width // tc, nb),
        in_specs=[pl.BlockSpec((s, tc), lambda c, b: (b, cb0 + c)),
                  pl.BlockSpec((s, tc), lambda c, b: (b, c)),
                  pl.BlockSpec((CONV_K, tc), lambda c, b: (0, wb0 + c)),
                  pl.BlockSpec((1, tc), lambda c, b: (0, wb0 + c)), pl.BlockSpec(memory_space=pl.ANY)],
        out_specs=[pl.BlockSpec((s, tc), lambda c, b: (b, cb0 + c)),
                   pl.BlockSpec((CONV_K, tc), lambda c, b: (0, c)),
                   pl.BlockSpec((1, tc), lambda c, b: (0, c))],
        out_shape=[jax.ShapeDtypeStruct(dproj.shape, dproj.dtype),
                   jax.ShapeDtypeStruct((CONV_K, width), F32),
                   jax.ShapeDtypeStruct((1, width), F32)],
        input_output_aliases={4: 0},
        scratch_shapes=[pltpu.VMEM((s + 2 * HALO, tc), F32), pltpu.VMEM((s + HALO, tc), F32),
                        pltpu.VMEM((rows + HALO, tc), F32), pltpu.VMEM((rows + 2 * HALO, tc), F32)],
        compiler_params=_params(("parallel", "arbitrary")), name=name)(proj, dact, conv_w, conv_b, dproj)


L = SSD_CHUNK
N_PAIR = SSD_HPG // 2


def _pair_lanes(v0, v1, lane):
    return jnp.where(lane < HEAD_DIM, v0, v1)


def _onehot(shape, row_shift, col_shift):
    row = lax.broadcasted_iota(jnp.int32, shape, 0)
    col = lax.broadcasted_iota(jnp.int32, shape, 1)
    return _bf(lax.shift_right_logical(row, row_shift) == lax.shift_right_logical(col, col_shift))


def _ssd_tables():
    sp = _onehot((LANES, N_PAIR * LANES), 0, 6)
    sf = _onehot((LANES, SSD_HPG * LANES), 0, 7)
    return (jnp.concatenate([sp] * 3, axis=0), jnp.concatenate([sf] * 3, axis=0),
            _onehot((N_PAIR * LANES, LANES), 6, 0), _onehot((SSD_HPG * LANES, LANES), 7, 0))


def _split3(a):
    a1 = _bf(a)
    r1 = a - a1.astype(F32)
    a2 = _bf(r1)
    return jnp.concatenate([a1, a2, _bf(r1 - a2.astype(F32))], axis=1)


def _split2(a):
    hi, lo = _trunc_split(a)
    return jnp.concatenate([hi, lo], axis=1)


def _tri_dot(tri_bf, a):
    a1 = _bf(a)
    r1 = a - a1.astype(F32)
    a2 = _bf(r1)
    parts = jnp.concatenate([a1, a2, _bf(r1 - a2.astype(F32))], axis=0)
    return _dot(jnp.concatenate([tri_bf] * 3, axis=1), parts)


HEAD_W = 4 * LANES + 2 * N_PAIR * LANES + SSD_HPG * LANES


def _ssd_head(dtr_ref, bias_ref, a_neg, c, tri_bf, spread_pair, spread_full, head_s, slot):
    raw = dtr_ref[pl.ds(pl.multiple_of(c * L, L), L), :] + bias_ref[...]
    dt = _softplus(raw)
    acs = _tri_dot(tri_bf, dt * a_neg)
    acs3 = _split3(acs)
    both = _dot(jnp.concatenate([_split3(dt), acs3], axis=0), spread_pair)
    parts = (raw, dt, acs, acs.T, both[:L], both[L:], _dot(acs3, spread_full))
    off = 0
    for p in parts:
        head_s[slot, :, off:off + p.shape[1]] = p
        off += p.shape[1]


def _ssd_head_load(head_s, slot):
    out, off = [], 0
    for w in (LANES, LANES, LANES, L, N_PAIR * LANES, N_PAIR * LANES, SSD_HPG * LANES):
        out.append(head_s[slot, :, off:off + w])
        off += w
    return out


def _ssd_fwd(xbc, dtr, bias, alog, dskip, nb, s):
    nc = s // L
    t = nb * s

    def body(x_ref, b_ref, c_ref, dtr_ref, bias_ref, alog_ref, dsk_ref, y_ref, hs_ref, h_scr, head_s):
        lane = lax.broadcasted_iota(jnp.int32, (L, LANES), 1)
        lane1 = lax.broadcasted_iota(jnp.int32, (1, LANES), 1)
        rowc = lax.broadcasted_iota(jnp.int32, (L, 1), 0)
        rr = lax.broadcasted_iota(jnp.int32, (L, L), 0)
        cc = lax.broadcasted_iota(jnp.int32, (L, L), 1)
        causal = rr >= cc
        h_scr[...] = jnp.zeros_like(h_scr)
        dsk = dsk_ref[...]
        spread_pair, spread_full, _, _ = _ssd_tables()
        a_neg = -jnp.exp(alog_ref[...])
        head = functools.partial(_ssd_head, dtr_ref, bias_ref, a_neg, tri_bf=_bf(cc <= rr),
                                 spread_pair=spread_pair, spread_full=spread_full, head_s=head_s)
        head(jnp.int32(0), slot=0)

        def chunk(c, slot):
            r0 = pl.multiple_of(c * L, L)
            head(jnp.minimum(c + 1, nc - 1), slot=1 - slot)
            _, _, acs, acs_t, dt_all, acs_all, acs_full = _ssd_head_load(head_s, slot)
            bb = _bf(b_ref[pl.ds(r0, L), :])
            cb = _bf(c_ref[pl.ds(r0, L), :])
            gm = _dot_nt(cb, bb)
            for pr in range(N_PAIR):
                j0, j1 = 2 * pr, 2 * pr + 1
                ms = []
                for j in (j0, j1):
                    seg = acs_full[:, j * LANES:(j + 1) * LANES] - acs_t[j:j + 1, :]
                    ms.append(_bf(gm * jnp.exp(jnp.where(causal, seg, -1e30))))
                x = x_ref[pl.ds(r0, L), pr * LANES:(pr + 1) * LANES]
                dt_p = dt_all[:, pr * LANES:(pr + 1) * LANES]
                acs_p = acs_all[:, pr * LANES:(pr + 1) * LANES]
                last_p = acs_p[L - 1:L, :]
                d_p = _pair_lanes(dsk[:, j0:j0 + 1], dsk[:, j1:j1 + 1], lane1)
                xd = x * dt_p
                xdb = _bf(xd)
                yds = _dot(jnp.concatenate(ms, axis=0), xdb)
                yd = _pair_lanes(yds[:L], yds[L:], lane)
                hp = h_scr[pr]
                yo = _dot_nt(cb, _bf(hp)) * jnp.exp(acs_p)
                y_ref[pl.ds(r0, L), pr * LANES:(pr + 1) * LANES] = yd + yo + x * d_p
                hs_ref[c, pr] = hp
                sp = _dot_tn(_bf(xd * jnp.exp(last_p - acs_p)), bb)
                cd = jnp.where(rowc < HEAD_DIM, jnp.exp(acs[L - 1:L, j0:j0 + 1]),
                               jnp.exp(acs[L - 1:L, j1:j1 + 1]))
                h_scr[pr] = hp * cd + sp

        def two_chunks(i, _):
            chunk(2 * i, 0)
            chunk(2 * i + 1, 1)
            return 0

        lax.fori_loop(0, nc // 2, two_chunks, 0)

    assert nc % 2 == 0
    grp = lambda b, g: (g, 0, 0)
    return pl.pallas_call(
        body, grid=(nb, SSD_GROUPS),
        in_specs=[pl.BlockSpec((s, 4 * LANES), lambda b, g: (b, g)),
                  pl.BlockSpec((s, LANES), lambda b, g: (b, 16 + g)),
                  pl.BlockSpec((s, LANES), lambda b, g: (b, 20 + g)),
                  pl.BlockSpec((None, s, LANES), lambda b, g: (g, b, 0)),
                  pl.BlockSpec((None, 1, LANES), grp), pl.BlockSpec((None, 1, LANES), grp),
                  pl.BlockSpec((None, 1, LANES), grp)],
        out_specs=[pl.BlockSpec((s, 4 * LANES), lambda b, g: (b, g)),
                   pl.BlockSpec((None, None, nc, N_PAIR, L, LANES), lambda b, g: (b, g, 0, 0, 0, 0))],
        out_shape=[jax.ShapeDtypeStruct((t, SSD_WIDTH), F32),
                   jax.ShapeDtypeStruct((nb, SSD_GROUPS, nc, N_PAIR, L, LANES), F32)],
        scratch_shapes=[pltpu.VMEM((N_PAIR, L, LANES), F32), pltpu.VMEM((2, L, HEAD_W), F32)],
        compiler_params=_params(("parallel", "parallel")), name="ssd_fwd")(
            xbc, xbc, xbc, dtr, bias, alog, dskip)


def _ssd_bwd(xbc, dtr, bias, alog, dskip, hs, dy, nb, s):
    nc = s // L
    t = nb * s

    def body(x_ref, b_ref, c_ref, dtr_ref, bias_ref, alog_ref, dsk_ref, hs_ref, dy_ref,
             dx_ref, db_ref, dc_ref, ddtr_ref, dbias_ref, dalog_ref, ddsk_ref, dh_scr, head_s):
        lane = lax.broadcasted_iota(jnp.int32, (L, LANES), 1)
        lane1 = lax.broadcasted_iota(jnp.int32, (1, LANES), 1)
        rowc = lax.broadcasted_iota(jnp.int32, (L, 1), 0)
        rr = lax.broadcasted_iota(jnp.int32, (L, L), 0)
        cc = lax.broadcasted_iota(jnp.int32, (L, L), 1)
        causal = rr >= cc
        tri_t = _bf(cc >= rr)
        dh_scr[...] = jnp.zeros_like(dh_scr)
        dsk = dsk_ref[...]
        a_neg = -jnp.exp(alog_ref[...])

        @pl.when(pl.program_id(1) == 0)
        def _():
            dbias_ref[...] = jnp.zeros_like(dbias_ref)
            dalog_ref[...] = jnp.zeros_like(dalog_ref)
            ddsk_ref[...] = jnp.zeros_like(ddsk_ref)

        spread_pair, spread_full, sum_pair, sum_full = _ssd_tables()
        sum_pair2 = jnp.concatenate([sum_pair] * 2, axis=0)
        sum_full2 = jnp.concatenate([sum_full] * 2, axis=0)
        head = functools.partial(_ssd_head, dtr_ref, bias_ref, a_neg, tri_bf=_bf(cc <= rr),
                                 spread_pair=spread_pair, spread_full=spread_full, head_s=head_s)
        head(jnp.int32(nc - 1), slot=0)

        def chunk(c, slot):
            r0 = pl.multiple_of(c * L, L)
            head(jnp.maximum(c - 1, 0), slot=1 - slot)
            raw, dt, acs, acs_t, dt_all, acs_all, acs_full = _ssd_head_load(head_s, slot)
            bb = _bf(b_ref[pl.ds(r0, L), :])
            cb = _bf(c_ref[pl.ds(r0, L), :])
            gm = _dot_nt(cb, bb)
            dg = jnp.zeros((L, L), F32)
            dbacc = jnp.zeros((L, LANES), F32)
            dcacc = jnp.zeros((L, LANES), F32)
            dacs_t = jnp.zeros((L, L), F32)
            ta_all, te_all, qm_all, tc_all, td_all, dcd_all = [], [], [], [], [], []
            for pr in range(N_PAIR):
                js = (2 * pr, 2 * pr + 1)
                lms, mfs = [], []
                for j in js:
                    seg = acs_full[:, j * LANES:(j + 1) * LANES] - acs_t[j:j + 1, :]
                    lm = jnp.exp(jnp.where(causal, seg, -1e30))
                    lms.append(lm)
                    mfs.append(gm * lm)
                x = x_ref[pl.ds(r0, L), pr * LANES:(pr + 1) * LANES]
                dyv = dy_ref[pl.ds(r0, L), pr * LANES:(pr + 1) * LANES]
                dt_p = dt_all[:, pr * LANES:(pr + 1) * LANES]
                acs_p = acs_all[:, pr * LANES:(pr + 1) * LANES]
                last_p = acs_p[L - 1:L, :]
                d_p = _pair_lanes(dsk[:, js[0]:js[0] + 1], dsk[:, js[1]:js[1] + 1], lane1)
                e_p = jnp.exp(acs_p)
                dte_p = jnp.exp(last_p - acs_p)
                xd = x * dt_p
                xdb = _bf(xd)
                hp = hs_ref[c, pr]
                hb = _bf(hp)
                dye = _bf(dyv * e_p)
                yo = _dot_nt(cb, hb) * e_p
                dcacc = dcacc + _dot(dye, hb)
                dh_out = _dot_tn(dye, cb)
                dyb = _bf(dyv)
                dms = _dot_nt(_bf(jnp.concatenate(
                    [jnp.where(lane < HEAD_DIM, dyv, 0.0), jnp.where(lane < HEAD_DIM, 0.0, dyv)], axis=0)), xdb)
                mty = _dot_tn(_bf(jnp.concatenate(mfs, axis=1)), dyb)
                dxd = jnp.where(lane < HEAD_DIM, mty[:L], mty[L:])
                for hh in range(2):
                    dm = dms[hh * L:(hh + 1) * L]
                    dg = dg + dm * lms[hh]
                    qm = dm * mfs[hh]
                    qm_all.append(qm)
                    dacs_t = dacs_t - jnp.where(rr == js[hh], jnp.sum(qm, axis=0, keepdims=True), 0.0)
                ds = dh_scr[pr]
                dsb = _bf(ds)
                xdd = xd * dte_p
                dxdd = _dot_nt(bb, dsb)
                dbacc = dbacc + _dot(_bf(xdd), dsb)
                dxd = dxd + dxdd * dte_p
                cd = jnp.where(rowc < HEAD_DIM, jnp.exp(acs[L - 1:L, js[0]:js[0] + 1]),
                               jnp.exp(acs[L - 1:L, js[1]:js[1] + 1]))
                prod = ds * hp
                dcd_all += [jnp.sum(prod[:HEAD_DIM], axis=0, keepdims=True),
                            jnp.sum(prod[HEAD_DIM:], axis=0, keepdims=True)]
                dh_scr[pr] = dh_out + ds * cd
                tcv = dxdd * xdd
                ta_all.append(dyv * yo - tcv)
                tc_all.append(jnp.sum(tcv, axis=0, keepdims=True))
                te_all.append(dxd * x)
                td_all.append(jnp.sum(dyv * x, axis=0, keepdims=True))
                dx_ref[pl.ds(r0, L), pr * LANES:(pr + 1) * LANES] = dxd * dt_p + dyv * d_p
            dgb = _bf(dg)
            dc_ref[pl.ds(r0, L), :] = dcacc + _dot(dgb, bb)
            db_ref[pl.ds(r0, L), :] = dbacc + _dot_tn(dgb, cb)
            last_add = jnp.zeros((1, LANES), F32)
            dd_add = jnp.zeros((1, LANES), F32)
            cd_row = jnp.exp(acs[L - 1:L, :])
            for j in range(SSD_HPG):
                hm = (lane1 >= HEAD_DIM) if j % 2 else (lane1 < HEAD_DIM)
                tcj = jnp.sum(jnp.where(hm, tc_all[j // 2], 0.0), axis=1, keepdims=True)
                tdj = jnp.sum(jnp.where(hm, td_all[j // 2], 0.0), axis=1, keepdims=True)
                dcdj = jnp.sum(dcd_all[j], axis=1, keepdims=True)
                last_add = last_add + jnp.where(lane1 == j, tcj + dcdj * cd_row, 0.0)
                dd_add = dd_add + jnp.where(lane1 == j, tdj, 0.0)
            dacs = (_dot(_split2(jnp.concatenate(ta_all, axis=1)), sum_pair2)
                    + _dot(_split2(jnp.concatenate(qm_all, axis=1)), sum_full2)
                    + dacs_t.T + jnp.where(rowc == L - 1, last_add, 0.0))
            ddta = _tri_dot(tri_t, dacs)
            ddt = _dot(_bf(jnp.concatenate(te_all, axis=1)), sum_pair) + ddta * a_neg
            ddraw = ddt * _sigmoid(raw)
            ddtr_ref[pl.ds(r0, L), :] = ddraw
            dbias_ref[...] += jnp.sum(ddraw, axis=0, keepdims=True)
            dalog_ref[...] += jnp.sum(ddta * dt, axis=0, keepdims=True) * a_neg
            ddsk_ref[...] += dd_add

        def two_chunks(i, _):
            chunk(nc - 1 - 2 * i, 0)
            chunk(nc - 2 - 2 * i, 1)
            return 0

        lax.fori_loop(0, nc // 2, two_chunks, 0)

    assert nc % 2 == 0
    grp = lambda g, b: (g, 0, 0)
    small = jax.ShapeDtypeStruct((SSD_GROUPS, 1, LANES), F32)
    return pl.pallas_call(
        body, grid=(SSD_GROUPS, nb),
        in_specs=[pl.BlockSpec((s, 4 * LANES), lambda g, b: (b, g)),
                  pl.BlockSpec((s, LANES), lambda g, b: (b, 16 + g)),
                  pl.BlockSpec((s, LANES), lambda g, b: (b, 20 + g)),
                  pl.BlockSpec((None, s, LANES), lambda g, b: (g, b, 0)),
                  pl.BlockSpec((None, 1, LANES), grp), pl.BlockSpec((None, 1, LANES), grp),
                  pl.BlockSpec((None, 1, LANES), grp),
                  pl.BlockSpec((None, None, nc, N_PAIR, L, LANES), lambda g, b: (b, g, 0, 0, 0, 0)),
                  pl.BlockSpec((s, 4 * LANES), lambda g, b: (b, g))],
        out_specs=[pl.BlockSpec((s, 4 * LANES), lambda g, b: (b, g)),
                   pl.BlockSpec((s, LANES), lambda g, b: (b, g)),
                   pl.BlockSpec((s, LANES), lambda g, b: (b, g)),
                   pl.BlockSpec((None, s, LANES), lambda g, b: (g, b, 0)),
                   pl.BlockSpec((None, 1, LANES), grp), pl.BlockSpec((None, 1, LANES), grp),
                   pl.BlockSpec((None, 1, LANES), grp)],
        out_shape=[jax.ShapeDtypeStruct((t, SSD_WIDTH), F32),
                   jax.ShapeDtypeStruct((t, SSD_GROUPS * SSD_STATE), F32),
                   jax.ShapeDtypeStruct((t, SSD_GROUPS * SSD_STATE), F32),
                   jax.ShapeDtypeStruct((SSD_GROUPS, t, LANES), F32), small, small, small],
        scratch_shapes=[pltpu.VMEM((N_PAIR, L, LANES), F32), pltpu.VMEM((2, L, HEAD_W), F32)],
        compiler_params=_params(("parallel", "arbitrary")), name="ssd_bwd")(
            xbc, xbc, xbc, dtr, bias, alog, dskip, hs, dy)


GN = SSD_WIDTH // SSD_GROUPS


def _post_fwd(o, proj, y, nw, *, tm=256):
    t = o.shape[0]

    def body(o_ref, za_ref, y_ref, zs_ref, nw_ref, ya_ref, ys_ref):
        za = za_ref[...].astype(F32)
        ya_ref[...] = _bf(o_ref[...] * (za * _sigmoid(za)))
        zs = zs_ref[...].astype(F32)
        yz = y_ref[...] * (zs * _sigmoid(zs))
        for g in range(SSD_GROUPS):
            sl = slice(g * GN, (g + 1) * GN)
            v = yz[:, sl]
            r = lax.rsqrt(jnp.mean(v * v, axis=-1, keepdims=True) + EPS)
            ys_ref[:, sl] = _bf(v * r * nw_ref[:, sl])

    return pl.pallas_call(
        body, grid=(t // tm,),
        in_specs=[pl.BlockSpec((tm, SB_WIDTH), lambda i: (i, 0)),
                  pl.BlockSpec((tm, SB_WIDTH), lambda i: (i, CB_ZA * LANES // SB_WIDTH)),
                  pl.BlockSpec((tm, SSD_WIDTH), lambda i: (i, 0)),
                  pl.BlockSpec((tm, SSD_WIDTH), lambda i: (i, CB_ZS * LANES // SSD_WIDTH)),
                  pl.BlockSpec((1, SSD_WIDTH), lambda i: (0, 0))],
        out_specs=[pl.BlockSpec((tm, SB_WIDTH), lambda i: (i, 0)),
                   pl.BlockSpec((tm, SSD_WIDTH), lambda i: (i, 0))],
        out_shape=[jax.ShapeDtypeStruct((t, SB_WIDTH), BF16), jax.ShapeDtypeStruct((t, SSD_WIDTH), BF16)],
        compiler_params=_params(("parallel",)), name="post_fwd")(o, proj, y, proj, nw)


def _post_bwd(dya, o, proj, dys, y, nw, dproj, *, tm=256):
    t = o.shape[0]
    zw = SB_WIDTH + SSD_WIDTH

    def body(dya_ref, o_ref, za_ref, dys_ref, y_ref, zs_ref, nw_ref, _, do_ref, dy_ref, dnw_ref, dz_ref):
        @pl.when(pl.program_id(0) == 0)
        def _():
            dnw_ref[...] = jnp.zeros_like(dnw_ref)

        za = za_ref[...].astype(F32)
        sa = _sigmoid(za)
        d = dya_ref[...].astype(F32)
        do_ref[...] = _bf(d * (za * sa))
        dz_ref[:, :SB_WIDTH] = _bf(d * o_ref[...] * (sa * (1.0 + za * (1.0 - sa))))
        zs = zs_ref[...].astype(F32)
        ss = _sigmoid(zs)
        silu = zs * ss
        yv = y_ref[...]
        yz = yv * silu
        dv = dys_ref[...].astype(F32)
        for g in range(SSD_GROUPS):
            sl = slice(g * GN, (g + 1) * GN)
            v = yz[:, sl]
            r = lax.rsqrt(jnp.mean(v * v, axis=-1, keepdims=True) + EPS)
            nrm = v * r
            dg = dv[:, sl]
            dnw_ref[:, sl] += jnp.sum(dg * nrm, axis=0, keepdims=True)
            dn = dg * nw_ref[:, sl]
            dyz = r * (dn - nrm * jnp.mean(dn * nrm, axis=-1, keepdims=True))
            dy_ref[:, sl] = dyz * silu[:, sl]
            dz_ref[:, SB_WIDTH + g * GN:SB_WIDTH + (g + 1) * GN] = _bf(
                dyz * yv[:, sl] * (ss[:, sl] * (1.0 + zs[:, sl] * (1.0 - ss[:, sl]))))

    a_spec = pl.BlockSpec((tm, SB_WIDTH), lambda i: (i, 0))
    s_spec = pl.BlockSpec((tm, SSD_WIDTH), lambda i: (i, 0))
    w_spec = pl.BlockSpec((1, SSD_WIDTH), lambda i: (0, 0))
    return pl.pallas_call(
        body, grid=(t // tm,),
        in_specs=[a_spec, a_spec, pl.BlockSpec((tm, SB_WIDTH), lambda i: (i, CB_ZA * LANES // SB_WIDTH)),
                  s_spec, s_spec, pl.BlockSpec((tm, SSD_WIDTH), lambda i: (i, CB_ZS * LANES // SSD_WIDTH)),
                  w_spec, pl.BlockSpec(memory_space=pl.ANY)],
        out_specs=[a_spec, s_spec, w_spec, pl.BlockSpec((tm, zw), lambda i: (i, 1))],
        out_shape=[jax.ShapeDtypeStruct((t, SB_WIDTH), BF16), jax.ShapeDtypeStruct((t, SSD_WIDTH), F32),
                   jax.ShapeDtypeStruct((1, SSD_WIDTH), F32), jax.ShapeDtypeStruct(dproj.shape, dproj.dtype)],
        input_output_aliases={7: 3},
        compiler_params=_params(("arbitrary",)), name="post_bwd")(dya, o, proj, dys, y, proj, nw, dproj)


def _merge_fwd(ya, ys, proj_g, *, tm=512):
    t = ya.shape[0]

    def body(ya_ref, ys_ref, ga_ref, gs_ref, m_ref):
        m_ref[...] = _bf(_sigmoid(ga_ref[...].astype(F32)) * ya_ref[...].astype(F32)
                         + _sigmoid(gs_ref[...].astype(F32)) * ys_ref[...].astype(F32))

    spec = pl.BlockSpec((tm, D_MODEL), lambda i: (i, 0))
    return pl.pallas_call(
        body, grid=(t // tm,),
        in_specs=[spec, spec, pl.BlockSpec((tm, D_MODEL), lambda i: (i, 0)),
                  pl.BlockSpec((tm, D_MODEL), lambda i: (i, 1))],
        out_specs=spec, out_shape=jax.ShapeDtypeStruct((t, D_MODEL), BF16),
        compiler_params=_params(("parallel",)), name="merge_fwd")(ya, ys, proj_g, proj_g)


def _merge_bwd(dm, ya, ys, proj_g, *, tm=512):
    t = ya.shape[0]

    def body(dm_ref, ya_ref, ys_ref, ga_ref, gs_ref, dya_ref, dys_ref, dg_ref):
        d = dm_ref[...].astype(F32)
        ga = _sigmoid(ga_ref[...].astype(F32))
        gs = _sigmoid(gs_ref[...].astype(F32))
        dya_ref[...] = _bf(d * ga)
        dys_ref[...] = _bf(d * gs)
        dg_ref[:, :D_MODEL] = _bf(d * ya_ref[...].astype(F32) * (ga * (1.0 - ga)))
        dg_ref[:, D_MODEL:] = _bf(d * ys_ref[...].astype(F32) * (gs * (1.0 - gs)))

    spec = pl.BlockSpec((tm, D_MODEL), lambda i: (i, 0))
    return pl.pallas_call(
        body, grid=(t // tm,),
        in_specs=[spec, spec, spec, pl.BlockSpec((tm, D_MODEL), lambda i: (i, 0)),
                  pl.BlockSpec((tm, D_MODEL), lambda i: (i, 1))],
        out_specs=[spec, spec, pl.BlockSpec((tm, 2 * D_MODEL), lambda i: (i, 0))],
        out_shape=[jax.ShapeDtypeStruct((t, D_MODEL), BF16), jax.ShapeDtypeStruct((t, D_MODEL), BF16),
                   jax.ShapeDtypeStruct((t, 2 * D_MODEL), BF16)],
        compiler_params=_params(("parallel",)), name="merge_bwd")(dm, ya, ys, proj_g, proj_g)


def _final(x, out, wf, target, *, tm=512):
    t = x.shape[0]

    def body(x_ref, o_ref, w_ref, t_ref, loss_ref, dx_ref, dw_ref):
        @pl.when(pl.program_id(0) == 0)
        def _():
            loss_ref[...] = jnp.zeros_like(loss_ref)
            dw_ref[...] = jnp.zeros_like(dw_ref)

        x2 = x_ref[...] + o_ref[...]
        r = lax.rsqrt(jnp.mean(x2 * x2, axis=-1, keepdims=True) + EPS)
        nrm = x2 * r
        e = nrm * w_ref[...] - t_ref[...]
        row_loss = jnp.mean(e * e, axis=-1, keepdims=True)
        loss_ref[...] += 0.5 * jnp.sum(row_loss, axis=0, keepdims=True)
        dyv = e * (1.0 / D_MODEL)
        dw_ref[...] += jnp.sum(dyv * nrm, axis=0, keepdims=True)
        dn = dyv * w_ref[...]
        dx_ref[...] = r * (dn - nrm * jnp.mean(dn * nrm, axis=-1, keepdims=True))

    spec = pl.BlockSpec((tm, D_MODEL), lambda i: (i, 0))
    fix = pl.BlockSpec((1, D_MODEL), lambda i: (0, 0))
    return pl.pallas_call(
        body, grid=(t // tm,),
        in_specs=[spec, spec, fix, spec],
        out_specs=[pl.BlockSpec((1, LANES), lambda i: (0, 0)), spec, fix],
        out_shape=[jax.ShapeDtypeStruct((1, LANES), F32), jax.ShapeDtypeStruct((t, D_MODEL), F32),
                   jax.ShapeDtypeStruct((1, D_MODEL), F32)],
        compiler_params=_params(("arbitrary",)), name="final_loss")(x, out, wf, target)


def _group_lanes(v):
    return jnp.pad(v.reshape(SSD_GROUPS, 1, SSD_HPG), ((0, 0), (0, 0), (0, LANES - SSD_HPG)))


def _ungroup_lanes(v):
    return v[:, :, :SSD_HPG].reshape(1, N_DT)


def _row_slabs(pieces, rows):
    total = sum(p.shape[0] for p in pieces)
    assert total % rows == 0
    slabs = []
    for lo in range(0, total, rows):
        parts, off = [], 0
        for p in pieces:
            a, b = max(lo, off), min(lo + rows, off + p.shape[0])
            if a < b:
                parts.append(p[a - off:b - off])
            off += p.shape[0]
        slabs.append(parts[0] if len(parts) == 1 else jnp.concatenate(parts, axis=0))
    return jnp.stack(slabs)


def _local_step(x, target, nb, s, w_in_t, conv_w, conv_b, norm_w, dt_bias, a_log, d_skip,
                ssm_norm_w, out_weights, final_norm_w, between=None):
    t = nb * s
    w_g_t = w_in_t[GATE_COL0:]
    w_dt_t = jnp.pad(w_in_t[DT_COL0:GATE_COL0], ((0, LANES - N_DT), (0, 0)))
    h, r1 = _rms_fwd(x, norm_w)
    proj = _matmul(h, w_in_t, nt=True, n=DT_COL0, out_dtype=BF16, name="proj_main", tm=MM_TALL)
    proj_g = _matmul(h, w_g_t, nt=True, out_dtype=BF16, name="proj_gate", tm=MM_TALL)
    dt_raw = _matmul(h, w_dt_t, nt=True, name="proj_dt")
    dtr = jnp.pad(dt_raw[:, :N_DT].reshape(t, SSD_GROUPS, SSD_HPG).transpose(1, 0, 2),
                  ((0, 0), (0, 0), (0, LANES - SSD_HPG)))
    bias_g, alog_g, dsk_g = _group_lanes(dt_bias), _group_lanes(a_log), _group_lanes(d_skip)

    o, att_a, att_sg = _attn_fwd(proj, nb, s)
    xbc = _conv_fwd(proj, conv_w, conv_b, nb, s)
    y, hs = _ssd_fwd(xbc, dtr, bias_g, alog_g, dsk_g, nb, s)
    ya_in, ys_in = _post_fwd(o, proj, y, ssm_norm_w)
    w_attn_out, w_ssm_out, w_o = out_weights(ys_in) if callable(out_weights) else out_weights
    ya = _matmul(ya_in, w_attn_out, out_dtype=BF16, name="attn_out")
    ys = _matmul(ys_in, w_ssm_out, out_dtype=BF16, name="ssm_out")
    merged = _merge_fwd(ya, ys, proj_g)
    out = _matmul(merged, w_o, name="out_proj")
    loss, dx2, d_final_w = _final(x, out, final_norm_w, target)

    dx2b = _bf(dx2)
    dmerged = _matmul(dx2b, w_o, nt=True, out_dtype=BF16, name="d_merged")
    d_w_o = _matmul(merged, dx2b, ta=True, name="d_w_o")
    dya, dys, dgate = _merge_bwd(dmerged, ya, ys, proj_g)
    d_w_attn_out = _matmul(ya_in, dya, ta=True, name="d_w_attn_out")
    d_w_ssm_out = _matmul(ys_in, dys, ta=True, name="d_w_ssm_out")
    dya_in = _matmul(dya, w_attn_out, nt=True, out_dtype=BF16, name="d_ya_in")
    dys_in = _matmul(dys, w_ssm_out, nt=True, name="d_ys_in")
    do, dy, d_ssm_norm_w, dproj = _post_bwd(dya_in, o, proj, dys_in, y, ssm_norm_w, lax.empty(proj.shape, BF16))
    dq, dk, dv = _attn_bwd(proj, do, att_a, att_sg, nb, s)
    dxs_act, db_act, dc_act, ddtr, dbias_g, dalog_g, ddsk_g = _ssd_bwd(
        xbc, dtr, bias_g, alog_g, dsk_g, hs, dy, nb, s)
    dproj, dwx, dbx = _conv_bwd(proj, dxs_act, conv_w, conv_b, dproj, nb, s, col0=0, name="conv_bwd_x")
    dproj, dwb, dbb = _conv_bwd(proj, db_act, conv_w, conv_b, dproj, nb, s, col0=SSD_WIDTH, name="conv_bwd_b")
    dproj, dwc, dbc = _conv_bwd(proj, dc_act, conv_w, conv_b, dproj, nb, s, col0=SSD_WIDTH + 512,
                                name="conv_bwd_c")
    for i, piece in enumerate((dq, dk, dv)):
        dproj = lax.dynamic_update_slice(dproj, piece, (0, i * SB_WIDTH))
    ddt = jnp.pad(_bf(ddtr[:, :, :SSD_HPG].transpose(1, 0, 2).reshape(t, N_DT)), ((0, 0), (0, LANES - N_DT)))
    wt = dict(ta=True, out_dtype=BF16)
    d_w_in_t = _row_slabs(
        [_matmul(dproj, h, name="d_w_main", **wt), _matmul(ddt, h, name="d_w_dt", **wt)[:N_DT],
         _matmul(dgate, h, name="d_w_gate", **wt)], SHARD)
    grads = dict(
        w_in_t=d_w_in_t,
        conv_w=jnp.concatenate([dwx, dwb, dwc], axis=1), conv_b=jnp.concatenate([dbx, dbb, dbc], axis=1),
        dt_bias=_ungroup_lanes(dbias_g), a_log=_ungroup_lanes(dalog_g), d_skip=_ungroup_lanes(ddsk_g),
        ssm_norm_w=d_ssm_norm_w, w_attn_out=d_w_attn_out, w_ssm_out=d_w_ssm_out, w_o=d_w_o,
        final_norm_w=d_final_w)
    extra = None
    if between is not None:
        zero, extra = between(grads)
        ddt = ddt + _bf(zero[0:1, 0:1])

    dh = _matmul(ddt, w_dt_t, name="d_h_dt")
    dh = _matmul(dgate, w_g_t, add=dh, name="d_h_gate")
    dh = _matmul(dproj, w_in_t, add=dh, name="d_h", tk=DT_COL0 // 4)
    grad_x, grads["norm_w"] = _rms_bwd(dh, x, r1, norm_w, dx2)
    return loss, grad_x, grads, extra


N_CHIP = N_DEV // 2


def _pair_exchange(scatter, bcast, *, name):
    arrays = list(scatter) + list(bcast)
    ns, n = len(scatter), len(arrays)
    n_sem = ns * N_CHIP + (n - ns) * (N_DEV - 1)

    def body(*refs):
        ins, outs = refs[:n], refs[n:2 * n]
        send_sems, recv_sems, local_sems = refs[2 * n:]
        x, y, c = lax.axis_index("x"), lax.axis_index("y"), lax.axis_index("c")
        me = 4 * x + 2 * y + c
        copies, k = [], 0
        for a in range(ns):
            for q in range(N_CHIP):
                cp = pltpu.make_async_remote_copy(
                    src_ref=ins[a].at[2 * q + 1 - c], dst_ref=outs[a].at[q],
                    send_sem=send_sems.at[k], recv_sem=recv_sems.at[k],
                    device_id=(x, y, 1 - c), device_id_type=pl.DeviceIdType.MESH)
                cp.start()
                copies.append(cp)
                k += 1
        for a in range(ns, n):
            cp = pltpu.make_async_copy(ins[a], outs[a].at[me], local_sems.at[a - ns])
            cp.start()
            copies.append(cp)
            for r in range(1, N_DEV):
                tx = 1 - x if (r >> 2) & 1 else x
                ty = 1 - y if (r >> 1) & 1 else y
                tc = 1 - c if r & 1 else c
                cp = pltpu.make_async_remote_copy(
                    src_ref=ins[a], dst_ref=outs[a].at[me], send_sem=send_sems.at[k], recv_sem=recv_sems.at[k],
                    device_id=(tx, ty, tc), device_id_type=pl.DeviceIdType.MESH)
                cp.start()
                copies.append(cp)
                k += 1
        for cp in copies:
            cp.wait()

    out_shape = [jax.ShapeDtypeStruct((N_CHIP,) + v.shape[1:] if i < ns else (N_DEV,) + v.shape, v.dtype)
                 for i, v in enumerate(arrays)]
    return pl.pallas_call(
        body, in_specs=[pl.BlockSpec(memory_space=pl.ANY)] * n,
        out_specs=[pl.BlockSpec(memory_space=pl.ANY)] * n, out_shape=out_shape,
        scratch_shapes=[pltpu.SemaphoreType.DMA((n_sem,)), pltpu.SemaphoreType.DMA((n_sem,)),
                        pltpu.SemaphoreType.DMA((max(n - ns, 1),))],
        compiler_params=pltpu.CompilerParams(has_side_effects=True), name=name)(*arrays)


def _pair_add(mine, recv, core, *, name, tc=None):
    _, r, c = recv.shape
    tc = c if tc is None else tc
    assert c % tc == 0

    def body(core_ref, m_ref, r_ref, o_ref):
        o_ref[...] = (m_ref[...].astype(F32) + r_ref[...].astype(F32)).astype(o_ref.dtype)

    spec = pl.BlockSpec((None, r, tc), lambda q, i, core_ref: (q, 0, i))
    return pl.pallas_call(
        body, grid_spec=pltpu.PrefetchScalarGridSpec(
            num_scalar_prefetch=1, grid=(N_CHIP, c // tc),
            in_specs=[pl.BlockSpec((None, r, tc), lambda q, i, core_ref: (2 * q + core_ref[0], 0, i)), spec],
            out_specs=spec),
        out_shape=jax.ShapeDtypeStruct(recv.shape, recv.dtype),
        compiler_params=_params(("parallel", "parallel")), name=name)(core, mine, recv)


_HBM = pl.BlockSpec(memory_space=pltpu.HBM)
_SEM = pl.BlockSpec(memory_space=pltpu.SEMAPHORE)
_DATAFLOW = pltpu.SideEffectType.DATAFLOW_SIDE_EFFECTING
_PEERS = {"chips": N_CHIP - 1, "all": N_DEV - 1}


def _split_copies(pattern, srcs, lands, send_sems, recv_sems):
    x, y, c = lax.axis_index("x"), lax.axis_index("y"), lax.axis_index("c")
    per = _PEERS[pattern]
    copies = []
    for a in range(len(srcs)):
        for r in range(1, per + 1):
            if pattern == "chips":
                tx = 1 - x if (r >> 1) & 1 else x
                ty = 1 - y if r & 1 else y
                tc = c
                src, dst = srcs[a].at[2 * tx + ty], lands[a].at[2 * x + y]
            else:
                tx = 1 - x if (r >> 2) & 1 else x
                ty = 1 - y if (r >> 1) & 1 else y
                tc = 1 - c if r & 1 else c
                src, dst = srcs[a], lands[a].at[4 * x + 2 * y + c]
            copies.append(pltpu.make_async_remote_copy(
                src_ref=src, dst_ref=dst, send_sem=send_sems.at[a * per + r - 1],
                recv_sem=recv_sems.at[a * per + r - 1], device_id=(tx, ty, tc), device_id_type=pl.DeviceIdType.MESH))
    return copies


def _split_start(pattern, arrays, *, name):
    n = len(arrays)
    n_sem = n * _PEERS[pattern]

    def body(*refs):
        srcs, lands = refs[:n], refs[n:2 * n]
        send_sems, recv_sems = refs[2 * n], refs[2 * n + 1]
        zero = refs[-1]
        for cp in _split_copies(pattern, srcs, lands, send_sems, recv_sems):
            cp.start()
        zero[...] = jnp.zeros_like(zero)

    src_t = [pltpu.HBM(v.shape, v.dtype) for v in arrays]
    land_t = [pltpu.HBM(v.shape if pattern == "chips" else (N_DEV,) + v.shape, v.dtype) for v in arrays]
    outs = pl.pallas_call(
        body, name=name,
        out_shape=(pltpu.SemaphoreType.DMA((n_sem,)), pltpu.SemaphoreType.DMA((n_sem,)), *src_t, *land_t,
                   jax.ShapeDtypeStruct((8, LANES), F32)),
        in_specs=[_HBM] * (2 * n),
        out_specs=(_SEM, _SEM, *[_HBM] * (2 * n), pl.BlockSpec(memory_space=pltpu.VMEM)),
        input_output_aliases={i: 2 + i for i in range(2 * n)},
        compiler_params=pltpu.CompilerParams(has_side_effects=_DATAFLOW))(
            *[pltpu.with_memory_space_constraint(v, pltpu.HBM) for v in arrays],
            *[pltpu.with_memory_space_constraint(lax.empty(t.shape, t.dtype), pltpu.HBM) for t in land_t])
    return outs[0], outs[1], outs[2:2 + n], outs[2 + n:2 + 2 * n], outs[-1]


def _split_wait(pattern, send_sems, recv_sems, srcs, lands, after, *, name):
    n = len(srcs)

    def body(*refs):
        for cp in _split_copies(pattern, refs[:n], refs[n:2 * n], refs[2 * n], refs[2 * n + 1]):
            cp.wait_send()
            cp.wait_recv()

    outs = pl.pallas_call(
        body, name=name, out_shape=[pltpu.HBM(v.shape, v.dtype) for v in (*srcs, *lands)],
        in_specs=[_HBM] * (2 * n) + [_SEM, _SEM, pl.BlockSpec(memory_space=pl.ANY)],
        out_specs=[_HBM] * (2 * n), input_output_aliases={i: i for i in range(2 * n)},
        compiler_params=pltpu.CompilerParams(has_side_effects=_DATAFLOW))(
            *srcs, *lands, send_sems, recv_sems, after)
    return outs[:n], outs[n:]


def _own_slab(lands, srcs, index, whole):
    out = []
    for land, src in zip(lands, srcs):
        own = src[None] if whole else lax.dynamic_slice(src, (index, 0, 0), (1,) + src.shape[1:])
        out.append(lax.dynamic_update_slice(land, own, (index, 0, 0)))
    return out


def _gather(arrays, *, name):
    n = len(arrays)
    per = N_DEV - 1

    def body(*refs):
        ins, outs = refs[:n], refs[n:2 * n]
        send_sems, recv_sems, local_sems = refs[2 * n:]
        x, y, c = lax.axis_index("x"), lax.axis_index("y"), lax.axis_index("c")
        me, sibling = (x, y, c), (x, y, 1 - c)
        chips = [(1 - x, y), (x, 1 - y), (1 - x, 1 - y)]

        def slab(a, block):
            return outs[a].at[4 * block[0] + 2 * block[1] + block[2]]

        def copy(a, k, block, to, src=None):
            rows = slab(a, block)
            return pltpu.make_async_remote_copy(
                src_ref=rows if src is None else src, dst_ref=rows,
                send_sem=send_sems.at[a * per + k], recv_sem=recv_sems.at[a * per + k],
                device_id=to, device_id_type=pl.DeviceIdType.MESH)

        started = []
        for a in range(n):
            mine = pltpu.make_async_copy(ins[a], slab(a, me), local_sems.at[a])
            mine.start()
            started.append(mine)
        for a in range(n):
            first = [copy(a, 1 + j, me, (*chip, c), src=ins[a]) for j, chip in enumerate(chips)]
            first.append(copy(a, 0, me, sibling, src=ins[a]))
            for cp in first:
                cp.start()
            started += first
        for j, chip in enumerate(chips):
            for a in range(n):
                copy(a, 1 + j, (*chip, c), me).wait_recv()
                passed = copy(a, 4 + j, (*chip, c), sibling)
                passed.start()
                started.append(passed)
        for a in range(n):
            copy(a, 0, sibling, me).wait_recv()
            for j, chip in enumerate(chips):
                copy(a, 4 + j, (*chip, 1 - c), me).wait_recv()
        for cp in started[:n]:
            cp.wait()
        for cp in started[n:]:
            cp.wait_send()

    return pl.pallas_call(
        body, in_specs=[pl.BlockSpec(memory_space=pl.ANY)] * n,
        out_specs=[pl.BlockSpec(memory_space=pl.ANY)] * n,
        out_shape=[jax.ShapeDtypeStruct((N_DEV,) + v.shape, v.dtype) for v in arrays],
        scratch_shapes=[pltpu.SemaphoreType.DMA((n * per,)), pltpu.SemaphoreType.DMA((n * per,)),
                        pltpu.SemaphoreType.DMA((n,))],
        compiler_params=pltpu.CompilerParams(has_side_effects=True), name=name)(*arrays)


def _adamw(parts, w, m, v, *, name, tc=None):
    r, c = w.shape
    n_parts = parts.shape[0]
    tc = c if tc is None else tc
    assert c % tc == 0

    def body(p_ref, w_ref, m_ref, v_ref, g_ref, d_ref, nm_ref, nv_ref):
        g = p_ref[0].astype(F32)
        for d in range(1, n_parts):
            g = g + p_ref[d].astype(F32)
        mm = ADAM_B1 * m_ref[...] + (1.0 - ADAM_B1) * g
        vv = ADAM_B2 * v_ref[...] + (1.0 - ADAM_B2) * (g * g)
        m_hat = mm / (1.0 - ADAM_B1 ** ADAM_STEP)
        v_hat = vv / (1.0 - ADAM_B2 ** ADAM_STEP)
        g_ref[...] = g
        d_ref[...] = -ADAM_LR * (m_hat / (jnp.sqrt(v_hat) + ADAM_EPS) + ADAM_WD * w_ref[...])
        nm_ref[...] = mm
        nv_ref[...] = vv

    spec = pl.BlockSpec((r, tc), lambda i: (0, i))
    out = jax.ShapeDtypeStruct((r, c), F32)
    return pl.pallas_call(
        body, grid=(c // tc,),
        in_specs=[pl.BlockSpec((n_parts, r, tc), lambda i: (0, 0, i)), spec, spec, spec],
        out_specs=[spec] * 4, out_shape=[out] * 4,
        compiler_params=_params(("parallel",)), name=name)(parts, w, m, v)


SMALL = (("norm_w", D_MODEL), ("conv_b", CONV_DIM), ("dt_bias", N_DT), ("a_log", N_DT),
         ("d_skip", N_DT), ("ssm_norm_w", SSD_WIDTH), ("final_norm_w", D_MODEL))
SMALL_USED = sum(size for _, size in SMALL)
SMALL_ROWS = 64
SHARD = D_PROJ // N_DEV


def _pack_small(vals, scalar=None):
    parts = [vals[k].reshape(-1).astype(F32) for k, _ in SMALL]
    if scalar is not None:
        parts.append(scalar.reshape(1))
    flat = jnp.concatenate(parts)
    return jnp.pad(flat, (0, SMALL_ROWS * LANES - flat.shape[0])).reshape(SMALL_ROWS, LANES)


def _unpack_small(packed, shapes):
    flat = packed.reshape(-1)
    out, off = {}, 0
    for k, size in SMALL:
        out[k] = flat[off:off + size].reshape(shapes[k])
        off += size
    return out


def kernel(x, norm_w, w_in, conv_w, conv_b, dt_bias, a_log, d_skip, ssm_norm_w, w_attn_out, w_ssm_out, w_o, final_norm_w, loss_target, m_norm_w, m_w_in, m_conv_w, m_conv_b, m_dt_bias, m_a_log, m_d_skip, m_ssm_norm_w, m_w_attn_out, m_w_ssm_out, m_w_o, m_final_norm_w, v_norm_w, v_w_in, v_conv_w, v_conv_b, v_dt_bias, v_a_log, v_d_skip, v_ssm_norm_w, v_w_attn_out, v_w_ssm_out, v_w_o, v_final_norm_w):
    nb, s, _ = x.shape
    t = nb * s
    weights = dict(norm_w=norm_w, w_in=w_in, conv_w=conv_w, conv_b=conv_b, dt_bias=dt_bias, a_log=a_log,
                   d_skip=d_skip, ssm_norm_w=ssm_norm_w, w_attn_out=w_attn_out, w_ssm_out=w_ssm_out,
                   w_o=w_o, final_norm_w=final_norm_w)
    moms = dict(norm_w=m_norm_w, w_in=m_w_in, conv_w=m_conv_w, conv_b=m_conv_b, dt_bias=m_dt_bias,
                a_log=m_a_log, d_skip=m_d_skip, ssm_norm_w=m_ssm_norm_w, w_attn_out=m_w_attn_out,
                w_ssm_out=m_w_ssm_out, w_o=m_w_o, final_norm_w=m_final_norm_w)
    vels = dict(norm_w=v_norm_w, w_in=v_w_in, conv_w=v_conv_w, conv_b=v_conv_b, dt_bias=v_dt_bias,
                a_log=v_a_log, d_skip=v_d_skip, ssm_norm_w=v_ssm_norm_w, w_attn_out=v_w_attn_out,
                w_ssm_out=v_w_ssm_out, w_o=v_w_o, final_norm_w=v_final_norm_w)

    tr_ = lambda a: jnp.transpose(a[0])
    g_in, g_conv = _gather([_bf(tr_(w_in)), conv_w[0]], name="gather_weights")
    conv_full = g_conv.transpose(1, 0, 2).reshape(CONV_K, CONV_DIM)
    o_send, o_recv, o_srcs, o_lands, o_zero = _split_start(
        "all", [_bf(w_attn_out[0]), _bf(w_ssm_out[0]), _bf(w_o[0])], name="gather_out_weights_start")

    core = lax.axis_index("c").astype(jnp.int32).reshape(1)
    chip = 2 * lax.axis_index("x") + lax.axis_index("y")
    me = 2 * chip + lax.axis_index("c")

    def out_weights(after):
        srcs, lands = _split_wait("all", o_send, o_recv, o_srcs, o_lands, after, name="gather_out_weights_wait")
        g_wa, g_ws, g_wo = _own_slab(lands, srcs, me, True)
        return (g_wa.reshape(SB_WIDTH, D_MODEL), g_ws.reshape(SSD_WIDTH, D_MODEL), g_wo.reshape(D_MODEL, D_MODEL))

    def send_weight_grads(grads):
        slabs = [grads["w_in_t"],
                 grads["conv_w"].reshape(CONV_K, N_DEV, CONV_DIM // N_DEV).transpose(1, 0, 2),
                 _bf(grads["w_attn_out"].reshape(N_DEV, SB_WIDTH // N_DEV, D_MODEL)),
                 _bf(grads["w_ssm_out"].reshape(N_DEV, SSD_WIDTH // N_DEV, D_MODEL)),
                 _bf(grads["w_o"].reshape(N_DEV, D_MODEL // N_DEV, D_MODEL))]
        recv = _pair_exchange(slabs, [], name="exchange_pairs")
        names = ("w_in", "conv_w", "w_attn_out", "w_ssm_out", "w_o")
        sums = [_pair_add(a, b, core, name="pair_add_" + k, tc=4 * LANES if k == "w_in" else None)
                for k, a, b in zip(names, slabs, recv)]
        send_sems, recv_sems, srcs, lands, zero = _split_start("chips", sums, name="exchange_chips_start")
        return zero, (send_sems, recv_sems, srcs, lands)

    loss_part, grad_x, grads, (send_sems, recv_sems, srcs, lands) = _local_step(
        x.reshape(t, D_MODEL), loss_target.reshape(t, D_MODEL), nb, s, g_in.reshape(D_PROJ, D_MODEL),
        conv_full, conv_b, norm_w + o_zero[0:1, 0:1], dt_bias, a_log, d_skip, ssm_norm_w, out_weights,
        final_norm_w.reshape(1, D_MODEL), between=send_weight_grads)

    (p_small,) = _pair_exchange([], [_pack_small(grads, loss_part[0, 0])], name="exchange_small")
    srcs, lands = _split_wait("chips", send_sems, recv_sems, srcs, lands, grad_x, name="exchange_chips_wait")
    p_in, p_conv, p_wa, p_ws, p_wo = _own_slab(lands, srcs, chip, False)

    res = {}
    res["w_in"] = [a.T for a in _adamw(p_in, tr_(w_in), tr_(m_w_in), tr_(v_w_in), name="adamw_w_in", tc=2 * LANES)]
    res["conv_w"] = _adamw(p_conv, conv_w[0], m_conv_w[0], v_conv_w[0], name="adamw_conv_w")
    res["w_attn_out"] = _adamw(p_wa, w_attn_out[0], m_w_attn_out[0], v_w_attn_out[0], name="adamw_w_attn_out")
    res["w_ssm_out"] = _adamw(p_ws, w_ssm_out[0], m_w_ssm_out[0], v_w_ssm_out[0], name="adamw_w_ssm_out")
    res["w_o"] = _adamw(p_wo, w_o[0], m_w_o[0], v_w_o[0], name="adamw_w_o")
    sm = _adamw(p_small, _pack_small(weights), _pack_small(moms), _pack_small(vels), name="adamw_small")
    loss = sm[0].reshape(-1)[SMALL_USED]
    shapes = {k: weights[k].shape for k, _ in SMALL}
    sm = [_unpack_small(v, shapes) for v in sm]
    for k, _ in SMALL:
        res[k] = tuple(part[k] for part in sm)

    order = ("norm_w", "w_in", "conv_w", "conv_b", "dt_bias", "a_log", "d_skip", "ssm_norm_w",
             "w_attn_out", "w_ssm_out", "w_o", "final_norm_w")
    outs = [loss, grad_x.reshape(nb, s, D_MODEL)]
    for i in range(4):
        outs += [res[k][i].reshape(weights[k].shape) for k in order]
    return tuple(outs)
```

```python
import functools

import jax
import jax.numpy as jnp
from jax import lax
from jax.experimental import pallas as pl
from jax.experimental.pallas import tpu as pltpu

F32 = jnp.float32
BF16 = jnp.bfloat16

D_MODEL = 1024
SB_WIDTH = 1024
HEAD_DIM = 64
SSD_WIDTH = 2048
SSD_GROUPS = 4
SSD_HPG = 8
SSD_STATE = 128
SSD_CHUNK = 128
CONV_K = 4
CONV_DIM = 3072
N_DT = 32
D_PROJ = 11296
DT_COL0 = 9216
GATE_COL0 = DT_COL0 + N_DT
EPS = 1e-6
N_DEV = 8

ADAM_LR = 0.001
ADAM_B1 = 0.9
ADAM_B2 = 0.999
ADAM_EPS = 1e-08
ADAM_WD = 0.01
ADAM_STEP = 10

LANES = 128
VMEM_LIMIT = 56 * 1024 * 1024

CB_Q, CB_K, CB_V, CB_ZA, CB_ZS, CB_XBC = 0, 8, 16, 24, 32, 48

NT_DIMS = (((1,), (1,)), ((), ()))
TN_DIMS = (((0,), (0,)), ((), ()))


def _params(sem):
    return pltpu.CompilerParams(dimension_semantics=sem, vmem_limit_bytes=VMEM_LIMIT)


def _dot(a, b):
    return jnp.dot(a, b, preferred_element_type=F32)


def _dot_nt(a, b):
    return lax.dot_general(a, b, NT_DIMS, preferred_element_type=F32)


def _dot_tn(a, b):
    return lax.dot_general(a, b, TN_DIMS, preferred_element_type=F32)


def _bf(a):
    return a.astype(BF16)


def _sigmoid(x):
    return 0.5 + 0.5 * jnp.tanh(0.5 * x)


def _softplus(x):
    return jnp.maximum(x, 0.0) + jnp.log(1.0 + jnp.exp(-jnp.abs(x)))


def _trunc_split(a):
    bits = lax.bitcast_convert_type(a, jnp.uint32) & jnp.uint32(0xFFFF0000)
    hi = lax.bitcast_convert_type(bits, F32)
    return _bf(hi), _bf(a - hi)


MM_TILE = 1024
MM_TK = 2048
MM_TALL = 2048


def _matmul(a, b, *, name, nt=False, ta=False, add=None, more=(), out_dtype=F32, tm=MM_TILE, tn=MM_TILE, tk=MM_TK,
            n=None):
    k, m = a.shape if ta else a.shape[::-1]
    n = (b.shape[0] if nt else b.shape[1]) if n is None else n
    tm, tn, tk = min(tm, m), min(tn, n), min(tk, k)
    assert m % tm == 0 and n % tn == 0 and k % tk == 0, (name, a.shape, b.shape)
    assert not (ta and nt) and not (more and (ta or nt))
    nk = k // tk
    has_add = add is not None
    n_in = 2 + has_add + 2 * len(more)

    def body(*refs):
        a_ref, b_ref = refs[:2]
        add_ref = refs[2] if has_add else None
        more_refs = refs[2 + has_add:n_in]
        o_ref = refs[n_in]
        acc_ref = refs[-1] if nk > 1 else None
        kk = pl.program_id(2)

        def product():
            av, bv = _bf(a_ref[...]), _bf(b_ref[...])
            return _dot_nt(av, bv) if nt else _dot_tn(av, bv) if ta else _dot(av, bv)

        def last(r):
            for a2_ref, b2_ref in zip(more_refs[::2], more_refs[1::2]):
                r = r + _dot(_bf(a2_ref[...]), _bf(b2_ref[...]))
            return (r + add_ref[...]) if has_add else r

        if nk == 1:
            o_ref[...] = last(product()).astype(out_dtype)
            return

        @pl.when(kk == 0)
        def _():
            acc_ref[...] = product()

        @pl.when((kk > 0) & (kk < nk - 1))
        def _():
            acc_ref[...] += product()

        @pl.when(kk == nk - 1)
        def _():
            o_ref[...] = last(acc_ref[...] + product()).astype(out_dtype)

    in_specs = [pl.BlockSpec((tk, tm), lambda j, i, kk: (kk, i)) if ta
                else pl.BlockSpec((tm, tk), lambda j, i, kk: (i, kk)),
                pl.BlockSpec((tn, tk), lambda j, i, kk: (j, kk)) if nt
                else pl.BlockSpec((tk, tn), lambda j, i, kk: (kk, j))]
    args = [a, b]
    if has_add:
        in_specs.append(pl.BlockSpec((tm, tn), lambda j, i, kk: (i, j)))
        args.append(add)
    for a2, b2 in more:
        assert a2.shape[0] == m and b2.shape == (a2.shape[1], n), (name, a2.shape, b2.shape)
        in_specs += [pl.BlockSpec((tm, a2.shape[1]), lambda j, i, kk: (i, 0)),
                     pl.BlockSpec((a2.shape[1], tn), lambda j, i, kk: (0, j))]
        args += [a2, b2]
    return pl.pallas_call(
        body, grid=(n // tn, m // tm, nk), in_specs=in_specs,
        out_specs=pl.BlockSpec((tm, tn), lambda j, i, kk: (i, j)),
        out_shape=jax.ShapeDtypeStruct((m, n), out_dtype),
        scratch_shapes=[pltpu.VMEM((tm, tn), F32)] if nk > 1 else [],
        compiler_params=_params(("parallel", "parallel", "arbitrary")), name=name)(*args)


def _rms_fwd(x, w, *, tm=512):
    t = x.shape[0]

    def body(x_ref, w_ref, h_ref, r_ref):
        xv = x_ref[...]
        r = lax.rsqrt(jnp.mean(xv * xv, axis=-1, keepdims=True) + EPS)
        h_ref[...] = _bf(xv * r * w_ref[...])
        r_ref[...] = r

    row = lambda i: (i, 0)
    return pl.pallas_call(
        body, grid=(t // tm,),
        in_specs=[pl.BlockSpec((tm, D_MODEL), row), pl.BlockSpec((1, D_MODEL), lambda i: (0, 0))],
        out_specs=[pl.BlockSpec((tm, D_MODEL), row), pl.BlockSpec((tm, 1), row)],
        out_shape=[jax.ShapeDtypeStruct((t, D_MODEL), BF16), jax.ShapeDtypeStruct((t, 1), F32)],
        compiler_params=_params(("parallel",)), name="rms_fwd")(x, w)


def _rms_bwd(dh, x, r, w, dres, *, tm=512):
    t = x.shape[0]

    def body(dh_ref, x_ref, r_ref, w_ref, dres_ref, dx_ref, dw_ref):
        @pl.when(pl.program_id(0) == 0)
        def _():
            dw_ref[...] = jnp.zeros_like(dw_ref)

        nrm = x_ref[...] * r_ref[...]
        dhv = dh_ref[...]
        dw_ref[...] += jnp.sum(dhv * nrm, axis=0, keepdims=True)
        dn = dhv * w_ref[...]
        dx = r_ref[...] * (dn - nrm * jnp.mean(dn * nrm, axis=-1, keepdims=True))
        dx_ref[...] = dx + dres_ref[...]

    row = lambda i: (i, 0)
    fix = lambda i: (0, 0)
    return pl.pallas_call(
        body, grid=(t // tm,),
        in_specs=[pl.BlockSpec((tm, D_MODEL), row), pl.BlockSpec((tm, D_MODEL), row),
                  pl.BlockSpec((tm, 1), row), pl.BlockSpec((1, D_MODEL), fix),
                  pl.BlockSpec((tm, D_MODEL), row)],
        out_specs=[pl.BlockSpec((tm, D_MODEL), row), pl.BlockSpec((1, D_MODEL), fix)],
        out_shape=[jax.ShapeDtypeStruct((t, D_MODEL), F32), jax.ShapeDtypeStruct((1, D_MODEL), F32)],
        compiler_params=_params(("arbitrary",)), name="rms_bwd")(dh, x, r, w, dres)


ATT_BLK = 256


def _attn_masks(blk):
    r_io = lax.broadcasted_iota(jnp.int32, (blk, blk), 0)
    c_io = lax.broadcasted_iota(jnp.int32, (blk, blk), 1)
    return r_io, c_io


def _attn_tiles(nq):
    return nq * (nq + 1) // 2


def _attn_fwd(proj, nb, s):
    blk = min(ATT_BLK, s)
    nq = s // blk
    nt = _attn_tiles(nq)
    scale = HEAD_DIM ** -0.5

    def body(q_ref, k_ref, v_ref, o_ref, a_out, sg_out, q0_s, q1_s, kb_s, v0_s, v1_s, z_s, a_s, acc_s, cr_s):
        head0 = lax.broadcasted_iota(jnp.int32, (s, LANES), 1) < HEAD_DIM
        r_io, c_io = _attn_masks(blk)
        tri = c_io < r_io
        tri2 = jnp.concatenate([tri, tri], axis=0)
        u_gt = _bf(r_io > c_io)
        u2 = jnp.concatenate([u_gt, u_gt], axis=0)
        qv = q_ref[...] * scale
        q0_s[...] = _bf(jnp.where(head0, qv, 0.0))
        q1_s[...] = _bf(jnp.where(head0, 0.0, qv))
        kb_s[...] = _bf(k_ref[...])
        vv = v_ref[...]
        v0_s[...] = _bf(jnp.where(head0, vv, 0.0))
        v1_s[...] = _bf(jnp.where(head0, 0.0, vv))

        def scores(qcat, kblk, slot):
            k0 = pl.multiple_of(jnp.maximum(kblk, 0) * blk, blk)
            z_s[slot] = _dot_nt(qcat, kb_s[pl.ds(k0, blk), :])

        def weights(slot, tile, diag, zslot=None):
            z = z_s[slot if zslot is None else zslot]
            cr = cr_s[...]
            sp = _softplus(z)
            spm = jnp.where(tri2, sp, 0.0) if diag else sp
            hi, lo = _trunc_split(spm)
            ps = _dot(jnp.concatenate([hi, lo], axis=1), u2)
            lb = z - sp
            a = jnp.exp(lb - (ps + cr))
            if diag:
                a = jnp.where(tri2, a, 0.0)
            ab = _bf(a)
            a_s[slot] = ab
            a_out[tile] = ab
            sg_out[tile] = _bf(jnp.exp(lb))
            cr_s[...] = cr + ps[:, 0:1] + spm[:, 0:1]

        def apply(slot, k0):
            a = a_s[slot]
            vcat = jnp.concatenate([v0_s[pl.ds(k0, blk), :], v1_s[pl.ds(k0, blk), :]], axis=0)
            acc_s[...] += _dot(jnp.concatenate([a[:blk], a[blk:]], axis=1), vcat)

        def queries(qi):
            q0 = pl.multiple_of(qi * blk, blk)
            return jnp.concatenate([q0_s[pl.ds(q0, blk), :], q1_s[pl.ds(q0, blk), :]], axis=0)

        def first_scores(qi):
            qcat = queries(qi)
            scores(qcat, qi, 2)
            scores(qcat, qi - 1, 3)

        first_scores(jnp.int32(0))

        def qblock(qi, _):
            q0 = pl.multiple_of(qi * blk, blk)
            tile0 = qi * (qi + 1) // 2
            qcat = queries(qi)
            z_s[1] = z_s[3]
            cr_s[...] = jnp.zeros_like(cr_s)
            acc_s[...] = jnp.zeros_like(acc_s)
            weights(0, tile0 + qi, True, zslot=2)
            first_scores(jnp.minimum(qi + 1, nq - 1))

            def step(j, slot):
                scores(qcat, qi - j - 1, 1 - slot)
                weights(slot, tile0 + qi - j, False)
                apply(1 - slot, pl.multiple_of((qi - j + 1) * blk, blk))

            def pair(i, _):
                step(2 * i + 1, 1)
                step(2 * i + 2, 0)
                return 0

            lax.fori_loop(0, qi // 2, pair, 0)

            @pl.when(qi % 2 == 1)
            def _():
                step(qi, 1)

            apply(qi & 1, 0)
            o_ref[pl.ds(q0, blk), :] = acc_s[...]
            return 0

        lax.fori_loop(0, nq, qblock, 0)

    t = nb * s
    tiles = pl.BlockSpec((None, None, nt, 2 * blk, blk), lambda b, p: (b, p, 0, 0, 0))
    tiles_shape = jax.ShapeDtypeStruct((nb, 8, nt, 2 * blk, blk), BF16)
    return pl.pallas_call(
        body, grid=(nb, 8),
        in_specs=[pl.BlockSpec((s, LANES), lambda b, p: (b, CB_Q + p)),
                  pl.BlockSpec((s, LANES), lambda b, p: (b, CB_K + p)),
                  pl.BlockSpec((s, LANES), lambda b, p: (b, CB_V + p))],
        out_specs=[pl.BlockSpec((s, LANES), lambda b, p: (b, p)), tiles, tiles],
        out_shape=[jax.ShapeDtypeStruct((t, SB_WIDTH), F32), tiles_shape, tiles_shape],
        scratch_shapes=[pltpu.VMEM((s, LANES), BF16)] * 5 + [
            pltpu.VMEM((4, 2 * blk, blk), F32), pltpu.VMEM((2, 2 * blk, blk), BF16),
            pltpu.VMEM((blk, LANES), F32), pltpu.VMEM((2 * blk, 1), F32)],
        compiler_params=_params(("parallel", "parallel")), name="attn_fwd")(proj, proj, proj)


def _attn_bwd(proj, do, a_all, sg_all, nb, s):
    blk = min(ATT_BLK, s)
    nq = s // blk
    nt = _attn_tiles(nq)
    scale = HEAD_DIM ** -0.5

    def body(q_ref, k_ref, v_ref, do_ref, a_in, sg_in, dq_ref, dk_ref, dv_ref,
             qt0_s, qt1_s, k0_s, k1_s, vb_s, d0_s, d1_s, dt0_s, dt1_s, da_s, dz_s, dkt_acc, dvt_acc, dq_acc, cw_s):
        head0 = lax.broadcasted_iota(jnp.int32, (s, LANES), 1) < HEAD_DIM
        r_io, c_io = _attn_masks(blk)
        tri = c_io < r_io
        tri2 = jnp.concatenate([tri, tri], axis=0)
        u_lt = _bf(r_io < c_io)
        head0_t = lax.broadcasted_iota(jnp.int32, (LANES, s), 0) < HEAD_DIM
        q_t = (q_ref[...].astype(F32) * scale).T
        qt0_s[...] = _bf(jnp.where(head0_t, q_t, 0.0))
        qt1_s[...] = _bf(jnp.where(head0_t, 0.0, q_t))
        kv = k_ref[...]
        k0_s[...] = _bf(jnp.where(head0, kv, 0.0))
        k1_s[...] = _bf(jnp.where(head0, 0.0, kv))
        vb_s[...] = _bf(v_ref[...])
        dov = do_ref[...]
        d0_s[...] = _bf(jnp.where(head0, dov, 0.0))
        d1_s[...] = _bf(jnp.where(head0, 0.0, dov))
        do_t = dov.astype(F32).T
        dt0_s[...] = _bf(jnp.where(head0_t, do_t, 0.0))
        dt1_s[...] = _bf(jnp.where(head0_t, 0.0, do_t))
        dkt_acc[...] = jnp.zeros_like(dkt_acc)
        dvt_acc[...] = jnp.zeros_like(dvt_acc)

        def qblock(qi, _):
            q0 = pl.multiple_of(qi * blk, blk)
            tile0 = qi * (qi + 1) // 2
            qs_ = pl.ds(q0, blk)
            dcat = jnp.concatenate([d0_s[qs_, :], d1_s[qs_, :]], axis=0)
            qcat_t = jnp.concatenate([qt0_s[:, qs_], qt1_s[:, qs_]], axis=1)
            dcat_t = jnp.concatenate([dt0_s[:, qs_], dt1_s[:, qs_]], axis=1)

            def scores(kblk, slot):
                da_s[slot] = _dot_nt(dcat, vb_s[pl.ds(pl.multiple_of(kblk * blk, blk), blk), :])

            def grads(slot, kblk, diag):
                cw = cw_s[...]
                w = a_in[tile0 + kblk].astype(F32) * da_s[slot]
                wex = _dot(_bf(w), u_lt)
                dz = w - sg_in[tile0 + kblk].astype(F32) * (w + (wex + cw))
                if diag:
                    dz = jnp.where(tri2, dz, 0.0)
                dz_s[slot] = _bf(dz)
                cw_s[...] = cw + wex[:, blk - 1:blk] + w[:, blk - 1:blk]

            def apply(slot, kblk):
                ks_ = pl.ds(pl.multiple_of(kblk * blk, blk), blk)
                dz = dz_s[slot]
                dq_acc[...] += _dot(jnp.concatenate([dz[:blk], dz[blk:]], axis=1),
                                    jnp.concatenate([k0_s[ks_, :], k1_s[ks_, :]], axis=0))
                dkt_acc[:, ks_] += _dot(qcat_t, dz)
                dvt_acc[:, ks_] += _dot(dcat_t, a_in[tile0 + kblk])

            def step(j, slot, diag, first=False):
                if not diag:
                    scores(j + 1, 1 - slot)
                grads(slot, j, diag)
                if not first:
                    apply(1 - slot, j - 1)

            scores(0, 0)
            dq_acc[...] = jnp.zeros_like(dq_acc)
            cw_s[...] = jnp.zeros_like(cw_s)

            @pl.when(qi == 0)
            def _():
                step(0, 0, True, first=True)

            @pl.when(qi > 0)
            def _():
                step(0, 0, False, first=True)

            def pair(i, _):
                step(2 * i + 1, 1, False)
                step(2 * i + 2, 0, False)
                return 0

            lax.fori_loop(0, (qi - 1) // 2, pair, 0)

            @pl.when((qi > 0) & (qi % 2 == 0))
            def _():
                step(qi - 1, 1, False)
                step(qi, 0, True)

            @pl.when(qi % 2 == 1)
            def _():
                step(qi, 1, True)

            apply(qi & 1, qi)
            dq_ref[qs_, :] = _bf(dq_acc[...] * scale)
            return 0

        lax.fori_loop(0, nq, qblock, 0)
        dk_ref[...] = _bf(dkt_acc[...].T)
        dv_ref[...] = _bf(dvt_acc[...].T)

    t = nb * s
    hp = lambda b, p: (b, p)
    out = jax.ShapeDtypeStruct((t, SB_WIDTH), BF16)
    tiles = pl.BlockSpec((None, None, nt, 2 * blk, blk), lambda b, p: (b, p, 0, 0, 0))
    return pl.pallas_call(
        body, grid=(nb, 8),
        in_specs=[pl.BlockSpec((s, LANES), lambda b, p: (b, CB_Q + p)),
                  pl.BlockSpec((s, LANES), lambda b, p: (b, CB_K + p)),
                  pl.BlockSpec((s, LANES), lambda b, p: (b, CB_V + p)),
                  pl.BlockSpec((s, LANES), hp), tiles, tiles],
        out_specs=[pl.BlockSpec((s, LANES), hp)] * 3,
        out_shape=[out, out, out],
        scratch_shapes=[pltpu.VMEM((LANES, s), BF16)] * 2 + [pltpu.VMEM((s, LANES), BF16)] * 5 + [
            pltpu.VMEM((LANES, s), BF16)] * 2 + [
            pltpu.VMEM((2, 2 * blk, blk), F32), pltpu.VMEM((2, 2 * blk, blk), BF16),
            pltpu.VMEM((LANES, s), F32), pltpu.VMEM((LANES, s), F32),
            pltpu.VMEM((blk, LANES), F32), pltpu.VMEM((2 * blk, 1), F32)],
        compiler_params=_params(("parallel", "parallel")), name="attn_bwd")(proj, proj, proj, do, a_all, sg_all)


CONV_TC = 256
CONV_ROWS = 64
HALO = 8


def _conv_rows(ws, w_ref, b_ref, n):
    taps = [ws[HALO - kk:HALO - kk + n, :] for kk in range(CONV_K)]
    y = b_ref[...] + w_ref[CONV_K - 1:CONV_K, :] * taps[0]
    for kk in range(1, CONV_K):
        y = y + w_ref[CONV_K - 1 - kk:CONV_K - kk, :] * taps[kk]
    return y, taps


def _conv_fwd(proj, conv_w, conv_b, nb, s):
    tc, rows = CONV_TC, min(CONV_ROWS, s)
    cb0 = CB_XBC * LANES // tc

    def body(x_ref, w_ref, b_ref, o_ref, xp, ws):
        xp[0:HALO, :] = jnp.zeros((HALO, tc), F32)
        xp[HALO:HALO + s, :] = x_ref[...].astype(F32)

        def chunk(i, _):
            r0 = pl.multiple_of(i * rows, rows)
            ws[...] = xp[pl.ds(r0, rows + HALO), :]
            y, _ = _conv_rows(ws, w_ref, b_ref, rows)
            o_ref[pl.ds(r0, rows), :] = y * _sigmoid(y)
            return 0

        lax.fori_loop(0, s // rows, chunk, 0)

    t = nb * s
    return pl.pallas_call(
        body, grid=(nb, CONV_DIM // tc),
        in_specs=[pl.BlockSpec((s, tc), lambda b, c: (b, cb0 + c)),
                  pl.BlockSpec((CONV_K, tc), lambda b, c: (0, c)),
                  pl.BlockSpec((1, tc), lambda b, c: (0, c))],
        out_specs=pl.BlockSpec((s, tc), lambda b, c: (b, c)),
        out_shape=jax.ShapeDtypeStruct((t, CONV_DIM), F32),
        scratch_shapes=[pltpu.VMEM((s + HALO, tc), F32), pltpu.VMEM((rows + HALO, tc), F32)],
        compiler_params=_params(("parallel", "parallel")), name="conv_fwd")(proj, conv_w, conv_b)


def _conv_bwd(proj, dact, conv_w, conv_b, dproj, nb, s, *, col0, name):
    width = dact.shape[1]
    tc, rows = min(CONV_TC, width), min(CONV_ROWS, s)
    cb0 = (CB_XBC * LANES + col0) // tc
    wb0 = col0 // tc

    def body(x_ref, d_ref, w_ref, b_ref, _, dx_ref, dw_ref, db_ref, xp, dp, dys, ws):
        @pl.when(pl.program_id(1) == 0)
        def _():
            dw_ref[...] = jnp.zeros_like(dw_ref)
            db_ref[...] = jnp.zeros_like(db_ref)

        n = rows + HALO
        pad = jnp.zeros((HALO, tc), F32)
        xp[0:HALO, :] = pad
        xp[HALO:HALO + s, :] = x_ref[...].astype(F32)
        xp[HALO + s:HALO + s + HALO, :] = pad
        dp[0:s, :] = d_ref[...]
        dp[s:s + HALO, :] = pad

        def chunk(i, acc):
            r0 = pl.multiple_of(i * rows, rows)
            ws[...] = xp[pl.ds(r0, n + HALO), :]
            y, taps = _conv_rows(ws, w_ref, b_ref, n)
            sg = _sigmoid(y)
            dy = dp[pl.ds(r0, n), :] * (sg * (1.0 + y * (1.0 - sg)))
            dys[...] = dy
            dx = w_ref[CONV_K - 1:CONV_K, :] * dy[0:rows]
            for kk in range(1, CONV_K):
                dx = dx + w_ref[CONV_K - 1 - kk:CONV_K - kk, :] * dys[kk:kk + rows, :]
            dx_ref[pl.ds(r0, rows), :] = _bf(dx)
            dyc = dy[0:rows]
            new = [acc[0] + jnp.sum(dyc, axis=0, keepdims=True)]
            for kk in range(CONV_K):
                new.append(acc[1 + kk] + jnp.sum(dyc * taps[kk][0:rows], axis=0, keepdims=True))
            return tuple(new)

        acc = lax.fori_loop(0, s // rows, chunk, (jnp.zeros((1, tc), F32),) * (1 + CONV_K))
        db_ref[...] += acc[0]
        for kk in range(CONV_K):
            dw_ref[CONV_K - 1 - kk:CONV_K - kk, :] += acc[1 + kk]

    t = nb * s
    return pl.pallas_call(
        body, grid=(width // tc, nb),
        in_specs=[pl.BlockSpec((s, tc), lambda c, b: (b, cb0 + c)),
                  pl.BlockSpec((s, tc), lambda c, b: (b, c)),
                  pl.BlockSpec((CONV_K, tc), lambda c, b: (0, wb0 + c)),
                  pl.BlockSpec((1, tc), lambda c, b: (0, wb0 + c)), pl.BlockSpec(memory_space=pl.ANY)],
        out_specs=[pl.BlockSpec((s, tc), lambda c, b: (b, cb0 + c)),
                   pl.BlockSpec((CONV_K, tc), lambda c, b: (0, c)),
                   pl.BlockSpec((1, tc), lambda c, b: (0, c))],
        out_shape=[jax.ShapeDtypeStruct(dproj.shape, dproj.dtype),
                   jax.ShapeDtypeStruct((CONV_K, width), F32),
                   jax.ShapeDtypeStruct((1, width), F32)],
        input_output_aliases={4: 0},
        scratch_shapes=[pltpu.VMEM((s + 2 * HALO, tc), F32), pltpu.VMEM((s + HALO, tc), F32),
                        pltpu.VMEM((rows + HALO, tc), F32), pltpu.VMEM((rows + 2 * HALO, tc), F32)],
        compiler_params=_params(("parallel", "arbitrary")), name=name)(proj, dact, conv_w, conv_b, dproj)


L = SSD_CHUNK
N_PAIR = SSD_HPG // 2


def _pair_lanes(v0, v1, lane):
    return jnp.where(lane < HEAD_DIM, v0, v1)


def _onehot(shape, row_shift, col_shift):
    row = lax.broadcasted_iota(jnp.int32, shape, 0)
    col = lax.broadcasted_iota(jnp.int32, shape, 1)
    return _bf(lax.shift_right_logical(row, row_shift) == lax.shift_right_logical(col, col_shift))


def _ssd_tables():
    sp = _onehot((LANES, N_PAIR * LANES), 0, 6)
    sf = _onehot((LANES, SSD_HPG * LANES), 0, 7)
    return (jnp.concatenate([sp] * 3, axis=0), jnp.concatenate([sf] * 3, axis=0),
            _onehot((N_PAIR * LANES, LANES), 6, 0), _onehot((SSD_HPG * LANES, LANES), 7, 0))


def _split3(a):
    a1 = _bf(a)
    r1 = a - a1.astype(F32)
    a2 = _bf(r1)
    return jnp.concatenate([a1, a2, _bf(r1 - a2.astype(F32))], axis=1)


def _split2(a):
    hi, lo = _trunc_split(a)
    return jnp.concatenate([hi, lo], axis=1)


def _tri_dot(tri_bf, a):
    a1 = _bf(a)
    r1 = a - a1.astype(F32)
    a2 = _bf(r1)
    parts = jnp.concatenate([a1, a2, _bf(r1 - a2.astype(F32))], axis=0)
    return _dot(jnp.concatenate([tri_bf] * 3, axis=1), parts)


HEAD_W = 4 * LANES + 2 * N_PAIR * LANES + SSD_HPG * LANES


def _ssd_head(dtr_ref, bias_ref, a_neg, c, tri_bf, spread_pair, spread_full, head_s, slot):
    raw = dtr_ref[pl.ds(pl.multiple_of(c * L, L), L), :] + bias_ref[...]
    dt = _softplus(raw)
    acs = _tri_dot(tri_bf, dt * a_neg)
    acs3 = _split3(acs)
    both = _dot(jnp.concatenate([_split3(dt), acs3], axis=0), spread_pair)
    parts = (raw, dt, acs, acs.T, both[:L], both[L:], _dot(acs3, spread_full))
    off = 0
    for p in parts:
        head_s[slot, :, off:off + p.shape[1]] = p
        off += p.shape[1]


def _ssd_head_load(head_s, slot):
    out, off = [], 0
    for w in (LANES, LANES, LANES, L, N_PAIR * LANES, N_PAIR * LANES, SSD_HPG * LANES):
        out.append(head_s[slot, :, off:off + w])
        off += w
    return out


def _ssd_fwd(xbc, dtr, bias, alog, dskip, nb, s):
    nc = s // L
    t = nb * s

    def body(x_ref, b_ref, c_ref, dtr_ref, bias_ref, alog_ref, dsk_ref, y_ref, hs_ref, h_scr, head_s):
        lane = lax.broadcasted_iota(jnp.int32, (L, LANES), 1)
        lane1 = lax.broadcasted_iota(jnp.int32, (1, LANES), 1)
        rowc = lax.broadcasted_iota(jnp.int32, (L, 1), 0)
        rr = lax.broadcasted_iota(jnp.int32, (L, L), 0)
        cc = lax.broadcasted_iota(jnp.int32, (L, L), 1)
        causal = rr >= cc
        h_scr[...] = jnp.zeros_like(h_scr)
        dsk = dsk_ref[...]
        spread_pair, spread_full, _, _ = _ssd_tables()
        a_neg = -jnp.exp(alog_ref[...])
        head = functools.partial(_ssd_head, dtr_ref, bias_ref, a_neg, tri_bf=_bf(cc <= rr),
                                 spread_pair=spread_pair, spread_full=spread_full, head_s=head_s)
        head(jnp.int32(0), slot=0)

        def chunk(c, slot):
            r0 = pl.multiple_of(c * L, L)
            head(jnp.minimum(c + 1, nc - 1), slot=1 - slot)
            _, _, acs, acs_t, dt_all, acs_all, acs_full = _ssd_head_load(head_s, slot)
            bb = _bf(b_ref[pl.ds(r0, L), :])
            cb = _bf(c_ref[pl.ds(r0, L), :])
            gm = _dot_nt(cb, bb)
            for pr in range(N_PAIR):
                j0, j1 = 2 * pr, 2 * pr + 1
                ms = []
                for j in (j0, j1):
                    seg = acs_full[:, j * LANES:(j + 1) * LANES] - acs_t[j:j + 1, :]
                    ms.append(_bf(gm * jnp.exp(jnp.where(causal, seg, -1e30))))
                x = x_ref[pl.ds(r0, L), pr * LANES:(pr + 1) * LANES]
                dt_p = dt_all[:, pr * LANES:(pr + 1) * LANES]
                acs_p = acs_all[:, pr * LANES:(pr + 1) * LANES]
                last_p = acs_p[L - 1:L, :]
                d_p = _pair_lanes(dsk[:, j0:j0 + 1], dsk[:, j1:j1 + 1], lane1)
                xd = x * dt_p
                xdb = _bf(xd)
                yds = _dot(jnp.concatenate(ms, axis=0), xdb)
                yd = _pair_lanes(yds[:L], yds[L:], lane)
                hp = h_scr[pr]
                yo = _dot_nt(cb, _bf(hp)) * jnp.exp(acs_p)
                y_ref[pl.ds(r0, L), pr * LANES:(pr + 1) * LANES] = yd + yo + x * d_p
                hs_ref[c, pr] = hp
                sp = _dot_tn(_bf(xd * jnp.exp(last_p - acs_p)), bb)
                cd = jnp.where(rowc < HEAD_DIM, jnp.exp(acs[L - 1:L, j0:j0 + 1]),
                               jnp.exp(acs[L - 1:L, j1:j1 + 1]))
                h_scr[pr] = hp * cd + sp

        def two_chunks(i, _):
            chunk(2 * i, 0)
            chunk(2 * i + 1, 1)
            return 0

        lax.fori_loop(0, nc // 2, two_chunks, 0)

    assert nc % 2 == 0
    grp = lambda b, g: (g, 0, 0)
    return pl.pallas_call(
        body, grid=(nb, SSD_GROUPS),
        in_specs=[pl.BlockSpec((s, 4 * LANES), lambda b, g: (b, g)),
                  pl.BlockSpec((s, LANES), lambda b, g: (b, 16 + g)),
                  pl.BlockSpec((s, LANES), lambda b, g: (b, 20 + g)),
                  pl.BlockSpec((None, s, LANES), lambda b, g: (g, b, 0)),
                  pl.BlockSpec((None, 1, LANES), grp), pl.BlockSpec((None, 1, LANES), grp),
                  pl.BlockSpec((None, 1, LANES), grp)],
        out_specs=[pl.BlockSpec((s, 4 * LANES), lambda b, g: (b, g)),
                   pl.BlockSpec((None, None, nc, N_PAIR, L, LANES), lambda b, g: (b, g, 0, 0, 0, 0))],
        out_shape=[jax.ShapeDtypeStruct((t, SSD_WIDTH), F32),
                   jax.ShapeDtypeStruct((nb, SSD_GROUPS, nc, N_PAIR, L, LANES), F32)],
        scratch_shapes=[pltpu.VMEM((N_PAIR, L, LANES), F32), pltpu.VMEM((2, L, HEAD_W), F32)],
        compiler_params=_params(("parallel", "parallel")), name="ssd_fwd")(
            xbc, xbc, xbc, dtr, bias, alog, dskip)


def _ssd_bwd(xbc, dtr, bias, alog, dskip, hs, dy, nb, s):
    nc = s // L
    t = nb * s

    def body(x_ref, b_ref, c_ref, dtr_ref, bias_ref, alog_ref, dsk_ref, hs_ref, dy_ref,
             dx_ref, db_ref, dc_ref, ddtr_ref, dbias_ref, dalog_ref, ddsk_ref, dh_scr, head_s):
        lane = lax.broadcasted_iota(jnp.int32, (L, LANES), 1)
        lane1 = lax.broadcasted_iota(jnp.int32, (1, LANES), 1)
        rowc = lax.broadcasted_iota(jnp.int32, (L, 1), 0)
        rr = lax.broadcasted_iota(jnp.int32, (L, L), 0)
        cc = lax.broadcasted_iota(jnp.int32, (L, L), 1)
        causal = rr >= cc
        tri_t = _bf(cc >= rr)
        dh_scr[...] = jnp.zeros_like(dh_scr)
        dsk = dsk_ref[...]
        a_neg = -jnp.exp(alog_ref[...])

        @pl.when(pl.program_id(1) == 0)
        def _():
            dbias_ref[...] = jnp.zeros_like(dbias_ref)
            dalog_ref[...] = jnp.zeros_like(dalog_ref)
            ddsk_ref[...] = jnp.zeros_like(ddsk_ref)

        spread_pair, spread_full, sum_pair, sum_full = _ssd_tables()
        sum_pair2 = jnp.concatenate([sum_pair] * 2, axis=0)
        sum_full2 = jnp.concatenate([sum_full] * 2, axis=0)
        head = functools.partial(_ssd_head, dtr_ref, bias_ref, a_neg, tri_bf=_bf(cc <= rr),
                                 spread_pair=spread_pair, spread_full=spread_full, head_s=head_s)
        head(jnp.int32(nc - 1), slot=0)

        def chunk(c, slot):
            r0 = pl.multiple_of(c * L, L)
            head(jnp.maximum(c - 1, 0), slot=1 - slot)
            raw, dt, acs, acs_t, dt_all, acs_all, acs_full = _ssd_head_load(head_s, slot)
            bb = _bf(b_ref[pl.ds(r0, L), :])
            cb = _bf(c_ref[pl.ds(r0, L), :])
            gm = _dot_nt(cb, bb)
            dg = jnp.zeros((L, L), F32)
            dbacc = jnp.zeros((L, LANES), F32)
            dcacc = jnp.zeros((L, LANES), F32)
            dacs_t = jnp.zeros((L, L), F32)
            ta_all, te_all, qm_all, tc_all, td_all, dcd_all = [], [], [], [], [], []
            for pr in range(N_PAIR):
                js = (2 * pr, 2 * pr + 1)
                lms, mfs = [], []
                for j in js:
                    seg = acs_full[:, j * LANES:(j + 1) * LANES] - acs_t[j:j + 1, :]
                    lm = jnp.exp(jnp.where(causal, seg, -1e30))
                    lms.append(lm)
                    mfs.append(gm * lm)
                x = x_ref[pl.ds(r0, L), pr * LANES:(pr + 1) * LANES]
                dyv = dy_ref[pl.ds(r0, L), pr * LANES:(pr + 1) * LANES]
                dt_p = dt_all[:, pr * LANES:(pr + 1) * LANES]
                acs_p = acs_all[:, pr * LANES:(pr + 1) * LANES]
                last_p = acs_p[L - 1:L, :]
                d_p = _pair_lanes(dsk[:, js[0]:js[0] + 1], dsk[:, js[1]:js[1] + 1], lane1)
                e_p = jnp.exp(acs_p)
                dte_p = jnp.exp(last_p - acs_p)
                xd = x * dt_p
                xdb = _bf(xd)
                hp = hs_ref[c, pr]
                hb = _bf(hp)
                dye = _bf(dyv * e_p)
                yo = _dot_nt(cb, hb) * e_p
                dcacc = dcacc + _dot(dye, hb)
                dh_out = _dot_tn(dye, cb)
                dyb = _bf(dyv)
                dms = _dot_nt(_bf(jnp.concatenate(
                    [jnp.where(lane < HEAD_DIM, dyv, 0.0), jnp.where(lane < HEAD_DIM, 0.0, dyv)], axis=0)), xdb)
                mty = _dot_tn(_bf(jnp.concatenate(mfs, axis=1)), dyb)
                dxd = jnp.where(lane < HEAD_DIM, mty[:L], mty[L:])
                for hh in range(2):
                    dm = dms[hh * L:(hh + 1) * L]
                    dg = dg + dm * lms[hh]
                    qm = dm * mfs[hh]
                    qm_all.append(qm)
                    dacs_t = dacs_t - jnp.where(rr == js[hh], jnp.sum(qm, axis=0, keepdims=True), 0.0)
                ds = dh_scr[pr]
                dsb = _bf(ds)
                xdd = xd * dte_p
                dxdd = _dot_nt(bb, dsb)
                dbacc = dbacc + _dot(_bf(xdd), dsb)
                dxd = dxd + dxdd * dte_p
                cd = jnp.where(rowc < HEAD_DIM, jnp.exp(acs[L - 1:L, js[0]:js[0] + 1]),
                               jnp.exp(acs[L - 1:L, js[1]:js[1] + 1]))
                prod = ds * hp
                dcd_all += [jnp.sum(prod[:HEAD_DIM], axis=0, keepdims=True),
                            jnp.sum(prod[HEAD_DIM:], axis=0, keepdims=True)]
                dh_scr[pr] = dh_out + ds * cd
                tcv = dxdd * xdd
                ta_all.append(dyv * yo - tcv)
                tc_all.append(jnp.sum(tcv, axis=0, keepdims=True))
                te_all.append(dxd * x)
                td_all.append(jnp.sum(dyv * x, axis=0, keepdims=True))
                dx_ref[pl.ds(r0, L), pr * LANES:(pr + 1) * LANES] = dxd * dt_p + dyv * d_p
            dgb = _bf(dg)
            dc_ref[pl.ds(r0, L), :] = dcacc + _dot(dgb, bb)
            db_ref[pl.ds(r0, L), :] = dbacc + _dot_tn(dgb, cb)
            last_add = jnp.zeros((1, LANES), F32)
            dd_add = jnp.zeros((1, LANES), F32)
            cd_row = jnp.exp(acs[L - 1:L, :])
            for j in range(SSD_HPG):
                hm = (lane1 >= HEAD_DIM) if j % 2 else (lane1 < HEAD_DIM)
                tcj = jnp.sum(jnp.where(hm, tc_all[j // 2], 0.0), axis=1, keepdims=True)
                tdj = jnp.sum(jnp.where(hm, td_all[j // 2], 0.0), axis=1, keepdims=True)
                dcdj = jnp.sum(dcd_all[j], axis=1, keepdims=True)
                last_add = last_add + jnp.where(lane1 == j, tcj + dcdj * cd_row, 0.0)
                dd_add = dd_add + jnp.where(lane1 == j, tdj, 0.0)
            dacs = (_dot(_split2(jnp.concatenate(ta_all, axis=1)), sum_pair2)
                    + _dot(_split2(jnp.concatenate(qm_all, axis=1)), sum_full2)
                    + dacs_t.T + jnp.where(rowc == L - 1, last_add, 0.0))
            ddta = _tri_dot(tri_t, dacs)
            ddt = _dot(_bf(jnp.concatenate(te_all, axis=1)), sum_pair) + ddta * a_neg
            ddraw = ddt * _sigmoid(raw)
            ddtr_ref[pl.ds(r0, L), :] = ddraw
            dbias_ref[...] += jnp.sum(ddraw, axis=0, keepdims=True)
            dalog_ref[...] += jnp.sum(ddta * dt, axis=0, keepdims=True) * a_neg
            ddsk_ref[...] += dd_add

        def two_chunks(i, _):
            chunk(nc - 1 - 2 * i, 0)
            chunk(nc - 2 - 2 * i, 1)
            return 0

        lax.fori_loop(0, nc // 2, two_chunks, 0)

    assert nc % 2 == 0
    grp = lambda g, b: (g, 0, 0)
    small = jax.ShapeDtypeStruct((SSD_GROUPS, 1, LANES), F32)
    return pl.pallas_call(
        body, grid=(SSD_GROUPS, nb),
        in_specs=[pl.BlockSpec((s, 4 * LANES), lambda g, b: (b, g)),
                  pl.BlockSpec((s, LANES), lambda g, b: (b, 16 + g)),
                  pl.BlockSpec((s, LANES), lambda g, b: (b, 20 + g)),
                  pl.BlockSpec((None, s, LANES), lambda g, b: (g, b, 0)),
                  pl.BlockSpec((None, 1, LANES), grp), pl.BlockSpec((None, 1, LANES), grp),
                  pl.BlockSpec((None, 1, LANES), grp),
                  pl.BlockSpec((None, None, nc, N_PAIR, L, LANES), lambda g, b: (b, g, 0, 0, 0, 0)),
                  pl.BlockSpec((s, 4 * LANES), lambda g, b: (b, g))],
        out_specs=[pl.BlockSpec((s, 4 * LANES), lambda g, b: (b, g)),
                   pl.BlockSpec((s, LANES), lambda g, b: (b, g)),
                   pl.BlockSpec((s, LANES), lambda g, b: (b, g)),
                   pl.BlockSpec((None, s, LANES), lambda g, b: (g, b, 0)),
                   pl.BlockSpec((None, 1, LANES), grp), pl.BlockSpec((None, 1, LANES), grp),
                   pl.BlockSpec((None, 1, LANES), grp)],
        out_shape=[jax.ShapeDtypeStruct((t, SSD_WIDTH), F32),
                   jax.ShapeDtypeStruct((t, SSD_GROUPS * SSD_STATE), F32),
                   jax.ShapeDtypeStruct((t, SSD_GROUPS * SSD_STATE), F32),
                   jax.ShapeDtypeStruct((SSD_GROUPS, t, LANES), F32), small, small, small],
        scratch_shapes=[pltpu.VMEM((N_PAIR, L, LANES), F32), pltpu.VMEM((2, L, HEAD_W), F32)],
        compiler_params=_params(("parallel", "arbitrary")), name="ssd_bwd")(
            xbc, xbc, xbc, dtr, bias, alog, dskip, hs, dy)


GN = SSD_WIDTH // SSD_GROUPS


def _post_fwd(o, proj, y, nw, *, tm=256):
    t = o.shape[0]

    def body(o_ref, za_ref, y_ref, zs_ref, nw_ref, ya_ref, ys_ref):
        za = za_ref[...].astype(F32)
        ya_ref[...] = _bf(o_ref[...] * (za * _sigmoid(za)))
        zs = zs_ref[...].astype(F32)
        yz = y_ref[...] * (zs * _sigmoid(zs))
        for g in range(SSD_GROUPS):
            sl = slice(g * GN, (g + 1) * GN)
            v = yz[:, sl]
            r = lax.rsqrt(jnp.mean(v * v, axis=-1, keepdims=True) + EPS)
            ys_ref[:, sl] = _bf(v * r * nw_ref[:, sl])

    return pl.pallas_call(
        body, grid=(t // tm,),
        in_specs=[pl.BlockSpec((tm, SB_WIDTH), lambda i: (i, 0)),
                  pl.BlockSpec((tm, SB_WIDTH), lambda i: (i, CB_ZA * LANES // SB_WIDTH)),
                  pl.BlockSpec((tm, SSD_WIDTH), lambda i: (i, 0)),
                  pl.BlockSpec((tm, SSD_WIDTH), lambda i: (i, CB_ZS * LANES // SSD_WIDTH)),
                  pl.BlockSpec((1, SSD_WIDTH), lambda i: (0, 0))],
        out_specs=[pl.BlockSpec((tm, SB_WIDTH), lambda i: (i, 0)),
                   pl.BlockSpec((tm, SSD_WIDTH), lambda i: (i, 0))],
        out_shape=[jax.ShapeDtypeStruct((t, SB_WIDTH), BF16), jax.ShapeDtypeStruct((t, SSD_WIDTH), BF16)],
        compiler_params=_params(("parallel",)), name="post_fwd")(o, proj, y, proj, nw)


def _post_bwd(dya, o, proj, dys, y, nw, dproj, *, tm=256):
    t = o.shape[0]
    zw = SB_WIDTH + SSD_WIDTH

    def body(dya_ref, o_ref, za_ref, dys_ref, y_ref, zs_ref, nw_ref, _, do_ref, dy_ref, dnw_ref, dz_ref):
        @pl.when(pl.program_id(0) == 0)
        def _():
            dnw_ref[...] = jnp.zeros_like(dnw_ref)

        za = za_ref[...].astype(F32)
        sa = _sigmoid(za)
        d = dya_ref[...].astype(F32)
        do_ref[...] = _bf(d * (za * sa))
        dz_ref[:, :SB_WIDTH] = _bf(d * o_ref[...] * (sa * (1.0 + za * (1.0 - sa))))
        zs = zs_ref[...].astype(F32)
        ss = _sigmoid(zs)
        silu = zs * ss
        yv = y_ref[...]
        yz = yv * silu
        dv = dys_ref[...].astype(F32)
        for g in range(SSD_GROUPS):
            sl = slice(g * GN, (g + 1) * GN)
            v = yz[:, sl]
            r = lax.rsqrt(jnp.mean(v * v, axis=-1, keepdims=True) + EPS)
            nrm = v * r
            dg = dv[:, sl]
            dnw_ref[:, sl] += jnp.sum(dg * nrm, axis=0, keepdims=True)
            dn = dg * nw_ref[:, sl]
            dyz = r * (dn - nrm * jnp.mean(dn * nrm, axis=-1, keepdims=True))
            dy_ref[:, sl] = dyz * silu[:, sl]
            dz_ref[:, SB_WIDTH + g * GN:SB_WIDTH + (g + 1) * GN] = _bf(
                dyz * yv[:, sl] * (ss[:, sl] * (1.0 + zs[:, sl] * (1.0 - ss[:, sl]))))

    a_spec = pl.BlockSpec((tm, SB_WIDTH), lambda i: (i, 0))
    s_spec = pl.BlockSpec((tm, SSD_WIDTH), lambda i: (i, 0))
    w_spec = pl.BlockSpec((1, SSD_WIDTH), lambda i: (0, 0))
    return pl.pallas_call(
        body, grid=(t // tm,),
        in_specs=[a_spec, a_spec, pl.BlockSpec((tm, SB_WIDTH), lambda i: (i, CB_ZA * LANES // SB_WIDTH)),
                  s_spec, s_spec, pl.BlockSpec((tm, SSD_WIDTH), lambda i: (i, CB_ZS * LANES // SSD_WIDTH)),
                  w_spec, pl.BlockSpec(memory_space=pl.ANY)],
        out_specs=[a_spec, s_spec, w_spec, pl.BlockSpec((tm, zw), lambda i: (i, 1))],
        out_shape=[jax.ShapeDtypeStruct((t, SB_WIDTH), BF16), jax.ShapeDtypeStruct((t, SSD_WIDTH), F32),
                   jax.ShapeDtypeStruct((1, SSD_WIDTH), F32), jax.ShapeDtypeStruct(dproj.shape, dproj.dtype)],
        input_output_aliases={7: 3},
        compiler_params=_params(("arbitrary",)), name="post_bwd")(dya, o, proj, dys, y, proj, nw, dproj)


def _merge_fwd(ya, ys, proj_g, *, tm=512):
    t = ya.shape[0]

    def body(ya_ref, ys_ref, ga_ref, gs_ref, m_ref):
        m_ref[...] = _bf(_sigmoid(ga_ref[...].astype(F32)) * ya_ref[...].astype(F32)
                         + _sigmoid(gs_ref[...].astype(F32)) * ys_ref[...].astype(F32))

    spec = pl.BlockSpec((tm, D_MODEL), lambda i: (i, 0))
    return pl.pallas_call(
        body, grid=(t // tm,),
        in_specs=[spec, spec, pl.BlockSpec((tm, D_MODEL), lambda i: (i, 0)),
                  pl.BlockSpec((tm, D_MODEL), lambda i: (i, 1))],
        out_specs=spec, out_shape=jax.ShapeDtypeStruct((t, D_MODEL), BF16),
        compiler_params=_params(("parallel",)), name="merge_fwd")(ya, ys, proj_g, proj_g)


def _merge_bwd(dm, ya, ys, proj_g, *, tm=512):
    t = ya.shape[0]

    def body(dm_ref, ya_ref, ys_ref, ga_ref, gs_ref, dya_ref, dys_ref, dg_ref):
        d = dm_ref[...].astype(F32)
        ga = _sigmoid(ga_ref[...].astype(F32))
        gs = _sigmoid(gs_ref[...].astype(F32))
        dya_ref[...] = _bf(d * ga)
        dys_ref[...] = _bf(d * gs)
        dg_ref[:, :D_MODEL] = _bf(d * ya_ref[...].astype(F32) * (ga * (1.0 - ga)))
        dg_ref[:, D_MODEL:] = _bf(d * ys_ref[...].astype(F32) * (gs * (1.0 - gs)))

    spec = pl.BlockSpec((tm, D_MODEL), lambda i: (i, 0))
    return pl.pallas_call(
        body, grid=(t // tm,),
        in_specs=[spec, spec, spec, pl.BlockSpec((tm, D_MODEL), lambda i: (i, 0)),
                  pl.BlockSpec((tm, D_MODEL), lambda i: (i, 1))],
        out_specs=[spec, spec, pl.BlockSpec((tm, 2 * D_MODEL), lambda i: (i, 0))],
        out_shape=[jax.ShapeDtypeStruct((t, D_MODEL), BF16), jax.ShapeDtypeStruct((t, D_MODEL), BF16),
                   jax.ShapeDtypeStruct((t, 2 * D_MODEL), BF16)],
        compiler_params=_params(("parallel",)), name="merge_bwd")(dm, ya, ys, proj_g, proj_g)


def _final(x, out, wf, target, *, tm=512):
    t = x.shape[0]

    def body(x_ref, o_ref, w_ref, t_ref, loss_ref, dx_ref, dw_ref):
        @pl.when(pl.program_id(0) == 0)
        def _():
            loss_ref[...] = jnp.zeros_like(loss_ref)
            dw_ref[...] = jnp.zeros_like(dw_ref)

        x2 = x_ref[...] + o_ref[...]
        r = lax.rsqrt(jnp.mean(x2 * x2, axis=-1, keepdims=True) + EPS)
        nrm = x2 * r
        e = nrm * w_ref[...] - t_ref[...]
        row_loss = jnp.mean(e * e, axis=-1, keepdims=True)
        loss_ref[...] += 0.5 * jnp.sum(row_loss, axis=0, keepdims=True)
        dyv = e * (1.0 / D_MODEL)
        dw_ref[...] += jnp.sum(dyv * nrm, axis=0, keepdims=True)
        dn = dyv * w_ref[...]
        dx_ref[...] = r * (dn - nrm * jnp.mean(dn * nrm, axis=-1, keepdims=True))

    spec = pl.BlockSpec((tm, D_MODEL), lambda i: (i, 0))
    fix = pl.BlockSpec((1, D_MODEL), lambda i: (0, 0))
    return pl.pallas_call(
        body, grid=(t // tm,),
        in_specs=[spec, spec, fix, spec],
        out_specs=[pl.BlockSpec((1, LANES), lambda i: (0, 0)), spec, fix],
        out_shape=[jax.ShapeDtypeStruct((1, LANES), F32), jax.ShapeDtypeStruct((t, D_MODEL), F32),
                   jax.ShapeDtypeStruct((1, D_MODEL), F32)],
        compiler_params=_params(("arbitrary",)), name="final_loss")(x, out, wf, target)


def _group_lanes(v):
    return jnp.pad(v.reshape(SSD_GROUPS, 1, SSD_HPG), ((0, 0), (0, 0), (0, LANES - SSD_HPG)))


def _ungroup_lanes(v):
    return v[:, :, :SSD_HPG].reshape(1, N_DT)


def _row_slabs(pieces, rows):
    total = sum(p.shape[0] for p in pieces)
    assert total % rows == 0
    slabs = []
    for lo in range(0, total, rows):
        parts, off = [], 0
        for p in pieces:
            a, b = max(lo, off), min(lo + rows, off + p.shape[0])
            if a < b:
                parts.append(p[a - off:b - off])
            off += p.shape[0]
        slabs.append(parts[0] if len(parts) == 1 else jnp.concatenate(parts, axis=0))
    return jnp.stack(slabs)


def _local_step(x, target, nb, s, w_in_t, conv_w, conv_b, norm_w, dt_bias, a_log, d_skip,
                ssm_norm_w, out_weights, final_norm_w, between=None):
    t = nb * s
    w_g_t = w_in_t[GATE_COL0:]
    w_dt_t = jnp.pad(w_in_t[DT_COL0:GATE_COL0], ((0, LANES - N_DT), (0, 0)))
    h, r1 = _rms_fwd(x, norm_w)
    proj = _matmul(h, w_in_t, nt=True, n=DT_COL0, out_dtype=BF16, name="proj_main", tm=MM_TALL)
    proj_g = _matmul(h, w_g_t, nt=True, out_dtype=BF16, name="proj_gate", tm=MM_TALL)
    dt_raw = _matmul(h, w_dt_t, nt=True, name="proj_dt")
    dtr = jnp.pad(dt_raw[:, :N_DT].reshape(t, SSD_GROUPS, SSD_HPG).transpose(1, 0, 2),
                  ((0, 0), (0, 0), (0, LANES - SSD_HPG)))
    bias_g, alog_g, dsk_g = _group_lanes(dt_bias), _group_lanes(a_log), _group_lanes(d_skip)

    o, att_a, att_sg = _attn_fwd(proj, nb, s)
    xbc = _conv_fwd(proj, conv_w, conv_b, nb, s)
    y, hs = _ssd_fwd(xbc, dtr, bias_g, alog_g, dsk_g, nb, s)
    ya_in, ys_in = _post_fwd(o, proj, y, ssm_norm_w)
    w_attn_out, w_ssm_out, w_o = out_weights(ys_in) if callable(out_weights) else out_weights
    ya = _matmul(ya_in, w_attn_out, out_dtype=BF16, name="attn_out")
    ys = _matmul(ys_in, w_ssm_out, out_dtype=BF16, name="ssm_out")
    merged = _merge_fwd(ya, ys, proj_g)
    out = _matmul(merged, w_o, name="out_proj")
    loss, dx2, d_final_w = _final(x, out, final_norm_w, target)

    dx2b = _bf(dx2)
    dmerged = _matmul(dx2b, w_o, nt=True, out_dtype=BF16, name="d_merged")
    d_w_o = _matmul(merged, dx2b, ta=True, name="d_w_o")
    dya, dys, dgate = _merge_bwd(dmerged, ya, ys, proj_g)
    d_w_attn_out = _matmul(ya_in, dya, ta=True, name="d_w_attn_out")
    d_w_ssm_out = _matmul(ys_in, dys, ta=True, name="d_w_ssm_out")
    dya_in = _matmul(dya, w_attn_out, nt=True, out_dtype=BF16, name="d_ya_in")
    dys_in = _matmul(dys, w_ssm_out, nt=True, name="d_ys_in")
    do, dy, d_ssm_norm_w, dproj = _post_bwd(dya_in, o, proj, dys_in, y, ssm_norm_w, lax.empty(proj.shape, BF16))
    dq, dk, dv = _attn_bwd(proj, do, att_a, att_sg, nb, s)
    dxs_act, db_act, dc_act, ddtr, dbias_g, dalog_g, ddsk_g = _ssd_bwd(
        xbc, dtr, bias_g, alog_g, dsk_g, hs, dy, nb, s)
    dproj, dwx, dbx = _conv_bwd(proj, dxs_act, conv_w, conv_b, dproj, nb, s, col0=0, name="conv_bwd_x")
    dproj, dwb, dbb = _conv_bwd(proj, db_act, conv_w, conv_b, dproj, nb, s, col0=SSD_WIDTH, name="conv_bwd_b")
    dproj, dwc, dbc = _conv_bwd(proj, dc_act, conv_w, conv_b, dproj, nb, s, col0=SSD_WIDTH + 512,
                                name="conv_bwd_c")
    for i, piece in enumerate((dq, dk, dv)):
        dproj = lax.dynamic_update_slice(dproj, piece, (0, i * SB_WIDTH))
    ddt = jnp.pad(_bf(ddtr[:, :, :SSD_HPG].transpose(1, 0, 2).reshape(t, N_DT)), ((0, 0), (0, LANES - N_DT)))
    wt = dict(ta=True, out_dtype=BF16)
    d_w_in_t = _row_slabs(
        [_matmul(dproj, h, name="d_w_main", **wt), _matmul(ddt, h, name="d_w_dt", **wt)[:N_DT],
         _matmul(dgate, h, name="d_w_gate", **wt)], SHARD)
    grads = dict(
        w_in_t=d_w_in_t,
        conv_w=jnp.concatenate([dwx, dwb, dwc], axis=1), conv_b=jnp.concatenate([dbx, dbb, dbc], axis=1),
        dt_bias=_ungroup_lanes(dbias_g), a_log=_ungroup_lanes(dalog_g), d_skip=_ungroup_lanes(ddsk_g),
        ssm_norm_w=d_ssm_norm_w, w_attn_out=d_w_attn_out, w_ssm_out=d_w_ssm_out, w_o=d_w_o,
        final_norm_w=d_final_w)
    extra = None
    if between is not None:
        zero, extra = between(grads)
        ddt = ddt + _bf(zero[0:1, 0:1])

    dh = _matmul(dproj, w_in_t, more=[(dgate, w_g_t), (ddt, w_dt_t)], name="d_h", tk=DT_COL0 // 4)
    grad_x, grads["norm_w"] = _rms_bwd(dh, x, r1, norm_w, dx2)
    return loss, grad_x, grads, extra


N_CHIP = N_DEV // 2


def _pair_exchange(scatter, bcast, *, name):
    arrays = list(scatter) + list(bcast)
    ns, n = len(scatter), len(arrays)
    n_sem = ns * N_CHIP + (n - ns) * (N_DEV - 1)

    def body(*refs):
        ins, outs = refs[:n], refs[n:2 * n]
        send_sems, recv_sems, local_sems = refs[2 * n:]
        x, y, c = lax.axis_index("x"), lax.axis_index("y"), lax.axis_index("c")
        me = 4 * x + 2 * y + c
        copies, k = [], 0
        for a in range(ns):
            for q in range(N_CHIP):
                cp = pltpu.make_async_remote_copy(
                    src_ref=ins[a].at[2 * q + 1 - c], dst_ref=outs[a].at[q],
                    send_sem=send_sems.at[k], recv_sem=recv_sems.at[k],
                    device_id=(x, y, 1 - c), device_id_type=pl.DeviceIdType.MESH)
                cp.start()
                copies.append(cp)
                k += 1
        for a in range(ns, n):
            cp = pltpu.make_async_copy(ins[a], outs[a].at[me], local_sems.at[a - ns])
            cp.start()
            copies.append(cp)
            for r in range(1, N_DEV):
                tx = 1 - x if (r >> 2) & 1 else x
                ty = 1 - y if (r >> 1) & 1 else y
                tc = 1 - c if r & 1 else c
                cp = pltpu.make_async_remote_copy(
                    src_ref=ins[a], dst_ref=outs[a].at[me], send_sem=send_sems.at[k], recv_sem=recv_sems.at[k],
                    device_id=(tx, ty, tc), device_id_type=pl.DeviceIdType.MESH)
                cp.start()
                copies.append(cp)
                k += 1
        for cp in copies:
            cp.wait()

    out_shape = [jax.ShapeDtypeStruct((N_CHIP,) + v.shape[1:] if i < ns else (N_DEV,) + v.shape, v.dtype)
                 for i, v in enumerate(arrays)]
    return pl.pallas_call(
        body, in_specs=[pl.BlockSpec(memory_space=pl.ANY)] * n,
        out_specs=[pl.BlockSpec(memory_space=pl.ANY)] * n, out_shape=out_shape,
        scratch_shapes=[pltpu.SemaphoreType.DMA((n_sem,)), pltpu.SemaphoreType.DMA((n_sem,)),
                        pltpu.SemaphoreType.DMA((max(n - ns, 1),))],
        compiler_params=pltpu.CompilerParams(has_side_effects=True), name=name)(*arrays)


def _pair_add(mine, recv, core, *, name, tc=None):
    _, r, c = recv.shape
    tc = c if tc is None else tc
    assert c % tc == 0

    def body(core_ref, m_ref, r_ref, o_ref):
        o_ref[...] = (m_ref[...].astype(F32) + r_ref[...].astype(F32)).astype(o_ref.dtype)

    spec = pl.BlockSpec((None, r, tc), lambda q, i, core_ref: (q, 0, i))
    return pl.pallas_call(
        body, grid_spec=pltpu.PrefetchScalarGridSpec(
            num_scalar_prefetch=1, grid=(N_CHIP, c // tc),
            in_specs=[pl.BlockSpec((None, r, tc), lambda q, i, core_ref: (2 * q + core_ref[0], 0, i)), spec],
            out_specs=spec),
        out_shape=jax.ShapeDtypeStruct(recv.shape, recv.dtype),
        compiler_params=_params(("parallel", "parallel")), name=name)(core, mine, recv)


_HBM = pl.BlockSpec(memory_space=pltpu.HBM)
_SEM = pl.BlockSpec(memory_space=pltpu.SEMAPHORE)
_DATAFLOW = pltpu.SideEffectType.DATAFLOW_SIDE_EFFECTING
_PEERS = {"chips": N_CHIP - 1, "all": N_DEV - 1}


def _split_copies(pattern, srcs, lands, send_sems, recv_sems):
    x, y, c = lax.axis_index("x"), lax.axis_index("y"), lax.axis_index("c")
    per = _PEERS[pattern]
    copies = []
    for a in range(len(srcs)):
        for r in range(1, per + 1):
            if pattern == "chips":
                tx = 1 - x if (r >> 1) & 1 else x
                ty = 1 - y if r & 1 else y
                tc = c
                src, dst = srcs[a].at[2 * tx + ty], lands[a].at[2 * x + y]
            else:
                tx = 1 - x if (r >> 2) & 1 else x
                ty = 1 - y if (r >> 1) & 1 else y
                tc = 1 - c if r & 1 else c
                src, dst = srcs[a], lands[a].at[4 * x + 2 * y + c]
            copies.append(pltpu.make_async_remote_copy(
                src_ref=src, dst_ref=dst, send_sem=send_sems.at[a * per + r - 1],
                recv_sem=recv_sems.at[a * per + r - 1], device_id=(tx, ty, tc), device_id_type=pl.DeviceIdType.MESH))
    return copies


def _split_start(pattern, arrays, *, name):
    n = len(arrays)
    n_sem = n * _PEERS[pattern]

    def body(*refs):
        srcs, lands = refs[:n], refs[n:2 * n]
        send_sems, recv_sems = refs[2 * n], refs[2 * n + 1]
        zero = refs[-1]
        for cp in _split_copies(pattern, srcs, lands, send_sems, recv_sems):
            cp.start()
        zero[...] = jnp.zeros_like(zero)

    src_t = [pltpu.HBM(v.shape, v.dtype) for v in arrays]
    land_t = [pltpu.HBM(v.shape if pattern == "chips" else (N_DEV,) + v.shape, v.dtype) for v in arrays]
    outs = pl.pallas_call(
        body, name=name,
        out_shape=(pltpu.SemaphoreType.DMA((n_sem,)), pltpu.SemaphoreType.DMA((n_sem,)), *src_t, *land_t,
                   jax.ShapeDtypeStruct((8, LANES), F32)),
        in_specs=[_HBM] * (2 * n),
        out_specs=(_SEM, _SEM, *[_HBM] * (2 * n), pl.BlockSpec(memory_space=pltpu.VMEM)),
        input_output_aliases={i: 2 + i for i in range(2 * n)},
        compiler_params=pltpu.CompilerParams(has_side_effects=_DATAFLOW))(
            *[pltpu.with_memory_space_constraint(v, pltpu.HBM) for v in arrays],
            *[pltpu.with_memory_space_constraint(lax.empty(t.shape, t.dtype), pltpu.HBM) for t in land_t])
    return outs[0], outs[1], outs[2:2 + n], outs[2 + n:2 + 2 * n], outs[-1]


def _split_wait(pattern, send_sems, recv_sems, srcs, lands, after, *, name):
    n = len(srcs)

    def body(*refs):
        for cp in _split_copies(pattern, refs[:n], refs[n:2 * n], refs[2 * n], refs[2 * n + 1]):
            cp.wait_send()
            cp.wait_recv()

    outs = pl.pallas_call(
        body, name=name, out_shape=[pltpu.HBM(v.shape, v.dtype) for v in (*srcs, *lands)],
        in_specs=[_HBM] * (2 * n) + [_SEM, _SEM, pl.BlockSpec(memory_space=pl.ANY)],
        out_specs=[_HBM] * (2 * n), input_output_aliases={i: i for i in range(2 * n)},
        compiler_params=pltpu.CompilerParams(has_side_effects=_DATAFLOW))(
            *srcs, *lands, send_sems, recv_sems, after)
    return outs[:n], outs[n:]


def _own_slab(lands, srcs, index, whole):
    out = []
    for land, src in zip(lands, srcs):
        own = src[None] if whole else lax.dynamic_slice(src, (index, 0, 0), (1,) + src.shape[1:])
        out.append(lax.dynamic_update_slice(land, own, (index, 0, 0)))
    return out


def _gather(arrays, *, name):
    n = len(arrays)
    per = N_DEV - 1

    def body(*refs):
        ins, outs = refs[:n], refs[n:2 * n]
        send_sems, recv_sems, local_sems = refs[2 * n:]
        x, y, c = lax.axis_index("x"), lax.axis_index("y"), lax.axis_index("c")
        me, sibling = (x, y, c), (x, y, 1 - c)
        chips = [(1 - x, y), (x, 1 - y), (1 - x, 1 - y)]

        def slab(a, block):
            return outs[a].at[4 * block[0] + 2 * block[1] + block[2]]

        def copy(a, k, block, to, src=None):
            rows = slab(a, block)
            return pltpu.make_async_remote_copy(
                src_ref=rows if src is None else src, dst_ref=rows,
                send_sem=send_sems.at[a * per + k], recv_sem=recv_sems.at[a * per + k],
                device_id=to, device_id_type=pl.DeviceIdType.MESH)

        started = []
        for a in range(n):
            mine = pltpu.make_async_copy(ins[a], slab(a, me), local_sems.at[a])
            mine.start()
            started.append(mine)
        for a in range(n):
            first = [copy(a, 1 + j, me, (*chip, c), src=ins[a]) for j, chip in enumerate(chips)]
            first.append(copy(a, 0, me, sibling, src=ins[a]))
            for cp in first:
                cp.start()
            started += first
        for j, chip in enumerate(chips):
            for a in range(n):
                copy(a, 1 + j, (*chip, c), me).wait_recv()
                passed = copy(a, 4 + j, (*chip, c), sibling)
                passed.start()
                started.append(passed)
        for a in range(n):
            copy(a, 0, sibling, me).wait_recv()
            for j, chip in enumerate(chips):
                copy(a, 4 + j, (*chip, 1 - c), me).wait_recv()
        for cp in started[:n]:
            cp.wait()
        for cp in started[n:]:
            cp.wait_send()

    return pl.pallas_call(
        body, in_specs=[pl.BlockSpec(memory_space=pl.ANY)] * n,
        out_specs=[pl.BlockSpec(memory_space=pl.ANY)] * n,
        out_shape=[jax.ShapeDtypeStruct((N_DEV,) + v.shape, v.dtype) for v in arrays],
        scratch_shapes=[pltpu.SemaphoreType.DMA((n * per,)), pltpu.SemaphoreType.DMA((n * per,)),
                        pltpu.SemaphoreType.DMA((n,))],
        compiler_params=pltpu.CompilerParams(has_side_effects=True), name=name)(*arrays)


def _adamw(parts, w, m, v, *, name, tc=None):
    r, c = w.shape
    n_parts = parts.shape[0]
    tc = c if tc is None else tc
    assert c % tc == 0

    def body(p_ref, w_ref, m_ref, v_ref, g_ref, d_ref, nm_ref, nv_ref):
        g = p_ref[0].astype(F32)
        for d in range(1, n_parts):
            g = g + p_ref[d].astype(F32)
        mm = ADAM_B1 * m_ref[...] + (1.0 - ADAM_B1) * g
        vv = ADAM_B2 * v_ref[...] + (1.0 - ADAM_B2) * (g * g)
        m_hat = mm / (1.0 - ADAM_B1 ** ADAM_STEP)
        v_hat = vv / (1.0 - ADAM_B2 ** ADAM_STEP)
        g_ref[...] = g
        d_ref[...] = -ADAM_LR * (m_hat / (jnp.sqrt(v_hat) + ADAM_EPS) + ADAM_WD * w_ref[...])
        nm_ref[...] = mm
        nv_ref[...] = vv

    spec = pl.BlockSpec((r, tc), lambda i: (0, i))
    out = jax.ShapeDtypeStruct((r, c), F32)
    return pl.pallas_call(
        body, grid=(c // tc,),
        in_specs=[pl.BlockSpec((n_parts, r, tc), lambda i: (0, 0, i)), spec, spec, spec],
        out_specs=[spec] * 4, out_shape=[out] * 4,
        compiler_params=_params(("parallel",)), name=name)(parts, w, m, v)


SMALL = (("norm_w", D_MODEL), ("conv_b", CONV_DIM), ("dt_bias", N_DT), ("a_log", N_DT),
         ("d_skip", N_DT), ("ssm_norm_w", SSD_WIDTH), ("final_norm_w", D_MODEL))
SMALL_USED = sum(size for _, size in SMALL)
SMALL_ROWS = 64
SHARD = D_PROJ // N_DEV


def _pack_small(vals, scalar=None):
    parts = [vals[k].reshape(-1).astype(F32) for k, _ in SMALL]
    if scalar is not None:
        parts.append(scalar.reshape(1))
    flat = jnp.concatenate(parts)
    return jnp.pad(flat, (0, SMALL_ROWS * LANES - flat.shape[0])).reshape(SMALL_ROWS, LANES)


def _unpack_small(packed, shapes):
    flat = packed.reshape(-1)
    out, off = {}, 0
    for k, size in SMALL:
        out[k] = flat[off:off + size].reshape(shapes[k])
        off += size
    return out


def kernel(x, norm_w, w_in, conv_w, conv_b, dt_bias, a_log, d_skip, ssm_norm_w, w_attn_out, w_ssm_out, w_o, final_norm_w, loss_target, m_norm_w, m_w_in, m_conv_w, m_conv_b, m_dt_bias, m_a_log, m_d_skip, m_ssm_norm_w, m_w_attn_out, m_w_ssm_out, m_w_o, m_final_norm_w, v_norm_w, v_w_in, v_conv_w, v_conv_b, v_dt_bias, v_a_log, v_d_skip, v_ssm_norm_w, v_w_attn_out, v_w_ssm_out, v_w_o, v_final_norm_w):
    nb, s, _ = x.shape
    t = nb * s
    weights = dict(norm_w=norm_w, w_in=w_in, conv_w=conv_w, conv_b=conv_b, dt_bias=dt_bias, a_log=a_log,
                   d_skip=d_skip, ssm_norm_w=ssm_norm_w, w_attn_out=w_attn_out, w_ssm_out=w_ssm_out,
                   w_o=w_o, final_norm_w=final_norm_w)
    moms = dict(norm_w=m_norm_w, w_in=m_w_in, conv_w=m_conv_w, conv_b=m_conv_b, dt_bias=m_dt_bias,
                a_log=m_a_log, d_skip=m_d_skip, ssm_norm_w=m_ssm_norm_w, w_attn_out=m_w_attn_out,
                w_ssm_out=m_w_ssm_out, w_o=m_w_o, final_norm_w=m_final_norm_w)
    vels = dict(norm_w=v_norm_w, w_in=v_w_in, conv_w=v_conv_w, conv_b=v_conv_b, dt_bias=v_dt_bias,
                a_log=v_a_log, d_skip=v_d_skip, ssm_norm_w=v_ssm_norm_w, w_attn_out=v_w_attn_out,
                w_ssm_out=v_w_ssm_out, w_o=v_w_o, final_norm_w=v_final_norm_w)

    tr_ = lambda a: jnp.transpose(a[0])
    g_in, g_conv = _gather([_bf(tr_(w_in)), conv_w[0]], name="gather_weights")
    conv_full = g_conv.transpose(1, 0, 2).reshape(CONV_K, CONV_DIM)
    o_send, o_recv, o_srcs, o_lands, o_zero = _split_start(
        "all", [_bf(w_attn_out[0]), _bf(w_ssm_out[0]), _bf(w_o[0])], name="gather_out_weights_start")

    core = lax.axis_index("c").astype(jnp.int32).reshape(1)
    chip = 2 * lax.axis_index("x") + lax.axis_index("y")
    me = 2 * chip + lax.axis_index("c")

    def out_weights(after):
        srcs, lands = _split_wait("all", o_send, o_recv, o_srcs, o_lands, after, name="gather_out_weights_wait")
        g_wa, g_ws, g_wo = _own_slab(lands, srcs, me, True)
        return (g_wa.reshape(SB_WIDTH, D_MODEL), g_ws.reshape(SSD_WIDTH, D_MODEL), g_wo.reshape(D_MODEL, D_MODEL))

    def send_weight_grads(grads):
        slabs = [grads["w_in_t"],
                 grads["conv_w"].reshape(CONV_K, N_DEV, CONV_DIM // N_DEV).transpose(1, 0, 2),
                 _bf(grads["w_attn_out"].reshape(N_DEV, SB_WIDTH // N_DEV, D_MODEL)),
                 _bf(grads["w_ssm_out"].reshape(N_DEV, SSD_WIDTH // N_DEV, D_MODEL)),
                 _bf(grads["w_o"].reshape(N_DEV, D_MODEL // N_DEV, D_MODEL))]
        recv = _pair_exchange(slabs, [], name="exchange_pairs")
        names = ("w_in", "conv_w", "w_attn_out", "w_ssm_out", "w_o")
        sums = [_pair_add(a, b, core, name="pair_add_" + k, tc=4 * LANES if k == "w_in" else None)
                for k, a, b in zip(names, slabs, recv)]
        send_sems, recv_sems, srcs, lands, zero = _split_start("chips", sums, name="exchange_chips_start")
        return zero, (send_sems, recv_sems, srcs, lands)

    loss_part, grad_x, grads, (send_sems, recv_sems, srcs, lands) = _local_step(
        x.reshape(t, D_MODEL), loss_target.reshape(t, D_MODEL), nb, s, g_in.reshape(D_PROJ, D_MODEL),
        conv_full, conv_b, norm_w + o_zero[0:1, 0:1], dt_bias, a_log, d_skip, ssm_norm_w, out_weights,
        final_norm_w.reshape(1, D_MODEL), between=send_weight_grads)

    (p_small,) = _pair_exchange([], [_pack_small(grads, loss_part[0, 0])], name="exchange_small")
    srcs, lands = _split_wait("chips", send_sems, recv_sems, srcs, lands, grad_x, name="exchange_chips_wait")
    p_in, p_conv, p_wa, p_ws, p_wo = _own_slab(lands, srcs, chip, False)

    res = {}
    res["w_in"] = [a.T for a in _adamw(p_in, tr_(w_in), tr_(m_w_in), tr_(v_w_in), name="adamw_w_in", tc=2 * LANES)]
    res["conv_w"] = _adamw(p_conv, conv_w[0], m_conv_w[0], v_conv_w[0], name="adamw_conv_w")
    res["w_attn_out"] = _adamw(p_wa, w_attn_out[0], m_w_attn_out[0], v_w_attn_out[0], name="adamw_w_attn_out")
    res["w_ssm_out"] = _adamw(p_ws, w_ssm_out[0], m_w_ssm_out[0], v_w_ssm_out[0], name="adamw_w_ssm_out")
    res["w_o"] = _adamw(p_wo, w_o[0], m_w_o[0], v_w_o[0], name="adamw_w_o")
    sm = _adamw(p_small, _pack_small(weights), _pack_small(moms), _pack_small(vels), name="adamw_small")
    loss = sm[0].reshape(-1)[SMALL_USED]
    shapes = {k: weights[k].shape for k, _ in SMALL}
    sm = [_unpack_small(v, shapes) for v in sm]
    for k, _ in SMALL:
        res[k] = tuple(part[k] for part in sm)

    order = ("norm_w", "w_in", "conv_w", "conv_b", "dt_bias", "a_log", "d_skip", "ssm_norm_w",
             "w_attn_out", "w_ssm_out", "w_o", "final_norm_w")
    outs = [loss, grad_x.reshape(nb, s, D_MODEL)]
    for i in range(4):
        outs += [res[k][i].reshape(weights[k].shape) for k in order]
    return tuple(outs)
```

```python
import functools

import jax
import jax.numpy as jnp
from jax import lax
from jax.experimental import pallas as pl
from jax.experimental.pallas import tpu as pltpu

F32 = jnp.float32
BF16 = jnp.bfloat16

D_MODEL = 1024
SB_WIDTH = 1024
HEAD_DIM = 64
SSD_WIDTH = 2048
SSD_GROUPS = 4
SSD_HPG = 8
SSD_STATE = 128
SSD_CHUNK = 128
CONV_K = 4
CONV_DIM = 3072
N_DT = 32
D_PROJ = 11296
DT_COL0 = 9216
GATE_COL0 = DT_COL0 + N_DT
EPS = 1e-6
N_DEV = 8

ADAM_LR = 0.001
ADAM_B1 = 0.9
ADAM_B2 = 0.999
ADAM_EPS = 1e-08
ADAM_WD = 0.01
ADAM_STEP = 10

LANES = 128
VMEM_LIMIT = 56 * 1024 * 1024

CB_Q, CB_K, CB_V, CB_ZA, CB_ZS, CB_XBC = 0, 8, 16, 24, 32, 48

NT_DIMS = (((1,), (1,)), ((), ()))
TN_DIMS = (((0,), (0,)), ((), ()))


def _params(sem):
    return pltpu.CompilerParams(dimension_semantics=sem, vmem_limit_bytes=VMEM_LIMIT)


def _dot(a, b):
    return jnp.dot(a, b, preferred_element_type=F32)


def _dot_nt(a, b):
    return lax.dot_general(a, b, NT_DIMS, preferred_element_type=F32)


def _dot_tn(a, b):
    return lax.dot_general(a, b, TN_DIMS, preferred_element_type=F32)


def _bf(a):
    return a.astype(BF16)


def _sigmoid(x):
    return 0.5 + 0.5 * jnp.tanh(0.5 * x)


def _softplus(x):
    return jnp.maximum(x, 0.0) + jnp.log(1.0 + jnp.exp(-jnp.abs(x)))


def _trunc_split(a):
    bits = lax.bitcast_convert_type(a, jnp.uint32) & jnp.uint32(0xFFFF0000)
    hi = lax.bitcast_convert_type(bits, F32)
    return _bf(hi), _bf(a - hi)


MM_TILE = 1024
MM_TK = 2048
MM_TALL = 2048


def _matmul(a, b, *, name, nt=False, ta=False, add=None, more=(), out_dtype=F32, tm=MM_TILE, tn=MM_TILE, tk=MM_TK,
            n=None):
    k, m = a.shape if ta else a.shape[::-1]
    n = (b.shape[0] if nt else b.shape[1]) if n is None else n
    tm, tn, tk = min(tm, m), min(tn, n), min(tk, k)
    assert m % tm == 0 and n % tn == 0 and k % tk == 0, (name, a.shape, b.shape)
    assert not (ta and nt) and not (more and (ta or nt))
    nk = k // tk
    has_add = add is not None
    n_in = 2 + has_add + 2 * len(more)

    def body(*refs):
        a_ref, b_ref = refs[:2]
        add_ref = refs[2] if has_add else None
        more_refs = refs[2 + has_add:n_in]
        o_ref = refs[n_in]
        acc_ref = refs[-1] if nk > 1 else None
        kk = pl.program_id(2)

        def product():
            av, bv = _bf(a_ref[...]), _bf(b_ref[...])
            return _dot_nt(av, bv) if nt else _dot_tn(av, bv) if ta else _dot(av, bv)

        def last(r):
            for a2_ref, b2_ref in zip(more_refs[::2], more_refs[1::2]):
                r = r + _dot(_bf(a2_ref[...]), _bf(b2_ref[...]))
            return (r + add_ref[...]) if has_add else r

        if nk == 1:
            o_ref[...] = last(product()).astype(out_dtype)
            return

        @pl.when(kk == 0)
        def _():
            acc_ref[...] = product()

        @pl.when((kk > 0) & (kk < nk - 1))
        def _():
            acc_ref[...] += product()

        @pl.when(kk == nk - 1)
        def _():
            o_ref[...] = last(acc_ref[...] + product()).astype(out_dtype)

    in_specs = [pl.BlockSpec((tk, tm), lambda j, i, kk: (kk, i)) if ta
                else pl.BlockSpec((tm, tk), lambda j, i, kk: (i, kk)),
                pl.BlockSpec((tn, tk), lambda j, i, kk: (j, kk)) if nt
                else pl.BlockSpec((tk, tn), lambda j, i, kk: (kk, j))]
    args = [a, b]
    if has_add:
        in_specs.append(pl.BlockSpec((tm, tn), lambda j, i, kk: (i, j)))
        args.append(add)
    for a2, b2 in more:
        assert a2.shape[0] == m and b2.shape == (a2.shape[1], n), (name, a2.shape, b2.shape)
        in_specs += [pl.BlockSpec((tm, a2.shape[1]), lambda j, i, kk: (i, 0)),
                     pl.BlockSpec((a2.shape[1], tn), lambda j, i, kk: (0, j))]
        args += [a2, b2]
    return pl.pallas_call(
        body, grid=(n // tn, m // tm, nk), in_specs=in_specs,
        out_specs=pl.BlockSpec((tm, tn), lambda j, i, kk: (i, j)),
        out_shape=jax.ShapeDtypeStruct((m, n), out_dtype),
        scratch_shapes=[pltpu.VMEM((tm, tn), F32)] if nk > 1 else [],
        compiler_params=_params(("parallel", "parallel", "arbitrary")), name=name)(*args)


def _rms_fwd(x, w, *, tm=512):
    t = x.shape[0]

    def body(x_ref, w_ref, h_ref, r_ref):
        xv = x_ref[...]
        r = lax.rsqrt(jnp.mean(xv * xv, axis=-1, keepdims=True) + EPS)
        h_ref[...] = _bf(xv * r * w_ref[...])
        r_ref[...] = r

    row = lambda i: (i, 0)
    return pl.pallas_call(
        body, grid=(t // tm,),
        in_specs=[pl.BlockSpec((tm, D_MODEL), row), pl.BlockSpec((1, D_MODEL), lambda i: (0, 0))],
        out_specs=[pl.BlockSpec((tm, D_MODEL), row), pl.BlockSpec((tm, 1), row)],
        out_shape=[jax.ShapeDtypeStruct((t, D_MODEL), BF16), jax.ShapeDtypeStruct((t, 1), F32)],
        compiler_params=_params(("parallel",)), name="rms_fwd")(x, w)


def _rms_bwd(dh, x, r, w, dres, *, tm=512):
    t = x.shape[0]

    def body(dh_ref, x_ref, r_ref, w_ref, dres_ref, dx_ref, dw_ref):
        @pl.when(pl.program_id(0) == 0)
        def _():
            dw_ref[...] = jnp.zeros_like(dw_ref)

        nrm = x_ref[...] * r_ref[...]
        dhv = dh_ref[...]
        dw_ref[...] += jnp.sum(dhv * nrm, axis=0, keepdims=True)
        dn = dhv * w_ref[...]
        dx = r_ref[...] * (dn - nrm * jnp.mean(dn * nrm, axis=-1, keepdims=True))
        dx_ref[...] = dx + dres_ref[...]

    row = lambda i: (i, 0)
    fix = lambda i: (0, 0)
    return pl.pallas_call(
        body, grid=(t // tm,),
        in_specs=[pl.BlockSpec((tm, D_MODEL), row), pl.BlockSpec((tm, D_MODEL), row),
                  pl.BlockSpec((tm, 1), row), pl.BlockSpec((1, D_MODEL), fix),
                  pl.BlockSpec((tm, D_MODEL), row)],
        out_specs=[pl.BlockSpec((tm, D_MODEL), row), pl.BlockSpec((1, D_MODEL), fix)],
        out_shape=[jax.ShapeDtypeStruct((t, D_MODEL), F32), jax.ShapeDtypeStruct((1, D_MODEL), F32)],
        compiler_params=_params(("arbitrary",)), name="rms_bwd")(dh, x, r, w, dres)


ATT_BLK = 256


def _attn_masks(blk):
    r_io = lax.broadcasted_iota(jnp.int32, (blk, blk), 0)
    c_io = lax.broadcasted_iota(jnp.int32, (blk, blk), 1)
    return r_io, c_io


def _attn_tiles(nq):
    return nq * (nq + 1) // 2


def _attn_fwd(proj, nb, s):
    blk = min(ATT_BLK, s)
    nq = s // blk
    nt = _attn_tiles(nq)
    scale = HEAD_DIM ** -0.5

    def body(q_ref, k_ref, v_ref, o_ref, a_out, sg_out, q0_s, q1_s, kb_s, v0_s, v1_s, z_s, a_s, acc_s, cr_s):
        head0 = lax.broadcasted_iota(jnp.int32, (s, LANES), 1) < HEAD_DIM
        r_io, c_io = _attn_masks(blk)
        tri = c_io < r_io
        tri2 = jnp.concatenate([tri, tri], axis=0)
        u_gt = _bf(r_io > c_io)
        u2 = jnp.concatenate([u_gt, u_gt], axis=0)
        qv = q_ref[...] * scale
        q0_s[...] = _bf(jnp.where(head0, qv, 0.0))
        q1_s[...] = _bf(jnp.where(head0, 0.0, qv))
        kb_s[...] = _bf(k_ref[...])
        vv = v_ref[...]
        v0_s[...] = _bf(jnp.where(head0, vv, 0.0))
        v1_s[...] = _bf(jnp.where(head0, 0.0, vv))

        def scores(qcat, kblk, slot):
            k0 = pl.multiple_of(jnp.maximum(kblk, 0) * blk, blk)
            z_s[slot] = _dot_nt(qcat, kb_s[pl.ds(k0, blk), :])

        def weights(slot, tile, diag, zslot=None):
            z = z_s[slot if zslot is None else zslot]
            cr = cr_s[...]
            sp = _softplus(z)
            spm = jnp.where(tri2, sp, 0.0) if diag else sp
            hi, lo = _trunc_split(spm)
            ps = _dot(jnp.concatenate([hi, lo], axis=1), u2)
            lb = z - sp
            a = jnp.exp(lb - (ps + cr))
            if diag:
                a = jnp.where(tri2, a, 0.0)
            ab = _bf(a)
            a_s[slot] = ab
            a_out[tile] = ab
            sg_out[tile] = _bf(jnp.exp(lb))
            cr_s[...] = cr + ps[:, 0:1] + spm[:, 0:1]

        def apply(slot, k0):
            a = a_s[slot]
            vcat = jnp.concatenate([v0_s[pl.ds(k0, blk), :], v1_s[pl.ds(k0, blk), :]], axis=0)
            acc_s[...] += _dot(jnp.concatenate([a[:blk], a[blk:]], axis=1), vcat)

        def queries(qi):
            q0 = pl.multiple_of(qi * blk, blk)
            return jnp.concatenate([q0_s[pl.ds(q0, blk), :], q1_s[pl.ds(q0, blk), :]], axis=0)

        def first_scores(qi):
            qcat = queries(qi)
            scores(qcat, qi, 2)
            scores(qcat, qi - 1, 3)

        first_scores(jnp.int32(0))

        def qblock(qi, _):
            q0 = pl.multiple_of(qi * blk, blk)
            tile0 = qi * (qi + 1) // 2
            qcat = queries(qi)
            z_s[1] = z_s[3]
            cr_s[...] = jnp.zeros_like(cr_s)
            acc_s[...] = jnp.zeros_like(acc_s)
            weights(0, tile0 + qi, True, zslot=2)
            first_scores(jnp.minimum(qi + 1, nq - 1))

            def step(j, slot):
                scores(qcat, qi - j - 1, 1 - slot)
                weights(slot, tile0 + qi - j, False)
                apply(1 - slot, pl.multiple_of((qi - j + 1) * blk, blk))

            def pair(i, _):
                step(2 * i + 1, 1)
                step(2 * i + 2, 0)
                return 0

            lax.fori_loop(0, qi // 2, pair, 0)

            @pl.when(qi % 2 == 1)
            def _():
                step(qi, 1)

            apply(qi & 1, 0)
            o_ref[pl.ds(q0, blk), :] = acc_s[...]
            return 0

        lax.fori_loop(0, nq, qblock, 0)

    t = nb * s
    tiles = pl.BlockSpec((None, None, nt, 2 * blk, blk), lambda b, p: (b, p, 0, 0, 0))
    tiles_shape = jax.ShapeDtypeStruct((nb, 8, nt, 2 * blk, blk), BF16)
    return pl.pallas_call(
        body, grid=(nb, 8),
        in_specs=[pl.BlockSpec((s, LANES), lambda b, p: (b, CB_Q + p)),
                  pl.BlockSpec((s, LANES), lambda b, p: (b, CB_K + p)),
                  pl.BlockSpec((s, LANES), lambda b, p: (b, CB_V + p))],
        out_specs=[pl.BlockSpec((s, LANES), lambda b, p: (b, p)), tiles, tiles],
        out_shape=[jax.ShapeDtypeStruct((t, SB_WIDTH), F32), tiles_shape, tiles_shape],
        scratch_shapes=[pltpu.VMEM((s, LANES), BF16)] * 5 + [
            pltpu.VMEM((4, 2 * blk, blk), F32), pltpu.VMEM((2, 2 * blk, blk), BF16),
            pltpu.VMEM((blk, LANES), F32), pltpu.VMEM((2 * blk, 1), F32)],
        compiler_params=_params(("parallel", "parallel")), name="attn_fwd")(proj, proj, proj)


def _attn_bwd(proj, do, a_all, sg_all, nb, s):
    blk = min(ATT_BLK, s)
    nq = s // blk
    nt = _attn_tiles(nq)
    scale = HEAD_DIM ** -0.5

    def body(q_ref, k_ref, v_ref, do_ref, a_in, sg_in, dq_ref, dk_ref, dv_ref,
             qt0_s, qt1_s, k0_s, k1_s, vb_s, d0_s, d1_s, dt0_s, dt1_s, da_s, dz_s, dkt_acc, dvt_acc, dq_acc, cw_s):
        head0 = lax.broadcasted_iota(jnp.int32, (s, LANES), 1) < HEAD_DIM
        r_io, c_io = _attn_masks(blk)
        tri = c_io < r_io
        tri2 = jnp.concatenate([tri, tri], axis=0)
        u_lt = _bf(r_io < c_io)
        head0_t = lax.broadcasted_iota(jnp.int32, (LANES, s), 0) < HEAD_DIM
        q_t = (q_ref[...].astype(F32) * scale).T
        qt0_s[...] = _bf(jnp.where(head0_t, q_t, 0.0))
        qt1_s[...] = _bf(jnp.where(head0_t, 0.0, q_t))
        kv = k_ref[...]
        k0_s[...] = _bf(jnp.where(head0, kv, 0.0))
        k1_s[...] = _bf(jnp.where(head0, 0.0, kv))
        vb_s[...] = _bf(v_ref[...])
        dov = do_ref[...]
        d0_s[...] = _bf(jnp.where(head0, dov, 0.0))
        d1_s[...] = _bf(jnp.where(head0, 0.0, dov))
        do_t = dov.astype(F32).T
        dt0_s[...] = _bf(jnp.where(head0_t, do_t, 0.0))
        dt1_s[...] = _bf(jnp.where(head0_t, 0.0, do_t))
        dkt_acc[...] = jnp.zeros_like(dkt_acc)
        dvt_acc[...] = jnp.zeros_like(dvt_acc)

        def qblock(qi, _):
            q0 = pl.multiple_of(qi * blk, blk)
            tile0 = qi * (qi + 1) // 2
            qs_ = pl.ds(q0, blk)
            dcat = jnp.concatenate([d0_s[qs_, :], d1_s[qs_, :]], axis=0)
            qcat_t = jnp.concatenate([qt0_s[:, qs_], qt1_s[:, qs_]], axis=1)
            dcat_t = jnp.concatenate([dt0_s[:, qs_], dt1_s[:, qs_]], axis=1)

            def scores(kblk, slot):
                da_s[slot] = _dot_nt(dcat, vb_s[pl.ds(pl.multiple_of(kblk * blk, blk), blk), :])

            def grads(slot, kblk, diag):
                cw = cw_s[...]
                w = a_in[tile0 + kblk].astype(F32) * da_s[slot]
                wex = _dot(_bf(w), u_lt)
                dz = w - sg_in[tile0 + kblk].astype(F32) * (w + (wex + cw))
                if diag:
                    dz = jnp.where(tri2, dz, 0.0)
                dz_s[slot] = _bf(dz)
                cw_s[...] = cw + wex[:, blk - 1:blk] + w[:, blk - 1:blk]

            def apply(slot, kblk):
                ks_ = pl.ds(pl.multiple_of(kblk * blk, blk), blk)
                dz = dz_s[slot]
                dq_acc[...] += _dot(jnp.concatenate([dz[:blk], dz[blk:]], axis=1),
                                    jnp.concatenate([k0_s[ks_, :], k1_s[ks_, :]], axis=0))
                dkt_acc[:, ks_] += _dot(qcat_t, dz)
                dvt_acc[:, ks_] += _dot(dcat_t, a_in[tile0 + kblk])

            def step(j, slot, diag, first=False):
                if not diag:
                    scores(j + 1, 1 - slot)
                grads(slot, j, diag)
                if not first:
                    apply(1 - slot, j - 1)

            scores(0, 0)
            dq_acc[...] = jnp.zeros_like(dq_acc)
            cw_s[...] = jnp.zeros_like(cw_s)

            @pl.when(qi == 0)
            def _():
                step(0, 0, True, first=True)

            @pl.when(qi > 0)
            def _():
                step(0, 0, False, first=True)

            def pair(i, _):
                step(2 * i + 1, 1, False)
                step(2 * i + 2, 0, False)
                return 0

            lax.fori_loop(0, (qi - 1) // 2, pair, 0)

            @pl.when((qi > 0) & (qi % 2 == 0))
            def _():
                step(qi - 1, 1, False)
                step(qi, 0, True)

            @pl.when(qi % 2 == 1)
            def _():
                step(qi, 1, True)

            apply(qi & 1, qi)
            dq_ref[qs_, :] = _bf(dq_acc[...] * scale)
            return 0

        lax.fori_loop(0, nq, qblock, 0)
        dk_ref[...] = _bf(dkt_acc[...].T)
        dv_ref[...] = _bf(dvt_acc[...].T)

    t = nb * s
    hp = lambda b, p: (b, p)
    out = jax.ShapeDtypeStruct((t, SB_WIDTH), BF16)
    tiles = pl.BlockSpec((None, None, nt, 2 * blk, blk), lambda b, p: (b, p, 0, 0, 0))
    return pl.pallas_call(
        body, grid=(nb, 8),
        in_specs=[pl.BlockSpec((s, LANES), lambda b, p: (b, CB_Q + p)),
                  pl.BlockSpec((s, LANES), lambda b, p: (b, CB_K + p)),
                  pl.BlockSpec((s, LANES), lambda b, p: (b, CB_V + p)),
                  pl.BlockSpec((s, LANES), hp), tiles, tiles],
        out_specs=[pl.BlockSpec((s, LANES), hp)] * 3,
        out_shape=[out, out, out],
        scratch_shapes=[pltpu.VMEM((LANES, s), BF16)] * 2 + [pltpu.VMEM((s, LANES), BF16)] * 5 + [
            pltpu.VMEM((LANES, s), BF16)] * 2 + [
            pltpu.VMEM((2, 2 * blk, blk), F32), pltpu.VMEM((2, 2 * blk, blk), BF16),
            pltpu.VMEM((LANES, s), F32), pltpu.VMEM((LANES, s), F32),
            pltpu.VMEM((blk, LANES), F32), pltpu.VMEM((2 * blk, 1), F32)],
        compiler_params=_params(("parallel", "parallel")), name="attn_bwd")(proj, proj, proj, do, a_all, sg_all)


CONV_TC = 256
CONV_ROWS = 64
HALO = 8


def _conv_rows(ws, w_ref, b_ref, n):
    taps = [ws[HALO - kk:HALO - kk + n, :] for kk in range(CONV_K)]
    y = b_ref[...] + w_ref[CONV_K - 1:CONV_K, :] * taps[0]
    for kk in range(1, CONV_K):
        y = y + w_ref[CONV_K - 1 - kk:CONV_K - kk, :] * taps[kk]
    return y, taps


def _conv_fwd(proj, conv_w, conv_b, nb, s):
    tc, rows = CONV_TC, min(CONV_ROWS, s)
    cb0 = CB_XBC * LANES // tc

    def body(x_ref, w_ref, b_ref, o_ref, xp, ws):
        xp[0:HALO, :] = jnp.zeros((HALO, tc), F32)
        xp[HALO:HALO + s, :] = x_ref[...].astype(F32)

        def chunk(i, _):
            r0 = pl.multiple_of(i * rows, rows)
            ws[...] = xp[pl.ds(r0, rows + HALO), :]
            y, _ = _conv_rows(ws, w_ref, b_ref, rows)
            o_ref[pl.ds(r0, rows), :] = y * _sigmoid(y)
            return 0

        lax.fori_loop(0, s // rows, chunk, 0)

    t = nb * s
    return pl.pallas_call(
        body, grid=(nb, CONV_DIM // tc),
        in_specs=[pl.BlockSpec((s, tc), lambda b, c: (b, cb0 + c)),
                  pl.BlockSpec((CONV_K, tc), lambda b, c: (0, c)),
                  pl.BlockSpec((1, tc), lambda b, c: (0, c))],
        out_specs=pl.BlockSpec((s, tc), lambda b, c: (b, c)),
        out_shape=jax.ShapeDtypeStruct((t, CONV_DIM), F32),
        scratch_shapes=[pltpu.VMEM((s + HALO, tc), F32), pltpu.VMEM((rows + HALO, tc), F32)],
        compiler_params=_params(("parallel", "parallel")), name="conv_fwd")(proj, conv_w, conv_b)


def _conv_bwd(proj, dact, conv_w, conv_b, dproj, nb, s, *, col0, name):
    width = dact.shape[1]
    tc, rows = min(CONV_TC, width), min(CONV_ROWS, s)
    cb0 = (CB_XBC * LANES + col0) // tc
    wb0 = col0 // tc

    def body(x_ref, d_ref, w_ref, b_ref, _, dx_ref, dw_ref, db_ref, xp, dp, dys, ws):
        @pl.when(pl.program_id(1) == 0)
        def _():
            dw_ref[...] = jnp.zeros_like(dw_ref)
            db_ref[...] = jnp.zeros_like(db_ref)

        n = rows + HALO
        pad = jnp.zeros((HALO, tc), F32)
        xp[0:HALO, :] = pad
        xp[HALO:HALO + s, :] = x_ref[...].astype(F32)
        xp[HALO + s:HALO + s + HALO, :] = pad
        dp[0:s, :] = d_ref[...]
        dp[s:s + HALO, :] = pad

        def chunk(i, acc):
            r0 = pl.multiple_of(i * rows, rows)
            ws[...] = xp[pl.ds(r0, n + HALO), :]
            y, taps = _conv_rows(ws, w_ref, b_ref, n)
            sg = _sigmoid(y)
            dy = dp[pl.ds(r0, n), :] * (sg * (1.0 + y * (1.0 - sg)))
            dys[...] = dy
            dx = w_ref[CONV_K - 1:CONV_K, :] * dy[0:rows]
            for kk in range(1, CONV_K):
                dx = dx + w_ref[CONV_K - 1 - kk:CONV_K - kk, :] * dys[kk:kk + rows, :]
            dx_ref[pl.ds(r0, rows), :] = _bf(dx)
            dyc = dy[0:rows]
            new = [acc[0] + jnp.sum(dyc, axis=0, keepdims=True)]
            for kk in range(CONV_K):
                new.append(acc[1 + kk] + jnp.sum(dyc * taps[kk][0:rows], axis=0, keepdims=True))
            return tuple(new)

        acc = lax.fori_loop(0, s // rows, chunk, (jnp.zeros((1, tc), F32),) * (1 + CONV_K))
        db_ref[...] += acc[0]
        for kk in range(CONV_K):
            dw_ref[CONV_K - 1 - kk:CONV_K - kk, :] += acc[1 + kk]

    t = nb * s
    return pl.pallas_call(
        body, grid=(width // tc, nb),
        in_specs=[pl.BlockSpec((s, tc), lambda c, b: (b, cb0 + c)),
                  pl.BlockSpec((s, tc), lambda c, b: (b, c)),
                  pl.BlockSpec((CONV_K, tc), lambda c, b: (0, wb0 + c)),
                  pl.BlockSpec((1, tc), lambda c, b: (0, wb0 + c)), pl.BlockSpec(memory_space=pl.ANY)],
        out_specs=[pl.BlockSpec((s, tc), lambda c, b: (b, cb0 + c)),
                   pl.BlockSpec((CONV_K, tc), lambda c, b: (0, c)),
                   pl.BlockSpec((1, tc), lambda c, b: (0, c))],
        out_shape=[jax.ShapeDtypeStruct(dproj.shape, dproj.dtype),
                   jax.ShapeDtypeStruct((CONV_K, width), F32),
                   jax.ShapeDtypeStruct((1, width), F32)],
        input_output_aliases={4: 0},
        scratch_shapes=[pltpu.VMEM((s + 2 * HALO, tc), F32), pltpu.VMEM((s + HALO, tc), F32),
                        pltpu.VMEM((rows + HALO, tc), F32), pltpu.VMEM((rows + 2 * HALO, tc), F32)],
        compiler_params=_params(("parallel", "arbitrary")), name=name)(proj, dact, conv_w, conv_b, dproj)


L = SSD_CHUNK
N_PAIR = SSD_HPG // 2


def _pair_lanes(v0, v1, lane):
    return jnp.where(lane < HEAD_DIM, v0, v1)


def _onehot(shape, row_shift, col_shift):
    row = lax.broadcasted_iota(jnp.int32, shape, 0)
    col = lax.broadcasted_iota(jnp.int32, shape, 1)
    return _bf(lax.shift_right_logical(row, row_shift) == lax.shift_right_logical(col, col_shift))


def _ssd_tables():
    sp = _onehot((LANES, N_PAIR * LANES), 0, 6)
    sf = _onehot((LANES, SSD_HPG * LANES), 0, 7)
    return (jnp.concatenate([sp] * 3, axis=0), jnp.concatenate([sf] * 3, axis=0),
            _onehot((N_PAIR * LANES, LANES), 6, 0), _onehot((SSD_HPG * LANES, LANES), 7, 0))


def _split3(a):
    a1 = _bf(a)
    r1 = a - a1.astype(F32)
    a2 = _bf(r1)
    return jnp.concatenate([a1, a2, _bf(r1 - a2.astype(F32))], axis=1)


def _split2(a):
    hi, lo = _trunc_split(a)
    return jnp.concatenate([hi, lo], axis=1)


def _tri_dot(tri_bf, a):
    a1 = _bf(a)
    r1 = a - a1.astype(F32)
    a2 = _bf(r1)
    parts = jnp.concatenate([a1, a2, _bf(r1 - a2.astype(F32))], axis=0)
    return _dot(jnp.concatenate([tri_bf] * 3, axis=1), parts)


HEAD_W = 4 * LANES + 2 * N_PAIR * LANES + SSD_HPG * LANES


def _ssd_head(dtr_ref, bias_ref, a_neg, c, tri_bf, spread_pair, spread_full, head_s, slot):
    raw = dtr_ref[pl.ds(pl.multiple_of(c * L, L), L), :] + bias_ref[...]
    dt = _softplus(raw)
    acs = _tri_dot(tri_bf, dt * a_neg)
    acs3 = _split3(acs)
    both = _dot(jnp.concatenate([_split3(dt), acs3], axis=0), spread_pair)
    parts = (raw, dt, acs, acs.T, both[:L], both[L:], _dot(acs3, spread_full))
    off = 0
    for p in parts:
        head_s[slot, :, off:off + p.shape[1]] = p
        off += p.shape[1]


def _ssd_head_load(head_s, slot):
    out, off = [], 0
    for w in (LANES, LANES, LANES, L, N_PAIR * LANES, N_PAIR * LANES, SSD_HPG * LANES):
        out.append(head_s[slot, :, off:off + w])
        off += w
    return out


def _ssd_fwd(xbc, dtr, bias, alog, dskip, nb, s):
    nc = s // L
    t = nb * s

    def body(x_ref, b_ref, c_ref, dtr_ref, bias_ref, alog_ref, dsk_ref, y_ref, hs_ref, h_scr, head_s):
        lane = lax.broadcasted_iota(jnp.int32, (L, LANES), 1)
        lane1 = lax.broadcasted_iota(jnp.int32, (1, LANES), 1)
        rowc = lax.broadcasted_iota(jnp.int32, (L, 1), 0)
        rr = lax.broadcasted_iota(jnp.int32, (L, L), 0)
        cc = lax.broadcasted_iota(jnp.int32, (L, L), 1)
        causal = rr >= cc
        h_scr[...] = jnp.zeros_like(h_scr)
        dsk = dsk_ref[...]
        spread_pair, spread_full, _, _ = _ssd_tables()
        a_neg = -jnp.exp(alog_ref[...])
        head = functools.partial(_ssd_head, dtr_ref, bias_ref, a_neg, tri_bf=_bf(cc <= rr),
                                 spread_pair=spread_pair, spread_full=spread_full, head_s=head_s)
        head(jnp.int32(0), slot=0)

        def chunk(c, slot):
            r0 = pl.multiple_of(c * L, L)
            head(jnp.minimum(c + 1, nc - 1), slot=1 - slot)
            _, _, acs, acs_t, dt_all, acs_all, acs_full = _ssd_head_load(head_s, slot)
            bb = _bf(b_ref[pl.ds(r0, L), :])
            cb = _bf(c_ref[pl.ds(r0, L), :])
            gm = _dot_nt(cb, bb)
            for pr in range(N_PAIR):
                j0, j1 = 2 * pr, 2 * pr + 1
                ms = []
                for j in (j0, j1):
                    seg = acs_full[:, j * LANES:(j + 1) * LANES] - acs_t[j:j + 1, :]
                    ms.append(_bf(gm * jnp.exp(jnp.where(causal, seg, -1e30))))
                x = x_ref[pl.ds(r0, L), pr * LANES:(pr + 1) * LANES]
                dt_p = dt_all[:, pr * LANES:(pr + 1) * LANES]
                acs_p = acs_all[:, pr * LANES:(pr + 1) * LANES]
                last_p = acs_p[L - 1:L, :]
                d_p = _pair_lanes(dsk[:, j0:j0 + 1], dsk[:, j1:j1 + 1], lane1)
                xd = x * dt_p
                xdb = _bf(xd)
                yds = _dot(jnp.concatenate(ms, axis=0), xdb)
                yd = _pair_lanes(yds[:L], yds[L:], lane)
                hp = h_scr[pr]
                yo = _dot_nt(cb, _bf(hp)) * jnp.exp(acs_p)
                y_ref[pl.ds(r0, L), pr * LANES:(pr + 1) * LANES] = yd + yo + x * d_p
                hs_ref[c, pr] = hp
                sp = _dot_tn(_bf(xd * jnp.exp(last_p - acs_p)), bb)
                cd = jnp.where(rowc < HEAD_DIM, jnp.exp(acs[L - 1:L, j0:j0 + 1]),
                               jnp.exp(acs[L - 1:L, j1:j1 + 1]))
                h_scr[pr] = hp * cd + sp

        def two_chunks(i, _):
            chunk(2 * i, 0)
            chunk(2 * i + 1, 1)
            return 0

        lax.fori_loop(0, nc // 2, two_chunks, 0)

    assert nc % 2 == 0
    grp = lambda b, g: (g, 0, 0)
    return pl.pallas_call(
        body, grid=(nb, SSD_GROUPS),
        in_specs=[pl.BlockSpec((s, 4 * LANES), lambda b, g: (b, g)),
                  pl.BlockSpec((s, LANES), lambda b, g: (b, 16 + g)),
                  pl.BlockSpec((s, LANES), lambda b, g: (b, 20 + g)),
                  pl.BlockSpec((None, s, LANES), lambda b, g: (g, b, 0)),
                  pl.BlockSpec((None, 1, LANES), grp), pl.BlockSpec((None, 1, LANES), grp),
                  pl.BlockSpec((None, 1, LANES), grp)],
        out_specs=[pl.BlockSpec((s, 4 * LANES), lambda b, g: (b, g)),
                   pl.BlockSpec((None, None, nc, N_PAIR, L, LANES), lambda b, g: (b, g, 0, 0, 0, 0))],
        out_shape=[jax.ShapeDtypeStruct((t, SSD_WIDTH), F32),
                   jax.ShapeDtypeStruct((nb, SSD_GROUPS, nc, N_PAIR, L, LANES), F32)],
        scratch_shapes=[pltpu.VMEM((N_PAIR, L, LANES), F32), pltpu.VMEM((2, L, HEAD_W), F32)],
        compiler_params=_params(("parallel", "parallel")), name="ssd_fwd")(
            xbc, xbc, xbc, dtr, bias, alog, dskip)


def _ssd_bwd(xbc, dtr, bias, alog, dskip, hs, dy, nb, s):
    nc = s // L
    t = nb * s

    def body(x_ref, b_ref, c_ref, dtr_ref, bias_ref, alog_ref, dsk_ref, hs_ref, dy_ref,
             dx_ref, db_ref, dc_ref, ddtr_ref, dbias_ref, dalog_ref, ddsk_ref, dh_scr, head_s):
        lane = lax.broadcasted_iota(jnp.int32, (L, LANES), 1)
        lane1 = lax.broadcasted_iota(jnp.int32, (1, LANES), 1)
        rowc = lax.broadcasted_iota(jnp.int32, (L, 1), 0)
        rr = lax.broadcasted_iota(jnp.int32, (L, L), 0)
        cc = lax.broadcasted_iota(jnp.int32, (L, L), 1)
        causal = rr >= cc
        tri_t = _bf(cc >= rr)
        dh_scr[...] = jnp.zeros_like(dh_scr)
        dsk = dsk_ref[...]
        a_neg = -jnp.exp(alog_ref[...])

        @pl.when(pl.program_id(1) == 0)
        def _():
            dbias_ref[...] = jnp.zeros_like(dbias_ref)
            dalog_ref[...] = jnp.zeros_like(dalog_ref)
            ddsk_ref[...] = jnp.zeros_like(ddsk_ref)

        spread_pair, spread_full, sum_pair, sum_full = _ssd_tables()
        sum_pair2 = jnp.concatenate([sum_pair] * 2, axis=0)
        sum_full2 = jnp.concatenate([sum_full] * 2, axis=0)
        head = functools.partial(_ssd_head, dtr_ref, bias_ref, a_neg, tri_bf=_bf(cc <= rr),
                                 spread_pair=spread_pair, spread_full=spread_full, head_s=head_s)
        head(jnp.int32(nc - 1), slot=0)

        def chunk(c, slot):
            r0 = pl.multiple_of(c * L, L)
            head(jnp.maximum(c - 1, 0), slot=1 - slot)
            raw, dt, acs, acs_t, dt_all, acs_all, acs_full = _ssd_head_load(head_s, slot)
            bb = _bf(b_ref[pl.ds(r0, L), :])
            cb = _bf(c_ref[pl.ds(r0, L), :])
            gm = _dot_nt(cb, bb)
            dg = jnp.zeros((L, L), F32)
            dbacc = jnp.zeros((L, LANES), F32)
            dcacc = jnp.zeros((L, LANES), F32)
            dacs_t = jnp.zeros((L, L), F32)
            ta_all, te_all, qm_all, tc_all, td_all, dcd_all = [], [], [], [], [], []
            for pr in range(N_PAIR):
                js = (2 * pr, 2 * pr + 1)
                lms, mfs = [], []
                for j in js:
                    seg = acs_full[:, j * LANES:(j + 1) * LANES] - acs_t[j:j + 1, :]
                    lm = jnp.exp(jnp.where(causal, seg, -1e30))
                    lms.append(lm)
                    mfs.append(gm * lm)
                x = x_ref[pl.ds(r0, L), pr * LANES:(pr + 1) * LANES]
                dyv = dy_ref[pl.ds(r0, L), pr * LANES:(pr + 1) * LANES]
                dt_p = dt_all[:, pr * LANES:(pr + 1) * LANES]
                acs_p = acs_all[:, pr * LANES:(pr + 1) * LANES]
                last_p = acs_p[L - 1:L, :]
                d_p = _pair_lanes(dsk[:, js[0]:js[0] + 1], dsk[:, js[1]:js[1] + 1], lane1)
                e_p = jnp.exp(acs_p)
                dte_p = jnp.exp(last_p - acs_p)
                xd = x * dt_p
                xdb = _bf(xd)
                hp = hs_ref[c, pr]
                hb = _bf(hp)
                dye = _bf(dyv * e_p)
                yo = _dot_nt(cb, hb) * e_p
                dcacc = dcacc + _dot(dye, hb)
                dh_out = _dot_tn(dye, cb)
                dyb = _bf(dyv)
                dms = _dot_nt(_bf(jnp.concatenate(
                    [jnp.where(lane < HEAD_DIM, dyv, 0.0), jnp.where(lane < HEAD_DIM, 0.0, dyv)], axis=0)), xdb)
                mty = _dot_tn(_bf(jnp.concatenate(mfs, axis=1)), dyb)
                dxd = jnp.where(lane < HEAD_DIM, mty[:L], mty[L:])
                for hh in range(2):
                    dm = dms[hh * L:(hh + 1) * L]
                    dg = dg + dm * lms[hh]
                    qm = dm * mfs[hh]
                    qm_all.append(qm)
                    dacs_t = dacs_t - jnp.where(rr == js[hh], jnp.sum(qm, axis=0, keepdims=True), 0.0)
                ds = dh_scr[pr]
                dsb = _bf(ds)
                xdd = xd * dte_p
                dxdd = _dot_nt(bb, dsb)
                dbacc = dbacc + _dot(_bf(xdd), dsb)
                dxd = dxd + dxdd * dte_p
                cd = jnp.where(rowc < HEAD_DIM, jnp.exp(acs[L - 1:L, js[0]:js[0] + 1]),
                               jnp.exp(acs[L - 1:L, js[1]:js[1] + 1]))
                prod = ds * hp
                dcd_all += [jnp.sum(prod[:HEAD_DIM], axis=0, keepdims=True),
                            jnp.sum(prod[HEAD_DIM:], axis=0, keepdims=True)]
                dh_scr[pr] = dh_out + ds * cd
                tcv = dxdd * xdd
                ta_all.append(dyv * yo - tcv)
                tc_all.append(jnp.sum(tcv, axis=0, keepdims=True))
                te_all.append(dxd * x)
                td_all.append(jnp.sum(dyv * x, axis=0, keepdims=True))
                dx_ref[pl.ds(r0, L), pr * LANES:(pr + 1) * LANES] = dxd * dt_p + dyv * d_p
            dgb = _bf(dg)
            dc_ref[pl.ds(r0, L), :] = dcacc + _dot(dgb, bb)
            db_ref[pl.ds(r0, L), :] = dbacc + _dot_tn(dgb, cb)
            last_add = jnp.zeros((1, LANES), F32)
            dd_add = jnp.zeros((1, LANES), F32)
            cd_row = jnp.exp(acs[L - 1:L, :])
            for j in range(SSD_HPG):
                hm = (lane1 >= HEAD_DIM) if j % 2 else (lane1 < HEAD_DIM)
                tcj = jnp.sum(jnp.where(hm, tc_all[j // 2], 0.0), axis=1, keepdims=True)
                tdj = jnp.sum(jnp.where(hm, td_all[j // 2], 0.0), axis=1, keepdims=True)
                dcdj = jnp.sum(dcd_all[j], axis=1, keepdims=True)
                last_add = last_add + jnp.where(lane1 == j, tcj + dcdj * cd_row, 0.0)
                dd_add = dd_add + jnp.where(lane1 == j, tdj, 0.0)
            dacs = (_dot(_split2(jnp.concatenate(ta_all, axis=1)), sum_pair2)
                    + _dot(_split2(jnp.concatenate(qm_all, axis=1)), sum_full2)
                    + dacs_t.T + jnp.where(rowc == L - 1, last_add, 0.0))
            ddta = _tri_dot(tri_t, dacs)
            ddt = _dot(_bf(jnp.concatenate(te_all, axis=1)), sum_pair) + ddta * a_neg
            ddraw = ddt * _sigmoid(raw)
            ddtr_ref[pl.ds(r0, L), :] = ddraw
            dbias_ref[...] += jnp.sum(ddraw, axis=0, keepdims=True)
            dalog_ref[...] += jnp.sum(ddta * dt, axis=0, keepdims=True) * a_neg
            ddsk_ref[...] += dd_add

        def two_chunks(i, _):
            chunk(nc - 1 - 2 * i, 0)
            chunk(nc - 2 - 2 * i, 1)
            return 0

        lax.fori_loop(0, nc // 2, two_chunks, 0)

    assert nc % 2 == 0
    grp = lambda g, b: (g, 0, 0)
    small = jax.ShapeDtypeStruct((SSD_GROUPS, 1, LANES), F32)
    return pl.pallas_call(
        body, grid=(SSD_GROUPS, nb),
        in_specs=[pl.BlockSpec((s, 4 * LANES), lambda g, b: (b, g)),
                  pl.BlockSpec((s, LANES), lambda g, b: (b, 16 + g)),
                  pl.BlockSpec((s, LANES), lambda g, b: (b, 20 + g)),
                  pl.BlockSpec((None, s, LANES), lambda g, b: (g, b, 0)),
                  pl.BlockSpec((None, 1, LANES), grp), pl.BlockSpec((None, 1, LANES), grp),
                  pl.BlockSpec((None, 1, LANES), grp),
                  pl.BlockSpec((None, None, nc, N_PAIR, L, LANES), lambda g, b: (b, g, 0, 0, 0, 0)),
                  pl.BlockSpec((s, 4 * LANES), lambda g, b: (b, g))],
        out_specs=[pl.BlockSpec((s, 4 * LANES), lambda g, b: (b, g)),
                   pl.BlockSpec((s, LANES), lambda g, b: (b, g)),
                   pl.BlockSpec((s, LANES), lambda g, b: (b, g)),
                   pl.BlockSpec((None, s, LANES), lambda g, b: (g, b, 0)),
                   pl.BlockSpec((None, 1, LANES), grp), pl.BlockSpec((None, 1, LANES), grp),
                   pl.BlockSpec((None, 1, LANES), grp)],
        out_shape=[jax.ShapeDtypeStruct((t, SSD_WIDTH), F32),
                   jax.ShapeDtypeStruct((t, SSD_GROUPS * SSD_STATE), F32),
                   jax.ShapeDtypeStruct((t, SSD_GROUPS * SSD_STATE), F32),
                   jax.ShapeDtypeStruct((SSD_GROUPS, t, LANES), F32), small, small, small],
        scratch_shapes=[pltpu.VMEM((N_PAIR, L, LANES), F32), pltpu.VMEM((2, L, HEAD_W), F32)],
        compiler_params=_params(("parallel", "arbitrary")), name="ssd_bwd")(
            xbc, xbc, xbc, dtr, bias, alog, dskip, hs, dy)


GN = SSD_WIDTH // SSD_GROUPS


def _post_fwd(o, proj, y, nw, *, tm=256):
    t = o.shape[0]

    def body(o_ref, za_ref, y_ref, zs_ref, nw_ref, ya_ref, ys_ref):
        za = za_ref[...].astype(F32)
        ya_ref[...] = _bf(o_ref[...] * (za * _sigmoid(za)))
        zs = zs_ref[...].astype(F32)
        yz = y_ref[...] * (zs * _sigmoid(zs))
        for g in range(SSD_GROUPS):
            sl = slice(g * GN, (g + 1) * GN)
            v = yz[:, sl]
            r = lax.rsqrt(jnp.mean(v * v, axis=-1, keepdims=True) + EPS)
            ys_ref[:, sl] = _bf(v * r * nw_ref[:, sl])

    return pl.pallas_call(
        body, grid=(t // tm,),
        in_specs=[pl.BlockSpec((tm, SB_WIDTH), lambda i: (i, 0)),
                  pl.BlockSpec((tm, SB_WIDTH), lambda i: (i, CB_ZA * LANES // SB_WIDTH)),
                  pl.BlockSpec((tm, SSD_WIDTH), lambda i: (i, 0)),
                  pl.BlockSpec((tm, SSD_WIDTH), lambda i: (i, CB_ZS * LANES // SSD_WIDTH)),
                  pl.BlockSpec((1, SSD_WIDTH), lambda i: (0, 0))],
        out_specs=[pl.BlockSpec((tm, SB_WIDTH), lambda i: (i, 0)),
                   pl.BlockSpec((tm, SSD_WIDTH), lambda i: (i, 0))],
        out_shape=[jax.ShapeDtypeStruct((t, SB_WIDTH), BF16), jax.ShapeDtypeStruct((t, SSD_WIDTH), BF16)],
        compiler_params=_params(("parallel",)), name="post_fwd")(o, proj, y, proj, nw)


def _post_bwd(dya, o, proj, dys, y, nw, dproj, *, tm=256):
    t = o.shape[0]
    zw = SB_WIDTH + SSD_WIDTH

    def body(dya_ref, o_ref, za_ref, dys_ref, y_ref, zs_ref, nw_ref, _, do_ref, dy_ref, dnw_ref, dz_ref):
        @pl.when(pl.program_id(0) == 0)
        def _():
            dnw_ref[...] = jnp.zeros_like(dnw_ref)

        za = za_ref[...].astype(F32)
        sa = _sigmoid(za)
        d = dya_ref[...].astype(F32)
        do_ref[...] = _bf(d * (za * sa))
        dz_ref[:, :SB_WIDTH] = _bf(d * o_ref[...] * (sa * (1.0 + za * (1.0 - sa))))
        zs = zs_ref[...].astype(F32)
        ss = _sigmoid(zs)
        silu = zs * ss
        yv = y_ref[...]
        yz = yv * silu
        dv = dys_ref[...].astype(F32)
        for g in range(SSD_GROUPS):
            sl = slice(g * GN, (g + 1) * GN)
            v = yz[:, sl]
            r = lax.rsqrt(jnp.mean(v * v, axis=-1, keepdims=True) + EPS)
            nrm = v * r
            dg = dv[:, sl]
            dnw_ref[:, sl] += jnp.sum(dg * nrm, axis=0, keepdims=True)
            dn = dg * nw_ref[:, sl]
            dyz = r * (dn - nrm * jnp.mean(dn * nrm, axis=-1, keepdims=True))
            dy_ref[:, sl] = dyz * silu[:, sl]
            dz_ref[:, SB_WIDTH + g * GN:SB_WIDTH + (g + 1) * GN] = _bf(
                dyz * yv[:, sl] * (ss[:, sl] * (1.0 + zs[:, sl] * (1.0 - ss[:, sl]))))

    a_spec = pl.BlockSpec((tm, SB_WIDTH), lambda i: (i, 0))
    s_spec = pl.BlockSpec((tm, SSD_WIDTH), lambda i: (i, 0))
    w_spec = pl.BlockSpec((1, SSD_WIDTH), lambda i: (0, 0))
    return pl.pallas_call(
        body, grid=(t // tm,),
        in_specs=[a_spec, a_spec, pl.BlockSpec((tm, SB_WIDTH), lambda i: (i, CB_ZA * LANES // SB_WIDTH)),
                  s_spec, s_spec, pl.BlockSpec((tm, SSD_WIDTH), lambda i: (i, CB_ZS * LANES // SSD_WIDTH)),
                  w_spec, pl.BlockSpec(memory_space=pl.ANY)],
        out_specs=[a_spec, s_spec, w_spec, pl.BlockSpec((tm, zw), lambda i: (i, 1))],
        out_shape=[jax.ShapeDtypeStruct((t, SB_WIDTH), BF16), jax.ShapeDtypeStruct((t, SSD_WIDTH), F32),
                   jax.ShapeDtypeStruct((1, SSD_WIDTH), F32), jax.ShapeDtypeStruct(dproj.shape, dproj.dtype)],
        input_output_aliases={7: 3},
        compiler_params=_params(("arbitrary",)), name="post_bwd")(dya, o, proj, dys, y, proj, nw, dproj)


def _merge_fwd(ya, ys, proj_g, *, tm=512):
    t = ya.shape[0]

    def body(ya_ref, ys_ref, ga_ref, gs_ref, m_ref):
        m_ref[...] = _bf(_sigmoid(ga_ref[...].astype(F32)) * ya_ref[...].astype(F32)
                         + _sigmoid(gs_ref[...].astype(F32)) * ys_ref[...].astype(F32))

    spec = pl.BlockSpec((tm, D_MODEL), lambda i: (i, 0))
    return pl.pallas_call(
        body, grid=(t // tm,),
        in_specs=[spec, spec, pl.BlockSpec((tm, D_MODEL), lambda i: (i, 0)),
                  pl.BlockSpec((tm, D_MODEL), lambda i: (i, 1))],
        out_specs=spec, out_shape=jax.ShapeDtypeStruct((t, D_MODEL), BF16),
        compiler_params=_params(("parallel",)), name="merge_fwd")(ya, ys, proj_g, proj_g)


def _merge_bwd(dm, ya, ys, proj_g, *, tm=512):
    t = ya.shape[0]

    def body(dm_ref, ya_ref, ys_ref, ga_ref, gs_ref, dya_ref, dys_ref, dg_ref):
        d = dm_ref[...].astype(F32)
        ga = _sigmoid(ga_ref[...].astype(F32))
        gs = _sigmoid(gs_ref[...].astype(F32))
        dya_ref[...] = _bf(d * ga)
        dys_ref[...] = _bf(d * gs)
        dg_ref[:, :D_MODEL] = _bf(d * ya_ref[...].astype(F32) * (ga * (1.0 - ga)))
        dg_ref[:, D_MODEL:] = _bf(d * ys_ref[...].astype(F32) * (gs * (1.0 - gs)))

    spec = pl.BlockSpec((tm, D_MODEL), lambda i: (i, 0))
    return pl.pallas_call(
        body, grid=(t // tm,),
        in_specs=[spec, spec, spec, pl.BlockSpec((tm, D_MODEL), lambda i: (i, 0)),
                  pl.BlockSpec((tm, D_MODEL), lambda i: (i, 1))],
        out_specs=[spec, spec, pl.BlockSpec((tm, 2 * D_MODEL), lambda i: (i, 0))],
        out_shape=[jax.ShapeDtypeStruct((t, D_MODEL), BF16), jax.ShapeDtypeStruct((t, D_MODEL), BF16),
                   jax.ShapeDtypeStruct((t, 2 * D_MODEL), BF16)],
        compiler_params=_params(("parallel",)), name="merge_bwd")(dm, ya, ys, proj_g, proj_g)


def _final(x, out, wf, target, *, tm=512):
    t = x.shape[0]

    def body(x_ref, o_ref, w_ref, t_ref, loss_ref, dx_ref, dw_ref):
        @pl.when(pl.program_id(0) == 0)
        def _():
            loss_ref[...] = jnp.zeros_like(loss_ref)
            dw_ref[...] = jnp.zeros_like(dw_ref)

        x2 = x_ref[...] + o_ref[...]
        r = lax.rsqrt(jnp.mean(x2 * x2, axis=-1, keepdims=True) + EPS)
        nrm = x2 * r
        e = nrm * w_ref[...] - t_ref[...]
        row_loss = jnp.mean(e * e, axis=-1, keepdims=True)
        loss_ref[...] += 0.5 * jnp.sum(row_loss, axis=0, keepdims=True)
        dyv = e * (1.0 / D_MODEL)
        dw_ref[...] += jnp.sum(dyv * nrm, axis=0, keepdims=True)
        dn = dyv * w_ref[...]
        dx_ref[...] = r * (dn - nrm * jnp.mean(dn * nrm, axis=-1, keepdims=True))

    spec = pl.BlockSpec((tm, D_MODEL), lambda i: (i, 0))
    fix = pl.BlockSpec((1, D_MODEL), lambda i: (0, 0))
    return pl.pallas_call(
        body, grid=(t // tm,),
        in_specs=[spec, spec, fix, spec],
        out_specs=[pl.BlockSpec((1, LANES), lambda i: (0, 0)), spec, fix],
        out_shape=[jax.ShapeDtypeStruct((1, LANES), F32), jax.ShapeDtypeStruct((t, D_MODEL), F32),
                   jax.ShapeDtypeStruct((1, D_MODEL), F32)],
        compiler_params=_params(("arbitrary",)), name="final_loss")(x, out, wf, target)


def _group_lanes(v):
    return jnp.pad(v.reshape(SSD_GROUPS, 1, SSD_HPG), ((0, 0), (0, 0), (0, LANES - SSD_HPG)))


def _ungroup_lanes(v):
    return v[:, :, :SSD_HPG].reshape(1, N_DT)


def _row_slabs(pieces, rows):
    total = sum(p.shape[0] for p in pieces)
    assert total % rows == 0
    slabs = []
    for lo in range(0, total, rows):
        parts, off = [], 0
        for p in pieces:
            a, b = max(lo, off), min(lo + rows, off + p.shape[0])
            if a < b:
                parts.append(p[a - off:b - off])
            off += p.shape[0]
        slabs.append(parts[0] if len(parts) == 1 else jnp.concatenate(parts, axis=0))
    return jnp.stack(slabs)


def _local_step(x, target, nb, s, w_in_t, conv_w, conv_b, norm_w, dt_bias, a_log, d_skip,
                ssm_norm_w, out_weights, final_norm_w, between=None, early=None):
    t = nb * s
    w_g_t = w_in_t[GATE_COL0:]
    w_dt_t = jnp.pad(w_in_t[DT_COL0:GATE_COL0], ((0, LANES - N_DT), (0, 0)))
    h, r1 = _rms_fwd(x, norm_w)
    proj = _matmul(h, w_in_t, nt=True, n=DT_COL0, out_dtype=BF16, name="proj_main", tm=MM_TALL)
    proj_g = _matmul(h, w_g_t, nt=True, out_dtype=BF16, name="proj_gate", tm=MM_TALL)
    dt_raw = _matmul(h, w_dt_t, nt=True, name="proj_dt")
    dtr = jnp.pad(dt_raw[:, :N_DT].reshape(t, SSD_GROUPS, SSD_HPG).transpose(1, 0, 2),
                  ((0, 0), (0, 0), (0, LANES - SSD_HPG)))
    bias_g, alog_g, dsk_g = _group_lanes(dt_bias), _group_lanes(a_log), _group_lanes(d_skip)

    o, att_a, att_sg = _attn_fwd(proj, nb, s)
    xbc = _conv_fwd(proj, conv_w, conv_b, nb, s)
    y, hs = _ssd_fwd(xbc, dtr, bias_g, alog_g, dsk_g, nb, s)
    ya_in, ys_in = _post_fwd(o, proj, y, ssm_norm_w)
    w_attn_out, w_ssm_out, w_o = out_weights(ys_in) if callable(out_weights) else out_weights
    ya = _matmul(ya_in, w_attn_out, out_dtype=BF16, name="attn_out")
    ys = _matmul(ys_in, w_ssm_out, out_dtype=BF16, name="ssm_out")
    merged = _merge_fwd(ya, ys, proj_g)
    out = _matmul(merged, w_o, name="out_proj")
    loss, dx2, d_final_w = _final(x, out, final_norm_w, target)

    dx2b = _bf(dx2)
    dmerged = _matmul(dx2b, w_o, nt=True, out_dtype=BF16, name="d_merged")
    d_w_o = _matmul(merged, dx2b, ta=True, name="d_w_o")
    dya, dys, dgate = _merge_bwd(dmerged, ya, ys, proj_g)
    d_w_attn_out = _matmul(ya_in, dya, ta=True, name="d_w_attn_out")
    d_w_ssm_out = _matmul(ys_in, dys, ta=True, name="d_w_ssm_out")
    if early is not None:
        ssm_norm_w = ssm_norm_w + early(dict(w_attn_out=d_w_attn_out, w_ssm_out=d_w_ssm_out, w_o=d_w_o))[0, 0]
    dya_in = _matmul(dya, w_attn_out, nt=True, out_dtype=BF16, name="d_ya_in")
    dys_in = _matmul(dys, w_ssm_out, nt=True, name="d_ys_in")
    do, dy, d_ssm_norm_w, dproj = _post_bwd(dya_in, o, proj, dys_in, y, ssm_norm_w, lax.empty(proj.shape, BF16))
    dq, dk, dv = _attn_bwd(proj, do, att_a, att_sg, nb, s)
    dxs_act, db_act, dc_act, ddtr, dbias_g, dalog_g, ddsk_g = _ssd_bwd(
        xbc, dtr, bias_g, alog_g, dsk_g, hs, dy, nb, s)
    dproj, dwx, dbx = _conv_bwd(proj, dxs_act, conv_w, conv_b, dproj, nb, s, col0=0, name="conv_bwd_x")
    dproj, dwb, dbb = _conv_bwd(proj, db_act, conv_w, conv_b, dproj, nb, s, col0=SSD_WIDTH, name="conv_bwd_b")
    dproj, dwc, dbc = _conv_bwd(proj, dc_act, conv_w, conv_b, dproj, nb, s, col0=SSD_WIDTH + 512,
                                name="conv_bwd_c")
    for i, piece in enumerate((dq, dk, dv)):
        dproj = lax.dynamic_update_slice(dproj, piece, (0, i * SB_WIDTH))
    ddt = jnp.pad(_bf(ddtr[:, :, :SSD_HPG].transpose(1, 0, 2).reshape(t, N_DT)), ((0, 0), (0, LANES - N_DT)))
    wt = dict(ta=True, out_dtype=BF16)
    d_w_in_t = _row_slabs(
        [_matmul(dproj, h, name="d_w_main", **wt), _matmul(ddt, h, name="d_w_dt", **wt)[:N_DT],
         _matmul(dgate, h, name="d_w_gate", **wt)], SHARD)
    grads = dict(
        w_in_t=d_w_in_t,
        conv_w=jnp.concatenate([dwx, dwb, dwc], axis=1), conv_b=jnp.concatenate([dbx, dbb, dbc], axis=1),
        dt_bias=_ungroup_lanes(dbias_g), a_log=_ungroup_lanes(dalog_g), d_skip=_ungroup_lanes(ddsk_g),
        ssm_norm_w=d_ssm_norm_w, w_attn_out=d_w_attn_out, w_ssm_out=d_w_ssm_out, w_o=d_w_o,
        final_norm_w=d_final_w)
    extra = None
    if between is not None:
        zero, extra = between(grads)
        ddt = ddt + _bf(zero[0:1, 0:1])

    dh = _matmul(dproj, w_in_t, more=[(dgate, w_g_t), (ddt, w_dt_t)], name="d_h", tk=DT_COL0 // 4)
    grad_x, grads["norm_w"] = _rms_bwd(dh, x, r1, norm_w, dx2)
    return loss, grad_x, grads, extra


N_CHIP = N_DEV // 2


def _pair_exchange(scatter, bcast, *, name):
    arrays = list(scatter) + list(bcast)
    ns, n = len(scatter), len(arrays)
    n_sem = ns * N_CHIP + (n - ns) * (N_DEV - 1)

    def body(*refs):
        ins, outs = refs[:n], refs[n:2 * n]
        send_sems, recv_sems, local_sems = refs[2 * n:]
        x, y, c = lax.axis_index("x"), lax.axis_index("y"), lax.axis_index("c")
        me = 4 * x + 2 * y + c
        copies, k = [], 0
        for a in range(ns):
            for q in range(N_CHIP):
                cp = pltpu.make_async_remote_copy(
                    src_ref=ins[a].at[2 * q + 1 - c], dst_ref=outs[a].at[q],
                    send_sem=send_sems.at[k], recv_sem=recv_sems.at[k],
                    device_id=(x, y, 1 - c), device_id_type=pl.DeviceIdType.MESH)
                cp.start()
                copies.append(cp)
                k += 1
        for a in range(ns, n):
            cp = pltpu.make_async_copy(ins[a], outs[a].at[me], local_sems.at[a - ns])
            cp.start()
            copies.append(cp)
            for r in range(1, N_DEV):
                tx = 1 - x if (r >> 2) & 1 else x
                ty = 1 - y if (r >> 1) & 1 else y
                tc = 1 - c if r & 1 else c
                cp = pltpu.make_async_remote_copy(
                    src_ref=ins[a], dst_ref=outs[a].at[me], send_sem=send_sems.at[k], recv_sem=recv_sems.at[k],
                    device_id=(tx, ty, tc), device_id_type=pl.DeviceIdType.MESH)
                cp.start()
                copies.append(cp)
                k += 1
        for cp in copies:
            cp.wait()

    out_shape = [jax.ShapeDtypeStruct((N_CHIP,) + v.shape[1:] if i < ns else (N_DEV,) + v.shape, v.dtype)
                 for i, v in enumerate(arrays)]
    return pl.pallas_call(
        body, in_specs=[pl.BlockSpec(memory_space=pl.ANY)] * n,
        out_specs=[pl.BlockSpec(memory_space=pl.ANY)] * n, out_shape=out_shape,
        scratch_shapes=[pltpu.SemaphoreType.DMA((n_sem,)), pltpu.SemaphoreType.DMA((n_sem,)),
                        pltpu.SemaphoreType.DMA((max(n - ns, 1),))],
        compiler_params=pltpu.CompilerParams(has_side_effects=True), name=name)(*arrays)


def _pair_add(mine, recv, core, *, name, tc=None):
    _, r, c = recv.shape
    tc = c if tc is None else tc
    assert c % tc == 0

    def body(core_ref, m_ref, r_ref, o_ref):
        o_ref[...] = (m_ref[...].astype(F32) + r_ref[...].astype(F32)).astype(o_ref.dtype)

    spec = pl.BlockSpec((None, r, tc), lambda q, i, core_ref: (q, 0, i))
    return pl.pallas_call(
        body, grid_spec=pltpu.PrefetchScalarGridSpec(
            num_scalar_prefetch=1, grid=(N_CHIP, c // tc),
            in_specs=[pl.BlockSpec((None, r, tc), lambda q, i, core_ref: (2 * q + core_ref[0], 0, i)), spec],
            out_specs=spec),
        out_shape=jax.ShapeDtypeStruct(recv.shape, recv.dtype),
        compiler_params=_params(("parallel", "parallel")), name=name)(core, mine, recv)


_HBM = pl.BlockSpec(memory_space=pltpu.HBM)
_SEM = pl.BlockSpec(memory_space=pltpu.SEMAPHORE)
_DATAFLOW = pltpu.SideEffectType.DATAFLOW_SIDE_EFFECTING
_PEERS = {"chips": N_CHIP - 1, "all": N_DEV - 1}


def _split_copies(pattern, srcs, lands, send_sems, recv_sems):
    x, y, c = lax.axis_index("x"), lax.axis_index("y"), lax.axis_index("c")
    per = _PEERS[pattern]
    copies = []
    for a in range(len(srcs)):
        for r in range(1, per + 1):
            if pattern == "chips":
                tx = 1 - x if (r >> 1) & 1 else x
                ty = 1 - y if r & 1 else y
                tc = c
                src, dst = srcs[a].at[2 * tx + ty], lands[a].at[2 * x + y]
            else:
                tx = 1 - x if (r >> 2) & 1 else x
                ty = 1 - y if (r >> 1) & 1 else y
                tc = 1 - c if r & 1 else c
                src, dst = srcs[a], lands[a].at[4 * x + 2 * y + c]
            copies.append(pltpu.make_async_remote_copy(
                src_ref=src, dst_ref=dst, send_sem=send_sems.at[a * per + r - 1],
                recv_sem=recv_sems.at[a * per + r - 1], device_id=(tx, ty, tc), device_id_type=pl.DeviceIdType.MESH))
    return copies


def _split_start(pattern, arrays, *, name):
    n = len(arrays)
    n_sem = n * _PEERS[pattern]

    def body(*refs):
        srcs, lands = refs[:n], refs[n:2 * n]
        send_sems, recv_sems = refs[2 * n], refs[2 * n + 1]
        zero = refs[-1]
        for cp in _split_copies(pattern, srcs, lands, send_sems, recv_sems):
            cp.start()
        zero[...] = jnp.zeros_like(zero)

    src_t = [pltpu.HBM(v.shape, v.dtype) for v in arrays]
    land_t = [pltpu.HBM(v.shape if pattern == "chips" else (N_DEV,) + v.shape, v.dtype) for v in arrays]
    outs = pl.pallas_call(
        body, name=name,
        out_shape=(pltpu.SemaphoreType.DMA((n_sem,)), pltpu.SemaphoreType.DMA((n_sem,)), *src_t, *land_t,
                   jax.ShapeDtypeStruct((8, LANES), F32)),
        in_specs=[_HBM] * (2 * n),
        out_specs=(_SEM, _SEM, *[_HBM] * (2 * n), pl.BlockSpec(memory_space=pltpu.VMEM)),
        input_output_aliases={i: 2 + i for i in range(2 * n)},
        compiler_params=pltpu.CompilerParams(has_side_effects=_DATAFLOW))(
            *[pltpu.with_memory_space_constraint(v, pltpu.HBM) for v in arrays],
            *[pltpu.with_memory_space_constraint(lax.empty(t.shape, t.dtype), pltpu.HBM) for t in land_t])
    return outs[0], outs[1], outs[2:2 + n], outs[2 + n:2 + 2 * n], outs[-1]


def _split_wait(pattern, send_sems, recv_sems, srcs, lands, after, *, name):
    n = len(srcs)

    def body(*refs):
        for cp in _split_copies(pattern, refs[:n], refs[n:2 * n], refs[2 * n], refs[2 * n + 1]):
            cp.wait_send()
            cp.wait_recv()

    outs = pl.pallas_call(
        body, name=name, out_shape=[pltpu.HBM(v.shape, v.dtype) for v in (*srcs, *lands)],
        in_specs=[_HBM] * (2 * n) + [_SEM, _SEM, pl.BlockSpec(memory_space=pl.ANY)],
        out_specs=[_HBM] * (2 * n), input_output_aliases={i: i for i in range(2 * n)},
        compiler_params=pltpu.CompilerParams(has_side_effects=_DATAFLOW))(
            *srcs, *lands, send_sems, recv_sems, after)
    return outs[:n], outs[n:]


def _own_slab(lands, srcs, index, whole):
    out = []
    for land, src in zip(lands, srcs):
        own = src[None] if whole else lax.dynamic_slice(src, (index, 0, 0), (1,) + src.shape[1:])
        out.append(lax.dynamic_update_slice(land, own, (index, 0, 0)))
    return out


def _gather(arrays, *, name):
    n = len(arrays)
    per = N_DEV - 1

    def body(*refs):
        ins, outs = refs[:n], refs[n:2 * n]
        send_sems, recv_sems, local_sems = refs[2 * n:]
        x, y, c = lax.axis_index("x"), lax.axis_index("y"), lax.axis_index("c")
        me, sibling = (x, y, c), (x, y, 1 - c)
        chips = [(1 - x, y), (x, 1 - y), (1 - x, 1 - y)]

        def slab(a, block):
            return outs[a].at[4 * block[0] + 2 * block[1] + block[2]]

        def copy(a, k, block, to, src=None):
            rows = slab(a, block)
            return pltpu.make_async_remote_copy(
                src_ref=rows if src is None else src, dst_ref=rows,
                send_sem=send_sems.at[a * per + k], recv_sem=recv_sems.at[a * per + k],
                device_id=to, device_id_type=pl.DeviceIdType.MESH)

        started = []
        for a in range(n):
            mine = pltpu.make_async_copy(ins[a], slab(a, me), local_sems.at[a])
            mine.start()
            started.append(mine)
        for a in range(n):
            first = [copy(a, 1 + j, me, (*chip, c), src=ins[a]) for j, chip in enumerate(chips)]
            first.append(copy(a, 0, me, sibling, src=ins[a]))
            for cp in first:
                cp.start()
            started += first
        for j, chip in enumerate(chips):
            for a in range(n):
                copy(a, 1 + j, (*chip, c), me).wait_recv()
                passed = copy(a, 4 + j, (*chip, c), sibling)
                passed.start()
                started.append(passed)
        for a in range(n):
            copy(a, 0, sibling, me).wait_recv()
            for j, chip in enumerate(chips):
                copy(a, 4 + j, (*chip, 1 - c), me).wait_recv()
        for cp in started[:n]:
            cp.wait()
        for cp in started[n:]:
            cp.wait_send()

    return pl.pallas_call(
        body, in_specs=[pl.BlockSpec(memory_space=pl.ANY)] * n,
        out_specs=[pl.BlockSpec(memory_space=pl.ANY)] * n,
        out_shape=[jax.ShapeDtypeStruct((N_DEV,) + v.shape, v.dtype) for v in arrays],
        scratch_shapes=[pltpu.SemaphoreType.DMA((n * per,)), pltpu.SemaphoreType.DMA((n * per,)),
                        pltpu.SemaphoreType.DMA((n,))],
        compiler_params=pltpu.CompilerParams(has_side_effects=True), name=name)(*arrays)


def _adamw(parts, w, m, v, *, name, tc=None):
    r, c = w.shape
    n_parts = parts.shape[0]
    tc = c if tc is None else tc
    assert c % tc == 0

    def body(p_ref, w_ref, m_ref, v_ref, g_ref, d_ref, nm_ref, nv_ref):
        g = p_ref[0].astype(F32)
        for d in range(1, n_parts):
            g = g + p_ref[d].astype(F32)
        mm = ADAM_B1 * m_ref[...] + (1.0 - ADAM_B1) * g
        vv = ADAM_B2 * v_ref[...] + (1.0 - ADAM_B2) * (g * g)
        m_hat = mm / (1.0 - ADAM_B1 ** ADAM_STEP)
        v_hat = vv / (1.0 - ADAM_B2 ** ADAM_STEP)
        g_ref[...] = g
        d_ref[...] = -ADAM_LR * (m_hat / (jnp.sqrt(v_hat) + ADAM_EPS) + ADAM_WD * w_ref[...])
        nm_ref[...] = mm
        nv_ref[...] = vv

    spec = pl.BlockSpec((r, tc), lambda i: (0, i))
    out = jax.ShapeDtypeStruct((r, c), F32)
    return pl.pallas_call(
        body, grid=(c // tc,),
        in_specs=[pl.BlockSpec((n_parts, r, tc), lambda i: (0, 0, i)), spec, spec, spec],
        out_specs=[spec] * 4, out_shape=[out] * 4,
        compiler_params=_params(("parallel",)), name=name)(parts, w, m, v)


SMALL = (("norm_w", D_MODEL), ("conv_b", CONV_DIM), ("dt_bias", N_DT), ("a_log", N_DT),
         ("d_skip", N_DT), ("ssm_norm_w", SSD_WIDTH), ("final_norm_w", D_MODEL))
SMALL_USED = sum(size for _, size in SMALL)
SMALL_ROWS = 64
SHARD = D_PROJ // N_DEV


def _pack_small(vals, scalar=None):
    parts = [vals[k].reshape(-1).astype(F32) for k, _ in SMALL]
    if scalar is not None:
        parts.append(scalar.reshape(1))
    flat = jnp.concatenate(parts)
    return jnp.pad(flat, (0, SMALL_ROWS * LANES - flat.shape[0])).reshape(SMALL_ROWS, LANES)


def _unpack_small(packed, shapes):
    flat = packed.reshape(-1)
    out, off = {}, 0
    for k, size in SMALL:
        out[k] = flat[off:off + size].reshape(shapes[k])
        off += size
    return out


def kernel(x, norm_w, w_in, conv_w, conv_b, dt_bias, a_log, d_skip, ssm_norm_w, w_attn_out, w_ssm_out, w_o, final_norm_w, loss_target, m_norm_w, m_w_in, m_conv_w, m_conv_b, m_dt_bias, m_a_log, m_d_skip, m_ssm_norm_w, m_w_attn_out, m_w_ssm_out, m_w_o, m_final_norm_w, v_norm_w, v_w_in, v_conv_w, v_conv_b, v_dt_bias, v_a_log, v_d_skip, v_ssm_norm_w, v_w_attn_out, v_w_ssm_out, v_w_o, v_final_norm_w):
    nb, s, _ = x.shape
    t = nb * s
    weights = dict(norm_w=norm_w, w_in=w_in, conv_w=conv_w, conv_b=conv_b, dt_bias=dt_bias, a_log=a_log,
                   d_skip=d_skip, ssm_norm_w=ssm_norm_w, w_attn_out=w_attn_out, w_ssm_out=w_ssm_out,
                   w_o=w_o, final_norm_w=final_norm_w)
    moms = dict(norm_w=m_norm_w, w_in=m_w_in, conv_w=m_conv_w, conv_b=m_conv_b, dt_bias=m_dt_bias,
                a_log=m_a_log, d_skip=m_d_skip, ssm_norm_w=m_ssm_norm_w, w_attn_out=m_w_attn_out,
                w_ssm_out=m_w_ssm_out, w_o=m_w_o, final_norm_w=m_final_norm_w)
    vels = dict(norm_w=v_norm_w, w_in=v_w_in, conv_w=v_conv_w, conv_b=v_conv_b, dt_bias=v_dt_bias,
                a_log=v_a_log, d_skip=v_d_skip, ssm_norm_w=v_ssm_norm_w, w_attn_out=v_w_attn_out,
                w_ssm_out=v_w_ssm_out, w_o=v_w_o, final_norm_w=v_final_norm_w)

    tr_ = lambda a: jnp.transpose(a[0])
    g_in, g_conv = _gather([_bf(tr_(w_in)), conv_w[0]], name="gather_weights")
    conv_full = g_conv.transpose(1, 0, 2).reshape(CONV_K, CONV_DIM)
    o_send, o_recv, o_srcs, o_lands, o_zero = _split_start(
        "all", [_bf(w_attn_out[0]), _bf(w_ssm_out[0]), _bf(w_o[0])], name="gather_out_weights_start")

    core = lax.axis_index("c").astype(jnp.int32).reshape(1)
    chip = 2 * lax.axis_index("x") + lax.axis_index("y")
    me = 2 * chip + lax.axis_index("c")

    def out_weights(after):
        srcs, lands = _split_wait("all", o_send, o_recv, o_srcs, o_lands, after, name="gather_out_weights_wait")
        g_wa, g_ws, g_wo = _own_slab(lands, srcs, me, True)
        return (g_wa.reshape(SB_WIDTH, D_MODEL), g_ws.reshape(SSD_WIDTH, D_MODEL), g_wo.reshape(D_MODEL, D_MODEL))

    def send_weight_grads(grads):
        slabs = dict(w_in=grads["w_in_t"],
                     conv_w=grads["conv_w"].reshape(CONV_K, N_DEV, CONV_DIM // N_DEV).transpose(1, 0, 2))
        return send_slabs(slabs, "")

    def send_out_weight_grads(grads):
        slabs = dict(w_attn_out=_bf(grads["w_attn_out"].reshape(N_DEV, SB_WIDTH // N_DEV, D_MODEL)),
                     w_ssm_out=_bf(grads["w_ssm_out"].reshape(N_DEV, SSD_WIDTH // N_DEV, D_MODEL)),
                     w_o=_bf(grads["w_o"].reshape(N_DEV, D_MODEL // N_DEV, D_MODEL)))
        zero, in_flight["out"] = send_slabs(slabs, "_out")
        return zero

    def send_slabs(slabs, tag):
        recv = _pair_exchange(list(slabs.values()), [], name="exchange_pairs" + tag)
        sums = [_pair_add(a, b, core, name="pair_add_" + k, tc=4 * LANES if k == "w_in" else None)
                for (k, a), b in zip(slabs.items(), recv)]
        send_sems, recv_sems, srcs, lands, zero = _split_start("chips", sums, name="exchange_chips" + tag + "_start")
        return zero, (send_sems, recv_sems, srcs, lands)

    in_flight = {}
    loss_part, grad_x, grads, (send_sems, recv_sems, srcs, lands) = _local_step(
        x.reshape(t, D_MODEL), loss_target.reshape(t, D_MODEL), nb, s, g_in.reshape(D_PROJ, D_MODEL),
        conv_full, conv_b, norm_w + o_zero[0:1, 0:1], dt_bias, a_log, d_skip, ssm_norm_w, out_weights,
        final_norm_w.reshape(1, D_MODEL), between=send_weight_grads, early=send_out_weight_grads)

    (p_small,) = _pair_exchange([], [_pack_small(grads, loss_part[0, 0])], name="exchange_small")
    srcs, lands = _split_wait("chips", send_sems, recv_sems, srcs, lands, grad_x, name="exchange_chips_wait")
    p_in, p_conv = _own_slab(lands, srcs, chip, False)
    srcs, lands = _split_wait("chips", *in_flight["out"], grad_x, name="exchange_chips_out_wait")
    p_wa, p_ws, p_wo = _own_slab(lands, srcs, chip, False)

    res = {}
    res["w_in"] = [a.T for a in _adamw(p_in, tr_(w_in), tr_(m_w_in), tr_(v_w_in), name="adamw_w_in", tc=2 * LANES)]
    res["conv_w"] = _adamw(p_conv, conv_w[0], m_conv_w[0], v_conv_w[0], name="adamw_conv_w")
    res["w_attn_out"] = _adamw(p_wa, w_attn_out[0], m_w_attn_out[0], v_w_attn_out[0], name="adamw_w_attn_out")
    res["w_ssm_out"] = _adamw(p_ws, w_ssm_out[0], m_w_ssm_out[0], v_w_ssm_out[0], name="adamw_w_ssm_out")
    res["w_o"] = _adamw(p_wo, w_o[0], m_w_o[0], v_w_o[0], name="adamw_w_o")
    sm = _adamw(p_small, _pack_small(weights), _pack_small(moms), _pack_small(vels), name="adamw_small")
    loss = sm[0].reshape(-1)[SMALL_USED]
    shapes = {k: weights[k].shape for k, _ in SMALL}
    sm = [_unpack_small(v, shapes) for v in sm]
    for k, _ in SMALL:
        res[k] = tuple(part[k] for part in sm)

    order = ("norm_w", "w_in", "conv_w", "conv_b", "dt_bias", "a_log", "d_skip", "ssm_norm_w",
             "w_attn_out", "w_ssm_out", "w_o", "final_norm_w")
    outs = [loss, grad_x.reshape(nb, s, D_MODEL)]
    for i in range(4):
        outs += [res[k][i].reshape(weights[k].shape) for k in order]
    return tuple(outs)
```
